```python
import math
import jax, jax.numpy as jnp
from jax import lax
import numpy as np

D_MODEL = 1024
BATCH = 8
SEQ = 8192
DEPTH = 1

MIX_WIDTH = D_MODEL
HEAD_DIM = 64
ATTN_WIDTH = MIX_WIDTH // 2
CONV_DIM = MIX_WIDTH - ATTN_WIDTH
N_Q_HEADS = ATTN_WIDTH // HEAD_DIM
N_KV_HEADS = 2
GQA_GROUP = N_Q_HEADS // N_KV_HEADS
KV_WIDTH = N_KV_HEADS * HEAD_DIM
N_CONV_GROUPS = CONV_DIM // HEAD_DIM
WINDOW = 128
BLOCK = 128
CONV_WIDTH = 3
NUM_BUCKETS = 32
MAX_DISTANCE = 128
D_FF = 2816
EPS = 1e-6
IN_SPLITS = (ATTN_WIDTH, ATTN_WIDTH + KV_WIDTH, ATTN_WIDTH + 2 * KV_WIDTH,
             ATTN_WIDTH + 2 * KV_WIDTH + CONV_DIM,
             ATTN_WIDTH + 2 * KV_WIDTH + 2 * CONV_DIM)
IN_COLS = ATTN_WIDTH + 2 * KV_WIDTH + 3 * CONV_DIM

kernel_name = "hymba_swa_sink_shortconv_macaron_t5bias"


def rms_norm(x, g):
    x32 = x.astype(jnp.float32)
    inv = lax.rsqrt(jnp.mean(x32 * x32, axis=-1, keepdims=True) + EPS)
    return (x32 * inv * g.astype(jnp.float32)).astype(x.dtype)


def swiglu(h, w_gate, w_up, w_down):
    return (jax.nn.silu(h @ w_gate) * (h @ w_up)) @ w_down


def t5_causal_bucket(dist):
    n = jnp.maximum(dist, 0)
    max_exact = NUM_BUCKETS // 2
    large = max_exact + (jnp.log(jnp.maximum(n, 1).astype(jnp.float32) / max_exact)
                         / math.log(MAX_DISTANCE / max_exact)
                         * (NUM_BUCKETS - max_exact)).astype(jnp.int32)
    large = jnp.minimum(large, NUM_BUCKETS - 1)
    return jnp.where(n < max_exact, n, large)


def band(t):
    pad = [(0, 0)] * t.ndim
    pad[1] = (1, 0)
    prev = jnp.pad(t, pad)[:, :-1]
    return jnp.concatenate([prev, t], axis=2)


def sliding_window_sink_attention(q, k, v, sinks, rel_table):
    b, s, _ = q.shape
    nb = s // BLOCK
    qb = q.reshape(b, nb, BLOCK, N_KV_HEADS, GQA_GROUP, HEAD_DIM).astype(jnp.float32)
    kb = band(k.reshape(b, nb, BLOCK, N_KV_HEADS, HEAD_DIM)).astype(jnp.float32)
    vb = band(v.reshape(b, nb, BLOCK, N_KV_HEADS, HEAD_DIM)).astype(jnp.float32)
    scores = jnp.einsum('bnqhgd,bnjhd->bnhgqj', qb, kb) * (HEAD_DIM ** -0.5)
    qi = jnp.arange(BLOCK, dtype=jnp.int32)[:, None]
    kj = jnp.arange(2 * BLOCK, dtype=jnp.int32)[None, :]
    dist = qi + BLOCK - kj
    bias = rel_table.astype(jnp.float32)[t5_causal_bucket(dist)]
    bias = bias.transpose(2, 0, 1).reshape(N_KV_HEADS, GQA_GROUP, BLOCK, 2 * BLOCK)
    key_abs = (jnp.arange(nb, dtype=jnp.int32)[:, None] * BLOCK
               + jnp.arange(2 * BLOCK, dtype=jnp.int32)[None, :] - BLOCK)
    valid = ((dist >= 0) & (dist < WINDOW))[None, :, :] & (key_abs >= 0)[:, None, :]
    valid = valid[None, :, None, None, :, :]
    scores = jnp.where(valid, scores + bias, -jnp.inf)
    sink = sinks.astype(jnp.float32).reshape(N_KV_HEADS, GQA_GROUP)[:, :, None, None]
    m = jnp.maximum(jnp.max(scores, axis=-1, keepdims=True), sink)
    p = jnp.exp(scores - m)
    denom = jnp.sum(p, axis=-1, keepdims=True) + jnp.exp(sink - m)
    out = jnp.einsum('bnhgqj,bnjhd->bnqhgd', p / denom, vb)
    return out.reshape(b, s, ATTN_WIDTH).astype(q.dtype)


def causal_short_conv(u, w):
    return lax.conv_general_dilated(
        u, w[:, None, :].astype(u.dtype), window_strides=(1,),
        padding=[(CONV_WIDTH - 1, 0)], dimension_numbers=('NWC', 'WIO', 'NWC'),
        feature_group_count=u.shape[-1])


def _fwd_setup_inputs(seed: int = 0) -> dict:
    key = jax.random.key(seed)
    ks = jax.random.split(key, 20)
    f32 = jnp.float32

    def nrm(k, shape, scale):
        return jax.random.normal(k, shape, f32) * scale

    def gain(k, shape):
        return 1.0 + 0.02 * jax.random.normal(k, shape, f32)

    L = DEPTH
    return {
        "x": nrm(ks[0], (BATCH, SEQ, D_MODEL), 1.0),
        "rel_bias_table": nrm(ks[1], (NUM_BUCKETS, N_Q_HEADS), 0.5),
        "ffn1_norm": gain(ks[2], (L, D_MODEL)),
        "ffn1_w_gate": nrm(ks[3], (L, D_MODEL, D_FF), D_MODEL ** -0.5),
        "ffn1_w_up": nrm(ks[4], (L, D_MODEL, D_FF), D_MODEL ** -0.5),
        "ffn1_w_down": nrm(ks[5], (L, D_FF, D_MODEL), D_FF ** -0.5),
        "mix_norm": gain(ks[6], (L, D_MODEL)),
        "w_in": nrm(ks[7], (L, D_MODEL, IN_COLS), D_MODEL ** -0.5),
        "conv_w": nrm(ks[8], (L, CONV_WIDTH, CONV_DIM), CONV_WIDTH ** -0.5),
        "attn_sinks": nrm(ks[9], (L, N_Q_HEADS), 0.5),
        "attn_out_norm": gain(ks[10], (L, ATTN_WIDTH)),
        "conv_out_norm": gain(ks[11], (L, CONV_DIM)),
        "w_out": nrm(ks[12], (L, MIX_WIDTH, D_MODEL), MIX_WIDTH ** -0.5),
        "ffn2_norm": gain(ks[13], (L, D_MODEL)),
        "ffn2_w_gate": nrm(ks[14], (L, D_MODEL, D_FF), D_MODEL ** -0.5),
        "ffn2_w_up": nrm(ks[15], (L, D_MODEL, D_FF), D_MODEL ** -0.5),
        "ffn2_w_down": nrm(ks[16], (L, D_FF, D_MODEL), D_FF ** -0.5),
        "final_norm": gain(ks[17], (D_MODEL,)),
    }


def _fwd_reference(x, rel_bias_table, ffn1_norm, ffn1_w_gate, ffn1_w_up, ffn1_w_down,
              mix_norm, w_in, conv_w, attn_sinks, attn_out_norm, conv_out_norm,
              w_out, ffn2_norm, ffn2_w_gate, ffn2_w_up, ffn2_w_down, final_norm):
    for l in range(DEPTH):
        x = x + 0.5 * swiglu(rms_norm(x, ffn1_norm[l]), ffn1_w_gate[l], ffn1_w_up[l], ffn1_w_down[l])
        h = rms_norm(x, mix_norm[l])
        z = h @ w_in[l]
        q, k, v, u, gate_b, gate_c = jnp.split(z, IN_SPLITS, axis=-1)
        attn = sliding_window_sink_attention(q, k, v, attn_sinks[l], rel_bias_table)
        conv = gate_b * causal_short_conv(gate_c * u, conv_w[l])
        mixed = jnp.concatenate([rms_norm(attn, attn_out_norm[l]),
                                 rms_norm(conv, conv_out_norm[l])], axis=-1)
        x = x + mixed @ w_out[l]
        x = x + 0.5 * swiglu(rms_norm(x, ffn2_norm[l]), ffn2_w_gate[l], ffn2_w_up[l], ffn2_w_down[l])
    return rms_norm(x, final_norm)


import jax as _jax
import jax.numpy as _jnp

TWIN_FORMAT = 'train_step'
FWD_PARAMS = ['x', 'rel_bias_table', 'ffn1_norm', 'ffn1_w_gate', 'ffn1_w_up', 'ffn1_w_down', 'mix_norm', 'w_in', 'conv_w', 'attn_sinks', 'attn_out_norm', 'conv_out_norm', 'w_out', 'ffn2_norm', 'ffn2_w_gate', 'ffn2_w_up', 'ffn2_w_down', 'final_norm']
TWIN_WEIGHTS = ['rel_bias_table', 'ffn1_norm', 'ffn1_w_gate', 'ffn1_w_up', 'ffn1_w_down', 'mix_norm', 'w_in', 'conv_w', 'attn_sinks', 'attn_out_norm', 'conv_out_norm', 'w_out', 'ffn2_norm', 'ffn2_w_gate', 'ffn2_w_up', 'ffn2_w_down', 'final_norm']
TWIN_DIFF_INPUT = 'x'
TWIN_INPUTS = ['x', 'rel_bias_table', 'ffn1_norm', 'ffn1_w_gate', 'ffn1_w_up', 'ffn1_w_down', 'mix_norm', 'w_in', 'conv_w', 'attn_sinks', 'attn_out_norm', 'conv_out_norm', 'w_out', 'ffn2_norm', 'ffn2_w_gate', 'ffn2_w_up', 'ffn2_w_down', 'final_norm', 'loss_target', 'm_rel_bias_table', 'm_ffn1_norm', 'm_ffn1_w_gate', 'm_ffn1_w_up', 'm_ffn1_w_down', 'm_mix_norm', 'm_w_in', 'm_conv_w', 'm_attn_sinks', 'm_attn_out_norm', 'm_conv_out_norm', 'm_w_out', 'm_ffn2_norm', 'm_ffn2_w_gate', 'm_ffn2_w_up', 'm_ffn2_w_down', 'm_final_norm', 'v_rel_bias_table', 'v_ffn1_norm', 'v_ffn1_w_gate', 'v_ffn1_w_up', 'v_ffn1_w_down', 'v_mix_norm', 'v_w_in', 'v_conv_w', 'v_attn_sinks', 'v_attn_out_norm', 'v_conv_out_norm', 'v_w_out', 'v_ffn2_norm', 'v_ffn2_w_gate', 'v_ffn2_w_up', 'v_ffn2_w_down', 'v_final_norm']
TWIN_OUTPUTS = ['loss', 'grad_x', 'grad_rel_bias_table', 'grad_ffn1_norm', 'grad_ffn1_w_gate', 'grad_ffn1_w_up', 'grad_ffn1_w_down', 'grad_mix_norm', 'grad_w_in', 'grad_conv_w', 'grad_attn_sinks', 'grad_attn_out_norm', 'grad_conv_out_norm', 'grad_w_out', 'grad_ffn2_norm', 'grad_ffn2_w_gate', 'grad_ffn2_w_up', 'grad_ffn2_w_down', 'grad_final_norm', 'delta_rel_bias_table', 'delta_ffn1_norm', 'delta_ffn1_w_gate', 'delta_ffn1_w_up', 'delta_ffn1_w_down', 'delta_mix_norm', 'delta_w_in', 'delta_conv_w', 'delta_attn_sinks', 'delta_attn_out_norm', 'delta_conv_out_norm', 'delta_w_out', 'delta_ffn2_norm', 'delta_ffn2_w_gate', 'delta_ffn2_w_up', 'delta_ffn2_w_down', 'delta_final_norm', 'new_m_rel_bias_table', 'new_m_ffn1_norm', 'new_m_ffn1_w_gate', 'new_m_ffn1_w_up', 'new_m_ffn1_w_down', 'new_m_mix_norm', 'new_m_w_in', 'new_m_conv_w', 'new_m_attn_sinks', 'new_m_attn_out_norm', 'new_m_conv_out_norm', 'new_m_w_out', 'new_m_ffn2_norm', 'new_m_ffn2_w_gate', 'new_m_ffn2_w_up', 'new_m_ffn2_w_down', 'new_m_final_norm', 'new_v_rel_bias_table', 'new_v_ffn1_norm', 'new_v_ffn1_w_gate', 'new_v_ffn1_w_up', 'new_v_ffn1_w_down', 'new_v_mix_norm', 'new_v_w_in', 'new_v_conv_w', 'new_v_attn_sinks', 'new_v_attn_out_norm', 'new_v_conv_out_norm', 'new_v_w_out', 'new_v_ffn2_norm', 'new_v_ffn2_w_gate', 'new_v_ffn2_w_up', 'new_v_ffn2_w_down', 'new_v_final_norm']
TWIN_LEAF_KINDS = {'loss': 'loss', 'grad_x': 'grad_x', 'grad_rel_bias_table': 'grad_w', 'grad_ffn1_norm': 'grad_w', 'grad_ffn1_w_gate': 'grad_w', 'grad_ffn1_w_up': 'grad_w', 'grad_ffn1_w_down': 'grad_w', 'grad_mix_norm': 'grad_w', 'grad_w_in': 'grad_w', 'grad_conv_w': 'grad_w', 'grad_attn_sinks': 'grad_w', 'grad_attn_out_norm': 'grad_w', 'grad_conv_out_norm': 'grad_w', 'grad_w_out': 'grad_w', 'grad_ffn2_norm': 'grad_w', 'grad_ffn2_w_gate': 'grad_w', 'grad_ffn2_w_up': 'grad_w', 'grad_ffn2_w_down': 'grad_w', 'grad_final_norm': 'grad_w', 'delta_rel_bias_table': 'delta_w', 'delta_ffn1_norm': 'delta_w', 'delta_ffn1_w_gate': 'delta_w', 'delta_ffn1_w_up': 'delta_w', 'delta_ffn1_w_down': 'delta_w', 'delta_mix_norm': 'delta_w', 'delta_w_in': 'delta_w', 'delta_conv_w': 'delta_w', 'delta_attn_sinks': 'delta_w', 'delta_attn_out_norm': 'delta_w', 'delta_conv_out_norm': 'delta_w', 'delta_w_out': 'delta_w', 'delta_ffn2_norm': 'delta_w', 'delta_ffn2_w_gate': 'delta_w', 'delta_ffn2_w_up': 'delta_w', 'delta_ffn2_w_down': 'delta_w', 'delta_final_norm': 'delta_w', 'new_m_rel_bias_table': 'new_m', 'new_m_ffn1_norm': 'new_m', 'new_m_ffn1_w_gate': 'new_m', 'new_m_ffn1_w_up': 'new_m', 'new_m_ffn1_w_down': 'new_m', 'new_m_mix_norm': 'new_m', 'new_m_w_in': 'new_m', 'new_m_conv_w': 'new_m', 'new_m_attn_sinks': 'new_m', 'new_m_attn_out_norm': 'new_m', 'new_m_conv_out_norm': 'new_m', 'new_m_w_out': 'new_m', 'new_m_ffn2_norm': 'new_m', 'new_m_ffn2_w_gate': 'new_m', 'new_m_ffn2_w_up': 'new_m', 'new_m_ffn2_w_down': 'new_m', 'new_m_final_norm': 'new_m', 'new_v_rel_bias_table': 'new_v', 'new_v_ffn1_norm': 'new_v', 'new_v_ffn1_w_gate': 'new_v', 'new_v_ffn1_w_up': 'new_v', 'new_v_ffn1_w_down': 'new_v', 'new_v_mix_norm': 'new_v', 'new_v_w_in': 'new_v', 'new_v_conv_w': 'new_v', 'new_v_attn_sinks': 'new_v', 'new_v_attn_out_norm': 'new_v', 'new_v_conv_out_norm': 'new_v', 'new_v_w_out': 'new_v', 'new_v_ffn2_norm': 'new_v', 'new_v_ffn2_w_gate': 'new_v', 'new_v_ffn2_w_up': 'new_v', 'new_v_ffn2_w_down': 'new_v', 'new_v_final_norm': 'new_v'}


def _forward(args):
    return _fwd_reference(*[args[k] for k in FWD_PARAMS])


def _output_shape():
    def fwd():
        inp = _fwd_setup_inputs(0)
        return _fwd_reference(*[inp[k] for k in FWD_PARAMS])
    out = _jax.eval_shape(fwd)
    return out.shape, out.dtype

N_MICROBATCH = 1
ADAM_LR = 0.001
ADAM_B1 = 0.9
ADAM_B2 = 0.999
ADAM_EPS = 1e-08
ADAM_WD = 0.01
ADAM_STEP = 10
PER_EXAMPLE_BATCH_AXIS = {'x': 0, 'loss_target': 0}
SHARED_INPUTS = []
_WEIGHT_DTYPES = {'rel_bias_table': _jnp.float32, 'ffn1_norm': _jnp.float32, 'ffn1_w_gate': _jnp.float32, 'ffn1_w_up': _jnp.float32, 'ffn1_w_down': _jnp.float32, 'mix_norm': _jnp.float32, 'w_in': _jnp.float32, 'conv_w': _jnp.float32, 'attn_sinks': _jnp.float32, 'attn_out_norm': _jnp.float32, 'conv_out_norm': _jnp.float32, 'w_out': _jnp.float32, 'ffn2_norm': _jnp.float32, 'ffn2_w_gate': _jnp.float32, 'ffn2_w_up': _jnp.float32, 'ffn2_w_down': _jnp.float32, 'final_norm': _jnp.float32}
MOMENT_SCALE = {'rel_bias_table': 2.176815e-01, 'ffn1_norm': 1.471950e-01, 'ffn1_w_gate': 6.370434e-02, 'ffn1_w_up': 6.167924e-02, 'ffn1_w_down': 1.022214e-01, 'mix_norm': 3.269609e-01, 'w_in': 2.044009e-01, 'conv_w': 1.953645e-01, 'attn_sinks': 7.808976e-02, 'attn_out_norm': 1.818764e-01, 'conv_out_norm': 1.866490e-01, 'w_out': 1.782153e-01, 'ffn2_norm': 7.443261e-02, 'ffn2_w_gate': 3.164703e-02, 'ffn2_w_up': 3.066598e-02, 'ffn2_w_down': 5.082006e-02, 'final_norm': 6.400068e+01}


def _to_microbatches(a, axis):
    t = _jnp.moveaxis(a, axis, 0)
    t = t.reshape((N_MICROBATCH, t.shape[0] // N_MICROBATCH) + t.shape[1:])
    return _jnp.moveaxis(t, 1, axis + 1)


def setup_inputs(seed: int = 0) -> dict:
    inp = _fwd_setup_inputs(seed)
    key = _jax.random.fold_in(_jax.random.key(seed), 7919)
    shape, _ = _output_shape()
    out = dict(inp)
    out["loss_target"] = _jax.random.normal(_jax.random.fold_in(key, 0), shape, _jnp.float32)
    for i, name in enumerate(TWIN_WEIGHTS):
        w = inp[name].astype(_jnp.float32)
        if MOMENT_SCALE is None:
            s = _jnp.sqrt(_jnp.mean(_jnp.square(w)) + 1e-30)
        else:
            s = MOMENT_SCALE[name]
        km, kv = _jax.random.split(_jax.random.fold_in(key, i + 1))
        out[name] = w
        out["m_" + name] = s * _jax.random.normal(km, w.shape, _jnp.float32)
        out["v_" + name] = (s * s) * _jax.random.uniform(kv, w.shape, _jnp.float32, 0.5, 1.5)
    if N_MICROBATCH > 1:
        for name, axis in PER_EXAMPLE_BATCH_AXIS.items():
            out[name] = _to_microbatches(out[name], axis)
    return {'x': out['x'], 'rel_bias_table': out['rel_bias_table'], 'ffn1_norm': out['ffn1_norm'], 'ffn1_w_gate': out['ffn1_w_gate'], 'ffn1_w_up': out['ffn1_w_up'], 'ffn1_w_down': out['ffn1_w_down'], 'mix_norm': out['mix_norm'], 'w_in': out['w_in'], 'conv_w': out['conv_w'], 'attn_sinks': out['attn_sinks'], 'attn_out_norm': out['attn_out_norm'], 'conv_out_norm': out['conv_out_norm'], 'w_out': out['w_out'], 'ffn2_norm': out['ffn2_norm'], 'ffn2_w_gate': out['ffn2_w_gate'], 'ffn2_w_up': out['ffn2_w_up'], 'ffn2_w_down': out['ffn2_w_down'], 'final_norm': out['final_norm'], 'loss_target': out['loss_target'], 'm_rel_bias_table': out['m_rel_bias_table'], 'm_ffn1_norm': out['m_ffn1_norm'], 'm_ffn1_w_gate': out['m_ffn1_w_gate'], 'm_ffn1_w_up': out['m_ffn1_w_up'], 'm_ffn1_w_down': out['m_ffn1_w_down'], 'm_mix_norm': out['m_mix_norm'], 'm_w_in': out['m_w_in'], 'm_conv_w': out['m_conv_w'], 'm_attn_sinks': out['m_attn_sinks'], 'm_attn_out_norm': out['m_attn_out_norm'], 'm_conv_out_norm': out['m_conv_out_norm'], 'm_w_out': out['m_w_out'], 'm_ffn2_norm': out['m_ffn2_norm'], 'm_ffn2_w_gate': out['m_ffn2_w_gate'], 'm_ffn2_w_up': out['m_ffn2_w_up'], 'm_ffn2_w_down': out['m_ffn2_w_down'], 'm_final_norm': out['m_final_norm'], 'v_rel_bias_table': out['v_rel_bias_table'], 'v_ffn1_norm': out['v_ffn1_norm'], 'v_ffn1_w_gate': out['v_ffn1_w_gate'], 'v_ffn1_w_up': out['v_ffn1_w_up'], 'v_ffn1_w_down': out['v_ffn1_w_down'], 'v_mix_norm': out['v_mix_norm'], 'v_w_in': out['v_w_in'], 'v_conv_w': out['v_conv_w'], 'v_attn_sinks': out['v_attn_sinks'], 'v_attn_out_norm': out['v_attn_out_norm'], 'v_conv_out_norm': out['v_conv_out_norm'], 'v_w_out': out['v_w_out'], 'v_ffn2_norm': out['v_ffn2_norm'], 'v_ffn2_w_gate': out['v_ffn2_w_gate'], 'v_ffn2_w_up': out['v_ffn2_w_up'], 'v_ffn2_w_down': out['v_ffn2_w_down'], 'v_final_norm': out['v_final_norm']}


def _loss(weights, diff, rest, loss_target):
    with _jax.named_scope("forward"):
        args = {**rest, TWIN_DIFF_INPUT: diff, **{k: w.astype(_WEIGHT_DTYPES[k]) for k, w in weights.items()}}
        y = _forward(args)
    with _jax.named_scope("loss_head"):
        err = _jnp.square(y.astype(_jnp.float32) - loss_target)
        return 0.5 * _jnp.sum(_jnp.mean(err, axis=-1)) if err.ndim else 0.5 * err


def _adamw(w, g, m, v):
    m = ADAM_B1 * m + (1.0 - ADAM_B1) * g
    v = ADAM_B2 * v + (1.0 - ADAM_B2) * _jnp.square(g)
    m_hat = m / (1.0 - ADAM_B1 ** ADAM_STEP)
    v_hat = v / (1.0 - ADAM_B2 ** ADAM_STEP)
    delta = -ADAM_LR * (m_hat / (_jnp.sqrt(v_hat) + ADAM_EPS) + ADAM_WD * w)
    return delta, m, v


def reference(x, rel_bias_table, ffn1_norm, ffn1_w_gate, ffn1_w_up, ffn1_w_down, mix_norm, w_in, conv_w, attn_sinks, attn_out_norm, conv_out_norm, w_out, ffn2_norm, ffn2_w_gate, ffn2_w_up, ffn2_w_down, final_norm, loss_target, m_rel_bias_table, m_ffn1_norm, m_ffn1_w_gate, m_ffn1_w_up, m_ffn1_w_down, m_mix_norm, m_w_in, m_conv_w, m_attn_sinks, m_attn_out_norm, m_conv_out_norm, m_w_out, m_ffn2_norm, m_ffn2_w_gate, m_ffn2_w_up, m_ffn2_w_down, m_final_norm, v_rel_bias_table, v_ffn1_norm, v_ffn1_w_gate, v_ffn1_w_up, v_ffn1_w_down, v_mix_norm, v_w_in, v_conv_w, v_attn_sinks, v_attn_out_norm, v_conv_out_norm, v_w_out, v_ffn2_norm, v_ffn2_w_gate, v_ffn2_w_up, v_ffn2_w_down, v_final_norm):
    given = dict(x=x, rel_bias_table=rel_bias_table, ffn1_norm=ffn1_norm, ffn1_w_gate=ffn1_w_gate, ffn1_w_up=ffn1_w_up, ffn1_w_down=ffn1_w_down, mix_norm=mix_norm, w_in=w_in, conv_w=conv_w, attn_sinks=attn_sinks, attn_out_norm=attn_out_norm, conv_out_norm=conv_out_norm, w_out=w_out, ffn2_norm=ffn2_norm, ffn2_w_gate=ffn2_w_gate, ffn2_w_up=ffn2_w_up, ffn2_w_down=ffn2_w_down, final_norm=final_norm, loss_target=loss_target, m_rel_bias_table=m_rel_bias_table, m_ffn1_norm=m_ffn1_norm, m_ffn1_w_gate=m_ffn1_w_gate, m_ffn1_w_up=m_ffn1_w_up, m_ffn1_w_down=m_ffn1_w_down, m_mix_norm=m_mix_norm, m_w_in=m_w_in, m_conv_w=m_conv_w, m_attn_sinks=m_attn_sinks, m_attn_out_norm=m_attn_out_norm, m_conv_out_norm=m_conv_out_norm, m_w_out=m_w_out, m_ffn2_norm=m_ffn2_norm, m_ffn2_w_gate=m_ffn2_w_gate, m_ffn2_w_up=m_ffn2_w_up, m_ffn2_w_down=m_ffn2_w_down, m_final_norm=m_final_norm, v_rel_bias_table=v_rel_bias_table, v_ffn1_norm=v_ffn1_norm, v_ffn1_w_gate=v_ffn1_w_gate, v_ffn1_w_up=v_ffn1_w_up, v_ffn1_w_down=v_ffn1_w_down, v_mix_norm=v_mix_norm, v_w_in=v_w_in, v_conv_w=v_conv_w, v_attn_sinks=v_attn_sinks, v_attn_out_norm=v_attn_out_norm, v_conv_out_norm=v_conv_out_norm, v_w_out=v_w_out, v_ffn2_norm=v_ffn2_norm, v_ffn2_w_gate=v_ffn2_w_gate, v_ffn2_w_up=v_ffn2_w_up, v_ffn2_w_down=v_ffn2_w_down, v_final_norm=v_final_norm)
    weights = {n: given[n] for n in TWIN_WEIGHTS}
    shared = {n: given[n] for n in SHARED_INPUTS}
    per_example = {n: given[n] for n in ['x']}
    grad_fn = _jax.value_and_grad(_loss, argnums=(0, 1))

    def one_microbatch(ex, loss_target):
        ex = dict(ex)
        diff = ex.pop(TWIN_DIFF_INPUT)
        return grad_fn(weights, diff, {**shared, **ex}, loss_target)

    if N_MICROBATCH == 1:
        loss, (grad_w, grad_x) = one_microbatch(per_example, given["loss_target"])
    else:
        def body(carry, xs):
            loss_sum, grad_sum = carry
            l_k, (gw_k, gx_k) = one_microbatch(xs[0], xs[1])
            with _jax.named_scope("update"):
                return (loss_sum + l_k, _jax.tree.map(_jnp.add, grad_sum, gw_k)), gx_k

        init = (_jnp.zeros((), _jnp.float32), _jax.tree.map(_jnp.zeros_like, weights))
        (loss, grad_w), grad_x = _jax.lax.scan(body, init, (per_example, given["loss_target"]))
    with _jax.named_scope("update"):
        delta_w, new_m, new_v = {}, {}, {}
        for n in TWIN_WEIGHTS:
            delta_w[n], new_m[n], new_v[n] = _adamw(weights[n], grad_w[n], given["m_" + n], given["v_" + n])
    return (loss, grad_x, *[grad_w[n] for n in TWIN_WEIGHTS], *[delta_w[n] for n in TWIN_WEIGHTS],
            *[new_m[n] for n in TWIN_WEIGHTS], *[new_v[n] for n in TWIN_WEIGHTS])
```

```python
import math

import numpy as np
import jax
import jax.numpy as jnp
from jax import lax
from jax.experimental import pallas as pl
from jax.experimental.pallas import tpu as pltpu

F32 = jnp.float32
BF16 = jnp.bfloat16

N_DEV = 8
EPS = 1e-6
HEAD_DIM = 64
N_Q_HEADS = 8
N_KV_HEADS = 2
GQA_GROUP = 4
ATTN_WIDTH = 512
KV_WIDTH = 128
CONV_DIM = 512
BLOCK = 128
WINDOW = 128
NUM_BUCKETS = 32
MAX_DISTANCE = 128
SCALE = HEAD_DIM ** -0.5
MASKED = -1e30

ADAM_LR = 0.001
ADAM_B1 = 0.9
ADAM_B2 = 0.999
ADAM_EPS = 1e-08
ADAM_WD = 0.01
ADAM_STEP = 10

VMEM_LIMIT_BYTES = 40 * 1024 * 1024
SUBLANES = 8
PACK_ROWS = 16
PACK_COLS = 1024

TM_FFN = 1024
TM_FFN_BWD = 512
TM_MIX = 512
TK_WGRAD = 512
TF_FFN = 256
ADAM_ROWS = 256


def _row_tile(rows, limit):
    best = SUBLANES
    for t in range(SUBLANES, min(rows, limit) + 1, SUBLANES):
        if rows % t == 0:
            best = t
    return best


def _params(*sem):
    return pltpu.CompilerParams(dimension_semantics=sem, vmem_limit_bytes=VMEM_LIMIT_BYTES)


def _dot(a, b):
    return jnp.dot(a, b, preferred_element_type=F32)


def _dot_nt(a, b):
    return lax.dot_general(a, b, (((1,), (1,)), ((), ())), preferred_element_type=F32)


def _dot_tn(a, b):
    return lax.dot_general(a, b, (((0,), (0,)), ((), ())), preferred_element_type=F32)


def _sigmoid(g):
    return 1.0 / (1.0 + jnp.exp(-g))


def _rms_stats(x):
    inv = lax.rsqrt(jnp.mean(x * x, axis=-1, keepdims=True) + EPS)
    return inv, x * inv


def _rms_bwd(dy, x, gain):
    inv, xhat = _rms_stats(x)
    dgain = jnp.sum(dy * xhat, axis=0, keepdims=True)
    dxh = dy * gain
    dx = inv * (dxh - xhat * jnp.mean(dxh * xhat, axis=-1, keepdims=True))
    return dx, dgain


def _exchange(arrs, scatter, name):
    n = len(arrs)
    out_shape = []
    for a in arrs:
        shp = a.shape if scatter else (N_DEV,) + a.shape
        out_shape.append(jax.ShapeDtypeStruct(shp, a.dtype))

    def body(*refs):
        ins, outs = refs[:n], refs[n:2 * n]
        send_sems, recv_sems, local_sems = refs[2 * n:]
        x, y, c = lax.axis_index("x"), lax.axis_index("y"), lax.axis_index("c")
        me = 4 * x + 2 * y + c
        peers = []
        for k in range(1, N_DEV):
            px = 1 - x if (k >> 2) & 1 else x
            py = 1 - y if (k >> 1) & 1 else y
            pc = 1 - c if k & 1 else c
            peers.append((px, py, pc))
        started = []
        for a in range(n):
            own = ins[a].at[me] if scatter else ins[a]
            loc = pltpu.make_async_copy(own, outs[a].at[me], local_sems.at[a])
            loc.start()
            started.append(loc)
        sends = []
        for a in range(n):
            for k, (px, py, pc) in enumerate(peers):
                src = ins[a].at[4 * px + 2 * py + pc] if scatter else ins[a]
                cp = pltpu.make_async_remote_copy(
                    src_ref=src, dst_ref=outs[a].at[me],
                    send_sem=send_sems.at[a, k], recv_sem=recv_sems.at[a, k],
                    device_id=(px, py, pc), device_id_type=pl.DeviceIdType.MESH)
                cp.start()
                sends.append(cp)
        for a in range(n):
            for k, (px, py, pc) in enumerate(peers):
                landed = outs[a].at[4 * px + 2 * py + pc]
                pltpu.make_async_remote_copy(
                    src_ref=landed, dst_ref=landed,
                    send_sem=send_sems.at[a, k], recv_sem=recv_sems.at[a, k],
                    device_id=(px, py, pc), device_id_type=pl.DeviceIdType.MESH).wait_recv()
        for cp in sends:
            cp.wait_send()
        for loc in started:
            loc.wait()

    hbm = pl.BlockSpec(memory_space=pl.ANY)
    return pl.pallas_call(
        body, name=name, out_shape=tuple(out_shape),
        in_specs=[hbm] * n, out_specs=tuple([hbm] * n),
        scratch_shapes=[pltpu.SemaphoreType.DMA((n, N_DEV - 1)),
                        pltpu.SemaphoreType.DMA((n, N_DEV - 1)),
                        pltpu.SemaphoreType.DMA((n,))],
    )(*arrs)


def _ffn_fwd(x, gain, wg, wu, wd, name, tm, tf):
    T, D = x.shape
    F = wg.shape[1]
    nj = F // tf

    def body(x_ref, g_ref, wg_ref, wu_ref, wd_ref, xo_ref, xn_ref, gate_ref, up_ref, xn_sc, acc_sc):
        j = pl.program_id(1)

        @pl.when(j == 0)
        def _():
            _, xhat = _rms_stats(x_ref[...])
            xn = (xhat * g_ref[...]).astype(BF16)
            xn_sc[...] = xn
            xn_ref[...] = xn
            acc_sc[...] = jnp.zeros_like(acc_sc)

        xn = xn_sc[...]
        g = _dot(xn, wg_ref[...])
        u = _dot(xn, wu_ref[...])
        gate_ref[...] = g.astype(BF16)
        up_ref[...] = u.astype(BF16)
        h = (g * _sigmoid(g) * u).astype(BF16)
        acc_sc[...] += _dot(h, wd_ref[...])

        @pl.when(j == nj - 1)
        def _():
            xo_ref[...] = x_ref[...] + 0.5 * acc_sc[...]

    return pl.pallas_call(
        body, name=name, grid=(T // tm, nj),
        in_specs=[pl.BlockSpec((tm, D), lambda i, j: (i, 0)),
                  pl.BlockSpec((1, D), lambda i, j: (0, 0)),
                  pl.BlockSpec((D, tf), lambda i, j: (0, j)),
                  pl.BlockSpec((D, tf), lambda i, j: (0, j)),
                  pl.BlockSpec((tf, D), lambda i, j: (j, 0))],
        out_specs=(pl.BlockSpec((tm, D), lambda i, j: (i, 0)),
                   pl.BlockSpec((tm, D), lambda i, j: (i, 0)),
                   pl.BlockSpec((tm, tf), lambda i, j: (i, j)),
                   pl.BlockSpec((tm, tf), lambda i, j: (i, j))),
        out_shape=(jax.ShapeDtypeStruct((T, D), F32), jax.ShapeDtypeStruct((T, D), BF16),
                   jax.ShapeDtypeStruct((T, F), BF16), jax.ShapeDtypeStruct((T, F), BF16)),
        scratch_shapes=[pltpu.VMEM((tm, D), BF16), pltpu.VMEM((tm, D), F32)],
        compiler_params=_params("arbitrary", "arbitrary"),
    )(x, gain, wg, wu, wd)


def _ffn_bwd(dy, x, gain, gate, up, wgt, wut, wdt, name, tm, tf):
    T, D = x.shape
    F = gate.shape[1]
    nj = F // tf

    def body(dy_ref, x_ref, g_ref, gate_ref, up_ref, wgt_ref, wut_ref, wdt_ref,
             dx_ref, dgain_ref, dg_ref, du_ref, hh_ref, do_ref, do_sc, acc_sc):
        i, j = pl.program_id(0), pl.program_id(1)

        @pl.when((i == 0) & (j == 0))
        def _():
            dgain_ref[...] = jnp.zeros_like(dgain_ref)

        @pl.when(j == 0)
        def _():
            do = (0.5 * dy_ref[...]).astype(BF16)
            do_sc[...] = do
            do_ref[...] = do
            acc_sc[...] = jnp.zeros_like(acc_sc)

        dh = _dot(do_sc[...], wdt_ref[...])
        g = gate_ref[...].astype(F32)
        u = up_ref[...].astype(F32)
        sig = _sigmoid(g)
        s = g * sig
        dg = (dh * u * (sig * (1.0 + g * (1.0 - sig)))).astype(BF16)
        du = (dh * s).astype(BF16)
        dg_ref[...] = dg
        du_ref[...] = du
        hh_ref[...] = (s * u).astype(BF16)
        acc_sc[...] += _dot(dg, wgt_ref[...]) + _dot(du, wut_ref[...])

        @pl.when(j == nj - 1)
        def _():
            dx, dgain = _rms_bwd(acc_sc[...], x_ref[...], g_ref[...])
            dgain_ref[...] += dgain
            dx_ref[...] = dy_ref[...] + dx

    return pl.pallas_call(
        body, name=name, grid=(T // tm, nj),
        in_specs=[pl.BlockSpec((tm, D), lambda i, j: (i, 0)),
                  pl.BlockSpec((tm, D), lambda i, j: (i, 0)),
                  pl.BlockSpec((1, D), lambda i, j: (0, 0)),
                  pl.BlockSpec((tm, tf), lambda i, j: (i, j)),
                  pl.BlockSpec((tm, tf), lambda i, j: (i, j)),
                  pl.BlockSpec((tf, D), lambda i, j: (j, 0)),
                  pl.BlockSpec((tf, D), lambda i, j: (j, 0)),
                  pl.BlockSpec((D, tf), lambda i, j: (0, j))],
        out_specs=(pl.BlockSpec((tm, D), lambda i, j: (i, 0)),
                   pl.BlockSpec((1, D), lambda i, j: (0, 0)),
                   pl.BlockSpec((tm, tf), lambda i, j: (i, j)),
                   pl.BlockSpec((tm, tf), lambda i, j: (i, j)),
                   pl.BlockSpec((tm, tf), lambda i, j: (i, j)),
                   pl.BlockSpec((tm, D), lambda i, j: (i, 0))),
        out_shape=(jax.ShapeDtypeStruct((T, D), F32), jax.ShapeDtypeStruct((1, D), F32),
                   jax.ShapeDtypeStruct((T, F), BF16), jax.ShapeDtypeStruct((T, F), BF16),
                   jax.ShapeDtypeStruct((T, F), BF16), jax.ShapeDtypeStruct((T, D), BF16)),
        scratch_shapes=[pltpu.VMEM((tm, D), BF16), pltpu.VMEM((tm, D), F32)],
        compiler_params=_params("arbitrary", "arbitrary"),
    )(dy, x, gain, gate, up, wgt, wut, wdt)


def _tn_matmul(a, b, name, bm, bn, tk):
    T, M = a.shape
    N = b.shape[1]

    def body(a_ref, b_ref, o_ref):
        @pl.when(pl.program_id(2) == 0)
        def _():
            o_ref[...] = jnp.zeros_like(o_ref)

        o_ref[...] += _dot_tn(a_ref[...], b_ref[...])

    return pl.pallas_call(
        body, name=name, grid=(M // bm, N // bn, T // tk),
        in_specs=[pl.BlockSpec((tk, bm), lambda i, j, k: (k, i)),
                  pl.BlockSpec((tk, bn), lambda i, j, k: (k, j))],
        out_specs=pl.BlockSpec((bm, bn), lambda i, j, k: (i, j)),
        out_shape=jax.ShapeDtypeStruct((M, N), F32),
        compiler_params=_params("arbitrary", "arbitrary", "arbitrary"),
    )(a, b)


_Z_SPLITS = (0, 512, 640, 768, 1280, 1792, 2304)


def _mixin_fwd(x, gain, w_in, name, tm):
    T, D = x.shape
    widths = [b - a for a, b in zip(_Z_SPLITS[:-1], _Z_SPLITS[1:])]

    def body(x_ref, g_ref, w_ref, hn_ref, *outs):
        _, xhat = _rms_stats(x_ref[...])
        hn = (xhat * g_ref[...]).astype(BF16)
        hn_ref[...] = hn
        for o_ref, lo, hi in zip(outs, _Z_SPLITS[:-1], _Z_SPLITS[1:]):
            o_ref[...] = _dot(hn, w_ref[:, lo:hi]).astype(BF16)

    return pl.pallas_call(
        body, name=name, grid=(T // tm,),
        in_specs=[pl.BlockSpec((tm, D), lambda i: (i, 0)),
                  pl.BlockSpec((1, D), lambda i: (0, 0)),
                  pl.BlockSpec(w_in.shape, lambda i: (0, 0))],
        out_specs=tuple([pl.BlockSpec((tm, D), lambda i: (i, 0))]
                        + [pl.BlockSpec((tm, w), lambda i: (i, 0)) for w in widths]),
        out_shape=tuple([jax.ShapeDtypeStruct((T, D), BF16)]
                        + [jax.ShapeDtypeStruct((T, w), BF16) for w in widths]),
        compiler_params=_params("arbitrary"),
    )(x, gain, w_in)


def _bucket_table():
    qi = np.arange(BLOCK, dtype=np.int32)[:, None]
    kj = np.arange(2 * BLOCK, dtype=np.int32)[None, :]
    dist = qi + BLOCK - kj
    n = np.maximum(dist, 0)
    max_exact = NUM_BUCKETS // 2
    large = max_exact + (np.log(np.maximum(n, 1).astype(np.float32) / max_exact)
                         / math.log(MAX_DISTANCE / max_exact)
                         * (NUM_BUCKETS - max_exact)).astype(np.int32)
    large = np.minimum(large, NUM_BUCKETS - 1)
    bucket = np.where(n < max_exact, n, large).astype(np.int32)
    valid = (dist >= 0) & (dist < WINDOW)
    return np.where(valid, bucket, -1).astype(np.int32)


def _bias_build(table, bucket, name):
    def body(t_ref, b_ref, o_ref):
        bk = b_ref[...]
        for h in range(N_Q_HEADS):
            def step(b, acc):
                return jnp.where(bk == b, t_ref[b, h], acc)
            o_ref[h] = lax.fori_loop(0, NUM_BUCKETS, step, jnp.full(bk.shape, MASKED, F32))

    return pl.pallas_call(
        body, name=name,
        in_specs=[pl.BlockSpec(memory_space=pltpu.SMEM), pl.BlockSpec(memory_space=pltpu.VMEM)],
        out_specs=pl.BlockSpec(memory_space=pltpu.VMEM),
        out_shape=jax.ShapeDtypeStruct((N_Q_HEADS,) + bucket.shape, F32),
    )(table, bucket)


def _bias_grad(dbias, bucket, name):
    def body(d_ref, b_ref, o_ref):
        bk = b_ref[...]
        row = lax.broadcasted_iota(jnp.int32, o_ref.shape, 0)
        lane = lax.broadcasted_iota(jnp.int32, o_ref.shape, 1)
        res = jnp.zeros(o_ref.shape, F32)
        for h in range(N_Q_HEADS):
            d = d_ref[h]

            def step(b, acc):
                tot = jnp.sum(jnp.where(bk == b, d, 0.0), axis=1, keepdims=True)
                tot = jnp.sum(tot, axis=0, keepdims=True)
                return jnp.where((row == b) & (lane == h), tot, acc)
            res = lax.fori_loop(0, NUM_BUCKETS, step, res)
        o_ref[...] = res

    return pl.pallas_call(
        body, name=name,
        in_specs=[pl.BlockSpec(memory_space=pltpu.VMEM), pl.BlockSpec(memory_space=pltpu.VMEM)],
        out_specs=pl.BlockSpec(memory_space=pltpu.VMEM),
        out_shape=jax.ShapeDtypeStruct((NUM_BUCKETS, 128), F32),
    )(dbias, bucket)


def _softmax_parts(qh, kc, kp, bias_ref, h, sink, first):
    sc = _dot_nt(qh, kc) * SCALE + bias_ref[h, :, BLOCK:2 * BLOCK]
    sp = _dot_nt(qh, kp) * SCALE + bias_ref[h, :, 0:BLOCK]
    sp = jnp.where(first, MASKED, sp)
    m = jnp.maximum(jnp.maximum(jnp.max(sc, axis=-1, keepdims=True),
                                jnp.max(sp, axis=-1, keepdims=True)), sink)
    pc = jnp.exp(sc - m)
    pp = jnp.exp(sp - m)
    ps = jnp.exp(sink - m)
    r = 1.0 / (jnp.sum(pc, axis=-1, keepdims=True) + jnp.sum(pp, axis=-1, keepdims=True) + ps)
    return pc * r, pp * r, ps * r


def _attn_fwd(q, k, v, bias, sinks, gain, name):
    T = q.shape[0]
    nb = T // BLOCK

    def body(sink_ref, q_ref, kc_ref, kp_ref, vc_ref, vp_ref, bias_ref, g_ref, raw_ref, nrm_ref, o_sc):
        first = pl.program_id(0) == 0
        for h in range(N_Q_HEADS):
            kv = h // GQA_GROUP
            hs = slice(h * HEAD_DIM, (h + 1) * HEAD_DIM)
            ks = slice(kv * HEAD_DIM, (kv + 1) * HEAD_DIM)
            pc, pp, _ = _softmax_parts(q_ref[:, hs], kc_ref[:, ks], kp_ref[:, ks], bias_ref, h,
                                       sink_ref[h], first)
            o_sc[:, hs] = _dot(pc.astype(BF16), vc_ref[:, ks]) + _dot(pp.astype(BF16), vp_ref[:, ks])
        o = o_sc[...]
        raw_ref[...] = o.astype(BF16)
        _, ohat = _rms_stats(o)
        nrm_ref[...] = (ohat * g_ref[...]).astype(BF16)

    cur = lambda n: (n, 0)
    prev = lambda n: (jnp.maximum(n - 1, 0), 0)
    return pl.pallas_call(
        body, name=name, grid=(nb,),
        in_specs=[pl.BlockSpec(memory_space=pltpu.SMEM),
                  pl.BlockSpec((BLOCK, ATTN_WIDTH), cur),
                  pl.BlockSpec((BLOCK, KV_WIDTH), cur), pl.BlockSpec((BLOCK, KV_WIDTH), prev),
                  pl.BlockSpec((BLOCK, KV_WIDTH), cur), pl.BlockSpec((BLOCK, KV_WIDTH), prev),
                  pl.BlockSpec(bias.shape, lambda n: (0, 0, 0)),
                  pl.BlockSpec((1, ATTN_WIDTH), lambda n: (0, 0))],
        out_specs=(pl.BlockSpec((BLOCK, ATTN_WIDTH), cur), pl.BlockSpec((BLOCK, ATTN_WIDTH), cur)),
        out_shape=(jax.ShapeDtypeStruct((T, ATTN_WIDTH), BF16), jax.ShapeDtypeStruct((T, ATTN_WIDTH), BF16)),
        scratch_shapes=[pltpu.VMEM((BLOCK, ATTN_WIDTH), F32)],
        compiler_params=_params("arbitrary"),
    )(sinks, q, k, k, v, v, bias, gain)


def _attn_bwd(dmixed, raw, q, k, v, bias, sinks, gain, name):
    T = q.shape[0]
    nb = T // BLOCK

    def body(sink_ref, dm_ref, raw_ref, q_ref, kc_ref, kp_ref, vc_ref, vp_ref, bias_ref, g_ref,
             dq_ref, dk_ref, dv_ref, dbias_ref, dsink_ref, dgain_ref,
             do_sc, dq_sc, kcur_sc, kprev_sc, vcur_sc, vprev_sc, kcarry_sc, vcarry_sc):
        i = pl.program_id(0)
        first = i == nb - 1

        @pl.when(i == 0)
        def _():
            kcarry_sc[...] = jnp.zeros_like(kcarry_sc)
            vcarry_sc[...] = jnp.zeros_like(vcarry_sc)
            dbias_ref[...] = jnp.zeros_like(dbias_ref)
            dsink_ref[...] = jnp.zeros_like(dsink_ref)
            dgain_ref[...] = jnp.zeros_like(dgain_ref)

        o = raw_ref[...].astype(F32)
        do, dgain = _rms_bwd(dm_ref[...].astype(F32), o, g_ref[...])
        dgain_ref[...] += dgain
        do_sc[...] = do
        lane = lax.broadcasted_iota(jnp.int32, dsink_ref.shape, 1)
        dsink = jnp.zeros(dsink_ref.shape, F32)
        for kv in range(N_KV_HEADS):
            ks = slice(kv * HEAD_DIM, (kv + 1) * HEAD_DIM)
            kc, kp, vc, vp = kc_ref[:, ks], kp_ref[:, ks], vc_ref[:, ks], vp_ref[:, ks]
            dkc = jnp.zeros((BLOCK, HEAD_DIM), F32)
            dkp = jnp.zeros((BLOCK, HEAD_DIM), F32)
            dvc = jnp.zeros((BLOCK, HEAD_DIM), F32)
            dvp = jnp.zeros((BLOCK, HEAD_DIM), F32)
            for g in range(GQA_GROUP):
                h = kv * GQA_GROUP + g
                hs = slice(h * HEAD_DIM, (h + 1) * HEAD_DIM)
                qh = q_ref[:, hs]
                pc, pp, ps = _softmax_parts(qh, kc, kp, bias_ref, h, sink_ref[h], first)
                doh = do_sc[:, hs]
                rowdot = jnp.sum(doh * raw_ref[:, hs].astype(F32), axis=-1, keepdims=True)
                dohb = doh.astype(BF16)
                dsc = pc * (_dot_nt(dohb, vc) - rowdot)
                dsp = pp * (_dot_nt(dohb, vp) - rowdot)
                dsink = dsink + jnp.where(lane == h, -jnp.sum(ps * rowdot, axis=0, keepdims=True), 0.0)
                dbias_ref[h, :, BLOCK:2 * BLOCK] += dsc
                dbias_ref[h, :, 0:BLOCK] += dsp
                dscb, dspb = dsc.astype(BF16), dsp.astype(BF16)
                dq_sc[:, hs] = (_dot(dscb, kc) + _dot(dspb, kp)) * SCALE
                dkc += _dot_tn(dscb, qh)
                dkp += _dot_tn(dspb, qh)
                dvc += _dot_tn(pc.astype(BF16), dohb)
                dvp += _dot_tn(pp.astype(BF16), dohb)
            kcur_sc[:, ks] = dkc * SCALE
            kprev_sc[:, ks] = dkp * SCALE
            vcur_sc[:, ks] = dvc
            vprev_sc[:, ks] = dvp
        dsink_ref[...] += dsink
        dq_ref[...] = dq_sc[...].astype(BF16)
        dk_ref[...] = (kcarry_sc[...] + kcur_sc[...]).astype(BF16)
        dv_ref[...] = (vcarry_sc[...] + vcur_sc[...]).astype(BF16)
        kcarry_sc[...] = kprev_sc[...]
        vcarry_sc[...] = vprev_sc[...]

    cur = lambda i: (nb - 1 - i, 0)
    prev = lambda i: (jnp.maximum(nb - 2 - i, 0), 0)
    const2 = lambda i: (0, 0)
    blk = lambda w: pltpu.VMEM((BLOCK, w), F32)
    return pl.pallas_call(
        body, name=name, grid=(nb,),
        in_specs=[pl.BlockSpec(memory_space=pltpu.SMEM),
                  pl.BlockSpec((BLOCK, ATTN_WIDTH), cur),
                  pl.BlockSpec((BLOCK, ATTN_WIDTH), cur),
                  pl.BlockSpec((BLOCK, ATTN_WIDTH), cur),
                  pl.BlockSpec((BLOCK, KV_WIDTH), cur), pl.BlockSpec((BLOCK, KV_WIDTH), prev),
                  pl.BlockSpec((BLOCK, KV_WIDTH), cur), pl.BlockSpec((BLOCK, KV_WIDTH), prev),
                  pl.BlockSpec(bias.shape, lambda i: (0, 0, 0)),
                  pl.BlockSpec((1, ATTN_WIDTH), const2)],
        out_specs=(pl.BlockSpec((BLOCK, ATTN_WIDTH), cur),
                   pl.BlockSpec((BLOCK, KV_WIDTH), cur), pl.BlockSpec((BLOCK, KV_WIDTH), cur),
                   pl.BlockSpec(bias.shape, lambda i: (0, 0, 0)),
                   pl.BlockSpec((SUBLANES, 128), const2),
                   pl.BlockSpec((1, ATTN_WIDTH), const2)),
        out_shape=(jax.ShapeDtypeStruct((T, ATTN_WIDTH), BF16),
                   jax.ShapeDtypeStruct((T, KV_WIDTH), BF16), jax.ShapeDtypeStruct((T, KV_WIDTH), BF16),
                   jax.ShapeDtypeStruct(bias.shape, F32),
                   jax.ShapeDtypeStruct((SUBLANES, 128), F32),
                   jax.ShapeDtypeStruct((1, ATTN_WIDTH), F32)),
        scratch_shapes=[blk(ATTN_WIDTH), blk(ATTN_WIDTH), blk(KV_WIDTH), blk(KV_WIDTH), blk(KV_WIDTH),
                        blk(KV_WIDTH), blk(KV_WIDTH), blk(KV_WIDTH)],
        compiler_params=_params("arbitrary"),
    )(sinks, dmixed, raw, q, k, k, v, v, bias, gain)


def _shift_down(cu, tail):
    row = lax.broadcasted_iota(jnp.int32, cu.shape, 0)
    t6, t7 = tail[6:7, :], tail[7:8, :]
    s1 = jnp.where(row == 0, t7, pltpu.roll(cu, 1, 0))
    s2 = jnp.where(row == 0, t6, jnp.where(row == 1, t7, pltpu.roll(cu, 2, 0)))
    return s1, s2


def _shift_up(d, head):
    n = d.shape[0]
    row = lax.broadcasted_iota(jnp.int32, d.shape, 0)
    h0, h1 = head[0:1, :], head[1:2, :]
    s1 = jnp.where(row == n - 1, h0, pltpu.roll(d, n - 1, 0))
    s2 = jnp.where(row == n - 1, h1, jnp.where(row == n - 2, h0, pltpu.roll(d, n - 2, 0)))
    return s1, s2


def _mixout_fwd(x, attn_n, u, gb, gc, conv_w, gain, w_out, name, tm):
    T, D = x.shape

    def body(x_ref, an_ref, u_ref, b_ref, c_ref, cw_ref, g_ref, wo_ref, xo_ref, cn_ref, tail_sc):
        @pl.when(pl.program_id(0) == 0)
        def _():
            tail_sc[...] = jnp.zeros_like(tail_sc)

        cu = c_ref[...].astype(F32) * u_ref[...].astype(F32)
        s1, s2 = _shift_down(cu, tail_sc[...])
        tail_sc[...] = cu[tm - SUBLANES:tm, :]
        pre = cw_ref[0:1, :] * s2 + cw_ref[1:2, :] * s1 + cw_ref[2:3, :] * cu
        conv = b_ref[...].astype(F32) * pre
        _, chat = _rms_stats(conv)
        cn = (chat * g_ref[...]).astype(BF16)
        cn_ref[...] = cn
        xo_ref[...] = (x_ref[...] + _dot(an_ref[...], wo_ref[0:ATTN_WIDTH, :])
                       + _dot(cn, wo_ref[ATTN_WIDTH:ATTN_WIDTH + CONV_DIM, :]))

    row = lambda i: (i, 0)
    const = lambda i: (0, 0)
    return pl.pallas_call(
        body, name=name, grid=(T // tm,),
        in_specs=[pl.BlockSpec((tm, D), row), pl.BlockSpec((tm, ATTN_WIDTH), row),
                  pl.BlockSpec((tm, CONV_DIM), row), pl.BlockSpec((tm, CONV_DIM), row),
                  pl.BlockSpec((tm, CONV_DIM), row),
                  pl.BlockSpec(conv_w.shape, const), pl.BlockSpec((1, CONV_DIM), const),
                  pl.BlockSpec(w_out.shape, const)],
        out_specs=(pl.BlockSpec((tm, D), row), pl.BlockSpec((tm, CONV_DIM), row)),
        out_shape=(jax.ShapeDtypeStruct((T, D), F32), jax.ShapeDtypeStruct((T, CONV_DIM), BF16)),
        scratch_shapes=[pltpu.VMEM((SUBLANES, CONV_DIM), F32)],
        compiler_params=_params("arbitrary"),
    )(x, attn_n, u, gb, gc, conv_w, gain, w_out)


def _mixout_bwd(dy, attn_n, conv_n, w_out_t, name, tm):
    T, D = dy.shape
    W = ATTN_WIDTH + CONV_DIM

    def body(dy_ref, an_ref, cn_ref, wt_ref, dm_ref, dw_ref):
        @pl.when(pl.program_id(0) == 0)
        def _():
            dw_ref[...] = jnp.zeros_like(dw_ref)

        dyb = dy_ref[...].astype(BF16)
        dm_ref[...] = _dot(dyb, wt_ref[...]).astype(BF16)
        dw_ref[0:ATTN_WIDTH, :] += _dot_tn(an_ref[...], dyb)
        dw_ref[ATTN_WIDTH:W, :] += _dot_tn(cn_ref[...], dyb)

    row = lambda i: (i, 0)
    const = lambda i: (0, 0)
    return pl.pallas_call(
        body, name=name, grid=(T // tm,),
        in_specs=[pl.BlockSpec((tm, D), row), pl.BlockSpec((tm, ATTN_WIDTH), row),
                  pl.BlockSpec((tm, CONV_DIM), row), pl.BlockSpec(w_out_t.shape, const)],
        out_specs=(pl.BlockSpec((tm, W), row), pl.BlockSpec((W, D), const)),
        out_shape=(jax.ShapeDtypeStruct((T, W), BF16), jax.ShapeDtypeStruct((W, D), F32)),
        compiler_params=_params("arbitrary"),
    )(dy, attn_n, conv_n, w_out_t)


def _conv_bwd(dmixed, u, gb, gc, conv_w, gain, name, tc):
    T = u.shape[0]
    nt = T // tc
    per8 = tc // SUBLANES

    def body(dm_ref, u_ref, b_ref, c_ref, ut_ref, ct_ref, cw_ref, g_ref,
             du_ref, db_ref, dc_ref, dcw_ref, dgain_ref, head_sc):
        i = pl.program_id(0)

        @pl.when(i == 0)
        def _():
            head_sc[...] = jnp.zeros_like(head_sc)
            dcw_ref[...] = jnp.zeros_like(dcw_ref)
            dgain_ref[...] = jnp.zeros_like(dgain_ref)

        uu = u_ref[...].astype(F32)
        cc = c_ref[...].astype(F32)
        bb = b_ref[...].astype(F32)
        cu = cc * uu
        tail = jnp.where(i == nt - 1, 0.0, ct_ref[...].astype(F32) * ut_ref[...].astype(F32))
        s1, s2 = _shift_down(cu, tail)
        w0, w1, w2 = cw_ref[0:1, :], cw_ref[1:2, :], cw_ref[2:3, :]
        pre = w0 * s2 + w1 * s1 + w2 * cu
        dconv, dgain = _rms_bwd(dm_ref[...].astype(F32), bb * pre, g_ref[...])
        dgain_ref[...] += dgain
        db_ref[...] = (dconv * pre).astype(BF16)
        dpre = dconv * bb
        dcw_ref[0:1, :] += jnp.sum(dpre * s2, axis=0, keepdims=True)
        dcw_ref[1:2, :] += jnp.sum(dpre * s1, axis=0, keepdims=True)
        dcw_ref[2:3, :] += jnp.sum(dpre * cu, axis=0, keepdims=True)
        n1, n2 = _shift_up(dpre, head_sc[...])
        head_sc[...] = dpre[0:SUBLANES, :]
        dcu = w2 * dpre + w1 * n1 + w0 * n2
        du_ref[...] = (dcu * cc).astype(BF16)
        dc_ref[...] = (dcu * uu).astype(BF16)

    rev = lambda i: (nt - 1 - i, 0)
    rev_right = lambda i: (nt - 1 - i, 1)
    tail_map = lambda i: (jnp.maximum((nt - 1 - i) * per8 - 1, 0), 0)
    const = lambda i: (0, 0)
    return pl.pallas_call(
        body, name=name, grid=(nt,),
        in_specs=[pl.BlockSpec((tc, CONV_DIM), rev_right),
                  pl.BlockSpec((tc, CONV_DIM), rev), pl.BlockSpec((tc, CONV_DIM), rev),
                  pl.BlockSpec((tc, CONV_DIM), rev),
                  pl.BlockSpec((SUBLANES, CONV_DIM), tail_map), pl.BlockSpec((SUBLANES, CONV_DIM), tail_map),
                  pl.BlockSpec(conv_w.shape, const), pl.BlockSpec((1, CONV_DIM), const)],
        out_specs=(pl.BlockSpec((tc, CONV_DIM), rev), pl.BlockSpec((tc, CONV_DIM), rev),
                   pl.BlockSpec((tc, CONV_DIM), rev),
                   pl.BlockSpec((SUBLANES, CONV_DIM), const), pl.BlockSpec((1, CONV_DIM), const)),
        out_shape=(jax.ShapeDtypeStruct((T, CONV_DIM), BF16), jax.ShapeDtypeStruct((T, CONV_DIM), BF16),
                   jax.ShapeDtypeStruct((T, CONV_DIM), BF16),
                   jax.ShapeDtypeStruct((SUBLANES, CONV_DIM), F32), jax.ShapeDtypeStruct((1, CONV_DIM), F32)),
        scratch_shapes=[pltpu.VMEM((SUBLANES, CONV_DIM), F32)],
        compiler_params=_params("arbitrary"),
    )(dmixed, u, gb, gc, u, gc, conv_w, gain)


def _mixin_bwd(dy, x, gain, dz, w_in_t, name, tm):
    T, D = x.shape
    nz = len(dz)

    def body(dy_ref, x_ref, g_ref, *rest):
        dz_refs, wt_ref, dx_ref, dgain_ref = rest[:nz], rest[nz], rest[nz + 1], rest[nz + 2]

        @pl.when(pl.program_id(0) == 0)
        def _():
            dgain_ref[...] = jnp.zeros_like(dgain_ref)

        dh = jnp.zeros((tm, D), F32)
        for r, lo, hi in zip(dz_refs, _Z_SPLITS[:-1], _Z_SPLITS[1:]):
            dh += _dot(r[...], wt_ref[lo:hi, :])
        dx, dgain = _rms_bwd(dh, x_ref[...], g_ref[...])
        dgain_ref[...] += dgain
        dx_ref[...] = dy_ref[...] + dx

    row = lambda i: (i, 0)
    const = lambda i: (0, 0)
    return pl.pallas_call(
        body, name=name, grid=(T // tm,),
        in_specs=[pl.BlockSpec((tm, D), row), pl.BlockSpec((tm, D), row), pl.BlockSpec((1, D), const)]
                 + [pl.BlockSpec((tm, a.shape[1]), row) for a in dz]
                 + [pl.BlockSpec(w_in_t.shape, const)],
        out_specs=(pl.BlockSpec((tm, D), row), pl.BlockSpec((1, D), const)),
        out_shape=(jax.ShapeDtypeStruct((T, D), F32), jax.ShapeDtypeStruct((1, D), F32)),
        compiler_params=_params("arbitrary"),
    )(dy, x, gain, *dz, w_in_t)


def _loss_head(x, target, gain, name, tm):
    T, D = x.shape

    def body(x_ref, t_ref, g_ref, dx_ref, dgain_ref, loss_ref):
        @pl.when(pl.program_id(0) == 0)
        def _():
            dgain_ref[...] = jnp.zeros_like(dgain_ref)
            loss_ref[...] = jnp.zeros_like(loss_ref)

        xv = x_ref[...]
        gain_v = g_ref[...]
        _, xhat = _rms_stats(xv)
        err = xhat * gain_v - t_ref[...]
        part = 0.5 * jnp.sum(jnp.mean(err * err, axis=-1, keepdims=True), axis=0, keepdims=True)
        loss_ref[...] += part
        dx, dgain = _rms_bwd(err * (1.0 / D), xv, gain_v)
        dgain_ref[...] += dgain
        dx_ref[...] = dx

    row = lambda i: (i, 0)
    const = lambda i: (0, 0)
    return pl.pallas_call(
        body, name=name, grid=(T // tm,),
        in_specs=[pl.BlockSpec((tm, D), row), pl.BlockSpec((tm, D), row), pl.BlockSpec((1, D), const)],
        out_specs=(pl.BlockSpec((tm, D), row), pl.BlockSpec((1, D), const),
                   pl.BlockSpec((SUBLANES, 128), const)),
        out_shape=(jax.ShapeDtypeStruct((T, D), F32), jax.ShapeDtypeStruct((1, D), F32),
                   jax.ShapeDtypeStruct((SUBLANES, 128), F32)),
        compiler_params=_params("arbitrary"),
    )(x, target, gain)


def _adamw(parts, w, m, v, name, tr):
    R, C = w.shape

    def body(p_ref, w_ref, m_ref, v_ref, g_ref, d_ref, nm_ref, nv_ref):
        g = p_ref[0].astype(F32)
        for d in range(1, N_DEV):
            g = g + p_ref[d].astype(F32)
        nm = ADAM_B1 * m_ref[...] + (1.0 - ADAM_B1) * g
        nv = ADAM_B2 * v_ref[...] + (1.0 - ADAM_B2) * (g * g)
        m_hat = nm / (1.0 - ADAM_B1 ** ADAM_STEP)
        v_hat = nv / (1.0 - ADAM_B2 ** ADAM_STEP)
        g_ref[...] = g
        nm_ref[...] = nm
        nv_ref[...] = nv
        d_ref[...] = -ADAM_LR * (m_hat / (jnp.sqrt(v_hat) + ADAM_EPS) + ADAM_WD * w_ref[...])

    row = lambda i: (i, 0)
    spec = pl.BlockSpec((tr, C), row)
    shp = jax.ShapeDtypeStruct((R, C), F32)
    return pl.pallas_call(
        body, name=name, grid=(R // tr,),
        in_specs=[pl.BlockSpec((N_DEV, tr, C), lambda i: (0, i, 0)), spec, spec, spec],
        out_specs=(spec, spec, spec, spec),
        out_shape=(shp, shp, shp, shp),
        compiler_params=_params("arbitrary"),
    )(parts, w, m, v)


def _blocks_of_columns(g, n):
    R, C = g.shape
    return g.reshape(R, n, C // n).transpose(1, 0, 2)


def _columns_of_blocks(g):
    n, R, w = g.shape
    return g.transpose(1, 0, 2).reshape(R, n * w)


def _pad_row(vec):
    vec = vec.reshape(1, -1)
    return jnp.pad(vec, ((0, 0), (0, PACK_COLS - vec.shape[1])))


def kernel(x, rel_bias_table, ffn1_norm, ffn1_w_gate, ffn1_w_up, ffn1_w_down, mix_norm, w_in, conv_w, attn_sinks, attn_out_norm, conv_out_norm, w_out, ffn2_norm, ffn2_w_gate, ffn2_w_up, ffn2_w_down, final_norm, loss_target, m_rel_bias_table, m_ffn1_norm, m_ffn1_w_gate, m_ffn1_w_up, m_ffn1_w_down, m_mix_norm, m_w_in, m_conv_w, m_attn_sinks, m_attn_out_norm, m_conv_out_norm, m_w_out, m_ffn2_norm, m_ffn2_w_gate, m_ffn2_w_up, m_ffn2_w_down, m_final_norm, v_rel_bias_table, v_ffn1_norm, v_ffn1_w_gate, v_ffn1_w_up, v_ffn1_w_down, v_mix_norm, v_w_in, v_conv_w, v_attn_sinks, v_attn_out_norm, v_conv_out_norm, v_w_out, v_ffn2_norm, v_ffn2_w_gate, v_ffn2_w_up, v_ffn2_w_down, v_final_norm):
    T, D = x.shape[1], x.shape[2]
    x0 = x[0]
    target = loss_target[0]
    tm = min(TM_FFN, T)
    tm_bwd = min(TM_FFN_BWD, T)
    tm_mix = min(TM_MIX, T)
    tk = min(TK_WGRAD, T)
    tf = TF_FFN
    me = 4 * lax.axis_index("x") + 2 * lax.axis_index("y") + lax.axis_index("c")

    big = [ffn1_w_gate[0], ffn1_w_up[0], ffn1_w_down[0], w_in[0], w_out[0],
           ffn2_w_gate[0], ffn2_w_up[0], ffn2_w_down[0]]
    gathered = _exchange([w.astype(BF16) for w in big] + [conv_w[0]], False, "gather_weights")
    wg1 = _columns_of_blocks(gathered[0])
    wu1 = _columns_of_blocks(gathered[1])
    wd1 = gathered[2].reshape(-1, D)
    win = _columns_of_blocks(gathered[3])
    wout = gathered[4].reshape(-1, D)
    wg2 = _columns_of_blocks(gathered[5])
    wu2 = _columns_of_blocks(gathered[6])
    wd2 = gathered[7].reshape(-1, D)
    cw = _columns_of_blocks(gathered[8])
    F = wg1.shape[1]
    bf = F // 2

    x1, xn1, gate1, up1 = _ffn_fwd(x0, ffn1_norm, wg1, wu1, wd1, "ffn1_fwd", tm, tf)
    hn, q, k, v, u, gb, gc = _mixin_fwd(x1, mix_norm, win, "mixin_fwd", tm_mix)
    bucket = jnp.asarray(_bucket_table())
    sinks = attn_sinks.reshape(-1)
    bias = _bias_build(rel_bias_table, bucket, "bias_build")
    attn_raw, attn_n = _attn_fwd(q, k, v, bias, sinks, attn_out_norm, "attn_fwd")
    x2, conv_n = _mixout_fwd(x1, attn_n, u, gb, gc, cw, conv_out_norm, wout, "mixout_fwd", tm_mix)
    x3, xn2, gate2, up2 = _ffn_fwd(x2, ffn2_norm, wg2, wu2, wd2, "ffn2_fwd", tm, tf)

    dx3, d_final, loss_part = _loss_head(x3, target, final_norm.reshape(1, D), "loss_head", tm_mix)
    dx2, d_ffn2_norm, dg2, du2, hh2, do2 = _ffn_bwd(
        dx3, x2, ffn2_norm, gate2, up2, wg2.T, wu2.T, wd2.T, "ffn2_bwd", tm_bwd, tf)
    d_wg2 = _tn_matmul(xn2, dg2, "ffn2_dwg", D, bf, tk)
    d_wu2 = _tn_matmul(xn2, du2, "ffn2_dwu", D, bf, tk)
    d_wd2 = _tn_matmul(hh2, do2, "ffn2_dwd", bf, D, tk)

    dmixed, d_wout = _mixout_bwd(dx2, attn_n, conv_n, wout.T, "mixout_bwd", tm_mix)
    dq, dk, dv, dbias, dsink, d_attn_norm = _attn_bwd(
        dmixed, attn_raw, q, k, v, bias, sinks, attn_out_norm, "attn_bwd")
    du, dgb, dgc, d_cw, d_conv_norm = _conv_bwd(dmixed, u, gb, gc, cw, conv_out_norm, "conv_bwd", tm_mix)
    d_table = _bias_grad(dbias, bucket, "bias_grad")
    dz = [dq, dk, dv, du, dgb, dgc]
    dx1, d_mix_norm = _mixin_bwd(dx2, x1, mix_norm, dz, win.T, "mixin_bwd", tm_mix)
    d_win = jnp.concatenate(
        [_tn_matmul(hn, p, "dwin_%d" % n, D, p.shape[1], tk) for n, p in enumerate(dz)], axis=1)

    dx0, d_ffn1_norm, dg1, du1, hh1, do1 = _ffn_bwd(
        dx1, x0, ffn1_norm, gate1, up1, wg1.T, wu1.T, wd1.T, "ffn1_bwd", tm_bwd, tf)
    d_wg1 = _tn_matmul(xn1, dg1, "ffn1_dwg", D, bf, tk)
    d_wu1 = _tn_matmul(xn1, du1, "ffn1_dwu", D, bf, tk)
    d_wd1 = _tn_matmul(hh1, do1, "ffn1_dwd", bf, D, tk)

    local_grads = [_blocks_of_columns(d_wg1, N_DEV), _blocks_of_columns(d_wu1, N_DEV),
                   d_wd1.reshape(N_DEV, -1, D), _blocks_of_columns(d_win, N_DEV),
                   d_wout.reshape(N_DEV, -1, D),
                   _blocks_of_columns(d_wg2, N_DEV), _blocks_of_columns(d_wu2, N_DEV),
                   d_wd2.reshape(N_DEV, -1, D)]
    parts = _exchange([g.astype(BF16) for g in local_grads], True, "exchange_grads")
    big_m = [m_ffn1_w_gate, m_ffn1_w_up, m_ffn1_w_down, m_w_in, m_w_out,
             m_ffn2_w_gate, m_ffn2_w_up, m_ffn2_w_down]
    big_v = [v_ffn1_w_gate, v_ffn1_w_up, v_ffn1_w_down, v_w_in, v_w_out,
             v_ffn2_w_gate, v_ffn2_w_up, v_ffn2_w_down]
    big_names = ["ffn1_w_gate", "ffn1_w_up", "ffn1_w_down", "w_in", "w_out",
                 "ffn2_w_gate", "ffn2_w_up", "ffn2_w_down"]
    res = {}
    for name, p, w, m_, v_ in zip(big_names, parts, big, big_m, big_v):
        rows = w.shape[0]
        g, dl, nm, nv = _adamw(p, w, m_[0], v_[0], "adamw_" + name, _row_tile(rows, ADAM_ROWS))
        res[name] = tuple(a[None] for a in (g, dl, nm, nv))

    def pack(ffn1, mixn, ffn2, fin, attn_n_, conv_n_, sink_, extra, convw, table):
        rows = [_pad_row(ffn1), _pad_row(mixn), _pad_row(ffn2), _pad_row(fin),
                _pad_row(jnp.concatenate([attn_n_.reshape(-1), conv_n_.reshape(-1)])),
                _pad_row(sink_), _pad_row(extra),
                jnp.zeros((1, PACK_COLS), F32),
                jnp.pad(convw, ((0, 0), (0, PACK_COLS - convw.shape[1]))),
                _pad_row(table),
                jnp.zeros((PACK_ROWS - 12, PACK_COLS), F32)]
        return jnp.concatenate(rows, axis=0)

    def own_channels(a):
        full = jnp.zeros((a.shape[1], CONV_DIM), F32)
        return lax.dynamic_update_slice(full, a[0], (0, me * a.shape[2]))

    g_pack = pack(d_ffn1_norm, d_mix_norm, d_ffn2_norm, d_final, d_attn_norm, d_conv_norm,
                  dsink[0, :N_Q_HEADS], loss_part[0, :1], d_cw[:3], d_table[:, :N_Q_HEADS])
    zero1 = jnp.zeros((1,), F32)
    w_pack = pack(ffn1_norm, mix_norm, ffn2_norm, final_norm, attn_out_norm, conv_out_norm,
                  attn_sinks, zero1, own_channels(conv_w), rel_bias_table)
    m_pack = pack(m_ffn1_norm, m_mix_norm, m_ffn2_norm, m_final_norm, m_attn_out_norm, m_conv_out_norm,
                  m_attn_sinks, zero1, own_channels(m_conv_w), m_rel_bias_table)
    v_pack = pack(v_ffn1_norm, v_mix_norm, v_ffn2_norm, v_final_norm, v_attn_out_norm, v_conv_out_norm,
                  v_attn_sinks, zero1, own_channels(v_conv_w), v_rel_bias_table)
    (g_all,) = _exchange([g_pack], False, "gather_small")
    packs = _adamw(g_all, w_pack, m_pack, v_pack, "adamw_small", PACK_ROWS)

    def unpack(pk):
        cwb = lax.dynamic_slice(pk[8:11, :CONV_DIM], (0, me * conv_w.shape[2]), (3, conv_w.shape[2]))
        return {
            "ffn1_norm": pk[0:1, :D], "mix_norm": pk[1:2, :D], "ffn2_norm": pk[2:3, :D],
            "final_norm": pk[3, :D],
            "attn_out_norm": pk[4:5, :ATTN_WIDTH], "conv_out_norm": pk[4:5, ATTN_WIDTH:ATTN_WIDTH + CONV_DIM],
            "attn_sinks": pk[5:6, :N_Q_HEADS],
            "conv_w": cwb[None],
            "rel_bias_table": pk[11, :NUM_BUCKETS * N_Q_HEADS].reshape(NUM_BUCKETS, N_Q_HEADS),
        }

    small = [unpack(pk) for pk in packs]
    loss = packs[0][6, 0]

    order = ["rel_bias_table", "ffn1_norm", "ffn1_w_gate", "ffn1_w_up", "ffn1_w_down", "mix_norm", "w_in",
             "conv_w", "attn_sinks", "attn_out_norm", "conv_out_norm", "w_out", "ffn2_norm",
             "ffn2_w_gate", "ffn2_w_up", "ffn2_w_down", "final_norm"]
    outs = [loss, dx0[None]]
    for kind in range(4):
        for name in order:
            outs.append(res[name][kind] if name in res else small[kind][name])
    return tuple(outs)
```

```python
import math

import numpy as np
import jax
import jax.numpy as jnp
from jax import lax
from jax.experimental import pallas as pl
from jax.experimental.pallas import tpu as pltpu

F32 = jnp.float32
BF16 = jnp.bfloat16

N_DEV = 8
EPS = 1e-6
HEAD_DIM = 64
N_Q_HEADS = 8
N_KV_HEADS = 2
GQA_GROUP = 4
ATTN_WIDTH = 512
KV_WIDTH = 128
CONV_DIM = 512
BLOCK = 128
WINDOW = 128
NUM_BUCKETS = 32
MAX_DISTANCE = 128
SCALE = HEAD_DIM ** -0.5
MASKED = -1e30

ADAM_LR = 0.001
ADAM_B1 = 0.9
ADAM_B2 = 0.999
ADAM_EPS = 1e-08
ADAM_WD = 0.01
ADAM_STEP = 10

VMEM_LIMIT_BYTES = 40 * 1024 * 1024
SUBLANES = 8
PACK_ROWS = 16
PACK_COLS = 1024

TM_FFN = 1024
TM_FFN_BWD = 512
TM_MIX = 512
TK_WGRAD = 512
TF_FFN = 256
ADAM_ROWS = 256


def _row_tile(rows, limit):
    best = SUBLANES
    for t in range(SUBLANES, min(rows, limit) + 1, SUBLANES):
        if rows % t == 0:
            best = t
    return best


def _params(*sem):
    return pltpu.CompilerParams(dimension_semantics=sem, vmem_limit_bytes=VMEM_LIMIT_BYTES)


def _dot(a, b):
    return jnp.dot(a, b, preferred_element_type=F32)


def _dot_nt(a, b):
    return lax.dot_general(a, b, (((1,), (1,)), ((), ())), preferred_element_type=F32)


def _dot_tn(a, b):
    return lax.dot_general(a, b, (((0,), (0,)), ((), ())), preferred_element_type=F32)


def _sigmoid(g):
    return 1.0 / (1.0 + jnp.exp(-g))


def _rms_stats(x):
    inv = lax.rsqrt(jnp.mean(x * x, axis=-1, keepdims=True) + EPS)
    return inv, x * inv


def _rms_bwd(dy, x, gain):
    inv, xhat = _rms_stats(x)
    dgain = jnp.sum(dy * xhat, axis=0, keepdims=True)
    dxh = dy * gain
    dx = inv * (dxh - xhat * jnp.mean(dxh * xhat, axis=-1, keepdims=True))
    return dx, dgain


def _exchange(arrs, scatter, name):
    n = len(arrs)
    out_shape = []
    for a in arrs:
        shp = a.shape if scatter else (N_DEV,) + a.shape
        out_shape.append(jax.ShapeDtypeStruct(shp, a.dtype))

    def body(*refs):
        ins, outs = refs[:n], refs[n:2 * n]
        send_sems, recv_sems, local_sems = refs[2 * n:]
        x, y, c = lax.axis_index("x"), lax.axis_index("y"), lax.axis_index("c")
        me = 4 * x + 2 * y + c
        peers = []
        for k in range(1, N_DEV):
            px = 1 - x if (k >> 2) & 1 else x
            py = 1 - y if (k >> 1) & 1 else y
            pc = 1 - c if k & 1 else c
            peers.append((px, py, pc))
        started = []
        for a in range(n):
            own = ins[a].at[me] if scatter else ins[a]
            loc = pltpu.make_async_copy(own, outs[a].at[me], local_sems.at[a])
            loc.start()
            started.append(loc)
        sends = []
        for a in range(n):
            for k, (px, py, pc) in enumerate(peers):
                src = ins[a].at[4 * px + 2 * py + pc] if scatter else ins[a]
                cp = pltpu.make_async_remote_copy(
                    src_ref=src, dst_ref=outs[a].at[me],
                    send_sem=send_sems.at[a, k], recv_sem=recv_sems.at[a, k],
                    device_id=(px, py, pc), device_id_type=pl.DeviceIdType.MESH)
                cp.start()
                sends.append(cp)
        for a in range(n):
            for k, (px, py, pc) in enumerate(peers):
                landed = outs[a].at[4 * px + 2 * py + pc]
                pltpu.make_async_remote_copy(
                    src_ref=landed, dst_ref=landed,
                    send_sem=send_sems.at[a, k], recv_sem=recv_sems.at[a, k],
                    device_id=(px, py, pc), device_id_type=pl.DeviceIdType.MESH).wait_recv()
        for cp in sends:
            cp.wait_send()
        for loc in started:
            loc.wait()

    hbm = pl.BlockSpec(memory_space=pl.ANY)
    return pl.pallas_call(
        body, name=name, out_shape=tuple(out_shape),
        in_specs=[hbm] * n, out_specs=tuple([hbm] * n),
        scratch_shapes=[pltpu.SemaphoreType.DMA((n, N_DEV - 1)),
                        pltpu.SemaphoreType.DMA((n, N_DEV - 1)),
                        pltpu.SemaphoreType.DMA((n,))],
    )(*arrs)


def _peer_list():
    x, y, c = lax.axis_index("x"), lax.axis_index("y"), lax.axis_index("c")
    peers = []
    for k in range(1, N_DEV):
        px = 1 - x if (k >> 2) & 1 else x
        py = 1 - y if (k >> 1) & 1 else y
        pc = 1 - c if k & 1 else c
        peers.append((px, py, pc))
    return 4 * x + 2 * y + c, peers


_HBM = pl.BlockSpec(memory_space=pltpu.HBM)
_SEM = pl.BlockSpec(memory_space=pltpu.SEMAPHORE)
_EFFECT = pltpu.SideEffectType.DATAFLOW_SIDE_EFFECTING


def _split_copies(srcs, lands, send_sems, recv_sems, scatter):
    me, peers = _peer_list()
    copies = []
    for a in range(len(srcs)):
        for k, (px, py, pc) in enumerate(peers):
            src = srcs[a].at[4 * px + 2 * py + pc] if scatter else srcs[a]
            copies.append(pltpu.make_async_remote_copy(
                src_ref=src, dst_ref=lands[a].at[me],
                send_sem=send_sems[a].at[k], recv_sem=recv_sems[a].at[k],
                device_id=(px, py, pc), device_id_type=pl.DeviceIdType.MESH))
    return copies


def _exchange_start(arrs, scatter, name, after):
    n = len(arrs)
    me = 4 * lax.axis_index("x") + 2 * lax.axis_index("y") + lax.axis_index("c")
    lands = []
    for a in arrs:
        own = lax.dynamic_index_in_dim(a, me, 0, keepdims=True) if scatter else a[None]
        shp = a.shape if scatter else (N_DEV,) + a.shape
        lands.append(lax.dynamic_update_slice(lax.empty(shp, a.dtype), own, (me,) + (0,) * (len(shp) - 1)))

    def body(*refs):
        srcs, lnds = refs[:n], refs[n:2 * n]
        outs = refs[2 * n + 1:]
        send_sems, recv_sems = outs[:n], outs[n:2 * n]
        token = outs[4 * n]
        for cp in _split_copies(srcs, lnds, send_sems, recv_sems, scatter):
            cp.start()
        token[...] = jnp.zeros_like(token)

    sem = pltpu.SemaphoreType.DMA((N_DEV - 1,))
    out_shape = ([sem] * (2 * n) + [pltpu.HBM(a.shape, a.dtype) for a in arrs]
                 + [pltpu.HBM(l.shape, l.dtype) for l in lands] + [jax.ShapeDtypeStruct((SUBLANES, 128), F32)])
    res = pl.pallas_call(
        body, name=name, out_shape=tuple(out_shape),
        in_specs=[_HBM] * (2 * n) + [pl.BlockSpec(memory_space=pl.ANY)],
        out_specs=tuple([_SEM] * (2 * n) + [_HBM] * (2 * n) + [pl.BlockSpec(memory_space=pltpu.VMEM)]),
        input_output_aliases={i: 2 * n + i for i in range(2 * n)},
        compiler_params=pltpu.CompilerParams(has_side_effects=_EFFECT),
    )(*[pltpu.with_memory_space_constraint(a, pltpu.HBM) for a in arrs],
      *[pltpu.with_memory_space_constraint(l, pltpu.HBM) for l in lands], after)
    handles = [(res[2 * n + a], res[3 * n + a], res[a], res[n + a]) for a in range(n)]
    return handles, res[4 * n]


def _exchange_wait(handles, scatter, name, after):
    n = len(handles)

    def body(*refs):
        srcs, lnds = refs[:n], refs[n:2 * n]
        send_sems, recv_sems = refs[2 * n:3 * n], refs[3 * n:4 * n]
        for cp in _split_copies(srcs, lnds, send_sems, recv_sems, scatter):
            cp.wait_send()
            cp.wait_recv()

    srcs = [h[0] for h in handles]
    lands = [h[1] for h in handles]
    res = pl.pallas_call(
        body, name=name,
        out_shape=tuple([pltpu.HBM(a.shape, a.dtype) for a in srcs] + [pltpu.HBM(l.shape, l.dtype) for l in lands]),
        in_specs=[_HBM] * (2 * n) + [_SEM] * (2 * n) + [pl.BlockSpec(memory_space=pl.ANY)],
        out_specs=tuple([_HBM] * (2 * n)),
        input_output_aliases={i: i for i in range(2 * n)},
        compiler_params=pltpu.CompilerParams(has_side_effects=_EFFECT),
    )(*srcs, *lands, *[h[2] for h in handles], *[h[3] for h in handles], after)
    return list(res[n:])


def _ffn_fwd(x, gain, wg, wu, wd, name, tm, tf):
    T, D = x.shape
    F = wg.shape[1]
    nj = F // tf

    def body(x_ref, g_ref, wg_ref, wu_ref, wd_ref, xo_ref, xn_ref, gate_ref, up_ref, xn_sc, acc_sc):
        j = pl.program_id(1)

        @pl.when(j == 0)
        def _():
            _, xhat = _rms_stats(x_ref[...])
            xn = (xhat * g_ref[...]).astype(BF16)
            xn_sc[...] = xn
            xn_ref[...] = xn
            acc_sc[...] = jnp.zeros_like(acc_sc)

        xn = xn_sc[...]
        g = _dot(xn, wg_ref[...])
        u = _dot(xn, wu_ref[...])
        gate_ref[...] = g.astype(BF16)
        up_ref[...] = u.astype(BF16)
        h = (g * _sigmoid(g) * u).astype(BF16)
        acc_sc[...] += _dot(h, wd_ref[...])

        @pl.when(j == nj - 1)
        def _():
            xo_ref[...] = x_ref[...] + 0.5 * acc_sc[...]

    return pl.pallas_call(
        body, name=name, grid=(T // tm, nj),
        in_specs=[pl.BlockSpec((tm, D), lambda i, j: (i, 0)),
                  pl.BlockSpec((1, D), lambda i, j: (0, 0)),
                  pl.BlockSpec((D, tf), lambda i, j: (0, j)),
                  pl.BlockSpec((D, tf), lambda i, j: (0, j)),
                  pl.BlockSpec((tf, D), lambda i, j: (j, 0))],
        out_specs=(pl.BlockSpec((tm, D), lambda i, j: (i, 0)),
                   pl.BlockSpec((tm, D), lambda i, j: (i, 0)),
                   pl.BlockSpec((tm, tf), lambda i, j: (i, j)),
                   pl.BlockSpec((tm, tf), lambda i, j: (i, j))),
        out_shape=(jax.ShapeDtypeStruct((T, D), F32), jax.ShapeDtypeStruct((T, D), BF16),
                   jax.ShapeDtypeStruct((T, F), BF16), jax.ShapeDtypeStruct((T, F), BF16)),
        scratch_shapes=[pltpu.VMEM((tm, D), BF16), pltpu.VMEM((tm, D), F32)],
        compiler_params=_params("arbitrary", "arbitrary"),
    )(x, gain, wg, wu, wd)


def _ffn_bwd(dy, x, gain, gate, up, wgt, wut, wdt, after, name, tm, tf):
    T, D = x.shape
    F = gate.shape[1]
    nj = F // tf

    def body(dy_ref, x_ref, g_ref, gate_ref, up_ref, wgt_ref, wut_ref, wdt_ref, after_ref,
             dx_ref, dgain_ref, dg_ref, du_ref, hh_ref, do_ref, do_sc, acc_sc):
        i, j = pl.program_id(0), pl.program_id(1)

        @pl.when((i == 0) & (j == 0))
        def _():
            dgain_ref[...] = jnp.zeros_like(dgain_ref)

        @pl.when(j == 0)
        def _():
            do = (0.5 * dy_ref[...]).astype(BF16)
            do_sc[...] = do
            do_ref[...] = do
            acc_sc[...] = jnp.zeros_like(acc_sc)

        dh = _dot(do_sc[...], wdt_ref[...])
        g = gate_ref[...].astype(F32)
        u = up_ref[...].astype(F32)
        sig = _sigmoid(g)
        s = g * sig
        dg = (dh * u * (sig * (1.0 + g * (1.0 - sig)))).astype(BF16)
        du = (dh * s).astype(BF16)
        dg_ref[...] = dg
        du_ref[...] = du
        hh_ref[...] = (s * u).astype(BF16)
        acc_sc[...] += _dot(dg, wgt_ref[...]) + _dot(du, wut_ref[...])

        @pl.when(j == nj - 1)
        def _():
            dx, dgain = _rms_bwd(acc_sc[...], x_ref[...], g_ref[...])
            dgain_ref[...] += dgain
            dx_ref[...] = dy_ref[...] + dx

    return pl.pallas_call(
        body, name=name, grid=(T // tm, nj),
        in_specs=[pl.BlockSpec((tm, D), lambda i, j: (i, 0)),
                  pl.BlockSpec((tm, D), lambda i, j: (i, 0)),
                  pl.BlockSpec((1, D), lambda i, j: (0, 0)),
                  pl.BlockSpec((tm, tf), lambda i, j: (i, j)),
                  pl.BlockSpec((tm, tf), lambda i, j: (i, j)),
                  pl.BlockSpec((tf, D), lambda i, j: (j, 0)),
                  pl.BlockSpec((tf, D), lambda i, j: (j, 0)),
                  pl.BlockSpec((D, tf), lambda i, j: (0, j)),
                  pl.BlockSpec(memory_space=pl.ANY)],
        out_specs=(pl.BlockSpec((tm, D), lambda i, j: (i, 0)),
                   pl.BlockSpec((1, D), lambda i, j: (0, 0)),
                   pl.BlockSpec((tm, tf), lambda i, j: (i, j)),
                   pl.BlockSpec((tm, tf), lambda i, j: (i, j)),
                   pl.BlockSpec((tm, tf), lambda i, j: (i, j)),
                   pl.BlockSpec((tm, D), lambda i, j: (i, 0))),
        out_shape=(jax.ShapeDtypeStruct((T, D), F32), jax.ShapeDtypeStruct((1, D), F32),
                   jax.ShapeDtypeStruct((T, F), BF16), jax.ShapeDtypeStruct((T, F), BF16),
                   jax.ShapeDtypeStruct((T, F), BF16), jax.ShapeDtypeStruct((T, D), BF16)),
        scratch_shapes=[pltpu.VMEM((tm, D), BF16), pltpu.VMEM((tm, D), F32)],
        compiler_params=_params("arbitrary", "arbitrary"),
    )(dy, x, gain, gate, up, wgt, wut, wdt, after)


def _tn_matmul(a, b, name, bm, bn, tk):
    T, M = a.shape
    N = b.shape[1]

    def body(a_ref, b_ref, o_ref):
        @pl.when(pl.program_id(2) == 0)
        def _():
            o_ref[...] = jnp.zeros_like(o_ref)

        o_ref[...] += _dot_tn(a_ref[...], b_ref[...])

    return pl.pallas_call(
        body, name=name, grid=(M // bm, N // bn, T // tk),
        in_specs=[pl.BlockSpec((tk, bm), lambda i, j, k: (k, i)),
                  pl.BlockSpec((tk, bn), lambda i, j, k: (k, j))],
        out_specs=pl.BlockSpec((bm, bn), lambda i, j, k: (i, j)),
        out_shape=jax.ShapeDtypeStruct((M, N), F32),
        compiler_params=_params("arbitrary", "arbitrary", "arbitrary"),
    )(a, b)


_Z_SPLITS = (0, 512, 640, 768, 1280, 1792, 2304)


def _mixin_fwd(x, gain, w_in, name, tm):
    T, D = x.shape
    widths = [b - a for a, b in zip(_Z_SPLITS[:-1], _Z_SPLITS[1:])]

    def body(x_ref, g_ref, w_ref, hn_ref, *outs):
        _, xhat = _rms_stats(x_ref[...])
        hn = (xhat * g_ref[...]).astype(BF16)
        hn_ref[...] = hn
        for o_ref, lo, hi in zip(outs, _Z_SPLITS[:-1], _Z_SPLITS[1:]):
            o_ref[...] = _dot(hn, w_ref[:, lo:hi]).astype(BF16)

    return pl.pallas_call(
        body, name=name, grid=(T // tm,),
        in_specs=[pl.BlockSpec((tm, D), lambda i: (i, 0)),
                  pl.BlockSpec((1, D), lambda i: (0, 0)),
                  pl.BlockSpec(w_in.shape, lambda i: (0, 0))],
        out_specs=tuple([pl.BlockSpec((tm, D), lambda i: (i, 0))]
                        + [pl.BlockSpec((tm, w), lambda i: (i, 0)) for w in widths]),
        out_shape=tuple([jax.ShapeDtypeStruct((T, D), BF16)]
                        + [jax.ShapeDtypeStruct((T, w), BF16) for w in widths]),
        compiler_params=_params("arbitrary"),
    )(x, gain, w_in)


def _bucket_table():
    qi = np.arange(BLOCK, dtype=np.int32)[:, None]
    kj = np.arange(2 * BLOCK, dtype=np.int32)[None, :]
    dist = qi + BLOCK - kj
    n = np.maximum(dist, 0)
    max_exact = NUM_BUCKETS // 2
    large = max_exact + (np.log(np.maximum(n, 1).astype(np.float32) / max_exact)
                         / math.log(MAX_DISTANCE / max_exact)
                         * (NUM_BUCKETS - max_exact)).astype(np.int32)
    large = np.minimum(large, NUM_BUCKETS - 1)
    bucket = np.where(n < max_exact, n, large).astype(np.int32)
    valid = (dist >= 0) & (dist < WINDOW)
    return np.where(valid, bucket, -1).astype(np.int32)


def _bias_build(table, bucket, name):
    def body(t_ref, b_ref, o_ref):
        bk = b_ref[...]
        for h in range(N_Q_HEADS):
            def step(b, acc):
                return jnp.where(bk == b, t_ref[b, h], acc)
            o_ref[h] = lax.fori_loop(0, NUM_BUCKETS, step, jnp.full(bk.shape, MASKED, F32))

    return pl.pallas_call(
        body, name=name,
        in_specs=[pl.BlockSpec(memory_space=pltpu.SMEM), pl.BlockSpec(memory_space=pltpu.VMEM)],
        out_specs=pl.BlockSpec(memory_space=pltpu.VMEM),
        out_shape=jax.ShapeDtypeStruct((N_Q_HEADS,) + bucket.shape, F32),
    )(table, bucket)


def _bias_grad(dbias, bucket, name):
    def body(d_ref, b_ref, o_ref):
        bk = b_ref[...]
        row = lax.broadcasted_iota(jnp.int32, o_ref.shape, 0)
        lane = lax.broadcasted_iota(jnp.int32, o_ref.shape, 1)
        res = jnp.zeros(o_ref.shape, F32)
        for h in range(N_Q_HEADS):
            d = d_ref[h]

            def step(b, acc):
                tot = jnp.sum(jnp.where(bk == b, d, 0.0), axis=1, keepdims=True)
                tot = jnp.sum(tot, axis=0, keepdims=True)
                return jnp.where((row == b) & (lane == h), tot, acc)
            res = lax.fori_loop(0, NUM_BUCKETS, step, res)
        o_ref[...] = res

    return pl.pallas_call(
        body, name=name,
        in_specs=[pl.BlockSpec(memory_space=pltpu.VMEM), pl.BlockSpec(memory_space=pltpu.VMEM)],
        out_specs=pl.BlockSpec(memory_space=pltpu.VMEM),
        out_shape=jax.ShapeDtypeStruct((NUM_BUCKETS, 128), F32),
    )(dbias, bucket)


def _softmax_parts(qh, kc, kp, bias_ref, h, sink, first):
    sc = _dot_nt(qh, kc) * SCALE + bias_ref[h, :, BLOCK:2 * BLOCK]
    sp = _dot_nt(qh, kp) * SCALE + bias_ref[h, :, 0:BLOCK]
    sp = jnp.where(first, MASKED, sp)
    m = jnp.maximum(jnp.maximum(jnp.max(sc, axis=-1, keepdims=True),
                                jnp.max(sp, axis=-1, keepdims=True)), sink)
    pc = jnp.exp(sc - m)
    pp = jnp.exp(sp - m)
    ps = jnp.exp(sink - m)
    r = 1.0 / (jnp.sum(pc, axis=-1, keepdims=True) + jnp.sum(pp, axis=-1, keepdims=True) + ps)
    return pc * r, pp * r, ps * r


def _attn_fwd(q, k, v, bias, sinks, gain, name):
    T = q.shape[0]
    nb = T // BLOCK

    def body(sink_ref, q_ref, kc_ref, kp_ref, vc_ref, vp_ref, bias_ref, g_ref, raw_ref, nrm_ref, o_sc):
        first = pl.program_id(0) == 0
        for h in range(N_Q_HEADS):
            kv = h // GQA_GROUP
            hs = slice(h * HEAD_DIM, (h + 1) * HEAD_DIM)
            ks = slice(kv * HEAD_DIM, (kv + 1) * HEAD_DIM)
            pc, pp, _ = _softmax_parts(q_ref[:, hs], kc_ref[:, ks], kp_ref[:, ks], bias_ref, h,
                                       sink_ref[h], first)
            o_sc[:, hs] = _dot(pc.astype(BF16), vc_ref[:, ks]) + _dot(pp.astype(BF16), vp_ref[:, ks])
        o = o_sc[...]
        raw_ref[...] = o.astype(BF16)
        _, ohat = _rms_stats(o)
        nrm_ref[...] = (ohat * g_ref[...]).astype(BF16)

    cur = lambda n: (n, 0)
    prev = lambda n: (jnp.maximum(n - 1, 0), 0)
    return pl.pallas_call(
        body, name=name, grid=(nb,),
        in_specs=[pl.BlockSpec(memory_space=pltpu.SMEM),
                  pl.BlockSpec((BLOCK, ATTN_WIDTH), cur),
                  pl.BlockSpec((BLOCK, KV_WIDTH), cur), pl.BlockSpec((BLOCK, KV_WIDTH), prev),
                  pl.BlockSpec((BLOCK, KV_WIDTH), cur), pl.BlockSpec((BLOCK, KV_WIDTH), prev),
                  pl.BlockSpec(bias.shape, lambda n: (0, 0, 0)),
                  pl.BlockSpec((1, ATTN_WIDTH), lambda n: (0, 0))],
        out_specs=(pl.BlockSpec((BLOCK, ATTN_WIDTH), cur), pl.BlockSpec((BLOCK, ATTN_WIDTH), cur)),
        out_shape=(jax.ShapeDtypeStruct((T, ATTN_WIDTH), BF16), jax.ShapeDtypeStruct((T, ATTN_WIDTH), BF16)),
        scratch_shapes=[pltpu.VMEM((BLOCK, ATTN_WIDTH), F32)],
        compiler_params=_params("arbitrary"),
    )(sinks, q, k, k, v, v, bias, gain)


def _attn_bwd(dmixed, raw, q, k, v, bias, sinks, gain, name):
    T = q.shape[0]
    nb = T // BLOCK

    def body(sink_ref, dm_ref, raw_ref, q_ref, kc_ref, kp_ref, vc_ref, vp_ref, bias_ref, g_ref,
             dq_ref, dk_ref, dv_ref, dbias_ref, dsink_ref, dgain_ref,
             do_sc, dq_sc, kcur_sc, kprev_sc, vcur_sc, vprev_sc, kcarry_sc, vcarry_sc):
        i = pl.program_id(0)
        first = i == nb - 1

        @pl.when(i == 0)
        def _():
            kcarry_sc[...] = jnp.zeros_like(kcarry_sc)
            vcarry_sc[...] = jnp.zeros_like(vcarry_sc)
            dbias_ref[...] = jnp.zeros_like(dbias_ref)
            dsink_ref[...] = jnp.zeros_like(dsink_ref)
            dgain_ref[...] = jnp.zeros_like(dgain_ref)

        o = raw_ref[...].astype(F32)
        do, dgain = _rms_bwd(dm_ref[...].astype(F32), o, g_ref[...])
        dgain_ref[...] += dgain
        do_sc[...] = do
        lane = lax.broadcasted_iota(jnp.int32, dsink_ref.shape, 1)
        dsink = jnp.zeros(dsink_ref.shape, F32)
        for kv in range(N_KV_HEADS):
            ks = slice(kv * HEAD_DIM, (kv + 1) * HEAD_DIM)
            kc, kp, vc, vp = kc_ref[:, ks], kp_ref[:, ks], vc_ref[:, ks], vp_ref[:, ks]
            dkc = jnp.zeros((BLOCK, HEAD_DIM), F32)
            dkp = jnp.zeros((BLOCK, HEAD_DIM), F32)
            dvc = jnp.zeros((BLOCK, HEAD_DIM), F32)
            dvp = jnp.zeros((BLOCK, HEAD_DIM), F32)
            for g in range(GQA_GROUP):
                h = kv * GQA_GROUP + g
                hs = slice(h * HEAD_DIM, (h + 1) * HEAD_DIM)
                qh = q_ref[:, hs]
                pc, pp, ps = _softmax_parts(qh, kc, kp, bias_ref, h, sink_ref[h], first)
                doh = do_sc[:, hs]
                rowdot = jnp.sum(doh * raw_ref[:, hs].astype(F32), axis=-1, keepdims=True)
                dohb = doh.astype(BF16)
                dsc = pc * (_dot_nt(dohb, vc) - rowdot)
                dsp = pp * (_dot_nt(dohb, vp) - rowdot)
                dsink = dsink + jnp.where(lane == h, -jnp.sum(ps * rowdot, axis=0, keepdims=True), 0.0)
                dbias_ref[h, :, BLOCK:2 * BLOCK] += dsc
                dbias_ref[h, :, 0:BLOCK] += dsp
                dscb, dspb = dsc.astype(BF16), dsp.astype(BF16)
                dq_sc[:, hs] = (_dot(dscb, kc) + _dot(dspb, kp)) * SCALE
                dkc += _dot_tn(dscb, qh)
                dkp += _dot_tn(dspb, qh)
                dvc += _dot_tn(pc.astype(BF16), dohb)
                dvp += _dot_tn(pp.astype(BF16), dohb)
            kcur_sc[:, ks] = dkc * SCALE
            kprev_sc[:, ks] = dkp * SCALE
            vcur_sc[:, ks] = dvc
            vprev_sc[:, ks] = dvp
        dsink_ref[...] += dsink
        dq_ref[...] = dq_sc[...].astype(BF16)
        dk_ref[...] = (kcarry_sc[...] + kcur_sc[...]).astype(BF16)
        dv_ref[...] = (vcarry_sc[...] + vcur_sc[...]).astype(BF16)
        kcarry_sc[...] = kprev_sc[...]
        vcarry_sc[...] = vprev_sc[...]

    cur = lambda i: (nb - 1 - i, 0)
    prev = lambda i: (jnp.maximum(nb - 2 - i, 0), 0)
    const2 = lambda i: (0, 0)
    blk = lambda w: pltpu.VMEM((BLOCK, w), F32)
    return pl.pallas_call(
        body, name=name, grid=(nb,),
        in_specs=[pl.BlockSpec(memory_space=pltpu.SMEM),
                  pl.BlockSpec((BLOCK, ATTN_WIDTH), cur),
                  pl.BlockSpec((BLOCK, ATTN_WIDTH), cur),
                  pl.BlockSpec((BLOCK, ATTN_WIDTH), cur),
                  pl.BlockSpec((BLOCK, KV_WIDTH), cur), pl.BlockSpec((BLOCK, KV_WIDTH), prev),
                  pl.BlockSpec((BLOCK, KV_WIDTH), cur), pl.BlockSpec((BLOCK, KV_WIDTH), prev),
                  pl.BlockSpec(bias.shape, lambda i: (0, 0, 0)),
                  pl.BlockSpec((1, ATTN_WIDTH), const2)],
        out_specs=(pl.BlockSpec((BLOCK, ATTN_WIDTH), cur),
                   pl.BlockSpec((BLOCK, KV_WIDTH), cur), pl.BlockSpec((BLOCK, KV_WIDTH), cur),
                   pl.BlockSpec(bias.shape, lambda i: (0, 0, 0)),
                   pl.BlockSpec((SUBLANES, 128), const2),
                   pl.BlockSpec((1, ATTN_WIDTH), const2)),
        out_shape=(jax.ShapeDtypeStruct((T, ATTN_WIDTH), BF16),
                   jax.ShapeDtypeStruct((T, KV_WIDTH), BF16), jax.ShapeDtypeStruct((T, KV_WIDTH), BF16),
                   jax.ShapeDtypeStruct(bias.shape, F32),
                   jax.ShapeDtypeStruct((SUBLANES, 128), F32),
                   jax.ShapeDtypeStruct((1, ATTN_WIDTH), F32)),
        scratch_shapes=[blk(ATTN_WIDTH), blk(ATTN_WIDTH), blk(KV_WIDTH), blk(KV_WIDTH), blk(KV_WIDTH),
                        blk(KV_WIDTH), blk(KV_WIDTH), blk(KV_WIDTH)],
        compiler_params=_params("arbitrary"),
    )(sinks, dmixed, raw, q, k, k, v, v, bias, gain)


def _shift_down(cu, tail):
    row = lax.broadcasted_iota(jnp.int32, cu.shape, 0)
    t6, t7 = tail[6:7, :], tail[7:8, :]
    s1 = jnp.where(row == 0, t7, pltpu.roll(cu, 1, 0))
    s2 = jnp.where(row == 0, t6, jnp.where(row == 1, t7, pltpu.roll(cu, 2, 0)))
    return s1, s2


def _shift_up(d, head):
    n = d.shape[0]
    row = lax.broadcasted_iota(jnp.int32, d.shape, 0)
    h0, h1 = head[0:1, :], head[1:2, :]
    s1 = jnp.where(row == n - 1, h0, pltpu.roll(d, n - 1, 0))
    s2 = jnp.where(row == n - 1, h1, jnp.where(row == n - 2, h0, pltpu.roll(d, n - 2, 0)))
    return s1, s2


def _mixout_fwd(x, attn_n, u, gb, gc, conv_w, gain, w_out, name, tm):
    T, D = x.shape

    def body(x_ref, an_ref, u_ref, b_ref, c_ref, cw_ref, g_ref, wo_ref, xo_ref, cn_ref, tail_sc):
        @pl.when(pl.program_id(0) == 0)
        def _():
            tail_sc[...] = jnp.zeros_like(tail_sc)

        cu = c_ref[...].astype(F32) * u_ref[...].astype(F32)
        s1, s2 = _shift_down(cu, tail_sc[...])
        tail_sc[...] = cu[tm - SUBLANES:tm, :]
        pre = cw_ref[0:1, :] * s2 + cw_ref[1:2, :] * s1 + cw_ref[2:3, :] * cu
        conv = b_ref[...].astype(F32) * pre
        _, chat = _rms_stats(conv)
        cn = (chat * g_ref[...]).astype(BF16)
        cn_ref[...] = cn
        xo_ref[...] = (x_ref[...] + _dot(an_ref[...], wo_ref[0:ATTN_WIDTH, :])
                       + _dot(cn, wo_ref[ATTN_WIDTH:ATTN_WIDTH + CONV_DIM, :]))

    row = lambda i: (i, 0)
    const = lambda i: (0, 0)
    return pl.pallas_call(
        body, name=name, grid=(T // tm,),
        in_specs=[pl.BlockSpec((tm, D), row), pl.BlockSpec((tm, ATTN_WIDTH), row),
                  pl.BlockSpec((tm, CONV_DIM), row), pl.BlockSpec((tm, CONV_DIM), row),
                  pl.BlockSpec((tm, CONV_DIM), row),
                  pl.BlockSpec(conv_w.shape, const), pl.BlockSpec((1, CONV_DIM), const),
                  pl.BlockSpec(w_out.shape, const)],
        out_specs=(pl.BlockSpec((tm, D), row), pl.BlockSpec((tm, CONV_DIM), row)),
        out_shape=(jax.ShapeDtypeStruct((T, D), F32), jax.ShapeDtypeStruct((T, CONV_DIM), BF16)),
        scratch_shapes=[pltpu.VMEM((SUBLANES, CONV_DIM), F32)],
        compiler_params=_params("arbitrary"),
    )(x, attn_n, u, gb, gc, conv_w, gain, w_out)


def _mixout_bwd(dy, attn_n, conv_n, w_out_t, after, name, tm):
    T, D = dy.shape
    W = ATTN_WIDTH + CONV_DIM

    def body(dy_ref, an_ref, cn_ref, wt_ref, after_ref, dm_ref, dw_ref):
        @pl.when(pl.program_id(0) == 0)
        def _():
            dw_ref[...] = jnp.zeros_like(dw_ref)

        dyb = dy_ref[...].astype(BF16)
        dm_ref[...] = _dot(dyb, wt_ref[...]).astype(BF16)
        dw_ref[0:ATTN_WIDTH, :] += _dot_tn(an_ref[...], dyb)
        dw_ref[ATTN_WIDTH:W, :] += _dot_tn(cn_ref[...], dyb)

    row = lambda i: (i, 0)
    const = lambda i: (0, 0)
    return pl.pallas_call(
        body, name=name, grid=(T // tm,),
        in_specs=[pl.BlockSpec((tm, D), row), pl.BlockSpec((tm, ATTN_WIDTH), row),
                  pl.BlockSpec((tm, CONV_DIM), row), pl.BlockSpec(w_out_t.shape, const),
                  pl.BlockSpec(memory_space=pl.ANY)],
        out_specs=(pl.BlockSpec((tm, W), row), pl.BlockSpec((W, D), const)),
        out_shape=(jax.ShapeDtypeStruct((T, W), BF16), jax.ShapeDtypeStruct((W, D), F32)),
        compiler_params=_params("arbitrary"),
    )(dy, attn_n, conv_n, w_out_t, after)


def _conv_bwd(dmixed, u, gb, gc, conv_w, gain, name, tc):
    T = u.shape[0]
    nt = T // tc
    per8 = tc // SUBLANES

    def body(dm_ref, u_ref, b_ref, c_ref, ut_ref, ct_ref, cw_ref, g_ref,
             du_ref, db_ref, dc_ref, dcw_ref, dgain_ref, head_sc):
        i = pl.program_id(0)

        @pl.when(i == 0)
        def _():
            head_sc[...] = jnp.zeros_like(head_sc)
            dcw_ref[...] = jnp.zeros_like(dcw_ref)
            dgain_ref[...] = jnp.zeros_like(dgain_ref)

        uu = u_ref[...].astype(F32)
        cc = c_ref[...].astype(F32)
        bb = b_ref[...].astype(F32)
        cu = cc * uu
        tail = jnp.where(i == nt - 1, 0.0, ct_ref[...].astype(F32) * ut_ref[...].astype(F32))
        s1, s2 = _shift_down(cu, tail)
        w0, w1, w2 = cw_ref[0:1, :], cw_ref[1:2, :], cw_ref[2:3, :]
        pre = w0 * s2 + w1 * s1 + w2 * cu
        dconv, dgain = _rms_bwd(dm_ref[...].astype(F32), bb * pre, g_ref[...])
        dgain_ref[...] += dgain
        db_ref[...] = (dconv * pre).astype(BF16)
        dpre = dconv * bb
        dcw_ref[0:1, :] += jnp.sum(dpre * s2, axis=0, keepdims=True)
        dcw_ref[1:2, :] += jnp.sum(dpre * s1, axis=0, keepdims=True)
        dcw_ref[2:3, :] += jnp.sum(dpre * cu, axis=0, keepdims=True)
        n1, n2 = _shift_up(dpre, head_sc[...])
        head_sc[...] = dpre[0:SUBLANES, :]
        dcu = w2 * dpre + w1 * n1 + w0 * n2
        du_ref[...] = (dcu * cc).astype(BF16)
        dc_ref[...] = (dcu * uu).astype(BF16)

    rev = lambda i: (nt - 1 - i, 0)
    rev_right = lambda i: (nt - 1 - i, 1)
    tail_map = lambda i: (jnp.maximum((nt - 1 - i) * per8 - 1, 0), 0)
    const = lambda i: (0, 0)
    return pl.pallas_call(
        body, name=name, grid=(nt,),
        in_specs=[pl.BlockSpec((tc, CONV_DIM), rev_right),
                  pl.BlockSpec((tc, CONV_DIM), rev), pl.BlockSpec((tc, CONV_DIM), rev),
                  pl.BlockSpec((tc, CONV_DIM), rev),
                  pl.BlockSpec((SUBLANES, CONV_DIM), tail_map), pl.BlockSpec((SUBLANES, CONV_DIM), tail_map),
                  pl.BlockSpec(conv_w.shape, const), pl.BlockSpec((1, CONV_DIM), const)],
        out_specs=(pl.BlockSpec((tc, CONV_DIM), rev), pl.BlockSpec((tc, CONV_DIM), rev),
                   pl.BlockSpec((tc, CONV_DIM), rev),
                   pl.BlockSpec((SUBLANES, CONV_DIM), const), pl.BlockSpec((1, CONV_DIM), const)),
        out_shape=(jax.ShapeDtypeStruct((T, CONV_DIM), BF16), jax.ShapeDtypeStruct((T, CONV_DIM), BF16),
                   jax.ShapeDtypeStruct((T, CONV_DIM), BF16),
                   jax.ShapeDtypeStruct((SUBLANES, CONV_DIM), F32), jax.ShapeDtypeStruct((1, CONV_DIM), F32)),
        scratch_shapes=[pltpu.VMEM((SUBLANES, CONV_DIM), F32)],
        compiler_params=_params("arbitrary"),
    )(dmixed, u, gb, gc, u, gc, conv_w, gain)


def _mixin_bwd(dy, x, gain, dz, w_in_t, name, tm):
    T, D = x.shape
    nz = len(dz)

    def body(dy_ref, x_ref, g_ref, *rest):
        dz_refs, wt_ref, dx_ref, dgain_ref = rest[:nz], rest[nz], rest[nz + 1], rest[nz + 2]

        @pl.when(pl.program_id(0) == 0)
        def _():
            dgain_ref[...] = jnp.zeros_like(dgain_ref)

        dh = jnp.zeros((tm, D), F32)
        for r, lo, hi in zip(dz_refs, _Z_SPLITS[:-1], _Z_SPLITS[1:]):
            dh += _dot(r[...], wt_ref[lo:hi, :])
        dx, dgain = _rms_bwd(dh, x_ref[...], g_ref[...])
        dgain_ref[...] += dgain
        dx_ref[...] = dy_ref[...] + dx

    row = lambda i: (i, 0)
    const = lambda i: (0, 0)
    return pl.pallas_call(
        body, name=name, grid=(T // tm,),
        in_specs=[pl.BlockSpec((tm, D), row), pl.BlockSpec((tm, D), row), pl.BlockSpec((1, D), const)]
                 + [pl.BlockSpec((tm, a.shape[1]), row) for a in dz]
                 + [pl.BlockSpec(w_in_t.shape, const)],
        out_specs=(pl.BlockSpec((tm, D), row), pl.BlockSpec((1, D), const)),
        out_shape=(jax.ShapeDtypeStruct((T, D), F32), jax.ShapeDtypeStruct((1, D), F32)),
        compiler_params=_params("arbitrary"),
    )(dy, x, gain, *dz, w_in_t)


def _loss_head(x, target, gain, name, tm):
    T, D = x.shape

    def body(x_ref, t_ref, g_ref, dx_ref, dgain_ref, loss_ref):
        @pl.when(pl.program_id(0) == 0)
        def _():
            dgain_ref[...] = jnp.zeros_like(dgain_ref)
            loss_ref[...] = jnp.zeros_like(loss_ref)

        xv = x_ref[...]
        gain_v = g_ref[...]
        _, xhat = _rms_stats(xv)
        err = xhat * gain_v - t_ref[...]
        part = 0.5 * jnp.sum(jnp.mean(err * err, axis=-1, keepdims=True), axis=0, keepdims=True)
        loss_ref[...] += part
        dx, dgain = _rms_bwd(err * (1.0 / D), xv, gain_v)
        dgain_ref[...] += dgain
        dx_ref[...] = dx

    row = lambda i: (i, 0)
    const = lambda i: (0, 0)
    return pl.pallas_call(
        body, name=name, grid=(T // tm,),
        in_specs=[pl.BlockSpec((tm, D), row), pl.BlockSpec((tm, D), row), pl.BlockSpec((1, D), const)],
        out_specs=(pl.BlockSpec((tm, D), row), pl.BlockSpec((1, D), const),
                   pl.BlockSpec((SUBLANES, 128), const)),
        out_shape=(jax.ShapeDtypeStruct((T, D), F32), jax.ShapeDtypeStruct((1, D), F32),
                   jax.ShapeDtypeStruct((SUBLANES, 128), F32)),
        compiler_params=_params("arbitrary"),
    )(x, target, gain)


def _adamw(parts, w, m, v, name, tr):
    R, C = w.shape

    def body(p_ref, w_ref, m_ref, v_ref, g_ref, d_ref, nm_ref, nv_ref):
        g = p_ref[0].astype(F32)
        for d in range(1, N_DEV):
            g = g + p_ref[d].astype(F32)
        nm = ADAM_B1 * m_ref[...] + (1.0 - ADAM_B1) * g
        nv = ADAM_B2 * v_ref[...] + (1.0 - ADAM_B2) * (g * g)
        m_hat = nm / (1.0 - ADAM_B1 ** ADAM_STEP)
        v_hat = nv / (1.0 - ADAM_B2 ** ADAM_STEP)
        g_ref[...] = g
        nm_ref[...] = nm
        nv_ref[...] = nv
        d_ref[...] = -ADAM_LR * (m_hat / (jnp.sqrt(v_hat) + ADAM_EPS) + ADAM_WD * w_ref[...])

    row = lambda i: (i, 0)
    spec = pl.BlockSpec((tr, C), row)
    shp = jax.ShapeDtypeStruct((R, C), F32)
    return pl.pallas_call(
        body, name=name, grid=(R // tr,),
        in_specs=[pl.BlockSpec((N_DEV, tr, C), lambda i: (0, i, 0)), spec, spec, spec],
        out_specs=(spec, spec, spec, spec),
        out_shape=(shp, shp, shp, shp),
        compiler_params=_params("arbitrary"),
    )(parts, w, m, v)


def _blocks_of_columns(g, n):
    R, C = g.shape
    return g.reshape(R, n, C // n).transpose(1, 0, 2)


def _columns_of_blocks(g):
    n, R, w = g.shape
    return g.transpose(1, 0, 2).reshape(R, n * w)


def _pad_row(vec):
    vec = vec.reshape(1, -1)
    return jnp.pad(vec, ((0, 0), (0, PACK_COLS - vec.shape[1])))


def kernel(x, rel_bias_table, ffn1_norm, ffn1_w_gate, ffn1_w_up, ffn1_w_down, mix_norm, w_in, conv_w, attn_sinks, attn_out_norm, conv_out_norm, w_out, ffn2_norm, ffn2_w_gate, ffn2_w_up, ffn2_w_down, final_norm, loss_target, m_rel_bias_table, m_ffn1_norm, m_ffn1_w_gate, m_ffn1_w_up, m_ffn1_w_down, m_mix_norm, m_w_in, m_conv_w, m_attn_sinks, m_attn_out_norm, m_conv_out_norm, m_w_out, m_ffn2_norm, m_ffn2_w_gate, m_ffn2_w_up, m_ffn2_w_down, m_final_norm, v_rel_bias_table, v_ffn1_norm, v_ffn1_w_gate, v_ffn1_w_up, v_ffn1_w_down, v_mix_norm, v_w_in, v_conv_w, v_attn_sinks, v_attn_out_norm, v_conv_out_norm, v_w_out, v_ffn2_norm, v_ffn2_w_gate, v_ffn2_w_up, v_ffn2_w_down, v_final_norm):
    T, D = x.shape[1], x.shape[2]
    x0 = x[0]
    target = loss_target[0]
    tm = min(TM_FFN, T)
    tm_bwd = min(TM_FFN_BWD, T)
    tm_mix = min(TM_MIX, T)
    tk = min(TK_WGRAD, T)
    tf = TF_FFN
    me = 4 * lax.axis_index("x") + 2 * lax.axis_index("y") + lax.axis_index("c")

    big = [ffn1_w_gate[0], ffn1_w_up[0], ffn1_w_down[0], w_in[0], w_out[0],
           ffn2_w_gate[0], ffn2_w_up[0], ffn2_w_down[0]]
    big_m = [m_ffn1_w_gate, m_ffn1_w_up, m_ffn1_w_down, m_w_in, m_w_out,
             m_ffn2_w_gate, m_ffn2_w_up, m_ffn2_w_down]
    big_v = [v_ffn1_w_gate, v_ffn1_w_up, v_ffn1_w_down, v_w_in, v_w_out,
             v_ffn2_w_gate, v_ffn2_w_up, v_ffn2_w_down]
    big_names = ["ffn1_w_gate", "ffn1_w_up", "ffn1_w_down", "w_in", "w_out",
                 "ffn2_w_gate", "ffn2_w_up", "ffn2_w_down"]
    handles, token = _exchange_start([w.astype(BF16) for w in big] + [conv_w[0]], False, "gather_start", ffn1_norm)
    first = _exchange_wait(handles[:3], False, "gather_wait_ffn1", token)
    wg1 = _columns_of_blocks(first[0])
    wu1 = _columns_of_blocks(first[1])
    wd1 = first[2].reshape(-1, D)
    F = wg1.shape[1]
    bf = F // 2

    x1, xn1, gate1, up1 = _ffn_fwd(x0, ffn1_norm, wg1, wu1, wd1, "ffn1_fwd", tm, tf)
    rest = _exchange_wait(handles[3:], False, "gather_wait_rest", x1)
    win = _columns_of_blocks(rest[0])
    wout = rest[1].reshape(-1, D)
    wg2 = _columns_of_blocks(rest[2])
    wu2 = _columns_of_blocks(rest[3])
    wd2 = rest[4].reshape(-1, D)
    cw = _columns_of_blocks(rest[5])
    hn, q, k, v, u, gb, gc = _mixin_fwd(x1, mix_norm, win, "mixin_fwd", tm_mix)
    bucket = jnp.asarray(_bucket_table())
    sinks = attn_sinks.reshape(-1)
    bias = _bias_build(rel_bias_table, bucket, "bias_build")
    attn_raw, attn_n = _attn_fwd(q, k, v, bias, sinks, attn_out_norm, "attn_fwd")
    x2, conv_n = _mixout_fwd(x1, attn_n, u, gb, gc, cw, conv_out_norm, wout, "mixout_fwd", tm_mix)
    x3, xn2, gate2, up2 = _ffn_fwd(x2, ffn2_norm, wg2, wu2, wd2, "ffn2_fwd", tm, tf)

    dx3, d_final, loss_part = _loss_head(x3, target, final_norm.reshape(1, D), "loss_head", tm_mix)
    dx2, d_ffn2_norm, dg2, du2, hh2, do2 = _ffn_bwd(
        dx3, x2, ffn2_norm, gate2, up2, wg2.T, wu2.T, wd2.T, dx3, "ffn2_bwd", tm_bwd, tf)
    d_wg2 = _tn_matmul(xn2, dg2, "ffn2_dwg", D, bf, tk)
    d_wu2 = _tn_matmul(xn2, du2, "ffn2_dwu", D, bf, tk)
    d_wd2 = _tn_matmul(hh2, do2, "ffn2_dwd", bf, D, tk)
    grads2 = [_blocks_of_columns(d_wg2, N_DEV).astype(BF16), _blocks_of_columns(d_wu2, N_DEV).astype(BF16),
              d_wd2.reshape(N_DEV, -1, D).astype(BF16)]
    handles2, token2 = _exchange_start(grads2, True, "grads_start_ffn2", d_wd2)

    dmixed, d_wout = _mixout_bwd(dx2, attn_n, conv_n, wout.T, token2, "mixout_bwd", tm_mix)
    dq, dk, dv, dbias, dsink, d_attn_norm = _attn_bwd(
        dmixed, attn_raw, q, k, v, bias, sinks, attn_out_norm, "attn_bwd")
    du, dgb, dgc, d_cw, d_conv_norm = _conv_bwd(dmixed, u, gb, gc, cw, conv_out_norm, "conv_bwd", tm_mix)
    d_table = _bias_grad(dbias, bucket, "bias_grad")
    dz = [dq, dk, dv, du, dgb, dgc]
    dx1, d_mix_norm = _mixin_bwd(dx2, x1, mix_norm, dz, win.T, "mixin_bwd", tm_mix)
    d_win = jnp.concatenate(
        [_tn_matmul(hn, p, "dwin_%d" % n, D, p.shape[1], tk) for n, p in enumerate(dz)], axis=1)
    grads_mix = [_blocks_of_columns(d_win, N_DEV).astype(BF16), d_wout.reshape(N_DEV, -1, D).astype(BF16)]
    handles_mix, token_mix = _exchange_start(grads_mix, True, "grads_start_mix", d_win)

    dx0, d_ffn1_norm, dg1, du1, hh1, do1 = _ffn_bwd(
        dx1, x0, ffn1_norm, gate1, up1, wg1.T, wu1.T, wd1.T, token_mix, "ffn1_bwd", tm_bwd, tf)
    d_wg1 = _tn_matmul(xn1, dg1, "ffn1_dwg", D, bf, tk)
    d_wu1 = _tn_matmul(xn1, du1, "ffn1_dwu", D, bf, tk)
    d_wd1 = _tn_matmul(hh1, do1, "ffn1_dwd", bf, D, tk)
    grads1 = [_blocks_of_columns(d_wg1, N_DEV).astype(BF16), _blocks_of_columns(d_wu1, N_DEV).astype(BF16),
              d_wd1.reshape(N_DEV, -1, D).astype(BF16)]
    handles1, token1 = _exchange_start(grads1, True, "grads_start_ffn1", d_wd1)

    res = {}

    def update(names, parts):
        last = None
        for name, p in zip(names, parts):
            i = big_names.index(name)
            w = big[i]
            g, dl, nm, nv = _adamw(p, w, big_m[i][0], big_v[i][0], "adamw_" + name,
                                   _row_tile(w.shape[0], ADAM_ROWS))
            res[name] = tuple(a[None] for a in (g, dl, nm, nv))
            last = g
        return last

    parts2 = _exchange_wait(handles2, True, "grads_wait_ffn2", token1)
    done2 = update(big_names[5:8], parts2)
    parts_mix = _exchange_wait(handles_mix, True, "grads_wait_mix", done2)
    done_mix = update(big_names[3:5], parts_mix)
    parts1 = _exchange_wait(handles1, True, "grads_wait_ffn1", done_mix)
    update(big_names[0:3], parts1)

    def pack(ffn1, mixn, ffn2, fin, attn_n_, conv_n_, sink_, extra, convw, table):
        rows = [_pad_row(ffn1), _pad_row(mixn), _pad_row(ffn2), _pad_row(fin),
                _pad_row(jnp.concatenate([attn_n_.reshape(-1), conv_n_.reshape(-1)])),
                _pad_row(sink_), _pad_row(extra),
                jnp.zeros((1, PACK_COLS), F32),
                jnp.pad(convw, ((0, 0), (0, PACK_COLS - convw.shape[1]))),
                _pad_row(table),
                jnp.zeros((PACK_ROWS - 12, PACK_COLS), F32)]
        return jnp.concatenate(rows, axis=0)

    def own_channels(a):
        full = jnp.zeros((a.shape[1], CONV_DIM), F32)
        return lax.dynamic_update_slice(full, a[0], (0, me * a.shape[2]))

    g_pack = pack(d_ffn1_norm, d_mix_norm, d_ffn2_norm, d_final, d_attn_norm, d_conv_norm,
                  dsink[0, :N_Q_HEADS], loss_part[0, :1], d_cw[:3], d_table[:, :N_Q_HEADS])
    zero1 = jnp.zeros((1,), F32)
    w_pack = pack(ffn1_norm, mix_norm, ffn2_norm, final_norm, attn_out_norm, conv_out_norm,
                  attn_sinks, zero1, own_channels(conv_w), rel_bias_table)
    m_pack = pack(m_ffn1_norm, m_mix_norm, m_ffn2_norm, m_final_norm, m_attn_out_norm, m_conv_out_norm,
                  m_attn_sinks, zero1, own_channels(m_conv_w), m_rel_bias_table)
    v_pack = pack(v_ffn1_norm, v_mix_norm, v_ffn2_norm, v_final_norm, v_attn_out_norm, v_conv_out_norm,
                  v_attn_sinks, zero1, own_channels(v_conv_w), v_rel_bias_table)
    (g_all,) = _exchange([g_pack], False, "gather_small")
    packs = _adamw(g_all, w_pack, m_pack, v_pack, "adamw_small", PACK_ROWS)

    def unpack(pk):
        cwb = lax.dynamic_slice(pk[8:11, :CONV_DIM], (0, me * conv_w.shape[2]), (3, conv_w.shape[2]))
        return {
            "ffn1_norm": pk[0:1, :D], "mix_norm": pk[1:2, :D], "ffn2_norm": pk[2:3, :D],
            "final_norm": pk[3, :D],
            "attn_out_norm": pk[4:5, :ATTN_WIDTH], "conv_out_norm": pk[4:5, ATTN_WIDTH:ATTN_WIDTH + CONV_DIM],
            "attn_sinks": pk[5:6, :N_Q_HEADS],
            "conv_w": cwb[None],
            "rel_bias_table": pk[11, :NUM_BUCKETS * N_Q_HEADS].reshape(NUM_BUCKETS, N_Q_HEADS),
        }

    small = [unpack(pk) for pk in packs]
    loss = packs[0][6, 0]

    order = ["rel_bias_table", "ffn1_norm", "ffn1_w_gate", "ffn1_w_up", "ffn1_w_down", "mix_norm", "w_in",
             "conv_w", "attn_sinks", "attn_out_norm", "conv_out_norm", "w_out", "ffn2_norm",
             "ffn2_w_gate", "ffn2_w_up", "ffn2_w_down", "final_norm"]
    outs = [loss, dx0[None]]
    for kind in range(4):
        for name in order:
            outs.append(res[name][kind] if name in res else small[kind][name])
    return tuple(outs)
```

```python
import math

import numpy as np
import jax
import jax.numpy as jnp
from jax import lax
from jax.experimental import pallas as pl
from jax.experimental.pallas import tpu as pltpu

F32 = jnp.float32
BF16 = jnp.bfloat16

N_DEV = 8
EPS = 1e-6
HEAD_DIM = 64
N_Q_HEADS = 8
N_KV_HEADS = 2
GQA_GROUP = 4
ATTN_WIDTH = 512
KV_WIDTH = 128
CONV_DIM = 512
BLOCK = 128
WINDOW = 128
NUM_BUCKETS = 32
MAX_DISTANCE = 128
SCALE = HEAD_DIM ** -0.5
MASKED = -1e30
GROUP_ROWS = GQA_GROUP * BLOCK

ADAM_LR = 0.001
ADAM_B1 = 0.9
ADAM_B2 = 0.999
ADAM_EPS = 1e-08
ADAM_WD = 0.01
ADAM_STEP = 10

VMEM_LIMIT_BYTES = 40 * 1024 * 1024
SUBLANES = 8
PACK_ROWS = 16
PACK_COLS = 1024

TM_FFN = 1024
TM_FFN_BWD = 512
TM_MIX = 512
TK_WGRAD = 1024
TF_FFN = 256
BM_WGRAD = 256
ATTN_BLOCKS = 4
ADAM_ROWS = 256


def _row_tile(rows, limit):
    best = rows
    for t in range(16, min(rows, limit) + 1, 16):
        if rows % t == 0:
            best = t
    return best


def _params(*sem):
    return pltpu.CompilerParams(dimension_semantics=sem, vmem_limit_bytes=VMEM_LIMIT_BYTES)


def _dot(a, b):
    return jnp.dot(a, b, preferred_element_type=F32)


def _dot_nt(a, b):
    return lax.dot_general(a, b, (((1,), (1,)), ((), ())), preferred_element_type=F32)


def _dot_tn(a, b):
    return lax.dot_general(a, b, (((0,), (0,)), ((), ())), preferred_element_type=F32)


def _sigmoid(g):
    return 1.0 / (1.0 + jnp.exp(-g))


def _rms_stats(x):
    inv = lax.rsqrt(jnp.mean(x * x, axis=-1, keepdims=True) + EPS)
    return inv, x * inv


def _rms_bwd(dy, x, gain):
    inv, xhat = _rms_stats(x)
    dgain = jnp.sum(dy * xhat, axis=0, keepdims=True)
    dxh = dy * gain
    dx = inv * (dxh - xhat * jnp.mean(dxh * xhat, axis=-1, keepdims=True))
    return dx, dgain


def _peer_list():
    x, y, c = lax.axis_index("x"), lax.axis_index("y"), lax.axis_index("c")
    peers = []
    for k in range(1, N_DEV):
        px = 1 - x if (k >> 2) & 1 else x
        py = 1 - y if (k >> 1) & 1 else y
        pc = 1 - c if k & 1 else c
        peers.append((px, py, pc))
    return 4 * x + 2 * y + c, peers


def _exchange(arrs, scatter, name):
    n = len(arrs)
    out_shape = []
    for a in arrs:
        shp = a.shape if scatter else (N_DEV,) + a.shape
        out_shape.append(jax.ShapeDtypeStruct(shp, a.dtype))

    def body(*refs):
        ins, outs = refs[:n], refs[n:2 * n]
        send_sems, recv_sems, local_sems = refs[2 * n:]
        me, peers = _peer_list()
        started = []
        for a in range(n):
            own = ins[a].at[me] if scatter else ins[a]
            loc = pltpu.make_async_copy(own, outs[a].at[me], local_sems.at[a])
            loc.start()
            started.append(loc)
        sends = []
        for a in range(n):
            for k, (px, py, pc) in enumerate(peers):
                src = ins[a].at[4 * px + 2 * py + pc] if scatter else ins[a]
                cp = pltpu.make_async_remote_copy(
                    src_ref=src, dst_ref=outs[a].at[me],
                    send_sem=send_sems.at[a, k], recv_sem=recv_sems.at[a, k],
                    device_id=(px, py, pc), device_id_type=pl.DeviceIdType.MESH)
                cp.start()
                sends.append(cp)
        for a in range(n):
            for k, (px, py, pc) in enumerate(peers):
                landed = outs[a].at[4 * px + 2 * py + pc]
                pltpu.make_async_remote_copy(
                    src_ref=landed, dst_ref=landed,
                    send_sem=send_sems.at[a, k], recv_sem=recv_sems.at[a, k],
                    device_id=(px, py, pc), device_id_type=pl.DeviceIdType.MESH).wait_recv()
        for cp in sends:
            cp.wait_send()
        for loc in started:
            loc.wait()

    hbm = pl.BlockSpec(memory_space=pl.ANY)
    return pl.pallas_call(
        body, name=name, out_shape=tuple(out_shape),
        in_specs=[hbm] * n, out_specs=tuple([hbm] * n),
        scratch_shapes=[pltpu.SemaphoreType.DMA((n, N_DEV - 1)),
                        pltpu.SemaphoreType.DMA((n, N_DEV - 1)),
                        pltpu.SemaphoreType.DMA((n,))],
    )(*arrs)


_HBM = pl.BlockSpec(memory_space=pltpu.HBM)
_SEM = pl.BlockSpec(memory_space=pltpu.SEMAPHORE)
_EFFECT = pltpu.SideEffectType.DATAFLOW_SIDE_EFFECTING


def _split_copies(srcs, lands, send_sems, recv_sems, scatter):
    me, peers = _peer_list()
    copies = []
    for a in range(len(srcs)):
        for k, (px, py, pc) in enumerate(peers):
            src = srcs[a].at[4 * px + 2 * py + pc] if scatter else srcs[a]
            copies.append(pltpu.make_async_remote_copy(
                src_ref=src, dst_ref=lands[a].at[me],
                send_sem=send_sems[a].at[k], recv_sem=recv_sems[a].at[k],
                device_id=(px, py, pc), device_id_type=pl.DeviceIdType.MESH))
    return copies


def _exchange_start(arrs, scatter, name, after):
    n = len(arrs)
    me = 4 * lax.axis_index("x") + 2 * lax.axis_index("y") + lax.axis_index("c")
    lands = []
    for a in arrs:
        own = lax.dynamic_index_in_dim(a, me, 0, keepdims=True) if scatter else a[None]
        shp = a.shape if scatter else (N_DEV,) + a.shape
        lands.append(lax.dynamic_update_slice(lax.empty(shp, a.dtype), own, (me,) + (0,) * (len(shp) - 1)))

    def body(*refs):
        srcs, lnds = refs[:n], refs[n:2 * n]
        outs = refs[2 * n + 1:]
        send_sems, recv_sems = outs[:n], outs[n:2 * n]
        token = outs[4 * n]
        for cp in _split_copies(srcs, lnds, send_sems, recv_sems, scatter):
            cp.start()
        token[...] = jnp.zeros_like(token)

    sem = pltpu.SemaphoreType.DMA((N_DEV - 1,))
    out_shape = ([sem] * (2 * n) + [pltpu.HBM(a.shape, a.dtype) for a in arrs]
                 + [pltpu.HBM(l.shape, l.dtype) for l in lands] + [jax.ShapeDtypeStruct((SUBLANES, 128), F32)])
    res = pl.pallas_call(
        body, name=name, out_shape=tuple(out_shape),
        in_specs=[_HBM] * (2 * n) + [pl.BlockSpec(memory_space=pl.ANY)],
        out_specs=tuple([_SEM] * (2 * n) + [_HBM] * (2 * n) + [pl.BlockSpec(memory_space=pltpu.VMEM)]),
        input_output_aliases={i: 2 * n + i for i in range(2 * n)},
        compiler_params=pltpu.CompilerParams(has_side_effects=_EFFECT),
    )(*[pltpu.with_memory_space_constraint(a, pltpu.HBM) for a in arrs],
      *[pltpu.with_memory_space_constraint(l, pltpu.HBM) for l in lands], after)
    handles = [(res[2 * n + a], res[3 * n + a], res[a], res[n + a]) for a in range(n)]
    return handles, res[4 * n]


def _exchange_wait(handles, scatter, name, after):
    n = len(handles)

    def body(*refs):
        srcs, lnds = refs[:n], refs[n:2 * n]
        send_sems, recv_sems = refs[2 * n:3 * n], refs[3 * n:4 * n]
        for cp in _split_copies(srcs, lnds, send_sems, recv_sems, scatter):
            cp.wait_send()
            cp.wait_recv()

    srcs = [h[0] for h in handles]
    lands = [h[1] for h in handles]
    res = pl.pallas_call(
        body, name=name,
        out_shape=tuple([pltpu.HBM(a.shape, a.dtype) for a in srcs] + [pltpu.HBM(l.shape, l.dtype) for l in lands]),
        in_specs=[_HBM] * (2 * n) + [_SEM] * (2 * n) + [pl.BlockSpec(memory_space=pl.ANY)],
        out_specs=tuple([_HBM] * (2 * n)),
        input_output_aliases={i: i for i in range(2 * n)},
        compiler_params=pltpu.CompilerParams(has_side_effects=_EFFECT),
    )(*srcs, *lands, *[h[2] for h in handles], *[h[3] for h in handles], after)
    return list(res[n:])


def _ffn_fwd(x, gain, wgt, wut, wd, name, tm, tf):
    T, D = x.shape
    F = wd.shape[0]
    nj = F // tf

    def body(x_ref, g_ref, wgt_ref, wut_ref, wd_ref, xo_ref, xn_ref, gate_ref, up_ref, xn_sc, acc_sc):
        j = pl.program_id(1)

        @pl.when(j == 0)
        def _():
            _, xhat = _rms_stats(x_ref[...])
            xn = (xhat * g_ref[...]).astype(BF16)
            xn_sc[...] = xn
            xn_ref[...] = xn
            acc_sc[...] = jnp.zeros_like(acc_sc)

        xn = xn_sc[...]
        g = _dot_nt(xn, wgt_ref[...])
        u = _dot_nt(xn, wut_ref[...])
        gate_ref[...] = g.astype(BF16)
        up_ref[...] = u.astype(BF16)
        h = (g * _sigmoid(g) * u).astype(BF16)
        acc_sc[...] += _dot(h, wd_ref[...])

        @pl.when(j == nj - 1)
        def _():
            xo_ref[...] = x_ref[...] + 0.5 * acc_sc[...]

    wspec = pl.BlockSpec((tf, D), lambda i, j: (j, 0))
    return pl.pallas_call(
        body, name=name, grid=(T // tm, nj),
        in_specs=[pl.BlockSpec((tm, D), lambda i, j: (i, 0)),
                  pl.BlockSpec((1, D), lambda i, j: (0, 0)),
                  wspec, wspec, wspec],
        out_specs=(pl.BlockSpec((tm, D), lambda i, j: (i, 0)),
                   pl.BlockSpec((tm, D), lambda i, j: (i, 0)),
                   pl.BlockSpec((tm, tf), lambda i, j: (i, j)),
                   pl.BlockSpec((tm, tf), lambda i, j: (i, j))),
        out_shape=(jax.ShapeDtypeStruct((T, D), F32), jax.ShapeDtypeStruct((T, D), BF16),
                   jax.ShapeDtypeStruct((T, F), BF16), jax.ShapeDtypeStruct((T, F), BF16)),
        scratch_shapes=[pltpu.VMEM((tm, D), BF16), pltpu.VMEM((tm, D), F32)],
        compiler_params=_params("arbitrary", "arbitrary"),
    )(x, gain, wgt, wut, wd)


def _ffn_bwd(dy, x, gain, gate, up, wgt, wut, wd, after, name, tm, tf):
    T, D = x.shape
    F = gate.shape[1]
    nj = F // tf

    def body(dy_ref, x_ref, g_ref, gate_ref, up_ref, wgt_ref, wut_ref, wd_ref, after_ref,
             dx_ref, dgain_ref, dg_ref, du_ref, hh_ref, do_ref, do_sc, acc_sc):
        i, j = pl.program_id(0), pl.program_id(1)

        @pl.when((i == 0) & (j == 0))
        def _():
            dgain_ref[...] = jnp.zeros_like(dgain_ref)

        @pl.when(j == 0)
        def _():
            do = (0.5 * dy_ref[...]).astype(BF16)
            do_sc[...] = do
            do_ref[...] = do
            acc_sc[...] = jnp.zeros_like(acc_sc)

        dh = _dot_nt(do_sc[...], wd_ref[...])
        g = gate_ref[...].astype(F32)
        u = up_ref[...].astype(F32)
        sig = _sigmoid(g)
        s = g * sig
        dg = (dh * u * (sig * (1.0 + g * (1.0 - sig)))).astype(BF16)
        du = (dh * s).astype(BF16)
        dg_ref[...] = dg
        du_ref[...] = du
        hh_ref[...] = (s * u).astype(BF16)
        acc_sc[...] += _dot(dg, wgt_ref[...]) + _dot(du, wut_ref[...])

        @pl.when(j == nj - 1)
        def _():
            dx, dgain = _rms_bwd(acc_sc[...], x_ref[...], g_ref[...])
            dgain_ref[...] += dgain
            dx_ref[...] = dy_ref[...] + dx

    wspec = pl.BlockSpec((tf, D), lambda i, j: (j, 0))
    return pl.pallas_call(
        body, name=name, grid=(T // tm, nj),
        in_specs=[pl.BlockSpec((tm, D), lambda i, j: (i, 0)),
                  pl.BlockSpec((tm, D), lambda i, j: (i, 0)),
                  pl.BlockSpec((1, D), lambda i, j: (0, 0)),
                  pl.BlockSpec((tm, tf), lambda i, j: (i, j)),
                  pl.BlockSpec((tm, tf), lambda i, j: (i, j)),
                  wspec, wspec, wspec,
                  pl.BlockSpec(memory_space=pl.ANY)],
        out_specs=(pl.BlockSpec((tm, D), lambda i, j: (i, 0)),
                   pl.BlockSpec((1, D), lambda i, j: (0, 0)),
                   pl.BlockSpec((tm, tf), lambda i, j: (i, j)),
                   pl.BlockSpec((tm, tf), lambda i, j: (i, j)),
                   pl.BlockSpec((tm, tf), lambda i, j: (i, j)),
                   pl.BlockSpec((tm, D), lambda i, j: (i, 0))),
        out_shape=(jax.ShapeDtypeStruct((T, D), F32), jax.ShapeDtypeStruct((1, D), F32),
                   jax.ShapeDtypeStruct((T, F), BF16), jax.ShapeDtypeStruct((T, F), BF16),
                   jax.ShapeDtypeStruct((T, F), BF16), jax.ShapeDtypeStruct((T, D), BF16)),
        scratch_shapes=[pltpu.VMEM((tm, D), BF16), pltpu.VMEM((tm, D), F32)],
        compiler_params=_params("arbitrary", "arbitrary"),
    )(dy, x, gain, gate, up, wgt, wut, wd, after)


def _ffn_wgrad(dg, du, hh, xn, do, name, bm, tk):
    T, F = dg.shape
    D = xn.shape[1]
    nk = T // tk

    def body(dg_ref, du_ref, hh_ref, xn_ref, do_ref, og_ref, ou_ref, od_ref, ag_sc, au_sc, ad_sc):
        k = pl.program_id(1)

        @pl.when(k == 0)
        def _():
            ag_sc[...] = jnp.zeros_like(ag_sc)
            au_sc[...] = jnp.zeros_like(au_sc)
            ad_sc[...] = jnp.zeros_like(ad_sc)

        xn = xn_ref[...]
        ag_sc[...] += _dot_tn(dg_ref[...], xn)
        au_sc[...] += _dot_tn(du_ref[...], xn)
        ad_sc[...] += _dot_tn(hh_ref[...], do_ref[...])

        @pl.when(k == nk - 1)
        def _():
            og_ref[...] = ag_sc[...].astype(BF16)
            ou_ref[...] = au_sc[...].astype(BF16)
            od_ref[...] = ad_sc[...].astype(BF16)

    aspec = pl.BlockSpec((tk, bm), lambda i, k: (k, i))
    bspec = pl.BlockSpec((tk, D), lambda i, k: (k, 0))
    ospec = pl.BlockSpec((bm, D), lambda i, k: (i, 0))
    oshape = jax.ShapeDtypeStruct((F, D), BF16)
    return pl.pallas_call(
        body, name=name, grid=(F // bm, nk),
        in_specs=[aspec, aspec, aspec, bspec, bspec],
        out_specs=(ospec, ospec, ospec), out_shape=(oshape, oshape, oshape),
        scratch_shapes=[pltpu.VMEM((bm, D), F32)] * 3,
        compiler_params=_params("arbitrary", "arbitrary"),
    )(dg, du, hh, xn, do)


_Z_SPLITS = (0, 512, 640, 768, 1280, 1792, 2304)


def _mixin_fwd(x, gain, w_in_t, name, tm):
    T, D = x.shape
    widths = [b - a for a, b in zip(_Z_SPLITS[:-1], _Z_SPLITS[1:])]

    def body(x_ref, g_ref, w_ref, hn_ref, *outs):
        _, xhat = _rms_stats(x_ref[...])
        hn = (xhat * g_ref[...]).astype(BF16)
        hn_ref[...] = hn
        for o_ref, lo, hi in zip(outs, _Z_SPLITS[:-1], _Z_SPLITS[1:]):
            o_ref[...] = _dot_nt(hn, w_ref[lo:hi, :]).astype(BF16)

    return pl.pallas_call(
        body, name=name, grid=(T // tm,),
        in_specs=[pl.BlockSpec((tm, D), lambda i: (i, 0)),
                  pl.BlockSpec((1, D), lambda i: (0, 0)),
                  pl.BlockSpec(w_in_t.shape, lambda i: (0, 0))],
        out_specs=tuple([pl.BlockSpec((tm, D), lambda i: (i, 0))]
                        + [pl.BlockSpec((tm, w), lambda i: (i, 0)) for w in widths]),
        out_shape=tuple([jax.ShapeDtypeStruct((T, D), BF16)]
                        + [jax.ShapeDtypeStruct((T, w), BF16) for w in widths]),
        compiler_params=_params("arbitrary"),
    )(x, gain, w_in_t)


def _bucket_table():
    qi = np.arange(BLOCK, dtype=np.int32)[:, None]
    kj = np.arange(2 * BLOCK, dtype=np.int32)[None, :]
    dist = qi + BLOCK - kj
    n = np.maximum(dist, 0)
    max_exact = NUM_BUCKETS // 2
    large = max_exact + (np.log(np.maximum(n, 1).astype(np.float32) / max_exact)
                         / math.log(MAX_DISTANCE / max_exact)
                         * (NUM_BUCKETS - max_exact)).astype(np.int32)
    large = np.minimum(large, NUM_BUCKETS - 1)
    bucket = np.where(n < max_exact, n, large).astype(np.int32)
    valid = (dist >= 0) & (dist < WINDOW)
    return np.where(valid, bucket, -1).astype(np.int32)


def _bias_build(table, bucket, name):
    def body(t_ref, b_ref, o_ref):
        bk = b_ref[...]
        for h in range(N_Q_HEADS):
            def step(b, acc):
                return jnp.where(bk == b, t_ref[b, h], acc)
            o_ref[h] = lax.fori_loop(0, NUM_BUCKETS, step, jnp.full(bk.shape, MASKED, F32))

    return pl.pallas_call(
        body, name=name,
        in_specs=[pl.BlockSpec(memory_space=pltpu.SMEM), pl.BlockSpec(memory_space=pltpu.VMEM)],
        out_specs=pl.BlockSpec(memory_space=pltpu.VMEM),
        out_shape=jax.ShapeDtypeStruct((N_Q_HEADS,) + bucket.shape, F32),
    )(table, bucket)


def _bias_grad(dbias, bucket, name):
    def body(d_ref, b_ref, o_ref):
        bk = b_ref[...]
        row = lax.broadcasted_iota(jnp.int32, o_ref.shape, 0)
        lane = lax.broadcasted_iota(jnp.int32, o_ref.shape, 1)
        res = jnp.zeros(o_ref.shape, F32)
        for h in range(N_Q_HEADS):
            d = d_ref[h]

            def step(b, acc):
                tot = jnp.sum(jnp.where(bk == b, d, 0.0), axis=1, keepdims=True)
                tot = jnp.sum(tot, axis=0, keepdims=True)
                return jnp.where((row == b) & (lane == h), tot, acc)
            res = lax.fori_loop(0, NUM_BUCKETS, step, res)
        o_ref[...] = res

    return pl.pallas_call(
        body, name=name,
        in_specs=[pl.BlockSpec(memory_space=pltpu.VMEM), pl.BlockSpec(memory_space=pltpu.VMEM)],
        out_specs=pl.BlockSpec(memory_space=pltpu.VMEM),
        out_shape=jax.ShapeDtypeStruct((NUM_BUCKETS, 128), F32),
    )(dbias, bucket)


def _head_cols(h):
    return slice(h * HEAD_DIM, (h + 1) * HEAD_DIM)


def _stack_heads(ref, r0, g, dtype):
    return jnp.concatenate(
        [ref[pl.ds(r0, BLOCK), _head_cols(GQA_GROUP * g + j)].astype(dtype) for j in range(GQA_GROUP)], axis=0)


def _unstack_heads(ref, r0, g, val):
    for j in range(GQA_GROUP):
        ref[pl.ds(r0, BLOCK), _head_cols(GQA_GROUP * g + j)] = val[j * BLOCK:(j + 1) * BLOCK, :]


def _group_softmax(qg, kb, bias, sink, no_prev):
    s = _dot_nt(qg, kb) * SCALE + bias
    col = lax.broadcasted_iota(jnp.int32, s.shape, 1)
    s = jnp.where(no_prev & (col < BLOCK), MASKED, s)
    m = jnp.maximum(jnp.max(s, axis=-1, keepdims=True), sink)
    p = jnp.exp(s - m)
    ps = jnp.exp(sink - m)
    r = 1.0 / (jnp.sum(p, axis=-1, keepdims=True) + ps)
    return p * r, ps * r


def _attn_fwd(q, k, v, bias, sinkcol, gain, name, nblk):
    T = q.shape[0]
    tq = nblk * BLOCK

    def body(q_ref, kc_ref, kp_ref, vc_ref, vp_ref, bias_ref, sink_ref, g_ref, raw_ref, nrm_ref,
             kf_sc, vf_sc, o_sc):
        i = pl.program_id(0)
        kf_sc[0:BLOCK, :] = kp_ref[...]
        kf_sc[BLOCK:BLOCK + tq, :] = kc_ref[...]
        vf_sc[0:BLOCK, :] = vp_ref[...]
        vf_sc[BLOCK:BLOCK + tq, :] = vc_ref[...]

        def block(b, carry):
            r0 = pl.multiple_of(b * BLOCK, BLOCK)
            no_prev = (i == 0) & (b == 0)
            for g in range(N_KV_HEADS):
                qg = _stack_heads(q_ref, r0, g, BF16)
                kb = kf_sc[pl.ds(r0, 2 * BLOCK), _head_cols(g)]
                vb = vf_sc[pl.ds(r0, 2 * BLOCK), _head_cols(g)]
                p, _ = _group_softmax(qg, kb, bias_ref[g], sink_ref[g], no_prev)
                _unstack_heads(o_sc, r0, g, _dot(p.astype(BF16), vb))
            return carry

        lax.fori_loop(0, nblk, block, 0)
        o = o_sc[...]
        raw_ref[...] = o.astype(BF16)
        _, ohat = _rms_stats(o)
        nrm_ref[...] = (ohat * g_ref[...]).astype(BF16)

    cur = lambda i: (i, 0)
    prev = lambda i: (jnp.maximum(i * nblk - 1, 0), 0)
    const3 = lambda i: (0, 0, 0)
    return pl.pallas_call(
        body, name=name, grid=(T // tq,),
        in_specs=[pl.BlockSpec((tq, ATTN_WIDTH), cur),
                  pl.BlockSpec((tq, KV_WIDTH), cur), pl.BlockSpec((BLOCK, KV_WIDTH), prev),
                  pl.BlockSpec((tq, KV_WIDTH), cur), pl.BlockSpec((BLOCK, KV_WIDTH), prev),
                  pl.BlockSpec(bias.shape, const3), pl.BlockSpec(sinkcol.shape, const3),
                  pl.BlockSpec((1, ATTN_WIDTH), lambda i: (0, 0))],
        out_specs=(pl.BlockSpec((tq, ATTN_WIDTH), cur), pl.BlockSpec((tq, ATTN_WIDTH), cur)),
        out_shape=(jax.ShapeDtypeStruct((T, ATTN_WIDTH), BF16), jax.ShapeDtypeStruct((T, ATTN_WIDTH), BF16)),
        scratch_shapes=[pltpu.VMEM((tq + BLOCK, KV_WIDTH), BF16), pltpu.VMEM((tq + BLOCK, KV_WIDTH), BF16),
                        pltpu.VMEM((tq, ATTN_WIDTH), F32)],
        compiler_params=_params("arbitrary"),
    )(q, k, k, v, v, bias, sinkcol, gain)


def _attn_bwd(dmixed, raw, q, k, v, bias, sinkcol, gain, name, nblk):
    T = q.shape[0]
    tq = nblk * BLOCK
    nt = T // tq

    def body(dm_ref, raw_ref, q_ref, kc_ref, kp_ref, vc_ref, vp_ref, bias_ref, sink_ref, g_ref,
             dq_ref, dk_ref, dv_ref, dbias_ref, dsink_ref, dgain_ref,
             do_sc, dq_sc, kf_sc, vf_sc, dkf_sc, dvf_sc, dsink_sc):
        i = pl.program_id(0)
        tile = nt - 1 - i

        @pl.when(i == 0)
        def _():
            dkf_sc[...] = jnp.zeros_like(dkf_sc)
            dvf_sc[...] = jnp.zeros_like(dvf_sc)
            dsink_sc[...] = jnp.zeros_like(dsink_sc)
            dbias_ref[...] = jnp.zeros_like(dbias_ref)
            dgain_ref[...] = jnp.zeros_like(dgain_ref)

        carry_k = dkf_sc[0:BLOCK, :]
        carry_v = dvf_sc[0:BLOCK, :]
        dkf_sc[0:tq, :] = jnp.zeros((tq, KV_WIDTH), F32)
        dvf_sc[0:tq, :] = jnp.zeros((tq, KV_WIDTH), F32)
        dkf_sc[tq:tq + BLOCK, :] = carry_k
        dvf_sc[tq:tq + BLOCK, :] = carry_v
        kf_sc[0:BLOCK, :] = kp_ref[...]
        kf_sc[BLOCK:BLOCK + tq, :] = kc_ref[...]
        vf_sc[0:BLOCK, :] = vp_ref[...]
        vf_sc[BLOCK:BLOCK + tq, :] = vc_ref[...]

        do, dgain = _rms_bwd(dm_ref[...].astype(F32), raw_ref[...].astype(F32), g_ref[...])
        dgain_ref[...] += dgain
        do_sc[...] = do

        def block(b, carry):
            r0 = pl.multiple_of(b * BLOCK, BLOCK)
            no_prev = (tile == 0) & (b == 0)
            for g in range(N_KV_HEADS):
                qg = _stack_heads(q_ref, r0, g, BF16)
                kb = kf_sc[pl.ds(r0, 2 * BLOCK), _head_cols(g)]
                vb = vf_sc[pl.ds(r0, 2 * BLOCK), _head_cols(g)]
                p, ps = _group_softmax(qg, kb, bias_ref[g], sink_ref[g], no_prev)
                dog = _stack_heads(do_sc, r0, g, F32)
                og = _stack_heads(raw_ref, r0, g, F32)
                rowdot = jnp.sum(dog * og, axis=-1, keepdims=True)
                dogb = dog.astype(BF16)
                ds = p * (_dot_nt(dogb, vb) - rowdot)
                dsink_sc[g] += -(ps * rowdot)
                dbias_ref[g] += ds
                dsb = ds.astype(BF16)
                _unstack_heads(dq_sc, r0, g, _dot(dsb, kb) * SCALE)
                dkf_sc[pl.ds(r0, 2 * BLOCK), _head_cols(g)] += _dot_tn(dsb, qg) * SCALE
                dvf_sc[pl.ds(r0, 2 * BLOCK), _head_cols(g)] += _dot_tn(p.astype(BF16), dogb)
            return carry

        lax.fori_loop(0, nblk, block, 0)
        dq_ref[...] = dq_sc[...].astype(BF16)
        dk_ref[...] = dkf_sc[BLOCK:BLOCK + tq, :].astype(BF16)
        dv_ref[...] = dvf_sc[BLOCK:BLOCK + tq, :].astype(BF16)

        @pl.when(i == nt - 1)
        def _():
            lane = lax.broadcasted_iota(jnp.int32, dsink_ref.shape, 1)
            tot = jnp.zeros(dsink_ref.shape, F32)
            for h in range(N_Q_HEADS):
                g, j = divmod(h, GQA_GROUP)
                part = jnp.sum(dsink_sc[g, j * BLOCK:(j + 1) * BLOCK, :], axis=0, keepdims=True)
                tot = jnp.where(lane == h, part, tot)
            dsink_ref[...] = tot

    cur = lambda i: (nt - 1 - i, 0)
    prev = lambda i: (jnp.maximum((nt - 1 - i) * nblk - 1, 0), 0)
    const2 = lambda i: (0, 0)
    const3 = lambda i: (0, 0, 0)
    return pl.pallas_call(
        body, name=name, grid=(nt,),
        in_specs=[pl.BlockSpec((tq, ATTN_WIDTH), cur),
                  pl.BlockSpec((tq, ATTN_WIDTH), cur),
                  pl.BlockSpec((tq, ATTN_WIDTH), cur),
                  pl.BlockSpec((tq, KV_WIDTH), cur), pl.BlockSpec((BLOCK, KV_WIDTH), prev),
                  pl.BlockSpec((tq, KV_WIDTH), cur), pl.BlockSpec((BLOCK, KV_WIDTH), prev),
                  pl.BlockSpec(bias.shape, const3), pl.BlockSpec(sinkcol.shape, const3),
                  pl.BlockSpec((1, ATTN_WIDTH), const2)],
        out_specs=(pl.BlockSpec((tq, ATTN_WIDTH), cur),
                   pl.BlockSpec((tq, KV_WIDTH), cur), pl.BlockSpec((tq, KV_WIDTH), cur),
                   pl.BlockSpec(bias.shape, const3),
                   pl.BlockSpec((SUBLANES, 128), const2),
                   pl.BlockSpec((1, ATTN_WIDTH), const2)),
        out_shape=(jax.ShapeDtypeStruct((T, ATTN_WIDTH), BF16),
                   jax.ShapeDtypeStruct((T, KV_WIDTH), BF16), jax.ShapeDtypeStruct((T, KV_WIDTH), BF16),
                   jax.ShapeDtypeStruct(bias.shape, F32),
                   jax.ShapeDtypeStruct((SUBLANES, 128), F32),
                   jax.ShapeDtypeStruct((1, ATTN_WIDTH), F32)),
        scratch_shapes=[pltpu.VMEM((tq, ATTN_WIDTH), F32), pltpu.VMEM((tq, ATTN_WIDTH), F32),
                        pltpu.VMEM((tq + BLOCK, KV_WIDTH), BF16), pltpu.VMEM((tq + BLOCK, KV_WIDTH), BF16),
                        pltpu.VMEM((tq + BLOCK, KV_WIDTH), F32), pltpu.VMEM((tq + BLOCK, KV_WIDTH), F32),
                        pltpu.VMEM((N_KV_HEADS, GROUP_ROWS, 1), F32)],
        compiler_params=_params("arbitrary"),
    )(dmixed, raw, q, k, k, v, v, bias, sinkcol, gain)


def _shift_down(cu, tail):
    row = lax.broadcasted_iota(jnp.int32, cu.shape, 0)
    t6, t7 = tail[6:7, :], tail[7:8, :]
    s1 = jnp.where(row == 0, t7, pltpu.roll(cu, 1, 0))
    s2 = jnp.where(row == 0, t6, jnp.where(row == 1, t7, pltpu.roll(cu, 2, 0)))
    return s1, s2


def _shift_up(d, head):
    n = d.shape[0]
    row = lax.broadcasted_iota(jnp.int32, d.shape, 0)
    h0, h1 = head[0:1, :], head[1:2, :]
    s1 = jnp.where(row == n - 1, h0, pltpu.roll(d, n - 1, 0))
    s2 = jnp.where(row == n - 1, h1, jnp.where(row == n - 2, h0, pltpu.roll(d, n - 2, 0)))
    return s1, s2


def _mixout_fwd(x, attn_n, u, gb, gc, conv_w, gain, w_out, name, tm):
    T, D = x.shape

    def body(x_ref, an_ref, u_ref, b_ref, c_ref, cw_ref, g_ref, wo_ref, xo_ref, cn_ref, tail_sc):
        @pl.when(pl.program_id(0) == 0)
        def _():
            tail_sc[...] = jnp.zeros_like(tail_sc)

        cu = c_ref[...].astype(F32) * u_ref[...].astype(F32)
        s1, s2 = _shift_down(cu, tail_sc[...])
        tail_sc[...] = cu[tm - SUBLANES:tm, :]
        pre = cw_ref[0:1, :] * s2 + cw_ref[1:2, :] * s1 + cw_ref[2:3, :] * cu
        conv = b_ref[...].astype(F32) * pre
        _, chat = _rms_stats(conv)
        cn = (chat * g_ref[...]).astype(BF16)
        cn_ref[...] = cn
        xo_ref[...] = (x_ref[...] + _dot(an_ref[...], wo_ref[0:ATTN_WIDTH, :])
                       + _dot(cn, wo_ref[ATTN_WIDTH:ATTN_WIDTH + CONV_DIM, :]))

    row = lambda i: (i, 0)
    const = lambda i: (0, 0)
    return pl.pallas_call(
        body, name=name, grid=(T // tm,),
        in_specs=[pl.BlockSpec((tm, D), row), pl.BlockSpec((tm, ATTN_WIDTH), row),
                  pl.BlockSpec((tm, CONV_DIM), row), pl.BlockSpec((tm, CONV_DIM), row),
                  pl.BlockSpec((tm, CONV_DIM), row),
                  pl.BlockSpec(conv_w.shape, const), pl.BlockSpec((1, CONV_DIM), const),
                  pl.BlockSpec(w_out.shape, const)],
        out_specs=(pl.BlockSpec((tm, D), row), pl.BlockSpec((tm, CONV_DIM), row)),
        out_shape=(jax.ShapeDtypeStruct((T, D), F32), jax.ShapeDtypeStruct((T, CONV_DIM), BF16)),
        scratch_shapes=[pltpu.VMEM((SUBLANES, CONV_DIM), F32)],
        compiler_params=_params("arbitrary"),
    )(x, attn_n, u, gb, gc, conv_w, gain, w_out)


def _mixout_bwd(dy, attn_n, conv_n, w_out, after, name, tm):
    T, D = dy.shape
    W = ATTN_WIDTH + CONV_DIM
    nt = T // tm

    def body(dy_ref, an_ref, cn_ref, w_ref, after_ref, dm_ref, dw_ref, dw_sc):
        i = pl.program_id(0)

        @pl.when(i == 0)
        def _():
            dw_sc[...] = jnp.zeros_like(dw_sc)

        dyb = dy_ref[...].astype(BF16)
        dm_ref[...] = _dot_nt(dyb, w_ref[...]).astype(BF16)
        dw_sc[0:ATTN_WIDTH, :] += _dot_tn(an_ref[...], dyb)
        dw_sc[ATTN_WIDTH:W, :] += _dot_tn(cn_ref[...], dyb)

        @pl.when(i == nt - 1)
        def _():
            dw_ref[...] = dw_sc[...].astype(BF16)

    row = lambda i: (i, 0)
    const = lambda i: (0, 0)
    return pl.pallas_call(
        body, name=name, grid=(nt,),
        in_specs=[pl.BlockSpec((tm, D), row), pl.BlockSpec((tm, ATTN_WIDTH), row),
                  pl.BlockSpec((tm, CONV_DIM), row), pl.BlockSpec(w_out.shape, const),
                  pl.BlockSpec(memory_space=pl.ANY)],
        out_specs=(pl.BlockSpec((tm, W), row), pl.BlockSpec((W, D), const)),
        out_shape=(jax.ShapeDtypeStruct((T, W), BF16), jax.ShapeDtypeStruct((W, D), BF16)),
        scratch_shapes=[pltpu.VMEM((W, D), F32)],
        compiler_params=_params("arbitrary"),
    )(dy, attn_n, conv_n, w_out, after)


def _conv_bwd(dmixed, u, gb, gc, conv_w, gain, name, tc):
    T = u.shape[0]
    nt = T // tc
    per8 = tc // SUBLANES

    def body(dm_ref, u_ref, b_ref, c_ref, ut_ref, ct_ref, cw_ref, g_ref,
             du_ref, db_ref, dc_ref, dcw_ref, dgain_ref, head_sc):
        i = pl.program_id(0)

        @pl.when(i == 0)
        def _():
            head_sc[...] = jnp.zeros_like(head_sc)
            dcw_ref[...] = jnp.zeros_like(dcw_ref)
            dgain_ref[...] = jnp.zeros_like(dgain_ref)

        uu = u_ref[...].astype(F32)
        cc = c_ref[...].astype(F32)
        bb = b_ref[...].astype(F32)
        cu = cc * uu
        tail = jnp.where(i == nt - 1, 0.0, ct_ref[...].astype(F32) * ut_ref[...].astype(F32))
        s1, s2 = _shift_down(cu, tail)
        w0, w1, w2 = cw_ref[0:1, :], cw_ref[1:2, :], cw_ref[2:3, :]
        pre = w0 * s2 + w1 * s1 + w2 * cu
        dconv, dgain = _rms_bwd(dm_ref[...].astype(F32), bb * pre, g_ref[...])
        dgain_ref[...] += dgain
        db_ref[...] = (dconv * pre).astype(BF16)
        dpre = dconv * bb
        dcw_ref[0:1, :] += jnp.sum(dpre * s2, axis=0, keepdims=True)
        dcw_ref[1:2, :] += jnp.sum(dpre * s1, axis=0, keepdims=True)
        dcw_ref[2:3, :] += jnp.sum(dpre * cu, axis=0, keepdims=True)
        n1, n2 = _shift_up(dpre, head_sc[...])
        head_sc[...] = dpre[0:SUBLANES, :]
        dcu = w2 * dpre + w1 * n1 + w0 * n2
        du_ref[...] = (dcu * cc).astype(BF16)
        dc_ref[...] = (dcu * uu).astype(BF16)

    rev = lambda i: (nt - 1 - i, 0)
    rev_right = lambda i: (nt - 1 - i, 1)
    tail_map = lambda i: (jnp.maximum((nt - 1 - i) * per8 - 1, 0), 0)
    const = lambda i: (0, 0)
    return pl.pallas_call(
        body, name=name, grid=(nt,),
        in_specs=[pl.BlockSpec((tc, CONV_DIM), rev_right),
                  pl.BlockSpec((tc, CONV_DIM), rev), pl.BlockSpec((tc, CONV_DIM), rev),
                  pl.BlockSpec((tc, CONV_DIM), rev),
                  pl.BlockSpec((SUBLANES, CONV_DIM), tail_map), pl.BlockSpec((SUBLANES, CONV_DIM), tail_map),
                  pl.BlockSpec(conv_w.shape, const), pl.BlockSpec((1, CONV_DIM), const)],
        out_specs=(pl.BlockSpec((tc, CONV_DIM), rev), pl.BlockSpec((tc, CONV_DIM), rev),
                   pl.BlockSpec((tc, CONV_DIM), rev),
                   pl.BlockSpec((SUBLANES, CONV_DIM), const), pl.BlockSpec((1, CONV_DIM), const)),
        out_shape=(jax.ShapeDtypeStruct((T, CONV_DIM), BF16), jax.ShapeDtypeStruct((T, CONV_DIM), BF16),
                   jax.ShapeDtypeStruct((T, CONV_DIM), BF16),
                   jax.ShapeDtypeStruct((SUBLANES, CONV_DIM), F32), jax.ShapeDtypeStruct((1, CONV_DIM), F32)),
        scratch_shapes=[pltpu.VMEM((SUBLANES, CONV_DIM), F32)],
        compiler_params=_params("arbitrary"),
    )(dmixed, u, gb, gc, u, gc, conv_w, gain)


def _mixin_bwd(dy, x, gain, dz, w_in_t, name, tm):
    T, D = x.shape
    nz = len(dz)

    def body(dy_ref, x_ref, g_ref, *rest):
        dz_refs, wt_ref, dx_ref, dgain_ref = rest[:nz], rest[nz], rest[nz + 1], rest[nz + 2]

        @pl.when(pl.program_id(0) == 0)
        def _():
            dgain_ref[...] = jnp.zeros_like(dgain_ref)

        dh = jnp.zeros((tm, D), F32)
        for r, lo, hi in zip(dz_refs, _Z_SPLITS[:-1], _Z_SPLITS[1:]):
            dh += _dot(r[...], wt_ref[lo:hi, :])
        dx, dgain = _rms_bwd(dh, x_ref[...], g_ref[...])
        dgain_ref[...] += dgain
        dx_ref[...] = dy_ref[...] + dx

    row = lambda i: (i, 0)
    const = lambda i: (0, 0)
    return pl.pallas_call(
        body, name=name, grid=(T // tm,),
        in_specs=[pl.BlockSpec((tm, D), row), pl.BlockSpec((tm, D), row), pl.BlockSpec((1, D), const)]
                 + [pl.BlockSpec((tm, a.shape[1]), row) for a in dz]
                 + [pl.BlockSpec(w_in_t.shape, const)],
        out_specs=(pl.BlockSpec((tm, D), row), pl.BlockSpec((1, D), const)),
        out_shape=(jax.ShapeDtypeStruct((T, D), F32), jax.ShapeDtypeStruct((1, D), F32)),
        compiler_params=_params("arbitrary"),
    )(dy, x, gain, *dz, w_in_t)


def _win_grad(dz, hn, name, tk):
    T, D = hn.shape
    nz = len(dz)
    nt = T // tk
    W = _Z_SPLITS[-1]

    def body(hn_ref, *rest):
        dz_refs, dw_ref, dw_sc = rest[:nz], rest[nz], rest[nz + 1]
        i = pl.program_id(0)

        @pl.when(i == 0)
        def _():
            dw_sc[...] = jnp.zeros_like(dw_sc)

        hn = hn_ref[...]
        for r, lo, hi in zip(dz_refs, _Z_SPLITS[:-1], _Z_SPLITS[1:]):
            dw_sc[lo:hi, :] += _dot_tn(r[...], hn)

        @pl.when(i == nt - 1)
        def _():
            dw_ref[...] = dw_sc[...].astype(BF16)

    row = lambda i: (i, 0)
    return pl.pallas_call(
        body, name=name, grid=(nt,),
        in_specs=[pl.BlockSpec((tk, D), row)] + [pl.BlockSpec((tk, a.shape[1]), row) for a in dz],
        out_specs=pl.BlockSpec((W, D), lambda i: (0, 0)),
        out_shape=jax.ShapeDtypeStruct((W, D), BF16),
        scratch_shapes=[pltpu.VMEM((W, D), F32)],
        compiler_params=_params("arbitrary"),
    )(hn, *dz)


def _loss_head(x, target, gain, name, tm):
    T, D = x.shape

    def body(x_ref, t_ref, g_ref, dx_ref, dgain_ref, loss_ref):
        @pl.when(pl.program_id(0) == 0)
        def _():
            dgain_ref[...] = jnp.zeros_like(dgain_ref)
            loss_ref[...] = jnp.zeros_like(loss_ref)

        xv = x_ref[...]
        gain_v = g_ref[...]
        _, xhat = _rms_stats(xv)
        err = xhat * gain_v - t_ref[...]
        part = 0.5 * jnp.sum(jnp.mean(err * err, axis=-1, keepdims=True), axis=0, keepdims=True)
        loss_ref[...] += part
        dx, dgain = _rms_bwd(err * (1.0 / D), xv, gain_v)
        dgain_ref[...] += dgain
        dx_ref[...] = dx

    row = lambda i: (i, 0)
    const = lambda i: (0, 0)
    return pl.pallas_call(
        body, name=name, grid=(T // tm,),
        in_specs=[pl.BlockSpec((tm, D), row), pl.BlockSpec((tm, D), row), pl.BlockSpec((1, D), const)],
        out_specs=(pl.BlockSpec((tm, D), row), pl.BlockSpec((1, D), const),
                   pl.BlockSpec((SUBLANES, 128), const)),
        out_shape=(jax.ShapeDtypeStruct((T, D), F32), jax.ShapeDtypeStruct((1, D), F32),
                   jax.ShapeDtypeStruct((SUBLANES, 128), F32)),
        compiler_params=_params("arbitrary"),
    )(x, target, gain)


def _sum_parts(parts, name, tr):
    P, R, C = parts.shape

    def body(p_ref, o_ref):
        g = p_ref[0].astype(F32)
        for d in range(1, P):
            g = g + p_ref[d].astype(F32)
        o_ref[...] = g

    return pl.pallas_call(
        body, name=name, grid=(R // tr,),
        in_specs=[pl.BlockSpec((P, tr, C), lambda i: (0, i, 0))],
        out_specs=pl.BlockSpec((tr, C), lambda i: (i, 0)),
        out_shape=jax.ShapeDtypeStruct((R, C), F32),
        compiler_params=_params("arbitrary"),
    )(parts)


def _adamw(parts, w, m, v, name, tr):
    P = parts.shape[0]
    R, C = w.shape

    def body(p_ref, w_ref, m_ref, v_ref, g_ref, d_ref, nm_ref, nv_ref):
        g = p_ref[0].astype(F32)
        for d in range(1, P):
            g = g + p_ref[d].astype(F32)
        nm = ADAM_B1 * m_ref[...] + (1.0 - ADAM_B1) * g
        nv = ADAM_B2 * v_ref[...] + (1.0 - ADAM_B2) * (g * g)
        m_hat = nm / (1.0 - ADAM_B1 ** ADAM_STEP)
        v_hat = nv / (1.0 - ADAM_B2 ** ADAM_STEP)
        g_ref[...] = g
        nm_ref[...] = nm
        nv_ref[...] = nv
        d_ref[...] = -ADAM_LR * (m_hat / (jnp.sqrt(v_hat) + ADAM_EPS) + ADAM_WD * w_ref[...])

    row = lambda i: (i, 0)
    spec = pl.BlockSpec((tr, C), row)
    shp = jax.ShapeDtypeStruct((R, C), F32)
    return pl.pallas_call(
        body, name=name, grid=(R // tr,),
        in_specs=[pl.BlockSpec((P, tr, C), lambda i: (0, i, 0)), spec, spec, spec],
        out_specs=(spec, spec, spec, spec),
        out_shape=(shp, shp, shp, shp),
        compiler_params=_params("arbitrary"),
    )(parts, w, m, v)


def _columns_of_blocks(g):
    n, R, w = g.shape
    return g.transpose(1, 0, 2).reshape(R, n * w)


def _pad_row(vec):
    vec = vec.reshape(1, -1)
    return jnp.pad(vec, ((0, 0), (0, PACK_COLS - vec.shape[1])))


def kernel(x, rel_bias_table, ffn1_norm, ffn1_w_gate, ffn1_w_up, ffn1_w_down, mix_norm, w_in, conv_w, attn_sinks, attn_out_norm, conv_out_norm, w_out, ffn2_norm, ffn2_w_gate, ffn2_w_up, ffn2_w_down, final_norm, loss_target, m_rel_bias_table, m_ffn1_norm, m_ffn1_w_gate, m_ffn1_w_up, m_ffn1_w_down, m_mix_norm, m_w_in, m_conv_w, m_attn_sinks, m_attn_out_norm, m_conv_out_norm, m_w_out, m_ffn2_norm, m_ffn2_w_gate, m_ffn2_w_up, m_ffn2_w_down, m_final_norm, v_rel_bias_table, v_ffn1_norm, v_ffn1_w_gate, v_ffn1_w_up, v_ffn1_w_down, v_mix_norm, v_w_in, v_conv_w, v_attn_sinks, v_attn_out_norm, v_conv_out_norm, v_w_out, v_ffn2_norm, v_ffn2_w_gate, v_ffn2_w_up, v_ffn2_w_down, v_final_norm):
    T, D = x.shape[1], x.shape[2]
    x0 = x[0]
    target = loss_target[0]
    tm = min(TM_FFN, T)
    tm_bwd = min(TM_FFN_BWD, T)
    tm_mix = min(TM_MIX, T)
    tk = min(TK_WGRAD, T)
    tf = TF_FFN
    nblk = min(ATTN_BLOCKS, T // BLOCK)
    me = 4 * lax.axis_index("x") + 2 * lax.axis_index("y") + lax.axis_index("c")

    big = {
        "ffn1_w_gate": (ffn1_w_gate[0], m_ffn1_w_gate[0], v_ffn1_w_gate[0], True),
        "ffn1_w_up": (ffn1_w_up[0], m_ffn1_w_up[0], v_ffn1_w_up[0], True),
        "ffn1_w_down": (ffn1_w_down[0], m_ffn1_w_down[0], v_ffn1_w_down[0], False),
        "w_in": (w_in[0], m_w_in[0], v_w_in[0], True),
        "w_out": (w_out[0], m_w_out[0], v_w_out[0], False),
        "ffn2_w_gate": (ffn2_w_gate[0], m_ffn2_w_gate[0], v_ffn2_w_gate[0], True),
        "ffn2_w_up": (ffn2_w_up[0], m_ffn2_w_up[0], v_ffn2_w_up[0], True),
        "ffn2_w_down": (ffn2_w_down[0], m_ffn2_w_down[0], v_ffn2_w_down[0], False),
    }

    def block_to_send(name):
        w, _, _, transposed = big[name]
        return (w.T if transposed else w).astype(BF16)

    names1 = ["ffn1_w_gate", "ffn1_w_up", "ffn1_w_down"]
    names_rest = ["w_in", "w_out", "ffn2_w_gate", "ffn2_w_up", "ffn2_w_down"]
    h1, token = _exchange_start([block_to_send(n) for n in names1], False, "gather_start_ffn1", ffn1_norm)
    h_rest, token = _exchange_start([block_to_send(n) for n in names_rest] + [conv_w[0]], False,
                                    "gather_start_rest", token)
    wgt1, wut1, wd1 = [g.reshape(-1, D) for g in _exchange_wait(h1, False, "gather_wait_ffn1", token)]

    x1, xn1, gate1, up1 = _ffn_fwd(x0, ffn1_norm, wgt1, wut1, wd1, "ffn1_fwd", tm, tf)
    mixw = _exchange_wait([h_rest[0], h_rest[1], h_rest[5]], False, "gather_wait_mix", x1)
    win_t = mixw[0].reshape(-1, D)
    wout = mixw[1].reshape(-1, D)
    cw = _columns_of_blocks(mixw[2])
    hn, q, k, v, u, gb, gc = _mixin_fwd(x1, mix_norm, win_t, "mixin_fwd", tm_mix)
    bucket = jnp.asarray(_bucket_table())
    bias = _bias_build(rel_bias_table, bucket, "bias_build").reshape(N_KV_HEADS, GROUP_ROWS, 2 * BLOCK)
    sinkcol = jnp.repeat(attn_sinks.reshape(N_KV_HEADS, GQA_GROUP, 1), BLOCK, axis=1).reshape(
        N_KV_HEADS, GROUP_ROWS, 1)
    attn_raw, attn_n = _attn_fwd(q, k, v, bias, sinkcol, attn_out_norm, "attn_fwd", nblk)
    x2, conv_n = _mixout_fwd(x1, attn_n, u, gb, gc, cw, conv_out_norm, wout, "mixout_fwd", tm_mix)
    wgt2, wut2, wd2 = [g.reshape(-1, D) for g in _exchange_wait(h_rest[2:5], False, "gather_wait_ffn2", x2)]
    x3, xn2, gate2, up2 = _ffn_fwd(x2, ffn2_norm, wgt2, wut2, wd2, "ffn2_fwd", tm, tf)

    def blocks(g):
        return g.reshape(N_DEV, -1, D)

    dx3, d_final, loss_part = _loss_head(x3, target, final_norm.reshape(1, D), "loss_head", tm_mix)
    dx2, d_ffn2_norm, dg2, du2, hh2, do2 = _ffn_bwd(
        dx3, x2, ffn2_norm, gate2, up2, wgt2, wut2, wd2, dx3, "ffn2_bwd", tm_bwd, tf)
    grads2 = _ffn_wgrad(dg2, du2, hh2, xn2, do2, "ffn2_wgrad", BM_WGRAD, tk)
    handles2, token2 = _exchange_start([blocks(g) for g in grads2], True, "grads_start_ffn2", dx2)

    dmixed, d_wout = _mixout_bwd(dx2, attn_n, conv_n, wout, token2, "mixout_bwd", tm_mix)
    dq, dk, dv, dbias, dsink, d_attn_norm = _attn_bwd(
        dmixed, attn_raw, q, k, v, bias, sinkcol, attn_out_norm, "attn_bwd", nblk)
    du, dgb, dgc, d_cw, d_conv_norm = _conv_bwd(dmixed, u, gb, gc, cw, conv_out_norm, "conv_bwd", tm_mix)
    d_table = _bias_grad(dbias.reshape(N_Q_HEADS, BLOCK, 2 * BLOCK), bucket, "bias_grad")
    dz = [dq, dk, dv, du, dgb, dgc]
    dx1, d_mix_norm = _mixin_bwd(dx2, x1, mix_norm, dz, win_t, "mixin_bwd", tm_mix)
    d_win_t = _win_grad(dz, hn, "win_grad", min(TM_MIX, T))
    handles_mix, token_mix = _exchange_start([blocks(d_win_t), blocks(d_wout)], True, "grads_start_mix", dx1)

    dx0, d_ffn1_norm, dg1, du1, hh1, do1 = _ffn_bwd(
        dx1, x0, ffn1_norm, gate1, up1, wgt1, wut1, wd1, token_mix, "ffn1_bwd", tm_bwd, tf)
    grads1 = _ffn_wgrad(dg1, du1, hh1, xn1, do1, "ffn1_wgrad", BM_WGRAD, tk)
    handles1, token1 = _exchange_start([blocks(g) for g in grads1], True, "grads_start_ffn1", dx0)

    res = {}

    def update(names, parts):
        last = None
        for name, p in zip(names, parts):
            w, m_, v_, transposed = big[name]
            if transposed:
                summed = _sum_parts(p, "sum_" + name, _row_tile(p.shape[1], ADAM_ROWS))
                p = summed.T[None]
            g, dl, nm, nv = _adamw(p, w, m_, v_, "adamw_" + name, _row_tile(w.shape[0], ADAM_ROWS))
            res[name] = tuple(a[None] for a in (g, dl, nm, nv))
            last = g
        return last

    parts2 = _exchange_wait(handles2, True, "grads_wait_ffn2", token1)
    done2 = update(["ffn2_w_gate", "ffn2_w_up", "ffn2_w_down"], parts2)
    parts_mix = _exchange_wait(handles_mix, True, "grads_wait_mix", done2)
    done_mix = update(["w_in", "w_out"], parts_mix)
    parts1 = _exchange_wait(handles1, True, "grads_wait_ffn1", done_mix)
    update(names1, parts1)

    def pack(ffn1, mixn, ffn2, fin, attn_n_, conv_n_, sink_, extra, convw, table):
        rows = [_pad_row(ffn1), _pad_row(mixn), _pad_row(ffn2), _pad_row(fin),
                _pad_row(jnp.concatenate([attn_n_.reshape(-1), conv_n_.reshape(-1)])),
                _pad_row(sink_), _pad_row(extra),
                jnp.zeros((1, PACK_COLS), F32),
                jnp.pad(convw, ((0, 0), (0, PACK_COLS - convw.shape[1]))),
                _pad_row(table),
                jnp.zeros((PACK_ROWS - 12, PACK_COLS), F32)]
        return jnp.concatenate(rows, axis=0)

    def own_channels(a):
        full = jnp.zeros((a.shape[1], CONV_DIM), F32)
        return lax.dynamic_update_slice(full, a[0], (0, me * a.shape[2]))

    g_pack = pack(d_ffn1_norm, d_mix_norm, d_ffn2_norm, d_final, d_attn_norm, d_conv_norm,
                  dsink[0, :N_Q_HEADS], loss_part[0, :1], d_cw[:3], d_table[:, :N_Q_HEADS])
    zero1 = jnp.zeros((1,), F32)
    w_pack = pack(ffn1_norm, mix_norm, ffn2_norm, final_norm, attn_out_norm, conv_out_norm,
                  attn_sinks, zero1, own_channels(conv_w), rel_bias_table)
    m_pack = pack(m_ffn1_norm, m_mix_norm, m_ffn2_norm, m_final_norm, m_attn_out_norm, m_conv_out_norm,
                  m_attn_sinks, zero1, own_channels(m_conv_w), m_rel_bias_table)
    v_pack = pack(v_ffn1_norm, v_mix_norm, v_ffn2_norm, v_final_norm, v_attn_out_norm, v_conv_out_norm,
                  v_attn_sinks, zero1, own_channels(v_conv_w), v_rel_bias_table)
    (g_all,) = _exchange([g_pack], False, "gather_small")
    packs = _adamw(g_all, w_pack, m_pack, v_pack, "adamw_small", PACK_ROWS)

    def unpack(pk):
        cwb = lax.dynamic_slice(pk[8:11, :CONV_DIM], (0, me * conv_w.shape[2]), (3, conv_w.shape[2]))
        return {
            "ffn1_norm": pk[0:1, :D], "mix_norm": pk[1:2, :D], "ffn2_norm": pk[2:3, :D],
            "final_norm": pk[3, :D],
            "attn_out_norm": pk[4:5, :ATTN_WIDTH], "conv_out_norm": pk[4:5, ATTN_WIDTH:ATTN_WIDTH + CONV_DIM],
            "attn_sinks": pk[5:6, :N_Q_HEADS],
            "conv_w": cwb[None],
            "rel_bias_table": pk[11, :NUM_BUCKETS * N_Q_HEADS].reshape(NUM_BUCKETS, N_Q_HEADS),
        }

    small = [unpack(pk) for pk in packs]
    loss = packs[0][6, 0]

    order = ["rel_bias_table", "ffn1_norm", "ffn1_w_gate", "ffn1_w_up", "ffn1_w_down", "mix_norm", "w_in",
             "conv_w", "attn_sinks", "attn_out_norm", "conv_out_norm", "w_out", "ffn2_norm",
             "ffn2_w_gate", "ffn2_w_up", "ffn2_w_down", "final_norm"]
    outs = [loss, dx0[None]]
    for kind in range(4):
        for name in order:
            outs.append(res[name][kind] if name in res else small[kind][name])
    return tuple(outs)
```

```python
import math

import numpy as np
import jax
import jax.numpy as jnp
from jax import lax
from jax.experimental import pallas as pl
from jax.experimental.pallas import tpu as pltpu

F32 = jnp.float32
BF16 = jnp.bfloat16

N_DEV = 8
EPS = 1e-6
HEAD_DIM = 64
N_Q_HEADS = 8
N_KV_HEADS = 2
GQA_GROUP = 4
ATTN_WIDTH = 512
KV_WIDTH = 128
CONV_DIM = 512
BLOCK = 128
WINDOW = 128
NUM_BUCKETS = 32
MAX_DISTANCE = 128
SCALE = HEAD_DIM ** -0.5
MASKED = -1e30
GROUP_ROWS = GQA_GROUP * BLOCK

ADAM_LR = 0.001
ADAM_B1 = 0.9
ADAM_B2 = 0.999
ADAM_EPS = 1e-08
ADAM_WD = 0.01
ADAM_STEP = 10

VMEM_LIMIT_BYTES = 40 * 1024 * 1024
SUBLANES = 8
PACK_ROWS = 16
PACK_COLS = 1024

TM_FFN = 1024
TM_FFN_BWD = 1024
TM_MIX = 512
TK_WGRAD = 1024
TF_FFN = 256
BM_WGRAD = 1408
ATTN_BLOCKS = 4
ADAM_ROWS = 256


def _row_tile(rows, limit):
    best = rows
    for t in range(16, min(rows, limit) + 1, 16):
        if rows % t == 0:
            best = t
    return best


def _params(*sem):
    return pltpu.CompilerParams(dimension_semantics=sem, vmem_limit_bytes=VMEM_LIMIT_BYTES)


def _dot(a, b):
    return jnp.dot(a, b, preferred_element_type=F32)


def _dot_nt(a, b):
    return lax.dot_general(a, b, (((1,), (1,)), ((), ())), preferred_element_type=F32)


def _dot_tn(a, b):
    return lax.dot_general(a, b, (((0,), (0,)), ((), ())), preferred_element_type=F32)


def _sigmoid(g):
    return 1.0 / (1.0 + jnp.exp(-g))


def _rms_stats(x):
    inv = lax.rsqrt(jnp.mean(x * x, axis=-1, keepdims=True) + EPS)
    return inv, x * inv


def _rms_bwd(dy, x, gain):
    inv, xhat = _rms_stats(x)
    dgain = jnp.sum(dy * xhat, axis=0, keepdims=True)
    dxh = dy * gain
    dx = inv * (dxh - xhat * jnp.mean(dxh * xhat, axis=-1, keepdims=True))
    return dx, dgain


def _peer_list():
    x, y, c = lax.axis_index("x"), lax.axis_index("y"), lax.axis_index("c")
    peers = []
    for k in range(1, N_DEV):
        px = 1 - x if (k >> 2) & 1 else x
        py = 1 - y if (k >> 1) & 1 else y
        pc = 1 - c if k & 1 else c
        peers.append((px, py, pc))
    return 4 * x + 2 * y + c, peers


def _exchange(arrs, scatter, name):
    n = len(arrs)
    out_shape = []
    for a in arrs:
        shp = a.shape if scatter else (N_DEV,) + a.shape
        out_shape.append(jax.ShapeDtypeStruct(shp, a.dtype))

    def body(*refs):
        ins, outs = refs[:n], refs[n:2 * n]
        send_sems, recv_sems, local_sems = refs[2 * n:]
        me, peers = _peer_list()
        started = []
        for a in range(n):
            own = ins[a].at[me] if scatter else ins[a]
            loc = pltpu.make_async_copy(own, outs[a].at[me], local_sems.at[a])
            loc.start()
            started.append(loc)
        sends = []
        for a in range(n):
            for k, (px, py, pc) in enumerate(peers):
                src = ins[a].at[4 * px + 2 * py + pc] if scatter else ins[a]
                cp = pltpu.make_async_remote_copy(
                    src_ref=src, dst_ref=outs[a].at[me],
                    send_sem=send_sems.at[a, k], recv_sem=recv_sems.at[a, k],
                    device_id=(px, py, pc), device_id_type=pl.DeviceIdType.MESH)
                cp.start()
                sends.append(cp)
        for a in range(n):
            for k, (px, py, pc) in enumerate(peers):
                landed = outs[a].at[4 * px + 2 * py + pc]
                pltpu.make_async_remote_copy(
                    src_ref=landed, dst_ref=landed,
                    send_sem=send_sems.at[a, k], recv_sem=recv_sems.at[a, k],
                    device_id=(px, py, pc), device_id_type=pl.DeviceIdType.MESH).wait_recv()
        for cp in sends:
            cp.wait_send()
        for loc in started:
            loc.wait()

    hbm = pl.BlockSpec(memory_space=pl.ANY)
    return pl.pallas_call(
        body, name=name, out_shape=tuple(out_shape),
        in_specs=[hbm] * n, out_specs=tuple([hbm] * n),
        scratch_shapes=[pltpu.SemaphoreType.DMA((n, N_DEV - 1)),
                        pltpu.SemaphoreType.DMA((n, N_DEV - 1)),
                        pltpu.SemaphoreType.DMA((n,))],
    )(*arrs)


_HBM = pl.BlockSpec(memory_space=pltpu.HBM)
_SEM = pl.BlockSpec(memory_space=pltpu.SEMAPHORE)
_EFFECT = pltpu.SideEffectType.DATAFLOW_SIDE_EFFECTING


def _split_copies(srcs, lands, send_sems, recv_sems, scatter):
    me, peers = _peer_list()
    copies = []
    for a in range(len(srcs)):
        for k, (px, py, pc) in enumerate(peers):
            src = srcs[a].at[4 * px + 2 * py + pc] if scatter else srcs[a]
            copies.append(pltpu.make_async_remote_copy(
                src_ref=src, dst_ref=lands[a].at[me],
                send_sem=send_sems[a].at[k], recv_sem=recv_sems[a].at[k],
                device_id=(px, py, pc), device_id_type=pl.DeviceIdType.MESH))
    return copies


def _exchange_start(arrs, scatter, name, after):
    n = len(arrs)
    me = 4 * lax.axis_index("x") + 2 * lax.axis_index("y") + lax.axis_index("c")
    lands = []
    for a in arrs:
        own = lax.dynamic_index_in_dim(a, me, 0, keepdims=True) if scatter else a[None]
        shp = a.shape if scatter else (N_DEV,) + a.shape
        lands.append(lax.dynamic_update_slice(lax.empty(shp, a.dtype), own, (me,) + (0,) * (len(shp) - 1)))

    def body(*refs):
        srcs, lnds = refs[:n], refs[n:2 * n]
        outs = refs[2 * n + 1:]
        send_sems, recv_sems = outs[:n], outs[n:2 * n]
        token = outs[4 * n]
        for cp in _split_copies(srcs, lnds, send_sems, recv_sems, scatter):
            cp.start()
        token[...] = jnp.zeros_like(token)

    sem = pltpu.SemaphoreType.DMA((N_DEV - 1,))
    out_shape = ([sem] * (2 * n) + [pltpu.HBM(a.shape, a.dtype) for a in arrs]
                 + [pltpu.HBM(l.shape, l.dtype) for l in lands] + [jax.ShapeDtypeStruct((SUBLANES, 128), F32)])
    res = pl.pallas_call(
        body, name=name, out_shape=tuple(out_shape),
        in_specs=[_HBM] * (2 * n) + [pl.BlockSpec(memory_space=pl.ANY)],
        out_specs=tuple([_SEM] * (2 * n) + [_HBM] * (2 * n) + [pl.BlockSpec(memory_space=pltpu.VMEM)]),
        input_output_aliases={i: 2 * n + i for i in range(2 * n)},
        compiler_params=pltpu.CompilerParams(has_side_effects=_EFFECT),
    )(*[pltpu.with_memory_space_constraint(a, pltpu.HBM) for a in arrs],
      *[pltpu.with_memory_space_constraint(l, pltpu.HBM) for l in lands], after)
    handles = [(res[2 * n + a], res[3 * n + a], res[a], res[n + a]) for a in range(n)]
    return handles, res[4 * n]


def _exchange_wait(handles, scatter, name, after):
    n = len(handles)

    def body(*refs):
        srcs, lnds = refs[:n], refs[n:2 * n]
        send_sems, recv_sems = refs[2 * n:3 * n], refs[3 * n:4 * n]
        for cp in _split_copies(srcs, lnds, send_sems, recv_sems, scatter):
            cp.wait_send()
            cp.wait_recv()

    srcs = [h[0] for h in handles]
    lands = [h[1] for h in handles]
    res = pl.pallas_call(
        body, name=name,
        out_shape=tuple([pltpu.HBM(a.shape, a.dtype) for a in srcs] + [pltpu.HBM(l.shape, l.dtype) for l in lands]),
        in_specs=[_HBM] * (2 * n) + [_SEM] * (2 * n) + [pl.BlockSpec(memory_space=pl.ANY)],
        out_specs=tuple([_HBM] * (2 * n)),
        input_output_aliases={i: i for i in range(2 * n)},
        compiler_params=pltpu.CompilerParams(has_side_effects=_EFFECT),
    )(*srcs, *lands, *[h[2] for h in handles], *[h[3] for h in handles], after)
    return list(res[n:])


def _ffn_fwd(x, gain, wgt, wut, wd, name, tm, tf):
    T, D = x.shape
    F = wd.shape[0]
    nj = F // tf

    def body(x_ref, g_ref, wgt_ref, wut_ref, wd_ref, xo_ref, xn_ref, gate_ref, up_ref, xn_sc, acc_sc):
        j = pl.program_id(1)

        @pl.when(j == 0)
        def _():
            _, xhat = _rms_stats(x_ref[...])
            xn = (xhat * g_ref[...]).astype(BF16)
            xn_sc[...] = xn
            xn_ref[...] = xn
            acc_sc[...] = jnp.zeros_like(acc_sc)

        xn = xn_sc[...]
        g = _dot_nt(xn, wgt_ref[...])
        u = _dot_nt(xn, wut_ref[...])
        gate_ref[...] = g.astype(BF16)
        up_ref[...] = u.astype(BF16)
        h = (g * _sigmoid(g) * u).astype(BF16)
        acc_sc[...] += _dot(h, wd_ref[...])

        @pl.when(j == nj - 1)
        def _():
            xo_ref[...] = x_ref[...] + 0.5 * acc_sc[...]

    wspec = pl.BlockSpec((tf, D), lambda i, j: (j, 0))
    return pl.pallas_call(
        body, name=name, grid=(T // tm, nj),
        in_specs=[pl.BlockSpec((tm, D), lambda i, j: (i, 0)),
                  pl.BlockSpec((1, D), lambda i, j: (0, 0)),
                  wspec, wspec, wspec],
        out_specs=(pl.BlockSpec((tm, D), lambda i, j: (i, 0)),
                   pl.BlockSpec((tm, D), lambda i, j: (i, 0)),
                   pl.BlockSpec((tm, tf), lambda i, j: (i, j)),
                   pl.BlockSpec((tm, tf), lambda i, j: (i, j))),
        out_shape=(jax.ShapeDtypeStruct((T, D), F32), jax.ShapeDtypeStruct((T, D), BF16),
                   jax.ShapeDtypeStruct((T, F), BF16), jax.ShapeDtypeStruct((T, F), BF16)),
        scratch_shapes=[pltpu.VMEM((tm, D), BF16), pltpu.VMEM((tm, D), F32)],
        compiler_params=_params("arbitrary", "arbitrary"),
    )(x, gain, wgt, wut, wd)


def _ffn_bwd(dy, x, gain, gate, up, wgt, wut, wd, after, name, tm, tf):
    T, D = x.shape
    F = gate.shape[1]
    nj = F // tf

    def body(dy_ref, x_ref, g_ref, gate_ref, up_ref, wgt_ref, wut_ref, wd_ref, after_ref,
             dx_ref, dgain_ref, dg_ref, du_ref, hh_ref, do_ref, do_sc, acc_sc):
        i, j = pl.program_id(0), pl.program_id(1)

        @pl.when((i == 0) & (j == 0))
        def _():
            dgain_ref[...] = jnp.zeros_like(dgain_ref)

        @pl.when(j == 0)
        def _():
            do = (0.5 * dy_ref[...]).astype(BF16)
            do_sc[...] = do
            do_ref[...] = do
            acc_sc[...] = jnp.zeros_like(acc_sc)

        dh = _dot_nt(do_sc[...], wd_ref[...])
        g = gate_ref[...].astype(F32)
        u = up_ref[...].astype(F32)
        sig = _sigmoid(g)
        s = g * sig
        dg = (dh * u * (sig * (1.0 + g * (1.0 - sig)))).astype(BF16)
        du = (dh * s).astype(BF16)
        dg_ref[...] = dg
        du_ref[...] = du
        hh_ref[...] = (s * u).astype(BF16)
        acc_sc[...] += _dot(dg, wgt_ref[...]) + _dot(du, wut_ref[...])

        @pl.when(j == nj - 1)
        def _():
            dx, dgain = _rms_bwd(acc_sc[...], x_ref[...], g_ref[...])
            dgain_ref[...] += dgain
            dx_ref[...] = dy_ref[...] + dx

    wspec = pl.BlockSpec((tf, D), lambda i, j: (j, 0))
    tile_spec = pl.BlockSpec((tm, D), lambda i, j: (i, 0), pipeline_mode=pl.Buffered(1))
    return pl.pallas_call(
        body, name=name, grid=(T // tm, nj),
        in_specs=[tile_spec, tile_spec,
                  pl.BlockSpec((1, D), lambda i, j: (0, 0)),
                  pl.BlockSpec((tm, tf), lambda i, j: (i, j)),
                  pl.BlockSpec((tm, tf), lambda i, j: (i, j)),
                  wspec, wspec, wspec,
                  pl.BlockSpec(memory_space=pl.ANY)],
        out_specs=(tile_spec,
                   pl.BlockSpec((1, D), lambda i, j: (0, 0)),
                   pl.BlockSpec((tm, tf), lambda i, j: (i, j)),
                   pl.BlockSpec((tm, tf), lambda i, j: (i, j)),
                   pl.BlockSpec((tm, tf), lambda i, j: (i, j)),
                   pl.BlockSpec((tm, D), lambda i, j: (i, 0))),
        out_shape=(jax.ShapeDtypeStruct((T, D), F32), jax.ShapeDtypeStruct((1, D), F32),
                   jax.ShapeDtypeStruct((T, F), BF16), jax.ShapeDtypeStruct((T, F), BF16),
                   jax.ShapeDtypeStruct((T, F), BF16), jax.ShapeDtypeStruct((T, D), BF16)),
        scratch_shapes=[pltpu.VMEM((tm, D), BF16), pltpu.VMEM((tm, D), F32)],
        compiler_params=_params("arbitrary", "arbitrary"),
    )(dy, x, gain, gate, up, wgt, wut, wd, after)


def _tn_grad(a, b, name, bm, tk):
    T, M = a.shape
    D = b.shape[1]
    nk = T // tk

    def body(a_ref, b_ref, o_ref, acc_sc):
        k = pl.program_id(1)

        @pl.when(k == 0)
        def _():
            acc_sc[...] = jnp.zeros_like(acc_sc)

        acc_sc[...] += _dot_tn(a_ref[...], b_ref[...])

        @pl.when(k == nk - 1)
        def _():
            o_ref[...] = acc_sc[...].astype(BF16)

    return pl.pallas_call(
        body, name=name, grid=(M // bm, nk),
        in_specs=[pl.BlockSpec((tk, bm), lambda i, k: (k, i)), pl.BlockSpec((tk, D), lambda i, k: (k, 0))],
        out_specs=pl.BlockSpec((bm, D), lambda i, k: (i, 0)),
        out_shape=jax.ShapeDtypeStruct((M, D), BF16),
        scratch_shapes=[pltpu.VMEM((bm, D), F32)],
        compiler_params=_params("arbitrary", "arbitrary"),
    )(a, b)


def _ffn_wgrad(dg, du, hh, xn, do, name, bm, tk):
    return [_tn_grad(dg, xn, name + "_gate", bm, tk), _tn_grad(du, xn, name + "_up", bm, tk),
            _tn_grad(hh, do, name + "_down", bm, tk)]


_Z_SPLITS = (0, 512, 640, 768, 1280, 1792, 2304)


def _mixin_fwd(x, gain, w_in_t, name, tm):
    T, D = x.shape
    widths = [b - a for a, b in zip(_Z_SPLITS[:-1], _Z_SPLITS[1:])]

    def body(x_ref, g_ref, w_ref, hn_ref, *outs):
        _, xhat = _rms_stats(x_ref[...])
        hn = (xhat * g_ref[...]).astype(BF16)
        hn_ref[...] = hn
        for o_ref, lo, hi in zip(outs, _Z_SPLITS[:-1], _Z_SPLITS[1:]):
            o_ref[...] = _dot_nt(hn, w_ref[lo:hi, :]).astype(BF16)

    return pl.pallas_call(
        body, name=name, grid=(T // tm,),
        in_specs=[pl.BlockSpec((tm, D), lambda i: (i, 0)),
                  pl.BlockSpec((1, D), lambda i: (0, 0)),
                  pl.BlockSpec(w_in_t.shape, lambda i: (0, 0))],
        out_specs=tuple([pl.BlockSpec((tm, D), lambda i: (i, 0))]
                        + [pl.BlockSpec((tm, w), lambda i: (i, 0)) for w in widths]),
        out_shape=tuple([jax.ShapeDtypeStruct((T, D), BF16)]
                        + [jax.ShapeDtypeStruct((T, w), BF16) for w in widths]),
        compiler_params=_params("arbitrary"),
    )(x, gain, w_in_t)


def _bucket_table():
    qi = np.arange(BLOCK, dtype=np.int32)[:, None]
    kj = np.arange(2 * BLOCK, dtype=np.int32)[None, :]
    dist = qi + BLOCK - kj
    n = np.maximum(dist, 0)
    max_exact = NUM_BUCKETS // 2
    large = max_exact + (np.log(np.maximum(n, 1).astype(np.float32) / max_exact)
                         / math.log(MAX_DISTANCE / max_exact)
                         * (NUM_BUCKETS - max_exact)).astype(np.int32)
    large = np.minimum(large, NUM_BUCKETS - 1)
    bucket = np.where(n < max_exact, n, large).astype(np.int32)
    valid = (dist >= 0) & (dist < WINDOW)
    return np.where(valid, bucket, -1).astype(np.int32)


def _bias_build(table, bucket, name):
    def body(t_ref, b_ref, o_ref):
        bk = b_ref[...]
        for h in range(N_Q_HEADS):
            def step(b, acc):
                return jnp.where(bk == b, t_ref[b, h], acc)
            o_ref[h] = lax.fori_loop(0, NUM_BUCKETS, step, jnp.full(bk.shape, MASKED, F32))

    return pl.pallas_call(
        body, name=name,
        in_specs=[pl.BlockSpec(memory_space=pltpu.SMEM), pl.BlockSpec(memory_space=pltpu.VMEM)],
        out_specs=pl.BlockSpec(memory_space=pltpu.VMEM),
        out_shape=jax.ShapeDtypeStruct((N_Q_HEADS,) + bucket.shape, F32),
    )(table, bucket)


def _bias_grad(dbias, bucket, name):
    def body(d_ref, b_ref, o_ref):
        bk = b_ref[...]
        row = lax.broadcasted_iota(jnp.int32, o_ref.shape, 0)
        lane = lax.broadcasted_iota(jnp.int32, o_ref.shape, 1)
        res = jnp.zeros(o_ref.shape, F32)
        for h in range(N_Q_HEADS):
            d = d_ref[h]

            def step(b, acc):
                tot = jnp.sum(jnp.where(bk == b, d, 0.0), axis=1, keepdims=True)
                tot = jnp.sum(tot, axis=0, keepdims=True)
                return jnp.where((row == b) & (lane == h), tot, acc)
            res = lax.fori_loop(0, NUM_BUCKETS, step, res)
        o_ref[...] = res

    return pl.pallas_call(
        body, name=name,
        in_specs=[pl.BlockSpec(memory_space=pltpu.VMEM), pl.BlockSpec(memory_space=pltpu.VMEM)],
        out_specs=pl.BlockSpec(memory_space=pltpu.VMEM),
        out_shape=jax.ShapeDtypeStruct((NUM_BUCKETS, 128), F32),
    )(dbias, bucket)


def _head_cols(h):
    return slice(h * HEAD_DIM, (h + 1) * HEAD_DIM)


def _stack_heads(ref, r0, g, dtype):
    return jnp.concatenate(
        [ref[pl.ds(r0, BLOCK), _head_cols(GQA_GROUP * g + j)].astype(dtype) for j in range(GQA_GROUP)], axis=0)


def _unstack_heads(ref, r0, g, val):
    for j in range(GQA_GROUP):
        ref[pl.ds(r0, BLOCK), _head_cols(GQA_GROUP * g + j)] = val[j * BLOCK:(j + 1) * BLOCK, :]


def _head_lanes(h):
    return slice(h * BLOCK, (h + 1) * BLOCK)


def _group_lanes(g):
    return slice(g * GROUP_ROWS, (g + 1) * GROUP_ROWS)


def _head_softmax(st, bias_t, sink, no_prev):
    s = st * SCALE + bias_t
    row = lax.broadcasted_iota(jnp.int32, s.shape, 0)
    s = jnp.where(no_prev & (row < BLOCK), MASKED, s)
    m = jnp.maximum(jnp.max(s, axis=0, keepdims=True), sink)
    p = jnp.exp(s - m)
    ps = jnp.exp(sink - m)
    r = 1.0 / (jnp.sum(p, axis=0, keepdims=True) + ps)
    return p * r, ps * r


def _load_band(kf_sc, vf_sc, kp_ref, kc_ref, vp_ref, vc_ref, tq):
    kf_sc[0:BLOCK, :] = kp_ref[...]
    kf_sc[BLOCK:BLOCK + tq, :] = kc_ref[...]
    vf_sc[0:BLOCK, :] = vp_ref[...]
    vf_sc[BLOCK:BLOCK + tq, :] = vc_ref[...]


def _attn_fwd(q, k, v, bias_t, sinks, gain, name, nblk):
    T = q.shape[0]
    tq = nblk * BLOCK

    def body(sink_ref, q_ref, kc_ref, kp_ref, vc_ref, vp_ref, bias_ref, g_ref, raw_ref, nrm_ref,
             kf_sc, vf_sc, o_sc, st_sc, pt_sc):
        i = pl.program_id(0)
        _load_band(kf_sc, vf_sc, kp_ref, kc_ref, vp_ref, vc_ref, tq)

        def block(b, carry):
            r0 = pl.multiple_of(b * BLOCK, BLOCK)
            no_prev = (i == 0) & (b == 0)
            for g in range(N_KV_HEADS):
                kb = kf_sc[pl.ds(r0, 2 * BLOCK), _head_cols(g)]
                st_sc[:, _group_lanes(g)] = _dot_nt(kb, _stack_heads(q_ref, r0, g, BF16))
            for h in range(N_Q_HEADS):
                p, _ = _head_softmax(st_sc[:, _head_lanes(h)], bias_ref[h], sink_ref[h], no_prev)
                pt_sc[:, _head_lanes(h)] = p.astype(BF16)
            for g in range(N_KV_HEADS):
                vb = vf_sc[pl.ds(r0, 2 * BLOCK), _head_cols(g)]
                _unstack_heads(o_sc, r0, g, _dot_tn(pt_sc[:, _group_lanes(g)], vb))
            return carry

        lax.fori_loop(0, nblk, block, 0)
        o = o_sc[...]
        raw_ref[...] = o.astype(BF16)
        _, ohat = _rms_stats(o)
        nrm_ref[...] = (ohat * g_ref[...]).astype(BF16)

    cur = lambda i: (i, 0)
    prev = lambda i: (jnp.maximum(i * nblk - 1, 0), 0)
    lanes = N_Q_HEADS * BLOCK
    return pl.pallas_call(
        body, name=name, grid=(T // tq,),
        in_specs=[pl.BlockSpec(memory_space=pltpu.SMEM),
                  pl.BlockSpec((tq, ATTN_WIDTH), cur),
                  pl.BlockSpec((tq, KV_WIDTH), cur), pl.BlockSpec((BLOCK, KV_WIDTH), prev),
                  pl.BlockSpec((tq, KV_WIDTH), cur), pl.BlockSpec((BLOCK, KV_WIDTH), prev),
                  pl.BlockSpec(bias_t.shape, lambda i: (0, 0, 0)),
                  pl.BlockSpec((1, ATTN_WIDTH), lambda i: (0, 0))],
        out_specs=(pl.BlockSpec((tq, ATTN_WIDTH), cur), pl.BlockSpec((tq, ATTN_WIDTH), cur)),
        out_shape=(jax.ShapeDtypeStruct((T, ATTN_WIDTH), BF16), jax.ShapeDtypeStruct((T, ATTN_WIDTH), BF16)),
        scratch_shapes=[pltpu.VMEM((tq + BLOCK, KV_WIDTH), BF16), pltpu.VMEM((tq + BLOCK, KV_WIDTH), BF16),
                        pltpu.VMEM((tq, ATTN_WIDTH), F32),
                        pltpu.VMEM((2 * BLOCK, lanes), F32), pltpu.VMEM((2 * BLOCK, lanes), BF16)],
        compiler_params=_params("arbitrary"),
    )(sinks, q, k, k, v, v, bias_t, gain)


def _attn_bwd(dmixed, raw, q, k, v, bias_t, sinks, gain, name, nblk):
    T = q.shape[0]
    tq = nblk * BLOCK
    nt = T // tq
    lanes = N_Q_HEADS * BLOCK

    def body(sink_ref, dm_ref, raw_ref, q_ref, kc_ref, kp_ref, vc_ref, vp_ref, bias_ref, g_ref,
             dq_ref, dk_ref, dv_ref, dbias_ref, dsink_ref, dgain_ref,
             do_sc, dq_sc, kf_sc, vf_sc, dkf_sc, dvf_sc, st_sc, dpt_sc, pt_sc, dst_sc, drow_sc,
             qs_sc, dos_sc, dsink_sc):
        i = pl.program_id(0)
        tile = nt - 1 - i

        @pl.when(i == 0)
        def _():
            dkf_sc[...] = jnp.zeros_like(dkf_sc)
            dvf_sc[...] = jnp.zeros_like(dvf_sc)
            dsink_sc[...] = jnp.zeros_like(dsink_sc)
            dbias_ref[...] = jnp.zeros_like(dbias_ref)
            dgain_ref[...] = jnp.zeros_like(dgain_ref)

        carry_k = dkf_sc[0:BLOCK, :]
        carry_v = dvf_sc[0:BLOCK, :]
        dkf_sc[0:tq, :] = jnp.zeros((tq, KV_WIDTH), F32)
        dvf_sc[0:tq, :] = jnp.zeros((tq, KV_WIDTH), F32)
        dkf_sc[tq:tq + BLOCK, :] = carry_k
        dvf_sc[tq:tq + BLOCK, :] = carry_v
        _load_band(kf_sc, vf_sc, kp_ref, kc_ref, vp_ref, vc_ref, tq)

        do, dgain = _rms_bwd(dm_ref[...].astype(F32), raw_ref[...].astype(F32), g_ref[...])
        dgain_ref[...] += dgain
        do_sc[...] = do
        ones = jnp.ones((SUBLANES, HEAD_DIM), BF16)

        def block(b, carry):
            r0 = pl.multiple_of(b * BLOCK, BLOCK)
            no_prev = (tile == 0) & (b == 0)
            for g in range(N_KV_HEADS):
                kb = kf_sc[pl.ds(r0, 2 * BLOCK), _head_cols(g)]
                vb = vf_sc[pl.ds(r0, 2 * BLOCK), _head_cols(g)]
                qg = _stack_heads(q_ref, r0, g, BF16)
                dog = _stack_heads(do_sc, r0, g, F32)
                prod = dog * _stack_heads(raw_ref, r0, g, F32)
                hi = prod.astype(BF16)
                lo = (prod - hi.astype(F32)).astype(BF16)
                drow_sc[:, _group_lanes(g)] = _dot_nt(ones, hi) + _dot_nt(ones, lo)
                dogb = dog.astype(BF16)
                qs_sc[g] = qg
                dos_sc[g] = dogb
                st_sc[:, _group_lanes(g)] = _dot_nt(kb, qg)
                dpt_sc[:, _group_lanes(g)] = _dot_nt(vb, dogb)
            for h in range(N_Q_HEADS):
                hl = _head_lanes(h)
                p, ps = _head_softmax(st_sc[:, hl], bias_ref[h], sink_ref[h], no_prev)
                rowdot = drow_sc[0:1, hl]
                ds = p * (dpt_sc[:, hl] - rowdot)
                dsink_sc[h:h + 1, :] += -(ps * rowdot)
                dbias_ref[h] += ds
                dst_sc[:, hl] = ds.astype(BF16)
                pt_sc[:, hl] = p.astype(BF16)
            for g in range(N_KV_HEADS):
                kb = kf_sc[pl.ds(r0, 2 * BLOCK), _head_cols(g)]
                dsg = dst_sc[:, _group_lanes(g)]
                _unstack_heads(dq_sc, r0, g, _dot_tn(dsg, kb) * SCALE)
                dkf_sc[pl.ds(r0, 2 * BLOCK), _head_cols(g)] += _dot(dsg, qs_sc[g]) * SCALE
                dvf_sc[pl.ds(r0, 2 * BLOCK), _head_cols(g)] += _dot(pt_sc[:, _group_lanes(g)], dos_sc[g])
            return carry

        lax.fori_loop(0, nblk, block, 0)
        dq_ref[...] = dq_sc[...].astype(BF16)
        dk_ref[...] = dkf_sc[BLOCK:BLOCK + tq, :].astype(BF16)
        dv_ref[...] = dvf_sc[BLOCK:BLOCK + tq, :].astype(BF16)

        @pl.when(i == nt - 1)
        def _():
            tot = jnp.sum(dsink_sc[...], axis=1, keepdims=True)
            dsink_ref[...] = jnp.broadcast_to(tot, dsink_ref.shape)

    cur = lambda i: (nt - 1 - i, 0)
    prev = lambda i: (jnp.maximum((nt - 1 - i) * nblk - 1, 0), 0)
    const2 = lambda i: (0, 0)
    const3 = lambda i: (0, 0, 0)
    return pl.pallas_call(
        body, name=name, grid=(nt,),
        in_specs=[pl.BlockSpec(memory_space=pltpu.SMEM),
                  pl.BlockSpec((tq, ATTN_WIDTH), cur),
                  pl.BlockSpec((tq, ATTN_WIDTH), cur),
                  pl.BlockSpec((tq, ATTN_WIDTH), cur),
                  pl.BlockSpec((tq, KV_WIDTH), cur), pl.BlockSpec((BLOCK, KV_WIDTH), prev),
                  pl.BlockSpec((tq, KV_WIDTH), cur), pl.BlockSpec((BLOCK, KV_WIDTH), prev),
                  pl.BlockSpec(bias_t.shape, const3),
                  pl.BlockSpec((1, ATTN_WIDTH), const2)],
        out_specs=(pl.BlockSpec((tq, ATTN_WIDTH), cur),
                   pl.BlockSpec((tq, KV_WIDTH), cur), pl.BlockSpec((tq, KV_WIDTH), cur),
                   pl.BlockSpec(bias_t.shape, const3),
                   pl.BlockSpec((N_Q_HEADS, 128), const2),
                   pl.BlockSpec((1, ATTN_WIDTH), const2)),
        out_shape=(jax.ShapeDtypeStruct((T, ATTN_WIDTH), BF16),
                   jax.ShapeDtypeStruct((T, KV_WIDTH), BF16), jax.ShapeDtypeStruct((T, KV_WIDTH), BF16),
                   jax.ShapeDtypeStruct(bias_t.shape, F32),
                   jax.ShapeDtypeStruct((N_Q_HEADS, 128), F32),
                   jax.ShapeDtypeStruct((1, ATTN_WIDTH), F32)),
        scratch_shapes=[pltpu.VMEM((tq, ATTN_WIDTH), F32), pltpu.VMEM((tq, ATTN_WIDTH), F32),
                        pltpu.VMEM((tq + BLOCK, KV_WIDTH), BF16), pltpu.VMEM((tq + BLOCK, KV_WIDTH), BF16),
                        pltpu.VMEM((tq + BLOCK, KV_WIDTH), F32), pltpu.VMEM((tq + BLOCK, KV_WIDTH), F32),
                        pltpu.VMEM((2 * BLOCK, lanes), F32), pltpu.VMEM((2 * BLOCK, lanes), F32),
                        pltpu.VMEM((2 * BLOCK, lanes), BF16), pltpu.VMEM((2 * BLOCK, lanes), BF16),
                        pltpu.VMEM((SUBLANES, lanes), F32),
                        pltpu.VMEM((N_KV_HEADS, GROUP_ROWS, HEAD_DIM), BF16),
                        pltpu.VMEM((N_KV_HEADS, GROUP_ROWS, HEAD_DIM), BF16),
                        pltpu.VMEM((N_Q_HEADS, 128), F32)],
        compiler_params=_params("arbitrary"),
    )(sinks, dmixed, raw, q, k, k, v, v, bias_t, gain)


def _shift_down(cu, tail):
    row = lax.broadcasted_iota(jnp.int32, cu.shape, 0)
    t6, t7 = tail[6:7, :], tail[7:8, :]
    s1 = jnp.where(row == 0, t7, pltpu.roll(cu, 1, 0))
    s2 = jnp.where(row == 0, t6, jnp.where(row == 1, t7, pltpu.roll(cu, 2, 0)))
    return s1, s2


def _shift_up(d, head):
    n = d.shape[0]
    row = lax.broadcasted_iota(jnp.int32, d.shape, 0)
    h0, h1 = head[0:1, :], head[1:2, :]
    s1 = jnp.where(row == n - 1, h0, pltpu.roll(d, n - 1, 0))
    s2 = jnp.where(row == n - 1, h1, jnp.where(row == n - 2, h0, pltpu.roll(d, n - 2, 0)))
    return s1, s2


def _mixout_fwd(x, attn_n, u, gb, gc, conv_w, gain, w_out, name, tm):
    T, D = x.shape

    def body(x_ref, an_ref, u_ref, b_ref, c_ref, cw_ref, g_ref, wo_ref, xo_ref, cn_ref, tail_sc):
        @pl.when(pl.program_id(0) == 0)
        def _():
            tail_sc[...] = jnp.zeros_like(tail_sc)

        cu = c_ref[...].astype(F32) * u_ref[...].astype(F32)
        s1, s2 = _shift_down(cu, tail_sc[...])
        tail_sc[...] = cu[tm - SUBLANES:tm, :]
        pre = cw_ref[0:1, :] * s2 + cw_ref[1:2, :] * s1 + cw_ref[2:3, :] * cu
        conv = b_ref[...].astype(F32) * pre
        _, chat = _rms_stats(conv)
        cn = (chat * g_ref[...]).astype(BF16)
        cn_ref[...] = cn
        xo_ref[...] = (x_ref[...] + _dot(an_ref[...], wo_ref[0:ATTN_WIDTH, :])
                       + _dot(cn, wo_ref[ATTN_WIDTH:ATTN_WIDTH + CONV_DIM, :]))

    row = lambda i: (i, 0)
    const = lambda i: (0, 0)
    return pl.pallas_call(
        body, name=name, grid=(T // tm,),
        in_specs=[pl.BlockSpec((tm, D), row), pl.BlockSpec((tm, ATTN_WIDTH), row),
                  pl.BlockSpec((tm, CONV_DIM), row), pl.BlockSpec((tm, CONV_DIM), row),
                  pl.BlockSpec((tm, CONV_DIM), row),
                  pl.BlockSpec(conv_w.shape, const), pl.BlockSpec((1, CONV_DIM), const),
                  pl.BlockSpec(w_out.shape, const)],
        out_specs=(pl.BlockSpec((tm, D), row), pl.BlockSpec((tm, CONV_DIM), row)),
        out_shape=(jax.ShapeDtypeStruct((T, D), F32), jax.ShapeDtypeStruct((T, CONV_DIM), BF16)),
        scratch_shapes=[pltpu.VMEM((SUBLANES, CONV_DIM), F32)],
        compiler_params=_params("arbitrary"),
    )(x, attn_n, u, gb, gc, conv_w, gain, w_out)


def _mixout_bwd(dy, attn_n, conv_n, w_out, after, name, tm):
    T, D = dy.shape
    W = ATTN_WIDTH + CONV_DIM
    nt = T // tm

    def body(dy_ref, an_ref, cn_ref, w_ref, after_ref, dm_ref, dw_ref, dw_sc):
        i = pl.program_id(0)

        @pl.when(i == 0)
        def _():
            dw_sc[...] = jnp.zeros_like(dw_sc)

        dyb = dy_ref[...].astype(BF16)
        dm_ref[...] = _dot_nt(dyb, w_ref[...]).astype(BF16)
        dw_sc[0:ATTN_WIDTH, :] += _dot_tn(an_ref[...], dyb)
        dw_sc[ATTN_WIDTH:W, :] += _dot_tn(cn_ref[...], dyb)

        @pl.when(i == nt - 1)
        def _():
            dw_ref[...] = dw_sc[...].astype(BF16)

    row = lambda i: (i, 0)
    const = lambda i: (0, 0)
    return pl.pallas_call(
        body, name=name, grid=(nt,),
        in_specs=[pl.BlockSpec((tm, D), row), pl.BlockSpec((tm, ATTN_WIDTH), row),
                  pl.BlockSpec((tm, CONV_DIM), row), pl.BlockSpec(w_out.shape, const),
                  pl.BlockSpec(memory_space=pl.ANY)],
        out_specs=(pl.BlockSpec((tm, W), row), pl.BlockSpec((W, D), const)),
        out_shape=(jax.ShapeDtypeStruct((T, W), BF16), jax.ShapeDtypeStruct((W, D), BF16)),
        scratch_shapes=[pltpu.VMEM((W, D), F32)],
        compiler_params=_params("arbitrary"),
    )(dy, attn_n, conv_n, w_out, after)


def _conv_bwd(dmixed, u, gb, gc, conv_w, gain, name, tc):
    T = u.shape[0]
    nt = T // tc
    per8 = tc // SUBLANES

    def body(dm_ref, u_ref, b_ref, c_ref, ut_ref, ct_ref, cw_ref, g_ref,
             du_ref, db_ref, dc_ref, dcw_ref, dgain_ref, head_sc):
        i = pl.program_id(0)

        @pl.when(i == 0)
        def _():
            head_sc[...] = jnp.zeros_like(head_sc)
            dcw_ref[...] = jnp.zeros_like(dcw_ref)
            dgain_ref[...] = jnp.zeros_like(dgain_ref)

        uu = u_ref[...].astype(F32)
        cc = c_ref[...].astype(F32)
        bb = b_ref[...].astype(F32)
        cu = cc * uu
        tail = jnp.where(i == nt - 1, 0.0, ct_ref[...].astype(F32) * ut_ref[...].astype(F32))
        s1, s2 = _shift_down(cu, tail)
        w0, w1, w2 = cw_ref[0:1, :], cw_ref[1:2, :], cw_ref[2:3, :]
        pre = w0 * s2 + w1 * s1 + w2 * cu
        dconv, dgain = _rms_bwd(dm_ref[...].astype(F32), bb * pre, g_ref[...])
        dgain_ref[...] += dgain
        db_ref[...] = (dconv * pre).astype(BF16)
        dpre = dconv * bb
        dcw_ref[0:1, :] += jnp.sum(dpre * s2, axis=0, keepdims=True)
        dcw_ref[1:2, :] += jnp.sum(dpre * s1, axis=0, keepdims=True)
        dcw_ref[2:3, :] += jnp.sum(dpre * cu, axis=0, keepdims=True)
        n1, n2 = _shift_up(dpre, head_sc[...])
        head_sc[...] = dpre[0:SUBLANES, :]
        dcu = w2 * dpre + w1 * n1 + w0 * n2
        du_ref[...] = (dcu * cc).astype(BF16)
        dc_ref[...] = (dcu * uu).astype(BF16)

    rev = lambda i: (nt - 1 - i, 0)
    rev_right = lambda i: (nt - 1 - i, 1)
    tail_map = lambda i: (jnp.maximum((nt - 1 - i) * per8 - 1, 0), 0)
    const = lambda i: (0, 0)
    return pl.pallas_call(
        body, name=name, grid=(nt,),
        in_specs=[pl.BlockSpec((tc, CONV_DIM), rev_right),
                  pl.BlockSpec((tc, CONV_DIM), rev), pl.BlockSpec((tc, CONV_DIM), rev),
                  pl.BlockSpec((tc, CONV_DIM), rev),
                  pl.BlockSpec((SUBLANES, CONV_DIM), tail_map), pl.BlockSpec((SUBLANES, CONV_DIM), tail_map),
                  pl.BlockSpec(conv_w.shape, const), pl.BlockSpec((1, CONV_DIM), const)],
        out_specs=(pl.BlockSpec((tc, CONV_DIM), rev), pl.BlockSpec((tc, CONV_DIM), rev),
                   pl.BlockSpec((tc, CONV_DIM), rev),
                   pl.BlockSpec((SUBLANES, CONV_DIM), const), pl.BlockSpec((1, CONV_DIM), const)),
        out_shape=(jax.ShapeDtypeStruct((T, CONV_DIM), BF16), jax.ShapeDtypeStruct((T, CONV_DIM), BF16),
                   jax.ShapeDtypeStruct((T, CONV_DIM), BF16),
                   jax.ShapeDtypeStruct((SUBLANES, CONV_DIM), F32), jax.ShapeDtypeStruct((1, CONV_DIM), F32)),
        scratch_shapes=[pltpu.VMEM((SUBLANES, CONV_DIM), F32)],
        compiler_params=_params("arbitrary"),
    )(dmixed, u, gb, gc, u, gc, conv_w, gain)


def _mixin_bwd(dy, x, gain, dz, w_in_t, name, tm):
    T, D = x.shape
    nz = len(dz)

    def body(dy_ref, x_ref, g_ref, *rest):
        dz_refs, wt_ref, dx_ref, dgain_ref = rest[:nz], rest[nz], rest[nz + 1], rest[nz + 2]

        @pl.when(pl.program_id(0) == 0)
        def _():
            dgain_ref[...] = jnp.zeros_like(dgain_ref)

        dh = jnp.zeros((tm, D), F32)
        for r, lo, hi in zip(dz_refs, _Z_SPLITS[:-1], _Z_SPLITS[1:]):
            dh += _dot(r[...], wt_ref[lo:hi, :])
        dx, dgain = _rms_bwd(dh, x_ref[...], g_ref[...])
        dgain_ref[...] += dgain
        dx_ref[...] = dy_ref[...] + dx

    row = lambda i: (i, 0)
    const = lambda i: (0, 0)
    return pl.pallas_call(
        body, name=name, grid=(T // tm,),
        in_specs=[pl.BlockSpec((tm, D), row), pl.BlockSpec((tm, D), row), pl.BlockSpec((1, D), const)]
                 + [pl.BlockSpec((tm, a.shape[1]), row) for a in dz]
                 + [pl.BlockSpec(w_in_t.shape, const)],
        out_specs=(pl.BlockSpec((tm, D), row), pl.BlockSpec((1, D), const)),
        out_shape=(jax.ShapeDtypeStruct((T, D), F32), jax.ShapeDtypeStruct((1, D), F32)),
        compiler_params=_params("arbitrary"),
    )(dy, x, gain, *dz, w_in_t)


def _win_grad(dz, hn, name, tk):
    T, D = hn.shape
    nz = len(dz)
    nt = T // tk
    W = _Z_SPLITS[-1]

    def body(hn_ref, *rest):
        dz_refs, dw_ref, dw_sc = rest[:nz], rest[nz], rest[nz + 1]
        i = pl.program_id(0)

        @pl.when(i == 0)
        def _():
            dw_sc[...] = jnp.zeros_like(dw_sc)

        hn = hn_ref[...]
        for r, lo, hi in zip(dz_refs, _Z_SPLITS[:-1], _Z_SPLITS[1:]):
            dw_sc[lo:hi, :] += _dot_tn(r[...], hn)

        @pl.when(i == nt - 1)
        def _():
            dw_ref[...] = dw_sc[...].astype(BF16)

    row = lambda i: (i, 0)
    return pl.pallas_call(
        body, name=name, grid=(nt,),
        in_specs=[pl.BlockSpec((tk, D), row)] + [pl.BlockSpec((tk, a.shape[1]), row) for a in dz],
        out_specs=pl.BlockSpec((W, D), lambda i: (0, 0)),
        out_shape=jax.ShapeDtypeStruct((W, D), BF16),
        scratch_shapes=[pltpu.VMEM((W, D), F32)],
        compiler_params=_params("arbitrary"),
    )(hn, *dz)


def _loss_head(x, target, gain, name, tm):
    T, D = x.shape

    def body(x_ref, t_ref, g_ref, dx_ref, dgain_ref, loss_ref):
        @pl.when(pl.program_id(0) == 0)
        def _():
            dgain_ref[...] = jnp.zeros_like(dgain_ref)
            loss_ref[...] = jnp.zeros_like(loss_ref)

        xv = x_ref[...]
        gain_v = g_ref[...]
        _, xhat = _rms_stats(xv)
        err = xhat * gain_v - t_ref[...]
        part = 0.5 * jnp.sum(jnp.mean(err * err, axis=-1, keepdims=True), axis=0, keepdims=True)
        loss_ref[...] += part
        dx, dgain = _rms_bwd(err * (1.0 / D), xv, gain_v)
        dgain_ref[...] += dgain
        dx_ref[...] = dx

    row = lambda i: (i, 0)
    const = lambda i: (0, 0)
    return pl.pallas_call(
        body, name=name, grid=(T // tm,),
        in_specs=[pl.BlockSpec((tm, D), row), pl.BlockSpec((tm, D), row), pl.BlockSpec((1, D), const)],
        out_specs=(pl.BlockSpec((tm, D), row), pl.BlockSpec((1, D), const),
                   pl.BlockSpec((SUBLANES, 128), const)),
        out_shape=(jax.ShapeDtypeStruct((T, D), F32), jax.ShapeDtypeStruct((1, D), F32),
                   jax.ShapeDtypeStruct((SUBLANES, 128), F32)),
        compiler_params=_params("arbitrary"),
    )(x, target, gain)


def _sum_parts(parts, name, tr):
    P, R, C = parts.shape

    def body(p_ref, o_ref):
        g = p_ref[0].astype(F32)
        for d in range(1, P):
            g = g + p_ref[d].astype(F32)
        o_ref[...] = g

    return pl.pallas_call(
        body, name=name, grid=(R // tr,),
        in_specs=[pl.BlockSpec((P, tr, C), lambda i: (0, i, 0))],
        out_specs=pl.BlockSpec((tr, C), lambda i: (i, 0)),
        out_shape=jax.ShapeDtypeStruct((R, C), F32),
        compiler_params=_params("arbitrary"),
    )(parts)


def _adamw(parts, w, m, v, name, tr):
    P = parts.shape[0]
    R, C = w.shape

    def body(p_ref, w_ref, m_ref, v_ref, g_ref, d_ref, nm_ref, nv_ref):
        g = p_ref[0].astype(F32)
        for d in range(1, P):
            g = g + p_ref[d].astype(F32)
        nm = ADAM_B1 * m_ref[...] + (1.0 - ADAM_B1) * g
        nv = ADAM_B2 * v_ref[...] + (1.0 - ADAM_B2) * (g * g)
        m_hat = nm / (1.0 - ADAM_B1 ** ADAM_STEP)
        v_hat = nv / (1.0 - ADAM_B2 ** ADAM_STEP)
        g_ref[...] = g
        nm_ref[...] = nm
        nv_ref[...] = nv
        d_ref[...] = -ADAM_LR * (m_hat / (jnp.sqrt(v_hat) + ADAM_EPS) + ADAM_WD * w_ref[...])

    row = lambda i: (i, 0)
    spec = pl.BlockSpec((tr, C), row)
    shp = jax.ShapeDtypeStruct((R, C), F32)
    return pl.pallas_call(
        body, name=name, grid=(R // tr,),
        in_specs=[pl.BlockSpec((P, tr, C), lambda i: (0, i, 0)), spec, spec, spec],
        out_specs=(spec, spec, spec, spec),
        out_shape=(shp, shp, shp, shp),
        compiler_params=_params("arbitrary"),
    )(parts, w, m, v)


def _columns_of_blocks(g):
    n, R, w = g.shape
    return g.transpose(1, 0, 2).reshape(R, n * w)


def _pad_row(vec):
    vec = vec.reshape(1, -1)
    return jnp.pad(vec, ((0, 0), (0, PACK_COLS - vec.shape[1])))


def kernel(x, rel_bias_table, ffn1_norm, ffn1_w_gate, ffn1_w_up, ffn1_w_down, mix_norm, w_in, conv_w, attn_sinks, attn_out_norm, conv_out_norm, w_out, ffn2_norm, ffn2_w_gate, ffn2_w_up, ffn2_w_down, final_norm, loss_target, m_rel_bias_table, m_ffn1_norm, m_ffn1_w_gate, m_ffn1_w_up, m_ffn1_w_down, m_mix_norm, m_w_in, m_conv_w, m_attn_sinks, m_attn_out_norm, m_conv_out_norm, m_w_out, m_ffn2_norm, m_ffn2_w_gate, m_ffn2_w_up, m_ffn2_w_down, m_final_norm, v_rel_bias_table, v_ffn1_norm, v_ffn1_w_gate, v_ffn1_w_up, v_ffn1_w_down, v_mix_norm, v_w_in, v_conv_w, v_attn_sinks, v_attn_out_norm, v_conv_out_norm, v_w_out, v_ffn2_norm, v_ffn2_w_gate, v_ffn2_w_up, v_ffn2_w_down, v_final_norm):
    T, D = x.shape[1], x.shape[2]
    x0 = x[0]
    target = loss_target[0]
    tm = min(TM_FFN, T)
    tm_bwd = min(TM_FFN_BWD, T)
    tm_mix = min(TM_MIX, T)
    tk = min(TK_WGRAD, T)
    tf = TF_FFN
    nblk = min(ATTN_BLOCKS, T // BLOCK)
    me = 4 * lax.axis_index("x") + 2 * lax.axis_index("y") + lax.axis_index("c")

    big = {
        "ffn1_w_gate": (ffn1_w_gate[0], m_ffn1_w_gate[0], v_ffn1_w_gate[0], True),
        "ffn1_w_up": (ffn1_w_up[0], m_ffn1_w_up[0], v_ffn1_w_up[0], True),
        "ffn1_w_down": (ffn1_w_down[0], m_ffn1_w_down[0], v_ffn1_w_down[0], False),
        "w_in": (w_in[0], m_w_in[0], v_w_in[0], True),
        "w_out": (w_out[0], m_w_out[0], v_w_out[0], False),
        "ffn2_w_gate": (ffn2_w_gate[0], m_ffn2_w_gate[0], v_ffn2_w_gate[0], True),
        "ffn2_w_up": (ffn2_w_up[0], m_ffn2_w_up[0], v_ffn2_w_up[0], True),
        "ffn2_w_down": (ffn2_w_down[0], m_ffn2_w_down[0], v_ffn2_w_down[0], False),
    }

    def block_to_send(name):
        w, _, _, transposed = big[name]
        return (w.T if transposed else w).astype(BF16)

    names1 = ["ffn1_w_gate", "ffn1_w_up", "ffn1_w_down"]
    names_rest = ["w_in", "w_out", "ffn2_w_gate", "ffn2_w_up", "ffn2_w_down"]
    h1, token = _exchange_start([block_to_send(n) for n in names1], False, "gather_start_ffn1", ffn1_norm)
    h_rest, token = _exchange_start([block_to_send(n) for n in names_rest[:2]] + [conv_w[0]]
                                    + [block_to_send(n) for n in names_rest[2:]], False,
                                    "gather_start_rest", token)
    wgt1, wut1, wd1 = [g.reshape(-1, D) for g in _exchange_wait(h1, False, "gather_wait_ffn1", token)]

    x1, xn1, gate1, up1 = _ffn_fwd(x0, ffn1_norm, wgt1, wut1, wd1, "ffn1_fwd", tm, tf)
    mixw = _exchange_wait(h_rest[:3], False, "gather_wait_mix", x1)
    win_t = mixw[0].reshape(-1, D)
    wout = mixw[1].reshape(-1, D)
    cw = _columns_of_blocks(mixw[2])
    hn, q, k, v, u, gb, gc = _mixin_fwd(x1, mix_norm, win_t, "mixin_fwd", tm_mix)
    bucket = jnp.asarray(_bucket_table().T.copy())
    sinks = attn_sinks.reshape(-1)
    bias_t = _bias_build(rel_bias_table, bucket, "bias_build")
    attn_raw, attn_n = _attn_fwd(q, k, v, bias_t, sinks, attn_out_norm, "attn_fwd", nblk)
    x2, conv_n = _mixout_fwd(x1, attn_n, u, gb, gc, cw, conv_out_norm, wout, "mixout_fwd", tm_mix)
    wgt2, wut2, wd2 = [g.reshape(-1, D) for g in _exchange_wait(h_rest[3:], False, "gather_wait_ffn2", x2)]
    x3, xn2, gate2, up2 = _ffn_fwd(x2, ffn2_norm, wgt2, wut2, wd2, "ffn2_fwd", tm, tf)

    def blocks(g):
        return g.reshape(N_DEV, -1, D)

    dx3, d_final, loss_part = _loss_head(x3, target, final_norm.reshape(1, D), "loss_head", tm_mix)
    dx2, d_ffn2_norm, dg2, du2, hh2, do2 = _ffn_bwd(
        dx3, x2, ffn2_norm, gate2, up2, wgt2, wut2, wd2, dx3, "ffn2_bwd", tm_bwd, tf)
    grads2 = _ffn_wgrad(dg2, du2, hh2, xn2, do2, "ffn2_wgrad", BM_WGRAD, tk)
    handles2, token2 = _exchange_start([blocks(g) for g in grads2], True, "grads_start_ffn2", dx2)

    dmixed, d_wout = _mixout_bwd(dx2, attn_n, conv_n, wout, token2, "mixout_bwd", tm_mix)
    dq, dk, dv, dbias, dsink, d_attn_norm = _attn_bwd(
        dmixed, attn_raw, q, k, v, bias_t, sinks, attn_out_norm, "attn_bwd", nblk)
    du, dgb, dgc, d_cw, d_conv_norm = _conv_bwd(dmixed, u, gb, gc, cw, conv_out_norm, "conv_bwd", tm_mix)
    d_table = _bias_grad(dbias, bucket, "bias_grad")
    dz = [dq, dk, dv, du, dgb, dgc]
    dx1, d_mix_norm = _mixin_bwd(dx2, x1, mix_norm, dz, win_t, "mixin_bwd", tm_mix)
    d_win_t = _win_grad(dz, hn, "win_grad", min(TM_MIX, T))
    handles_mix, token_mix = _exchange_start([blocks(d_win_t), blocks(d_wout)], True, "grads_start_mix", dx1)

    dx0, d_ffn1_norm, dg1, du1, hh1, do1 = _ffn_bwd(
        dx1, x0, ffn1_norm, gate1, up1, wgt1, wut1, wd1, token_mix, "ffn1_bwd", tm_bwd, tf)
    grads1 = _ffn_wgrad(dg1, du1, hh1, xn1, do1, "ffn1_wgrad", BM_WGRAD, tk)
    handles1, token1 = _exchange_start([blocks(g) for g in grads1], True, "grads_start_ffn1", dx0)

    res = {}

    def update(names, parts):
        last = None
        for name, p in zip(names, parts):
            w, m_, v_, transposed = big[name]
            if transposed:
                summed = _sum_parts(p, "sum_" + name, _row_tile(p.shape[1], ADAM_ROWS))
                p = summed.T[None]
            g, dl, nm, nv = _adamw(p, w, m_, v_, "adamw_" + name, _row_tile(w.shape[0], ADAM_ROWS))
            res[name] = tuple(a[None] for a in (g, dl, nm, nv))
            last = g
        return last

    parts2 = _exchange_wait(handles2, True, "grads_wait_ffn2", token1)
    done2 = update(["ffn2_w_gate", "ffn2_w_up", "ffn2_w_down"], parts2)
    parts_mix = _exchange_wait(handles_mix, True, "grads_wait_mix", done2)
    done_mix = update(["w_in", "w_out"], parts_mix)
    parts1 = _exchange_wait(handles1, True, "grads_wait_ffn1", done_mix)
    update(names1, parts1)

    def pack(ffn1, mixn, ffn2, fin, attn_n_, conv_n_, sink_, extra, convw, table):
        rows = [_pad_row(ffn1), _pad_row(mixn), _pad_row(ffn2), _pad_row(fin),
                _pad_row(jnp.concatenate([attn_n_.reshape(-1), conv_n_.reshape(-1)])),
                _pad_row(sink_), _pad_row(extra),
                jnp.zeros((1, PACK_COLS), F32),
                jnp.pad(convw, ((0, 0), (0, PACK_COLS - convw.shape[1]))),
                _pad_row(table),
                jnp.zeros((PACK_ROWS - 12, PACK_COLS), F32)]
        return jnp.concatenate(rows, axis=0)

    def own_channels(a):
        full = jnp.zeros((a.shape[1], CONV_DIM), F32)
        return lax.dynamic_update_slice(full, a[0], (0, me * a.shape[2]))

    g_pack = pack(d_ffn1_norm, d_mix_norm, d_ffn2_norm, d_final, d_attn_norm, d_conv_norm,
                  dsink[:, 0], loss_part[0, :1], d_cw[:3], d_table[:, :N_Q_HEADS])
    zero1 = jnp.zeros((1,), F32)
    w_pack = pack(ffn1_norm, mix_norm, ffn2_norm, final_norm, attn_out_norm, conv_out_norm,
                  attn_sinks, zero1, own_channels(conv_w), rel_bias_table)
    m_pack = pack(m_ffn1_norm, m_mix_norm, m_ffn2_norm, m_final_norm, m_attn_out_norm, m_conv_out_norm,
                  m_attn_sinks, zero1, own_channels(m_conv_w), m_rel_bias_table)
    v_pack = pack(v_ffn1_norm, v_mix_norm, v_ffn2_norm, v_final_norm, v_attn_out_norm, v_conv_out_norm,
                  v_attn_sinks, zero1, own_channels(v_conv_w), v_rel_bias_table)
    (g_all,) = _exchange([g_pack], False, "gather_small")
    packs = _adamw(g_all, w_pack, m_pack, v_pack, "adamw_small", PACK_ROWS)

    def unpack(pk):
        cwb = lax.dynamic_slice(pk[8:11, :CONV_DIM], (0, me * conv_w.shape[2]), (3, conv_w.shape[2]))
        return {
            "ffn1_norm": pk[0:1, :D], "mix_norm": pk[1:2, :D], "ffn2_norm": pk[2:3, :D],
            "final_norm": pk[3, :D],
            "attn_out_norm": pk[4:5, :ATTN_WIDTH], "conv_out_norm": pk[4:5, ATTN_WIDTH:ATTN_WIDTH + CONV_DIM],
            "attn_sinks": pk[5:6, :N_Q_HEADS],
            "conv_w": cwb[None],
            "rel_bias_table": pk[11, :NUM_BUCKETS * N_Q_HEADS].reshape(NUM_BUCKETS, N_Q_HEADS),
        }

    small = [unpack(pk) for pk in packs]
    loss = packs[0][6, 0]

    order = ["rel_bias_table", "ffn1_norm", "ffn1_w_gate", "ffn1_w_up", "ffn1_w_down", "mix_norm", "w_in",
             "conv_w", "attn_sinks", "attn_out_norm", "conv_out_norm", "w_out", "ffn2_norm",
             "ffn2_w_gate", "ffn2_w_up", "ffn2_w_down", "final_norm"]
    outs = [loss, dx0[None]]
    for kind in range(4):
        for name in order:
            outs.append(res[name][kind] if name in res else small[kind][name])
    return tuple(outs)
```

```python
import math

import numpy as np
import jax
import jax.numpy as jnp
from jax import lax
from jax.experimental import pallas as pl
from jax.experimental.pallas import tpu as pltpu

F32 = jnp.float32
BF16 = jnp.bfloat16

N_DEV = 8
EPS = 1e-6
HEAD_DIM = 64
N_Q_HEADS = 8
N_KV_HEADS = 2
GQA_GROUP = 4
ATTN_WIDTH = 512
KV_WIDTH = 128
CONV_DIM = 512
BLOCK = 128
WINDOW = 128
NUM_BUCKETS = 32
MAX_DISTANCE = 128
SCALE = HEAD_DIM ** -0.5
MASKED = -1e30
GROUP_ROWS = GQA_GROUP * BLOCK

ADAM_LR = 0.001
ADAM_B1 = 0.9
ADAM_B2 = 0.999
ADAM_EPS = 1e-08
ADAM_WD = 0.01
ADAM_STEP = 10

VMEM_LIMIT_BYTES = 40 * 1024 * 1024
SUBLANES = 8
PACK_ROWS = 16
PACK_COLS = 1024

TM_FFN = 1024
TM_FFN_BWD = 1024
TM_MIX = 512
TK_WGRAD = 1024
TF_FFN = 256
BM_WGRAD = 1408
ATTN_BLOCKS = 4
ADAM_ROWS = 256


def _row_tile(rows, limit):
    best = rows
    for t in range(16, min(rows, limit) + 1, 16):
        if rows % t == 0:
            best = t
    return best


def _params(*sem):
    return pltpu.CompilerParams(dimension_semantics=sem, vmem_limit_bytes=VMEM_LIMIT_BYTES)


def _dot(a, b):
    return jnp.dot(a, b, preferred_element_type=F32)


def _dot_nt(a, b):
    return lax.dot_general(a, b, (((1,), (1,)), ((), ())), preferred_element_type=F32)


def _dot_tn(a, b):
    return lax.dot_general(a, b, (((0,), (0,)), ((), ())), preferred_element_type=F32)


def _sigmoid(g):
    return 0.5 * jnp.tanh(0.5 * g) + 0.5


def _rms_stats(x):
    inv = lax.rsqrt(jnp.mean(x * x, axis=-1, keepdims=True) + EPS)
    return inv, x * inv


def _rms_bwd(dy, x, gain):
    inv, xhat = _rms_stats(x)
    dgain = jnp.sum(dy * xhat, axis=0, keepdims=True)
    dxh = dy * gain
    dx = inv * (dxh - xhat * jnp.mean(dxh * xhat, axis=-1, keepdims=True))
    return dx, dgain


def _peer_list():
    x, y, c = lax.axis_index("x"), lax.axis_index("y"), lax.axis_index("c")
    peers = []
    for k in range(1, N_DEV):
        px = 1 - x if (k >> 2) & 1 else x
        py = 1 - y if (k >> 1) & 1 else y
        pc = 1 - c if k & 1 else c
        peers.append((px, py, pc))
    return 4 * x + 2 * y + c, peers


def _exchange(arrs, scatter, name):
    n = len(arrs)
    out_shape = []
    for a in arrs:
        shp = a.shape if scatter else (N_DEV,) + a.shape
        out_shape.append(jax.ShapeDtypeStruct(shp, a.dtype))

    def body(*refs):
        ins, outs = refs[:n], refs[n:2 * n]
        send_sems, recv_sems, local_sems = refs[2 * n:]
        me, peers = _peer_list()
        started = []
        for a in range(n):
            own = ins[a].at[me] if scatter else ins[a]
            loc = pltpu.make_async_copy(own, outs[a].at[me], local_sems.at[a])
            loc.start()
            started.append(loc)
        sends = []
        for a in range(n):
            for k, (px, py, pc) in enumerate(peers):
                src = ins[a].at[4 * px + 2 * py + pc] if scatter else ins[a]
                cp = pltpu.make_async_remote_copy(
                    src_ref=src, dst_ref=outs[a].at[me],
                    send_sem=send_sems.at[a, k], recv_sem=recv_sems.at[a, k],
                    device_id=(px, py, pc), device_id_type=pl.DeviceIdType.MESH)
                cp.start()
                sends.append(cp)
        for a in range(n):
            for k, (px, py, pc) in enumerate(peers):
                landed = outs[a].at[4 * px + 2 * py + pc]
                pltpu.make_async_remote_copy(
                    src_ref=landed, dst_ref=landed,
                    send_sem=send_sems.at[a, k], recv_sem=recv_sems.at[a, k],
                    device_id=(px, py, pc), device_id_type=pl.DeviceIdType.MESH).wait_recv()
        for cp in sends:
            cp.wait_send()
        for loc in started:
            loc.wait()

    hbm = pl.BlockSpec(memory_space=pl.ANY)
    return pl.pallas_call(
        body, name=name, out_shape=tuple(out_shape),
        in_specs=[hbm] * n, out_specs=tuple([hbm] * n),
        scratch_shapes=[pltpu.SemaphoreType.DMA((n, N_DEV - 1)),
                        pltpu.SemaphoreType.DMA((n, N_DEV - 1)),
                        pltpu.SemaphoreType.DMA((n,))],
    )(*arrs)


_HBM = pl.BlockSpec(memory_space=pltpu.HBM)
_SEM = pl.BlockSpec(memory_space=pltpu.SEMAPHORE)
_EFFECT = pltpu.SideEffectType.DATAFLOW_SIDE_EFFECTING


def _split_copies(srcs, lands, send_sems, recv_sems, scatter):
    me, peers = _peer_list()
    copies = []
    for a in range(len(srcs)):
        for k, (px, py, pc) in enumerate(peers):
            src = srcs[a].at[4 * px + 2 * py + pc] if scatter else srcs[a]
            copies.append(pltpu.make_async_remote_copy(
                src_ref=src, dst_ref=lands[a].at[me],
                send_sem=send_sems[a].at[k], recv_sem=recv_sems[a].at[k],
                device_id=(px, py, pc), device_id_type=pl.DeviceIdType.MESH))
    return copies


def _exchange_start(arrs, scatter, name, after):
    n = len(arrs)
    me = 4 * lax.axis_index("x") + 2 * lax.axis_index("y") + lax.axis_index("c")
    lands = []
    for a in arrs:
        own = lax.dynamic_index_in_dim(a, me, 0, keepdims=True) if scatter else a[None]
        shp = a.shape if scatter else (N_DEV,) + a.shape
        lands.append(lax.dynamic_update_slice(lax.empty(shp, a.dtype), own, (me,) + (0,) * (len(shp) - 1)))

    def body(*refs):
        srcs, lnds = refs[:n], refs[n:2 * n]
        outs = refs[2 * n + 1:]
        send_sems, recv_sems = outs[:n], outs[n:2 * n]
        token = outs[4 * n]
        for cp in _split_copies(srcs, lnds, send_sems, recv_sems, scatter):
            cp.start()
        token[...] = jnp.zeros_like(token)

    sem = pltpu.SemaphoreType.DMA((N_DEV - 1,))
    out_shape = ([sem] * (2 * n) + [pltpu.HBM(a.shape, a.dtype) for a in arrs]
                 + [pltpu.HBM(l.shape, l.dtype) for l in lands] + [jax.ShapeDtypeStruct((SUBLANES, 128), F32)])
    res = pl.pallas_call(
        body, name=name, out_shape=tuple(out_shape),
        in_specs=[_HBM] * (2 * n) + [pl.BlockSpec(memory_space=pl.ANY)],
        out_specs=tuple([_SEM] * (2 * n) + [_HBM] * (2 * n) + [pl.BlockSpec(memory_space=pltpu.VMEM)]),
        input_output_aliases={i: 2 * n + i for i in range(2 * n)},
        compiler_params=pltpu.CompilerParams(has_side_effects=_EFFECT),
    )(*[pltpu.with_memory_space_constraint(a, pltpu.HBM) for a in arrs],
      *[pltpu.with_memory_space_constraint(l, pltpu.HBM) for l in lands], after)
    handles = [(res[2 * n + a], res[3 * n + a], res[a], res[n + a]) for a in range(n)]
    return handles, res[4 * n]


def _exchange_wait(handles, scatter, name, after):
    n = len(handles)

    def body(*refs):
        srcs, lnds = refs[:n], refs[n:2 * n]
        send_sems, recv_sems = refs[2 * n:3 * n], refs[3 * n:4 * n]
        for cp in _split_copies(srcs, lnds, send_sems, recv_sems, scatter):
            cp.wait_send()
            cp.wait_recv()

    srcs = [h[0] for h in handles]
    lands = [h[1] for h in handles]
    res = pl.pallas_call(
        body, name=name,
        out_shape=tuple([pltpu.HBM(a.shape, a.dtype) for a in srcs] + [pltpu.HBM(l.shape, l.dtype) for l in lands]),
        in_specs=[_HBM] * (2 * n) + [_SEM] * (2 * n) + [pl.BlockSpec(memory_space=pl.ANY)],
        out_specs=tuple([_HBM] * (2 * n)),
        input_output_aliases={i: i for i in range(2 * n)},
        compiler_params=pltpu.CompilerParams(has_side_effects=_EFFECT),
    )(*srcs, *lands, *[h[2] for h in handles], *[h[3] for h in handles], after)
    return list(res[n:])


def _stack_gate_up(wgt, wut, tf):
    F, D = wgt.shape
    return jnp.stack([wgt.reshape(F // tf, tf, D), wut.reshape(F // tf, tf, D)], axis=1).reshape(2 * F, D)


def _unstack_gate_up(w, tf):
    F2, D = w.shape
    w4 = w.reshape(F2 // (2 * tf), 2, tf, D)
    return w4[:, 0].reshape(F2 // 2, D), w4[:, 1].reshape(F2 // 2, D)


def _ffn_fwd(x, gain, wgu, wd, name, tm, tf):
    T, D = x.shape
    F = wd.shape[0]
    nj = F // tf

    def body(x_ref, g_ref, wgu_ref, wd_ref, xo_ref, xn_ref, gu_ref, xn_sc, acc_sc):
        j = pl.program_id(1)

        @pl.when(j == 0)
        def _():
            _, xhat = _rms_stats(x_ref[...])
            xn = (xhat * g_ref[...]).astype(BF16)
            xn_sc[...] = xn
            xn_ref[...] = xn
            acc_sc[...] = jnp.zeros_like(acc_sc)

        gu = _dot_nt(xn_sc[...], wgu_ref[...])
        gu_ref[...] = gu.astype(BF16)
        g, u = gu[:, :tf], gu[:, tf:]
        h = (g * _sigmoid(g) * u).astype(BF16)
        acc_sc[...] += _dot(h, wd_ref[...])

        @pl.when(j == nj - 1)
        def _():
            xo_ref[...] = x_ref[...] + 0.5 * acc_sc[...]

    return pl.pallas_call(
        body, name=name, grid=(T // tm, nj),
        in_specs=[pl.BlockSpec((tm, D), lambda i, j: (i, 0)),
                  pl.BlockSpec((1, D), lambda i, j: (0, 0)),
                  pl.BlockSpec((2 * tf, D), lambda i, j: (j, 0)),
                  pl.BlockSpec((tf, D), lambda i, j: (j, 0))],
        out_specs=(pl.BlockSpec((tm, D), lambda i, j: (i, 0)),
                   pl.BlockSpec((tm, D), lambda i, j: (i, 0)),
                   pl.BlockSpec((tm, 2 * tf), lambda i, j: (i, j))),
        out_shape=(jax.ShapeDtypeStruct((T, D), F32), jax.ShapeDtypeStruct((T, D), BF16),
                   jax.ShapeDtypeStruct((T, 2 * F), BF16)),
        scratch_shapes=[pltpu.VMEM((tm, D), BF16), pltpu.VMEM((tm, D), F32)],
        compiler_params=_params("arbitrary", "arbitrary"),
    )(x, gain, wgu, wd)


def _ffn_bwd(dy, x, gain, gu, wgu, wd, after, name, tm, tf):
    T, D = x.shape
    F = wd.shape[0]
    nj = F // tf

    def body(dy_ref, x_ref, g_ref, gu_ref, wgu_ref, wd_ref, after_ref,
             dx_ref, dgain_ref, dgu_ref, hh_ref, do_ref, do_sc, acc_sc):
        i, j = pl.program_id(0), pl.program_id(1)

        @pl.when((i == 0) & (j == 0))
        def _():
            dgain_ref[...] = jnp.zeros_like(dgain_ref)

        @pl.when(j == 0)
        def _():
            do = (0.5 * dy_ref[...]).astype(BF16)
            do_sc[...] = do
            do_ref[...] = do
            acc_sc[...] = jnp.zeros_like(acc_sc)

        dh = _dot_nt(do_sc[...], wd_ref[...])
        g = gu_ref[:, :tf].astype(F32)
        u = gu_ref[:, tf:].astype(F32)
        sig = _sigmoid(g)
        s = g * sig
        dgu_ref[:, :tf] = (dh * u * (sig + s * (1.0 - sig))).astype(BF16)
        dgu_ref[:, tf:] = (dh * s).astype(BF16)
        hh_ref[...] = (s * u).astype(BF16)
        acc_sc[...] += _dot(dgu_ref[...], wgu_ref[...])

        @pl.when(j == nj - 1)
        def _():
            dx, dgain = _rms_bwd(acc_sc[...], x_ref[...], g_ref[...])
            dgain_ref[...] += dgain
            dx_ref[...] = dy_ref[...] + dx

    tile_spec = pl.BlockSpec((tm, D), lambda i, j: (i, 0), pipeline_mode=pl.Buffered(1))
    return pl.pallas_call(
        body, name=name, grid=(T // tm, nj),
        in_specs=[tile_spec, tile_spec,
                  pl.BlockSpec((1, D), lambda i, j: (0, 0)),
                  pl.BlockSpec((tm, 2 * tf), lambda i, j: (i, j)),
                  pl.BlockSpec((2 * tf, D), lambda i, j: (j, 0)),
                  pl.BlockSpec((tf, D), lambda i, j: (j, 0)),
                  pl.BlockSpec(memory_space=pl.ANY)],
        out_specs=(tile_spec,
                   pl.BlockSpec((1, D), lambda i, j: (0, 0)),
                   pl.BlockSpec((tm, 2 * tf), lambda i, j: (i, j)),
                   pl.BlockSpec((tm, tf), lambda i, j: (i, j)),
                   pl.BlockSpec((tm, D), lambda i, j: (i, 0))),
        out_shape=(jax.ShapeDtypeStruct((T, D), F32), jax.ShapeDtypeStruct((1, D), F32),
                   jax.ShapeDtypeStruct((T, 2 * F), BF16), jax.ShapeDtypeStruct((T, F), BF16),
                   jax.ShapeDtypeStruct((T, D), BF16)),
        scratch_shapes=[pltpu.VMEM((tm, D), BF16), pltpu.VMEM((tm, D), F32)],
        compiler_params=_params("arbitrary", "arbitrary"),
    )(dy, x, gain, gu, wgu, wd, after)


def _tn_grad(a, b, name, bm, tk):
    T, M = a.shape
    D = b.shape[1]
    nk = T // tk

    def body(a_ref, b_ref, o_ref, acc_sc):
        k = pl.program_id(1)

        @pl.when(k == 0)
        def _():
            acc_sc[...] = jnp.zeros_like(acc_sc)

        acc_sc[...] += _dot_tn(a_ref[...], b_ref[...])

        @pl.when(k == nk - 1)
        def _():
            o_ref[...] = acc_sc[...].astype(BF16)

    return pl.pallas_call(
        body, name=name, grid=(M // bm, nk),
        in_specs=[pl.BlockSpec((tk, bm), lambda i, k: (k, i)), pl.BlockSpec((tk, D), lambda i, k: (k, 0))],
        out_specs=pl.BlockSpec((bm, D), lambda i, k: (i, 0)),
        out_shape=jax.ShapeDtypeStruct((M, D), BF16),
        scratch_shapes=[pltpu.VMEM((bm, D), F32)],
        compiler_params=_params("arbitrary", "arbitrary"),
    )(a, b)


def _ffn_wgrad(dgu, hh, xn, do, name, bm, tk, tf):
    d_gate, d_up = _unstack_gate_up(_tn_grad(dgu, xn, name + "_gu", bm, tk), tf)
    return [d_gate, d_up, _tn_grad(hh, do, name + "_down", bm, tk)]


_Z_SPLITS = (0, 512, 640, 768, 1280, 1792, 2304)


def _mixin_fwd(x, gain, w_in_t, name, tm):
    T, D = x.shape
    widths = [b - a for a, b in zip(_Z_SPLITS[:-1], _Z_SPLITS[1:])]

    def body(x_ref, g_ref, w_ref, hn_ref, *outs):
        _, xhat = _rms_stats(x_ref[...])
        hn = (xhat * g_ref[...]).astype(BF16)
        hn_ref[...] = hn
        for o_ref, lo, hi in zip(outs, _Z_SPLITS[:-1], _Z_SPLITS[1:]):
            o_ref[...] = _dot_nt(hn, w_ref[lo:hi, :]).astype(BF16)

    return pl.pallas_call(
        body, name=name, grid=(T // tm,),
        in_specs=[pl.BlockSpec((tm, D), lambda i: (i, 0)),
                  pl.BlockSpec((1, D), lambda i: (0, 0)),
                  pl.BlockSpec(w_in_t.shape, lambda i: (0, 0))],
        out_specs=tuple([pl.BlockSpec((tm, D), lambda i: (i, 0))]
                        + [pl.BlockSpec((tm, w), lambda i: (i, 0)) for w in widths]),
        out_shape=tuple([jax.ShapeDtypeStruct((T, D), BF16)]
                        + [jax.ShapeDtypeStruct((T, w), BF16) for w in widths]),
        compiler_params=_params("arbitrary"),
    )(x, gain, w_in_t)


def _bucket_table():
    qi = np.arange(BLOCK, dtype=np.int32)[:, None]
    kj = np.arange(2 * BLOCK, dtype=np.int32)[None, :]
    dist = qi + BLOCK - kj
    n = np.maximum(dist, 0)
    max_exact = NUM_BUCKETS // 2
    large = max_exact + (np.log(np.maximum(n, 1).astype(np.float32) / max_exact)
                         / math.log(MAX_DISTANCE / max_exact)
                         * (NUM_BUCKETS - max_exact)).astype(np.int32)
    large = np.minimum(large, NUM_BUCKETS - 1)
    bucket = np.where(n < max_exact, n, large).astype(np.int32)
    valid = (dist >= 0) & (dist < WINDOW)
    return np.where(valid, bucket, -1).astype(np.int32)


def _bias_build(table, bucket, name):
    def body(t_ref, b_ref, o_ref):
        bk = b_ref[...]
        for h in range(N_Q_HEADS):
            def step(b, acc):
                return jnp.where(bk == b, t_ref[b, h], acc)
            o_ref[h] = lax.fori_loop(0, NUM_BUCKETS, step, jnp.full(bk.shape, MASKED, F32))

    return pl.pallas_call(
        body, name=name,
        in_specs=[pl.BlockSpec(memory_space=pltpu.SMEM), pl.BlockSpec(memory_space=pltpu.VMEM)],
        out_specs=pl.BlockSpec(memory_space=pltpu.VMEM),
        out_shape=jax.ShapeDtypeStruct((N_Q_HEADS,) + bucket.shape, F32),
    )(table, bucket)


def _bias_grad(dbias, bucket, name):
    def body(d_ref, b_ref, o_ref):
        bk = b_ref[...]
        row = lax.broadcasted_iota(jnp.int32, o_ref.shape, 0)
        lane = lax.broadcasted_iota(jnp.int32, o_ref.shape, 1)
        res = jnp.zeros(o_ref.shape, F32)
        for h in range(N_Q_HEADS):
            d = d_ref[h]

            def step(b, acc):
                tot = jnp.sum(jnp.where(bk == b, d, 0.0), axis=1, keepdims=True)
                tot = jnp.sum(tot, axis=0, keepdims=True)
                return jnp.where((row == b) & (lane == h), tot, acc)
            res = lax.fori_loop(0, NUM_BUCKETS, step, res)
        o_ref[...] = res

    return pl.pallas_call(
        body, name=name,
        in_specs=[pl.BlockSpec(memory_space=pltpu.VMEM), pl.BlockSpec(memory_space=pltpu.VMEM)],
        out_specs=pl.BlockSpec(memory_space=pltpu.VMEM),
        out_shape=jax.ShapeDtypeStruct((NUM_BUCKETS, 128), F32),
    )(dbias, bucket)


def _head_cols(h):
    return slice(h * HEAD_DIM, (h + 1) * HEAD_DIM)


def _stack_heads(ref, r0, g, dtype):
    return jnp.concatenate(
        [ref[pl.ds(r0, BLOCK), _head_cols(GQA_GROUP * g + j)].astype(dtype) for j in range(GQA_GROUP)], axis=0)


def _unstack_heads(ref, r0, g, val):
    for j in range(GQA_GROUP):
        ref[pl.ds(r0, BLOCK), _head_cols(GQA_GROUP * g + j)] = val[j * BLOCK:(j + 1) * BLOCK, :]


def _head_lanes(h):
    return slice(h * BLOCK, (h + 1) * BLOCK)


def _group_lanes(g):
    return slice(g * GROUP_ROWS, (g + 1) * GROUP_ROWS)


def _head_softmax(st, bias_t, sink, no_prev):
    s = st * SCALE + bias_t
    row = lax.broadcasted_iota(jnp.int32, s.shape, 0)
    s = jnp.where(no_prev & (row < BLOCK), MASKED, s)
    m = jnp.maximum(jnp.max(s, axis=0, keepdims=True), sink)
    p = jnp.exp(s - m)
    ps = jnp.exp(sink - m)
    r = 1.0 / (jnp.sum(p, axis=0, keepdims=True) + ps)
    return p * r, ps * r


def _load_band(kf_sc, vf_sc, kp_ref, kc_ref, vp_ref, vc_ref, tq):
    kf_sc[0:BLOCK, :] = kp_ref[...]
    kf_sc[BLOCK:BLOCK + tq, :] = kc_ref[...]
    vf_sc[0:BLOCK, :] = vp_ref[...]
    vf_sc[BLOCK:BLOCK + tq, :] = vc_ref[...]


def _attn_fwd(q, k, v, bias_t, sinks, gain, name, nblk):
    T = q.shape[0]
    tq = nblk * BLOCK

    def body(sink_ref, q_ref, kc_ref, kp_ref, vc_ref, vp_ref, bias_ref, g_ref, raw_ref, nrm_ref,
             kf_sc, vf_sc, o_sc, st_sc, pt_sc):
        i = pl.program_id(0)
        _load_band(kf_sc, vf_sc, kp_ref, kc_ref, vp_ref, vc_ref, tq)

        def block(b, carry):
            r0 = pl.multiple_of(b * BLOCK, BLOCK)
            no_prev = (i == 0) & (b == 0)
            for g in range(N_KV_HEADS):
                kb = kf_sc[pl.ds(r0, 2 * BLOCK), _head_cols(g)]
                st_sc[:, _group_lanes(g)] = _dot_nt(kb, _stack_heads(q_ref, r0, g, BF16))
            for h in range(N_Q_HEADS):
                p, _ = _head_softmax(st_sc[:, _head_lanes(h)], bias_ref[h], sink_ref[h], no_prev)
                pt_sc[:, _head_lanes(h)] = p.astype(BF16)
            for g in range(N_KV_HEADS):
                vb = vf_sc[pl.ds(r0, 2 * BLOCK), _head_cols(g)]
                _unstack_heads(o_sc, r0, g, _dot_tn(pt_sc[:, _group_lanes(g)], vb))
            return carry

        lax.fori_loop(0, nblk, block, 0)
        o = o_sc[...]
        raw_ref[...] = o.astype(BF16)
        _, ohat = _rms_stats(o)
        nrm_ref[...] = (ohat * g_ref[...]).astype(BF16)

    cur = lambda i: (i, 0)
    prev = lambda i: (jnp.maximum(i * nblk - 1, 0), 0)
    lanes = N_Q_HEADS * BLOCK
    return pl.pallas_call(
        body, name=name, grid=(T // tq,),
        in_specs=[pl.BlockSpec(memory_space=pltpu.SMEM),
                  pl.BlockSpec((tq, ATTN_WIDTH), cur),
                  pl.BlockSpec((tq, KV_WIDTH), cur), pl.BlockSpec((BLOCK, KV_WIDTH), prev),
                  pl.BlockSpec((tq, KV_WIDTH), cur), pl.BlockSpec((BLOCK, KV_WIDTH), prev),
                  pl.BlockSpec(bias_t.shape, lambda i: (0, 0, 0)),
                  pl.BlockSpec((1, ATTN_WIDTH), lambda i: (0, 0))],
        out_specs=(pl.BlockSpec((tq, ATTN_WIDTH), cur), pl.BlockSpec((tq, ATTN_WIDTH), cur)),
        out_shape=(jax.ShapeDtypeStruct((T, ATTN_WIDTH), BF16), jax.ShapeDtypeStruct((T, ATTN_WIDTH), BF16)),
        scratch_shapes=[pltpu.VMEM((tq + BLOCK, KV_WIDTH), BF16), pltpu.VMEM((tq + BLOCK, KV_WIDTH), BF16),
                        pltpu.VMEM((tq, ATTN_WIDTH), F32),
                        pltpu.VMEM((2 * BLOCK, lanes), F32), pltpu.VMEM((2 * BLOCK, lanes), BF16)],
        compiler_params=_params("arbitrary"),
    )(sinks, q, k, k, v, v, bias_t, gain)


def _attn_bwd(dmixed, raw, q, k, v, bias_t, sinks, gain, name, nblk):
    T = q.shape[0]
    tq = nblk * BLOCK
    nt = T // tq
    lanes = N_Q_HEADS * BLOCK

    def body(sink_ref, dm_ref, raw_ref, q_ref, kc_ref, kp_ref, vc_ref, vp_ref, bias_ref, g_ref,
             dq_ref, dk_ref, dv_ref, dbias_ref, dsink_ref, dgain_ref,
             do_sc, dq_sc, kf_sc, vf_sc, dkf_sc, dvf_sc, st_sc, dpt_sc, pt_sc, dst_sc, drow_sc,
             qs_sc, dos_sc, dsink_sc):
        i = pl.program_id(0)
        tile = nt - 1 - i

        @pl.when(i == 0)
        def _():
            dkf_sc[...] = jnp.zeros_like(dkf_sc)
            dvf_sc[...] = jnp.zeros_like(dvf_sc)
            dsink_sc[...] = jnp.zeros_like(dsink_sc)
            dbias_ref[...] = jnp.zeros_like(dbias_ref)
            dgain_ref[...] = jnp.zeros_like(dgain_ref)

        carry_k = dkf_sc[0:BLOCK, :]
        carry_v = dvf_sc[0:BLOCK, :]
        dkf_sc[0:tq, :] = jnp.zeros((tq, KV_WIDTH), F32)
        dvf_sc[0:tq, :] = jnp.zeros((tq, KV_WIDTH), F32)
        dkf_sc[tq:tq + BLOCK, :] = carry_k
        dvf_sc[tq:tq + BLOCK, :] = carry_v
        _load_band(kf_sc, vf_sc, kp_ref, kc_ref, vp_ref, vc_ref, tq)

        do, dgain = _rms_bwd(dm_ref[...].astype(F32), raw_ref[...].astype(F32), g_ref[...])
        dgain_ref[...] += dgain
        do_sc[...] = do
        ones = jnp.ones((SUBLANES, HEAD_DIM), BF16)

        def block(b, carry):
            r0 = pl.multiple_of(b * BLOCK, BLOCK)
            no_prev = (tile == 0) & (b == 0)
            for g in range(N_KV_HEADS):
                kb = kf_sc[pl.ds(r0, 2 * BLOCK), _head_cols(g)]
                vb = vf_sc[pl.ds(r0, 2 * BLOCK), _head_cols(g)]
                qg = _stack_heads(q_ref, r0, g, BF16)
                dog = _stack_heads(do_sc, r0, g, F32)
                prod = dog * _stack_heads(raw_ref, r0, g, F32)
                hi = prod.astype(BF16)
                lo = (prod - hi.astype(F32)).astype(BF16)
                drow_sc[:, _group_lanes(g)] = _dot_nt(ones, hi) + _dot_nt(ones, lo)
                dogb = dog.astype(BF16)
                qs_sc[g] = qg
                dos_sc[g] = dogb
                st_sc[:, _group_lanes(g)] = _dot_nt(kb, qg)
                dpt_sc[:, _group_lanes(g)] = _dot_nt(vb, dogb)
            for h in range(N_Q_HEADS):
                hl = _head_lanes(h)
                p, ps = _head_softmax(st_sc[:, hl], bias_ref[h], sink_ref[h], no_prev)
                rowdot = drow_sc[0:1, hl]
                ds = p * (dpt_sc[:, hl] - rowdot)
                dsink_sc[h:h + 1, :] += -(ps * rowdot)
                dbias_ref[h] += ds
                dst_sc[:, hl] = ds.astype(BF16)
                pt_sc[:, hl] = p.astype(BF16)
            for g in range(N_KV_HEADS):
                kb = kf_sc[pl.ds(r0, 2 * BLOCK), _head_cols(g)]
                dsg = dst_sc[:, _group_lanes(g)]
                _unstack_heads(dq_sc, r0, g, _dot_tn(dsg, kb) * SCALE)
                dkf_sc[pl.ds(r0, 2 * BLOCK), _head_cols(g)] += _dot(dsg, qs_sc[g]) * SCALE
                dvf_sc[pl.ds(r0, 2 * BLOCK), _head_cols(g)] += _dot(pt_sc[:, _group_lanes(g)], dos_sc[g])
            return carry

        lax.fori_loop(0, nblk, block, 0)
        dq_ref[...] = dq_sc[...].astype(BF16)
        dk_ref[...] = dkf_sc[BLOCK:BLOCK + tq, :].astype(BF16)
        dv_ref[...] = dvf_sc[BLOCK:BLOCK + tq, :].astype(BF16)

        @pl.when(i == nt - 1)
        def _():
            tot = jnp.sum(dsink_sc[...], axis=1, keepdims=True)
            dsink_ref[...] = jnp.broadcast_to(tot, dsink_ref.shape)

    cur = lambda i: (nt - 1 - i, 0)
    prev = lambda i: (jnp.maximum((nt - 1 - i) * nblk - 1, 0), 0)
    const2 = lambda i: (0, 0)
    const3 = lambda i: (0, 0, 0)
    return pl.pallas_call(
        body, name=name, grid=(nt,),
        in_specs=[pl.BlockSpec(memory_space=pltpu.SMEM),
                  pl.BlockSpec((tq, ATTN_WIDTH), cur),
                  pl.BlockSpec((tq, ATTN_WIDTH), cur),
                  pl.BlockSpec((tq, ATTN_WIDTH), cur),
                  pl.BlockSpec((tq, KV_WIDTH), cur), pl.BlockSpec((BLOCK, KV_WIDTH), prev),
                  pl.BlockSpec((tq, KV_WIDTH), cur), pl.BlockSpec((BLOCK, KV_WIDTH), prev),
                  pl.BlockSpec(bias_t.shape, const3),
                  pl.BlockSpec((1, ATTN_WIDTH), const2)],
        out_specs=(pl.BlockSpec((tq, ATTN_WIDTH), cur),
                   pl.BlockSpec((tq, KV_WIDTH), cur), pl.BlockSpec((tq, KV_WIDTH), cur),
                   pl.BlockSpec(bias_t.shape, const3),
                   pl.BlockSpec((N_Q_HEADS, 128), const2),
                   pl.BlockSpec((1, ATTN_WIDTH), const2)),
        out_shape=(jax.ShapeDtypeStruct((T, ATTN_WIDTH), BF16),
                   jax.ShapeDtypeStruct((T, KV_WIDTH), BF16), jax.ShapeDtypeStruct((T, KV_WIDTH), BF16),
                   jax.ShapeDtypeStruct(bias_t.shape, F32),
                   jax.ShapeDtypeStruct((N_Q_HEADS, 128), F32),
                   jax.ShapeDtypeStruct((1, ATTN_WIDTH), F32)),
        scratch_shapes=[pltpu.VMEM((tq, ATTN_WIDTH), F32), pltpu.VMEM((tq, ATTN_WIDTH), F32),
                        pltpu.VMEM((tq + BLOCK, KV_WIDTH), BF16), pltpu.VMEM((tq + BLOCK, KV_WIDTH), BF16),
                        pltpu.VMEM((tq + BLOCK, KV_WIDTH), F32), pltpu.VMEM((tq + BLOCK, KV_WIDTH), F32),
                        pltpu.VMEM((2 * BLOCK, lanes), F32), pltpu.VMEM((2 * BLOCK, lanes), F32),
                        pltpu.VMEM((2 * BLOCK, lanes), BF16), pltpu.VMEM((2 * BLOCK, lanes), BF16),
                        pltpu.VMEM((SUBLANES, lanes), F32),
                        pltpu.VMEM((N_KV_HEADS, GROUP_ROWS, HEAD_DIM), BF16),
                        pltpu.VMEM((N_KV_HEADS, GROUP_ROWS, HEAD_DIM), BF16),
                        pltpu.VMEM((N_Q_HEADS, 128), F32)],
        compiler_params=_params("arbitrary"),
    )(sinks, dmixed, raw, q, k, k, v, v, bias_t, gain)


def _shift_down(cu, tail):
    row = lax.broadcasted_iota(jnp.int32, cu.shape, 0)
    t6, t7 = tail[6:7, :], tail[7:8, :]
    s1 = jnp.where(row == 0, t7, pltpu.roll(cu, 1, 0))
    s2 = jnp.where(row == 0, t6, jnp.where(row == 1, t7, pltpu.roll(cu, 2, 0)))
    return s1, s2


def _shift_up(d, head):
    n = d.shape[0]
    row = lax.broadcasted_iota(jnp.int32, d.shape, 0)
    h0, h1 = head[0:1, :], head[1:2, :]
    s1 = jnp.where(row == n - 1, h0, pltpu.roll(d, n - 1, 0))
    s2 = jnp.where(row == n - 1, h1, jnp.where(row == n - 2, h0, pltpu.roll(d, n - 2, 0)))
    return s1, s2


def _mixout_fwd(x, attn_n, u, gb, gc, conv_w, gain, w_out, name, tm):
    T, D = x.shape

    def body(x_ref, an_ref, u_ref, b_ref, c_ref, cw_ref, g_ref, wo_ref, xo_ref, cn_ref, tail_sc):
        @pl.when(pl.program_id(0) == 0)
        def _():
            tail_sc[...] = jnp.zeros_like(tail_sc)

        cu = c_ref[...].astype(F32) * u_ref[...].astype(F32)
        s1, s2 = _shift_down(cu, tail_sc[...])
        tail_sc[...] = cu[tm - SUBLANES:tm, :]
        pre = cw_ref[0:1, :] * s2 + cw_ref[1:2, :] * s1 + cw_ref[2:3, :] * cu
        conv = b_ref[...].astype(F32) * pre
        _, chat = _rms_stats(conv)
        cn = (chat * g_ref[...]).astype(BF16)
        cn_ref[...] = cn
        xo_ref[...] = (x_ref[...] + _dot(an_ref[...], wo_ref[0:ATTN_WIDTH, :])
                       + _dot(cn, wo_ref[ATTN_WIDTH:ATTN_WIDTH + CONV_DIM, :]))

    row = lambda i: (i, 0)
    const = lambda i: (0, 0)
    return pl.pallas_call(
        body, name=name, grid=(T // tm,),
        in_specs=[pl.BlockSpec((tm, D), row), pl.BlockSpec((tm, ATTN_WIDTH), row),
                  pl.BlockSpec((tm, CONV_DIM), row), pl.BlockSpec((tm, CONV_DIM), row),
                  pl.BlockSpec((tm, CONV_DIM), row),
                  pl.BlockSpec(conv_w.shape, const), pl.BlockSpec((1, CONV_DIM), const),
                  pl.BlockSpec(w_out.shape, const)],
        out_specs=(pl.BlockSpec((tm, D), row), pl.BlockSpec((tm, CONV_DIM), row)),
        out_shape=(jax.ShapeDtypeStruct((T, D), F32), jax.ShapeDtypeStruct((T, CONV_DIM), BF16)),
        scratch_shapes=[pltpu.VMEM((SUBLANES, CONV_DIM), F32)],
        compiler_params=_params("arbitrary"),
    )(x, attn_n, u, gb, gc, conv_w, gain, w_out)


def _mixout_bwd(dy, attn_n, conv_n, w_out, after, name, tm):
    T, D = dy.shape
    W = ATTN_WIDTH + CONV_DIM
    nt = T // tm

    def body(dy_ref, an_ref, cn_ref, w_ref, after_ref, dm_ref, dw_ref, dw_sc):
        i = pl.program_id(0)

        @pl.when(i == 0)
        def _():
            dw_sc[...] = jnp.zeros_like(dw_sc)

        dyb = dy_ref[...].astype(BF16)
        dm_ref[...] = _dot_nt(dyb, w_ref[...]).astype(BF16)
        dw_sc[0:ATTN_WIDTH, :] += _dot_tn(an_ref[...], dyb)
        dw_sc[ATTN_WIDTH:W, :] += _dot_tn(cn_ref[...], dyb)

        @pl.when(i == nt - 1)
        def _():
            dw_ref[...] = dw_sc[...].astype(BF16)

    row = lambda i: (i, 0)
    const = lambda i: (0, 0)
    return pl.pallas_call(
        body, name=name, grid=(nt,),
        in_specs=[pl.BlockSpec((tm, D), row), pl.BlockSpec((tm, ATTN_WIDTH), row),
                  pl.BlockSpec((tm, CONV_DIM), row), pl.BlockSpec(w_out.shape, const),
                  pl.BlockSpec(memory_space=pl.ANY)],
        out_specs=(pl.BlockSpec((tm, W), row), pl.BlockSpec((W, D), const)),
        out_shape=(jax.ShapeDtypeStruct((T, W), BF16), jax.ShapeDtypeStruct((W, D), BF16)),
        scratch_shapes=[pltpu.VMEM((W, D), F32)],
        compiler_params=_params("arbitrary"),
    )(dy, attn_n, conv_n, w_out, after)


def _conv_bwd(dmixed, u, gb, gc, conv_w, gain, name, tc):
    T = u.shape[0]
    nt = T // tc
    per8 = tc // SUBLANES

    def body(dm_ref, u_ref, b_ref, c_ref, ut_ref, ct_ref, cw_ref, g_ref,
             du_ref, db_ref, dc_ref, dcw_ref, dgain_ref, head_sc):
        i = pl.program_id(0)

        @pl.when(i == 0)
        def _():
            head_sc[...] = jnp.zeros_like(head_sc)
            dcw_ref[...] = jnp.zeros_like(dcw_ref)
            dgain_ref[...] = jnp.zeros_like(dgain_ref)

        uu = u_ref[...].astype(F32)
        cc = c_ref[...].astype(F32)
        bb = b_ref[...].astype(F32)
        cu = cc * uu
        tail = jnp.where(i == nt - 1, 0.0, ct_ref[...].astype(F32) * ut_ref[...].astype(F32))
        s1, s2 = _shift_down(cu, tail)
        w0, w1, w2 = cw_ref[0:1, :], cw_ref[1:2, :], cw_ref[2:3, :]
        pre = w0 * s2 + w1 * s1 + w2 * cu
        dconv, dgain = _rms_bwd(dm_ref[...].astype(F32), bb * pre, g_ref[...])
        dgain_ref[...] += dgain
        db_ref[...] = (dconv * pre).astype(BF16)
        dpre = dconv * bb
        dcw_ref[0:1, :] += jnp.sum(dpre * s2, axis=0, keepdims=True)
        dcw_ref[1:2, :] += jnp.sum(dpre * s1, axis=0, keepdims=True)
        dcw_ref[2:3, :] += jnp.sum(dpre * cu, axis=0, keepdims=True)
        n1, n2 = _shift_up(dpre, head_sc[...])
        head_sc[...] = dpre[0:SUBLANES, :]
        dcu = w2 * dpre + w1 * n1 + w0 * n2
        du_ref[...] = (dcu * cc).astype(BF16)
        dc_ref[...] = (dcu * uu).astype(BF16)

    rev = lambda i: (nt - 1 - i, 0)
    rev_right = lambda i: (nt - 1 - i, 1)
    tail_map = lambda i: (jnp.maximum((nt - 1 - i) * per8 - 1, 0), 0)
    const = lambda i: (0, 0)
    return pl.pallas_call(
        body, name=name, grid=(nt,),
        in_specs=[pl.BlockSpec((tc, CONV_DIM), rev_right),
                  pl.BlockSpec((tc, CONV_DIM), rev), pl.BlockSpec((tc, CONV_DIM), rev),
                  pl.BlockSpec((tc, CONV_DIM), rev),
                  pl.BlockSpec((SUBLANES, CONV_DIM), tail_map), pl.BlockSpec((SUBLANES, CONV_DIM), tail_map),
                  pl.BlockSpec(conv_w.shape, const), pl.BlockSpec((1, CONV_DIM), const)],
        out_specs=(pl.BlockSpec((tc, CONV_DIM), rev), pl.BlockSpec((tc, CONV_DIM), rev),
                   pl.BlockSpec((tc, CONV_DIM), rev),
                   pl.BlockSpec((SUBLANES, CONV_DIM), const), pl.BlockSpec((1, CONV_DIM), const)),
        out_shape=(jax.ShapeDtypeStruct((T, CONV_DIM), BF16), jax.ShapeDtypeStruct((T, CONV_DIM), BF16),
                   jax.ShapeDtypeStruct((T, CONV_DIM), BF16),
                   jax.ShapeDtypeStruct((SUBLANES, CONV_DIM), F32), jax.ShapeDtypeStruct((1, CONV_DIM), F32)),
        scratch_shapes=[pltpu.VMEM((SUBLANES, CONV_DIM), F32)],
        compiler_params=_params("arbitrary"),
    )(dmixed, u, gb, gc, u, gc, conv_w, gain)


def _mixin_bwd(dy, x, gain, dz, w_in_t, name, tm):
    T, D = x.shape
    nz = len(dz)

    def body(dy_ref, x_ref, g_ref, *rest):
        dz_refs, wt_ref, dx_ref, dgain_ref = rest[:nz], rest[nz], rest[nz + 1], rest[nz + 2]

        @pl.when(pl.program_id(0) == 0)
        def _():
            dgain_ref[...] = jnp.zeros_like(dgain_ref)

        dh = jnp.zeros((tm, D), F32)
        for r, lo, hi in zip(dz_refs, _Z_SPLITS[:-1], _Z_SPLITS[1:]):
            dh += _dot(r[...], wt_ref[lo:hi, :])
        dx, dgain = _rms_bwd(dh, x_ref[...], g_ref[...])
        dgain_ref[...] += dgain
        dx_ref[...] = dy_ref[...] + dx

    row = lambda i: (i, 0)
    const = lambda i: (0, 0)
    return pl.pallas_call(
        body, name=name, grid=(T // tm,),
        in_specs=[pl.BlockSpec((tm, D), row), pl.BlockSpec((tm, D), row), pl.BlockSpec((1, D), const)]
                 + [pl.BlockSpec((tm, a.shape[1]), row) for a in dz]
                 + [pl.BlockSpec(w_in_t.shape, const)],
        out_specs=(pl.BlockSpec((tm, D), row), pl.BlockSpec((1, D), const)),
        out_shape=(jax.ShapeDtypeStruct((T, D), F32), jax.ShapeDtypeStruct((1, D), F32)),
        compiler_params=_params("arbitrary"),
    )(dy, x, gain, *dz, w_in_t)


def _win_grad(dz, hn, name, tk):
    T, D = hn.shape
    nz = len(dz)
    nt = T // tk
    W = _Z_SPLITS[-1]

    def body(hn_ref, *rest):
        dz_refs, dw_ref, dw_sc = rest[:nz], rest[nz], rest[nz + 1]
        i = pl.program_id(0)

        @pl.when(i == 0)
        def _():
            dw_sc[...] = jnp.zeros_like(dw_sc)

        hn = hn_ref[...]
        for r, lo, hi in zip(dz_refs, _Z_SPLITS[:-1], _Z_SPLITS[1:]):
            dw_sc[lo:hi, :] += _dot_tn(r[...], hn)

        @pl.when(i == nt - 1)
        def _():
            dw_ref[...] = dw_sc[...].astype(BF16)

    row = lambda i: (i, 0)
    return pl.pallas_call(
        body, name=name, grid=(nt,),
        in_specs=[pl.BlockSpec((tk, D), row)] + [pl.BlockSpec((tk, a.shape[1]), row) for a in dz],
        out_specs=pl.BlockSpec((W, D), lambda i: (0, 0)),
        out_shape=jax.ShapeDtypeStruct((W, D), BF16),
        scratch_shapes=[pltpu.VMEM((W, D), F32)],
        compiler_params=_params("arbitrary"),
    )(hn, *dz)


def _loss_head(x, target, gain, name, tm):
    T, D = x.shape

    def body(x_ref, t_ref, g_ref, dx_ref, dgain_ref, loss_ref):
        @pl.when(pl.program_id(0) == 0)
        def _():
            dgain_ref[...] = jnp.zeros_like(dgain_ref)
            loss_ref[...] = jnp.zeros_like(loss_ref)

        xv = x_ref[...]
        gain_v = g_ref[...]
        _, xhat = _rms_stats(xv)
        err = xhat * gain_v - t_ref[...]
        part = 0.5 * jnp.sum(jnp.mean(err * err, axis=-1, keepdims=True), axis=0, keepdims=True)
        loss_ref[...] += part
        dx, dgain = _rms_bwd(err * (1.0 / D), xv, gain_v)
        dgain_ref[...] += dgain
        dx_ref[...] = dx

    row = lambda i: (i, 0)
    const = lambda i: (0, 0)
    return pl.pallas_call(
        body, name=name, grid=(T // tm,),
        in_specs=[pl.BlockSpec((tm, D), row), pl.BlockSpec((tm, D), row), pl.BlockSpec((1, D), const)],
        out_specs=(pl.BlockSpec((tm, D), row), pl.BlockSpec((1, D), const),
                   pl.BlockSpec((SUBLANES, 128), const)),
        out_shape=(jax.ShapeDtypeStruct((T, D), F32), jax.ShapeDtypeStruct((1, D), F32),
                   jax.ShapeDtypeStruct((SUBLANES, 128), F32)),
        compiler_params=_params("arbitrary"),
    )(x, target, gain)


def _sum_parts(parts, name, tr):
    P, R, C = parts.shape

    def body(p_ref, o_ref):
        g = p_ref[0].astype(F32)
        for d in range(1, P):
            g = g + p_ref[d].astype(F32)
        o_ref[...] = g

    return pl.pallas_call(
        body, name=name, grid=(R // tr,),
        in_specs=[pl.BlockSpec((P, tr, C), lambda i: (0, i, 0))],
        out_specs=pl.BlockSpec((tr, C), lambda i: (i, 0)),
        out_shape=jax.ShapeDtypeStruct((R, C), F32),
        compiler_params=_params("arbitrary"),
    )(parts)


def _adamw(parts, w, m, v, name, tr):
    P = parts.shape[0]
    R, C = w.shape

    def body(p_ref, w_ref, m_ref, v_ref, g_ref, d_ref, nm_ref, nv_ref):
        g = p_ref[0].astype(F32)
        for d in range(1, P):
            g = g + p_ref[d].astype(F32)
        nm = ADAM_B1 * m_ref[...] + (1.0 - ADAM_B1) * g
        nv = ADAM_B2 * v_ref[...] + (1.0 - ADAM_B2) * (g * g)
        m_hat = nm / (1.0 - ADAM_B1 ** ADAM_STEP)
        v_hat = nv / (1.0 - ADAM_B2 ** ADAM_STEP)
        g_ref[...] = g
        nm_ref[...] = nm
        nv_ref[...] = nv
        d_ref[...] = -ADAM_LR * (m_hat / (jnp.sqrt(v_hat) + ADAM_EPS) + ADAM_WD * w_ref[...])

    row = lambda i: (i, 0)
    spec = pl.BlockSpec((tr, C), row)
    shp = jax.ShapeDtypeStruct((R, C), F32)
    return pl.pallas_call(
        body, name=name, grid=(R // tr,),
        in_specs=[pl.BlockSpec((P, tr, C), lambda i: (0, i, 0)), spec, spec, spec],
        out_specs=(spec, spec, spec, spec),
        out_shape=(shp, shp, shp, shp),
        compiler_params=_params("arbitrary"),
    )(parts, w, m, v)


def _columns_of_blocks(g):
    n, R, w = g.shape
    return g.transpose(1, 0, 2).reshape(R, n * w)


def _pad_row(vec):
    vec = vec.reshape(1, -1)
    return jnp.pad(vec, ((0, 0), (0, PACK_COLS - vec.shape[1])))


def kernel(x, rel_bias_table, ffn1_norm, ffn1_w_gate, ffn1_w_up, ffn1_w_down, mix_norm, w_in, conv_w, attn_sinks, attn_out_norm, conv_out_norm, w_out, ffn2_norm, ffn2_w_gate, ffn2_w_up, ffn2_w_down, final_norm, loss_target, m_rel_bias_table, m_ffn1_norm, m_ffn1_w_gate, m_ffn1_w_up, m_ffn1_w_down, m_mix_norm, m_w_in, m_conv_w, m_attn_sinks, m_attn_out_norm, m_conv_out_norm, m_w_out, m_ffn2_norm, m_ffn2_w_gate, m_ffn2_w_up, m_ffn2_w_down, m_final_norm, v_rel_bias_table, v_ffn1_norm, v_ffn1_w_gate, v_ffn1_w_up, v_ffn1_w_down, v_mix_norm, v_w_in, v_conv_w, v_attn_sinks, v_attn_out_norm, v_conv_out_norm, v_w_out, v_ffn2_norm, v_ffn2_w_gate, v_ffn2_w_up, v_ffn2_w_down, v_final_norm):
    T, D = x.shape[1], x.shape[2]
    x0 = x[0]
    target = loss_target[0]
    tm = min(TM_FFN, T)
    tm_bwd = min(TM_FFN_BWD, T)
    tm_mix = min(TM_MIX, T)
    tk = min(TK_WGRAD, T)
    tf = TF_FFN
    nblk = min(ATTN_BLOCKS, T // BLOCK)
    me = 4 * lax.axis_index("x") + 2 * lax.axis_index("y") + lax.axis_index("c")

    big = {
        "ffn1_w_gate": (ffn1_w_gate[0], m_ffn1_w_gate[0], v_ffn1_w_gate[0], True),
        "ffn1_w_up": (ffn1_w_up[0], m_ffn1_w_up[0], v_ffn1_w_up[0], True),
        "ffn1_w_down": (ffn1_w_down[0], m_ffn1_w_down[0], v_ffn1_w_down[0], False),
        "w_in": (w_in[0], m_w_in[0], v_w_in[0], True),
        "w_out": (w_out[0], m_w_out[0], v_w_out[0], False),
        "ffn2_w_gate": (ffn2_w_gate[0], m_ffn2_w_gate[0], v_ffn2_w_gate[0], True),
        "ffn2_w_up": (ffn2_w_up[0], m_ffn2_w_up[0], v_ffn2_w_up[0], True),
        "ffn2_w_down": (ffn2_w_down[0], m_ffn2_w_down[0], v_ffn2_w_down[0], False),
    }

    def block_to_send(name):
        w, _, _, transposed = big[name]
        return (w.T if transposed else w).astype(BF16)

    names1 = ["ffn1_w_gate", "ffn1_w_up", "ffn1_w_down"]
    names_rest = ["w_in", "w_out", "ffn2_w_gate", "ffn2_w_up", "ffn2_w_down"]
    h1, token = _exchange_start([block_to_send(n) for n in names1], False, "gather_start_ffn1", ffn1_norm)
    h_rest, token = _exchange_start([block_to_send(n) for n in names_rest[:2]] + [conv_w[0]]
                                    + [block_to_send(n) for n in names_rest[2:]], False,
                                    "gather_start_rest", token)
    wgt1, wut1, wd1 = [g.reshape(-1, D) for g in _exchange_wait(h1, False, "gather_wait_ffn1", token)]

    wgu1 = _stack_gate_up(wgt1, wut1, tf)
    x1, xn1, gu1 = _ffn_fwd(x0, ffn1_norm, wgu1, wd1, "ffn1_fwd", tm, tf)
    mixw = _exchange_wait(h_rest[:3], False, "gather_wait_mix", x1)
    win_t = mixw[0].reshape(-1, D)
    wout = mixw[1].reshape(-1, D)
    cw = _columns_of_blocks(mixw[2])
    hn, q, k, v, u, gb, gc = _mixin_fwd(x1, mix_norm, win_t, "mixin_fwd", tm_mix)
    bucket = jnp.asarray(_bucket_table().T.copy())
    sinks = attn_sinks.reshape(-1)
    bias_t = _bias_build(rel_bias_table, bucket, "bias_build")
    attn_raw, attn_n = _attn_fwd(q, k, v, bias_t, sinks, attn_out_norm, "attn_fwd", nblk)
    x2, conv_n = _mixout_fwd(x1, attn_n, u, gb, gc, cw, conv_out_norm, wout, "mixout_fwd", tm_mix)
    wgt2, wut2, wd2 = [g.reshape(-1, D) for g in _exchange_wait(h_rest[3:], False, "gather_wait_ffn2", x2)]
    wgu2 = _stack_gate_up(wgt2, wut2, tf)
    x3, xn2, gu2 = _ffn_fwd(x2, ffn2_norm, wgu2, wd2, "ffn2_fwd", tm, tf)

    def blocks(g):
        return g.reshape(N_DEV, -1, D)

    dx3, d_final, loss_part = _loss_head(x3, target, final_norm.reshape(1, D), "loss_head", tm_mix)
    dx2, d_ffn2_norm, dgu2, hh2, do2 = _ffn_bwd(
        dx3, x2, ffn2_norm, gu2, wgu2, wd2, dx3, "ffn2_bwd", tm_bwd, tf)
    grads2 = _ffn_wgrad(dgu2, hh2, xn2, do2, "ffn2_wgrad", BM_WGRAD, tk, tf)
    handles2, token2 = _exchange_start([blocks(g) for g in grads2], True, "grads_start_ffn2", dx2)

    dmixed, d_wout = _mixout_bwd(dx2, attn_n, conv_n, wout, token2, "mixout_bwd", tm_mix)
    dq, dk, dv, dbias, dsink, d_attn_norm = _attn_bwd(
        dmixed, attn_raw, q, k, v, bias_t, sinks, attn_out_norm, "attn_bwd", nblk)
    du, dgb, dgc, d_cw, d_conv_norm = _conv_bwd(dmixed, u, gb, gc, cw, conv_out_norm, "conv_bwd", tm_mix)
    d_table = _bias_grad(dbias, bucket, "bias_grad")
    dz = [dq, dk, dv, du, dgb, dgc]
    dx1, d_mix_norm = _mixin_bwd(dx2, x1, mix_norm, dz, win_t, "mixin_bwd", tm_mix)
    d_win_t = _win_grad(dz, hn, "win_grad", min(TM_MIX, T))
    handles_mix, token_mix = _exchange_start([blocks(d_win_t), blocks(d_wout)], True, "grads_start_mix", dx1)

    dx0, d_ffn1_norm, dgu1, hh1, do1 = _ffn_bwd(
        dx1, x0, ffn1_norm, gu1, wgu1, wd1, token_mix, "ffn1_bwd", tm_bwd, tf)
    grads1 = _ffn_wgrad(dgu1, hh1, xn1, do1, "ffn1_wgrad", BM_WGRAD, tk, tf)
    handles1, token1 = _exchange_start([blocks(g) for g in grads1], True, "grads_start_ffn1", dx0)

    res = {}

    def update(names, parts):
        last = None
        for name, p in zip(names, parts):
            w, m_, v_, transposed = big[name]
            if transposed:
                summed = _sum_parts(p, "sum_" + name, _row_tile(p.shape[1], ADAM_ROWS))
                p = summed.T[None]
            g, dl, nm, nv = _adamw(p, w, m_, v_, "adamw_" + name, _row_tile(w.shape[0], ADAM_ROWS))
            res[name] = tuple(a[None] for a in (g, dl, nm, nv))
            last = g
        return last

    parts2 = _exchange_wait(handles2, True, "grads_wait_ffn2", token1)
    done2 = update(["ffn2_w_gate", "ffn2_w_up", "ffn2_w_down"], parts2)
    parts_mix = _exchange_wait(handles_mix, True, "grads_wait_mix", done2)
    done_mix = update(["w_in", "w_out"], parts_mix)
    parts1 = _exchange_wait(handles1, True, "grads_wait_ffn1", done_mix)
    update(names1, parts1)

    def pack(ffn1, mixn, ffn2, fin, attn_n_, conv_n_, sink_, extra, convw, table):
        rows = [_pad_row(ffn1), _pad_row(mixn), _pad_row(ffn2), _pad_row(fin),
                _pad_row(jnp.concatenate([attn_n_.reshape(-1), conv_n_.reshape(-1)])),
                _pad_row(sink_), _pad_row(extra),
                jnp.zeros((1, PACK_COLS), F32),
                jnp.pad(convw, ((0, 0), (0, PACK_COLS - convw.shape[1]))),
                _pad_row(table),
                jnp.zeros((PACK_ROWS - 12, PACK_COLS), F32)]
        return jnp.concatenate(rows, axis=0)

    def own_channels(a):
        full = jnp.zeros((a.shape[1], CONV_DIM), F32)
        return lax.dynamic_update_slice(full, a[0], (0, me * a.shape[2]))

    g_pack = pack(d_ffn1_norm, d_mix_norm, d_ffn2_norm, d_final, d_attn_norm, d_conv_norm,
                  dsink[:, 0], loss_part[0, :1], d_cw[:3], d_table[:, :N_Q_HEADS])
    zero1 = jnp.zeros((1,), F32)
    w_pack = pack(ffn1_norm, mix_norm, ffn2_norm, final_norm, attn_out_norm, conv_out_norm,
                  attn_sinks, zero1, own_channels(conv_w), rel_bias_table)
    m_pack = pack(m_ffn1_norm, m_mix_norm, m_ffn2_norm, m_final_norm, m_attn_out_norm, m_conv_out_norm,
                  m_attn_sinks, zero1, own_channels(m_conv_w), m_rel_bias_table)
    v_pack = pack(v_ffn1_norm, v_mix_norm, v_ffn2_norm, v_final_norm, v_attn_out_norm, v_conv_out_norm,
                  v_attn_sinks, zero1, own_channels(v_conv_w), v_rel_bias_table)
    (g_all,) = _exchange([g_pack], False, "gather_small")
    packs = _adamw(g_all, w_pack, m_pack, v_pack, "adamw_small", PACK_ROWS)

    def unpack(pk):
        cwb = lax.dynamic_slice(pk[8:11, :CONV_DIM], (0, me * conv_w.shape[2]), (3, conv_w.shape[2]))
        return {
            "ffn1_norm": pk[0:1, :D], "mix_norm": pk[1:2, :D], "ffn2_norm": pk[2:3, :D],
            "final_norm": pk[3, :D],
            "attn_out_norm": pk[4:5, :ATTN_WIDTH], "conv_out_norm": pk[4:5, ATTN_WIDTH:ATTN_WIDTH + CONV_DIM],
            "attn_sinks": pk[5:6, :N_Q_HEADS],
            "conv_w": cwb[None],
            "rel_bias_table": pk[11, :NUM_BUCKETS * N_Q_HEADS].reshape(NUM_BUCKETS, N_Q_HEADS),
        }

    small = [unpack(pk) for pk in packs]
    loss = packs[0][6, 0]

    order = ["rel_bias_table", "ffn1_norm", "ffn1_w_gate", "ffn1_w_up", "ffn1_w_down", "mix_norm", "w_in",
             "conv_w", "attn_sinks", "attn_out_norm", "conv_out_norm", "w_out", "ffn2_norm",
             "ffn2_w_gate", "ffn2_w_up", "ffn2_w_down", "final_norm"]
    outs = [loss, dx0[None]]
    for kind in range(4):
        for name in order:
            outs.append(res[name][kind] if name in res else small[kind][name])
    return tuple(outs)
```

```python
import math

import numpy as np
import jax
import jax.numpy as jnp
from jax import lax
from jax.experimental import pallas as pl
from jax.experimental.pallas import tpu as pltpu

F32 = jnp.float32
BF16 = jnp.bfloat16

N_DEV = 8
EPS = 1e-6
HEAD_DIM = 64
N_Q_HEADS = 8
N_KV_HEADS = 2
GQA_GROUP = 4
ATTN_WIDTH = 512
KV_WIDTH = 128
CONV_DIM = 512
BLOCK = 128
WINDOW = 128
NUM_BUCKETS = 32
MAX_DISTANCE = 128
SCALE = HEAD_DIM ** -0.5
MASKED = -1e30
GROUP_ROWS = GQA_GROUP * BLOCK

ADAM_LR = 0.001
ADAM_B1 = 0.9
ADAM_B2 = 0.999
ADAM_EPS = 1e-08
ADAM_WD = 0.01
ADAM_STEP = 10

VMEM_LIMIT_BYTES = 40 * 1024 * 1024
SUBLANES = 8
PACK_ROWS = 16
PACK_COLS = 1024

TM_FFN = 1024
TM_FFN_BWD = 1024
TM_MIX = 512
TK_WGRAD = 1024
TF_FFN = 256
BM_WGRAD = 1408
ROW_GROUPS = 4
ATTN_BLOCKS = 4
ADAM_ROWS = 256


def _row_tile(rows, limit):
    best = rows
    for t in range(16, min(rows, limit) + 1, 16):
        if rows % t == 0:
            best = t
    return best


def _params(*sem):
    return pltpu.CompilerParams(dimension_semantics=sem, vmem_limit_bytes=VMEM_LIMIT_BYTES)


def _dot(a, b):
    return jnp.dot(a, b, preferred_element_type=F32)


def _dot_nt(a, b):
    return lax.dot_general(a, b, (((1,), (1,)), ((), ())), preferred_element_type=F32)


def _dot_tn(a, b):
    return lax.dot_general(a, b, (((0,), (0,)), ((), ())), preferred_element_type=F32)


def _sigmoid(g):
    return 0.5 * jnp.tanh(0.5 * g) + 0.5


def _rms_stats(x):
    inv = lax.rsqrt(jnp.mean(x * x, axis=-1, keepdims=True) + EPS)
    return inv, x * inv


def _rms_bwd(dy, x, gain):
    inv, xhat = _rms_stats(x)
    dgain = jnp.sum(dy * xhat, axis=0, keepdims=True)
    dxh = dy * gain
    dx = inv * (dxh - xhat * jnp.mean(dxh * xhat, axis=-1, keepdims=True))
    return dx, dgain


def _peer_list():
    x, y, c = lax.axis_index("x"), lax.axis_index("y"), lax.axis_index("c")
    peers = []
    for k in range(1, N_DEV):
        px = 1 - x if (k >> 2) & 1 else x
        py = 1 - y if (k >> 1) & 1 else y
        pc = 1 - c if k & 1 else c
        peers.append((px, py, pc))
    return 4 * x + 2 * y + c, peers


def _exchange(arrs, scatter, name):
    n = len(arrs)
    out_shape = []
    for a in arrs:
        shp = a.shape if scatter else (N_DEV,) + a.shape
        out_shape.append(jax.ShapeDtypeStruct(shp, a.dtype))

    def body(*refs):
        ins, outs = refs[:n], refs[n:2 * n]
        send_sems, recv_sems, local_sems = refs[2 * n:]
        me, peers = _peer_list()
        started = []
        for a in range(n):
            own = ins[a].at[me] if scatter else ins[a]
            loc = pltpu.make_async_copy(own, outs[a].at[me], local_sems.at[a])
            loc.start()
            started.append(loc)
        sends = []
        for a in range(n):
            for k, (px, py, pc) in enumerate(peers):
                src = ins[a].at[4 * px + 2 * py + pc] if scatter else ins[a]
                cp = pltpu.make_async_remote_copy(
                    src_ref=src, dst_ref=outs[a].at[me],
                    send_sem=send_sems.at[a, k], recv_sem=recv_sems.at[a, k],
                    device_id=(px, py, pc), device_id_type=pl.DeviceIdType.MESH)
                cp.start()
                sends.append(cp)
        for a in range(n):
            for k, (px, py, pc) in enumerate(peers):
                landed = outs[a].at[4 * px + 2 * py + pc]
                pltpu.make_async_remote_copy(
                    src_ref=landed, dst_ref=landed,
                    send_sem=send_sems.at[a, k], recv_sem=recv_sems.at[a, k],
                    device_id=(px, py, pc), device_id_type=pl.DeviceIdType.MESH).wait_recv()
        for cp in sends:
            cp.wait_send()
        for loc in started:
            loc.wait()

    hbm = pl.BlockSpec(memory_space=pl.ANY)
    return pl.pallas_call(
        body, name=name, out_shape=tuple(out_shape),
        in_specs=[hbm] * n, out_specs=tuple([hbm] * n),
        scratch_shapes=[pltpu.SemaphoreType.DMA((n, N_DEV - 1)),
                        pltpu.SemaphoreType.DMA((n, N_DEV - 1)),
                        pltpu.SemaphoreType.DMA((n,))],
    )(*arrs)


_HBM = pl.BlockSpec(memory_space=pltpu.HBM)
_SEM = pl.BlockSpec(memory_space=pltpu.SEMAPHORE)
_EFFECT = pltpu.SideEffectType.DATAFLOW_SIDE_EFFECTING


def _split_copies(srcs, lands, send_sems, recv_sems, scatter):
    me, peers = _peer_list()
    copies = []
    for a in range(len(srcs)):
        for k, (px, py, pc) in enumerate(peers):
            src = srcs[a].at[4 * px + 2 * py + pc] if scatter else srcs[a]
            copies.append(pltpu.make_async_remote_copy(
                src_ref=src, dst_ref=lands[a].at[me],
                send_sem=send_sems[a].at[k], recv_sem=recv_sems[a].at[k],
                device_id=(px, py, pc), device_id_type=pl.DeviceIdType.MESH))
    return copies


def _exchange_start(arrs, scatter, name, after):
    n = len(arrs)
    me = 4 * lax.axis_index("x") + 2 * lax.axis_index("y") + lax.axis_index("c")
    lands = []
    for a in arrs:
        own = lax.dynamic_index_in_dim(a, me, 0, keepdims=True) if scatter else a[None]
        shp = a.shape if scatter else (N_DEV,) + a.shape
        lands.append(lax.dynamic_update_slice(lax.empty(shp, a.dtype), own, (me,) + (0,) * (len(shp) - 1)))

    def body(*refs):
        srcs, lnds = refs[:n], refs[n:2 * n]
        outs = refs[2 * n + 1:]
        send_sems, recv_sems = outs[:n], outs[n:2 * n]
        token = outs[4 * n]
        for cp in _split_copies(srcs, lnds, send_sems, recv_sems, scatter):
            cp.start()
        token[...] = jnp.zeros_like(token)

    sem = pltpu.SemaphoreType.DMA((N_DEV - 1,))
    out_shape = ([sem] * (2 * n) + [pltpu.HBM(a.shape, a.dtype) for a in arrs]
                 + [pltpu.HBM(l.shape, l.dtype) for l in lands] + [jax.ShapeDtypeStruct((SUBLANES, 128), F32)])
    res = pl.pallas_call(
        body, name=name, out_shape=tuple(out_shape),
        in_specs=[_HBM] * (2 * n) + [pl.BlockSpec(memory_space=pl.ANY)],
        out_specs=tuple([_SEM] * (2 * n) + [_HBM] * (2 * n) + [pl.BlockSpec(memory_space=pltpu.VMEM)]),
        input_output_aliases={i: 2 * n + i for i in range(2 * n)},
        compiler_params=pltpu.CompilerParams(has_side_effects=_EFFECT),
    )(*[pltpu.with_memory_space_constraint(a, pltpu.HBM) for a in arrs],
      *[pltpu.with_memory_space_constraint(l, pltpu.HBM) for l in lands], after)
    handles = [(res[2 * n + a], res[3 * n + a], res[a], res[n + a]) for a in range(n)]
    return handles, res[4 * n]


def _exchange_wait(handles, scatter, name, after):
    n = len(handles)

    def body(*refs):
        srcs, lnds = refs[:n], refs[n:2 * n]
        send_sems, recv_sems = refs[2 * n:3 * n], refs[3 * n:4 * n]
        for cp in _split_copies(srcs, lnds, send_sems, recv_sems, scatter):
            cp.wait_send()
            cp.wait_recv()

    srcs = [h[0] for h in handles]
    lands = [h[1] for h in handles]
    res = pl.pallas_call(
        body, name=name,
        out_shape=tuple([pltpu.HBM(a.shape, a.dtype) for a in srcs] + [pltpu.HBM(l.shape, l.dtype) for l in lands]),
        in_specs=[_HBM] * (2 * n) + [_SEM] * (2 * n) + [pl.BlockSpec(memory_space=pl.ANY)],
        out_specs=tuple([_HBM] * (2 * n)),
        input_output_aliases={i: i for i in range(2 * n)},
        compiler_params=pltpu.CompilerParams(has_side_effects=_EFFECT),
    )(*srcs, *lands, *[h[2] for h in handles], *[h[3] for h in handles], after)
    return list(res[n:])


def _stack_gate_up(wgt, wut, tf):
    F, D = wgt.shape
    return jnp.stack([wgt.reshape(F // tf, tf, D), wut.reshape(F // tf, tf, D)], axis=1).reshape(2 * F, D)


def _unstack_gate_up(w, tf):
    F2, D = w.shape
    w4 = w.reshape(F2 // (2 * tf), 2, tf, D)
    return w4[:, 0].reshape(F2 // 2, D), w4[:, 1].reshape(F2 // 2, D)


def _ffn_fwd(x, gain, wgu, wd, name, tm, tf):
    T, D = x.shape
    F = wd.shape[0]
    nj = F // tf

    def body(x_ref, g_ref, wgu_ref, wd_ref, xo_ref, xn_ref, gu_ref, xn_sc, acc_sc):
        j = pl.program_id(1)

        @pl.when(j == 0)
        def _():
            _, xhat = _rms_stats(x_ref[...])
            xn = (xhat * g_ref[...]).astype(BF16)
            xn_sc[...] = xn
            xn_ref[...] = xn
            acc_sc[...] = jnp.zeros_like(acc_sc)

        groups = [slice(r * (tm // ROW_GROUPS), (r + 1) * (tm // ROW_GROUPS)) for r in range(ROW_GROUPS)]
        gus = [_dot_nt(xn_sc[rows, :], wgu_ref[...]) for rows in groups]
        hs = []
        for rows, gu in zip(groups, gus):
            gu_ref[rows, :] = gu.astype(BF16)
            g, u = gu[:, :tf], gu[:, tf:]
            hs.append((g * _sigmoid(g) * u).astype(BF16))
        for rows, h in zip(groups, hs):
            acc_sc[rows, :] += _dot(h, wd_ref[...])

        @pl.when(j == nj - 1)
        def _():
            xo_ref[...] = x_ref[...] + 0.5 * acc_sc[...]

    return pl.pallas_call(
        body, name=name, grid=(T // tm, nj),
        in_specs=[pl.BlockSpec((tm, D), lambda i, j: (i, 0)),
                  pl.BlockSpec((1, D), lambda i, j: (0, 0)),
                  pl.BlockSpec((2 * tf, D), lambda i, j: (j, 0)),
                  pl.BlockSpec((tf, D), lambda i, j: (j, 0))],
        out_specs=(pl.BlockSpec((tm, D), lambda i, j: (i, 0)),
                   pl.BlockSpec((tm, D), lambda i, j: (i, 0)),
                   pl.BlockSpec((tm, 2 * tf), lambda i, j: (i, j))),
        out_shape=(jax.ShapeDtypeStruct((T, D), F32), jax.ShapeDtypeStruct((T, D), BF16),
                   jax.ShapeDtypeStruct((T, 2 * F), BF16)),
        scratch_shapes=[pltpu.VMEM((tm, D), BF16), pltpu.VMEM((tm, D), F32)],
        compiler_params=_params("arbitrary", "arbitrary"),
    )(x, gain, wgu, wd)


def _ffn_bwd(dy, x, gain, gu, wgu, wd, after, name, tm, tf):
    T, D = x.shape
    F = wd.shape[0]
    nj = F // tf

    def body(dy_ref, x_ref, g_ref, gu_ref, wgu_ref, wd_ref, after_ref,
             dx_ref, dgain_ref, dgu_ref, hh_ref, do_ref, do_sc, acc_sc):
        i, j = pl.program_id(0), pl.program_id(1)

        @pl.when((i == 0) & (j == 0))
        def _():
            dgain_ref[...] = jnp.zeros_like(dgain_ref)

        @pl.when(j == 0)
        def _():
            do = (0.5 * dy_ref[...]).astype(BF16)
            do_sc[...] = do
            do_ref[...] = do
            acc_sc[...] = jnp.zeros_like(acc_sc)

        groups = [slice(r * (tm // ROW_GROUPS), (r + 1) * (tm // ROW_GROUPS)) for r in range(ROW_GROUPS)]
        dhs = [_dot_nt(do_sc[rows, :], wd_ref[...]) for rows in groups]
        for rows, dh in zip(groups, dhs):
            g = gu_ref[rows, :tf].astype(F32)
            u = gu_ref[rows, tf:].astype(F32)
            sig = _sigmoid(g)
            s = g * sig
            dgu_ref[rows, :tf] = (dh * u * (sig + s * (1.0 - sig))).astype(BF16)
            dgu_ref[rows, tf:] = (dh * s).astype(BF16)
            hh_ref[rows, :] = (s * u).astype(BF16)
        for rows in groups:
            acc_sc[rows, :] += _dot(dgu_ref[rows, :], wgu_ref[...])

        @pl.when(j == nj - 1)
        def _():
            dx, dgain = _rms_bwd(acc_sc[...], x_ref[...], g_ref[...])
            dgain_ref[...] += dgain
            dx_ref[...] = dy_ref[...] + dx

    tile_spec = pl.BlockSpec((tm, D), lambda i, j: (i, 0), pipeline_mode=pl.Buffered(1))
    return pl.pallas_call(
        body, name=name, grid=(T // tm, nj),
        in_specs=[tile_spec, tile_spec,
                  pl.BlockSpec((1, D), lambda i, j: (0, 0)),
                  pl.BlockSpec((tm, 2 * tf), lambda i, j: (i, j)),
                  pl.BlockSpec((2 * tf, D), lambda i, j: (j, 0)),
                  pl.BlockSpec((tf, D), lambda i, j: (j, 0)),
                  pl.BlockSpec(memory_space=pl.ANY)],
        out_specs=(tile_spec,
                   pl.BlockSpec((1, D), lambda i, j: (0, 0)),
                   pl.BlockSpec((tm, 2 * tf), lambda i, j: (i, j)),
                   pl.BlockSpec((tm, tf), lambda i, j: (i, j)),
                   pl.BlockSpec((tm, D), lambda i, j: (i, 0))),
        out_shape=(jax.ShapeDtypeStruct((T, D), F32), jax.ShapeDtypeStruct((1, D), F32),
                   jax.ShapeDtypeStruct((T, 2 * F), BF16), jax.ShapeDtypeStruct((T, F), BF16),
                   jax.ShapeDtypeStruct((T, D), BF16)),
        scratch_shapes=[pltpu.VMEM((tm, D), BF16), pltpu.VMEM((tm, D), F32)],
        compiler_params=_params("arbitrary", "arbitrary"),
    )(dy, x, gain, gu, wgu, wd, after)


def _tn_grad(a, b, name, bm, tk):
    T, M = a.shape
    D = b.shape[1]
    nk = T // tk

    def body(a_ref, b_ref, o_ref, acc_sc):
        k = pl.program_id(1)

        @pl.when(k == 0)
        def _():
            acc_sc[...] = jnp.zeros_like(acc_sc)

        acc_sc[...] += _dot_tn(a_ref[...], b_ref[...])

        @pl.when(k == nk - 1)
        def _():
            o_ref[...] = acc_sc[...].astype(BF16)

    return pl.pallas_call(
        body, name=name, grid=(M // bm, nk),
        in_specs=[pl.BlockSpec((tk, bm), lambda i, k: (k, i)), pl.BlockSpec((tk, D), lambda i, k: (k, 0))],
        out_specs=pl.BlockSpec((bm, D), lambda i, k: (i, 0)),
        out_shape=jax.ShapeDtypeStruct((M, D), BF16),
        scratch_shapes=[pltpu.VMEM((bm, D), F32)],
        compiler_params=_params("arbitrary", "arbitrary"),
    )(a, b)


def _ffn_wgrad(dgu, hh, xn, do, name, bm, tk, tf):
    d_gate, d_up = _unstack_gate_up(_tn_grad(dgu, xn, name + "_gu", bm, tk), tf)
    return [d_gate, d_up, _tn_grad(hh, do, name + "_down", bm, tk)]


_Z_SPLITS = (0, 512, 640, 768, 1280, 1792, 2304)


def _mixin_fwd(x, gain, w_in_t, name, tm):
    T, D = x.shape
    widths = [b - a for a, b in zip(_Z_SPLITS[:-1], _Z_SPLITS[1:])]

    def body(x_ref, g_ref, w_ref, hn_ref, *outs):
        _, xhat = _rms_stats(x_ref[...])
        hn = (xhat * g_ref[...]).astype(BF16)
        hn_ref[...] = hn
        for o_ref, lo, hi in zip(outs, _Z_SPLITS[:-1], _Z_SPLITS[1:]):
            o_ref[...] = _dot_nt(hn, w_ref[lo:hi, :]).astype(BF16)

    return pl.pallas_call(
        body, name=name, grid=(T // tm,),
        in_specs=[pl.BlockSpec((tm, D), lambda i: (i, 0)),
                  pl.BlockSpec((1, D), lambda i: (0, 0)),
                  pl.BlockSpec(w_in_t.shape, lambda i: (0, 0))],
        out_specs=tuple([pl.BlockSpec((tm, D), lambda i: (i, 0))]
                        + [pl.BlockSpec((tm, w), lambda i: (i, 0)) for w in widths]),
        out_shape=tuple([jax.ShapeDtypeStruct((T, D), BF16)]
                        + [jax.ShapeDtypeStruct((T, w), BF16) for w in widths]),
        compiler_params=_params("arbitrary"),
    )(x, gain, w_in_t)


def _bucket_table():
    qi = np.arange(BLOCK, dtype=np.int32)[:, None]
    kj = np.arange(2 * BLOCK, dtype=np.int32)[None, :]
    dist = qi + BLOCK - kj
    n = np.maximum(dist, 0)
    max_exact = NUM_BUCKETS // 2
    large = max_exact + (np.log(np.maximum(n, 1).astype(np.float32) / max_exact)
                         / math.log(MAX_DISTANCE / max_exact)
                         * (NUM_BUCKETS - max_exact)).astype(np.int32)
    large = np.minimum(large, NUM_BUCKETS - 1)
    bucket = np.where(n < max_exact, n, large).astype(np.int32)
    valid = (dist >= 0) & (dist < WINDOW)
    return np.where(valid, bucket, -1).astype(np.int32)


def _bias_build(table, bucket, name):
    def body(t_ref, b_ref, o_ref):
        bk = b_ref[...]
        for h in range(N_Q_HEADS):
            def step(b, acc):
                return jnp.where(bk == b, t_ref[b, h], acc)
            o_ref[h] = lax.fori_loop(0, NUM_BUCKETS, step, jnp.full(bk.shape, MASKED, F32))

    return pl.pallas_call(
        body, name=name,
        in_specs=[pl.BlockSpec(memory_space=pltpu.SMEM), pl.BlockSpec(memory_space=pltpu.VMEM)],
        out_specs=pl.BlockSpec(memory_space=pltpu.VMEM),
        out_shape=jax.ShapeDtypeStruct((N_Q_HEADS,) + bucket.shape, F32),
    )(table, bucket)


def _bias_grad(dbias, bucket, name):
    def body(d_ref, b_ref, o_ref):
        bk = b_ref[...]
        row = lax.broadcasted_iota(jnp.int32, o_ref.shape, 0)
        lane = lax.broadcasted_iota(jnp.int32, o_ref.shape, 1)
        res = jnp.zeros(o_ref.shape, F32)
        for h in range(N_Q_HEADS):
            d = d_ref[h]

            def step(b, acc):
                tot = jnp.sum(jnp.where(bk == b, d, 0.0), axis=1, keepdims=True)
                tot = jnp.sum(tot, axis=0, keepdims=True)
                return jnp.where((row == b) & (lane == h), tot, acc)
            res = lax.fori_loop(0, NUM_BUCKETS, step, res)
        o_ref[...] = res

    return pl.pallas_call(
        body, name=name,
        in_specs=[pl.BlockSpec(memory_space=pltpu.VMEM), pl.BlockSpec(memory_space=pltpu.VMEM)],
        out_specs=pl.BlockSpec(memory_space=pltpu.VMEM),
        out_shape=jax.ShapeDtypeStruct((NUM_BUCKETS, 128), F32),
    )(dbias, bucket)


def _head_cols(h):
    return slice(h * HEAD_DIM, (h + 1) * HEAD_DIM)


def _stack_heads(ref, r0, g, dtype):
    return jnp.concatenate(
        [ref[pl.ds(r0, BLOCK), _head_cols(GQA_GROUP * g + j)].astype(dtype) for j in range(GQA_GROUP)], axis=0)


def _unstack_heads(ref, r0, g, val):
    for j in range(GQA_GROUP):
        ref[pl.ds(r0, BLOCK), _head_cols(GQA_GROUP * g + j)] = val[j * BLOCK:(j + 1) * BLOCK, :]


def _head_lanes(h):
    return slice(h * BLOCK, (h + 1) * BLOCK)


def _group_lanes(g):
    return slice(g * GROUP_ROWS, (g + 1) * GROUP_ROWS)


def _head_softmax(st, bias_t, sink, no_prev):
    s = st * SCALE + bias_t
    row = lax.broadcasted_iota(jnp.int32, s.shape, 0)
    s = jnp.where(no_prev & (row < BLOCK), MASKED, s)
    m = jnp.maximum(jnp.max(s, axis=0, keepdims=True), sink)
    p = jnp.exp(s - m)
    ps = jnp.exp(sink - m)
    r = 1.0 / (jnp.sum(p, axis=0, keepdims=True) + ps)
    return p * r, ps * r


def _load_band(kf_sc, vf_sc, kp_ref, kc_ref, vp_ref, vc_ref, tq):
    kf_sc[0:BLOCK, :] = kp_ref[...]
    kf_sc[BLOCK:BLOCK + tq, :] = kc_ref[...]
    vf_sc[0:BLOCK, :] = vp_ref[...]
    vf_sc[BLOCK:BLOCK + tq, :] = vc_ref[...]


def _attn_fwd(q, k, v, bias_t, sinks, gain, name, nblk):
    T = q.shape[0]
    tq = nblk * BLOCK

    def body(sink_ref, q_ref, kc_ref, kp_ref, vc_ref, vp_ref, bias_ref, g_ref, raw_ref, nrm_ref,
             kf_sc, vf_sc, o_sc, st_sc, pt_sc):
        i = pl.program_id(0)
        _load_band(kf_sc, vf_sc, kp_ref, kc_ref, vp_ref, vc_ref, tq)

        def block(b, carry):
            r0 = pl.multiple_of(b * BLOCK, BLOCK)
            no_prev = (i == 0) & (b == 0)
            for g in range(N_KV_HEADS):
                kb = kf_sc[pl.ds(r0, 2 * BLOCK), _head_cols(g)]
                st_sc[:, _group_lanes(g)] = _dot_nt(kb, _stack_heads(q_ref, r0, g, BF16))
            for h in range(N_Q_HEADS):
                p, _ = _head_softmax(st_sc[:, _head_lanes(h)], bias_ref[h], sink_ref[h], no_prev)
                pt_sc[:, _head_lanes(h)] = p.astype(BF16)
            for g in range(N_KV_HEADS):
                vb = vf_sc[pl.ds(r0, 2 * BLOCK), _head_cols(g)]
                _unstack_heads(o_sc, r0, g, _dot_tn(pt_sc[:, _group_lanes(g)], vb))
            return carry

        lax.fori_loop(0, nblk, block, 0)
        o = o_sc[...]
        raw_ref[...] = o.astype(BF16)
        _, ohat = _rms_stats(o)
        nrm_ref[...] = (ohat * g_ref[...]).astype(BF16)

    cur = lambda i: (i, 0)
    prev = lambda i: (jnp.maximum(i * nblk - 1, 0), 0)
    lanes = N_Q_HEADS * BLOCK
    return pl.pallas_call(
        body, name=name, grid=(T // tq,),
        in_specs=[pl.BlockSpec(memory_space=pltpu.SMEM),
                  pl.BlockSpec((tq, ATTN_WIDTH), cur),
                  pl.BlockSpec((tq, KV_WIDTH), cur), pl.BlockSpec((BLOCK, KV_WIDTH), prev),
                  pl.BlockSpec((tq, KV_WIDTH), cur), pl.BlockSpec((BLOCK, KV_WIDTH), prev),
                  pl.BlockSpec(bias_t.shape, lambda i: (0, 0, 0)),
                  pl.BlockSpec((1, ATTN_WIDTH), lambda i: (0, 0))],
        out_specs=(pl.BlockSpec((tq, ATTN_WIDTH), cur), pl.BlockSpec((tq, ATTN_WIDTH), cur)),
        out_shape=(jax.ShapeDtypeStruct((T, ATTN_WIDTH), BF16), jax.ShapeDtypeStruct((T, ATTN_WIDTH), BF16)),
        scratch_shapes=[pltpu.VMEM((tq + BLOCK, KV_WIDTH), BF16), pltpu.VMEM((tq + BLOCK, KV_WIDTH), BF16),
                        pltpu.VMEM((tq, ATTN_WIDTH), F32),
                        pltpu.VMEM((2 * BLOCK, lanes), F32), pltpu.VMEM((2 * BLOCK, lanes), BF16)],
        compiler_params=_params("arbitrary"),
    )(sinks, q, k, k, v, v, bias_t, gain)


def _attn_bwd(dmixed, raw, q, k, v, bias_t, sinks, gain, name, nblk):
    T = q.shape[0]
    tq = nblk * BLOCK
    nt = T // tq
    lanes = N_Q_HEADS * BLOCK

    def body(sink_ref, dm_ref, raw_ref, q_ref, kc_ref, kp_ref, vc_ref, vp_ref, bias_ref, g_ref,
             dq_ref, dk_ref, dv_ref, dbias_ref, dsink_ref, dgain_ref,
             do_sc, dq_sc, kf_sc, vf_sc, dkf_sc, dvf_sc, st_sc, dpt_sc, pt_sc, dst_sc, drow_sc,
             qs_sc, dos_sc, dsink_sc):
        i = pl.program_id(0)
        tile = nt - 1 - i

        @pl.when(i == 0)
        def _():
            dkf_sc[...] = jnp.zeros_like(dkf_sc)
            dvf_sc[...] = jnp.zeros_like(dvf_sc)
            dsink_sc[...] = jnp.zeros_like(dsink_sc)
            dbias_ref[...] = jnp.zeros_like(dbias_ref)
            dgain_ref[...] = jnp.zeros_like(dgain_ref)

        carry_k = dkf_sc[0:BLOCK, :]
        carry_v = dvf_sc[0:BLOCK, :]
        dkf_sc[0:tq, :] = jnp.zeros((tq, KV_WIDTH), F32)
        dvf_sc[0:tq, :] = jnp.zeros((tq, KV_WIDTH), F32)
        dkf_sc[tq:tq + BLOCK, :] = carry_k
        dvf_sc[tq:tq + BLOCK, :] = carry_v
        _load_band(kf_sc, vf_sc, kp_ref, kc_ref, vp_ref, vc_ref, tq)

        do, dgain = _rms_bwd(dm_ref[...].astype(F32), raw_ref[...].astype(F32), g_ref[...])
        dgain_ref[...] += dgain
        do_sc[...] = do
        ones = jnp.ones((SUBLANES, HEAD_DIM), BF16)

        def block(b, carry):
            r0 = pl.multiple_of(b * BLOCK, BLOCK)
            no_prev = (tile == 0) & (b == 0)
            for g in range(N_KV_HEADS):
                kb = kf_sc[pl.ds(r0, 2 * BLOCK), _head_cols(g)]
                vb = vf_sc[pl.ds(r0, 2 * BLOCK), _head_cols(g)]
                qg = _stack_heads(q_ref, r0, g, BF16)
                dog = _stack_heads(do_sc, r0, g, F32)
                prod = dog * _stack_heads(raw_ref, r0, g, F32)
                hi = prod.astype(BF16)
                lo = (prod - hi.astype(F32)).astype(BF16)
                drow_sc[:, _group_lanes(g)] = _dot_nt(ones, hi) + _dot_nt(ones, lo)
                dogb = dog.astype(BF16)
                qs_sc[g] = qg
                dos_sc[g] = dogb
                st_sc[:, _group_lanes(g)] = _dot_nt(kb, qg)
                dpt_sc[:, _group_lanes(g)] = _dot_nt(vb, dogb)
            for h in range(N_Q_HEADS):
                hl = _head_lanes(h)
                p, ps = _head_softmax(st_sc[:, hl], bias_ref[h], sink_ref[h], no_prev)
                rowdot = drow_sc[0:1, hl]
                ds = p * (dpt_sc[:, hl] - rowdot)
                dsink_sc[h:h + 1, :] += -(ps * rowdot)
                dbias_ref[h] += ds
                dst_sc[:, hl] = ds.astype(BF16)
                pt_sc[:, hl] = p.astype(BF16)
            for g in range(N_KV_HEADS):
                kb = kf_sc[pl.ds(r0, 2 * BLOCK), _head_cols(g)]
                dsg = dst_sc[:, _group_lanes(g)]
                _unstack_heads(dq_sc, r0, g, _dot_tn(dsg, kb) * SCALE)
                dkf_sc[pl.ds(r0, 2 * BLOCK), _head_cols(g)] += _dot(dsg, qs_sc[g]) * SCALE
                dvf_sc[pl.ds(r0, 2 * BLOCK), _head_cols(g)] += _dot(pt_sc[:, _group_lanes(g)], dos_sc[g])
            return carry

        lax.fori_loop(0, nblk, block, 0)
        dq_ref[...] = dq_sc[...].astype(BF16)
        dk_ref[...] = dkf_sc[BLOCK:BLOCK + tq, :].astype(BF16)
        dv_ref[...] = dvf_sc[BLOCK:BLOCK + tq, :].astype(BF16)

        @pl.when(i == nt - 1)
        def _():
            tot = jnp.sum(dsink_sc[...], axis=1, keepdims=True)
            dsink_ref[...] = jnp.broadcast_to(tot, dsink_ref.shape)

    cur = lambda i: (nt - 1 - i, 0)
    prev = lambda i: (jnp.maximum((nt - 1 - i) * nblk - 1, 0), 0)
    const2 = lambda i: (0, 0)
    const3 = lambda i: (0, 0, 0)
    return pl.pallas_call(
        body, name=name, grid=(nt,),
        in_specs=[pl.BlockSpec(memory_space=pltpu.SMEM),
                  pl.BlockSpec((tq, ATTN_WIDTH), cur),
                  pl.BlockSpec((tq, ATTN_WIDTH), cur),
                  pl.BlockSpec((tq, ATTN_WIDTH), cur),
                  pl.BlockSpec((tq, KV_WIDTH), cur), pl.BlockSpec((BLOCK, KV_WIDTH), prev),
                  pl.BlockSpec((tq, KV_WIDTH), cur), pl.BlockSpec((BLOCK, KV_WIDTH), prev),
                  pl.BlockSpec(bias_t.shape, const3),
                  pl.BlockSpec((1, ATTN_WIDTH), const2)],
        out_specs=(pl.BlockSpec((tq, ATTN_WIDTH), cur),
                   pl.BlockSpec((tq, KV_WIDTH), cur), pl.BlockSpec((tq, KV_WIDTH), cur),
                   pl.BlockSpec(bias_t.shape, const3),
                   pl.BlockSpec((N_Q_HEADS, 128), const2),
                   pl.BlockSpec((1, ATTN_WIDTH), const2)),
        out_shape=(jax.ShapeDtypeStruct((T, ATTN_WIDTH), BF16),
                   jax.ShapeDtypeStruct((T, KV_WIDTH), BF16), jax.ShapeDtypeStruct((T, KV_WIDTH), BF16),
                   jax.ShapeDtypeStruct(bias_t.shape, F32),
                   jax.ShapeDtypeStruct((N_Q_HEADS, 128), F32),
                   jax.ShapeDtypeStruct((1, ATTN_WIDTH), F32)),
        scratch_shapes=[pltpu.VMEM((tq, ATTN_WIDTH), F32), pltpu.VMEM((tq, ATTN_WIDTH), F32),
                        pltpu.VMEM((tq + BLOCK, KV_WIDTH), BF16), pltpu.VMEM((tq + BLOCK, KV_WIDTH), BF16),
                        pltpu.VMEM((tq + BLOCK, KV_WIDTH), F32), pltpu.VMEM((tq + BLOCK, KV_WIDTH), F32),
                        pltpu.VMEM((2 * BLOCK, lanes), F32), pltpu.VMEM((2 * BLOCK, lanes), F32),
                        pltpu.VMEM((2 * BLOCK, lanes), BF16), pltpu.VMEM((2 * BLOCK, lanes), BF16),
                        pltpu.VMEM((SUBLANES, lanes), F32),
                        pltpu.VMEM((N_KV_HEADS, GROUP_ROWS, HEAD_DIM), BF16),
                        pltpu.VMEM((N_KV_HEADS, GROUP_ROWS, HEAD_DIM), BF16),
                        pltpu.VMEM((N_Q_HEADS, 128), F32)],
        compiler_params=_params("arbitrary"),
    )(sinks, dmixed, raw, q, k, k, v, v, bias_t, gain)


def _shift_down(cu, tail):
    row = lax.broadcasted_iota(jnp.int32, cu.shape, 0)
    t6, t7 = tail[6:7, :], tail[7:8, :]
    s1 = jnp.where(row == 0, t7, pltpu.roll(cu, 1, 0))
    s2 = jnp.where(row == 0, t6, jnp.where(row == 1, t7, pltpu.roll(cu, 2, 0)))
    return s1, s2


def _shift_up(d, head):
    n = d.shape[0]
    row = lax.broadcasted_iota(jnp.int32, d.shape, 0)
    h0, h1 = head[0:1, :], head[1:2, :]
    s1 = jnp.where(row == n - 1, h0, pltpu.roll(d, n - 1, 0))
    s2 = jnp.where(row == n - 1, h1, jnp.where(row == n - 2, h0, pltpu.roll(d, n - 2, 0)))
    return s1, s2


def _mixout_fwd(x, attn_n, u, gb, gc, conv_w, gain, w_out, name, tm):
    T, D = x.shape

    def body(x_ref, an_ref, u_ref, b_ref, c_ref, cw_ref, g_ref, wo_ref, xo_ref, cn_ref, tail_sc):
        @pl.when(pl.program_id(0) == 0)
        def _():
            tail_sc[...] = jnp.zeros_like(tail_sc)

        cu = c_ref[...].astype(F32) * u_ref[...].astype(F32)
        s1, s2 = _shift_down(cu, tail_sc[...])
        tail_sc[...] = cu[tm - SUBLANES:tm, :]
        pre = cw_ref[0:1, :] * s2 + cw_ref[1:2, :] * s1 + cw_ref[2:3, :] * cu
        conv = b_ref[...].astype(F32) * pre
        _, chat = _rms_stats(conv)
        cn = (chat * g_ref[...]).astype(BF16)
        cn_ref[...] = cn
        xo_ref[...] = (x_ref[...] + _dot(an_ref[...], wo_ref[0:ATTN_WIDTH, :])
                       + _dot(cn, wo_ref[ATTN_WIDTH:ATTN_WIDTH + CONV_DIM, :]))

    row = lambda i: (i, 0)
    const = lambda i: (0, 0)
    return pl.pallas_call(
        body, name=name, grid=(T // tm,),
        in_specs=[pl.BlockSpec((tm, D), row), pl.BlockSpec((tm, ATTN_WIDTH), row),
                  pl.BlockSpec((tm, CONV_DIM), row), pl.BlockSpec((tm, CONV_DIM), row),
                  pl.BlockSpec((tm, CONV_DIM), row),
                  pl.BlockSpec(conv_w.shape, const), pl.BlockSpec((1, CONV_DIM), const),
                  pl.BlockSpec(w_out.shape, const)],
        out_specs=(pl.BlockSpec((tm, D), row), pl.BlockSpec((tm, CONV_DIM), row)),
        out_shape=(jax.ShapeDtypeStruct((T, D), F32), jax.ShapeDtypeStruct((T, CONV_DIM), BF16)),
        scratch_shapes=[pltpu.VMEM((SUBLANES, CONV_DIM), F32)],
        compiler_params=_params("arbitrary"),
    )(x, attn_n, u, gb, gc, conv_w, gain, w_out)


def _mixout_bwd(dy, attn_n, conv_n, w_out, after, name, tm):
    T, D = dy.shape
    W = ATTN_WIDTH + CONV_DIM
    nt = T // tm

    def body(dy_ref, an_ref, cn_ref, w_ref, after_ref, dm_ref, dw_ref, dw_sc):
        i = pl.program_id(0)

        @pl.when(i == 0)
        def _():
            dw_sc[...] = jnp.zeros_like(dw_sc)

        dyb = dy_ref[...].astype(BF16)
        dm_ref[...] = _dot_nt(dyb, w_ref[...]).astype(BF16)
        dw_sc[0:ATTN_WIDTH, :] += _dot_tn(an_ref[...], dyb)
        dw_sc[ATTN_WIDTH:W, :] += _dot_tn(cn_ref[...], dyb)

        @pl.when(i == nt - 1)
        def _():
            dw_ref[...] = dw_sc[...].astype(BF16)

    row = lambda i: (i, 0)
    const = lambda i: (0, 0)
    return pl.pallas_call(
        body, name=name, grid=(nt,),
        in_specs=[pl.BlockSpec((tm, D), row), pl.BlockSpec((tm, ATTN_WIDTH), row),
                  pl.BlockSpec((tm, CONV_DIM), row), pl.BlockSpec(w_out.shape, const),
                  pl.BlockSpec(memory_space=pl.ANY)],
        out_specs=(pl.BlockSpec((tm, W), row), pl.BlockSpec((W, D), const)),
        out_shape=(jax.ShapeDtypeStruct((T, W), BF16), jax.ShapeDtypeStruct((W, D), BF16)),
        scratch_shapes=[pltpu.VMEM((W, D), F32)],
        compiler_params=_params("arbitrary"),
    )(dy, attn_n, conv_n, w_out, after)


def _conv_bwd(dmixed, u, gb, gc, conv_w, gain, name, tc):
    T = u.shape[0]
    nt = T // tc
    per8 = tc // SUBLANES

    def body(dm_ref, u_ref, b_ref, c_ref, ut_ref, ct_ref, cw_ref, g_ref,
             du_ref, db_ref, dc_ref, dcw_ref, dgain_ref, head_sc):
        i = pl.program_id(0)

        @pl.when(i == 0)
        def _():
            head_sc[...] = jnp.zeros_like(head_sc)
            dcw_ref[...] = jnp.zeros_like(dcw_ref)
            dgain_ref[...] = jnp.zeros_like(dgain_ref)

        uu = u_ref[...].astype(F32)
        cc = c_ref[...].astype(F32)
        bb = b_ref[...].astype(F32)
        cu = cc * uu
        tail = jnp.where(i == nt - 1, 0.0, ct_ref[...].astype(F32) * ut_ref[...].astype(F32))
        s1, s2 = _shift_down(cu, tail)
        w0, w1, w2 = cw_ref[0:1, :], cw_ref[1:2, :], cw_ref[2:3, :]
        pre = w0 * s2 + w1 * s1 + w2 * cu
        dconv, dgain = _rms_bwd(dm_ref[...].astype(F32), bb * pre, g_ref[...])
        dgain_ref[...] += dgain
        db_ref[...] = (dconv * pre).astype(BF16)
        dpre = dconv * bb
        dcw_ref[0:1, :] += jnp.sum(dpre * s2, axis=0, keepdims=True)
        dcw_ref[1:2, :] += jnp.sum(dpre * s1, axis=0, keepdims=True)
        dcw_ref[2:3, :] += jnp.sum(dpre * cu, axis=0, keepdims=True)
        n1, n2 = _shift_up(dpre, head_sc[...])
        head_sc[...] = dpre[0:SUBLANES, :]
        dcu = w2 * dpre + w1 * n1 + w0 * n2
        du_ref[...] = (dcu * cc).astype(BF16)
        dc_ref[...] = (dcu * uu).astype(BF16)

    rev = lambda i: (nt - 1 - i, 0)
    rev_right = lambda i: (nt - 1 - i, 1)
    tail_map = lambda i: (jnp.maximum((nt - 1 - i) * per8 - 1, 0), 0)
    const = lambda i: (0, 0)
    return pl.pallas_call(
        body, name=name, grid=(nt,),
        in_specs=[pl.BlockSpec((tc, CONV_DIM), rev_right),
                  pl.BlockSpec((tc, CONV_DIM), rev), pl.BlockSpec((tc, CONV_DIM), rev),
                  pl.BlockSpec((tc, CONV_DIM), rev),
                  pl.BlockSpec((SUBLANES, CONV_DIM), tail_map), pl.BlockSpec((SUBLANES, CONV_DIM), tail_map),
                  pl.BlockSpec(conv_w.shape, const), pl.BlockSpec((1, CONV_DIM), const)],
        out_specs=(pl.BlockSpec((tc, CONV_DIM), rev), pl.BlockSpec((tc, CONV_DIM), rev),
                   pl.BlockSpec((tc, CONV_DIM), rev),
                   pl.BlockSpec((SUBLANES, CONV_DIM), const), pl.BlockSpec((1, CONV_DIM), const)),
        out_shape=(jax.ShapeDtypeStruct((T, CONV_DIM), BF16), jax.ShapeDtypeStruct((T, CONV_DIM), BF16),
                   jax.ShapeDtypeStruct((T, CONV_DIM), BF16),
                   jax.ShapeDtypeStruct((SUBLANES, CONV_DIM), F32), jax.ShapeDtypeStruct((1, CONV_DIM), F32)),
        scratch_shapes=[pltpu.VMEM((SUBLANES, CONV_DIM), F32)],
        compiler_params=_params("arbitrary"),
    )(dmixed, u, gb, gc, u, gc, conv_w, gain)


def _mixin_bwd(dy, x, gain, dz, w_in_t, name, tm):
    T, D = x.shape
    nz = len(dz)

    def body(dy_ref, x_ref, g_ref, *rest):
        dz_refs, wt_ref, dx_ref, dgain_ref = rest[:nz], rest[nz], rest[nz + 1], rest[nz + 2]

        @pl.when(pl.program_id(0) == 0)
        def _():
            dgain_ref[...] = jnp.zeros_like(dgain_ref)

        dh = jnp.zeros((tm, D), F32)
        for r, lo, hi in zip(dz_refs, _Z_SPLITS[:-1], _Z_SPLITS[1:]):
            dh += _dot(r[...], wt_ref[lo:hi, :])
        dx, dgain = _rms_bwd(dh, x_ref[...], g_ref[...])
        dgain_ref[...] += dgain
        dx_ref[...] = dy_ref[...] + dx

    row = lambda i: (i, 0)
    const = lambda i: (0, 0)
    return pl.pallas_call(
        body, name=name, grid=(T // tm,),
        in_specs=[pl.BlockSpec((tm, D), row), pl.BlockSpec((tm, D), row), pl.BlockSpec((1, D), const)]
                 + [pl.BlockSpec((tm, a.shape[1]), row) for a in dz]
                 + [pl.BlockSpec(w_in_t.shape, const)],
        out_specs=(pl.BlockSpec((tm, D), row), pl.BlockSpec((1, D), const)),
        out_shape=(jax.ShapeDtypeStruct((T, D), F32), jax.ShapeDtypeStruct((1, D), F32)),
        compiler_params=_params("arbitrary"),
    )(dy, x, gain, *dz, w_in_t)


def _win_grad(dz, hn, name, tk):
    T, D = hn.shape
    nz = len(dz)
    nt = T // tk
    W = _Z_SPLITS[-1]

    def body(hn_ref, *rest):
        dz_refs, dw_ref, dw_sc = rest[:nz], rest[nz], rest[nz + 1]
        i = pl.program_id(0)

        @pl.when(i == 0)
        def _():
            dw_sc[...] = jnp.zeros_like(dw_sc)

        hn = hn_ref[...]
        for r, lo, hi in zip(dz_refs, _Z_SPLITS[:-1], _Z_SPLITS[1:]):
            dw_sc[lo:hi, :] += _dot_tn(r[...], hn)

        @pl.when(i == nt - 1)
        def _():
            dw_ref[...] = dw_sc[...].astype(BF16)

    row = lambda i: (i, 0)
    return pl.pallas_call(
        body, name=name, grid=(nt,),
        in_specs=[pl.BlockSpec((tk, D), row)] + [pl.BlockSpec((tk, a.shape[1]), row) for a in dz],
        out_specs=pl.BlockSpec((W, D), lambda i: (0, 0)),
        out_shape=jax.ShapeDtypeStruct((W, D), BF16),
        scratch_shapes=[pltpu.VMEM((W, D), F32)],
        compiler_params=_params("arbitrary"),
    )(hn, *dz)


def _loss_head(x, target, gain, name, tm):
    T, D = x.shape

    def body(x_ref, t_ref, g_ref, dx_ref, dgain_ref, loss_ref):
        @pl.when(pl.program_id(0) == 0)
        def _():
            dgain_ref[...] = jnp.zeros_like(dgain_ref)
            loss_ref[...] = jnp.zeros_like(loss_ref)

        xv = x_ref[...]
        gain_v = g_ref[...]
        _, xhat = _rms_stats(xv)
        err = xhat * gain_v - t_ref[...]
        part = 0.5 * jnp.sum(jnp.mean(err * err, axis=-1, keepdims=True), axis=0, keepdims=True)
        loss_ref[...] += part
        dx, dgain = _rms_bwd(err * (1.0 / D), xv, gain_v)
        dgain_ref[...] += dgain
        dx_ref[...] = dx

    row = lambda i: (i, 0)
    const = lambda i: (0, 0)
    return pl.pallas_call(
        body, name=name, grid=(T // tm,),
        in_specs=[pl.BlockSpec((tm, D), row), pl.BlockSpec((tm, D), row), pl.BlockSpec((1, D), const)],
        out_specs=(pl.BlockSpec((tm, D), row), pl.BlockSpec((1, D), const),
                   pl.BlockSpec((SUBLANES, 128), const)),
        out_shape=(jax.ShapeDtypeStruct((T, D), F32), jax.ShapeDtypeStruct((1, D), F32),
                   jax.ShapeDtypeStruct((SUBLANES, 128), F32)),
        compiler_params=_params("arbitrary"),
    )(x, target, gain)


def _sum_parts(parts, name, tr):
    P, R, C = parts.shape

    def body(p_ref, o_ref):
        g = p_ref[0].astype(F32)
        for d in range(1, P):
            g = g + p_ref[d].astype(F32)
        o_ref[...] = g

    return pl.pallas_call(
        body, name=name, grid=(R // tr,),
        in_specs=[pl.BlockSpec((P, tr, C), lambda i: (0, i, 0))],
        out_specs=pl.BlockSpec((tr, C), lambda i: (i, 0)),
        out_shape=jax.ShapeDtypeStruct((R, C), F32),
        compiler_params=_params("arbitrary"),
    )(parts)


def _adamw(parts, w, m, v, name, tr):
    P = parts.shape[0]
    R, C = w.shape

    def body(p_ref, w_ref, m_ref, v_ref, g_ref, d_ref, nm_ref, nv_ref):
        g = p_ref[0].astype(F32)
        for d in range(1, P):
            g = g + p_ref[d].astype(F32)
        nm = ADAM_B1 * m_ref[...] + (1.0 - ADAM_B1) * g
        nv = ADAM_B2 * v_ref[...] + (1.0 - ADAM_B2) * (g * g)
        m_hat = nm / (1.0 - ADAM_B1 ** ADAM_STEP)
        v_hat = nv / (1.0 - ADAM_B2 ** ADAM_STEP)
        g_ref[...] = g
        nm_ref[...] = nm
        nv_ref[...] = nv
        d_ref[...] = -ADAM_LR * (m_hat / (jnp.sqrt(v_hat) + ADAM_EPS) + ADAM_WD * w_ref[...])

    row = lambda i: (i, 0)
    spec = pl.BlockSpec((tr, C), row)
    shp = jax.ShapeDtypeStruct((R, C), F32)
    return pl.pallas_call(
        body, name=name, grid=(R // tr,),
        in_specs=[pl.BlockSpec((P, tr, C), lambda i: (0, i, 0)), spec, spec, spec],
        out_specs=(spec, spec, spec, spec),
        out_shape=(shp, shp, shp, shp),
        compiler_params=_params("arbitrary"),
    )(parts, w, m, v)


def _columns_of_blocks(g):
    n, R, w = g.shape
    return g.transpose(1, 0, 2).reshape(R, n * w)


def _pad_row(vec):
    vec = vec.reshape(1, -1)
    return jnp.pad(vec, ((0, 0), (0, PACK_COLS - vec.shape[1])))


def kernel(x, rel_bias_table, ffn1_norm, ffn1_w_gate, ffn1_w_up, ffn1_w_down, mix_norm, w_in, conv_w, attn_sinks, attn_out_norm, conv_out_norm, w_out, ffn2_norm, ffn2_w_gate, ffn2_w_up, ffn2_w_down, final_norm, loss_target, m_rel_bias_table, m_ffn1_norm, m_ffn1_w_gate, m_ffn1_w_up, m_ffn1_w_down, m_mix_norm, m_w_in, m_conv_w, m_attn_sinks, m_attn_out_norm, m_conv_out_norm, m_w_out, m_ffn2_norm, m_ffn2_w_gate, m_ffn2_w_up, m_ffn2_w_down, m_final_norm, v_rel_bias_table, v_ffn1_norm, v_ffn1_w_gate, v_ffn1_w_up, v_ffn1_w_down, v_mix_norm, v_w_in, v_conv_w, v_attn_sinks, v_attn_out_norm, v_conv_out_norm, v_w_out, v_ffn2_norm, v_ffn2_w_gate, v_ffn2_w_up, v_ffn2_w_down, v_final_norm):
    T, D = x.shape[1], x.shape[2]
    x0 = x[0]
    target = loss_target[0]
    tm = min(TM_FFN, T)
    tm_bwd = min(TM_FFN_BWD, T)
    tm_mix = min(TM_MIX, T)
    tk = min(TK_WGRAD, T)
    tf = TF_FFN
    nblk = min(ATTN_BLOCKS, T // BLOCK)
    me = 4 * lax.axis_index("x") + 2 * lax.axis_index("y") + lax.axis_index("c")

    big = {
        "ffn1_w_gate": (ffn1_w_gate[0], m_ffn1_w_gate[0], v_ffn1_w_gate[0], True),
        "ffn1_w_up": (ffn1_w_up[0], m_ffn1_w_up[0], v_ffn1_w_up[0], True),
        "ffn1_w_down": (ffn1_w_down[0], m_ffn1_w_down[0], v_ffn1_w_down[0], False),
        "w_in": (w_in[0], m_w_in[0], v_w_in[0], True),
        "w_out": (w_out[0], m_w_out[0], v_w_out[0], False),
        "ffn2_w_gate": (ffn2_w_gate[0], m_ffn2_w_gate[0], v_ffn2_w_gate[0], True),
        "ffn2_w_up": (ffn2_w_up[0], m_ffn2_w_up[0], v_ffn2_w_up[0], True),
        "ffn2_w_down": (ffn2_w_down[0], m_ffn2_w_down[0], v_ffn2_w_down[0], False),
    }

    def block_to_send(name):
        w, _, _, transposed = big[name]
        return (w.T if transposed else w).astype(BF16)

    names1 = ["ffn1_w_gate", "ffn1_w_up", "ffn1_w_down"]
    names_rest = ["w_in", "w_out", "ffn2_w_gate", "ffn2_w_up", "ffn2_w_down"]
    h1, token = _exchange_start([block_to_send(n) for n in names1], False, "gather_start_ffn1", ffn1_norm)
    h_rest, token = _exchange_start([block_to_send(n) for n in names_rest[:2]] + [conv_w[0]]
                                    + [block_to_send(n) for n in names_rest[2:]], False,
                                    "gather_start_rest", token)
    wgt1, wut1, wd1 = [g.reshape(-1, D) for g in _exchange_wait(h1, False, "gather_wait_ffn1", token)]

    wgu1 = _stack_gate_up(wgt1, wut1, tf)
    x1, xn1, gu1 = _ffn_fwd(x0, ffn1_norm, wgu1, wd1, "ffn1_fwd", tm, tf)
    mixw = _exchange_wait(h_rest[:3], False, "gather_wait_mix", x1)
    win_t = mixw[0].reshape(-1, D)
    wout = mixw[1].reshape(-1, D)
    cw = _columns_of_blocks(mixw[2])
    hn, q, k, v, u, gb, gc = _mixin_fwd(x1, mix_norm, win_t, "mixin_fwd", tm_mix)
    bucket = jnp.asarray(_bucket_table().T.copy())
    sinks = attn_sinks.reshape(-1)
    bias_t = _bias_build(rel_bias_table, bucket, "bias_build")
    attn_raw, attn_n = _attn_fwd(q, k, v, bias_t, sinks, attn_out_norm, "attn_fwd", nblk)
    x2, conv_n = _mixout_fwd(x1, attn_n, u, gb, gc, cw, conv_out_norm, wout, "mixout_fwd", tm_mix)
    wgt2, wut2, wd2 = [g.reshape(-1, D) for g in _exchange_wait(h_rest[3:], False, "gather_wait_ffn2", x2)]
    wgu2 = _stack_gate_up(wgt2, wut2, tf)
    x3, xn2, gu2 = _ffn_fwd(x2, ffn2_norm, wgu2, wd2, "ffn2_fwd", tm, tf)

    def blocks(g):
        return g.reshape(N_DEV, -1, D)

    dx3, d_final, loss_part = _loss_head(x3, target, final_norm.reshape(1, D), "loss_head", tm_mix)
    dx2, d_ffn2_norm, dgu2, hh2, do2 = _ffn_bwd(
        dx3, x2, ffn2_norm, gu2, wgu2, wd2, dx3, "ffn2_bwd", tm_bwd, tf)
    grads2 = _ffn_wgrad(dgu2, hh2, xn2, do2, "ffn2_wgrad", BM_WGRAD, tk, tf)
    handles2, token2 = _exchange_start([blocks(g) for g in grads2], True, "grads_start_ffn2", dx2)

    dmixed, d_wout = _mixout_bwd(dx2, attn_n, conv_n, wout, token2, "mixout_bwd", tm_mix)
    dq, dk, dv, dbias, dsink, d_attn_norm = _attn_bwd(
        dmixed, attn_raw, q, k, v, bias_t, sinks, attn_out_norm, "attn_bwd", nblk)
    du, dgb, dgc, d_cw, d_conv_norm = _conv_bwd(dmixed, u, gb, gc, cw, conv_out_norm, "conv_bwd", tm_mix)
    d_table = _bias_grad(dbias, bucket, "bias_grad")
    dz = [dq, dk, dv, du, dgb, dgc]
    dx1, d_mix_norm = _mixin_bwd(dx2, x1, mix_norm, dz, win_t, "mixin_bwd", tm_mix)
    d_win_t = _win_grad(dz, hn, "win_grad", min(TM_MIX, T))
    handles_mix, token_mix = _exchange_start([blocks(d_win_t), blocks(d_wout)], True, "grads_start_mix", dx1)

    dx0, d_ffn1_norm, dgu1, hh1, do1 = _ffn_bwd(
        dx1, x0, ffn1_norm, gu1, wgu1, wd1, token_mix, "ffn1_bwd", tm_bwd, tf)
    grads1 = _ffn_wgrad(dgu1, hh1, xn1, do1, "ffn1_wgrad", BM_WGRAD, tk, tf)
    handles1, token1 = _exchange_start([blocks(g) for g in grads1], True, "grads_start_ffn1", dx0)

    res = {}

    def update(names, parts):
        last = None
        for name, p in zip(names, parts):
            w, m_, v_, transposed = big[name]
            if transposed:
                summed = _sum_parts(p, "sum_" + name, _row_tile(p.shape[1], ADAM_ROWS))
                p = summed.T[None]
            g, dl, nm, nv = _adamw(p, w, m_, v_, "adamw_" + name, _row_tile(w.shape[0], ADAM_ROWS))
            res[name] = tuple(a[None] for a in (g, dl, nm, nv))
            last = g
        return last

    parts2 = _exchange_wait(handles2, True, "grads_wait_ffn2", token1)
    done2 = update(["ffn2_w_gate", "ffn2_w_up", "ffn2_w_down"], parts2)
    parts_mix = _exchange_wait(handles_mix, True, "grads_wait_mix", done2)
    done_mix = update(["w_in", "w_out"], parts_mix)
    parts1 = _exchange_wait(handles1, True, "grads_wait_ffn1", done_mix)
    update(names1, parts1)

    def pack(ffn1, mixn, ffn2, fin, attn_n_, conv_n_, sink_, extra, convw, table):
        rows = [_pad_row(ffn1), _pad_row(mixn), _pad_row(ffn2), _pad_row(fin),
                _pad_row(jnp.concatenate([attn_n_.reshape(-1), conv_n_.reshape(-1)])),
                _pad_row(sink_), _pad_row(extra),
                jnp.zeros((1, PACK_COLS), F32),
                jnp.pad(convw, ((0, 0), (0, PACK_COLS - convw.shape[1]))),
                _pad_row(table),
                jnp.zeros((PACK_ROWS - 12, PACK_COLS), F32)]
        return jnp.concatenate(rows, axis=0)

    def own_channels(a):
        full = jnp.zeros((a.shape[1], CONV_DIM), F32)
        return lax.dynamic_update_slice(full, a[0], (0, me * a.shape[2]))

    g_pack = pack(d_ffn1_norm, d_mix_norm, d_ffn2_norm, d_final, d_attn_norm, d_conv_norm,
                  dsink[:, 0], loss_part[0, :1], d_cw[:3], d_table[:, :N_Q_HEADS])
    zero1 = jnp.zeros((1,), F32)
    w_pack = pack(ffn1_norm, mix_norm, ffn2_norm, final_norm, attn_out_norm, conv_out_norm,
                  attn_sinks, zero1, own_channels(conv_w), rel_bias_table)
    m_pack = pack(m_ffn1_norm, m_mix_norm, m_ffn2_norm, m_final_norm, m_attn_out_norm, m_conv_out_norm,
                  m_attn_sinks, zero1, own_channels(m_conv_w), m_rel_bias_table)
    v_pack = pack(v_ffn1_norm, v_mix_norm, v_ffn2_norm, v_final_norm, v_attn_out_norm, v_conv_out_norm,
                  v_attn_sinks, zero1, own_channels(v_conv_w), v_rel_bias_table)
    (g_all,) = _exchange([g_pack], False, "gather_small")
    packs = _adamw(g_all, w_pack, m_pack, v_pack, "adamw_small", PACK_ROWS)

    def unpack(pk):
        cwb = lax.dynamic_slice(pk[8:11, :CONV_DIM], (0, me * conv_w.shape[2]), (3, conv_w.shape[2]))
        return {
            "ffn1_norm": pk[0:1, :D], "mix_norm": pk[1:2, :D], "ffn2_norm": pk[2:3, :D],
            "final_norm": pk[3, :D],
            "attn_out_norm": pk[4:5, :ATTN_WIDTH], "conv_out_norm": pk[4:5, ATTN_WIDTH:ATTN_WIDTH + CONV_DIM],
            "attn_sinks": pk[5:6, :N_Q_HEADS],
            "conv_w": cwb[None],
            "rel_bias_table": pk[11, :NUM_BUCKETS * N_Q_HEADS].reshape(NUM_BUCKETS, N_Q_HEADS),
        }

    small = [unpack(pk) for pk in packs]
    loss = packs[0][6, 0]

    order = ["rel_bias_table", "ffn1_norm", "ffn1_w_gate", "ffn1_w_up", "ffn1_w_down", "mix_norm", "w_in",
             "conv_w", "attn_sinks", "attn_out_norm", "conv_out_norm", "w_out", "ffn2_norm",
             "ffn2_w_gate", "ffn2_w_up", "ffn2_w_down", "final_norm"]
    outs = [loss, dx0[None]]
    for kind in range(4):
        for name in order:
            outs.append(res[name][kind] if name in res else small[kind][name])
    return tuple(outs)
```

```python
import math

import numpy as np
import jax
import jax.numpy as jnp
from jax import lax
from jax.experimental import pallas as pl
from jax.experimental.pallas import tpu as pltpu

F32 = jnp.float32
BF16 = jnp.bfloat16

N_DEV = 8
EPS = 1e-6
HEAD_DIM = 64
N_Q_HEADS = 8
N_KV_HEADS = 2
GQA_GROUP = 4
ATTN_WIDTH = 512
KV_WIDTH = 128
CONV_DIM = 512
BLOCK = 128
WINDOW = 128
NUM_BUCKETS = 32
MAX_DISTANCE = 128
SCALE = HEAD_DIM ** -0.5
MASKED = -1e30
GROUP_ROWS = GQA_GROUP * BLOCK

ADAM_LR = 0.001
ADAM_B1 = 0.9
ADAM_B2 = 0.999
ADAM_EPS = 1e-08
ADAM_WD = 0.01
ADAM_STEP = 10

VMEM_LIMIT_BYTES = 40 * 1024 * 1024
VMEM_LIMIT_FFN_BYTES = 52 * 1024 * 1024
SUBLANES = 8
PACK_ROWS = 16
PACK_COLS = 1024

TM_FFN = 1024
TM_FFN_BWD = 1024
TM_MIX = 512
TK_WGRAD = 1024
TF_FFN = 256
ROW_GROUPS = 4
ATTN_BLOCKS = 4
ADAM_ROWS = 256


def _row_tile(rows, limit):
    best = rows
    for t in range(16, min(rows, limit) + 1, 16):
        if rows % t == 0:
            best = t
    return best


def _params(*sem):
    return pltpu.CompilerParams(dimension_semantics=sem, vmem_limit_bytes=VMEM_LIMIT_BYTES)


def _dot(a, b):
    return jnp.dot(a, b, preferred_element_type=F32)


def _dot_nt(a, b):
    return lax.dot_general(a, b, (((1,), (1,)), ((), ())), preferred_element_type=F32)


def _dot_tn(a, b):
    return lax.dot_general(a, b, (((0,), (0,)), ((), ())), preferred_element_type=F32)


def _sigmoid(g):
    return 0.5 * jnp.tanh(0.5 * g) + 0.5


def _rms_stats(x):
    inv = lax.rsqrt(jnp.mean(x * x, axis=-1, keepdims=True) + EPS)
    return inv, x * inv


def _rms_bwd(dy, x, gain):
    inv, xhat = _rms_stats(x)
    dgain = jnp.sum(dy * xhat, axis=0, keepdims=True)
    dxh = dy * gain
    dx = inv * (dxh - xhat * jnp.mean(dxh * xhat, axis=-1, keepdims=True))
    return dx, dgain


def _peer_list():
    x, y, c = lax.axis_index("x"), lax.axis_index("y"), lax.axis_index("c")
    peers = []
    for k in range(1, N_DEV):
        px = 1 - x if (k >> 2) & 1 else x
        py = 1 - y if (k >> 1) & 1 else y
        pc = 1 - c if k & 1 else c
        peers.append((px, py, pc))
    return 4 * x + 2 * y + c, peers


def _exchange(arrs, scatter, name):
    n = len(arrs)
    out_shape = []
    for a in arrs:
        shp = a.shape if scatter else (N_DEV,) + a.shape
        out_shape.append(jax.ShapeDtypeStruct(shp, a.dtype))

    def body(*refs):
        ins, outs = refs[:n], refs[n:2 * n]
        send_sems, recv_sems, local_sems = refs[2 * n:]
        me, peers = _peer_list()
        started = []
        for a in range(n):
            own = ins[a].at[me] if scatter else ins[a]
            loc = pltpu.make_async_copy(own, outs[a].at[me], local_sems.at[a])
            loc.start()
            started.append(loc)
        sends = []
        for a in range(n):
            for k, (px, py, pc) in enumerate(peers):
                src = ins[a].at[4 * px + 2 * py + pc] if scatter else ins[a]
                cp = pltpu.make_async_remote_copy(
                    src_ref=src, dst_ref=outs[a].at[me],
                    send_sem=send_sems.at[a, k], recv_sem=recv_sems.at[a, k],
                    device_id=(px, py, pc), device_id_type=pl.DeviceIdType.MESH)
                cp.start()
                sends.append(cp)
        for a in range(n):
            for k, (px, py, pc) in enumerate(peers):
                landed = outs[a].at[4 * px + 2 * py + pc]
                pltpu.make_async_remote_copy(
                    src_ref=landed, dst_ref=landed,
                    send_sem=send_sems.at[a, k], recv_sem=recv_sems.at[a, k],
                    device_id=(px, py, pc), device_id_type=pl.DeviceIdType.MESH).wait_recv()
        for cp in sends:
            cp.wait_send()
        for loc in started:
            loc.wait()

    hbm = pl.BlockSpec(memory_space=pl.ANY)
    return pl.pallas_call(
        body, name=name, out_shape=tuple(out_shape),
        in_specs=[hbm] * n, out_specs=tuple([hbm] * n),
        scratch_shapes=[pltpu.SemaphoreType.DMA((n, N_DEV - 1)),
                        pltpu.SemaphoreType.DMA((n, N_DEV - 1)),
                        pltpu.SemaphoreType.DMA((n,))],
    )(*arrs)


_HBM = pl.BlockSpec(memory_space=pltpu.HBM)
_SEM = pl.BlockSpec(memory_space=pltpu.SEMAPHORE)
_EFFECT = pltpu.SideEffectType.DATAFLOW_SIDE_EFFECTING


def _split_copies(srcs, lands, send_sems, recv_sems, scatter):
    me, peers = _peer_list()
    copies = []
    for a in range(len(srcs)):
        for k, (px, py, pc) in enumerate(peers):
            src = srcs[a].at[4 * px + 2 * py + pc] if scatter else srcs[a]
            copies.append(pltpu.make_async_remote_copy(
                src_ref=src, dst_ref=lands[a].at[me],
                send_sem=send_sems[a].at[k], recv_sem=recv_sems[a].at[k],
                device_id=(px, py, pc), device_id_type=pl.DeviceIdType.MESH))
    return copies


def _exchange_start(arrs, scatter, name, after):
    n = len(arrs)
    me = 4 * lax.axis_index("x") + 2 * lax.axis_index("y") + lax.axis_index("c")
    lands = []
    for a in arrs:
        own = lax.dynamic_index_in_dim(a, me, 0, keepdims=True) if scatter else a[None]
        shp = a.shape if scatter else (N_DEV,) + a.shape
        lands.append(lax.dynamic_update_slice(lax.empty(shp, a.dtype), own, (me,) + (0,) * (len(shp) - 1)))

    def body(*refs):
        srcs, lnds = refs[:n], refs[n:2 * n]
        outs = refs[2 * n + 1:]
        send_sems, recv_sems = outs[:n], outs[n:2 * n]
        token = outs[4 * n]
        for cp in _split_copies(srcs, lnds, send_sems, recv_sems, scatter):
            cp.start()
        token[...] = jnp.zeros_like(token)

    sem = pltpu.SemaphoreType.DMA((N_DEV - 1,))
    out_shape = ([sem] * (2 * n) + [pltpu.HBM(a.shape, a.dtype) for a in arrs]
                 + [pltpu.HBM(l.shape, l.dtype) for l in lands] + [jax.ShapeDtypeStruct((SUBLANES, 128), F32)])
    res = pl.pallas_call(
        body, name=name, out_shape=tuple(out_shape),
        in_specs=[_HBM] * (2 * n) + [pl.BlockSpec(memory_space=pl.ANY)],
        out_specs=tuple([_SEM] * (2 * n) + [_HBM] * (2 * n) + [pl.BlockSpec(memory_space=pltpu.VMEM)]),
        input_output_aliases={i: 2 * n + i for i in range(2 * n)},
        compiler_params=pltpu.CompilerParams(has_side_effects=_EFFECT),
    )(*[pltpu.with_memory_space_constraint(a, pltpu.HBM) for a in arrs],
      *[pltpu.with_memory_space_constraint(l, pltpu.HBM) for l in lands], after)
    handles = [(res[2 * n + a], res[3 * n + a], res[a], res[n + a]) for a in range(n)]
    return handles, res[4 * n]


def _exchange_wait(handles, scatter, name, after):
    n = len(handles)

    def body(*refs):
        srcs, lnds = refs[:n], refs[n:2 * n]
        send_sems, recv_sems = refs[2 * n:3 * n], refs[3 * n:4 * n]
        for cp in _split_copies(srcs, lnds, send_sems, recv_sems, scatter):
            cp.wait_send()
            cp.wait_recv()

    srcs = [h[0] for h in handles]
    lands = [h[1] for h in handles]
    res = pl.pallas_call(
        body, name=name,
        out_shape=tuple([pltpu.HBM(a.shape, a.dtype) for a in srcs] + [pltpu.HBM(l.shape, l.dtype) for l in lands]),
        in_specs=[_HBM] * (2 * n) + [_SEM] * (2 * n) + [pl.BlockSpec(memory_space=pl.ANY)],
        out_specs=tuple([_HBM] * (2 * n)),
        input_output_aliases={i: i for i in range(2 * n)},
        compiler_params=pltpu.CompilerParams(has_side_effects=_EFFECT),
    )(*srcs, *lands, *[h[2] for h in handles], *[h[3] for h in handles], after)
    return list(res[n:])


def _stack_gate_up(wgt, wut, tf):
    F, D = wgt.shape
    return jnp.stack([wgt.reshape(F // tf, tf, D), wut.reshape(F // tf, tf, D)], axis=1).reshape(2 * F, D)


def _unstack_gate_up(w, tf):
    F2, D = w.shape
    w4 = w.reshape(F2 // (2 * tf), 2, tf, D)
    return w4[:, 0].reshape(F2 // 2, D), w4[:, 1].reshape(F2 // 2, D)


def _row_groups(tm):
    return [slice(r * (tm // ROW_GROUPS), (r + 1) * (tm // ROW_GROUPS)) for r in range(ROW_GROUPS)]


def _resident(shape):
    return pl.BlockSpec(shape, lambda *_: (0,) * len(shape), pipeline_mode=pl.Buffered(1))


def _ffn_fwd(x, gain, wgu, wd, name, tm, tf):
    T, D = x.shape
    F = wd.shape[0]
    nj = F // tf

    def body(x_ref, g_ref, wgu_ref, wd_ref, xo_ref, xn_ref, gu_ref, xn_sc, acc_sc):
        j = pl.program_id(1)

        @pl.when(j == 0)
        def _():
            _, xhat = _rms_stats(x_ref[...])
            xn = (xhat * g_ref[...]).astype(BF16)
            xn_sc[...] = xn
            xn_ref[...] = xn
            acc_sc[...] = jnp.zeros_like(acc_sc)

        wgu_j = wgu_ref[pl.ds(pl.multiple_of(j * 2 * tf, 2 * tf), 2 * tf), :]
        wd_j = wd_ref[pl.ds(pl.multiple_of(j * tf, tf), tf), :]
        groups = _row_groups(tm)
        gus = [_dot_nt(xn_sc[rows, :], wgu_j) for rows in groups]
        hs = []
        for rows, gu in zip(groups, gus):
            gu_ref[rows, :] = gu.astype(BF16)
            g, u = gu[:, :tf], gu[:, tf:]
            hs.append((g * _sigmoid(g) * u).astype(BF16))
        for rows, h in zip(groups, hs):
            acc_sc[rows, :] += _dot(h, wd_j)

        @pl.when(j == nj - 1)
        def _():
            xo_ref[...] = x_ref[...] + 0.5 * acc_sc[...]

    tile = pl.BlockSpec((tm, D), lambda i, j: (i, 0), pipeline_mode=pl.Buffered(1))
    return pl.pallas_call(
        body, name=name, grid=(T // tm, nj),
        in_specs=[tile, pl.BlockSpec((1, D), lambda i, j: (0, 0)), _resident(wgu.shape), _resident(wd.shape)],
        out_specs=(tile,
                   pl.BlockSpec((tm, D), lambda i, j: (i, 0)),
                   pl.BlockSpec((None, tm, 2 * tf), lambda i, j: (j, i, 0))),
        out_shape=(jax.ShapeDtypeStruct((T, D), F32), jax.ShapeDtypeStruct((T, D), BF16),
                   jax.ShapeDtypeStruct((nj, T, 2 * tf), BF16)),
        scratch_shapes=[pltpu.VMEM((tm, D), BF16), pltpu.VMEM((tm, D), F32)],
        compiler_params=pltpu.CompilerParams(dimension_semantics=("arbitrary", "arbitrary"),
                                             vmem_limit_bytes=VMEM_LIMIT_FFN_BYTES),
    )(x, gain, wgu, wd)


def _ffn_bwd(dy, x, gain, gu, wgu, wd, after, name, tm, tf):
    T, D = x.shape
    F = wd.shape[0]
    nj = F // tf

    def body(dy_ref, x_ref, g_ref, gu_ref, wgu_ref, wd_ref, after_ref,
             dx_ref, dgain_ref, dgu_ref, hh_ref, do_ref, do_sc, acc_sc):
        i, j = pl.program_id(0), pl.program_id(1)

        @pl.when((i == 0) & (j == 0))
        def _():
            dgain_ref[...] = jnp.zeros_like(dgain_ref)

        @pl.when(j == 0)
        def _():
            do = (0.5 * dy_ref[...]).astype(BF16)
            do_sc[...] = do
            do_ref[...] = do
            acc_sc[...] = jnp.zeros_like(acc_sc)

        wgu_j = wgu_ref[pl.ds(pl.multiple_of(j * 2 * tf, 2 * tf), 2 * tf), :]
        wd_j = wd_ref[pl.ds(pl.multiple_of(j * tf, tf), tf), :]
        groups = _row_groups(tm)
        dhs = [_dot_nt(do_sc[rows, :], wd_j) for rows in groups]
        for rows, dh in zip(groups, dhs):
            g = gu_ref[rows, :tf].astype(F32)
            u = gu_ref[rows, tf:].astype(F32)
            sig = _sigmoid(g)
            s = g * sig
            dgu_ref[rows, :tf] = (dh * u * (sig + s * (1.0 - sig))).astype(BF16)
            dgu_ref[rows, tf:] = (dh * s).astype(BF16)
            hh_ref[rows, :] = (s * u).astype(BF16)
        for rows in groups:
            acc_sc[rows, :] += _dot(dgu_ref[rows, :], wgu_j)

        @pl.when(j == nj - 1)
        def _():
            dx, dgain = _rms_bwd(acc_sc[...], x_ref[...], g_ref[...])
            dgain_ref[...] += dgain
            dx_ref[...] = dy_ref[...] + dx

    tile = pl.BlockSpec((tm, D), lambda i, j: (i, 0), pipeline_mode=pl.Buffered(1))
    chunk2 = pl.BlockSpec((None, tm, 2 * tf), lambda i, j: (j, i, 0))
    return pl.pallas_call(
        body, name=name, grid=(T // tm, nj),
        in_specs=[tile, tile, pl.BlockSpec((1, D), lambda i, j: (0, 0)), chunk2,
                  _resident(wgu.shape), _resident(wd.shape), pl.BlockSpec(memory_space=pl.ANY)],
        out_specs=(tile,
                   pl.BlockSpec((1, D), lambda i, j: (0, 0)),
                   chunk2,
                   pl.BlockSpec((None, tm, tf), lambda i, j: (j, i, 0)),
                   pl.BlockSpec((tm, D), lambda i, j: (i, 0))),
        out_shape=(jax.ShapeDtypeStruct((T, D), F32), jax.ShapeDtypeStruct((1, D), F32),
                   jax.ShapeDtypeStruct((nj, T, 2 * tf), BF16), jax.ShapeDtypeStruct((nj, T, tf), BF16),
                   jax.ShapeDtypeStruct((T, D), BF16)),
        scratch_shapes=[pltpu.VMEM((tm, D), BF16), pltpu.VMEM((tm, D), F32)],
        compiler_params=pltpu.CompilerParams(dimension_semantics=("arbitrary", "arbitrary"),
                                             vmem_limit_bytes=VMEM_LIMIT_FFN_BYTES),
    )(dy, x, gain, gu, wgu, wd, after)


def _chunk_grad(a, b, name, tk):
    nc, T, w = a.shape
    D = b.shape[1]
    nk = T // tk

    def body(a_ref, b_ref, o_ref, acc_sc):
        k = pl.program_id(1)

        @pl.when(k == 0)
        def _():
            acc_sc[...] = jnp.zeros_like(acc_sc)

        acc_sc[...] += _dot_tn(a_ref[...], b_ref[pl.ds(pl.multiple_of(k * tk, tk), tk), :])

        @pl.when(k == nk - 1)
        def _():
            o_ref[...] = acc_sc[...].astype(BF16)

    return pl.pallas_call(
        body, name=name, grid=(nc, nk),
        in_specs=[pl.BlockSpec((None, tk, w), lambda c, k: (c, k, 0)), _resident(b.shape)],
        out_specs=pl.BlockSpec((w, D), lambda c, k: (c, 0)),
        out_shape=jax.ShapeDtypeStruct((nc * w, D), BF16),
        scratch_shapes=[pltpu.VMEM((w, D), F32)],
        compiler_params=_params("arbitrary", "arbitrary"),
    )(a, b)


def _ffn_wgrad(dgu, hh, xn, do, name, tk, tf):
    d_gate, d_up = _unstack_gate_up(_chunk_grad(dgu, xn, name + "_gu", tk), tf)
    return [d_gate, d_up, _chunk_grad(hh, do, name + "_down", tk)]


_Z_SPLITS = (0, 512, 640, 768, 1280, 1792, 2304)


def _mixin_fwd(x, gain, w_in_t, name, tm):
    T, D = x.shape
    widths = [b - a for a, b in zip(_Z_SPLITS[:-1], _Z_SPLITS[1:])]

    def body(x_ref, g_ref, w_ref, hn_ref, *outs):
        _, xhat = _rms_stats(x_ref[...])
        hn = (xhat * g_ref[...]).astype(BF16)
        hn_ref[...] = hn
        for o_ref, lo, hi in zip(outs, _Z_SPLITS[:-1], _Z_SPLITS[1:]):
            o_ref[...] = _dot_nt(hn, w_ref[lo:hi, :]).astype(BF16)

    return pl.pallas_call(
        body, name=name, grid=(T // tm,),
        in_specs=[pl.BlockSpec((tm, D), lambda i: (i, 0)),
                  pl.BlockSpec((1, D), lambda i: (0, 0)),
                  pl.BlockSpec(w_in_t.shape, lambda i: (0, 0))],
        out_specs=tuple([pl.BlockSpec((tm, D), lambda i: (i, 0))]
                        + [pl.BlockSpec((tm, w), lambda i: (i, 0)) for w in widths]),
        out_shape=tuple([jax.ShapeDtypeStruct((T, D), BF16)]
                        + [jax.ShapeDtypeStruct((T, w), BF16) for w in widths]),
        compiler_params=_params("arbitrary"),
    )(x, gain, w_in_t)


def _bucket_table():
    qi = np.arange(BLOCK, dtype=np.int32)[:, None]
    kj = np.arange(2 * BLOCK, dtype=np.int32)[None, :]
    dist = qi + BLOCK - kj
    n = np.maximum(dist, 0)
    max_exact = NUM_BUCKETS // 2
    large = max_exact + (np.log(np.maximum(n, 1).astype(np.float32) / max_exact)
                         / math.log(MAX_DISTANCE / max_exact)
                         * (NUM_BUCKETS - max_exact)).astype(np.int32)
    large = np.minimum(large, NUM_BUCKETS - 1)
    bucket = np.where(n < max_exact, n, large).astype(np.int32)
    valid = (dist >= 0) & (dist < WINDOW)
    return np.where(valid, bucket, -1).astype(np.int32)


def _bias_build(table, bucket, name):
    def body(t_ref, b_ref, o_ref):
        bk = b_ref[...]
        for h in range(N_Q_HEADS):
            def step(b, acc):
                return jnp.where(bk == b, t_ref[b, h], acc)
            o_ref[h] = lax.fori_loop(0, NUM_BUCKETS, step, jnp.full(bk.shape, MASKED, F32))

    return pl.pallas_call(
        body, name=name,
        in_specs=[pl.BlockSpec(memory_space=pltpu.SMEM), pl.BlockSpec(memory_space=pltpu.VMEM)],
        out_specs=pl.BlockSpec(memory_space=pltpu.VMEM),
        out_shape=jax.ShapeDtypeStruct((N_Q_HEADS,) + bucket.shape, F32),
    )(table, bucket)


def _bias_grad(dbias, bucket, name):
    def body(d_ref, b_ref, o_ref):
        bk = b_ref[...]
        row = lax.broadcasted_iota(jnp.int32, o_ref.shape, 0)
        lane = lax.broadcasted_iota(jnp.int32, o_ref.shape, 1)
        res = jnp.zeros(o_ref.shape, F32)
        for h in range(N_Q_HEADS):
            d = d_ref[h]

            def step(b, acc):
                tot = jnp.sum(jnp.where(bk == b, d, 0.0), axis=1, keepdims=True)
                tot = jnp.sum(tot, axis=0, keepdims=True)
                return jnp.where((row == b) & (lane == h), tot, acc)
            res = lax.fori_loop(0, NUM_BUCKETS, step, res)
        o_ref[...] = res

    return pl.pallas_call(
        body, name=name,
        in_specs=[pl.BlockSpec(memory_space=pltpu.VMEM), pl.BlockSpec(memory_space=pltpu.VMEM)],
        out_specs=pl.BlockSpec(memory_space=pltpu.VMEM),
        out_shape=jax.ShapeDtypeStruct((NUM_BUCKETS, 128), F32),
    )(dbias, bucket)


def _head_cols(h):
    return slice(h * HEAD_DIM, (h + 1) * HEAD_DIM)


def _stack_heads(ref, r0, g, dtype):
    return jnp.concatenate(
        [ref[pl.ds(r0, BLOCK), _head_cols(GQA_GROUP * g + j)].astype(dtype) for j in range(GQA_GROUP)], axis=0)


def _unstack_heads(ref, r0, g, val):
    for j in range(GQA_GROUP):
        ref[pl.ds(r0, BLOCK), _head_cols(GQA_GROUP * g + j)] = val[j * BLOCK:(j + 1) * BLOCK, :]


def _head_lanes(h):
    return slice(h * BLOCK, (h + 1) * BLOCK)


def _group_lanes(g):
    return slice(g * GROUP_ROWS, (g + 1) * GROUP_ROWS)


def _head_softmax(st, bias_t, sink, no_prev):
    s = st * SCALE + bias_t
    row = lax.broadcasted_iota(jnp.int32, s.shape, 0)
    s = jnp.where(no_prev & (row < BLOCK), MASKED, s)
    m = jnp.maximum(jnp.max(s, axis=0, keepdims=True), sink)
    p = jnp.exp(s - m)
    ps = jnp.exp(sink - m)
    r = 1.0 / (jnp.sum(p, axis=0, keepdims=True) + ps)
    return p * r, ps * r


def _load_band(kf_sc, vf_sc, kp_ref, kc_ref, vp_ref, vc_ref, tq):
    kf_sc[0:BLOCK, :] = kp_ref[...]
    kf_sc[BLOCK:BLOCK + tq, :] = kc_ref[...]
    vf_sc[0:BLOCK, :] = vp_ref[...]
    vf_sc[BLOCK:BLOCK + tq, :] = vc_ref[...]


def _attn_fwd(q, k, v, bias_t, sinks, gain, name, nblk):
    T = q.shape[0]
    tq = nblk * BLOCK

    def body(sink_ref, q_ref, kc_ref, kp_ref, vc_ref, vp_ref, bias_ref, g_ref, raw_ref, nrm_ref,
             kf_sc, vf_sc, o_sc, st_sc, pt_sc):
        i = pl.program_id(0)
        _load_band(kf_sc, vf_sc, kp_ref, kc_ref, vp_ref, vc_ref, tq)

        def block(b, carry):
            r0 = pl.multiple_of(b * BLOCK, BLOCK)
            no_prev = (i == 0) & (b == 0)
            for g in range(N_KV_HEADS):
                kb = kf_sc[pl.ds(r0, 2 * BLOCK), _head_cols(g)]
                st_sc[:, _group_lanes(g)] = _dot_nt(kb, _stack_heads(q_ref, r0, g, BF16))
            for h in range(N_Q_HEADS):
                p, _ = _head_softmax(st_sc[:, _head_lanes(h)], bias_ref[h], sink_ref[h], no_prev)
                pt_sc[:, _head_lanes(h)] = p.astype(BF16)
            for g in range(N_KV_HEADS):
                vb = vf_sc[pl.ds(r0, 2 * BLOCK), _head_cols(g)]
                _unstack_heads(o_sc, r0, g, _dot_tn(pt_sc[:, _group_lanes(g)], vb))
            return carry

        lax.fori_loop(0, nblk, block, 0)
        o = o_sc[...]
        raw_ref[...] = o.astype(BF16)
        _, ohat = _rms_stats(o)
        nrm_ref[...] = (ohat * g_ref[...]).astype(BF16)

    cur = lambda i: (i, 0)
    prev = lambda i: (jnp.maximum(i * nblk - 1, 0), 0)
    lanes = N_Q_HEADS * BLOCK
    return pl.pallas_call(
        body, name=name, grid=(T // tq,),
        in_specs=[pl.BlockSpec(memory_space=pltpu.SMEM),
                  pl.BlockSpec((tq, ATTN_WIDTH), cur),
                  pl.BlockSpec((tq, KV_WIDTH), cur), pl.BlockSpec((BLOCK, KV_WIDTH), prev),
                  pl.BlockSpec((tq, KV_WIDTH), cur), pl.BlockSpec((BLOCK, KV_WIDTH), prev),
                  pl.BlockSpec(bias_t.shape, lambda i: (0, 0, 0)),
                  pl.BlockSpec((1, ATTN_WIDTH), lambda i: (0, 0))],
        out_specs=(pl.BlockSpec((tq, ATTN_WIDTH), cur), pl.BlockSpec((tq, ATTN_WIDTH), cur)),
        out_shape=(jax.ShapeDtypeStruct((T, ATTN_WIDTH), BF16), jax.ShapeDtypeStruct((T, ATTN_WIDTH), BF16)),
        scratch_shapes=[pltpu.VMEM((tq + BLOCK, KV_WIDTH), BF16), pltpu.VMEM((tq + BLOCK, KV_WIDTH), BF16),
                        pltpu.VMEM((tq, ATTN_WIDTH), F32),
                        pltpu.VMEM((2 * BLOCK, lanes), F32), pltpu.VMEM((2 * BLOCK, lanes), BF16)],
        compiler_params=_params("arbitrary"),
    )(sinks, q, k, k, v, v, bias_t, gain)


def _attn_bwd(dmixed, raw, q, k, v, bias_t, sinks, gain, name, nblk):
    T = q.shape[0]
    tq = nblk * BLOCK
    nt = T // tq
    lanes = N_Q_HEADS * BLOCK

    def body(sink_ref, dm_ref, raw_ref, q_ref, kc_ref, kp_ref, vc_ref, vp_ref, bias_ref, g_ref,
             dq_ref, dk_ref, dv_ref, dbias_ref, dsink_ref, dgain_ref,
             do_sc, dq_sc, kf_sc, vf_sc, dkf_sc, dvf_sc, st_sc, dpt_sc, pt_sc, dst_sc, drow_sc,
             qs_sc, dos_sc, dsink_sc):
        i = pl.program_id(0)
        tile = nt - 1 - i

        @pl.when(i == 0)
        def _():
            dkf_sc[...] = jnp.zeros_like(dkf_sc)
            dvf_sc[...] = jnp.zeros_like(dvf_sc)
            dsink_sc[...] = jnp.zeros_like(dsink_sc)
            dbias_ref[...] = jnp.zeros_like(dbias_ref)
            dgain_ref[...] = jnp.zeros_like(dgain_ref)

        carry_k = dkf_sc[0:BLOCK, :]
        carry_v = dvf_sc[0:BLOCK, :]
        dkf_sc[0:tq, :] = jnp.zeros((tq, KV_WIDTH), F32)
        dvf_sc[0:tq, :] = jnp.zeros((tq, KV_WIDTH), F32)
        dkf_sc[tq:tq + BLOCK, :] = carry_k
        dvf_sc[tq:tq + BLOCK, :] = carry_v
        _load_band(kf_sc, vf_sc, kp_ref, kc_ref, vp_ref, vc_ref, tq)

        do, dgain = _rms_bwd(dm_ref[...].astype(F32), raw_ref[...].astype(F32), g_ref[...])
        dgain_ref[...] += dgain
        do_sc[...] = do
        ones = jnp.ones((SUBLANES, HEAD_DIM), BF16)

        def block(b, carry):
            r0 = pl.multiple_of(b * BLOCK, BLOCK)
            no_prev = (tile == 0) & (b == 0)
            for g in range(N_KV_HEADS):
                kb = kf_sc[pl.ds(r0, 2 * BLOCK), _head_cols(g)]
                vb = vf_sc[pl.ds(r0, 2 * BLOCK), _head_cols(g)]
                qg = _stack_heads(q_ref, r0, g, BF16)
                dog = _stack_heads(do_sc, r0, g, F32)
                prod = dog * _stack_heads(raw_ref, r0, g, F32)
                hi = prod.astype(BF16)
                lo = (prod - hi.astype(F32)).astype(BF16)
                drow_sc[:, _group_lanes(g)] = _dot_nt(ones, hi) + _dot_nt(ones, lo)
                dogb = dog.astype(BF16)
                qs_sc[g] = qg
                dos_sc[g] = dogb
                st_sc[:, _group_lanes(g)] = _dot_nt(kb, qg)
                dpt_sc[:, _group_lanes(g)] = _dot_nt(vb, dogb)
            for h in range(N_Q_HEADS):
                hl = _head_lanes(h)
                p, ps = _head_softmax(st_sc[:, hl], bias_ref[h], sink_ref[h], no_prev)
                rowdot = drow_sc[0:1, hl]
                ds = p * (dpt_sc[:, hl] - rowdot)
                dsink_sc[h:h + 1, :] += -(ps * rowdot)
                dbias_ref[h] += ds
                dst_sc[:, hl] = ds.astype(BF16)
                pt_sc[:, hl] = p.astype(BF16)
            for g in range(N_KV_HEADS):
                kb = kf_sc[pl.ds(r0, 2 * BLOCK), _head_cols(g)]
                dsg = dst_sc[:, _group_lanes(g)]
                _unstack_heads(dq_sc, r0, g, _dot_tn(dsg, kb) * SCALE)
                dkf_sc[pl.ds(r0, 2 * BLOCK), _head_cols(g)] += _dot(dsg, qs_sc[g]) * SCALE
                dvf_sc[pl.ds(r0, 2 * BLOCK), _head_cols(g)] += _dot(pt_sc[:, _group_lanes(g)], dos_sc[g])
            return carry

        lax.fori_loop(0, nblk, block, 0)
        dq_ref[...] = dq_sc[...].astype(BF16)
        dk_ref[...] = dkf_sc[BLOCK:BLOCK + tq, :].astype(BF16)
        dv_ref[...] = dvf_sc[BLOCK:BLOCK + tq, :].astype(BF16)

        @pl.when(i == nt - 1)
        def _():
            tot = jnp.sum(dsink_sc[...], axis=1, keepdims=True)
            dsink_ref[...] = jnp.broadcast_to(tot, dsink_ref.shape)

    cur = lambda i: (nt - 1 - i, 0)
    prev = lambda i: (jnp.maximum((nt - 1 - i) * nblk - 1, 0), 0)
    const2 = lambda i: (0, 0)
    const3 = lambda i: (0, 0, 0)
    return pl.pallas_call(
        body, name=name, grid=(nt,),
        in_specs=[pl.BlockSpec(memory_space=pltpu.SMEM),
                  pl.BlockSpec((tq, ATTN_WIDTH), cur),
                  pl.BlockSpec((tq, ATTN_WIDTH), cur),
                  pl.BlockSpec((tq, ATTN_WIDTH), cur),
                  pl.BlockSpec((tq, KV_WIDTH), cur), pl.BlockSpec((BLOCK, KV_WIDTH), prev),
                  pl.BlockSpec((tq, KV_WIDTH), cur), pl.BlockSpec((BLOCK, KV_WIDTH), prev),
                  pl.BlockSpec(bias_t.shape, const3),
                  pl.BlockSpec((1, ATTN_WIDTH), const2)],
        out_specs=(pl.BlockSpec((tq, ATTN_WIDTH), cur),
                   pl.BlockSpec((tq, KV_WIDTH), cur), pl.BlockSpec((tq, KV_WIDTH), cur),
                   pl.BlockSpec(bias_t.shape, const3),
                   pl.BlockSpec((N_Q_HEADS, 128), const2),
                   pl.BlockSpec((1, ATTN_WIDTH), const2)),
        out_shape=(jax.ShapeDtypeStruct((T, ATTN_WIDTH), BF16),
                   jax.ShapeDtypeStruct((T, KV_WIDTH), BF16), jax.ShapeDtypeStruct((T, KV_WIDTH), BF16),
                   jax.ShapeDtypeStruct(bias_t.shape, F32),
                   jax.ShapeDtypeStruct((N_Q_HEADS, 128), F32),
                   jax.ShapeDtypeStruct((1, ATTN_WIDTH), F32)),
        scratch_shapes=[pltpu.VMEM((tq, ATTN_WIDTH), F32), pltpu.VMEM((tq, ATTN_WIDTH), F32),
                        pltpu.VMEM((tq + BLOCK, KV_WIDTH), BF16), pltpu.VMEM((tq + BLOCK, KV_WIDTH), BF16),
                        pltpu.VMEM((tq + BLOCK, KV_WIDTH), F32), pltpu.VMEM((tq + BLOCK, KV_WIDTH), F32),
                        pltpu.VMEM((2 * BLOCK, lanes), F32), pltpu.VMEM((2 * BLOCK, lanes), F32),
                        pltpu.VMEM((2 * BLOCK, lanes), BF16), pltpu.VMEM((2 * BLOCK, lanes), BF16),
                        pltpu.VMEM((SUBLANES, lanes), F32),
                        pltpu.VMEM((N_KV_HEADS, GROUP_ROWS, HEAD_DIM), BF16),
                        pltpu.VMEM((N_KV_HEADS, GROUP_ROWS, HEAD_DIM), BF16),
                        pltpu.VMEM((N_Q_HEADS, 128), F32)],
        compiler_params=_params("arbitrary"),
    )(sinks, dmixed, raw, q, k, k, v, v, bias_t, gain)


def _shift_down(cu, tail):
    row = lax.broadcasted_iota(jnp.int32, cu.shape, 0)
    t6, t7 = tail[6:7, :], tail[7:8, :]
    s1 = jnp.where(row == 0, t7, pltpu.roll(cu, 1, 0))
    s2 = jnp.where(row == 0, t6, jnp.where(row == 1, t7, pltpu.roll(cu, 2, 0)))
    return s1, s2


def _shift_up(d, head):
    n = d.shape[0]
    row = lax.broadcasted_iota(jnp.int32, d.shape, 0)
    h0, h1 = head[0:1, :], head[1:2, :]
    s1 = jnp.where(row == n - 1, h0, pltpu.roll(d, n - 1, 0))
    s2 = jnp.where(row == n - 1, h1, jnp.where(row == n - 2, h0, pltpu.roll(d, n - 2, 0)))
    return s1, s2


def _mixout_fwd(x, attn_n, u, gb, gc, conv_w, gain, w_out, name, tm):
    T, D = x.shape

    def body(x_ref, an_ref, u_ref, b_ref, c_ref, cw_ref, g_ref, wo_ref, xo_ref, cn_ref, tail_sc):
        @pl.when(pl.program_id(0) == 0)
        def _():
            tail_sc[...] = jnp.zeros_like(tail_sc)

        cu = c_ref[...].astype(F32) * u_ref[...].astype(F32)
        s1, s2 = _shift_down(cu, tail_sc[...])
        tail_sc[...] = cu[tm - SUBLANES:tm, :]
        pre = cw_ref[0:1, :] * s2 + cw_ref[1:2, :] * s1 + cw_ref[2:3, :] * cu
        conv = b_ref[...].astype(F32) * pre
        _, chat = _rms_stats(conv)
        cn = (chat * g_ref[...]).astype(BF16)
        cn_ref[...] = cn
        xo_ref[...] = (x_ref[...] + _dot(an_ref[...], wo_ref[0:ATTN_WIDTH, :])
                       + _dot(cn, wo_ref[ATTN_WIDTH:ATTN_WIDTH + CONV_DIM, :]))

    row = lambda i: (i, 0)
    const = lambda i: (0, 0)
    return pl.pallas_call(
        body, name=name, grid=(T // tm,),
        in_specs=[pl.BlockSpec((tm, D), row), pl.BlockSpec((tm, ATTN_WIDTH), row),
                  pl.BlockSpec((tm, CONV_DIM), row), pl.BlockSpec((tm, CONV_DIM), row),
                  pl.BlockSpec((tm, CONV_DIM), row),
                  pl.BlockSpec(conv_w.shape, const), pl.BlockSpec((1, CONV_DIM), const),
                  pl.BlockSpec(w_out.shape, const)],
        out_specs=(pl.BlockSpec((tm, D), row), pl.BlockSpec((tm, CONV_DIM), row)),
        out_shape=(jax.ShapeDtypeStruct((T, D), F32), jax.ShapeDtypeStruct((T, CONV_DIM), BF16)),
        scratch_shapes=[pltpu.VMEM((SUBLANES, CONV_DIM), F32)],
        compiler_params=_params("arbitrary"),
    )(x, attn_n, u, gb, gc, conv_w, gain, w_out)


def _mixout_bwd(dy, attn_n, conv_n, w_out, after, name, tm):
    T, D = dy.shape
    W = ATTN_WIDTH + CONV_DIM
    nt = T // tm

    def body(dy_ref, an_ref, cn_ref, w_ref, after_ref, dm_ref, dw_ref, dw_sc):
        i = pl.program_id(0)

        @pl.when(i == 0)
        def _():
            dw_sc[...] = jnp.zeros_like(dw_sc)

        dyb = dy_ref[...].astype(BF16)
        dm_ref[...] = _dot_nt(dyb, w_ref[...]).astype(BF16)
        dw_sc[0:ATTN_WIDTH, :] += _dot_tn(an_ref[...], dyb)
        dw_sc[ATTN_WIDTH:W, :] += _dot_tn(cn_ref[...], dyb)

        @pl.when(i == nt - 1)
        def _():
            dw_ref[...] = dw_sc[...].astype(BF16)

    row = lambda i: (i, 0)
    const = lambda i: (0, 0)
    return pl.pallas_call(
        body, name=name, grid=(nt,),
        in_specs=[pl.BlockSpec((tm, D), row), pl.BlockSpec((tm, ATTN_WIDTH), row),
                  pl.BlockSpec((tm, CONV_DIM), row), pl.BlockSpec(w_out.shape, const),
                  pl.BlockSpec(memory_space=pl.ANY)],
        out_specs=(pl.BlockSpec((tm, W), row), pl.BlockSpec((W, D), const)),
        out_shape=(jax.ShapeDtypeStruct((T, W), BF16), jax.ShapeDtypeStruct((W, D), BF16)),
        scratch_shapes=[pltpu.VMEM((W, D), F32)],
        compiler_params=_params("arbitrary"),
    )(dy, attn_n, conv_n, w_out, after)


def _conv_bwd(dmixed, u, gb, gc, conv_w, gain, name, tc):
    T = u.shape[0]
    nt = T // tc
    per8 = tc // SUBLANES

    def body(dm_ref, u_ref, b_ref, c_ref, ut_ref, ct_ref, cw_ref, g_ref,
             du_ref, db_ref, dc_ref, dcw_ref, dgain_ref, head_sc):
        i = pl.program_id(0)

        @pl.when(i == 0)
        def _():
            head_sc[...] = jnp.zeros_like(head_sc)
            dcw_ref[...] = jnp.zeros_like(dcw_ref)
            dgain_ref[...] = jnp.zeros_like(dgain_ref)

        uu = u_ref[...].astype(F32)
        cc = c_ref[...].astype(F32)
        bb = b_ref[...].astype(F32)
        cu = cc * uu
        tail = jnp.where(i == nt - 1, 0.0, ct_ref[...].astype(F32) * ut_ref[...].astype(F32))
        s1, s2 = _shift_down(cu, tail)
        w0, w1, w2 = cw_ref[0:1, :], cw_ref[1:2, :], cw_ref[2:3, :]
        pre = w0 * s2 + w1 * s1 + w2 * cu
        dconv, dgain = _rms_bwd(dm_ref[...].astype(F32), bb * pre, g_ref[...])
        dgain_ref[...] += dgain
        db_ref[...] = (dconv * pre).astype(BF16)
        dpre = dconv * bb
        dcw_ref[0:1, :] += jnp.sum(dpre * s2, axis=0, keepdims=True)
        dcw_ref[1:2, :] += jnp.sum(dpre * s1, axis=0, keepdims=True)
        dcw_ref[2:3, :] += jnp.sum(dpre * cu, axis=0, keepdims=True)
        n1, n2 = _shift_up(dpre, head_sc[...])
        head_sc[...] = dpre[0:SUBLANES, :]
        dcu = w2 * dpre + w1 * n1 + w0 * n2
        du_ref[...] = (dcu * cc).astype(BF16)
        dc_ref[...] = (dcu * uu).astype(BF16)

    rev = lambda i: (nt - 1 - i, 0)
    rev_right = lambda i: (nt - 1 - i, 1)
    tail_map = lambda i: (jnp.maximum((nt - 1 - i) * per8 - 1, 0), 0)
    const = lambda i: (0, 0)
    return pl.pallas_call(
        body, name=name, grid=(nt,),
        in_specs=[pl.BlockSpec((tc, CONV_DIM), rev_right),
                  pl.BlockSpec((tc, CONV_DIM), rev), pl.BlockSpec((tc, CONV_DIM), rev),
                  pl.BlockSpec((tc, CONV_DIM), rev),
                  pl.BlockSpec((SUBLANES, CONV_DIM), tail_map), pl.BlockSpec((SUBLANES, CONV_DIM), tail_map),
                  pl.BlockSpec(conv_w.shape, const), pl.BlockSpec((1, CONV_DIM), const)],
        out_specs=(pl.BlockSpec((tc, CONV_DIM), rev), pl.BlockSpec((tc, CONV_DIM), rev),
                   pl.BlockSpec((tc, CONV_DIM), rev),
                   pl.BlockSpec((SUBLANES, CONV_DIM), const), pl.BlockSpec((1, CONV_DIM), const)),
        out_shape=(jax.ShapeDtypeStruct((T, CONV_DIM), BF16), jax.ShapeDtypeStruct((T, CONV_DIM), BF16),
                   jax.ShapeDtypeStruct((T, CONV_DIM), BF16),
                   jax.ShapeDtypeStruct((SUBLANES, CONV_DIM), F32), jax.ShapeDtypeStruct((1, CONV_DIM), F32)),
        scratch_shapes=[pltpu.VMEM((SUBLANES, CONV_DIM), F32)],
        compiler_params=_params("arbitrary"),
    )(dmixed, u, gb, gc, u, gc, conv_w, gain)


def _mixin_bwd(dy, x, gain, dz, w_in_t, name, tm):
    T, D = x.shape
    nz = len(dz)

    def body(dy_ref, x_ref, g_ref, *rest):
        dz_refs, wt_ref, dx_ref, dgain_ref = rest[:nz], rest[nz], rest[nz + 1], rest[nz + 2]

        @pl.when(pl.program_id(0) == 0)
        def _():
            dgain_ref[...] = jnp.zeros_like(dgain_ref)

        dh = jnp.zeros((tm, D), F32)
        for r, lo, hi in zip(dz_refs, _Z_SPLITS[:-1], _Z_SPLITS[1:]):
            dh += _dot(r[...], wt_ref[lo:hi, :])
        dx, dgain = _rms_bwd(dh, x_ref[...], g_ref[...])
        dgain_ref[...] += dgain
        dx_ref[...] = dy_ref[...] + dx

    row = lambda i: (i, 0)
    const = lambda i: (0, 0)
    return pl.pallas_call(
        body, name=name, grid=(T // tm,),
        in_specs=[pl.BlockSpec((tm, D), row), pl.BlockSpec((tm, D), row), pl.BlockSpec((1, D), const)]
                 + [pl.BlockSpec((tm, a.shape[1]), row) for a in dz]
                 + [pl.BlockSpec(w_in_t.shape, const)],
        out_specs=(pl.BlockSpec((tm, D), row), pl.BlockSpec((1, D), const)),
        out_shape=(jax.ShapeDtypeStruct((T, D), F32), jax.ShapeDtypeStruct((1, D), F32)),
        compiler_params=_params("arbitrary"),
    )(dy, x, gain, *dz, w_in_t)


def _win_grad(dz, hn, name, tk):
    T, D = hn.shape
    nz = len(dz)
    nt = T // tk
    W = _Z_SPLITS[-1]

    def body(hn_ref, *rest):
        dz_refs, dw_ref, dw_sc = rest[:nz], rest[nz], rest[nz + 1]
        i = pl.program_id(0)

        @pl.when(i == 0)
        def _():
            dw_sc[...] = jnp.zeros_like(dw_sc)

        hn = hn_ref[...]
        for r, lo, hi in zip(dz_refs, _Z_SPLITS[:-1], _Z_SPLITS[1:]):
            dw_sc[lo:hi, :] += _dot_tn(r[...], hn)

        @pl.when(i == nt - 1)
        def _():
            dw_ref[...] = dw_sc[...].astype(BF16)

    row = lambda i: (i, 0)
    return pl.pallas_call(
        body, name=name, grid=(nt,),
        in_specs=[pl.BlockSpec((tk, D), row)] + [pl.BlockSpec((tk, a.shape[1]), row) for a in dz],
        out_specs=pl.BlockSpec((W, D), lambda i: (0, 0)),
        out_shape=jax.ShapeDtypeStruct((W, D), BF16),
        scratch_shapes=[pltpu.VMEM((W, D), F32)],
        compiler_params=_params("arbitrary"),
    )(hn, *dz)


def _loss_head(x, target, gain, name, tm):
    T, D = x.shape

    def body(x_ref, t_ref, g_ref, dx_ref, dgain_ref, loss_ref):
        @pl.when(pl.program_id(0) == 0)
        def _():
            dgain_ref[...] = jnp.zeros_like(dgain_ref)
            loss_ref[...] = jnp.zeros_like(loss_ref)

        xv = x_ref[...]
        gain_v = g_ref[...]
        _, xhat = _rms_stats(xv)
        err = xhat * gain_v - t_ref[...]
        part = 0.5 * jnp.sum(jnp.mean(err * err, axis=-1, keepdims=True), axis=0, keepdims=True)
        loss_ref[...] += part
        dx, dgain = _rms_bwd(err * (1.0 / D), xv, gain_v)
        dgain_ref[...] += dgain
        dx_ref[...] = dx

    row = lambda i: (i, 0)
    const = lambda i: (0, 0)
    return pl.pallas_call(
        body, name=name, grid=(T // tm,),
        in_specs=[pl.BlockSpec((tm, D), row), pl.BlockSpec((tm, D), row), pl.BlockSpec((1, D), const)],
        out_specs=(pl.BlockSpec((tm, D), row), pl.BlockSpec((1, D), const),
                   pl.BlockSpec((SUBLANES, 128), const)),
        out_shape=(jax.ShapeDtypeStruct((T, D), F32), jax.ShapeDtypeStruct((1, D), F32),
                   jax.ShapeDtypeStruct((SUBLANES, 128), F32)),
        compiler_params=_params("arbitrary"),
    )(x, target, gain)


def _sum_parts(parts, name, tr):
    P, R, C = parts.shape

    def body(p_ref, o_ref):
        g = p_ref[0].astype(F32)
        for d in range(1, P):
            g = g + p_ref[d].astype(F32)
        o_ref[...] = g

    return pl.pallas_call(
        body, name=name, grid=(R // tr,),
        in_specs=[pl.BlockSpec((P, tr, C), lambda i: (0, i, 0))],
        out_specs=pl.BlockSpec((tr, C), lambda i: (i, 0)),
        out_shape=jax.ShapeDtypeStruct((R, C), F32),
        compiler_params=_params("arbitrary"),
    )(parts)


def _adamw(parts, w, m, v, name, tr):
    P = parts.shape[0]
    R, C = w.shape

    def body(p_ref, w_ref, m_ref, v_ref, g_ref, d_ref, nm_ref, nv_ref):
        g = p_ref[0].astype(F32)
        for d in range(1, P):
            g = g + p_ref[d].astype(F32)
        nm = ADAM_B1 * m_ref[...] + (1.0 - ADAM_B1) * g
        nv = ADAM_B2 * v_ref[...] + (1.0 - ADAM_B2) * (g * g)
        m_hat = nm / (1.0 - ADAM_B1 ** ADAM_STEP)
        v_hat = nv / (1.0 - ADAM_B2 ** ADAM_STEP)
        g_ref[...] = g
        nm_ref[...] = nm
        nv_ref[...] = nv
        d_ref[...] = -ADAM_LR * (m_hat / (jnp.sqrt(v_hat) + ADAM_EPS) + ADAM_WD * w_ref[...])

    row = lambda i: (i, 0)
    spec = pl.BlockSpec((tr, C), row)
    shp = jax.ShapeDtypeStruct((R, C), F32)
    return pl.pallas_call(
        body, name=name, grid=(R // tr,),
        in_specs=[pl.BlockSpec((P, tr, C), lambda i: (0, i, 0)), spec, spec, spec],
        out_specs=(spec, spec, spec, spec),
        out_shape=(shp, shp, shp, shp),
        compiler_params=_params("arbitrary"),
    )(parts, w, m, v)


def _columns_of_blocks(g):
    n, R, w = g.shape
    return g.transpose(1, 0, 2).reshape(R, n * w)


def _pad_row(vec):
    vec = vec.reshape(1, -1)
    return jnp.pad(vec, ((0, 0), (0, PACK_COLS - vec.shape[1])))


def kernel(x, rel_bias_table, ffn1_norm, ffn1_w_gate, ffn1_w_up, ffn1_w_down, mix_norm, w_in, conv_w, attn_sinks, attn_out_norm, conv_out_norm, w_out, ffn2_norm, ffn2_w_gate, ffn2_w_up, ffn2_w_down, final_norm, loss_target, m_rel_bias_table, m_ffn1_norm, m_ffn1_w_gate, m_ffn1_w_up, m_ffn1_w_down, m_mix_norm, m_w_in, m_conv_w, m_attn_sinks, m_attn_out_norm, m_conv_out_norm, m_w_out, m_ffn2_norm, m_ffn2_w_gate, m_ffn2_w_up, m_ffn2_w_down, m_final_norm, v_rel_bias_table, v_ffn1_norm, v_ffn1_w_gate, v_ffn1_w_up, v_ffn1_w_down, v_mix_norm, v_w_in, v_conv_w, v_attn_sinks, v_attn_out_norm, v_conv_out_norm, v_w_out, v_ffn2_norm, v_ffn2_w_gate, v_ffn2_w_up, v_ffn2_w_down, v_final_norm):
    T, D = x.shape[1], x.shape[2]
    x0 = x[0]
    target = loss_target[0]
    tm = min(TM_FFN, T)
    tm_bwd = min(TM_FFN_BWD, T)
    tm_mix = min(TM_MIX, T)
    tk = min(TK_WGRAD, T)
    tf = TF_FFN
    nblk = min(ATTN_BLOCKS, T // BLOCK)
    me = 4 * lax.axis_index("x") + 2 * lax.axis_index("y") + lax.axis_index("c")

    big = {
        "ffn1_w_gate": (ffn1_w_gate[0], m_ffn1_w_gate[0], v_ffn1_w_gate[0], True),
        "ffn1_w_up": (ffn1_w_up[0], m_ffn1_w_up[0], v_ffn1_w_up[0], True),
        "ffn1_w_down": (ffn1_w_down[0], m_ffn1_w_down[0], v_ffn1_w_down[0], False),
        "w_in": (w_in[0], m_w_in[0], v_w_in[0], True),
        "w_out": (w_out[0], m_w_out[0], v_w_out[0], False),
        "ffn2_w_gate": (ffn2_w_gate[0], m_ffn2_w_gate[0], v_ffn2_w_gate[0], True),
        "ffn2_w_up": (ffn2_w_up[0], m_ffn2_w_up[0], v_ffn2_w_up[0], True),
        "ffn2_w_down": (ffn2_w_down[0], m_ffn2_w_down[0], v_ffn2_w_down[0], False),
    }

    def block_to_send(name):
        w, _, _, transposed = big[name]
        return (w.T if transposed else w).astype(BF16)

    names1 = ["ffn1_w_gate", "ffn1_w_up", "ffn1_w_down"]
    names_rest = ["w_in", "w_out", "ffn2_w_gate", "ffn2_w_up", "ffn2_w_down"]
    h1, token = _exchange_start([block_to_send(n) for n in names1], False, "gather_start_ffn1", ffn1_norm)
    h_rest, token = _exchange_start([block_to_send(n) for n in names_rest[:2]] + [conv_w[0]]
                                    + [block_to_send(n) for n in names_rest[2:]], False,
                                    "gather_start_rest", token)
    wgt1, wut1, wd1 = [g.reshape(-1, D) for g in _exchange_wait(h1, False, "gather_wait_ffn1", token)]

    wgu1 = _stack_gate_up(wgt1, wut1, tf)
    x1, xn1, gu1 = _ffn_fwd(x0, ffn1_norm, wgu1, wd1, "ffn1_fwd", tm, tf)
    mixw = _exchange_wait(h_rest[:3], False, "gather_wait_mix", x1)
    win_t = mixw[0].reshape(-1, D)
    wout = mixw[1].reshape(-1, D)
    cw = _columns_of_blocks(mixw[2])
    hn, q, k, v, u, gb, gc = _mixin_fwd(x1, mix_norm, win_t, "mixin_fwd", tm_mix)
    bucket = jnp.asarray(_bucket_table().T.copy())
    sinks = attn_sinks.reshape(-1)
    bias_t = _bias_build(rel_bias_table, bucket, "bias_build")
    attn_raw, attn_n = _attn_fwd(q, k, v, bias_t, sinks, attn_out_norm, "attn_fwd", nblk)
    x2, conv_n = _mixout_fwd(x1, attn_n, u, gb, gc, cw, conv_out_norm, wout, "mixout_fwd", tm_mix)
    wgt2, wut2, wd2 = [g.reshape(-1, D) for g in _exchange_wait(h_rest[3:], False, "gather_wait_ffn2", x2)]
    wgu2 = _stack_gate_up(wgt2, wut2, tf)
    x3, xn2, gu2 = _ffn_fwd(x2, ffn2_norm, wgu2, wd2, "ffn2_fwd", tm, tf)

    def blocks(g):
        return g.reshape(N_DEV, -1, D)

    dx3, d_final, loss_part = _loss_head(x3, target, final_norm.reshape(1, D), "loss_head", tm_mix)
    dx2, d_ffn2_norm, dgu2, hh2, do2 = _ffn_bwd(
        dx3, x2, ffn2_norm, gu2, wgu2, wd2, dx3, "ffn2_bwd", tm_bwd, tf)
    grads2 = _ffn_wgrad(dgu2, hh2, xn2, do2, "ffn2_wgrad", tk, tf)
    handles2, token2 = _exchange_start([blocks(g) for g in grads2], True, "grads_start_ffn2", dx2)

    dmixed, d_wout = _mixout_bwd(dx2, attn_n, conv_n, wout, token2, "mixout_bwd", tm_mix)
    dq, dk, dv, dbias, dsink, d_attn_norm = _attn_bwd(
        dmixed, attn_raw, q, k, v, bias_t, sinks, attn_out_norm, "attn_bwd", nblk)
    du, dgb, dgc, d_cw, d_conv_norm = _conv_bwd(dmixed, u, gb, gc, cw, conv_out_norm, "conv_bwd", tm_mix)
    d_table = _bias_grad(dbias, bucket, "bias_grad")
    dz = [dq, dk, dv, du, dgb, dgc]
    dx1, d_mix_norm = _mixin_bwd(dx2, x1, mix_norm, dz, win_t, "mixin_bwd", tm_mix)
    d_win_t = _win_grad(dz, hn, "win_grad", min(TM_MIX, T))
    handles_mix, token_mix = _exchange_start([blocks(d_win_t), blocks(d_wout)], True, "grads_start_mix", dx1)

    dx0, d_ffn1_norm, dgu1, hh1, do1 = _ffn_bwd(
        dx1, x0, ffn1_norm, gu1, wgu1, wd1, token_mix, "ffn1_bwd", tm_bwd, tf)
    grads1 = _ffn_wgrad(dgu1, hh1, xn1, do1, "ffn1_wgrad", tk, tf)
    handles1, token1 = _exchange_start([blocks(g) for g in grads1], True, "grads_start_ffn1", dx0)

    res = {}

    def update(names, parts):
        last = None
        for name, p in zip(names, parts):
            w, m_, v_, transposed = big[name]
            if transposed:
                summed = _sum_parts(p, "sum_" + name, _row_tile(p.shape[1], ADAM_ROWS))
                p = summed.T[None]
            g, dl, nm, nv = _adamw(p, w, m_, v_, "adamw_" + name, _row_tile(w.shape[0], ADAM_ROWS))
            res[name] = tuple(a[None] for a in (g, dl, nm, nv))
            last = g
        return last

    parts2 = _exchange_wait(handles2, True, "grads_wait_ffn2", token1)
    done2 = update(["ffn2_w_gate", "ffn2_w_up", "ffn2_w_down"], parts2)
    parts_mix = _exchange_wait(handles_mix, True, "grads_wait_mix", done2)
    done_mix = update(["w_in", "w_out"], parts_mix)
    parts1 = _exchange_wait(handles1, True, "grads_wait_ffn1", done_mix)
    update(names1, parts1)

    def pack(ffn1, mixn, ffn2, fin, attn_n_, conv_n_, sink_, extra, convw, table):
        rows = [_pad_row(ffn1), _pad_row(mixn), _pad_row(ffn2), _pad_row(fin),
                _pad_row(jnp.concatenate([attn_n_.reshape(-1), conv_n_.reshape(-1)])),
                _pad_row(sink_), _pad_row(extra),
                jnp.zeros((1, PACK_COLS), F32),
                jnp.pad(convw, ((0, 0), (0, PACK_COLS - convw.shape[1]))),
                _pad_row(table),
                jnp.zeros((PACK_ROWS - 12, PACK_COLS), F32)]
        return jnp.concatenate(rows, axis=0)

    def own_channels(a):
        full = jnp.zeros((a.shape[1], CONV_DIM), F32)
        return lax.dynamic_update_slice(full, a[0], (0, me * a.shape[2]))

    g_pack = pack(d_ffn1_norm, d_mix_norm, d_ffn2_norm, d_final, d_attn_norm, d_conv_norm,
                  dsink[:, 0], loss_part[0, :1], d_cw[:3], d_table[:, :N_Q_HEADS])
    zero1 = jnp.zeros((1,), F32)
    w_pack = pack(ffn1_norm, mix_norm, ffn2_norm, final_norm, attn_out_norm, conv_out_norm,
                  attn_sinks, zero1, own_channels(conv_w), rel_bias_table)
    m_pack = pack(m_ffn1_norm, m_mix_norm, m_ffn2_norm, m_final_norm, m_attn_out_norm, m_conv_out_norm,
                  m_attn_sinks, zero1, own_channels(m_conv_w), m_rel_bias_table)
    v_pack = pack(v_ffn1_norm, v_mix_norm, v_ffn2_norm, v_final_norm, v_attn_out_norm, v_conv_out_norm,
                  v_attn_sinks, zero1, own_channels(v_conv_w), v_rel_bias_table)
    (g_all,) = _exchange([g_pack], False, "gather_small")
    packs = _adamw(g_all, w_pack, m_pack, v_pack, "adamw_small", PACK_ROWS)

    def unpack(pk):
        cwb = lax.dynamic_slice(pk[8:11, :CONV_DIM], (0, me * conv_w.shape[2]), (3, conv_w.shape[2]))
        return {
            "ffn1_norm": pk[0:1, :D], "mix_norm": pk[1:2, :D], "ffn2_norm": pk[2:3, :D],
            "final_norm": pk[3, :D],
            "attn_out_norm": pk[4:5, :ATTN_WIDTH], "conv_out_norm": pk[4:5, ATTN_WIDTH:ATTN_WIDTH + CONV_DIM],
            "attn_sinks": pk[5:6, :N_Q_HEADS],
            "conv_w": cwb[None],
            "rel_bias_table": pk[11, :NUM_BUCKETS * N_Q_HEADS].reshape(NUM_BUCKETS, N_Q_HEADS),
        }

    small = [unpack(pk) for pk in packs]
    loss = packs[0][6, 0]

    order = ["rel_bias_table", "ffn1_norm", "ffn1_w_gate", "ffn1_w_up", "ffn1_w_down", "mix_norm", "w_in",
             "conv_w", "attn_sinks", "attn_out_norm", "conv_out_norm", "w_out", "ffn2_norm",
             "ffn2_w_gate", "ffn2_w_up", "ffn2_w_down", "final_norm"]
    outs = [loss, dx0[None]]
    for kind in range(4):
        for name in order:
            outs.append(res[name][kind] if name in res else small[kind][name])
    return tuple(outs)
```

```python
import math

import numpy as np
import jax
import jax.numpy as jnp
from jax import lax
from jax.experimental import pallas as pl
from jax.experimental.pallas import tpu as pltpu

F32 = jnp.float32
BF16 = jnp.bfloat16

N_DEV = 8
EPS = 1e-6
HEAD_DIM = 64
N_Q_HEADS = 8
N_KV_HEADS = 2
GQA_GROUP = 4
ATTN_WIDTH = 512
KV_WIDTH = 128
CONV_DIM = 512
BLOCK = 128
WINDOW = 128
NUM_BUCKETS = 32
MAX_DISTANCE = 128
SCALE = HEAD_DIM ** -0.5
MASKED = -1e30
GROUP_ROWS = GQA_GROUP * BLOCK

ADAM_LR = 0.001
ADAM_B1 = 0.9
ADAM_B2 = 0.999
ADAM_EPS = 1e-08
ADAM_WD = 0.01
ADAM_STEP = 10

VMEM_LIMIT_BYTES = 40 * 1024 * 1024
SUBLANES = 8
PACK_ROWS = 16
PACK_COLS = 1024

TM_FFN = 1024
TM_FFN_BWD = 1024
TM_MIX = 512
TK_WGRAD = 1024
TF_FFN = 256
BM_WGRAD = 1408
ROW_GROUPS = 4
ATTN_BLOCKS = 4
ADAM_ROWS = 256


def _row_tile(rows, limit):
    best = rows
    for t in range(16, min(rows, limit) + 1, 16):
        if rows % t == 0:
            best = t
    return best


def _params(*sem):
    return pltpu.CompilerParams(dimension_semantics=sem, vmem_limit_bytes=VMEM_LIMIT_BYTES)


def _dot(a, b):
    return jnp.dot(a, b, preferred_element_type=F32)


def _dot_nt(a, b):
    return lax.dot_general(a, b, (((1,), (1,)), ((), ())), preferred_element_type=F32)


def _dot_tn(a, b):
    return lax.dot_general(a, b, (((0,), (0,)), ((), ())), preferred_element_type=F32)


def _sigmoid(g):
    return 0.5 * jnp.tanh(0.5 * g) + 0.5


def _rms_stats(x):
    inv = lax.rsqrt(jnp.mean(x * x, axis=-1, keepdims=True) + EPS)
    return inv, x * inv


def _rms_bwd(dy, x, gain):
    inv, xhat = _rms_stats(x)
    dgain = jnp.sum(dy * xhat, axis=0, keepdims=True)
    dxh = dy * gain
    dx = inv * (dxh - xhat * jnp.mean(dxh * xhat, axis=-1, keepdims=True))
    return dx, dgain


def _peer_list():
    x, y, c = lax.axis_index("x"), lax.axis_index("y"), lax.axis_index("c")
    peers = []
    for k in range(1, N_DEV):
        px = 1 - x if (k >> 2) & 1 else x
        py = 1 - y if (k >> 1) & 1 else y
        pc = 1 - c if k & 1 else c
        peers.append((px, py, pc))
    return 4 * x + 2 * y + c, peers


def _exchange(arrs, scatter, name, after):
    n = len(arrs)
    out_shape = []
    for a in arrs:
        shp = a.shape if scatter else (N_DEV,) + a.shape
        out_shape.append(jax.ShapeDtypeStruct(shp, a.dtype))

    def body(*refs):
        ins, outs = refs[:n], refs[n + 1:2 * n + 1]
        send_sems, recv_sems, local_sems = refs[2 * n + 1:]
        me, peers = _peer_list()
        started = []
        for a in range(n):
            own = ins[a].at[me] if scatter else ins[a]
            loc = pltpu.make_async_copy(own, outs[a].at[me], local_sems.at[a])
            loc.start()
            started.append(loc)
        sends = []
        for a in range(n):
            for k, (px, py, pc) in enumerate(peers):
                src = ins[a].at[4 * px + 2 * py + pc] if scatter else ins[a]
                cp = pltpu.make_async_remote_copy(
                    src_ref=src, dst_ref=outs[a].at[me],
                    send_sem=send_sems.at[a, k], recv_sem=recv_sems.at[a, k],
                    device_id=(px, py, pc), device_id_type=pl.DeviceIdType.MESH)
                cp.start()
                sends.append(cp)
        for a in range(n):
            for k, (px, py, pc) in enumerate(peers):
                landed = outs[a].at[4 * px + 2 * py + pc]
                pltpu.make_async_remote_copy(
                    src_ref=landed, dst_ref=landed,
                    send_sem=send_sems.at[a, k], recv_sem=recv_sems.at[a, k],
                    device_id=(px, py, pc), device_id_type=pl.DeviceIdType.MESH).wait_recv()
        for cp in sends:
            cp.wait_send()
        for loc in started:
            loc.wait()

    hbm = pl.BlockSpec(memory_space=pl.ANY)
    return pl.pallas_call(
        body, name=name, out_shape=tuple(out_shape),
        in_specs=[hbm] * (n + 1), out_specs=tuple([hbm] * n),
        scratch_shapes=[pltpu.SemaphoreType.DMA((n, N_DEV - 1)),
                        pltpu.SemaphoreType.DMA((n, N_DEV - 1)),
                        pltpu.SemaphoreType.DMA((n,))],
    )(*arrs, after)


_HBM = pl.BlockSpec(memory_space=pltpu.HBM)
_SEM = pl.BlockSpec(memory_space=pltpu.SEMAPHORE)
_EFFECT = pltpu.SideEffectType.DATAFLOW_SIDE_EFFECTING


def _split_copies(srcs, lands, send_sems, recv_sems, scatter):
    me, peers = _peer_list()
    copies = []
    for a in range(len(srcs)):
        for k, (px, py, pc) in enumerate(peers):
            src = srcs[a].at[4 * px + 2 * py + pc] if scatter else srcs[a]
            copies.append(pltpu.make_async_remote_copy(
                src_ref=src, dst_ref=lands[a].at[me],
                send_sem=send_sems[a].at[k], recv_sem=recv_sems[a].at[k],
                device_id=(px, py, pc), device_id_type=pl.DeviceIdType.MESH))
    return copies


def _exchange_start(arrs, scatter, name, after):
    n = len(arrs)
    me = 4 * lax.axis_index("x") + 2 * lax.axis_index("y") + lax.axis_index("c")
    lands = []
    for a in arrs:
        own = lax.dynamic_index_in_dim(a, me, 0, keepdims=True) if scatter else a[None]
        shp = a.shape if scatter else (N_DEV,) + a.shape
        lands.append(lax.dynamic_update_slice(lax.empty(shp, a.dtype), own, (me,) + (0,) * (len(shp) - 1)))

    def body(*refs):
        srcs, lnds = refs[:n], refs[n:2 * n]
        outs = refs[2 * n + 1:]
        send_sems, recv_sems = outs[:n], outs[n:2 * n]
        token = outs[4 * n]
        for cp in _split_copies(srcs, lnds, send_sems, recv_sems, scatter):
            cp.start()
        token[...] = jnp.zeros_like(token)

    sem = pltpu.SemaphoreType.DMA((N_DEV - 1,))
    out_shape = ([sem] * (2 * n) + [pltpu.HBM(a.shape, a.dtype) for a in arrs]
                 + [pltpu.HBM(l.shape, l.dtype) for l in lands] + [jax.ShapeDtypeStruct((SUBLANES, 128), F32)])
    res = pl.pallas_call(
        body, name=name, out_shape=tuple(out_shape),
        in_specs=[_HBM] * (2 * n) + [pl.BlockSpec(memory_space=pl.ANY)],
        out_specs=tuple([_SEM] * (2 * n) + [_HBM] * (2 * n) + [pl.BlockSpec(memory_space=pltpu.VMEM)]),
        input_output_aliases={i: 2 * n + i for i in range(2 * n)},
        compiler_params=pltpu.CompilerParams(has_side_effects=_EFFECT),
    )(*[pltpu.with_memory_space_constraint(a, pltpu.HBM) for a in arrs],
      *[pltpu.with_memory_space_constraint(l, pltpu.HBM) for l in lands], after)
    handles = [(res[2 * n + a], res[3 * n + a], res[a], res[n + a]) for a in range(n)]
    return handles, res[4 * n]


def _exchange_wait(handles, scatter, name, after):
    n = len(handles)

    def body(*refs):
        srcs, lnds = refs[:n], refs[n:2 * n]
        send_sems, recv_sems = refs[2 * n:3 * n], refs[3 * n:4 * n]
        for cp in _split_copies(srcs, lnds, send_sems, recv_sems, scatter):
            cp.wait_send()
            cp.wait_recv()

    srcs = [h[0] for h in handles]
    lands = [h[1] for h in handles]
    res = pl.pallas_call(
        body, name=name,
        out_shape=tuple([pltpu.HBM(a.shape, a.dtype) for a in srcs] + [pltpu.HBM(l.shape, l.dtype) for l in lands]),
        in_specs=[_HBM] * (2 * n) + [_SEM] * (2 * n) + [pl.BlockSpec(memory_space=pl.ANY)],
        out_specs=tuple([_HBM] * (2 * n)),
        input_output_aliases={i: i for i in range(2 * n)},
        compiler_params=pltpu.CompilerParams(has_side_effects=_EFFECT),
    )(*srcs, *lands, *[h[2] for h in handles], *[h[3] for h in handles], after)
    return list(res[n:])


def _stack_gate_up(wgt, wut, tf):
    F, D = wgt.shape
    return jnp.stack([wgt.reshape(F // tf, tf, D), wut.reshape(F // tf, tf, D)], axis=1).reshape(2 * F, D)


def _unstack_gate_up(w, tf):
    F2, D = w.shape
    w4 = w.reshape(F2 // (2 * tf), 2, tf, D)
    return w4[:, 0].reshape(F2 // 2, D), w4[:, 1].reshape(F2 // 2, D)


def _row_groups(tm):
    return [slice(r * (tm // ROW_GROUPS), (r + 1) * (tm // ROW_GROUPS)) for r in range(ROW_GROUPS)]


def _ffn_fwd(x, gain, wgu, wd, name, tm, tf):
    T, D = x.shape
    F = wd.shape[0]
    nj = F // tf

    def body(x_ref, g_ref, wgu_ref, wd_ref, xo_ref, xn_ref, gu_ref, xn_sc, acc_sc):
        j = pl.program_id(1)

        @pl.when(j == 0)
        def _():
            _, xhat = _rms_stats(x_ref[...])
            xn = (xhat * g_ref[...]).astype(BF16)
            xn_sc[...] = xn
            xn_ref[...] = xn
            acc_sc[...] = jnp.zeros_like(acc_sc)

        groups = _row_groups(tm)
        gus = [_dot_nt(xn_sc[rows, :], wgu_ref[...]) for rows in groups]
        hs = []
        for rows, gu in zip(groups, gus):
            gu_ref[rows, :] = gu.astype(BF16)
            g, u = gu[:, :tf], gu[:, tf:]
            hs.append((g * _sigmoid(g) * u).astype(BF16))
        for rows, h in zip(groups, hs):
            acc_sc[rows, :] += _dot(h, wd_ref[...])

        @pl.when(j == nj - 1)
        def _():
            xo_ref[...] = x_ref[...] + 0.5 * acc_sc[...]

    return pl.pallas_call(
        body, name=name, grid=(T // tm, nj),
        in_specs=[pl.BlockSpec((tm, D), lambda i, j: (i, 0)),
                  pl.BlockSpec((1, D), lambda i, j: (0, 0)),
                  pl.BlockSpec((2 * tf, D), lambda i, j: (j, 0)),
                  pl.BlockSpec((tf, D), lambda i, j: (j, 0))],
        out_specs=(pl.BlockSpec((tm, D), lambda i, j: (i, 0)),
                   pl.BlockSpec((tm, D), lambda i, j: (i, 0)),
                   pl.BlockSpec((tm, 2 * tf), lambda i, j: (i, j))),
        out_shape=(jax.ShapeDtypeStruct((T, D), F32), jax.ShapeDtypeStruct((T, D), BF16),
                   jax.ShapeDtypeStruct((T, 2 * F), BF16)),
        scratch_shapes=[pltpu.VMEM((tm, D), BF16), pltpu.VMEM((tm, D), F32)],
        compiler_params=_params("arbitrary", "arbitrary"),
    )(x, gain, wgu, wd)


def _ffn_bwd(dy, x, gain, gu, wgu, wd, after, name, tm, tf):
    T, D = x.shape
    F = wd.shape[0]
    nj = F // tf

    def body(dy_ref, x_ref, g_ref, gu_ref, wgu_ref, wd_ref, after_ref,
             dx_ref, dgain_ref, dgu_ref, hh_ref, do_ref, do_sc, acc_sc):
        i, j = pl.program_id(0), pl.program_id(1)

        @pl.when((i == 0) & (j == 0))
        def _():
            dgain_ref[...] = jnp.zeros_like(dgain_ref)

        @pl.when(j == 0)
        def _():
            do = (0.5 * dy_ref[...]).astype(BF16)
            do_sc[...] = do
            do_ref[...] = do
            acc_sc[...] = jnp.zeros_like(acc_sc)

        groups = _row_groups(tm)
        dhs = [_dot_nt(do_sc[rows, :], wd_ref[...]) for rows in groups]
        for rows, dh in zip(groups, dhs):
            g = gu_ref[rows, :tf].astype(F32)
            u = gu_ref[rows, tf:].astype(F32)
            sig = _sigmoid(g)
            s = g * sig
            dgu_ref[rows, :tf] = (dh * u * (sig + s * (1.0 - sig))).astype(BF16)
            dgu_ref[rows, tf:] = (dh * s).astype(BF16)
            hh_ref[rows, :] = (s * u).astype(BF16)
        for rows in groups:
            acc_sc[rows, :] += _dot(dgu_ref[rows, :], wgu_ref[...])

        @pl.when(j == nj - 1)
        def _():
            dx, dgain = _rms_bwd(acc_sc[...], x_ref[...], g_ref[...])
            dgain_ref[...] += dgain
            dx_ref[...] = dy_ref[...] + dx

    tile = pl.BlockSpec((tm, D), lambda i, j: (i, 0), pipeline_mode=pl.Buffered(1))
    return pl.pallas_call(
        body, name=name, grid=(T // tm, nj),
        in_specs=[tile, tile,
                  pl.BlockSpec((1, D), lambda i, j: (0, 0)),
                  pl.BlockSpec((tm, 2 * tf), lambda i, j: (i, j)),
                  pl.BlockSpec((2 * tf, D), lambda i, j: (j, 0)),
                  pl.BlockSpec((tf, D), lambda i, j: (j, 0)),
                  pl.BlockSpec(memory_space=pl.ANY)],
        out_specs=(tile,
                   pl.BlockSpec((1, D), lambda i, j: (0, 0)),
                   pl.BlockSpec((tm, 2 * tf), lambda i, j: (i, j)),
                   pl.BlockSpec((tm, tf), lambda i, j: (i, j)),
                   pl.BlockSpec((tm, D), lambda i, j: (i, 0))),
        out_shape=(jax.ShapeDtypeStruct((T, D), F32), jax.ShapeDtypeStruct((1, D), F32),
                   jax.ShapeDtypeStruct((T, 2 * F), BF16), jax.ShapeDtypeStruct((T, F), BF16),
                   jax.ShapeDtypeStruct((T, D), BF16)),
        scratch_shapes=[pltpu.VMEM((tm, D), BF16), pltpu.VMEM((tm, D), F32)],
        compiler_params=_params("arbitrary", "arbitrary"),
    )(dy, x, gain, gu, wgu, wd, after)


def _tn_grad(a, b, name, bm, tk):
    T, M = a.shape
    D = b.shape[1]
    nk = T // tk

    def body(a_ref, b_ref, o_ref, acc_sc):
        k = pl.program_id(1)

        @pl.when(k == 0)
        def _():
            acc_sc[...] = jnp.zeros_like(acc_sc)

        acc_sc[...] += _dot_tn(a_ref[...], b_ref[...])

        @pl.when(k == nk - 1)
        def _():
            o_ref[...] = acc_sc[...].astype(BF16)

    return pl.pallas_call(
        body, name=name, grid=(M // bm, nk),
        in_specs=[pl.BlockSpec((tk, bm), lambda i, k: (k, i)), pl.BlockSpec((tk, D), lambda i, k: (k, 0))],
        out_specs=pl.BlockSpec((bm, D), lambda i, k: (i, 0)),
        out_shape=jax.ShapeDtypeStruct((M, D), BF16),
        scratch_shapes=[pltpu.VMEM((bm, D), F32)],
        compiler_params=_params("arbitrary", "arbitrary"),
    )(a, b)


_Z_SPLITS = (0, 512, 640, 768, 1280, 1792, 2304)


def _mixin_fwd(x, gain, w_in_t, name, tm):
    T, D = x.shape
    widths = [b - a for a, b in zip(_Z_SPLITS[:-1], _Z_SPLITS[1:])]

    def body(x_ref, g_ref, w_ref, hn_ref, *outs):
        _, xhat = _rms_stats(x_ref[...])
        hn = (xhat * g_ref[...]).astype(BF16)
        hn_ref[...] = hn
        for o_ref, lo, hi in zip(outs, _Z_SPLITS[:-1], _Z_SPLITS[1:]):
            o_ref[...] = _dot_nt(hn, w_ref[lo:hi, :]).astype(BF16)

    return pl.pallas_call(
        body, name=name, grid=(T // tm,),
        in_specs=[pl.BlockSpec((tm, D), lambda i: (i, 0)),
                  pl.BlockSpec((1, D), lambda i: (0, 0)),
                  pl.BlockSpec(w_in_t.shape, lambda i: (0, 0))],
        out_specs=tuple([pl.BlockSpec((tm, D), lambda i: (i, 0))]
                        + [pl.BlockSpec((tm, w), lambda i: (i, 0)) for w in widths]),
        out_shape=tuple([jax.ShapeDtypeStruct((T, D), BF16)]
                        + [jax.ShapeDtypeStruct((T, w), BF16) for w in widths]),
        compiler_params=_params("arbitrary"),
    )(x, gain, w_in_t)


def _bucket_table():
    qi = np.arange(BLOCK, dtype=np.int32)[:, None]
    kj = np.arange(2 * BLOCK, dtype=np.int32)[None, :]
    dist = qi + BLOCK - kj
    n = np.maximum(dist, 0)
    max_exact = NUM_BUCKETS // 2
    large = max_exact + (np.log(np.maximum(n, 1).astype(np.float32) / max_exact)
                         / math.log(MAX_DISTANCE / max_exact)
                         * (NUM_BUCKETS - max_exact)).astype(np.int32)
    large = np.minimum(large, NUM_BUCKETS - 1)
    bucket = np.where(n < max_exact, n, large).astype(np.int32)
    valid = (dist >= 0) & (dist < WINDOW)
    return np.where(valid, bucket, -1).astype(np.int32)


def _bias_build(table, bucket, name):
    def body(t_ref, b_ref, o_ref):
        bk = b_ref[...]
        for h in range(N_Q_HEADS):
            def step(b, acc):
                return jnp.where(bk == b, t_ref[b, h], acc)
            o_ref[h] = lax.fori_loop(0, NUM_BUCKETS, step, jnp.full(bk.shape, MASKED, F32))

    return pl.pallas_call(
        body, name=name,
        in_specs=[pl.BlockSpec(memory_space=pltpu.SMEM), pl.BlockSpec(memory_space=pltpu.VMEM)],
        out_specs=pl.BlockSpec(memory_space=pltpu.VMEM),
        out_shape=jax.ShapeDtypeStruct((N_Q_HEADS,) + bucket.shape, F32),
    )(table, bucket)


def _bias_grad(dbias, bucket, name):
    def body(d_ref, b_ref, o_ref):
        bk = b_ref[...]
        row = lax.broadcasted_iota(jnp.int32, o_ref.shape, 0)
        lane = lax.broadcasted_iota(jnp.int32, o_ref.shape, 1)
        res = jnp.zeros(o_ref.shape, F32)
        for h in range(N_Q_HEADS):
            d = d_ref[h]

            def step(b, acc):
                tot = jnp.sum(jnp.where(bk == b, d, 0.0), axis=1, keepdims=True)
                tot = jnp.sum(tot, axis=0, keepdims=True)
                return jnp.where((row == b) & (lane == h), tot, acc)
            res = lax.fori_loop(0, NUM_BUCKETS, step, res)
        o_ref[...] = res

    return pl.pallas_call(
        body, name=name,
        in_specs=[pl.BlockSpec(memory_space=pltpu.VMEM), pl.BlockSpec(memory_space=pltpu.VMEM)],
        out_specs=pl.BlockSpec(memory_space=pltpu.VMEM),
        out_shape=jax.ShapeDtypeStruct((NUM_BUCKETS, 128), F32),
    )(dbias, bucket)


def _head_cols(h):
    return slice(h * HEAD_DIM, (h + 1) * HEAD_DIM)


def _stack_heads(ref, r0, g, dtype):
    return jnp.concatenate(
        [ref[pl.ds(r0, BLOCK), _head_cols(GQA_GROUP * g + j)].astype(dtype) for j in range(GQA_GROUP)], axis=0)


def _unstack_heads(ref, r0, g, val):
    for j in range(GQA_GROUP):
        ref[pl.ds(r0, BLOCK), _head_cols(GQA_GROUP * g + j)] = val[j * BLOCK:(j + 1) * BLOCK, :]


def _head_lanes(h):
    return slice(h * BLOCK, (h + 1) * BLOCK)


def _group_lanes(g):
    return slice(g * GROUP_ROWS, (g + 1) * GROUP_ROWS)


def _head_softmax(st, bias_t, sink, no_prev):
    s = st * SCALE + bias_t
    row = lax.broadcasted_iota(jnp.int32, s.shape, 0)
    s = jnp.where(no_prev & (row < BLOCK), MASKED, s)
    m = jnp.maximum(jnp.max(s, axis=0, keepdims=True), sink)
    p = jnp.exp(s - m)
    ps = jnp.exp(sink - m)
    r = 1.0 / (jnp.sum(p, axis=0, keepdims=True) + ps)
    return p * r, ps * r


def _load_band(kf_sc, vf_sc, kp_ref, kc_ref, vp_ref, vc_ref, tq):
    kf_sc[0:BLOCK, :] = kp_ref[...]
    kf_sc[BLOCK:BLOCK + tq, :] = kc_ref[...]
    vf_sc[0:BLOCK, :] = vp_ref[...]
    vf_sc[BLOCK:BLOCK + tq, :] = vc_ref[...]


def _attn_fwd(q, k, v, bias_t, sinks, gain, name, nblk):
    T = q.shape[0]
    tq = nblk * BLOCK

    def body(sink_ref, q_ref, kc_ref, kp_ref, vc_ref, vp_ref, bias_ref, g_ref, raw_ref, nrm_ref,
             kf_sc, vf_sc, o_sc, st_sc, pt_sc):
        i = pl.program_id(0)
        _load_band(kf_sc, vf_sc, kp_ref, kc_ref, vp_ref, vc_ref, tq)

        def block(b, carry):
            r0 = pl.multiple_of(b * BLOCK, BLOCK)
            no_prev = (i == 0) & (b == 0)
            for g in range(N_KV_HEADS):
                kb = kf_sc[pl.ds(r0, 2 * BLOCK), _head_cols(g)]
                st_sc[:, _group_lanes(g)] = _dot_nt(kb, _stack_heads(q_ref, r0, g, BF16))
            for h in range(N_Q_HEADS):
                p, _ = _head_softmax(st_sc[:, _head_lanes(h)], bias_ref[h], sink_ref[h], no_prev)
                pt_sc[:, _head_lanes(h)] = p.astype(BF16)
            for g in range(N_KV_HEADS):
                vb = vf_sc[pl.ds(r0, 2 * BLOCK), _head_cols(g)]
                _unstack_heads(o_sc, r0, g, _dot_tn(pt_sc[:, _group_lanes(g)], vb))
            return carry

        lax.fori_loop(0, nblk, block, 0)
        o = o_sc[...]
        raw_ref[...] = o.astype(BF16)
        _, ohat = _rms_stats(o)
        nrm_ref[...] = (ohat * g_ref[...]).astype(BF16)

    cur = lambda i: (i, 0)
    prev = lambda i: (jnp.maximum(i * nblk - 1, 0), 0)
    lanes = N_Q_HEADS * BLOCK
    return pl.pallas_call(
        body, name=name, grid=(T // tq,),
        in_specs=[pl.BlockSpec(memory_space=pltpu.SMEM),
                  pl.BlockSpec((tq, ATTN_WIDTH), cur),
                  pl.BlockSpec((tq, KV_WIDTH), cur), pl.BlockSpec((BLOCK, KV_WIDTH), prev),
                  pl.BlockSpec((tq, KV_WIDTH), cur), pl.BlockSpec((BLOCK, KV_WIDTH), prev),
                  pl.BlockSpec(bias_t.shape, lambda i: (0, 0, 0)),
                  pl.BlockSpec((1, ATTN_WIDTH), lambda i: (0, 0))],
        out_specs=(pl.BlockSpec((tq, ATTN_WIDTH), cur), pl.BlockSpec((tq, ATTN_WIDTH), cur)),
        out_shape=(jax.ShapeDtypeStruct((T, ATTN_WIDTH), BF16), jax.ShapeDtypeStruct((T, ATTN_WIDTH), BF16)),
        scratch_shapes=[pltpu.VMEM((tq + BLOCK, KV_WIDTH), BF16), pltpu.VMEM((tq + BLOCK, KV_WIDTH), BF16),
                        pltpu.VMEM((tq, ATTN_WIDTH), F32),
                        pltpu.VMEM((2 * BLOCK, lanes), F32), pltpu.VMEM((2 * BLOCK, lanes), BF16)],
        compiler_params=_params("arbitrary"),
    )(sinks, q, k, k, v, v, bias_t, gain)


def _attn_bwd(dmixed, raw, q, k, v, bias_t, sinks, gain, name, nblk):
    T = q.shape[0]
    tq = nblk * BLOCK
    nt = T // tq
    lanes = N_Q_HEADS * BLOCK

    def body(sink_ref, dm_ref, raw_ref, q_ref, kc_ref, kp_ref, vc_ref, vp_ref, bias_ref, g_ref,
             dq_ref, dk_ref, dv_ref, dbias_ref, dsink_ref, dgain_ref,
             do_sc, dq_sc, kf_sc, vf_sc, dkf_sc, dvf_sc, st_sc, dpt_sc, pt_sc, dst_sc, drow_sc,
             qs_sc, dos_sc, dsink_sc):
        i = pl.program_id(0)
        tile = nt - 1 - i

        @pl.when(i == 0)
        def _():
            dkf_sc[...] = jnp.zeros_like(dkf_sc)
            dvf_sc[...] = jnp.zeros_like(dvf_sc)
            dsink_sc[...] = jnp.zeros_like(dsink_sc)
            dbias_ref[...] = jnp.zeros_like(dbias_ref)
            dgain_ref[...] = jnp.zeros_like(dgain_ref)

        carry_k = dkf_sc[0:BLOCK, :]
        carry_v = dvf_sc[0:BLOCK, :]
        dkf_sc[0:tq, :] = jnp.zeros((tq, KV_WIDTH), F32)
        dvf_sc[0:tq, :] = jnp.zeros((tq, KV_WIDTH), F32)
        dkf_sc[tq:tq + BLOCK, :] = carry_k
        dvf_sc[tq:tq + BLOCK, :] = carry_v
        _load_band(kf_sc, vf_sc, kp_ref, kc_ref, vp_ref, vc_ref, tq)

        do, dgain = _rms_bwd(dm_ref[...].astype(F32), raw_ref[...].astype(F32), g_ref[...])
        dgain_ref[...] += dgain
        do_sc[...] = do
        ones = jnp.ones((SUBLANES, HEAD_DIM), BF16)

        def block(b, carry):
            r0 = pl.multiple_of(b * BLOCK, BLOCK)
            no_prev = (tile == 0) & (b == 0)
            for g in range(N_KV_HEADS):
                kb = kf_sc[pl.ds(r0, 2 * BLOCK), _head_cols(g)]
                vb = vf_sc[pl.ds(r0, 2 * BLOCK), _head_cols(g)]
                qg = _stack_heads(q_ref, r0, g, BF16)
                dog = _stack_heads(do_sc, r0, g, F32)
                prod = dog * _stack_heads(raw_ref, r0, g, F32)
                hi = prod.astype(BF16)
                lo = (prod - hi.astype(F32)).astype(BF16)
                drow_sc[:, _group_lanes(g)] = _dot_nt(ones, hi) + _dot_nt(ones, lo)
                dogb = dog.astype(BF16)
                qs_sc[g] = qg
                dos_sc[g] = dogb
                st_sc[:, _group_lanes(g)] = _dot_nt(kb, qg)
                dpt_sc[:, _group_lanes(g)] = _dot_nt(vb, dogb)
            for h in range(N_Q_HEADS):
                hl = _head_lanes(h)
                p, ps = _head_softmax(st_sc[:, hl], bias_ref[h], sink_ref[h], no_prev)
                rowdot = drow_sc[0:1, hl]
                ds = p * (dpt_sc[:, hl] - rowdot)
                dsink_sc[h:h + 1, :] += -(ps * rowdot)
                dbias_ref[h] += ds
                dst_sc[:, hl] = ds.astype(BF16)
                pt_sc[:, hl] = p.astype(BF16)
            for g in range(N_KV_HEADS):
                kb = kf_sc[pl.ds(r0, 2 * BLOCK), _head_cols(g)]
                dsg = dst_sc[:, _group_lanes(g)]
                _unstack_heads(dq_sc, r0, g, _dot_tn(dsg, kb) * SCALE)
                dkf_sc[pl.ds(r0, 2 * BLOCK), _head_cols(g)] += _dot(dsg, qs_sc[g]) * SCALE
                dvf_sc[pl.ds(r0, 2 * BLOCK), _head_cols(g)] += _dot(pt_sc[:, _group_lanes(g)], dos_sc[g])
            return carry

        lax.fori_loop(0, nblk, block, 0)
        dq_ref[...] = dq_sc[...].astype(BF16)
        dk_ref[...] = dkf_sc[BLOCK:BLOCK + tq, :].astype(BF16)
        dv_ref[...] = dvf_sc[BLOCK:BLOCK + tq, :].astype(BF16)

        @pl.when(i == nt - 1)
        def _():
            tot = jnp.sum(dsink_sc[...], axis=1, keepdims=True)
            dsink_ref[...] = jnp.broadcast_to(tot, dsink_ref.shape)

    cur = lambda i: (nt - 1 - i, 0)
    prev = lambda i: (jnp.maximum((nt - 1 - i) * nblk - 1, 0), 0)
    const2 = lambda i: (0, 0)
    const3 = lambda i: (0, 0, 0)
    return pl.pallas_call(
        body, name=name, grid=(nt,),
        in_specs=[pl.BlockSpec(memory_space=pltpu.SMEM),
                  pl.BlockSpec((tq, ATTN_WIDTH), cur),
                  pl.BlockSpec((tq, ATTN_WIDTH), cur),
                  pl.BlockSpec((tq, ATTN_WIDTH), cur),
                  pl.BlockSpec((tq, KV_WIDTH), cur), pl.BlockSpec((BLOCK, KV_WIDTH), prev),
                  pl.BlockSpec((tq, KV_WIDTH), cur), pl.BlockSpec((BLOCK, KV_WIDTH), prev),
                  pl.BlockSpec(bias_t.shape, const3),
                  pl.BlockSpec((1, ATTN_WIDTH), const2)],
        out_specs=(pl.BlockSpec((tq, ATTN_WIDTH), cur),
                   pl.BlockSpec((tq, KV_WIDTH), cur), pl.BlockSpec((tq, KV_WIDTH), cur),
                   pl.BlockSpec(bias_t.shape, const3),
                   pl.BlockSpec((N_Q_HEADS, 128), const2),
                   pl.BlockSpec((1, ATTN_WIDTH), const2)),
        out_shape=(jax.ShapeDtypeStruct((T, ATTN_WIDTH), BF16),
                   jax.ShapeDtypeStruct((T, KV_WIDTH), BF16), jax.ShapeDtypeStruct((T, KV_WIDTH), BF16),
                   jax.ShapeDtypeStruct(bias_t.shape, F32),
                   jax.ShapeDtypeStruct((N_Q_HEADS, 128), F32),
                   jax.ShapeDtypeStruct((1, ATTN_WIDTH), F32)),
        scratch_shapes=[pltpu.VMEM((tq, ATTN_WIDTH), F32), pltpu.VMEM((tq, ATTN_WIDTH), F32),
                        pltpu.VMEM((tq + BLOCK, KV_WIDTH), BF16), pltpu.VMEM((tq + BLOCK, KV_WIDTH), BF16),
                        pltpu.VMEM((tq + BLOCK, KV_WIDTH), F32), pltpu.VMEM((tq + BLOCK, KV_WIDTH), F32),
                        pltpu.VMEM((2 * BLOCK, lanes), F32), pltpu.VMEM((2 * BLOCK, lanes), F32),
                        pltpu.VMEM((2 * BLOCK, lanes), BF16), pltpu.VMEM((2 * BLOCK, lanes), BF16),
                        pltpu.VMEM((SUBLANES, lanes), F32),
                        pltpu.VMEM((N_KV_HEADS, GROUP_ROWS, HEAD_DIM), BF16),
                        pltpu.VMEM((N_KV_HEADS, GROUP_ROWS, HEAD_DIM), BF16),
                        pltpu.VMEM((N_Q_HEADS, 128), F32)],
        compiler_params=_params("arbitrary"),
    )(sinks, dmixed, raw, q, k, k, v, v, bias_t, gain)


def _shift_down(cu, tail):
    row = lax.broadcasted_iota(jnp.int32, cu.shape, 0)
    t6, t7 = tail[6:7, :], tail[7:8, :]
    s1 = jnp.where(row == 0, t7, pltpu.roll(cu, 1, 0))
    s2 = jnp.where(row == 0, t6, jnp.where(row == 1, t7, pltpu.roll(cu, 2, 0)))
    return s1, s2


def _shift_up(d, head):
    n = d.shape[0]
    row = lax.broadcasted_iota(jnp.int32, d.shape, 0)
    h0, h1 = head[0:1, :], head[1:2, :]
    s1 = jnp.where(row == n - 1, h0, pltpu.roll(d, n - 1, 0))
    s2 = jnp.where(row == n - 1, h1, jnp.where(row == n - 2, h0, pltpu.roll(d, n - 2, 0)))
    return s1, s2


def _mixout_fwd(x, attn_n, u, gb, gc, conv_w, gain, w_out, name, tm):
    T, D = x.shape

    def body(x_ref, an_ref, u_ref, b_ref, c_ref, cw_ref, g_ref, wo_ref, xo_ref, cn_ref, tail_sc):
        @pl.when(pl.program_id(0) == 0)
        def _():
            tail_sc[...] = jnp.zeros_like(tail_sc)

        cu = c_ref[...].astype(F32) * u_ref[...].astype(F32)
        s1, s2 = _shift_down(cu, tail_sc[...])
        tail_sc[...] = cu[tm - SUBLANES:tm, :]
        pre = cw_ref[0:1, :] * s2 + cw_ref[1:2, :] * s1 + cw_ref[2:3, :] * cu
        conv = b_ref[...].astype(F32) * pre
        _, chat = _rms_stats(conv)
        cn = (chat * g_ref[...]).astype(BF16)
        cn_ref[...] = cn
        xo_ref[...] = (x_ref[...] + _dot(an_ref[...], wo_ref[0:ATTN_WIDTH, :])
                       + _dot(cn, wo_ref[ATTN_WIDTH:ATTN_WIDTH + CONV_DIM, :]))

    row = lambda i: (i, 0)
    const = lambda i: (0, 0)
    return pl.pallas_call(
        body, name=name, grid=(T // tm,),
        in_specs=[pl.BlockSpec((tm, D), row), pl.BlockSpec((tm, ATTN_WIDTH), row),
                  pl.BlockSpec((tm, CONV_DIM), row), pl.BlockSpec((tm, CONV_DIM), row),
                  pl.BlockSpec((tm, CONV_DIM), row),
                  pl.BlockSpec(conv_w.shape, const), pl.BlockSpec((1, CONV_DIM), const),
                  pl.BlockSpec(w_out.shape, const)],
        out_specs=(pl.BlockSpec((tm, D), row), pl.BlockSpec((tm, CONV_DIM), row)),
        out_shape=(jax.ShapeDtypeStruct((T, D), F32), jax.ShapeDtypeStruct((T, CONV_DIM), BF16)),
        scratch_shapes=[pltpu.VMEM((SUBLANES, CONV_DIM), F32)],
        compiler_params=_params("arbitrary"),
    )(x, attn_n, u, gb, gc, conv_w, gain, w_out)


def _mixout_bwd(dy, attn_n, conv_n, w_out, after, name, tm):
    T, D = dy.shape
    W = ATTN_WIDTH + CONV_DIM
    nt = T // tm

    def body(dy_ref, an_ref, cn_ref, w_ref, after_ref, dm_ref, dw_ref, dw_sc):
        i = pl.program_id(0)

        @pl.when(i == 0)
        def _():
            dw_sc[...] = jnp.zeros_like(dw_sc)

        dyb = dy_ref[...].astype(BF16)
        dm_ref[...] = _dot_nt(dyb, w_ref[...]).astype(BF16)
        dw_sc[0:ATTN_WIDTH, :] += _dot_tn(an_ref[...], dyb)
        dw_sc[ATTN_WIDTH:W, :] += _dot_tn(cn_ref[...], dyb)

        @pl.when(i == nt - 1)
        def _():
            dw_ref[...] = dw_sc[...].astype(BF16)

    row = lambda i: (i, 0)
    const = lambda i: (0, 0)
    return pl.pallas_call(
        body, name=name, grid=(nt,),
        in_specs=[pl.BlockSpec((tm, D), row), pl.BlockSpec((tm, ATTN_WIDTH), row),
                  pl.BlockSpec((tm, CONV_DIM), row), pl.BlockSpec(w_out.shape, const),
                  pl.BlockSpec(memory_space=pl.ANY)],
        out_specs=(pl.BlockSpec((tm, W), row), pl.BlockSpec((W, D), const)),
        out_shape=(jax.ShapeDtypeStruct((T, W), BF16), jax.ShapeDtypeStruct((W, D), BF16)),
        scratch_shapes=[pltpu.VMEM((W, D), F32)],
        compiler_params=_params("arbitrary"),
    )(dy, attn_n, conv_n, w_out, after)


def _conv_bwd(dmixed, u, gb, gc, conv_w, gain, name, tc):
    T = u.shape[0]
    nt = T // tc
    per8 = tc // SUBLANES

    def body(dm_ref, u_ref, b_ref, c_ref, ut_ref, ct_ref, cw_ref, g_ref,
             du_ref, db_ref, dc_ref, dcw_ref, dgain_ref, head_sc):
        i = pl.program_id(0)

        @pl.when(i == 0)
        def _():
            head_sc[...] = jnp.zeros_like(head_sc)
            dcw_ref[...] = jnp.zeros_like(dcw_ref)
            dgain_ref[...] = jnp.zeros_like(dgain_ref)

        uu = u_ref[...].astype(F32)
        cc = c_ref[...].astype(F32)
        bb = b_ref[...].astype(F32)
        cu = cc * uu
        tail = jnp.where(i == nt - 1, 0.0, ct_ref[...].astype(F32) * ut_ref[...].astype(F32))
        s1, s2 = _shift_down(cu, tail)
        w0, w1, w2 = cw_ref[0:1, :], cw_ref[1:2, :], cw_ref[2:3, :]
        pre = w0 * s2 + w1 * s1 + w2 * cu
        dconv, dgain = _rms_bwd(dm_ref[...].astype(F32), bb * pre, g_ref[...])
        dgain_ref[...] += dgain
        db_ref[...] = (dconv * pre).astype(BF16)
        dpre = dconv * bb
        dcw_ref[0:1, :] += jnp.sum(dpre * s2, axis=0, keepdims=True)
        dcw_ref[1:2, :] += jnp.sum(dpre * s1, axis=0, keepdims=True)
        dcw_ref[2:3, :] += jnp.sum(dpre * cu, axis=0, keepdims=True)
        n1, n2 = _shift_up(dpre, head_sc[...])
        head_sc[...] = dpre[0:SUBLANES, :]
        dcu = w2 * dpre + w1 * n1 + w0 * n2
        du_ref[...] = (dcu * cc).astype(BF16)
        dc_ref[...] = (dcu * uu).astype(BF16)

    rev = lambda i: (nt - 1 - i, 0)
    rev_right = lambda i: (nt - 1 - i, 1)
    tail_map = lambda i: (jnp.maximum((nt - 1 - i) * per8 - 1, 0), 0)
    const = lambda i: (0, 0)
    return pl.pallas_call(
        body, name=name, grid=(nt,),
        in_specs=[pl.BlockSpec((tc, CONV_DIM), rev_right),
                  pl.BlockSpec((tc, CONV_DIM), rev), pl.BlockSpec((tc, CONV_DIM), rev),
                  pl.BlockSpec((tc, CONV_DIM), rev),
                  pl.BlockSpec((SUBLANES, CONV_DIM), tail_map), pl.BlockSpec((SUBLANES, CONV_DIM), tail_map),
                  pl.BlockSpec(conv_w.shape, const), pl.BlockSpec((1, CONV_DIM), const)],
        out_specs=(pl.BlockSpec((tc, CONV_DIM), rev), pl.BlockSpec((tc, CONV_DIM), rev),
                   pl.BlockSpec((tc, CONV_DIM), rev),
                   pl.BlockSpec((SUBLANES, CONV_DIM), const), pl.BlockSpec((1, CONV_DIM), const)),
        out_shape=(jax.ShapeDtypeStruct((T, CONV_DIM), BF16), jax.ShapeDtypeStruct((T, CONV_DIM), BF16),
                   jax.ShapeDtypeStruct((T, CONV_DIM), BF16),
                   jax.ShapeDtypeStruct((SUBLANES, CONV_DIM), F32), jax.ShapeDtypeStruct((1, CONV_DIM), F32)),
        scratch_shapes=[pltpu.VMEM((SUBLANES, CONV_DIM), F32)],
        compiler_params=_params("arbitrary"),
    )(dmixed, u, gb, gc, u, gc, conv_w, gain)


def _mixin_bwd(dy, x, gain, dz, w_in_t, name, tm):
    T, D = x.shape
    nz = len(dz)

    def body(dy_ref, x_ref, g_ref, *rest):
        dz_refs, wt_ref, dx_ref, dgain_ref = rest[:nz], rest[nz], rest[nz + 1], rest[nz + 2]

        @pl.when(pl.program_id(0) == 0)
        def _():
            dgain_ref[...] = jnp.zeros_like(dgain_ref)

        dh = jnp.zeros((tm, D), F32)
        for r, lo, hi in zip(dz_refs, _Z_SPLITS[:-1], _Z_SPLITS[1:]):
            dh += _dot(r[...], wt_ref[lo:hi, :])
        dx, dgain = _rms_bwd(dh, x_ref[...], g_ref[...])
        dgain_ref[...] += dgain
        dx_ref[...] = dy_ref[...] + dx

    row = lambda i: (i, 0)
    const = lambda i: (0, 0)
    return pl.pallas_call(
        body, name=name, grid=(T // tm,),
        in_specs=[pl.BlockSpec((tm, D), row), pl.BlockSpec((tm, D), row), pl.BlockSpec((1, D), const)]
                 + [pl.BlockSpec((tm, a.shape[1]), row) for a in dz]
                 + [pl.BlockSpec(w_in_t.shape, const)],
        out_specs=(pl.BlockSpec((tm, D), row), pl.BlockSpec((1, D), const)),
        out_shape=(jax.ShapeDtypeStruct((T, D), F32), jax.ShapeDtypeStruct((1, D), F32)),
        compiler_params=_params("arbitrary"),
    )(dy, x, gain, *dz, w_in_t)


def _win_grad(dz, hn, name, tk):
    T, D = hn.shape
    nz = len(dz)
    nt = T // tk
    W = _Z_SPLITS[-1]

    def body(hn_ref, *rest):
        dz_refs, dw_ref, dw_sc = rest[:nz], rest[nz], rest[nz + 1]
        i = pl.program_id(0)

        @pl.when(i == 0)
        def _():
            dw_sc[...] = jnp.zeros_like(dw_sc)

        hn = hn_ref[...]
        for r, lo, hi in zip(dz_refs, _Z_SPLITS[:-1], _Z_SPLITS[1:]):
            dw_sc[lo:hi, :] += _dot_tn(r[...], hn)

        @pl.when(i == nt - 1)
        def _():
            dw_ref[...] = dw_sc[...].astype(BF16)

    row = lambda i: (i, 0)
    return pl.pallas_call(
        body, name=name, grid=(nt,),
        in_specs=[pl.BlockSpec((tk, D), row)] + [pl.BlockSpec((tk, a.shape[1]), row) for a in dz],
        out_specs=pl.BlockSpec((W, D), lambda i: (0, 0)),
        out_shape=jax.ShapeDtypeStruct((W, D), BF16),
        scratch_shapes=[pltpu.VMEM((W, D), F32)],
        compiler_params=_params("arbitrary"),
    )(hn, *dz)


def _loss_head(x, target, gain, name, tm):
    T, D = x.shape

    def body(x_ref, t_ref, g_ref, dx_ref, dgain_ref, loss_ref):
        @pl.when(pl.program_id(0) == 0)
        def _():
            dgain_ref[...] = jnp.zeros_like(dgain_ref)
            loss_ref[...] = jnp.zeros_like(loss_ref)

        xv = x_ref[...]
        gain_v = g_ref[...]
        _, xhat = _rms_stats(xv)
        err = xhat * gain_v - t_ref[...]
        part = 0.5 * jnp.sum(jnp.mean(err * err, axis=-1, keepdims=True), axis=0, keepdims=True)
        loss_ref[...] += part
        dx, dgain = _rms_bwd(err * (1.0 / D), xv, gain_v)
        dgain_ref[...] += dgain
        dx_ref[...] = dx

    row = lambda i: (i, 0)
    const = lambda i: (0, 0)
    return pl.pallas_call(
        body, name=name, grid=(T // tm,),
        in_specs=[pl.BlockSpec((tm, D), row), pl.BlockSpec((tm, D), row), pl.BlockSpec((1, D), const)],
        out_specs=(pl.BlockSpec((tm, D), row), pl.BlockSpec((1, D), const),
                   pl.BlockSpec((SUBLANES, 128), const)),
        out_shape=(jax.ShapeDtypeStruct((T, D), F32), jax.ShapeDtypeStruct((1, D), F32),
                   jax.ShapeDtypeStruct((SUBLANES, 128), F32)),
        compiler_params=_params("arbitrary"),
    )(x, target, gain)


def _sum_parts(parts, name, tr):
    P, R, C = parts.shape

    def body(p_ref, o_ref):
        g = p_ref[0].astype(F32)
        for d in range(1, P):
            g = g + p_ref[d].astype(F32)
        o_ref[...] = g

    return pl.pallas_call(
        body, name=name, grid=(R // tr,),
        in_specs=[pl.BlockSpec((P, tr, C), lambda i: (0, i, 0))],
        out_specs=pl.BlockSpec((tr, C), lambda i: (i, 0)),
        out_shape=jax.ShapeDtypeStruct((R, C), F32),
        compiler_params=_params("arbitrary"),
    )(parts)


def _adamw(parts, w, m, v, name, tr):
    P = parts.shape[0]
    R, C = w.shape

    def body(p_ref, w_ref, m_ref, v_ref, g_ref, d_ref, nm_ref, nv_ref):
        g = p_ref[0].astype(F32)
        for d in range(1, P):
            g = g + p_ref[d].astype(F32)
        nm = ADAM_B1 * m_ref[...] + (1.0 - ADAM_B1) * g
        nv = ADAM_B2 * v_ref[...] + (1.0 - ADAM_B2) * (g * g)
        m_hat = nm / (1.0 - ADAM_B1 ** ADAM_STEP)
        v_hat = nv / (1.0 - ADAM_B2 ** ADAM_STEP)
        g_ref[...] = g
        nm_ref[...] = nm
        nv_ref[...] = nv
        d_ref[...] = -ADAM_LR * (m_hat / (jnp.sqrt(v_hat) + ADAM_EPS) + ADAM_WD * w_ref[...])

    row = lambda i: (i, 0)
    spec = pl.BlockSpec((tr, C), row)
    shp = jax.ShapeDtypeStruct((R, C), F32)
    return pl.pallas_call(
        body, name=name, grid=(R // tr,),
        in_specs=[pl.BlockSpec((P, tr, C), lambda i: (0, i, 0)), spec, spec, spec],
        out_specs=(spec, spec, spec, spec),
        out_shape=(shp, shp, shp, shp),
        compiler_params=_params("arbitrary"),
    )(parts, w, m, v)


def _columns_of_blocks(g):
    n, R, w = g.shape
    return g.transpose(1, 0, 2).reshape(R, n * w)


def _pad_row(vec):
    vec = vec.reshape(1, -1)
    return jnp.pad(vec, ((0, 0), (0, PACK_COLS - vec.shape[1])))


def kernel(x, rel_bias_table, ffn1_norm, ffn1_w_gate, ffn1_w_up, ffn1_w_down, mix_norm, w_in, conv_w, attn_sinks, attn_out_norm, conv_out_norm, w_out, ffn2_norm, ffn2_w_gate, ffn2_w_up, ffn2_w_down, final_norm, loss_target, m_rel_bias_table, m_ffn1_norm, m_ffn1_w_gate, m_ffn1_w_up, m_ffn1_w_down, m_mix_norm, m_w_in, m_conv_w, m_attn_sinks, m_attn_out_norm, m_conv_out_norm, m_w_out, m_ffn2_norm, m_ffn2_w_gate, m_ffn2_w_up, m_ffn2_w_down, m_final_norm, v_rel_bias_table, v_ffn1_norm, v_ffn1_w_gate, v_ffn1_w_up, v_ffn1_w_down, v_mix_norm, v_w_in, v_conv_w, v_attn_sinks, v_attn_out_norm, v_conv_out_norm, v_w_out, v_ffn2_norm, v_ffn2_w_gate, v_ffn2_w_up, v_ffn2_w_down, v_final_norm):
    T, D = x.shape[1], x.shape[2]
    x0 = x[0]
    target = loss_target[0]
    tm = min(TM_FFN, T)
    tm_bwd = min(TM_FFN_BWD, T)
    tm_mix = min(TM_MIX, T)
    tk = min(TK_WGRAD, T)
    tf = TF_FFN
    nblk = min(ATTN_BLOCKS, T // BLOCK)
    me = 4 * lax.axis_index("x") + 2 * lax.axis_index("y") + lax.axis_index("c")

    big = {
        "ffn1_w_gate": (ffn1_w_gate[0], m_ffn1_w_gate[0], v_ffn1_w_gate[0], True),
        "ffn1_w_up": (ffn1_w_up[0], m_ffn1_w_up[0], v_ffn1_w_up[0], True),
        "ffn1_w_down": (ffn1_w_down[0], m_ffn1_w_down[0], v_ffn1_w_down[0], False),
        "w_in": (w_in[0], m_w_in[0], v_w_in[0], True),
        "w_out": (w_out[0], m_w_out[0], v_w_out[0], False),
        "ffn2_w_gate": (ffn2_w_gate[0], m_ffn2_w_gate[0], v_ffn2_w_gate[0], True),
        "ffn2_w_up": (ffn2_w_up[0], m_ffn2_w_up[0], v_ffn2_w_up[0], True),
        "ffn2_w_down": (ffn2_w_down[0], m_ffn2_w_down[0], v_ffn2_w_down[0], False),
    }

    def block_to_send(name):
        w, _, _, transposed = big[name]
        return (w.T if transposed else w).astype(BF16)

    names1 = ["ffn1_w_gate", "ffn1_w_up", "ffn1_w_down"]
    names_rest = ["w_in", "w_out", "ffn2_w_gate", "ffn2_w_up", "ffn2_w_down"]
    h1, token = _exchange_start([block_to_send(n) for n in names1], False, "gather_start_ffn1", ffn1_norm)
    h_rest, token = _exchange_start([block_to_send(n) for n in names_rest[:2]] + [conv_w[0]]
                                    + [block_to_send(n) for n in names_rest[2:]], False,
                                    "gather_start_rest", token)
    wgt1, wut1, wd1 = [g.reshape(-1, D) for g in _exchange_wait(h1, False, "gather_wait_ffn1", token)]

    wgu1 = _stack_gate_up(wgt1, wut1, tf)
    x1, xn1, gu1 = _ffn_fwd(x0, ffn1_norm, wgu1, wd1, "ffn1_fwd", tm, tf)
    mixw = _exchange_wait(h_rest[:3], False, "gather_wait_mix", x1)
    win_t = mixw[0].reshape(-1, D)
    wout = mixw[1].reshape(-1, D)
    cw = _columns_of_blocks(mixw[2])
    hn, q, k, v, u, gb, gc = _mixin_fwd(x1, mix_norm, win_t, "mixin_fwd", tm_mix)
    bucket = jnp.asarray(_bucket_table().T.copy())
    sinks = attn_sinks.reshape(-1)
    bias_t = _bias_build(rel_bias_table, bucket, "bias_build")
    attn_raw, attn_n = _attn_fwd(q, k, v, bias_t, sinks, attn_out_norm, "attn_fwd", nblk)
    x2, conv_n = _mixout_fwd(x1, attn_n, u, gb, gc, cw, conv_out_norm, wout, "mixout_fwd", tm_mix)
    wgt2, wut2, wd2 = [g.reshape(-1, D) for g in _exchange_wait(h_rest[3:], False, "gather_wait_ffn2", x2)]
    wgu2 = _stack_gate_up(wgt2, wut2, tf)
    x3, xn2, gu2 = _ffn_fwd(x2, ffn2_norm, wgu2, wd2, "ffn2_fwd", tm, tf)

    def blocks(g):
        return g.reshape(N_DEV, -1, D)

    dx3, d_final, loss_part = _loss_head(x3, target, final_norm.reshape(1, D), "loss_head", tm_mix)
    dx2, d_ffn2_norm, dgu2, hh2, do2 = _ffn_bwd(
        dx3, x2, ffn2_norm, gu2, wgu2, wd2, dx3, "ffn2_bwd", tm_bwd, tf)
    d_wg2, d_wu2 = _unstack_gate_up(_tn_grad(dgu2, xn2, "ffn2_wgrad_gu", BM_WGRAD, tk), tf)
    d_wd2 = _tn_grad(hh2, do2, "ffn2_wgrad_down", BM_WGRAD, tk)
    handles2, token2 = _exchange_start([blocks(d_wg2), blocks(d_wu2), blocks(d_wd2)], True,
                                       "grads_start_ffn2", dx2)

    dmixed, d_wout = _mixout_bwd(dx2, attn_n, conv_n, wout, token2, "mixout_bwd", tm_mix)
    dq, dk, dv, dbias, dsink, d_attn_norm = _attn_bwd(
        dmixed, attn_raw, q, k, v, bias_t, sinks, attn_out_norm, "attn_bwd", nblk)
    du, dgb, dgc, d_cw, d_conv_norm = _conv_bwd(dmixed, u, gb, gc, cw, conv_out_norm, "conv_bwd", tm_mix)
    d_table = _bias_grad(dbias, bucket, "bias_grad")
    dz = [dq, dk, dv, du, dgb, dgc]
    dx1, d_mix_norm = _mixin_bwd(dx2, x1, mix_norm, dz, win_t, "mixin_bwd", tm_mix)
    d_win_t = _win_grad(dz, hn, "win_grad", min(TM_MIX, T))
    handles_mix, token_mix = _exchange_start([blocks(d_win_t), blocks(d_wout)], True, "grads_start_mix", dx1)

    dx0, d_ffn1_norm, dgu1, hh1, do1 = _ffn_bwd(
        dx1, x0, ffn1_norm, gu1, wgu1, wd1, token_mix, "ffn1_bwd", tm_bwd, tf)
    d_wg1, d_wu1 = _unstack_gate_up(_tn_grad(dgu1, xn1, "ffn1_wgrad_gu", BM_WGRAD, tk), tf)
    handles1_gu, token1 = _exchange_start([blocks(d_wg1), blocks(d_wu1)], True, "grads_start_ffn1_gu", dx0)
    d_wd1 = _tn_grad(hh1, do1, "ffn1_wgrad_down", BM_WGRAD, tk)
    handles1_d, token1 = _exchange_start([blocks(d_wd1)], True, "grads_start_ffn1_down", token1)

    res = {}

    def update(names, parts):
        last = None
        for name, p in zip(names, parts):
            w, m_, v_, transposed = big[name]
            if transposed:
                summed = _sum_parts(p, "sum_" + name, _row_tile(p.shape[1], ADAM_ROWS))
                p = summed.T[None]
            g, dl, nm, nv = _adamw(p, w, m_, v_, "adamw_" + name, _row_tile(w.shape[0], ADAM_ROWS))
            res[name] = tuple(a[None] for a in (g, dl, nm, nv))
            last = g
        return last

    parts2 = _exchange_wait(handles2, True, "grads_wait_ffn2", token1)
    done2 = update(["ffn2_w_gate", "ffn2_w_up", "ffn2_w_down"], parts2)
    parts_mix = _exchange_wait(handles_mix, True, "grads_wait_mix", done2)
    done_mix = update(["w_in", "w_out"], parts_mix)

    def pack(ffn1, mixn, ffn2, fin, attn_n_, conv_n_, sink_, extra, convw, table):
        rows = [_pad_row(ffn1), _pad_row(mixn), _pad_row(ffn2), _pad_row(fin),
                _pad_row(jnp.concatenate([attn_n_.reshape(-1), conv_n_.reshape(-1)])),
                _pad_row(sink_), _pad_row(extra),
                jnp.zeros((1, PACK_COLS), F32),
                jnp.pad(convw, ((0, 0), (0, PACK_COLS - convw.shape[1]))),
                _pad_row(table),
                jnp.zeros((PACK_ROWS - 12, PACK_COLS), F32)]
        return jnp.concatenate(rows, axis=0)

    def own_channels(a):
        full = jnp.zeros((a.shape[1], CONV_DIM), F32)
        return lax.dynamic_update_slice(full, a[0], (0, me * a.shape[2]))

    g_pack = pack(d_ffn1_norm, d_mix_norm, d_ffn2_norm, d_final, d_attn_norm, d_conv_norm,
                  dsink[:, 0], loss_part[0, :1], d_cw[:3], d_table[:, :N_Q_HEADS])
    zero1 = jnp.zeros((1,), F32)
    w_pack = pack(ffn1_norm, mix_norm, ffn2_norm, final_norm, attn_out_norm, conv_out_norm,
                  attn_sinks, zero1, own_channels(conv_w), rel_bias_table)
    m_pack = pack(m_ffn1_norm, m_mix_norm, m_ffn2_norm, m_final_norm, m_attn_out_norm, m_conv_out_norm,
                  m_attn_sinks, zero1, own_channels(m_conv_w), m_rel_bias_table)
    v_pack = pack(v_ffn1_norm, v_mix_norm, v_ffn2_norm, v_final_norm, v_attn_out_norm, v_conv_out_norm,
                  v_attn_sinks, zero1, own_channels(v_conv_w), v_rel_bias_table)
    (g_all,) = _exchange([g_pack], False, "gather_small", done_mix)
    packs = _adamw(g_all, w_pack, m_pack, v_pack, "adamw_small", PACK_ROWS)

    parts1_gu = _exchange_wait(handles1_gu, True, "grads_wait_ffn1_gu", packs[0])
    done1_gu = update(["ffn1_w_gate", "ffn1_w_up"], parts1_gu)
    parts1_d = _exchange_wait(handles1_d, True, "grads_wait_ffn1_down", done1_gu)
    update(["ffn1_w_down"], parts1_d)

    def unpack(pk):
        cwb = lax.dynamic_slice(pk[8:11, :CONV_DIM], (0, me * conv_w.shape[2]), (3, conv_w.shape[2]))
        return {
            "ffn1_norm": pk[0:1, :D], "mix_norm": pk[1:2, :D], "ffn2_norm": pk[2:3, :D],
            "final_norm": pk[3, :D],
            "attn_out_norm": pk[4:5, :ATTN_WIDTH], "conv_out_norm": pk[4:5, ATTN_WIDTH:ATTN_WIDTH + CONV_DIM],
            "attn_sinks": pk[5:6, :N_Q_HEADS],
            "conv_w": cwb[None],
            "rel_bias_table": pk[11, :NUM_BUCKETS * N_Q_HEADS].reshape(NUM_BUCKETS, N_Q_HEADS),
        }

    small = [unpack(pk) for pk in packs]
    loss = packs[0][6, 0]

    order = ["rel_bias_table", "ffn1_norm", "ffn1_w_gate", "ffn1_w_up", "ffn1_w_down", "mix_norm", "w_in",
             "conv_w", "attn_sinks", "attn_out_norm", "conv_out_norm", "w_out", "ffn2_norm",
             "ffn2_w_gate", "ffn2_w_up", "ffn2_w_down", "final_norm"]
    outs = [loss, dx0[None]]
    for kind in range(4):
        for name in order:
            outs.append(res[name][kind] if name in res else small[kind][name])
    return tuple(outs)
```

```python
import math

import numpy as np
import jax
import jax.numpy as jnp
from jax import lax
from jax.experimental import pallas as pl
from jax.experimental.pallas import tpu as pltpu

F32 = jnp.float32
BF16 = jnp.bfloat16

N_DEV = 8
EPS = 1e-6
HEAD_DIM = 64
N_Q_HEADS = 8
N_KV_HEADS = 2
GQA_GROUP = 4
ATTN_WIDTH = 512
KV_WIDTH = 128
CONV_DIM = 512
BLOCK = 128
WINDOW = 128
NUM_BUCKETS = 32
MAX_DISTANCE = 128
SCALE = HEAD_DIM ** -0.5
MASKED = -1e30
GROUP_ROWS = GQA_GROUP * BLOCK

ADAM_LR = 0.001
ADAM_B1 = 0.9
ADAM_B2 = 0.999
ADAM_EPS = 1e-08
ADAM_WD = 0.01
ADAM_STEP = 10

VMEM_LIMIT_BYTES = 40 * 1024 * 1024
SUBLANES = 8
PACK_ROWS = 16
PACK_COLS = 1024

TM_FFN = 1024
TM_FFN_BWD = 1024
TM_MIX = 512
TK_WGRAD = 1024
TF_FFN = 256
BM_WGRAD = 1408
ROW_GROUPS = 4
ATTN_BLOCKS = 4
ADAM_ROWS = 256


def _row_tile(rows, limit):
    best = rows
    for t in range(16, min(rows, limit) + 1, 16):
        if rows % t == 0:
            best = t
    return best


def _params(*sem):
    return pltpu.CompilerParams(dimension_semantics=sem, vmem_limit_bytes=VMEM_LIMIT_BYTES)


def _hbm(*arrays):
    return [pltpu.with_memory_space_constraint(a, pltpu.HBM) for a in arrays]


def _dot(a, b):
    return jnp.dot(a, b, preferred_element_type=F32)


def _dot_nt(a, b):
    return lax.dot_general(a, b, (((1,), (1,)), ((), ())), preferred_element_type=F32)


def _dot_tn(a, b):
    return lax.dot_general(a, b, (((0,), (0,)), ((), ())), preferred_element_type=F32)


def _sigmoid(g):
    return 0.5 * jnp.tanh(0.5 * g) + 0.5


def _rms_stats(x):
    inv = lax.rsqrt(jnp.mean(x * x, axis=-1, keepdims=True) + EPS)
    return inv, x * inv


def _rms_bwd(dy, x, gain):
    inv, xhat = _rms_stats(x)
    dgain = jnp.sum(dy * xhat, axis=0, keepdims=True)
    dxh = dy * gain
    dx = inv * (dxh - xhat * jnp.mean(dxh * xhat, axis=-1, keepdims=True))
    return dx, dgain


def _peer_list():
    x, y, c = lax.axis_index("x"), lax.axis_index("y"), lax.axis_index("c")
    peers = []
    for k in range(1, N_DEV):
        px = 1 - x if (k >> 2) & 1 else x
        py = 1 - y if (k >> 1) & 1 else y
        pc = 1 - c if k & 1 else c
        peers.append((px, py, pc))
    return 4 * x + 2 * y + c, peers


def _exchange(arrs, scatter, name, after):
    n = len(arrs)
    out_shape = []
    for a in arrs:
        shp = a.shape if scatter else (N_DEV,) + a.shape
        out_shape.append(jax.ShapeDtypeStruct(shp, a.dtype))

    def body(*refs):
        ins, outs = refs[:n], refs[n + 1:2 * n + 1]
        send_sems, recv_sems, local_sems = refs[2 * n + 1:]
        me, peers = _peer_list()
        started = []
        for a in range(n):
            own = ins[a].at[me] if scatter else ins[a]
            loc = pltpu.make_async_copy(own, outs[a].at[me], local_sems.at[a])
            loc.start()
            started.append(loc)
        sends = []
        for a in range(n):
            for k, (px, py, pc) in enumerate(peers):
                src = ins[a].at[4 * px + 2 * py + pc] if scatter else ins[a]
                cp = pltpu.make_async_remote_copy(
                    src_ref=src, dst_ref=outs[a].at[me],
                    send_sem=send_sems.at[a, k], recv_sem=recv_sems.at[a, k],
                    device_id=(px, py, pc), device_id_type=pl.DeviceIdType.MESH)
                cp.start()
                sends.append(cp)
        for a in range(n):
            for k, (px, py, pc) in enumerate(peers):
                landed = outs[a].at[4 * px + 2 * py + pc]
                pltpu.make_async_remote_copy(
                    src_ref=landed, dst_ref=landed,
                    send_sem=send_sems.at[a, k], recv_sem=recv_sems.at[a, k],
                    device_id=(px, py, pc), device_id_type=pl.DeviceIdType.MESH).wait_recv()
        for cp in sends:
            cp.wait_send()
        for loc in started:
            loc.wait()

    hbm = pl.BlockSpec(memory_space=pl.ANY)
    return pl.pallas_call(
        body, name=name, out_shape=tuple(out_shape),
        in_specs=[hbm] * (n + 1), out_specs=tuple([hbm] * n),
        scratch_shapes=[pltpu.SemaphoreType.DMA((n, N_DEV - 1)),
                        pltpu.SemaphoreType.DMA((n, N_DEV - 1)),
                        pltpu.SemaphoreType.DMA((n,))],
    )(*arrs, after)


_HBM = pl.BlockSpec(memory_space=pltpu.HBM)
_SEM = pl.BlockSpec(memory_space=pltpu.SEMAPHORE)
_EFFECT = pltpu.SideEffectType.DATAFLOW_SIDE_EFFECTING


def _split_copies(srcs, lands, send_sems, recv_sems, scatter):
    me, peers = _peer_list()
    copies = []
    for a in range(len(srcs)):
        for k, (px, py, pc) in enumerate(peers):
            src = srcs[a].at[4 * px + 2 * py + pc] if scatter else srcs[a]
            copies.append(pltpu.make_async_remote_copy(
                src_ref=src, dst_ref=lands[a].at[me],
                send_sem=send_sems[a].at[k], recv_sem=recv_sems[a].at[k],
                device_id=(px, py, pc), device_id_type=pl.DeviceIdType.MESH))
    return copies


def _exchange_start(arrs, scatter, name, after):
    n = len(arrs)
    me = 4 * lax.axis_index("x") + 2 * lax.axis_index("y") + lax.axis_index("c")
    lands = []
    for a in arrs:
        own = lax.dynamic_index_in_dim(a, me, 0, keepdims=True) if scatter else a[None]
        shp = a.shape if scatter else (N_DEV,) + a.shape
        lands.append(lax.dynamic_update_slice(lax.empty(shp, a.dtype), own, (me,) + (0,) * (len(shp) - 1)))

    def body(*refs):
        srcs, lnds = refs[:n], refs[n:2 * n]
        outs = refs[2 * n + 1:]
        send_sems, recv_sems = outs[:n], outs[n:2 * n]
        token = outs[4 * n]
        for cp in _split_copies(srcs, lnds, send_sems, recv_sems, scatter):
            cp.start()
        token[...] = jnp.zeros_like(token)

    sem = pltpu.SemaphoreType.DMA((N_DEV - 1,))
    out_shape = ([sem] * (2 * n) + [pltpu.HBM(a.shape, a.dtype) for a in arrs]
                 + [pltpu.HBM(l.shape, l.dtype) for l in lands] + [jax.ShapeDtypeStruct((SUBLANES, 128), F32)])
    res = pl.pallas_call(
        body, name=name, out_shape=tuple(out_shape),
        in_specs=[_HBM] * (2 * n) + [pl.BlockSpec(memory_space=pl.ANY)],
        out_specs=tuple([_SEM] * (2 * n) + [_HBM] * (2 * n) + [pl.BlockSpec(memory_space=pltpu.VMEM)]),
        input_output_aliases={i: 2 * n + i for i in range(2 * n)},
        compiler_params=pltpu.CompilerParams(has_side_effects=_EFFECT),
    )(*[pltpu.with_memory_space_constraint(a, pltpu.HBM) for a in arrs],
      *[pltpu.with_memory_space_constraint(l, pltpu.HBM) for l in lands], after)
    handles = [(res[2 * n + a], res[3 * n + a], res[a], res[n + a]) for a in range(n)]
    return handles, res[4 * n]


def _exchange_wait(handles, scatter, name, after):
    n = len(handles)

    def body(*refs):
        srcs, lnds = refs[:n], refs[n:2 * n]
        send_sems, recv_sems = refs[2 * n:3 * n], refs[3 * n:4 * n]
        for cp in _split_copies(srcs, lnds, send_sems, recv_sems, scatter):
            cp.wait_send()
            cp.wait_recv()

    srcs = [h[0] for h in handles]
    lands = [h[1] for h in handles]
    res = pl.pallas_call(
        body, name=name,
        out_shape=tuple([pltpu.HBM(a.shape, a.dtype) for a in srcs] + [pltpu.HBM(l.shape, l.dtype) for l in lands]),
        in_specs=[_HBM] * (2 * n) + [_SEM] * (2 * n) + [pl.BlockSpec(memory_space=pl.ANY)],
        out_specs=tuple([_HBM] * (2 * n)),
        input_output_aliases={i: i for i in range(2 * n)},
        compiler_params=pltpu.CompilerParams(has_side_effects=_EFFECT),
    )(*srcs, *lands, *[h[2] for h in handles], *[h[3] for h in handles], after)
    return list(res[n:])


def _stack_gate_up(wgt, wut, tf):
    F, D = wgt.shape
    return jnp.stack([wgt.reshape(F // tf, tf, D), wut.reshape(F // tf, tf, D)], axis=1).reshape(2 * F, D)


def _unstack_gate_up(w, tf):
    F2, D = w.shape
    w4 = w.reshape(F2 // (2 * tf), 2, tf, D)
    return w4[:, 0].reshape(F2 // 2, D), w4[:, 1].reshape(F2 // 2, D)


def _row_groups(tm):
    return [slice(r * (tm // ROW_GROUPS), (r + 1) * (tm // ROW_GROUPS)) for r in range(ROW_GROUPS)]


def _ffn_fwd(x, gain, wgu, wd, name, tm, tf):
    T, D = x.shape
    F = wd.shape[0]
    nj = F // tf

    def body(x_ref, g_ref, wgu_ref, wd_ref, xo_ref, xn_ref, gu_ref, xn_sc, acc_sc):
        j = pl.program_id(1)

        @pl.when(j == 0)
        def _():
            _, xhat = _rms_stats(x_ref[...])
            xn = (xhat * g_ref[...]).astype(BF16)
            xn_sc[...] = xn
            xn_ref[...] = xn
            acc_sc[...] = jnp.zeros_like(acc_sc)

        groups = _row_groups(tm)
        gus = [_dot_nt(xn_sc[rows, :], wgu_ref[...]) for rows in groups]
        hs = []
        for rows, gu in zip(groups, gus):
            gu_ref[rows, :] = gu.astype(BF16)
            g, u = gu[:, :tf], gu[:, tf:]
            hs.append((g * _sigmoid(g) * u).astype(BF16))
        for rows, h in zip(groups, hs):
            acc_sc[rows, :] += _dot(h, wd_ref[...])

        @pl.when(j == nj - 1)
        def _():
            xo_ref[...] = x_ref[...] + 0.5 * acc_sc[...]

    return pl.pallas_call(
        body, name=name, grid=(T // tm, nj),
        in_specs=[pl.BlockSpec((tm, D), lambda i, j: (i, 0)),
                  pl.BlockSpec((1, D), lambda i, j: (0, 0)),
                  pl.BlockSpec((2 * tf, D), lambda i, j: (j, 0)),
                  pl.BlockSpec((tf, D), lambda i, j: (j, 0))],
        out_specs=(pl.BlockSpec((tm, D), lambda i, j: (i, 0)),
                   pl.BlockSpec((tm, D), lambda i, j: (i, 0)),
                   pl.BlockSpec((tm, 2 * tf), lambda i, j: (i, j))),
        out_shape=(jax.ShapeDtypeStruct((T, D), F32), jax.ShapeDtypeStruct((T, D), BF16),
                   jax.ShapeDtypeStruct((T, 2 * F), BF16)),
        scratch_shapes=[pltpu.VMEM((tm, D), BF16), pltpu.VMEM((tm, D), F32)],
        compiler_params=_params("arbitrary", "arbitrary"),
    )(*_hbm(x, gain, wgu, wd))


def _ffn_bwd(dy, x, gain, gu, wgu, wd, after, name, tm, tf):
    T, D = x.shape
    F = wd.shape[0]
    nj = F // tf

    def body(dy_ref, x_ref, g_ref, gu_ref, wgu_ref, wd_ref, after_ref,
             dx_ref, dgain_ref, dgu_ref, hh_ref, do_ref, do_sc, acc_sc):
        i, j = pl.program_id(0), pl.program_id(1)

        @pl.when((i == 0) & (j == 0))
        def _():
            dgain_ref[...] = jnp.zeros_like(dgain_ref)

        @pl.when(j == 0)
        def _():
            do = (0.5 * dy_ref[...]).astype(BF16)
            do_sc[...] = do
            do_ref[...] = do
            acc_sc[...] = jnp.zeros_like(acc_sc)

        groups = _row_groups(tm)
        dhs = [_dot_nt(do_sc[rows, :], wd_ref[...]) for rows in groups]
        for rows, dh in zip(groups, dhs):
            g = gu_ref[rows, :tf].astype(F32)
            u = gu_ref[rows, tf:].astype(F32)
            sig = _sigmoid(g)
            s = g * sig
            dgu_ref[rows, :tf] = (dh * u * (sig + s * (1.0 - sig))).astype(BF16)
            dgu_ref[rows, tf:] = (dh * s).astype(BF16)
            hh_ref[rows, :] = (s * u).astype(BF16)
        for rows in groups:
            acc_sc[rows, :] += _dot(dgu_ref[rows, :], wgu_ref[...])

        @pl.when(j == nj - 1)
        def _():
            dx, dgain = _rms_bwd(acc_sc[...], x_ref[...], g_ref[...])
            dgain_ref[...] += dgain
            dx_ref[...] = dy_ref[...] + dx

    tile = pl.BlockSpec((tm, D), lambda i, j: (i, 0), pipeline_mode=pl.Buffered(1))
    return pl.pallas_call(
        body, name=name, grid=(T // tm, nj),
        in_specs=[tile, tile,
                  pl.BlockSpec((1, D), lambda i, j: (0, 0)),
                  pl.BlockSpec((tm, 2 * tf), lambda i, j: (i, j)),
                  pl.BlockSpec((2 * tf, D), lambda i, j: (j, 0)),
                  pl.BlockSpec((tf, D), lambda i, j: (j, 0)),
                  pl.BlockSpec(memory_space=pl.ANY)],
        out_specs=(tile,
                   pl.BlockSpec((1, D), lambda i, j: (0, 0)),
                   pl.BlockSpec((tm, 2 * tf), lambda i, j: (i, j)),
                   pl.BlockSpec((tm, tf), lambda i, j: (i, j)),
                   pl.BlockSpec((tm, D), lambda i, j: (i, 0))),
        out_shape=(jax.ShapeDtypeStruct((T, D), F32), jax.ShapeDtypeStruct((1, D), F32),
                   jax.ShapeDtypeStruct((T, 2 * F), BF16), jax.ShapeDtypeStruct((T, F), BF16),
                   jax.ShapeDtypeStruct((T, D), BF16)),
        scratch_shapes=[pltpu.VMEM((tm, D), BF16), pltpu.VMEM((tm, D), F32)],
        compiler_params=_params("arbitrary", "arbitrary"),
    )(*_hbm(dy, x, gain, gu, wgu, wd, after))


def _tn_grad(a, b, name, bm, tk):
    T, M = a.shape
    D = b.shape[1]
    nk = T // tk

    def body(a_ref, b_ref, o_ref, acc_sc):
        k = pl.program_id(1)

        @pl.when(k == 0)
        def _():
            acc_sc[...] = jnp.zeros_like(acc_sc)

        acc_sc[...] += _dot_tn(a_ref[...], b_ref[...])

        @pl.when(k == nk - 1)
        def _():
            o_ref[...] = acc_sc[...].astype(BF16)

    return pl.pallas_call(
        body, name=name, grid=(M // bm, nk),
        in_specs=[pl.BlockSpec((tk, bm), lambda i, k: (k, i)), pl.BlockSpec((tk, D), lambda i, k: (k, 0))],
        out_specs=pl.BlockSpec((bm, D), lambda i, k: (i, 0)),
        out_shape=jax.ShapeDtypeStruct((M, D), BF16),
        scratch_shapes=[pltpu.VMEM((bm, D), F32)],
        compiler_params=_params("arbitrary", "arbitrary"),
    )(*_hbm(a, b))


_Z_SPLITS = (0, 512, 640, 768, 1280, 1792, 2304)


def _mixin_fwd(x, gain, w_in_t, name, tm):
    T, D = x.shape
    widths = [b - a for a, b in zip(_Z_SPLITS[:-1], _Z_SPLITS[1:])]

    def body(x_ref, g_ref, w_ref, hn_ref, *outs):
        _, xhat = _rms_stats(x_ref[...])
        hn = (xhat * g_ref[...]).astype(BF16)
        hn_ref[...] = hn
        for o_ref, lo, hi in zip(outs, _Z_SPLITS[:-1], _Z_SPLITS[1:]):
            o_ref[...] = _dot_nt(hn, w_ref[lo:hi, :]).astype(BF16)

    return pl.pallas_call(
        body, name=name, grid=(T // tm,),
        in_specs=[pl.BlockSpec((tm, D), lambda i: (i, 0)),
                  pl.BlockSpec((1, D), lambda i: (0, 0)),
                  pl.BlockSpec(w_in_t.shape, lambda i: (0, 0))],
        out_specs=tuple([pl.BlockSpec((tm, D), lambda i: (i, 0))]
                        + [pl.BlockSpec((tm, w), lambda i: (i, 0)) for w in widths]),
        out_shape=tuple([jax.ShapeDtypeStruct((T, D), BF16)]
                        + [jax.ShapeDtypeStruct((T, w), BF16) for w in widths]),
        compiler_params=_params("arbitrary"),
    )(*_hbm(x, gain, w_in_t))


def _bucket_table():
    qi = np.arange(BLOCK, dtype=np.int32)[:, None]
    kj = np.arange(2 * BLOCK, dtype=np.int32)[None, :]
    dist = qi + BLOCK - kj
    n = np.maximum(dist, 0)
    max_exact = NUM_BUCKETS // 2
    large = max_exact + (np.log(np.maximum(n, 1).astype(np.float32) / max_exact)
                         / math.log(MAX_DISTANCE / max_exact)
                         * (NUM_BUCKETS - max_exact)).astype(np.int32)
    large = np.minimum(large, NUM_BUCKETS - 1)
    bucket = np.where(n < max_exact, n, large).astype(np.int32)
    valid = (dist >= 0) & (dist < WINDOW)
    return np.where(valid, bucket, -1).astype(np.int32)


def _bias_build(table, bucket, name):
    def body(t_ref, b_ref, o_ref):
        bk = b_ref[...]
        for h in range(N_Q_HEADS):
            def step(b, acc):
                return jnp.where(bk == b, t_ref[b, h], acc)
            o_ref[h] = lax.fori_loop(0, NUM_BUCKETS, step, jnp.full(bk.shape, MASKED, F32))

    return pl.pallas_call(
        body, name=name,
        in_specs=[pl.BlockSpec(memory_space=pltpu.SMEM), pl.BlockSpec(memory_space=pltpu.VMEM)],
        out_specs=pl.BlockSpec(memory_space=pltpu.VMEM),
        out_shape=jax.ShapeDtypeStruct((N_Q_HEADS,) + bucket.shape, F32),
    )(table, bucket)


def _bias_grad(dbias, bucket, name):
    def body(d_ref, b_ref, o_ref):
        bk = b_ref[...]
        row = lax.broadcasted_iota(jnp.int32, o_ref.shape, 0)
        lane = lax.broadcasted_iota(jnp.int32, o_ref.shape, 1)
        res = jnp.zeros(o_ref.shape, F32)
        for h in range(N_Q_HEADS):
            d = d_ref[h]

            def step(b, acc):
                tot = jnp.sum(jnp.where(bk == b, d, 0.0), axis=1, keepdims=True)
                tot = jnp.sum(tot, axis=0, keepdims=True)
                return jnp.where((row == b) & (lane == h), tot, acc)
            res = lax.fori_loop(0, NUM_BUCKETS, step, res)
        o_ref[...] = res

    return pl.pallas_call(
        body, name=name,
        in_specs=[pl.BlockSpec(memory_space=pltpu.VMEM), pl.BlockSpec(memory_space=pltpu.VMEM)],
        out_specs=pl.BlockSpec(memory_space=pltpu.VMEM),
        out_shape=jax.ShapeDtypeStruct((NUM_BUCKETS, 128), F32),
    )(*_hbm(dbias, bucket))


def _head_cols(h):
    return slice(h * HEAD_DIM, (h + 1) * HEAD_DIM)


def _stack_heads(ref, r0, g, dtype):
    return jnp.concatenate(
        [ref[pl.ds(r0, BLOCK), _head_cols(GQA_GROUP * g + j)].astype(dtype) for j in range(GQA_GROUP)], axis=0)


def _unstack_heads(ref, r0, g, val):
    for j in range(GQA_GROUP):
        ref[pl.ds(r0, BLOCK), _head_cols(GQA_GROUP * g + j)] = val[j * BLOCK:(j + 1) * BLOCK, :]


def _head_lanes(h):
    return slice(h * BLOCK, (h + 1) * BLOCK)


def _group_lanes(g):
    return slice(g * GROUP_ROWS, (g + 1) * GROUP_ROWS)


def _head_softmax(st, bias_t, sink, no_prev):
    s = st * SCALE + bias_t
    row = lax.broadcasted_iota(jnp.int32, s.shape, 0)
    s = jnp.where(no_prev & (row < BLOCK), MASKED, s)
    m = jnp.maximum(jnp.max(s, axis=0, keepdims=True), sink)
    p = jnp.exp(s - m)
    ps = jnp.exp(sink - m)
    r = 1.0 / (jnp.sum(p, axis=0, keepdims=True) + ps)
    return p * r, ps * r


def _load_band(kf_sc, vf_sc, kp_ref, kc_ref, vp_ref, vc_ref, tq):
    kf_sc[0:BLOCK, :] = kp_ref[...]
    kf_sc[BLOCK:BLOCK + tq, :] = kc_ref[...]
    vf_sc[0:BLOCK, :] = vp_ref[...]
    vf_sc[BLOCK:BLOCK + tq, :] = vc_ref[...]


def _attn_fwd(q, k, v, bias_t, sinks, gain, name, nblk):
    T = q.shape[0]
    tq = nblk * BLOCK

    def body(sink_ref, q_ref, kc_ref, kp_ref, vc_ref, vp_ref, bias_ref, g_ref, raw_ref, nrm_ref,
             kf_sc, vf_sc, o_sc, st_sc, pt_sc):
        i = pl.program_id(0)
        _load_band(kf_sc, vf_sc, kp_ref, kc_ref, vp_ref, vc_ref, tq)

        def block(b, carry):
            r0 = pl.multiple_of(b * BLOCK, BLOCK)
            no_prev = (i == 0) & (b == 0)
            for g in range(N_KV_HEADS):
                kb = kf_sc[pl.ds(r0, 2 * BLOCK), _head_cols(g)]
                st_sc[:, _group_lanes(g)] = _dot_nt(kb, _stack_heads(q_ref, r0, g, BF16))
            for h in range(N_Q_HEADS):
                p, _ = _head_softmax(st_sc[:, _head_lanes(h)], bias_ref[h], sink_ref[h], no_prev)
                pt_sc[:, _head_lanes(h)] = p.astype(BF16)
            for g in range(N_KV_HEADS):
                vb = vf_sc[pl.ds(r0, 2 * BLOCK), _head_cols(g)]
                _unstack_heads(o_sc, r0, g, _dot_tn(pt_sc[:, _group_lanes(g)], vb))
            return carry

        lax.fori_loop(0, nblk, block, 0)
        o = o_sc[...]
        raw_ref[...] = o.astype(BF16)
        _, ohat = _rms_stats(o)
        nrm_ref[...] = (ohat * g_ref[...]).astype(BF16)

    cur = lambda i: (i, 0)
    prev = lambda i: (jnp.maximum(i * nblk - 1, 0), 0)
    lanes = N_Q_HEADS * BLOCK
    return pl.pallas_call(
        body, name=name, grid=(T // tq,),
        in_specs=[pl.BlockSpec(memory_space=pltpu.SMEM),
                  pl.BlockSpec((tq, ATTN_WIDTH), cur),
                  pl.BlockSpec((tq, KV_WIDTH), cur), pl.BlockSpec((BLOCK, KV_WIDTH), prev),
                  pl.BlockSpec((tq, KV_WIDTH), cur), pl.BlockSpec((BLOCK, KV_WIDTH), prev),
                  pl.BlockSpec(bias_t.shape, lambda i: (0, 0, 0)),
                  pl.BlockSpec((1, ATTN_WIDTH), lambda i: (0, 0))],
        out_specs=(pl.BlockSpec((tq, ATTN_WIDTH), cur), pl.BlockSpec((tq, ATTN_WIDTH), cur)),
        out_shape=(jax.ShapeDtypeStruct((T, ATTN_WIDTH), BF16), jax.ShapeDtypeStruct((T, ATTN_WIDTH), BF16)),
        scratch_shapes=[pltpu.VMEM((tq + BLOCK, KV_WIDTH), BF16), pltpu.VMEM((tq + BLOCK, KV_WIDTH), BF16),
                        pltpu.VMEM((tq, ATTN_WIDTH), F32),
                        pltpu.VMEM((2 * BLOCK, lanes), F32), pltpu.VMEM((2 * BLOCK, lanes), BF16)],
        compiler_params=_params("arbitrary"),
    )(sinks, *_hbm(q, k, k, v, v, bias_t, gain))


def _attn_bwd(dmixed, raw, q, k, v, bias_t, sinks, gain, name, nblk):
    T = q.shape[0]
    tq = nblk * BLOCK
    nt = T // tq
    lanes = N_Q_HEADS * BLOCK

    def body(sink_ref, dm_ref, raw_ref, q_ref, kc_ref, kp_ref, vc_ref, vp_ref, bias_ref, g_ref,
             dq_ref, dk_ref, dv_ref, dbias_ref, dsink_ref, dgain_ref,
             do_sc, dq_sc, kf_sc, vf_sc, dkf_sc, dvf_sc, st_sc, dpt_sc, pt_sc, dst_sc, drow_sc,
             qs_sc, dos_sc, dsink_sc):
        i = pl.program_id(0)
        tile = nt - 1 - i

        @pl.when(i == 0)
        def _():
            dkf_sc[...] = jnp.zeros_like(dkf_sc)
            dvf_sc[...] = jnp.zeros_like(dvf_sc)
            dsink_sc[...] = jnp.zeros_like(dsink_sc)
            dbias_ref[...] = jnp.zeros_like(dbias_ref)
            dgain_ref[...] = jnp.zeros_like(dgain_ref)

        carry_k = dkf_sc[0:BLOCK, :]
        carry_v = dvf_sc[0:BLOCK, :]
        dkf_sc[0:tq, :] = jnp.zeros((tq, KV_WIDTH), F32)
        dvf_sc[0:tq, :] = jnp.zeros((tq, KV_WIDTH), F32)
        dkf_sc[tq:tq + BLOCK, :] = carry_k
        dvf_sc[tq:tq + BLOCK, :] = carry_v
        _load_band(kf_sc, vf_sc, kp_ref, kc_ref, vp_ref, vc_ref, tq)

        do, dgain = _rms_bwd(dm_ref[...].astype(F32), raw_ref[...].astype(F32), g_ref[...])
        dgain_ref[...] += dgain
        do_sc[...] = do
        ones = jnp.ones((SUBLANES, HEAD_DIM), BF16)

        def block(b, carry):
            r0 = pl.multiple_of(b * BLOCK, BLOCK)
            no_prev = (tile == 0) & (b == 0)
            for g in range(N_KV_HEADS):
                kb = kf_sc[pl.ds(r0, 2 * BLOCK), _head_cols(g)]
                vb = vf_sc[pl.ds(r0, 2 * BLOCK), _head_cols(g)]
                qg = _stack_heads(q_ref, r0, g, BF16)
                dog = _stack_heads(do_sc, r0, g, F32)
                prod = dog * _stack_heads(raw_ref, r0, g, F32)
                hi = prod.astype(BF16)
                lo = (prod - hi.astype(F32)).astype(BF16)
                drow_sc[:, _group_lanes(g)] = _dot_nt(ones, hi) + _dot_nt(ones, lo)
                dogb = dog.astype(BF16)
                qs_sc[g] = qg
                dos_sc[g] = dogb
                st_sc[:, _group_lanes(g)] = _dot_nt(kb, qg)
                dpt_sc[:, _group_lanes(g)] = _dot_nt(vb, dogb)
            for h in range(N_Q_HEADS):
                hl = _head_lanes(h)
                p, ps = _head_softmax(st_sc[:, hl], bias_ref[h], sink_ref[h], no_prev)
                rowdot = drow_sc[0:1, hl]
                ds = p * (dpt_sc[:, hl] - rowdot)
                dsink_sc[h:h + 1, :] += -(ps * rowdot)
                dbias_ref[h] += ds
                dst_sc[:, hl] = ds.astype(BF16)
                pt_sc[:, hl] = p.astype(BF16)
            for g in range(N_KV_HEADS):
                kb = kf_sc[pl.ds(r0, 2 * BLOCK), _head_cols(g)]
                dsg = dst_sc[:, _group_lanes(g)]
                _unstack_heads(dq_sc, r0, g, _dot_tn(dsg, kb) * SCALE)
                dkf_sc[pl.ds(r0, 2 * BLOCK), _head_cols(g)] += _dot(dsg, qs_sc[g]) * SCALE
                dvf_sc[pl.ds(r0, 2 * BLOCK), _head_cols(g)] += _dot(pt_sc[:, _group_lanes(g)], dos_sc[g])
            return carry

        lax.fori_loop(0, nblk, block, 0)
        dq_ref[...] = dq_sc[...].astype(BF16)
        dk_ref[...] = dkf_sc[BLOCK:BLOCK + tq, :].astype(BF16)
        dv_ref[...] = dvf_sc[BLOCK:BLOCK + tq, :].astype(BF16)

        @pl.when(i == nt - 1)
        def _():
            tot = jnp.sum(dsink_sc[...], axis=1, keepdims=True)
            dsink_ref[...] = jnp.broadcast_to(tot, dsink_ref.shape)

    cur = lambda i: (nt - 1 - i, 0)
    prev = lambda i: (jnp.maximum((nt - 1 - i) * nblk - 1, 0), 0)
    const2 = lambda i: (0, 0)
    const3 = lambda i: (0, 0, 0)
    return pl.pallas_call(
        body, name=name, grid=(nt,),
        in_specs=[pl.BlockSpec(memory_space=pltpu.SMEM),
                  pl.BlockSpec((tq, ATTN_WIDTH), cur),
                  pl.BlockSpec((tq, ATTN_WIDTH), cur),
                  pl.BlockSpec((tq, ATTN_WIDTH), cur),
                  pl.BlockSpec((tq, KV_WIDTH), cur), pl.BlockSpec((BLOCK, KV_WIDTH), prev),
                  pl.BlockSpec((tq, KV_WIDTH), cur), pl.BlockSpec((BLOCK, KV_WIDTH), prev),
                  pl.BlockSpec(bias_t.shape, const3),
                  pl.BlockSpec((1, ATTN_WIDTH), const2)],
        out_specs=(pl.BlockSpec((tq, ATTN_WIDTH), cur),
                   pl.BlockSpec((tq, KV_WIDTH), cur), pl.BlockSpec((tq, KV_WIDTH), cur),
                   pl.BlockSpec(bias_t.shape, const3),
                   pl.BlockSpec((N_Q_HEADS, 128), const2),
                   pl.BlockSpec((1, ATTN_WIDTH), const2)),
        out_shape=(jax.ShapeDtypeStruct((T, ATTN_WIDTH), BF16),
                   jax.ShapeDtypeStruct((T, KV_WIDTH), BF16), jax.ShapeDtypeStruct((T, KV_WIDTH), BF16),
                   jax.ShapeDtypeStruct(bias_t.shape, F32),
                   jax.ShapeDtypeStruct((N_Q_HEADS, 128), F32),
                   jax.ShapeDtypeStruct((1, ATTN_WIDTH), F32)),
        scratch_shapes=[pltpu.VMEM((tq, ATTN_WIDTH), F32), pltpu.VMEM((tq, ATTN_WIDTH), F32),
                        pltpu.VMEM((tq + BLOCK, KV_WIDTH), BF16), pltpu.VMEM((tq + BLOCK, KV_WIDTH), BF16),
                        pltpu.VMEM((tq + BLOCK, KV_WIDTH), F32), pltpu.VMEM((tq + BLOCK, KV_WIDTH), F32),
                        pltpu.VMEM((2 * BLOCK, lanes), F32), pltpu.VMEM((2 * BLOCK, lanes), F32),
                        pltpu.VMEM((2 * BLOCK, lanes), BF16), pltpu.VMEM((2 * BLOCK, lanes), BF16),
                        pltpu.VMEM((SUBLANES, lanes), F32),
                        pltpu.VMEM((N_KV_HEADS, GROUP_ROWS, HEAD_DIM), BF16),
                        pltpu.VMEM((N_KV_HEADS, GROUP_ROWS, HEAD_DIM), BF16),
                        pltpu.VMEM((N_Q_HEADS, 128), F32)],
        compiler_params=_params("arbitrary"),
    )(sinks, *_hbm(dmixed, raw, q, k, k, v, v, bias_t, gain))


def _shift_down(cu, tail):
    row = lax.broadcasted_iota(jnp.int32, cu.shape, 0)
    t6, t7 = tail[6:7, :], tail[7:8, :]
    s1 = jnp.where(row == 0, t7, pltpu.roll(cu, 1, 0))
    s2 = jnp.where(row == 0, t6, jnp.where(row == 1, t7, pltpu.roll(cu, 2, 0)))
    return s1, s2


def _shift_up(d, head):
    n = d.shape[0]
    row = lax.broadcasted_iota(jnp.int32, d.shape, 0)
    h0, h1 = head[0:1, :], head[1:2, :]
    s1 = jnp.where(row == n - 1, h0, pltpu.roll(d, n - 1, 0))
    s2 = jnp.where(row == n - 1, h1, jnp.where(row == n - 2, h0, pltpu.roll(d, n - 2, 0)))
    return s1, s2


def _mixout_fwd(x, attn_n, u, gb, gc, conv_w, gain, w_out, name, tm):
    T, D = x.shape

    def body(x_ref, an_ref, u_ref, b_ref, c_ref, cw_ref, g_ref, wo_ref, xo_ref, cn_ref, tail_sc):
        @pl.when(pl.program_id(0) == 0)
        def _():
            tail_sc[...] = jnp.zeros_like(tail_sc)

        cu = c_ref[...].astype(F32) * u_ref[...].astype(F32)
        s1, s2 = _shift_down(cu, tail_sc[...])
        tail_sc[...] = cu[tm - SUBLANES:tm, :]
        pre = cw_ref[0:1, :] * s2 + cw_ref[1:2, :] * s1 + cw_ref[2:3, :] * cu
        conv = b_ref[...].astype(F32) * pre
        _, chat = _rms_stats(conv)
        cn = (chat * g_ref[...]).astype(BF16)
        cn_ref[...] = cn
        xo_ref[...] = (x_ref[...] + _dot(an_ref[...], wo_ref[0:ATTN_WIDTH, :])
                       + _dot(cn, wo_ref[ATTN_WIDTH:ATTN_WIDTH + CONV_DIM, :]))

    row = lambda i: (i, 0)
    const = lambda i: (0, 0)
    return pl.pallas_call(
        body, name=name, grid=(T // tm,),
        in_specs=[pl.BlockSpec((tm, D), row), pl.BlockSpec((tm, ATTN_WIDTH), row),
                  pl.BlockSpec((tm, CONV_DIM), row), pl.BlockSpec((tm, CONV_DIM), row),
                  pl.BlockSpec((tm, CONV_DIM), row),
                  pl.BlockSpec(conv_w.shape, const), pl.BlockSpec((1, CONV_DIM), const),
                  pl.BlockSpec(w_out.shape, const)],
        out_specs=(pl.BlockSpec((tm, D), row), pl.BlockSpec((tm, CONV_DIM), row)),
        out_shape=(jax.ShapeDtypeStruct((T, D), F32), jax.ShapeDtypeStruct((T, CONV_DIM), BF16)),
        scratch_shapes=[pltpu.VMEM((SUBLANES, CONV_DIM), F32)],
        compiler_params=_params("arbitrary"),
    )(*_hbm(x, attn_n, u, gb, gc, conv_w, gain, w_out))


def _mixout_bwd(dy, attn_n, conv_n, w_out, after, name, tm):
    T, D = dy.shape
    W = ATTN_WIDTH + CONV_DIM
    nt = T // tm

    def body(dy_ref, an_ref, cn_ref, w_ref, after_ref, dm_ref, dw_ref, dw_sc):
        i = pl.program_id(0)

        @pl.when(i == 0)
        def _():
            dw_sc[...] = jnp.zeros_like(dw_sc)

        dyb = dy_ref[...].astype(BF16)
        dm_ref[...] = _dot_nt(dyb, w_ref[...]).astype(BF16)
        dw_sc[0:ATTN_WIDTH, :] += _dot_tn(an_ref[...], dyb)
        dw_sc[ATTN_WIDTH:W, :] += _dot_tn(cn_ref[...], dyb)

        @pl.when(i == nt - 1)
        def _():
            dw_ref[...] = dw_sc[...].astype(BF16)

    row = lambda i: (i, 0)
    const = lambda i: (0, 0)
    return pl.pallas_call(
        body, name=name, grid=(nt,),
        in_specs=[pl.BlockSpec((tm, D), row), pl.BlockSpec((tm, ATTN_WIDTH), row),
                  pl.BlockSpec((tm, CONV_DIM), row), pl.BlockSpec(w_out.shape, const),
                  pl.BlockSpec(memory_space=pl.ANY)],
        out_specs=(pl.BlockSpec((tm, W), row), pl.BlockSpec((W, D), const)),
        out_shape=(jax.ShapeDtypeStruct((T, W), BF16), jax.ShapeDtypeStruct((W, D), BF16)),
        scratch_shapes=[pltpu.VMEM((W, D), F32)],
        compiler_params=_params("arbitrary"),
    )(*_hbm(dy, attn_n, conv_n, w_out, after))


def _conv_bwd(dmixed, u, gb, gc, conv_w, gain, name, tc):
    T = u.shape[0]
    nt = T // tc
    per8 = tc // SUBLANES

    def body(dm_ref, u_ref, b_ref, c_ref, ut_ref, ct_ref, cw_ref, g_ref,
             du_ref, db_ref, dc_ref, dcw_ref, dgain_ref, head_sc):
        i = pl.program_id(0)

        @pl.when(i == 0)
        def _():
            head_sc[...] = jnp.zeros_like(head_sc)
            dcw_ref[...] = jnp.zeros_like(dcw_ref)
            dgain_ref[...] = jnp.zeros_like(dgain_ref)

        uu = u_ref[...].astype(F32)
        cc = c_ref[...].astype(F32)
        bb = b_ref[...].astype(F32)
        cu = cc * uu
        tail = jnp.where(i == nt - 1, 0.0, ct_ref[...].astype(F32) * ut_ref[...].astype(F32))
        s1, s2 = _shift_down(cu, tail)
        w0, w1, w2 = cw_ref[0:1, :], cw_ref[1:2, :], cw_ref[2:3, :]
        pre = w0 * s2 + w1 * s1 + w2 * cu
        dconv, dgain = _rms_bwd(dm_ref[...].astype(F32), bb * pre, g_ref[...])
        dgain_ref[...] += dgain
        db_ref[...] = (dconv * pre).astype(BF16)
        dpre = dconv * bb
        dcw_ref[0:1, :] += jnp.sum(dpre * s2, axis=0, keepdims=True)
        dcw_ref[1:2, :] += jnp.sum(dpre * s1, axis=0, keepdims=True)
        dcw_ref[2:3, :] += jnp.sum(dpre * cu, axis=0, keepdims=True)
        n1, n2 = _shift_up(dpre, head_sc[...])
        head_sc[...] = dpre[0:SUBLANES, :]
        dcu = w2 * dpre + w1 * n1 + w0 * n2
        du_ref[...] = (dcu * cc).astype(BF16)
        dc_ref[...] = (dcu * uu).astype(BF16)

    rev = lambda i: (nt - 1 - i, 0)
    rev_right = lambda i: (nt - 1 - i, 1)
    tail_map = lambda i: (jnp.maximum((nt - 1 - i) * per8 - 1, 0), 0)
    const = lambda i: (0, 0)
    return pl.pallas_call(
        body, name=name, grid=(nt,),
        in_specs=[pl.BlockSpec((tc, CONV_DIM), rev_right),
                  pl.BlockSpec((tc, CONV_DIM), rev), pl.BlockSpec((tc, CONV_DIM), rev),
                  pl.BlockSpec((tc, CONV_DIM), rev),
                  pl.BlockSpec((SUBLANES, CONV_DIM), tail_map), pl.BlockSpec((SUBLANES, CONV_DIM), tail_map),
                  pl.BlockSpec(conv_w.shape, const), pl.BlockSpec((1, CONV_DIM), const)],
        out_specs=(pl.BlockSpec((tc, CONV_DIM), rev), pl.BlockSpec((tc, CONV_DIM), rev),
                   pl.BlockSpec((tc, CONV_DIM), rev),
                   pl.BlockSpec((SUBLANES, CONV_DIM), const), pl.BlockSpec((1, CONV_DIM), const)),
        out_shape=(jax.ShapeDtypeStruct((T, CONV_DIM), BF16), jax.ShapeDtypeStruct((T, CONV_DIM), BF16),
                   jax.ShapeDtypeStruct((T, CONV_DIM), BF16),
                   jax.ShapeDtypeStruct((SUBLANES, CONV_DIM), F32), jax.ShapeDtypeStruct((1, CONV_DIM), F32)),
        scratch_shapes=[pltpu.VMEM((SUBLANES, CONV_DIM), F32)],
        compiler_params=_params("arbitrary"),
    )(*_hbm(dmixed, u, gb, gc, u, gc, conv_w, gain))


def _mixin_bwd(dy, x, gain, dz, w_in_t, name, tm):
    T, D = x.shape
    nz = len(dz)

    def body(dy_ref, x_ref, g_ref, *rest):
        dz_refs, wt_ref, dx_ref, dgain_ref = rest[:nz], rest[nz], rest[nz + 1], rest[nz + 2]

        @pl.when(pl.program_id(0) == 0)
        def _():
            dgain_ref[...] = jnp.zeros_like(dgain_ref)

        dh = jnp.zeros((tm, D), F32)
        for r, lo, hi in zip(dz_refs, _Z_SPLITS[:-1], _Z_SPLITS[1:]):
            dh += _dot(r[...], wt_ref[lo:hi, :])
        dx, dgain = _rms_bwd(dh, x_ref[...], g_ref[...])
        dgain_ref[...] += dgain
        dx_ref[...] = dy_ref[...] + dx

    row = lambda i: (i, 0)
    const = lambda i: (0, 0)
    return pl.pallas_call(
        body, name=name, grid=(T // tm,),
        in_specs=[pl.BlockSpec((tm, D), row), pl.BlockSpec((tm, D), row), pl.BlockSpec((1, D), const)]
                 + [pl.BlockSpec((tm, a.shape[1]), row) for a in dz]
                 + [pl.BlockSpec(w_in_t.shape, const)],
        out_specs=(pl.BlockSpec((tm, D), row), pl.BlockSpec((1, D), const)),
        out_shape=(jax.ShapeDtypeStruct((T, D), F32), jax.ShapeDtypeStruct((1, D), F32)),
        compiler_params=_params("arbitrary"),
    )(*_hbm(dy, x, gain, *dz, w_in_t))


def _win_grad(dz, hn, name, tk):
    T, D = hn.shape
    nz = len(dz)
    nt = T // tk
    W = _Z_SPLITS[-1]

    def body(hn_ref, *rest):
        dz_refs, dw_ref, dw_sc = rest[:nz], rest[nz], rest[nz + 1]
        i = pl.program_id(0)

        @pl.when(i == 0)
        def _():
            dw_sc[...] = jnp.zeros_like(dw_sc)

        hn = hn_ref[...]
        for r, lo, hi in zip(dz_refs, _Z_SPLITS[:-1], _Z_SPLITS[1:]):
            dw_sc[lo:hi, :] += _dot_tn(r[...], hn)

        @pl.when(i == nt - 1)
        def _():
            dw_ref[...] = dw_sc[...].astype(BF16)

    row = lambda i: (i, 0)
    return pl.pallas_call(
        body, name=name, grid=(nt,),
        in_specs=[pl.BlockSpec((tk, D), row)] + [pl.BlockSpec((tk, a.shape[1]), row) for a in dz],
        out_specs=pl.BlockSpec((W, D), lambda i: (0, 0)),
        out_shape=jax.ShapeDtypeStruct((W, D), BF16),
        scratch_shapes=[pltpu.VMEM((W, D), F32)],
        compiler_params=_params("arbitrary"),
    )(*_hbm(hn, *dz))


def _loss_head(x, target, gain, name, tm):
    T, D = x.shape

    def body(x_ref, t_ref, g_ref, dx_ref, dgain_ref, loss_ref):
        @pl.when(pl.program_id(0) == 0)
        def _():
            dgain_ref[...] = jnp.zeros_like(dgain_ref)
            loss_ref[...] = jnp.zeros_like(loss_ref)

        xv = x_ref[...]
        gain_v = g_ref[...]
        _, xhat = _rms_stats(xv)
        err = xhat * gain_v - t_ref[...]
        part = 0.5 * jnp.sum(jnp.mean(err * err, axis=-1, keepdims=True), axis=0, keepdims=True)
        loss_ref[...] += part
        dx, dgain = _rms_bwd(err * (1.0 / D), xv, gain_v)
        dgain_ref[...] += dgain
        dx_ref[...] = dx

    row = lambda i: (i, 0)
    const = lambda i: (0, 0)
    return pl.pallas_call(
        body, name=name, grid=(T // tm,),
        in_specs=[pl.BlockSpec((tm, D), row), pl.BlockSpec((tm, D), row), pl.BlockSpec((1, D), const)],
        out_specs=(pl.BlockSpec((tm, D), row), pl.BlockSpec((1, D), const),
                   pl.BlockSpec((SUBLANES, 128), const)),
        out_shape=(jax.ShapeDtypeStruct((T, D), F32), jax.ShapeDtypeStruct((1, D), F32),
                   jax.ShapeDtypeStruct((SUBLANES, 128), F32)),
        compiler_params=_params("arbitrary"),
    )(*_hbm(x, target, gain))


def _adamw(parts, w, m, v, name, tr):
    P = parts.shape[0]
    R, C = w.shape

    def body(p_ref, w_ref, m_ref, v_ref, g_ref, d_ref, nm_ref, nv_ref):
        g = p_ref[0].astype(F32)
        for d in range(1, P):
            g = g + p_ref[d].astype(F32)
        nm = ADAM_B1 * m_ref[...] + (1.0 - ADAM_B1) * g
        nv = ADAM_B2 * v_ref[...] + (1.0 - ADAM_B2) * (g * g)
        m_hat = nm / (1.0 - ADAM_B1 ** ADAM_STEP)
        v_hat = nv / (1.0 - ADAM_B2 ** ADAM_STEP)
        g_ref[...] = g
        nm_ref[...] = nm
        nv_ref[...] = nv
        d_ref[...] = -ADAM_LR * (m_hat / (jnp.sqrt(v_hat) + ADAM_EPS) + ADAM_WD * w_ref[...])

    row = lambda i: (i, 0)
    spec = pl.BlockSpec((tr, C), row)
    shp = jax.ShapeDtypeStruct((R, C), F32)
    return pl.pallas_call(
        body, name=name, grid=(R // tr,),
        in_specs=[pl.BlockSpec((P, tr, C), lambda i: (0, i, 0)), spec, spec, spec],
        out_specs=(spec, spec, spec, spec),
        out_shape=(shp, shp, shp, shp),
        compiler_params=_params("arbitrary"),
    )(*_hbm(parts, w, m, v))


def _columns_of_blocks(g):
    n, R, w = g.shape
    return g.transpose(1, 0, 2).reshape(R, n * w)


def _pad_row(vec):
    vec = vec.reshape(1, -1)
    return jnp.pad(vec, ((0, 0), (0, PACK_COLS - vec.shape[1])))


def kernel(x, rel_bias_table, ffn1_norm, ffn1_w_gate, ffn1_w_up, ffn1_w_down, mix_norm, w_in, conv_w, attn_sinks, attn_out_norm, conv_out_norm, w_out, ffn2_norm, ffn2_w_gate, ffn2_w_up, ffn2_w_down, final_norm, loss_target, m_rel_bias_table, m_ffn1_norm, m_ffn1_w_gate, m_ffn1_w_up, m_ffn1_w_down, m_mix_norm, m_w_in, m_conv_w, m_attn_sinks, m_attn_out_norm, m_conv_out_norm, m_w_out, m_ffn2_norm, m_ffn2_w_gate, m_ffn2_w_up, m_ffn2_w_down, m_final_norm, v_rel_bias_table, v_ffn1_norm, v_ffn1_w_gate, v_ffn1_w_up, v_ffn1_w_down, v_mix_norm, v_w_in, v_conv_w, v_attn_sinks, v_attn_out_norm, v_conv_out_norm, v_w_out, v_ffn2_norm, v_ffn2_w_gate, v_ffn2_w_up, v_ffn2_w_down, v_final_norm):
    T, D = x.shape[1], x.shape[2]
    x0 = x[0]
    target = loss_target[0]
    tm = min(TM_FFN, T)
    tm_bwd = min(TM_FFN_BWD, T)
    tm_mix = min(TM_MIX, T)
    tk = min(TK_WGRAD, T)
    tf = TF_FFN
    nblk = min(ATTN_BLOCKS, T // BLOCK)
    me = 4 * lax.axis_index("x") + 2 * lax.axis_index("y") + lax.axis_index("c")

    big = {
        "ffn1_w_gate": (ffn1_w_gate[0], m_ffn1_w_gate[0], v_ffn1_w_gate[0], True),
        "ffn1_w_up": (ffn1_w_up[0], m_ffn1_w_up[0], v_ffn1_w_up[0], True),
        "ffn1_w_down": (ffn1_w_down[0], m_ffn1_w_down[0], v_ffn1_w_down[0], False),
        "w_in": (w_in[0], m_w_in[0], v_w_in[0], True),
        "w_out": (w_out[0], m_w_out[0], v_w_out[0], False),
        "ffn2_w_gate": (ffn2_w_gate[0], m_ffn2_w_gate[0], v_ffn2_w_gate[0], True),
        "ffn2_w_up": (ffn2_w_up[0], m_ffn2_w_up[0], v_ffn2_w_up[0], True),
        "ffn2_w_down": (ffn2_w_down[0], m_ffn2_w_down[0], v_ffn2_w_down[0], False),
    }

    def block_to_send(name):
        w, _, _, transposed = big[name]
        return (w.T if transposed else w).astype(BF16)

    names1 = ["ffn1_w_gate", "ffn1_w_up", "ffn1_w_down"]
    names_rest = ["w_in", "w_out", "ffn2_w_gate", "ffn2_w_up", "ffn2_w_down"]
    h1, token = _exchange_start([block_to_send(n) for n in names1], False, "gather_start_ffn1", ffn1_norm)
    h_rest, token = _exchange_start([block_to_send(n) for n in names_rest[:2]] + [conv_w[0]]
                                    + [block_to_send(n) for n in names_rest[2:]], False,
                                    "gather_start_rest", token)
    wgt1, wut1, wd1 = [g.reshape(-1, D) for g in _exchange_wait(h1, False, "gather_wait_ffn1", token)]

    wgu1 = _stack_gate_up(wgt1, wut1, tf)
    x1, xn1, gu1 = _ffn_fwd(x0, ffn1_norm, wgu1, wd1, "ffn1_fwd", tm, tf)
    mixw = _exchange_wait(h_rest[:3], False, "gather_wait_mix", x1)
    win_t = mixw[0].reshape(-1, D)
    wout = mixw[1].reshape(-1, D)
    cw = _columns_of_blocks(mixw[2])
    hn, q, k, v, u, gb, gc = _mixin_fwd(x1, mix_norm, win_t, "mixin_fwd", tm_mix)
    bucket = jnp.asarray(_bucket_table().T.copy())
    sinks = attn_sinks.reshape(-1)
    bias_t = _bias_build(rel_bias_table, bucket, "bias_build")
    attn_raw, attn_n = _attn_fwd(q, k, v, bias_t, sinks, attn_out_norm, "attn_fwd", nblk)
    x2, conv_n = _mixout_fwd(x1, attn_n, u, gb, gc, cw, conv_out_norm, wout, "mixout_fwd", tm_mix)
    wgt2, wut2, wd2 = [g.reshape(-1, D) for g in _exchange_wait(h_rest[3:], False, "gather_wait_ffn2", x2)]
    wgu2 = _stack_gate_up(wgt2, wut2, tf)
    x3, xn2, gu2 = _ffn_fwd(x2, ffn2_norm, wgu2, wd2, "ffn2_fwd", tm, tf)

    def blocks(g):
        return g.reshape(N_DEV, -1, D)

    dx3, d_final, loss_part = _loss_head(x3, target, final_norm.reshape(1, D), "loss_head", tm_mix)
    dx2, d_ffn2_norm, dgu2, hh2, do2 = _ffn_bwd(
        dx3, x2, ffn2_norm, gu2, wgu2, wd2, dx3, "ffn2_bwd", tm_bwd, tf)
    d_wg2, d_wu2 = _unstack_gate_up(_tn_grad(dgu2, xn2, "ffn2_wgrad_gu", BM_WGRAD, tk), tf)
    d_wd2 = _tn_grad(hh2, do2, "ffn2_wgrad_down", BM_WGRAD, tk)
    handles2, token2 = _exchange_start([blocks(d_wg2), blocks(d_wu2), blocks(d_wd2)], True,
                                       "grads_start_ffn2", dx2)

    dmixed, d_wout = _mixout_bwd(dx2, attn_n, conv_n, wout, token2, "mixout_bwd", tm_mix)
    dq, dk, dv, dbias, dsink, d_attn_norm = _attn_bwd(
        dmixed, attn_raw, q, k, v, bias_t, sinks, attn_out_norm, "attn_bwd", nblk)
    du, dgb, dgc, d_cw, d_conv_norm = _conv_bwd(dmixed, u, gb, gc, cw, conv_out_norm, "conv_bwd", tm_mix)
    d_table = _bias_grad(dbias, bucket, "bias_grad")
    dz = [dq, dk, dv, du, dgb, dgc]
    dx1, d_mix_norm = _mixin_bwd(dx2, x1, mix_norm, dz, win_t, "mixin_bwd", tm_mix)
    d_win_t = _win_grad(dz, hn, "win_grad", min(TM_MIX, T))
    handles_mix, token_mix = _exchange_start([blocks(d_win_t), blocks(d_wout)], True, "grads_start_mix", dx1)

    dx0, d_ffn1_norm, dgu1, hh1, do1 = _ffn_bwd(
        dx1, x0, ffn1_norm, gu1, wgu1, wd1, token_mix, "ffn1_bwd", tm_bwd, tf)
    d_wg1, d_wu1 = _unstack_gate_up(_tn_grad(dgu1, xn1, "ffn1_wgrad_gu", BM_WGRAD, tk), tf)
    handles1_gu, token1 = _exchange_start([blocks(d_wg1), blocks(d_wu1)], True, "grads_start_ffn1_gu", dx0)
    d_wd1 = _tn_grad(hh1, do1, "ffn1_wgrad_down", BM_WGRAD, tk)
    handles1_d, token1 = _exchange_start([blocks(d_wd1)], True, "grads_start_ffn1_down", token1)

    res = {}

    def update(names, parts):
        last = None
        for name, p in zip(names, parts):
            w, m_, v_, transposed = big[name]
            if transposed:
                w, m_, v_ = w.T, m_.T, v_.T
            new = _adamw(p, w, m_, v_, "adamw_" + name, _row_tile(w.shape[0], ADAM_ROWS))
            res[name] = tuple((a.T if transposed else a)[None] for a in new)
            last = new[0]
        return last

    parts2 = _exchange_wait(handles2, True, "grads_wait_ffn2", token1)
    done2 = update(["ffn2_w_gate", "ffn2_w_up", "ffn2_w_down"], parts2)
    parts_mix = _exchange_wait(handles_mix, True, "grads_wait_mix", done2)
    done_mix = update(["w_in", "w_out"], parts_mix)

    def pack(ffn1, mixn, ffn2, fin, attn_n_, conv_n_, sink_, extra, convw, table):
        rows = [_pad_row(ffn1), _pad_row(mixn), _pad_row(ffn2), _pad_row(fin),
                _pad_row(jnp.concatenate([attn_n_.reshape(-1), conv_n_.reshape(-1)])),
                _pad_row(sink_), _pad_row(extra),
                jnp.zeros((1, PACK_COLS), F32),
                jnp.pad(convw, ((0, 0), (0, PACK_COLS - convw.shape[1]))),
                _pad_row(table),
                jnp.zeros((PACK_ROWS - 12, PACK_COLS), F32)]
        return jnp.concatenate(rows, axis=0)

    def own_channels(a):
        full = jnp.zeros((a.shape[1], CONV_DIM), F32)
        return lax.dynamic_update_slice(full, a[0], (0, me * a.shape[2]))

    g_pack = pack(d_ffn1_norm, d_mix_norm, d_ffn2_norm, d_final, d_attn_norm, d_conv_norm,
                  dsink[:, 0], loss_part[0, :1], d_cw[:3], d_table[:, :N_Q_HEADS])
    zero1 = jnp.zeros((1,), F32)
    w_pack = pack(ffn1_norm, mix_norm, ffn2_norm, final_norm, attn_out_norm, conv_out_norm,
                  attn_sinks, zero1, own_channels(conv_w), rel_bias_table)
    m_pack = pack(m_ffn1_norm, m_mix_norm, m_ffn2_norm, m_final_norm, m_attn_out_norm, m_conv_out_norm,
                  m_attn_sinks, zero1, own_channels(m_conv_w), m_rel_bias_table)
    v_pack = pack(v_ffn1_norm, v_mix_norm, v_ffn2_norm, v_final_norm, v_attn_out_norm, v_conv_out_norm,
                  v_attn_sinks, zero1, own_channels(v_conv_w), v_rel_bias_table)
    (g_all,) = _exchange([g_pack], False, "gather_small", done_mix)
    packs = _adamw(g_all, w_pack, m_pack, v_pack, "adamw_small", PACK_ROWS)

    parts1_gu = _exchange_wait(handles1_gu, True, "grads_wait_ffn1_gu", packs[0])
    done1_gu = update(["ffn1_w_gate", "ffn1_w_up"], parts1_gu)
    parts1_d = _exchange_wait(handles1_d, True, "grads_wait_ffn1_down", done1_gu)
    update(["ffn1_w_down"], parts1_d)

    def unpack(pk):
        cwb = lax.dynamic_slice(pk[8:11, :CONV_DIM], (0, me * conv_w.shape[2]), (3, conv_w.shape[2]))
        return {
            "ffn1_norm": pk[0:1, :D], "mix_norm": pk[1:2, :D], "ffn2_norm": pk[2:3, :D],
            "final_norm": pk[3, :D],
            "attn_out_norm": pk[4:5, :ATTN_WIDTH], "conv_out_norm": pk[4:5, ATTN_WIDTH:ATTN_WIDTH + CONV_DIM],
            "attn_sinks": pk[5:6, :N_Q_HEADS],
            "conv_w": cwb[None],
            "rel_bias_table": pk[11, :NUM_BUCKETS * N_Q_HEADS].reshape(NUM_BUCKETS, N_Q_HEADS),
        }

    small = [unpack(pk) for pk in packs]
    loss = packs[0][6, 0]

    order = ["rel_bias_table", "ffn1_norm", "ffn1_w_gate", "ffn1_w_up", "ffn1_w_down", "mix_norm", "w_in",
             "conv_w", "attn_sinks", "attn_out_norm", "conv_out_norm", "w_out", "ffn2_norm",
             "ffn2_w_gate", "ffn2_w_up", "ffn2_w_down", "final_norm"]
    outs = [loss, dx0[None]]
    for kind in range(4):
        for name in order:
            outs.append(res[name][kind] if name in res else small[kind][name])
    return tuple(outs)
```

```python
import math

import numpy as np
import jax
import jax.numpy as jnp
from jax import lax
from jax.experimental import pallas as pl
from jax.experimental.pallas import tpu as pltpu

F32 = jnp.float32
BF16 = jnp.bfloat16

N_DEV = 8
EPS = 1e-6
HEAD_DIM = 64
N_Q_HEADS = 8
N_KV_HEADS = 2
GQA_GROUP = 4
ATTN_WIDTH = 512
KV_WIDTH = 128
CONV_DIM = 512
BLOCK = 128
WINDOW = 128
NUM_BUCKETS = 32
MAX_DISTANCE = 128
SCALE = HEAD_DIM ** -0.5
MASKED = -1e30
GROUP_ROWS = GQA_GROUP * BLOCK

ADAM_LR = 0.001
ADAM_B1 = 0.9
ADAM_B2 = 0.999
ADAM_EPS = 1e-08
ADAM_WD = 0.01
ADAM_STEP = 10

VMEM_LIMIT_BYTES = 40 * 1024 * 1024
SUBLANES = 8
PACK_ROWS = 16
PACK_COLS = 1024

TM_FFN = 1024
TM_FFN_BWD = 1024
TM_MIX = 512
TK_WGRAD = 1024
TF_FFN = 256
BM_WGRAD = 1408
ROW_GROUPS = 4
ATTN_BLOCKS = 4
ADAM_ROWS = 256


def _row_tile(rows, limit):
    best = rows
    for t in range(16, min(rows, limit) + 1, 16):
        if rows % t == 0:
            best = t
    return best


def _params(*sem):
    return pltpu.CompilerParams(dimension_semantics=sem, vmem_limit_bytes=VMEM_LIMIT_BYTES)


def _hbm(*arrays):
    return [pltpu.with_memory_space_constraint(a, pltpu.HBM) for a in arrays]


def _dot(a, b):
    return jnp.dot(a, b, preferred_element_type=F32)


def _dot_nt(a, b):
    return lax.dot_general(a, b, (((1,), (1,)), ((), ())), preferred_element_type=F32)


def _dot_tn(a, b):
    return lax.dot_general(a, b, (((0,), (0,)), ((), ())), preferred_element_type=F32)


def _sigmoid(g):
    return 0.5 * jnp.tanh(0.5 * g) + 0.5


def _rms_stats(x):
    inv = lax.rsqrt(jnp.mean(x * x, axis=-1, keepdims=True) + EPS)
    return inv, x * inv


def _rms_bwd(dy, x, gain):
    inv, xhat = _rms_stats(x)
    dgain = jnp.sum(dy * xhat, axis=0, keepdims=True)
    dxh = dy * gain
    dx = inv * (dxh - xhat * jnp.mean(dxh * xhat, axis=-1, keepdims=True))
    return dx, dgain


def _peer_list():
    x, y, c = lax.axis_index("x"), lax.axis_index("y"), lax.axis_index("c")
    peers = []
    for k in range(1, N_DEV):
        px = 1 - x if (k >> 2) & 1 else x
        py = 1 - y if (k >> 1) & 1 else y
        pc = 1 - c if k & 1 else c
        peers.append((px, py, pc))
    return 4 * x + 2 * y + c, peers


def _exchange(arrs, scatter, name, after):
    n = len(arrs)
    out_shape = []
    for a in arrs:
        shp = a.shape if scatter else (N_DEV,) + a.shape
        out_shape.append(jax.ShapeDtypeStruct(shp, a.dtype))

    def body(*refs):
        ins, outs = refs[:n], refs[n + 1:2 * n + 1]
        send_sems, recv_sems, local_sems = refs[2 * n + 1:]
        me, peers = _peer_list()
        started = []
        for a in range(n):
            own = ins[a].at[me] if scatter else ins[a]
            loc = pltpu.make_async_copy(own, outs[a].at[me], local_sems.at[a])
            loc.start()
            started.append(loc)
        sends = []
        for a in range(n):
            for k, (px, py, pc) in enumerate(peers):
                src = ins[a].at[4 * px + 2 * py + pc] if scatter else ins[a]
                cp = pltpu.make_async_remote_copy(
                    src_ref=src, dst_ref=outs[a].at[me],
                    send_sem=send_sems.at[a, k], recv_sem=recv_sems.at[a, k],
                    device_id=(px, py, pc), device_id_type=pl.DeviceIdType.MESH)
                cp.start()
                sends.append(cp)
        for a in range(n):
            for k, (px, py, pc) in enumerate(peers):
                landed = outs[a].at[4 * px + 2 * py + pc]
                pltpu.make_async_remote_copy(
                    src_ref=landed, dst_ref=landed,
                    send_sem=send_sems.at[a, k], recv_sem=recv_sems.at[a, k],
                    device_id=(px, py, pc), device_id_type=pl.DeviceIdType.MESH).wait_recv()
        for cp in sends:
            cp.wait_send()
        for loc in started:
            loc.wait()

    hbm = pl.BlockSpec(memory_space=pl.ANY)
    return pl.pallas_call(
        body, name=name, out_shape=tuple(out_shape),
        in_specs=[hbm] * (n + 1), out_specs=tuple([hbm] * n),
        scratch_shapes=[pltpu.SemaphoreType.DMA((n, N_DEV - 1)),
                        pltpu.SemaphoreType.DMA((n, N_DEV - 1)),
                        pltpu.SemaphoreType.DMA((n,))],
    )(*arrs, after)


def _gather_two_level(arrs, name, after):
    n = len(arrs)
    out_shape = tuple(jax.ShapeDtypeStruct((N_DEV,) + a.shape, a.dtype) for a in arrs)

    def body(*refs):
        ins, outs = refs[:n], refs[n + 1:2 * n + 1]
        send_sems, recv_sems, local_sems = refs[2 * n + 1:]
        x, y, c = lax.axis_index("x"), lax.axis_index("y"), lax.axis_index("c")
        me, sibling = (x, y, c), (x, y, 1 - c)
        chips = [(1 - x, y), (x, 1 - y), (1 - x, 1 - y)]

        def copy(a, k, block, to, src=None):
            slot = outs[a].at[4 * block[0] + 2 * block[1] + block[2]]
            return pltpu.make_async_remote_copy(
                src_ref=slot if src is None else src, dst_ref=slot,
                send_sem=send_sems.at[a, k], recv_sem=recv_sems.at[a, k],
                device_id=to, device_id_type=pl.DeviceIdType.MESH)

        started = []
        for a in range(n):
            loc = pltpu.make_async_copy(ins[a], outs[a].at[4 * x + 2 * y + c], local_sems.at[a])
            loc.start()
            started.append(loc)
        sends = []
        for a in range(n):
            sends.append(copy(a, 0, me, sibling, src=ins[a]))
            sends += [copy(a, 1 + j, me, (*chip, c), src=ins[a]) for j, chip in enumerate(chips)]
        for cp in sends:
            cp.start()
        for j, chip in enumerate(chips):
            for a in range(n):
                copy(a, 1 + j, (*chip, c), me).wait_recv()
                fwd = copy(a, 4 + j, (*chip, c), sibling)
                fwd.start()
                sends.append(fwd)
        for a in range(n):
            copy(a, 0, sibling, me).wait_recv()
            for j, chip in enumerate(chips):
                copy(a, 4 + j, (*chip, 1 - c), me).wait_recv()
        for cp in sends:
            cp.wait_send()
        for loc in started:
            loc.wait()

    hbm = pl.BlockSpec(memory_space=pl.ANY)
    return pl.pallas_call(
        body, name=name, out_shape=out_shape,
        in_specs=[hbm] * (n + 1), out_specs=tuple([hbm] * n),
        scratch_shapes=[pltpu.SemaphoreType.DMA((n, N_DEV - 1)),
                        pltpu.SemaphoreType.DMA((n, N_DEV - 1)),
                        pltpu.SemaphoreType.DMA((n,))],
    )(*arrs, after)


_HBM = pl.BlockSpec(memory_space=pltpu.HBM)
_SEM = pl.BlockSpec(memory_space=pltpu.SEMAPHORE)
_EFFECT = pltpu.SideEffectType.DATAFLOW_SIDE_EFFECTING


def _split_copies(srcs, lands, send_sems, recv_sems, scatter):
    me, peers = _peer_list()
    copies = []
    for a in range(len(srcs)):
        for k, (px, py, pc) in enumerate(peers):
            src = srcs[a].at[4 * px + 2 * py + pc] if scatter else srcs[a]
            copies.append(pltpu.make_async_remote_copy(
                src_ref=src, dst_ref=lands[a].at[me],
                send_sem=send_sems[a].at[k], recv_sem=recv_sems[a].at[k],
                device_id=(px, py, pc), device_id_type=pl.DeviceIdType.MESH))
    return copies


def _exchange_start(arrs, scatter, name, after):
    n = len(arrs)
    me = 4 * lax.axis_index("x") + 2 * lax.axis_index("y") + lax.axis_index("c")
    lands = []
    for a in arrs:
        own = lax.dynamic_index_in_dim(a, me, 0, keepdims=True) if scatter else a[None]
        shp = a.shape if scatter else (N_DEV,) + a.shape
        lands.append(lax.dynamic_update_slice(lax.empty(shp, a.dtype), own, (me,) + (0,) * (len(shp) - 1)))

    def body(*refs):
        srcs, lnds = refs[:n], refs[n:2 * n]
        outs = refs[2 * n + 1:]
        send_sems, recv_sems = outs[:n], outs[n:2 * n]
        token = outs[4 * n]
        for cp in _split_copies(srcs, lnds, send_sems, recv_sems, scatter):
            cp.start()
        token[...] = jnp.zeros_like(token)

    sem = pltpu.SemaphoreType.DMA((N_DEV - 1,))
    out_shape = ([sem] * (2 * n) + [pltpu.HBM(a.shape, a.dtype) for a in arrs]
                 + [pltpu.HBM(l.shape, l.dtype) for l in lands] + [jax.ShapeDtypeStruct((SUBLANES, 128), F32)])
    res = pl.pallas_call(
        body, name=name, out_shape=tuple(out_shape),
        in_specs=[_HBM] * (2 * n) + [pl.BlockSpec(memory_space=pl.ANY)],
        out_specs=tuple([_SEM] * (2 * n) + [_HBM] * (2 * n) + [pl.BlockSpec(memory_space=pltpu.VMEM)]),
        input_output_aliases={i: 2 * n + i for i in range(2 * n)},
        compiler_params=pltpu.CompilerParams(has_side_effects=_EFFECT),
    )(*[pltpu.with_memory_space_constraint(a, pltpu.HBM) for a in arrs],
      *[pltpu.with_memory_space_constraint(l, pltpu.HBM) for l in lands], after)
    handles = [(res[2 * n + a], res[3 * n + a], res[a], res[n + a]) for a in range(n)]
    return handles, res[4 * n]


def _exchange_wait(handles, scatter, name, after):
    n = len(handles)

    def body(*refs):
        srcs, lnds = refs[:n], refs[n:2 * n]
        send_sems, recv_sems = refs[2 * n:3 * n], refs[3 * n:4 * n]
        for cp in _split_copies(srcs, lnds, send_sems, recv_sems, scatter):
            cp.wait_send()
            cp.wait_recv()

    srcs = [h[0] for h in handles]
    lands = [h[1] for h in handles]
    res = pl.pallas_call(
        body, name=name,
        out_shape=tuple([pltpu.HBM(a.shape, a.dtype) for a in srcs] + [pltpu.HBM(l.shape, l.dtype) for l in lands]),
        in_specs=[_HBM] * (2 * n) + [_SEM] * (2 * n) + [pl.BlockSpec(memory_space=pl.ANY)],
        out_specs=tuple([_HBM] * (2 * n)),
        input_output_aliases={i: i for i in range(2 * n)},
        compiler_params=pltpu.CompilerParams(has_side_effects=_EFFECT),
    )(*srcs, *lands, *[h[2] for h in handles], *[h[3] for h in handles], after)
    return list(res[n:])


def _stack_gate_up(wgt, wut, tf):
    F, D = wgt.shape
    return jnp.stack([wgt.reshape(F // tf, tf, D), wut.reshape(F // tf, tf, D)], axis=1).reshape(2 * F, D)


def _unstack_gate_up(w, tf):
    F2, D = w.shape
    w4 = w.reshape(F2 // (2 * tf), 2, tf, D)
    return w4[:, 0].reshape(F2 // 2, D), w4[:, 1].reshape(F2 // 2, D)


def _row_groups(tm):
    return [slice(r * (tm // ROW_GROUPS), (r + 1) * (tm // ROW_GROUPS)) for r in range(ROW_GROUPS)]


def _ffn_fwd(x, gain, wgu, wd, after, name, tm, tf):
    T, D = x.shape
    F = wd.shape[0]
    nj = F // tf

    def body(x_ref, g_ref, wgu_ref, wd_ref, after_ref, xo_ref, xn_ref, gu_ref, xn_sc, acc_sc):
        j = pl.program_id(1)

        @pl.when(j == 0)
        def _():
            _, xhat = _rms_stats(x_ref[...])
            xn = (xhat * g_ref[...]).astype(BF16)
            xn_sc[...] = xn
            xn_ref[...] = xn
            acc_sc[...] = jnp.zeros_like(acc_sc)

        groups = _row_groups(tm)
        gus = [_dot_nt(xn_sc[rows, :], wgu_ref[...]) for rows in groups]
        hs = []
        for rows, gu in zip(groups, gus):
            gu_ref[rows, :] = gu.astype(BF16)
            g, u = gu[:, :tf], gu[:, tf:]
            hs.append((g * _sigmoid(g) * u).astype(BF16))
        for rows, h in zip(groups, hs):
            acc_sc[rows, :] += _dot(h, wd_ref[...])

        @pl.when(j == nj - 1)
        def _():
            xo_ref[...] = x_ref[...] + 0.5 * acc_sc[...]

    return pl.pallas_call(
        body, name=name, grid=(T // tm, nj),
        in_specs=[pl.BlockSpec((tm, D), lambda i, j: (i, 0)),
                  pl.BlockSpec((1, D), lambda i, j: (0, 0)),
                  pl.BlockSpec((2 * tf, D), lambda i, j: (j, 0)),
                  pl.BlockSpec((tf, D), lambda i, j: (j, 0)),
                  pl.BlockSpec(memory_space=pl.ANY)],
        out_specs=(pl.BlockSpec((tm, D), lambda i, j: (i, 0)),
                   pl.BlockSpec((tm, D), lambda i, j: (i, 0)),
                   pl.BlockSpec((tm, 2 * tf), lambda i, j: (i, j))),
        out_shape=(jax.ShapeDtypeStruct((T, D), F32), jax.ShapeDtypeStruct((T, D), BF16),
                   jax.ShapeDtypeStruct((T, 2 * F), BF16)),
        scratch_shapes=[pltpu.VMEM((tm, D), BF16), pltpu.VMEM((tm, D), F32)],
        compiler_params=_params("arbitrary", "arbitrary"),
    )(*_hbm(x, gain, wgu, wd, after))


def _ffn_bwd(dy, x, gain, gu, wgu, wd, after, name, tm, tf):
    T, D = x.shape
    F = wd.shape[0]
    nj = F // tf

    def body(dy_ref, x_ref, g_ref, gu_ref, wgu_ref, wd_ref, after_ref,
             dx_ref, dgain_ref, dgu_ref, hh_ref, do_ref, do_sc, acc_sc):
        i, j = pl.program_id(0), pl.program_id(1)

        @pl.when((i == 0) & (j == 0))
        def _():
            dgain_ref[...] = jnp.zeros_like(dgain_ref)

        @pl.when(j == 0)
        def _():
            do = (0.5 * dy_ref[...]).astype(BF16)
            do_sc[...] = do
            do_ref[...] = do
            acc_sc[...] = jnp.zeros_like(acc_sc)

        groups = _row_groups(tm)
        dhs = [_dot_nt(do_sc[rows, :], wd_ref[...]) for rows in groups]
        for rows, dh in zip(groups, dhs):
            g = gu_ref[rows, :tf].astype(F32)
            u = gu_ref[rows, tf:].astype(F32)
            sig = _sigmoid(g)
            s = g * sig
            dgu_ref[rows, :tf] = (dh * u * (sig + s * (1.0 - sig))).astype(BF16)
            dgu_ref[rows, tf:] = (dh * s).astype(BF16)
            hh_ref[rows, :] = (s * u).astype(BF16)
        for rows in groups:
            acc_sc[rows, :] += _dot(dgu_ref[rows, :], wgu_ref[...])

        @pl.when(j == nj - 1)
        def _():
            dx, dgain = _rms_bwd(acc_sc[...], x_ref[...], g_ref[...])
            dgain_ref[...] += dgain
            dx_ref[...] = dy_ref[...] + dx

    tile = pl.BlockSpec((tm, D), lambda i, j: (i, 0), pipeline_mode=pl.Buffered(1))
    return pl.pallas_call(
        body, name=name, grid=(T // tm, nj),
        in_specs=[tile, tile,
                  pl.BlockSpec((1, D), lambda i, j: (0, 0)),
                  pl.BlockSpec((tm, 2 * tf), lambda i, j: (i, j)),
                  pl.BlockSpec((2 * tf, D), lambda i, j: (j, 0)),
                  pl.BlockSpec((tf, D), lambda i, j: (j, 0)),
                  pl.BlockSpec(memory_space=pl.ANY)],
        out_specs=(tile,
                   pl.BlockSpec((1, D), lambda i, j: (0, 0)),
                   pl.BlockSpec((tm, 2 * tf), lambda i, j: (i, j)),
                   pl.BlockSpec((tm, tf), lambda i, j: (i, j)),
                   pl.BlockSpec((tm, D), lambda i, j: (i, 0))),
        out_shape=(jax.ShapeDtypeStruct((T, D), F32), jax.ShapeDtypeStruct((1, D), F32),
                   jax.ShapeDtypeStruct((T, 2 * F), BF16), jax.ShapeDtypeStruct((T, F), BF16),
                   jax.ShapeDtypeStruct((T, D), BF16)),
        scratch_shapes=[pltpu.VMEM((tm, D), BF16), pltpu.VMEM((tm, D), F32)],
        compiler_params=_params("arbitrary", "arbitrary"),
    )(*_hbm(dy, x, gain, gu, wgu, wd, after))


def _tn_grad(a, b, name, bm, tk):
    T, M = a.shape
    D = b.shape[1]
    nk = T // tk

    def body(a_ref, b_ref, o_ref, acc_sc):
        k = pl.program_id(1)

        @pl.when(k == 0)
        def _():
            acc_sc[...] = jnp.zeros_like(acc_sc)

        acc_sc[...] += _dot_tn(a_ref[...], b_ref[...])

        @pl.when(k == nk - 1)
        def _():
            o_ref[...] = acc_sc[...].astype(BF16)

    return pl.pallas_call(
        body, name=name, grid=(M // bm, nk),
        in_specs=[pl.BlockSpec((tk, bm), lambda i, k: (k, i)), pl.BlockSpec((tk, D), lambda i, k: (k, 0))],
        out_specs=pl.BlockSpec((bm, D), lambda i, k: (i, 0)),
        out_shape=jax.ShapeDtypeStruct((M, D), BF16),
        scratch_shapes=[pltpu.VMEM((bm, D), F32)],
        compiler_params=_params("arbitrary", "arbitrary"),
    )(*_hbm(a, b))


_Z_SPLITS = (0, 512, 640, 768, 1280, 1792, 2304)


def _mixin_fwd(x, gain, w_in_t, name, tm):
    T, D = x.shape
    widths = [b - a for a, b in zip(_Z_SPLITS[:-1], _Z_SPLITS[1:])]

    def body(x_ref, g_ref, w_ref, hn_ref, *outs):
        _, xhat = _rms_stats(x_ref[...])
        hn = (xhat * g_ref[...]).astype(BF16)
        hn_ref[...] = hn
        for o_ref, lo, hi in zip(outs, _Z_SPLITS[:-1], _Z_SPLITS[1:]):
            o_ref[...] = _dot_nt(hn, w_ref[lo:hi, :]).astype(BF16)

    return pl.pallas_call(
        body, name=name, grid=(T // tm,),
        in_specs=[pl.BlockSpec((tm, D), lambda i: (i, 0)),
                  pl.BlockSpec((1, D), lambda i: (0, 0)),
                  pl.BlockSpec(w_in_t.shape, lambda i: (0, 0))],
        out_specs=tuple([pl.BlockSpec((tm, D), lambda i: (i, 0))]
                        + [pl.BlockSpec((tm, w), lambda i: (i, 0)) for w in widths]),
        out_shape=tuple([jax.ShapeDtypeStruct((T, D), BF16)]
                        + [jax.ShapeDtypeStruct((T, w), BF16) for w in widths]),
        compiler_params=_params("arbitrary"),
    )(*_hbm(x, gain, w_in_t))


def _bucket_table():
    qi = np.arange(BLOCK, dtype=np.int32)[:, None]
    kj = np.arange(2 * BLOCK, dtype=np.int32)[None, :]
    dist = qi + BLOCK - kj
    n = np.maximum(dist, 0)
    max_exact = NUM_BUCKETS // 2
    large = max_exact + (np.log(np.maximum(n, 1).astype(np.float32) / max_exact)
                         / math.log(MAX_DISTANCE / max_exact)
                         * (NUM_BUCKETS - max_exact)).astype(np.int32)
    large = np.minimum(large, NUM_BUCKETS - 1)
    bucket = np.where(n < max_exact, n, large).astype(np.int32)
    valid = (dist >= 0) & (dist < WINDOW)
    return np.where(valid, bucket, -1).astype(np.int32)


def _bias_build(table, bucket, name):
    def body(t_ref, b_ref, o_ref):
        bk = b_ref[...]
        for h in range(N_Q_HEADS):
            def step(b, acc):
                return jnp.where(bk == b, t_ref[b, h], acc)
            o_ref[h] = lax.fori_loop(0, NUM_BUCKETS, step, jnp.full(bk.shape, MASKED, F32))

    return pl.pallas_call(
        body, name=name,
        in_specs=[pl.BlockSpec(memory_space=pltpu.SMEM), pl.BlockSpec(memory_space=pltpu.VMEM)],
        out_specs=pl.BlockSpec(memory_space=pltpu.VMEM),
        out_shape=jax.ShapeDtypeStruct((N_Q_HEADS,) + bucket.shape, F32),
    )(table, bucket)


def _bias_grad(dbias, bucket, name):
    def body(d_ref, b_ref, o_ref):
        bk = b_ref[...]
        row = lax.broadcasted_iota(jnp.int32, o_ref.shape, 0)
        lane = lax.broadcasted_iota(jnp.int32, o_ref.shape, 1)
        res = jnp.zeros(o_ref.shape, F32)
        for h in range(N_Q_HEADS):
            d = d_ref[h]

            def step(b, acc):
                tot = jnp.sum(jnp.where(bk == b, d, 0.0), axis=1, keepdims=True)
                tot = jnp.sum(tot, axis=0, keepdims=True)
                return jnp.where((row == b) & (lane == h), tot, acc)
            res = lax.fori_loop(0, NUM_BUCKETS, step, res)
        o_ref[...] = res

    return pl.pallas_call(
        body, name=name,
        in_specs=[pl.BlockSpec(memory_space=pltpu.VMEM), pl.BlockSpec(memory_space=pltpu.VMEM)],
        out_specs=pl.BlockSpec(memory_space=pltpu.VMEM),
        out_shape=jax.ShapeDtypeStruct((NUM_BUCKETS, 128), F32),
    )(*_hbm(dbias, bucket))


def _head_cols(h):
    return slice(h * HEAD_DIM, (h + 1) * HEAD_DIM)


def _stack_heads(ref, r0, g, dtype):
    return jnp.concatenate(
        [ref[pl.ds(r0, BLOCK), _head_cols(GQA_GROUP * g + j)].astype(dtype) for j in range(GQA_GROUP)], axis=0)


def _unstack_heads(ref, r0, g, val):
    for j in range(GQA_GROUP):
        ref[pl.ds(r0, BLOCK), _head_cols(GQA_GROUP * g + j)] = val[j * BLOCK:(j + 1) * BLOCK, :]


def _head_lanes(h):
    return slice(h * BLOCK, (h + 1) * BLOCK)


def _group_lanes(g):
    return slice(g * GROUP_ROWS, (g + 1) * GROUP_ROWS)


def _head_softmax(st, bias_t, sink, no_prev):
    s = st * SCALE + bias_t
    row = lax.broadcasted_iota(jnp.int32, s.shape, 0)
    s = jnp.where(no_prev & (row < BLOCK), MASKED, s)
    m = jnp.maximum(jnp.max(s, axis=0, keepdims=True), sink)
    p = jnp.exp(s - m)
    ps = jnp.exp(sink - m)
    r = 1.0 / (jnp.sum(p, axis=0, keepdims=True) + ps)
    return p * r, ps * r


def _load_band(kf_sc, vf_sc, kp_ref, kc_ref, vp_ref, vc_ref, tq):
    kf_sc[0:BLOCK, :] = kp_ref[...]
    kf_sc[BLOCK:BLOCK + tq, :] = kc_ref[...]
    vf_sc[0:BLOCK, :] = vp_ref[...]
    vf_sc[BLOCK:BLOCK + tq, :] = vc_ref[...]


def _attn_fwd(q, k, v, bias_t, sinks, gain, name, nblk):
    T = q.shape[0]
    tq = nblk * BLOCK

    def body(sink_ref, q_ref, kc_ref, kp_ref, vc_ref, vp_ref, bias_ref, g_ref, raw_ref, nrm_ref,
             kf_sc, vf_sc, o_sc, st_sc, pt_sc):
        i = pl.program_id(0)
        _load_band(kf_sc, vf_sc, kp_ref, kc_ref, vp_ref, vc_ref, tq)

        def block(b, carry):
            r0 = pl.multiple_of(b * BLOCK, BLOCK)
            no_prev = (i == 0) & (b == 0)
            for g in range(N_KV_HEADS):
                kb = kf_sc[pl.ds(r0, 2 * BLOCK), _head_cols(g)]
                st_sc[:, _group_lanes(g)] = _dot_nt(kb, _stack_heads(q_ref, r0, g, BF16))
            for h in range(N_Q_HEADS):
                p, _ = _head_softmax(st_sc[:, _head_lanes(h)], bias_ref[h], sink_ref[h], no_prev)
                pt_sc[:, _head_lanes(h)] = p.astype(BF16)
            for g in range(N_KV_HEADS):
                vb = vf_sc[pl.ds(r0, 2 * BLOCK), _head_cols(g)]
                _unstack_heads(o_sc, r0, g, _dot_tn(pt_sc[:, _group_lanes(g)], vb))
            return carry

        lax.fori_loop(0, nblk, block, 0)
        o = o_sc[...]
        raw_ref[...] = o.astype(BF16)
        _, ohat = _rms_stats(o)
        nrm_ref[...] = (ohat * g_ref[...]).astype(BF16)

    cur = lambda i: (i, 0)
    prev = lambda i: (jnp.maximum(i * nblk - 1, 0), 0)
    lanes = N_Q_HEADS * BLOCK
    return pl.pallas_call(
        body, name=name, grid=(T // tq,),
        in_specs=[pl.BlockSpec(memory_space=pltpu.SMEM),
                  pl.BlockSpec((tq, ATTN_WIDTH), cur),
                  pl.BlockSpec((tq, KV_WIDTH), cur), pl.BlockSpec((BLOCK, KV_WIDTH), prev),
                  pl.BlockSpec((tq, KV_WIDTH), cur), pl.BlockSpec((BLOCK, KV_WIDTH), prev),
                  pl.BlockSpec(bias_t.shape, lambda i: (0, 0, 0)),
                  pl.BlockSpec((1, ATTN_WIDTH), lambda i: (0, 0))],
        out_specs=(pl.BlockSpec((tq, ATTN_WIDTH), cur), pl.BlockSpec((tq, ATTN_WIDTH), cur)),
        out_shape=(jax.ShapeDtypeStruct((T, ATTN_WIDTH), BF16), jax.ShapeDtypeStruct((T, ATTN_WIDTH), BF16)),
        scratch_shapes=[pltpu.VMEM((tq + BLOCK, KV_WIDTH), BF16), pltpu.VMEM((tq + BLOCK, KV_WIDTH), BF16),
                        pltpu.VMEM((tq, ATTN_WIDTH), F32),
                        pltpu.VMEM((2 * BLOCK, lanes), F32), pltpu.VMEM((2 * BLOCK, lanes), BF16)],
        compiler_params=_params("arbitrary"),
    )(sinks, *_hbm(q, k, k, v, v, bias_t, gain))


def _attn_bwd(dmixed, raw, q, k, v, bias_t, sinks, gain, name, nblk):
    T = q.shape[0]
    tq = nblk * BLOCK
    nt = T // tq
    lanes = N_Q_HEADS * BLOCK

    def body(sink_ref, dm_ref, raw_ref, q_ref, kc_ref, kp_ref, vc_ref, vp_ref, bias_ref, g_ref,
             dq_ref, dk_ref, dv_ref, dbias_ref, dsink_ref, dgain_ref,
             do_sc, dq_sc, kf_sc, vf_sc, dkf_sc, dvf_sc, st_sc, dpt_sc, pt_sc, dst_sc, drow_sc,
             qs_sc, dos_sc, dsink_sc):
        i = pl.program_id(0)
        tile = nt - 1 - i

        @pl.when(i == 0)
        def _():
            dkf_sc[...] = jnp.zeros_like(dkf_sc)
            dvf_sc[...] = jnp.zeros_like(dvf_sc)
            dsink_sc[...] = jnp.zeros_like(dsink_sc)
            dbias_ref[...] = jnp.zeros_like(dbias_ref)
            dgain_ref[...] = jnp.zeros_like(dgain_ref)

        carry_k = dkf_sc[0:BLOCK, :]
        carry_v = dvf_sc[0:BLOCK, :]
        dkf_sc[0:tq, :] = jnp.zeros((tq, KV_WIDTH), F32)
        dvf_sc[0:tq, :] = jnp.zeros((tq, KV_WIDTH), F32)
        dkf_sc[tq:tq + BLOCK, :] = carry_k
        dvf_sc[tq:tq + BLOCK, :] = carry_v
        _load_band(kf_sc, vf_sc, kp_ref, kc_ref, vp_ref, vc_ref, tq)

        do, dgain = _rms_bwd(dm_ref[...].astype(F32), raw_ref[...].astype(F32), g_ref[...])
        dgain_ref[...] += dgain
        do_sc[...] = do
        ones = jnp.ones((SUBLANES, HEAD_DIM), BF16)

        def block(b, carry):
            r0 = pl.multiple_of(b * BLOCK, BLOCK)
            no_prev = (tile == 0) & (b == 0)
            for g in range(N_KV_HEADS):
                kb = kf_sc[pl.ds(r0, 2 * BLOCK), _head_cols(g)]
                vb = vf_sc[pl.ds(r0, 2 * BLOCK), _head_cols(g)]
                qg = _stack_heads(q_ref, r0, g, BF16)
                dog = _stack_heads(do_sc, r0, g, F32)
                prod = dog * _stack_heads(raw_ref, r0, g, F32)
                hi = prod.astype(BF16)
                lo = (prod - hi.astype(F32)).astype(BF16)
                drow_sc[:, _group_lanes(g)] = _dot_nt(ones, hi) + _dot_nt(ones, lo)
                dogb = dog.astype(BF16)
                qs_sc[g] = qg
                dos_sc[g] = dogb
                st_sc[:, _group_lanes(g)] = _dot_nt(kb, qg)
                dpt_sc[:, _group_lanes(g)] = _dot_nt(vb, dogb)
            for h in range(N_Q_HEADS):
                hl = _head_lanes(h)
                p, ps = _head_softmax(st_sc[:, hl], bias_ref[h], sink_ref[h], no_prev)
                rowdot = drow_sc[0:1, hl]
                ds = p * (dpt_sc[:, hl] - rowdot)
                dsink_sc[h:h + 1, :] += -(ps * rowdot)
                dbias_ref[h] += ds
                dst_sc[:, hl] = ds.astype(BF16)
                pt_sc[:, hl] = p.astype(BF16)
            for g in range(N_KV_HEADS):
                kb = kf_sc[pl.ds(r0, 2 * BLOCK), _head_cols(g)]
                dsg = dst_sc[:, _group_lanes(g)]
                _unstack_heads(dq_sc, r0, g, _dot_tn(dsg, kb) * SCALE)
                dkf_sc[pl.ds(r0, 2 * BLOCK), _head_cols(g)] += _dot(dsg, qs_sc[g]) * SCALE
                dvf_sc[pl.ds(r0, 2 * BLOCK), _head_cols(g)] += _dot(pt_sc[:, _group_lanes(g)], dos_sc[g])
            return carry

        lax.fori_loop(0, nblk, block, 0)
        dq_ref[...] = dq_sc[...].astype(BF16)
        dk_ref[...] = dkf_sc[BLOCK:BLOCK + tq, :].astype(BF16)
        dv_ref[...] = dvf_sc[BLOCK:BLOCK + tq, :].astype(BF16)

        @pl.when(i == nt - 1)
        def _():
            tot = jnp.sum(dsink_sc[...], axis=1, keepdims=True)
            dsink_ref[...] = jnp.broadcast_to(tot, dsink_ref.shape)

    cur = lambda i: (nt - 1 - i, 0)
    prev = lambda i: (jnp.maximum((nt - 1 - i) * nblk - 1, 0), 0)
    const2 = lambda i: (0, 0)
    const3 = lambda i: (0, 0, 0)
    return pl.pallas_call(
        body, name=name, grid=(nt,),
        in_specs=[pl.BlockSpec(memory_space=pltpu.SMEM),
                  pl.BlockSpec((tq, ATTN_WIDTH), cur),
                  pl.BlockSpec((tq, ATTN_WIDTH), cur),
                  pl.BlockSpec((tq, ATTN_WIDTH), cur),
                  pl.BlockSpec((tq, KV_WIDTH), cur), pl.BlockSpec((BLOCK, KV_WIDTH), prev),
                  pl.BlockSpec((tq, KV_WIDTH), cur), pl.BlockSpec((BLOCK, KV_WIDTH), prev),
                  pl.BlockSpec(bias_t.shape, const3),
                  pl.BlockSpec((1, ATTN_WIDTH), const2)],
        out_specs=(pl.BlockSpec((tq, ATTN_WIDTH), cur),
                   pl.BlockSpec((tq, KV_WIDTH), cur), pl.BlockSpec((tq, KV_WIDTH), cur),
                   pl.BlockSpec(bias_t.shape, const3),
                   pl.BlockSpec((N_Q_HEADS, 128), const2),
                   pl.BlockSpec((1, ATTN_WIDTH), const2)),
        out_shape=(jax.ShapeDtypeStruct((T, ATTN_WIDTH), BF16),
                   jax.ShapeDtypeStruct((T, KV_WIDTH), BF16), jax.ShapeDtypeStruct((T, KV_WIDTH), BF16),
                   jax.ShapeDtypeStruct(bias_t.shape, F32),
                   jax.ShapeDtypeStruct((N_Q_HEADS, 128), F32),
                   jax.ShapeDtypeStruct((1, ATTN_WIDTH), F32)),
        scratch_shapes=[pltpu.VMEM((tq, ATTN_WIDTH), F32), pltpu.VMEM((tq, ATTN_WIDTH), F32),
                        pltpu.VMEM((tq + BLOCK, KV_WIDTH), BF16), pltpu.VMEM((tq + BLOCK, KV_WIDTH), BF16),
                        pltpu.VMEM((tq + BLOCK, KV_WIDTH), F32), pltpu.VMEM((tq + BLOCK, KV_WIDTH), F32),
                        pltpu.VMEM((2 * BLOCK, lanes), F32), pltpu.VMEM((2 * BLOCK, lanes), F32),
                        pltpu.VMEM((2 * BLOCK, lanes), BF16), pltpu.VMEM((2 * BLOCK, lanes), BF16),
                        pltpu.VMEM((SUBLANES, lanes), F32),
                        pltpu.VMEM((N_KV_HEADS, GROUP_ROWS, HEAD_DIM), BF16),
                        pltpu.VMEM((N_KV_HEADS, GROUP_ROWS, HEAD_DIM), BF16),
                        pltpu.VMEM((N_Q_HEADS, 128), F32)],
        compiler_params=_params("arbitrary"),
    )(sinks, *_hbm(dmixed, raw, q, k, k, v, v, bias_t, gain))


def _shift_down(cu, tail):
    row = lax.broadcasted_iota(jnp.int32, cu.shape, 0)
    t6, t7 = tail[6:7, :], tail[7:8, :]
    s1 = jnp.where(row == 0, t7, pltpu.roll(cu, 1, 0))
    s2 = jnp.where(row == 0, t6, jnp.where(row == 1, t7, pltpu.roll(cu, 2, 0)))
    return s1, s2


def _shift_up(d, head):
    n = d.shape[0]
    row = lax.broadcasted_iota(jnp.int32, d.shape, 0)
    h0, h1 = head[0:1, :], head[1:2, :]
    s1 = jnp.where(row == n - 1, h0, pltpu.roll(d, n - 1, 0))
    s2 = jnp.where(row == n - 1, h1, jnp.where(row == n - 2, h0, pltpu.roll(d, n - 2, 0)))
    return s1, s2


def _mixout_fwd(x, attn_n, u, gb, gc, conv_w, gain, w_out, name, tm):
    T, D = x.shape

    def body(x_ref, an_ref, u_ref, b_ref, c_ref, cw_ref, g_ref, wo_ref, xo_ref, cn_ref, tail_sc):
        @pl.when(pl.program_id(0) == 0)
        def _():
            tail_sc[...] = jnp.zeros_like(tail_sc)

        cu = c_ref[...].astype(F32) * u_ref[...].astype(F32)
        s1, s2 = _shift_down(cu, tail_sc[...])
        tail_sc[...] = cu[tm - SUBLANES:tm, :]
        pre = cw_ref[0:1, :] * s2 + cw_ref[1:2, :] * s1 + cw_ref[2:3, :] * cu
        conv = b_ref[...].astype(F32) * pre
        _, chat = _rms_stats(conv)
        cn = (chat * g_ref[...]).astype(BF16)
        cn_ref[...] = cn
        xo_ref[...] = (x_ref[...] + _dot(an_ref[...], wo_ref[0:ATTN_WIDTH, :])
                       + _dot(cn, wo_ref[ATTN_WIDTH:ATTN_WIDTH + CONV_DIM, :]))

    row = lambda i: (i, 0)
    const = lambda i: (0, 0)
    return pl.pallas_call(
        body, name=name, grid=(T // tm,),
        in_specs=[pl.BlockSpec((tm, D), row), pl.BlockSpec((tm, ATTN_WIDTH), row),
                  pl.BlockSpec((tm, CONV_DIM), row), pl.BlockSpec((tm, CONV_DIM), row),
                  pl.BlockSpec((tm, CONV_DIM), row),
                  pl.BlockSpec(conv_w.shape, const), pl.BlockSpec((1, CONV_DIM), const),
                  pl.BlockSpec(w_out.shape, const)],
        out_specs=(pl.BlockSpec((tm, D), row), pl.BlockSpec((tm, CONV_DIM), row)),
        out_shape=(jax.ShapeDtypeStruct((T, D), F32), jax.ShapeDtypeStruct((T, CONV_DIM), BF16)),
        scratch_shapes=[pltpu.VMEM((SUBLANES, CONV_DIM), F32)],
        compiler_params=_params("arbitrary"),
    )(*_hbm(x, attn_n, u, gb, gc, conv_w, gain, w_out))


def _mixout_bwd(dy, attn_n, conv_n, w_out, after, name, tm):
    T, D = dy.shape
    W = ATTN_WIDTH + CONV_DIM
    nt = T // tm

    def body(dy_ref, an_ref, cn_ref, w_ref, after_ref, dm_ref, dw_ref, dw_sc):
        i = pl.program_id(0)

        @pl.when(i == 0)
        def _():
            dw_sc[...] = jnp.zeros_like(dw_sc)

        dyb = dy_ref[...].astype(BF16)
        dm_ref[...] = _dot_nt(dyb, w_ref[...]).astype(BF16)
        dw_sc[0:ATTN_WIDTH, :] += _dot_tn(an_ref[...], dyb)
        dw_sc[ATTN_WIDTH:W, :] += _dot_tn(cn_ref[...], dyb)

        @pl.when(i == nt - 1)
        def _():
            dw_ref[...] = dw_sc[...].astype(BF16)

    row = lambda i: (i, 0)
    const = lambda i: (0, 0)
    return pl.pallas_call(
        body, name=name, grid=(nt,),
        in_specs=[pl.BlockSpec((tm, D), row), pl.BlockSpec((tm, ATTN_WIDTH), row),
                  pl.BlockSpec((tm, CONV_DIM), row), pl.BlockSpec(w_out.shape, const),
                  pl.BlockSpec(memory_space=pl.ANY)],
        out_specs=(pl.BlockSpec((tm, W), row), pl.BlockSpec((W, D), const)),
        out_shape=(jax.ShapeDtypeStruct((T, W), BF16), jax.ShapeDtypeStruct((W, D), BF16)),
        scratch_shapes=[pltpu.VMEM((W, D), F32)],
        compiler_params=_params("arbitrary"),
    )(*_hbm(dy, attn_n, conv_n, w_out, after))


def _conv_bwd(dmixed, u, gb, gc, conv_w, gain, name, tc):
    T = u.shape[0]
    nt = T // tc
    per8 = tc // SUBLANES

    def body(dm_ref, u_ref, b_ref, c_ref, ut_ref, ct_ref, cw_ref, g_ref,
             du_ref, db_ref, dc_ref, dcw_ref, dgain_ref, head_sc):
        i = pl.program_id(0)

        @pl.when(i == 0)
        def _():
            head_sc[...] = jnp.zeros_like(head_sc)
            dcw_ref[...] = jnp.zeros_like(dcw_ref)
            dgain_ref[...] = jnp.zeros_like(dgain_ref)

        uu = u_ref[...].astype(F32)
        cc = c_ref[...].astype(F32)
        bb = b_ref[...].astype(F32)
        cu = cc * uu
        tail = jnp.where(i == nt - 1, 0.0, ct_ref[...].astype(F32) * ut_ref[...].astype(F32))
        s1, s2 = _shift_down(cu, tail)
        w0, w1, w2 = cw_ref[0:1, :], cw_ref[1:2, :], cw_ref[2:3, :]
        pre = w0 * s2 + w1 * s1 + w2 * cu
        dconv, dgain = _rms_bwd(dm_ref[...].astype(F32), bb * pre, g_ref[...])
        dgain_ref[...] += dgain
        db_ref[...] = (dconv * pre).astype(BF16)
        dpre = dconv * bb
        dcw_ref[0:1, :] += jnp.sum(dpre * s2, axis=0, keepdims=True)
        dcw_ref[1:2, :] += jnp.sum(dpre * s1, axis=0, keepdims=True)
        dcw_ref[2:3, :] += jnp.sum(dpre * cu, axis=0, keepdims=True)
        n1, n2 = _shift_up(dpre, head_sc[...])
        head_sc[...] = dpre[0:SUBLANES, :]
        dcu = w2 * dpre + w1 * n1 + w0 * n2
        du_ref[...] = (dcu * cc).astype(BF16)
        dc_ref[...] = (dcu * uu).astype(BF16)

    rev = lambda i: (nt - 1 - i, 0)
    rev_right = lambda i: (nt - 1 - i, 1)
    tail_map = lambda i: (jnp.maximum((nt - 1 - i) * per8 - 1, 0), 0)
    const = lambda i: (0, 0)
    return pl.pallas_call(
        body, name=name, grid=(nt,),
        in_specs=[pl.BlockSpec((tc, CONV_DIM), rev_right),
                  pl.BlockSpec((tc, CONV_DIM), rev), pl.BlockSpec((tc, CONV_DIM), rev),
                  pl.BlockSpec((tc, CONV_DIM), rev),
                  pl.BlockSpec((SUBLANES, CONV_DIM), tail_map), pl.BlockSpec((SUBLANES, CONV_DIM), tail_map),
                  pl.BlockSpec(conv_w.shape, const), pl.BlockSpec((1, CONV_DIM), const)],
        out_specs=(pl.BlockSpec((tc, CONV_DIM), rev), pl.BlockSpec((tc, CONV_DIM), rev),
                   pl.BlockSpec((tc, CONV_DIM), rev),
                   pl.BlockSpec((SUBLANES, CONV_DIM), const), pl.BlockSpec((1, CONV_DIM), const)),
        out_shape=(jax.ShapeDtypeStruct((T, CONV_DIM), BF16), jax.ShapeDtypeStruct((T, CONV_DIM), BF16),
                   jax.ShapeDtypeStruct((T, CONV_DIM), BF16),
                   jax.ShapeDtypeStruct((SUBLANES, CONV_DIM), F32), jax.ShapeDtypeStruct((1, CONV_DIM), F32)),
        scratch_shapes=[pltpu.VMEM((SUBLANES, CONV_DIM), F32)],
        compiler_params=_params("arbitrary"),
    )(*_hbm(dmixed, u, gb, gc, u, gc, conv_w, gain))


def _mixin_bwd(dy, x, gain, dz, w_in_t, name, tm):
    T, D = x.shape
    nz = len(dz)

    def body(dy_ref, x_ref, g_ref, *rest):
        dz_refs, wt_ref, dx_ref, dgain_ref = rest[:nz], rest[nz], rest[nz + 1], rest[nz + 2]

        @pl.when(pl.program_id(0) == 0)
        def _():
            dgain_ref[...] = jnp.zeros_like(dgain_ref)

        dh = jnp.zeros((tm, D), F32)
        for r, lo, hi in zip(dz_refs, _Z_SPLITS[:-1], _Z_SPLITS[1:]):
            dh += _dot(r[...], wt_ref[lo:hi, :])
        dx, dgain = _rms_bwd(dh, x_ref[...], g_ref[...])
        dgain_ref[...] += dgain
        dx_ref[...] = dy_ref[...] + dx

    row = lambda i: (i, 0)
    const = lambda i: (0, 0)
    return pl.pallas_call(
        body, name=name, grid=(T // tm,),
        in_specs=[pl.BlockSpec((tm, D), row), pl.BlockSpec((tm, D), row), pl.BlockSpec((1, D), const)]
                 + [pl.BlockSpec((tm, a.shape[1]), row) for a in dz]
                 + [pl.BlockSpec(w_in_t.shape, const)],
        out_specs=(pl.BlockSpec((tm, D), row), pl.BlockSpec((1, D), const)),
        out_shape=(jax.ShapeDtypeStruct((T, D), F32), jax.ShapeDtypeStruct((1, D), F32)),
        compiler_params=_params("arbitrary"),
    )(*_hbm(dy, x, gain, *dz, w_in_t))


def _win_grad(dz, hn, name, tk):
    T, D = hn.shape
    nz = len(dz)
    nt = T // tk
    W = _Z_SPLITS[-1]

    def body(hn_ref, *rest):
        dz_refs, dw_ref, dw_sc = rest[:nz], rest[nz], rest[nz + 1]
        i = pl.program_id(0)

        @pl.when(i == 0)
        def _():
            dw_sc[...] = jnp.zeros_like(dw_sc)

        hn = hn_ref[...]
        for r, lo, hi in zip(dz_refs, _Z_SPLITS[:-1], _Z_SPLITS[1:]):
            dw_sc[lo:hi, :] += _dot_tn(r[...], hn)

        @pl.when(i == nt - 1)
        def _():
            dw_ref[...] = dw_sc[...].astype(BF16)

    row = lambda i: (i, 0)
    return pl.pallas_call(
        body, name=name, grid=(nt,),
        in_specs=[pl.BlockSpec((tk, D), row)] + [pl.BlockSpec((tk, a.shape[1]), row) for a in dz],
        out_specs=pl.BlockSpec((W, D), lambda i: (0, 0)),
        out_shape=jax.ShapeDtypeStruct((W, D), BF16),
        scratch_shapes=[pltpu.VMEM((W, D), F32)],
        compiler_params=_params("arbitrary"),
    )(*_hbm(hn, *dz))


def _loss_head(x, target, gain, name, tm):
    T, D = x.shape

    def body(x_ref, t_ref, g_ref, dx_ref, dgain_ref, loss_ref):
        @pl.when(pl.program_id(0) == 0)
        def _():
            dgain_ref[...] = jnp.zeros_like(dgain_ref)
            loss_ref[...] = jnp.zeros_like(loss_ref)

        xv = x_ref[...]
        gain_v = g_ref[...]
        _, xhat = _rms_stats(xv)
        err = xhat * gain_v - t_ref[...]
        part = 0.5 * jnp.sum(jnp.mean(err * err, axis=-1, keepdims=True), axis=0, keepdims=True)
        loss_ref[...] += part
        dx, dgain = _rms_bwd(err * (1.0 / D), xv, gain_v)
        dgain_ref[...] += dgain
        dx_ref[...] = dx

    row = lambda i: (i, 0)
    const = lambda i: (0, 0)
    return pl.pallas_call(
        body, name=name, grid=(T // tm,),
        in_specs=[pl.BlockSpec((tm, D), row), pl.BlockSpec((tm, D), row), pl.BlockSpec((1, D), const)],
        out_specs=(pl.BlockSpec((tm, D), row), pl.BlockSpec((1, D), const),
                   pl.BlockSpec((SUBLANES, 128), const)),
        out_shape=(jax.ShapeDtypeStruct((T, D), F32), jax.ShapeDtypeStruct((1, D), F32),
                   jax.ShapeDtypeStruct((SUBLANES, 128), F32)),
        compiler_params=_params("arbitrary"),
    )(*_hbm(x, target, gain))


def _adamw(parts, w, m, v, name, tr):
    P = parts.shape[0]
    R, C = w.shape

    def body(p_ref, w_ref, m_ref, v_ref, g_ref, d_ref, nm_ref, nv_ref):
        g = p_ref[0].astype(F32)
        for d in range(1, P):
            g = g + p_ref[d].astype(F32)
        nm = ADAM_B1 * m_ref[...] + (1.0 - ADAM_B1) * g
        nv = ADAM_B2 * v_ref[...] + (1.0 - ADAM_B2) * (g * g)
        m_hat = nm / (1.0 - ADAM_B1 ** ADAM_STEP)
        v_hat = nv / (1.0 - ADAM_B2 ** ADAM_STEP)
        g_ref[...] = g
        nm_ref[...] = nm
        nv_ref[...] = nv
        d_ref[...] = -ADAM_LR * (m_hat / (jnp.sqrt(v_hat) + ADAM_EPS) + ADAM_WD * w_ref[...])

    row = lambda i: (i, 0)
    spec = pl.BlockSpec((tr, C), row)
    shp = jax.ShapeDtypeStruct((R, C), F32)
    return pl.pallas_call(
        body, name=name, grid=(R // tr,),
        in_specs=[pl.BlockSpec((P, tr, C), lambda i: (0, i, 0)), spec, spec, spec],
        out_specs=(spec, spec, spec, spec),
        out_shape=(shp, shp, shp, shp),
        compiler_params=_params("arbitrary"),
    )(*_hbm(parts, w, m, v))


def _columns_of_blocks(g):
    n, R, w = g.shape
    return g.transpose(1, 0, 2).reshape(R, n * w)


def _pad_row(vec):
    vec = vec.reshape(1, -1)
    return jnp.pad(vec, ((0, 0), (0, PACK_COLS - vec.shape[1])))


def kernel(x, rel_bias_table, ffn1_norm, ffn1_w_gate, ffn1_w_up, ffn1_w_down, mix_norm, w_in, conv_w, attn_sinks, attn_out_norm, conv_out_norm, w_out, ffn2_norm, ffn2_w_gate, ffn2_w_up, ffn2_w_down, final_norm, loss_target, m_rel_bias_table, m_ffn1_norm, m_ffn1_w_gate, m_ffn1_w_up, m_ffn1_w_down, m_mix_norm, m_w_in, m_conv_w, m_attn_sinks, m_attn_out_norm, m_conv_out_norm, m_w_out, m_ffn2_norm, m_ffn2_w_gate, m_ffn2_w_up, m_ffn2_w_down, m_final_norm, v_rel_bias_table, v_ffn1_norm, v_ffn1_w_gate, v_ffn1_w_up, v_ffn1_w_down, v_mix_norm, v_w_in, v_conv_w, v_attn_sinks, v_attn_out_norm, v_conv_out_norm, v_w_out, v_ffn2_norm, v_ffn2_w_gate, v_ffn2_w_up, v_ffn2_w_down, v_final_norm):
    T, D = x.shape[1], x.shape[2]
    x0 = x[0]
    target = loss_target[0]
    tm = min(TM_FFN, T)
    tm_bwd = min(TM_FFN_BWD, T)
    tm_mix = min(TM_MIX, T)
    tk = min(TK_WGRAD, T)
    tf = TF_FFN
    nblk = min(ATTN_BLOCKS, T // BLOCK)
    me = 4 * lax.axis_index("x") + 2 * lax.axis_index("y") + lax.axis_index("c")

    big = {
        "ffn1_w_gate": (ffn1_w_gate[0], m_ffn1_w_gate[0], v_ffn1_w_gate[0], True),
        "ffn1_w_up": (ffn1_w_up[0], m_ffn1_w_up[0], v_ffn1_w_up[0], True),
        "ffn1_w_down": (ffn1_w_down[0], m_ffn1_w_down[0], v_ffn1_w_down[0], False),
        "w_in": (w_in[0], m_w_in[0], v_w_in[0], True),
        "w_out": (w_out[0], m_w_out[0], v_w_out[0], False),
        "ffn2_w_gate": (ffn2_w_gate[0], m_ffn2_w_gate[0], v_ffn2_w_gate[0], True),
        "ffn2_w_up": (ffn2_w_up[0], m_ffn2_w_up[0], v_ffn2_w_up[0], True),
        "ffn2_w_down": (ffn2_w_down[0], m_ffn2_w_down[0], v_ffn2_w_down[0], False),
    }

    def block_to_send(name):
        w, _, _, transposed = big[name]
        return (w.T if transposed else w).astype(BF16)

    names1 = ["ffn1_w_gate", "ffn1_w_up", "ffn1_w_down"]
    names_rest = ["w_in", "w_out", "ffn2_w_gate", "ffn2_w_up", "ffn2_w_down"]
    first = _gather_two_level([block_to_send(n) for n in names1], "gather_ffn1", ffn1_norm)
    wgt1, wut1, wd1 = [g.reshape(-1, D) for g in first]
    h_rest, token = _exchange_start([block_to_send(n) for n in names_rest[:2]] + [conv_w[0]]
                                    + [block_to_send(n) for n in names_rest[2:]], False,
                                    "gather_start_rest", wd1)

    wgu1 = _stack_gate_up(wgt1, wut1, tf)
    x1, xn1, gu1 = _ffn_fwd(x0, ffn1_norm, wgu1, wd1, token, "ffn1_fwd", tm, tf)
    mixw = _exchange_wait(h_rest[:3], False, "gather_wait_mix", x1)
    win_t = mixw[0].reshape(-1, D)
    wout = mixw[1].reshape(-1, D)
    cw = _columns_of_blocks(mixw[2])
    hn, q, k, v, u, gb, gc = _mixin_fwd(x1, mix_norm, win_t, "mixin_fwd", tm_mix)
    bucket = jnp.asarray(_bucket_table().T.copy())
    sinks = attn_sinks.reshape(-1)
    bias_t = _bias_build(rel_bias_table, bucket, "bias_build")
    attn_raw, attn_n = _attn_fwd(q, k, v, bias_t, sinks, attn_out_norm, "attn_fwd", nblk)
    x2, conv_n = _mixout_fwd(x1, attn_n, u, gb, gc, cw, conv_out_norm, wout, "mixout_fwd", tm_mix)
    wgt2, wut2, wd2 = [g.reshape(-1, D) for g in _exchange_wait(h_rest[3:], False, "gather_wait_ffn2", x2)]
    wgu2 = _stack_gate_up(wgt2, wut2, tf)
    x3, xn2, gu2 = _ffn_fwd(x2, ffn2_norm, wgu2, wd2, x2, "ffn2_fwd", tm, tf)

    def blocks(g):
        return g.reshape(N_DEV, -1, D)

    dx3, d_final, loss_part = _loss_head(x3, target, final_norm.reshape(1, D), "loss_head", tm_mix)
    dx2, d_ffn2_norm, dgu2, hh2, do2 = _ffn_bwd(
        dx3, x2, ffn2_norm, gu2, wgu2, wd2, dx3, "ffn2_bwd", tm_bwd, tf)
    d_wg2, d_wu2 = _unstack_gate_up(_tn_grad(dgu2, xn2, "ffn2_wgrad_gu", BM_WGRAD, tk), tf)
    d_wd2 = _tn_grad(hh2, do2, "ffn2_wgrad_down", BM_WGRAD, tk)
    handles2, token2 = _exchange_start([blocks(d_wg2), blocks(d_wu2), blocks(d_wd2)], True,
                                       "grads_start_ffn2", dx2)

    dmixed, d_wout = _mixout_bwd(dx2, attn_n, conv_n, wout, token2, "mixout_bwd", tm_mix)
    dq, dk, dv, dbias, dsink, d_attn_norm = _attn_bwd(
        dmixed, attn_raw, q, k, v, bias_t, sinks, attn_out_norm, "attn_bwd", nblk)
    du, dgb, dgc, d_cw, d_conv_norm = _conv_bwd(dmixed, u, gb, gc, cw, conv_out_norm, "conv_bwd", tm_mix)
    d_table = _bias_grad(dbias, bucket, "bias_grad")
    dz = [dq, dk, dv, du, dgb, dgc]
    dx1, d_mix_norm = _mixin_bwd(dx2, x1, mix_norm, dz, win_t, "mixin_bwd", tm_mix)
    d_win_t = _win_grad(dz, hn, "win_grad", min(TM_MIX, T))
    handles_mix, token_mix = _exchange_start([blocks(d_win_t), blocks(d_wout)], True, "grads_start_mix", dx1)

    dx0, d_ffn1_norm, dgu1, hh1, do1 = _ffn_bwd(
        dx1, x0, ffn1_norm, gu1, wgu1, wd1, token_mix, "ffn1_bwd", tm_bwd, tf)
    def pack(ffn1, mixn, ffn2, fin, attn_n_, conv_n_, sink_, extra, convw, table):
        rows = [_pad_row(ffn1), _pad_row(mixn), _pad_row(ffn2), _pad_row(fin),
                _pad_row(jnp.concatenate([attn_n_.reshape(-1), conv_n_.reshape(-1)])),
                _pad_row(sink_), _pad_row(extra),
                jnp.zeros((1, PACK_COLS), F32),
                jnp.pad(convw, ((0, 0), (0, PACK_COLS - convw.shape[1]))),
                _pad_row(table),
                jnp.zeros((PACK_ROWS - 12, PACK_COLS), F32)]
        return jnp.concatenate(rows, axis=0)

    def own_channels(a):
        full = jnp.zeros((a.shape[1], CONV_DIM), F32)
        return lax.dynamic_update_slice(full, a[0], (0, me * a.shape[2]))

    g_pack = pack(d_ffn1_norm, d_mix_norm, d_ffn2_norm, d_final, d_attn_norm, d_conv_norm,
                  dsink[:, 0], loss_part[0, :1], d_cw[:3], d_table[:, :N_Q_HEADS])
    zero1 = jnp.zeros((1,), F32)
    w_pack = pack(ffn1_norm, mix_norm, ffn2_norm, final_norm, attn_out_norm, conv_out_norm,
                  attn_sinks, zero1, own_channels(conv_w), rel_bias_table)
    m_pack = pack(m_ffn1_norm, m_mix_norm, m_ffn2_norm, m_final_norm, m_attn_out_norm, m_conv_out_norm,
                  m_attn_sinks, zero1, own_channels(m_conv_w), m_rel_bias_table)
    v_pack = pack(v_ffn1_norm, v_mix_norm, v_ffn2_norm, v_final_norm, v_attn_out_norm, v_conv_out_norm,
                  v_attn_sinks, zero1, own_channels(v_conv_w), v_rel_bias_table)
    (g_all,) = _exchange([g_pack], False, "gather_small", dx0)
    packs = _adamw(g_all, w_pack, m_pack, v_pack, "adamw_small", PACK_ROWS)

    d_wg1, d_wu1 = _unstack_gate_up(_tn_grad(dgu1, xn1, "ffn1_wgrad_gu", BM_WGRAD, tk), tf)
    handles1_gu, token1 = _exchange_start([blocks(d_wg1), blocks(d_wu1)], True, "grads_start_ffn1_gu", packs[0])
    d_wd1 = _tn_grad(hh1, do1, "ffn1_wgrad_down", BM_WGRAD, tk)
    handles1_d, token1 = _exchange_start([blocks(d_wd1)], True, "grads_start_ffn1_down", token1)

    res = {}

    def update(names, parts):
        last = None
        for name, p in zip(names, parts):
            w, m_, v_, transposed = big[name]
            if transposed:
                w, m_, v_ = w.T, m_.T, v_.T
            new = _adamw(p, w, m_, v_, "adamw_" + name, _row_tile(w.shape[0], ADAM_ROWS))
            res[name] = tuple((a.T if transposed else a)[None] for a in new)
            last = new[0]
        return last

    parts2 = _exchange_wait(handles2, True, "grads_wait_ffn2", token1)
    done2 = update(["ffn2_w_gate", "ffn2_w_up", "ffn2_w_down"], parts2)
    parts_mix = _exchange_wait(handles_mix, True, "grads_wait_mix", done2)
    done_mix = update(["w_in", "w_out"], parts_mix)

    parts1_gu = _exchange_wait(handles1_gu, True, "grads_wait_ffn1_gu", done_mix)
    done1_gu = update(["ffn1_w_gate", "ffn1_w_up"], parts1_gu)
    parts1_d = _exchange_wait(handles1_d, True, "grads_wait_ffn1_down", done1_gu)
    update(["ffn1_w_down"], parts1_d)

    def unpack(pk):
        cwb = lax.dynamic_slice(pk[8:11, :CONV_DIM], (0, me * conv_w.shape[2]), (3, conv_w.shape[2]))
        return {
            "ffn1_norm": pk[0:1, :D], "mix_norm": pk[1:2, :D], "ffn2_norm": pk[2:3, :D],
            "final_norm": pk[3, :D],
            "attn_out_norm": pk[4:5, :ATTN_WIDTH], "conv_out_norm": pk[4:5, ATTN_WIDTH:ATTN_WIDTH + CONV_DIM],
            "attn_sinks": pk[5:6, :N_Q_HEADS],
            "conv_w": cwb[None],
            "rel_bias_table": pk[11, :NUM_BUCKETS * N_Q_HEADS].reshape(NUM_BUCKETS, N_Q_HEADS),
        }

    small = [unpack(pk) for pk in packs]
    loss = packs[0][6, 0]

    order = ["rel_bias_table", "ffn1_norm", "ffn1_w_gate", "ffn1_w_up", "ffn1_w_down", "mix_norm", "w_in",
             "conv_w", "attn_sinks", "attn_out_norm", "conv_out_norm", "w_out", "ffn2_norm",
             "ffn2_w_gate", "ffn2_w_up", "ffn2_w_down", "final_norm"]
    outs = [loss, dx0[None]]
    for kind in range(4):
        for name in order:
            outs.append(res[name][kind] if name in res else small[kind][name])
    return tuple(outs)
```

```python
import math

import numpy as np
import jax
import jax.numpy as jnp
from jax import lax
from jax.experimental import pallas as pl
from jax.experimental.pallas import tpu as pltpu

F32 = jnp.float32
BF16 = jnp.bfloat16

N_DEV = 8
EPS = 1e-6
HEAD_DIM = 64
N_Q_HEADS = 8
N_KV_HEADS = 2
GQA_GROUP = 4
ATTN_WIDTH = 512
KV_WIDTH = 128
CONV_DIM = 512
BLOCK = 128
WINDOW = 128
NUM_BUCKETS = 32
MAX_DISTANCE = 128
SCALE = HEAD_DIM ** -0.5
MASKED = -1e30
GROUP_ROWS = GQA_GROUP * BLOCK

ADAM_LR = 0.001
ADAM_B1 = 0.9
ADAM_B2 = 0.999
ADAM_EPS = 1e-08
ADAM_WD = 0.01
ADAM_STEP = 10

VMEM_LIMIT_BYTES = 40 * 1024 * 1024
SUBLANES = 8
PACK_ROWS = 16
PACK_COLS = 1024

TM_FFN = 1024
TM_FFN_BWD = 1024
TM_MIX = 512
TK_WGRAD = 1024
TF_FFN = 256
BM_WGRAD = 1408
ROW_GROUPS = 4
ATTN_BLOCKS = 4
ADAM_ROWS = 256


def _row_tile(rows, limit):
    best = rows
    for t in range(16, min(rows, limit) + 1, 16):
        if rows % t == 0:
            best = t
    return best


def _params(*sem):
    return pltpu.CompilerParams(dimension_semantics=sem, vmem_limit_bytes=VMEM_LIMIT_BYTES)


def _hbm(*arrays):
    return [pltpu.with_memory_space_constraint(a, pltpu.HBM) for a in arrays]


def _dot(a, b):
    return jnp.dot(a, b, preferred_element_type=F32)


def _dot_nt(a, b):
    return lax.dot_general(a, b, (((1,), (1,)), ((), ())), preferred_element_type=F32)


def _dot_tn(a, b):
    return lax.dot_general(a, b, (((0,), (0,)), ((), ())), preferred_element_type=F32)


def _sigmoid(g):
    return 0.5 * jnp.tanh(0.5 * g) + 0.5


def _rms_stats(x):
    inv = lax.rsqrt(jnp.mean(x * x, axis=-1, keepdims=True) + EPS)
    return inv, x * inv


def _rms_bwd(dy, x, gain):
    inv, xhat = _rms_stats(x)
    dgain = jnp.sum(dy * xhat, axis=0, keepdims=True)
    dxh = dy * gain
    dx = inv * (dxh - xhat * jnp.mean(dxh * xhat, axis=-1, keepdims=True))
    return dx, dgain


def _peer_list():
    x, y, c = lax.axis_index("x"), lax.axis_index("y"), lax.axis_index("c")
    peers = []
    for k in range(1, N_DEV):
        px = 1 - x if (k >> 2) & 1 else x
        py = 1 - y if (k >> 1) & 1 else y
        pc = 1 - c if k & 1 else c
        peers.append((px, py, pc))
    return 4 * x + 2 * y + c, peers


def _exchange(arrs, scatter, name, after):
    n = len(arrs)
    out_shape = []
    for a in arrs:
        shp = a.shape if scatter else (N_DEV,) + a.shape
        out_shape.append(jax.ShapeDtypeStruct(shp, a.dtype))

    def body(*refs):
        ins, outs = refs[:n], refs[n + 1:2 * n + 1]
        send_sems, recv_sems, local_sems = refs[2 * n + 1:]
        me, peers = _peer_list()
        started = []
        for a in range(n):
            own = ins[a].at[me] if scatter else ins[a]
            loc = pltpu.make_async_copy(own, outs[a].at[me], local_sems.at[a])
            loc.start()
            started.append(loc)
        sends = []
        for a in range(n):
            for k, (px, py, pc) in enumerate(peers):
                src = ins[a].at[4 * px + 2 * py + pc] if scatter else ins[a]
                cp = pltpu.make_async_remote_copy(
                    src_ref=src, dst_ref=outs[a].at[me],
                    send_sem=send_sems.at[a, k], recv_sem=recv_sems.at[a, k],
                    device_id=(px, py, pc), device_id_type=pl.DeviceIdType.MESH)
                cp.start()
                sends.append(cp)
        for a in range(n):
            for k, (px, py, pc) in enumerate(peers):
                landed = outs[a].at[4 * px + 2 * py + pc]
                pltpu.make_async_remote_copy(
                    src_ref=landed, dst_ref=landed,
                    send_sem=send_sems.at[a, k], recv_sem=recv_sems.at[a, k],
                    device_id=(px, py, pc), device_id_type=pl.DeviceIdType.MESH).wait_recv()
        for cp in sends:
            cp.wait_send()
        for loc in started:
            loc.wait()

    hbm = pl.BlockSpec(memory_space=pl.ANY)
    return pl.pallas_call(
        body, name=name, out_shape=tuple(out_shape),
        in_specs=[hbm] * (n + 1), out_specs=tuple([hbm] * n),
        scratch_shapes=[pltpu.SemaphoreType.DMA((n, N_DEV - 1)),
                        pltpu.SemaphoreType.DMA((n, N_DEV - 1)),
                        pltpu.SemaphoreType.DMA((n,))],
    )(*arrs, after)


def _gather_two_level(arrs, name, after):
    n = len(arrs)
    out_shape = tuple(jax.ShapeDtypeStruct((N_DEV,) + a.shape, a.dtype) for a in arrs)

    def body(*refs):
        ins, outs = refs[:n], refs[n + 1:2 * n + 1]
        send_sems, recv_sems, local_sems = refs[2 * n + 1:]
        x, y, c = lax.axis_index("x"), lax.axis_index("y"), lax.axis_index("c")
        me, sibling = (x, y, c), (x, y, 1 - c)
        chips = [(1 - x, y), (x, 1 - y), (1 - x, 1 - y)]

        def copy(a, k, block, to, src=None):
            slot = outs[a].at[4 * block[0] + 2 * block[1] + block[2]]
            return pltpu.make_async_remote_copy(
                src_ref=slot if src is None else src, dst_ref=slot,
                send_sem=send_sems.at[a, k], recv_sem=recv_sems.at[a, k],
                device_id=to, device_id_type=pl.DeviceIdType.MESH)

        started = []
        for a in range(n):
            loc = pltpu.make_async_copy(ins[a], outs[a].at[4 * x + 2 * y + c], local_sems.at[a])
            loc.start()
            started.append(loc)
        sends = []
        for a in range(n):
            sends.append(copy(a, 0, me, sibling, src=ins[a]))
            sends += [copy(a, 1 + j, me, (*chip, c), src=ins[a]) for j, chip in enumerate(chips)]
        for cp in sends:
            cp.start()
        for j, chip in enumerate(chips):
            for a in range(n):
                copy(a, 1 + j, (*chip, c), me).wait_recv()
                fwd = copy(a, 4 + j, (*chip, c), sibling)
                fwd.start()
                sends.append(fwd)
        for a in range(n):
            copy(a, 0, sibling, me).wait_recv()
            for j, chip in enumerate(chips):
                copy(a, 4 + j, (*chip, 1 - c), me).wait_recv()
        for cp in sends:
            cp.wait_send()
        for loc in started:
            loc.wait()

    hbm = pl.BlockSpec(memory_space=pl.ANY)
    return pl.pallas_call(
        body, name=name, out_shape=out_shape,
        in_specs=[hbm] * (n + 1), out_specs=tuple([hbm] * n),
        scratch_shapes=[pltpu.SemaphoreType.DMA((n, N_DEV - 1)),
                        pltpu.SemaphoreType.DMA((n, N_DEV - 1)),
                        pltpu.SemaphoreType.DMA((n,))],
    )(*arrs, after)


_HBM = pl.BlockSpec(memory_space=pltpu.HBM)
_SEM = pl.BlockSpec(memory_space=pltpu.SEMAPHORE)
_EFFECT = pltpu.SideEffectType.DATAFLOW_SIDE_EFFECTING


def _split_copies(srcs, lands, send_sems, recv_sems, scatter):
    me, peers = _peer_list()
    copies = []
    for a in range(len(srcs)):
        for k, (px, py, pc) in enumerate(peers):
            src = srcs[a].at[4 * px + 2 * py + pc] if scatter else srcs[a]
            copies.append(pltpu.make_async_remote_copy(
                src_ref=src, dst_ref=lands[a].at[me],
                send_sem=send_sems[a].at[k], recv_sem=recv_sems[a].at[k],
                device_id=(px, py, pc), device_id_type=pl.DeviceIdType.MESH))
    return copies


def _exchange_start(arrs, scatter, name, after):
    n = len(arrs)
    me = 4 * lax.axis_index("x") + 2 * lax.axis_index("y") + lax.axis_index("c")
    lands = []
    for a in arrs:
        own = lax.dynamic_index_in_dim(a, me, 0, keepdims=True) if scatter else a[None]
        shp = a.shape if scatter else (N_DEV,) + a.shape
        lands.append(lax.dynamic_update_slice(lax.empty(shp, a.dtype), own, (me,) + (0,) * (len(shp) - 1)))

    def body(*refs):
        srcs, lnds = refs[:n], refs[n:2 * n]
        outs = refs[2 * n + 1:]
        send_sems, recv_sems = outs[:n], outs[n:2 * n]
        token = outs[4 * n]
        for cp in _split_copies(srcs, lnds, send_sems, recv_sems, scatter):
            cp.start()
        token[...] = jnp.zeros_like(token)

    sem = pltpu.SemaphoreType.DMA((N_DEV - 1,))
    out_shape = ([sem] * (2 * n) + [pltpu.HBM(a.shape, a.dtype) for a in arrs]
                 + [pltpu.HBM(l.shape, l.dtype) for l in lands] + [jax.ShapeDtypeStruct((SUBLANES, 128), F32)])
    res = pl.pallas_call(
        body, name=name, out_shape=tuple(out_shape),
        in_specs=[_HBM] * (2 * n) + [pl.BlockSpec(memory_space=pl.ANY)],
        out_specs=tuple([_SEM] * (2 * n) + [_HBM] * (2 * n) + [pl.BlockSpec(memory_space=pltpu.VMEM)]),
        input_output_aliases={i: 2 * n + i for i in range(2 * n)},
        compiler_params=pltpu.CompilerParams(has_side_effects=_EFFECT),
    )(*[pltpu.with_memory_space_constraint(a, pltpu.HBM) for a in arrs],
      *[pltpu.with_memory_space_constraint(l, pltpu.HBM) for l in lands], after)
    handles = [(res[2 * n + a], res[3 * n + a], res[a], res[n + a]) for a in range(n)]
    return handles, res[4 * n]


def _exchange_wait(handles, scatter, name, after):
    n = len(handles)

    def body(*refs):
        srcs, lnds = refs[:n], refs[n:2 * n]
        send_sems, recv_sems = refs[2 * n:3 * n], refs[3 * n:4 * n]
        for cp in _split_copies(srcs, lnds, send_sems, recv_sems, scatter):
            cp.wait_send()
            cp.wait_recv()

    srcs = [h[0] for h in handles]
    lands = [h[1] for h in handles]
    res = pl.pallas_call(
        body, name=name,
        out_shape=tuple([pltpu.HBM(a.shape, a.dtype) for a in srcs] + [pltpu.HBM(l.shape, l.dtype) for l in lands]),
        in_specs=[_HBM] * (2 * n) + [_SEM] * (2 * n) + [pl.BlockSpec(memory_space=pl.ANY)],
        out_specs=tuple([_HBM] * (2 * n)),
        input_output_aliases={i: i for i in range(2 * n)},
        compiler_params=pltpu.CompilerParams(has_side_effects=_EFFECT),
    )(*srcs, *lands, *[h[2] for h in handles], *[h[3] for h in handles], after)
    return list(res[n:])


def _stack_gate_up(wgt, wut, tf):
    F, D = wgt.shape
    return jnp.stack([wgt.reshape(F // tf, tf, D), wut.reshape(F // tf, tf, D)], axis=1).reshape(2 * F, D)


def _unstack_gate_up(w, tf):
    F2, D = w.shape
    w4 = w.reshape(F2 // (2 * tf), 2, tf, D)
    return w4[:, 0].reshape(F2 // 2, D), w4[:, 1].reshape(F2 // 2, D)


def _row_groups(tm):
    return [slice(r * (tm // ROW_GROUPS), (r + 1) * (tm // ROW_GROUPS)) for r in range(ROW_GROUPS)]


def _ffn_fwd(x, gain, wgu, wd, after, name, tm, tf):
    T, D = x.shape
    F = wd.shape[0]
    nj = F // tf

    def body(x_ref, g_ref, wgu_ref, wd_ref, after_ref, xo_ref, xn_ref, gu_ref, xn_sc, acc_sc):
        j = pl.program_id(1)

        @pl.when(j == 0)
        def _():
            _, xhat = _rms_stats(x_ref[...])
            xn = (xhat * g_ref[...]).astype(BF16)
            xn_sc[...] = xn
            xn_ref[...] = xn
            acc_sc[...] = jnp.zeros_like(acc_sc)

        groups = _row_groups(tm)
        gus = [_dot_nt(xn_sc[rows, :], wgu_ref[...]) for rows in groups]
        hs = []
        for rows, gu in zip(groups, gus):
            gu_ref[rows, :] = gu.astype(BF16)
            g, u = gu[:, :tf], gu[:, tf:]
            hs.append((g * _sigmoid(g) * u).astype(BF16))
        for rows, h in zip(groups, hs):
            acc_sc[rows, :] += _dot(h, wd_ref[...])

        @pl.when(j == nj - 1)
        def _():
            xo_ref[...] = x_ref[...] + 0.5 * acc_sc[...]

    return pl.pallas_call(
        body, name=name, grid=(T // tm, nj),
        in_specs=[pl.BlockSpec((tm, D), lambda i, j: (i, 0)),
                  pl.BlockSpec((1, D), lambda i, j: (0, 0)),
                  pl.BlockSpec((2 * tf, D), lambda i, j: (j, 0)),
                  pl.BlockSpec((tf, D), lambda i, j: (j, 0)),
                  pl.BlockSpec(memory_space=pl.ANY)],
        out_specs=(pl.BlockSpec((tm, D), lambda i, j: (i, 0)),
                   pl.BlockSpec((tm, D), lambda i, j: (i, 0)),
                   pl.BlockSpec((tm, 2 * tf), lambda i, j: (i, j))),
        out_shape=(jax.ShapeDtypeStruct((T, D), F32), jax.ShapeDtypeStruct((T, D), BF16),
                   jax.ShapeDtypeStruct((T, 2 * F), BF16)),
        scratch_shapes=[pltpu.VMEM((tm, D), BF16), pltpu.VMEM((tm, D), F32)],
        compiler_params=_params("arbitrary", "arbitrary"),
    )(*_hbm(x, gain, wgu, wd, after))


def _ffn_bwd(dy, x, gain, gu, wgu, wd, after, name, tm, tf):
    T, D = x.shape
    F = wd.shape[0]
    nj = F // tf

    def body(dy_ref, x_ref, g_ref, gu_ref, wgu_ref, wd_ref, after_ref,
             dx_ref, dgain_ref, dgu_ref, hh_ref, do_ref, do_sc, acc_sc):
        i, j = pl.program_id(0), pl.program_id(1)

        @pl.when((i == 0) & (j == 0))
        def _():
            dgain_ref[...] = jnp.zeros_like(dgain_ref)

        @pl.when(j == 0)
        def _():
            do = (0.5 * dy_ref[...]).astype(BF16)
            do_sc[...] = do
            do_ref[...] = do
            acc_sc[...] = jnp.zeros_like(acc_sc)

        groups = _row_groups(tm)
        dhs = [_dot_nt(do_sc[rows, :], wd_ref[...]) for rows in groups]
        for rows, dh in zip(groups, dhs):
            g = gu_ref[rows, :tf].astype(F32)
            u = gu_ref[rows, tf:].astype(F32)
            sig = _sigmoid(g)
            s = g * sig
            dgu_ref[rows, :tf] = (dh * u * (sig + s * (1.0 - sig))).astype(BF16)
            dgu_ref[rows, tf:] = (dh * s).astype(BF16)
            hh_ref[rows, :] = (s * u).astype(BF16)
        for rows in groups:
            acc_sc[rows, :] += _dot(dgu_ref[rows, :], wgu_ref[...])

        @pl.when(j == nj - 1)
        def _():
            dx, dgain = _rms_bwd(acc_sc[...], x_ref[...], g_ref[...])
            dgain_ref[...] += dgain
            dx_ref[...] = dy_ref[...] + dx

    tile = pl.BlockSpec((tm, D), lambda i, j: (i, 0), pipeline_mode=pl.Buffered(1))
    return pl.pallas_call(
        body, name=name, grid=(T // tm, nj),
        in_specs=[tile, tile,
                  pl.BlockSpec((1, D), lambda i, j: (0, 0)),
                  pl.BlockSpec((tm, 2 * tf), lambda i, j: (i, j)),
                  pl.BlockSpec((2 * tf, D), lambda i, j: (j, 0)),
                  pl.BlockSpec((tf, D), lambda i, j: (j, 0)),
                  pl.BlockSpec(memory_space=pl.ANY)],
        out_specs=(tile,
                   pl.BlockSpec((1, D), lambda i, j: (0, 0)),
                   pl.BlockSpec((tm, 2 * tf), lambda i, j: (i, j)),
                   pl.BlockSpec((tm, tf), lambda i, j: (i, j)),
                   pl.BlockSpec((tm, D), lambda i, j: (i, 0))),
        out_shape=(jax.ShapeDtypeStruct((T, D), F32), jax.ShapeDtypeStruct((1, D), F32),
                   jax.ShapeDtypeStruct((T, 2 * F), BF16), jax.ShapeDtypeStruct((T, F), BF16),
                   jax.ShapeDtypeStruct((T, D), BF16)),
        scratch_shapes=[pltpu.VMEM((tm, D), BF16), pltpu.VMEM((tm, D), F32)],
        compiler_params=_params("arbitrary", "arbitrary"),
    )(*_hbm(dy, x, gain, gu, wgu, wd, after))


def _tn_grad(a, b, after, name, bm, tk):
    T, M = a.shape
    D = b.shape[1]
    nk = T // tk

    def body(a_ref, b_ref, after_ref, o_ref, acc_sc):
        k = pl.program_id(1)

        @pl.when(k == 0)
        def _():
            acc_sc[...] = jnp.zeros_like(acc_sc)

        acc_sc[...] += _dot_tn(a_ref[...], b_ref[...])

        @pl.when(k == nk - 1)
        def _():
            o_ref[...] = acc_sc[...].astype(BF16)

    return pl.pallas_call(
        body, name=name, grid=(M // bm, nk),
        in_specs=[pl.BlockSpec((tk, bm), lambda i, k: (k, i)), pl.BlockSpec((tk, D), lambda i, k: (k, 0)),
                  pl.BlockSpec(memory_space=pl.ANY)],
        out_specs=pl.BlockSpec((bm, D), lambda i, k: (i, 0)),
        out_shape=jax.ShapeDtypeStruct((M, D), BF16),
        scratch_shapes=[pltpu.VMEM((bm, D), F32)],
        compiler_params=_params("arbitrary", "arbitrary"),
    )(*_hbm(a, b, after))


_Z_SPLITS = (0, 512, 640, 768, 1280, 1792, 2304)


def _mixin_fwd(x, gain, w_in_t, name, tm):
    T, D = x.shape
    widths = [b - a for a, b in zip(_Z_SPLITS[:-1], _Z_SPLITS[1:])]

    def body(x_ref, g_ref, w_ref, hn_ref, *outs):
        _, xhat = _rms_stats(x_ref[...])
        hn = (xhat * g_ref[...]).astype(BF16)
        hn_ref[...] = hn
        for o_ref, lo, hi in zip(outs, _Z_SPLITS[:-1], _Z_SPLITS[1:]):
            o_ref[...] = _dot_nt(hn, w_ref[lo:hi, :]).astype(BF16)

    return pl.pallas_call(
        body, name=name, grid=(T // tm,),
        in_specs=[pl.BlockSpec((tm, D), lambda i: (i, 0)),
                  pl.BlockSpec((1, D), lambda i: (0, 0)),
                  pl.BlockSpec(w_in_t.shape, lambda i: (0, 0))],
        out_specs=tuple([pl.BlockSpec((tm, D), lambda i: (i, 0))]
                        + [pl.BlockSpec((tm, w), lambda i: (i, 0)) for w in widths]),
        out_shape=tuple([jax.ShapeDtypeStruct((T, D), BF16)]
                        + [jax.ShapeDtypeStruct((T, w), BF16) for w in widths]),
        compiler_params=_params("arbitrary"),
    )(*_hbm(x, gain, w_in_t))


def _bucket_table():
    qi = np.arange(BLOCK, dtype=np.int32)[:, None]
    kj = np.arange(2 * BLOCK, dtype=np.int32)[None, :]
    dist = qi + BLOCK - kj
    n = np.maximum(dist, 0)
    max_exact = NUM_BUCKETS // 2
    large = max_exact + (np.log(np.maximum(n, 1).astype(np.float32) / max_exact)
                         / math.log(MAX_DISTANCE / max_exact)
                         * (NUM_BUCKETS - max_exact)).astype(np.int32)
    large = np.minimum(large, NUM_BUCKETS - 1)
    bucket = np.where(n < max_exact, n, large).astype(np.int32)
    valid = (dist >= 0) & (dist < WINDOW)
    return np.where(valid, bucket, -1).astype(np.int32)


def _bias_build(table, bucket, name):
    def body(t_ref, b_ref, o_ref):
        bk = b_ref[...]
        for h in range(N_Q_HEADS):
            def step(b, acc):
                return jnp.where(bk == b, t_ref[b, h], acc)
            o_ref[h] = lax.fori_loop(0, NUM_BUCKETS, step, jnp.full(bk.shape, MASKED, F32))

    return pl.pallas_call(
        body, name=name,
        in_specs=[pl.BlockSpec(memory_space=pltpu.SMEM), pl.BlockSpec(memory_space=pltpu.VMEM)],
        out_specs=pl.BlockSpec(memory_space=pltpu.VMEM),
        out_shape=jax.ShapeDtypeStruct((N_Q_HEADS,) + bucket.shape, F32),
    )(table, bucket)


def _bias_grad(dbias, bucket, name):
    def body(d_ref, b_ref, o_ref):
        bk = b_ref[...]
        row = lax.broadcasted_iota(jnp.int32, o_ref.shape, 0)
        lane = lax.broadcasted_iota(jnp.int32, o_ref.shape, 1)
        res = jnp.zeros(o_ref.shape, F32)
        for h in range(N_Q_HEADS):
            d = d_ref[h]

            def step(b, acc):
                part = jnp.sum(jnp.where(bk == b, d, 0.0), axis=0, keepdims=True)
                return jnp.where(row == b, part, acc)
            per_lane = lax.fori_loop(0, NUM_BUCKETS, step, jnp.zeros(o_ref.shape, F32))
            res = jnp.where(lane == h, jnp.sum(per_lane, axis=1, keepdims=True), res)
        o_ref[...] = res

    return pl.pallas_call(
        body, name=name,
        in_specs=[pl.BlockSpec(memory_space=pltpu.VMEM), pl.BlockSpec(memory_space=pltpu.VMEM)],
        out_specs=pl.BlockSpec(memory_space=pltpu.VMEM),
        out_shape=jax.ShapeDtypeStruct((NUM_BUCKETS, 128), F32),
    )(*_hbm(dbias, bucket))


def _head_cols(h):
    return slice(h * HEAD_DIM, (h + 1) * HEAD_DIM)


def _stack_heads(ref, r0, g, dtype):
    return jnp.concatenate(
        [ref[pl.ds(r0, BLOCK), _head_cols(GQA_GROUP * g + j)].astype(dtype) for j in range(GQA_GROUP)], axis=0)


def _unstack_heads(ref, r0, g, val):
    for j in range(GQA_GROUP):
        ref[pl.ds(r0, BLOCK), _head_cols(GQA_GROUP * g + j)] = val[j * BLOCK:(j + 1) * BLOCK, :]


def _head_lanes(h):
    return slice(h * BLOCK, (h + 1) * BLOCK)


def _group_lanes(g):
    return slice(g * GROUP_ROWS, (g + 1) * GROUP_ROWS)


def _head_softmax(st, bias_t, sink, no_prev):
    s = st * SCALE + bias_t
    row = lax.broadcasted_iota(jnp.int32, s.shape, 0)
    s = jnp.where(no_prev & (row < BLOCK), MASKED, s)
    m = jnp.maximum(jnp.max(s, axis=0, keepdims=True), sink)
    p = jnp.exp(s - m)
    ps = jnp.exp(sink - m)
    r = 1.0 / (jnp.sum(p, axis=0, keepdims=True) + ps)
    return p * r, ps * r


def _load_band(kf_sc, vf_sc, kp_ref, kc_ref, vp_ref, vc_ref, tq):
    kf_sc[0:BLOCK, :] = kp_ref[...]
    kf_sc[BLOCK:BLOCK + tq, :] = kc_ref[...]
    vf_sc[0:BLOCK, :] = vp_ref[...]
    vf_sc[BLOCK:BLOCK + tq, :] = vc_ref[...]


def _attn_fwd(q, k, v, bias_t, sinks, gain, name, nblk):
    T = q.shape[0]
    tq = nblk * BLOCK

    def body(sink_ref, q_ref, kc_ref, kp_ref, vc_ref, vp_ref, bias_ref, g_ref, raw_ref, nrm_ref,
             kf_sc, vf_sc, o_sc, st_sc, pt_sc):
        i = pl.program_id(0)
        _load_band(kf_sc, vf_sc, kp_ref, kc_ref, vp_ref, vc_ref, tq)

        def block(b, carry):
            r0 = pl.multiple_of(b * BLOCK, BLOCK)
            no_prev = (i == 0) & (b == 0)
            for g in range(N_KV_HEADS):
                kb = kf_sc[pl.ds(r0, 2 * BLOCK), _head_cols(g)]
                st_sc[:, _group_lanes(g)] = _dot_nt(kb, _stack_heads(q_ref, r0, g, BF16))
            for h in range(N_Q_HEADS):
                p, _ = _head_softmax(st_sc[:, _head_lanes(h)], bias_ref[h], sink_ref[h], no_prev)
                pt_sc[:, _head_lanes(h)] = p.astype(BF16)
            for g in range(N_KV_HEADS):
                vb = vf_sc[pl.ds(r0, 2 * BLOCK), _head_cols(g)]
                _unstack_heads(o_sc, r0, g, _dot_tn(pt_sc[:, _group_lanes(g)], vb))
            return carry

        lax.fori_loop(0, nblk, block, 0)
        o = o_sc[...]
        raw_ref[...] = o.astype(BF16)
        _, ohat = _rms_stats(o)
        nrm_ref[...] = (ohat * g_ref[...]).astype(BF16)

    cur = lambda i: (i, 0)
    prev = lambda i: (jnp.maximum(i * nblk - 1, 0), 0)
    lanes = N_Q_HEADS * BLOCK
    return pl.pallas_call(
        body, name=name, grid=(T // tq,),
        in_specs=[pl.BlockSpec(memory_space=pltpu.SMEM),
                  pl.BlockSpec((tq, ATTN_WIDTH), cur),
                  pl.BlockSpec((tq, KV_WIDTH), cur), pl.BlockSpec((BLOCK, KV_WIDTH), prev),
                  pl.BlockSpec((tq, KV_WIDTH), cur), pl.BlockSpec((BLOCK, KV_WIDTH), prev),
                  pl.BlockSpec(bias_t.shape, lambda i: (0, 0, 0)),
                  pl.BlockSpec((1, ATTN_WIDTH), lambda i: (0, 0))],
        out_specs=(pl.BlockSpec((tq, ATTN_WIDTH), cur), pl.BlockSpec((tq, ATTN_WIDTH), cur)),
        out_shape=(jax.ShapeDtypeStruct((T, ATTN_WIDTH), BF16), jax.ShapeDtypeStruct((T, ATTN_WIDTH), BF16)),
        scratch_shapes=[pltpu.VMEM((tq + BLOCK, KV_WIDTH), BF16), pltpu.VMEM((tq + BLOCK, KV_WIDTH), BF16),
                        pltpu.VMEM((tq, ATTN_WIDTH), F32),
                        pltpu.VMEM((2 * BLOCK, lanes), F32), pltpu.VMEM((2 * BLOCK, lanes), BF16)],
        compiler_params=_params("arbitrary"),
    )(sinks, *_hbm(q, k, k, v, v, bias_t, gain))


def _attn_bwd(dmixed, raw, q, k, v, bias_t, sinks, gain, name, nblk):
    T = q.shape[0]
    tq = nblk * BLOCK
    nt = T // tq
    lanes = N_Q_HEADS * BLOCK

    def body(sink_ref, dm_ref, raw_ref, q_ref, kc_ref, kp_ref, vc_ref, vp_ref, bias_ref, g_ref,
             dq_ref, dk_ref, dv_ref, dbias_ref, dsink_ref, dgain_ref,
             do_sc, dq_sc, kf_sc, vf_sc, dkf_sc, dvf_sc, st_sc, dpt_sc, pt_sc, dst_sc, drow_sc,
             qs_sc, dos_sc, dsink_sc):
        i = pl.program_id(0)
        tile = nt - 1 - i

        @pl.when(i == 0)
        def _():
            dkf_sc[...] = jnp.zeros_like(dkf_sc)
            dvf_sc[...] = jnp.zeros_like(dvf_sc)
            dsink_sc[...] = jnp.zeros_like(dsink_sc)
            dbias_ref[...] = jnp.zeros_like(dbias_ref)
            dgain_ref[...] = jnp.zeros_like(dgain_ref)

        carry_k = dkf_sc[0:BLOCK, :]
        carry_v = dvf_sc[0:BLOCK, :]
        dkf_sc[0:tq, :] = jnp.zeros((tq, KV_WIDTH), F32)
        dvf_sc[0:tq, :] = jnp.zeros((tq, KV_WIDTH), F32)
        dkf_sc[tq:tq + BLOCK, :] = carry_k
        dvf_sc[tq:tq + BLOCK, :] = carry_v
        _load_band(kf_sc, vf_sc, kp_ref, kc_ref, vp_ref, vc_ref, tq)

        do, dgain = _rms_bwd(dm_ref[...].astype(F32), raw_ref[...].astype(F32), g_ref[...])
        dgain_ref[...] += dgain
        do_sc[...] = do
        ones = jnp.ones((SUBLANES, HEAD_DIM), BF16)

        def block(b, carry):
            r0 = pl.multiple_of(b * BLOCK, BLOCK)
            no_prev = (tile == 0) & (b == 0)
            for g in range(N_KV_HEADS):
                kb = kf_sc[pl.ds(r0, 2 * BLOCK), _head_cols(g)]
                vb = vf_sc[pl.ds(r0, 2 * BLOCK), _head_cols(g)]
                qg = _stack_heads(q_ref, r0, g, BF16)
                dog = _stack_heads(do_sc, r0, g, F32)
                prod = dog * _stack_heads(raw_ref, r0, g, F32)
                hi = prod.astype(BF16)
                lo = (prod - hi.astype(F32)).astype(BF16)
                drow_sc[:, _group_lanes(g)] = _dot_nt(ones, hi) + _dot_nt(ones, lo)
                dogb = dog.astype(BF16)
                qs_sc[g] = qg
                dos_sc[g] = dogb
                st_sc[:, _group_lanes(g)] = _dot_nt(kb, qg)
                dpt_sc[:, _group_lanes(g)] = _dot_nt(vb, dogb)
            for h in range(N_Q_HEADS):
                hl = _head_lanes(h)
                p, ps = _head_softmax(st_sc[:, hl], bias_ref[h], sink_ref[h], no_prev)
                rowdot = drow_sc[0:1, hl]
                ds = p * (dpt_sc[:, hl] - rowdot)
                dsink_sc[h:h + 1, :] += -(ps * rowdot)
                dbias_ref[h] += ds
                dst_sc[:, hl] = ds.astype(BF16)
                pt_sc[:, hl] = p.astype(BF16)
            for g in range(N_KV_HEADS):
                kb = kf_sc[pl.ds(r0, 2 * BLOCK), _head_cols(g)]
                dsg = dst_sc[:, _group_lanes(g)]
                _unstack_heads(dq_sc, r0, g, _dot_tn(dsg, kb) * SCALE)
                dkf_sc[pl.ds(r0, 2 * BLOCK), _head_cols(g)] += _dot(dsg, qs_sc[g]) * SCALE
                dvf_sc[pl.ds(r0, 2 * BLOCK), _head_cols(g)] += _dot(pt_sc[:, _group_lanes(g)], dos_sc[g])
            return carry

        lax.fori_loop(0, nblk, block, 0)
        dq_ref[...] = dq_sc[...].astype(BF16)
        dk_ref[...] = dkf_sc[BLOCK:BLOCK + tq, :].astype(BF16)
        dv_ref[...] = dvf_sc[BLOCK:BLOCK + tq, :].astype(BF16)

        @pl.when(i == nt - 1)
        def _():
            tot = jnp.sum(dsink_sc[...], axis=1, keepdims=True)
            dsink_ref[...] = jnp.broadcast_to(tot, dsink_ref.shape)

    cur = lambda i: (nt - 1 - i, 0)
    prev = lambda i: (jnp.maximum((nt - 1 - i) * nblk - 1, 0), 0)
    const2 = lambda i: (0, 0)
    const3 = lambda i: (0, 0, 0)
    return pl.pallas_call(
        body, name=name, grid=(nt,),
        in_specs=[pl.BlockSpec(memory_space=pltpu.SMEM),
                  pl.BlockSpec((tq, ATTN_WIDTH), cur),
                  pl.BlockSpec((tq, ATTN_WIDTH), cur),
                  pl.BlockSpec((tq, ATTN_WIDTH), cur),
                  pl.BlockSpec((tq, KV_WIDTH), cur), pl.BlockSpec((BLOCK, KV_WIDTH), prev),
                  pl.BlockSpec((tq, KV_WIDTH), cur), pl.BlockSpec((BLOCK, KV_WIDTH), prev),
                  pl.BlockSpec(bias_t.shape, const3),
                  pl.BlockSpec((1, ATTN_WIDTH), const2)],
        out_specs=(pl.BlockSpec((tq, ATTN_WIDTH), cur),
                   pl.BlockSpec((tq, KV_WIDTH), cur), pl.BlockSpec((tq, KV_WIDTH), cur),
                   pl.BlockSpec(bias_t.shape, const3),
                   pl.BlockSpec((N_Q_HEADS, 128), const2),
                   pl.BlockSpec((1, ATTN_WIDTH), const2)),
        out_shape=(jax.ShapeDtypeStruct((T, ATTN_WIDTH), BF16),
                   jax.ShapeDtypeStruct((T, KV_WIDTH), BF16), jax.ShapeDtypeStruct((T, KV_WIDTH), BF16),
                   jax.ShapeDtypeStruct(bias_t.shape, F32),
                   jax.ShapeDtypeStruct((N_Q_HEADS, 128), F32),
                   jax.ShapeDtypeStruct((1, ATTN_WIDTH), F32)),
        scratch_shapes=[pltpu.VMEM((tq, ATTN_WIDTH), F32), pltpu.VMEM((tq, ATTN_WIDTH), F32),
                        pltpu.VMEM((tq + BLOCK, KV_WIDTH), BF16), pltpu.VMEM((tq + BLOCK, KV_WIDTH), BF16),
                        pltpu.VMEM((tq + BLOCK, KV_WIDTH), F32), pltpu.VMEM((tq + BLOCK, KV_WIDTH), F32),
                        pltpu.VMEM((2 * BLOCK, lanes), F32), pltpu.VMEM((2 * BLOCK, lanes), F32),
                        pltpu.VMEM((2 * BLOCK, lanes), BF16), pltpu.VMEM((2 * BLOCK, lanes), BF16),
                        pltpu.VMEM((SUBLANES, lanes), F32),
                        pltpu.VMEM((N_KV_HEADS, GROUP_ROWS, HEAD_DIM), BF16),
                        pltpu.VMEM((N_KV_HEADS, GROUP_ROWS, HEAD_DIM), BF16),
                        pltpu.VMEM((N_Q_HEADS, 128), F32)],
        compiler_params=_params("arbitrary"),
    )(sinks, *_hbm(dmixed, raw, q, k, k, v, v, bias_t, gain))


def _shift_down(cu, tail):
    row = lax.broadcasted_iota(jnp.int32, cu.shape, 0)
    t6, t7 = tail[6:7, :], tail[7:8, :]
    s1 = jnp.where(row == 0, t7, pltpu.roll(cu, 1, 0))
    s2 = jnp.where(row == 0, t6, jnp.where(row == 1, t7, pltpu.roll(cu, 2, 0)))
    return s1, s2


def _shift_up(d, head):
    n = d.shape[0]
    row = lax.broadcasted_iota(jnp.int32, d.shape, 0)
    h0, h1 = head[0:1, :], head[1:2, :]
    s1 = jnp.where(row == n - 1, h0, pltpu.roll(d, n - 1, 0))
    s2 = jnp.where(row == n - 1, h1, jnp.where(row == n - 2, h0, pltpu.roll(d, n - 2, 0)))
    return s1, s2


def _mixout_fwd(x, attn_n, u, gb, gc, conv_w, gain, w_out, name, tm):
    T, D = x.shape

    def body(x_ref, an_ref, u_ref, b_ref, c_ref, cw_ref, g_ref, wo_ref, xo_ref, cn_ref, tail_sc):
        @pl.when(pl.program_id(0) == 0)
        def _():
            tail_sc[...] = jnp.zeros_like(tail_sc)

        cu = c_ref[...].astype(F32) * u_ref[...].astype(F32)
        s1, s2 = _shift_down(cu, tail_sc[...])
        tail_sc[...] = cu[tm - SUBLANES:tm, :]
        pre = cw_ref[0:1, :] * s2 + cw_ref[1:2, :] * s1 + cw_ref[2:3, :] * cu
        conv = b_ref[...].astype(F32) * pre
        _, chat = _rms_stats(conv)
        cn = (chat * g_ref[...]).astype(BF16)
        cn_ref[...] = cn
        xo_ref[...] = (x_ref[...] + _dot(an_ref[...], wo_ref[0:ATTN_WIDTH, :])
                       + _dot(cn, wo_ref[ATTN_WIDTH:ATTN_WIDTH + CONV_DIM, :]))

    row = lambda i: (i, 0)
    const = lambda i: (0, 0)
    return pl.pallas_call(
        body, name=name, grid=(T // tm,),
        in_specs=[pl.BlockSpec((tm, D), row), pl.BlockSpec((tm, ATTN_WIDTH), row),
                  pl.BlockSpec((tm, CONV_DIM), row), pl.BlockSpec((tm, CONV_DIM), row),
                  pl.BlockSpec((tm, CONV_DIM), row),
                  pl.BlockSpec(conv_w.shape, const), pl.BlockSpec((1, CONV_DIM), const),
                  pl.BlockSpec(w_out.shape, const)],
        out_specs=(pl.BlockSpec((tm, D), row), pl.BlockSpec((tm, CONV_DIM), row)),
        out_shape=(jax.ShapeDtypeStruct((T, D), F32), jax.ShapeDtypeStruct((T, CONV_DIM), BF16)),
        scratch_shapes=[pltpu.VMEM((SUBLANES, CONV_DIM), F32)],
        compiler_params=_params("arbitrary"),
    )(*_hbm(x, attn_n, u, gb, gc, conv_w, gain, w_out))


def _mixout_bwd(dy, attn_n, conv_n, w_out, after, name, tm):
    T, D = dy.shape
    W = ATTN_WIDTH + CONV_DIM
    nt = T // tm

    def body(dy_ref, an_ref, cn_ref, w_ref, after_ref, dm_ref, dw_ref, dw_sc):
        i = pl.program_id(0)

        @pl.when(i == 0)
        def _():
            dw_sc[...] = jnp.zeros_like(dw_sc)

        dyb = dy_ref[...].astype(BF16)
        dm_ref[...] = _dot_nt(dyb, w_ref[...]).astype(BF16)
        dw_sc[0:ATTN_WIDTH, :] += _dot_tn(an_ref[...], dyb)
        dw_sc[ATTN_WIDTH:W, :] += _dot_tn(cn_ref[...], dyb)

        @pl.when(i == nt - 1)
        def _():
            dw_ref[...] = dw_sc[...].astype(BF16)

    row = lambda i: (i, 0)
    const = lambda i: (0, 0)
    return pl.pallas_call(
        body, name=name, grid=(nt,),
        in_specs=[pl.BlockSpec((tm, D), row), pl.BlockSpec((tm, ATTN_WIDTH), row),
                  pl.BlockSpec((tm, CONV_DIM), row), pl.BlockSpec(w_out.shape, const),
                  pl.BlockSpec(memory_space=pl.ANY)],
        out_specs=(pl.BlockSpec((tm, W), row), pl.BlockSpec((W, D), const)),
        out_shape=(jax.ShapeDtypeStruct((T, W), BF16), jax.ShapeDtypeStruct((W, D), BF16)),
        scratch_shapes=[pltpu.VMEM((W, D), F32)],
        compiler_params=_params("arbitrary"),
    )(*_hbm(dy, attn_n, conv_n, w_out, after))


def _conv_bwd(dmixed, u, gb, gc, conv_w, gain, name, tc):
    T = u.shape[0]
    nt = T // tc
    per8 = tc // SUBLANES

    def body(dm_ref, u_ref, b_ref, c_ref, ut_ref, ct_ref, cw_ref, g_ref,
             du_ref, db_ref, dc_ref, dcw_ref, dgain_ref, head_sc):
        i = pl.program_id(0)

        @pl.when(i == 0)
        def _():
            head_sc[...] = jnp.zeros_like(head_sc)
            dcw_ref[...] = jnp.zeros_like(dcw_ref)
            dgain_ref[...] = jnp.zeros_like(dgain_ref)

        uu = u_ref[...].astype(F32)
        cc = c_ref[...].astype(F32)
        bb = b_ref[...].astype(F32)
        cu = cc * uu
        tail = jnp.where(i == nt - 1, 0.0, ct_ref[...].astype(F32) * ut_ref[...].astype(F32))
        s1, s2 = _shift_down(cu, tail)
        w0, w1, w2 = cw_ref[0:1, :], cw_ref[1:2, :], cw_ref[2:3, :]
        pre = w0 * s2 + w1 * s1 + w2 * cu
        dconv, dgain = _rms_bwd(dm_ref[...].astype(F32), bb * pre, g_ref[...])
        dgain_ref[...] += dgain
        db_ref[...] = (dconv * pre).astype(BF16)
        dpre = dconv * bb
        dcw_ref[0:1, :] += jnp.sum(dpre * s2, axis=0, keepdims=True)
        dcw_ref[1:2, :] += jnp.sum(dpre * s1, axis=0, keepdims=True)
        dcw_ref[2:3, :] += jnp.sum(dpre * cu, axis=0, keepdims=True)
        n1, n2 = _shift_up(dpre, head_sc[...])
        head_sc[...] = dpre[0:SUBLANES, :]
        dcu = w2 * dpre + w1 * n1 + w0 * n2
        du_ref[...] = (dcu * cc).astype(BF16)
        dc_ref[...] = (dcu * uu).astype(BF16)

    rev = lambda i: (nt - 1 - i, 0)
    rev_right = lambda i: (nt - 1 - i, 1)
    tail_map = lambda i: (jnp.maximum((nt - 1 - i) * per8 - 1, 0), 0)
    const = lambda i: (0, 0)
    return pl.pallas_call(
        body, name=name, grid=(nt,),
        in_specs=[pl.BlockSpec((tc, CONV_DIM), rev_right),
                  pl.BlockSpec((tc, CONV_DIM), rev), pl.BlockSpec((tc, CONV_DIM), rev),
                  pl.BlockSpec((tc, CONV_DIM), rev),
                  pl.BlockSpec((SUBLANES, CONV_DIM), tail_map), pl.BlockSpec((SUBLANES, CONV_DIM), tail_map),
                  pl.BlockSpec(conv_w.shape, const), pl.BlockSpec((1, CONV_DIM), const)],
        out_specs=(pl.BlockSpec((tc, CONV_DIM), rev), pl.BlockSpec((tc, CONV_DIM), rev),
                   pl.BlockSpec((tc, CONV_DIM), rev),
                   pl.BlockSpec((SUBLANES, CONV_DIM), const), pl.BlockSpec((1, CONV_DIM), const)),
        out_shape=(jax.ShapeDtypeStruct((T, CONV_DIM), BF16), jax.ShapeDtypeStruct((T, CONV_DIM), BF16),
                   jax.ShapeDtypeStruct((T, CONV_DIM), BF16),
                   jax.ShapeDtypeStruct((SUBLANES, CONV_DIM), F32), jax.ShapeDtypeStruct((1, CONV_DIM), F32)),
        scratch_shapes=[pltpu.VMEM((SUBLANES, CONV_DIM), F32)],
        compiler_params=_params("arbitrary"),
    )(*_hbm(dmixed, u, gb, gc, u, gc, conv_w, gain))


def _mixin_bwd(dy, x, gain, dz, w_in_t, name, tm):
    T, D = x.shape
    nz = len(dz)

    def body(dy_ref, x_ref, g_ref, *rest):
        dz_refs, wt_ref, dx_ref, dgain_ref = rest[:nz], rest[nz], rest[nz + 1], rest[nz + 2]

        @pl.when(pl.program_id(0) == 0)
        def _():
            dgain_ref[...] = jnp.zeros_like(dgain_ref)

        dh = jnp.zeros((tm, D), F32)
        for r, lo, hi in zip(dz_refs, _Z_SPLITS[:-1], _Z_SPLITS[1:]):
            dh += _dot(r[...], wt_ref[lo:hi, :])
        dx, dgain = _rms_bwd(dh, x_ref[...], g_ref[...])
        dgain_ref[...] += dgain
        dx_ref[...] = dy_ref[...] + dx

    row = lambda i: (i, 0)
    const = lambda i: (0, 0)
    return pl.pallas_call(
        body, name=name, grid=(T // tm,),
        in_specs=[pl.BlockSpec((tm, D), row), pl.BlockSpec((tm, D), row), pl.BlockSpec((1, D), const)]
                 + [pl.BlockSpec((tm, a.shape[1]), row) for a in dz]
                 + [pl.BlockSpec(w_in_t.shape, const)],
        out_specs=(pl.BlockSpec((tm, D), row), pl.BlockSpec((1, D), const)),
        out_shape=(jax.ShapeDtypeStruct((T, D), F32), jax.ShapeDtypeStruct((1, D), F32)),
        compiler_params=_params("arbitrary"),
    )(*_hbm(dy, x, gain, *dz, w_in_t))


def _win_grad(dz, hn, name, tk):
    T, D = hn.shape
    nz = len(dz)
    nt = T // tk
    W = _Z_SPLITS[-1]

    def body(hn_ref, *rest):
        dz_refs, dw_ref, dw_sc = rest[:nz], rest[nz], rest[nz + 1]
        i = pl.program_id(0)

        @pl.when(i == 0)
        def _():
            dw_sc[...] = jnp.zeros_like(dw_sc)

        hn = hn_ref[...]
        for r, lo, hi in zip(dz_refs, _Z_SPLITS[:-1], _Z_SPLITS[1:]):
            dw_sc[lo:hi, :] += _dot_tn(r[...], hn)

        @pl.when(i == nt - 1)
        def _():
            dw_ref[...] = dw_sc[...].astype(BF16)

    row = lambda i: (i, 0)
    return pl.pallas_call(
        body, name=name, grid=(nt,),
        in_specs=[pl.BlockSpec((tk, D), row)] + [pl.BlockSpec((tk, a.shape[1]), row) for a in dz],
        out_specs=pl.BlockSpec((W, D), lambda i: (0, 0)),
        out_shape=jax.ShapeDtypeStruct((W, D), BF16),
        scratch_shapes=[pltpu.VMEM((W, D), F32)],
        compiler_params=_params("arbitrary"),
    )(*_hbm(hn, *dz))


def _loss_head(x, target, gain, name, tm):
    T, D = x.shape

    def body(x_ref, t_ref, g_ref, dx_ref, dgain_ref, loss_ref):
        @pl.when(pl.program_id(0) == 0)
        def _():
            dgain_ref[...] = jnp.zeros_like(dgain_ref)
            loss_ref[...] = jnp.zeros_like(loss_ref)

        xv = x_ref[...]
        gain_v = g_ref[...]
        _, xhat = _rms_stats(xv)
        err = xhat * gain_v - t_ref[...]
        part = 0.5 * jnp.sum(jnp.mean(err * err, axis=-1, keepdims=True), axis=0, keepdims=True)
        loss_ref[...] += part
        dx, dgain = _rms_bwd(err * (1.0 / D), xv, gain_v)
        dgain_ref[...] += dgain
        dx_ref[...] = dx

    row = lambda i: (i, 0)
    const = lambda i: (0, 0)
    return pl.pallas_call(
        body, name=name, grid=(T // tm,),
        in_specs=[pl.BlockSpec((tm, D), row), pl.BlockSpec((tm, D), row), pl.BlockSpec((1, D), const)],
        out_specs=(pl.BlockSpec((tm, D), row), pl.BlockSpec((1, D), const),
                   pl.BlockSpec((SUBLANES, 128), const)),
        out_shape=(jax.ShapeDtypeStruct((T, D), F32), jax.ShapeDtypeStruct((1, D), F32),
                   jax.ShapeDtypeStruct((SUBLANES, 128), F32)),
        compiler_params=_params("arbitrary"),
    )(*_hbm(x, target, gain))


def _adamw(parts, w, m, v, name, tr):
    P = parts.shape[0]
    R, C = w.shape

    def body(p_ref, w_ref, m_ref, v_ref, g_ref, d_ref, nm_ref, nv_ref):
        g = p_ref[0].astype(F32)
        for d in range(1, P):
            g = g + p_ref[d].astype(F32)
        nm = ADAM_B1 * m_ref[...] + (1.0 - ADAM_B1) * g
        nv = ADAM_B2 * v_ref[...] + (1.0 - ADAM_B2) * (g * g)
        m_hat = nm / (1.0 - ADAM_B1 ** ADAM_STEP)
        v_hat = nv / (1.0 - ADAM_B2 ** ADAM_STEP)
        g_ref[...] = g
        nm_ref[...] = nm
        nv_ref[...] = nv
        d_ref[...] = -ADAM_LR * (m_hat / (jnp.sqrt(v_hat) + ADAM_EPS) + ADAM_WD * w_ref[...])

    row = lambda i: (i, 0)
    spec = pl.BlockSpec((tr, C), row)
    shp = jax.ShapeDtypeStruct((R, C), F32)
    return pl.pallas_call(
        body, name=name, grid=(R // tr,),
        in_specs=[pl.BlockSpec((P, tr, C), lambda i: (0, i, 0)), spec, spec, spec],
        out_specs=(spec, spec, spec, spec),
        out_shape=(shp, shp, shp, shp),
        compiler_params=_params("arbitrary"),
    )(*_hbm(parts, w, m, v))


def _columns_of_blocks(g):
    n, R, w = g.shape
    return g.transpose(1, 0, 2).reshape(R, n * w)


def _pad_row(vec):
    vec = vec.reshape(1, -1)
    return jnp.pad(vec, ((0, 0), (0, PACK_COLS - vec.shape[1])))


def kernel(x, rel_bias_table, ffn1_norm, ffn1_w_gate, ffn1_w_up, ffn1_w_down, mix_norm, w_in, conv_w, attn_sinks, attn_out_norm, conv_out_norm, w_out, ffn2_norm, ffn2_w_gate, ffn2_w_up, ffn2_w_down, final_norm, loss_target, m_rel_bias_table, m_ffn1_norm, m_ffn1_w_gate, m_ffn1_w_up, m_ffn1_w_down, m_mix_norm, m_w_in, m_conv_w, m_attn_sinks, m_attn_out_norm, m_conv_out_norm, m_w_out, m_ffn2_norm, m_ffn2_w_gate, m_ffn2_w_up, m_ffn2_w_down, m_final_norm, v_rel_bias_table, v_ffn1_norm, v_ffn1_w_gate, v_ffn1_w_up, v_ffn1_w_down, v_mix_norm, v_w_in, v_conv_w, v_attn_sinks, v_attn_out_norm, v_conv_out_norm, v_w_out, v_ffn2_norm, v_ffn2_w_gate, v_ffn2_w_up, v_ffn2_w_down, v_final_norm):
    T, D = x.shape[1], x.shape[2]
    x0 = x[0]
    target = loss_target[0]
    tm = min(TM_FFN, T)
    tm_bwd = min(TM_FFN_BWD, T)
    tm_mix = min(TM_MIX, T)
    tk = min(TK_WGRAD, T)
    tf = TF_FFN
    nblk = min(ATTN_BLOCKS, T // BLOCK)
    me = 4 * lax.axis_index("x") + 2 * lax.axis_index("y") + lax.axis_index("c")

    big = {
        "ffn1_w_gate": (ffn1_w_gate[0], m_ffn1_w_gate[0], v_ffn1_w_gate[0], True),
        "ffn1_w_up": (ffn1_w_up[0], m_ffn1_w_up[0], v_ffn1_w_up[0], True),
        "ffn1_w_down": (ffn1_w_down[0], m_ffn1_w_down[0], v_ffn1_w_down[0], False),
        "w_in": (w_in[0], m_w_in[0], v_w_in[0], True),
        "w_out": (w_out[0], m_w_out[0], v_w_out[0], False),
        "ffn2_w_gate": (ffn2_w_gate[0], m_ffn2_w_gate[0], v_ffn2_w_gate[0], True),
        "ffn2_w_up": (ffn2_w_up[0], m_ffn2_w_up[0], v_ffn2_w_up[0], True),
        "ffn2_w_down": (ffn2_w_down[0], m_ffn2_w_down[0], v_ffn2_w_down[0], False),
    }

    def block_to_send(name):
        w, _, _, transposed = big[name]
        return (w.T if transposed else w).astype(BF16)

    names1 = ["ffn1_w_gate", "ffn1_w_up", "ffn1_w_down"]
    names_rest = ["w_in", "w_out", "ffn2_w_gate", "ffn2_w_up", "ffn2_w_down"]
    first = _gather_two_level([block_to_send(n) for n in names1], "gather_ffn1", ffn1_norm)
    wgt1, wut1, wd1 = [g.reshape(-1, D) for g in first]
    h_rest, token = _exchange_start([block_to_send(n) for n in names_rest[:2]] + [conv_w[0]]
                                    + [block_to_send(n) for n in names_rest[2:]], False,
                                    "gather_start_rest", wd1)

    wgu1 = _stack_gate_up(wgt1, wut1, tf)
    x1, xn1, gu1 = _ffn_fwd(x0, ffn1_norm, wgu1, wd1, token, "ffn1_fwd", tm, tf)
    mixw = _exchange_wait(h_rest[:3], False, "gather_wait_mix", x1)
    win_t = mixw[0].reshape(-1, D)
    wout = mixw[1].reshape(-1, D)
    cw = _columns_of_blocks(mixw[2])
    hn, q, k, v, u, gb, gc = _mixin_fwd(x1, mix_norm, win_t, "mixin_fwd", tm_mix)
    bucket = jnp.asarray(_bucket_table().T.copy())
    sinks = attn_sinks.reshape(-1)
    bias_t = _bias_build(rel_bias_table, bucket, "bias_build")
    attn_raw, attn_n = _attn_fwd(q, k, v, bias_t, sinks, attn_out_norm, "attn_fwd", nblk)
    x2, conv_n = _mixout_fwd(x1, attn_n, u, gb, gc, cw, conv_out_norm, wout, "mixout_fwd", tm_mix)
    wgt2, wut2, wd2 = [g.reshape(-1, D) for g in _exchange_wait(h_rest[3:], False, "gather_wait_ffn2", x2)]
    wgu2 = _stack_gate_up(wgt2, wut2, tf)
    x3, xn2, gu2 = _ffn_fwd(x2, ffn2_norm, wgu2, wd2, x2, "ffn2_fwd", tm, tf)

    def blocks(g):
        return g.reshape(N_DEV, -1, D)

    dx3, d_final, loss_part = _loss_head(x3, target, final_norm.reshape(1, D), "loss_head", tm_mix)
    dx2, d_ffn2_norm, dgu2, hh2, do2 = _ffn_bwd(
        dx3, x2, ffn2_norm, gu2, wgu2, wd2, dx3, "ffn2_bwd", tm_bwd, tf)
    d_wg2, d_wu2 = _unstack_gate_up(_tn_grad(dgu2, xn2, dx2, "ffn2_wgrad_gu", BM_WGRAD, tk), tf)
    d_wd2 = _tn_grad(hh2, do2, dx2, "ffn2_wgrad_down", BM_WGRAD, tk)
    handles2, token2 = _exchange_start([blocks(d_wg2), blocks(d_wu2), blocks(d_wd2)], True,
                                       "grads_start_ffn2", dx2)

    dmixed, d_wout = _mixout_bwd(dx2, attn_n, conv_n, wout, token2, "mixout_bwd", tm_mix)
    dq, dk, dv, dbias, dsink, d_attn_norm = _attn_bwd(
        dmixed, attn_raw, q, k, v, bias_t, sinks, attn_out_norm, "attn_bwd", nblk)
    du, dgb, dgc, d_cw, d_conv_norm = _conv_bwd(dmixed, u, gb, gc, cw, conv_out_norm, "conv_bwd", tm_mix)
    d_table = _bias_grad(dbias, bucket, "bias_grad")
    dz = [dq, dk, dv, du, dgb, dgc]
    dx1, d_mix_norm = _mixin_bwd(dx2, x1, mix_norm, dz, win_t, "mixin_bwd", tm_mix)
    d_win_t = _win_grad(dz, hn, "win_grad", min(TM_MIX, T))
    handles_mix, token_mix = _exchange_start([blocks(d_win_t), blocks(d_wout)], True, "grads_start_mix", d_table)

    dx0, d_ffn1_norm, dgu1, hh1, do1 = _ffn_bwd(
        dx1, x0, ffn1_norm, gu1, wgu1, wd1, token_mix, "ffn1_bwd", tm_bwd, tf)
    def pack(ffn1, mixn, ffn2, fin, attn_n_, conv_n_, sink_, extra, convw, table):
        rows = [_pad_row(ffn1), _pad_row(mixn), _pad_row(ffn2), _pad_row(fin),
                _pad_row(jnp.concatenate([attn_n_.reshape(-1), conv_n_.reshape(-1)])),
                _pad_row(sink_), _pad_row(extra),
                jnp.zeros((1, PACK_COLS), F32),
                jnp.pad(convw, ((0, 0), (0, PACK_COLS - convw.shape[1]))),
                _pad_row(table),
                jnp.zeros((PACK_ROWS - 12, PACK_COLS), F32)]
        return jnp.concatenate(rows, axis=0)

    def own_channels(a):
        full = jnp.zeros((a.shape[1], CONV_DIM), F32)
        return lax.dynamic_update_slice(full, a[0], (0, me * a.shape[2]))

    g_pack = pack(d_ffn1_norm, d_mix_norm, d_ffn2_norm, d_final, d_attn_norm, d_conv_norm,
                  dsink[:, 0], loss_part[0, :1], d_cw[:3], d_table[:, :N_Q_HEADS])
    zero1 = jnp.zeros((1,), F32)
    w_pack = pack(ffn1_norm, mix_norm, ffn2_norm, final_norm, attn_out_norm, conv_out_norm,
                  attn_sinks, zero1, own_channels(conv_w), rel_bias_table)
    m_pack = pack(m_ffn1_norm, m_mix_norm, m_ffn2_norm, m_final_norm, m_attn_out_norm, m_conv_out_norm,
                  m_attn_sinks, zero1, own_channels(m_conv_w), m_rel_bias_table)
    v_pack = pack(v_ffn1_norm, v_mix_norm, v_ffn2_norm, v_final_norm, v_attn_out_norm, v_conv_out_norm,
                  v_attn_sinks, zero1, own_channels(v_conv_w), v_rel_bias_table)
    (g_all,) = _exchange([g_pack], False, "gather_small", dx0)
    packs = _adamw(g_all, w_pack, m_pack, v_pack, "adamw_small", PACK_ROWS)

    d_wg1, d_wu1 = _unstack_gate_up(_tn_grad(dgu1, xn1, packs[0], "ffn1_wgrad_gu", BM_WGRAD, tk), tf)
    handles1_gu, token1 = _exchange_start([blocks(d_wg1), blocks(d_wu1)], True, "grads_start_ffn1_gu", dx0)
    d_wd1 = _tn_grad(hh1, do1, token1, "ffn1_wgrad_down", BM_WGRAD, tk)
    handles1_d, token1 = _exchange_start([blocks(d_wd1)], True, "grads_start_ffn1_down", token1)

    res = {}

    def update(names, parts):
        last = None
        for name, p in zip(names, parts):
            w, m_, v_, transposed = big[name]
            if transposed:
                w, m_, v_ = w.T, m_.T, v_.T
            new = _adamw(p, w, m_, v_, "adamw_" + name, _row_tile(w.shape[0], ADAM_ROWS))
            res[name] = tuple((a.T if transposed else a)[None] for a in new)
            last = new[0]
        return last

    parts2 = _exchange_wait(handles2, True, "grads_wait_ffn2", token1)
    done2 = update(["ffn2_w_gate", "ffn2_w_up", "ffn2_w_down"], parts2)
    parts_mix = _exchange_wait(handles_mix, True, "grads_wait_mix", done2)
    done_mix = update(["w_in", "w_out"], parts_mix)

    parts1_gu = _exchange_wait(handles1_gu, True, "grads_wait_ffn1_gu", done_mix)
    done1_gu = update(["ffn1_w_gate", "ffn1_w_up"], parts1_gu)
    parts1_d = _exchange_wait(handles1_d, True, "grads_wait_ffn1_down", done1_gu)
    update(["ffn1_w_down"], parts1_d)

    def unpack(pk):
        cwb = lax.dynamic_slice(pk[8:11, :CONV_DIM], (0, me * conv_w.shape[2]), (3, conv_w.shape[2]))
        return {
            "ffn1_norm": pk[0:1, :D], "mix_norm": pk[1:2, :D], "ffn2_norm": pk[2:3, :D],
            "final_norm": pk[3, :D],
            "attn_out_norm": pk[4:5, :ATTN_WIDTH], "conv_out_norm": pk[4:5, ATTN_WIDTH:ATTN_WIDTH + CONV_DIM],
            "attn_sinks": pk[5:6, :N_Q_HEADS],
            "conv_w": cwb[None],
            "rel_bias_table": pk[11, :NUM_BUCKETS * N_Q_HEADS].reshape(NUM_BUCKETS, N_Q_HEADS),
        }

    small = [unpack(pk) for pk in packs]
    loss = packs[0][6, 0]

    order = ["rel_bias_table", "ffn1_norm", "ffn1_w_gate", "ffn1_w_up", "ffn1_w_down", "mix_norm", "w_in",
             "conv_w", "attn_sinks", "attn_out_norm", "conv_out_norm", "w_out", "ffn2_norm",
             "ffn2_w_gate", "ffn2_w_up", "ffn2_w_down", "final_norm"]
    outs = [loss, dx0[None]]
    for kind in range(4):
        for name in order:
            outs.append(res[name][kind] if name in res else small[kind][name])
    return tuple(outs)
```

```python
import math

import numpy as np
import jax
import jax.numpy as jnp
from jax import lax
from jax.experimental import pallas as pl
from jax.experimental.pallas import tpu as pltpu

F32 = jnp.float32
BF16 = jnp.bfloat16

N_DEV = 8
EPS = 1e-6
HEAD_DIM = 64
N_Q_HEADS = 8
N_KV_HEADS = 2
GQA_GROUP = 4
ATTN_WIDTH = 512
KV_WIDTH = 128
CONV_DIM = 512
BLOCK = 128
WINDOW = 128
NUM_BUCKETS = 32
MAX_DISTANCE = 128
SCALE = HEAD_DIM ** -0.5
MASKED = -1e30
GROUP_ROWS = GQA_GROUP * BLOCK

ADAM_LR = 0.001
ADAM_B1 = 0.9
ADAM_B2 = 0.999
ADAM_EPS = 1e-08
ADAM_WD = 0.01
ADAM_STEP = 10

VMEM_LIMIT_BYTES = 40 * 1024 * 1024
SUBLANES = 8
PACK_ROWS = 16
PACK_COLS = 1024

TM_FFN = 1024
TM_FFN_BWD = 1024
TM_MIX = 512
TK_WGRAD = 1024
TF_FFN = 256
BM_WGRAD = 1408
ROW_GROUPS = 4
ATTN_BLOCKS = 4
ADAM_ROWS = 256


def _row_tile(rows, limit):
    best = rows
    for t in range(16, min(rows, limit) + 1, 16):
        if rows % t == 0:
            best = t
    return best


def _params(*sem):
    return pltpu.CompilerParams(dimension_semantics=sem, vmem_limit_bytes=VMEM_LIMIT_BYTES)


def _hbm(*arrays):
    return [pltpu.with_memory_space_constraint(a, pltpu.HBM) for a in arrays]


def _dot(a, b):
    return jnp.dot(a, b, preferred_element_type=F32)


def _dot_nt(a, b):
    return lax.dot_general(a, b, (((1,), (1,)), ((), ())), preferred_element_type=F32)


def _dot_tn(a, b):
    return lax.dot_general(a, b, (((0,), (0,)), ((), ())), preferred_element_type=F32)


def _sigmoid(g):
    return 0.5 * jnp.tanh(0.5 * g) + 0.5


def _rms_stats(x):
    inv = lax.rsqrt(jnp.mean(x * x, axis=-1, keepdims=True) + EPS)
    return inv, x * inv


def _rms_bwd(dy, x, gain):
    inv, xhat = _rms_stats(x)
    dgain = jnp.sum(dy * xhat, axis=0, keepdims=True)
    dxh = dy * gain
    dx = inv * (dxh - xhat * jnp.mean(dxh * xhat, axis=-1, keepdims=True))
    return dx, dgain


def _peer_list():
    x, y, c = lax.axis_index("x"), lax.axis_index("y"), lax.axis_index("c")
    peers = []
    for k in range(1, N_DEV):
        px = 1 - x if (k >> 2) & 1 else x
        py = 1 - y if (k >> 1) & 1 else y
        pc = 1 - c if k & 1 else c
        peers.append((px, py, pc))
    return 4 * x + 2 * y + c, peers


def _exchange(arrs, scatter, name, after):
    n = len(arrs)
    out_shape = []
    for a in arrs:
        shp = a.shape if scatter else (N_DEV,) + a.shape
        out_shape.append(jax.ShapeDtypeStruct(shp, a.dtype))

    def body(*refs):
        ins, outs = refs[:n], refs[n + 1:2 * n + 1]
        send_sems, recv_sems, local_sems = refs[2 * n + 1:]
        me, peers = _peer_list()
        started = []
        for a in range(n):
            own = ins[a].at[me] if scatter else ins[a]
            loc = pltpu.make_async_copy(own, outs[a].at[me], local_sems.at[a])
            loc.start()
            started.append(loc)
        sends = []
        for a in range(n):
            for k, (px, py, pc) in enumerate(peers):
                src = ins[a].at[4 * px + 2 * py + pc] if scatter else ins[a]
                cp = pltpu.make_async_remote_copy(
                    src_ref=src, dst_ref=outs[a].at[me],
                    send_sem=send_sems.at[a, k], recv_sem=recv_sems.at[a, k],
                    device_id=(px, py, pc), device_id_type=pl.DeviceIdType.MESH)
                cp.start()
                sends.append(cp)
        for a in range(n):
            for k, (px, py, pc) in enumerate(peers):
                landed = outs[a].at[4 * px + 2 * py + pc]
                pltpu.make_async_remote_copy(
                    src_ref=landed, dst_ref=landed,
                    send_sem=send_sems.at[a, k], recv_sem=recv_sems.at[a, k],
                    device_id=(px, py, pc), device_id_type=pl.DeviceIdType.MESH).wait_recv()
        for cp in sends:
            cp.wait_send()
        for loc in started:
            loc.wait()

    hbm = pl.BlockSpec(memory_space=pl.ANY)
    return pl.pallas_call(
        body, name=name, out_shape=tuple(out_shape),
        in_specs=[hbm] * (n + 1), out_specs=tuple([hbm] * n),
        scratch_shapes=[pltpu.SemaphoreType.DMA((n, N_DEV - 1)),
                        pltpu.SemaphoreType.DMA((n, N_DEV - 1)),
                        pltpu.SemaphoreType.DMA((n,))],
    )(*arrs, after)


def _gather_two_level(arrs, name, after):
    n = len(arrs)
    out_shape = tuple(jax.ShapeDtypeStruct((N_DEV,) + a.shape, a.dtype) for a in arrs)

    def body(*refs):
        ins, outs = refs[:n], refs[n + 1:2 * n + 1]
        send_sems, recv_sems, local_sems = refs[2 * n + 1:]
        x, y, c = lax.axis_index("x"), lax.axis_index("y"), lax.axis_index("c")
        me, sibling = (x, y, c), (x, y, 1 - c)
        chips = [(1 - x, y), (x, 1 - y), (1 - x, 1 - y)]

        def copy(a, k, block, to, src=None):
            slot = outs[a].at[4 * block[0] + 2 * block[1] + block[2]]
            return pltpu.make_async_remote_copy(
                src_ref=slot if src is None else src, dst_ref=slot,
                send_sem=send_sems.at[a, k], recv_sem=recv_sems.at[a, k],
                device_id=to, device_id_type=pl.DeviceIdType.MESH)

        started = []
        for a in range(n):
            loc = pltpu.make_async_copy(ins[a], outs[a].at[4 * x + 2 * y + c], local_sems.at[a])
            loc.start()
            started.append(loc)
        sends = []
        for a in range(n):
            sends.append(copy(a, 0, me, sibling, src=ins[a]))
            sends += [copy(a, 1 + j, me, (*chip, c), src=ins[a]) for j, chip in enumerate(chips)]
        for cp in sends:
            cp.start()
        for j, chip in enumerate(chips):
            for a in range(n):
                copy(a, 1 + j, (*chip, c), me).wait_recv()
                fwd = copy(a, 4 + j, (*chip, c), sibling)
                fwd.start()
                sends.append(fwd)
        for a in range(n):
            copy(a, 0, sibling, me).wait_recv()
            for j, chip in enumerate(chips):
                copy(a, 4 + j, (*chip, 1 - c), me).wait_recv()
        for cp in sends:
            cp.wait_send()
        for loc in started:
            loc.wait()

    hbm = pl.BlockSpec(memory_space=pl.ANY)
    return pl.pallas_call(
        body, name=name, out_shape=out_shape,
        in_specs=[hbm] * (n + 1), out_specs=tuple([hbm] * n),
        scratch_shapes=[pltpu.SemaphoreType.DMA((n, N_DEV - 1)),
                        pltpu.SemaphoreType.DMA((n, N_DEV - 1)),
                        pltpu.SemaphoreType.DMA((n,))],
    )(*arrs, after)


_HBM = pl.BlockSpec(memory_space=pltpu.HBM)
_SEM = pl.BlockSpec(memory_space=pltpu.SEMAPHORE)
_EFFECT = pltpu.SideEffectType.DATAFLOW_SIDE_EFFECTING


def _split_copies(srcs, lands, send_sems, recv_sems, scatter):
    me, peers = _peer_list()
    copies = []
    for a in range(len(srcs)):
        for k, (px, py, pc) in enumerate(peers):
            src = srcs[a].at[4 * px + 2 * py + pc] if scatter else srcs[a]
            copies.append(pltpu.make_async_remote_copy(
                src_ref=src, dst_ref=lands[a].at[me],
                send_sem=send_sems[a].at[k], recv_sem=recv_sems[a].at[k],
                device_id=(px, py, pc), device_id_type=pl.DeviceIdType.MESH))
    return copies


def _exchange_start(arrs, scatter, name, after):
    n = len(arrs)
    me = 4 * lax.axis_index("x") + 2 * lax.axis_index("y") + lax.axis_index("c")
    lands = []
    for a in arrs:
        own = lax.dynamic_index_in_dim(a, me, 0, keepdims=True) if scatter else a[None]
        shp = a.shape if scatter else (N_DEV,) + a.shape
        lands.append(lax.dynamic_update_slice(lax.empty(shp, a.dtype), own, (me,) + (0,) * (len(shp) - 1)))

    def body(*refs):
        srcs, lnds = refs[:n], refs[n:2 * n]
        outs = refs[2 * n + 1:]
        send_sems, recv_sems = outs[:n], outs[n:2 * n]
        token = outs[4 * n]
        for cp in _split_copies(srcs, lnds, send_sems, recv_sems, scatter):
            cp.start()
        token[...] = jnp.zeros_like(token)

    sem = pltpu.SemaphoreType.DMA((N_DEV - 1,))
    out_shape = ([sem] * (2 * n) + [pltpu.HBM(a.shape, a.dtype) for a in arrs]
                 + [pltpu.HBM(l.shape, l.dtype) for l in lands] + [jax.ShapeDtypeStruct((SUBLANES, 128), F32)])
    res = pl.pallas_call(
        body, name=name, out_shape=tuple(out_shape),
        in_specs=[_HBM] * (2 * n) + [pl.BlockSpec(memory_space=pl.ANY)],
        out_specs=tuple([_SEM] * (2 * n) + [_HBM] * (2 * n) + [pl.BlockSpec(memory_space=pltpu.VMEM)]),
        input_output_aliases={i: 2 * n + i for i in range(2 * n)},
        compiler_params=pltpu.CompilerParams(has_side_effects=_EFFECT),
    )(*[pltpu.with_memory_space_constraint(a, pltpu.HBM) for a in arrs],
      *[pltpu.with_memory_space_constraint(l, pltpu.HBM) for l in lands], after)
    handles = [(res[2 * n + a], res[3 * n + a], res[a], res[n + a]) for a in range(n)]
    return handles, res[4 * n]


def _exchange_wait(handles, scatter, name, after):
    n = len(handles)

    def body(*refs):
        srcs, lnds = refs[:n], refs[n:2 * n]
        send_sems, recv_sems = refs[2 * n:3 * n], refs[3 * n:4 * n]
        for cp in _split_copies(srcs, lnds, send_sems, recv_sems, scatter):
            cp.wait_send()
            cp.wait_recv()

    srcs = [h[0] for h in handles]
    lands = [h[1] for h in handles]
    res = pl.pallas_call(
        body, name=name,
        out_shape=tuple([pltpu.HBM(a.shape, a.dtype) for a in srcs] + [pltpu.HBM(l.shape, l.dtype) for l in lands]),
        in_specs=[_HBM] * (2 * n) + [_SEM] * (2 * n) + [pl.BlockSpec(memory_space=pl.ANY)],
        out_specs=tuple([_HBM] * (2 * n)),
        input_output_aliases={i: i for i in range(2 * n)},
        compiler_params=pltpu.CompilerParams(has_side_effects=_EFFECT),
    )(*srcs, *lands, *[h[2] for h in handles], *[h[3] for h in handles], after)
    return list(res[n:])


def _stack_gate_up(wgt, wut, tf):
    F, D = wgt.shape
    return jnp.stack([wgt.reshape(F // tf, tf, D), wut.reshape(F // tf, tf, D)], axis=1).reshape(2 * F, D)


def _row_groups(tm):
    return [slice(r * (tm // ROW_GROUPS), (r + 1) * (tm // ROW_GROUPS)) for r in range(ROW_GROUPS)]


def _ffn_fwd(x, gain, wgu, wd, after, name, tm, tf):
    T, D = x.shape
    F = wd.shape[0]
    nj = F // tf

    def body(x_ref, g_ref, wgu_ref, wd_ref, after_ref, xo_ref, xn_ref, gu_ref, xn_sc, acc_sc):
        j = pl.program_id(1)

        @pl.when(j == 0)
        def _():
            _, xhat = _rms_stats(x_ref[...])
            xn = (xhat * g_ref[...]).astype(BF16)
            xn_sc[...] = xn
            xn_ref[...] = xn
            acc_sc[...] = jnp.zeros_like(acc_sc)

        groups = _row_groups(tm)
        gus = [_dot_nt(xn_sc[rows, :], wgu_ref[...]) for rows in groups]
        hs = []
        for rows, gu in zip(groups, gus):
            gu_ref[rows, :] = gu.astype(BF16)
            g, u = gu[:, :tf], gu[:, tf:]
            hs.append((g * _sigmoid(g) * u).astype(BF16))
        for rows, h in zip(groups, hs):
            acc_sc[rows, :] += _dot(h, wd_ref[...])

        @pl.when(j == nj - 1)
        def _():
            xo_ref[...] = x_ref[...] + 0.5 * acc_sc[...]

    return pl.pallas_call(
        body, name=name, grid=(T // tm, nj),
        in_specs=[pl.BlockSpec((tm, D), lambda i, j: (i, 0)),
                  pl.BlockSpec((1, D), lambda i, j: (0, 0)),
                  pl.BlockSpec((2 * tf, D), lambda i, j: (j, 0)),
                  pl.BlockSpec((tf, D), lambda i, j: (j, 0)),
                  pl.BlockSpec(memory_space=pl.ANY)],
        out_specs=(pl.BlockSpec((tm, D), lambda i, j: (i, 0)),
                   pl.BlockSpec((tm, D), lambda i, j: (i, 0)),
                   pl.BlockSpec((tm, 2 * tf), lambda i, j: (i, j))),
        out_shape=(jax.ShapeDtypeStruct((T, D), F32), jax.ShapeDtypeStruct((T, D), BF16),
                   jax.ShapeDtypeStruct((T, 2 * F), BF16)),
        scratch_shapes=[pltpu.VMEM((tm, D), BF16), pltpu.VMEM((tm, D), F32)],
        compiler_params=_params("arbitrary", "arbitrary"),
    )(*_hbm(x, gain, wgu, wd, after))


def _ffn_bwd(dy, x, gain, gu, wgu, wd, after, name, tm, tf):
    T, D = x.shape
    F = wd.shape[0]
    nj = F // tf

    def body(dy_ref, x_ref, g_ref, gu_ref, wgu_ref, wd_ref, after_ref,
             dx_ref, dgain_ref, dg_ref, du_ref, hh_ref, do_ref, do_sc, acc_sc, dgu_sc):
        i, j = pl.program_id(0), pl.program_id(1)

        @pl.when((i == 0) & (j == 0))
        def _():
            dgain_ref[...] = jnp.zeros_like(dgain_ref)

        @pl.when(j == 0)
        def _():
            do = (0.5 * dy_ref[...]).astype(BF16)
            do_sc[...] = do
            do_ref[...] = do
            acc_sc[...] = jnp.zeros_like(acc_sc)

        groups = _row_groups(tm)
        dhs = [_dot_nt(do_sc[rows, :], wd_ref[...]) for rows in groups]
        for rows, dh in zip(groups, dhs):
            g = gu_ref[rows, :tf].astype(F32)
            u = gu_ref[rows, tf:].astype(F32)
            sig = _sigmoid(g)
            s = g * sig
            dg = (dh * u * (sig + s * (1.0 - sig))).astype(BF16)
            du = (dh * s).astype(BF16)
            dg_ref[rows, :] = dg
            du_ref[rows, :] = du
            dgu_sc[rows, :tf] = dg
            dgu_sc[rows, tf:] = du
            hh_ref[rows, :] = (s * u).astype(BF16)
        for rows in groups:
            acc_sc[rows, :] += _dot(dgu_sc[rows, :], wgu_ref[...])

        @pl.when(j == nj - 1)
        def _():
            dx, dgain = _rms_bwd(acc_sc[...], x_ref[...], g_ref[...])
            dgain_ref[...] += dgain
            dx_ref[...] = dy_ref[...] + dx

    tile = pl.BlockSpec((tm, D), lambda i, j: (i, 0), pipeline_mode=pl.Buffered(1))
    return pl.pallas_call(
        body, name=name, grid=(T // tm, nj),
        in_specs=[tile, tile,
                  pl.BlockSpec((1, D), lambda i, j: (0, 0)),
                  pl.BlockSpec((tm, 2 * tf), lambda i, j: (i, j)),
                  pl.BlockSpec((2 * tf, D), lambda i, j: (j, 0)),
                  pl.BlockSpec((tf, D), lambda i, j: (j, 0)),
                  pl.BlockSpec(memory_space=pl.ANY)],
        out_specs=(tile,
                   pl.BlockSpec((1, D), lambda i, j: (0, 0)),
                   pl.BlockSpec((tm, tf), lambda i, j: (i, j)),
                   pl.BlockSpec((tm, tf), lambda i, j: (i, j)),
                   pl.BlockSpec((tm, tf), lambda i, j: (i, j)),
                   pl.BlockSpec((tm, D), lambda i, j: (i, 0))),
        out_shape=(jax.ShapeDtypeStruct((T, D), F32), jax.ShapeDtypeStruct((1, D), F32),
                   jax.ShapeDtypeStruct((T, F), BF16), jax.ShapeDtypeStruct((T, F), BF16),
                   jax.ShapeDtypeStruct((T, F), BF16), jax.ShapeDtypeStruct((T, D), BF16)),
        scratch_shapes=[pltpu.VMEM((tm, D), BF16), pltpu.VMEM((tm, D), F32), pltpu.VMEM((tm, 2 * tf), BF16)],
        compiler_params=_params("arbitrary", "arbitrary"),
    )(*_hbm(dy, x, gain, gu, wgu, wd, after))


def _tn_grad(a, b, after, name, bm, tk):
    T, M = a.shape
    D = b.shape[1]
    nk = T // tk

    def body(a_ref, b_ref, after_ref, o_ref, acc_sc):
        k = pl.program_id(1)

        @pl.when(k == 0)
        def _():
            acc_sc[...] = jnp.zeros_like(acc_sc)

        acc_sc[...] += _dot_tn(a_ref[...], b_ref[...])

        @pl.when(k == nk - 1)
        def _():
            o_ref[...] = acc_sc[...].astype(BF16)

    return pl.pallas_call(
        body, name=name, grid=(M // bm, nk),
        in_specs=[pl.BlockSpec((tk, bm), lambda i, k: (k, i)), pl.BlockSpec((tk, D), lambda i, k: (k, 0)),
                  pl.BlockSpec(memory_space=pl.ANY)],
        out_specs=pl.BlockSpec((bm, D), lambda i, k: (i, 0)),
        out_shape=jax.ShapeDtypeStruct((M, D), BF16),
        scratch_shapes=[pltpu.VMEM((bm, D), F32)],
        compiler_params=_params("arbitrary", "arbitrary"),
    )(*_hbm(a, b, after))


_Z_SPLITS = (0, 512, 640, 768, 1280, 1792, 2304)


def _mixin_fwd(x, gain, w_in_t, name, tm):
    T, D = x.shape
    widths = [b - a for a, b in zip(_Z_SPLITS[:-1], _Z_SPLITS[1:])]

    def body(x_ref, g_ref, w_ref, hn_ref, *outs):
        _, xhat = _rms_stats(x_ref[...])
        hn = (xhat * g_ref[...]).astype(BF16)
        hn_ref[...] = hn
        for o_ref, lo, hi in zip(outs, _Z_SPLITS[:-1], _Z_SPLITS[1:]):
            o_ref[...] = _dot_nt(hn, w_ref[lo:hi, :]).astype(BF16)

    return pl.pallas_call(
        body, name=name, grid=(T // tm,),
        in_specs=[pl.BlockSpec((tm, D), lambda i: (i, 0)),
                  pl.BlockSpec((1, D), lambda i: (0, 0)),
                  pl.BlockSpec(w_in_t.shape, lambda i: (0, 0))],
        out_specs=tuple([pl.BlockSpec((tm, D), lambda i: (i, 0))]
                        + [pl.BlockSpec((tm, w), lambda i: (i, 0)) for w in widths]),
        out_shape=tuple([jax.ShapeDtypeStruct((T, D), BF16)]
                        + [jax.ShapeDtypeStruct((T, w), BF16) for w in widths]),
        compiler_params=_params("arbitrary"),
    )(*_hbm(x, gain, w_in_t))


def _bucket_table():
    qi = np.arange(BLOCK, dtype=np.int32)[:, None]
    kj = np.arange(2 * BLOCK, dtype=np.int32)[None, :]
    dist = qi + BLOCK - kj
    n = np.maximum(dist, 0)
    max_exact = NUM_BUCKETS // 2
    large = max_exact + (np.log(np.maximum(n, 1).astype(np.float32) / max_exact)
                         / math.log(MAX_DISTANCE / max_exact)
                         * (NUM_BUCKETS - max_exact)).astype(np.int32)
    large = np.minimum(large, NUM_BUCKETS - 1)
    bucket = np.where(n < max_exact, n, large).astype(np.int32)
    valid = (dist >= 0) & (dist < WINDOW)
    return np.where(valid, bucket, -1).astype(np.int32)


def _bias_build(table, bucket, name):
    def body(t_ref, b_ref, o_ref):
        bk = b_ref[...]
        for h in range(N_Q_HEADS):
            def step(b, acc):
                return jnp.where(bk == b, t_ref[b, h], acc)
            o_ref[h] = lax.fori_loop(0, NUM_BUCKETS, step, jnp.full(bk.shape, MASKED, F32))

    return pl.pallas_call(
        body, name=name,
        in_specs=[pl.BlockSpec(memory_space=pltpu.SMEM), pl.BlockSpec(memory_space=pltpu.VMEM)],
        out_specs=pl.BlockSpec(memory_space=pltpu.VMEM),
        out_shape=jax.ShapeDtypeStruct((N_Q_HEADS,) + bucket.shape, F32),
    )(table, bucket)


def _bias_grad(dbias, bucket, name):
    def body(d_ref, b_ref, o_ref):
        bk = b_ref[...]
        row = lax.broadcasted_iota(jnp.int32, o_ref.shape, 0)
        lane = lax.broadcasted_iota(jnp.int32, o_ref.shape, 1)
        res = jnp.zeros(o_ref.shape, F32)
        for h in range(N_Q_HEADS):
            d = d_ref[h]

            def step(b, acc):
                part = jnp.sum(jnp.where(bk == b, d, 0.0), axis=0, keepdims=True)
                return jnp.where(row == b, part, acc)
            per_lane = lax.fori_loop(0, NUM_BUCKETS, step, jnp.zeros(o_ref.shape, F32))
            res = jnp.where(lane == h, jnp.sum(per_lane, axis=1, keepdims=True), res)
        o_ref[...] = res

    return pl.pallas_call(
        body, name=name,
        in_specs=[pl.BlockSpec(memory_space=pltpu.VMEM), pl.BlockSpec(memory_space=pltpu.VMEM)],
        out_specs=pl.BlockSpec(memory_space=pltpu.VMEM),
        out_shape=jax.ShapeDtypeStruct((NUM_BUCKETS, 128), F32),
    )(*_hbm(dbias, bucket))


def _head_cols(h):
    return slice(h * HEAD_DIM, (h + 1) * HEAD_DIM)


def _stack_heads(ref, r0, g, dtype):
    return jnp.concatenate(
        [ref[pl.ds(r0, BLOCK), _head_cols(GQA_GROUP * g + j)].astype(dtype) for j in range(GQA_GROUP)], axis=0)


def _unstack_heads(ref, r0, g, val):
    for j in range(GQA_GROUP):
        ref[pl.ds(r0, BLOCK), _head_cols(GQA_GROUP * g + j)] = val[j * BLOCK:(j + 1) * BLOCK, :]


def _head_lanes(h):
    return slice(h * BLOCK, (h + 1) * BLOCK)


def _group_lanes(g):
    return slice(g * GROUP_ROWS, (g + 1) * GROUP_ROWS)


def _head_softmax(st, bias_t, sink, no_prev):
    s = st * SCALE + bias_t
    row = lax.broadcasted_iota(jnp.int32, s.shape, 0)
    s = jnp.where(no_prev & (row < BLOCK), MASKED, s)
    m = jnp.maximum(jnp.max(s, axis=0, keepdims=True), sink)
    p = jnp.exp(s - m)
    ps = jnp.exp(sink - m)
    r = 1.0 / (jnp.sum(p, axis=0, keepdims=True) + ps)
    return p * r, ps * r


def _load_band(kf_sc, vf_sc, kp_ref, kc_ref, vp_ref, vc_ref, tq):
    kf_sc[0:BLOCK, :] = kp_ref[...]
    kf_sc[BLOCK:BLOCK + tq, :] = kc_ref[...]
    vf_sc[0:BLOCK, :] = vp_ref[...]
    vf_sc[BLOCK:BLOCK + tq, :] = vc_ref[...]


def _attn_fwd(q, k, v, bias_t, sinks, gain, name, nblk):
    T = q.shape[0]
    tq = nblk * BLOCK

    def body(sink_ref, q_ref, kc_ref, kp_ref, vc_ref, vp_ref, bias_ref, g_ref, raw_ref, nrm_ref,
             kf_sc, vf_sc, o_sc, st_sc, pt_sc):
        i = pl.program_id(0)
        _load_band(kf_sc, vf_sc, kp_ref, kc_ref, vp_ref, vc_ref, tq)

        def block(b, carry):
            r0 = pl.multiple_of(b * BLOCK, BLOCK)
            no_prev = (i == 0) & (b == 0)
            for g in range(N_KV_HEADS):
                kb = kf_sc[pl.ds(r0, 2 * BLOCK), _head_cols(g)]
                st_sc[:, _group_lanes(g)] = _dot_nt(kb, _stack_heads(q_ref, r0, g, BF16))
            for h in range(N_Q_HEADS):
                p, _ = _head_softmax(st_sc[:, _head_lanes(h)], bias_ref[h], sink_ref[h], no_prev)
                pt_sc[:, _head_lanes(h)] = p.astype(BF16)
            for g in range(N_KV_HEADS):
                vb = vf_sc[pl.ds(r0, 2 * BLOCK), _head_cols(g)]
                _unstack_heads(o_sc, r0, g, _dot_tn(pt_sc[:, _group_lanes(g)], vb))
            return carry

        lax.fori_loop(0, nblk, block, 0)
        o = o_sc[...]
        raw_ref[...] = o.astype(BF16)
        _, ohat = _rms_stats(o)
        nrm_ref[...] = (ohat * g_ref[...]).astype(BF16)

    cur = lambda i: (i, 0)
    prev = lambda i: (jnp.maximum(i * nblk - 1, 0), 0)
    lanes = N_Q_HEADS * BLOCK
    return pl.pallas_call(
        body, name=name, grid=(T // tq,),
        in_specs=[pl.BlockSpec(memory_space=pltpu.SMEM),
                  pl.BlockSpec((tq, ATTN_WIDTH), cur),
                  pl.BlockSpec((tq, KV_WIDTH), cur), pl.BlockSpec((BLOCK, KV_WIDTH), prev),
                  pl.BlockSpec((tq, KV_WIDTH), cur), pl.BlockSpec((BLOCK, KV_WIDTH), prev),
                  pl.BlockSpec(bias_t.shape, lambda i: (0, 0, 0)),
                  pl.BlockSpec((1, ATTN_WIDTH), lambda i: (0, 0))],
        out_specs=(pl.BlockSpec((tq, ATTN_WIDTH), cur), pl.BlockSpec((tq, ATTN_WIDTH), cur)),
        out_shape=(jax.ShapeDtypeStruct((T, ATTN_WIDTH), BF16), jax.ShapeDtypeStruct((T, ATTN_WIDTH), BF16)),
        scratch_shapes=[pltpu.VMEM((tq + BLOCK, KV_WIDTH), BF16), pltpu.VMEM((tq + BLOCK, KV_WIDTH), BF16),
                        pltpu.VMEM((tq, ATTN_WIDTH), F32),
                        pltpu.VMEM((2 * BLOCK, lanes), F32), pltpu.VMEM((2 * BLOCK, lanes), BF16)],
        compiler_params=_params("arbitrary"),
    )(sinks, *_hbm(q, k, k, v, v, bias_t, gain))


def _attn_bwd(dmixed, raw, q, k, v, bias_t, sinks, gain, name, nblk):
    T = q.shape[0]
    tq = nblk * BLOCK
    nt = T // tq
    lanes = N_Q_HEADS * BLOCK

    def body(sink_ref, dm_ref, raw_ref, q_ref, kc_ref, kp_ref, vc_ref, vp_ref, bias_ref, g_ref,
             dq_ref, dk_ref, dv_ref, dbias_ref, dsink_ref, dgain_ref,
             do_sc, dq_sc, kf_sc, vf_sc, dkf_sc, dvf_sc, st_sc, dpt_sc, pt_sc, dst_sc, drow_sc,
             qs_sc, dos_sc, dsink_sc):
        i = pl.program_id(0)
        tile = nt - 1 - i

        @pl.when(i == 0)
        def _():
            dkf_sc[...] = jnp.zeros_like(dkf_sc)
            dvf_sc[...] = jnp.zeros_like(dvf_sc)
            dsink_sc[...] = jnp.zeros_like(dsink_sc)
            dbias_ref[...] = jnp.zeros_like(dbias_ref)
            dgain_ref[...] = jnp.zeros_like(dgain_ref)

        carry_k = dkf_sc[0:BLOCK, :]
        carry_v = dvf_sc[0:BLOCK, :]
        dkf_sc[0:tq, :] = jnp.zeros((tq, KV_WIDTH), F32)
        dvf_sc[0:tq, :] = jnp.zeros((tq, KV_WIDTH), F32)
        dkf_sc[tq:tq + BLOCK, :] = carry_k
        dvf_sc[tq:tq + BLOCK, :] = carry_v
        _load_band(kf_sc, vf_sc, kp_ref, kc_ref, vp_ref, vc_ref, tq)

        do, dgain = _rms_bwd(dm_ref[...].astype(F32), raw_ref[...].astype(F32), g_ref[...])
        dgain_ref[...] += dgain
        do_sc[...] = do
        ones = jnp.ones((SUBLANES, HEAD_DIM), BF16)

        def block(b, carry):
            r0 = pl.multiple_of(b * BLOCK, BLOCK)
            no_prev = (tile == 0) & (b == 0)
            for g in range(N_KV_HEADS):
                kb = kf_sc[pl.ds(r0, 2 * BLOCK), _head_cols(g)]
                vb = vf_sc[pl.ds(r0, 2 * BLOCK), _head_cols(g)]
                qg = _stack_heads(q_ref, r0, g, BF16)
                dog = _stack_heads(do_sc, r0, g, F32)
                prod = dog * _stack_heads(raw_ref, r0, g, F32)
                hi = prod.astype(BF16)
                lo = (prod - hi.astype(F32)).astype(BF16)
                drow_sc[:, _group_lanes(g)] = _dot_nt(ones, hi) + _dot_nt(ones, lo)
                dogb = dog.astype(BF16)
                qs_sc[g] = qg
                dos_sc[g] = dogb
                st_sc[:, _group_lanes(g)] = _dot_nt(kb, qg)
                dpt_sc[:, _group_lanes(g)] = _dot_nt(vb, dogb)
            for h in range(N_Q_HEADS):
                hl = _head_lanes(h)
                p, ps = _head_softmax(st_sc[:, hl], bias_ref[h], sink_ref[h], no_prev)
                rowdot = drow_sc[0:1, hl]
                ds = p * (dpt_sc[:, hl] - rowdot)
                dsink_sc[h:h + 1, :] += -(ps * rowdot)
                dbias_ref[h] += ds
                dst_sc[:, hl] = ds.astype(BF16)
                pt_sc[:, hl] = p.astype(BF16)
            for g in range(N_KV_HEADS):
                kb = kf_sc[pl.ds(r0, 2 * BLOCK), _head_cols(g)]
                dsg = dst_sc[:, _group_lanes(g)]
                _unstack_heads(dq_sc, r0, g, _dot_tn(dsg, kb) * SCALE)
                dkf_sc[pl.ds(r0, 2 * BLOCK), _head_cols(g)] += _dot(dsg, qs_sc[g]) * SCALE
                dvf_sc[pl.ds(r0, 2 * BLOCK), _head_cols(g)] += _dot(pt_sc[:, _group_lanes(g)], dos_sc[g])
            return carry

        lax.fori_loop(0, nblk, block, 0)
        dq_ref[...] = dq_sc[...].astype(BF16)
        dk_ref[...] = dkf_sc[BLOCK:BLOCK + tq, :].astype(BF16)
        dv_ref[...] = dvf_sc[BLOCK:BLOCK + tq, :].astype(BF16)

        @pl.when(i == nt - 1)
        def _():
            tot = jnp.sum(dsink_sc[...], axis=1, keepdims=True)
            dsink_ref[...] = jnp.broadcast_to(tot, dsink_ref.shape)

    cur = lambda i: (nt - 1 - i, 0)
    prev = lambda i: (jnp.maximum((nt - 1 - i) * nblk - 1, 0), 0)
    const2 = lambda i: (0, 0)
    const3 = lambda i: (0, 0, 0)
    return pl.pallas_call(
        body, name=name, grid=(nt,),
        in_specs=[pl.BlockSpec(memory_space=pltpu.SMEM),
                  pl.BlockSpec((tq, ATTN_WIDTH), cur),
                  pl.BlockSpec((tq, ATTN_WIDTH), cur),
                  pl.BlockSpec((tq, ATTN_WIDTH), cur),
                  pl.BlockSpec((tq, KV_WIDTH), cur), pl.BlockSpec((BLOCK, KV_WIDTH), prev),
                  pl.BlockSpec((tq, KV_WIDTH), cur), pl.BlockSpec((BLOCK, KV_WIDTH), prev),
                  pl.BlockSpec(bias_t.shape, const3),
                  pl.BlockSpec((1, ATTN_WIDTH), const2)],
        out_specs=(pl.BlockSpec((tq, ATTN_WIDTH), cur),
                   pl.BlockSpec((tq, KV_WIDTH), cur), pl.BlockSpec((tq, KV_WIDTH), cur),
                   pl.BlockSpec(bias_t.shape, const3),
                   pl.BlockSpec((N_Q_HEADS, 128), const2),
                   pl.BlockSpec((1, ATTN_WIDTH), const2)),
        out_shape=(jax.ShapeDtypeStruct((T, ATTN_WIDTH), BF16),
                   jax.ShapeDtypeStruct((T, KV_WIDTH), BF16), jax.ShapeDtypeStruct((T, KV_WIDTH), BF16),
                   jax.ShapeDtypeStruct(bias_t.shape, F32),
                   jax.ShapeDtypeStruct((N_Q_HEADS, 128), F32),
                   jax.ShapeDtypeStruct((1, ATTN_WIDTH), F32)),
        scratch_shapes=[pltpu.VMEM((tq, ATTN_WIDTH), F32), pltpu.VMEM((tq, ATTN_WIDTH), F32),
                        pltpu.VMEM((tq + BLOCK, KV_WIDTH), BF16), pltpu.VMEM((tq + BLOCK, KV_WIDTH), BF16),
                        pltpu.VMEM((tq + BLOCK, KV_WIDTH), F32), pltpu.VMEM((tq + BLOCK, KV_WIDTH), F32),
                        pltpu.VMEM((2 * BLOCK, lanes), F32), pltpu.VMEM((2 * BLOCK, lanes), F32),
                        pltpu.VMEM((2 * BLOCK, lanes), BF16), pltpu.VMEM((2 * BLOCK, lanes), BF16),
                        pltpu.VMEM((SUBLANES, lanes), F32),
                        pltpu.VMEM((N_KV_HEADS, GROUP_ROWS, HEAD_DIM), BF16),
                        pltpu.VMEM((N_KV_HEADS, GROUP_ROWS, HEAD_DIM), BF16),
                        pltpu.VMEM((N_Q_HEADS, 128), F32)],
        compiler_params=_params("arbitrary"),
    )(sinks, *_hbm(dmixed, raw, q, k, k, v, v, bias_t, gain))


def _shift_down(cu, tail):
    row = lax.broadcasted_iota(jnp.int32, cu.shape, 0)
    t6, t7 = tail[6:7, :], tail[7:8, :]
    s1 = jnp.where(row == 0, t7, pltpu.roll(cu, 1, 0))
    s2 = jnp.where(row == 0, t6, jnp.where(row == 1, t7, pltpu.roll(cu, 2, 0)))
    return s1, s2


def _shift_up(d, head):
    n = d.shape[0]
    row = lax.broadcasted_iota(jnp.int32, d.shape, 0)
    h0, h1 = head[0:1, :], head[1:2, :]
    s1 = jnp.where(row == n - 1, h0, pltpu.roll(d, n - 1, 0))
    s2 = jnp.where(row == n - 1, h1, jnp.where(row == n - 2, h0, pltpu.roll(d, n - 2, 0)))
    return s1, s2


def _mixout_fwd(x, attn_n, u, gb, gc, conv_w, gain, w_out, name, tm):
    T, D = x.shape

    def body(x_ref, an_ref, u_ref, b_ref, c_ref, cw_ref, g_ref, wo_ref, xo_ref, cn_ref, tail_sc):
        @pl.when(pl.program_id(0) == 0)
        def _():
            tail_sc[...] = jnp.zeros_like(tail_sc)

        cu = c_ref[...].astype(F32) * u_ref[...].astype(F32)
        s1, s2 = _shift_down(cu, tail_sc[...])
        tail_sc[...] = cu[tm - SUBLANES:tm, :]
        pre = cw_ref[0:1, :] * s2 + cw_ref[1:2, :] * s1 + cw_ref[2:3, :] * cu
        conv = b_ref[...].astype(F32) * pre
        _, chat = _rms_stats(conv)
        cn = (chat * g_ref[...]).astype(BF16)
        cn_ref[...] = cn
        xo_ref[...] = (x_ref[...] + _dot(an_ref[...], wo_ref[0:ATTN_WIDTH, :])
                       + _dot(cn, wo_ref[ATTN_WIDTH:ATTN_WIDTH + CONV_DIM, :]))

    row = lambda i: (i, 0)
    const = lambda i: (0, 0)
    return pl.pallas_call(
        body, name=name, grid=(T // tm,),
        in_specs=[pl.BlockSpec((tm, D), row), pl.BlockSpec((tm, ATTN_WIDTH), row),
                  pl.BlockSpec((tm, CONV_DIM), row), pl.BlockSpec((tm, CONV_DIM), row),
                  pl.BlockSpec((tm, CONV_DIM), row),
                  pl.BlockSpec(conv_w.shape, const), pl.BlockSpec((1, CONV_DIM), const),
                  pl.BlockSpec(w_out.shape, const)],
        out_specs=(pl.BlockSpec((tm, D), row), pl.BlockSpec((tm, CONV_DIM), row)),
        out_shape=(jax.ShapeDtypeStruct((T, D), F32), jax.ShapeDtypeStruct((T, CONV_DIM), BF16)),
        scratch_shapes=[pltpu.VMEM((SUBLANES, CONV_DIM), F32)],
        compiler_params=_params("arbitrary"),
    )(*_hbm(x, attn_n, u, gb, gc, conv_w, gain, w_out))


def _mixout_bwd(dy, attn_n, conv_n, w_out, after, name, tm):
    T, D = dy.shape
    W = ATTN_WIDTH + CONV_DIM
    nt = T // tm

    def body(dy_ref, an_ref, cn_ref, w_ref, after_ref, dm_ref, dw_ref, dw_sc):
        i = pl.program_id(0)

        @pl.when(i == 0)
        def _():
            dw_sc[...] = jnp.zeros_like(dw_sc)

        dyb = dy_ref[...].astype(BF16)
        dm_ref[...] = _dot_nt(dyb, w_ref[...]).astype(BF16)
        dw_sc[0:ATTN_WIDTH, :] += _dot_tn(an_ref[...], dyb)
        dw_sc[ATTN_WIDTH:W, :] += _dot_tn(cn_ref[...], dyb)

        @pl.when(i == nt - 1)
        def _():
            dw_ref[...] = dw_sc[...].astype(BF16)

    row = lambda i: (i, 0)
    const = lambda i: (0, 0)
    return pl.pallas_call(
        body, name=name, grid=(nt,),
        in_specs=[pl.BlockSpec((tm, D), row), pl.BlockSpec((tm, ATTN_WIDTH), row),
                  pl.BlockSpec((tm, CONV_DIM), row), pl.BlockSpec(w_out.shape, const),
                  pl.BlockSpec(memory_space=pl.ANY)],
        out_specs=(pl.BlockSpec((tm, W), row), pl.BlockSpec((W, D), const)),
        out_shape=(jax.ShapeDtypeStruct((T, W), BF16), jax.ShapeDtypeStruct((W, D), BF16)),
        scratch_shapes=[pltpu.VMEM((W, D), F32)],
        compiler_params=_params("arbitrary"),
    )(*_hbm(dy, attn_n, conv_n, w_out, after))


def _conv_bwd(dmixed, u, gb, gc, conv_w, gain, name, tc):
    T = u.shape[0]
    nt = T // tc
    per8 = tc // SUBLANES

    def body(dm_ref, u_ref, b_ref, c_ref, ut_ref, ct_ref, cw_ref, g_ref,
             du_ref, db_ref, dc_ref, dcw_ref, dgain_ref, head_sc):
        i = pl.program_id(0)

        @pl.when(i == 0)
        def _():
            head_sc[...] = jnp.zeros_like(head_sc)
            dcw_ref[...] = jnp.zeros_like(dcw_ref)
            dgain_ref[...] = jnp.zeros_like(dgain_ref)

        uu = u_ref[...].astype(F32)
        cc = c_ref[...].astype(F32)
        bb = b_ref[...].astype(F32)
        cu = cc * uu
        tail = jnp.where(i == nt - 1, 0.0, ct_ref[...].astype(F32) * ut_ref[...].astype(F32))
        s1, s2 = _shift_down(cu, tail)
        w0, w1, w2 = cw_ref[0:1, :], cw_ref[1:2, :], cw_ref[2:3, :]
        pre = w0 * s2 + w1 * s1 + w2 * cu
        dconv, dgain = _rms_bwd(dm_ref[...].astype(F32), bb * pre, g_ref[...])
        dgain_ref[...] += dgain
        db_ref[...] = (dconv * pre).astype(BF16)
        dpre = dconv * bb
        dcw_ref[0:1, :] += jnp.sum(dpre * s2, axis=0, keepdims=True)
        dcw_ref[1:2, :] += jnp.sum(dpre * s1, axis=0, keepdims=True)
        dcw_ref[2:3, :] += jnp.sum(dpre * cu, axis=0, keepdims=True)
        n1, n2 = _shift_up(dpre, head_sc[...])
        head_sc[...] = dpre[0:SUBLANES, :]
        dcu = w2 * dpre + w1 * n1 + w0 * n2
        du_ref[...] = (dcu * cc).astype(BF16)
        dc_ref[...] = (dcu * uu).astype(BF16)

    rev = lambda i: (nt - 1 - i, 0)
    rev_right = lambda i: (nt - 1 - i, 1)
    tail_map = lambda i: (jnp.maximum((nt - 1 - i) * per8 - 1, 0), 0)
    const = lambda i: (0, 0)
    return pl.pallas_call(
        body, name=name, grid=(nt,),
        in_specs=[pl.BlockSpec((tc, CONV_DIM), rev_right),
                  pl.BlockSpec((tc, CONV_DIM), rev), pl.BlockSpec((tc, CONV_DIM), rev),
                  pl.BlockSpec((tc, CONV_DIM), rev),
                  pl.BlockSpec((SUBLANES, CONV_DIM), tail_map), pl.BlockSpec((SUBLANES, CONV_DIM), tail_map),
                  pl.BlockSpec(conv_w.shape, const), pl.BlockSpec((1, CONV_DIM), const)],
        out_specs=(pl.BlockSpec((tc, CONV_DIM), rev), pl.BlockSpec((tc, CONV_DIM), rev),
                   pl.BlockSpec((tc, CONV_DIM), rev),
                   pl.BlockSpec((SUBLANES, CONV_DIM), const), pl.BlockSpec((1, CONV_DIM), const)),
        out_shape=(jax.ShapeDtypeStruct((T, CONV_DIM), BF16), jax.ShapeDtypeStruct((T, CONV_DIM), BF16),
                   jax.ShapeDtypeStruct((T, CONV_DIM), BF16),
                   jax.ShapeDtypeStruct((SUBLANES, CONV_DIM), F32), jax.ShapeDtypeStruct((1, CONV_DIM), F32)),
        scratch_shapes=[pltpu.VMEM((SUBLANES, CONV_DIM), F32)],
        compiler_params=_params("arbitrary"),
    )(*_hbm(dmixed, u, gb, gc, u, gc, conv_w, gain))


def _mixin_bwd(dy, x, gain, dz, w_in_t, name, tm):
    T, D = x.shape
    nz = len(dz)

    def body(dy_ref, x_ref, g_ref, *rest):
        dz_refs, wt_ref, dx_ref, dgain_ref = rest[:nz], rest[nz], rest[nz + 1], rest[nz + 2]

        @pl.when(pl.program_id(0) == 0)
        def _():
            dgain_ref[...] = jnp.zeros_like(dgain_ref)

        dh = jnp.zeros((tm, D), F32)
        for r, lo, hi in zip(dz_refs, _Z_SPLITS[:-1], _Z_SPLITS[1:]):
            dh += _dot(r[...], wt_ref[lo:hi, :])
        dx, dgain = _rms_bwd(dh, x_ref[...], g_ref[...])
        dgain_ref[...] += dgain
        dx_ref[...] = dy_ref[...] + dx

    row = lambda i: (i, 0)
    const = lambda i: (0, 0)
    return pl.pallas_call(
        body, name=name, grid=(T // tm,),
        in_specs=[pl.BlockSpec((tm, D), row), pl.BlockSpec((tm, D), row), pl.BlockSpec((1, D), const)]
                 + [pl.BlockSpec((tm, a.shape[1]), row) for a in dz]
                 + [pl.BlockSpec(w_in_t.shape, const)],
        out_specs=(pl.BlockSpec((tm, D), row), pl.BlockSpec((1, D), const)),
        out_shape=(jax.ShapeDtypeStruct((T, D), F32), jax.ShapeDtypeStruct((1, D), F32)),
        compiler_params=_params("arbitrary"),
    )(*_hbm(dy, x, gain, *dz, w_in_t))


def _win_grad(dz, hn, name, tk):
    T, D = hn.shape
    nz = len(dz)
    nt = T // tk
    W = _Z_SPLITS[-1]

    def body(hn_ref, *rest):
        dz_refs, dw_ref, dw_sc = rest[:nz], rest[nz], rest[nz + 1]
        i = pl.program_id(0)

        @pl.when(i == 0)
        def _():
            dw_sc[...] = jnp.zeros_like(dw_sc)

        hn = hn_ref[...]
        for r, lo, hi in zip(dz_refs, _Z_SPLITS[:-1], _Z_SPLITS[1:]):
            dw_sc[lo:hi, :] += _dot_tn(r[...], hn)

        @pl.when(i == nt - 1)
        def _():
            dw_ref[...] = dw_sc[...].astype(BF16)

    row = lambda i: (i, 0)
    return pl.pallas_call(
        body, name=name, grid=(nt,),
        in_specs=[pl.BlockSpec((tk, D), row)] + [pl.BlockSpec((tk, a.shape[1]), row) for a in dz],
        out_specs=pl.BlockSpec((W, D), lambda i: (0, 0)),
        out_shape=jax.ShapeDtypeStruct((W, D), BF16),
        scratch_shapes=[pltpu.VMEM((W, D), F32)],
        compiler_params=_params("arbitrary"),
    )(*_hbm(hn, *dz))


def _loss_head(x, target, gain, name, tm):
    T, D = x.shape

    def body(x_ref, t_ref, g_ref, dx_ref, dgain_ref, loss_ref):
        @pl.when(pl.program_id(0) == 0)
        def _():
            dgain_ref[...] = jnp.zeros_like(dgain_ref)
            loss_ref[...] = jnp.zeros_like(loss_ref)

        xv = x_ref[...]
        gain_v = g_ref[...]
        _, xhat = _rms_stats(xv)
        err = xhat * gain_v - t_ref[...]
        part = 0.5 * jnp.sum(jnp.mean(err * err, axis=-1, keepdims=True), axis=0, keepdims=True)
        loss_ref[...] += part
        dx, dgain = _rms_bwd(err * (1.0 / D), xv, gain_v)
        dgain_ref[...] += dgain
        dx_ref[...] = dx

    row = lambda i: (i, 0)
    const = lambda i: (0, 0)
    return pl.pallas_call(
        body, name=name, grid=(T // tm,),
        in_specs=[pl.BlockSpec((tm, D), row), pl.BlockSpec((tm, D), row), pl.BlockSpec((1, D), const)],
        out_specs=(pl.BlockSpec((tm, D), row), pl.BlockSpec((1, D), const),
                   pl.BlockSpec((SUBLANES, 128), const)),
        out_shape=(jax.ShapeDtypeStruct((T, D), F32), jax.ShapeDtypeStruct((1, D), F32),
                   jax.ShapeDtypeStruct((SUBLANES, 128), F32)),
        compiler_params=_params("arbitrary"),
    )(*_hbm(x, target, gain))


def _adamw(parts, w, m, v, name, tr):
    P = parts.shape[0]
    R, C = w.shape

    def body(p_ref, w_ref, m_ref, v_ref, g_ref, d_ref, nm_ref, nv_ref):
        g = p_ref[0].astype(F32)
        for d in range(1, P):
            g = g + p_ref[d].astype(F32)
        nm = ADAM_B1 * m_ref[...] + (1.0 - ADAM_B1) * g
        nv = ADAM_B2 * v_ref[...] + (1.0 - ADAM_B2) * (g * g)
        m_hat = nm / (1.0 - ADAM_B1 ** ADAM_STEP)
        v_hat = nv / (1.0 - ADAM_B2 ** ADAM_STEP)
        g_ref[...] = g
        nm_ref[...] = nm
        nv_ref[...] = nv
        d_ref[...] = -ADAM_LR * (m_hat / (jnp.sqrt(v_hat) + ADAM_EPS) + ADAM_WD * w_ref[...])

    row = lambda i: (i, 0)
    spec = pl.BlockSpec((tr, C), row)
    shp = jax.ShapeDtypeStruct((R, C), F32)
    return pl.pallas_call(
        body, name=name, grid=(R // tr,),
        in_specs=[pl.BlockSpec((P, tr, C), lambda i: (0, i, 0)), spec, spec, spec],
        out_specs=(spec, spec, spec, spec),
        out_shape=(shp, shp, shp, shp),
        compiler_params=_params("arbitrary"),
    )(*_hbm(parts, w, m, v))


def _columns_of_blocks(g):
    n, R, w = g.shape
    return g.transpose(1, 0, 2).reshape(R, n * w)


def _pad_row(vec):
    vec = vec.reshape(1, -1)
    return jnp.pad(vec, ((0, 0), (0, PACK_COLS - vec.shape[1])))


def kernel(x, rel_bias_table, ffn1_norm, ffn1_w_gate, ffn1_w_up, ffn1_w_down, mix_norm, w_in, conv_w, attn_sinks, attn_out_norm, conv_out_norm, w_out, ffn2_norm, ffn2_w_gate, ffn2_w_up, ffn2_w_down, final_norm, loss_target, m_rel_bias_table, m_ffn1_norm, m_ffn1_w_gate, m_ffn1_w_up, m_ffn1_w_down, m_mix_norm, m_w_in, m_conv_w, m_attn_sinks, m_attn_out_norm, m_conv_out_norm, m_w_out, m_ffn2_norm, m_ffn2_w_gate, m_ffn2_w_up, m_ffn2_w_down, m_final_norm, v_rel_bias_table, v_ffn1_norm, v_ffn1_w_gate, v_ffn1_w_up, v_ffn1_w_down, v_mix_norm, v_w_in, v_conv_w, v_attn_sinks, v_attn_out_norm, v_conv_out_norm, v_w_out, v_ffn2_norm, v_ffn2_w_gate, v_ffn2_w_up, v_ffn2_w_down, v_final_norm):
    T, D = x.shape[1], x.shape[2]
    x0 = x[0]
    target = loss_target[0]
    tm = min(TM_FFN, T)
    tm_bwd = min(TM_FFN_BWD, T)
    tm_mix = min(TM_MIX, T)
    tk = min(TK_WGRAD, T)
    tf = TF_FFN
    nblk = min(ATTN_BLOCKS, T // BLOCK)
    me = 4 * lax.axis_index("x") + 2 * lax.axis_index("y") + lax.axis_index("c")

    big = {
        "ffn1_w_gate": (ffn1_w_gate[0], m_ffn1_w_gate[0], v_ffn1_w_gate[0], True),
        "ffn1_w_up": (ffn1_w_up[0], m_ffn1_w_up[0], v_ffn1_w_up[0], True),
        "ffn1_w_down": (ffn1_w_down[0], m_ffn1_w_down[0], v_ffn1_w_down[0], False),
        "w_in": (w_in[0], m_w_in[0], v_w_in[0], True),
        "w_out": (w_out[0], m_w_out[0], v_w_out[0], False),
        "ffn2_w_gate": (ffn2_w_gate[0], m_ffn2_w_gate[0], v_ffn2_w_gate[0], True),
        "ffn2_w_up": (ffn2_w_up[0], m_ffn2_w_up[0], v_ffn2_w_up[0], True),
        "ffn2_w_down": (ffn2_w_down[0], m_ffn2_w_down[0], v_ffn2_w_down[0], False),
    }

    def block_to_send(name):
        w, _, _, transposed = big[name]
        return (w.T if transposed else w).astype(BF16)

    names1 = ["ffn1_w_gate", "ffn1_w_up", "ffn1_w_down"]
    names_rest = ["w_in", "w_out", "ffn2_w_gate", "ffn2_w_up", "ffn2_w_down"]
    first = _gather_two_level([block_to_send(n) for n in names1], "gather_ffn1", ffn1_norm)
    wgt1, wut1, wd1 = [g.reshape(-1, D) for g in first]
    h_rest, token = _exchange_start([block_to_send(n) for n in names_rest[:2]] + [conv_w[0]]
                                    + [block_to_send(n) for n in names_rest[2:]], False,
                                    "gather_start_rest", wd1)

    wgu1 = _stack_gate_up(wgt1, wut1, tf)
    x1, xn1, gu1 = _ffn_fwd(x0, ffn1_norm, wgu1, wd1, token, "ffn1_fwd", tm, tf)
    mixw = _exchange_wait(h_rest[:3], False, "gather_wait_mix", x1)
    win_t = mixw[0].reshape(-1, D)
    wout = mixw[1].reshape(-1, D)
    cw = _columns_of_blocks(mixw[2])
    hn, q, k, v, u, gb, gc = _mixin_fwd(x1, mix_norm, win_t, "mixin_fwd", tm_mix)
    bucket = jnp.asarray(_bucket_table().T.copy())
    sinks = attn_sinks.reshape(-1)
    bias_t = _bias_build(rel_bias_table, bucket, "bias_build")
    attn_raw, attn_n = _attn_fwd(q, k, v, bias_t, sinks, attn_out_norm, "attn_fwd", nblk)
    x2, conv_n = _mixout_fwd(x1, attn_n, u, gb, gc, cw, conv_out_norm, wout, "mixout_fwd", tm_mix)
    wgt2, wut2, wd2 = [g.reshape(-1, D) for g in _exchange_wait(h_rest[3:], False, "gather_wait_ffn2", x2)]
    wgu2 = _stack_gate_up(wgt2, wut2, tf)
    x3, xn2, gu2 = _ffn_fwd(x2, ffn2_norm, wgu2, wd2, x2, "ffn2_fwd", tm, tf)

    def blocks(g):
        return g.reshape(N_DEV, -1, D)

    dx3, d_final, loss_part = _loss_head(x3, target, final_norm.reshape(1, D), "loss_head", tm_mix)
    dx2, d_ffn2_norm, dg2, du2, hh2, do2 = _ffn_bwd(
        dx3, x2, ffn2_norm, gu2, wgu2, wd2, dx3, "ffn2_bwd", tm_bwd, tf)
    d_wg2 = _tn_grad(dg2, xn2, dx2, "ffn2_wgrad_gate", BM_WGRAD, tk)
    d_wu2 = _tn_grad(du2, xn2, dx2, "ffn2_wgrad_up", BM_WGRAD, tk)
    d_wd2 = _tn_grad(hh2, do2, dx2, "ffn2_wgrad_down", BM_WGRAD, tk)
    handles2, token2 = _exchange_start([blocks(d_wg2), blocks(d_wu2), blocks(d_wd2)], True,
                                       "grads_start_ffn2", dx2)

    dmixed, d_wout = _mixout_bwd(dx2, attn_n, conv_n, wout, token2, "mixout_bwd", tm_mix)
    dq, dk, dv, dbias, dsink, d_attn_norm = _attn_bwd(
        dmixed, attn_raw, q, k, v, bias_t, sinks, attn_out_norm, "attn_bwd", nblk)
    du, dgb, dgc, d_cw, d_conv_norm = _conv_bwd(dmixed, u, gb, gc, cw, conv_out_norm, "conv_bwd", tm_mix)
    d_table = _bias_grad(dbias, bucket, "bias_grad")
    dz = [dq, dk, dv, du, dgb, dgc]
    dx1, d_mix_norm = _mixin_bwd(dx2, x1, mix_norm, dz, win_t, "mixin_bwd", tm_mix)
    d_win_t = _win_grad(dz, hn, "win_grad", min(TM_MIX, T))
    handles_mix, token_mix = _exchange_start([blocks(d_win_t), blocks(d_wout)], True, "grads_start_mix", d_table)

    dx0, d_ffn1_norm, dg1, du1, hh1, do1 = _ffn_bwd(
        dx1, x0, ffn1_norm, gu1, wgu1, wd1, token_mix, "ffn1_bwd", tm_bwd, tf)
    def pack(ffn1, mixn, ffn2, fin, attn_n_, conv_n_, sink_, extra, convw, table):
        rows = [_pad_row(ffn1), _pad_row(mixn), _pad_row(ffn2), _pad_row(fin),
                _pad_row(jnp.concatenate([attn_n_.reshape(-1), conv_n_.reshape(-1)])),
                _pad_row(sink_), _pad_row(extra),
                jnp.zeros((1, PACK_COLS), F32),
                jnp.pad(convw, ((0, 0), (0, PACK_COLS - convw.shape[1]))),
                _pad_row(table),
                jnp.zeros((PACK_ROWS - 12, PACK_COLS), F32)]
        return jnp.concatenate(rows, axis=0)

    def own_channels(a):
        full = jnp.zeros((a.shape[1], CONV_DIM), F32)
        return lax.dynamic_update_slice(full, a[0], (0, me * a.shape[2]))

    g_pack = pack(d_ffn1_norm, d_mix_norm, d_ffn2_norm, d_final, d_attn_norm, d_conv_norm,
                  dsink[:, 0], loss_part[0, :1], d_cw[:3], d_table[:, :N_Q_HEADS])
    zero1 = jnp.zeros((1,), F32)
    w_pack = pack(ffn1_norm, mix_norm, ffn2_norm, final_norm, attn_out_norm, conv_out_norm,
                  attn_sinks, zero1, own_channels(conv_w), rel_bias_table)
    m_pack = pack(m_ffn1_norm, m_mix_norm, m_ffn2_norm, m_final_norm, m_attn_out_norm, m_conv_out_norm,
                  m_attn_sinks, zero1, own_channels(m_conv_w), m_rel_bias_table)
    v_pack = pack(v_ffn1_norm, v_mix_norm, v_ffn2_norm, v_final_norm, v_attn_out_norm, v_conv_out_norm,
                  v_attn_sinks, zero1, own_channels(v_conv_w), v_rel_bias_table)
    (g_all,) = _exchange([g_pack], False, "gather_small", dx0)
    packs = _adamw(g_all, w_pack, m_pack, v_pack, "adamw_small", PACK_ROWS)

    d_wd1 = _tn_grad(hh1, do1, packs[0], "ffn1_wgrad_down", BM_WGRAD, tk)
    handles1_d, token1 = _exchange_start([blocks(d_wd1)], True, "grads_start_ffn1_down", dx0)
    d_wg1 = _tn_grad(dg1, xn1, token1, "ffn1_wgrad_gate", BM_WGRAD, tk)
    handles1_g, token1 = _exchange_start([blocks(d_wg1)], True, "grads_start_ffn1_gate", token1)
    d_wu1 = _tn_grad(du1, xn1, token1, "ffn1_wgrad_up", BM_WGRAD, tk)
    handles1_u, token1 = _exchange_start([blocks(d_wu1)], True, "grads_start_ffn1_up", token1)

    res = {}

    def update(names, parts):
        last = None
        for name, p in zip(names, parts):
            w, m_, v_, transposed = big[name]
            if transposed:
                w, m_, v_ = w.T, m_.T, v_.T
            new = _adamw(p, w, m_, v_, "adamw_" + name, _row_tile(w.shape[0], ADAM_ROWS))
            res[name] = tuple((a.T if transposed else a)[None] for a in new)
            last = new[0]
        return last

    parts2 = _exchange_wait(handles2, True, "grads_wait_ffn2", token1)
    done2 = update(["ffn2_w_gate", "ffn2_w_up", "ffn2_w_down"], parts2)
    parts_mix = _exchange_wait(handles_mix, True, "grads_wait_mix", done2)
    done_mix = update(["w_in", "w_out"], parts_mix)

    parts1_d = _exchange_wait(handles1_d, True, "grads_wait_ffn1_down", done_mix)
    done1 = update(["ffn1_w_down"], parts1_d)
    parts1_g = _exchange_wait(handles1_g, True, "grads_wait_ffn1_gate", done1)
    done1 = update(["ffn1_w_gate"], parts1_g)
    parts1_u = _exchange_wait(handles1_u, True, "grads_wait_ffn1_up", done1)
    update(["ffn1_w_up"], parts1_u)

    def unpack(pk):
        cwb = lax.dynamic_slice(pk[8:11, :CONV_DIM], (0, me * conv_w.shape[2]), (3, conv_w.shape[2]))
        return {
            "ffn1_norm": pk[0:1, :D], "mix_norm": pk[1:2, :D], "ffn2_norm": pk[2:3, :D],
            "final_norm": pk[3, :D],
            "attn_out_norm": pk[4:5, :ATTN_WIDTH], "conv_out_norm": pk[4:5, ATTN_WIDTH:ATTN_WIDTH + CONV_DIM],
            "attn_sinks": pk[5:6, :N_Q_HEADS],
            "conv_w": cwb[None],
            "rel_bias_table": pk[11, :NUM_BUCKETS * N_Q_HEADS].reshape(NUM_BUCKETS, N_Q_HEADS),
        }

    small = [unpack(pk) for pk in packs]
    loss = packs[0][6, 0]

    order = ["rel_bias_table", "ffn1_norm", "ffn1_w_gate", "ffn1_w_up", "ffn1_w_down", "mix_norm", "w_in",
             "conv_w", "attn_sinks", "attn_out_norm", "conv_out_norm", "w_out", "ffn2_norm",
             "ffn2_w_gate", "ffn2_w_up", "ffn2_w_down", "final_norm"]
    outs = [loss, dx0[None]]
    for kind in range(4):
        for name in order:
            outs.append(res[name][kind] if name in res else small[kind][name])
    return tuple(outs)
```

```python
import math

import numpy as np
import jax
import jax.numpy as jnp
from jax import lax
from jax.experimental import pallas as pl
from jax.experimental.pallas import tpu as pltpu

F32 = jnp.float32
BF16 = jnp.bfloat16

N_DEV = 8
EPS = 1e-6
HEAD_DIM = 64
N_Q_HEADS = 8
N_KV_HEADS = 2
GQA_GROUP = 4
ATTN_WIDTH = 512
KV_WIDTH = 128
CONV_DIM = 512
BLOCK = 128
WINDOW = 128
NUM_BUCKETS = 32
MAX_DISTANCE = 128
SCALE = HEAD_DIM ** -0.5
MASKED = -1e30
GROUP_ROWS = GQA_GROUP * BLOCK

ADAM_LR = 0.001
ADAM_B1 = 0.9
ADAM_B2 = 0.999
ADAM_EPS = 1e-08
ADAM_WD = 0.01
ADAM_STEP = 10

VMEM_LIMIT_BYTES = 40 * 1024 * 1024
SUBLANES = 8
PACK_ROWS = 16
PACK_COLS = 1024

TM_FFN = 1024
TM_FFN_BWD = 1024
TM_MIX = 512
TK_WGRAD = 1024
TF_FFN = 256
BM_WGRAD = 1408
ROW_GROUPS = 4
ATTN_BLOCKS = 4
ADAM_ROWS = 256


def _row_tile(rows, limit):
    best = rows
    for t in range(16, min(rows, limit) + 1, 16):
        if rows % t == 0:
            best = t
    return best


def _params(*sem):
    return pltpu.CompilerParams(dimension_semantics=sem, vmem_limit_bytes=VMEM_LIMIT_BYTES)


def _hbm(*arrays):
    return [pltpu.with_memory_space_constraint(a, pltpu.HBM) for a in arrays]


def _dot(a, b):
    return jnp.dot(a, b, preferred_element_type=F32)


def _dot_nt(a, b):
    return lax.dot_general(a, b, (((1,), (1,)), ((), ())), preferred_element_type=F32)


def _dot_tn(a, b):
    return lax.dot_general(a, b, (((0,), (0,)), ((), ())), preferred_element_type=F32)


def _sigmoid(g):
    return 0.5 * jnp.tanh(0.5 * g) + 0.5


def _rms_stats(x):
    inv = lax.rsqrt(jnp.mean(x * x, axis=-1, keepdims=True) + EPS)
    return inv, x * inv


def _rms_bwd(dy, x, gain):
    inv, xhat = _rms_stats(x)
    dgain = jnp.sum(dy * xhat, axis=0, keepdims=True)
    dxh = dy * gain
    dx = inv * (dxh - xhat * jnp.mean(dxh * xhat, axis=-1, keepdims=True))
    return dx, dgain


def _peer_list():
    x, y, c = lax.axis_index("x"), lax.axis_index("y"), lax.axis_index("c")
    peers = []
    for k in range(1, N_DEV):
        px = 1 - x if (k >> 2) & 1 else x
        py = 1 - y if (k >> 1) & 1 else y
        pc = 1 - c if k & 1 else c
        peers.append((px, py, pc))
    return 4 * x + 2 * y + c, peers


def _exchange(arrs, scatter, name, after):
    n = len(arrs)
    out_shape = []
    for a in arrs:
        shp = a.shape if scatter else (N_DEV,) + a.shape
        out_shape.append(jax.ShapeDtypeStruct(shp, a.dtype))

    def body(*refs):
        ins, outs = refs[:n], refs[n + 1:2 * n + 1]
        send_sems, recv_sems, local_sems = refs[2 * n + 1:]
        me, peers = _peer_list()
        started = []
        for a in range(n):
            own = ins[a].at[me] if scatter else ins[a]
            loc = pltpu.make_async_copy(own, outs[a].at[me], local_sems.at[a])
            loc.start()
            started.append(loc)
        sends = []
        for a in range(n):
            for k, (px, py, pc) in enumerate(peers):
                src = ins[a].at[4 * px + 2 * py + pc] if scatter else ins[a]
                cp = pltpu.make_async_remote_copy(
                    src_ref=src, dst_ref=outs[a].at[me],
                    send_sem=send_sems.at[a, k], recv_sem=recv_sems.at[a, k],
                    device_id=(px, py, pc), device_id_type=pl.DeviceIdType.MESH)
                cp.start()
                sends.append(cp)
        for a in range(n):
            for k, (px, py, pc) in enumerate(peers):
                landed = outs[a].at[4 * px + 2 * py + pc]
                pltpu.make_async_remote_copy(
                    src_ref=landed, dst_ref=landed,
                    send_sem=send_sems.at[a, k], recv_sem=recv_sems.at[a, k],
                    device_id=(px, py, pc), device_id_type=pl.DeviceIdType.MESH).wait_recv()
        for cp in sends:
            cp.wait_send()
        for loc in started:
            loc.wait()

    hbm = pl.BlockSpec(memory_space=pl.ANY)
    return pl.pallas_call(
        body, name=name, out_shape=tuple(out_shape),
        in_specs=[hbm] * (n + 1), out_specs=tuple([hbm] * n),
        scratch_shapes=[pltpu.SemaphoreType.DMA((n, N_DEV - 1)),
                        pltpu.SemaphoreType.DMA((n, N_DEV - 1)),
                        pltpu.SemaphoreType.DMA((n,))],
    )(*arrs, after)


def _gather_two_level(arrs, name, after):
    n = len(arrs)
    out_shape = tuple(jax.ShapeDtypeStruct((N_DEV,) + a.shape, a.dtype) for a in arrs)

    def body(*refs):
        ins, outs = refs[:n], refs[n + 1:2 * n + 1]
        send_sems, recv_sems, local_sems = refs[2 * n + 1:]
        x, y, c = lax.axis_index("x"), lax.axis_index("y"), lax.axis_index("c")
        me, sibling = (x, y, c), (x, y, 1 - c)
        chips = [(1 - x, y), (x, 1 - y), (1 - x, 1 - y)]

        def copy(a, k, block, to, src=None):
            slot = outs[a].at[4 * block[0] + 2 * block[1] + block[2]]
            return pltpu.make_async_remote_copy(
                src_ref=slot if src is None else src, dst_ref=slot,
                send_sem=send_sems.at[a, k], recv_sem=recv_sems.at[a, k],
                device_id=to, device_id_type=pl.DeviceIdType.MESH)

        started = []
        for a in range(n):
            loc = pltpu.make_async_copy(ins[a], outs[a].at[4 * x + 2 * y + c], local_sems.at[a])
            loc.start()
            started.append(loc)
        sends = []
        for a in range(n):
            sends.append(copy(a, 0, me, sibling, src=ins[a]))
            sends += [copy(a, 1 + j, me, (*chip, c), src=ins[a]) for j, chip in enumerate(chips)]
        for cp in sends:
            cp.start()
        for j, chip in enumerate(chips):
            for a in range(n):
                copy(a, 1 + j, (*chip, c), me).wait_recv()
                fwd = copy(a, 4 + j, (*chip, c), sibling)
                fwd.start()
                sends.append(fwd)
        for a in range(n):
            copy(a, 0, sibling, me).wait_recv()
            for j, chip in enumerate(chips):
                copy(a, 4 + j, (*chip, 1 - c), me).wait_recv()
        for cp in sends:
            cp.wait_send()
        for loc in started:
            loc.wait()

    hbm = pl.BlockSpec(memory_space=pl.ANY)
    return pl.pallas_call(
        body, name=name, out_shape=out_shape,
        in_specs=[hbm] * (n + 1), out_specs=tuple([hbm] * n),
        scratch_shapes=[pltpu.SemaphoreType.DMA((n, N_DEV - 1)),
                        pltpu.SemaphoreType.DMA((n, N_DEV - 1)),
                        pltpu.SemaphoreType.DMA((n,))],
    )(*arrs, after)


_HBM = pl.BlockSpec(memory_space=pltpu.HBM)
_SEM = pl.BlockSpec(memory_space=pltpu.SEMAPHORE)
_EFFECT = pltpu.SideEffectType.DATAFLOW_SIDE_EFFECTING


def _split_copies(srcs, lands, send_sems, recv_sems, scatter):
    me, peers = _peer_list()
    copies = []
    for a in range(len(srcs)):
        for k, (px, py, pc) in enumerate(peers):
            src = srcs[a].at[4 * px + 2 * py + pc] if scatter else srcs[a]
            copies.append(pltpu.make_async_remote_copy(
                src_ref=src, dst_ref=lands[a].at[me],
                send_sem=send_sems[a].at[k], recv_sem=recv_sems[a].at[k],
                device_id=(px, py, pc), device_id_type=pl.DeviceIdType.MESH))
    return copies


def _exchange_start(arrs, scatter, name, after):
    n = len(arrs)
    me = 4 * lax.axis_index("x") + 2 * lax.axis_index("y") + lax.axis_index("c")
    lands = []
    for a in arrs:
        own = lax.dynamic_index_in_dim(a, me, 0, keepdims=True) if scatter else a[None]
        shp = a.shape if scatter else (N_DEV,) + a.shape
        lands.append(lax.dynamic_update_slice(lax.empty(shp, a.dtype), own, (me,) + (0,) * (len(shp) - 1)))

    def body(*refs):
        srcs, lnds = refs[:n], refs[n:2 * n]
        outs = refs[2 * n + 1:]
        send_sems, recv_sems = outs[:n], outs[n:2 * n]
        token = outs[4 * n]
        for cp in _split_copies(srcs, lnds, send_sems, recv_sems, scatter):
            cp.start()
        token[...] = jnp.zeros_like(token)

    sem = pltpu.SemaphoreType.DMA((N_DEV - 1,))
    out_shape = ([sem] * (2 * n) + [pltpu.HBM(a.shape, a.dtype) for a in arrs]
                 + [pltpu.HBM(l.shape, l.dtype) for l in lands] + [jax.ShapeDtypeStruct((SUBLANES, 128), F32)])
    res = pl.pallas_call(
        body, name=name, out_shape=tuple(out_shape),
        in_specs=[_HBM] * (2 * n) + [pl.BlockSpec(memory_space=pl.ANY)],
        out_specs=tuple([_SEM] * (2 * n) + [_HBM] * (2 * n) + [pl.BlockSpec(memory_space=pltpu.VMEM)]),
        input_output_aliases={i: 2 * n + i for i in range(2 * n)},
        compiler_params=pltpu.CompilerParams(has_side_effects=_EFFECT),
    )(*[pltpu.with_memory_space_constraint(a, pltpu.HBM) for a in arrs],
      *[pltpu.with_memory_space_constraint(l, pltpu.HBM) for l in lands], after)
    handles = [(res[2 * n + a], res[3 * n + a], res[a], res[n + a]) for a in range(n)]
    return handles, res[4 * n]


def _exchange_wait(handles, scatter, name, after):
    n = len(handles)

    def body(*refs):
        srcs, lnds = refs[:n], refs[n:2 * n]
        send_sems, recv_sems = refs[2 * n:3 * n], refs[3 * n:4 * n]
        for cp in _split_copies(srcs, lnds, send_sems, recv_sems, scatter):
            cp.wait_send()
            cp.wait_recv()

    srcs = [h[0] for h in handles]
    lands = [h[1] for h in handles]
    res = pl.pallas_call(
        body, name=name,
        out_shape=tuple([pltpu.HBM(a.shape, a.dtype) for a in srcs] + [pltpu.HBM(l.shape, l.dtype) for l in lands]),
        in_specs=[_HBM] * (2 * n) + [_SEM] * (2 * n) + [pl.BlockSpec(memory_space=pl.ANY)],
        out_specs=tuple([_HBM] * (2 * n)),
        input_output_aliases={i: i for i in range(2 * n)},
        compiler_params=pltpu.CompilerParams(has_side_effects=_EFFECT),
    )(*srcs, *lands, *[h[2] for h in handles], *[h[3] for h in handles], after)
    return list(res[n:])


def _row_groups(tm):
    return [slice(r * (tm // ROW_GROUPS), (r + 1) * (tm // ROW_GROUPS)) for r in range(ROW_GROUPS)]


def _ffn_fwd(x, gain, wgt, wut, wd, after, name, tm, tf, head=None):
    T, D = x.shape
    F = wd.shape[0]
    nj = F // tf
    n_in = 8 if head else 6

    def body(*refs):
        x_ref, g_ref, wgt_ref, wut_ref, wd_ref, after_ref = refs[:6]
        outs = refs[n_in:]
        xo_ref, xn_ref, gu_ref = outs[:3]
        xn_sc, acc_sc = outs[-2:]
        i, j = pl.program_id(0), pl.program_id(1)

        @pl.when(j == 0)
        def _():
            _, xhat = _rms_stats(x_ref[...])
            xn = (xhat * g_ref[...]).astype(BF16)
            xn_sc[...] = xn
            xn_ref[...] = xn
            acc_sc[...] = jnp.zeros_like(acc_sc)

        groups = _row_groups(tm)
        gs = [_dot_nt(xn_sc[rows, :], wgt_ref[...]) for rows in groups]
        us = [_dot_nt(xn_sc[rows, :], wut_ref[...]) for rows in groups]
        hs = []
        for rows, g, u in zip(groups, gs, us):
            gu_ref[rows, :tf] = g.astype(BF16)
            gu_ref[rows, tf:] = u.astype(BF16)
            hs.append((g * _sigmoid(g) * u).astype(BF16))
        for rows, h in zip(groups, hs):
            acc_sc[rows, :] += _dot(h, wd_ref[...])

        if head:
            t_ref, fg_ref = refs[6:8]
            dgain_ref, loss_ref = outs[3:5]

            @pl.when((i == 0) & (j == 0))
            def _():
                dgain_ref[...] = jnp.zeros_like(dgain_ref)
                loss_ref[...] = jnp.zeros_like(loss_ref)

        @pl.when(j == nj - 1)
        def _():
            xo = x_ref[...] + 0.5 * acc_sc[...]
            if head:
                fg = fg_ref[...]
                _, xhat = _rms_stats(xo)
                err = xhat * fg - t_ref[...]
                loss_ref[...] += 0.5 * jnp.sum(jnp.mean(err * err, axis=-1, keepdims=True), axis=0, keepdims=True)
                dx, dgain = _rms_bwd(err * (1.0 / D), xo, fg)
                dgain_ref[...] += dgain
                xo_ref[...] = dx
            else:
                xo_ref[...] = xo

    tile = pl.BlockSpec((tm, D), lambda i, j: (i, 0), pipeline_mode=pl.Buffered(1))
    const = pl.BlockSpec((1, D), lambda i, j: (0, 0))
    wspec = pl.BlockSpec((tf, D), lambda i, j: (j, 0))
    in_specs = [tile, const, wspec, wspec, wspec, pl.BlockSpec(memory_space=pl.ANY)]
    out_specs = [tile, pl.BlockSpec((tm, D), lambda i, j: (i, 0)), pl.BlockSpec((tm, 2 * tf), lambda i, j: (i, j))]
    out_shape = [jax.ShapeDtypeStruct((T, D), F32), jax.ShapeDtypeStruct((T, D), BF16),
                 jax.ShapeDtypeStruct((T, 2 * F), BF16)]
    operands = [x, gain, wgt, wut, wd, after]
    if head:
        in_specs += [tile, const]
        out_specs += [const, pl.BlockSpec((SUBLANES, 128), lambda i, j: (0, 0))]
        out_shape += [jax.ShapeDtypeStruct((1, D), F32), jax.ShapeDtypeStruct((SUBLANES, 128), F32)]
        operands += list(head)
    return pl.pallas_call(
        body, name=name, grid=(T // tm, nj),
        in_specs=in_specs, out_specs=tuple(out_specs), out_shape=tuple(out_shape),
        scratch_shapes=[pltpu.VMEM((tm, D), BF16), pltpu.VMEM((tm, D), F32)],
        compiler_params=_params("arbitrary", "arbitrary"),
    )(*_hbm(*operands))


def _ffn_bwd(dy, x, gain, gu, wgt, wut, wd, after, name, tm, tf):
    T, D = x.shape
    F = wd.shape[0]
    nj = F // tf

    def body(dy_ref, x_ref, g_ref, gu_ref, wgt_ref, wut_ref, wd_ref, after_ref,
             dx_ref, dgain_ref, dg_ref, du_ref, hh_ref, do_ref, do_sc, acc_sc):
        i, j = pl.program_id(0), pl.program_id(1)

        @pl.when((i == 0) & (j == 0))
        def _():
            dgain_ref[...] = jnp.zeros_like(dgain_ref)

        @pl.when(j == 0)
        def _():
            do = (0.5 * dy_ref[...]).astype(BF16)
            do_sc[...] = do
            do_ref[...] = do
            acc_sc[...] = jnp.zeros_like(acc_sc)

        groups = _row_groups(tm)
        dhs = [_dot_nt(do_sc[rows, :], wd_ref[...]) for rows in groups]
        for rows, dh in zip(groups, dhs):
            g = gu_ref[rows, :tf].astype(F32)
            u = gu_ref[rows, tf:].astype(F32)
            sig = _sigmoid(g)
            s = g * sig
            dg_ref[rows, :] = (dh * u * (sig + s * (1.0 - sig))).astype(BF16)
            du_ref[rows, :] = (dh * s).astype(BF16)
            hh_ref[rows, :] = (s * u).astype(BF16)
        for rows in groups:
            acc_sc[rows, :] += _dot(dg_ref[rows, :], wgt_ref[...]) + _dot(du_ref[rows, :], wut_ref[...])

        @pl.when(j == nj - 1)
        def _():
            dx, dgain = _rms_bwd(acc_sc[...], x_ref[...], g_ref[...])
            dgain_ref[...] += dgain
            dx_ref[...] = dy_ref[...] + dx

    tile = pl.BlockSpec((tm, D), lambda i, j: (i, 0), pipeline_mode=pl.Buffered(1))
    return pl.pallas_call(
        body, name=name, grid=(T // tm, nj),
        in_specs=[tile, tile,
                  pl.BlockSpec((1, D), lambda i, j: (0, 0)),
                  pl.BlockSpec((tm, 2 * tf), lambda i, j: (i, j)),
                  pl.BlockSpec((tf, D), lambda i, j: (j, 0)),
                  pl.BlockSpec((tf, D), lambda i, j: (j, 0)),
                  pl.BlockSpec((tf, D), lambda i, j: (j, 0)),
                  pl.BlockSpec(memory_space=pl.ANY)],
        out_specs=(tile,
                   pl.BlockSpec((1, D), lambda i, j: (0, 0)),
                   pl.BlockSpec((tm, tf), lambda i, j: (i, j)),
                   pl.BlockSpec((tm, tf), lambda i, j: (i, j)),
                   pl.BlockSpec((tm, tf), lambda i, j: (i, j)),
                   pl.BlockSpec((tm, D), lambda i, j: (i, 0))),
        out_shape=(jax.ShapeDtypeStruct((T, D), F32), jax.ShapeDtypeStruct((1, D), F32),
                   jax.ShapeDtypeStruct((T, F), BF16), jax.ShapeDtypeStruct((T, F), BF16),
                   jax.ShapeDtypeStruct((T, F), BF16), jax.ShapeDtypeStruct((T, D), BF16)),
        scratch_shapes=[pltpu.VMEM((tm, D), BF16), pltpu.VMEM((tm, D), F32)],
        compiler_params=_params("arbitrary", "arbitrary"),
    )(*_hbm(dy, x, gain, gu, wgt, wut, wd, after))


def _tn_grad(a, b, after, name, bm, tk):
    T, M = a.shape
    D = b.shape[1]
    nk = T // tk

    def body(a_ref, b_ref, after_ref, o_ref, acc_sc):
        k = pl.program_id(1)

        @pl.when(k == 0)
        def _():
            acc_sc[...] = jnp.zeros_like(acc_sc)

        acc_sc[...] += _dot_tn(a_ref[...], b_ref[...])

        @pl.when(k == nk - 1)
        def _():
            o_ref[...] = acc_sc[...].astype(BF16)

    return pl.pallas_call(
        body, name=name, grid=(M // bm, nk),
        in_specs=[pl.BlockSpec((tk, bm), lambda i, k: (k, i)), pl.BlockSpec((tk, D), lambda i, k: (k, 0)),
                  pl.BlockSpec(memory_space=pl.ANY)],
        out_specs=pl.BlockSpec((bm, D), lambda i, k: (i, 0)),
        out_shape=jax.ShapeDtypeStruct((M, D), BF16),
        scratch_shapes=[pltpu.VMEM((bm, D), F32)],
        compiler_params=_params("arbitrary", "arbitrary"),
    )(*_hbm(a, b, after))


_Z_SPLITS = (0, 512, 640, 768, 1280, 1792, 2304)


def _mixin_fwd(x, gain, w_in_t, name, tm):
    T, D = x.shape
    widths = [b - a for a, b in zip(_Z_SPLITS[:-1], _Z_SPLITS[1:])]

    def body(x_ref, g_ref, w_ref, hn_ref, *outs):
        _, xhat = _rms_stats(x_ref[...])
        hn = (xhat * g_ref[...]).astype(BF16)
        hn_ref[...] = hn
        for o_ref, lo, hi in zip(outs, _Z_SPLITS[:-1], _Z_SPLITS[1:]):
            o_ref[...] = _dot_nt(hn, w_ref[lo:hi, :]).astype(BF16)

    return pl.pallas_call(
        body, name=name, grid=(T // tm,),
        in_specs=[pl.BlockSpec((tm, D), lambda i: (i, 0)),
                  pl.BlockSpec((1, D), lambda i: (0, 0)),
                  pl.BlockSpec(w_in_t.shape, lambda i: (0, 0))],
        out_specs=tuple([pl.BlockSpec((tm, D), lambda i: (i, 0))]
                        + [pl.BlockSpec((tm, w), lambda i: (i, 0)) for w in widths]),
        out_shape=tuple([jax.ShapeDtypeStruct((T, D), BF16)]
                        + [jax.ShapeDtypeStruct((T, w), BF16) for w in widths]),
        compiler_params=_params("arbitrary"),
    )(*_hbm(x, gain, w_in_t))


def _bucket_table():
    qi = np.arange(BLOCK, dtype=np.int32)[:, None]
    kj = np.arange(2 * BLOCK, dtype=np.int32)[None, :]
    dist = qi + BLOCK - kj
    n = np.maximum(dist, 0)
    max_exact = NUM_BUCKETS // 2
    large = max_exact + (np.log(np.maximum(n, 1).astype(np.float32) / max_exact)
                         / math.log(MAX_DISTANCE / max_exact)
                         * (NUM_BUCKETS - max_exact)).astype(np.int32)
    large = np.minimum(large, NUM_BUCKETS - 1)
    bucket = np.where(n < max_exact, n, large).astype(np.int32)
    valid = (dist >= 0) & (dist < WINDOW)
    return np.where(valid, bucket, -1).astype(np.int32)


def _bias_build(table, bucket, name):
    def body(t_ref, b_ref, o_ref):
        bk = b_ref[...]
        for h in range(N_Q_HEADS):
            def step(b, acc):
                return jnp.where(bk == b, t_ref[b, h], acc)
            o_ref[h] = lax.fori_loop(0, NUM_BUCKETS, step, jnp.full(bk.shape, MASKED, F32))

    return pl.pallas_call(
        body, name=name,
        in_specs=[pl.BlockSpec(memory_space=pltpu.SMEM), pl.BlockSpec(memory_space=pltpu.VMEM)],
        out_specs=pl.BlockSpec(memory_space=pltpu.VMEM),
        out_shape=jax.ShapeDtypeStruct((N_Q_HEADS,) + bucket.shape, F32),
    )(table, bucket)


def _bias_grad(dbias, bucket, name):
    def body(d_ref, b_ref, o_ref):
        bk = b_ref[...]
        row = lax.broadcasted_iota(jnp.int32, o_ref.shape, 0)
        lane = lax.broadcasted_iota(jnp.int32, o_ref.shape, 1)
        res = jnp.zeros(o_ref.shape, F32)
        for h in range(N_Q_HEADS):
            d = d_ref[h]

            def step(b, acc):
                part = jnp.sum(jnp.where(bk == b, d, 0.0), axis=0, keepdims=True)
                return jnp.where(row == b, part, acc)
            per_lane = lax.fori_loop(0, NUM_BUCKETS, step, jnp.zeros(o_ref.shape, F32))
            res = jnp.where(lane == h, jnp.sum(per_lane, axis=1, keepdims=True), res)
        o_ref[...] = res

    return pl.pallas_call(
        body, name=name,
        in_specs=[pl.BlockSpec(memory_space=pltpu.VMEM), pl.BlockSpec(memory_space=pltpu.VMEM)],
        out_specs=pl.BlockSpec(memory_space=pltpu.VMEM),
        out_shape=jax.ShapeDtypeStruct((NUM_BUCKETS, 128), F32),
    )(*_hbm(dbias, bucket))


def _head_cols(h):
    return slice(h * HEAD_DIM, (h + 1) * HEAD_DIM)


def _stack_heads(ref, r0, g, dtype):
    return jnp.concatenate(
        [ref[pl.ds(r0, BLOCK), _head_cols(GQA_GROUP * g + j)].astype(dtype) for j in range(GQA_GROUP)], axis=0)


def _unstack_heads(ref, r0, g, val):
    for j in range(GQA_GROUP):
        ref[pl.ds(r0, BLOCK), _head_cols(GQA_GROUP * g + j)] = val[j * BLOCK:(j + 1) * BLOCK, :]


def _head_lanes(h):
    return slice(h * BLOCK, (h + 1) * BLOCK)


def _group_lanes(g):
    return slice(g * GROUP_ROWS, (g + 1) * GROUP_ROWS)


def _head_softmax(st, bias_t, sink, no_prev):
    s = st * SCALE + bias_t
    row = lax.broadcasted_iota(jnp.int32, s.shape, 0)
    s = jnp.where(no_prev & (row < BLOCK), MASKED, s)
    m = jnp.maximum(jnp.max(s, axis=0, keepdims=True), sink)
    p = jnp.exp(s - m)
    ps = jnp.exp(sink - m)
    r = 1.0 / (jnp.sum(p, axis=0, keepdims=True) + ps)
    return p * r, ps * r


def _load_band(kf_sc, vf_sc, kp_ref, kc_ref, vp_ref, vc_ref, tq):
    kf_sc[0:BLOCK, :] = kp_ref[...]
    kf_sc[BLOCK:BLOCK + tq, :] = kc_ref[...]
    vf_sc[0:BLOCK, :] = vp_ref[...]
    vf_sc[BLOCK:BLOCK + tq, :] = vc_ref[...]


def _attn_fwd(q, k, v, bias_t, sinks, gain, name, nblk):
    T = q.shape[0]
    tq = nblk * BLOCK

    def body(sink_ref, q_ref, kc_ref, kp_ref, vc_ref, vp_ref, bias_ref, g_ref, raw_ref, nrm_ref,
             kf_sc, vf_sc, o_sc, st_sc, pt_sc):
        i = pl.program_id(0)
        _load_band(kf_sc, vf_sc, kp_ref, kc_ref, vp_ref, vc_ref, tq)

        def block(b, carry):
            r0 = pl.multiple_of(b * BLOCK, BLOCK)
            no_prev = (i == 0) & (b == 0)
            for g in range(N_KV_HEADS):
                kb = kf_sc[pl.ds(r0, 2 * BLOCK), _head_cols(g)]
                st_sc[:, _group_lanes(g)] = _dot_nt(kb, _stack_heads(q_ref, r0, g, BF16))
            for h in range(N_Q_HEADS):
                p, _ = _head_softmax(st_sc[:, _head_lanes(h)], bias_ref[h], sink_ref[h], no_prev)
                pt_sc[:, _head_lanes(h)] = p.astype(BF16)
            for g in range(N_KV_HEADS):
                vb = vf_sc[pl.ds(r0, 2 * BLOCK), _head_cols(g)]
                _unstack_heads(o_sc, r0, g, _dot_tn(pt_sc[:, _group_lanes(g)], vb))
            return carry

        lax.fori_loop(0, nblk, block, 0)
        o = o_sc[...]
        raw_ref[...] = o.astype(BF16)
        _, ohat = _rms_stats(o)
        nrm_ref[...] = (ohat * g_ref[...]).astype(BF16)

    cur = lambda i: (i, 0)
    prev = lambda i: (jnp.maximum(i * nblk - 1, 0), 0)
    lanes = N_Q_HEADS * BLOCK
    return pl.pallas_call(
        body, name=name, grid=(T // tq,),
        in_specs=[pl.BlockSpec(memory_space=pltpu.SMEM),
                  pl.BlockSpec((tq, ATTN_WIDTH), cur),
                  pl.BlockSpec((tq, KV_WIDTH), cur), pl.BlockSpec((BLOCK, KV_WIDTH), prev),
                  pl.BlockSpec((tq, KV_WIDTH), cur), pl.BlockSpec((BLOCK, KV_WIDTH), prev),
                  pl.BlockSpec(bias_t.shape, lambda i: (0, 0, 0)),
                  pl.BlockSpec((1, ATTN_WIDTH), lambda i: (0, 0))],
        out_specs=(pl.BlockSpec((tq, ATTN_WIDTH), cur), pl.BlockSpec((tq, ATTN_WIDTH), cur)),
        out_shape=(jax.ShapeDtypeStruct((T, ATTN_WIDTH), BF16), jax.ShapeDtypeStruct((T, ATTN_WIDTH), BF16)),
        scratch_shapes=[pltpu.VMEM((tq + BLOCK, KV_WIDTH), BF16), pltpu.VMEM((tq + BLOCK, KV_WIDTH), BF16),
                        pltpu.VMEM((tq, ATTN_WIDTH), F32),
                        pltpu.VMEM((2 * BLOCK, lanes), F32), pltpu.VMEM((2 * BLOCK, lanes), BF16)],
        compiler_params=_params("arbitrary"),
    )(sinks, *_hbm(q, k, k, v, v, bias_t, gain))


def _attn_bwd(dmixed, raw, q, k, v, bias_t, sinks, gain, name, nblk):
    T = q.shape[0]
    tq = nblk * BLOCK
    nt = T // tq
    lanes = N_Q_HEADS * BLOCK

    def body(sink_ref, dm_ref, raw_ref, q_ref, kc_ref, kp_ref, vc_ref, vp_ref, bias_ref, g_ref,
             dq_ref, dk_ref, dv_ref, dbias_ref, dsink_ref, dgain_ref,
             do_sc, dq_sc, kf_sc, vf_sc, dkf_sc, dvf_sc, st_sc, dpt_sc, pt_sc, dst_sc, drow_sc,
             qs_sc, dos_sc, dsink_sc):
        i = pl.program_id(0)
        tile = nt - 1 - i

        @pl.when(i == 0)
        def _():
            dkf_sc[...] = jnp.zeros_like(dkf_sc)
            dvf_sc[...] = jnp.zeros_like(dvf_sc)
            dsink_sc[...] = jnp.zeros_like(dsink_sc)
            dbias_ref[...] = jnp.zeros_like(dbias_ref)
            dgain_ref[...] = jnp.zeros_like(dgain_ref)

        carry_k = dkf_sc[0:BLOCK, :]
        carry_v = dvf_sc[0:BLOCK, :]
        dkf_sc[0:tq, :] = jnp.zeros((tq, KV_WIDTH), F32)
        dvf_sc[0:tq, :] = jnp.zeros((tq, KV_WIDTH), F32)
        dkf_sc[tq:tq + BLOCK, :] = carry_k
        dvf_sc[tq:tq + BLOCK, :] = carry_v
        _load_band(kf_sc, vf_sc, kp_ref, kc_ref, vp_ref, vc_ref, tq)

        do, dgain = _rms_bwd(dm_ref[...].astype(F32), raw_ref[...].astype(F32), g_ref[...])
        dgain_ref[...] += dgain
        do_sc[...] = do
        ones = jnp.ones((SUBLANES, HEAD_DIM), BF16)

        def block(b, carry):
            r0 = pl.multiple_of(b * BLOCK, BLOCK)
            no_prev = (tile == 0) & (b == 0)
            for g in range(N_KV_HEADS):
                kb = kf_sc[pl.ds(r0, 2 * BLOCK), _head_cols(g)]
                vb = vf_sc[pl.ds(r0, 2 * BLOCK), _head_cols(g)]
                qg = _stack_heads(q_ref, r0, g, BF16)
                dog = _stack_heads(do_sc, r0, g, F32)
                prod = dog * _stack_heads(raw_ref, r0, g, F32)
                hi = prod.astype(BF16)
                lo = (prod - hi.astype(F32)).astype(BF16)
                drow_sc[:, _group_lanes(g)] = _dot_nt(ones, hi) + _dot_nt(ones, lo)
                dogb = dog.astype(BF16)
                qs_sc[g] = qg
                dos_sc[g] = dogb
                st_sc[:, _group_lanes(g)] = _dot_nt(kb, qg)
                dpt_sc[:, _group_lanes(g)] = _dot_nt(vb, dogb)
            for h in range(N_Q_HEADS):
                hl = _head_lanes(h)
                p, ps = _head_softmax(st_sc[:, hl], bias_ref[h], sink_ref[h], no_prev)
                rowdot = drow_sc[0:1, hl]
                ds = p * (dpt_sc[:, hl] - rowdot)
                dsink_sc[h:h + 1, :] += -(ps * rowdot)
                dbias_ref[h] += ds
                dst_sc[:, hl] = ds.astype(BF16)
                pt_sc[:, hl] = p.astype(BF16)
            for g in range(N_KV_HEADS):
                kb = kf_sc[pl.ds(r0, 2 * BLOCK), _head_cols(g)]
                dsg = dst_sc[:, _group_lanes(g)]
                _unstack_heads(dq_sc, r0, g, _dot_tn(dsg, kb) * SCALE)
                dkf_sc[pl.ds(r0, 2 * BLOCK), _head_cols(g)] += _dot(dsg, qs_sc[g]) * SCALE
                dvf_sc[pl.ds(r0, 2 * BLOCK), _head_cols(g)] += _dot(pt_sc[:, _group_lanes(g)], dos_sc[g])
            return carry

        lax.fori_loop(0, nblk, block, 0)
        dq_ref[...] = dq_sc[...].astype(BF16)
        dk_ref[...] = dkf_sc[BLOCK:BLOCK + tq, :].astype(BF16)
        dv_ref[...] = dvf_sc[BLOCK:BLOCK + tq, :].astype(BF16)

        @pl.when(i == nt - 1)
        def _():
            tot = jnp.sum(dsink_sc[...], axis=1, keepdims=True)
            dsink_ref[...] = jnp.broadcast_to(tot, dsink_ref.shape)

    cur = lambda i: (nt - 1 - i, 0)
    prev = lambda i: (jnp.maximum((nt - 1 - i) * nblk - 1, 0), 0)
    const2 = lambda i: (0, 0)
    const3 = lambda i: (0, 0, 0)
    return pl.pallas_call(
        body, name=name, grid=(nt,),
        in_specs=[pl.BlockSpec(memory_space=pltpu.SMEM),
                  pl.BlockSpec((tq, ATTN_WIDTH), cur),
                  pl.BlockSpec((tq, ATTN_WIDTH), cur),
                  pl.BlockSpec((tq, ATTN_WIDTH), cur),
                  pl.BlockSpec((tq, KV_WIDTH), cur), pl.BlockSpec((BLOCK, KV_WIDTH), prev),
                  pl.BlockSpec((tq, KV_WIDTH), cur), pl.BlockSpec((BLOCK, KV_WIDTH), prev),
                  pl.BlockSpec(bias_t.shape, const3),
                  pl.BlockSpec((1, ATTN_WIDTH), const2)],
        out_specs=(pl.BlockSpec((tq, ATTN_WIDTH), cur),
                   pl.BlockSpec((tq, KV_WIDTH), cur), pl.BlockSpec((tq, KV_WIDTH), cur),
                   pl.BlockSpec(bias_t.shape, const3),
                   pl.BlockSpec((N_Q_HEADS, 128), const2),
                   pl.BlockSpec((1, ATTN_WIDTH), const2)),
        out_shape=(jax.ShapeDtypeStruct((T, ATTN_WIDTH), BF16),
                   jax.ShapeDtypeStruct((T, KV_WIDTH), BF16), jax.ShapeDtypeStruct((T, KV_WIDTH), BF16),
                   jax.ShapeDtypeStruct(bias_t.shape, F32),
                   jax.ShapeDtypeStruct((N_Q_HEADS, 128), F32),
                   jax.ShapeDtypeStruct((1, ATTN_WIDTH), F32)),
        scratch_shapes=[pltpu.VMEM((tq, ATTN_WIDTH), F32), pltpu.VMEM((tq, ATTN_WIDTH), F32),
                        pltpu.VMEM((tq + BLOCK, KV_WIDTH), BF16), pltpu.VMEM((tq + BLOCK, KV_WIDTH), BF16),
                        pltpu.VMEM((tq + BLOCK, KV_WIDTH), F32), pltpu.VMEM((tq + BLOCK, KV_WIDTH), F32),
                        pltpu.VMEM((2 * BLOCK, lanes), F32), pltpu.VMEM((2 * BLOCK, lanes), F32),
                        pltpu.VMEM((2 * BLOCK, lanes), BF16), pltpu.VMEM((2 * BLOCK, lanes), BF16),
                        pltpu.VMEM((SUBLANES, lanes), F32),
                        pltpu.VMEM((N_KV_HEADS, GROUP_ROWS, HEAD_DIM), BF16),
                        pltpu.VMEM((N_KV_HEADS, GROUP_ROWS, HEAD_DIM), BF16),
                        pltpu.VMEM((N_Q_HEADS, 128), F32)],
        compiler_params=_params("arbitrary"),
    )(sinks, *_hbm(dmixed, raw, q, k, k, v, v, bias_t, gain))


def _shift_down(cu, tail):
    row = lax.broadcasted_iota(jnp.int32, cu.shape, 0)
    t6, t7 = tail[6:7, :], tail[7:8, :]
    s1 = jnp.where(row == 0, t7, pltpu.roll(cu, 1, 0))
    s2 = jnp.where(row == 0, t6, jnp.where(row == 1, t7, pltpu.roll(cu, 2, 0)))
    return s1, s2


def _shift_up(d, head):
    n = d.shape[0]
    row = lax.broadcasted_iota(jnp.int32, d.shape, 0)
    h0, h1 = head[0:1, :], head[1:2, :]
    s1 = jnp.where(row == n - 1, h0, pltpu.roll(d, n - 1, 0))
    s2 = jnp.where(row == n - 1, h1, jnp.where(row == n - 2, h0, pltpu.roll(d, n - 2, 0)))
    return s1, s2


def _mixout_fwd(x, attn_n, u, gb, gc, conv_w, gain, w_out, name, tm):
    T, D = x.shape

    def body(x_ref, an_ref, u_ref, b_ref, c_ref, cw_ref, g_ref, wo_ref, xo_ref, cn_ref, tail_sc):
        @pl.when(pl.program_id(0) == 0)
        def _():
            tail_sc[...] = jnp.zeros_like(tail_sc)

        cu = c_ref[...].astype(F32) * u_ref[...].astype(F32)
        s1, s2 = _shift_down(cu, tail_sc[...])
        tail_sc[...] = cu[tm - SUBLANES:tm, :]
        pre = cw_ref[0:1, :] * s2 + cw_ref[1:2, :] * s1 + cw_ref[2:3, :] * cu
        conv = b_ref[...].astype(F32) * pre
        _, chat = _rms_stats(conv)
        cn = (chat * g_ref[...]).astype(BF16)
        cn_ref[...] = cn
        xo_ref[...] = (x_ref[...] + _dot(an_ref[...], wo_ref[0:ATTN_WIDTH, :])
                       + _dot(cn, wo_ref[ATTN_WIDTH:ATTN_WIDTH + CONV_DIM, :]))

    row = lambda i: (i, 0)
    const = lambda i: (0, 0)
    return pl.pallas_call(
        body, name=name, grid=(T // tm,),
        in_specs=[pl.BlockSpec((tm, D), row), pl.BlockSpec((tm, ATTN_WIDTH), row),
                  pl.BlockSpec((tm, CONV_DIM), row), pl.BlockSpec((tm, CONV_DIM), row),
                  pl.BlockSpec((tm, CONV_DIM), row),
                  pl.BlockSpec(conv_w.shape, const), pl.BlockSpec((1, CONV_DIM), const),
                  pl.BlockSpec(w_out.shape, const)],
        out_specs=(pl.BlockSpec((tm, D), row), pl.BlockSpec((tm, CONV_DIM), row)),
        out_shape=(jax.ShapeDtypeStruct((T, D), F32), jax.ShapeDtypeStruct((T, CONV_DIM), BF16)),
        scratch_shapes=[pltpu.VMEM((SUBLANES, CONV_DIM), F32)],
        compiler_params=_params("arbitrary"),
    )(*_hbm(x, attn_n, u, gb, gc, conv_w, gain, w_out))


def _mixout_bwd(dy, attn_n, conv_n, w_out, after, name, tm):
    T, D = dy.shape
    W = ATTN_WIDTH + CONV_DIM
    nt = T // tm

    def body(dy_ref, an_ref, cn_ref, w_ref, after_ref, dm_ref, dw_ref, dw_sc):
        i = pl.program_id(0)

        @pl.when(i == 0)
        def _():
            dw_sc[...] = jnp.zeros_like(dw_sc)

        dyb = dy_ref[...].astype(BF16)
        dm_ref[...] = _dot_nt(dyb, w_ref[...]).astype(BF16)
        dw_sc[0:ATTN_WIDTH, :] += _dot_tn(an_ref[...], dyb)
        dw_sc[ATTN_WIDTH:W, :] += _dot_tn(cn_ref[...], dyb)

        @pl.when(i == nt - 1)
        def _():
            dw_ref[...] = dw_sc[...].astype(BF16)

    row = lambda i: (i, 0)
    const = lambda i: (0, 0)
    return pl.pallas_call(
        body, name=name, grid=(nt,),
        in_specs=[pl.BlockSpec((tm, D), row), pl.BlockSpec((tm, ATTN_WIDTH), row),
                  pl.BlockSpec((tm, CONV_DIM), row), pl.BlockSpec(w_out.shape, const),
                  pl.BlockSpec(memory_space=pl.ANY)],
        out_specs=(pl.BlockSpec((tm, W), row), pl.BlockSpec((W, D), const)),
        out_shape=(jax.ShapeDtypeStruct((T, W), BF16), jax.ShapeDtypeStruct((W, D), BF16)),
        scratch_shapes=[pltpu.VMEM((W, D), F32)],
        compiler_params=_params("arbitrary"),
    )(*_hbm(dy, attn_n, conv_n, w_out, after))


def _conv_bwd(dmixed, u, gb, gc, conv_w, gain, name, tc):
    T = u.shape[0]
    nt = T // tc
    per8 = tc // SUBLANES

    def body(dm_ref, u_ref, b_ref, c_ref, ut_ref, ct_ref, cw_ref, g_ref,
             du_ref, db_ref, dc_ref, dcw_ref, dgain_ref, head_sc):
        i = pl.program_id(0)

        @pl.when(i == 0)
        def _():
            head_sc[...] = jnp.zeros_like(head_sc)
            dcw_ref[...] = jnp.zeros_like(dcw_ref)
            dgain_ref[...] = jnp.zeros_like(dgain_ref)

        uu = u_ref[...].astype(F32)
        cc = c_ref[...].astype(F32)
        bb = b_ref[...].astype(F32)
        cu = cc * uu
        tail = jnp.where(i == nt - 1, 0.0, ct_ref[...].astype(F32) * ut_ref[...].astype(F32))
        s1, s2 = _shift_down(cu, tail)
        w0, w1, w2 = cw_ref[0:1, :], cw_ref[1:2, :], cw_ref[2:3, :]
        pre = w0 * s2 + w1 * s1 + w2 * cu
        dconv, dgain = _rms_bwd(dm_ref[...].astype(F32), bb * pre, g_ref[...])
        dgain_ref[...] += dgain
        db_ref[...] = (dconv * pre).astype(BF16)
        dpre = dconv * bb
        dcw_ref[0:1, :] += jnp.sum(dpre * s2, axis=0, keepdims=True)
        dcw_ref[1:2, :] += jnp.sum(dpre * s1, axis=0, keepdims=True)
        dcw_ref[2:3, :] += jnp.sum(dpre * cu, axis=0, keepdims=True)
        n1, n2 = _shift_up(dpre, head_sc[...])
        head_sc[...] = dpre[0:SUBLANES, :]
        dcu = w2 * dpre + w1 * n1 + w0 * n2
        du_ref[...] = (dcu * cc).astype(BF16)
        dc_ref[...] = (dcu * uu).astype(BF16)

    rev = lambda i: (nt - 1 - i, 0)
    rev_right = lambda i: (nt - 1 - i, 1)
    tail_map = lambda i: (jnp.maximum((nt - 1 - i) * per8 - 1, 0), 0)
    const = lambda i: (0, 0)
    return pl.pallas_call(
        body, name=name, grid=(nt,),
        in_specs=[pl.BlockSpec((tc, CONV_DIM), rev_right),
                  pl.BlockSpec((tc, CONV_DIM), rev), pl.BlockSpec((tc, CONV_DIM), rev),
                  pl.BlockSpec((tc, CONV_DIM), rev),
                  pl.BlockSpec((SUBLANES, CONV_DIM), tail_map), pl.BlockSpec((SUBLANES, CONV_DIM), tail_map),
                  pl.BlockSpec(conv_w.shape, const), pl.BlockSpec((1, CONV_DIM), const)],
        out_specs=(pl.BlockSpec((tc, CONV_DIM), rev), pl.BlockSpec((tc, CONV_DIM), rev),
                   pl.BlockSpec((tc, CONV_DIM), rev),
                   pl.BlockSpec((SUBLANES, CONV_DIM), const), pl.BlockSpec((1, CONV_DIM), const)),
        out_shape=(jax.ShapeDtypeStruct((T, CONV_DIM), BF16), jax.ShapeDtypeStruct((T, CONV_DIM), BF16),
                   jax.ShapeDtypeStruct((T, CONV_DIM), BF16),
                   jax.ShapeDtypeStruct((SUBLANES, CONV_DIM), F32), jax.ShapeDtypeStruct((1, CONV_DIM), F32)),
        scratch_shapes=[pltpu.VMEM((SUBLANES, CONV_DIM), F32)],
        compiler_params=_params("arbitrary"),
    )(*_hbm(dmixed, u, gb, gc, u, gc, conv_w, gain))


def _mixin_bwd(dy, x, gain, dz, w_in_t, name, tm):
    T, D = x.shape
    nz = len(dz)

    def body(dy_ref, x_ref, g_ref, *rest):
        dz_refs, wt_ref, dx_ref, dgain_ref = rest[:nz], rest[nz], rest[nz + 1], rest[nz + 2]

        @pl.when(pl.program_id(0) == 0)
        def _():
            dgain_ref[...] = jnp.zeros_like(dgain_ref)

        dh = jnp.zeros((tm, D), F32)
        for r, lo, hi in zip(dz_refs, _Z_SPLITS[:-1], _Z_SPLITS[1:]):
            dh += _dot(r[...], wt_ref[lo:hi, :])
        dx, dgain = _rms_bwd(dh, x_ref[...], g_ref[...])
        dgain_ref[...] += dgain
        dx_ref[...] = dy_ref[...] + dx

    row = lambda i: (i, 0)
    const = lambda i: (0, 0)
    return pl.pallas_call(
        body, name=name, grid=(T // tm,),
        in_specs=[pl.BlockSpec((tm, D), row), pl.BlockSpec((tm, D), row), pl.BlockSpec((1, D), const)]
                 + [pl.BlockSpec((tm, a.shape[1]), row) for a in dz]
                 + [pl.BlockSpec(w_in_t.shape, const)],
        out_specs=(pl.BlockSpec((tm, D), row), pl.BlockSpec((1, D), const)),
        out_shape=(jax.ShapeDtypeStruct((T, D), F32), jax.ShapeDtypeStruct((1, D), F32)),
        compiler_params=_params("arbitrary"),
    )(*_hbm(dy, x, gain, *dz, w_in_t))


def _win_grad(dz, hn, name, tk):
    T, D = hn.shape
    nz = len(dz)
    nt = T // tk
    W = _Z_SPLITS[-1]

    def body(hn_ref, *rest):
        dz_refs, dw_ref, dw_sc = rest[:nz], rest[nz], rest[nz + 1]
        i = pl.program_id(0)

        @pl.when(i == 0)
        def _():
            dw_sc[...] = jnp.zeros_like(dw_sc)

        hn = hn_ref[...]
        for r, lo, hi in zip(dz_refs, _Z_SPLITS[:-1], _Z_SPLITS[1:]):
            dw_sc[lo:hi, :] += _dot_tn(r[...], hn)

        @pl.when(i == nt - 1)
        def _():
            dw_ref[...] = dw_sc[...].astype(BF16)

    row = lambda i: (i, 0)
    return pl.pallas_call(
        body, name=name, grid=(nt,),
        in_specs=[pl.BlockSpec((tk, D), row)] + [pl.BlockSpec((tk, a.shape[1]), row) for a in dz],
        out_specs=pl.BlockSpec((W, D), lambda i: (0, 0)),
        out_shape=jax.ShapeDtypeStruct((W, D), BF16),
        scratch_shapes=[pltpu.VMEM((W, D), F32)],
        compiler_params=_params("arbitrary"),
    )(*_hbm(hn, *dz))


def _adamw(parts, w, m, v, name, tr):
    P = parts.shape[0]
    R, C = w.shape

    def body(p_ref, w_ref, m_ref, v_ref, g_ref, d_ref, nm_ref, nv_ref):
        g = p_ref[0].astype(F32)
        for d in range(1, P):
            g = g + p_ref[d].astype(F32)
        nm = ADAM_B1 * m_ref[...] + (1.0 - ADAM_B1) * g
        nv = ADAM_B2 * v_ref[...] + (1.0 - ADAM_B2) * (g * g)
        m_hat = nm / (1.0 - ADAM_B1 ** ADAM_STEP)
        v_hat = nv / (1.0 - ADAM_B2 ** ADAM_STEP)
        g_ref[...] = g
        nm_ref[...] = nm
        nv_ref[...] = nv
        d_ref[...] = -ADAM_LR * (m_hat / (jnp.sqrt(v_hat) + ADAM_EPS) + ADAM_WD * w_ref[...])

    row = lambda i: (i, 0)
    spec = pl.BlockSpec((tr, C), row)
    shp = jax.ShapeDtypeStruct((R, C), F32)
    return pl.pallas_call(
        body, name=name, grid=(R // tr,),
        in_specs=[pl.BlockSpec((P, tr, C), lambda i: (0, i, 0)), spec, spec, spec],
        out_specs=(spec, spec, spec, spec),
        out_shape=(shp, shp, shp, shp),
        compiler_params=_params("arbitrary"),
    )(*_hbm(parts, w, m, v))


def _columns_of_blocks(g):
    n, R, w = g.shape
    return g.transpose(1, 0, 2).reshape(R, n * w)


def _pad_row(vec):
    vec = vec.reshape(1, -1)
    return jnp.pad(vec, ((0, 0), (0, PACK_COLS - vec.shape[1])))


def kernel(x, rel_bias_table, ffn1_norm, ffn1_w_gate, ffn1_w_up, ffn1_w_down, mix_norm, w_in, conv_w, attn_sinks, attn_out_norm, conv_out_norm, w_out, ffn2_norm, ffn2_w_gate, ffn2_w_up, ffn2_w_down, final_norm, loss_target, m_rel_bias_table, m_ffn1_norm, m_ffn1_w_gate, m_ffn1_w_up, m_ffn1_w_down, m_mix_norm, m_w_in, m_conv_w, m_attn_sinks, m_attn_out_norm, m_conv_out_norm, m_w_out, m_ffn2_norm, m_ffn2_w_gate, m_ffn2_w_up, m_ffn2_w_down, m_final_norm, v_rel_bias_table, v_ffn1_norm, v_ffn1_w_gate, v_ffn1_w_up, v_ffn1_w_down, v_mix_norm, v_w_in, v_conv_w, v_attn_sinks, v_attn_out_norm, v_conv_out_norm, v_w_out, v_ffn2_norm, v_ffn2_w_gate, v_ffn2_w_up, v_ffn2_w_down, v_final_norm):
    T, D = x.shape[1], x.shape[2]
    x0 = x[0]
    target = loss_target[0]
    tm = min(TM_FFN, T)
    tm_bwd = min(TM_FFN_BWD, T)
    tm_mix = min(TM_MIX, T)
    tk = min(TK_WGRAD, T)
    tf = TF_FFN
    nblk = min(ATTN_BLOCKS, T // BLOCK)
    me = 4 * lax.axis_index("x") + 2 * lax.axis_index("y") + lax.axis_index("c")

    big = {
        "ffn1_w_gate": (ffn1_w_gate[0], m_ffn1_w_gate[0], v_ffn1_w_gate[0], True),
        "ffn1_w_up": (ffn1_w_up[0], m_ffn1_w_up[0], v_ffn1_w_up[0], True),
        "ffn1_w_down": (ffn1_w_down[0], m_ffn1_w_down[0], v_ffn1_w_down[0], False),
        "w_in": (w_in[0], m_w_in[0], v_w_in[0], True),
        "w_out": (w_out[0], m_w_out[0], v_w_out[0], False),
        "ffn2_w_gate": (ffn2_w_gate[0], m_ffn2_w_gate[0], v_ffn2_w_gate[0], True),
        "ffn2_w_up": (ffn2_w_up[0], m_ffn2_w_up[0], v_ffn2_w_up[0], True),
        "ffn2_w_down": (ffn2_w_down[0], m_ffn2_w_down[0], v_ffn2_w_down[0], False),
    }

    def block_to_send(name):
        w, _, _, transposed = big[name]
        return (w.T if transposed else w).astype(BF16)

    names1 = ["ffn1_w_gate", "ffn1_w_up", "ffn1_w_down"]
    names_rest = ["w_in", "w_out", "ffn2_w_gate", "ffn2_w_up", "ffn2_w_down"]
    first = _gather_two_level([block_to_send(n) for n in names1], "gather_ffn1", ffn1_norm)
    wgt1, wut1, wd1 = [g.reshape(-1, D) for g in first]
    h_rest, token = _exchange_start([block_to_send(n) for n in names_rest[:2]] + [conv_w[0]]
                                    + [block_to_send(n) for n in names_rest[2:]], False,
                                    "gather_start_rest", wd1)

    x1, xn1, gu1 = _ffn_fwd(x0, ffn1_norm, wgt1, wut1, wd1, token, "ffn1_fwd", tm, tf)
    mixw = _exchange_wait(h_rest[:3], False, "gather_wait_mix", x1)
    win_t = mixw[0].reshape(-1, D)
    wout = mixw[1].reshape(-1, D)
    cw = _columns_of_blocks(mixw[2])
    hn, q, k, v, u, gb, gc = _mixin_fwd(x1, mix_norm, win_t, "mixin_fwd", tm_mix)
    bucket = jnp.asarray(_bucket_table().T.copy())
    sinks = attn_sinks.reshape(-1)
    bias_t = _bias_build(rel_bias_table, bucket, "bias_build")
    attn_raw, attn_n = _attn_fwd(q, k, v, bias_t, sinks, attn_out_norm, "attn_fwd", nblk)
    x2, conv_n = _mixout_fwd(x1, attn_n, u, gb, gc, cw, conv_out_norm, wout, "mixout_fwd", tm_mix)
    wgt2, wut2, wd2 = [g.reshape(-1, D) for g in _exchange_wait(h_rest[3:], False, "gather_wait_ffn2", x2)]
    dx3, xn2, gu2, d_final, loss_part = _ffn_fwd(x2, ffn2_norm, wgt2, wut2, wd2, x2, "ffn2_fwd", tm, tf,
                                                 head=(target, final_norm.reshape(1, D)))

    def blocks(g):
        return g.reshape(N_DEV, -1, D)

    dx2, d_ffn2_norm, dg2, du2, hh2, do2 = _ffn_bwd(
        dx3, x2, ffn2_norm, gu2, wgt2, wut2, wd2, dx3, "ffn2_bwd", tm_bwd, tf)
    d_wg2 = _tn_grad(dg2, xn2, dx2, "ffn2_wgrad_gate", BM_WGRAD, tk)
    d_wu2 = _tn_grad(du2, xn2, dx2, "ffn2_wgrad_up", BM_WGRAD, tk)
    d_wd2 = _tn_grad(hh2, do2, dx2, "ffn2_wgrad_down", BM_WGRAD, tk)
    handles2, token2 = _exchange_start([blocks(d_wg2), blocks(d_wu2), blocks(d_wd2)], True,
                                       "grads_start_ffn2", dx2)

    dmixed, d_wout = _mixout_bwd(dx2, attn_n, conv_n, wout, token2, "mixout_bwd", tm_mix)
    dq, dk, dv, dbias, dsink, d_attn_norm = _attn_bwd(
        dmixed, attn_raw, q, k, v, bias_t, sinks, attn_out_norm, "attn_bwd", nblk)
    du, dgb, dgc, d_cw, d_conv_norm = _conv_bwd(dmixed, u, gb, gc, cw, conv_out_norm, "conv_bwd", tm_mix)
    d_table = _bias_grad(dbias, bucket, "bias_grad")
    dz = [dq, dk, dv, du, dgb, dgc]
    dx1, d_mix_norm = _mixin_bwd(dx2, x1, mix_norm, dz, win_t, "mixin_bwd", tm_mix)
    d_win_t = _win_grad(dz, hn, "win_grad", min(TM_MIX, T))
    handles_mix, token_mix = _exchange_start([blocks(d_win_t), blocks(d_wout)], True, "grads_start_mix", d_table)

    dx0, d_ffn1_norm, dg1, du1, hh1, do1 = _ffn_bwd(
        dx1, x0, ffn1_norm, gu1, wgt1, wut1, wd1, token_mix, "ffn1_bwd", tm_bwd, tf)
    def pack(ffn1, mixn, ffn2, fin, attn_n_, conv_n_, sink_, extra, convw, table):
        rows = [_pad_row(ffn1), _pad_row(mixn), _pad_row(ffn2), _pad_row(fin),
                _pad_row(jnp.concatenate([attn_n_.reshape(-1), conv_n_.reshape(-1)])),
                _pad_row(sink_), _pad_row(extra),
                jnp.zeros((1, PACK_COLS), F32),
                jnp.pad(convw, ((0, 0), (0, PACK_COLS - convw.shape[1]))),
                _pad_row(table),
                jnp.zeros((PACK_ROWS - 12, PACK_COLS), F32)]
        return jnp.concatenate(rows, axis=0)

    def own_channels(a):
        full = jnp.zeros((a.shape[1], CONV_DIM), F32)
        return lax.dynamic_update_slice(full, a[0], (0, me * a.shape[2]))

    g_pack = pack(d_ffn1_norm, d_mix_norm, d_ffn2_norm, d_final, d_attn_norm, d_conv_norm,
                  dsink[:, 0], loss_part[0, :1], d_cw[:3], d_table[:, :N_Q_HEADS])
    zero1 = jnp.zeros((1,), F32)
    w_pack = pack(ffn1_norm, mix_norm, ffn2_norm, final_norm, attn_out_norm, conv_out_norm,
                  attn_sinks, zero1, own_channels(conv_w), rel_bias_table)
    m_pack = pack(m_ffn1_norm, m_mix_norm, m_ffn2_norm, m_final_norm, m_attn_out_norm, m_conv_out_norm,
                  m_attn_sinks, zero1, own_channels(m_conv_w), m_rel_bias_table)
    v_pack = pack(v_ffn1_norm, v_mix_norm, v_ffn2_norm, v_final_norm, v_attn_out_norm, v_conv_out_norm,
                  v_attn_sinks, zero1, own_channels(v_conv_w), v_rel_bias_table)
    (g_all,) = _exchange([g_pack], False, "gather_small", dx0)
    packs = _adamw(g_all, w_pack, m_pack, v_pack, "adamw_small", PACK_ROWS)

    d_wd1 = _tn_grad(hh1, do1, packs[0], "ffn1_wgrad_down", BM_WGRAD, tk)
    handles1_d, token1 = _exchange_start([blocks(d_wd1)], True, "grads_start_ffn1_down", dx0)
    d_wg1 = _tn_grad(dg1, xn1, token1, "ffn1_wgrad_gate", BM_WGRAD, tk)
    handles1_g, token1 = _exchange_start([blocks(d_wg1)], True, "grads_start_ffn1_gate", token1)
    d_wu1 = _tn_grad(du1, xn1, token1, "ffn1_wgrad_up", BM_WGRAD, tk)
    handles1_u, token1 = _exchange_start([blocks(d_wu1)], True, "grads_start_ffn1_up", token1)

    res = {}

    def update(names, parts):
        last = None
        for name, p in zip(names, parts):
            w, m_, v_, transposed = big[name]
            if transposed:
                w, m_, v_ = w.T, m_.T, v_.T
            new = _adamw(p, w, m_, v_, "adamw_" + name, _row_tile(w.shape[0], ADAM_ROWS))
            res[name] = tuple((a.T if transposed else a)[None] for a in new)
            last = new[0]
        return last

    parts2 = _exchange_wait(handles2, True, "grads_wait_ffn2", token1)
    done2 = update(["ffn2_w_gate", "ffn2_w_up", "ffn2_w_down"], parts2)
    parts_mix = _exchange_wait(handles_mix, True, "grads_wait_mix", done2)
    done_mix = update(["w_in", "w_out"], parts_mix)

    parts1_d = _exchange_wait(handles1_d, True, "grads_wait_ffn1_down", done_mix)
    done1 = update(["ffn1_w_down"], parts1_d)
    parts1_g = _exchange_wait(handles1_g, True, "grads_wait_ffn1_gate", done1)
    done1 = update(["ffn1_w_gate"], parts1_g)
    parts1_u = _exchange_wait(handles1_u, True, "grads_wait_ffn1_up", done1)
    update(["ffn1_w_up"], parts1_u)

    def unpack(pk):
        cwb = lax.dynamic_slice(pk[8:11, :CONV_DIM], (0, me * conv_w.shape[2]), (3, conv_w.shape[2]))
        return {
            "ffn1_norm": pk[0:1, :D], "mix_norm": pk[1:2, :D], "ffn2_norm": pk[2:3, :D],
            "final_norm": pk[3, :D],
            "attn_out_norm": pk[4:5, :ATTN_WIDTH], "conv_out_norm": pk[4:5, ATTN_WIDTH:ATTN_WIDTH + CONV_DIM],
            "attn_sinks": pk[5:6, :N_Q_HEADS],
            "conv_w": cwb[None],
            "rel_bias_table": pk[11, :NUM_BUCKETS * N_Q_HEADS].reshape(NUM_BUCKETS, N_Q_HEADS),
        }

    small = [unpack(pk) for pk in packs]
    loss = packs[0][6, 0]

    order = ["rel_bias_table", "ffn1_norm", "ffn1_w_gate", "ffn1_w_up", "ffn1_w_down", "mix_norm", "w_in",
             "conv_w", "attn_sinks", "attn_out_norm", "conv_out_norm", "w_out", "ffn2_norm",
             "ffn2_w_gate", "ffn2_w_up", "ffn2_w_down", "final_norm"]
    outs = [loss, dx0[None]]
    for kind in range(4):
        for name in order:
            outs.append(res[name][kind] if name in res else small[kind][name])
    return tuple(outs)
```

```python
import math

import numpy as np
import jax
import jax.numpy as jnp
from jax import lax
from jax.experimental import pallas as pl
from jax.experimental.pallas import tpu as pltpu

F32 = jnp.float32
BF16 = jnp.bfloat16

N_DEV = 8
EPS = 1e-6
HEAD_DIM = 64
N_Q_HEADS = 8
N_KV_HEADS = 2
GQA_GROUP = 4
ATTN_WIDTH = 512
KV_WIDTH = 128
CONV_DIM = 512
BLOCK = 128
WINDOW = 128
NUM_BUCKETS = 32
MAX_DISTANCE = 128
SCALE = HEAD_DIM ** -0.5
MASKED = -1e30
GROUP_ROWS = GQA_GROUP * BLOCK

ADAM_LR = 0.001
ADAM_B1 = 0.9
ADAM_B2 = 0.999
ADAM_EPS = 1e-08
ADAM_WD = 0.01
ADAM_STEP = 10

VMEM_LIMIT_BYTES = 40 * 1024 * 1024
SUBLANES = 8
PACK_ROWS = 16
PACK_COLS = 1024

TM_FFN = 1024
TM_FFN_BWD = 1024
TM_MIX = 512
TK_WGRAD = 1024
TF_FFN = 256
BM_WGRAD = 1408
ROW_GROUPS = 4
ATTN_BLOCKS = 4
ADAM_ROWS = 256


def _row_tile(rows, limit):
    best = rows
    for t in range(16, min(rows, limit) + 1, 16):
        if rows % t == 0:
            best = t
    return best


def _params(*sem):
    return pltpu.CompilerParams(dimension_semantics=sem, vmem_limit_bytes=VMEM_LIMIT_BYTES)


def _hbm(*arrays):
    return [pltpu.with_memory_space_constraint(a, pltpu.HBM) for a in arrays]


def _dot(a, b):
    return jnp.dot(a, b, preferred_element_type=F32)


def _dot_nt(a, b):
    return lax.dot_general(a, b, (((1,), (1,)), ((), ())), preferred_element_type=F32)


def _dot_tn(a, b):
    return lax.dot_general(a, b, (((0,), (0,)), ((), ())), preferred_element_type=F32)


def _sigmoid(g):
    return 0.5 * jnp.tanh(0.5 * g) + 0.5


def _rms_stats(x):
    inv = lax.rsqrt(jnp.mean(x * x, axis=-1, keepdims=True) + EPS)
    return inv, x * inv


def _rms_bwd(dy, x, gain):
    inv, xhat = _rms_stats(x)
    dgain = jnp.sum(dy * xhat, axis=0, keepdims=True)
    dxh = dy * gain
    dx = inv * (dxh - xhat * jnp.mean(dxh * xhat, axis=-1, keepdims=True))
    return dx, dgain


def _peer_list():
    x, y, c = lax.axis_index("x"), lax.axis_index("y"), lax.axis_index("c")
    peers = []
    for k in range(1, N_DEV):
        px = 1 - x if (k >> 2) & 1 else x
        py = 1 - y if (k >> 1) & 1 else y
        pc = 1 - c if k & 1 else c
        peers.append((px, py, pc))
    return 4 * x + 2 * y + c, peers


def _exchange(arrs, scatter, name, after):
    n = len(arrs)
    out_shape = []
    for a in arrs:
        shp = a.shape if scatter else (N_DEV,) + a.shape
        out_shape.append(jax.ShapeDtypeStruct(shp, a.dtype))

    def body(*refs):
        ins, outs = refs[:n], refs[n + 1:2 * n + 1]
        send_sems, recv_sems, local_sems = refs[2 * n + 1:]
        me, peers = _peer_list()
        started = []
        for a in range(n):
            own = ins[a].at[me] if scatter else ins[a]
            loc = pltpu.make_async_copy(own, outs[a].at[me], local_sems.at[a])
            loc.start()
            started.append(loc)
        sends = []
        for a in range(n):
            for k, (px, py, pc) in enumerate(peers):
                src = ins[a].at[4 * px + 2 * py + pc] if scatter else ins[a]
                cp = pltpu.make_async_remote_copy(
                    src_ref=src, dst_ref=outs[a].at[me],
                    send_sem=send_sems.at[a, k], recv_sem=recv_sems.at[a, k],
                    device_id=(px, py, pc), device_id_type=pl.DeviceIdType.MESH)
                cp.start()
                sends.append(cp)
        for a in range(n):
            for k, (px, py, pc) in enumerate(peers):
                landed = outs[a].at[4 * px + 2 * py + pc]
                pltpu.make_async_remote_copy(
                    src_ref=landed, dst_ref=landed,
                    send_sem=send_sems.at[a, k], recv_sem=recv_sems.at[a, k],
                    device_id=(px, py, pc), device_id_type=pl.DeviceIdType.MESH).wait_recv()
        for cp in sends:
            cp.wait_send()
        for loc in started:
            loc.wait()

    hbm = pl.BlockSpec(memory_space=pl.ANY)
    return pl.pallas_call(
        body, name=name, out_shape=tuple(out_shape),
        in_specs=[hbm] * (n + 1), out_specs=tuple([hbm] * n),
        scratch_shapes=[pltpu.SemaphoreType.DMA((n, N_DEV - 1)),
                        pltpu.SemaphoreType.DMA((n, N_DEV - 1)),
                        pltpu.SemaphoreType.DMA((n,))],
    )(*arrs, after)


def _gather_two_level(arrs, name, after):
    n = len(arrs)
    out_shape = tuple(jax.ShapeDtypeStruct((N_DEV,) + a.shape, a.dtype) for a in arrs)

    def body(*refs):
        ins, outs = refs[:n], refs[n + 1:2 * n + 1]
        send_sems, recv_sems, local_sems = refs[2 * n + 1:]
        x, y, c = lax.axis_index("x"), lax.axis_index("y"), lax.axis_index("c")
        me, sibling = (x, y, c), (x, y, 1 - c)
        chips = [(1 - x, y), (x, 1 - y), (1 - x, 1 - y)]

        def copy(a, k, block, to, src=None):
            slot = outs[a].at[4 * block[0] + 2 * block[1] + block[2]]
            return pltpu.make_async_remote_copy(
                src_ref=slot if src is None else src, dst_ref=slot,
                send_sem=send_sems.at[a, k], recv_sem=recv_sems.at[a, k],
                device_id=to, device_id_type=pl.DeviceIdType.MESH)

        started = []
        for a in range(n):
            loc = pltpu.make_async_copy(ins[a], outs[a].at[4 * x + 2 * y + c], local_sems.at[a])
            loc.start()
            started.append(loc)
        sends = []
        for a in range(n):
            sends.append(copy(a, 0, me, sibling, src=ins[a]))
            sends += [copy(a, 1 + j, me, (*chip, c), src=ins[a]) for j, chip in enumerate(chips)]
        for cp in sends:
            cp.start()
        for j, chip in enumerate(chips):
            for a in range(n):
                copy(a, 1 + j, (*chip, c), me).wait_recv()
                fwd = copy(a, 4 + j, (*chip, c), sibling)
                fwd.start()
                sends.append(fwd)
        for a in range(n):
            copy(a, 0, sibling, me).wait_recv()
            for j, chip in enumerate(chips):
                copy(a, 4 + j, (*chip, 1 - c), me).wait_recv()
        for cp in sends:
            cp.wait_send()
        for loc in started:
            loc.wait()

    hbm = pl.BlockSpec(memory_space=pl.ANY)
    return pl.pallas_call(
        body, name=name, out_shape=out_shape,
        in_specs=[hbm] * (n + 1), out_specs=tuple([hbm] * n),
        scratch_shapes=[pltpu.SemaphoreType.DMA((n, N_DEV - 1)),
                        pltpu.SemaphoreType.DMA((n, N_DEV - 1)),
                        pltpu.SemaphoreType.DMA((n,))],
    )(*arrs, after)


_HBM = pl.BlockSpec(memory_space=pltpu.HBM)
_SEM = pl.BlockSpec(memory_space=pltpu.SEMAPHORE)
_EFFECT = pltpu.SideEffectType.DATAFLOW_SIDE_EFFECTING


def _split_copies(srcs, lands, send_sems, recv_sems, scatter):
    me, peers = _peer_list()
    copies = []
    for a in range(len(srcs)):
        for k, (px, py, pc) in enumerate(peers):
            src = srcs[a].at[4 * px + 2 * py + pc] if scatter else srcs[a]
            copies.append(pltpu.make_async_remote_copy(
                src_ref=src, dst_ref=lands[a].at[me],
                send_sem=send_sems[a].at[k], recv_sem=recv_sems[a].at[k],
                device_id=(px, py, pc), device_id_type=pl.DeviceIdType.MESH))
    return copies


def _exchange_start(arrs, scatter, name, after):
    n = len(arrs)
    me = 4 * lax.axis_index("x") + 2 * lax.axis_index("y") + lax.axis_index("c")
    lands = []
    for a in arrs:
        own = lax.dynamic_index_in_dim(a, me, 0, keepdims=True) if scatter else a[None]
        shp = a.shape if scatter else (N_DEV,) + a.shape
        lands.append(lax.dynamic_update_slice(lax.empty(shp, a.dtype), own, (me,) + (0,) * (len(shp) - 1)))

    def body(*refs):
        srcs, lnds = refs[:n], refs[n:2 * n]
        outs = refs[2 * n + 1:]
        send_sems, recv_sems = outs[:n], outs[n:2 * n]
        token = outs[4 * n]
        for cp in _split_copies(srcs, lnds, send_sems, recv_sems, scatter):
            cp.start()
        token[...] = jnp.zeros_like(token)

    sem = pltpu.SemaphoreType.DMA((N_DEV - 1,))
    out_shape = ([sem] * (2 * n) + [pltpu.HBM(a.shape, a.dtype) for a in arrs]
                 + [pltpu.HBM(l.shape, l.dtype) for l in lands] + [jax.ShapeDtypeStruct((SUBLANES, 128), F32)])
    res = pl.pallas_call(
        body, name=name, out_shape=tuple(out_shape),
        in_specs=[_HBM] * (2 * n) + [pl.BlockSpec(memory_space=pl.ANY)],
        out_specs=tuple([_SEM] * (2 * n) + [_HBM] * (2 * n) + [pl.BlockSpec(memory_space=pltpu.VMEM)]),
        input_output_aliases={i: 2 * n + i for i in range(2 * n)},
        compiler_params=pltpu.CompilerParams(has_side_effects=_EFFECT),
    )(*[pltpu.with_memory_space_constraint(a, pltpu.HBM) for a in arrs],
      *[pltpu.with_memory_space_constraint(l, pltpu.HBM) for l in lands], after)
    handles = [(res[2 * n + a], res[3 * n + a], res[a], res[n + a]) for a in range(n)]
    return handles, res[4 * n]


def _exchange_wait(handles, scatter, name, after):
    n = len(handles)

    def body(*refs):
        srcs, lnds = refs[:n], refs[n:2 * n]
        send_sems, recv_sems = refs[2 * n:3 * n], refs[3 * n:4 * n]
        for cp in _split_copies(srcs, lnds, send_sems, recv_sems, scatter):
            cp.wait_send()
            cp.wait_recv()

    srcs = [h[0] for h in handles]
    lands = [h[1] for h in handles]
    res = pl.pallas_call(
        body, name=name,
        out_shape=tuple([pltpu.HBM(a.shape, a.dtype) for a in srcs] + [pltpu.HBM(l.shape, l.dtype) for l in lands]),
        in_specs=[_HBM] * (2 * n) + [_SEM] * (2 * n) + [pl.BlockSpec(memory_space=pl.ANY)],
        out_specs=tuple([_HBM] * (2 * n)),
        input_output_aliases={i: i for i in range(2 * n)},
        compiler_params=pltpu.CompilerParams(has_side_effects=_EFFECT),
    )(*srcs, *lands, *[h[2] for h in handles], *[h[3] for h in handles], after)
    return list(res[n:])


def _row_groups(tm):
    return [slice(r * (tm // ROW_GROUPS), (r + 1) * (tm // ROW_GROUPS)) for r in range(ROW_GROUPS)]


def _stack_gate_up(wgt, wut, tf):
    F, D = wgt.shape
    return jnp.stack([wgt.reshape(F // tf, tf, D), wut.reshape(F // tf, tf, D)], axis=1).reshape(2 * F, D)


def _ffn_fwd(x, gain, wgu, wd, after, name, tm, tf, head=None):
    T, D = x.shape
    F = wd.shape[0]
    nj = F // tf
    n_in = 7 if head else 5

    def body(*refs):
        x_ref, g_ref, wgu_ref, wd_ref, after_ref = refs[:5]
        outs = refs[n_in:]
        xo_ref, xn_ref, gu_ref = outs[:3]
        xn_sc, acc_sc = outs[-2:]
        i, j = pl.program_id(0), pl.program_id(1)

        @pl.when(j == 0)
        def _():
            _, xhat = _rms_stats(x_ref[...])
            xn = (xhat * g_ref[...]).astype(BF16)
            xn_sc[...] = xn
            xn_ref[...] = xn
            acc_sc[...] = jnp.zeros_like(acc_sc)

        groups = _row_groups(tm)
        gus = [_dot_nt(xn_sc[rows, :], wgu_ref[...]) for rows in groups]
        hs = []
        for rows, gu in zip(groups, gus):
            gu_ref[rows, :] = gu.astype(BF16)
            g, u = gu[:, :tf], gu[:, tf:]
            hs.append((g * _sigmoid(g) * u).astype(BF16))
        for rows, h in zip(groups, hs):
            acc_sc[rows, :] += _dot(h, wd_ref[...])

        if head:
            t_ref, fg_ref = refs[5:7]
            dgain_ref, loss_ref = outs[3:5]

            @pl.when((i == 0) & (j == 0))
            def _():
                dgain_ref[...] = jnp.zeros_like(dgain_ref)
                loss_ref[...] = jnp.zeros_like(loss_ref)

        @pl.when(j == nj - 1)
        def _():
            xo = x_ref[...] + 0.5 * acc_sc[...]
            if head:
                fg = fg_ref[...]
                _, xhat = _rms_stats(xo)
                err = xhat * fg - t_ref[...]
                loss_ref[...] += 0.5 * jnp.sum(jnp.mean(err * err, axis=-1, keepdims=True), axis=0, keepdims=True)
                dx, dgain = _rms_bwd(err * (1.0 / D), xo, fg)
                dgain_ref[...] += dgain
                xo_ref[...] = dx
            else:
                xo_ref[...] = xo

    tile = pl.BlockSpec((tm, D), lambda i, j: (i, 0), pipeline_mode=pl.Buffered(1))
    const = pl.BlockSpec((1, D), lambda i, j: (0, 0))
    in_specs = [tile, const, pl.BlockSpec((2 * tf, D), lambda i, j: (j, 0)),
                pl.BlockSpec((tf, D), lambda i, j: (j, 0)), pl.BlockSpec(memory_space=pl.ANY)]
    out_specs = [tile, pl.BlockSpec((tm, D), lambda i, j: (i, 0)), pl.BlockSpec((tm, 2 * tf), lambda i, j: (i, j))]
    out_shape = [jax.ShapeDtypeStruct((T, D), F32), jax.ShapeDtypeStruct((T, D), BF16),
                 jax.ShapeDtypeStruct((T, 2 * F), BF16)]
    operands = [x, gain, wgu, wd, after]
    if head:
        in_specs += [tile, const]
        out_specs += [const, pl.BlockSpec((SUBLANES, 128), lambda i, j: (0, 0))]
        out_shape += [jax.ShapeDtypeStruct((1, D), F32), jax.ShapeDtypeStruct((SUBLANES, 128), F32)]
        operands += list(head)
    return pl.pallas_call(
        body, name=name, grid=(T // tm, nj),
        in_specs=in_specs, out_specs=tuple(out_specs), out_shape=tuple(out_shape),
        scratch_shapes=[pltpu.VMEM((tm, D), BF16), pltpu.VMEM((tm, D), F32)],
        compiler_params=_params("arbitrary", "arbitrary"),
    )(*_hbm(*operands))


def _ffn_bwd(dy, x, gain, gu, wgu, wd, after, name, tm, tf):
    T, D = x.shape
    F = wd.shape[0]
    nj = F // tf

    def body(dy_ref, x_ref, g_ref, gu_ref, wgu_ref, wd_ref, after_ref,
             dx_ref, dgain_ref, dg_ref, du_ref, hh_ref, do_ref, do_sc, acc_sc, dgu_sc):
        i, j = pl.program_id(0), pl.program_id(1)

        @pl.when((i == 0) & (j == 0))
        def _():
            dgain_ref[...] = jnp.zeros_like(dgain_ref)

        @pl.when(j == 0)
        def _():
            do = (0.5 * dy_ref[...]).astype(BF16)
            do_sc[...] = do
            do_ref[...] = do
            acc_sc[...] = jnp.zeros_like(acc_sc)

        groups = _row_groups(tm)
        dhs = [_dot_nt(do_sc[rows, :], wd_ref[...]) for rows in groups]
        for rows, dh in zip(groups, dhs):
            g = gu_ref[rows, :tf].astype(F32)
            u = gu_ref[rows, tf:].astype(F32)
            sig = _sigmoid(g)
            s = g * sig
            dg = (dh * u * (sig + s * (1.0 - sig))).astype(BF16)
            du = (dh * s).astype(BF16)
            dg_ref[rows, :] = dg
            du_ref[rows, :] = du
            dgu_sc[rows, :tf] = dg
            dgu_sc[rows, tf:] = du
            hh_ref[rows, :] = (s * u).astype(BF16)
        for rows in groups:
            acc_sc[rows, :] += _dot(dgu_sc[rows, :], wgu_ref[...])

        @pl.when(j == nj - 1)
        def _():
            dx, dgain = _rms_bwd(acc_sc[...], x_ref[...], g_ref[...])
            dgain_ref[...] += dgain
            dx_ref[...] = dy_ref[...] + dx

    tile = pl.BlockSpec((tm, D), lambda i, j: (i, 0), pipeline_mode=pl.Buffered(1))
    return pl.pallas_call(
        body, name=name, grid=(T // tm, nj),
        in_specs=[tile, tile,
                  pl.BlockSpec((1, D), lambda i, j: (0, 0)),
                  pl.BlockSpec((tm, 2 * tf), lambda i, j: (i, j)),
                  pl.BlockSpec((2 * tf, D), lambda i, j: (j, 0)),
                  pl.BlockSpec((tf, D), lambda i, j: (j, 0)),
                  pl.BlockSpec(memory_space=pl.ANY)],
        out_specs=(tile,
                   pl.BlockSpec((1, D), lambda i, j: (0, 0)),
                   pl.BlockSpec((tm, tf), lambda i, j: (i, j)),
                   pl.BlockSpec((tm, tf), lambda i, j: (i, j)),
                   pl.BlockSpec((tm, tf), lambda i, j: (i, j)),
                   pl.BlockSpec((tm, D), lambda i, j: (i, 0))),
        out_shape=(jax.ShapeDtypeStruct((T, D), F32), jax.ShapeDtypeStruct((1, D), F32),
                   jax.ShapeDtypeStruct((T, F), BF16), jax.ShapeDtypeStruct((T, F), BF16),
                   jax.ShapeDtypeStruct((T, F), BF16), jax.ShapeDtypeStruct((T, D), BF16)),
        scratch_shapes=[pltpu.VMEM((tm, D), BF16), pltpu.VMEM((tm, D), F32), pltpu.VMEM((tm, 2 * tf), BF16)],
        compiler_params=_params("arbitrary", "arbitrary"),
    )(*_hbm(dy, x, gain, gu, wgu, wd, after))


def _tn_grad(a, b, after, name, bm, tk):
    T, M = a.shape
    D = b.shape[1]
    nk = T // tk

    def body(a_ref, b_ref, after_ref, o_ref, acc_sc):
        k = pl.program_id(1)

        @pl.when(k == 0)
        def _():
            acc_sc[...] = jnp.zeros_like(acc_sc)

        acc_sc[...] += _dot_tn(a_ref[...], b_ref[...])

        @pl.when(k == nk - 1)
        def _():
            o_ref[...] = acc_sc[...].astype(BF16)

    return pl.pallas_call(
        body, name=name, grid=(M // bm, nk),
        in_specs=[pl.BlockSpec((tk, bm), lambda i, k: (k, i)), pl.BlockSpec((tk, D), lambda i, k: (k, 0)),
                  pl.BlockSpec(memory_space=pl.ANY)],
        out_specs=pl.BlockSpec((bm, D), lambda i, k: (i, 0)),
        out_shape=jax.ShapeDtypeStruct((M, D), BF16),
        scratch_shapes=[pltpu.VMEM((bm, D), F32)],
        compiler_params=_params("arbitrary", "arbitrary"),
    )(*_hbm(a, b, after))


_Z_SPLITS = (0, 512, 640, 768, 1280, 1792, 2304)


def _mixin_fwd(x, gain, w_in_t, name, tm):
    T, D = x.shape
    widths = [b - a for a, b in zip(_Z_SPLITS[:-1], _Z_SPLITS[1:])]

    def body(x_ref, g_ref, w_ref, hn_ref, *outs):
        _, xhat = _rms_stats(x_ref[...])
        hn = (xhat * g_ref[...]).astype(BF16)
        hn_ref[...] = hn
        for o_ref, lo, hi in zip(outs, _Z_SPLITS[:-1], _Z_SPLITS[1:]):
            o_ref[...] = _dot_nt(hn, w_ref[lo:hi, :]).astype(BF16)

    return pl.pallas_call(
        body, name=name, grid=(T // tm,),
        in_specs=[pl.BlockSpec((tm, D), lambda i: (i, 0)),
                  pl.BlockSpec((1, D), lambda i: (0, 0)),
                  pl.BlockSpec(w_in_t.shape, lambda i: (0, 0))],
        out_specs=tuple([pl.BlockSpec((tm, D), lambda i: (i, 0))]
                        + [pl.BlockSpec((tm, w), lambda i: (i, 0)) for w in widths]),
        out_shape=tuple([jax.ShapeDtypeStruct((T, D), BF16)]
                        + [jax.ShapeDtypeStruct((T, w), BF16) for w in widths]),
        compiler_params=_params("arbitrary"),
    )(*_hbm(x, gain, w_in_t))


def _bucket_table():
    qi = np.arange(BLOCK, dtype=np.int32)[:, None]
    kj = np.arange(2 * BLOCK, dtype=np.int32)[None, :]
    dist = qi + BLOCK - kj
    n = np.maximum(dist, 0)
    max_exact = NUM_BUCKETS // 2
    large = max_exact + (np.log(np.maximum(n, 1).astype(np.float32) / max_exact)
                         / math.log(MAX_DISTANCE / max_exact)
                         * (NUM_BUCKETS - max_exact)).astype(np.int32)
    large = np.minimum(large, NUM_BUCKETS - 1)
    bucket = np.where(n < max_exact, n, large).astype(np.int32)
    valid = (dist >= 0) & (dist < WINDOW)
    return np.where(valid, bucket, -1).astype(np.int32)


def _bias_build(table, bucket, name):
    def body(t_ref, b_ref, o_ref):
        bk = b_ref[...]
        for h in range(N_Q_HEADS):
            def step(b, acc):
                return jnp.where(bk == b, t_ref[b, h], acc)
            o_ref[h] = lax.fori_loop(0, NUM_BUCKETS, step, jnp.full(bk.shape, MASKED, F32))

    return pl.pallas_call(
        body, name=name,
        in_specs=[pl.BlockSpec(memory_space=pltpu.SMEM), pl.BlockSpec(memory_space=pltpu.VMEM)],
        out_specs=pl.BlockSpec(memory_space=pltpu.VMEM),
        out_shape=jax.ShapeDtypeStruct((N_Q_HEADS,) + bucket.shape, F32),
    )(table, bucket)


def _bias_grad(dbias, bucket, name):
    def body(d_ref, b_ref, o_ref):
        bk = b_ref[...]
        row = lax.broadcasted_iota(jnp.int32, o_ref.shape, 0)
        lane = lax.broadcasted_iota(jnp.int32, o_ref.shape, 1)
        res = jnp.zeros(o_ref.shape, F32)
        for h in range(N_Q_HEADS):
            d = d_ref[h]

            def step(b, acc):
                part = jnp.sum(jnp.where(bk == b, d, 0.0), axis=0, keepdims=True)
                return jnp.where(row == b, part, acc)
            per_lane = lax.fori_loop(0, NUM_BUCKETS, step, jnp.zeros(o_ref.shape, F32))
            res = jnp.where(lane == h, jnp.sum(per_lane, axis=1, keepdims=True), res)
        o_ref[...] = res

    return pl.pallas_call(
        body, name=name,
        in_specs=[pl.BlockSpec(memory_space=pltpu.VMEM), pl.BlockSpec(memory_space=pltpu.VMEM)],
        out_specs=pl.BlockSpec(memory_space=pltpu.VMEM),
        out_shape=jax.ShapeDtypeStruct((NUM_BUCKETS, 128), F32),
    )(*_hbm(dbias, bucket))


def _head_cols(h):
    return slice(h * HEAD_DIM, (h + 1) * HEAD_DIM)


def _stack_heads(ref, r0, g, dtype):
    return jnp.concatenate(
        [ref[pl.ds(r0, BLOCK), _head_cols(GQA_GROUP * g + j)].astype(dtype) for j in range(GQA_GROUP)], axis=0)


def _unstack_heads(ref, r0, g, val):
    for j in range(GQA_GROUP):
        ref[pl.ds(r0, BLOCK), _head_cols(GQA_GROUP * g + j)] = val[j * BLOCK:(j + 1) * BLOCK, :]


def _head_lanes(h):
    return slice(h * BLOCK, (h + 1) * BLOCK)


def _group_lanes(g):
    return slice(g * GROUP_ROWS, (g + 1) * GROUP_ROWS)


def _head_softmax(st, bias_t, sink, no_prev):
    s = st * SCALE + bias_t
    row = lax.broadcasted_iota(jnp.int32, s.shape, 0)
    s = jnp.where(no_prev & (row < BLOCK), MASKED, s)
    m = jnp.maximum(jnp.max(s, axis=0, keepdims=True), sink)
    p = jnp.exp(s - m)
    ps = jnp.exp(sink - m)
    r = 1.0 / (jnp.sum(p, axis=0, keepdims=True) + ps)
    return p * r, ps * r


def _load_band(kf_sc, vf_sc, kp_ref, kc_ref, vp_ref, vc_ref, tq):
    kf_sc[0:BLOCK, :] = kp_ref[...]
    kf_sc[BLOCK:BLOCK + tq, :] = kc_ref[...]
    vf_sc[0:BLOCK, :] = vp_ref[...]
    vf_sc[BLOCK:BLOCK + tq, :] = vc_ref[...]


def _attn_fwd(q, k, v, bias_t, sinks, gain, name, nblk):
    T = q.shape[0]
    tq = nblk * BLOCK

    def body(sink_ref, q_ref, kc_ref, kp_ref, vc_ref, vp_ref, bias_ref, g_ref, raw_ref, nrm_ref,
             kf_sc, vf_sc, o_sc, st_sc, pt_sc):
        i = pl.program_id(0)
        _load_band(kf_sc, vf_sc, kp_ref, kc_ref, vp_ref, vc_ref, tq)

        def block(b, carry):
            r0 = pl.multiple_of(b * BLOCK, BLOCK)
            no_prev = (i == 0) & (b == 0)
            for g in range(N_KV_HEADS):
                kb = kf_sc[pl.ds(r0, 2 * BLOCK), _head_cols(g)]
                st_sc[:, _group_lanes(g)] = _dot_nt(kb, _stack_heads(q_ref, r0, g, BF16))
            for h in range(N_Q_HEADS):
                p, _ = _head_softmax(st_sc[:, _head_lanes(h)], bias_ref[h], sink_ref[h], no_prev)
                pt_sc[:, _head_lanes(h)] = p.astype(BF16)
            for g in range(N_KV_HEADS):
                vb = vf_sc[pl.ds(r0, 2 * BLOCK), _head_cols(g)]
                _unstack_heads(o_sc, r0, g, _dot_tn(pt_sc[:, _group_lanes(g)], vb))
            return carry

        lax.fori_loop(0, nblk, block, 0)
        o = o_sc[...]
        raw_ref[...] = o.astype(BF16)
        _, ohat = _rms_stats(o)
        nrm_ref[...] = (ohat * g_ref[...]).astype(BF16)

    cur = lambda i: (i, 0)
    prev = lambda i: (jnp.maximum(i * nblk - 1, 0), 0)
    lanes = N_Q_HEADS * BLOCK
    return pl.pallas_call(
        body, name=name, grid=(T // tq,),
        in_specs=[pl.BlockSpec(memory_space=pltpu.SMEM),
                  pl.BlockSpec((tq, ATTN_WIDTH), cur),
                  pl.BlockSpec((tq, KV_WIDTH), cur), pl.BlockSpec((BLOCK, KV_WIDTH), prev),
                  pl.BlockSpec((tq, KV_WIDTH), cur), pl.BlockSpec((BLOCK, KV_WIDTH), prev),
                  pl.BlockSpec(bias_t.shape, lambda i: (0, 0, 0)),
                  pl.BlockSpec((1, ATTN_WIDTH), lambda i: (0, 0))],
        out_specs=(pl.BlockSpec((tq, ATTN_WIDTH), cur), pl.BlockSpec((tq, ATTN_WIDTH), cur)),
        out_shape=(jax.ShapeDtypeStruct((T, ATTN_WIDTH), BF16), jax.ShapeDtypeStruct((T, ATTN_WIDTH), BF16)),
        scratch_shapes=[pltpu.VMEM((tq + BLOCK, KV_WIDTH), BF16), pltpu.VMEM((tq + BLOCK, KV_WIDTH), BF16),
                        pltpu.VMEM((tq, ATTN_WIDTH), F32),
                        pltpu.VMEM((2 * BLOCK, lanes), F32), pltpu.VMEM((2 * BLOCK, lanes), BF16)],
        compiler_params=_params("arbitrary"),
    )(sinks, *_hbm(q, k, k, v, v, bias_t, gain))


def _attn_bwd(dmixed, raw, q, k, v, bias_t, sinks, gain, name, nblk):
    T = q.shape[0]
    tq = nblk * BLOCK
    nt = T // tq
    lanes = N_Q_HEADS * BLOCK

    def body(sink_ref, dm_ref, raw_ref, q_ref, kc_ref, kp_ref, vc_ref, vp_ref, bias_ref, g_ref,
             dq_ref, dk_ref, dv_ref, dbias_ref, dsink_ref, dgain_ref,
             do_sc, dq_sc, kf_sc, vf_sc, dkf_sc, dvf_sc, st_sc, dpt_sc, pt_sc, dst_sc, drow_sc,
             qs_sc, dos_sc, dsink_sc):
        i = pl.program_id(0)
        tile = nt - 1 - i

        @pl.when(i == 0)
        def _():
            dkf_sc[...] = jnp.zeros_like(dkf_sc)
            dvf_sc[...] = jnp.zeros_like(dvf_sc)
            dsink_sc[...] = jnp.zeros_like(dsink_sc)
            dbias_ref[...] = jnp.zeros_like(dbias_ref)
            dgain_ref[...] = jnp.zeros_like(dgain_ref)

        carry_k = dkf_sc[0:BLOCK, :]
        carry_v = dvf_sc[0:BLOCK, :]
        dkf_sc[0:tq, :] = jnp.zeros((tq, KV_WIDTH), F32)
        dvf_sc[0:tq, :] = jnp.zeros((tq, KV_WIDTH), F32)
        dkf_sc[tq:tq + BLOCK, :] = carry_k
        dvf_sc[tq:tq + BLOCK, :] = carry_v
        _load_band(kf_sc, vf_sc, kp_ref, kc_ref, vp_ref, vc_ref, tq)

        do, dgain = _rms_bwd(dm_ref[...].astype(F32), raw_ref[...].astype(F32), g_ref[...])
        dgain_ref[...] += dgain
        do_sc[...] = do
        ones = jnp.ones((SUBLANES, HEAD_DIM), BF16)

        def block(b, carry):
            r0 = pl.multiple_of(b * BLOCK, BLOCK)
            no_prev = (tile == 0) & (b == 0)
            for g in range(N_KV_HEADS):
                kb = kf_sc[pl.ds(r0, 2 * BLOCK), _head_cols(g)]
                vb = vf_sc[pl.ds(r0, 2 * BLOCK), _head_cols(g)]
                qg = _stack_heads(q_ref, r0, g, BF16)
                dog = _stack_heads(do_sc, r0, g, F32)
                prod = dog * _stack_heads(raw_ref, r0, g, F32)
                hi = prod.astype(BF16)
                lo = (prod - hi.astype(F32)).astype(BF16)
                drow_sc[:, _group_lanes(g)] = _dot_nt(ones, hi) + _dot_nt(ones, lo)
                dogb = dog.astype(BF16)
                qs_sc[g] = qg
                dos_sc[g] = dogb
                st_sc[:, _group_lanes(g)] = _dot_nt(kb, qg)
                dpt_sc[:, _group_lanes(g)] = _dot_nt(vb, dogb)
            for h in range(N_Q_HEADS):
                hl = _head_lanes(h)
                p, ps = _head_softmax(st_sc[:, hl], bias_ref[h], sink_ref[h], no_prev)
                rowdot = drow_sc[0:1, hl]
                ds = p * (dpt_sc[:, hl] - rowdot)
                dsink_sc[h:h + 1, :] += -(ps * rowdot)
                dbias_ref[h] += ds
                dst_sc[:, hl] = ds.astype(BF16)
                pt_sc[:, hl] = p.astype(BF16)
            for g in range(N_KV_HEADS):
                kb = kf_sc[pl.ds(r0, 2 * BLOCK), _head_cols(g)]
                dsg = dst_sc[:, _group_lanes(g)]
                _unstack_heads(dq_sc, r0, g, _dot_tn(dsg, kb) * SCALE)
                dkf_sc[pl.ds(r0, 2 * BLOCK), _head_cols(g)] += _dot(dsg, qs_sc[g]) * SCALE
                dvf_sc[pl.ds(r0, 2 * BLOCK), _head_cols(g)] += _dot(pt_sc[:, _group_lanes(g)], dos_sc[g])
            return carry

        lax.fori_loop(0, nblk, block, 0)
        dq_ref[...] = dq_sc[...].astype(BF16)
        dk_ref[...] = dkf_sc[BLOCK:BLOCK + tq, :].astype(BF16)
        dv_ref[...] = dvf_sc[BLOCK:BLOCK + tq, :].astype(BF16)

        @pl.when(i == nt - 1)
        def _():
            tot = jnp.sum(dsink_sc[...], axis=1, keepdims=True)
            dsink_ref[...] = jnp.broadcast_to(tot, dsink_ref.shape)

    cur = lambda i: (nt - 1 - i, 0)
    prev = lambda i: (jnp.maximum((nt - 1 - i) * nblk - 1, 0), 0)
    const2 = lambda i: (0, 0)
    const3 = lambda i: (0, 0, 0)
    return pl.pallas_call(
        body, name=name, grid=(nt,),
        in_specs=[pl.BlockSpec(memory_space=pltpu.SMEM),
                  pl.BlockSpec((tq, ATTN_WIDTH), cur),
                  pl.BlockSpec((tq, ATTN_WIDTH), cur),
                  pl.BlockSpec((tq, ATTN_WIDTH), cur),
                  pl.BlockSpec((tq, KV_WIDTH), cur), pl.BlockSpec((BLOCK, KV_WIDTH), prev),
                  pl.BlockSpec((tq, KV_WIDTH), cur), pl.BlockSpec((BLOCK, KV_WIDTH), prev),
                  pl.BlockSpec(bias_t.shape, const3),
                  pl.BlockSpec((1, ATTN_WIDTH), const2)],
        out_specs=(pl.BlockSpec((tq, ATTN_WIDTH), cur),
                   pl.BlockSpec((tq, KV_WIDTH), cur), pl.BlockSpec((tq, KV_WIDTH), cur),
                   pl.BlockSpec(bias_t.shape, const3),
                   pl.BlockSpec((N_Q_HEADS, 128), const2),
                   pl.BlockSpec((1, ATTN_WIDTH), const2)),
        out_shape=(jax.ShapeDtypeStruct((T, ATTN_WIDTH), BF16),
                   jax.ShapeDtypeStruct((T, KV_WIDTH), BF16), jax.ShapeDtypeStruct((T, KV_WIDTH), BF16),
                   jax.ShapeDtypeStruct(bias_t.shape, F32),
                   jax.ShapeDtypeStruct((N_Q_HEADS, 128), F32),
                   jax.ShapeDtypeStruct((1, ATTN_WIDTH), F32)),
        scratch_shapes=[pltpu.VMEM((tq, ATTN_WIDTH), F32), pltpu.VMEM((tq, ATTN_WIDTH), F32),
                        pltpu.VMEM((tq + BLOCK, KV_WIDTH), BF16), pltpu.VMEM((tq + BLOCK, KV_WIDTH), BF16),
                        pltpu.VMEM((tq + BLOCK, KV_WIDTH), F32), pltpu.VMEM((tq + BLOCK, KV_WIDTH), F32),
                        pltpu.VMEM((2 * BLOCK, lanes), F32), pltpu.VMEM((2 * BLOCK, lanes), F32),
                        pltpu.VMEM((2 * BLOCK, lanes), BF16), pltpu.VMEM((2 * BLOCK, lanes), BF16),
                        pltpu.VMEM((SUBLANES, lanes), F32),
                        pltpu.VMEM((N_KV_HEADS, GROUP_ROWS, HEAD_DIM), BF16),
                        pltpu.VMEM((N_KV_HEADS, GROUP_ROWS, HEAD_DIM), BF16),
                        pltpu.VMEM((N_Q_HEADS, 128), F32)],
        compiler_params=_params("arbitrary"),
    )(sinks, *_hbm(dmixed, raw, q, k, k, v, v, bias_t, gain))


def _shift_down(cu, tail):
    row = lax.broadcasted_iota(jnp.int32, cu.shape, 0)
    t6, t7 = tail[6:7, :], tail[7:8, :]
    s1 = jnp.where(row == 0, t7, pltpu.roll(cu, 1, 0))
    s2 = jnp.where(row == 0, t6, jnp.where(row == 1, t7, pltpu.roll(cu, 2, 0)))
    return s1, s2


def _shift_up(d, head):
    n = d.shape[0]
    row = lax.broadcasted_iota(jnp.int32, d.shape, 0)
    h0, h1 = head[0:1, :], head[1:2, :]
    s1 = jnp.where(row == n - 1, h0, pltpu.roll(d, n - 1, 0))
    s2 = jnp.where(row == n - 1, h1, jnp.where(row == n - 2, h0, pltpu.roll(d, n - 2, 0)))
    return s1, s2


def _mixout_fwd(x, attn_n, u, gb, gc, conv_w, gain, w_out, name, tm):
    T, D = x.shape

    def body(x_ref, an_ref, u_ref, b_ref, c_ref, cw_ref, g_ref, wo_ref, xo_ref, cn_ref, tail_sc):
        @pl.when(pl.program_id(0) == 0)
        def _():
            tail_sc[...] = jnp.zeros_like(tail_sc)

        cu = c_ref[...].astype(F32) * u_ref[...].astype(F32)
        s1, s2 = _shift_down(cu, tail_sc[...])
        tail_sc[...] = cu[tm - SUBLANES:tm, :]
        pre = cw_ref[0:1, :] * s2 + cw_ref[1:2, :] * s1 + cw_ref[2:3, :] * cu
        conv = b_ref[...].astype(F32) * pre
        _, chat = _rms_stats(conv)
        cn = (chat * g_ref[...]).astype(BF16)
        cn_ref[...] = cn
        xo_ref[...] = (x_ref[...] + _dot(an_ref[...], wo_ref[0:ATTN_WIDTH, :])
                       + _dot(cn, wo_ref[ATTN_WIDTH:ATTN_WIDTH + CONV_DIM, :]))

    row = lambda i: (i, 0)
    const = lambda i: (0, 0)
    return pl.pallas_call(
        body, name=name, grid=(T // tm,),
        in_specs=[pl.BlockSpec((tm, D), row), pl.BlockSpec((tm, ATTN_WIDTH), row),
                  pl.BlockSpec((tm, CONV_DIM), row), pl.BlockSpec((tm, CONV_DIM), row),
                  pl.BlockSpec((tm, CONV_DIM), row),
                  pl.BlockSpec(conv_w.shape, const), pl.BlockSpec((1, CONV_DIM), const),
                  pl.BlockSpec(w_out.shape, const)],
        out_specs=(pl.BlockSpec((tm, D), row), pl.BlockSpec((tm, CONV_DIM), row)),
        out_shape=(jax.ShapeDtypeStruct((T, D), F32), jax.ShapeDtypeStruct((T, CONV_DIM), BF16)),
        scratch_shapes=[pltpu.VMEM((SUBLANES, CONV_DIM), F32)],
        compiler_params=_params("arbitrary"),
    )(*_hbm(x, attn_n, u, gb, gc, conv_w, gain, w_out))


def _mixout_bwd(dy, attn_n, conv_n, w_out, after, name, tm):
    T, D = dy.shape
    W = ATTN_WIDTH + CONV_DIM
    nt = T // tm

    def body(dy_ref, an_ref, cn_ref, w_ref, after_ref, dm_ref, dw_ref, dw_sc):
        i = pl.program_id(0)

        @pl.when(i == 0)
        def _():
            dw_sc[...] = jnp.zeros_like(dw_sc)

        dyb = dy_ref[...].astype(BF16)
        dm_ref[...] = _dot_nt(dyb, w_ref[...]).astype(BF16)
        dw_sc[0:ATTN_WIDTH, :] += _dot_tn(an_ref[...], dyb)
        dw_sc[ATTN_WIDTH:W, :] += _dot_tn(cn_ref[...], dyb)

        @pl.when(i == nt - 1)
        def _():
            dw_ref[...] = dw_sc[...].astype(BF16)

    row = lambda i: (i, 0)
    const = lambda i: (0, 0)
    return pl.pallas_call(
        body, name=name, grid=(nt,),
        in_specs=[pl.BlockSpec((tm, D), row), pl.BlockSpec((tm, ATTN_WIDTH), row),
                  pl.BlockSpec((tm, CONV_DIM), row), pl.BlockSpec(w_out.shape, const),
                  pl.BlockSpec(memory_space=pl.ANY)],
        out_specs=(pl.BlockSpec((tm, W), row), pl.BlockSpec((W, D), const)),
        out_shape=(jax.ShapeDtypeStruct((T, W), BF16), jax.ShapeDtypeStruct((W, D), BF16)),
        scratch_shapes=[pltpu.VMEM((W, D), F32)],
        compiler_params=_params("arbitrary"),
    )(*_hbm(dy, attn_n, conv_n, w_out, after))


def _conv_bwd(dmixed, u, gb, gc, conv_w, gain, name, tc):
    T = u.shape[0]
    nt = T // tc
    per8 = tc // SUBLANES

    def body(dm_ref, u_ref, b_ref, c_ref, ut_ref, ct_ref, cw_ref, g_ref,
             du_ref, db_ref, dc_ref, dcw_ref, dgain_ref, head_sc):
        i = pl.program_id(0)

        @pl.when(i == 0)
        def _():
            head_sc[...] = jnp.zeros_like(head_sc)
            dcw_ref[...] = jnp.zeros_like(dcw_ref)
            dgain_ref[...] = jnp.zeros_like(dgain_ref)

        uu = u_ref[...].astype(F32)
        cc = c_ref[...].astype(F32)
        bb = b_ref[...].astype(F32)
        cu = cc * uu
        tail = jnp.where(i == nt - 1, 0.0, ct_ref[...].astype(F32) * ut_ref[...].astype(F32))
        s1, s2 = _shift_down(cu, tail)
        w0, w1, w2 = cw_ref[0:1, :], cw_ref[1:2, :], cw_ref[2:3, :]
        pre = w0 * s2 + w1 * s1 + w2 * cu
        dconv, dgain = _rms_bwd(dm_ref[...].astype(F32), bb * pre, g_ref[...])
        dgain_ref[...] += dgain
        db_ref[...] = (dconv * pre).astype(BF16)
        dpre = dconv * bb
        dcw_ref[0:1, :] += jnp.sum(dpre * s2, axis=0, keepdims=True)
        dcw_ref[1:2, :] += jnp.sum(dpre * s1, axis=0, keepdims=True)
        dcw_ref[2:3, :] += jnp.sum(dpre * cu, axis=0, keepdims=True)
        n1, n2 = _shift_up(dpre, head_sc[...])
        head_sc[...] = dpre[0:SUBLANES, :]
        dcu = w2 * dpre + w1 * n1 + w0 * n2
        du_ref[...] = (dcu * cc).astype(BF16)
        dc_ref[...] = (dcu * uu).astype(BF16)

    rev = lambda i: (nt - 1 - i, 0)
    rev_right = lambda i: (nt - 1 - i, 1)
    tail_map = lambda i: (jnp.maximum((nt - 1 - i) * per8 - 1, 0), 0)
    const = lambda i: (0, 0)
    return pl.pallas_call(
        body, name=name, grid=(nt,),
        in_specs=[pl.BlockSpec((tc, CONV_DIM), rev_right),
                  pl.BlockSpec((tc, CONV_DIM), rev), pl.BlockSpec((tc, CONV_DIM), rev),
                  pl.BlockSpec((tc, CONV_DIM), rev),
                  pl.BlockSpec((SUBLANES, CONV_DIM), tail_map), pl.BlockSpec((SUBLANES, CONV_DIM), tail_map),
                  pl.BlockSpec(conv_w.shape, const), pl.BlockSpec((1, CONV_DIM), const)],
        out_specs=(pl.BlockSpec((tc, CONV_DIM), rev), pl.BlockSpec((tc, CONV_DIM), rev),
                   pl.BlockSpec((tc, CONV_DIM), rev),
                   pl.BlockSpec((SUBLANES, CONV_DIM), const), pl.BlockSpec((1, CONV_DIM), const)),
        out_shape=(jax.ShapeDtypeStruct((T, CONV_DIM), BF16), jax.ShapeDtypeStruct((T, CONV_DIM), BF16),
                   jax.ShapeDtypeStruct((T, CONV_DIM), BF16),
                   jax.ShapeDtypeStruct((SUBLANES, CONV_DIM), F32), jax.ShapeDtypeStruct((1, CONV_DIM), F32)),
        scratch_shapes=[pltpu.VMEM((SUBLANES, CONV_DIM), F32)],
        compiler_params=_params("arbitrary"),
    )(*_hbm(dmixed, u, gb, gc, u, gc, conv_w, gain))


def _mixin_bwd(dy, x, gain, dz, w_in_t, name, tm):
    T, D = x.shape
    nz = len(dz)

    def body(dy_ref, x_ref, g_ref, *rest):
        dz_refs, wt_ref, dx_ref, dgain_ref = rest[:nz], rest[nz], rest[nz + 1], rest[nz + 2]

        @pl.when(pl.program_id(0) == 0)
        def _():
            dgain_ref[...] = jnp.zeros_like(dgain_ref)

        dh = jnp.zeros((tm, D), F32)
        for r, lo, hi in zip(dz_refs, _Z_SPLITS[:-1], _Z_SPLITS[1:]):
            dh += _dot(r[...], wt_ref[lo:hi, :])
        dx, dgain = _rms_bwd(dh, x_ref[...], g_ref[...])
        dgain_ref[...] += dgain
        dx_ref[...] = dy_ref[...] + dx

    row = lambda i: (i, 0)
    const = lambda i: (0, 0)
    return pl.pallas_call(
        body, name=name, grid=(T // tm,),
        in_specs=[pl.BlockSpec((tm, D), row), pl.BlockSpec((tm, D), row), pl.BlockSpec((1, D), const)]
                 + [pl.BlockSpec((tm, a.shape[1]), row) for a in dz]
                 + [pl.BlockSpec(w_in_t.shape, const)],
        out_specs=(pl.BlockSpec((tm, D), row), pl.BlockSpec((1, D), const)),
        out_shape=(jax.ShapeDtypeStruct((T, D), F32), jax.ShapeDtypeStruct((1, D), F32)),
        compiler_params=_params("arbitrary"),
    )(*_hbm(dy, x, gain, *dz, w_in_t))


def _win_grad(dz, hn, name, tk):
    T, D = hn.shape
    nz = len(dz)
    nt = T // tk
    W = _Z_SPLITS[-1]

    def body(hn_ref, *rest):
        dz_refs, dw_ref, dw_sc = rest[:nz], rest[nz], rest[nz + 1]
        i = pl.program_id(0)

        @pl.when(i == 0)
        def _():
            dw_sc[...] = jnp.zeros_like(dw_sc)

        hn = hn_ref[...]
        for r, lo, hi in zip(dz_refs, _Z_SPLITS[:-1], _Z_SPLITS[1:]):
            dw_sc[lo:hi, :] += _dot_tn(r[...], hn)

        @pl.when(i == nt - 1)
        def _():
            dw_ref[...] = dw_sc[...].astype(BF16)

    row = lambda i: (i, 0)
    return pl.pallas_call(
        body, name=name, grid=(nt,),
        in_specs=[pl.BlockSpec((tk, D), row)] + [pl.BlockSpec((tk, a.shape[1]), row) for a in dz],
        out_specs=pl.BlockSpec((W, D), lambda i: (0, 0)),
        out_shape=jax.ShapeDtypeStruct((W, D), BF16),
        scratch_shapes=[pltpu.VMEM((W, D), F32)],
        compiler_params=_params("arbitrary"),
    )(*_hbm(hn, *dz))


def _adamw(parts, w, m, v, name, tr):
    P = parts.shape[0]
    R, C = w.shape

    def body(p_ref, w_ref, m_ref, v_ref, g_ref, d_ref, nm_ref, nv_ref):
        g = p_ref[0].astype(F32)
        for d in range(1, P):
            g = g + p_ref[d].astype(F32)
        nm = ADAM_B1 * m_ref[...] + (1.0 - ADAM_B1) * g
        nv = ADAM_B2 * v_ref[...] + (1.0 - ADAM_B2) * (g * g)
        m_hat = nm / (1.0 - ADAM_B1 ** ADAM_STEP)
        v_hat = nv / (1.0 - ADAM_B2 ** ADAM_STEP)
        g_ref[...] = g
        nm_ref[...] = nm
        nv_ref[...] = nv
        d_ref[...] = -ADAM_LR * (m_hat / (jnp.sqrt(v_hat) + ADAM_EPS) + ADAM_WD * w_ref[...])

    row = lambda i: (i, 0)
    spec = pl.BlockSpec((tr, C), row)
    shp = jax.ShapeDtypeStruct((R, C), F32)
    return pl.pallas_call(
        body, name=name, grid=(R // tr,),
        in_specs=[pl.BlockSpec((P, tr, C), lambda i: (0, i, 0)), spec, spec, spec],
        out_specs=(spec, spec, spec, spec),
        out_shape=(shp, shp, shp, shp),
        compiler_params=_params("arbitrary"),
    )(*_hbm(parts, w, m, v))


def _columns_of_blocks(g):
    n, R, w = g.shape
    return g.transpose(1, 0, 2).reshape(R, n * w)


def _pad_row(vec):
    vec = vec.reshape(1, -1)
    return jnp.pad(vec, ((0, 0), (0, PACK_COLS - vec.shape[1])))


def kernel(x, rel_bias_table, ffn1_norm, ffn1_w_gate, ffn1_w_up, ffn1_w_down, mix_norm, w_in, conv_w, attn_sinks, attn_out_norm, conv_out_norm, w_out, ffn2_norm, ffn2_w_gate, ffn2_w_up, ffn2_w_down, final_norm, loss_target, m_rel_bias_table, m_ffn1_norm, m_ffn1_w_gate, m_ffn1_w_up, m_ffn1_w_down, m_mix_norm, m_w_in, m_conv_w, m_attn_sinks, m_attn_out_norm, m_conv_out_norm, m_w_out, m_ffn2_norm, m_ffn2_w_gate, m_ffn2_w_up, m_ffn2_w_down, m_final_norm, v_rel_bias_table, v_ffn1_norm, v_ffn1_w_gate, v_ffn1_w_up, v_ffn1_w_down, v_mix_norm, v_w_in, v_conv_w, v_attn_sinks, v_attn_out_norm, v_conv_out_norm, v_w_out, v_ffn2_norm, v_ffn2_w_gate, v_ffn2_w_up, v_ffn2_w_down, v_final_norm):
    T, D = x.shape[1], x.shape[2]
    x0 = x[0]
    target = loss_target[0]
    tm = min(TM_FFN, T)
    tm_bwd = min(TM_FFN_BWD, T)
    tm_mix = min(TM_MIX, T)
    tk = min(TK_WGRAD, T)
    tf = TF_FFN
    nblk = min(ATTN_BLOCKS, T // BLOCK)
    me = 4 * lax.axis_index("x") + 2 * lax.axis_index("y") + lax.axis_index("c")

    big = {
        "ffn1_w_gate": (ffn1_w_gate[0], m_ffn1_w_gate[0], v_ffn1_w_gate[0], True),
        "ffn1_w_up": (ffn1_w_up[0], m_ffn1_w_up[0], v_ffn1_w_up[0], True),
        "ffn1_w_down": (ffn1_w_down[0], m_ffn1_w_down[0], v_ffn1_w_down[0], False),
        "w_in": (w_in[0], m_w_in[0], v_w_in[0], True),
        "w_out": (w_out[0], m_w_out[0], v_w_out[0], False),
        "ffn2_w_gate": (ffn2_w_gate[0], m_ffn2_w_gate[0], v_ffn2_w_gate[0], True),
        "ffn2_w_up": (ffn2_w_up[0], m_ffn2_w_up[0], v_ffn2_w_up[0], True),
        "ffn2_w_down": (ffn2_w_down[0], m_ffn2_w_down[0], v_ffn2_w_down[0], False),
    }

    def block_to_send(name):
        w, _, _, transposed = big[name]
        return (w.T if transposed else w).astype(BF16)

    names1 = ["ffn1_w_gate", "ffn1_w_up", "ffn1_w_down"]
    names_rest = ["w_in", "w_out", "ffn2_w_gate", "ffn2_w_up", "ffn2_w_down"]
    first = _gather_two_level([block_to_send(n) for n in names1], "gather_ffn1", ffn1_norm)
    wgt1, wut1, wd1 = [g.reshape(-1, D) for g in first]
    h_rest, token = _exchange_start([block_to_send(n) for n in names_rest[:2]] + [conv_w[0]]
                                    + [block_to_send(n) for n in names_rest[2:]], False,
                                    "gather_start_rest", wd1)

    wgu1 = _stack_gate_up(wgt1, wut1, tf)
    x1, xn1, gu1 = _ffn_fwd(x0, ffn1_norm, wgu1, wd1, token, "ffn1_fwd", tm, tf)
    mixw = _exchange_wait(h_rest[:3], False, "gather_wait_mix", x1)
    win_t = mixw[0].reshape(-1, D)
    wout = mixw[1].reshape(-1, D)
    cw = _columns_of_blocks(mixw[2])
    hn, q, k, v, u, gb, gc = _mixin_fwd(x1, mix_norm, win_t, "mixin_fwd", tm_mix)
    bucket = jnp.asarray(_bucket_table().T.copy())
    sinks = attn_sinks.reshape(-1)
    bias_t = _bias_build(rel_bias_table, bucket, "bias_build")
    attn_raw, attn_n = _attn_fwd(q, k, v, bias_t, sinks, attn_out_norm, "attn_fwd", nblk)
    x2, conv_n = _mixout_fwd(x1, attn_n, u, gb, gc, cw, conv_out_norm, wout, "mixout_fwd", tm_mix)
    wgt2, wut2, wd2 = [g.reshape(-1, D) for g in _exchange_wait(h_rest[3:], False, "gather_wait_ffn2", x2)]
    wgu2 = _stack_gate_up(wgt2, wut2, tf)
    dx3, xn2, gu2, d_final, loss_part = _ffn_fwd(x2, ffn2_norm, wgu2, wd2, x2, "ffn2_fwd", tm, tf,
                                                 head=(target, final_norm.reshape(1, D)))

    def blocks(g):
        return g.reshape(N_DEV, -1, D)

    dx2, d_ffn2_norm, dg2, du2, hh2, do2 = _ffn_bwd(
        dx3, x2, ffn2_norm, gu2, wgu2, wd2, dx3, "ffn2_bwd", tm_bwd, tf)
    d_wg2 = _tn_grad(dg2, xn2, dx2, "ffn2_wgrad_gate", BM_WGRAD, tk)
    d_wu2 = _tn_grad(du2, xn2, dx2, "ffn2_wgrad_up", BM_WGRAD, tk)
    d_wd2 = _tn_grad(hh2, do2, dx2, "ffn2_wgrad_down", BM_WGRAD, tk)
    handles2, token2 = _exchange_start([blocks(d_wg2), blocks(d_wu2), blocks(d_wd2)], True,
                                       "grads_start_ffn2", dx2)

    dmixed, d_wout = _mixout_bwd(dx2, attn_n, conv_n, wout, token2, "mixout_bwd", tm_mix)
    dq, dk, dv, dbias, dsink, d_attn_norm = _attn_bwd(
        dmixed, attn_raw, q, k, v, bias_t, sinks, attn_out_norm, "attn_bwd", nblk)
    du, dgb, dgc, d_cw, d_conv_norm = _conv_bwd(dmixed, u, gb, gc, cw, conv_out_norm, "conv_bwd", tm_mix)
    d_table = _bias_grad(dbias, bucket, "bias_grad")
    dz = [dq, dk, dv, du, dgb, dgc]
    dx1, d_mix_norm = _mixin_bwd(dx2, x1, mix_norm, dz, win_t, "mixin_bwd", tm_mix)
    d_win_t = _win_grad(dz, hn, "win_grad", min(TM_MIX, T))
    handles_mix, token_mix = _exchange_start([blocks(d_win_t), blocks(d_wout)], True, "grads_start_mix", d_table)

    dx0, d_ffn1_norm, dg1, du1, hh1, do1 = _ffn_bwd(
        dx1, x0, ffn1_norm, gu1, wgu1, wd1, token_mix, "ffn1_bwd", tm_bwd, tf)
    def pack(ffn1, mixn, ffn2, fin, attn_n_, conv_n_, sink_, extra, convw, table):
        rows = [_pad_row(ffn1), _pad_row(mixn), _pad_row(ffn2), _pad_row(fin),
                _pad_row(jnp.concatenate([attn_n_.reshape(-1), conv_n_.reshape(-1)])),
                _pad_row(sink_), _pad_row(extra),
                jnp.zeros((1, PACK_COLS), F32),
                jnp.pad(convw, ((0, 0), (0, PACK_COLS - convw.shape[1]))),
                _pad_row(table),
                jnp.zeros((PACK_ROWS - 12, PACK_COLS), F32)]
        return jnp.concatenate(rows, axis=0)

    def own_channels(a):
        full = jnp.zeros((a.shape[1], CONV_DIM), F32)
        return lax.dynamic_update_slice(full, a[0], (0, me * a.shape[2]))

    g_pack = pack(d_ffn1_norm, d_mix_norm, d_ffn2_norm, d_final, d_attn_norm, d_conv_norm,
                  dsink[:, 0], loss_part[0, :1], d_cw[:3], d_table[:, :N_Q_HEADS])
    zero1 = jnp.zeros((1,), F32)
    w_pack = pack(ffn1_norm, mix_norm, ffn2_norm, final_norm, attn_out_norm, conv_out_norm,
                  attn_sinks, zero1, own_channels(conv_w), rel_bias_table)
    m_pack = pack(m_ffn1_norm, m_mix_norm, m_ffn2_norm, m_final_norm, m_attn_out_norm, m_conv_out_norm,
                  m_attn_sinks, zero1, own_channels(m_conv_w), m_rel_bias_table)
    v_pack = pack(v_ffn1_norm, v_mix_norm, v_ffn2_norm, v_final_norm, v_attn_out_norm, v_conv_out_norm,
                  v_attn_sinks, zero1, own_channels(v_conv_w), v_rel_bias_table)
    (g_all,) = _exchange([g_pack], False, "gather_small", dx0)
    packs = _adamw(g_all, w_pack, m_pack, v_pack, "adamw_small", PACK_ROWS)

    d_wd1 = _tn_grad(hh1, do1, packs[0], "ffn1_wgrad_down", BM_WGRAD, tk)
    handles1_d, token1 = _exchange_start([blocks(d_wd1)], True, "grads_start_ffn1_down", dx0)
    d_wg1 = _tn_grad(dg1, xn1, token1, "ffn1_wgrad_gate", BM_WGRAD, tk)
    handles1_g, token1 = _exchange_start([blocks(d_wg1)], True, "grads_start_ffn1_gate", token1)
    d_wu1 = _tn_grad(du1, xn1, token1, "ffn1_wgrad_up", BM_WGRAD, tk)
    handles1_u, token1 = _exchange_start([blocks(d_wu1)], True, "grads_start_ffn1_up", token1)

    res = {}

    def update(names, parts):
        last = None
        for name, p in zip(names, parts):
            w, m_, v_, transposed = big[name]
            if transposed:
                w, m_, v_ = w.T, m_.T, v_.T
            new = _adamw(p, w, m_, v_, "adamw_" + name, _row_tile(w.shape[0], ADAM_ROWS))
            res[name] = tuple((a.T if transposed else a)[None] for a in new)
            last = new[0]
        return last

    parts2 = _exchange_wait(handles2, True, "grads_wait_ffn2", token1)
    done2 = update(["ffn2_w_gate", "ffn2_w_up", "ffn2_w_down"], parts2)
    parts_mix = _exchange_wait(handles_mix, True, "grads_wait_mix", done2)
    done_mix = update(["w_in", "w_out"], parts_mix)

    parts1_d = _exchange_wait(handles1_d, True, "grads_wait_ffn1_down", done_mix)
    done1 = update(["ffn1_w_down"], parts1_d)
    parts1_g = _exchange_wait(handles1_g, True, "grads_wait_ffn1_gate", done1)
    done1 = update(["ffn1_w_gate"], parts1_g)
    parts1_u = _exchange_wait(handles1_u, True, "grads_wait_ffn1_up", done1)
    update(["ffn1_w_up"], parts1_u)

    def unpack(pk):
        cwb = lax.dynamic_slice(pk[8:11, :CONV_DIM], (0, me * conv_w.shape[2]), (3, conv_w.shape[2]))
        return {
            "ffn1_norm": pk[0:1, :D], "mix_norm": pk[1:2, :D], "ffn2_norm": pk[2:3, :D],
            "final_norm": pk[3, :D],
            "attn_out_norm": pk[4:5, :ATTN_WIDTH], "conv_out_norm": pk[4:5, ATTN_WIDTH:ATTN_WIDTH + CONV_DIM],
            "attn_sinks": pk[5:6, :N_Q_HEADS],
            "conv_w": cwb[None],
            "rel_bias_table": pk[11, :NUM_BUCKETS * N_Q_HEADS].reshape(NUM_BUCKETS, N_Q_HEADS),
        }

    small = [unpack(pk) for pk in packs]
    loss = packs[0][6, 0]

    order = ["rel_bias_table", "ffn1_norm", "ffn1_w_gate", "ffn1_w_up", "ffn1_w_down", "mix_norm", "w_in",
             "conv_w", "attn_sinks", "attn_out_norm", "conv_out_norm", "w_out", "ffn2_norm",
             "ffn2_w_gate", "ffn2_w_up", "ffn2_w_down", "final_norm"]
    outs = [loss, dx0[None]]
    for kind in range(4):
        for name in order:
            outs.append(res[name][kind] if name in res else small[kind][name])
    return tuple(outs)
```

```python
import math

import numpy as np
import jax
import jax.numpy as jnp
from jax import lax
from jax.experimental import pallas as pl
from jax.experimental.pallas import tpu as pltpu

F32 = jnp.float32
BF16 = jnp.bfloat16

N_DEV = 8
EPS = 1e-6
HEAD_DIM = 64
N_Q_HEADS = 8
N_KV_HEADS = 2
GQA_GROUP = 4
ATTN_WIDTH = 512
KV_WIDTH = 128
CONV_DIM = 512
BLOCK = 128
WINDOW = 128
NUM_BUCKETS = 32
MAX_DISTANCE = 128
SCALE = HEAD_DIM ** -0.5
MASKED = -1e30
GROUP_ROWS = GQA_GROUP * BLOCK

ADAM_LR = 0.001
ADAM_B1 = 0.9
ADAM_B2 = 0.999
ADAM_EPS = 1e-08
ADAM_WD = 0.01
ADAM_STEP = 10

VMEM_LIMIT_BYTES = 40 * 1024 * 1024
VMEM_LIMIT_FFN_BYTES = 50 * 1024 * 1024
SUBLANES = 8
PACK_ROWS = 16
PACK_COLS = 1024

TM_FFN = 1024
TM_FFN_BWD = 1024
TM_MIX = 512
TK_WGRAD = 1024
TF_FFN = 256
BM_WGRAD = 1408
ROW_GROUPS = 4
ATTN_BLOCKS = 4
ADAM_ROWS = 256


def _row_tile(rows, limit):
    best = rows
    for t in range(16, min(rows, limit) + 1, 16):
        if rows % t == 0:
            best = t
    return best


def _params(*sem):
    return pltpu.CompilerParams(dimension_semantics=sem, vmem_limit_bytes=VMEM_LIMIT_BYTES)


def _hbm(*arrays):
    return [pltpu.with_memory_space_constraint(a, pltpu.HBM) for a in arrays]


def _dot(a, b):
    return jnp.dot(a, b, preferred_element_type=F32)


def _dot_nt(a, b):
    return lax.dot_general(a, b, (((1,), (1,)), ((), ())), preferred_element_type=F32)


def _dot_tn(a, b):
    return lax.dot_general(a, b, (((0,), (0,)), ((), ())), preferred_element_type=F32)


def _sigmoid(g):
    return 0.5 * jnp.tanh(0.5 * g) + 0.5


def _rms_stats(x):
    inv = lax.rsqrt(jnp.mean(x * x, axis=-1, keepdims=True) + EPS)
    return inv, x * inv


def _rms_bwd(dy, x, gain):
    inv, xhat = _rms_stats(x)
    dgain = jnp.sum(dy * xhat, axis=0, keepdims=True)
    dxh = dy * gain
    dx = inv * (dxh - xhat * jnp.mean(dxh * xhat, axis=-1, keepdims=True))
    return dx, dgain


def _peer_list():
    x, y, c = lax.axis_index("x"), lax.axis_index("y"), lax.axis_index("c")
    peers = []
    for k in range(1, N_DEV):
        px = 1 - x if (k >> 2) & 1 else x
        py = 1 - y if (k >> 1) & 1 else y
        pc = 1 - c if k & 1 else c
        peers.append((px, py, pc))
    return 4 * x + 2 * y + c, peers


def _exchange(arrs, scatter, name, after):
    n = len(arrs)
    out_shape = []
    for a in arrs:
        shp = a.shape if scatter else (N_DEV,) + a.shape
        out_shape.append(jax.ShapeDtypeStruct(shp, a.dtype))

    def body(*refs):
        ins, outs = refs[:n], refs[n + 1:2 * n + 1]
        send_sems, recv_sems, local_sems = refs[2 * n + 1:]
        me, peers = _peer_list()
        started = []
        for a in range(n):
            own = ins[a].at[me] if scatter else ins[a]
            loc = pltpu.make_async_copy(own, outs[a].at[me], local_sems.at[a])
            loc.start()
            started.append(loc)
        sends = []
        for a in range(n):
            for k, (px, py, pc) in enumerate(peers):
                src = ins[a].at[4 * px + 2 * py + pc] if scatter else ins[a]
                cp = pltpu.make_async_remote_copy(
                    src_ref=src, dst_ref=outs[a].at[me],
                    send_sem=send_sems.at[a, k], recv_sem=recv_sems.at[a, k],
                    device_id=(px, py, pc), device_id_type=pl.DeviceIdType.MESH)
                cp.start()
                sends.append(cp)
        for a in range(n):
            for k, (px, py, pc) in enumerate(peers):
                landed = outs[a].at[4 * px + 2 * py + pc]
                pltpu.make_async_remote_copy(
                    src_ref=landed, dst_ref=landed,
                    send_sem=send_sems.at[a, k], recv_sem=recv_sems.at[a, k],
                    device_id=(px, py, pc), device_id_type=pl.DeviceIdType.MESH).wait_recv()
        for cp in sends:
            cp.wait_send()
        for loc in started:
            loc.wait()

    hbm = pl.BlockSpec(memory_space=pl.ANY)
    return pl.pallas_call(
        body, name=name, out_shape=tuple(out_shape),
        in_specs=[hbm] * (n + 1), out_specs=tuple([hbm] * n),
        scratch_shapes=[pltpu.SemaphoreType.DMA((n, N_DEV - 1)),
                        pltpu.SemaphoreType.DMA((n, N_DEV - 1)),
                        pltpu.SemaphoreType.DMA((n,))],
    )(*arrs, after)


def _gather_two_level(arrs, name, after):
    n = len(arrs)
    out_shape = tuple(jax.ShapeDtypeStruct((N_DEV,) + a.shape, a.dtype) for a in arrs)

    def body(*refs):
        ins, outs = refs[:n], refs[n + 1:2 * n + 1]
        send_sems, recv_sems, local_sems = refs[2 * n + 1:]
        x, y, c = lax.axis_index("x"), lax.axis_index("y"), lax.axis_index("c")
        me, sibling = (x, y, c), (x, y, 1 - c)
        chips = [(1 - x, y), (x, 1 - y), (1 - x, 1 - y)]

        def copy(a, k, block, to, src=None):
            slot = outs[a].at[4 * block[0] + 2 * block[1] + block[2]]
            return pltpu.make_async_remote_copy(
                src_ref=slot if src is None else src, dst_ref=slot,
                send_sem=send_sems.at[a, k], recv_sem=recv_sems.at[a, k],
                device_id=to, device_id_type=pl.DeviceIdType.MESH)

        started = []
        for a in range(n):
            loc = pltpu.make_async_copy(ins[a], outs[a].at[4 * x + 2 * y + c], local_sems.at[a])
            loc.start()
            started.append(loc)
        sends = []
        for a in range(n):
            sends.append(copy(a, 0, me, sibling, src=ins[a]))
            sends += [copy(a, 1 + j, me, (*chip, c), src=ins[a]) for j, chip in enumerate(chips)]
        for cp in sends:
            cp.start()
        for j, chip in enumerate(chips):
            for a in range(n):
                copy(a, 1 + j, (*chip, c), me).wait_recv()
                fwd = copy(a, 4 + j, (*chip, c), sibling)
                fwd.start()
                sends.append(fwd)
        for a in range(n):
            copy(a, 0, sibling, me).wait_recv()
            for j, chip in enumerate(chips):
                copy(a, 4 + j, (*chip, 1 - c), me).wait_recv()
        for cp in sends:
            cp.wait_send()
        for loc in started:
            loc.wait()

    hbm = pl.BlockSpec(memory_space=pl.ANY)
    return pl.pallas_call(
        body, name=name, out_shape=out_shape,
        in_specs=[hbm] * (n + 1), out_specs=tuple([hbm] * n),
        scratch_shapes=[pltpu.SemaphoreType.DMA((n, N_DEV - 1)),
                        pltpu.SemaphoreType.DMA((n, N_DEV - 1)),
                        pltpu.SemaphoreType.DMA((n,))],
    )(*arrs, after)


_HBM = pl.BlockSpec(memory_space=pltpu.HBM)
_SEM = pl.BlockSpec(memory_space=pltpu.SEMAPHORE)
_EFFECT = pltpu.SideEffectType.DATAFLOW_SIDE_EFFECTING


def _split_copies(srcs, lands, send_sems, recv_sems, scatter):
    me, peers = _peer_list()
    copies = []
    for a in range(len(srcs)):
        for k, (px, py, pc) in enumerate(peers):
            src = srcs[a].at[4 * px + 2 * py + pc] if scatter else srcs[a]
            copies.append(pltpu.make_async_remote_copy(
                src_ref=src, dst_ref=lands[a].at[me],
                send_sem=send_sems[a].at[k], recv_sem=recv_sems[a].at[k],
                device_id=(px, py, pc), device_id_type=pl.DeviceIdType.MESH))
    return copies


def _exchange_start(arrs, scatter, name, after):
    n = len(arrs)
    me = 4 * lax.axis_index("x") + 2 * lax.axis_index("y") + lax.axis_index("c")
    lands = []
    for a in arrs:
        own = lax.dynamic_index_in_dim(a, me, 0, keepdims=True) if scatter else a[None]
        shp = a.shape if scatter else (N_DEV,) + a.shape
        lands.append(lax.dynamic_update_slice(lax.empty(shp, a.dtype), own, (me,) + (0,) * (len(shp) - 1)))

    def body(*refs):
        srcs, lnds = refs[:n], refs[n:2 * n]
        outs = refs[2 * n + 1:]
        send_sems, recv_sems = outs[:n], outs[n:2 * n]
        token = outs[4 * n]
        for cp in _split_copies(srcs, lnds, send_sems, recv_sems, scatter):
            cp.start()
        token[...] = jnp.zeros_like(token)

    sem = pltpu.SemaphoreType.DMA((N_DEV - 1,))
    out_shape = ([sem] * (2 * n) + [pltpu.HBM(a.shape, a.dtype) for a in arrs]
                 + [pltpu.HBM(l.shape, l.dtype) for l in lands] + [jax.ShapeDtypeStruct((SUBLANES, 128), F32)])
    res = pl.pallas_call(
        body, name=name, out_shape=tuple(out_shape),
        in_specs=[_HBM] * (2 * n) + [pl.BlockSpec(memory_space=pl.ANY)],
        out_specs=tuple([_SEM] * (2 * n) + [_HBM] * (2 * n) + [pl.BlockSpec(memory_space=pltpu.VMEM)]),
        input_output_aliases={i: 2 * n + i for i in range(2 * n)},
        compiler_params=pltpu.CompilerParams(has_side_effects=_EFFECT),
    )(*[pltpu.with_memory_space_constraint(a, pltpu.HBM) for a in arrs],
      *[pltpu.with_memory_space_constraint(l, pltpu.HBM) for l in lands], after)
    handles = [(res[2 * n + a], res[3 * n + a], res[a], res[n + a]) for a in range(n)]
    return handles, res[4 * n]


def _exchange_wait(handles, scatter, name, after):
    n = len(handles)

    def body(*refs):
        srcs, lnds = refs[:n], refs[n:2 * n]
        send_sems, recv_sems = refs[2 * n:3 * n], refs[3 * n:4 * n]
        for cp in _split_copies(srcs, lnds, send_sems, recv_sems, scatter):
            cp.wait_send()
            cp.wait_recv()

    srcs = [h[0] for h in handles]
    lands = [h[1] for h in handles]
    res = pl.pallas_call(
        body, name=name,
        out_shape=tuple([pltpu.HBM(a.shape, a.dtype) for a in srcs] + [pltpu.HBM(l.shape, l.dtype) for l in lands]),
        in_specs=[_HBM] * (2 * n) + [_SEM] * (2 * n) + [pl.BlockSpec(memory_space=pl.ANY)],
        out_specs=tuple([_HBM] * (2 * n)),
        input_output_aliases={i: i for i in range(2 * n)},
        compiler_params=pltpu.CompilerParams(has_side_effects=_EFFECT),
    )(*srcs, *lands, *[h[2] for h in handles], *[h[3] for h in handles], after)
    return list(res[n:])


def _row_groups(tm):
    return [slice(r * (tm // ROW_GROUPS), (r + 1) * (tm // ROW_GROUPS)) for r in range(ROW_GROUPS)]


def _stack_gate_up(wgt, wut, tf):
    F, D = wgt.shape
    return jnp.stack([wgt.reshape(F // tf, tf, D), wut.reshape(F // tf, tf, D)], axis=1).reshape(2 * F, D)


def _ffn_fwd(x, gain, wgu, wd, after, name, tm, tf, head=None):
    T, D = x.shape
    F = wd.shape[0]
    nj = F // tf
    n_in = 7 if head else 5

    def body(*refs):
        x_ref, g_ref, wgu_ref, wd_ref, after_ref = refs[:5]
        outs = refs[n_in:]
        xo_ref, xn_ref, gu_ref = outs[:3]
        xn_sc, acc_sc = outs[-2:]
        i, j = pl.program_id(0), pl.program_id(1)

        @pl.when(j == 0)
        def _():
            _, xhat = _rms_stats(x_ref[...])
            xn = (xhat * g_ref[...]).astype(BF16)
            xn_sc[...] = xn
            xn_ref[...] = xn
            acc_sc[...] = jnp.zeros_like(acc_sc)

        groups = _row_groups(tm)
        gus = [_dot_nt(xn_sc[rows, :], wgu_ref[...]) for rows in groups]
        hs = []
        for rows, gu in zip(groups, gus):
            gu_ref[rows, :] = gu.astype(BF16)
            g, u = gu[:, :tf], gu[:, tf:]
            hs.append((g * _sigmoid(g) * u).astype(BF16))
        for rows, h in zip(groups, hs):
            acc_sc[rows, :] += _dot(h, wd_ref[...])

        if head:
            t_ref, fg_ref = refs[5:7]
            dgain_ref, loss_ref = outs[3:5]

            @pl.when((i == 0) & (j == 0))
            def _():
                dgain_ref[...] = jnp.zeros_like(dgain_ref)
                loss_ref[...] = jnp.zeros_like(loss_ref)

        @pl.when(j == nj - 1)
        def _():
            for rows in groups:
                xo = x_ref[rows, :] + 0.5 * acc_sc[rows, :]
                if head:
                    fg = fg_ref[...]
                    _, xhat = _rms_stats(xo)
                    err = xhat * fg - t_ref[rows, :]
                    loss_ref[...] += 0.5 * jnp.sum(jnp.mean(err * err, axis=-1, keepdims=True),
                                                   axis=0, keepdims=True)
                    dx, dgain = _rms_bwd(err * (1.0 / D), xo, fg)
                    dgain_ref[...] += dgain
                    xo_ref[rows, :] = dx
                else:
                    xo_ref[rows, :] = xo

    tile = pl.BlockSpec((tm, D), lambda i, j: (i, 0))
    const = pl.BlockSpec((1, D), lambda i, j: (0, 0))
    in_specs = [tile, const, pl.BlockSpec((2 * tf, D), lambda i, j: (j, 0)),
                pl.BlockSpec((tf, D), lambda i, j: (j, 0)), pl.BlockSpec(memory_space=pl.ANY)]
    out_specs = [tile, pl.BlockSpec((tm, D), lambda i, j: (i, 0)), pl.BlockSpec((tm, 2 * tf), lambda i, j: (i, j))]
    out_shape = [jax.ShapeDtypeStruct((T, D), F32), jax.ShapeDtypeStruct((T, D), BF16),
                 jax.ShapeDtypeStruct((T, 2 * F), BF16)]
    operands = [x, gain, wgu, wd, after]
    if head:
        in_specs += [pl.BlockSpec((tm, D), lambda i, j: (i, 0), pipeline_mode=pl.Buffered(1)), const]
        out_specs += [const, pl.BlockSpec((SUBLANES, 128), lambda i, j: (0, 0))]
        out_shape += [jax.ShapeDtypeStruct((1, D), F32), jax.ShapeDtypeStruct((SUBLANES, 128), F32)]
        operands += list(head)
    return pl.pallas_call(
        body, name=name, grid=(T // tm, nj),
        in_specs=in_specs, out_specs=tuple(out_specs), out_shape=tuple(out_shape),
        scratch_shapes=[pltpu.VMEM((tm, D), BF16), pltpu.VMEM((tm, D), F32)],
        compiler_params=pltpu.CompilerParams(dimension_semantics=("arbitrary", "arbitrary"),
                                             vmem_limit_bytes=VMEM_LIMIT_FFN_BYTES),
    )(*_hbm(*operands))


def _ffn_bwd(dy, x, gain, gu, wgu, wd, after, name, tm, tf):
    T, D = x.shape
    F = wd.shape[0]
    nj = F // tf

    def body(dy_ref, x_ref, g_ref, gu_ref, wgu_ref, wd_ref, after_ref,
             dx_ref, dgain_ref, dg_ref, du_ref, hh_ref, do_ref, do_sc, acc_sc, dgu_sc):
        i, j = pl.program_id(0), pl.program_id(1)

        @pl.when((i == 0) & (j == 0))
        def _():
            dgain_ref[...] = jnp.zeros_like(dgain_ref)

        @pl.when(j == 0)
        def _():
            do = (0.5 * dy_ref[...]).astype(BF16)
            do_sc[...] = do
            do_ref[...] = do
            acc_sc[...] = jnp.zeros_like(acc_sc)

        groups = _row_groups(tm)
        dhs = [_dot_nt(do_sc[rows, :], wd_ref[...]) for rows in groups]
        for rows, dh in zip(groups, dhs):
            g = gu_ref[rows, :tf].astype(F32)
            u = gu_ref[rows, tf:].astype(F32)
            sig = _sigmoid(g)
            s = g * sig
            dg = (dh * u * (sig + s * (1.0 - sig))).astype(BF16)
            du = (dh * s).astype(BF16)
            dg_ref[rows, :] = dg
            du_ref[rows, :] = du
            dgu_sc[rows, :tf] = dg
            dgu_sc[rows, tf:] = du
            hh_ref[rows, :] = (s * u).astype(BF16)
        for rows in groups:
            acc_sc[rows, :] += _dot(dgu_sc[rows, :], wgu_ref[...])

        @pl.when(j == nj - 1)
        def _():
            for rows in groups:
                dx, dgain = _rms_bwd(acc_sc[rows, :], x_ref[rows, :], g_ref[...])
                dgain_ref[...] += dgain
                dx_ref[rows, :] = dy_ref[rows, :] + dx

    tile_in = pl.BlockSpec((tm, D), lambda i, j: (i, 0))
    tile = pl.BlockSpec((tm, D), lambda i, j: (i, 0), pipeline_mode=pl.Buffered(1))
    return pl.pallas_call(
        body, name=name, grid=(T // tm, nj),
        in_specs=[tile_in, tile_in,
                  pl.BlockSpec((1, D), lambda i, j: (0, 0)),
                  pl.BlockSpec((tm, 2 * tf), lambda i, j: (i, j)),
                  pl.BlockSpec((2 * tf, D), lambda i, j: (j, 0)),
                  pl.BlockSpec((tf, D), lambda i, j: (j, 0)),
                  pl.BlockSpec(memory_space=pl.ANY)],
        out_specs=(tile,
                   pl.BlockSpec((1, D), lambda i, j: (0, 0)),
                   pl.BlockSpec((tm, tf), lambda i, j: (i, j)),
                   pl.BlockSpec((tm, tf), lambda i, j: (i, j)),
                   pl.BlockSpec((tm, tf), lambda i, j: (i, j)),
                   pl.BlockSpec((tm, D), lambda i, j: (i, 0))),
        out_shape=(jax.ShapeDtypeStruct((T, D), F32), jax.ShapeDtypeStruct((1, D), F32),
                   jax.ShapeDtypeStruct((T, F), BF16), jax.ShapeDtypeStruct((T, F), BF16),
                   jax.ShapeDtypeStruct((T, F), BF16), jax.ShapeDtypeStruct((T, D), BF16)),
        scratch_shapes=[pltpu.VMEM((tm, D), BF16), pltpu.VMEM((tm, D), F32), pltpu.VMEM((tm, 2 * tf), BF16)],
        compiler_params=pltpu.CompilerParams(dimension_semantics=("arbitrary", "arbitrary"),
                                             vmem_limit_bytes=VMEM_LIMIT_FFN_BYTES),
    )(*_hbm(dy, x, gain, gu, wgu, wd, after))


def _tn_grad(a, b, after, name, bm, tk):
    T, M = a.shape
    D = b.shape[1]
    nk = T // tk

    def body(a_ref, b_ref, after_ref, o_ref, acc_sc):
        k = pl.program_id(1)

        @pl.when(k == 0)
        def _():
            acc_sc[...] = jnp.zeros_like(acc_sc)

        acc_sc[...] += _dot_tn(a_ref[...], b_ref[...])

        @pl.when(k == nk - 1)
        def _():
            o_ref[...] = acc_sc[...].astype(BF16)

    return pl.pallas_call(
        body, name=name, grid=(M // bm, nk),
        in_specs=[pl.BlockSpec((tk, bm), lambda i, k: (k, i)), pl.BlockSpec((tk, D), lambda i, k: (k, 0)),
                  pl.BlockSpec(memory_space=pl.ANY)],
        out_specs=pl.BlockSpec((bm, D), lambda i, k: (i, 0)),
        out_shape=jax.ShapeDtypeStruct((M, D), BF16),
        scratch_shapes=[pltpu.VMEM((bm, D), F32)],
        compiler_params=_params("arbitrary", "arbitrary"),
    )(*_hbm(a, b, after))


_Z_SPLITS = (0, 512, 640, 768, 1280, 1792, 2304)


def _mixin_fwd(x, gain, w_in_t, name, tm):
    T, D = x.shape
    widths = [b - a for a, b in zip(_Z_SPLITS[:-1], _Z_SPLITS[1:])]

    def body(x_ref, g_ref, w_ref, hn_ref, *outs):
        _, xhat = _rms_stats(x_ref[...])
        hn = (xhat * g_ref[...]).astype(BF16)
        hn_ref[...] = hn
        for o_ref, lo, hi in zip(outs, _Z_SPLITS[:-1], _Z_SPLITS[1:]):
            o_ref[...] = _dot_nt(hn, w_ref[lo:hi, :]).astype(BF16)

    return pl.pallas_call(
        body, name=name, grid=(T // tm,),
        in_specs=[pl.BlockSpec((tm, D), lambda i: (i, 0)),
                  pl.BlockSpec((1, D), lambda i: (0, 0)),
                  pl.BlockSpec(w_in_t.shape, lambda i: (0, 0))],
        out_specs=tuple([pl.BlockSpec((tm, D), lambda i: (i, 0))]
                        + [pl.BlockSpec((tm, w), lambda i: (i, 0)) for w in widths]),
        out_shape=tuple([jax.ShapeDtypeStruct((T, D), BF16)]
                        + [jax.ShapeDtypeStruct((T, w), BF16) for w in widths]),
        compiler_params=_params("arbitrary"),
    )(*_hbm(x, gain, w_in_t))


def _bucket_table():
    qi = np.arange(BLOCK, dtype=np.int32)[:, None]
    kj = np.arange(2 * BLOCK, dtype=np.int32)[None, :]
    dist = qi + BLOCK - kj
    n = np.maximum(dist, 0)
    max_exact = NUM_BUCKETS // 2
    large = max_exact + (np.log(np.maximum(n, 1).astype(np.float32) / max_exact)
                         / math.log(MAX_DISTANCE / max_exact)
                         * (NUM_BUCKETS - max_exact)).astype(np.int32)
    large = np.minimum(large, NUM_BUCKETS - 1)
    bucket = np.where(n < max_exact, n, large).astype(np.int32)
    valid = (dist >= 0) & (dist < WINDOW)
    return np.where(valid, bucket, -1).astype(np.int32)


def _bias_build(table, bucket, name):
    def body(t_ref, b_ref, o_ref):
        bk = b_ref[...]
        for h in range(N_Q_HEADS):
            def step(b, acc):
                return jnp.where(bk == b, t_ref[b, h], acc)
            o_ref[h] = lax.fori_loop(0, NUM_BUCKETS, step, jnp.full(bk.shape, MASKED, F32))

    return pl.pallas_call(
        body, name=name,
        in_specs=[pl.BlockSpec(memory_space=pltpu.SMEM), pl.BlockSpec(memory_space=pltpu.VMEM)],
        out_specs=pl.BlockSpec(memory_space=pltpu.VMEM),
        out_shape=jax.ShapeDtypeStruct((N_Q_HEADS,) + bucket.shape, F32),
    )(table, bucket)


def _bias_grad(dbias, bucket, name):
    def body(d_ref, b_ref, o_ref):
        bk = b_ref[...]
        row = lax.broadcasted_iota(jnp.int32, o_ref.shape, 0)
        lane = lax.broadcasted_iota(jnp.int32, o_ref.shape, 1)
        res = jnp.zeros(o_ref.shape, F32)
        for h in range(N_Q_HEADS):
            d = d_ref[h]

            def step(b, acc):
                part = jnp.sum(jnp.where(bk == b, d, 0.0), axis=0, keepdims=True)
                return jnp.where(row == b, part, acc)
            per_lane = lax.fori_loop(0, NUM_BUCKETS, step, jnp.zeros(o_ref.shape, F32))
            res = jnp.where(lane == h, jnp.sum(per_lane, axis=1, keepdims=True), res)
        o_ref[...] = res

    return pl.pallas_call(
        body, name=name,
        in_specs=[pl.BlockSpec(memory_space=pltpu.VMEM), pl.BlockSpec(memory_space=pltpu.VMEM)],
        out_specs=pl.BlockSpec(memory_space=pltpu.VMEM),
        out_shape=jax.ShapeDtypeStruct((NUM_BUCKETS, 128), F32),
    )(*_hbm(dbias, bucket))


def _head_cols(h):
    return slice(h * HEAD_DIM, (h + 1) * HEAD_DIM)


def _stack_heads(ref, r0, g, dtype):
    return jnp.concatenate(
        [ref[pl.ds(r0, BLOCK), _head_cols(GQA_GROUP * g + j)].astype(dtype) for j in range(GQA_GROUP)], axis=0)


def _unstack_heads(ref, r0, g, val):
    for j in range(GQA_GROUP):
        ref[pl.ds(r0, BLOCK), _head_cols(GQA_GROUP * g + j)] = val[j * BLOCK:(j + 1) * BLOCK, :]


def _head_lanes(h):
    return slice(h * BLOCK, (h + 1) * BLOCK)


def _group_lanes(g):
    return slice(g * GROUP_ROWS, (g + 1) * GROUP_ROWS)


def _head_softmax(st, bias_t, sink, no_prev):
    s = st * SCALE + bias_t
    row = lax.broadcasted_iota(jnp.int32, s.shape, 0)
    s = jnp.where(no_prev & (row < BLOCK), MASKED, s)
    m = jnp.maximum(jnp.max(s, axis=0, keepdims=True), sink)
    p = jnp.exp(s - m)
    ps = jnp.exp(sink - m)
    r = 1.0 / (jnp.sum(p, axis=0, keepdims=True) + ps)
    return p * r, ps * r


def _load_band(kf_sc, vf_sc, kp_ref, kc_ref, vp_ref, vc_ref, tq):
    kf_sc[0:BLOCK, :] = kp_ref[...]
    kf_sc[BLOCK:BLOCK + tq, :] = kc_ref[...]
    vf_sc[0:BLOCK, :] = vp_ref[...]
    vf_sc[BLOCK:BLOCK + tq, :] = vc_ref[...]


def _attn_fwd(q, k, v, bias_t, sinks, gain, name, nblk):
    T = q.shape[0]
    tq = nblk * BLOCK

    def body(sink_ref, q_ref, kc_ref, kp_ref, vc_ref, vp_ref, bias_ref, g_ref, raw_ref, nrm_ref,
             kf_sc, vf_sc, o_sc, st_sc, pt_sc):
        i = pl.program_id(0)
        _load_band(kf_sc, vf_sc, kp_ref, kc_ref, vp_ref, vc_ref, tq)

        def block(b, carry):
            r0 = pl.multiple_of(b * BLOCK, BLOCK)
            no_prev = (i == 0) & (b == 0)
            for g in range(N_KV_HEADS):
                kb = kf_sc[pl.ds(r0, 2 * BLOCK), _head_cols(g)]
                st_sc[:, _group_lanes(g)] = _dot_nt(kb, _stack_heads(q_ref, r0, g, BF16))
            for h in range(N_Q_HEADS):
                p, _ = _head_softmax(st_sc[:, _head_lanes(h)], bias_ref[h], sink_ref[h], no_prev)
                pt_sc[:, _head_lanes(h)] = p.astype(BF16)
            for g in range(N_KV_HEADS):
                vb = vf_sc[pl.ds(r0, 2 * BLOCK), _head_cols(g)]
                _unstack_heads(o_sc, r0, g, _dot_tn(pt_sc[:, _group_lanes(g)], vb))
            return carry

        lax.fori_loop(0, nblk, block, 0)
        o = o_sc[...]
        raw_ref[...] = o.astype(BF16)
        _, ohat = _rms_stats(o)
        nrm_ref[...] = (ohat * g_ref[...]).astype(BF16)

    cur = lambda i: (i, 0)
    prev = lambda i: (jnp.maximum(i * nblk - 1, 0), 0)
    lanes = N_Q_HEADS * BLOCK
    return pl.pallas_call(
        body, name=name, grid=(T // tq,),
        in_specs=[pl.BlockSpec(memory_space=pltpu.SMEM),
                  pl.BlockSpec((tq, ATTN_WIDTH), cur),
                  pl.BlockSpec((tq, KV_WIDTH), cur), pl.BlockSpec((BLOCK, KV_WIDTH), prev),
                  pl.BlockSpec((tq, KV_WIDTH), cur), pl.BlockSpec((BLOCK, KV_WIDTH), prev),
                  pl.BlockSpec(bias_t.shape, lambda i: (0, 0, 0)),
                  pl.BlockSpec((1, ATTN_WIDTH), lambda i: (0, 0))],
        out_specs=(pl.BlockSpec((tq, ATTN_WIDTH), cur), pl.BlockSpec((tq, ATTN_WIDTH), cur)),
        out_shape=(jax.ShapeDtypeStruct((T, ATTN_WIDTH), BF16), jax.ShapeDtypeStruct((T, ATTN_WIDTH), BF16)),
        scratch_shapes=[pltpu.VMEM((tq + BLOCK, KV_WIDTH), BF16), pltpu.VMEM((tq + BLOCK, KV_WIDTH), BF16),
                        pltpu.VMEM((tq, ATTN_WIDTH), F32),
                        pltpu.VMEM((2 * BLOCK, lanes), F32), pltpu.VMEM((2 * BLOCK, lanes), BF16)],
        compiler_params=_params("arbitrary"),
    )(sinks, *_hbm(q, k, k, v, v, bias_t, gain))


def _attn_bwd(dmixed, raw, q, k, v, bias_t, sinks, gain, name, nblk):
    T = q.shape[0]
    tq = nblk * BLOCK
    nt = T // tq
    lanes = N_Q_HEADS * BLOCK

    def body(sink_ref, dm_ref, raw_ref, q_ref, kc_ref, kp_ref, vc_ref, vp_ref, bias_ref, g_ref,
             dq_ref, dk_ref, dv_ref, dbias_ref, dsink_ref, dgain_ref,
             do_sc, dq_sc, kf_sc, vf_sc, dkf_sc, dvf_sc, st_sc, dpt_sc, pt_sc, dst_sc, drow_sc,
             qs_sc, dos_sc, dsink_sc):
        i = pl.program_id(0)
        tile = nt - 1 - i

        @pl.when(i == 0)
        def _():
            dkf_sc[...] = jnp.zeros_like(dkf_sc)
            dvf_sc[...] = jnp.zeros_like(dvf_sc)
            dsink_sc[...] = jnp.zeros_like(dsink_sc)
            dbias_ref[...] = jnp.zeros_like(dbias_ref)
            dgain_ref[...] = jnp.zeros_like(dgain_ref)

        carry_k = dkf_sc[0:BLOCK, :]
        carry_v = dvf_sc[0:BLOCK, :]
        dkf_sc[0:tq, :] = jnp.zeros((tq, KV_WIDTH), F32)
        dvf_sc[0:tq, :] = jnp.zeros((tq, KV_WIDTH), F32)
        dkf_sc[tq:tq + BLOCK, :] = carry_k
        dvf_sc[tq:tq + BLOCK, :] = carry_v
        _load_band(kf_sc, vf_sc, kp_ref, kc_ref, vp_ref, vc_ref, tq)

        do, dgain = _rms_bwd(dm_ref[...].astype(F32), raw_ref[...].astype(F32), g_ref[...])
        dgain_ref[...] += dgain
        do_sc[...] = do
        ones = jnp.ones((SUBLANES, HEAD_DIM), BF16)

        def block(b, carry):
            r0 = pl.multiple_of(b * BLOCK, BLOCK)
            no_prev = (tile == 0) & (b == 0)
            for g in range(N_KV_HEADS):
                kb = kf_sc[pl.ds(r0, 2 * BLOCK), _head_cols(g)]
                vb = vf_sc[pl.ds(r0, 2 * BLOCK), _head_cols(g)]
                qg = _stack_heads(q_ref, r0, g, BF16)
                dog = _stack_heads(do_sc, r0, g, F32)
                prod = dog * _stack_heads(raw_ref, r0, g, F32)
                hi = prod.astype(BF16)
                lo = (prod - hi.astype(F32)).astype(BF16)
                drow_sc[:, _group_lanes(g)] = _dot_nt(ones, hi) + _dot_nt(ones, lo)
                dogb = dog.astype(BF16)
                qs_sc[g] = qg
                dos_sc[g] = dogb
                st_sc[:, _group_lanes(g)] = _dot_nt(kb, qg)
                dpt_sc[:, _group_lanes(g)] = _dot_nt(vb, dogb)
            for h in range(N_Q_HEADS):
                hl = _head_lanes(h)
                p, ps = _head_softmax(st_sc[:, hl], bias_ref[h], sink_ref[h], no_prev)
                rowdot = drow_sc[0:1, hl]
                ds = p * (dpt_sc[:, hl] - rowdot)
                dsink_sc[h:h + 1, :] += -(ps * rowdot)
                dbias_ref[h] += ds
                dst_sc[:, hl] = ds.astype(BF16)
                pt_sc[:, hl] = p.astype(BF16)
            for g in range(N_KV_HEADS):
                kb = kf_sc[pl.ds(r0, 2 * BLOCK), _head_cols(g)]
                dsg = dst_sc[:, _group_lanes(g)]
                _unstack_heads(dq_sc, r0, g, _dot_tn(dsg, kb) * SCALE)
                dkf_sc[pl.ds(r0, 2 * BLOCK), _head_cols(g)] += _dot(dsg, qs_sc[g]) * SCALE
                dvf_sc[pl.ds(r0, 2 * BLOCK), _head_cols(g)] += _dot(pt_sc[:, _group_lanes(g)], dos_sc[g])
            return carry

        lax.fori_loop(0, nblk, block, 0)
        dq_ref[...] = dq_sc[...].astype(BF16)
        dk_ref[...] = dkf_sc[BLOCK:BLOCK + tq, :].astype(BF16)
        dv_ref[...] = dvf_sc[BLOCK:BLOCK + tq, :].astype(BF16)

        @pl.when(i == nt - 1)
        def _():
            tot = jnp.sum(dsink_sc[...], axis=1, keepdims=True)
            dsink_ref[...] = jnp.broadcast_to(tot, dsink_ref.shape)

    cur = lambda i: (nt - 1 - i, 0)
    prev = lambda i: (jnp.maximum((nt - 1 - i) * nblk - 1, 0), 0)
    const2 = lambda i: (0, 0)
    const3 = lambda i: (0, 0, 0)
    return pl.pallas_call(
        body, name=name, grid=(nt,),
        in_specs=[pl.BlockSpec(memory_space=pltpu.SMEM),
                  pl.BlockSpec((tq, ATTN_WIDTH), cur),
                  pl.BlockSpec((tq, ATTN_WIDTH), cur),
                  pl.BlockSpec((tq, ATTN_WIDTH), cur),
                  pl.BlockSpec((tq, KV_WIDTH), cur), pl.BlockSpec((BLOCK, KV_WIDTH), prev),
                  pl.BlockSpec((tq, KV_WIDTH), cur), pl.BlockSpec((BLOCK, KV_WIDTH), prev),
                  pl.BlockSpec(bias_t.shape, const3),
                  pl.BlockSpec((1, ATTN_WIDTH), const2)],
        out_specs=(pl.BlockSpec((tq, ATTN_WIDTH), cur),
                   pl.BlockSpec((tq, KV_WIDTH), cur), pl.BlockSpec((tq, KV_WIDTH), cur),
                   pl.BlockSpec(bias_t.shape, const3),
                   pl.BlockSpec((N_Q_HEADS, 128), const2),
                   pl.BlockSpec((1, ATTN_WIDTH), const2)),
        out_shape=(jax.ShapeDtypeStruct((T, ATTN_WIDTH), BF16),
                   jax.ShapeDtypeStruct((T, KV_WIDTH), BF16), jax.ShapeDtypeStruct((T, KV_WIDTH), BF16),
                   jax.ShapeDtypeStruct(bias_t.shape, F32),
                   jax.ShapeDtypeStruct((N_Q_HEADS, 128), F32),
                   jax.ShapeDtypeStruct((1, ATTN_WIDTH), F32)),
        scratch_shapes=[pltpu.VMEM((tq, ATTN_WIDTH), F32), pltpu.VMEM((tq, ATTN_WIDTH), F32),
                        pltpu.VMEM((tq + BLOCK, KV_WIDTH), BF16), pltpu.VMEM((tq + BLOCK, KV_WIDTH), BF16),
                        pltpu.VMEM((tq + BLOCK, KV_WIDTH), F32), pltpu.VMEM((tq + BLOCK, KV_WIDTH), F32),
                        pltpu.VMEM((2 * BLOCK, lanes), F32), pltpu.VMEM((2 * BLOCK, lanes), F32),
                        pltpu.VMEM((2 * BLOCK, lanes), BF16), pltpu.VMEM((2 * BLOCK, lanes), BF16),
                        pltpu.VMEM((SUBLANES, lanes), F32),
                        pltpu.VMEM((N_KV_HEADS, GROUP_ROWS, HEAD_DIM), BF16),
                        pltpu.VMEM((N_KV_HEADS, GROUP_ROWS, HEAD_DIM), BF16),
                        pltpu.VMEM((N_Q_HEADS, 128), F32)],
        compiler_params=_params("arbitrary"),
    )(sinks, *_hbm(dmixed, raw, q, k, k, v, v, bias_t, gain))


def _shift_down(cu, tail):
    row = lax.broadcasted_iota(jnp.int32, cu.shape, 0)
    t6, t7 = tail[6:7, :], tail[7:8, :]
    s1 = jnp.where(row == 0, t7, pltpu.roll(cu, 1, 0))
    s2 = jnp.where(row == 0, t6, jnp.where(row == 1, t7, pltpu.roll(cu, 2, 0)))
    return s1, s2


def _shift_up(d, head):
    n = d.shape[0]
    row = lax.broadcasted_iota(jnp.int32, d.shape, 0)
    h0, h1 = head[0:1, :], head[1:2, :]
    s1 = jnp.where(row == n - 1, h0, pltpu.roll(d, n - 1, 0))
    s2 = jnp.where(row == n - 1, h1, jnp.where(row == n - 2, h0, pltpu.roll(d, n - 2, 0)))
    return s1, s2


def _mixout_fwd(x, attn_n, u, gb, gc, conv_w, gain, w_out, name, tm):
    T, D = x.shape

    def body(x_ref, an_ref, u_ref, b_ref, c_ref, cw_ref, g_ref, wo_ref, xo_ref, cn_ref, tail_sc):
        @pl.when(pl.program_id(0) == 0)
        def _():
            tail_sc[...] = jnp.zeros_like(tail_sc)

        cu = c_ref[...].astype(F32) * u_ref[...].astype(F32)
        s1, s2 = _shift_down(cu, tail_sc[...])
        tail_sc[...] = cu[tm - SUBLANES:tm, :]
        pre = cw_ref[0:1, :] * s2 + cw_ref[1:2, :] * s1 + cw_ref[2:3, :] * cu
        conv = b_ref[...].astype(F32) * pre
        _, chat = _rms_stats(conv)
        cn = (chat * g_ref[...]).astype(BF16)
        cn_ref[...] = cn
        xo_ref[...] = (x_ref[...] + _dot(an_ref[...], wo_ref[0:ATTN_WIDTH, :])
                       + _dot(cn, wo_ref[ATTN_WIDTH:ATTN_WIDTH + CONV_DIM, :]))

    row = lambda i: (i, 0)
    const = lambda i: (0, 0)
    return pl.pallas_call(
        body, name=name, grid=(T // tm,),
        in_specs=[pl.BlockSpec((tm, D), row), pl.BlockSpec((tm, ATTN_WIDTH), row),
                  pl.BlockSpec((tm, CONV_DIM), row), pl.BlockSpec((tm, CONV_DIM), row),
                  pl.BlockSpec((tm, CONV_DIM), row),
                  pl.BlockSpec(conv_w.shape, const), pl.BlockSpec((1, CONV_DIM), const),
                  pl.BlockSpec(w_out.shape, const)],
        out_specs=(pl.BlockSpec((tm, D), row), pl.BlockSpec((tm, CONV_DIM), row)),
        out_shape=(jax.ShapeDtypeStruct((T, D), F32), jax.ShapeDtypeStruct((T, CONV_DIM), BF16)),
        scratch_shapes=[pltpu.VMEM((SUBLANES, CONV_DIM), F32)],
        compiler_params=_params("arbitrary"),
    )(*_hbm(x, attn_n, u, gb, gc, conv_w, gain, w_out))


def _mixout_bwd(dy, attn_n, conv_n, w_out, after, name, tm):
    T, D = dy.shape
    W = ATTN_WIDTH + CONV_DIM
    nt = T // tm

    def body(dy_ref, an_ref, cn_ref, w_ref, after_ref, dm_ref, dw_ref, dw_sc):
        i = pl.program_id(0)

        @pl.when(i == 0)
        def _():
            dw_sc[...] = jnp.zeros_like(dw_sc)

        dyb = dy_ref[...].astype(BF16)
        dm_ref[...] = _dot_nt(dyb, w_ref[...]).astype(BF16)
        dw_sc[0:ATTN_WIDTH, :] += _dot_tn(an_ref[...], dyb)
        dw_sc[ATTN_WIDTH:W, :] += _dot_tn(cn_ref[...], dyb)

        @pl.when(i == nt - 1)
        def _():
            dw_ref[...] = dw_sc[...].astype(BF16)

    row = lambda i: (i, 0)
    const = lambda i: (0, 0)
    return pl.pallas_call(
        body, name=name, grid=(nt,),
        in_specs=[pl.BlockSpec((tm, D), row), pl.BlockSpec((tm, ATTN_WIDTH), row),
                  pl.BlockSpec((tm, CONV_DIM), row), pl.BlockSpec(w_out.shape, const),
                  pl.BlockSpec(memory_space=pl.ANY)],
        out_specs=(pl.BlockSpec((tm, W), row), pl.BlockSpec((W, D), const)),
        out_shape=(jax.ShapeDtypeStruct((T, W), BF16), jax.ShapeDtypeStruct((W, D), BF16)),
        scratch_shapes=[pltpu.VMEM((W, D), F32)],
        compiler_params=_params("arbitrary"),
    )(*_hbm(dy, attn_n, conv_n, w_out, after))


def _conv_bwd(dmixed, u, gb, gc, conv_w, gain, name, tc):
    T = u.shape[0]
    nt = T // tc
    per8 = tc // SUBLANES

    def body(dm_ref, u_ref, b_ref, c_ref, ut_ref, ct_ref, cw_ref, g_ref,
             du_ref, db_ref, dc_ref, dcw_ref, dgain_ref, head_sc):
        i = pl.program_id(0)

        @pl.when(i == 0)
        def _():
            head_sc[...] = jnp.zeros_like(head_sc)
            dcw_ref[...] = jnp.zeros_like(dcw_ref)
            dgain_ref[...] = jnp.zeros_like(dgain_ref)

        uu = u_ref[...].astype(F32)
        cc = c_ref[...].astype(F32)
        bb = b_ref[...].astype(F32)
        cu = cc * uu
        tail = jnp.where(i == nt - 1, 0.0, ct_ref[...].astype(F32) * ut_ref[...].astype(F32))
        s1, s2 = _shift_down(cu, tail)
        w0, w1, w2 = cw_ref[0:1, :], cw_ref[1:2, :], cw_ref[2:3, :]
        pre = w0 * s2 + w1 * s1 + w2 * cu
        dconv, dgain = _rms_bwd(dm_ref[...].astype(F32), bb * pre, g_ref[...])
        dgain_ref[...] += dgain
        db_ref[...] = (dconv * pre).astype(BF16)
        dpre = dconv * bb
        dcw_ref[0:1, :] += jnp.sum(dpre * s2, axis=0, keepdims=True)
        dcw_ref[1:2, :] += jnp.sum(dpre * s1, axis=0, keepdims=True)
        dcw_ref[2:3, :] += jnp.sum(dpre * cu, axis=0, keepdims=True)
        n1, n2 = _shift_up(dpre, head_sc[...])
        head_sc[...] = dpre[0:SUBLANES, :]
        dcu = w2 * dpre + w1 * n1 + w0 * n2
        du_ref[...] = (dcu * cc).astype(BF16)
        dc_ref[...] = (dcu * uu).astype(BF16)

    rev = lambda i: (nt - 1 - i, 0)
    rev_right = lambda i: (nt - 1 - i, 1)
    tail_map = lambda i: (jnp.maximum((nt - 1 - i) * per8 - 1, 0), 0)
    const = lambda i: (0, 0)
    return pl.pallas_call(
        body, name=name, grid=(nt,),
        in_specs=[pl.BlockSpec((tc, CONV_DIM), rev_right),
                  pl.BlockSpec((tc, CONV_DIM), rev), pl.BlockSpec((tc, CONV_DIM), rev),
                  pl.BlockSpec((tc, CONV_DIM), rev),
                  pl.BlockSpec((SUBLANES, CONV_DIM), tail_map), pl.BlockSpec((SUBLANES, CONV_DIM), tail_map),
                  pl.BlockSpec(conv_w.shape, const), pl.BlockSpec((1, CONV_DIM), const)],
        out_specs=(pl.BlockSpec((tc, CONV_DIM), rev), pl.BlockSpec((tc, CONV_DIM), rev),
                   pl.BlockSpec((tc, CONV_DIM), rev),
                   pl.BlockSpec((SUBLANES, CONV_DIM), const), pl.BlockSpec((1, CONV_DIM), const)),
        out_shape=(jax.ShapeDtypeStruct((T, CONV_DIM), BF16), jax.ShapeDtypeStruct((T, CONV_DIM), BF16),
                   jax.ShapeDtypeStruct((T, CONV_DIM), BF16),
                   jax.ShapeDtypeStruct((SUBLANES, CONV_DIM), F32), jax.ShapeDtypeStruct((1, CONV_DIM), F32)),
        scratch_shapes=[pltpu.VMEM((SUBLANES, CONV_DIM), F32)],
        compiler_params=_params("arbitrary"),
    )(*_hbm(dmixed, u, gb, gc, u, gc, conv_w, gain))


def _mixin_bwd(dy, x, gain, dz, w_in_t, name, tm):
    T, D = x.shape
    nz = len(dz)

    def body(dy_ref, x_ref, g_ref, *rest):
        dz_refs, wt_ref, dx_ref, dgain_ref = rest[:nz], rest[nz], rest[nz + 1], rest[nz + 2]

        @pl.when(pl.program_id(0) == 0)
        def _():
            dgain_ref[...] = jnp.zeros_like(dgain_ref)

        dh = jnp.zeros((tm, D), F32)
        for r, lo, hi in zip(dz_refs, _Z_SPLITS[:-1], _Z_SPLITS[1:]):
            dh += _dot(r[...], wt_ref[lo:hi, :])
        dx, dgain = _rms_bwd(dh, x_ref[...], g_ref[...])
        dgain_ref[...] += dgain
        dx_ref[...] = dy_ref[...] + dx

    row = lambda i: (i, 0)
    const = lambda i: (0, 0)
    return pl.pallas_call(
        body, name=name, grid=(T // tm,),
        in_specs=[pl.BlockSpec((tm, D), row), pl.BlockSpec((tm, D), row), pl.BlockSpec((1, D), const)]
                 + [pl.BlockSpec((tm, a.shape[1]), row) for a in dz]
                 + [pl.BlockSpec(w_in_t.shape, const)],
        out_specs=(pl.BlockSpec((tm, D), row), pl.BlockSpec((1, D), const)),
        out_shape=(jax.ShapeDtypeStruct((T, D), F32), jax.ShapeDtypeStruct((1, D), F32)),
        compiler_params=_params("arbitrary"),
    )(*_hbm(dy, x, gain, *dz, w_in_t))


def _win_grad(dz, hn, name, tk):
    T, D = hn.shape
    nz = len(dz)
    nt = T // tk
    W = _Z_SPLITS[-1]

    def body(hn_ref, *rest):
        dz_refs, dw_ref, dw_sc = rest[:nz], rest[nz], rest[nz + 1]
        i = pl.program_id(0)

        @pl.when(i == 0)
        def _():
            dw_sc[...] = jnp.zeros_like(dw_sc)

        hn = hn_ref[...]
        for r, lo, hi in zip(dz_refs, _Z_SPLITS[:-1], _Z_SPLITS[1:]):
            dw_sc[lo:hi, :] += _dot_tn(r[...], hn)

        @pl.when(i == nt - 1)
        def _():
            dw_ref[...] = dw_sc[...].astype(BF16)

    row = lambda i: (i, 0)
    return pl.pallas_call(
        body, name=name, grid=(nt,),
        in_specs=[pl.BlockSpec((tk, D), row)] + [pl.BlockSpec((tk, a.shape[1]), row) for a in dz],
        out_specs=pl.BlockSpec((W, D), lambda i: (0, 0)),
        out_shape=jax.ShapeDtypeStruct((W, D), BF16),
        scratch_shapes=[pltpu.VMEM((W, D), F32)],
        compiler_params=_params("arbitrary"),
    )(*_hbm(hn, *dz))


def _adamw(parts, w, m, v, name, tr):
    P = parts.shape[0]
    R, C = w.shape

    def body(p_ref, w_ref, m_ref, v_ref, g_ref, d_ref, nm_ref, nv_ref):
        g = p_ref[0].astype(F32)
        for d in range(1, P):
            g = g + p_ref[d].astype(F32)
        nm = ADAM_B1 * m_ref[...] + (1.0 - ADAM_B1) * g
        nv = ADAM_B2 * v_ref[...] + (1.0 - ADAM_B2) * (g * g)
        m_hat = nm / (1.0 - ADAM_B1 ** ADAM_STEP)
        v_hat = nv / (1.0 - ADAM_B2 ** ADAM_STEP)
        g_ref[...] = g
        nm_ref[...] = nm
        nv_ref[...] = nv
        d_ref[...] = -ADAM_LR * (m_hat / (jnp.sqrt(v_hat) + ADAM_EPS) + ADAM_WD * w_ref[...])

    row = lambda i: (i, 0)
    spec = pl.BlockSpec((tr, C), row)
    shp = jax.ShapeDtypeStruct((R, C), F32)
    return pl.pallas_call(
        body, name=name, grid=(R // tr,),
        in_specs=[pl.BlockSpec((P, tr, C), lambda i: (0, i, 0)), spec, spec, spec],
        out_specs=(spec, spec, spec, spec),
        out_shape=(shp, shp, shp, shp),
        compiler_params=_params("arbitrary"),
    )(*_hbm(parts, w, m, v))


def _columns_of_blocks(g):
    n, R, w = g.shape
    return g.transpose(1, 0, 2).reshape(R, n * w)


def _pad_row(vec):
    vec = vec.reshape(1, -1)
    return jnp.pad(vec, ((0, 0), (0, PACK_COLS - vec.shape[1])))


def kernel(x, rel_bias_table, ffn1_norm, ffn1_w_gate, ffn1_w_up, ffn1_w_down, mix_norm, w_in, conv_w, attn_sinks, attn_out_norm, conv_out_norm, w_out, ffn2_norm, ffn2_w_gate, ffn2_w_up, ffn2_w_down, final_norm, loss_target, m_rel_bias_table, m_ffn1_norm, m_ffn1_w_gate, m_ffn1_w_up, m_ffn1_w_down, m_mix_norm, m_w_in, m_conv_w, m_attn_sinks, m_attn_out_norm, m_conv_out_norm, m_w_out, m_ffn2_norm, m_ffn2_w_gate, m_ffn2_w_up, m_ffn2_w_down, m_final_norm, v_rel_bias_table, v_ffn1_norm, v_ffn1_w_gate, v_ffn1_w_up, v_ffn1_w_down, v_mix_norm, v_w_in, v_conv_w, v_attn_sinks, v_attn_out_norm, v_conv_out_norm, v_w_out, v_ffn2_norm, v_ffn2_w_gate, v_ffn2_w_up, v_ffn2_w_down, v_final_norm):
    T, D = x.shape[1], x.shape[2]
    x0 = x[0]
    target = loss_target[0]
    tm = min(TM_FFN, T)
    tm_bwd = min(TM_FFN_BWD, T)
    tm_mix = min(TM_MIX, T)
    tk = min(TK_WGRAD, T)
    tf = TF_FFN
    nblk = min(ATTN_BLOCKS, T // BLOCK)
    me = 4 * lax.axis_index("x") + 2 * lax.axis_index("y") + lax.axis_index("c")

    big = {
        "ffn1_w_gate": (ffn1_w_gate[0], m_ffn1_w_gate[0], v_ffn1_w_gate[0], True),
        "ffn1_w_up": (ffn1_w_up[0], m_ffn1_w_up[0], v_ffn1_w_up[0], True),
        "ffn1_w_down": (ffn1_w_down[0], m_ffn1_w_down[0], v_ffn1_w_down[0], False),
        "w_in": (w_in[0], m_w_in[0], v_w_in[0], True),
        "w_out": (w_out[0], m_w_out[0], v_w_out[0], False),
        "ffn2_w_gate": (ffn2_w_gate[0], m_ffn2_w_gate[0], v_ffn2_w_gate[0], True),
        "ffn2_w_up": (ffn2_w_up[0], m_ffn2_w_up[0], v_ffn2_w_up[0], True),
        "ffn2_w_down": (ffn2_w_down[0], m_ffn2_w_down[0], v_ffn2_w_down[0], False),
    }

    def block_to_send(name):
        w, _, _, transposed = big[name]
        return (w.T if transposed else w).astype(BF16)

    names1 = ["ffn1_w_gate", "ffn1_w_up", "ffn1_w_down"]
    names_rest = ["w_in", "w_out", "ffn2_w_gate", "ffn2_w_up", "ffn2_w_down"]
    first = _gather_two_level([block_to_send(n) for n in names1], "gather_ffn1", ffn1_norm)
    wgt1, wut1, wd1 = [g.reshape(-1, D) for g in first]
    h_rest, token = _exchange_start([block_to_send(n) for n in names_rest[:2]] + [conv_w[0]]
                                    + [block_to_send(n) for n in names_rest[2:]], False,
                                    "gather_start_rest", wd1)

    wgu1 = _stack_gate_up(wgt1, wut1, tf)
    x1, xn1, gu1 = _ffn_fwd(x0, ffn1_norm, wgu1, wd1, token, "ffn1_fwd", tm, tf)
    mixw = _exchange_wait(h_rest[:3], False, "gather_wait_mix", x1)
    win_t = mixw[0].reshape(-1, D)
    wout = mixw[1].reshape(-1, D)
    cw = _columns_of_blocks(mixw[2])
    hn, q, k, v, u, gb, gc = _mixin_fwd(x1, mix_norm, win_t, "mixin_fwd", tm_mix)
    bucket = jnp.asarray(_bucket_table().T.copy())
    sinks = attn_sinks.reshape(-1)
    bias_t = _bias_build(rel_bias_table, bucket, "bias_build")
    attn_raw, attn_n = _attn_fwd(q, k, v, bias_t, sinks, attn_out_norm, "attn_fwd", nblk)
    x2, conv_n = _mixout_fwd(x1, attn_n, u, gb, gc, cw, conv_out_norm, wout, "mixout_fwd", tm_mix)
    wgt2, wut2, wd2 = [g.reshape(-1, D) for g in _exchange_wait(h_rest[3:], False, "gather_wait_ffn2", x2)]
    wgu2 = _stack_gate_up(wgt2, wut2, tf)
    dx3, xn2, gu2, d_final, loss_part = _ffn_fwd(x2, ffn2_norm, wgu2, wd2, x2, "ffn2_fwd", tm, tf,
                                                 head=(target, final_norm.reshape(1, D)))

    def blocks(g):
        return g.reshape(N_DEV, -1, D)

    dx2, d_ffn2_norm, dg2, du2, hh2, do2 = _ffn_bwd(
        dx3, x2, ffn2_norm, gu2, wgu2, wd2, dx3, "ffn2_bwd", tm_bwd, tf)
    d_wg2 = _tn_grad(dg2, xn2, dx2, "ffn2_wgrad_gate", BM_WGRAD, tk)
    d_wu2 = _tn_grad(du2, xn2, dx2, "ffn2_wgrad_up", BM_WGRAD, tk)
    d_wd2 = _tn_grad(hh2, do2, dx2, "ffn2_wgrad_down", BM_WGRAD, tk)
    handles2, token2 = _exchange_start([blocks(d_wg2), blocks(d_wu2), blocks(d_wd2)], True,
                                       "grads_start_ffn2", dx2)

    dmixed, d_wout = _mixout_bwd(dx2, attn_n, conv_n, wout, token2, "mixout_bwd", tm_mix)
    dq, dk, dv, dbias, dsink, d_attn_norm = _attn_bwd(
        dmixed, attn_raw, q, k, v, bias_t, sinks, attn_out_norm, "attn_bwd", nblk)
    du, dgb, dgc, d_cw, d_conv_norm = _conv_bwd(dmixed, u, gb, gc, cw, conv_out_norm, "conv_bwd", tm_mix)
    d_table = _bias_grad(dbias, bucket, "bias_grad")
    dz = [dq, dk, dv, du, dgb, dgc]
    dx1, d_mix_norm = _mixin_bwd(dx2, x1, mix_norm, dz, win_t, "mixin_bwd", tm_mix)
    d_win_t = _win_grad(dz, hn, "win_grad", min(TM_MIX, T))
    handles_mix, token_mix = _exchange_start([blocks(d_win_t), blocks(d_wout)], True, "grads_start_mix", d_table)

    dx0, d_ffn1_norm, dg1, du1, hh1, do1 = _ffn_bwd(
        dx1, x0, ffn1_norm, gu1, wgu1, wd1, token_mix, "ffn1_bwd", tm_bwd, tf)
    def pack(ffn1, mixn, ffn2, fin, attn_n_, conv_n_, sink_, extra, convw, table):
        rows = [_pad_row(ffn1), _pad_row(mixn), _pad_row(ffn2), _pad_row(fin),
                _pad_row(jnp.concatenate([attn_n_.reshape(-1), conv_n_.reshape(-1)])),
                _pad_row(sink_), _pad_row(extra),
                jnp.zeros((1, PACK_COLS), F32),
                jnp.pad(convw, ((0, 0), (0, PACK_COLS - convw.shape[1]))),
                _pad_row(table),
                jnp.zeros((PACK_ROWS - 12, PACK_COLS), F32)]
        return jnp.concatenate(rows, axis=0)

    def own_channels(a):
        full = jnp.zeros((a.shape[1], CONV_DIM), F32)
        return lax.dynamic_update_slice(full, a[0], (0, me * a.shape[2]))

    g_pack = pack(d_ffn1_norm, d_mix_norm, d_ffn2_norm, d_final, d_attn_norm, d_conv_norm,
                  dsink[:, 0], loss_part[0, :1], d_cw[:3], d_table[:, :N_Q_HEADS])
    zero1 = jnp.zeros((1,), F32)
    w_pack = pack(ffn1_norm, mix_norm, ffn2_norm, final_norm, attn_out_norm, conv_out_norm,
                  attn_sinks, zero1, own_channels(conv_w), rel_bias_table)
    m_pack = pack(m_ffn1_norm, m_mix_norm, m_ffn2_norm, m_final_norm, m_attn_out_norm, m_conv_out_norm,
                  m_attn_sinks, zero1, own_channels(m_conv_w), m_rel_bias_table)
    v_pack = pack(v_ffn1_norm, v_mix_norm, v_ffn2_norm, v_final_norm, v_attn_out_norm, v_conv_out_norm,
                  v_attn_sinks, zero1, own_channels(v_conv_w), v_rel_bias_table)
    (g_all,) = _exchange([g_pack], False, "gather_small", dx0)
    packs = _adamw(g_all, w_pack, m_pack, v_pack, "adamw_small", PACK_ROWS)

    d_wd1 = _tn_grad(hh1, do1, packs[0], "ffn1_wgrad_down", BM_WGRAD, tk)
    handles1_d, token1 = _exchange_start([blocks(d_wd1)], True, "grads_start_ffn1_down", dx0)
    d_wg1 = _tn_grad(dg1, xn1, token1, "ffn1_wgrad_gate", BM_WGRAD, tk)
    handles1_g, token1 = _exchange_start([blocks(d_wg1)], True, "grads_start_ffn1_gate", token1)
    d_wu1 = _tn_grad(du1, xn1, token1, "ffn1_wgrad_up", BM_WGRAD, tk)
    handles1_u, token1 = _exchange_start([blocks(d_wu1)], True, "grads_start_ffn1_up", token1)

    res = {}

    def update(names, parts):
        last = None
        for name, p in zip(names, parts):
            w, m_, v_, transposed = big[name]
            if transposed:
                w, m_, v_ = w.T, m_.T, v_.T
            new = _adamw(p, w, m_, v_, "adamw_" + name, _row_tile(w.shape[0], ADAM_ROWS))
            res[name] = tuple((a.T if transposed else a)[None] for a in new)
            last = new[0]
        return last

    parts2 = _exchange_wait(handles2, True, "grads_wait_ffn2", token1)
    done2 = update(["ffn2_w_gate", "ffn2_w_up", "ffn2_w_down"], parts2)
    parts_mix = _exchange_wait(handles_mix, True, "grads_wait_mix", done2)
    done_mix = update(["w_in", "w_out"], parts_mix)

    parts1_d = _exchange_wait(handles1_d, True, "grads_wait_ffn1_down", done_mix)
    done1 = update(["ffn1_w_down"], parts1_d)
    parts1_g = _exchange_wait(handles1_g, True, "grads_wait_ffn1_gate", done1)
    done1 = update(["ffn1_w_gate"], parts1_g)
    parts1_u = _exchange_wait(handles1_u, True, "grads_wait_ffn1_up", done1)
    update(["ffn1_w_up"], parts1_u)

    def unpack(pk):
        cwb = lax.dynamic_slice(pk[8:11, :CONV_DIM], (0, me * conv_w.shape[2]), (3, conv_w.shape[2]))
        return {
            "ffn1_norm": pk[0:1, :D], "mix_norm": pk[1:2, :D], "ffn2_norm": pk[2:3, :D],
            "final_norm": pk[3, :D],
            "attn_out_norm": pk[4:5, :ATTN_WIDTH], "conv_out_norm": pk[4:5, ATTN_WIDTH:ATTN_WIDTH + CONV_DIM],
            "attn_sinks": pk[5:6, :N_Q_HEADS],
            "conv_w": cwb[None],
            "rel_bias_table": pk[11, :NUM_BUCKETS * N_Q_HEADS].reshape(NUM_BUCKETS, N_Q_HEADS),
        }

    small = [unpack(pk) for pk in packs]
    loss = packs[0][6, 0]

    order = ["rel_bias_table", "ffn1_norm", "ffn1_w_gate", "ffn1_w_up", "ffn1_w_down", "mix_norm", "w_in",
             "conv_w", "attn_sinks", "attn_out_norm", "conv_out_norm", "w_out", "ffn2_norm",
             "ffn2_w_gate", "ffn2_w_up", "ffn2_w_down", "final_norm"]
    outs = [loss, dx0[None]]
    for kind in range(4):
        for name in order:
            outs.append(res[name][kind] if name in res else small[kind][name])
    return tuple(outs)
```

```python
import math

import numpy as np
import jax
import jax.numpy as jnp
from jax import lax
from jax.experimental import pallas as pl
from jax.experimental.pallas import tpu as pltpu

F32 = jnp.float32
BF16 = jnp.bfloat16

N_DEV = 8
EPS = 1e-6
HEAD_DIM = 64
N_Q_HEADS = 8
N_KV_HEADS = 2
GQA_GROUP = 4
ATTN_WIDTH = 512
KV_WIDTH = 128
CONV_DIM = 512
BLOCK = 128
WINDOW = 128
NUM_BUCKETS = 32
MAX_DISTANCE = 128
SCALE = HEAD_DIM ** -0.5
MASKED = -1e30
GROUP_ROWS = GQA_GROUP * BLOCK

ADAM_LR = 0.001
ADAM_B1 = 0.9
ADAM_B2 = 0.999
ADAM_EPS = 1e-08
ADAM_WD = 0.01
ADAM_STEP = 10

VMEM_LIMIT_BYTES = 40 * 1024 * 1024
VMEM_LIMIT_FFN_BYTES = 50 * 1024 * 1024
SUBLANES = 8
PACK_ROWS = 16
PACK_COLS = 1024

TM_FFN = 1024
TM_FFN_BWD = 1024
TM_MIX = 512
TK_WGRAD = 1024
TF_FFN = 256
BM_WGRAD = 1408
ROW_GROUPS = 4
ATTN_BLOCKS = 4
ADAM_ROWS = 256


def _row_tile(rows, limit):
    best = rows
    for t in range(16, min(rows, limit) + 1, 16):
        if rows % t == 0:
            best = t
    return best


def _params(*sem):
    return pltpu.CompilerParams(dimension_semantics=sem, vmem_limit_bytes=VMEM_LIMIT_BYTES)


def _hbm(*arrays):
    return [pltpu.with_memory_space_constraint(a, pltpu.HBM) for a in arrays]


def _dot(a, b):
    return jnp.dot(a, b, preferred_element_type=F32)


def _dot_nt(a, b):
    return lax.dot_general(a, b, (((1,), (1,)), ((), ())), preferred_element_type=F32)


def _dot_tn(a, b):
    return lax.dot_general(a, b, (((0,), (0,)), ((), ())), preferred_element_type=F32)


def _sigmoid(g):
    return 0.5 * jnp.tanh(0.5 * g) + 0.5


def _rms_stats(x):
    inv = lax.rsqrt(jnp.mean(x * x, axis=-1, keepdims=True) + EPS)
    return inv, x * inv


def _rms_bwd(dy, x, gain):
    inv, xhat = _rms_stats(x)
    dgain = jnp.sum(dy * xhat, axis=0, keepdims=True)
    dxh = dy * gain
    dx = inv * (dxh - xhat * jnp.mean(dxh * xhat, axis=-1, keepdims=True))
    return dx, dgain


def _peer_list():
    x, y, c = lax.axis_index("x"), lax.axis_index("y"), lax.axis_index("c")
    peers = []
    for k in range(1, N_DEV):
        px = 1 - x if (k >> 2) & 1 else x
        py = 1 - y if (k >> 1) & 1 else y
        pc = 1 - c if k & 1 else c
        peers.append((px, py, pc))
    return 4 * x + 2 * y + c, peers


def _exchange(arrs, scatter, name, after):
    n = len(arrs)
    out_shape = []
    for a in arrs:
        shp = a.shape if scatter else (N_DEV,) + a.shape
        out_shape.append(jax.ShapeDtypeStruct(shp, a.dtype))

    def body(*refs):
        ins, outs = refs[:n], refs[n + 1:2 * n + 1]
        send_sems, recv_sems, local_sems = refs[2 * n + 1:]
        me, peers = _peer_list()
        started = []
        for a in range(n):
            own = ins[a].at[me] if scatter else ins[a]
            loc = pltpu.make_async_copy(own, outs[a].at[me], local_sems.at[a])
            loc.start()
            started.append(loc)
        sends = []
        for a in range(n):
            for k, (px, py, pc) in enumerate(peers):
                src = ins[a].at[4 * px + 2 * py + pc] if scatter else ins[a]
                cp = pltpu.make_async_remote_copy(
                    src_ref=src, dst_ref=outs[a].at[me],
                    send_sem=send_sems.at[a, k], recv_sem=recv_sems.at[a, k],
                    device_id=(px, py, pc), device_id_type=pl.DeviceIdType.MESH)
                cp.start()
                sends.append(cp)
        for a in range(n):
            for k, (px, py, pc) in enumerate(peers):
                landed = outs[a].at[4 * px + 2 * py + pc]
                pltpu.make_async_remote_copy(
                    src_ref=landed, dst_ref=landed,
                    send_sem=send_sems.at[a, k], recv_sem=recv_sems.at[a, k],
                    device_id=(px, py, pc), device_id_type=pl.DeviceIdType.MESH).wait_recv()
        for cp in sends:
            cp.wait_send()
        for loc in started:
            loc.wait()

    hbm = pl.BlockSpec(memory_space=pl.ANY)
    return pl.pallas_call(
        body, name=name, out_shape=tuple(out_shape),
        in_specs=[hbm] * (n + 1), out_specs=tuple([hbm] * n),
        scratch_shapes=[pltpu.SemaphoreType.DMA((n, N_DEV - 1)),
                        pltpu.SemaphoreType.DMA((n, N_DEV - 1)),
                        pltpu.SemaphoreType.DMA((n,))],
    )(*arrs, after)


def _gather_two_level(arrs, name, after):
    n = len(arrs)
    out_shape = tuple(jax.ShapeDtypeStruct((N_DEV,) + a.shape, a.dtype) for a in arrs)

    def body(*refs):
        ins, outs = refs[:n], refs[n + 1:2 * n + 1]
        send_sems, recv_sems, local_sems = refs[2 * n + 1:]
        x, y, c = lax.axis_index("x"), lax.axis_index("y"), lax.axis_index("c")
        me, sibling = (x, y, c), (x, y, 1 - c)
        chips = [(1 - x, y), (x, 1 - y), (1 - x, 1 - y)]

        def copy(a, k, block, to, src=None):
            slot = outs[a].at[4 * block[0] + 2 * block[1] + block[2]]
            return pltpu.make_async_remote_copy(
                src_ref=slot if src is None else src, dst_ref=slot,
                send_sem=send_sems.at[a, k], recv_sem=recv_sems.at[a, k],
                device_id=to, device_id_type=pl.DeviceIdType.MESH)

        started = []
        for a in range(n):
            loc = pltpu.make_async_copy(ins[a], outs[a].at[4 * x + 2 * y + c], local_sems.at[a])
            loc.start()
            started.append(loc)
        sends = []
        for a in range(n):
            sends.append(copy(a, 0, me, sibling, src=ins[a]))
            sends += [copy(a, 1 + j, me, (*chip, c), src=ins[a]) for j, chip in enumerate(chips)]
        for cp in sends:
            cp.start()
        for j, chip in enumerate(chips):
            for a in range(n):
                copy(a, 1 + j, (*chip, c), me).wait_recv()
                fwd = copy(a, 4 + j, (*chip, c), sibling)
                fwd.start()
                sends.append(fwd)
        for a in range(n):
            copy(a, 0, sibling, me).wait_recv()
            for j, chip in enumerate(chips):
                copy(a, 4 + j, (*chip, 1 - c), me).wait_recv()
        for cp in sends:
            cp.wait_send()
        for loc in started:
            loc.wait()

    hbm = pl.BlockSpec(memory_space=pl.ANY)
    return pl.pallas_call(
        body, name=name, out_shape=out_shape,
        in_specs=[hbm] * (n + 1), out_specs=tuple([hbm] * n),
        scratch_shapes=[pltpu.SemaphoreType.DMA((n, N_DEV - 1)),
                        pltpu.SemaphoreType.DMA((n, N_DEV - 1)),
                        pltpu.SemaphoreType.DMA((n,))],
    )(*arrs, after)


_HBM = pl.BlockSpec(memory_space=pltpu.HBM)
_SEM = pl.BlockSpec(memory_space=pltpu.SEMAPHORE)
_EFFECT = pltpu.SideEffectType.DATAFLOW_SIDE_EFFECTING


def _split_copies(srcs, lands, send_sems, recv_sems, scatter):
    me, peers = _peer_list()
    copies = []
    for a in range(len(srcs)):
        for k, (px, py, pc) in enumerate(peers):
            src = srcs[a].at[4 * px + 2 * py + pc] if scatter else srcs[a]
            copies.append(pltpu.make_async_remote_copy(
                src_ref=src, dst_ref=lands[a].at[me],
                send_sem=send_sems[a].at[k], recv_sem=recv_sems[a].at[k],
                device_id=(px, py, pc), device_id_type=pl.DeviceIdType.MESH))
    return copies


def _exchange_start(arrs, scatter, name, after):
    n = len(arrs)
    me = 4 * lax.axis_index("x") + 2 * lax.axis_index("y") + lax.axis_index("c")
    lands = []
    for a in arrs:
        own = lax.dynamic_index_in_dim(a, me, 0, keepdims=True) if scatter else a[None]
        shp = a.shape if scatter else (N_DEV,) + a.shape
        lands.append(lax.dynamic_update_slice(lax.empty(shp, a.dtype), own, (me,) + (0,) * (len(shp) - 1)))

    def body(*refs):
        srcs, lnds = refs[:n], refs[n:2 * n]
        outs = refs[2 * n + 1:]
        send_sems, recv_sems = outs[:n], outs[n:2 * n]
        token = outs[4 * n]
        for cp in _split_copies(srcs, lnds, send_sems, recv_sems, scatter):
            cp.start()
        token[...] = jnp.zeros_like(token)

    sem = pltpu.SemaphoreType.DMA((N_DEV - 1,))
    out_shape = ([sem] * (2 * n) + [pltpu.HBM(a.shape, a.dtype) for a in arrs]
                 + [pltpu.HBM(l.shape, l.dtype) for l in lands] + [jax.ShapeDtypeStruct((SUBLANES, 128), F32)])
    res = pl.pallas_call(
        body, name=name, out_shape=tuple(out_shape),
        in_specs=[_HBM] * (2 * n) + [pl.BlockSpec(memory_space=pl.ANY)],
        out_specs=tuple([_SEM] * (2 * n) + [_HBM] * (2 * n) + [pl.BlockSpec(memory_space=pltpu.VMEM)]),
        input_output_aliases={i: 2 * n + i for i in range(2 * n)},
        compiler_params=pltpu.CompilerParams(has_side_effects=_EFFECT),
    )(*[pltpu.with_memory_space_constraint(a, pltpu.HBM) for a in arrs],
      *[pltpu.with_memory_space_constraint(l, pltpu.HBM) for l in lands], after)
    handles = [(res[2 * n + a], res[3 * n + a], res[a], res[n + a]) for a in range(n)]
    return handles, res[4 * n]


def _exchange_wait(handles, scatter, name, after):
    n = len(handles)

    def body(*refs):
        srcs, lnds = refs[:n], refs[n:2 * n]
        send_sems, recv_sems = refs[2 * n:3 * n], refs[3 * n:4 * n]
        for cp in _split_copies(srcs, lnds, send_sems, recv_sems, scatter):
            cp.wait_send()
            cp.wait_recv()

    srcs = [h[0] for h in handles]
    lands = [h[1] for h in handles]
    res = pl.pallas_call(
        body, name=name,
        out_shape=tuple([pltpu.HBM(a.shape, a.dtype) for a in srcs] + [pltpu.HBM(l.shape, l.dtype) for l in lands]),
        in_specs=[_HBM] * (2 * n) + [_SEM] * (2 * n) + [pl.BlockSpec(memory_space=pl.ANY)],
        out_specs=tuple([_HBM] * (2 * n)),
        input_output_aliases={i: i for i in range(2 * n)},
        compiler_params=pltpu.CompilerParams(has_side_effects=_EFFECT),
    )(*srcs, *lands, *[h[2] for h in handles], *[h[3] for h in handles], after)
    return list(res[n:])


def _row_groups(tm):
    return [slice(r * (tm // ROW_GROUPS), (r + 1) * (tm // ROW_GROUPS)) for r in range(ROW_GROUPS)]


def _stack_gate_up(wgt, wut, tf):
    F, D = wgt.shape
    return jnp.stack([wgt.reshape(F // tf, tf, D), wut.reshape(F // tf, tf, D)], axis=1).reshape(2 * F, D)


def _ffn_fwd(x, gain, wgu, wd, after, name, tm, tf, head=None):
    T, D = x.shape
    F = wd.shape[0]
    nj = F // tf
    n_in = 7 if head else 5

    def body(*refs):
        x_ref, g_ref, wgu_ref, wd_ref, after_ref = refs[:5]
        outs = refs[n_in:]
        xo_ref, xn_ref, gu_ref = outs[:3]
        xn_sc, acc_sc = outs[-2:]
        i, j = pl.program_id(0), pl.program_id(1)

        @pl.when(j == 0)
        def _():
            _, xhat = _rms_stats(x_ref[...])
            xn = (xhat * g_ref[...]).astype(BF16)
            xn_sc[...] = xn
            xn_ref[...] = xn
            acc_sc[...] = jnp.zeros_like(acc_sc)

        groups = _row_groups(tm)
        gus = [_dot_nt(xn_sc[rows, :], wgu_ref[...]) for rows in groups]
        hs = []
        for rows, gu in zip(groups, gus):
            gu_ref[rows, :] = gu.astype(BF16)
            g, u = gu[:, :tf], gu[:, tf:]
            hs.append((g * _sigmoid(g) * u).astype(BF16))
        for rows, h in zip(groups, hs):
            acc_sc[rows, :] += _dot(h, wd_ref[...])

        if head:
            t_ref, fg_ref = refs[5:7]
            dgain_ref, loss_ref = outs[3:5]

            @pl.when((i == 0) & (j == 0))
            def _():
                dgain_ref[...] = jnp.zeros_like(dgain_ref)
                loss_ref[...] = jnp.zeros_like(loss_ref)

        @pl.when(j == nj - 1)
        def _():
            for rows in groups:
                xo = x_ref[rows, :] + 0.5 * acc_sc[rows, :]
                if head:
                    fg = fg_ref[...]
                    _, xhat = _rms_stats(xo)
                    err = xhat * fg - t_ref[rows, :]
                    loss_ref[...] += 0.5 * jnp.sum(jnp.mean(err * err, axis=-1, keepdims=True),
                                                   axis=0, keepdims=True)
                    dx, dgain = _rms_bwd(err * (1.0 / D), xo, fg)
                    dgain_ref[...] += dgain
                    xo_ref[rows, :] = dx
                else:
                    xo_ref[rows, :] = xo

    tile = pl.BlockSpec((tm, D), lambda i, j: (i, 0))
    const = pl.BlockSpec((1, D), lambda i, j: (0, 0))
    in_specs = [tile, const, pl.BlockSpec((2 * tf, D), lambda i, j: (j, 0)),
                pl.BlockSpec((tf, D), lambda i, j: (j, 0)), pl.BlockSpec(memory_space=pl.ANY)]
    out_specs = [tile, pl.BlockSpec((tm, D), lambda i, j: (i, 0)), pl.BlockSpec((tm, 2 * tf), lambda i, j: (i, j))]
    out_shape = [jax.ShapeDtypeStruct((T, D), F32), jax.ShapeDtypeStruct((T, D), BF16),
                 jax.ShapeDtypeStruct((T, 2 * F), BF16)]
    operands = [x, gain, wgu, wd, after]
    if head:
        in_specs += [tile, const]
        out_specs += [const, pl.BlockSpec((SUBLANES, 128), lambda i, j: (0, 0))]
        out_shape += [jax.ShapeDtypeStruct((1, D), F32), jax.ShapeDtypeStruct((SUBLANES, 128), F32)]
        operands += list(head)
    return pl.pallas_call(
        body, name=name, grid=(T // tm, nj),
        in_specs=in_specs, out_specs=tuple(out_specs), out_shape=tuple(out_shape),
        scratch_shapes=[pltpu.VMEM((tm, D), BF16), pltpu.VMEM((tm, D), F32)],
        compiler_params=pltpu.CompilerParams(dimension_semantics=("arbitrary", "arbitrary"),
                                             vmem_limit_bytes=VMEM_LIMIT_FFN_BYTES),
    )(*_hbm(*operands))


def _ffn_bwd(dy, x, gain, gu, wgu, wd, after, name, tm, tf):
    T, D = x.shape
    F = wd.shape[0]
    nj = F // tf

    def body(dy_ref, x_ref, g_ref, gu_ref, wgu_ref, wd_ref, after_ref,
             dx_ref, dgain_ref, dg_ref, du_ref, hh_ref, do_ref, do_sc, acc_sc, dgu_sc):
        i, j = pl.program_id(0), pl.program_id(1)

        @pl.when((i == 0) & (j == 0))
        def _():
            dgain_ref[...] = jnp.zeros_like(dgain_ref)

        @pl.when(j == 0)
        def _():
            do = (0.5 * dy_ref[...]).astype(BF16)
            do_sc[...] = do
            do_ref[...] = do
            acc_sc[...] = jnp.zeros_like(acc_sc)

        groups = _row_groups(tm)
        dhs = [_dot_nt(do_sc[rows, :], wd_ref[...]) for rows in groups]
        for rows, dh in zip(groups, dhs):
            g = gu_ref[rows, :tf].astype(F32)
            u = gu_ref[rows, tf:].astype(F32)
            sig = _sigmoid(g)
            s = g * sig
            dg = (dh * u * (sig + s * (1.0 - sig))).astype(BF16)
            du = (dh * s).astype(BF16)
            dg_ref[rows, :] = dg
            du_ref[rows, :] = du
            dgu_sc[rows, :tf] = dg
            dgu_sc[rows, tf:] = du
            hh_ref[rows, :] = (s * u).astype(BF16)
        for rows in groups:
            acc_sc[rows, :] += _dot(dgu_sc[rows, :], wgu_ref[...])

        @pl.when(j == nj - 1)
        def _():
            for rows in groups:
                dx, dgain = _rms_bwd(acc_sc[rows, :], x_ref[rows, :], g_ref[...])
                dgain_ref[...] += dgain
                dx_ref[rows, :] = dy_ref[rows, :] + dx

    tile_in = pl.BlockSpec((tm, D), lambda i, j: (i, 0))
    tile = pl.BlockSpec((tm, D), lambda i, j: (i, 0), pipeline_mode=pl.Buffered(1))
    return pl.pallas_call(
        body, name=name, grid=(T // tm, nj),
        in_specs=[tile_in, tile_in,
                  pl.BlockSpec((1, D), lambda i, j: (0, 0)),
                  pl.BlockSpec((tm, 2 * tf), lambda i, j: (i, j)),
                  pl.BlockSpec((2 * tf, D), lambda i, j: (j, 0)),
                  pl.BlockSpec((tf, D), lambda i, j: (j, 0)),
                  pl.BlockSpec(memory_space=pl.ANY)],
        out_specs=(tile,
                   pl.BlockSpec((1, D), lambda i, j: (0, 0)),
                   pl.BlockSpec((tm, tf), lambda i, j: (i, j)),
                   pl.BlockSpec((tm, tf), lambda i, j: (i, j)),
                   pl.BlockSpec((tm, tf), lambda i, j: (i, j)),
                   pl.BlockSpec((tm, D), lambda i, j: (i, 0))),
        out_shape=(jax.ShapeDtypeStruct((T, D), F32), jax.ShapeDtypeStruct((1, D), F32),
                   jax.ShapeDtypeStruct((T, F), BF16), jax.ShapeDtypeStruct((T, F), BF16),
                   jax.ShapeDtypeStruct((T, F), BF16), jax.ShapeDtypeStruct((T, D), BF16)),
        scratch_shapes=[pltpu.VMEM((tm, D), BF16), pltpu.VMEM((tm, D), F32), pltpu.VMEM((tm, 2 * tf), BF16)],
        compiler_params=pltpu.CompilerParams(dimension_semantics=("arbitrary", "arbitrary"),
                                             vmem_limit_bytes=VMEM_LIMIT_FFN_BYTES),
    )(*_hbm(dy, x, gain, gu, wgu, wd, after))


def _tn_grad(a, b, after, name, bm, tk):
    T, M = a.shape
    D = b.shape[1]
    nk = T // tk

    def body(a_ref, b_ref, after_ref, o_ref, acc_sc):
        k = pl.program_id(1)

        @pl.when(k == 0)
        def _():
            acc_sc[...] = jnp.zeros_like(acc_sc)

        acc_sc[...] += _dot_tn(a_ref[...], b_ref[...])

        @pl.when(k == nk - 1)
        def _():
            o_ref[...] = acc_sc[...].astype(BF16)

    return pl.pallas_call(
        body, name=name, grid=(M // bm, nk),
        in_specs=[pl.BlockSpec((tk, bm), lambda i, k: (k, i)), pl.BlockSpec((tk, D), lambda i, k: (k, 0)),
                  pl.BlockSpec(memory_space=pl.ANY)],
        out_specs=pl.BlockSpec((bm, D), lambda i, k: (i, 0)),
        out_shape=jax.ShapeDtypeStruct((M, D), BF16),
        scratch_shapes=[pltpu.VMEM((bm, D), F32)],
        compiler_params=_params("arbitrary", "arbitrary"),
    )(*_hbm(a, b, after))


_Z_SPLITS = (0, 512, 640, 768, 1280, 1792, 2304)


def _mixin_fwd(x, gain, w_in_t, name, tm):
    T, D = x.shape
    widths = [b - a for a, b in zip(_Z_SPLITS[:-1], _Z_SPLITS[1:])]

    def body(x_ref, g_ref, w_ref, hn_ref, *outs):
        _, xhat = _rms_stats(x_ref[...])
        hn = (xhat * g_ref[...]).astype(BF16)
        hn_ref[...] = hn
        for o_ref, lo, hi in zip(outs, _Z_SPLITS[:-1], _Z_SPLITS[1:]):
            o_ref[...] = _dot_nt(hn, w_ref[lo:hi, :]).astype(BF16)

    return pl.pallas_call(
        body, name=name, grid=(T // tm,),
        in_specs=[pl.BlockSpec((tm, D), lambda i: (i, 0)),
                  pl.BlockSpec((1, D), lambda i: (0, 0)),
                  pl.BlockSpec(w_in_t.shape, lambda i: (0, 0))],
        out_specs=tuple([pl.BlockSpec((tm, D), lambda i: (i, 0))]
                        + [pl.BlockSpec((tm, w), lambda i: (i, 0)) for w in widths]),
        out_shape=tuple([jax.ShapeDtypeStruct((T, D), BF16)]
                        + [jax.ShapeDtypeStruct((T, w), BF16) for w in widths]),
        compiler_params=_params("arbitrary"),
    )(*_hbm(x, gain, w_in_t))


def _bucket_table():
    qi = np.arange(BLOCK, dtype=np.int32)[:, None]
    kj = np.arange(2 * BLOCK, dtype=np.int32)[None, :]
    dist = qi + BLOCK - kj
    n = np.maximum(dist, 0)
    max_exact = NUM_BUCKETS // 2
    large = max_exact + (np.log(np.maximum(n, 1).astype(np.float32) / max_exact)
                         / math.log(MAX_DISTANCE / max_exact)
                         * (NUM_BUCKETS - max_exact)).astype(np.int32)
    large = np.minimum(large, NUM_BUCKETS - 1)
    bucket = np.where(n < max_exact, n, large).astype(np.int32)
    valid = (dist >= 0) & (dist < WINDOW)
    return np.where(valid, bucket, -1).astype(np.int32)


def _bias_build(table, bucket, name):
    def body(t_ref, b_ref, o_ref):
        bk = b_ref[...]
        for h in range(N_Q_HEADS):
            def step(b, acc):
                return jnp.where(bk == b, t_ref[b, h], acc)
            o_ref[h] = lax.fori_loop(0, NUM_BUCKETS, step, jnp.full(bk.shape, MASKED, F32))

    return pl.pallas_call(
        body, name=name,
        in_specs=[pl.BlockSpec(memory_space=pltpu.SMEM), pl.BlockSpec(memory_space=pltpu.VMEM)],
        out_specs=pl.BlockSpec(memory_space=pltpu.VMEM),
        out_shape=jax.ShapeDtypeStruct((N_Q_HEADS,) + bucket.shape, F32),
    )(table, bucket)


def _bias_grad(dbias, bucket, name):
    def body(d_ref, b_ref, o_ref):
        bk = b_ref[...]
        row = lax.broadcasted_iota(jnp.int32, o_ref.shape, 0)
        lane = lax.broadcasted_iota(jnp.int32, o_ref.shape, 1)
        res = jnp.zeros(o_ref.shape, F32)
        for h in range(N_Q_HEADS):
            d = d_ref[h]

            def step(b, acc):
                part = jnp.sum(jnp.where(bk == b, d, 0.0), axis=0, keepdims=True)
                return jnp.where(row == b, part, acc)
            per_lane = lax.fori_loop(0, NUM_BUCKETS, step, jnp.zeros(o_ref.shape, F32))
            res = jnp.where(lane == h, jnp.sum(per_lane, axis=1, keepdims=True), res)
        o_ref[...] = res

    return pl.pallas_call(
        body, name=name,
        in_specs=[pl.BlockSpec(memory_space=pltpu.VMEM), pl.BlockSpec(memory_space=pltpu.VMEM)],
        out_specs=pl.BlockSpec(memory_space=pltpu.VMEM),
        out_shape=jax.ShapeDtypeStruct((NUM_BUCKETS, 128), F32),
    )(*_hbm(dbias, bucket))


def _head_cols(h):
    return slice(h * HEAD_DIM, (h + 1) * HEAD_DIM)


def _stack_heads(ref, r0, g, dtype):
    return jnp.concatenate(
        [ref[pl.ds(r0, BLOCK), _head_cols(GQA_GROUP * g + j)].astype(dtype) for j in range(GQA_GROUP)], axis=0)


def _unstack_heads(ref, r0, g, val):
    for j in range(GQA_GROUP):
        ref[pl.ds(r0, BLOCK), _head_cols(GQA_GROUP * g + j)] = val[j * BLOCK:(j + 1) * BLOCK, :]


def _head_lanes(h):
    return slice(h * BLOCK, (h + 1) * BLOCK)


def _group_lanes(g):
    return slice(g * GROUP_ROWS, (g + 1) * GROUP_ROWS)


def _head_softmax(st, bias_t, sink, no_prev):
    s = st * SCALE + bias_t
    row = lax.broadcasted_iota(jnp.int32, s.shape, 0)
    s = jnp.where(no_prev & (row < BLOCK), MASKED, s)
    m = jnp.maximum(jnp.max(s, axis=0, keepdims=True), sink)
    p = jnp.exp(s - m)
    ps = jnp.exp(sink - m)
    r = 1.0 / (jnp.sum(p, axis=0, keepdims=True) + ps)
    return p * r, ps * r


def _load_band(kf_sc, vf_sc, kp_ref, kc_ref, vp_ref, vc_ref, tq):
    kf_sc[0:BLOCK, :] = kp_ref[...]
    kf_sc[BLOCK:BLOCK + tq, :] = kc_ref[...]
    vf_sc[0:BLOCK, :] = vp_ref[...]
    vf_sc[BLOCK:BLOCK + tq, :] = vc_ref[...]


def _attn_fwd(q, k, v, bias_t, sinks, gain, name, nblk):
    T = q.shape[0]
    tq = nblk * BLOCK

    def body(sink_ref, q_ref, kc_ref, kp_ref, vc_ref, vp_ref, bias_ref, g_ref, raw_ref, nrm_ref,
             kf_sc, vf_sc, o_sc, st_sc, pt_sc):
        i = pl.program_id(0)
        _load_band(kf_sc, vf_sc, kp_ref, kc_ref, vp_ref, vc_ref, tq)

        def block(b, carry):
            r0 = pl.multiple_of(b * BLOCK, BLOCK)
            no_prev = (i == 0) & (b == 0)
            for g in range(N_KV_HEADS):
                kb = kf_sc[pl.ds(r0, 2 * BLOCK), _head_cols(g)]
                st_sc[:, _group_lanes(g)] = _dot_nt(kb, _stack_heads(q_ref, r0, g, BF16))
            for h in range(N_Q_HEADS):
                p, _ = _head_softmax(st_sc[:, _head_lanes(h)], bias_ref[h], sink_ref[h], no_prev)
                pt_sc[:, _head_lanes(h)] = p.astype(BF16)
            for g in range(N_KV_HEADS):
                vb = vf_sc[pl.ds(r0, 2 * BLOCK), _head_cols(g)]
                _unstack_heads(o_sc, r0, g, _dot_tn(pt_sc[:, _group_lanes(g)], vb))
            return carry

        lax.fori_loop(0, nblk, block, 0, unroll=True)
        o = o_sc[...]
        raw_ref[...] = o.astype(BF16)
        _, ohat = _rms_stats(o)
        nrm_ref[...] = (ohat * g_ref[...]).astype(BF16)

    cur = lambda i: (i, 0)
    prev = lambda i: (jnp.maximum(i * nblk - 1, 0), 0)
    lanes = N_Q_HEADS * BLOCK
    return pl.pallas_call(
        body, name=name, grid=(T // tq,),
        in_specs=[pl.BlockSpec(memory_space=pltpu.SMEM),
                  pl.BlockSpec((tq, ATTN_WIDTH), cur),
                  pl.BlockSpec((tq, KV_WIDTH), cur), pl.BlockSpec((BLOCK, KV_WIDTH), prev),
                  pl.BlockSpec((tq, KV_WIDTH), cur), pl.BlockSpec((BLOCK, KV_WIDTH), prev),
                  pl.BlockSpec(bias_t.shape, lambda i: (0, 0, 0)),
                  pl.BlockSpec((1, ATTN_WIDTH), lambda i: (0, 0))],
        out_specs=(pl.BlockSpec((tq, ATTN_WIDTH), cur), pl.BlockSpec((tq, ATTN_WIDTH), cur)),
        out_shape=(jax.ShapeDtypeStruct((T, ATTN_WIDTH), BF16), jax.ShapeDtypeStruct((T, ATTN_WIDTH), BF16)),
        scratch_shapes=[pltpu.VMEM((tq + BLOCK, KV_WIDTH), BF16), pltpu.VMEM((tq + BLOCK, KV_WIDTH), BF16),
                        pltpu.VMEM((tq, ATTN_WIDTH), F32),
                        pltpu.VMEM((2 * BLOCK, lanes), F32), pltpu.VMEM((2 * BLOCK, lanes), BF16)],
        compiler_params=_params("arbitrary"),
    )(sinks, *_hbm(q, k, k, v, v, bias_t, gain))


def _attn_bwd(dmixed, raw, q, k, v, bias_t, sinks, gain, name, nblk):
    T = q.shape[0]
    tq = nblk * BLOCK
    nt = T // tq
    lanes = N_Q_HEADS * BLOCK

    def body(sink_ref, dm_ref, raw_ref, q_ref, kc_ref, kp_ref, vc_ref, vp_ref, bias_ref, g_ref,
             dq_ref, dk_ref, dv_ref, dbias_ref, dsink_ref, dgain_ref,
             do_sc, dq_sc, kf_sc, vf_sc, dkf_sc, dvf_sc, st_sc, dpt_sc, pt_sc, dst_sc, drow_sc,
             qs_sc, dos_sc, dsink_sc):
        i = pl.program_id(0)
        tile = nt - 1 - i

        @pl.when(i == 0)
        def _():
            dkf_sc[...] = jnp.zeros_like(dkf_sc)
            dvf_sc[...] = jnp.zeros_like(dvf_sc)
            dsink_sc[...] = jnp.zeros_like(dsink_sc)
            dbias_ref[...] = jnp.zeros_like(dbias_ref)
            dgain_ref[...] = jnp.zeros_like(dgain_ref)

        carry_k = dkf_sc[0:BLOCK, :]
        carry_v = dvf_sc[0:BLOCK, :]
        dkf_sc[0:tq, :] = jnp.zeros((tq, KV_WIDTH), F32)
        dvf_sc[0:tq, :] = jnp.zeros((tq, KV_WIDTH), F32)
        dkf_sc[tq:tq + BLOCK, :] = carry_k
        dvf_sc[tq:tq + BLOCK, :] = carry_v
        _load_band(kf_sc, vf_sc, kp_ref, kc_ref, vp_ref, vc_ref, tq)

        do, dgain = _rms_bwd(dm_ref[...].astype(F32), raw_ref[...].astype(F32), g_ref[...])
        dgain_ref[...] += dgain
        do_sc[...] = do
        ones = jnp.ones((SUBLANES, HEAD_DIM), BF16)

        def block(b, carry):
            r0 = pl.multiple_of(b * BLOCK, BLOCK)
            no_prev = (tile == 0) & (b == 0)
            for g in range(N_KV_HEADS):
                kb = kf_sc[pl.ds(r0, 2 * BLOCK), _head_cols(g)]
                vb = vf_sc[pl.ds(r0, 2 * BLOCK), _head_cols(g)]
                qg = _stack_heads(q_ref, r0, g, BF16)
                dog = _stack_heads(do_sc, r0, g, F32)
                prod = dog * _stack_heads(raw_ref, r0, g, F32)
                hi = prod.astype(BF16)
                lo = (prod - hi.astype(F32)).astype(BF16)
                drow_sc[:, _group_lanes(g)] = _dot_nt(ones, hi) + _dot_nt(ones, lo)
                dogb = dog.astype(BF16)
                qs_sc[g] = qg
                dos_sc[g] = dogb
                st_sc[:, _group_lanes(g)] = _dot_nt(kb, qg)
                dpt_sc[:, _group_lanes(g)] = _dot_nt(vb, dogb)
            for h in range(N_Q_HEADS):
                hl = _head_lanes(h)
                p, ps = _head_softmax(st_sc[:, hl], bias_ref[h], sink_ref[h], no_prev)
                rowdot = drow_sc[0:1, hl]
                ds = p * (dpt_sc[:, hl] - rowdot)
                dsink_sc[h:h + 1, :] += -(ps * rowdot)
                dbias_ref[h] += ds
                dst_sc[:, hl] = ds.astype(BF16)
                pt_sc[:, hl] = p.astype(BF16)
            for g in range(N_KV_HEADS):
                kb = kf_sc[pl.ds(r0, 2 * BLOCK), _head_cols(g)]
                dsg = dst_sc[:, _group_lanes(g)]
                _unstack_heads(dq_sc, r0, g, _dot_tn(dsg, kb) * SCALE)
                dkf_sc[pl.ds(r0, 2 * BLOCK), _head_cols(g)] += _dot(dsg, qs_sc[g]) * SCALE
                dvf_sc[pl.ds(r0, 2 * BLOCK), _head_cols(g)] += _dot(pt_sc[:, _group_lanes(g)], dos_sc[g])
            return carry

        lax.fori_loop(0, nblk, block, 0, unroll=True)
        dq_ref[...] = dq_sc[...].astype(BF16)
        dk_ref[...] = dkf_sc[BLOCK:BLOCK + tq, :].astype(BF16)
        dv_ref[...] = dvf_sc[BLOCK:BLOCK + tq, :].astype(BF16)

        @pl.when(i == nt - 1)
        def _():
            tot = jnp.sum(dsink_sc[...], axis=1, keepdims=True)
            dsink_ref[...] = jnp.broadcast_to(tot, dsink_ref.shape)

    cur = lambda i: (nt - 1 - i, 0)
    prev = lambda i: (jnp.maximum((nt - 1 - i) * nblk - 1, 0), 0)
    const2 = lambda i: (0, 0)
    const3 = lambda i: (0, 0, 0)
    return pl.pallas_call(
        body, name=name, grid=(nt,),
        in_specs=[pl.BlockSpec(memory_space=pltpu.SMEM),
                  pl.BlockSpec((tq, ATTN_WIDTH), cur),
                  pl.BlockSpec((tq, ATTN_WIDTH), cur),
                  pl.BlockSpec((tq, ATTN_WIDTH), cur),
                  pl.BlockSpec((tq, KV_WIDTH), cur), pl.BlockSpec((BLOCK, KV_WIDTH), prev),
                  pl.BlockSpec((tq, KV_WIDTH), cur), pl.BlockSpec((BLOCK, KV_WIDTH), prev),
                  pl.BlockSpec(bias_t.shape, const3),
                  pl.BlockSpec((1, ATTN_WIDTH), const2)],
        out_specs=(pl.BlockSpec((tq, ATTN_WIDTH), cur),
                   pl.BlockSpec((tq, KV_WIDTH), cur), pl.BlockSpec((tq, KV_WIDTH), cur),
                   pl.BlockSpec(bias_t.shape, const3),
                   pl.BlockSpec((N_Q_HEADS, 128), const2),
                   pl.BlockSpec((1, ATTN_WIDTH), const2)),
        out_shape=(jax.ShapeDtypeStruct((T, ATTN_WIDTH), BF16),
                   jax.ShapeDtypeStruct((T, KV_WIDTH), BF16), jax.ShapeDtypeStruct((T, KV_WIDTH), BF16),
                   jax.ShapeDtypeStruct(bias_t.shape, F32),
                   jax.ShapeDtypeStruct((N_Q_HEADS, 128), F32),
                   jax.ShapeDtypeStruct((1, ATTN_WIDTH), F32)),
        scratch_shapes=[pltpu.VMEM((tq, ATTN_WIDTH), F32), pltpu.VMEM((tq, ATTN_WIDTH), F32),
                        pltpu.VMEM((tq + BLOCK, KV_WIDTH), BF16), pltpu.VMEM((tq + BLOCK, KV_WIDTH), BF16),
                        pltpu.VMEM((tq + BLOCK, KV_WIDTH), F32), pltpu.VMEM((tq + BLOCK, KV_WIDTH), F32),
                        pltpu.VMEM((2 * BLOCK, lanes), F32), pltpu.VMEM((2 * BLOCK, lanes), F32),
                        pltpu.VMEM((2 * BLOCK, lanes), BF16), pltpu.VMEM((2 * BLOCK, lanes), BF16),
                        pltpu.VMEM((SUBLANES, lanes), F32),
                        pltpu.VMEM((N_KV_HEADS, GROUP_ROWS, HEAD_DIM), BF16),
                        pltpu.VMEM((N_KV_HEADS, GROUP_ROWS, HEAD_DIM), BF16),
                        pltpu.VMEM((N_Q_HEADS, 128), F32)],
        compiler_params=_params("arbitrary"),
    )(sinks, *_hbm(dmixed, raw, q, k, k, v, v, bias_t, gain))


def _shift_down(cu, tail):
    row = lax.broadcasted_iota(jnp.int32, cu.shape, 0)
    t6, t7 = tail[6:7, :], tail[7:8, :]
    s1 = jnp.where(row == 0, t7, pltpu.roll(cu, 1, 0))
    s2 = jnp.where(row == 0, t6, jnp.where(row == 1, t7, pltpu.roll(cu, 2, 0)))
    return s1, s2


def _shift_up(d, head):
    n = d.shape[0]
    row = lax.broadcasted_iota(jnp.int32, d.shape, 0)
    h0, h1 = head[0:1, :], head[1:2, :]
    s1 = jnp.where(row == n - 1, h0, pltpu.roll(d, n - 1, 0))
    s2 = jnp.where(row == n - 1, h1, jnp.where(row == n - 2, h0, pltpu.roll(d, n - 2, 0)))
    return s1, s2


def _mixout_fwd(x, attn_n, u, gb, gc, conv_w, gain, w_out, name, tm):
    T, D = x.shape

    def body(x_ref, an_ref, u_ref, b_ref, c_ref, cw_ref, g_ref, wo_ref, xo_ref, cn_ref, tail_sc):
        @pl.when(pl.program_id(0) == 0)
        def _():
            tail_sc[...] = jnp.zeros_like(tail_sc)

        cu = c_ref[...].astype(F32) * u_ref[...].astype(F32)
        s1, s2 = _shift_down(cu, tail_sc[...])
        tail_sc[...] = cu[tm - SUBLANES:tm, :]
        pre = cw_ref[0:1, :] * s2 + cw_ref[1:2, :] * s1 + cw_ref[2:3, :] * cu
        conv = b_ref[...].astype(F32) * pre
        _, chat = _rms_stats(conv)
        cn = (chat * g_ref[...]).astype(BF16)
        cn_ref[...] = cn
        xo_ref[...] = (x_ref[...] + _dot(an_ref[...], wo_ref[0:ATTN_WIDTH, :])
                       + _dot(cn, wo_ref[ATTN_WIDTH:ATTN_WIDTH + CONV_DIM, :]))

    row = lambda i: (i, 0)
    const = lambda i: (0, 0)
    return pl.pallas_call(
        body, name=name, grid=(T // tm,),
        in_specs=[pl.BlockSpec((tm, D), row), pl.BlockSpec((tm, ATTN_WIDTH), row),
                  pl.BlockSpec((tm, CONV_DIM), row), pl.BlockSpec((tm, CONV_DIM), row),
                  pl.BlockSpec((tm, CONV_DIM), row),
                  pl.BlockSpec(conv_w.shape, const), pl.BlockSpec((1, CONV_DIM), const),
                  pl.BlockSpec(w_out.shape, const)],
        out_specs=(pl.BlockSpec((tm, D), row), pl.BlockSpec((tm, CONV_DIM), row)),
        out_shape=(jax.ShapeDtypeStruct((T, D), F32), jax.ShapeDtypeStruct((T, CONV_DIM), BF16)),
        scratch_shapes=[pltpu.VMEM((SUBLANES, CONV_DIM), F32)],
        compiler_params=_params("arbitrary"),
    )(*_hbm(x, attn_n, u, gb, gc, conv_w, gain, w_out))


def _mixout_bwd(dy, attn_n, conv_n, w_out, after, name, tm):
    T, D = dy.shape
    W = ATTN_WIDTH + CONV_DIM
    nt = T // tm

    def body(dy_ref, an_ref, cn_ref, w_ref, after_ref, dm_ref, dw_ref, dw_sc):
        i = pl.program_id(0)

        @pl.when(i == 0)
        def _():
            dw_sc[...] = jnp.zeros_like(dw_sc)

        dyb = dy_ref[...].astype(BF16)
        dm_ref[...] = _dot_nt(dyb, w_ref[...]).astype(BF16)
        dw_sc[0:ATTN_WIDTH, :] += _dot_tn(an_ref[...], dyb)
        dw_sc[ATTN_WIDTH:W, :] += _dot_tn(cn_ref[...], dyb)

        @pl.when(i == nt - 1)
        def _():
            dw_ref[...] = dw_sc[...].astype(BF16)

    row = lambda i: (i, 0)
    const = lambda i: (0, 0)
    return pl.pallas_call(
        body, name=name, grid=(nt,),
        in_specs=[pl.BlockSpec((tm, D), row), pl.BlockSpec((tm, ATTN_WIDTH), row),
                  pl.BlockSpec((tm, CONV_DIM), row), pl.BlockSpec(w_out.shape, const),
                  pl.BlockSpec(memory_space=pl.ANY)],
        out_specs=(pl.BlockSpec((tm, W), row), pl.BlockSpec((W, D), const)),
        out_shape=(jax.ShapeDtypeStruct((T, W), BF16), jax.ShapeDtypeStruct((W, D), BF16)),
        scratch_shapes=[pltpu.VMEM((W, D), F32)],
        compiler_params=_params("arbitrary"),
    )(*_hbm(dy, attn_n, conv_n, w_out, after))


def _conv_bwd(dmixed, u, gb, gc, conv_w, gain, name, tc):
    T = u.shape[0]
    nt = T // tc
    per8 = tc // SUBLANES

    def body(dm_ref, u_ref, b_ref, c_ref, ut_ref, ct_ref, cw_ref, g_ref,
             du_ref, db_ref, dc_ref, dcw_ref, dgain_ref, head_sc):
        i = pl.program_id(0)

        @pl.when(i == 0)
        def _():
            head_sc[...] = jnp.zeros_like(head_sc)
            dcw_ref[...] = jnp.zeros_like(dcw_ref)
            dgain_ref[...] = jnp.zeros_like(dgain_ref)

        uu = u_ref[...].astype(F32)
        cc = c_ref[...].astype(F32)
        bb = b_ref[...].astype(F32)
        cu = cc * uu
        tail = jnp.where(i == nt - 1, 0.0, ct_ref[...].astype(F32) * ut_ref[...].astype(F32))
        s1, s2 = _shift_down(cu, tail)
        w0, w1, w2 = cw_ref[0:1, :], cw_ref[1:2, :], cw_ref[2:3, :]
        pre = w0 * s2 + w1 * s1 + w2 * cu
        dconv, dgain = _rms_bwd(dm_ref[...].astype(F32), bb * pre, g_ref[...])
        dgain_ref[...] += dgain
        db_ref[...] = (dconv * pre).astype(BF16)
        dpre = dconv * bb
        dcw_ref[0:1, :] += jnp.sum(dpre * s2, axis=0, keepdims=True)
        dcw_ref[1:2, :] += jnp.sum(dpre * s1, axis=0, keepdims=True)
        dcw_ref[2:3, :] += jnp.sum(dpre * cu, axis=0, keepdims=True)
        n1, n2 = _shift_up(dpre, head_sc[...])
        head_sc[...] = dpre[0:SUBLANES, :]
        dcu = w2 * dpre + w1 * n1 + w0 * n2
        du_ref[...] = (dcu * cc).astype(BF16)
        dc_ref[...] = (dcu * uu).astype(BF16)

    rev = lambda i: (nt - 1 - i, 0)
    rev_right = lambda i: (nt - 1 - i, 1)
    tail_map = lambda i: (jnp.maximum((nt - 1 - i) * per8 - 1, 0), 0)
    const = lambda i: (0, 0)
    return pl.pallas_call(
        body, name=name, grid=(nt,),
        in_specs=[pl.BlockSpec((tc, CONV_DIM), rev_right),
                  pl.BlockSpec((tc, CONV_DIM), rev), pl.BlockSpec((tc, CONV_DIM), rev),
                  pl.BlockSpec((tc, CONV_DIM), rev),
                  pl.BlockSpec((SUBLANES, CONV_DIM), tail_map), pl.BlockSpec((SUBLANES, CONV_DIM), tail_map),
                  pl.BlockSpec(conv_w.shape, const), pl.BlockSpec((1, CONV_DIM), const)],
        out_specs=(pl.BlockSpec((tc, CONV_DIM), rev), pl.BlockSpec((tc, CONV_DIM), rev),
                   pl.BlockSpec((tc, CONV_DIM), rev),
                   pl.BlockSpec((SUBLANES, CONV_DIM), const), pl.BlockSpec((1, CONV_DIM), const)),
        out_shape=(jax.ShapeDtypeStruct((T, CONV_DIM), BF16), jax.ShapeDtypeStruct((T, CONV_DIM), BF16),
                   jax.ShapeDtypeStruct((T, CONV_DIM), BF16),
                   jax.ShapeDtypeStruct((SUBLANES, CONV_DIM), F32), jax.ShapeDtypeStruct((1, CONV_DIM), F32)),
        scratch_shapes=[pltpu.VMEM((SUBLANES, CONV_DIM), F32)],
        compiler_params=_params("arbitrary"),
    )(*_hbm(dmixed, u, gb, gc, u, gc, conv_w, gain))


def _mixin_bwd(dy, x, gain, dz, w_in_t, name, tm):
    T, D = x.shape
    nz = len(dz)

    def body(dy_ref, x_ref, g_ref, *rest):
        dz_refs, wt_ref, dx_ref, dgain_ref = rest[:nz], rest[nz], rest[nz + 1], rest[nz + 2]

        @pl.when(pl.program_id(0) == 0)
        def _():
            dgain_ref[...] = jnp.zeros_like(dgain_ref)

        dh = jnp.zeros((tm, D), F32)
        for r, lo, hi in zip(dz_refs, _Z_SPLITS[:-1], _Z_SPLITS[1:]):
            dh += _dot(r[...], wt_ref[lo:hi, :])
        dx, dgain = _rms_bwd(dh, x_ref[...], g_ref[...])
        dgain_ref[...] += dgain
        dx_ref[...] = dy_ref[...] + dx

    row = lambda i: (i, 0)
    const = lambda i: (0, 0)
    return pl.pallas_call(
        body, name=name, grid=(T // tm,),
        in_specs=[pl.BlockSpec((tm, D), row), pl.BlockSpec((tm, D), row), pl.BlockSpec((1, D), const)]
                 + [pl.BlockSpec((tm, a.shape[1]), row) for a in dz]
                 + [pl.BlockSpec(w_in_t.shape, const)],
        out_specs=(pl.BlockSpec((tm, D), row), pl.BlockSpec((1, D), const)),
        out_shape=(jax.ShapeDtypeStruct((T, D), F32), jax.ShapeDtypeStruct((1, D), F32)),
        compiler_params=_params("arbitrary"),
    )(*_hbm(dy, x, gain, *dz, w_in_t))


def _win_grad(dz, hn, name, tk):
    T, D = hn.shape
    nz = len(dz)
    nt = T // tk
    W = _Z_SPLITS[-1]

    def body(hn_ref, *rest):
        dz_refs, dw_ref, dw_sc = rest[:nz], rest[nz], rest[nz + 1]
        i = pl.program_id(0)

        @pl.when(i == 0)
        def _():
            dw_sc[...] = jnp.zeros_like(dw_sc)

        hn = hn_ref[...]
        for r, lo, hi in zip(dz_refs, _Z_SPLITS[:-1], _Z_SPLITS[1:]):
            dw_sc[lo:hi, :] += _dot_tn(r[...], hn)

        @pl.when(i == nt - 1)
        def _():
            dw_ref[...] = dw_sc[...].astype(BF16)

    row = lambda i: (i, 0)
    return pl.pallas_call(
        body, name=name, grid=(nt,),
        in_specs=[pl.BlockSpec((tk, D), row)] + [pl.BlockSpec((tk, a.shape[1]), row) for a in dz],
        out_specs=pl.BlockSpec((W, D), lambda i: (0, 0)),
        out_shape=jax.ShapeDtypeStruct((W, D), BF16),
        scratch_shapes=[pltpu.VMEM((W, D), F32)],
        compiler_params=_params("arbitrary"),
    )(*_hbm(hn, *dz))


def _adamw(parts, w, m, v, name, tr):
    P = parts.shape[0]
    R, C = w.shape

    def body(p_ref, w_ref, m_ref, v_ref, g_ref, d_ref, nm_ref, nv_ref):
        g = p_ref[0].astype(F32)
        for d in range(1, P):
            g = g + p_ref[d].astype(F32)
        nm = ADAM_B1 * m_ref[...] + (1.0 - ADAM_B1) * g
        nv = ADAM_B2 * v_ref[...] + (1.0 - ADAM_B2) * (g * g)
        m_hat = nm / (1.0 - ADAM_B1 ** ADAM_STEP)
        v_hat = nv / (1.0 - ADAM_B2 ** ADAM_STEP)
        g_ref[...] = g
        nm_ref[...] = nm
        nv_ref[...] = nv
        d_ref[...] = -ADAM_LR * (m_hat / (jnp.sqrt(v_hat) + ADAM_EPS) + ADAM_WD * w_ref[...])

    row = lambda i: (i, 0)
    spec = pl.BlockSpec((tr, C), row)
    shp = jax.ShapeDtypeStruct((R, C), F32)
    return pl.pallas_call(
        body, name=name, grid=(R // tr,),
        in_specs=[pl.BlockSpec((P, tr, C), lambda i: (0, i, 0)), spec, spec, spec],
        out_specs=(spec, spec, spec, spec),
        out_shape=(shp, shp, shp, shp),
        compiler_params=_params("arbitrary"),
    )(*_hbm(parts, w, m, v))


def _columns_of_blocks(g):
    n, R, w = g.shape
    return g.transpose(1, 0, 2).reshape(R, n * w)


def _pad_row(vec):
    vec = vec.reshape(1, -1)
    return jnp.pad(vec, ((0, 0), (0, PACK_COLS - vec.shape[1])))


def kernel(x, rel_bias_table, ffn1_norm, ffn1_w_gate, ffn1_w_up, ffn1_w_down, mix_norm, w_in, conv_w, attn_sinks, attn_out_norm, conv_out_norm, w_out, ffn2_norm, ffn2_w_gate, ffn2_w_up, ffn2_w_down, final_norm, loss_target, m_rel_bias_table, m_ffn1_norm, m_ffn1_w_gate, m_ffn1_w_up, m_ffn1_w_down, m_mix_norm, m_w_in, m_conv_w, m_attn_sinks, m_attn_out_norm, m_conv_out_norm, m_w_out, m_ffn2_norm, m_ffn2_w_gate, m_ffn2_w_up, m_ffn2_w_down, m_final_norm, v_rel_bias_table, v_ffn1_norm, v_ffn1_w_gate, v_ffn1_w_up, v_ffn1_w_down, v_mix_norm, v_w_in, v_conv_w, v_attn_sinks, v_attn_out_norm, v_conv_out_norm, v_w_out, v_ffn2_norm, v_ffn2_w_gate, v_ffn2_w_up, v_ffn2_w_down, v_final_norm):
    T, D = x.shape[1], x.shape[2]
    x0 = x[0]
    target = loss_target[0]
    tm = min(TM_FFN, T)
    tm_bwd = min(TM_FFN_BWD, T)
    tm_mix = min(TM_MIX, T)
    tk = min(TK_WGRAD, T)
    tf = TF_FFN
    nblk = min(ATTN_BLOCKS, T // BLOCK)
    me = 4 * lax.axis_index("x") + 2 * lax.axis_index("y") + lax.axis_index("c")

    big = {
        "ffn1_w_gate": (ffn1_w_gate[0], m_ffn1_w_gate[0], v_ffn1_w_gate[0], True),
        "ffn1_w_up": (ffn1_w_up[0], m_ffn1_w_up[0], v_ffn1_w_up[0], True),
        "ffn1_w_down": (ffn1_w_down[0], m_ffn1_w_down[0], v_ffn1_w_down[0], False),
        "w_in": (w_in[0], m_w_in[0], v_w_in[0], True),
        "w_out": (w_out[0], m_w_out[0], v_w_out[0], False),
        "ffn2_w_gate": (ffn2_w_gate[0], m_ffn2_w_gate[0], v_ffn2_w_gate[0], True),
        "ffn2_w_up": (ffn2_w_up[0], m_ffn2_w_up[0], v_ffn2_w_up[0], True),
        "ffn2_w_down": (ffn2_w_down[0], m_ffn2_w_down[0], v_ffn2_w_down[0], False),
    }

    def block_to_send(name):
        w, _, _, transposed = big[name]
        return (w.T if transposed else w).astype(BF16)

    names1 = ["ffn1_w_gate", "ffn1_w_up", "ffn1_w_down"]
    names_rest = ["w_in", "w_out", "ffn2_w_gate", "ffn2_w_up", "ffn2_w_down"]
    first = _gather_two_level([block_to_send(n) for n in names1], "gather_ffn1", ffn1_norm)
    wgt1, wut1, wd1 = [g.reshape(-1, D) for g in first]
    h_rest, token = _exchange_start([block_to_send(n) for n in names_rest[:2]] + [conv_w[0]]
                                    + [block_to_send(n) for n in names_rest[2:]], False,
                                    "gather_start_rest", wd1)

    wgu1 = _stack_gate_up(wgt1, wut1, tf)
    x1, xn1, gu1 = _ffn_fwd(x0, ffn1_norm, wgu1, wd1, token, "ffn1_fwd", tm, tf)
    mixw = _exchange_wait(h_rest[:3], False, "gather_wait_mix", x1)
    win_t = mixw[0].reshape(-1, D)
    wout = mixw[1].reshape(-1, D)
    cw = _columns_of_blocks(mixw[2])
    hn, q, k, v, u, gb, gc = _mixin_fwd(x1, mix_norm, win_t, "mixin_fwd", tm_mix)
    bucket = jnp.asarray(_bucket_table().T.copy())
    sinks = attn_sinks.reshape(-1)
    bias_t = _bias_build(rel_bias_table, bucket, "bias_build")
    attn_raw, attn_n = _attn_fwd(q, k, v, bias_t, sinks, attn_out_norm, "attn_fwd", nblk)
    x2, conv_n = _mixout_fwd(x1, attn_n, u, gb, gc, cw, conv_out_norm, wout, "mixout_fwd", tm_mix)
    wgt2, wut2, wd2 = [g.reshape(-1, D) for g in _exchange_wait(h_rest[3:], False, "gather_wait_ffn2", x2)]
    wgu2 = _stack_gate_up(wgt2, wut2, tf)
    dx3, xn2, gu2, d_final, loss_part = _ffn_fwd(x2, ffn2_norm, wgu2, wd2, x2, "ffn2_fwd", tm, tf,
                                                 head=(target, final_norm.reshape(1, D)))

    def blocks(g):
        return g.reshape(N_DEV, -1, D)

    dx2, d_ffn2_norm, dg2, du2, hh2, do2 = _ffn_bwd(
        dx3, x2, ffn2_norm, gu2, wgu2, wd2, dx3, "ffn2_bwd", tm_bwd, tf)
    d_wg2 = _tn_grad(dg2, xn2, dx2, "ffn2_wgrad_gate", BM_WGRAD, tk)
    d_wu2 = _tn_grad(du2, xn2, dx2, "ffn2_wgrad_up", BM_WGRAD, tk)
    d_wd2 = _tn_grad(hh2, do2, dx2, "ffn2_wgrad_down", BM_WGRAD, tk)
    handles2, token2 = _exchange_start([blocks(d_wg2), blocks(d_wu2), blocks(d_wd2)], True,
                                       "grads_start_ffn2", dx2)

    dmixed, d_wout = _mixout_bwd(dx2, attn_n, conv_n, wout, token2, "mixout_bwd", tm_mix)
    dq, dk, dv, dbias, dsink, d_attn_norm = _attn_bwd(
        dmixed, attn_raw, q, k, v, bias_t, sinks, attn_out_norm, "attn_bwd", nblk)
    du, dgb, dgc, d_cw, d_conv_norm = _conv_bwd(dmixed, u, gb, gc, cw, conv_out_norm, "conv_bwd", tm_mix)
    d_table = _bias_grad(dbias, bucket, "bias_grad")
    dz = [dq, dk, dv, du, dgb, dgc]
    dx1, d_mix_norm = _mixin_bwd(dx2, x1, mix_norm, dz, win_t, "mixin_bwd", tm_mix)
    d_win_t = _win_grad(dz, hn, "win_grad", min(TM_MIX, T))
    handles_mix, token_mix = _exchange_start([blocks(d_win_t), blocks(d_wout)], True, "grads_start_mix", d_table)

    dx0, d_ffn1_norm, dg1, du1, hh1, do1 = _ffn_bwd(
        dx1, x0, ffn1_norm, gu1, wgu1, wd1, token_mix, "ffn1_bwd", tm_bwd, tf)
    def pack(ffn1, mixn, ffn2, fin, attn_n_, conv_n_, sink_, extra, convw, table):
        rows = [_pad_row(ffn1), _pad_row(mixn), _pad_row(ffn2), _pad_row(fin),
                _pad_row(jnp.concatenate([attn_n_.reshape(-1), conv_n_.reshape(-1)])),
                _pad_row(sink_), _pad_row(extra),
                jnp.zeros((1, PACK_COLS), F32),
                jnp.pad(convw, ((0, 0), (0, PACK_COLS - convw.shape[1]))),
                _pad_row(table),
                jnp.zeros((PACK_ROWS - 12, PACK_COLS), F32)]
        return jnp.concatenate(rows, axis=0)

    def own_channels(a):
        full = jnp.zeros((a.shape[1], CONV_DIM), F32)
        return lax.dynamic_update_slice(full, a[0], (0, me * a.shape[2]))

    g_pack = pack(d_ffn1_norm, d_mix_norm, d_ffn2_norm, d_final, d_attn_norm, d_conv_norm,
                  dsink[:, 0], loss_part[0, :1], d_cw[:3], d_table[:, :N_Q_HEADS])
    zero1 = jnp.zeros((1,), F32)
    w_pack = pack(ffn1_norm, mix_norm, ffn2_norm, final_norm, attn_out_norm, conv_out_norm,
                  attn_sinks, zero1, own_channels(conv_w), rel_bias_table)
    m_pack = pack(m_ffn1_norm, m_mix_norm, m_ffn2_norm, m_final_norm, m_attn_out_norm, m_conv_out_norm,
                  m_attn_sinks, zero1, own_channels(m_conv_w), m_rel_bias_table)
    v_pack = pack(v_ffn1_norm, v_mix_norm, v_ffn2_norm, v_final_norm, v_attn_out_norm, v_conv_out_norm,
                  v_attn_sinks, zero1, own_channels(v_conv_w), v_rel_bias_table)
    (g_all,) = _exchange([g_pack], False, "gather_small", dx0)
    packs = _adamw(g_all, w_pack, m_pack, v_pack, "adamw_small", PACK_ROWS)

    d_wd1 = _tn_grad(hh1, do1, packs[0], "ffn1_wgrad_down", BM_WGRAD, tk)
    handles1_d, token1 = _exchange_start([blocks(d_wd1)], True, "grads_start_ffn1_down", dx0)
    d_wg1 = _tn_grad(dg1, xn1, token1, "ffn1_wgrad_gate", BM_WGRAD, tk)
    handles1_g, token1 = _exchange_start([blocks(d_wg1)], True, "grads_start_ffn1_gate", token1)
    d_wu1 = _tn_grad(du1, xn1, token1, "ffn1_wgrad_up", BM_WGRAD, tk)
    handles1_u, token1 = _exchange_start([blocks(d_wu1)], True, "grads_start_ffn1_up", token1)

    res = {}

    def update(names, parts):
        last = None
        for name, p in zip(names, parts):
            w, m_, v_, transposed = big[name]
            if transposed:
                w, m_, v_ = w.T, m_.T, v_.T
            new = _adamw(p, w, m_, v_, "adamw_" + name, _row_tile(w.shape[0], ADAM_ROWS))
            res[name] = tuple((a.T if transposed else a)[None] for a in new)
            last = new[0]
        return last

    parts2 = _exchange_wait(handles2, True, "grads_wait_ffn2", token1)
    done2 = update(["ffn2_w_gate", "ffn2_w_up", "ffn2_w_down"], parts2)
    parts_mix = _exchange_wait(handles_mix, True, "grads_wait_mix", done2)
    done_mix = update(["w_in", "w_out"], parts_mix)

    parts1_d = _exchange_wait(handles1_d, True, "grads_wait_ffn1_down", done_mix)
    done1 = update(["ffn1_w_down"], parts1_d)
    parts1_g = _exchange_wait(handles1_g, True, "grads_wait_ffn1_gate", done1)
    done1 = update(["ffn1_w_gate"], parts1_g)
    parts1_u = _exchange_wait(handles1_u, True, "grads_wait_ffn1_up", done1)
    update(["ffn1_w_up"], parts1_u)

    def unpack(pk):
        cwb = lax.dynamic_slice(pk[8:11, :CONV_DIM], (0, me * conv_w.shape[2]), (3, conv_w.shape[2]))
        return {
            "ffn1_norm": pk[0:1, :D], "mix_norm": pk[1:2, :D], "ffn2_norm": pk[2:3, :D],
            "final_norm": pk[3, :D],
            "attn_out_norm": pk[4:5, :ATTN_WIDTH], "conv_out_norm": pk[4:5, ATTN_WIDTH:ATTN_WIDTH + CONV_DIM],
            "attn_sinks": pk[5:6, :N_Q_HEADS],
            "conv_w": cwb[None],
            "rel_bias_table": pk[11, :NUM_BUCKETS * N_Q_HEADS].reshape(NUM_BUCKETS, N_Q_HEADS),
        }

    small = [unpack(pk) for pk in packs]
    loss = packs[0][6, 0]

    order = ["rel_bias_table", "ffn1_norm", "ffn1_w_gate", "ffn1_w_up", "ffn1_w_down", "mix_norm", "w_in",
             "conv_w", "attn_sinks", "attn_out_norm", "conv_out_norm", "w_out", "ffn2_norm",
             "ffn2_w_gate", "ffn2_w_up", "ffn2_w_down", "final_norm"]
    outs = [loss, dx0[None]]
    for kind in range(4):
        for name in order:
            outs.append(res[name][kind] if name in res else small[kind][name])
    return tuple(outs)
```

```python
import math

import numpy as np
import jax
import jax.numpy as jnp
from jax import lax
from jax.experimental import pallas as pl
from jax.experimental.pallas import tpu as pltpu

F32 = jnp.float32
BF16 = jnp.bfloat16

N_DEV = 8
EPS = 1e-6
HEAD_DIM = 64
N_Q_HEADS = 8
N_KV_HEADS = 2
GQA_GROUP = 4
ATTN_WIDTH = 512
KV_WIDTH = 128
CONV_DIM = 512
BLOCK = 128
WINDOW = 128
NUM_BUCKETS = 32
MAX_DISTANCE = 128
SCALE = HEAD_DIM ** -0.5
MASKED = -1e30
GROUP_ROWS = GQA_GROUP * BLOCK

ADAM_LR = 0.001
ADAM_B1 = 0.9
ADAM_B2 = 0.999
ADAM_EPS = 1e-08
ADAM_WD = 0.01
ADAM_STEP = 10

VMEM_LIMIT_BYTES = 40 * 1024 * 1024
VMEM_LIMIT_FFN_BYTES = 50 * 1024 * 1024
SUBLANES = 8
PACK_ROWS = 16
PACK_COLS = 1024

TM_FFN = 1024
TM_FFN_BWD = 1024
TM_MIX = 512
TK_WGRAD = 2048
TF_FFN = 256
BM_WGRAD = 1408
ROW_GROUPS = 4
ATTN_BLOCKS = 4
ADAM_ROWS = 256


def _row_tile(rows, limit):
    best = rows
    for t in range(16, min(rows, limit) + 1, 16):
        if rows % t == 0:
            best = t
    return best


def _params(*sem):
    return pltpu.CompilerParams(dimension_semantics=sem, vmem_limit_bytes=VMEM_LIMIT_BYTES)


def _hbm(*arrays):
    return [pltpu.with_memory_space_constraint(a, pltpu.HBM) for a in arrays]


def _dot(a, b):
    return jnp.dot(a, b, preferred_element_type=F32)


def _dot_nt(a, b):
    return lax.dot_general(a, b, (((1,), (1,)), ((), ())), preferred_element_type=F32)


def _dot_tn(a, b):
    return lax.dot_general(a, b, (((0,), (0,)), ((), ())), preferred_element_type=F32)


def _sigmoid(g):
    return 0.5 * jnp.tanh(0.5 * g) + 0.5


def _rms_stats(x):
    inv = lax.rsqrt(jnp.mean(x * x, axis=-1, keepdims=True) + EPS)
    return inv, x * inv


def _rms_bwd(dy, x, gain):
    inv, xhat = _rms_stats(x)
    dgain = jnp.sum(dy * xhat, axis=0, keepdims=True)
    dxh = dy * gain
    dx = inv * (dxh - xhat * jnp.mean(dxh * xhat, axis=-1, keepdims=True))
    return dx, dgain


def _peer_list():
    x, y, c = lax.axis_index("x"), lax.axis_index("y"), lax.axis_index("c")
    peers = []
    for k in range(1, N_DEV):
        px = 1 - x if (k >> 2) & 1 else x
        py = 1 - y if (k >> 1) & 1 else y
        pc = 1 - c if k & 1 else c
        peers.append((px, py, pc))
    return 4 * x + 2 * y + c, peers


def _exchange(arrs, scatter, name, after):
    n = len(arrs)
    out_shape = []
    for a in arrs:
        shp = a.shape if scatter else (N_DEV,) + a.shape
        out_shape.append(jax.ShapeDtypeStruct(shp, a.dtype))

    def body(*refs):
        ins, outs = refs[:n], refs[n + 1:2 * n + 1]
        send_sems, recv_sems, local_sems = refs[2 * n + 1:]
        me, peers = _peer_list()
        started = []
        for a in range(n):
            own = ins[a].at[me] if scatter else ins[a]
            loc = pltpu.make_async_copy(own, outs[a].at[me], local_sems.at[a])
            loc.start()
            started.append(loc)
        sends = []
        for a in range(n):
            for k, (px, py, pc) in enumerate(peers):
                src = ins[a].at[4 * px + 2 * py + pc] if scatter else ins[a]
                cp = pltpu.make_async_remote_copy(
                    src_ref=src, dst_ref=outs[a].at[me],
                    send_sem=send_sems.at[a, k], recv_sem=recv_sems.at[a, k],
                    device_id=(px, py, pc), device_id_type=pl.DeviceIdType.MESH)
                cp.start()
                sends.append(cp)
        for a in range(n):
            for k, (px, py, pc) in enumerate(peers):
                landed = outs[a].at[4 * px + 2 * py + pc]
                pltpu.make_async_remote_copy(
                    src_ref=landed, dst_ref=landed,
                    send_sem=send_sems.at[a, k], recv_sem=recv_sems.at[a, k],
                    device_id=(px, py, pc), device_id_type=pl.DeviceIdType.MESH).wait_recv()
        for cp in sends:
            cp.wait_send()
        for loc in started:
            loc.wait()

    hbm = pl.BlockSpec(memory_space=pl.ANY)
    return pl.pallas_call(
        body, name=name, out_shape=tuple(out_shape),
        in_specs=[hbm] * (n + 1), out_specs=tuple([hbm] * n),
        scratch_shapes=[pltpu.SemaphoreType.DMA((n, N_DEV - 1)),
                        pltpu.SemaphoreType.DMA((n, N_DEV - 1)),
                        pltpu.SemaphoreType.DMA((n,))],
    )(*arrs, after)


def _gather_two_level(arrs, name, after):
    n = len(arrs)
    out_shape = tuple(jax.ShapeDtypeStruct((N_DEV,) + a.shape, a.dtype) for a in arrs)

    def body(*refs):
        ins, outs = refs[:n], refs[n + 1:2 * n + 1]
        send_sems, recv_sems, local_sems = refs[2 * n + 1:]
        x, y, c = lax.axis_index("x"), lax.axis_index("y"), lax.axis_index("c")
        me, sibling = (x, y, c), (x, y, 1 - c)
        chips = [(1 - x, y), (x, 1 - y), (1 - x, 1 - y)]

        def copy(a, k, block, to, src=None):
            slot = outs[a].at[4 * block[0] + 2 * block[1] + block[2]]
            return pltpu.make_async_remote_copy(
                src_ref=slot if src is None else src, dst_ref=slot,
                send_sem=send_sems.at[a, k], recv_sem=recv_sems.at[a, k],
                device_id=to, device_id_type=pl.DeviceIdType.MESH)

        started = []
        for a in range(n):
            loc = pltpu.make_async_copy(ins[a], outs[a].at[4 * x + 2 * y + c], local_sems.at[a])
            loc.start()
            started.append(loc)
        sends = []
        for a in range(n):
            sends.append(copy(a, 0, me, sibling, src=ins[a]))
            sends += [copy(a, 1 + j, me, (*chip, c), src=ins[a]) for j, chip in enumerate(chips)]
        for cp in sends:
            cp.start()
        for j, chip in enumerate(chips):
            for a in range(n):
                copy(a, 1 + j, (*chip, c), me).wait_recv()
                fwd = copy(a, 4 + j, (*chip, c), sibling)
                fwd.start()
                sends.append(fwd)
        for a in range(n):
            copy(a, 0, sibling, me).wait_recv()
            for j, chip in enumerate(chips):
                copy(a, 4 + j, (*chip, 1 - c), me).wait_recv()
        for cp in sends:
            cp.wait_send()
        for loc in started:
            loc.wait()

    hbm = pl.BlockSpec(memory_space=pl.ANY)
    return pl.pallas_call(
        body, name=name, out_shape=out_shape,
        in_specs=[hbm] * (n + 1), out_specs=tuple([hbm] * n),
        scratch_shapes=[pltpu.SemaphoreType.DMA((n, N_DEV - 1)),
                        pltpu.SemaphoreType.DMA((n, N_DEV - 1)),
                        pltpu.SemaphoreType.DMA((n,))],
    )(*arrs, after)


_HBM = pl.BlockSpec(memory_space=pltpu.HBM)
_SEM = pl.BlockSpec(memory_space=pltpu.SEMAPHORE)
_EFFECT = pltpu.SideEffectType.DATAFLOW_SIDE_EFFECTING


def _split_copies(srcs, lands, send_sems, recv_sems, scatter):
    me, peers = _peer_list()
    copies = []
    for a in range(len(srcs)):
        for k, (px, py, pc) in enumerate(peers):
            src = srcs[a].at[4 * px + 2 * py + pc] if scatter else srcs[a]
            copies.append(pltpu.make_async_remote_copy(
                src_ref=src, dst_ref=lands[a].at[me],
                send_sem=send_sems[a].at[k], recv_sem=recv_sems[a].at[k],
                device_id=(px, py, pc), device_id_type=pl.DeviceIdType.MESH))
    return copies


def _exchange_start(arrs, scatter, name, after):
    n = len(arrs)
    me = 4 * lax.axis_index("x") + 2 * lax.axis_index("y") + lax.axis_index("c")
    lands = []
    for a in arrs:
        own = lax.dynamic_index_in_dim(a, me, 0, keepdims=True) if scatter else a[None]
        shp = a.shape if scatter else (N_DEV,) + a.shape
        lands.append(lax.dynamic_update_slice(lax.empty(shp, a.dtype), own, (me,) + (0,) * (len(shp) - 1)))

    def body(*refs):
        srcs, lnds = refs[:n], refs[n:2 * n]
        outs = refs[2 * n + 1:]
        send_sems, recv_sems = outs[:n], outs[n:2 * n]
        token = outs[4 * n]
        for cp in _split_copies(srcs, lnds, send_sems, recv_sems, scatter):
            cp.start()
        token[...] = jnp.zeros_like(token)

    sem = pltpu.SemaphoreType.DMA((N_DEV - 1,))
    out_shape = ([sem] * (2 * n) + [pltpu.HBM(a.shape, a.dtype) for a in arrs]
                 + [pltpu.HBM(l.shape, l.dtype) for l in lands] + [jax.ShapeDtypeStruct((SUBLANES, 128), F32)])
    res = pl.pallas_call(
        body, name=name, out_shape=tuple(out_shape),
        in_specs=[_HBM] * (2 * n) + [pl.BlockSpec(memory_space=pl.ANY)],
        out_specs=tuple([_SEM] * (2 * n) + [_HBM] * (2 * n) + [pl.BlockSpec(memory_space=pltpu.VMEM)]),
        input_output_aliases={i: 2 * n + i for i in range(2 * n)},
        compiler_params=pltpu.CompilerParams(has_side_effects=_EFFECT),
    )(*[pltpu.with_memory_space_constraint(a, pltpu.HBM) for a in arrs],
      *[pltpu.with_memory_space_constraint(l, pltpu.HBM) for l in lands], after)
    handles = [(res[2 * n + a], res[3 * n + a], res[a], res[n + a]) for a in range(n)]
    return handles, res[4 * n]


def _exchange_wait(handles, scatter, name, after):
    n = len(handles)

    def body(*refs):
        srcs, lnds = refs[:n], refs[n:2 * n]
        send_sems, recv_sems = refs[2 * n:3 * n], refs[3 * n:4 * n]
        for cp in _split_copies(srcs, lnds, send_sems, recv_sems, scatter):
            cp.wait_send()
            cp.wait_recv()

    srcs = [h[0] for h in handles]
    lands = [h[1] for h in handles]
    res = pl.pallas_call(
        body, name=name,
        out_shape=tuple([pltpu.HBM(a.shape, a.dtype) for a in srcs] + [pltpu.HBM(l.shape, l.dtype) for l in lands]),
        in_specs=[_HBM] * (2 * n) + [_SEM] * (2 * n) + [pl.BlockSpec(memory_space=pl.ANY)],
        out_specs=tuple([_HBM] * (2 * n)),
        input_output_aliases={i: i for i in range(2 * n)},
        compiler_params=pltpu.CompilerParams(has_side_effects=_EFFECT),
    )(*srcs, *lands, *[h[2] for h in handles], *[h[3] for h in handles], after)
    return list(res[n:])


def _row_groups(tm):
    return [slice(r * (tm // ROW_GROUPS), (r + 1) * (tm // ROW_GROUPS)) for r in range(ROW_GROUPS)]


def _stack_gate_up(wgt, wut, tf):
    F, D = wgt.shape
    return jnp.stack([wgt.reshape(F // tf, tf, D), wut.reshape(F // tf, tf, D)], axis=1).reshape(2 * F, D)


def _ffn_fwd(x, gain, wgu, wd, after, name, tm, tf, head=None):
    T, D = x.shape
    F = wd.shape[0]
    nj = F // tf
    n_in = 7 if head else 5

    def body(*refs):
        x_ref, g_ref, wgu_ref, wd_ref, after_ref = refs[:5]
        outs = refs[n_in:]
        xo_ref, xn_ref, gu_ref = outs[:3]
        xn_sc, acc_sc = outs[-2:]
        i, j = pl.program_id(0), pl.program_id(1)

        @pl.when(j == 0)
        def _():
            _, xhat = _rms_stats(x_ref[...])
            xn = (xhat * g_ref[...]).astype(BF16)
            xn_sc[...] = xn
            xn_ref[...] = xn
            acc_sc[...] = jnp.zeros_like(acc_sc)

        groups = _row_groups(tm)
        gus = [_dot_nt(xn_sc[rows, :], wgu_ref[...]) for rows in groups]
        hs = []
        for rows, gu in zip(groups, gus):
            gu_ref[rows, :] = gu.astype(BF16)
            g, u = gu[:, :tf], gu[:, tf:]
            hs.append((g * _sigmoid(g) * u).astype(BF16))
        for rows, h in zip(groups, hs):
            acc_sc[rows, :] += _dot(h, wd_ref[...])

        if head:
            t_ref, fg_ref = refs[5:7]
            dgain_ref, loss_ref = outs[3:5]

            @pl.when((i == 0) & (j == 0))
            def _():
                dgain_ref[...] = jnp.zeros_like(dgain_ref)
                loss_ref[...] = jnp.zeros_like(loss_ref)

        @pl.when(j == nj - 1)
        def _():
            for rows in groups:
                xo = x_ref[rows, :] + 0.5 * acc_sc[rows, :]
                if head:
                    fg = fg_ref[...]
                    _, xhat = _rms_stats(xo)
                    err = xhat * fg - t_ref[rows, :]
                    loss_ref[...] += 0.5 * jnp.sum(jnp.mean(err * err, axis=-1, keepdims=True),
                                                   axis=0, keepdims=True)
                    dx, dgain = _rms_bwd(err * (1.0 / D), xo, fg)
                    dgain_ref[...] += dgain
                    xo_ref[rows, :] = dx
                else:
                    xo_ref[rows, :] = xo

    tile = pl.BlockSpec((tm, D), lambda i, j: (i, 0))
    const = pl.BlockSpec((1, D), lambda i, j: (0, 0))
    in_specs = [tile, const, pl.BlockSpec((2 * tf, D), lambda i, j: (j, 0)),
                pl.BlockSpec((tf, D), lambda i, j: (j, 0)), pl.BlockSpec(memory_space=pl.ANY)]
    out_specs = [tile, pl.BlockSpec((tm, D), lambda i, j: (i, 0)), pl.BlockSpec((tm, 2 * tf), lambda i, j: (i, j))]
    out_shape = [jax.ShapeDtypeStruct((T, D), F32), jax.ShapeDtypeStruct((T, D), BF16),
                 jax.ShapeDtypeStruct((T, 2 * F), BF16)]
    operands = [x, gain, wgu, wd, after]
    if head:
        in_specs += [tile, const]
        out_specs += [const, pl.BlockSpec((SUBLANES, 128), lambda i, j: (0, 0))]
        out_shape += [jax.ShapeDtypeStruct((1, D), F32), jax.ShapeDtypeStruct((SUBLANES, 128), F32)]
        operands += list(head)
    return pl.pallas_call(
        body, name=name, grid=(T // tm, nj),
        in_specs=in_specs, out_specs=tuple(out_specs), out_shape=tuple(out_shape),
        scratch_shapes=[pltpu.VMEM((tm, D), BF16), pltpu.VMEM((tm, D), F32)],
        compiler_params=pltpu.CompilerParams(dimension_semantics=("arbitrary", "arbitrary"),
                                             vmem_limit_bytes=VMEM_LIMIT_FFN_BYTES),
    )(*_hbm(*operands))


def _ffn_bwd(dy, x, gain, gu, wgu, wd, after, name, tm, tf):
    T, D = x.shape
    F = wd.shape[0]
    nj = F // tf

    def body(dy_ref, x_ref, g_ref, gu_ref, wgu_ref, wd_ref, after_ref,
             dx_ref, dgain_ref, dg_ref, du_ref, hh_ref, do_ref, do_sc, acc_sc, dgu_sc):
        i, j = pl.program_id(0), pl.program_id(1)

        @pl.when((i == 0) & (j == 0))
        def _():
            dgain_ref[...] = jnp.zeros_like(dgain_ref)

        @pl.when(j == 0)
        def _():
            do = (0.5 * dy_ref[...]).astype(BF16)
            do_sc[...] = do
            do_ref[...] = do
            acc_sc[...] = jnp.zeros_like(acc_sc)

        groups = _row_groups(tm)
        dhs = [_dot_nt(do_sc[rows, :], wd_ref[...]) for rows in groups]
        for rows, dh in zip(groups, dhs):
            g = gu_ref[rows, :tf].astype(F32)
            u = gu_ref[rows, tf:].astype(F32)
            sig = _sigmoid(g)
            s = g * sig
            dg = (dh * u * (sig + s * (1.0 - sig))).astype(BF16)
            du = (dh * s).astype(BF16)
            dg_ref[rows, :] = dg
            du_ref[rows, :] = du
            dgu_sc[rows, :tf] = dg
            dgu_sc[rows, tf:] = du
            hh_ref[rows, :] = (s * u).astype(BF16)
        for rows in groups:
            acc_sc[rows, :] += _dot(dgu_sc[rows, :], wgu_ref[...])

        @pl.when(j == nj - 1)
        def _():
            for rows in groups:
                dx, dgain = _rms_bwd(acc_sc[rows, :], x_ref[rows, :], g_ref[...])
                dgain_ref[...] += dgain
                dx_ref[rows, :] = dy_ref[rows, :] + dx

    tile_in = pl.BlockSpec((tm, D), lambda i, j: (i, 0))
    tile = pl.BlockSpec((tm, D), lambda i, j: (i, 0), pipeline_mode=pl.Buffered(1))
    return pl.pallas_call(
        body, name=name, grid=(T // tm, nj),
        in_specs=[tile_in, tile_in,
                  pl.BlockSpec((1, D), lambda i, j: (0, 0)),
                  pl.BlockSpec((tm, 2 * tf), lambda i, j: (i, j)),
                  pl.BlockSpec((2 * tf, D), lambda i, j: (j, 0)),
                  pl.BlockSpec((tf, D), lambda i, j: (j, 0)),
                  pl.BlockSpec(memory_space=pl.ANY)],
        out_specs=(tile,
                   pl.BlockSpec((1, D), lambda i, j: (0, 0)),
                   pl.BlockSpec((tm, tf), lambda i, j: (i, j)),
                   pl.BlockSpec((tm, tf), lambda i, j: (i, j)),
                   pl.BlockSpec((tm, tf), lambda i, j: (i, j)),
                   pl.BlockSpec((tm, D), lambda i, j: (i, 0))),
        out_shape=(jax.ShapeDtypeStruct((T, D), F32), jax.ShapeDtypeStruct((1, D), F32),
                   jax.ShapeDtypeStruct((T, F), BF16), jax.ShapeDtypeStruct((T, F), BF16),
                   jax.ShapeDtypeStruct((T, F), BF16), jax.ShapeDtypeStruct((T, D), BF16)),
        scratch_shapes=[pltpu.VMEM((tm, D), BF16), pltpu.VMEM((tm, D), F32), pltpu.VMEM((tm, 2 * tf), BF16)],
        compiler_params=pltpu.CompilerParams(dimension_semantics=("arbitrary", "arbitrary"),
                                             vmem_limit_bytes=VMEM_LIMIT_FFN_BYTES),
    )(*_hbm(dy, x, gain, gu, wgu, wd, after))


def _tn_grad(a, b, after, name, bm, tk):
    T, M = a.shape
    D = b.shape[1]
    nk = T // tk

    def body(a_ref, b_ref, after_ref, o_ref, acc_sc):
        k = pl.program_id(1)

        @pl.when(k == 0)
        def _():
            acc_sc[...] = jnp.zeros_like(acc_sc)

        acc_sc[...] += _dot_tn(a_ref[...], b_ref[...])

        @pl.when(k == nk - 1)
        def _():
            o_ref[...] = acc_sc[...].astype(BF16)

    return pl.pallas_call(
        body, name=name, grid=(M // bm, nk),
        in_specs=[pl.BlockSpec((tk, bm), lambda i, k: (k, i)), pl.BlockSpec((tk, D), lambda i, k: (k, 0)),
                  pl.BlockSpec(memory_space=pl.ANY)],
        out_specs=pl.BlockSpec((bm, D), lambda i, k: (i, 0)),
        out_shape=jax.ShapeDtypeStruct((M, D), BF16),
        scratch_shapes=[pltpu.VMEM((bm, D), F32)],
        compiler_params=_params("arbitrary", "arbitrary"),
    )(*_hbm(a, b, after))


_Z_SPLITS = (0, 512, 640, 768, 1280, 1792, 2304)


def _mixin_fwd(x, gain, w_in_t, name, tm):
    T, D = x.shape
    widths = [b - a for a, b in zip(_Z_SPLITS[:-1], _Z_SPLITS[1:])]

    def body(x_ref, g_ref, w_ref, hn_ref, *outs):
        _, xhat = _rms_stats(x_ref[...])
        hn = (xhat * g_ref[...]).astype(BF16)
        hn_ref[...] = hn
        for o_ref, lo, hi in zip(outs, _Z_SPLITS[:-1], _Z_SPLITS[1:]):
            o_ref[...] = _dot_nt(hn, w_ref[lo:hi, :]).astype(BF16)

    return pl.pallas_call(
        body, name=name, grid=(T // tm,),
        in_specs=[pl.BlockSpec((tm, D), lambda i: (i, 0)),
                  pl.BlockSpec((1, D), lambda i: (0, 0)),
                  pl.BlockSpec(w_in_t.shape, lambda i: (0, 0))],
        out_specs=tuple([pl.BlockSpec((tm, D), lambda i: (i, 0))]
                        + [pl.BlockSpec((tm, w), lambda i: (i, 0)) for w in widths]),
        out_shape=tuple([jax.ShapeDtypeStruct((T, D), BF16)]
                        + [jax.ShapeDtypeStruct((T, w), BF16) for w in widths]),
        compiler_params=_params("arbitrary"),
    )(*_hbm(x, gain, w_in_t))


def _bucket_table():
    qi = np.arange(BLOCK, dtype=np.int32)[:, None]
    kj = np.arange(2 * BLOCK, dtype=np.int32)[None, :]
    dist = qi + BLOCK - kj
    n = np.maximum(dist, 0)
    max_exact = NUM_BUCKETS // 2
    large = max_exact + (np.log(np.maximum(n, 1).astype(np.float32) / max_exact)
                         / math.log(MAX_DISTANCE / max_exact)
                         * (NUM_BUCKETS - max_exact)).astype(np.int32)
    large = np.minimum(large, NUM_BUCKETS - 1)
    bucket = np.where(n < max_exact, n, large).astype(np.int32)
    valid = (dist >= 0) & (dist < WINDOW)
    return np.where(valid, bucket, -1).astype(np.int32)


def _bias_build(table, bucket, name):
    def body(t_ref, b_ref, o_ref):
        bk = b_ref[...]
        for h in range(N_Q_HEADS):
            def step(b, acc):
                return jnp.where(bk == b, t_ref[b, h], acc)
            o_ref[h] = lax.fori_loop(0, NUM_BUCKETS, step, jnp.full(bk.shape, MASKED, F32))

    return pl.pallas_call(
        body, name=name,
        in_specs=[pl.BlockSpec(memory_space=pltpu.SMEM), pl.BlockSpec(memory_space=pltpu.VMEM)],
        out_specs=pl.BlockSpec(memory_space=pltpu.VMEM),
        out_shape=jax.ShapeDtypeStruct((N_Q_HEADS,) + bucket.shape, F32),
    )(table, bucket)


def _bias_grad(dbias, bucket, name):
    def body(d_ref, b_ref, o_ref):
        bk = b_ref[...]
        row = lax.broadcasted_iota(jnp.int32, o_ref.shape, 0)
        lane = lax.broadcasted_iota(jnp.int32, o_ref.shape, 1)
        res = jnp.zeros(o_ref.shape, F32)
        for h in range(N_Q_HEADS):
            d = d_ref[h]

            def step(b, acc):
                part = jnp.sum(jnp.where(bk == b, d, 0.0), axis=0, keepdims=True)
                return jnp.where(row == b, part, acc)
            per_lane = lax.fori_loop(0, NUM_BUCKETS, step, jnp.zeros(o_ref.shape, F32))
            res = jnp.where(lane == h, jnp.sum(per_lane, axis=1, keepdims=True), res)
        o_ref[...] = res

    return pl.pallas_call(
        body, name=name,
        in_specs=[pl.BlockSpec(memory_space=pltpu.VMEM), pl.BlockSpec(memory_space=pltpu.VMEM)],
        out_specs=pl.BlockSpec(memory_space=pltpu.VMEM),
        out_shape=jax.ShapeDtypeStruct((NUM_BUCKETS, 128), F32),
    )(*_hbm(dbias, bucket))


def _head_cols(h):
    return slice(h * HEAD_DIM, (h + 1) * HEAD_DIM)


def _stack_heads(ref, r0, g, dtype):
    return jnp.concatenate(
        [ref[pl.ds(r0, BLOCK), _head_cols(GQA_GROUP * g + j)].astype(dtype) for j in range(GQA_GROUP)], axis=0)


def _unstack_heads(ref, r0, g, val):
    for j in range(GQA_GROUP):
        ref[pl.ds(r0, BLOCK), _head_cols(GQA_GROUP * g + j)] = val[j * BLOCK:(j + 1) * BLOCK, :]


def _head_lanes(h):
    return slice(h * BLOCK, (h + 1) * BLOCK)


def _group_lanes(g):
    return slice(g * GROUP_ROWS, (g + 1) * GROUP_ROWS)


def _head_softmax(st, bias_t, sink, no_prev):
    s = st * SCALE + bias_t
    row = lax.broadcasted_iota(jnp.int32, s.shape, 0)
    s = jnp.where(no_prev & (row < BLOCK), MASKED, s)
    m = jnp.maximum(jnp.max(s, axis=0, keepdims=True), sink)
    p = jnp.exp(s - m)
    ps = jnp.exp(sink - m)
    r = 1.0 / (jnp.sum(p, axis=0, keepdims=True) + ps)
    return p * r, ps * r


def _load_band(kf_sc, vf_sc, kp_ref, kc_ref, vp_ref, vc_ref, tq):
    kf_sc[0:BLOCK, :] = kp_ref[...]
    kf_sc[BLOCK:BLOCK + tq, :] = kc_ref[...]
    vf_sc[0:BLOCK, :] = vp_ref[...]
    vf_sc[BLOCK:BLOCK + tq, :] = vc_ref[...]


def _attn_fwd(q, k, v, bias_t, sinks, gain, name, nblk):
    T = q.shape[0]
    tq = nblk * BLOCK

    def body(sink_ref, q_ref, kc_ref, kp_ref, vc_ref, vp_ref, bias_ref, g_ref, raw_ref, nrm_ref,
             kf_sc, vf_sc, o_sc, st_sc, pt_sc):
        i = pl.program_id(0)
        _load_band(kf_sc, vf_sc, kp_ref, kc_ref, vp_ref, vc_ref, tq)

        def block(b, carry):
            r0 = pl.multiple_of(b * BLOCK, BLOCK)
            no_prev = (i == 0) & (b == 0)
            for g in range(N_KV_HEADS):
                kb = kf_sc[pl.ds(r0, 2 * BLOCK), _head_cols(g)]
                st_sc[:, _group_lanes(g)] = _dot_nt(kb, _stack_heads(q_ref, r0, g, BF16))
            for h in range(N_Q_HEADS):
                p, _ = _head_softmax(st_sc[:, _head_lanes(h)], bias_ref[h], sink_ref[h], no_prev)
                pt_sc[:, _head_lanes(h)] = p.astype(BF16)
            for g in range(N_KV_HEADS):
                vb = vf_sc[pl.ds(r0, 2 * BLOCK), _head_cols(g)]
                _unstack_heads(o_sc, r0, g, _dot_tn(pt_sc[:, _group_lanes(g)], vb))
            return carry

        lax.fori_loop(0, nblk, block, 0, unroll=True)
        o = o_sc[...]
        raw_ref[...] = o.astype(BF16)
        _, ohat = _rms_stats(o)
        nrm_ref[...] = (ohat * g_ref[...]).astype(BF16)

    cur = lambda i: (i, 0)
    prev = lambda i: (jnp.maximum(i * nblk - 1, 0), 0)
    lanes = N_Q_HEADS * BLOCK
    return pl.pallas_call(
        body, name=name, grid=(T // tq,),
        in_specs=[pl.BlockSpec(memory_space=pltpu.SMEM),
                  pl.BlockSpec((tq, ATTN_WIDTH), cur),
                  pl.BlockSpec((tq, KV_WIDTH), cur), pl.BlockSpec((BLOCK, KV_WIDTH), prev),
                  pl.BlockSpec((tq, KV_WIDTH), cur), pl.BlockSpec((BLOCK, KV_WIDTH), prev),
                  pl.BlockSpec(bias_t.shape, lambda i: (0, 0, 0)),
                  pl.BlockSpec((1, ATTN_WIDTH), lambda i: (0, 0))],
        out_specs=(pl.BlockSpec((tq, ATTN_WIDTH), cur), pl.BlockSpec((tq, ATTN_WIDTH), cur)),
        out_shape=(jax.ShapeDtypeStruct((T, ATTN_WIDTH), BF16), jax.ShapeDtypeStruct((T, ATTN_WIDTH), BF16)),
        scratch_shapes=[pltpu.VMEM((tq + BLOCK, KV_WIDTH), BF16), pltpu.VMEM((tq + BLOCK, KV_WIDTH), BF16),
                        pltpu.VMEM((tq, ATTN_WIDTH), F32),
                        pltpu.VMEM((2 * BLOCK, lanes), F32), pltpu.VMEM((2 * BLOCK, lanes), BF16)],
        compiler_params=_params("arbitrary"),
    )(sinks, *_hbm(q, k, k, v, v, bias_t, gain))


def _attn_bwd(dmixed, raw, q, k, v, bias_t, sinks, gain, name, nblk):
    T = q.shape[0]
    tq = nblk * BLOCK
    nt = T // tq
    lanes = N_Q_HEADS * BLOCK

    def body(sink_ref, dm_ref, raw_ref, q_ref, kc_ref, kp_ref, vc_ref, vp_ref, bias_ref, g_ref,
             dq_ref, dk_ref, dv_ref, dbias_ref, dsink_ref, dgain_ref,
             do_sc, dq_sc, kf_sc, vf_sc, dkf_sc, dvf_sc, st_sc, dpt_sc, pt_sc, dst_sc, drow_sc,
             qs_sc, dos_sc, dsink_sc):
        i = pl.program_id(0)
        tile = nt - 1 - i

        @pl.when(i == 0)
        def _():
            dkf_sc[...] = jnp.zeros_like(dkf_sc)
            dvf_sc[...] = jnp.zeros_like(dvf_sc)
            dsink_sc[...] = jnp.zeros_like(dsink_sc)
            dbias_ref[...] = jnp.zeros_like(dbias_ref)
            dgain_ref[...] = jnp.zeros_like(dgain_ref)

        carry_k = dkf_sc[0:BLOCK, :]
        carry_v = dvf_sc[0:BLOCK, :]
        dkf_sc[0:tq, :] = jnp.zeros((tq, KV_WIDTH), F32)
        dvf_sc[0:tq, :] = jnp.zeros((tq, KV_WIDTH), F32)
        dkf_sc[tq:tq + BLOCK, :] = carry_k
        dvf_sc[tq:tq + BLOCK, :] = carry_v
        _load_band(kf_sc, vf_sc, kp_ref, kc_ref, vp_ref, vc_ref, tq)

        do, dgain = _rms_bwd(dm_ref[...].astype(F32), raw_ref[...].astype(F32), g_ref[...])
        dgain_ref[...] += dgain
        do_sc[...] = do
        ones = jnp.ones((SUBLANES, HEAD_DIM), BF16)

        def block(b, carry):
            r0 = pl.multiple_of(b * BLOCK, BLOCK)
            no_prev = (tile == 0) & (b == 0)
            for g in range(N_KV_HEADS):
                kb = kf_sc[pl.ds(r0, 2 * BLOCK), _head_cols(g)]
                vb = vf_sc[pl.ds(r0, 2 * BLOCK), _head_cols(g)]
                qg = _stack_heads(q_ref, r0, g, BF16)
                dog = _stack_heads(do_sc, r0, g, F32)
                prod = dog * _stack_heads(raw_ref, r0, g, F32)
                hi = prod.astype(BF16)
                lo = (prod - hi.astype(F32)).astype(BF16)
                drow_sc[:, _group_lanes(g)] = _dot_nt(ones, hi) + _dot_nt(ones, lo)
                dogb = dog.astype(BF16)
                qs_sc[g] = qg
                dos_sc[g] = dogb
                st_sc[:, _group_lanes(g)] = _dot_nt(kb, qg)
                dpt_sc[:, _group_lanes(g)] = _dot_nt(vb, dogb)
            for h in range(N_Q_HEADS):
                hl = _head_lanes(h)
                p, ps = _head_softmax(st_sc[:, hl], bias_ref[h], sink_ref[h], no_prev)
                rowdot = drow_sc[0:1, hl]
                ds = p * (dpt_sc[:, hl] - rowdot)
                dsink_sc[h:h + 1, :] += -(ps * rowdot)
                dbias_ref[h] += ds
                dst_sc[:, hl] = ds.astype(BF16)
                pt_sc[:, hl] = p.astype(BF16)
            for g in range(N_KV_HEADS):
                kb = kf_sc[pl.ds(r0, 2 * BLOCK), _head_cols(g)]
                dsg = dst_sc[:, _group_lanes(g)]
                _unstack_heads(dq_sc, r0, g, _dot_tn(dsg, kb) * SCALE)
                dkf_sc[pl.ds(r0, 2 * BLOCK), _head_cols(g)] += _dot(dsg, qs_sc[g]) * SCALE
                dvf_sc[pl.ds(r0, 2 * BLOCK), _head_cols(g)] += _dot(pt_sc[:, _group_lanes(g)], dos_sc[g])
            return carry

        lax.fori_loop(0, nblk, block, 0, unroll=True)
        dq_ref[...] = dq_sc[...].astype(BF16)
        dk_ref[...] = dkf_sc[BLOCK:BLOCK + tq, :].astype(BF16)
        dv_ref[...] = dvf_sc[BLOCK:BLOCK + tq, :].astype(BF16)

        @pl.when(i == nt - 1)
        def _():
            tot = jnp.sum(dsink_sc[...], axis=1, keepdims=True)
            dsink_ref[...] = jnp.broadcast_to(tot, dsink_ref.shape)

    cur = lambda i: (nt - 1 - i, 0)
    prev = lambda i: (jnp.maximum((nt - 1 - i) * nblk - 1, 0), 0)
    const2 = lambda i: (0, 0)
    const3 = lambda i: (0, 0, 0)
    return pl.pallas_call(
        body, name=name, grid=(nt,),
        in_specs=[pl.BlockSpec(memory_space=pltpu.SMEM),
                  pl.BlockSpec((tq, ATTN_WIDTH), cur),
                  pl.BlockSpec((tq, ATTN_WIDTH), cur),
                  pl.BlockSpec((tq, ATTN_WIDTH), cur),
                  pl.BlockSpec((tq, KV_WIDTH), cur), pl.BlockSpec((BLOCK, KV_WIDTH), prev),
                  pl.BlockSpec((tq, KV_WIDTH), cur), pl.BlockSpec((BLOCK, KV_WIDTH), prev),
                  pl.BlockSpec(bias_t.shape, const3),
                  pl.BlockSpec((1, ATTN_WIDTH), const2)],
        out_specs=(pl.BlockSpec((tq, ATTN_WIDTH), cur),
                   pl.BlockSpec((tq, KV_WIDTH), cur), pl.BlockSpec((tq, KV_WIDTH), cur),
                   pl.BlockSpec(bias_t.shape, const3),
                   pl.BlockSpec((N_Q_HEADS, 128), const2),
                   pl.BlockSpec((1, ATTN_WIDTH), const2)),
        out_shape=(jax.ShapeDtypeStruct((T, ATTN_WIDTH), BF16),
                   jax.ShapeDtypeStruct((T, KV_WIDTH), BF16), jax.ShapeDtypeStruct((T, KV_WIDTH), BF16),
                   jax.ShapeDtypeStruct(bias_t.shape, F32),
                   jax.ShapeDtypeStruct((N_Q_HEADS, 128), F32),
                   jax.ShapeDtypeStruct((1, ATTN_WIDTH), F32)),
        scratch_shapes=[pltpu.VMEM((tq, ATTN_WIDTH), F32), pltpu.VMEM((tq, ATTN_WIDTH), F32),
                        pltpu.VMEM((tq + BLOCK, KV_WIDTH), BF16), pltpu.VMEM((tq + BLOCK, KV_WIDTH), BF16),
                        pltpu.VMEM((tq + BLOCK, KV_WIDTH), F32), pltpu.VMEM((tq + BLOCK, KV_WIDTH), F32),
                        pltpu.VMEM((2 * BLOCK, lanes), F32), pltpu.VMEM((2 * BLOCK, lanes), F32),
                        pltpu.VMEM((2 * BLOCK, lanes), BF16), pltpu.VMEM((2 * BLOCK, lanes), BF16),
                        pltpu.VMEM((SUBLANES, lanes), F32),
                        pltpu.VMEM((N_KV_HEADS, GROUP_ROWS, HEAD_DIM), BF16),
                        pltpu.VMEM((N_KV_HEADS, GROUP_ROWS, HEAD_DIM), BF16),
                        pltpu.VMEM((N_Q_HEADS, 128), F32)],
        compiler_params=_params("arbitrary"),
    )(sinks, *_hbm(dmixed, raw, q, k, k, v, v, bias_t, gain))


def _shift_down(cu, tail):
    row = lax.broadcasted_iota(jnp.int32, cu.shape, 0)
    t6, t7 = tail[6:7, :], tail[7:8, :]
    s1 = jnp.where(row == 0, t7, pltpu.roll(cu, 1, 0))
    s2 = jnp.where(row == 0, t6, jnp.where(row == 1, t7, pltpu.roll(cu, 2, 0)))
    return s1, s2


def _shift_up(d, head):
    n = d.shape[0]
    row = lax.broadcasted_iota(jnp.int32, d.shape, 0)
    h0, h1 = head[0:1, :], head[1:2, :]
    s1 = jnp.where(row == n - 1, h0, pltpu.roll(d, n - 1, 0))
    s2 = jnp.where(row == n - 1, h1, jnp.where(row == n - 2, h0, pltpu.roll(d, n - 2, 0)))
    return s1, s2


def _mixout_fwd(x, attn_n, u, gb, gc, conv_w, gain, w_out, name, tm):
    T, D = x.shape

    def body(x_ref, an_ref, u_ref, b_ref, c_ref, cw_ref, g_ref, wo_ref, xo_ref, cn_ref, tail_sc):
        @pl.when(pl.program_id(0) == 0)
        def _():
            tail_sc[...] = jnp.zeros_like(tail_sc)

        cu = c_ref[...].astype(F32) * u_ref[...].astype(F32)
        s1, s2 = _shift_down(cu, tail_sc[...])
        tail_sc[...] = cu[tm - SUBLANES:tm, :]
        pre = cw_ref[0:1, :] * s2 + cw_ref[1:2, :] * s1 + cw_ref[2:3, :] * cu
        conv = b_ref[...].astype(F32) * pre
        _, chat = _rms_stats(conv)
        cn = (chat * g_ref[...]).astype(BF16)
        cn_ref[...] = cn
        xo_ref[...] = (x_ref[...] + _dot(an_ref[...], wo_ref[0:ATTN_WIDTH, :])
                       + _dot(cn, wo_ref[ATTN_WIDTH:ATTN_WIDTH + CONV_DIM, :]))

    row = lambda i: (i, 0)
    const = lambda i: (0, 0)
    return pl.pallas_call(
        body, name=name, grid=(T // tm,),
        in_specs=[pl.BlockSpec((tm, D), row), pl.BlockSpec((tm, ATTN_WIDTH), row),
                  pl.BlockSpec((tm, CONV_DIM), row), pl.BlockSpec((tm, CONV_DIM), row),
                  pl.BlockSpec((tm, CONV_DIM), row),
                  pl.BlockSpec(conv_w.shape, const), pl.BlockSpec((1, CONV_DIM), const),
                  pl.BlockSpec(w_out.shape, const)],
        out_specs=(pl.BlockSpec((tm, D), row), pl.BlockSpec((tm, CONV_DIM), row)),
        out_shape=(jax.ShapeDtypeStruct((T, D), F32), jax.ShapeDtypeStruct((T, CONV_DIM), BF16)),
        scratch_shapes=[pltpu.VMEM((SUBLANES, CONV_DIM), F32)],
        compiler_params=_params("arbitrary"),
    )(*_hbm(x, attn_n, u, gb, gc, conv_w, gain, w_out))


def _mixout_bwd(dy, attn_n, conv_n, w_out, after, name, tm):
    T, D = dy.shape
    W = ATTN_WIDTH + CONV_DIM
    nt = T // tm

    def body(dy_ref, an_ref, cn_ref, w_ref, after_ref, dm_ref, dw_ref, dw_sc):
        i = pl.program_id(0)

        @pl.when(i == 0)
        def _():
            dw_sc[...] = jnp.zeros_like(dw_sc)

        dyb = dy_ref[...].astype(BF16)
        dm_ref[...] = _dot_nt(dyb, w_ref[...]).astype(BF16)
        dw_sc[0:ATTN_WIDTH, :] += _dot_tn(an_ref[...], dyb)
        dw_sc[ATTN_WIDTH:W, :] += _dot_tn(cn_ref[...], dyb)

        @pl.when(i == nt - 1)
        def _():
            dw_ref[...] = dw_sc[...].astype(BF16)

    row = lambda i: (i, 0)
    const = lambda i: (0, 0)
    return pl.pallas_call(
        body, name=name, grid=(nt,),
        in_specs=[pl.BlockSpec((tm, D), row), pl.BlockSpec((tm, ATTN_WIDTH), row),
                  pl.BlockSpec((tm, CONV_DIM), row), pl.BlockSpec(w_out.shape, const),
                  pl.BlockSpec(memory_space=pl.ANY)],
        out_specs=(pl.BlockSpec((tm, W), row), pl.BlockSpec((W, D), const)),
        out_shape=(jax.ShapeDtypeStruct((T, W), BF16), jax.ShapeDtypeStruct((W, D), BF16)),
        scratch_shapes=[pltpu.VMEM((W, D), F32)],
        compiler_params=_params("arbitrary"),
    )(*_hbm(dy, attn_n, conv_n, w_out, after))


def _conv_bwd(dmixed, u, gb, gc, conv_w, gain, name, tc):
    T = u.shape[0]
    nt = T // tc
    per8 = tc // SUBLANES

    def body(dm_ref, u_ref, b_ref, c_ref, ut_ref, ct_ref, cw_ref, g_ref,
             du_ref, db_ref, dc_ref, dcw_ref, dgain_ref, head_sc):
        i = pl.program_id(0)

        @pl.when(i == 0)
        def _():
            head_sc[...] = jnp.zeros_like(head_sc)
            dcw_ref[...] = jnp.zeros_like(dcw_ref)
            dgain_ref[...] = jnp.zeros_like(dgain_ref)

        uu = u_ref[...].astype(F32)
        cc = c_ref[...].astype(F32)
        bb = b_ref[...].astype(F32)
        cu = cc * uu
        tail = jnp.where(i == nt - 1, 0.0, ct_ref[...].astype(F32) * ut_ref[...].astype(F32))
        s1, s2 = _shift_down(cu, tail)
        w0, w1, w2 = cw_ref[0:1, :], cw_ref[1:2, :], cw_ref[2:3, :]
        pre = w0 * s2 + w1 * s1 + w2 * cu
        dconv, dgain = _rms_bwd(dm_ref[...].astype(F32), bb * pre, g_ref[...])
        dgain_ref[...] += dgain
        db_ref[...] = (dconv * pre).astype(BF16)
        dpre = dconv * bb
        dcw_ref[0:1, :] += jnp.sum(dpre * s2, axis=0, keepdims=True)
        dcw_ref[1:2, :] += jnp.sum(dpre * s1, axis=0, keepdims=True)
        dcw_ref[2:3, :] += jnp.sum(dpre * cu, axis=0, keepdims=True)
        n1, n2 = _shift_up(dpre, head_sc[...])
        head_sc[...] = dpre[0:SUBLANES, :]
        dcu = w2 * dpre + w1 * n1 + w0 * n2
        du_ref[...] = (dcu * cc).astype(BF16)
        dc_ref[...] = (dcu * uu).astype(BF16)

    rev = lambda i: (nt - 1 - i, 0)
    rev_right = lambda i: (nt - 1 - i, 1)
    tail_map = lambda i: (jnp.maximum((nt - 1 - i) * per8 - 1, 0), 0)
    const = lambda i: (0, 0)
    return pl.pallas_call(
        body, name=name, grid=(nt,),
        in_specs=[pl.BlockSpec((tc, CONV_DIM), rev_right),
                  pl.BlockSpec((tc, CONV_DIM), rev), pl.BlockSpec((tc, CONV_DIM), rev),
                  pl.BlockSpec((tc, CONV_DIM), rev),
                  pl.BlockSpec((SUBLANES, CONV_DIM), tail_map), pl.BlockSpec((SUBLANES, CONV_DIM), tail_map),
                  pl.BlockSpec(conv_w.shape, const), pl.BlockSpec((1, CONV_DIM), const)],
        out_specs=(pl.BlockSpec((tc, CONV_DIM), rev), pl.BlockSpec((tc, CONV_DIM), rev),
                   pl.BlockSpec((tc, CONV_DIM), rev),
                   pl.BlockSpec((SUBLANES, CONV_DIM), const), pl.BlockSpec((1, CONV_DIM), const)),
        out_shape=(jax.ShapeDtypeStruct((T, CONV_DIM), BF16), jax.ShapeDtypeStruct((T, CONV_DIM), BF16),
                   jax.ShapeDtypeStruct((T, CONV_DIM), BF16),
                   jax.ShapeDtypeStruct((SUBLANES, CONV_DIM), F32), jax.ShapeDtypeStruct((1, CONV_DIM), F32)),
        scratch_shapes=[pltpu.VMEM((SUBLANES, CONV_DIM), F32)],
        compiler_params=_params("arbitrary"),
    )(*_hbm(dmixed, u, gb, gc, u, gc, conv_w, gain))


def _mixin_bwd(dy, x, gain, dz, w_in_t, name, tm):
    T, D = x.shape
    nz = len(dz)

    def body(dy_ref, x_ref, g_ref, *rest):
        dz_refs, wt_ref, dx_ref, dgain_ref = rest[:nz], rest[nz], rest[nz + 1], rest[nz + 2]

        @pl.when(pl.program_id(0) == 0)
        def _():
            dgain_ref[...] = jnp.zeros_like(dgain_ref)

        dh = jnp.zeros((tm, D), F32)
        for r, lo, hi in zip(dz_refs, _Z_SPLITS[:-1], _Z_SPLITS[1:]):
            dh += _dot(r[...], wt_ref[lo:hi, :])
        dx, dgain = _rms_bwd(dh, x_ref[...], g_ref[...])
        dgain_ref[...] += dgain
        dx_ref[...] = dy_ref[...] + dx

    row = lambda i: (i, 0)
    const = lambda i: (0, 0)
    return pl.pallas_call(
        body, name=name, grid=(T // tm,),
        in_specs=[pl.BlockSpec((tm, D), row), pl.BlockSpec((tm, D), row), pl.BlockSpec((1, D), const)]
                 + [pl.BlockSpec((tm, a.shape[1]), row) for a in dz]
                 + [pl.BlockSpec(w_in_t.shape, const)],
        out_specs=(pl.BlockSpec((tm, D), row), pl.BlockSpec((1, D), const)),
        out_shape=(jax.ShapeDtypeStruct((T, D), F32), jax.ShapeDtypeStruct((1, D), F32)),
        compiler_params=_params("arbitrary"),
    )(*_hbm(dy, x, gain, *dz, w_in_t))


def _win_grad(dz, hn, name, tk):
    T, D = hn.shape
    nz = len(dz)
    nt = T // tk
    W = _Z_SPLITS[-1]

    def body(hn_ref, *rest):
        dz_refs, dw_ref, dw_sc = rest[:nz], rest[nz], rest[nz + 1]
        i = pl.program_id(0)

        @pl.when(i == 0)
        def _():
            dw_sc[...] = jnp.zeros_like(dw_sc)

        hn = hn_ref[...]
        for r, lo, hi in zip(dz_refs, _Z_SPLITS[:-1], _Z_SPLITS[1:]):
            dw_sc[lo:hi, :] += _dot_tn(r[...], hn)

        @pl.when(i == nt - 1)
        def _():
            dw_ref[...] = dw_sc[...].astype(BF16)

    row = lambda i: (i, 0)
    return pl.pallas_call(
        body, name=name, grid=(nt,),
        in_specs=[pl.BlockSpec((tk, D), row)] + [pl.BlockSpec((tk, a.shape[1]), row) for a in dz],
        out_specs=pl.BlockSpec((W, D), lambda i: (0, 0)),
        out_shape=jax.ShapeDtypeStruct((W, D), BF16),
        scratch_shapes=[pltpu.VMEM((W, D), F32)],
        compiler_params=_params("arbitrary"),
    )(*_hbm(hn, *dz))


def _adamw(parts, w, m, v, name, tr):
    P = parts.shape[0]
    R, C = w.shape

    def body(p_ref, w_ref, m_ref, v_ref, g_ref, d_ref, nm_ref, nv_ref):
        g = p_ref[0].astype(F32)
        for d in range(1, P):
            g = g + p_ref[d].astype(F32)
        nm = ADAM_B1 * m_ref[...] + (1.0 - ADAM_B1) * g
        nv = ADAM_B2 * v_ref[...] + (1.0 - ADAM_B2) * (g * g)
        m_hat = nm / (1.0 - ADAM_B1 ** ADAM_STEP)
        v_hat = nv / (1.0 - ADAM_B2 ** ADAM_STEP)
        g_ref[...] = g
        nm_ref[...] = nm
        nv_ref[...] = nv
        d_ref[...] = -ADAM_LR * (m_hat / (jnp.sqrt(v_hat) + ADAM_EPS) + ADAM_WD * w_ref[...])

    row = lambda i: (i, 0)
    spec = pl.BlockSpec((tr, C), row)
    shp = jax.ShapeDtypeStruct((R, C), F32)
    return pl.pallas_call(
        body, name=name, grid=(R // tr,),
        in_specs=[pl.BlockSpec((P, tr, C), lambda i: (0, i, 0)), spec, spec, spec],
        out_specs=(spec, spec, spec, spec),
        out_shape=(shp, shp, shp, shp),
        compiler_params=_params("arbitrary"),
    )(*_hbm(parts, w, m, v))


def _columns_of_blocks(g):
    n, R, w = g.shape
    return g.transpose(1, 0, 2).reshape(R, n * w)


def _pad_row(vec):
    vec = vec.reshape(1, -1)
    return jnp.pad(vec, ((0, 0), (0, PACK_COLS - vec.shape[1])))


def kernel(x, rel_bias_table, ffn1_norm, ffn1_w_gate, ffn1_w_up, ffn1_w_down, mix_norm, w_in, conv_w, attn_sinks, attn_out_norm, conv_out_norm, w_out, ffn2_norm, ffn2_w_gate, ffn2_w_up, ffn2_w_down, final_norm, loss_target, m_rel_bias_table, m_ffn1_norm, m_ffn1_w_gate, m_ffn1_w_up, m_ffn1_w_down, m_mix_norm, m_w_in, m_conv_w, m_attn_sinks, m_attn_out_norm, m_conv_out_norm, m_w_out, m_ffn2_norm, m_ffn2_w_gate, m_ffn2_w_up, m_ffn2_w_down, m_final_norm, v_rel_bias_table, v_ffn1_norm, v_ffn1_w_gate, v_ffn1_w_up, v_ffn1_w_down, v_mix_norm, v_w_in, v_conv_w, v_attn_sinks, v_attn_out_norm, v_conv_out_norm, v_w_out, v_ffn2_norm, v_ffn2_w_gate, v_ffn2_w_up, v_ffn2_w_down, v_final_norm):
    T, D = x.shape[1], x.shape[2]
    x0 = x[0]
    target = loss_target[0]
    tm = min(TM_FFN, T)
    tm_bwd = min(TM_FFN_BWD, T)
    tm_mix = min(TM_MIX, T)
    tk = min(TK_WGRAD, T)
    tf = TF_FFN
    nblk = min(ATTN_BLOCKS, T // BLOCK)
    me = 4 * lax.axis_index("x") + 2 * lax.axis_index("y") + lax.axis_index("c")

    big = {
        "ffn1_w_gate": (ffn1_w_gate[0], m_ffn1_w_gate[0], v_ffn1_w_gate[0], True),
        "ffn1_w_up": (ffn1_w_up[0], m_ffn1_w_up[0], v_ffn1_w_up[0], True),
        "ffn1_w_down": (ffn1_w_down[0], m_ffn1_w_down[0], v_ffn1_w_down[0], False),
        "w_in": (w_in[0], m_w_in[0], v_w_in[0], True),
        "w_out": (w_out[0], m_w_out[0], v_w_out[0], False),
        "ffn2_w_gate": (ffn2_w_gate[0], m_ffn2_w_gate[0], v_ffn2_w_gate[0], True),
        "ffn2_w_up": (ffn2_w_up[0], m_ffn2_w_up[0], v_ffn2_w_up[0], True),
        "ffn2_w_down": (ffn2_w_down[0], m_ffn2_w_down[0], v_ffn2_w_down[0], False),
    }

    def block_to_send(name):
        w, _, _, transposed = big[name]
        return (w.T if transposed else w).astype(BF16)

    names1 = ["ffn1_w_gate", "ffn1_w_up", "ffn1_w_down"]
    names_rest = ["w_in", "w_out", "ffn2_w_gate", "ffn2_w_up", "ffn2_w_down"]
    first = _gather_two_level([block_to_send(n) for n in names1], "gather_ffn1", ffn1_norm)
    wgt1, wut1, wd1 = [g.reshape(-1, D) for g in first]
    h_rest, token = _exchange_start([block_to_send(n) for n in names_rest[:2]] + [conv_w[0]]
                                    + [block_to_send(n) for n in names_rest[2:]], False,
                                    "gather_start_rest", wd1)

    wgu1 = _stack_gate_up(wgt1, wut1, tf)
    x1, xn1, gu1 = _ffn_fwd(x0, ffn1_norm, wgu1, wd1, token, "ffn1_fwd", tm, tf)
    mixw = _exchange_wait(h_rest[:3], False, "gather_wait_mix", x1)
    win_t = mixw[0].reshape(-1, D)
    wout = mixw[1].reshape(-1, D)
    cw = _columns_of_blocks(mixw[2])
    hn, q, k, v, u, gb, gc = _mixin_fwd(x1, mix_norm, win_t, "mixin_fwd", tm_mix)
    bucket = jnp.asarray(_bucket_table().T.copy())
    sinks = attn_sinks.reshape(-1)
    bias_t = _bias_build(rel_bias_table, bucket, "bias_build")
    attn_raw, attn_n = _attn_fwd(q, k, v, bias_t, sinks, attn_out_norm, "attn_fwd", nblk)
    x2, conv_n = _mixout_fwd(x1, attn_n, u, gb, gc, cw, conv_out_norm, wout, "mixout_fwd", tm_mix)
    wgt2, wut2, wd2 = [g.reshape(-1, D) for g in _exchange_wait(h_rest[3:], False, "gather_wait_ffn2", x2)]
    wgu2 = _stack_gate_up(wgt2, wut2, tf)
    dx3, xn2, gu2, d_final, loss_part = _ffn_fwd(x2, ffn2_norm, wgu2, wd2, x2, "ffn2_fwd", tm, tf,
                                                 head=(target, final_norm.reshape(1, D)))

    def blocks(g):
        return g.reshape(N_DEV, -1, D)

    dx2, d_ffn2_norm, dg2, du2, hh2, do2 = _ffn_bwd(
        dx3, x2, ffn2_norm, gu2, wgu2, wd2, dx3, "ffn2_bwd", tm_bwd, tf)
    d_wg2 = _tn_grad(dg2, xn2, dx2, "ffn2_wgrad_gate", BM_WGRAD, tk)
    d_wu2 = _tn_grad(du2, xn2, dx2, "ffn2_wgrad_up", BM_WGRAD, tk)
    d_wd2 = _tn_grad(hh2, do2, dx2, "ffn2_wgrad_down", BM_WGRAD, tk)
    handles2, token2 = _exchange_start([blocks(d_wg2), blocks(d_wu2), blocks(d_wd2)], True,
                                       "grads_start_ffn2", dx2)

    dmixed, d_wout = _mixout_bwd(dx2, attn_n, conv_n, wout, token2, "mixout_bwd", tm_mix)
    dq, dk, dv, dbias, dsink, d_attn_norm = _attn_bwd(
        dmixed, attn_raw, q, k, v, bias_t, sinks, attn_out_norm, "attn_bwd", nblk)
    du, dgb, dgc, d_cw, d_conv_norm = _conv_bwd(dmixed, u, gb, gc, cw, conv_out_norm, "conv_bwd", tm_mix)
    d_table = _bias_grad(dbias, bucket, "bias_grad")
    dz = [dq, dk, dv, du, dgb, dgc]
    dx1, d_mix_norm = _mixin_bwd(dx2, x1, mix_norm, dz, win_t, "mixin_bwd", tm_mix)
    d_win_t = _win_grad(dz, hn, "win_grad", min(TM_MIX, T))
    handles_mix, token_mix = _exchange_start([blocks(d_win_t), blocks(d_wout)], True, "grads_start_mix", d_table)

    dx0, d_ffn1_norm, dg1, du1, hh1, do1 = _ffn_bwd(
        dx1, x0, ffn1_norm, gu1, wgu1, wd1, token_mix, "ffn1_bwd", tm_bwd, tf)
    def pack(ffn1, mixn, ffn2, fin, attn_n_, conv_n_, sink_, extra, convw, table):
        rows = [_pad_row(ffn1), _pad_row(mixn), _pad_row(ffn2), _pad_row(fin),
                _pad_row(jnp.concatenate([attn_n_.reshape(-1), conv_n_.reshape(-1)])),
                _pad_row(sink_), _pad_row(extra),
                jnp.zeros((1, PACK_COLS), F32),
                jnp.pad(convw, ((0, 0), (0, PACK_COLS - convw.shape[1]))),
                _pad_row(table),
                jnp.zeros((PACK_ROWS - 12, PACK_COLS), F32)]
        return jnp.concatenate(rows, axis=0)

    def own_channels(a):
        full = jnp.zeros((a.shape[1], CONV_DIM), F32)
        return lax.dynamic_update_slice(full, a[0], (0, me * a.shape[2]))

    g_pack = pack(d_ffn1_norm, d_mix_norm, d_ffn2_norm, d_final, d_attn_norm, d_conv_norm,
                  dsink[:, 0], loss_part[0, :1], d_cw[:3], d_table[:, :N_Q_HEADS])
    zero1 = jnp.zeros((1,), F32)
    w_pack = pack(ffn1_norm, mix_norm, ffn2_norm, final_norm, attn_out_norm, conv_out_norm,
                  attn_sinks, zero1, own_channels(conv_w), rel_bias_table)
    m_pack = pack(m_ffn1_norm, m_mix_norm, m_ffn2_norm, m_final_norm, m_attn_out_norm, m_conv_out_norm,
                  m_attn_sinks, zero1, own_channels(m_conv_w), m_rel_bias_table)
    v_pack = pack(v_ffn1_norm, v_mix_norm, v_ffn2_norm, v_final_norm, v_attn_out_norm, v_conv_out_norm,
                  v_attn_sinks, zero1, own_channels(v_conv_w), v_rel_bias_table)
    (g_all,) = _exchange([g_pack], False, "gather_small", dx0)
    packs = _adamw(g_all, w_pack, m_pack, v_pack, "adamw_small", PACK_ROWS)

    d_wd1 = _tn_grad(hh1, do1, packs[0], "ffn1_wgrad_down", BM_WGRAD, tk)
    handles1_d, token1 = _exchange_start([blocks(d_wd1)], True, "grads_start_ffn1_down", dx0)
    d_wg1 = _tn_grad(dg1, xn1, token1, "ffn1_wgrad_gate", BM_WGRAD, tk)
    handles1_g, token1 = _exchange_start([blocks(d_wg1)], True, "grads_start_ffn1_gate", token1)
    d_wu1 = _tn_grad(du1, xn1, token1, "ffn1_wgrad_up", BM_WGRAD, tk)
    handles1_u, token1 = _exchange_start([blocks(d_wu1)], True, "grads_start_ffn1_up", token1)

    res = {}

    def update(names, parts):
        last = None
        for name, p in zip(names, parts):
            w, m_, v_, transposed = big[name]
            if transposed:
                w, m_, v_ = w.T, m_.T, v_.T
            new = _adamw(p, w, m_, v_, "adamw_" + name, _row_tile(w.shape[0], ADAM_ROWS))
            res[name] = tuple((a.T if transposed else a)[None] for a in new)
            last = new[0]
        return last

    parts2 = _exchange_wait(handles2, True, "grads_wait_ffn2", token1)
    done2 = update(["ffn2_w_gate", "ffn2_w_up", "ffn2_w_down"], parts2)
    parts_mix = _exchange_wait(handles_mix, True, "grads_wait_mix", done2)
    done_mix = update(["w_in", "w_out"], parts_mix)

    parts1_d = _exchange_wait(handles1_d, True, "grads_wait_ffn1_down", done_mix)
    done1 = update(["ffn1_w_down"], parts1_d)
    parts1_g = _exchange_wait(handles1_g, True, "grads_wait_ffn1_gate", done1)
    done1 = update(["ffn1_w_gate"], parts1_g)
    parts1_u = _exchange_wait(handles1_u, True, "grads_wait_ffn1_up", done1)
    update(["ffn1_w_up"], parts1_u)

    def unpack(pk):
        cwb = lax.dynamic_slice(pk[8:11, :CONV_DIM], (0, me * conv_w.shape[2]), (3, conv_w.shape[2]))
        return {
            "ffn1_norm": pk[0:1, :D], "mix_norm": pk[1:2, :D], "ffn2_norm": pk[2:3, :D],
            "final_norm": pk[3, :D],
            "attn_out_norm": pk[4:5, :ATTN_WIDTH], "conv_out_norm": pk[4:5, ATTN_WIDTH:ATTN_WIDTH + CONV_DIM],
            "attn_sinks": pk[5:6, :N_Q_HEADS],
            "conv_w": cwb[None],
            "rel_bias_table": pk[11, :NUM_BUCKETS * N_Q_HEADS].reshape(NUM_BUCKETS, N_Q_HEADS),
        }

    small = [unpack(pk) for pk in packs]
    loss = packs[0][6, 0]

    order = ["rel_bias_table", "ffn1_norm", "ffn1_w_gate", "ffn1_w_up", "ffn1_w_down", "mix_norm", "w_in",
             "conv_w", "attn_sinks", "attn_out_norm", "conv_out_norm", "w_out", "ffn2_norm",
             "ffn2_w_gate", "ffn2_w_up", "ffn2_w_down", "final_norm"]
    outs = [loss, dx0[None]]
    for kind in range(4):
        for name in order:
            outs.append(res[name][kind] if name in res else small[kind][name])
    return tuple(outs)
```

```python
import math

import numpy as np
import jax
import jax.numpy as jnp
from jax import lax
from jax.experimental import pallas as pl
from jax.experimental.pallas import tpu as pltpu

F32 = jnp.float32
BF16 = jnp.bfloat16

N_DEV = 8
EPS = 1e-6
HEAD_DIM = 64
N_Q_HEADS = 8
N_KV_HEADS = 2
GQA_GROUP = 4
ATTN_WIDTH = 512
KV_WIDTH = 128
CONV_DIM = 512
BLOCK = 128
WINDOW = 128
NUM_BUCKETS = 32
MAX_DISTANCE = 128
SCALE = HEAD_DIM ** -0.5
MASKED = -1e30
GROUP_ROWS = GQA_GROUP * BLOCK

ADAM_LR = 0.001
ADAM_B1 = 0.9
ADAM_B2 = 0.999
ADAM_EPS = 1e-08
ADAM_WD = 0.01
ADAM_STEP = 10

VMEM_LIMIT_BYTES = 40 * 1024 * 1024
VMEM_LIMIT_FFN_BYTES = 50 * 1024 * 1024
SUBLANES = 8
PACK_ROWS = 16
PACK_COLS = 1024

TM_FFN = 1024
TM_FFN_BWD = 1024
TM_MIX = 512
TK_WGRAD = 2048
TF_FFN = 256
BM_WGRAD = 1408
ROW_GROUPS = 4
ATTN_BLOCKS = 8
ADAM_ROWS = 256


def _row_tile(rows, limit):
    best = rows
    for t in range(16, min(rows, limit) + 1, 16):
        if rows % t == 0:
            best = t
    return best


def _params(*sem):
    return pltpu.CompilerParams(dimension_semantics=sem, vmem_limit_bytes=VMEM_LIMIT_BYTES)


def _hbm(*arrays):
    return [pltpu.with_memory_space_constraint(a, pltpu.HBM) for a in arrays]


def _dot(a, b):
    return jnp.dot(a, b, preferred_element_type=F32)


def _dot_nt(a, b):
    return lax.dot_general(a, b, (((1,), (1,)), ((), ())), preferred_element_type=F32)


def _dot_tn(a, b):
    return lax.dot_general(a, b, (((0,), (0,)), ((), ())), preferred_element_type=F32)


def _sigmoid(g):
    return 0.5 * jnp.tanh(0.5 * g) + 0.5


def _rms_stats(x):
    inv = lax.rsqrt(jnp.mean(x * x, axis=-1, keepdims=True) + EPS)
    return inv, x * inv


def _rms_bwd(dy, x, gain):
    inv, xhat = _rms_stats(x)
    dgain = jnp.sum(dy * xhat, axis=0, keepdims=True)
    dxh = dy * gain
    dx = inv * (dxh - xhat * jnp.mean(dxh * xhat, axis=-1, keepdims=True))
    return dx, dgain


def _peer_list():
    x, y, c = lax.axis_index("x"), lax.axis_index("y"), lax.axis_index("c")
    peers = []
    for k in range(1, N_DEV):
        px = 1 - x if (k >> 2) & 1 else x
        py = 1 - y if (k >> 1) & 1 else y
        pc = 1 - c if k & 1 else c
        peers.append((px, py, pc))
    return 4 * x + 2 * y + c, peers


def _exchange(arrs, scatter, name, after):
    n = len(arrs)
    out_shape = []
    for a in arrs:
        shp = a.shape if scatter else (N_DEV,) + a.shape
        out_shape.append(jax.ShapeDtypeStruct(shp, a.dtype))

    def body(*refs):
        ins, outs = refs[:n], refs[n + 1:2 * n + 1]
        send_sems, recv_sems, local_sems = refs[2 * n + 1:]
        me, peers = _peer_list()
        started = []
        for a in range(n):
            own = ins[a].at[me] if scatter else ins[a]
            loc = pltpu.make_async_copy(own, outs[a].at[me], local_sems.at[a])
            loc.start()
            started.append(loc)
        sends = []
        for a in range(n):
            for k, (px, py, pc) in enumerate(peers):
                src = ins[a].at[4 * px + 2 * py + pc] if scatter else ins[a]
                cp = pltpu.make_async_remote_copy(
                    src_ref=src, dst_ref=outs[a].at[me],
                    send_sem=send_sems.at[a, k], recv_sem=recv_sems.at[a, k],
                    device_id=(px, py, pc), device_id_type=pl.DeviceIdType.MESH)
                cp.start()
                sends.append(cp)
        for a in range(n):
            for k, (px, py, pc) in enumerate(peers):
                landed = outs[a].at[4 * px + 2 * py + pc]
                pltpu.make_async_remote_copy(
                    src_ref=landed, dst_ref=landed,
                    send_sem=send_sems.at[a, k], recv_sem=recv_sems.at[a, k],
                    device_id=(px, py, pc), device_id_type=pl.DeviceIdType.MESH).wait_recv()
        for cp in sends:
            cp.wait_send()
        for loc in started:
            loc.wait()

    hbm = pl.BlockSpec(memory_space=pl.ANY)
    return pl.pallas_call(
        body, name=name, out_shape=tuple(out_shape),
        in_specs=[hbm] * (n + 1), out_specs=tuple([hbm] * n),
        scratch_shapes=[pltpu.SemaphoreType.DMA((n, N_DEV - 1)),
                        pltpu.SemaphoreType.DMA((n, N_DEV - 1)),
                        pltpu.SemaphoreType.DMA((n,))],
    )(*arrs, after)


def _gather_two_level(arrs, name, after):
    n = len(arrs)
    out_shape = tuple(jax.ShapeDtypeStruct((N_DEV,) + a.shape, a.dtype) for a in arrs)

    def body(*refs):
        ins, outs = refs[:n], refs[n + 1:2 * n + 1]
        send_sems, recv_sems, local_sems = refs[2 * n + 1:]
        x, y, c = lax.axis_index("x"), lax.axis_index("y"), lax.axis_index("c")
        me, sibling = (x, y, c), (x, y, 1 - c)
        chips = [(1 - x, y), (x, 1 - y), (1 - x, 1 - y)]

        def copy(a, k, block, to, src=None):
            slot = outs[a].at[4 * block[0] + 2 * block[1] + block[2]]
            return pltpu.make_async_remote_copy(
                src_ref=slot if src is None else src, dst_ref=slot,
                send_sem=send_sems.at[a, k], recv_sem=recv_sems.at[a, k],
                device_id=to, device_id_type=pl.DeviceIdType.MESH)

        started = []
        for a in range(n):
            loc = pltpu.make_async_copy(ins[a], outs[a].at[4 * x + 2 * y + c], local_sems.at[a])
            loc.start()
            started.append(loc)
        sends = []
        for a in range(n):
            sends.append(copy(a, 0, me, sibling, src=ins[a]))
            sends += [copy(a, 1 + j, me, (*chip, c), src=ins[a]) for j, chip in enumerate(chips)]
        for cp in sends:
            cp.start()
        for j, chip in enumerate(chips):
            for a in range(n):
                copy(a, 1 + j, (*chip, c), me).wait_recv()
                fwd = copy(a, 4 + j, (*chip, c), sibling)
                fwd.start()
                sends.append(fwd)
        for a in range(n):
            copy(a, 0, sibling, me).wait_recv()
            for j, chip in enumerate(chips):
                copy(a, 4 + j, (*chip, 1 - c), me).wait_recv()
        for cp in sends:
            cp.wait_send()
        for loc in started:
            loc.wait()

    hbm = pl.BlockSpec(memory_space=pl.ANY)
    return pl.pallas_call(
        body, name=name, out_shape=out_shape,
        in_specs=[hbm] * (n + 1), out_specs=tuple([hbm] * n),
        scratch_shapes=[pltpu.SemaphoreType.DMA((n, N_DEV - 1)),
                        pltpu.SemaphoreType.DMA((n, N_DEV - 1)),
                        pltpu.SemaphoreType.DMA((n,))],
    )(*arrs, after)


_HBM = pl.BlockSpec(memory_space=pltpu.HBM)
_SEM = pl.BlockSpec(memory_space=pltpu.SEMAPHORE)
_EFFECT = pltpu.SideEffectType.DATAFLOW_SIDE_EFFECTING


def _split_copies(srcs, lands, send_sems, recv_sems, scatter):
    me, peers = _peer_list()
    copies = []
    for a in range(len(srcs)):
        for k, (px, py, pc) in enumerate(peers):
            src = srcs[a].at[4 * px + 2 * py + pc] if scatter else srcs[a]
            copies.append(pltpu.make_async_remote_copy(
                src_ref=src, dst_ref=lands[a].at[me],
                send_sem=send_sems[a].at[k], recv_sem=recv_sems[a].at[k],
                device_id=(px, py, pc), device_id_type=pl.DeviceIdType.MESH))
    return copies


def _exchange_start(arrs, scatter, name, after):
    n = len(arrs)
    me = 4 * lax.axis_index("x") + 2 * lax.axis_index("y") + lax.axis_index("c")
    lands = []
    for a in arrs:
        own = lax.dynamic_index_in_dim(a, me, 0, keepdims=True) if scatter else a[None]
        shp = a.shape if scatter else (N_DEV,) + a.shape
        lands.append(lax.dynamic_update_slice(lax.empty(shp, a.dtype), own, (me,) + (0,) * (len(shp) - 1)))

    def body(*refs):
        srcs, lnds = refs[:n], refs[n:2 * n]
        outs = refs[2 * n + 1:]
        send_sems, recv_sems = outs[:n], outs[n:2 * n]
        token = outs[4 * n]
        for cp in _split_copies(srcs, lnds, send_sems, recv_sems, scatter):
            cp.start()
        token[...] = jnp.zeros_like(token)

    sem = pltpu.SemaphoreType.DMA((N_DEV - 1,))
    out_shape = ([sem] * (2 * n) + [pltpu.HBM(a.shape, a.dtype) for a in arrs]
                 + [pltpu.HBM(l.shape, l.dtype) for l in lands] + [jax.ShapeDtypeStruct((SUBLANES, 128), F32)])
    res = pl.pallas_call(
        body, name=name, out_shape=tuple(out_shape),
        in_specs=[_HBM] * (2 * n) + [pl.BlockSpec(memory_space=pl.ANY)],
        out_specs=tuple([_SEM] * (2 * n) + [_HBM] * (2 * n) + [pl.BlockSpec(memory_space=pltpu.VMEM)]),
        input_output_aliases={i: 2 * n + i for i in range(2 * n)},
        compiler_params=pltpu.CompilerParams(has_side_effects=_EFFECT),
    )(*[pltpu.with_memory_space_constraint(a, pltpu.HBM) for a in arrs],
      *[pltpu.with_memory_space_constraint(l, pltpu.HBM) for l in lands], after)
    handles = [(res[2 * n + a], res[3 * n + a], res[a], res[n + a]) for a in range(n)]
    return handles, res[4 * n]


def _exchange_wait(handles, scatter, name, after):
    n = len(handles)

    def body(*refs):
        srcs, lnds = refs[:n], refs[n:2 * n]
        send_sems, recv_sems = refs[2 * n:3 * n], refs[3 * n:4 * n]
        for cp in _split_copies(srcs, lnds, send_sems, recv_sems, scatter):
            cp.wait_send()
            cp.wait_recv()

    srcs = [h[0] for h in handles]
    lands = [h[1] for h in handles]
    res = pl.pallas_call(
        body, name=name,
        out_shape=tuple([pltpu.HBM(a.shape, a.dtype) for a in srcs] + [pltpu.HBM(l.shape, l.dtype) for l in lands]),
        in_specs=[_HBM] * (2 * n) + [_SEM] * (2 * n) + [pl.BlockSpec(memory_space=pl.ANY)],
        out_specs=tuple([_HBM] * (2 * n)),
        input_output_aliases={i: i for i in range(2 * n)},
        compiler_params=pltpu.CompilerParams(has_side_effects=_EFFECT),
    )(*srcs, *lands, *[h[2] for h in handles], *[h[3] for h in handles], after)
    return list(res[n:])


def _row_groups(tm):
    return [slice(r * (tm // ROW_GROUPS), (r + 1) * (tm // ROW_GROUPS)) for r in range(ROW_GROUPS)]


def _stack_gate_up(wgt, wut, tf):
    F, D = wgt.shape
    return jnp.stack([wgt.reshape(F // tf, tf, D), wut.reshape(F // tf, tf, D)], axis=1).reshape(2 * F, D)


def _ffn_fwd(x, gain, wgu, wd, after, name, tm, tf, head=None):
    T, D = x.shape
    F = wd.shape[0]
    nj = F // tf
    n_in = 7 if head else 5

    def body(*refs):
        x_ref, g_ref, wgu_ref, wd_ref, after_ref = refs[:5]
        outs = refs[n_in:]
        xo_ref, xn_ref, gu_ref = outs[:3]
        xn_sc, acc_sc = outs[-2:]
        i, j = pl.program_id(0), pl.program_id(1)

        @pl.when(j == 0)
        def _():
            _, xhat = _rms_stats(x_ref[...])
            xn = (xhat * g_ref[...]).astype(BF16)
            xn_sc[...] = xn
            xn_ref[...] = xn
            acc_sc[...] = jnp.zeros_like(acc_sc)

        groups = _row_groups(tm)
        gus = [_dot_nt(xn_sc[rows, :], wgu_ref[...]) for rows in groups]
        hs = []
        for rows, gu in zip(groups, gus):
            gu_ref[rows, :] = gu.astype(BF16)
            g, u = gu[:, :tf], gu[:, tf:]
            hs.append((g * _sigmoid(g) * u).astype(BF16))
        for rows, h in zip(groups, hs):
            acc_sc[rows, :] += _dot(h, wd_ref[...])

        if head:
            t_ref, fg_ref = refs[5:7]
            dgain_ref, loss_ref = outs[3:5]

            @pl.when((i == 0) & (j == 0))
            def _():
                dgain_ref[...] = jnp.zeros_like(dgain_ref)
                loss_ref[...] = jnp.zeros_like(loss_ref)

        @pl.when(j == nj - 1)
        def _():
            for rows in groups:
                xo = x_ref[rows, :] + 0.5 * acc_sc[rows, :]
                if head:
                    fg = fg_ref[...]
                    _, xhat = _rms_stats(xo)
                    err = xhat * fg - t_ref[rows, :]
                    loss_ref[...] += 0.5 * jnp.sum(jnp.mean(err * err, axis=-1, keepdims=True),
                                                   axis=0, keepdims=True)
                    dx, dgain = _rms_bwd(err * (1.0 / D), xo, fg)
                    dgain_ref[...] += dgain
                    xo_ref[rows, :] = dx
                else:
                    xo_ref[rows, :] = xo

    tile = pl.BlockSpec((tm, D), lambda i, j: (i, 0))
    const = pl.BlockSpec((1, D), lambda i, j: (0, 0))
    in_specs = [tile, const, pl.BlockSpec((2 * tf, D), lambda i, j: (j, 0)),
                pl.BlockSpec((tf, D), lambda i, j: (j, 0)), pl.BlockSpec(memory_space=pl.ANY)]
    out_specs = [tile, pl.BlockSpec((tm, D), lambda i, j: (i, 0)), pl.BlockSpec((tm, 2 * tf), lambda i, j: (i, j))]
    out_shape = [jax.ShapeDtypeStruct((T, D), F32), jax.ShapeDtypeStruct((T, D), BF16),
                 jax.ShapeDtypeStruct((T, 2 * F), BF16)]
    operands = [x, gain, wgu, wd, after]
    if head:
        in_specs += [tile, const]
        out_specs += [const, pl.BlockSpec((SUBLANES, 128), lambda i, j: (0, 0))]
        out_shape += [jax.ShapeDtypeStruct((1, D), F32), jax.ShapeDtypeStruct((SUBLANES, 128), F32)]
        operands += list(head)
    return pl.pallas_call(
        body, name=name, grid=(T // tm, nj),
        in_specs=in_specs, out_specs=tuple(out_specs), out_shape=tuple(out_shape),
        scratch_shapes=[pltpu.VMEM((tm, D), BF16), pltpu.VMEM((tm, D), F32)],
        compiler_params=pltpu.CompilerParams(dimension_semantics=("arbitrary", "arbitrary"),
                                             vmem_limit_bytes=VMEM_LIMIT_FFN_BYTES),
    )(*_hbm(*operands))


def _ffn_bwd(dy, x, gain, gu, wgu, wd, after, name, tm, tf):
    T, D = x.shape
    F = wd.shape[0]
    nj = F // tf

    def body(dy_ref, x_ref, g_ref, gu_ref, wgu_ref, wd_ref, after_ref,
             dx_ref, dgain_ref, dg_ref, du_ref, hh_ref, do_ref, do_sc, acc_sc, dgu_sc):
        i, j = pl.program_id(0), pl.program_id(1)

        @pl.when((i == 0) & (j == 0))
        def _():
            dgain_ref[...] = jnp.zeros_like(dgain_ref)

        @pl.when(j == 0)
        def _():
            do = (0.5 * dy_ref[...]).astype(BF16)
            do_sc[...] = do
            do_ref[...] = do
            acc_sc[...] = jnp.zeros_like(acc_sc)

        groups = _row_groups(tm)
        dhs = [_dot_nt(do_sc[rows, :], wd_ref[...]) for rows in groups]
        for rows, dh in zip(groups, dhs):
            g = gu_ref[rows, :tf].astype(F32)
            u = gu_ref[rows, tf:].astype(F32)
            sig = _sigmoid(g)
            s = g * sig
            dg = (dh * u * (sig + s * (1.0 - sig))).astype(BF16)
            du = (dh * s).astype(BF16)
            dg_ref[rows, :] = dg
            du_ref[rows, :] = du
            dgu_sc[rows, :tf] = dg
            dgu_sc[rows, tf:] = du
            hh_ref[rows, :] = (s * u).astype(BF16)
        for rows in groups:
            acc_sc[rows, :] += _dot(dgu_sc[rows, :], wgu_ref[...])

        @pl.when(j == nj - 1)
        def _():
            for rows in groups:
                dx, dgain = _rms_bwd(acc_sc[rows, :], x_ref[rows, :], g_ref[...])
                dgain_ref[...] += dgain
                dx_ref[rows, :] = dy_ref[rows, :] + dx

    tile_in = pl.BlockSpec((tm, D), lambda i, j: (i, 0))
    tile = pl.BlockSpec((tm, D), lambda i, j: (i, 0), pipeline_mode=pl.Buffered(1))
    return pl.pallas_call(
        body, name=name, grid=(T // tm, nj),
        in_specs=[tile_in, tile_in,
                  pl.BlockSpec((1, D), lambda i, j: (0, 0)),
                  pl.BlockSpec((tm, 2 * tf), lambda i, j: (i, j)),
                  pl.BlockSpec((2 * tf, D), lambda i, j: (j, 0)),
                  pl.BlockSpec((tf, D), lambda i, j: (j, 0)),
                  pl.BlockSpec(memory_space=pl.ANY)],
        out_specs=(tile,
                   pl.BlockSpec((1, D), lambda i, j: (0, 0)),
                   pl.BlockSpec((tm, tf), lambda i, j: (i, j)),
                   pl.BlockSpec((tm, tf), lambda i, j: (i, j)),
                   pl.BlockSpec((tm, tf), lambda i, j: (i, j)),
                   pl.BlockSpec((tm, D), lambda i, j: (i, 0))),
        out_shape=(jax.ShapeDtypeStruct((T, D), F32), jax.ShapeDtypeStruct((1, D), F32),
                   jax.ShapeDtypeStruct((T, F), BF16), jax.ShapeDtypeStruct((T, F), BF16),
                   jax.ShapeDtypeStruct((T, F), BF16), jax.ShapeDtypeStruct((T, D), BF16)),
        scratch_shapes=[pltpu.VMEM((tm, D), BF16), pltpu.VMEM((tm, D), F32), pltpu.VMEM((tm, 2 * tf), BF16)],
        compiler_params=pltpu.CompilerParams(dimension_semantics=("arbitrary", "arbitrary"),
                                             vmem_limit_bytes=VMEM_LIMIT_FFN_BYTES),
    )(*_hbm(dy, x, gain, gu, wgu, wd, after))


def _tn_grad(a, b, after, name, bm, tk):
    T, M = a.shape
    D = b.shape[1]
    nk = T // tk

    def body(a_ref, b_ref, after_ref, o_ref, acc_sc):
        k = pl.program_id(1)

        @pl.when(k == 0)
        def _():
            acc_sc[...] = jnp.zeros_like(acc_sc)

        acc_sc[...] += _dot_tn(a_ref[...], b_ref[...])

        @pl.when(k == nk - 1)
        def _():
            o_ref[...] = acc_sc[...].astype(BF16)

    return pl.pallas_call(
        body, name=name, grid=(M // bm, nk),
        in_specs=[pl.BlockSpec((tk, bm), lambda i, k: (k, i)), pl.BlockSpec((tk, D), lambda i, k: (k, 0)),
                  pl.BlockSpec(memory_space=pl.ANY)],
        out_specs=pl.BlockSpec((bm, D), lambda i, k: (i, 0)),
        out_shape=jax.ShapeDtypeStruct((M, D), BF16),
        scratch_shapes=[pltpu.VMEM((bm, D), F32)],
        compiler_params=_params("arbitrary", "arbitrary"),
    )(*_hbm(a, b, after))


_Z_SPLITS = (0, 512, 640, 768, 1280, 1792, 2304)


def _mixin_fwd(x, gain, w_in_t, name, tm):
    T, D = x.shape
    widths = [b - a for a, b in zip(_Z_SPLITS[:-1], _Z_SPLITS[1:])]

    def body(x_ref, g_ref, w_ref, hn_ref, *outs):
        _, xhat = _rms_stats(x_ref[...])
        hn = (xhat * g_ref[...]).astype(BF16)
        hn_ref[...] = hn
        for n in (0, 3, 4, 5):
            outs[n][...] = _dot_nt(hn, w_ref[_Z_SPLITS[n]:_Z_SPLITS[n + 1], :]).astype(BF16)
        kv = _dot_nt(hn, w_ref[_Z_SPLITS[1]:_Z_SPLITS[3], :]).astype(BF16)
        outs[1][...] = kv[:, :KV_WIDTH]
        outs[2][...] = kv[:, KV_WIDTH:]

    return pl.pallas_call(
        body, name=name, grid=(T // tm,),
        in_specs=[pl.BlockSpec((tm, D), lambda i: (i, 0)),
                  pl.BlockSpec((1, D), lambda i: (0, 0)),
                  pl.BlockSpec(w_in_t.shape, lambda i: (0, 0))],
        out_specs=tuple([pl.BlockSpec((tm, D), lambda i: (i, 0))]
                        + [pl.BlockSpec((tm, w), lambda i: (i, 0)) for w in widths]),
        out_shape=tuple([jax.ShapeDtypeStruct((T, D), BF16)]
                        + [jax.ShapeDtypeStruct((T, w), BF16) for w in widths]),
        compiler_params=_params("arbitrary"),
    )(*_hbm(x, gain, w_in_t))


def _bucket_table():
    qi = np.arange(BLOCK, dtype=np.int32)[:, None]
    kj = np.arange(2 * BLOCK, dtype=np.int32)[None, :]
    dist = qi + BLOCK - kj
    n = np.maximum(dist, 0)
    max_exact = NUM_BUCKETS // 2
    large = max_exact + (np.log(np.maximum(n, 1).astype(np.float32) / max_exact)
                         / math.log(MAX_DISTANCE / max_exact)
                         * (NUM_BUCKETS - max_exact)).astype(np.int32)
    large = np.minimum(large, NUM_BUCKETS - 1)
    bucket = np.where(n < max_exact, n, large).astype(np.int32)
    valid = (dist >= 0) & (dist < WINDOW)
    return np.where(valid, bucket, -1).astype(np.int32)


def _bias_build(table, bucket, name):
    def body(t_ref, b_ref, o_ref):
        bk = b_ref[...]
        for h in range(N_Q_HEADS):
            def step(b, acc):
                return jnp.where(bk == b, t_ref[b, h], acc)
            o_ref[h] = lax.fori_loop(0, NUM_BUCKETS, step, jnp.full(bk.shape, MASKED, F32))

    return pl.pallas_call(
        body, name=name,
        in_specs=[pl.BlockSpec(memory_space=pltpu.SMEM), pl.BlockSpec(memory_space=pltpu.VMEM)],
        out_specs=pl.BlockSpec(memory_space=pltpu.VMEM),
        out_shape=jax.ShapeDtypeStruct((N_Q_HEADS,) + bucket.shape, F32),
    )(table, bucket)


def _bias_grad(dbias, bucket, name):
    def body(d_ref, b_ref, o_ref):
        bk = b_ref[...]
        row = lax.broadcasted_iota(jnp.int32, o_ref.shape, 0)
        lane = lax.broadcasted_iota(jnp.int32, o_ref.shape, 1)
        res = jnp.zeros(o_ref.shape, F32)
        for h in range(N_Q_HEADS):
            d = d_ref[h]

            def step(b, acc):
                part = jnp.sum(jnp.where(bk == b, d, 0.0), axis=0, keepdims=True)
                return jnp.where(row == b, part, acc)
            per_lane = lax.fori_loop(0, NUM_BUCKETS, step, jnp.zeros(o_ref.shape, F32))
            res = jnp.where(lane == h, jnp.sum(per_lane, axis=1, keepdims=True), res)
        o_ref[...] = res

    return pl.pallas_call(
        body, name=name,
        in_specs=[pl.BlockSpec(memory_space=pltpu.VMEM), pl.BlockSpec(memory_space=pltpu.VMEM)],
        out_specs=pl.BlockSpec(memory_space=pltpu.VMEM),
        out_shape=jax.ShapeDtypeStruct((NUM_BUCKETS, 128), F32),
    )(*_hbm(dbias, bucket))


def _head_cols(h):
    return slice(h * HEAD_DIM, (h + 1) * HEAD_DIM)


def _stack_heads(ref, r0, g, dtype):
    return jnp.concatenate(
        [ref[pl.ds(r0, BLOCK), _head_cols(GQA_GROUP * g + j)].astype(dtype) for j in range(GQA_GROUP)], axis=0)


def _unstack_heads(ref, r0, g, val):
    for j in range(GQA_GROUP):
        ref[pl.ds(r0, BLOCK), _head_cols(GQA_GROUP * g + j)] = val[j * BLOCK:(j + 1) * BLOCK, :]


def _head_lanes(h):
    return slice(h * BLOCK, (h + 1) * BLOCK)


def _group_lanes(g):
    return slice(g * GROUP_ROWS, (g + 1) * GROUP_ROWS)


def _head_softmax(st, bias_t, sink, no_prev):
    s = st * SCALE + bias_t
    row = lax.broadcasted_iota(jnp.int32, s.shape, 0)
    s = jnp.where(no_prev & (row < BLOCK), MASKED, s)
    m = jnp.maximum(jnp.max(s, axis=0, keepdims=True), sink)
    p = jnp.exp(s - m)
    ps = jnp.exp(sink - m)
    r = 1.0 / (jnp.sum(p, axis=0, keepdims=True) + ps)
    return p * r, ps * r


def _load_band(kf_sc, vf_sc, kp_ref, kc_ref, vp_ref, vc_ref, tq):
    kf_sc[0:BLOCK, :] = kp_ref[...]
    kf_sc[BLOCK:BLOCK + tq, :] = kc_ref[...]
    vf_sc[0:BLOCK, :] = vp_ref[...]
    vf_sc[BLOCK:BLOCK + tq, :] = vc_ref[...]


def _attn_fwd(q, k, v, bias_t, sinks, gain, name, nblk):
    T = q.shape[0]
    tq = nblk * BLOCK

    def body(sink_ref, q_ref, kc_ref, kp_ref, vc_ref, vp_ref, bias_ref, g_ref, raw_ref, nrm_ref,
             kf_sc, vf_sc, o_sc, st_sc, pt_sc):
        i = pl.program_id(0)
        _load_band(kf_sc, vf_sc, kp_ref, kc_ref, vp_ref, vc_ref, tq)

        def block(b, carry):
            r0 = pl.multiple_of(b * BLOCK, BLOCK)
            no_prev = (i == 0) & (b == 0)
            for g in range(N_KV_HEADS):
                kb = kf_sc[pl.ds(r0, 2 * BLOCK), _head_cols(g)]
                st_sc[:, _group_lanes(g)] = _dot_nt(kb, _stack_heads(q_ref, r0, g, BF16))
            for h in range(N_Q_HEADS):
                p, _ = _head_softmax(st_sc[:, _head_lanes(h)], bias_ref[h], sink_ref[h], no_prev)
                pt_sc[:, _head_lanes(h)] = p.astype(BF16)
            for g in range(N_KV_HEADS):
                vb = vf_sc[pl.ds(r0, 2 * BLOCK), _head_cols(g)]
                _unstack_heads(o_sc, r0, g, _dot_tn(pt_sc[:, _group_lanes(g)], vb))
            return carry

        lax.fori_loop(0, nblk, block, 0, unroll=True)
        o = o_sc[...]
        raw_ref[...] = o.astype(BF16)
        _, ohat = _rms_stats(o)
        nrm_ref[...] = (ohat * g_ref[...]).astype(BF16)

    cur = lambda i: (i, 0)
    prev = lambda i: (jnp.maximum(i * nblk - 1, 0), 0)
    lanes = N_Q_HEADS * BLOCK
    return pl.pallas_call(
        body, name=name, grid=(T // tq,),
        in_specs=[pl.BlockSpec(memory_space=pltpu.SMEM),
                  pl.BlockSpec((tq, ATTN_WIDTH), cur),
                  pl.BlockSpec((tq, KV_WIDTH), cur), pl.BlockSpec((BLOCK, KV_WIDTH), prev),
                  pl.BlockSpec((tq, KV_WIDTH), cur), pl.BlockSpec((BLOCK, KV_WIDTH), prev),
                  pl.BlockSpec(bias_t.shape, lambda i: (0, 0, 0)),
                  pl.BlockSpec((1, ATTN_WIDTH), lambda i: (0, 0))],
        out_specs=(pl.BlockSpec((tq, ATTN_WIDTH), cur), pl.BlockSpec((tq, ATTN_WIDTH), cur)),
        out_shape=(jax.ShapeDtypeStruct((T, ATTN_WIDTH), BF16), jax.ShapeDtypeStruct((T, ATTN_WIDTH), BF16)),
        scratch_shapes=[pltpu.VMEM((tq + BLOCK, KV_WIDTH), BF16), pltpu.VMEM((tq + BLOCK, KV_WIDTH), BF16),
                        pltpu.VMEM((tq, ATTN_WIDTH), F32),
                        pltpu.VMEM((2 * BLOCK, lanes), F32), pltpu.VMEM((2 * BLOCK, lanes), BF16)],
        compiler_params=_params("arbitrary"),
    )(sinks, *_hbm(q, k, k, v, v, bias_t, gain))


def _attn_bwd(dmixed, raw, q, k, v, bias_t, sinks, gain, name, nblk):
    T = q.shape[0]
    tq = nblk * BLOCK
    nt = T // tq
    lanes = N_Q_HEADS * BLOCK

    def body(sink_ref, dm_ref, raw_ref, q_ref, kc_ref, kp_ref, vc_ref, vp_ref, bias_ref, g_ref,
             dq_ref, dk_ref, dv_ref, dbias_ref, dsink_ref, dgain_ref,
             do_sc, dq_sc, kf_sc, vf_sc, dkf_sc, dvf_sc, st_sc, dpt_sc, pt_sc, dst_sc, drow_sc,
             qs_sc, dos_sc, dsink_sc):
        i = pl.program_id(0)
        tile = nt - 1 - i

        @pl.when(i == 0)
        def _():
            dkf_sc[...] = jnp.zeros_like(dkf_sc)
            dvf_sc[...] = jnp.zeros_like(dvf_sc)
            dsink_sc[...] = jnp.zeros_like(dsink_sc)
            dbias_ref[...] = jnp.zeros_like(dbias_ref)
            dgain_ref[...] = jnp.zeros_like(dgain_ref)

        carry_k = dkf_sc[0:BLOCK, :]
        carry_v = dvf_sc[0:BLOCK, :]
        dkf_sc[0:tq, :] = jnp.zeros((tq, KV_WIDTH), F32)
        dvf_sc[0:tq, :] = jnp.zeros((tq, KV_WIDTH), F32)
        dkf_sc[tq:tq + BLOCK, :] = carry_k
        dvf_sc[tq:tq + BLOCK, :] = carry_v
        _load_band(kf_sc, vf_sc, kp_ref, kc_ref, vp_ref, vc_ref, tq)

        do, dgain = _rms_bwd(dm_ref[...].astype(F32), raw_ref[...].astype(F32), g_ref[...])
        dgain_ref[...] += dgain
        do_sc[...] = do
        ones = jnp.ones((SUBLANES, HEAD_DIM), BF16)

        def block(b, carry):
            r0 = pl.multiple_of(b * BLOCK, BLOCK)
            no_prev = (tile == 0) & (b == 0)
            for g in range(N_KV_HEADS):
                kb = kf_sc[pl.ds(r0, 2 * BLOCK), _head_cols(g)]
                vb = vf_sc[pl.ds(r0, 2 * BLOCK), _head_cols(g)]
                qg = _stack_heads(q_ref, r0, g, BF16)
                dog = _stack_heads(do_sc, r0, g, F32)
                prod = dog * _stack_heads(raw_ref, r0, g, F32)
                hi = prod.astype(BF16)
                lo = (prod - hi.astype(F32)).astype(BF16)
                drow_sc[:, _group_lanes(g)] = _dot_nt(ones, hi) + _dot_nt(ones, lo)
                dogb = dog.astype(BF16)
                qs_sc[g] = qg
                dos_sc[g] = dogb
                st_sc[:, _group_lanes(g)] = _dot_nt(kb, qg)
                dpt_sc[:, _group_lanes(g)] = _dot_nt(vb, dogb)
            for h in range(N_Q_HEADS):
                hl = _head_lanes(h)
                p, ps = _head_softmax(st_sc[:, hl], bias_ref[h], sink_ref[h], no_prev)
                rowdot = drow_sc[0:1, hl]
                ds = p * (dpt_sc[:, hl] - rowdot)
                dsink_sc[h:h + 1, :] += -(ps * rowdot)
                dbias_ref[h] += ds
                dst_sc[:, hl] = ds.astype(BF16)
                pt_sc[:, hl] = p.astype(BF16)
            for g in range(N_KV_HEADS):
                kb = kf_sc[pl.ds(r0, 2 * BLOCK), _head_cols(g)]
                dsg = dst_sc[:, _group_lanes(g)]
                _unstack_heads(dq_sc, r0, g, _dot_tn(dsg, kb) * SCALE)
                dkf_sc[pl.ds(r0, 2 * BLOCK), _head_cols(g)] += _dot(dsg, qs_sc[g]) * SCALE
                dvf_sc[pl.ds(r0, 2 * BLOCK), _head_cols(g)] += _dot(pt_sc[:, _group_lanes(g)], dos_sc[g])
            return carry

        lax.fori_loop(0, nblk, block, 0, unroll=True)
        dq_ref[...] = dq_sc[...].astype(BF16)
        dk_ref[...] = dkf_sc[BLOCK:BLOCK + tq, :].astype(BF16)
        dv_ref[...] = dvf_sc[BLOCK:BLOCK + tq, :].astype(BF16)

        @pl.when(i == nt - 1)
        def _():
            tot = jnp.sum(dsink_sc[...], axis=1, keepdims=True)
            dsink_ref[...] = jnp.broadcast_to(tot, dsink_ref.shape)

    cur = lambda i: (nt - 1 - i, 0)
    prev = lambda i: (jnp.maximum((nt - 1 - i) * nblk - 1, 0), 0)
    const2 = lambda i: (0, 0)
    const3 = lambda i: (0, 0, 0)
    return pl.pallas_call(
        body, name=name, grid=(nt,),
        in_specs=[pl.BlockSpec(memory_space=pltpu.SMEM),
                  pl.BlockSpec((tq, ATTN_WIDTH), cur),
                  pl.BlockSpec((tq, ATTN_WIDTH), cur),
                  pl.BlockSpec((tq, ATTN_WIDTH), cur),
                  pl.BlockSpec((tq, KV_WIDTH), cur), pl.BlockSpec((BLOCK, KV_WIDTH), prev),
                  pl.BlockSpec((tq, KV_WIDTH), cur), pl.BlockSpec((BLOCK, KV_WIDTH), prev),
                  pl.BlockSpec(bias_t.shape, const3),
                  pl.BlockSpec((1, ATTN_WIDTH), const2)],
        out_specs=(pl.BlockSpec((tq, ATTN_WIDTH), cur),
                   pl.BlockSpec((tq, KV_WIDTH), cur), pl.BlockSpec((tq, KV_WIDTH), cur),
                   pl.BlockSpec(bias_t.shape, const3),
                   pl.BlockSpec((N_Q_HEADS, 128), const2),
                   pl.BlockSpec((1, ATTN_WIDTH), const2)),
        out_shape=(jax.ShapeDtypeStruct((T, ATTN_WIDTH), BF16),
                   jax.ShapeDtypeStruct((T, KV_WIDTH), BF16), jax.ShapeDtypeStruct((T, KV_WIDTH), BF16),
                   jax.ShapeDtypeStruct(bias_t.shape, F32),
                   jax.ShapeDtypeStruct((N_Q_HEADS, 128), F32),
                   jax.ShapeDtypeStruct((1, ATTN_WIDTH), F32)),
        scratch_shapes=[pltpu.VMEM((tq, ATTN_WIDTH), F32), pltpu.VMEM((tq, ATTN_WIDTH), F32),
                        pltpu.VMEM((tq + BLOCK, KV_WIDTH), BF16), pltpu.VMEM((tq + BLOCK, KV_WIDTH), BF16),
                        pltpu.VMEM((tq + BLOCK, KV_WIDTH), F32), pltpu.VMEM((tq + BLOCK, KV_WIDTH), F32),
                        pltpu.VMEM((2 * BLOCK, lanes), F32), pltpu.VMEM((2 * BLOCK, lanes), F32),
                        pltpu.VMEM((2 * BLOCK, lanes), BF16), pltpu.VMEM((2 * BLOCK, lanes), BF16),
                        pltpu.VMEM((SUBLANES, lanes), F32),
                        pltpu.VMEM((N_KV_HEADS, GROUP_ROWS, HEAD_DIM), BF16),
                        pltpu.VMEM((N_KV_HEADS, GROUP_ROWS, HEAD_DIM), BF16),
                        pltpu.VMEM((N_Q_HEADS, 128), F32)],
        compiler_params=_params("arbitrary"),
    )(sinks, *_hbm(dmixed, raw, q, k, k, v, v, bias_t, gain))


def _shift_down(cu, tail):
    row = lax.broadcasted_iota(jnp.int32, cu.shape, 0)
    t6, t7 = tail[6:7, :], tail[7:8, :]
    s1 = jnp.where(row == 0, t7, pltpu.roll(cu, 1, 0))
    s2 = jnp.where(row == 0, t6, jnp.where(row == 1, t7, pltpu.roll(cu, 2, 0)))
    return s1, s2


def _shift_up(d, head):
    n = d.shape[0]
    row = lax.broadcasted_iota(jnp.int32, d.shape, 0)
    h0, h1 = head[0:1, :], head[1:2, :]
    s1 = jnp.where(row == n - 1, h0, pltpu.roll(d, n - 1, 0))
    s2 = jnp.where(row == n - 1, h1, jnp.where(row == n - 2, h0, pltpu.roll(d, n - 2, 0)))
    return s1, s2


def _mixout_fwd(x, attn_n, u, gb, gc, conv_w, gain, w_out, name, tm):
    T, D = x.shape

    def body(x_ref, an_ref, u_ref, b_ref, c_ref, cw_ref, g_ref, wo_ref, xo_ref, cn_ref, tail_sc):
        @pl.when(pl.program_id(0) == 0)
        def _():
            tail_sc[...] = jnp.zeros_like(tail_sc)

        cu = c_ref[...].astype(F32) * u_ref[...].astype(F32)
        s1, s2 = _shift_down(cu, tail_sc[...])
        tail_sc[...] = cu[tm - SUBLANES:tm, :]
        pre = cw_ref[0:1, :] * s2 + cw_ref[1:2, :] * s1 + cw_ref[2:3, :] * cu
        conv = b_ref[...].astype(F32) * pre
        _, chat = _rms_stats(conv)
        cn = (chat * g_ref[...]).astype(BF16)
        cn_ref[...] = cn
        xo_ref[...] = (x_ref[...] + _dot(an_ref[...], wo_ref[0:ATTN_WIDTH, :])
                       + _dot(cn, wo_ref[ATTN_WIDTH:ATTN_WIDTH + CONV_DIM, :]))

    row = lambda i: (i, 0)
    const = lambda i: (0, 0)
    return pl.pallas_call(
        body, name=name, grid=(T // tm,),
        in_specs=[pl.BlockSpec((tm, D), row), pl.BlockSpec((tm, ATTN_WIDTH), row),
                  pl.BlockSpec((tm, CONV_DIM), row), pl.BlockSpec((tm, CONV_DIM), row),
                  pl.BlockSpec((tm, CONV_DIM), row),
                  pl.BlockSpec(conv_w.shape, const), pl.BlockSpec((1, CONV_DIM), const),
                  pl.BlockSpec(w_out.shape, const)],
        out_specs=(pl.BlockSpec((tm, D), row), pl.BlockSpec((tm, CONV_DIM), row)),
        out_shape=(jax.ShapeDtypeStruct((T, D), F32), jax.ShapeDtypeStruct((T, CONV_DIM), BF16)),
        scratch_shapes=[pltpu.VMEM((SUBLANES, CONV_DIM), F32)],
        compiler_params=_params("arbitrary"),
    )(*_hbm(x, attn_n, u, gb, gc, conv_w, gain, w_out))


def _mixout_bwd(dy, attn_n, conv_n, w_out, after, name, tm):
    T, D = dy.shape
    W = ATTN_WIDTH + CONV_DIM
    nt = T // tm

    def body(dy_ref, an_ref, cn_ref, w_ref, after_ref, dm_ref, dw_ref, dw_sc):
        i = pl.program_id(0)

        @pl.when(i == 0)
        def _():
            dw_sc[...] = jnp.zeros_like(dw_sc)

        dyb = dy_ref[...].astype(BF16)
        dm_ref[...] = _dot_nt(dyb, w_ref[...]).astype(BF16)
        dw_sc[0:ATTN_WIDTH, :] += _dot_tn(an_ref[...], dyb)
        dw_sc[ATTN_WIDTH:W, :] += _dot_tn(cn_ref[...], dyb)

        @pl.when(i == nt - 1)
        def _():
            dw_ref[...] = dw_sc[...].astype(BF16)

    row = lambda i: (i, 0)
    const = lambda i: (0, 0)
    return pl.pallas_call(
        body, name=name, grid=(nt,),
        in_specs=[pl.BlockSpec((tm, D), row), pl.BlockSpec((tm, ATTN_WIDTH), row),
                  pl.BlockSpec((tm, CONV_DIM), row), pl.BlockSpec(w_out.shape, const),
                  pl.BlockSpec(memory_space=pl.ANY)],
        out_specs=(pl.BlockSpec((tm, W), row), pl.BlockSpec((W, D), const)),
        out_shape=(jax.ShapeDtypeStruct((T, W), BF16), jax.ShapeDtypeStruct((W, D), BF16)),
        scratch_shapes=[pltpu.VMEM((W, D), F32)],
        compiler_params=_params("arbitrary"),
    )(*_hbm(dy, attn_n, conv_n, w_out, after))


def _conv_bwd(dmixed, u, gb, gc, conv_w, gain, name, tc):
    T = u.shape[0]
    nt = T // tc
    per8 = tc // SUBLANES

    def body(dm_ref, u_ref, b_ref, c_ref, ut_ref, ct_ref, cw_ref, g_ref,
             du_ref, db_ref, dc_ref, dcw_ref, dgain_ref, head_sc):
        i = pl.program_id(0)

        @pl.when(i == 0)
        def _():
            head_sc[...] = jnp.zeros_like(head_sc)
            dcw_ref[...] = jnp.zeros_like(dcw_ref)
            dgain_ref[...] = jnp.zeros_like(dgain_ref)

        uu = u_ref[...].astype(F32)
        cc = c_ref[...].astype(F32)
        bb = b_ref[...].astype(F32)
        cu = cc * uu
        tail = jnp.where(i == nt - 1, 0.0, ct_ref[...].astype(F32) * ut_ref[...].astype(F32))
        s1, s2 = _shift_down(cu, tail)
        w0, w1, w2 = cw_ref[0:1, :], cw_ref[1:2, :], cw_ref[2:3, :]
        pre = w0 * s2 + w1 * s1 + w2 * cu
        dconv, dgain = _rms_bwd(dm_ref[...].astype(F32), bb * pre, g_ref[...])
        dgain_ref[...] += dgain
        db_ref[...] = (dconv * pre).astype(BF16)
        dpre = dconv * bb
        dcw_ref[0:1, :] += jnp.sum(dpre * s2, axis=0, keepdims=True)
        dcw_ref[1:2, :] += jnp.sum(dpre * s1, axis=0, keepdims=True)
        dcw_ref[2:3, :] += jnp.sum(dpre * cu, axis=0, keepdims=True)
        n1, n2 = _shift_up(dpre, head_sc[...])
        head_sc[...] = dpre[0:SUBLANES, :]
        dcu = w2 * dpre + w1 * n1 + w0 * n2
        du_ref[...] = (dcu * cc).astype(BF16)
        dc_ref[...] = (dcu * uu).astype(BF16)

    rev = lambda i: (nt - 1 - i, 0)
    rev_right = lambda i: (nt - 1 - i, 1)
    tail_map = lambda i: (jnp.maximum((nt - 1 - i) * per8 - 1, 0), 0)
    const = lambda i: (0, 0)
    return pl.pallas_call(
        body, name=name, grid=(nt,),
        in_specs=[pl.BlockSpec((tc, CONV_DIM), rev_right),
                  pl.BlockSpec((tc, CONV_DIM), rev), pl.BlockSpec((tc, CONV_DIM), rev),
                  pl.BlockSpec((tc, CONV_DIM), rev),
                  pl.BlockSpec((SUBLANES, CONV_DIM), tail_map), pl.BlockSpec((SUBLANES, CONV_DIM), tail_map),
                  pl.BlockSpec(conv_w.shape, const), pl.BlockSpec((1, CONV_DIM), const)],
        out_specs=(pl.BlockSpec((tc, CONV_DIM), rev), pl.BlockSpec((tc, CONV_DIM), rev),
                   pl.BlockSpec((tc, CONV_DIM), rev),
                   pl.BlockSpec((SUBLANES, CONV_DIM), const), pl.BlockSpec((1, CONV_DIM), const)),
        out_shape=(jax.ShapeDtypeStruct((T, CONV_DIM), BF16), jax.ShapeDtypeStruct((T, CONV_DIM), BF16),
                   jax.ShapeDtypeStruct((T, CONV_DIM), BF16),
                   jax.ShapeDtypeStruct((SUBLANES, CONV_DIM), F32), jax.ShapeDtypeStruct((1, CONV_DIM), F32)),
        scratch_shapes=[pltpu.VMEM((SUBLANES, CONV_DIM), F32)],
        compiler_params=_params("arbitrary"),
    )(*_hbm(dmixed, u, gb, gc, u, gc, conv_w, gain))


def _mixin_bwd(dy, x, gain, dz, w_in_t, name, tm):
    T, D = x.shape
    nz = len(dz)

    def body(dy_ref, x_ref, g_ref, *rest):
        dz_refs, wt_ref, dx_ref, dgain_ref, dz_sc = rest[:nz], rest[nz], rest[nz + 1], rest[nz + 2], rest[nz + 3]

        @pl.when(pl.program_id(0) == 0)
        def _():
            dgain_ref[...] = jnp.zeros_like(dgain_ref)

        for r, lo, hi in zip(dz_refs, _Z_SPLITS[:-1], _Z_SPLITS[1:]):
            dz_sc[:, lo:hi] = r[...]
        dx, dgain = _rms_bwd(_dot(dz_sc[...], wt_ref[...]), x_ref[...], g_ref[...])
        dgain_ref[...] += dgain
        dx_ref[...] = dy_ref[...] + dx

    row = lambda i: (i, 0)
    const = lambda i: (0, 0)
    return pl.pallas_call(
        body, name=name, grid=(T // tm,),
        in_specs=[pl.BlockSpec((tm, D), row), pl.BlockSpec((tm, D), row), pl.BlockSpec((1, D), const)]
                 + [pl.BlockSpec((tm, a.shape[1]), row) for a in dz]
                 + [pl.BlockSpec(w_in_t.shape, const)],
        out_specs=(pl.BlockSpec((tm, D), row), pl.BlockSpec((1, D), const)),
        out_shape=(jax.ShapeDtypeStruct((T, D), F32), jax.ShapeDtypeStruct((1, D), F32)),
        scratch_shapes=[pltpu.VMEM((tm, _Z_SPLITS[-1]), BF16)],
        compiler_params=_params("arbitrary"),
    )(*_hbm(dy, x, gain, *dz, w_in_t))


def _win_grad(dz, hn, name, tk):
    T, D = hn.shape
    nz = len(dz)
    nt = T // tk
    W = _Z_SPLITS[-1]

    def body(hn_ref, *rest):
        dz_refs, dw_ref, dw_sc, kv_sc = rest[:nz], rest[nz], rest[nz + 1], rest[nz + 2]
        i = pl.program_id(0)

        @pl.when(i == 0)
        def _():
            dw_sc[...] = jnp.zeros_like(dw_sc)

        hn = hn_ref[...]
        kv_sc[:, :KV_WIDTH] = dz_refs[1][...]
        kv_sc[:, KV_WIDTH:] = dz_refs[2][...]
        dw_sc[_Z_SPLITS[1]:_Z_SPLITS[3], :] += _dot_tn(kv_sc[...], hn)
        for n in (0, 3, 4, 5):
            dw_sc[_Z_SPLITS[n]:_Z_SPLITS[n + 1], :] += _dot_tn(dz_refs[n][...], hn)

        @pl.when(i == nt - 1)
        def _():
            dw_ref[...] = dw_sc[...].astype(BF16)

    row = lambda i: (i, 0)
    return pl.pallas_call(
        body, name=name, grid=(nt,),
        in_specs=[pl.BlockSpec((tk, D), row)] + [pl.BlockSpec((tk, a.shape[1]), row) for a in dz],
        out_specs=pl.BlockSpec((W, D), lambda i: (0, 0)),
        out_shape=jax.ShapeDtypeStruct((W, D), BF16),
        scratch_shapes=[pltpu.VMEM((W, D), F32), pltpu.VMEM((tk, 2 * KV_WIDTH), BF16)],
        compiler_params=_params("arbitrary"),
    )(*_hbm(hn, *dz))


def _adamw(parts, w, m, v, name, tr):
    P = parts.shape[0]
    R, C = w.shape

    def body(p_ref, w_ref, m_ref, v_ref, g_ref, d_ref, nm_ref, nv_ref):
        g = p_ref[0].astype(F32)
        for d in range(1, P):
            g = g + p_ref[d].astype(F32)
        nm = ADAM_B1 * m_ref[...] + (1.0 - ADAM_B1) * g
        nv = ADAM_B2 * v_ref[...] + (1.0 - ADAM_B2) * (g * g)
        m_hat = nm / (1.0 - ADAM_B1 ** ADAM_STEP)
        v_hat = nv / (1.0 - ADAM_B2 ** ADAM_STEP)
        g_ref[...] = g
        nm_ref[...] = nm
        nv_ref[...] = nv
        d_ref[...] = -ADAM_LR * (m_hat / (jnp.sqrt(v_hat) + ADAM_EPS) + ADAM_WD * w_ref[...])

    row = lambda i: (i, 0)
    spec = pl.BlockSpec((tr, C), row)
    shp = jax.ShapeDtypeStruct((R, C), F32)
    return pl.pallas_call(
        body, name=name, grid=(R // tr,),
        in_specs=[pl.BlockSpec((P, tr, C), lambda i: (0, i, 0)), spec, spec, spec],
        out_specs=(spec, spec, spec, spec),
        out_shape=(shp, shp, shp, shp),
        compiler_params=_params("arbitrary"),
    )(*_hbm(parts, w, m, v))


def _columns_of_blocks(g):
    n, R, w = g.shape
    return g.transpose(1, 0, 2).reshape(R, n * w)


def _pad_row(vec):
    vec = vec.reshape(1, -1)
    return jnp.pad(vec, ((0, 0), (0, PACK_COLS - vec.shape[1])))


def kernel(x, rel_bias_table, ffn1_norm, ffn1_w_gate, ffn1_w_up, ffn1_w_down, mix_norm, w_in, conv_w, attn_sinks, attn_out_norm, conv_out_norm, w_out, ffn2_norm, ffn2_w_gate, ffn2_w_up, ffn2_w_down, final_norm, loss_target, m_rel_bias_table, m_ffn1_norm, m_ffn1_w_gate, m_ffn1_w_up, m_ffn1_w_down, m_mix_norm, m_w_in, m_conv_w, m_attn_sinks, m_attn_out_norm, m_conv_out_norm, m_w_out, m_ffn2_norm, m_ffn2_w_gate, m_ffn2_w_up, m_ffn2_w_down, m_final_norm, v_rel_bias_table, v_ffn1_norm, v_ffn1_w_gate, v_ffn1_w_up, v_ffn1_w_down, v_mix_norm, v_w_in, v_conv_w, v_attn_sinks, v_attn_out_norm, v_conv_out_norm, v_w_out, v_ffn2_norm, v_ffn2_w_gate, v_ffn2_w_up, v_ffn2_w_down, v_final_norm):
    T, D = x.shape[1], x.shape[2]
    x0 = x[0]
    target = loss_target[0]
    tm = min(TM_FFN, T)
    tm_bwd = min(TM_FFN_BWD, T)
    tm_mix = min(TM_MIX, T)
    tk = min(TK_WGRAD, T)
    tf = TF_FFN
    nblk = min(ATTN_BLOCKS, T // BLOCK)
    me = 4 * lax.axis_index("x") + 2 * lax.axis_index("y") + lax.axis_index("c")

    big = {
        "ffn1_w_gate": (ffn1_w_gate[0], m_ffn1_w_gate[0], v_ffn1_w_gate[0], True),
        "ffn1_w_up": (ffn1_w_up[0], m_ffn1_w_up[0], v_ffn1_w_up[0], True),
        "ffn1_w_down": (ffn1_w_down[0], m_ffn1_w_down[0], v_ffn1_w_down[0], False),
        "w_in": (w_in[0], m_w_in[0], v_w_in[0], True),
        "w_out": (w_out[0], m_w_out[0], v_w_out[0], False),
        "ffn2_w_gate": (ffn2_w_gate[0], m_ffn2_w_gate[0], v_ffn2_w_gate[0], True),
        "ffn2_w_up": (ffn2_w_up[0], m_ffn2_w_up[0], v_ffn2_w_up[0], True),
        "ffn2_w_down": (ffn2_w_down[0], m_ffn2_w_down[0], v_ffn2_w_down[0], False),
    }

    def block_to_send(name):
        w, _, _, transposed = big[name]
        return (w.T if transposed else w).astype(BF16)

    names1 = ["ffn1_w_gate", "ffn1_w_up", "ffn1_w_down"]
    names_rest = ["w_in", "w_out", "ffn2_w_gate", "ffn2_w_up", "ffn2_w_down"]
    first = _gather_two_level([block_to_send(n) for n in names1], "gather_ffn1", ffn1_norm)
    wgt1, wut1, wd1 = [g.reshape(-1, D) for g in first]
    h_rest, token = _exchange_start([block_to_send(n) for n in names_rest[:2]] + [conv_w[0]]
                                    + [block_to_send(n) for n in names_rest[2:]], False,
                                    "gather_start_rest", wd1)

    wgu1 = _stack_gate_up(wgt1, wut1, tf)
    x1, xn1, gu1 = _ffn_fwd(x0, ffn1_norm, wgu1, wd1, token, "ffn1_fwd", tm, tf)
    mixw = _exchange_wait(h_rest[:3], False, "gather_wait_mix", x1)
    win_t = mixw[0].reshape(-1, D)
    wout = mixw[1].reshape(-1, D)
    cw = _columns_of_blocks(mixw[2])
    hn, q, k, v, u, gb, gc = _mixin_fwd(x1, mix_norm, win_t, "mixin_fwd", tm_mix)
    bucket = jnp.asarray(_bucket_table().T.copy())
    sinks = attn_sinks.reshape(-1)
    bias_t = _bias_build(rel_bias_table, bucket, "bias_build")
    attn_raw, attn_n = _attn_fwd(q, k, v, bias_t, sinks, attn_out_norm, "attn_fwd", nblk)
    x2, conv_n = _mixout_fwd(x1, attn_n, u, gb, gc, cw, conv_out_norm, wout, "mixout_fwd", tm_mix)
    wgt2, wut2, wd2 = [g.reshape(-1, D) for g in _exchange_wait(h_rest[3:], False, "gather_wait_ffn2", x2)]
    wgu2 = _stack_gate_up(wgt2, wut2, tf)
    dx3, xn2, gu2, d_final, loss_part = _ffn_fwd(x2, ffn2_norm, wgu2, wd2, x2, "ffn2_fwd", tm, tf,
                                                 head=(target, final_norm.reshape(1, D)))

    def blocks(g):
        return g.reshape(N_DEV, -1, D)

    dx2, d_ffn2_norm, dg2, du2, hh2, do2 = _ffn_bwd(
        dx3, x2, ffn2_norm, gu2, wgu2, wd2, dx3, "ffn2_bwd", tm_bwd, tf)
    d_wg2 = _tn_grad(dg2, xn2, dx2, "ffn2_wgrad_gate", BM_WGRAD, tk)
    d_wu2 = _tn_grad(du2, xn2, dx2, "ffn2_wgrad_up", BM_WGRAD, tk)
    d_wd2 = _tn_grad(hh2, do2, dx2, "ffn2_wgrad_down", BM_WGRAD, tk)
    handles2, token2 = _exchange_start([blocks(d_wg2), blocks(d_wu2), blocks(d_wd2)], True,
                                       "grads_start_ffn2", dx2)

    dmixed, d_wout = _mixout_bwd(dx2, attn_n, conv_n, wout, token2, "mixout_bwd", tm_mix)
    dq, dk, dv, dbias, dsink, d_attn_norm = _attn_bwd(
        dmixed, attn_raw, q, k, v, bias_t, sinks, attn_out_norm, "attn_bwd", nblk)
    du, dgb, dgc, d_cw, d_conv_norm = _conv_bwd(dmixed, u, gb, gc, cw, conv_out_norm, "conv_bwd", tm_mix)
    d_table = _bias_grad(dbias, bucket, "bias_grad")
    dz = [dq, dk, dv, du, dgb, dgc]
    dx1, d_mix_norm = _mixin_bwd(dx2, x1, mix_norm, dz, win_t, "mixin_bwd", tm_mix)
    d_win_t = _win_grad(dz, hn, "win_grad", min(TM_MIX, T))
    handles_mix, token_mix = _exchange_start([blocks(d_win_t), blocks(d_wout)], True, "grads_start_mix", d_table)

    dx0, d_ffn1_norm, dg1, du1, hh1, do1 = _ffn_bwd(
        dx1, x0, ffn1_norm, gu1, wgu1, wd1, token_mix, "ffn1_bwd", tm_bwd, tf)
    def pack(ffn1, mixn, ffn2, fin, attn_n_, conv_n_, sink_, extra, convw, table):
        rows = [_pad_row(ffn1), _pad_row(mixn), _pad_row(ffn2), _pad_row(fin),
                _pad_row(jnp.concatenate([attn_n_.reshape(-1), conv_n_.reshape(-1)])),
                _pad_row(sink_), _pad_row(extra),
                jnp.zeros((1, PACK_COLS), F32),
                jnp.pad(convw, ((0, 0), (0, PACK_COLS - convw.shape[1]))),
                _pad_row(table),
                jnp.zeros((PACK_ROWS - 12, PACK_COLS), F32)]
        return jnp.concatenate(rows, axis=0)

    def own_channels(a):
        full = jnp.zeros((a.shape[1], CONV_DIM), F32)
        return lax.dynamic_update_slice(full, a[0], (0, me * a.shape[2]))

    g_pack = pack(d_ffn1_norm, d_mix_norm, d_ffn2_norm, d_final, d_attn_norm, d_conv_norm,
                  dsink[:, 0], loss_part[0, :1], d_cw[:3], d_table[:, :N_Q_HEADS])
    zero1 = jnp.zeros((1,), F32)
    w_pack = pack(ffn1_norm, mix_norm, ffn2_norm, final_norm, attn_out_norm, conv_out_norm,
                  attn_sinks, zero1, own_channels(conv_w), rel_bias_table)
    m_pack = pack(m_ffn1_norm, m_mix_norm, m_ffn2_norm, m_final_norm, m_attn_out_norm, m_conv_out_norm,
                  m_attn_sinks, zero1, own_channels(m_conv_w), m_rel_bias_table)
    v_pack = pack(v_ffn1_norm, v_mix_norm, v_ffn2_norm, v_final_norm, v_attn_out_norm, v_conv_out_norm,
                  v_attn_sinks, zero1, own_channels(v_conv_w), v_rel_bias_table)
    (g_all,) = _exchange([g_pack], False, "gather_small", dx0)
    packs = _adamw(g_all, w_pack, m_pack, v_pack, "adamw_small", PACK_ROWS)

    d_wd1 = _tn_grad(hh1, do1, packs[0], "ffn1_wgrad_down", BM_WGRAD, tk)
    handles1_d, token1 = _exchange_start([blocks(d_wd1)], True, "grads_start_ffn1_down", dx0)
    d_wg1 = _tn_grad(dg1, xn1, token1, "ffn1_wgrad_gate", BM_WGRAD, tk)
    handles1_g, token1 = _exchange_start([blocks(d_wg1)], True, "grads_start_ffn1_gate", token1)
    d_wu1 = _tn_grad(du1, xn1, token1, "ffn1_wgrad_up", BM_WGRAD, tk)
    handles1_u, token1 = _exchange_start([blocks(d_wu1)], True, "grads_start_ffn1_up", token1)

    res = {}

    def update(names, parts):
        last = None
        for name, p in zip(names, parts):
            w, m_, v_, transposed = big[name]
            if transposed:
                w, m_, v_ = w.T, m_.T, v_.T
            new = _adamw(p, w, m_, v_, "adamw_" + name, _row_tile(w.shape[0], ADAM_ROWS))
            res[name] = tuple((a.T if transposed else a)[None] for a in new)
            last = new[0]
        return last

    parts2 = _exchange_wait(handles2, True, "grads_wait_ffn2", token1)
    done2 = update(["ffn2_w_gate", "ffn2_w_up", "ffn2_w_down"], parts2)
    parts_mix = _exchange_wait(handles_mix, True, "grads_wait_mix", done2)
    done_mix = update(["w_in", "w_out"], parts_mix)

    parts1_d = _exchange_wait(handles1_d, True, "grads_wait_ffn1_down", done_mix)
    done1 = update(["ffn1_w_down"], parts1_d)
    parts1_g = _exchange_wait(handles1_g, True, "grads_wait_ffn1_gate", done1)
    done1 = update(["ffn1_w_gate"], parts1_g)
    parts1_u = _exchange_wait(handles1_u, True, "grads_wait_ffn1_up", done1)
    update(["ffn1_w_up"], parts1_u)

    def unpack(pk):
        cwb = lax.dynamic_slice(pk[8:11, :CONV_DIM], (0, me * conv_w.shape[2]), (3, conv_w.shape[2]))
        return {
            "ffn1_norm": pk[0:1, :D], "mix_norm": pk[1:2, :D], "ffn2_norm": pk[2:3, :D],
            "final_norm": pk[3, :D],
            "attn_out_norm": pk[4:5, :ATTN_WIDTH], "conv_out_norm": pk[4:5, ATTN_WIDTH:ATTN_WIDTH + CONV_DIM],
            "attn_sinks": pk[5:6, :N_Q_HEADS],
            "conv_w": cwb[None],
            "rel_bias_table": pk[11, :NUM_BUCKETS * N_Q_HEADS].reshape(NUM_BUCKETS, N_Q_HEADS),
        }

    small = [unpack(pk) for pk in packs]
    loss = packs[0][6, 0]

    order = ["rel_bias_table", "ffn1_norm", "ffn1_w_gate", "ffn1_w_up", "ffn1_w_down", "mix_norm", "w_in",
             "conv_w", "attn_sinks", "attn_out_norm", "conv_out_norm", "w_out", "ffn2_norm",
             "ffn2_w_gate", "ffn2_w_up", "ffn2_w_down", "final_norm"]
    outs = [loss, dx0[None]]
    for kind in range(4):
        for name in order:
            outs.append(res[name][kind] if name in res else small[kind][name])
    return tuple(outs)
```

```python
import math

import numpy as np
import jax
import jax.numpy as jnp
from jax import lax
from jax.experimental import pallas as pl
from jax.experimental.pallas import tpu as pltpu

F32 = jnp.float32
BF16 = jnp.bfloat16

N_DEV = 8
EPS = 1e-6
HEAD_DIM = 64
N_Q_HEADS = 8
N_KV_HEADS = 2
GQA_GROUP = 4
ATTN_WIDTH = 512
KV_WIDTH = 128
CONV_DIM = 512
BLOCK = 128
WINDOW = 128
NUM_BUCKETS = 32
MAX_DISTANCE = 128
SCALE = HEAD_DIM ** -0.5
MASKED = -1e30
GROUP_ROWS = GQA_GROUP * BLOCK

ADAM_LR = 0.001
ADAM_B1 = 0.9
ADAM_B2 = 0.999
ADAM_EPS = 1e-08
ADAM_WD = 0.01
ADAM_STEP = 10

VMEM_LIMIT_BYTES = 40 * 1024 * 1024
VMEM_LIMIT_FFN_BYTES = 50 * 1024 * 1024
SUBLANES = 8
PACK_ROWS = 16
PACK_COLS = 1024

TM_FFN = 1024
TM_FFN_BWD = 1024
TM_MIX = 512
TK_WGRAD = 2048
TF_FFN = 256
BM_WGRAD = 1408
ROW_GROUPS = 4
ATTN_BLOCKS = 8
ADAM_ROWS = 256


def _row_tile(rows, limit):
    best = rows
    for t in range(16, min(rows, limit) + 1, 16):
        if rows % t == 0:
            best = t
    return best


def _params(*sem):
    return pltpu.CompilerParams(dimension_semantics=sem, vmem_limit_bytes=VMEM_LIMIT_BYTES)


def _hbm(*arrays):
    return [pltpu.with_memory_space_constraint(a, pltpu.HBM) for a in arrays]


def _dot(a, b):
    return jnp.dot(a, b, preferred_element_type=F32)


def _dot_nt(a, b):
    return lax.dot_general(a, b, (((1,), (1,)), ((), ())), preferred_element_type=F32)


def _dot_tn(a, b):
    return lax.dot_general(a, b, (((0,), (0,)), ((), ())), preferred_element_type=F32)


def _sigmoid(g):
    return 0.5 * jnp.tanh(0.5 * g) + 0.5


def _rms_stats(x):
    inv = lax.rsqrt(jnp.mean(x * x, axis=-1, keepdims=True) + EPS)
    return inv, x * inv


def _rms_bwd(dy, x, gain):
    inv, xhat = _rms_stats(x)
    dgain = jnp.sum(dy * xhat, axis=0, keepdims=True)
    dxh = dy * gain
    dx = inv * (dxh - xhat * jnp.mean(dxh * xhat, axis=-1, keepdims=True))
    return dx, dgain


def _peer_list():
    x, y, c = lax.axis_index("x"), lax.axis_index("y"), lax.axis_index("c")
    peers = []
    for k in range(1, N_DEV):
        px = 1 - x if (k >> 2) & 1 else x
        py = 1 - y if (k >> 1) & 1 else y
        pc = 1 - c if k & 1 else c
        peers.append((px, py, pc))
    return 4 * x + 2 * y + c, peers


def _exchange(arrs, scatter, name, after):
    n = len(arrs)
    out_shape = []
    for a in arrs:
        shp = a.shape if scatter else (N_DEV,) + a.shape
        out_shape.append(jax.ShapeDtypeStruct(shp, a.dtype))

    def body(*refs):
        ins, outs = refs[:n], refs[n + 1:2 * n + 1]
        send_sems, recv_sems, local_sems = refs[2 * n + 1:]
        me, peers = _peer_list()
        started = []
        for a in range(n):
            own = ins[a].at[me] if scatter else ins[a]
            loc = pltpu.make_async_copy(own, outs[a].at[me], local_sems.at[a])
            loc.start()
            started.append(loc)
        sends = []
        for a in range(n):
            for k, (px, py, pc) in enumerate(peers):
                src = ins[a].at[4 * px + 2 * py + pc] if scatter else ins[a]
                cp = pltpu.make_async_remote_copy(
                    src_ref=src, dst_ref=outs[a].at[me],
                    send_sem=send_sems.at[a, k], recv_sem=recv_sems.at[a, k],
                    device_id=(px, py, pc), device_id_type=pl.DeviceIdType.MESH)
                cp.start()
                sends.append(cp)
        for a in range(n):
            for k, (px, py, pc) in enumerate(peers):
                landed = outs[a].at[4 * px + 2 * py + pc]
                pltpu.make_async_remote_copy(
                    src_ref=landed, dst_ref=landed,
                    send_sem=send_sems.at[a, k], recv_sem=recv_sems.at[a, k],
                    device_id=(px, py, pc), device_id_type=pl.DeviceIdType.MESH).wait_recv()
        for cp in sends:
            cp.wait_send()
        for loc in started:
            loc.wait()

    hbm = pl.BlockSpec(memory_space=pl.ANY)
    return pl.pallas_call(
        body, name=name, out_shape=tuple(out_shape),
        in_specs=[hbm] * (n + 1), out_specs=tuple([hbm] * n),
        scratch_shapes=[pltpu.SemaphoreType.DMA((n, N_DEV - 1)),
                        pltpu.SemaphoreType.DMA((n, N_DEV - 1)),
                        pltpu.SemaphoreType.DMA((n,))],
    )(*arrs, after)


def _gather_two_level(arrs, name, after):
    n = len(arrs)
    out_shape = tuple(jax.ShapeDtypeStruct((N_DEV,) + a.shape, a.dtype) for a in arrs)

    def body(*refs):
        ins, outs = refs[:n], refs[n + 1:2 * n + 1]
        send_sems, recv_sems, local_sems = refs[2 * n + 1:]
        x, y, c = lax.axis_index("x"), lax.axis_index("y"), lax.axis_index("c")
        me, sibling = (x, y, c), (x, y, 1 - c)
        chips = [(1 - x, y), (x, 1 - y), (1 - x, 1 - y)]

        def copy(a, k, block, to, src=None):
            slot = outs[a].at[4 * block[0] + 2 * block[1] + block[2]]
            return pltpu.make_async_remote_copy(
                src_ref=slot if src is None else src, dst_ref=slot,
                send_sem=send_sems.at[a, k], recv_sem=recv_sems.at[a, k],
                device_id=to, device_id_type=pl.DeviceIdType.MESH)

        started = []
        for a in range(n):
            loc = pltpu.make_async_copy(ins[a], outs[a].at[4 * x + 2 * y + c], local_sems.at[a])
            loc.start()
            started.append(loc)
        sends = []
        for a in range(n):
            sends.append(copy(a, 0, me, sibling, src=ins[a]))
            sends += [copy(a, 1 + j, me, (*chip, c), src=ins[a]) for j, chip in enumerate(chips)]
        for cp in sends:
            cp.start()
        for j, chip in enumerate(chips):
            for a in range(n):
                copy(a, 1 + j, (*chip, c), me).wait_recv()
                fwd = copy(a, 4 + j, (*chip, c), sibling)
                fwd.start()
                sends.append(fwd)
        for a in range(n):
            copy(a, 0, sibling, me).wait_recv()
            for j, chip in enumerate(chips):
                copy(a, 4 + j, (*chip, 1 - c), me).wait_recv()
        for cp in sends:
            cp.wait_send()
        for loc in started:
            loc.wait()

    hbm = pl.BlockSpec(memory_space=pl.ANY)
    return pl.pallas_call(
        body, name=name, out_shape=out_shape,
        in_specs=[hbm] * (n + 1), out_specs=tuple([hbm] * n),
        scratch_shapes=[pltpu.SemaphoreType.DMA((n, N_DEV - 1)),
                        pltpu.SemaphoreType.DMA((n, N_DEV - 1)),
                        pltpu.SemaphoreType.DMA((n,))],
    )(*arrs, after)


_HBM = pl.BlockSpec(memory_space=pltpu.HBM)
_SEM = pl.BlockSpec(memory_space=pltpu.SEMAPHORE)
_EFFECT = pltpu.SideEffectType.DATAFLOW_SIDE_EFFECTING


def _split_copies(srcs, lands, send_sems, recv_sems, scatter):
    me, peers = _peer_list()
    copies = []
    for a in range(len(srcs)):
        for k, (px, py, pc) in enumerate(peers):
            src = srcs[a].at[4 * px + 2 * py + pc] if scatter else srcs[a]
            copies.append(pltpu.make_async_remote_copy(
                src_ref=src, dst_ref=lands[a].at[me],
                send_sem=send_sems[a].at[k], recv_sem=recv_sems[a].at[k],
                device_id=(px, py, pc), device_id_type=pl.DeviceIdType.MESH))
    return copies


def _own_copies(srcs, lands, recv_sems, scatter):
    me, _ = _peer_list()
    return [pltpu.make_async_copy(srcs[a].at[me] if scatter else srcs[a], lands[a].at[me],
                                  recv_sems[a].at[N_DEV - 1]) for a in range(len(srcs))]


def _exchange_start(arrs, scatter, name, after):
    n = len(arrs)
    lands = [lax.empty(a.shape if scatter else (N_DEV,) + a.shape, a.dtype) for a in arrs]

    def body(*refs):
        srcs, lnds = refs[:n], refs[n:2 * n]
        outs = refs[2 * n + 1:]
        send_sems, recv_sems = outs[:n], outs[n:2 * n]
        token = outs[4 * n]
        for cp in _split_copies(srcs, lnds, send_sems, recv_sems, scatter):
            cp.start()
        for cp in _own_copies(srcs, lnds, recv_sems, scatter):
            cp.start()
        token[...] = jnp.zeros_like(token)

    sem = pltpu.SemaphoreType.DMA((N_DEV,))
    out_shape = ([sem] * (2 * n) + [pltpu.HBM(a.shape, a.dtype) for a in arrs]
                 + [pltpu.HBM(l.shape, l.dtype) for l in lands] + [jax.ShapeDtypeStruct((SUBLANES, 128), F32)])
    res = pl.pallas_call(
        body, name=name, out_shape=tuple(out_shape),
        in_specs=[_HBM] * (2 * n) + [pl.BlockSpec(memory_space=pl.ANY)],
        out_specs=tuple([_SEM] * (2 * n) + [_HBM] * (2 * n) + [pl.BlockSpec(memory_space=pltpu.VMEM)]),
        input_output_aliases={i: 2 * n + i for i in range(2 * n)},
        compiler_params=pltpu.CompilerParams(has_side_effects=_EFFECT),
    )(*[pltpu.with_memory_space_constraint(a, pltpu.HBM) for a in arrs],
      *[pltpu.with_memory_space_constraint(l, pltpu.HBM) for l in lands], after)
    handles = [(res[2 * n + a], res[3 * n + a], res[a], res[n + a]) for a in range(n)]
    return handles, res[4 * n]


def _exchange_wait(handles, scatter, name, after):
    n = len(handles)

    def body(*refs):
        srcs, lnds = refs[:n], refs[n:2 * n]
        send_sems, recv_sems = refs[2 * n:3 * n], refs[3 * n:4 * n]
        for cp in _split_copies(srcs, lnds, send_sems, recv_sems, scatter):
            cp.wait_send()
            cp.wait_recv()
        for cp in _own_copies(srcs, lnds, recv_sems, scatter):
            cp.wait()

    srcs = [h[0] for h in handles]
    lands = [h[1] for h in handles]
    res = pl.pallas_call(
        body, name=name,
        out_shape=tuple([pltpu.HBM(a.shape, a.dtype) for a in srcs] + [pltpu.HBM(l.shape, l.dtype) for l in lands]),
        in_specs=[_HBM] * (2 * n) + [_SEM] * (2 * n) + [pl.BlockSpec(memory_space=pl.ANY)],
        out_specs=tuple([_HBM] * (2 * n)),
        input_output_aliases={i: i for i in range(2 * n)},
        compiler_params=pltpu.CompilerParams(has_side_effects=_EFFECT),
    )(*srcs, *lands, *[h[2] for h in handles], *[h[3] for h in handles], after)
    return list(res[n:])


def _row_groups(tm):
    return [slice(r * (tm // ROW_GROUPS), (r + 1) * (tm // ROW_GROUPS)) for r in range(ROW_GROUPS)]


def _stack_gate_up(wgt, wut, tf):
    F, D = wgt.shape
    return jnp.stack([wgt.reshape(F // tf, tf, D), wut.reshape(F // tf, tf, D)], axis=1).reshape(2 * F, D)


def _ffn_fwd(x, gain, wgu, wd, after, name, tm, tf, head=None):
    T, D = x.shape
    F = wd.shape[0]
    nj = F // tf
    n_in = 7 if head else 5

    def body(*refs):
        x_ref, g_ref, wgu_ref, wd_ref, after_ref = refs[:5]
        outs = refs[n_in:]
        xo_ref, xn_ref, gu_ref = outs[:3]
        xn_sc, acc_sc = outs[-2:]
        i, j = pl.program_id(0), pl.program_id(1)

        @pl.when(j == 0)
        def _():
            _, xhat = _rms_stats(x_ref[...])
            xn = (xhat * g_ref[...]).astype(BF16)
            xn_sc[...] = xn
            xn_ref[...] = xn
            acc_sc[...] = jnp.zeros_like(acc_sc)

        groups = _row_groups(tm)
        gus = [_dot_nt(xn_sc[rows, :], wgu_ref[...]) for rows in groups]
        hs = []
        for rows, gu in zip(groups, gus):
            gu_ref[rows, :] = gu.astype(BF16)
            g, u = gu[:, :tf], gu[:, tf:]
            hs.append((g * _sigmoid(g) * u).astype(BF16))
        for rows, h in zip(groups, hs):
            acc_sc[rows, :] += _dot(h, wd_ref[...])

        if head:
            t_ref, fg_ref = refs[5:7]
            dgain_ref, loss_ref = outs[3:5]

            @pl.when((i == 0) & (j == 0))
            def _():
                dgain_ref[...] = jnp.zeros_like(dgain_ref)
                loss_ref[...] = jnp.zeros_like(loss_ref)

        @pl.when(j == nj - 1)
        def _():
            for rows in groups:
                xo = x_ref[rows, :] + 0.5 * acc_sc[rows, :]
                if head:
                    fg = fg_ref[...]
                    _, xhat = _rms_stats(xo)
                    err = xhat * fg - t_ref[rows, :]
                    loss_ref[...] += 0.5 * jnp.sum(jnp.mean(err * err, axis=-1, keepdims=True),
                                                   axis=0, keepdims=True)
                    dx, dgain = _rms_bwd(err * (1.0 / D), xo, fg)
                    dgain_ref[...] += dgain
                    xo_ref[rows, :] = dx
                else:
                    xo_ref[rows, :] = xo

    tile = pl.BlockSpec((tm, D), lambda i, j: (i, 0))
    const = pl.BlockSpec((1, D), lambda i, j: (0, 0))
    in_specs = [tile, const, pl.BlockSpec((2 * tf, D), lambda i, j: (j, 0)),
                pl.BlockSpec((tf, D), lambda i, j: (j, 0)), pl.BlockSpec(memory_space=pl.ANY)]
    out_specs = [tile, pl.BlockSpec((tm, D), lambda i, j: (i, 0)), pl.BlockSpec((tm, 2 * tf), lambda i, j: (i, j))]
    out_shape = [jax.ShapeDtypeStruct((T, D), F32), jax.ShapeDtypeStruct((T, D), BF16),
                 jax.ShapeDtypeStruct((T, 2 * F), BF16)]
    operands = [x, gain, wgu, wd, after]
    if head:
        in_specs += [tile, const]
        out_specs += [const, pl.BlockSpec((SUBLANES, 128), lambda i, j: (0, 0))]
        out_shape += [jax.ShapeDtypeStruct((1, D), F32), jax.ShapeDtypeStruct((SUBLANES, 128), F32)]
        operands += list(head)
    return pl.pallas_call(
        body, name=name, grid=(T // tm, nj),
        in_specs=in_specs, out_specs=tuple(out_specs), out_shape=tuple(out_shape),
        scratch_shapes=[pltpu.VMEM((tm, D), BF16), pltpu.VMEM((tm, D), F32)],
        compiler_params=pltpu.CompilerParams(dimension_semantics=("arbitrary", "arbitrary"),
                                             vmem_limit_bytes=VMEM_LIMIT_FFN_BYTES),
    )(*_hbm(*operands))


def _ffn_bwd(dy, x, gain, gu, wgu, wd, after, name, tm, tf):
    T, D = x.shape
    F = wd.shape[0]
    nj = F // tf

    def body(dy_ref, x_ref, g_ref, gu_ref, wgu_ref, wd_ref, after_ref,
             dx_ref, dgain_ref, dg_ref, du_ref, hh_ref, do_ref, do_sc, acc_sc, dgu_sc):
        i, j = pl.program_id(0), pl.program_id(1)

        @pl.when((i == 0) & (j == 0))
        def _():
            dgain_ref[...] = jnp.zeros_like(dgain_ref)

        @pl.when(j == 0)
        def _():
            do = (0.5 * dy_ref[...]).astype(BF16)
            do_sc[...] = do
            do_ref[...] = do
            acc_sc[...] = jnp.zeros_like(acc_sc)

        groups = _row_groups(tm)
        dhs = [_dot_nt(do_sc[rows, :], wd_ref[...]) for rows in groups]
        for rows, dh in zip(groups, dhs):
            g = gu_ref[rows, :tf].astype(F32)
            u = gu_ref[rows, tf:].astype(F32)
            sig = _sigmoid(g)
            s = g * sig
            dg = (dh * u * (sig + s * (1.0 - sig))).astype(BF16)
            du = (dh * s).astype(BF16)
            dg_ref[rows, :] = dg
            du_ref[rows, :] = du
            dgu_sc[rows, :tf] = dg
            dgu_sc[rows, tf:] = du
            hh_ref[rows, :] = (s * u).astype(BF16)
        for rows in groups:
            acc_sc[rows, :] += _dot(dgu_sc[rows, :], wgu_ref[...])

        @pl.when(j == nj - 1)
        def _():
            for rows in groups:
                dx, dgain = _rms_bwd(acc_sc[rows, :], x_ref[rows, :], g_ref[...])
                dgain_ref[...] += dgain
                dx_ref[rows, :] = dy_ref[rows, :] + dx

    tile_in = pl.BlockSpec((tm, D), lambda i, j: (i, 0))
    tile = pl.BlockSpec((tm, D), lambda i, j: (i, 0), pipeline_mode=pl.Buffered(1))
    return pl.pallas_call(
        body, name=name, grid=(T // tm, nj),
        in_specs=[tile_in, tile_in,
                  pl.BlockSpec((1, D), lambda i, j: (0, 0)),
                  pl.BlockSpec((tm, 2 * tf), lambda i, j: (i, j)),
                  pl.BlockSpec((2 * tf, D), lambda i, j: (j, 0)),
                  pl.BlockSpec((tf, D), lambda i, j: (j, 0)),
                  pl.BlockSpec(memory_space=pl.ANY)],
        out_specs=(tile,
                   pl.BlockSpec((1, D), lambda i, j: (0, 0)),
                   pl.BlockSpec((tm, tf), lambda i, j: (i, j)),
                   pl.BlockSpec((tm, tf), lambda i, j: (i, j)),
                   pl.BlockSpec((tm, tf), lambda i, j: (i, j)),
                   pl.BlockSpec((tm, D), lambda i, j: (i, 0))),
        out_shape=(jax.ShapeDtypeStruct((T, D), F32), jax.ShapeDtypeStruct((1, D), F32),
                   jax.ShapeDtypeStruct((T, F), BF16), jax.ShapeDtypeStruct((T, F), BF16),
                   jax.ShapeDtypeStruct((T, F), BF16), jax.ShapeDtypeStruct((T, D), BF16)),
        scratch_shapes=[pltpu.VMEM((tm, D), BF16), pltpu.VMEM((tm, D), F32), pltpu.VMEM((tm, 2 * tf), BF16)],
        compiler_params=pltpu.CompilerParams(dimension_semantics=("arbitrary", "arbitrary"),
                                             vmem_limit_bytes=VMEM_LIMIT_FFN_BYTES),
    )(*_hbm(dy, x, gain, gu, wgu, wd, after))


def _tn_grad(a, b, after, name, bm, tk):
    T, M = a.shape
    D = b.shape[1]
    nk = T // tk

    def body(a_ref, b_ref, after_ref, o_ref, acc_sc):
        k = pl.program_id(1)

        @pl.when(k == 0)
        def _():
            acc_sc[...] = jnp.zeros_like(acc_sc)

        acc_sc[...] += _dot_tn(a_ref[...], b_ref[...])

        @pl.when(k == nk - 1)
        def _():
            o_ref[...] = acc_sc[...].astype(BF16)

    return pl.pallas_call(
        body, name=name, grid=(M // bm, nk),
        in_specs=[pl.BlockSpec((tk, bm), lambda i, k: (k, i)), pl.BlockSpec((tk, D), lambda i, k: (k, 0)),
                  pl.BlockSpec(memory_space=pl.ANY)],
        out_specs=pl.BlockSpec((bm, D), lambda i, k: (i, 0)),
        out_shape=jax.ShapeDtypeStruct((M, D), BF16),
        scratch_shapes=[pltpu.VMEM((bm, D), F32)],
        compiler_params=_params("arbitrary", "arbitrary"),
    )(*_hbm(a, b, after))


_Z_SPLITS = (0, 512, 640, 768, 1280, 1792, 2304)


def _mixin_fwd(x, gain, w_in_t, name, tm):
    T, D = x.shape
    widths = [b - a for a, b in zip(_Z_SPLITS[:-1], _Z_SPLITS[1:])]

    def body(x_ref, g_ref, w_ref, hn_ref, *outs):
        _, xhat = _rms_stats(x_ref[...])
        hn = (xhat * g_ref[...]).astype(BF16)
        hn_ref[...] = hn
        for n in (0, 3, 4, 5):
            outs[n][...] = _dot_nt(hn, w_ref[_Z_SPLITS[n]:_Z_SPLITS[n + 1], :]).astype(BF16)
        kv = _dot_nt(hn, w_ref[_Z_SPLITS[1]:_Z_SPLITS[3], :]).astype(BF16)
        outs[1][...] = kv[:, :KV_WIDTH]
        outs[2][...] = kv[:, KV_WIDTH:]

    return pl.pallas_call(
        body, name=name, grid=(T // tm,),
        in_specs=[pl.BlockSpec((tm, D), lambda i: (i, 0)),
                  pl.BlockSpec((1, D), lambda i: (0, 0)),
                  pl.BlockSpec(w_in_t.shape, lambda i: (0, 0))],
        out_specs=tuple([pl.BlockSpec((tm, D), lambda i: (i, 0))]
                        + [pl.BlockSpec((tm, w), lambda i: (i, 0)) for w in widths]),
        out_shape=tuple([jax.ShapeDtypeStruct((T, D), BF16)]
                        + [jax.ShapeDtypeStruct((T, w), BF16) for w in widths]),
        compiler_params=_params("arbitrary"),
    )(*_hbm(x, gain, w_in_t))


def _bucket_table():
    qi = np.arange(BLOCK, dtype=np.int32)[:, None]
    kj = np.arange(2 * BLOCK, dtype=np.int32)[None, :]
    dist = qi + BLOCK - kj
    n = np.maximum(dist, 0)
    max_exact = NUM_BUCKETS // 2
    large = max_exact + (np.log(np.maximum(n, 1).astype(np.float32) / max_exact)
                         / math.log(MAX_DISTANCE / max_exact)
                         * (NUM_BUCKETS - max_exact)).astype(np.int32)
    large = np.minimum(large, NUM_BUCKETS - 1)
    bucket = np.where(n < max_exact, n, large).astype(np.int32)
    valid = (dist >= 0) & (dist < WINDOW)
    return np.where(valid, bucket, -1).astype(np.int32)


def _bias_build(table, bucket, name):
    def body(t_ref, b_ref, o_ref):
        bk = b_ref[...]
        for h in range(N_Q_HEADS):
            def step(b, acc):
                return jnp.where(bk == b, t_ref[b, h], acc)
            o_ref[h] = lax.fori_loop(0, NUM_BUCKETS, step, jnp.full(bk.shape, MASKED, F32))

    return pl.pallas_call(
        body, name=name,
        in_specs=[pl.BlockSpec(memory_space=pltpu.SMEM), pl.BlockSpec(memory_space=pltpu.VMEM)],
        out_specs=pl.BlockSpec(memory_space=pltpu.VMEM),
        out_shape=jax.ShapeDtypeStruct((N_Q_HEADS,) + bucket.shape, F32),
    )(table, bucket)


def _bias_grad(dbias, bucket, name):
    def body(d_ref, b_ref, o_ref):
        bk = b_ref[...]
        row = lax.broadcasted_iota(jnp.int32, o_ref.shape, 0)
        lane = lax.broadcasted_iota(jnp.int32, o_ref.shape, 1)
        res = jnp.zeros(o_ref.shape, F32)
        for h in range(N_Q_HEADS):
            d = d_ref[h]

            def step(b, acc):
                part = jnp.sum(jnp.where(bk == b, d, 0.0), axis=0, keepdims=True)
                return jnp.where(row == b, part, acc)
            per_lane = lax.fori_loop(0, NUM_BUCKETS, step, jnp.zeros(o_ref.shape, F32))
            res = jnp.where(lane == h, jnp.sum(per_lane, axis=1, keepdims=True), res)
        o_ref[...] = res

    return pl.pallas_call(
        body, name=name,
        in_specs=[pl.BlockSpec(memory_space=pltpu.VMEM), pl.BlockSpec(memory_space=pltpu.VMEM)],
        out_specs=pl.BlockSpec(memory_space=pltpu.VMEM),
        out_shape=jax.ShapeDtypeStruct((NUM_BUCKETS, 128), F32),
    )(*_hbm(dbias, bucket))


def _head_cols(h):
    return slice(h * HEAD_DIM, (h + 1) * HEAD_DIM)


def _stack_heads(ref, r0, g, dtype):
    return jnp.concatenate(
        [ref[pl.ds(r0, BLOCK), _head_cols(GQA_GROUP * g + j)].astype(dtype) for j in range(GQA_GROUP)], axis=0)


def _unstack_heads(ref, r0, g, val):
    for j in range(GQA_GROUP):
        ref[pl.ds(r0, BLOCK), _head_cols(GQA_GROUP * g + j)] = val[j * BLOCK:(j + 1) * BLOCK, :]


def _head_lanes(h):
    return slice(h * BLOCK, (h + 1) * BLOCK)


def _group_lanes(g):
    return slice(g * GROUP_ROWS, (g + 1) * GROUP_ROWS)


def _head_softmax(st, bias_t, sink, no_prev):
    s = st * SCALE + bias_t
    row = lax.broadcasted_iota(jnp.int32, s.shape, 0)
    s = jnp.where(no_prev & (row < BLOCK), MASKED, s)
    m = jnp.maximum(jnp.max(s, axis=0, keepdims=True), sink)
    p = jnp.exp(s - m)
    ps = jnp.exp(sink - m)
    r = 1.0 / (jnp.sum(p, axis=0, keepdims=True) + ps)
    return p * r, ps * r


def _load_band(kf_sc, vf_sc, kp_ref, kc_ref, vp_ref, vc_ref, tq):
    kf_sc[0:BLOCK, :] = kp_ref[...]
    kf_sc[BLOCK:BLOCK + tq, :] = kc_ref[...]
    vf_sc[0:BLOCK, :] = vp_ref[...]
    vf_sc[BLOCK:BLOCK + tq, :] = vc_ref[...]


def _attn_fwd(q, k, v, bias_t, sinks, gain, name, nblk):
    T = q.shape[0]
    tq = nblk * BLOCK

    def body(sink_ref, q_ref, kc_ref, kp_ref, vc_ref, vp_ref, bias_ref, g_ref, raw_ref, nrm_ref,
             kf_sc, vf_sc, o_sc, st_sc, pt_sc):
        i = pl.program_id(0)
        _load_band(kf_sc, vf_sc, kp_ref, kc_ref, vp_ref, vc_ref, tq)

        def block(b, carry):
            r0 = pl.multiple_of(b * BLOCK, BLOCK)
            no_prev = (i == 0) & (b == 0)
            for g in range(N_KV_HEADS):
                kb = kf_sc[pl.ds(r0, 2 * BLOCK), _head_cols(g)]
                st_sc[:, _group_lanes(g)] = _dot_nt(kb, _stack_heads(q_ref, r0, g, BF16))
            for h in range(N_Q_HEADS):
                p, _ = _head_softmax(st_sc[:, _head_lanes(h)], bias_ref[h], sink_ref[h], no_prev)
                pt_sc[:, _head_lanes(h)] = p.astype(BF16)
            for g in range(N_KV_HEADS):
                vb = vf_sc[pl.ds(r0, 2 * BLOCK), _head_cols(g)]
                _unstack_heads(o_sc, r0, g, _dot_tn(pt_sc[:, _group_lanes(g)], vb))
            return carry

        lax.fori_loop(0, nblk, block, 0, unroll=True)
        o = o_sc[...]
        raw_ref[...] = o.astype(BF16)
        _, ohat = _rms_stats(o)
        nrm_ref[...] = (ohat * g_ref[...]).astype(BF16)

    cur = lambda i: (i, 0)
    prev = lambda i: (jnp.maximum(i * nblk - 1, 0), 0)
    lanes = N_Q_HEADS * BLOCK
    return pl.pallas_call(
        body, name=name, grid=(T // tq,),
        in_specs=[pl.BlockSpec(memory_space=pltpu.SMEM),
                  pl.BlockSpec((tq, ATTN_WIDTH), cur),
                  pl.BlockSpec((tq, KV_WIDTH), cur), pl.BlockSpec((BLOCK, KV_WIDTH), prev),
                  pl.BlockSpec((tq, KV_WIDTH), cur), pl.BlockSpec((BLOCK, KV_WIDTH), prev),
                  pl.BlockSpec(bias_t.shape, lambda i: (0, 0, 0)),
                  pl.BlockSpec((1, ATTN_WIDTH), lambda i: (0, 0))],
        out_specs=(pl.BlockSpec((tq, ATTN_WIDTH), cur), pl.BlockSpec((tq, ATTN_WIDTH), cur)),
        out_shape=(jax.ShapeDtypeStruct((T, ATTN_WIDTH), BF16), jax.ShapeDtypeStruct((T, ATTN_WIDTH), BF16)),
        scratch_shapes=[pltpu.VMEM((tq + BLOCK, KV_WIDTH), BF16), pltpu.VMEM((tq + BLOCK, KV_WIDTH), BF16),
                        pltpu.VMEM((tq, ATTN_WIDTH), F32),
                        pltpu.VMEM((2 * BLOCK, lanes), F32), pltpu.VMEM((2 * BLOCK, lanes), BF16)],
        compiler_params=_params("arbitrary"),
    )(sinks, *_hbm(q, k, k, v, v, bias_t, gain))


def _attn_bwd(dmixed, raw, q, k, v, bias_t, sinks, gain, name, nblk):
    T = q.shape[0]
    tq = nblk * BLOCK
    nt = T // tq
    lanes = N_Q_HEADS * BLOCK

    def body(sink_ref, dm_ref, raw_ref, q_ref, kc_ref, kp_ref, vc_ref, vp_ref, bias_ref, g_ref,
             dq_ref, dk_ref, dv_ref, dbias_ref, dsink_ref, dgain_ref,
             do_sc, dq_sc, kf_sc, vf_sc, dkf_sc, dvf_sc, st_sc, dpt_sc, pt_sc, dst_sc, drow_sc,
             qs_sc, dos_sc, dsink_sc):
        i = pl.program_id(0)
        tile = nt - 1 - i

        @pl.when(i == 0)
        def _():
            dkf_sc[...] = jnp.zeros_like(dkf_sc)
            dvf_sc[...] = jnp.zeros_like(dvf_sc)
            dsink_sc[...] = jnp.zeros_like(dsink_sc)
            dbias_ref[...] = jnp.zeros_like(dbias_ref)
            dgain_ref[...] = jnp.zeros_like(dgain_ref)

        carry_k = dkf_sc[0:BLOCK, :]
        carry_v = dvf_sc[0:BLOCK, :]
        dkf_sc[0:tq, :] = jnp.zeros((tq, KV_WIDTH), F32)
        dvf_sc[0:tq, :] = jnp.zeros((tq, KV_WIDTH), F32)
        dkf_sc[tq:tq + BLOCK, :] = carry_k
        dvf_sc[tq:tq + BLOCK, :] = carry_v
        _load_band(kf_sc, vf_sc, kp_ref, kc_ref, vp_ref, vc_ref, tq)

        do, dgain = _rms_bwd(dm_ref[...].astype(F32), raw_ref[...].astype(F32), g_ref[...])
        dgain_ref[...] += dgain
        do_sc[...] = do
        ones = jnp.ones((SUBLANES, HEAD_DIM), BF16)

        def block(b, carry):
            r0 = pl.multiple_of(b * BLOCK, BLOCK)
            no_prev = (tile == 0) & (b == 0)
            for g in range(N_KV_HEADS):
                kb = kf_sc[pl.ds(r0, 2 * BLOCK), _head_cols(g)]
                vb = vf_sc[pl.ds(r0, 2 * BLOCK), _head_cols(g)]
                qg = _stack_heads(q_ref, r0, g, BF16)
                dog = _stack_heads(do_sc, r0, g, F32)
                prod = dog * _stack_heads(raw_ref, r0, g, F32)
                hi = prod.astype(BF16)
                lo = (prod - hi.astype(F32)).astype(BF16)
                drow_sc[:, _group_lanes(g)] = _dot_nt(ones, hi) + _dot_nt(ones, lo)
                dogb = dog.astype(BF16)
                qs_sc[g] = qg
                dos_sc[g] = dogb
                st_sc[:, _group_lanes(g)] = _dot_nt(kb, qg)
                dpt_sc[:, _group_lanes(g)] = _dot_nt(vb, dogb)
            for h in range(N_Q_HEADS):
                hl = _head_lanes(h)
                p, ps = _head_softmax(st_sc[:, hl], bias_ref[h], sink_ref[h], no_prev)
                rowdot = drow_sc[0:1, hl]
                ds = p * (dpt_sc[:, hl] - rowdot)
                dsink_sc[h:h + 1, :] += -(ps * rowdot)
                dbias_ref[h] += ds
                dst_sc[:, hl] = ds.astype(BF16)
                pt_sc[:, hl] = p.astype(BF16)
            for g in range(N_KV_HEADS):
                kb = kf_sc[pl.ds(r0, 2 * BLOCK), _head_cols(g)]
                dsg = dst_sc[:, _group_lanes(g)]
                _unstack_heads(dq_sc, r0, g, _dot_tn(dsg, kb) * SCALE)
                dkf_sc[pl.ds(r0, 2 * BLOCK), _head_cols(g)] += _dot(dsg, qs_sc[g]) * SCALE
                dvf_sc[pl.ds(r0, 2 * BLOCK), _head_cols(g)] += _dot(pt_sc[:, _group_lanes(g)], dos_sc[g])
            return carry

        lax.fori_loop(0, nblk, block, 0, unroll=True)
        dq_ref[...] = dq_sc[...].astype(BF16)
        dk_ref[...] = dkf_sc[BLOCK:BLOCK + tq, :].astype(BF16)
        dv_ref[...] = dvf_sc[BLOCK:BLOCK + tq, :].astype(BF16)

        @pl.when(i == nt - 1)
        def _():
            tot = jnp.sum(dsink_sc[...], axis=1, keepdims=True)
            dsink_ref[...] = jnp.broadcast_to(tot, dsink_ref.shape)

    cur = lambda i: (nt - 1 - i, 0)
    prev = lambda i: (jnp.maximum((nt - 1 - i) * nblk - 1, 0), 0)
    const2 = lambda i: (0, 0)
    const3 = lambda i: (0, 0, 0)
    return pl.pallas_call(
        body, name=name, grid=(nt,),
        in_specs=[pl.BlockSpec(memory_space=pltpu.SMEM),
                  pl.BlockSpec((tq, ATTN_WIDTH), cur),
                  pl.BlockSpec((tq, ATTN_WIDTH), cur),
                  pl.BlockSpec((tq, ATTN_WIDTH), cur),
                  pl.BlockSpec((tq, KV_WIDTH), cur), pl.BlockSpec((BLOCK, KV_WIDTH), prev),
                  pl.BlockSpec((tq, KV_WIDTH), cur), pl.BlockSpec((BLOCK, KV_WIDTH), prev),
                  pl.BlockSpec(bias_t.shape, const3),
                  pl.BlockSpec((1, ATTN_WIDTH), const2)],
        out_specs=(pl.BlockSpec((tq, ATTN_WIDTH), cur),
                   pl.BlockSpec((tq, KV_WIDTH), cur), pl.BlockSpec((tq, KV_WIDTH), cur),
                   pl.BlockSpec(bias_t.shape, const3),
                   pl.BlockSpec((N_Q_HEADS, 128), const2),
                   pl.BlockSpec((1, ATTN_WIDTH), const2)),
        out_shape=(jax.ShapeDtypeStruct((T, ATTN_WIDTH), BF16),
                   jax.ShapeDtypeStruct((T, KV_WIDTH), BF16), jax.ShapeDtypeStruct((T, KV_WIDTH), BF16),
                   jax.ShapeDtypeStruct(bias_t.shape, F32),
                   jax.ShapeDtypeStruct((N_Q_HEADS, 128), F32),
                   jax.ShapeDtypeStruct((1, ATTN_WIDTH), F32)),
        scratch_shapes=[pltpu.VMEM((tq, ATTN_WIDTH), F32), pltpu.VMEM((tq, ATTN_WIDTH), F32),
                        pltpu.VMEM((tq + BLOCK, KV_WIDTH), BF16), pltpu.VMEM((tq + BLOCK, KV_WIDTH), BF16),
                        pltpu.VMEM((tq + BLOCK, KV_WIDTH), F32), pltpu.VMEM((tq + BLOCK, KV_WIDTH), F32),
                        pltpu.VMEM((2 * BLOCK, lanes), F32), pltpu.VMEM((2 * BLOCK, lanes), F32),
                        pltpu.VMEM((2 * BLOCK, lanes), BF16), pltpu.VMEM((2 * BLOCK, lanes), BF16),
                        pltpu.VMEM((SUBLANES, lanes), F32),
                        pltpu.VMEM((N_KV_HEADS, GROUP_ROWS, HEAD_DIM), BF16),
                        pltpu.VMEM((N_KV_HEADS, GROUP_ROWS, HEAD_DIM), BF16),
                        pltpu.VMEM((N_Q_HEADS, 128), F32)],
        compiler_params=_params("arbitrary"),
    )(sinks, *_hbm(dmixed, raw, q, k, k, v, v, bias_t, gain))


def _shift_down(cu, tail):
    row = lax.broadcasted_iota(jnp.int32, cu.shape, 0)
    t6, t7 = tail[6:7, :], tail[7:8, :]
    s1 = jnp.where(row == 0, t7, pltpu.roll(cu, 1, 0))
    s2 = jnp.where(row == 0, t6, jnp.where(row == 1, t7, pltpu.roll(cu, 2, 0)))
    return s1, s2


def _shift_up(d, head):
    n = d.shape[0]
    row = lax.broadcasted_iota(jnp.int32, d.shape, 0)
    h0, h1 = head[0:1, :], head[1:2, :]
    s1 = jnp.where(row == n - 1, h0, pltpu.roll(d, n - 1, 0))
    s2 = jnp.where(row == n - 1, h1, jnp.where(row == n - 2, h0, pltpu.roll(d, n - 2, 0)))
    return s1, s2


def _mixout_fwd(x, attn_n, u, gb, gc, conv_w, gain, w_out, name, tm):
    T, D = x.shape

    def body(x_ref, an_ref, u_ref, b_ref, c_ref, cw_ref, g_ref, wo_ref, xo_ref, cn_ref, tail_sc):
        @pl.when(pl.program_id(0) == 0)
        def _():
            tail_sc[...] = jnp.zeros_like(tail_sc)

        cu = c_ref[...].astype(F32) * u_ref[...].astype(F32)
        s1, s2 = _shift_down(cu, tail_sc[...])
        tail_sc[...] = cu[tm - SUBLANES:tm, :]
        pre = cw_ref[0:1, :] * s2 + cw_ref[1:2, :] * s1 + cw_ref[2:3, :] * cu
        conv = b_ref[...].astype(F32) * pre
        _, chat = _rms_stats(conv)
        cn = (chat * g_ref[...]).astype(BF16)
        cn_ref[...] = cn
        xo_ref[...] = (x_ref[...] + _dot(an_ref[...], wo_ref[0:ATTN_WIDTH, :])
                       + _dot(cn, wo_ref[ATTN_WIDTH:ATTN_WIDTH + CONV_DIM, :]))

    row = lambda i: (i, 0)
    const = lambda i: (0, 0)
    return pl.pallas_call(
        body, name=name, grid=(T // tm,),
        in_specs=[pl.BlockSpec((tm, D), row), pl.BlockSpec((tm, ATTN_WIDTH), row),
                  pl.BlockSpec((tm, CONV_DIM), row), pl.BlockSpec((tm, CONV_DIM), row),
                  pl.BlockSpec((tm, CONV_DIM), row),
                  pl.BlockSpec(conv_w.shape, const), pl.BlockSpec((1, CONV_DIM), const),
                  pl.BlockSpec(w_out.shape, const)],
        out_specs=(pl.BlockSpec((tm, D), row), pl.BlockSpec((tm, CONV_DIM), row)),
        out_shape=(jax.ShapeDtypeStruct((T, D), F32), jax.ShapeDtypeStruct((T, CONV_DIM), BF16)),
        scratch_shapes=[pltpu.VMEM((SUBLANES, CONV_DIM), F32)],
        compiler_params=_params("arbitrary"),
    )(*_hbm(x, attn_n, u, gb, gc, conv_w, gain, w_out))


def _mixout_bwd(dy, attn_n, conv_n, w_out, after, name, tm):
    T, D = dy.shape
    W = ATTN_WIDTH + CONV_DIM
    nt = T // tm

    def body(dy_ref, an_ref, cn_ref, w_ref, after_ref, dm_ref, dw_ref, dw_sc):
        i = pl.program_id(0)

        @pl.when(i == 0)
        def _():
            dw_sc[...] = jnp.zeros_like(dw_sc)

        dyb = dy_ref[...].astype(BF16)
        dm_ref[...] = _dot_nt(dyb, w_ref[...]).astype(BF16)
        dw_sc[0:ATTN_WIDTH, :] += _dot_tn(an_ref[...], dyb)
        dw_sc[ATTN_WIDTH:W, :] += _dot_tn(cn_ref[...], dyb)

        @pl.when(i == nt - 1)
        def _():
            dw_ref[...] = dw_sc[...].astype(BF16)

    row = lambda i: (i, 0)
    const = lambda i: (0, 0)
    return pl.pallas_call(
        body, name=name, grid=(nt,),
        in_specs=[pl.BlockSpec((tm, D), row), pl.BlockSpec((tm, ATTN_WIDTH), row),
                  pl.BlockSpec((tm, CONV_DIM), row), pl.BlockSpec(w_out.shape, const),
                  pl.BlockSpec(memory_space=pl.ANY)],
        out_specs=(pl.BlockSpec((tm, W), row), pl.BlockSpec((W, D), const)),
        out_shape=(jax.ShapeDtypeStruct((T, W), BF16), jax.ShapeDtypeStruct((W, D), BF16)),
        scratch_shapes=[pltpu.VMEM((W, D), F32)],
        compiler_params=_params("arbitrary"),
    )(*_hbm(dy, attn_n, conv_n, w_out, after))


def _conv_bwd(dmixed, u, gb, gc, conv_w, gain, name, tc):
    T = u.shape[0]
    nt = T // tc
    per8 = tc // SUBLANES

    def body(dm_ref, u_ref, b_ref, c_ref, ut_ref, ct_ref, cw_ref, g_ref,
             du_ref, db_ref, dc_ref, dcw_ref, dgain_ref, head_sc):
        i = pl.program_id(0)

        @pl.when(i == 0)
        def _():
            head_sc[...] = jnp.zeros_like(head_sc)
            dcw_ref[...] = jnp.zeros_like(dcw_ref)
            dgain_ref[...] = jnp.zeros_like(dgain_ref)

        uu = u_ref[...].astype(F32)
        cc = c_ref[...].astype(F32)
        bb = b_ref[...].astype(F32)
        cu = cc * uu
        tail = jnp.where(i == nt - 1, 0.0, ct_ref[...].astype(F32) * ut_ref[...].astype(F32))
        s1, s2 = _shift_down(cu, tail)
        w0, w1, w2 = cw_ref[0:1, :], cw_ref[1:2, :], cw_ref[2:3, :]
        pre = w0 * s2 + w1 * s1 + w2 * cu
        dconv, dgain = _rms_bwd(dm_ref[...].astype(F32), bb * pre, g_ref[...])
        dgain_ref[...] += dgain
        db_ref[...] = (dconv * pre).astype(BF16)
        dpre = dconv * bb
        dcw_ref[0:1, :] += jnp.sum(dpre * s2, axis=0, keepdims=True)
        dcw_ref[1:2, :] += jnp.sum(dpre * s1, axis=0, keepdims=True)
        dcw_ref[2:3, :] += jnp.sum(dpre * cu, axis=0, keepdims=True)
        n1, n2 = _shift_up(dpre, head_sc[...])
        head_sc[...] = dpre[0:SUBLANES, :]
        dcu = w2 * dpre + w1 * n1 + w0 * n2
        du_ref[...] = (dcu * cc).astype(BF16)
        dc_ref[...] = (dcu * uu).astype(BF16)

    rev = lambda i: (nt - 1 - i, 0)
    rev_right = lambda i: (nt - 1 - i, 1)
    tail_map = lambda i: (jnp.maximum((nt - 1 - i) * per8 - 1, 0), 0)
    const = lambda i: (0, 0)
    return pl.pallas_call(
        body, name=name, grid=(nt,),
        in_specs=[pl.BlockSpec((tc, CONV_DIM), rev_right),
                  pl.BlockSpec((tc, CONV_DIM), rev), pl.BlockSpec((tc, CONV_DIM), rev),
                  pl.BlockSpec((tc, CONV_DIM), rev),
                  pl.BlockSpec((SUBLANES, CONV_DIM), tail_map), pl.BlockSpec((SUBLANES, CONV_DIM), tail_map),
                  pl.BlockSpec(conv_w.shape, const), pl.BlockSpec((1, CONV_DIM), const)],
        out_specs=(pl.BlockSpec((tc, CONV_DIM), rev), pl.BlockSpec((tc, CONV_DIM), rev),
                   pl.BlockSpec((tc, CONV_DIM), rev),
                   pl.BlockSpec((SUBLANES, CONV_DIM), const), pl.BlockSpec((1, CONV_DIM), const)),
        out_shape=(jax.ShapeDtypeStruct((T, CONV_DIM), BF16), jax.ShapeDtypeStruct((T, CONV_DIM), BF16),
                   jax.ShapeDtypeStruct((T, CONV_DIM), BF16),
                   jax.ShapeDtypeStruct((SUBLANES, CONV_DIM), F32), jax.ShapeDtypeStruct((1, CONV_DIM), F32)),
        scratch_shapes=[pltpu.VMEM((SUBLANES, CONV_DIM), F32)],
        compiler_params=_params("arbitrary"),
    )(*_hbm(dmixed, u, gb, gc, u, gc, conv_w, gain))


def _mixin_bwd(dy, x, gain, dz, w_in_t, name, tm):
    T, D = x.shape
    nz = len(dz)

    def body(dy_ref, x_ref, g_ref, *rest):
        dz_refs, wt_ref, dx_ref, dgain_ref, dz_sc = rest[:nz], rest[nz], rest[nz + 1], rest[nz + 2], rest[nz + 3]

        @pl.when(pl.program_id(0) == 0)
        def _():
            dgain_ref[...] = jnp.zeros_like(dgain_ref)

        for r, lo, hi in zip(dz_refs, _Z_SPLITS[:-1], _Z_SPLITS[1:]):
            dz_sc[:, lo:hi] = r[...]
        dx, dgain = _rms_bwd(_dot(dz_sc[...], wt_ref[...]), x_ref[...], g_ref[...])
        dgain_ref[...] += dgain
        dx_ref[...] = dy_ref[...] + dx

    row = lambda i: (i, 0)
    const = lambda i: (0, 0)
    return pl.pallas_call(
        body, name=name, grid=(T // tm,),
        in_specs=[pl.BlockSpec((tm, D), row), pl.BlockSpec((tm, D), row), pl.BlockSpec((1, D), const)]
                 + [pl.BlockSpec((tm, a.shape[1]), row) for a in dz]
                 + [pl.BlockSpec(w_in_t.shape, const)],
        out_specs=(pl.BlockSpec((tm, D), row), pl.BlockSpec((1, D), const)),
        out_shape=(jax.ShapeDtypeStruct((T, D), F32), jax.ShapeDtypeStruct((1, D), F32)),
        scratch_shapes=[pltpu.VMEM((tm, _Z_SPLITS[-1]), BF16)],
        compiler_params=_params("arbitrary"),
    )(*_hbm(dy, x, gain, *dz, w_in_t))


def _win_grad(dz, hn, name, tk):
    T, D = hn.shape
    nz = len(dz)
    nt = T // tk
    W = _Z_SPLITS[-1]

    def body(hn_ref, *rest):
        dz_refs, dw_ref, dw_sc, kv_sc = rest[:nz], rest[nz], rest[nz + 1], rest[nz + 2]
        i = pl.program_id(0)

        @pl.when(i == 0)
        def _():
            dw_sc[...] = jnp.zeros_like(dw_sc)

        hn = hn_ref[...]
        kv_sc[:, :KV_WIDTH] = dz_refs[1][...]
        kv_sc[:, KV_WIDTH:] = dz_refs[2][...]
        dw_sc[_Z_SPLITS[1]:_Z_SPLITS[3], :] += _dot_tn(kv_sc[...], hn)
        for n in (0, 3, 4, 5):
            dw_sc[_Z_SPLITS[n]:_Z_SPLITS[n + 1], :] += _dot_tn(dz_refs[n][...], hn)

        @pl.when(i == nt - 1)
        def _():
            dw_ref[...] = dw_sc[...].astype(BF16)

    row = lambda i: (i, 0)
    return pl.pallas_call(
        body, name=name, grid=(nt,),
        in_specs=[pl.BlockSpec((tk, D), row)] + [pl.BlockSpec((tk, a.shape[1]), row) for a in dz],
        out_specs=pl.BlockSpec((W, D), lambda i: (0, 0)),
        out_shape=jax.ShapeDtypeStruct((W, D), BF16),
        scratch_shapes=[pltpu.VMEM((W, D), F32), pltpu.VMEM((tk, 2 * KV_WIDTH), BF16)],
        compiler_params=_params("arbitrary"),
    )(*_hbm(hn, *dz))


def _adamw(parts, w, m, v, name, tr):
    P = parts.shape[0]
    R, C = w.shape

    def body(p_ref, w_ref, m_ref, v_ref, g_ref, d_ref, nm_ref, nv_ref):
        g = p_ref[0].astype(F32)
        for d in range(1, P):
            g = g + p_ref[d].astype(F32)
        nm = ADAM_B1 * m_ref[...] + (1.0 - ADAM_B1) * g
        nv = ADAM_B2 * v_ref[...] + (1.0 - ADAM_B2) * (g * g)
        m_hat = nm / (1.0 - ADAM_B1 ** ADAM_STEP)
        v_hat = nv / (1.0 - ADAM_B2 ** ADAM_STEP)
        g_ref[...] = g
        nm_ref[...] = nm
        nv_ref[...] = nv
        d_ref[...] = -ADAM_LR * (m_hat / (jnp.sqrt(v_hat) + ADAM_EPS) + ADAM_WD * w_ref[...])

    row = lambda i: (i, 0)
    spec = pl.BlockSpec((tr, C), row)
    shp = jax.ShapeDtypeStruct((R, C), F32)
    return pl.pallas_call(
        body, name=name, grid=(R // tr,),
        in_specs=[pl.BlockSpec((P, tr, C), lambda i: (0, i, 0)), spec, spec, spec],
        out_specs=(spec, spec, spec, spec),
        out_shape=(shp, shp, shp, shp),
        compiler_params=_params("arbitrary"),
    )(*_hbm(parts, w, m, v))


def _columns_of_blocks(g):
    n, R, w = g.shape
    return g.transpose(1, 0, 2).reshape(R, n * w)


def _pad_row(vec):
    vec = vec.reshape(1, -1)
    return jnp.pad(vec, ((0, 0), (0, PACK_COLS - vec.shape[1])))


def kernel(x, rel_bias_table, ffn1_norm, ffn1_w_gate, ffn1_w_up, ffn1_w_down, mix_norm, w_in, conv_w, attn_sinks, attn_out_norm, conv_out_norm, w_out, ffn2_norm, ffn2_w_gate, ffn2_w_up, ffn2_w_down, final_norm, loss_target, m_rel_bias_table, m_ffn1_norm, m_ffn1_w_gate, m_ffn1_w_up, m_ffn1_w_down, m_mix_norm, m_w_in, m_conv_w, m_attn_sinks, m_attn_out_norm, m_conv_out_norm, m_w_out, m_ffn2_norm, m_ffn2_w_gate, m_ffn2_w_up, m_ffn2_w_down, m_final_norm, v_rel_bias_table, v_ffn1_norm, v_ffn1_w_gate, v_ffn1_w_up, v_ffn1_w_down, v_mix_norm, v_w_in, v_conv_w, v_attn_sinks, v_attn_out_norm, v_conv_out_norm, v_w_out, v_ffn2_norm, v_ffn2_w_gate, v_ffn2_w_up, v_ffn2_w_down, v_final_norm):
    T, D = x.shape[1], x.shape[2]
    x0 = x[0]
    target = loss_target[0]
    tm = min(TM_FFN, T)
    tm_bwd = min(TM_FFN_BWD, T)
    tm_mix = min(TM_MIX, T)
    tk = min(TK_WGRAD, T)
    tf = TF_FFN
    nblk = min(ATTN_BLOCKS, T // BLOCK)
    me = 4 * lax.axis_index("x") + 2 * lax.axis_index("y") + lax.axis_index("c")

    big = {
        "ffn1_w_gate": (ffn1_w_gate[0], m_ffn1_w_gate[0], v_ffn1_w_gate[0], True),
        "ffn1_w_up": (ffn1_w_up[0], m_ffn1_w_up[0], v_ffn1_w_up[0], True),
        "ffn1_w_down": (ffn1_w_down[0], m_ffn1_w_down[0], v_ffn1_w_down[0], False),
        "w_in": (w_in[0], m_w_in[0], v_w_in[0], True),
        "w_out": (w_out[0], m_w_out[0], v_w_out[0], False),
        "ffn2_w_gate": (ffn2_w_gate[0], m_ffn2_w_gate[0], v_ffn2_w_gate[0], True),
        "ffn2_w_up": (ffn2_w_up[0], m_ffn2_w_up[0], v_ffn2_w_up[0], True),
        "ffn2_w_down": (ffn2_w_down[0], m_ffn2_w_down[0], v_ffn2_w_down[0], False),
    }

    def block_to_send(name):
        w, _, _, transposed = big[name]
        return (w.T if transposed else w).astype(BF16)

    names1 = ["ffn1_w_gate", "ffn1_w_up", "ffn1_w_down"]
    names_rest = ["w_in", "w_out", "ffn2_w_gate", "ffn2_w_up", "ffn2_w_down"]
    first = _gather_two_level([block_to_send(n) for n in names1], "gather_ffn1", ffn1_norm)
    wgt1, wut1, wd1 = [g.reshape(-1, D) for g in first]
    h_rest, token = _exchange_start([block_to_send(n) for n in names_rest[:2]] + [conv_w[0]]
                                    + [block_to_send(n) for n in names_rest[2:]], False,
                                    "gather_start_rest", wd1)

    wgu1 = _stack_gate_up(wgt1, wut1, tf)
    x1, xn1, gu1 = _ffn_fwd(x0, ffn1_norm, wgu1, wd1, token, "ffn1_fwd", tm, tf)
    mixw = _exchange_wait(h_rest[:3], False, "gather_wait_mix", x1)
    win_t = mixw[0].reshape(-1, D)
    wout = mixw[1].reshape(-1, D)
    cw = _columns_of_blocks(mixw[2])
    hn, q, k, v, u, gb, gc = _mixin_fwd(x1, mix_norm, win_t, "mixin_fwd", tm_mix)
    bucket = jnp.asarray(_bucket_table().T.copy())
    sinks = attn_sinks.reshape(-1)
    bias_t = _bias_build(rel_bias_table, bucket, "bias_build")
    attn_raw, attn_n = _attn_fwd(q, k, v, bias_t, sinks, attn_out_norm, "attn_fwd", nblk)
    x2, conv_n = _mixout_fwd(x1, attn_n, u, gb, gc, cw, conv_out_norm, wout, "mixout_fwd", tm_mix)
    wgt2, wut2, wd2 = [g.reshape(-1, D) for g in _exchange_wait(h_rest[3:], False, "gather_wait_ffn2", x2)]
    wgu2 = _stack_gate_up(wgt2, wut2, tf)
    dx3, xn2, gu2, d_final, loss_part = _ffn_fwd(x2, ffn2_norm, wgu2, wd2, x2, "ffn2_fwd", tm, tf,
                                                 head=(target, final_norm.reshape(1, D)))

    def blocks(g):
        return g.reshape(N_DEV, -1, D)

    dx2, d_ffn2_norm, dg2, du2, hh2, do2 = _ffn_bwd(
        dx3, x2, ffn2_norm, gu2, wgu2, wd2, dx3, "ffn2_bwd", tm_bwd, tf)
    d_wg2 = _tn_grad(dg2, xn2, dx2, "ffn2_wgrad_gate", BM_WGRAD, tk)
    d_wu2 = _tn_grad(du2, xn2, dx2, "ffn2_wgrad_up", BM_WGRAD, tk)
    d_wd2 = _tn_grad(hh2, do2, dx2, "ffn2_wgrad_down", BM_WGRAD, tk)
    handles2, token2 = _exchange_start([blocks(d_wg2), blocks(d_wu2), blocks(d_wd2)], True,
                                       "grads_start_ffn2", dx2)

    dmixed, d_wout = _mixout_bwd(dx2, attn_n, conv_n, wout, token2, "mixout_bwd", tm_mix)
    dq, dk, dv, dbias, dsink, d_attn_norm = _attn_bwd(
        dmixed, attn_raw, q, k, v, bias_t, sinks, attn_out_norm, "attn_bwd", nblk)
    du, dgb, dgc, d_cw, d_conv_norm = _conv_bwd(dmixed, u, gb, gc, cw, conv_out_norm, "conv_bwd", tm_mix)
    d_table = _bias_grad(dbias, bucket, "bias_grad")
    dz = [dq, dk, dv, du, dgb, dgc]
    dx1, d_mix_norm = _mixin_bwd(dx2, x1, mix_norm, dz, win_t, "mixin_bwd", tm_mix)
    d_win_t = _win_grad(dz, hn, "win_grad", min(TM_MIX, T))
    handles_mix, token_mix = _exchange_start([blocks(d_win_t), blocks(d_wout)], True, "grads_start_mix", d_table)

    dx0, d_ffn1_norm, dg1, du1, hh1, do1 = _ffn_bwd(
        dx1, x0, ffn1_norm, gu1, wgu1, wd1, token_mix, "ffn1_bwd", tm_bwd, tf)
    def pack(ffn1, mixn, ffn2, fin, attn_n_, conv_n_, sink_, extra, convw, table):
        rows = [_pad_row(ffn1), _pad_row(mixn), _pad_row(ffn2), _pad_row(fin),
                _pad_row(jnp.concatenate([attn_n_.reshape(-1), conv_n_.reshape(-1)])),
                _pad_row(sink_), _pad_row(extra),
                jnp.zeros((1, PACK_COLS), F32),
                jnp.pad(convw, ((0, 0), (0, PACK_COLS - convw.shape[1]))),
                _pad_row(table),
                jnp.zeros((PACK_ROWS - 12, PACK_COLS), F32)]
        return jnp.concatenate(rows, axis=0)

    def own_channels(a):
        full = jnp.zeros((a.shape[1], CONV_DIM), F32)
        return lax.dynamic_update_slice(full, a[0], (0, me * a.shape[2]))

    g_pack = pack(d_ffn1_norm, d_mix_norm, d_ffn2_norm, d_final, d_attn_norm, d_conv_norm,
                  dsink[:, 0], loss_part[0, :1], d_cw[:3], d_table[:, :N_Q_HEADS])
    zero1 = jnp.zeros((1,), F32)
    w_pack = pack(ffn1_norm, mix_norm, ffn2_norm, final_norm, attn_out_norm, conv_out_norm,
                  attn_sinks, zero1, own_channels(conv_w), rel_bias_table)
    m_pack = pack(m_ffn1_norm, m_mix_norm, m_ffn2_norm, m_final_norm, m_attn_out_norm, m_conv_out_norm,
                  m_attn_sinks, zero1, own_channels(m_conv_w), m_rel_bias_table)
    v_pack = pack(v_ffn1_norm, v_mix_norm, v_ffn2_norm, v_final_norm, v_attn_out_norm, v_conv_out_norm,
                  v_attn_sinks, zero1, own_channels(v_conv_w), v_rel_bias_table)
    (g_all,) = _exchange([g_pack], False, "gather_small", dx0)
    packs = _adamw(g_all, w_pack, m_pack, v_pack, "adamw_small", PACK_ROWS)

    d_wd1 = _tn_grad(hh1, do1, packs[0], "ffn1_wgrad_down", BM_WGRAD, tk)
    handles1_d, token1 = _exchange_start([blocks(d_wd1)], True, "grads_start_ffn1_down", dx0)
    d_wg1 = _tn_grad(dg1, xn1, token1, "ffn1_wgrad_gate", BM_WGRAD, tk)
    handles1_g, token1 = _exchange_start([blocks(d_wg1)], True, "grads_start_ffn1_gate", token1)
    d_wu1 = _tn_grad(du1, xn1, token1, "ffn1_wgrad_up", BM_WGRAD, tk)
    handles1_u, token1 = _exchange_start([blocks(d_wu1)], True, "grads_start_ffn1_up", token1)

    res = {}

    def update(names, parts):
        last = None
        for name, p in zip(names, parts):
            w, m_, v_, transposed = big[name]
            if transposed:
                w, m_, v_ = w.T, m_.T, v_.T
            new = _adamw(p, w, m_, v_, "adamw_" + name, _row_tile(w.shape[0], ADAM_ROWS))
            res[name] = tuple((a.T if transposed else a)[None] for a in new)
            last = new[0]
        return last

    parts2 = _exchange_wait(handles2, True, "grads_wait_ffn2", token1)
    done2 = update(["ffn2_w_gate", "ffn2_w_up", "ffn2_w_down"], parts2)
    parts_mix = _exchange_wait(handles_mix, True, "grads_wait_mix", done2)
    done_mix = update(["w_in", "w_out"], parts_mix)

    parts1_d = _exchange_wait(handles1_d, True, "grads_wait_ffn1_down", done_mix)
    done1 = update(["ffn1_w_down"], parts1_d)
    parts1_g = _exchange_wait(handles1_g, True, "grads_wait_ffn1_gate", done1)
    done1 = update(["ffn1_w_gate"], parts1_g)
    parts1_u = _exchange_wait(handles1_u, True, "grads_wait_ffn1_up", done1)
    update(["ffn1_w_up"], parts1_u)

    def unpack(pk):
        cwb = lax.dynamic_slice(pk[8:11, :CONV_DIM], (0, me * conv_w.shape[2]), (3, conv_w.shape[2]))
        return {
            "ffn1_norm": pk[0:1, :D], "mix_norm": pk[1:2, :D], "ffn2_norm": pk[2:3, :D],
            "final_norm": pk[3, :D],
            "attn_out_norm": pk[4:5, :ATTN_WIDTH], "conv_out_norm": pk[4:5, ATTN_WIDTH:ATTN_WIDTH + CONV_DIM],
            "attn_sinks": pk[5:6, :N_Q_HEADS],
            "conv_w": cwb[None],
            "rel_bias_table": pk[11, :NUM_BUCKETS * N_Q_HEADS].reshape(NUM_BUCKETS, N_Q_HEADS),
        }

    small = [unpack(pk) for pk in packs]
    loss = packs[0][6, 0]

    order = ["rel_bias_table", "ffn1_norm", "ffn1_w_gate", "ffn1_w_up", "ffn1_w_down", "mix_norm", "w_in",
             "conv_w", "attn_sinks", "attn_out_norm", "conv_out_norm", "w_out", "ffn2_norm",
             "ffn2_w_gate", "ffn2_w_up", "ffn2_w_down", "final_norm"]
    outs = [loss, dx0[None]]
    for kind in range(4):
        for name in order:
            outs.append(res[name][kind] if name in res else small[kind][name])
    return tuple(outs)
```

```python
import math

import numpy as np
import jax
import jax.numpy as jnp
from jax import lax
from jax.experimental import pallas as pl
from jax.experimental.pallas import tpu as pltpu

F32 = jnp.float32
BF16 = jnp.bfloat16

N_DEV = 8
EPS = 1e-6
HEAD_DIM = 64
N_Q_HEADS = 8
N_KV_HEADS = 2
GQA_GROUP = 4
ATTN_WIDTH = 512
KV_WIDTH = 128
CONV_DIM = 512
BLOCK = 128
WINDOW = 128
NUM_BUCKETS = 32
MAX_DISTANCE = 128
SCALE = HEAD_DIM ** -0.5
MASKED = -1e30
GROUP_ROWS = GQA_GROUP * BLOCK

ADAM_LR = 0.001
ADAM_B1 = 0.9
ADAM_B2 = 0.999
ADAM_EPS = 1e-08
ADAM_WD = 0.01
ADAM_STEP = 10

VMEM_LIMIT_BYTES = 40 * 1024 * 1024
VMEM_LIMIT_FFN_BYTES = 50 * 1024 * 1024
SUBLANES = 8
PACK_ROWS = 16
PACK_COLS = 1024

TM_FFN = 1024
TM_FFN_BWD = 1024
TM_MIX = 512
TK_WGRAD = 2048
TF_FFN = 256
BM_WGRAD = 1408
ROW_GROUPS = 4
ATTN_BLOCKS = 8
ADAM_ROWS = 256


def _row_tile(rows, limit):
    best = rows
    for t in range(16, min(rows, limit) + 1, 16):
        if rows % t == 0:
            best = t
    return best


def _params(*sem):
    return pltpu.CompilerParams(dimension_semantics=sem, vmem_limit_bytes=VMEM_LIMIT_BYTES)


def _hbm(*arrays):
    return [pltpu.with_memory_space_constraint(a, pltpu.HBM) for a in arrays]


def _dot(a, b):
    return jnp.dot(a, b, preferred_element_type=F32)


def _dot_nt(a, b):
    return lax.dot_general(a, b, (((1,), (1,)), ((), ())), preferred_element_type=F32)


def _dot_tn(a, b):
    return lax.dot_general(a, b, (((0,), (0,)), ((), ())), preferred_element_type=F32)


def _sigmoid(g):
    return 0.5 * jnp.tanh(0.5 * g) + 0.5


def _rms_stats(x):
    inv = lax.rsqrt(jnp.mean(x * x, axis=-1, keepdims=True) + EPS)
    return inv, x * inv


def _rms_bwd(dy, x, gain):
    inv, xhat = _rms_stats(x)
    dgain = jnp.sum(dy * xhat, axis=0, keepdims=True)
    dxh = dy * gain
    dx = inv * (dxh - xhat * jnp.mean(dxh * xhat, axis=-1, keepdims=True))
    return dx, dgain


def _peer_list():
    x, y, c = lax.axis_index("x"), lax.axis_index("y"), lax.axis_index("c")
    peers = []
    for k in range(1, N_DEV):
        px = 1 - x if (k >> 2) & 1 else x
        py = 1 - y if (k >> 1) & 1 else y
        pc = 1 - c if k & 1 else c
        peers.append((px, py, pc))
    return 4 * x + 2 * y + c, peers


def _exchange(arrs, scatter, name, after):
    n = len(arrs)
    out_shape = []
    for a in arrs:
        shp = a.shape if scatter else (N_DEV,) + a.shape
        out_shape.append(jax.ShapeDtypeStruct(shp, a.dtype))

    def body(*refs):
        ins, outs = refs[:n], refs[n + 1:2 * n + 1]
        send_sems, recv_sems, local_sems = refs[2 * n + 1:]
        me, peers = _peer_list()
        started = []
        for a in range(n):
            own = ins[a].at[me] if scatter else ins[a]
            loc = pltpu.make_async_copy(own, outs[a].at[me], local_sems.at[a])
            loc.start()
            started.append(loc)
        sends = []
        for a in range(n):
            for k, (px, py, pc) in enumerate(peers):
                src = ins[a].at[4 * px + 2 * py + pc] if scatter else ins[a]
                cp = pltpu.make_async_remote_copy(
                    src_ref=src, dst_ref=outs[a].at[me],
                    send_sem=send_sems.at[a, k], recv_sem=recv_sems.at[a, k],
                    device_id=(px, py, pc), device_id_type=pl.DeviceIdType.MESH)
                cp.start()
                sends.append(cp)
        for a in range(n):
            for k, (px, py, pc) in enumerate(peers):
                landed = outs[a].at[4 * px + 2 * py + pc]
                pltpu.make_async_remote_copy(
                    src_ref=landed, dst_ref=landed,
                    send_sem=send_sems.at[a, k], recv_sem=recv_sems.at[a, k],
                    device_id=(px, py, pc), device_id_type=pl.DeviceIdType.MESH).wait_recv()
        for cp in sends:
            cp.wait_send()
        for loc in started:
            loc.wait()

    hbm = pl.BlockSpec(memory_space=pl.ANY)
    return pl.pallas_call(
        body, name=name, out_shape=tuple(out_shape),
        in_specs=[hbm] * (n + 1), out_specs=tuple([hbm] * n),
        scratch_shapes=[pltpu.SemaphoreType.DMA((n, N_DEV - 1)),
                        pltpu.SemaphoreType.DMA((n, N_DEV - 1)),
                        pltpu.SemaphoreType.DMA((n,))],
    )(*arrs, after)


def _gather_two_level(arrs, name, after):
    n = len(arrs)
    out_shape = tuple(jax.ShapeDtypeStruct((N_DEV,) + a.shape, a.dtype) for a in arrs)

    def body(*refs):
        ins, outs = refs[:n], refs[n + 1:2 * n + 1]
        send_sems, recv_sems, local_sems = refs[2 * n + 1:]
        x, y, c = lax.axis_index("x"), lax.axis_index("y"), lax.axis_index("c")
        me, sibling = (x, y, c), (x, y, 1 - c)
        chips = [(1 - x, y), (x, 1 - y), (1 - x, 1 - y)]

        def copy(a, k, block, to, src=None):
            slot = outs[a].at[4 * block[0] + 2 * block[1] + block[2]]
            return pltpu.make_async_remote_copy(
                src_ref=slot if src is None else src, dst_ref=slot,
                send_sem=send_sems.at[a, k], recv_sem=recv_sems.at[a, k],
                device_id=to, device_id_type=pl.DeviceIdType.MESH)

        started = []
        for a in range(n):
            loc = pltpu.make_async_copy(ins[a], outs[a].at[4 * x + 2 * y + c], local_sems.at[a])
            loc.start()
            started.append(loc)
        sends = []
        for a in range(n):
            sends.append(copy(a, 0, me, sibling, src=ins[a]))
            sends += [copy(a, 1 + j, me, (*chip, c), src=ins[a]) for j, chip in enumerate(chips)]
        for cp in sends:
            cp.start()
        for j, chip in enumerate(chips):
            for a in range(n):
                copy(a, 1 + j, (*chip, c), me).wait_recv()
                fwd = copy(a, 4 + j, (*chip, c), sibling)
                fwd.start()
                sends.append(fwd)
        for a in range(n):
            copy(a, 0, sibling, me).wait_recv()
            for j, chip in enumerate(chips):
                copy(a, 4 + j, (*chip, 1 - c), me).wait_recv()
        for cp in sends:
            cp.wait_send()
        for loc in started:
            loc.wait()

    hbm = pl.BlockSpec(memory_space=pl.ANY)
    return pl.pallas_call(
        body, name=name, out_shape=out_shape,
        in_specs=[hbm] * (n + 1), out_specs=tuple([hbm] * n),
        scratch_shapes=[pltpu.SemaphoreType.DMA((n, N_DEV - 1)),
                        pltpu.SemaphoreType.DMA((n, N_DEV - 1)),
                        pltpu.SemaphoreType.DMA((n,))],
    )(*arrs, after)


_HBM = pl.BlockSpec(memory_space=pltpu.HBM)
_SEM = pl.BlockSpec(memory_space=pltpu.SEMAPHORE)
_EFFECT = pltpu.SideEffectType.DATAFLOW_SIDE_EFFECTING


def _split_copies(srcs, lands, send_sems, recv_sems, scatter):
    me, peers = _peer_list()
    copies = []
    for a in range(len(srcs)):
        for k, (px, py, pc) in enumerate(peers):
            src = srcs[a].at[4 * px + 2 * py + pc] if scatter else srcs[a]
            copies.append(pltpu.make_async_remote_copy(
                src_ref=src, dst_ref=lands[a].at[me],
                send_sem=send_sems[a].at[k], recv_sem=recv_sems[a].at[k],
                device_id=(px, py, pc), device_id_type=pl.DeviceIdType.MESH))
    return copies


def _own_copies(srcs, lands, recv_sems, scatter):
    me, _ = _peer_list()
    return [pltpu.make_async_copy(srcs[a].at[me] if scatter else srcs[a], lands[a].at[me],
                                  recv_sems[a].at[N_DEV - 1]) for a in range(len(srcs))]


def _exchange_start(arrs, scatter, name, after):
    n = len(arrs)
    lands = [lax.empty(a.shape if scatter else (N_DEV,) + a.shape, a.dtype) for a in arrs]

    def body(*refs):
        srcs, lnds = refs[:n], refs[n:2 * n]
        outs = refs[2 * n + 1:]
        send_sems, recv_sems = outs[:n], outs[n:2 * n]
        token = outs[4 * n]
        for cp in _split_copies(srcs, lnds, send_sems, recv_sems, scatter):
            cp.start()
        for cp in _own_copies(srcs, lnds, recv_sems, scatter):
            cp.start()
        token[...] = jnp.zeros_like(token)

    sem = pltpu.SemaphoreType.DMA((N_DEV,))
    out_shape = ([sem] * (2 * n) + [pltpu.HBM(a.shape, a.dtype) for a in arrs]
                 + [pltpu.HBM(l.shape, l.dtype) for l in lands] + [jax.ShapeDtypeStruct((SUBLANES, 128), F32)])
    res = pl.pallas_call(
        body, name=name, out_shape=tuple(out_shape),
        in_specs=[_HBM] * (2 * n) + [pl.BlockSpec(memory_space=pl.ANY)],
        out_specs=tuple([_SEM] * (2 * n) + [_HBM] * (2 * n) + [pl.BlockSpec(memory_space=pltpu.VMEM)]),
        input_output_aliases={i: 2 * n + i for i in range(2 * n)},
        compiler_params=pltpu.CompilerParams(has_side_effects=_EFFECT),
    )(*[pltpu.with_memory_space_constraint(a, pltpu.HBM) for a in arrs],
      *[pltpu.with_memory_space_constraint(l, pltpu.HBM) for l in lands], after)
    handles = [(res[2 * n + a], res[3 * n + a], res[a], res[n + a]) for a in range(n)]
    return handles, res[4 * n]


def _exchange_wait(handles, scatter, name, after):
    n = len(handles)

    def body(*refs):
        srcs, lnds = refs[:n], refs[n:2 * n]
        send_sems, recv_sems = refs[2 * n:3 * n], refs[3 * n:4 * n]
        for cp in _split_copies(srcs, lnds, send_sems, recv_sems, scatter):
            cp.wait_send()
            cp.wait_recv()
        for cp in _own_copies(srcs, lnds, recv_sems, scatter):
            cp.wait()

    srcs = [h[0] for h in handles]
    lands = [h[1] for h in handles]
    res = pl.pallas_call(
        body, name=name,
        out_shape=tuple([pltpu.HBM(a.shape, a.dtype) for a in srcs] + [pltpu.HBM(l.shape, l.dtype) for l in lands]),
        in_specs=[_HBM] * (2 * n) + [_SEM] * (2 * n) + [pl.BlockSpec(memory_space=pl.ANY)],
        out_specs=tuple([_HBM] * (2 * n)),
        input_output_aliases={i: i for i in range(2 * n)},
        compiler_params=pltpu.CompilerParams(has_side_effects=_EFFECT),
    )(*srcs, *lands, *[h[2] for h in handles], *[h[3] for h in handles], after)
    return list(res[n:])


def _row_groups(tm):
    return [slice(r * (tm // ROW_GROUPS), (r + 1) * (tm // ROW_GROUPS)) for r in range(ROW_GROUPS)]


def _stack_gate_up(wgt, wut, tf):
    F, D = wgt.shape
    return jnp.stack([wgt.reshape(F // tf, tf, D), wut.reshape(F // tf, tf, D)], axis=1).reshape(2 * F, D)


def _ffn_fwd(x, gain, wgu, wd, after, name, tm, tf, head=None):
    T, D = x.shape
    F = wd.shape[0]
    nj = F // tf
    n_in = 7 if head else 5

    def body(*refs):
        x_ref, g_ref, wgu_ref, wd_ref, after_ref = refs[:5]
        outs = refs[n_in:]
        xo_ref, xn_ref, gu_ref = outs[:3]
        xn_sc, acc_sc = outs[-2:]
        i, j = pl.program_id(0), pl.program_id(1)

        @pl.when(j == 0)
        def _():
            _, xhat = _rms_stats(x_ref[...])
            xn = (xhat * g_ref[...]).astype(BF16)
            xn_sc[...] = xn
            xn_ref[...] = xn
            acc_sc[...] = jnp.zeros_like(acc_sc)

        groups = _row_groups(tm)
        gus = [_dot_nt(xn_sc[rows, :], wgu_ref[...]) for rows in groups]
        hs = []
        for rows, gu in zip(groups, gus):
            gu_ref[rows, :] = gu.astype(BF16)
            g, u = gu[:, :tf], gu[:, tf:]
            hs.append((g * _sigmoid(g) * u).astype(BF16))
        for rows, h in zip(groups, hs):
            acc_sc[rows, :] += _dot(h, wd_ref[...])

        if head:
            t_ref, fg_ref = refs[5:7]
            dgain_ref, loss_ref = outs[3:5]

            @pl.when((i == 0) & (j == 0))
            def _():
                dgain_ref[...] = jnp.zeros_like(dgain_ref)
                loss_ref[...] = jnp.zeros_like(loss_ref)

        @pl.when(j == nj - 1)
        def _():
            for rows in groups:
                xo = x_ref[rows, :] + 0.5 * acc_sc[rows, :]
                if head:
                    fg = fg_ref[...]
                    _, xhat = _rms_stats(xo)
                    err = xhat * fg - t_ref[rows, :]
                    loss_ref[...] += 0.5 * jnp.sum(jnp.mean(err * err, axis=-1, keepdims=True),
                                                   axis=0, keepdims=True)
                    dx, dgain = _rms_bwd(err * (1.0 / D), xo, fg)
                    dgain_ref[...] += dgain
                    xo_ref[rows, :] = dx
                else:
                    xo_ref[rows, :] = xo

    tile = pl.BlockSpec((tm, D), lambda i, j: (i, 0))
    const = pl.BlockSpec((1, D), lambda i, j: (0, 0))
    in_specs = [tile, const, pl.BlockSpec((2 * tf, D), lambda i, j: (j, 0)),
                pl.BlockSpec((tf, D), lambda i, j: (j, 0)), pl.BlockSpec(memory_space=pl.ANY)]
    out_specs = [tile, pl.BlockSpec((tm, D), lambda i, j: (i, 0)), pl.BlockSpec((tm, 2 * tf), lambda i, j: (i, j))]
    out_shape = [pltpu.HBM((T,D), F32), pltpu.HBM((T,D), BF16),
                 pltpu.HBM((T,2 * F), BF16)]
    operands = [x, gain, wgu, wd, after]
    if head:
        in_specs += [tile, const]
        out_specs += [const, pl.BlockSpec((SUBLANES, 128), lambda i, j: (0, 0))]
        out_shape += [jax.ShapeDtypeStruct((1, D), F32), jax.ShapeDtypeStruct((SUBLANES, 128), F32)]
        operands += list(head)
    return pl.pallas_call(
        body, name=name, grid=(T // tm, nj),
        in_specs=in_specs, out_specs=tuple(out_specs), out_shape=tuple(out_shape),
        scratch_shapes=[pltpu.VMEM((tm, D), BF16), pltpu.VMEM((tm, D), F32)],
        compiler_params=pltpu.CompilerParams(dimension_semantics=("arbitrary", "arbitrary"),
                                             vmem_limit_bytes=VMEM_LIMIT_FFN_BYTES),
    )(*_hbm(*operands))


def _ffn_bwd(dy, x, gain, gu, wgu, wd, after, name, tm, tf):
    T, D = x.shape
    F = wd.shape[0]
    nj = F // tf

    def body(dy_ref, x_ref, g_ref, gu_ref, wgu_ref, wd_ref, after_ref,
             dx_ref, dgain_ref, dg_ref, du_ref, hh_ref, do_ref, do_sc, acc_sc, dgu_sc):
        i, j = pl.program_id(0), pl.program_id(1)

        @pl.when((i == 0) & (j == 0))
        def _():
            dgain_ref[...] = jnp.zeros_like(dgain_ref)

        @pl.when(j == 0)
        def _():
            do = (0.5 * dy_ref[...]).astype(BF16)
            do_sc[...] = do
            do_ref[...] = do
            acc_sc[...] = jnp.zeros_like(acc_sc)

        groups = _row_groups(tm)
        dhs = [_dot_nt(do_sc[rows, :], wd_ref[...]) for rows in groups]
        for rows, dh in zip(groups, dhs):
            g = gu_ref[rows, :tf].astype(F32)
            u = gu_ref[rows, tf:].astype(F32)
            sig = _sigmoid(g)
            s = g * sig
            dg = (dh * u * (sig + s * (1.0 - sig))).astype(BF16)
            du = (dh * s).astype(BF16)
            dg_ref[rows, :] = dg
            du_ref[rows, :] = du
            dgu_sc[rows, :tf] = dg
            dgu_sc[rows, tf:] = du
            hh_ref[rows, :] = (s * u).astype(BF16)
        for rows in groups:
            acc_sc[rows, :] += _dot(dgu_sc[rows, :], wgu_ref[...])

        @pl.when(j == nj - 1)
        def _():
            for rows in groups:
                dx, dgain = _rms_bwd(acc_sc[rows, :], x_ref[rows, :], g_ref[...])
                dgain_ref[...] += dgain
                dx_ref[rows, :] = dy_ref[rows, :] + dx

    tile_in = pl.BlockSpec((tm, D), lambda i, j: (i, 0))
    tile = pl.BlockSpec((tm, D), lambda i, j: (i, 0), pipeline_mode=pl.Buffered(1))
    return pl.pallas_call(
        body, name=name, grid=(T // tm, nj),
        in_specs=[tile_in, tile_in,
                  pl.BlockSpec((1, D), lambda i, j: (0, 0)),
                  pl.BlockSpec((tm, 2 * tf), lambda i, j: (i, j)),
                  pl.BlockSpec((2 * tf, D), lambda i, j: (j, 0)),
                  pl.BlockSpec((tf, D), lambda i, j: (j, 0)),
                  pl.BlockSpec(memory_space=pl.ANY)],
        out_specs=(tile,
                   pl.BlockSpec((1, D), lambda i, j: (0, 0)),
                   pl.BlockSpec((tm, tf), lambda i, j: (i, j)),
                   pl.BlockSpec((tm, tf), lambda i, j: (i, j)),
                   pl.BlockSpec((tm, tf), lambda i, j: (i, j)),
                   pl.BlockSpec((tm, D), lambda i, j: (i, 0))),
        out_shape=(pltpu.HBM((T,D), F32), jax.ShapeDtypeStruct((1, D), F32),
                   pltpu.HBM((T,F), BF16), pltpu.HBM((T,F), BF16),
                   pltpu.HBM((T,F), BF16), pltpu.HBM((T,D), BF16)),
        scratch_shapes=[pltpu.VMEM((tm, D), BF16), pltpu.VMEM((tm, D), F32), pltpu.VMEM((tm, 2 * tf), BF16)],
        compiler_params=pltpu.CompilerParams(dimension_semantics=("arbitrary", "arbitrary"),
                                             vmem_limit_bytes=VMEM_LIMIT_FFN_BYTES),
    )(*_hbm(dy, x, gain, gu, wgu, wd, after))


def _tn_grad(a, b, after, name, bm, tk):
    T, M = a.shape
    D = b.shape[1]
    nk = T // tk

    def body(a_ref, b_ref, after_ref, o_ref, acc_sc):
        k = pl.program_id(1)

        @pl.when(k == 0)
        def _():
            acc_sc[...] = jnp.zeros_like(acc_sc)

        acc_sc[...] += _dot_tn(a_ref[...], b_ref[...])

        @pl.when(k == nk - 1)
        def _():
            o_ref[...] = acc_sc[...].astype(BF16)

    return pl.pallas_call(
        body, name=name, grid=(M // bm, nk),
        in_specs=[pl.BlockSpec((tk, bm), lambda i, k: (k, i)), pl.BlockSpec((tk, D), lambda i, k: (k, 0)),
                  pl.BlockSpec(memory_space=pl.ANY)],
        out_specs=pl.BlockSpec((bm, D), lambda i, k: (i, 0)),
        out_shape=pltpu.HBM((M, D), BF16),
        scratch_shapes=[pltpu.VMEM((bm, D), F32)],
        compiler_params=_params("arbitrary", "arbitrary"),
    )(*_hbm(a, b, after))


_Z_SPLITS = (0, 512, 640, 768, 1280, 1792, 2304)


def _mixin_fwd(x, gain, w_in_t, name, tm):
    T, D = x.shape
    widths = [b - a for a, b in zip(_Z_SPLITS[:-1], _Z_SPLITS[1:])]

    def body(x_ref, g_ref, w_ref, hn_ref, *outs):
        _, xhat = _rms_stats(x_ref[...])
        hn = (xhat * g_ref[...]).astype(BF16)
        hn_ref[...] = hn
        for n in (0, 3, 4, 5):
            outs[n][...] = _dot_nt(hn, w_ref[_Z_SPLITS[n]:_Z_SPLITS[n + 1], :]).astype(BF16)
        kv = _dot_nt(hn, w_ref[_Z_SPLITS[1]:_Z_SPLITS[3], :]).astype(BF16)
        outs[1][...] = kv[:, :KV_WIDTH]
        outs[2][...] = kv[:, KV_WIDTH:]

    return pl.pallas_call(
        body, name=name, grid=(T // tm,),
        in_specs=[pl.BlockSpec((tm, D), lambda i: (i, 0)),
                  pl.BlockSpec((1, D), lambda i: (0, 0)),
                  pl.BlockSpec(w_in_t.shape, lambda i: (0, 0))],
        out_specs=tuple([pl.BlockSpec((tm, D), lambda i: (i, 0))]
                        + [pl.BlockSpec((tm, w), lambda i: (i, 0)) for w in widths]),
        out_shape=tuple([pltpu.HBM((T,D), BF16)]
                        + [pltpu.HBM((T,w), BF16) for w in widths]),
        compiler_params=_params("arbitrary"),
    )(*_hbm(x, gain, w_in_t))


def _bucket_table():
    qi = np.arange(BLOCK, dtype=np.int32)[:, None]
    kj = np.arange(2 * BLOCK, dtype=np.int32)[None, :]
    dist = qi + BLOCK - kj
    n = np.maximum(dist, 0)
    max_exact = NUM_BUCKETS // 2
    large = max_exact + (np.log(np.maximum(n, 1).astype(np.float32) / max_exact)
                         / math.log(MAX_DISTANCE / max_exact)
                         * (NUM_BUCKETS - max_exact)).astype(np.int32)
    large = np.minimum(large, NUM_BUCKETS - 1)
    bucket = np.where(n < max_exact, n, large).astype(np.int32)
    valid = (dist >= 0) & (dist < WINDOW)
    return np.where(valid, bucket, -1).astype(np.int32)


def _bias_build(table, bucket, name):
    def body(t_ref, b_ref, o_ref):
        bk = b_ref[...]
        for h in range(N_Q_HEADS):
            def step(b, acc):
                return jnp.where(bk == b, t_ref[b, h], acc)
            o_ref[h] = lax.fori_loop(0, NUM_BUCKETS, step, jnp.full(bk.shape, MASKED, F32))

    return pl.pallas_call(
        body, name=name,
        in_specs=[pl.BlockSpec(memory_space=pltpu.SMEM), pl.BlockSpec(memory_space=pltpu.VMEM)],
        out_specs=pl.BlockSpec(memory_space=pltpu.VMEM),
        out_shape=jax.ShapeDtypeStruct((N_Q_HEADS,) + bucket.shape, F32),
    )(table, bucket)


def _bias_grad(dbias, bucket, name):
    def body(d_ref, b_ref, o_ref):
        bk = b_ref[...]
        row = lax.broadcasted_iota(jnp.int32, o_ref.shape, 0)
        lane = lax.broadcasted_iota(jnp.int32, o_ref.shape, 1)
        res = jnp.zeros(o_ref.shape, F32)
        for h in range(N_Q_HEADS):
            d = d_ref[h]

            def step(b, acc):
                part = jnp.sum(jnp.where(bk == b, d, 0.0), axis=0, keepdims=True)
                return jnp.where(row == b, part, acc)
            per_lane = lax.fori_loop(0, NUM_BUCKETS, step, jnp.zeros(o_ref.shape, F32))
            res = jnp.where(lane == h, jnp.sum(per_lane, axis=1, keepdims=True), res)
        o_ref[...] = res

    return pl.pallas_call(
        body, name=name,
        in_specs=[pl.BlockSpec(memory_space=pltpu.VMEM), pl.BlockSpec(memory_space=pltpu.VMEM)],
        out_specs=pl.BlockSpec(memory_space=pltpu.VMEM),
        out_shape=jax.ShapeDtypeStruct((NUM_BUCKETS, 128), F32),
    )(*_hbm(dbias, bucket))


def _head_cols(h):
    return slice(h * HEAD_DIM, (h + 1) * HEAD_DIM)


def _stack_heads(ref, r0, g, dtype):
    return jnp.concatenate(
        [ref[pl.ds(r0, BLOCK), _head_cols(GQA_GROUP * g + j)].astype(dtype) for j in range(GQA_GROUP)], axis=0)


def _unstack_heads(ref, r0, g, val):
    for j in range(GQA_GROUP):
        ref[pl.ds(r0, BLOCK), _head_cols(GQA_GROUP * g + j)] = val[j * BLOCK:(j + 1) * BLOCK, :]


def _head_lanes(h):
    return slice(h * BLOCK, (h + 1) * BLOCK)


def _group_lanes(g):
    return slice(g * GROUP_ROWS, (g + 1) * GROUP_ROWS)


def _head_softmax(st, bias_t, sink, no_prev):
    s = st * SCALE + bias_t
    row = lax.broadcasted_iota(jnp.int32, s.shape, 0)
    s = jnp.where(no_prev & (row < BLOCK), MASKED, s)
    m = jnp.maximum(jnp.max(s, axis=0, keepdims=True), sink)
    p = jnp.exp(s - m)
    ps = jnp.exp(sink - m)
    r = 1.0 / (jnp.sum(p, axis=0, keepdims=True) + ps)
    return p * r, ps * r


def _load_band(kf_sc, vf_sc, kp_ref, kc_ref, vp_ref, vc_ref, tq):
    kf_sc[0:BLOCK, :] = kp_ref[...]
    kf_sc[BLOCK:BLOCK + tq, :] = kc_ref[...]
    vf_sc[0:BLOCK, :] = vp_ref[...]
    vf_sc[BLOCK:BLOCK + tq, :] = vc_ref[...]


def _attn_fwd(q, k, v, bias_t, sinks, gain, name, nblk):
    T = q.shape[0]
    tq = nblk * BLOCK

    def body(sink_ref, q_ref, kc_ref, kp_ref, vc_ref, vp_ref, bias_ref, g_ref, raw_ref, nrm_ref,
             kf_sc, vf_sc, o_sc, st_sc, pt_sc):
        i = pl.program_id(0)
        _load_band(kf_sc, vf_sc, kp_ref, kc_ref, vp_ref, vc_ref, tq)

        def block(b, carry):
            r0 = pl.multiple_of(b * BLOCK, BLOCK)
            no_prev = (i == 0) & (b == 0)
            for g in range(N_KV_HEADS):
                kb = kf_sc[pl.ds(r0, 2 * BLOCK), _head_cols(g)]
                st_sc[:, _group_lanes(g)] = _dot_nt(kb, _stack_heads(q_ref, r0, g, BF16))
            for h in range(N_Q_HEADS):
                p, _ = _head_softmax(st_sc[:, _head_lanes(h)], bias_ref[h], sink_ref[h], no_prev)
                pt_sc[:, _head_lanes(h)] = p.astype(BF16)
            for g in range(N_KV_HEADS):
                vb = vf_sc[pl.ds(r0, 2 * BLOCK), _head_cols(g)]
                _unstack_heads(o_sc, r0, g, _dot_tn(pt_sc[:, _group_lanes(g)], vb))
            return carry

        lax.fori_loop(0, nblk, block, 0, unroll=True)
        o = o_sc[...]
        raw_ref[...] = o.astype(BF16)
        _, ohat = _rms_stats(o)
        nrm_ref[...] = (ohat * g_ref[...]).astype(BF16)

    cur = lambda i: (i, 0)
    prev = lambda i: (jnp.maximum(i * nblk - 1, 0), 0)
    lanes = N_Q_HEADS * BLOCK
    return pl.pallas_call(
        body, name=name, grid=(T // tq,),
        in_specs=[pl.BlockSpec(memory_space=pltpu.SMEM),
                  pl.BlockSpec((tq, ATTN_WIDTH), cur),
                  pl.BlockSpec((tq, KV_WIDTH), cur), pl.BlockSpec((BLOCK, KV_WIDTH), prev),
                  pl.BlockSpec((tq, KV_WIDTH), cur), pl.BlockSpec((BLOCK, KV_WIDTH), prev),
                  pl.BlockSpec(bias_t.shape, lambda i: (0, 0, 0)),
                  pl.BlockSpec((1, ATTN_WIDTH), lambda i: (0, 0))],
        out_specs=(pl.BlockSpec((tq, ATTN_WIDTH), cur), pl.BlockSpec((tq, ATTN_WIDTH), cur)),
        out_shape=(pltpu.HBM((T,ATTN_WIDTH), BF16), pltpu.HBM((T,ATTN_WIDTH), BF16)),
        scratch_shapes=[pltpu.VMEM((tq + BLOCK, KV_WIDTH), BF16), pltpu.VMEM((tq + BLOCK, KV_WIDTH), BF16),
                        pltpu.VMEM((tq, ATTN_WIDTH), F32),
                        pltpu.VMEM((2 * BLOCK, lanes), F32), pltpu.VMEM((2 * BLOCK, lanes), BF16)],
        compiler_params=_params("arbitrary"),
    )(sinks, *_hbm(q, k, k, v, v, bias_t, gain))


def _attn_bwd(dmixed, raw, q, k, v, bias_t, sinks, gain, name, nblk):
    T = q.shape[0]
    tq = nblk * BLOCK
    nt = T // tq
    lanes = N_Q_HEADS * BLOCK

    def body(sink_ref, dm_ref, raw_ref, q_ref, kc_ref, kp_ref, vc_ref, vp_ref, bias_ref, g_ref,
             dq_ref, dk_ref, dv_ref, dbias_ref, dsink_ref, dgain_ref,
             do_sc, dq_sc, kf_sc, vf_sc, dkf_sc, dvf_sc, st_sc, dpt_sc, pt_sc, dst_sc, drow_sc,
             qs_sc, dos_sc, dsink_sc):
        i = pl.program_id(0)
        tile = nt - 1 - i

        @pl.when(i == 0)
        def _():
            dkf_sc[...] = jnp.zeros_like(dkf_sc)
            dvf_sc[...] = jnp.zeros_like(dvf_sc)
            dsink_sc[...] = jnp.zeros_like(dsink_sc)
            dbias_ref[...] = jnp.zeros_like(dbias_ref)
            dgain_ref[...] = jnp.zeros_like(dgain_ref)

        carry_k = dkf_sc[0:BLOCK, :]
        carry_v = dvf_sc[0:BLOCK, :]
        dkf_sc[0:tq, :] = jnp.zeros((tq, KV_WIDTH), F32)
        dvf_sc[0:tq, :] = jnp.zeros((tq, KV_WIDTH), F32)
        dkf_sc[tq:tq + BLOCK, :] = carry_k
        dvf_sc[tq:tq + BLOCK, :] = carry_v
        _load_band(kf_sc, vf_sc, kp_ref, kc_ref, vp_ref, vc_ref, tq)

        do, dgain = _rms_bwd(dm_ref[...].astype(F32), raw_ref[...].astype(F32), g_ref[...])
        dgain_ref[...] += dgain
        do_sc[...] = do
        ones = jnp.ones((SUBLANES, HEAD_DIM), BF16)

        def block(b, carry):
            r0 = pl.multiple_of(b * BLOCK, BLOCK)
            no_prev = (tile == 0) & (b == 0)
            for g in range(N_KV_HEADS):
                kb = kf_sc[pl.ds(r0, 2 * BLOCK), _head_cols(g)]
                vb = vf_sc[pl.ds(r0, 2 * BLOCK), _head_cols(g)]
                qg = _stack_heads(q_ref, r0, g, BF16)
                dog = _stack_heads(do_sc, r0, g, F32)
                prod = dog * _stack_heads(raw_ref, r0, g, F32)
                hi = prod.astype(BF16)
                lo = (prod - hi.astype(F32)).astype(BF16)
                drow_sc[:, _group_lanes(g)] = _dot_nt(ones, hi) + _dot_nt(ones, lo)
                dogb = dog.astype(BF16)
                qs_sc[g] = qg
                dos_sc[g] = dogb
                st_sc[:, _group_lanes(g)] = _dot_nt(kb, qg)
                dpt_sc[:, _group_lanes(g)] = _dot_nt(vb, dogb)
            for h in range(N_Q_HEADS):
                hl = _head_lanes(h)
                p, ps = _head_softmax(st_sc[:, hl], bias_ref[h], sink_ref[h], no_prev)
                rowdot = drow_sc[0:1, hl]
                ds = p * (dpt_sc[:, hl] - rowdot)
                dsink_sc[h:h + 1, :] += -(ps * rowdot)
                dbias_ref[h] += ds
                dst_sc[:, hl] = ds.astype(BF16)
                pt_sc[:, hl] = p.astype(BF16)
            for g in range(N_KV_HEADS):
                kb = kf_sc[pl.ds(r0, 2 * BLOCK), _head_cols(g)]
                dsg = dst_sc[:, _group_lanes(g)]
                _unstack_heads(dq_sc, r0, g, _dot_tn(dsg, kb) * SCALE)
                dkf_sc[pl.ds(r0, 2 * BLOCK), _head_cols(g)] += _dot(dsg, qs_sc[g]) * SCALE
                dvf_sc[pl.ds(r0, 2 * BLOCK), _head_cols(g)] += _dot(pt_sc[:, _group_lanes(g)], dos_sc[g])
            return carry

        lax.fori_loop(0, nblk, block, 0, unroll=True)
        dq_ref[...] = dq_sc[...].astype(BF16)
        dk_ref[...] = dkf_sc[BLOCK:BLOCK + tq, :].astype(BF16)
        dv_ref[...] = dvf_sc[BLOCK:BLOCK + tq, :].astype(BF16)

        @pl.when(i == nt - 1)
        def _():
            tot = jnp.sum(dsink_sc[...], axis=1, keepdims=True)
            dsink_ref[...] = jnp.broadcast_to(tot, dsink_ref.shape)

    cur = lambda i: (nt - 1 - i, 0)
    prev = lambda i: (jnp.maximum((nt - 1 - i) * nblk - 1, 0), 0)
    const2 = lambda i: (0, 0)
    const3 = lambda i: (0, 0, 0)
    return pl.pallas_call(
        body, name=name, grid=(nt,),
        in_specs=[pl.BlockSpec(memory_space=pltpu.SMEM),
                  pl.BlockSpec((tq, ATTN_WIDTH), cur),
                  pl.BlockSpec((tq, ATTN_WIDTH), cur),
                  pl.BlockSpec((tq, ATTN_WIDTH), cur),
                  pl.BlockSpec((tq, KV_WIDTH), cur), pl.BlockSpec((BLOCK, KV_WIDTH), prev),
                  pl.BlockSpec((tq, KV_WIDTH), cur), pl.BlockSpec((BLOCK, KV_WIDTH), prev),
                  pl.BlockSpec(bias_t.shape, const3),
                  pl.BlockSpec((1, ATTN_WIDTH), const2)],
        out_specs=(pl.BlockSpec((tq, ATTN_WIDTH), cur),
                   pl.BlockSpec((tq, KV_WIDTH), cur), pl.BlockSpec((tq, KV_WIDTH), cur),
                   pl.BlockSpec(bias_t.shape, const3),
                   pl.BlockSpec((N_Q_HEADS, 128), const2),
                   pl.BlockSpec((1, ATTN_WIDTH), const2)),
        out_shape=(pltpu.HBM((T,ATTN_WIDTH), BF16),
                   pltpu.HBM((T,KV_WIDTH), BF16), pltpu.HBM((T,KV_WIDTH), BF16),
                   jax.ShapeDtypeStruct(bias_t.shape, F32),
                   jax.ShapeDtypeStruct((N_Q_HEADS, 128), F32),
                   jax.ShapeDtypeStruct((1, ATTN_WIDTH), F32)),
        scratch_shapes=[pltpu.VMEM((tq, ATTN_WIDTH), F32), pltpu.VMEM((tq, ATTN_WIDTH), F32),
                        pltpu.VMEM((tq + BLOCK, KV_WIDTH), BF16), pltpu.VMEM((tq + BLOCK, KV_WIDTH), BF16),
                        pltpu.VMEM((tq + BLOCK, KV_WIDTH), F32), pltpu.VMEM((tq + BLOCK, KV_WIDTH), F32),
                        pltpu.VMEM((2 * BLOCK, lanes), F32), pltpu.VMEM((2 * BLOCK, lanes), F32),
                        pltpu.VMEM((2 * BLOCK, lanes), BF16), pltpu.VMEM((2 * BLOCK, lanes), BF16),
                        pltpu.VMEM((SUBLANES, lanes), F32),
                        pltpu.VMEM((N_KV_HEADS, GROUP_ROWS, HEAD_DIM), BF16),
                        pltpu.VMEM((N_KV_HEADS, GROUP_ROWS, HEAD_DIM), BF16),
                        pltpu.VMEM((N_Q_HEADS, 128), F32)],
        compiler_params=_params("arbitrary"),
    )(sinks, *_hbm(dmixed, raw, q, k, k, v, v, bias_t, gain))


def _shift_down(cu, tail):
    row = lax.broadcasted_iota(jnp.int32, cu.shape, 0)
    t6, t7 = tail[6:7, :], tail[7:8, :]
    s1 = jnp.where(row == 0, t7, pltpu.roll(cu, 1, 0))
    s2 = jnp.where(row == 0, t6, jnp.where(row == 1, t7, pltpu.roll(cu, 2, 0)))
    return s1, s2


def _shift_up(d, head):
    n = d.shape[0]
    row = lax.broadcasted_iota(jnp.int32, d.shape, 0)
    h0, h1 = head[0:1, :], head[1:2, :]
    s1 = jnp.where(row == n - 1, h0, pltpu.roll(d, n - 1, 0))
    s2 = jnp.where(row == n - 1, h1, jnp.where(row == n - 2, h0, pltpu.roll(d, n - 2, 0)))
    return s1, s2


def _mixout_fwd(x, attn_n, u, gb, gc, conv_w, gain, w_out, name, tm):
    T, D = x.shape

    def body(x_ref, an_ref, u_ref, b_ref, c_ref, cw_ref, g_ref, wo_ref, xo_ref, cn_ref, tail_sc):
        @pl.when(pl.program_id(0) == 0)
        def _():
            tail_sc[...] = jnp.zeros_like(tail_sc)

        cu = c_ref[...].astype(F32) * u_ref[...].astype(F32)
        s1, s2 = _shift_down(cu, tail_sc[...])
        tail_sc[...] = cu[tm - SUBLANES:tm, :]
        pre = cw_ref[0:1, :] * s2 + cw_ref[1:2, :] * s1 + cw_ref[2:3, :] * cu
        conv = b_ref[...].astype(F32) * pre
        _, chat = _rms_stats(conv)
        cn = (chat * g_ref[...]).astype(BF16)
        cn_ref[...] = cn
        xo_ref[...] = (x_ref[...] + _dot(an_ref[...], wo_ref[0:ATTN_WIDTH, :])
                       + _dot(cn, wo_ref[ATTN_WIDTH:ATTN_WIDTH + CONV_DIM, :]))

    row = lambda i: (i, 0)
    const = lambda i: (0, 0)
    return pl.pallas_call(
        body, name=name, grid=(T // tm,),
        in_specs=[pl.BlockSpec((tm, D), row), pl.BlockSpec((tm, ATTN_WIDTH), row),
                  pl.BlockSpec((tm, CONV_DIM), row), pl.BlockSpec((tm, CONV_DIM), row),
                  pl.BlockSpec((tm, CONV_DIM), row),
                  pl.BlockSpec(conv_w.shape, const), pl.BlockSpec((1, CONV_DIM), const),
                  pl.BlockSpec(w_out.shape, const)],
        out_specs=(pl.BlockSpec((tm, D), row), pl.BlockSpec((tm, CONV_DIM), row)),
        out_shape=(pltpu.HBM((T,D), F32), pltpu.HBM((T,CONV_DIM), BF16)),
        scratch_shapes=[pltpu.VMEM((SUBLANES, CONV_DIM), F32)],
        compiler_params=_params("arbitrary"),
    )(*_hbm(x, attn_n, u, gb, gc, conv_w, gain, w_out))


def _mixout_bwd(dy, attn_n, conv_n, w_out, after, name, tm):
    T, D = dy.shape
    W = ATTN_WIDTH + CONV_DIM
    nt = T // tm

    def body(dy_ref, an_ref, cn_ref, w_ref, after_ref, dm_ref, dw_ref, dw_sc):
        i = pl.program_id(0)

        @pl.when(i == 0)
        def _():
            dw_sc[...] = jnp.zeros_like(dw_sc)

        dyb = dy_ref[...].astype(BF16)
        dm_ref[...] = _dot_nt(dyb, w_ref[...]).astype(BF16)
        dw_sc[0:ATTN_WIDTH, :] += _dot_tn(an_ref[...], dyb)
        dw_sc[ATTN_WIDTH:W, :] += _dot_tn(cn_ref[...], dyb)

        @pl.when(i == nt - 1)
        def _():
            dw_ref[...] = dw_sc[...].astype(BF16)

    row = lambda i: (i, 0)
    const = lambda i: (0, 0)
    return pl.pallas_call(
        body, name=name, grid=(nt,),
        in_specs=[pl.BlockSpec((tm, D), row), pl.BlockSpec((tm, ATTN_WIDTH), row),
                  pl.BlockSpec((tm, CONV_DIM), row), pl.BlockSpec(w_out.shape, const),
                  pl.BlockSpec(memory_space=pl.ANY)],
        out_specs=(pl.BlockSpec((tm, W), row), pl.BlockSpec((W, D), const)),
        out_shape=(pltpu.HBM((T,W), BF16), pltpu.HBM((W, D), BF16)),
        scratch_shapes=[pltpu.VMEM((W, D), F32)],
        compiler_params=_params("arbitrary"),
    )(*_hbm(dy, attn_n, conv_n, w_out, after))


def _conv_bwd(dmixed, u, gb, gc, conv_w, gain, name, tc):
    T = u.shape[0]
    nt = T // tc
    per8 = tc // SUBLANES

    def body(dm_ref, u_ref, b_ref, c_ref, ut_ref, ct_ref, cw_ref, g_ref,
             du_ref, db_ref, dc_ref, dcw_ref, dgain_ref, head_sc):
        i = pl.program_id(0)

        @pl.when(i == 0)
        def _():
            head_sc[...] = jnp.zeros_like(head_sc)
            dcw_ref[...] = jnp.zeros_like(dcw_ref)
            dgain_ref[...] = jnp.zeros_like(dgain_ref)

        uu = u_ref[...].astype(F32)
        cc = c_ref[...].astype(F32)
        bb = b_ref[...].astype(F32)
        cu = cc * uu
        tail = jnp.where(i == nt - 1, 0.0, ct_ref[...].astype(F32) * ut_ref[...].astype(F32))
        s1, s2 = _shift_down(cu, tail)
        w0, w1, w2 = cw_ref[0:1, :], cw_ref[1:2, :], cw_ref[2:3, :]
        pre = w0 * s2 + w1 * s1 + w2 * cu
        dconv, dgain = _rms_bwd(dm_ref[...].astype(F32), bb * pre, g_ref[...])
        dgain_ref[...] += dgain
        db_ref[...] = (dconv * pre).astype(BF16)
        dpre = dconv * bb
        dcw_ref[0:1, :] += jnp.sum(dpre * s2, axis=0, keepdims=True)
        dcw_ref[1:2, :] += jnp.sum(dpre * s1, axis=0, keepdims=True)
        dcw_ref[2:3, :] += jnp.sum(dpre * cu, axis=0, keepdims=True)
        n1, n2 = _shift_up(dpre, head_sc[...])
        head_sc[...] = dpre[0:SUBLANES, :]
        dcu = w2 * dpre + w1 * n1 + w0 * n2
        du_ref[...] = (dcu * cc).astype(BF16)
        dc_ref[...] = (dcu * uu).astype(BF16)

    rev = lambda i: (nt - 1 - i, 0)
    rev_right = lambda i: (nt - 1 - i, 1)
    tail_map = lambda i: (jnp.maximum((nt - 1 - i) * per8 - 1, 0), 0)
    const = lambda i: (0, 0)
    return pl.pallas_call(
        body, name=name, grid=(nt,),
        in_specs=[pl.BlockSpec((tc, CONV_DIM), rev_right),
                  pl.BlockSpec((tc, CONV_DIM), rev), pl.BlockSpec((tc, CONV_DIM), rev),
                  pl.BlockSpec((tc, CONV_DIM), rev),
                  pl.BlockSpec((SUBLANES, CONV_DIM), tail_map), pl.BlockSpec((SUBLANES, CONV_DIM), tail_map),
                  pl.BlockSpec(conv_w.shape, const), pl.BlockSpec((1, CONV_DIM), const)],
        out_specs=(pl.BlockSpec((tc, CONV_DIM), rev), pl.BlockSpec((tc, CONV_DIM), rev),
                   pl.BlockSpec((tc, CONV_DIM), rev),
                   pl.BlockSpec((SUBLANES, CONV_DIM), const), pl.BlockSpec((1, CONV_DIM), const)),
        out_shape=(pltpu.HBM((T,CONV_DIM), BF16), pltpu.HBM((T,CONV_DIM), BF16),
                   pltpu.HBM((T,CONV_DIM), BF16),
                   jax.ShapeDtypeStruct((SUBLANES, CONV_DIM), F32), jax.ShapeDtypeStruct((1, CONV_DIM), F32)),
        scratch_shapes=[pltpu.VMEM((SUBLANES, CONV_DIM), F32)],
        compiler_params=_params("arbitrary"),
    )(*_hbm(dmixed, u, gb, gc, u, gc, conv_w, gain))


def _mixin_bwd(dy, x, gain, dz, w_in_t, name, tm):
    T, D = x.shape
    nz = len(dz)

    def body(dy_ref, x_ref, g_ref, *rest):
        dz_refs, wt_ref, dx_ref, dgain_ref, dz_sc = rest[:nz], rest[nz], rest[nz + 1], rest[nz + 2], rest[nz + 3]

        @pl.when(pl.program_id(0) == 0)
        def _():
            dgain_ref[...] = jnp.zeros_like(dgain_ref)

        for r, lo, hi in zip(dz_refs, _Z_SPLITS[:-1], _Z_SPLITS[1:]):
            dz_sc[:, lo:hi] = r[...]
        dx, dgain = _rms_bwd(_dot(dz_sc[...], wt_ref[...]), x_ref[...], g_ref[...])
        dgain_ref[...] += dgain
        dx_ref[...] = dy_ref[...] + dx

    row = lambda i: (i, 0)
    const = lambda i: (0, 0)
    return pl.pallas_call(
        body, name=name, grid=(T // tm,),
        in_specs=[pl.BlockSpec((tm, D), row), pl.BlockSpec((tm, D), row), pl.BlockSpec((1, D), const)]
                 + [pl.BlockSpec((tm, a.shape[1]), row) for a in dz]
                 + [pl.BlockSpec(w_in_t.shape, const)],
        out_specs=(pl.BlockSpec((tm, D), row), pl.BlockSpec((1, D), const)),
        out_shape=(pltpu.HBM((T,D), F32), jax.ShapeDtypeStruct((1, D), F32)),
        scratch_shapes=[pltpu.VMEM((tm, _Z_SPLITS[-1]), BF16)],
        compiler_params=_params("arbitrary"),
    )(*_hbm(dy, x, gain, *dz, w_in_t))


def _win_grad(dz, hn, name, tk):
    T, D = hn.shape
    nz = len(dz)
    nt = T // tk
    W = _Z_SPLITS[-1]

    def body(hn_ref, *rest):
        dz_refs, dw_ref, dw_sc, kv_sc = rest[:nz], rest[nz], rest[nz + 1], rest[nz + 2]
        i = pl.program_id(0)

        @pl.when(i == 0)
        def _():
            dw_sc[...] = jnp.zeros_like(dw_sc)

        hn = hn_ref[...]
        kv_sc[:, :KV_WIDTH] = dz_refs[1][...]
        kv_sc[:, KV_WIDTH:] = dz_refs[2][...]
        dw_sc[_Z_SPLITS[1]:_Z_SPLITS[3], :] += _dot_tn(kv_sc[...], hn)
        for n in (0, 3, 4, 5):
            dw_sc[_Z_SPLITS[n]:_Z_SPLITS[n + 1], :] += _dot_tn(dz_refs[n][...], hn)

        @pl.when(i == nt - 1)
        def _():
            dw_ref[...] = dw_sc[...].astype(BF16)

    row = lambda i: (i, 0)
    return pl.pallas_call(
        body, name=name, grid=(nt,),
        in_specs=[pl.BlockSpec((tk, D), row)] + [pl.BlockSpec((tk, a.shape[1]), row) for a in dz],
        out_specs=pl.BlockSpec((W, D), lambda i: (0, 0)),
        out_shape=pltpu.HBM((W, D), BF16),
        scratch_shapes=[pltpu.VMEM((W, D), F32), pltpu.VMEM((tk, 2 * KV_WIDTH), BF16)],
        compiler_params=_params("arbitrary"),
    )(*_hbm(hn, *dz))


def _adamw(parts, w, m, v, name, tr):
    P = parts.shape[0]
    R, C = w.shape

    def body(p_ref, w_ref, m_ref, v_ref, g_ref, d_ref, nm_ref, nv_ref):
        g = p_ref[0].astype(F32)
        for d in range(1, P):
            g = g + p_ref[d].astype(F32)
        nm = ADAM_B1 * m_ref[...] + (1.0 - ADAM_B1) * g
        nv = ADAM_B2 * v_ref[...] + (1.0 - ADAM_B2) * (g * g)
        m_hat = nm / (1.0 - ADAM_B1 ** ADAM_STEP)
        v_hat = nv / (1.0 - ADAM_B2 ** ADAM_STEP)
        g_ref[...] = g
        nm_ref[...] = nm
        nv_ref[...] = nv
        d_ref[...] = -ADAM_LR * (m_hat / (jnp.sqrt(v_hat) + ADAM_EPS) + ADAM_WD * w_ref[...])

    row = lambda i: (i, 0)
    spec = pl.BlockSpec((tr, C), row)
    shp = pltpu.HBM((R, C), F32)
    return pl.pallas_call(
        body, name=name, grid=(R // tr,),
        in_specs=[pl.BlockSpec((P, tr, C), lambda i: (0, i, 0)), spec, spec, spec],
        out_specs=(spec, spec, spec, spec),
        out_shape=(shp, shp, shp, shp),
        compiler_params=_params("arbitrary"),
    )(*_hbm(parts, w, m, v))


def _columns_of_blocks(g):
    n, R, w = g.shape
    return g.transpose(1, 0, 2).reshape(R, n * w)


def _pad_row(vec):
    vec = vec.reshape(1, -1)
    return jnp.pad(vec, ((0, 0), (0, PACK_COLS - vec.shape[1])))


def kernel(x, rel_bias_table, ffn1_norm, ffn1_w_gate, ffn1_w_up, ffn1_w_down, mix_norm, w_in, conv_w, attn_sinks, attn_out_norm, conv_out_norm, w_out, ffn2_norm, ffn2_w_gate, ffn2_w_up, ffn2_w_down, final_norm, loss_target, m_rel_bias_table, m_ffn1_norm, m_ffn1_w_gate, m_ffn1_w_up, m_ffn1_w_down, m_mix_norm, m_w_in, m_conv_w, m_attn_sinks, m_attn_out_norm, m_conv_out_norm, m_w_out, m_ffn2_norm, m_ffn2_w_gate, m_ffn2_w_up, m_ffn2_w_down, m_final_norm, v_rel_bias_table, v_ffn1_norm, v_ffn1_w_gate, v_ffn1_w_up, v_ffn1_w_down, v_mix_norm, v_w_in, v_conv_w, v_attn_sinks, v_attn_out_norm, v_conv_out_norm, v_w_out, v_ffn2_norm, v_ffn2_w_gate, v_ffn2_w_up, v_ffn2_w_down, v_final_norm):
    T, D = x.shape[1], x.shape[2]
    x0 = x[0]
    target = loss_target[0]
    tm = min(TM_FFN, T)
    tm_bwd = min(TM_FFN_BWD, T)
    tm_mix = min(TM_MIX, T)
    tk = min(TK_WGRAD, T)
    tf = TF_FFN
    nblk = min(ATTN_BLOCKS, T // BLOCK)
    me = 4 * lax.axis_index("x") + 2 * lax.axis_index("y") + lax.axis_index("c")

    big = {
        "ffn1_w_gate": (ffn1_w_gate[0], m_ffn1_w_gate[0], v_ffn1_w_gate[0], True),
        "ffn1_w_up": (ffn1_w_up[0], m_ffn1_w_up[0], v_ffn1_w_up[0], True),
        "ffn1_w_down": (ffn1_w_down[0], m_ffn1_w_down[0], v_ffn1_w_down[0], False),
        "w_in": (w_in[0], m_w_in[0], v_w_in[0], True),
        "w_out": (w_out[0], m_w_out[0], v_w_out[0], False),
        "ffn2_w_gate": (ffn2_w_gate[0], m_ffn2_w_gate[0], v_ffn2_w_gate[0], True),
        "ffn2_w_up": (ffn2_w_up[0], m_ffn2_w_up[0], v_ffn2_w_up[0], True),
        "ffn2_w_down": (ffn2_w_down[0], m_ffn2_w_down[0], v_ffn2_w_down[0], False),
    }

    def block_to_send(name):
        w, _, _, transposed = big[name]
        return (w.T if transposed else w).astype(BF16)

    names1 = ["ffn1_w_gate", "ffn1_w_up", "ffn1_w_down"]
    names_rest = ["w_in", "w_out", "ffn2_w_gate", "ffn2_w_up", "ffn2_w_down"]
    first = _gather_two_level([block_to_send(n) for n in names1], "gather_ffn1", ffn1_norm)
    wgt1, wut1, wd1 = [g.reshape(-1, D) for g in first]
    h_rest, token = _exchange_start([block_to_send(n) for n in names_rest[:2]] + [conv_w[0]]
                                    + [block_to_send(n) for n in names_rest[2:]], False,
                                    "gather_start_rest", wd1)

    wgu1 = _stack_gate_up(wgt1, wut1, tf)
    x1, xn1, gu1 = _ffn_fwd(x0, ffn1_norm, wgu1, wd1, token, "ffn1_fwd", tm, tf)
    mixw = _exchange_wait(h_rest[:3], False, "gather_wait_mix", x1)
    win_t = mixw[0].reshape(-1, D)
    wout = mixw[1].reshape(-1, D)
    cw = _columns_of_blocks(mixw[2])
    hn, q, k, v, u, gb, gc = _mixin_fwd(x1, mix_norm, win_t, "mixin_fwd", tm_mix)
    bucket = jnp.asarray(_bucket_table().T.copy())
    sinks = attn_sinks.reshape(-1)
    bias_t = _bias_build(rel_bias_table, bucket, "bias_build")
    attn_raw, attn_n = _attn_fwd(q, k, v, bias_t, sinks, attn_out_norm, "attn_fwd", nblk)
    x2, conv_n = _mixout_fwd(x1, attn_n, u, gb, gc, cw, conv_out_norm, wout, "mixout_fwd", tm_mix)
    wgt2, wut2, wd2 = [g.reshape(-1, D) for g in _exchange_wait(h_rest[3:], False, "gather_wait_ffn2", x2)]
    wgu2 = _stack_gate_up(wgt2, wut2, tf)
    dx3, xn2, gu2, d_final, loss_part = _ffn_fwd(x2, ffn2_norm, wgu2, wd2, x2, "ffn2_fwd", tm, tf,
                                                 head=(target, final_norm.reshape(1, D)))

    def blocks(g):
        return g.reshape(N_DEV, -1, D)

    dx2, d_ffn2_norm, dg2, du2, hh2, do2 = _ffn_bwd(
        dx3, x2, ffn2_norm, gu2, wgu2, wd2, dx3, "ffn2_bwd", tm_bwd, tf)
    d_wg2 = _tn_grad(dg2, xn2, dx2, "ffn2_wgrad_gate", BM_WGRAD, tk)
    d_wu2 = _tn_grad(du2, xn2, dx2, "ffn2_wgrad_up", BM_WGRAD, tk)
    d_wd2 = _tn_grad(hh2, do2, dx2, "ffn2_wgrad_down", BM_WGRAD, tk)
    handles2, token2 = _exchange_start([blocks(d_wg2), blocks(d_wu2), blocks(d_wd2)], True,
                                       "grads_start_ffn2", dx2)

    dmixed, d_wout = _mixout_bwd(dx2, attn_n, conv_n, wout, token2, "mixout_bwd", tm_mix)
    dq, dk, dv, dbias, dsink, d_attn_norm = _attn_bwd(
        dmixed, attn_raw, q, k, v, bias_t, sinks, attn_out_norm, "attn_bwd", nblk)
    du, dgb, dgc, d_cw, d_conv_norm = _conv_bwd(dmixed, u, gb, gc, cw, conv_out_norm, "conv_bwd", tm_mix)
    d_table = _bias_grad(dbias, bucket, "bias_grad")
    dz = [dq, dk, dv, du, dgb, dgc]
    dx1, d_mix_norm = _mixin_bwd(dx2, x1, mix_norm, dz, win_t, "mixin_bwd", tm_mix)
    d_win_t = _win_grad(dz, hn, "win_grad", min(TM_MIX, T))
    handles_mix, token_mix = _exchange_start([blocks(d_win_t), blocks(d_wout)], True, "grads_start_mix", d_table)

    dx0, d_ffn1_norm, dg1, du1, hh1, do1 = _ffn_bwd(
        dx1, x0, ffn1_norm, gu1, wgu1, wd1, token_mix, "ffn1_bwd", tm_bwd, tf)
    def pack(ffn1, mixn, ffn2, fin, attn_n_, conv_n_, sink_, extra, convw, table):
        rows = [_pad_row(ffn1), _pad_row(mixn), _pad_row(ffn2), _pad_row(fin),
                _pad_row(jnp.concatenate([attn_n_.reshape(-1), conv_n_.reshape(-1)])),
                _pad_row(sink_), _pad_row(extra),
                jnp.zeros((1, PACK_COLS), F32),
                jnp.pad(convw, ((0, 0), (0, PACK_COLS - convw.shape[1]))),
                _pad_row(table),
                jnp.zeros((PACK_ROWS - 12, PACK_COLS), F32)]
        return jnp.concatenate(rows, axis=0)

    def own_channels(a):
        full = jnp.zeros((a.shape[1], CONV_DIM), F32)
        return lax.dynamic_update_slice(full, a[0], (0, me * a.shape[2]))

    g_pack = pack(d_ffn1_norm, d_mix_norm, d_ffn2_norm, d_final, d_attn_norm, d_conv_norm,
                  dsink[:, 0], loss_part[0, :1], d_cw[:3], d_table[:, :N_Q_HEADS])
    zero1 = jnp.zeros((1,), F32)
    w_pack = pack(ffn1_norm, mix_norm, ffn2_norm, final_norm, attn_out_norm, conv_out_norm,
                  attn_sinks, zero1, own_channels(conv_w), rel_bias_table)
    m_pack = pack(m_ffn1_norm, m_mix_norm, m_ffn2_norm, m_final_norm, m_attn_out_norm, m_conv_out_norm,
                  m_attn_sinks, zero1, own_channels(m_conv_w), m_rel_bias_table)
    v_pack = pack(v_ffn1_norm, v_mix_norm, v_ffn2_norm, v_final_norm, v_attn_out_norm, v_conv_out_norm,
                  v_attn_sinks, zero1, own_channels(v_conv_w), v_rel_bias_table)
    (g_all,) = _exchange([g_pack], False, "gather_small", dx0)
    packs = _adamw(g_all, w_pack, m_pack, v_pack, "adamw_small", PACK_ROWS)

    d_wd1 = _tn_grad(hh1, do1, packs[0], "ffn1_wgrad_down", BM_WGRAD, tk)
    handles1_d, token1 = _exchange_start([blocks(d_wd1)], True, "grads_start_ffn1_down", dx0)
    d_wg1 = _tn_grad(dg1, xn1, token1, "ffn1_wgrad_gate", BM_WGRAD, tk)
    handles1_g, token1 = _exchange_start([blocks(d_wg1)], True, "grads_start_ffn1_gate", token1)
    d_wu1 = _tn_grad(du1, xn1, token1, "ffn1_wgrad_up", BM_WGRAD, tk)
    handles1_u, token1 = _exchange_start([blocks(d_wu1)], True, "grads_start_ffn1_up", token1)

    res = {}

    def update(names, parts):
        last = None
        for name, p in zip(names, parts):
            w, m_, v_, transposed = big[name]
            if transposed:
                w, m_, v_ = w.T, m_.T, v_.T
            new = _adamw(p, w, m_, v_, "adamw_" + name, _row_tile(w.shape[0], ADAM_ROWS))
            res[name] = tuple((a.T if transposed else a)[None] for a in new)
            last = new[0]
        return last

    parts2 = _exchange_wait(handles2, True, "grads_wait_ffn2", token1)
    done2 = update(["ffn2_w_gate", "ffn2_w_up", "ffn2_w_down"], parts2)
    parts_mix = _exchange_wait(handles_mix, True, "grads_wait_mix", done2)
    done_mix = update(["w_in", "w_out"], parts_mix)

    parts1_d = _exchange_wait(handles1_d, True, "grads_wait_ffn1_down", done_mix)
    done1 = update(["ffn1_w_down"], parts1_d)
    parts1_g = _exchange_wait(handles1_g, True, "grads_wait_ffn1_gate", done1)
    done1 = update(["ffn1_w_gate"], parts1_g)
    parts1_u = _exchange_wait(handles1_u, True, "grads_wait_ffn1_up", done1)
    update(["ffn1_w_up"], parts1_u)

    def unpack(pk):
        cwb = lax.dynamic_slice(pk[8:11, :CONV_DIM], (0, me * conv_w.shape[2]), (3, conv_w.shape[2]))
        return {
            "ffn1_norm": pk[0:1, :D], "mix_norm": pk[1:2, :D], "ffn2_norm": pk[2:3, :D],
            "final_norm": pk[3, :D],
            "attn_out_norm": pk[4:5, :ATTN_WIDTH], "conv_out_norm": pk[4:5, ATTN_WIDTH:ATTN_WIDTH + CONV_DIM],
            "attn_sinks": pk[5:6, :N_Q_HEADS],
            "conv_w": cwb[None],
            "rel_bias_table": pk[11, :NUM_BUCKETS * N_Q_HEADS].reshape(NUM_BUCKETS, N_Q_HEADS),
        }

    small = [unpack(pk) for pk in packs]
    loss = packs[0][6, 0]

    order = ["rel_bias_table", "ffn1_norm", "ffn1_w_gate", "ffn1_w_up", "ffn1_w_down", "mix_norm", "w_in",
             "conv_w", "attn_sinks", "attn_out_norm", "conv_out_norm", "w_out", "ffn2_norm",
             "ffn2_w_gate", "ffn2_w_up", "ffn2_w_down", "final_norm"]
    outs = [loss, dx0[None]]
    for kind in range(4):
        for name in order:
            outs.append(res[name][kind] if name in res else small[kind][name])
    return tuple(outs)
```

```python
import math

import numpy as np
import jax
import jax.numpy as jnp
from jax import lax
from jax.experimental import pallas as pl
from jax.experimental.pallas import tpu as pltpu

F32 = jnp.float32
BF16 = jnp.bfloat16

N_DEV = 8
EPS = 1e-6
HEAD_DIM = 64
N_Q_HEADS = 8
N_KV_HEADS = 2
GQA_GROUP = 4
ATTN_WIDTH = 512
KV_WIDTH = 128
CONV_DIM = 512
BLOCK = 128
WINDOW = 128
NUM_BUCKETS = 32
MAX_DISTANCE = 128
SCALE = HEAD_DIM ** -0.5
MASKED = -1e30
GROUP_ROWS = GQA_GROUP * BLOCK

ADAM_LR = 0.001
ADAM_B1 = 0.9
ADAM_B2 = 0.999
ADAM_EPS = 1e-08
ADAM_WD = 0.01
ADAM_STEP = 10

VMEM_LIMIT_BYTES = 40 * 1024 * 1024
VMEM_LIMIT_FFN_BYTES = 50 * 1024 * 1024
SUBLANES = 8
LANES = 128
PACK_ROWS = 16
PACK_COLS = 1024

TM_FFN = 1024
TM_FFN_BWD = 1024
TM_MIX = 512
TK_WGRAD = 2048
TF_FFN = 256
BM_WGRAD = 1408
ROW_GROUPS = 4
ATTN_BLOCKS = 8
ADAM_ROWS = 256


def _row_tile(rows, limit):
    best = rows
    for t in range(16, min(rows, limit) + 1, 16):
        if rows % t == 0:
            best = t
    return best


def _params(*sem):
    return pltpu.CompilerParams(dimension_semantics=sem, vmem_limit_bytes=VMEM_LIMIT_BYTES)


def _hbm(*arrays):
    return [pltpu.with_memory_space_constraint(a, pltpu.HBM) for a in arrays]


def _dot(a, b):
    return jnp.dot(a, b, preferred_element_type=F32)


def _dot_nt(a, b):
    return lax.dot_general(a, b, (((1,), (1,)), ((), ())), preferred_element_type=F32)


def _dot_tn(a, b):
    return lax.dot_general(a, b, (((0,), (0,)), ((), ())), preferred_element_type=F32)


def _sigmoid(g):
    return 0.5 * jnp.tanh(0.5 * g) + 0.5


def _rms_stats(x):
    inv = lax.rsqrt(jnp.mean(x * x, axis=-1, keepdims=True) + EPS)
    return inv, x * inv


def _rms_bwd(dy, x, gain):
    inv, xhat = _rms_stats(x)
    dgain = jnp.sum(dy * xhat, axis=0, keepdims=True)
    dxh = dy * gain
    dx = inv * (dxh - xhat * jnp.mean(dxh * xhat, axis=-1, keepdims=True))
    return dx, dgain


def _peer_list():
    x, y, c = lax.axis_index("x"), lax.axis_index("y"), lax.axis_index("c")
    peers = []
    for k in range(1, N_DEV):
        px = 1 - x if (k >> 2) & 1 else x
        py = 1 - y if (k >> 1) & 1 else y
        pc = 1 - c if k & 1 else c
        peers.append((px, py, pc))
    return 4 * x + 2 * y + c, peers


def _gather_two_level(arrs, name, after):
    n = len(arrs)
    out_shape = tuple(jax.ShapeDtypeStruct((N_DEV,) + a.shape, a.dtype) for a in arrs)

    def body(*refs):
        ins, outs = refs[:n], refs[n + 1:2 * n + 1]
        send_sems, recv_sems, local_sems = refs[2 * n + 1:]
        x, y, c = lax.axis_index("x"), lax.axis_index("y"), lax.axis_index("c")
        me, sibling = (x, y, c), (x, y, 1 - c)
        chips = [(1 - x, y), (x, 1 - y), (1 - x, 1 - y)]

        def copy(a, k, block, to, src=None):
            slot = outs[a].at[4 * block[0] + 2 * block[1] + block[2]]
            return pltpu.make_async_remote_copy(
                src_ref=slot if src is None else src, dst_ref=slot,
                send_sem=send_sems.at[a, k], recv_sem=recv_sems.at[a, k],
                device_id=to, device_id_type=pl.DeviceIdType.MESH)

        started = []
        for a in range(n):
            loc = pltpu.make_async_copy(ins[a], outs[a].at[4 * x + 2 * y + c], local_sems.at[a])
            loc.start()
            started.append(loc)
        sends = []
        for a in range(n):
            sends.append(copy(a, 0, me, sibling, src=ins[a]))
            sends += [copy(a, 1 + j, me, (*chip, c), src=ins[a]) for j, chip in enumerate(chips)]
        for cp in sends:
            cp.start()
        for j, chip in enumerate(chips):
            for a in range(n):
                copy(a, 1 + j, (*chip, c), me).wait_recv()
                fwd = copy(a, 4 + j, (*chip, c), sibling)
                fwd.start()
                sends.append(fwd)
        for a in range(n):
            copy(a, 0, sibling, me).wait_recv()
            for j, chip in enumerate(chips):
                copy(a, 4 + j, (*chip, 1 - c), me).wait_recv()
        for cp in sends:
            cp.wait_send()
        for loc in started:
            loc.wait()

    hbm = pl.BlockSpec(memory_space=pl.ANY)
    return pl.pallas_call(
        body, name=name, out_shape=out_shape,
        in_specs=[hbm] * (n + 1), out_specs=tuple([hbm] * n),
        scratch_shapes=[pltpu.SemaphoreType.DMA((n, N_DEV - 1)),
                        pltpu.SemaphoreType.DMA((n, N_DEV - 1)),
                        pltpu.SemaphoreType.DMA((n,))],
    )(*arrs, after)


_HBM = pl.BlockSpec(memory_space=pltpu.HBM)
_SEM = pl.BlockSpec(memory_space=pltpu.SEMAPHORE)
_EFFECT = pltpu.SideEffectType.DATAFLOW_SIDE_EFFECTING


def _split_copies(srcs, lands, send_sems, recv_sems, scatter):
    me, peers = _peer_list()
    copies = []
    for a in range(len(srcs)):
        for k, (px, py, pc) in enumerate(peers):
            src = srcs[a].at[4 * px + 2 * py + pc] if scatter else srcs[a]
            copies.append(pltpu.make_async_remote_copy(
                src_ref=src, dst_ref=lands[a].at[me],
                send_sem=send_sems[a].at[k], recv_sem=recv_sems[a].at[k],
                device_id=(px, py, pc), device_id_type=pl.DeviceIdType.MESH))
    return copies


def _own_copies(srcs, lands, recv_sems, scatter):
    me, _ = _peer_list()
    return [pltpu.make_async_copy(srcs[a].at[me] if scatter else srcs[a], lands[a].at[me],
                                  recv_sems[a].at[N_DEV - 1]) for a in range(len(srcs))]


def _exchange_start(arrs, scatter, name, after):
    n = len(arrs)
    lands = [lax.empty(a.shape if scatter else (N_DEV,) + a.shape, a.dtype) for a in arrs]

    def body(*refs):
        srcs, lnds = refs[:n], refs[n:2 * n]
        outs = refs[2 * n + 1:]
        send_sems, recv_sems = outs[:n], outs[n:2 * n]
        token = outs[4 * n]
        for cp in _split_copies(srcs, lnds, send_sems, recv_sems, scatter):
            cp.start()
        for cp in _own_copies(srcs, lnds, recv_sems, scatter):
            cp.start()
        token[...] = jnp.zeros_like(token)

    sem = pltpu.SemaphoreType.DMA((N_DEV,))
    out_shape = ([sem] * (2 * n) + [pltpu.HBM(a.shape, a.dtype) for a in arrs]
                 + [pltpu.HBM(l.shape, l.dtype) for l in lands] + [jax.ShapeDtypeStruct((SUBLANES, LANES), F32)])
    res = pl.pallas_call(
        body, name=name, out_shape=tuple(out_shape),
        in_specs=[_HBM] * (2 * n) + [pl.BlockSpec(memory_space=pl.ANY)],
        out_specs=tuple([_SEM] * (2 * n) + [_HBM] * (2 * n) + [pl.BlockSpec(memory_space=pltpu.VMEM)]),
        input_output_aliases={i: 2 * n + i for i in range(2 * n)},
        compiler_params=pltpu.CompilerParams(has_side_effects=_EFFECT),
    )(*[pltpu.with_memory_space_constraint(a, pltpu.HBM) for a in arrs],
      *[pltpu.with_memory_space_constraint(l, pltpu.HBM) for l in lands], after)
    handles = [(res[2 * n + a], res[3 * n + a], res[a], res[n + a]) for a in range(n)]
    return handles, res[4 * n]


def _exchange_wait(handles, scatter, name, after):
    n = len(handles)

    def body(*refs):
        srcs, lnds = refs[:n], refs[n:2 * n]
        send_sems, recv_sems = refs[2 * n:3 * n], refs[3 * n:4 * n]
        for cp in _split_copies(srcs, lnds, send_sems, recv_sems, scatter):
            cp.wait_send()
            cp.wait_recv()
        for cp in _own_copies(srcs, lnds, recv_sems, scatter):
            cp.wait()

    srcs = [h[0] for h in handles]
    lands = [h[1] for h in handles]
    res = pl.pallas_call(
        body, name=name,
        out_shape=tuple([pltpu.HBM(a.shape, a.dtype) for a in srcs] + [pltpu.HBM(l.shape, l.dtype) for l in lands]),
        in_specs=[_HBM] * (2 * n) + [_SEM] * (2 * n) + [pl.BlockSpec(memory_space=pl.ANY)],
        out_specs=tuple([_HBM] * (2 * n)),
        input_output_aliases={i: i for i in range(2 * n)},
        compiler_params=pltpu.CompilerParams(has_side_effects=_EFFECT),
    )(*srcs, *lands, *[h[2] for h in handles], *[h[3] for h in handles], after)
    return list(res[n:])


def _row_groups(tm):
    return [slice(r * (tm // ROW_GROUPS), (r + 1) * (tm // ROW_GROUPS)) for r in range(ROW_GROUPS)]


def _stack_gate_up(wgt, wut, tf):
    F, D = wgt.shape
    return jnp.stack([wgt.reshape(F // tf, tf, D), wut.reshape(F // tf, tf, D)], axis=1).reshape(2 * F, D)


def _ffn_fwd(x, gain, wgu, wd, after, name, tm, tf, head=None):
    T, D = x.shape
    F = wd.shape[0]
    nj = F // tf
    n_in = 7 if head else 5

    def body(*refs):
        x_ref, g_ref, wgu_ref, wd_ref, after_ref = refs[:5]
        outs = refs[n_in:]
        xo_ref, xn_ref, gu_ref = outs[:3]
        xn_sc, acc_sc = outs[-2:]
        i, j = pl.program_id(0), pl.program_id(1)

        @pl.when(j == 0)
        def _():
            _, xhat = _rms_stats(x_ref[...])
            xn = (xhat * g_ref[...]).astype(BF16)
            xn_sc[...] = xn
            xn_ref[...] = xn
            acc_sc[...] = jnp.zeros_like(acc_sc)

        groups = _row_groups(tm)
        gus = [_dot_nt(xn_sc[rows, :], wgu_ref[...]) for rows in groups]
        hs = []
        for rows, gu in zip(groups, gus):
            gu_ref[rows, :] = gu.astype(BF16)
            g, u = gu[:, :tf], gu[:, tf:]
            hs.append((g * _sigmoid(g) * u).astype(BF16))
        for rows, h in zip(groups, hs):
            acc_sc[rows, :] += _dot(h, wd_ref[...])

        if head:
            t_ref, fg_ref = refs[5:7]
            dgain_ref, loss_ref = outs[3:5]

            @pl.when((i == 0) & (j == 0))
            def _():
                dgain_ref[...] = jnp.zeros_like(dgain_ref)
                loss_ref[...] = jnp.zeros_like(loss_ref)

        @pl.when(j == nj - 1)
        def _():
            for rows in groups:
                xo = x_ref[rows, :] + 0.5 * acc_sc[rows, :]
                if head:
                    fg = fg_ref[...]
                    _, xhat = _rms_stats(xo)
                    err = xhat * fg - t_ref[rows, :]
                    loss_ref[...] += 0.5 * jnp.sum(jnp.mean(err * err, axis=-1, keepdims=True),
                                                   axis=0, keepdims=True)
                    dx, dgain = _rms_bwd(err * (1.0 / D), xo, fg)
                    dgain_ref[...] += dgain
                    xo_ref[rows, :] = dx
                else:
                    xo_ref[rows, :] = xo

    tile = pl.BlockSpec((tm, D), lambda i, j: (i, 0))
    const = pl.BlockSpec((1, D), lambda i, j: (0, 0))
    in_specs = [tile, const, pl.BlockSpec((2 * tf, D), lambda i, j: (j, 0)),
                pl.BlockSpec((tf, D), lambda i, j: (j, 0)), pl.BlockSpec(memory_space=pl.ANY)]
    out_specs = [tile, pl.BlockSpec((tm, D), lambda i, j: (i, 0)), pl.BlockSpec((tm, 2 * tf), lambda i, j: (i, j))]
    out_shape = [pltpu.HBM((T,D), F32), pltpu.HBM((T,D), BF16),
                 pltpu.HBM((T,2 * F), BF16)]
    operands = [x, gain, wgu, wd, after]
    if head:
        in_specs += [tile, const]
        out_specs += [const, pl.BlockSpec((SUBLANES, LANES), lambda i, j: (0, 0))]
        out_shape += [jax.ShapeDtypeStruct((1, D), F32), jax.ShapeDtypeStruct((SUBLANES, LANES), F32)]
        operands += list(head)
    return pl.pallas_call(
        body, name=name, grid=(T // tm, nj),
        in_specs=in_specs, out_specs=tuple(out_specs), out_shape=tuple(out_shape),
        scratch_shapes=[pltpu.VMEM((tm, D), BF16), pltpu.VMEM((tm, D), F32)],
        compiler_params=pltpu.CompilerParams(dimension_semantics=("arbitrary", "arbitrary"),
                                             vmem_limit_bytes=VMEM_LIMIT_FFN_BYTES),
    )(*_hbm(*operands))


def _ffn_bwd(dy, x, gain, gu, wgu, wd, after, name, tm, tf):
    T, D = x.shape
    F = wd.shape[0]
    nj = F // tf

    def body(dy_ref, x_ref, g_ref, gu_ref, wgu_ref, wd_ref, after_ref,
             dx_ref, dgain_ref, dg_ref, du_ref, hh_ref, do_ref, do_sc, acc_sc, dgu_sc):
        i, j = pl.program_id(0), pl.program_id(1)

        @pl.when((i == 0) & (j == 0))
        def _():
            dgain_ref[...] = jnp.zeros_like(dgain_ref)

        @pl.when(j == 0)
        def _():
            do = (0.5 * dy_ref[...]).astype(BF16)
            do_sc[...] = do
            do_ref[...] = do
            acc_sc[...] = jnp.zeros_like(acc_sc)

        groups = _row_groups(tm)
        dhs = [_dot_nt(do_sc[rows, :], wd_ref[...]) for rows in groups]
        for rows, dh in zip(groups, dhs):
            g = gu_ref[rows, :tf].astype(F32)
            u = gu_ref[rows, tf:].astype(F32)
            sig = _sigmoid(g)
            s = g * sig
            dg = (dh * u * (sig + s * (1.0 - sig))).astype(BF16)
            du = (dh * s).astype(BF16)
            dg_ref[rows, :] = dg
            du_ref[rows, :] = du
            dgu_sc[rows, :tf] = dg
            dgu_sc[rows, tf:] = du
            hh_ref[rows, :] = (s * u).astype(BF16)
        for rows in groups:
            acc_sc[rows, :] += _dot(dgu_sc[rows, :], wgu_ref[...])

        @pl.when(j == nj - 1)
        def _():
            for rows in groups:
                dx, dgain = _rms_bwd(acc_sc[rows, :], x_ref[rows, :], g_ref[...])
                dgain_ref[...] += dgain
                dx_ref[rows, :] = dy_ref[rows, :] + dx

    tile_in = pl.BlockSpec((tm, D), lambda i, j: (i, 0))
    tile = pl.BlockSpec((tm, D), lambda i, j: (i, 0), pipeline_mode=pl.Buffered(1))
    return pl.pallas_call(
        body, name=name, grid=(T // tm, nj),
        in_specs=[tile_in, tile_in,
                  pl.BlockSpec((1, D), lambda i, j: (0, 0)),
                  pl.BlockSpec((tm, 2 * tf), lambda i, j: (i, j)),
                  pl.BlockSpec((2 * tf, D), lambda i, j: (j, 0)),
                  pl.BlockSpec((tf, D), lambda i, j: (j, 0)),
                  pl.BlockSpec(memory_space=pl.ANY)],
        out_specs=(tile,
                   pl.BlockSpec((1, D), lambda i, j: (0, 0)),
                   pl.BlockSpec((tm, tf), lambda i, j: (i, j)),
                   pl.BlockSpec((tm, tf), lambda i, j: (i, j)),
                   pl.BlockSpec((tm, tf), lambda i, j: (i, j)),
                   pl.BlockSpec((tm, D), lambda i, j: (i, 0))),
        out_shape=(pltpu.HBM((T,D), F32), jax.ShapeDtypeStruct((1, D), F32),
                   pltpu.HBM((T,F), BF16), pltpu.HBM((T,F), BF16),
                   pltpu.HBM((T,F), BF16), pltpu.HBM((T,D), BF16)),
        scratch_shapes=[pltpu.VMEM((tm, D), BF16), pltpu.VMEM((tm, D), F32), pltpu.VMEM((tm, 2 * tf), BF16)],
        compiler_params=pltpu.CompilerParams(dimension_semantics=("arbitrary", "arbitrary"),
                                             vmem_limit_bytes=VMEM_LIMIT_FFN_BYTES),
    )(*_hbm(dy, x, gain, gu, wgu, wd, after))


def _tn_grad(a, b, after, name, bm, tk):
    T, M = a.shape
    D = b.shape[1]
    nk = T // tk

    def body(a_ref, b_ref, after_ref, o_ref, acc_sc):
        k = pl.program_id(1)

        @pl.when(k == 0)
        def _():
            acc_sc[...] = jnp.zeros_like(acc_sc)

        acc_sc[...] += _dot_tn(a_ref[...], b_ref[...])

        @pl.when(k == nk - 1)
        def _():
            o_ref[...] = acc_sc[...].astype(BF16)

    return pl.pallas_call(
        body, name=name, grid=(M // bm, nk),
        in_specs=[pl.BlockSpec((tk, bm), lambda i, k: (k, i)), pl.BlockSpec((tk, D), lambda i, k: (k, 0)),
                  pl.BlockSpec(memory_space=pl.ANY)],
        out_specs=pl.BlockSpec((bm, D), lambda i, k: (i, 0)),
        out_shape=pltpu.HBM((M, D), BF16),
        scratch_shapes=[pltpu.VMEM((bm, D), F32)],
        compiler_params=_params("arbitrary", "arbitrary"),
    )(*_hbm(a, b, after))


_Z_SPLITS = (0, 512, 640, 768, 1280, 1792, 2304)


def _mixin_fwd(x, gain, w_in_t, name, tm):
    T, D = x.shape
    widths = [b - a for a, b in zip(_Z_SPLITS[:-1], _Z_SPLITS[1:])]

    def body(x_ref, g_ref, w_ref, hn_ref, *outs):
        _, xhat = _rms_stats(x_ref[...])
        hn = (xhat * g_ref[...]).astype(BF16)
        hn_ref[...] = hn
        for n in (0, 3, 4, 5):
            outs[n][...] = _dot_nt(hn, w_ref[_Z_SPLITS[n]:_Z_SPLITS[n + 1], :]).astype(BF16)
        kv = _dot_nt(hn, w_ref[_Z_SPLITS[1]:_Z_SPLITS[3], :]).astype(BF16)
        outs[1][...] = kv[:, :KV_WIDTH]
        outs[2][...] = kv[:, KV_WIDTH:]

    return pl.pallas_call(
        body, name=name, grid=(T // tm,),
        in_specs=[pl.BlockSpec((tm, D), lambda i: (i, 0)),
                  pl.BlockSpec((1, D), lambda i: (0, 0)),
                  pl.BlockSpec(w_in_t.shape, lambda i: (0, 0))],
        out_specs=tuple([pl.BlockSpec((tm, D), lambda i: (i, 0))]
                        + [pl.BlockSpec((tm, w), lambda i: (i, 0)) for w in widths]),
        out_shape=tuple([pltpu.HBM((T,D), BF16)]
                        + [pltpu.HBM((T,w), BF16) for w in widths]),
        compiler_params=_params("arbitrary"),
    )(*_hbm(x, gain, w_in_t))


def _bucket_table():
    qi = np.arange(BLOCK, dtype=np.int32)[:, None]
    kj = np.arange(2 * BLOCK, dtype=np.int32)[None, :]
    dist = qi + BLOCK - kj
    n = np.maximum(dist, 0)
    max_exact = NUM_BUCKETS // 2
    large = max_exact + (np.log(np.maximum(n, 1).astype(np.float32) / max_exact)
                         / math.log(MAX_DISTANCE / max_exact)
                         * (NUM_BUCKETS - max_exact)).astype(np.int32)
    large = np.minimum(large, NUM_BUCKETS - 1)
    bucket = np.where(n < max_exact, n, large).astype(np.int32)
    valid = (dist >= 0) & (dist < WINDOW)
    return np.where(valid, bucket, -1).astype(np.int32)


def _bias_build(table, bucket, name):
    def body(t_ref, b_ref, o_ref):
        bk = b_ref[...]
        for h in range(N_Q_HEADS):
            def step(b, acc):
                return jnp.where(bk == b, t_ref[b, h], acc)
            o_ref[h] = lax.fori_loop(0, NUM_BUCKETS, step, jnp.full(bk.shape, MASKED, F32))

    return pl.pallas_call(
        body, name=name,
        in_specs=[pl.BlockSpec(memory_space=pltpu.SMEM), pl.BlockSpec(memory_space=pltpu.VMEM)],
        out_specs=pl.BlockSpec(memory_space=pltpu.VMEM),
        out_shape=jax.ShapeDtypeStruct((N_Q_HEADS,) + bucket.shape, F32),
    )(table, bucket)


def _bias_grad(dbias, bucket, name):
    def body(d_ref, b_ref, o_ref):
        bk = b_ref[...]
        row = lax.broadcasted_iota(jnp.int32, o_ref.shape, 0)
        lane = lax.broadcasted_iota(jnp.int32, o_ref.shape, 1)
        res = jnp.zeros(o_ref.shape, F32)
        for h in range(N_Q_HEADS):
            d = d_ref[h]

            def step(b, acc):
                part = jnp.sum(jnp.where(bk == b, d, 0.0), axis=0, keepdims=True)
                return jnp.where(row == b, part, acc)
            per_lane = lax.fori_loop(0, NUM_BUCKETS, step, jnp.zeros(o_ref.shape, F32))
            res = jnp.where(lane == h, jnp.sum(per_lane, axis=1, keepdims=True), res)
        o_ref[...] = res

    return pl.pallas_call(
        body, name=name,
        in_specs=[pl.BlockSpec(memory_space=pltpu.VMEM), pl.BlockSpec(memory_space=pltpu.VMEM)],
        out_specs=pl.BlockSpec(memory_space=pltpu.VMEM),
        out_shape=jax.ShapeDtypeStruct((NUM_BUCKETS, LANES), F32),
    )(*_hbm(dbias, bucket))


def _head_cols(h):
    return slice(h * HEAD_DIM, (h + 1) * HEAD_DIM)


def _stack_heads(ref, r0, g, dtype):
    return jnp.concatenate(
        [ref[pl.ds(r0, BLOCK), _head_cols(GQA_GROUP * g + j)].astype(dtype) for j in range(GQA_GROUP)], axis=0)


def _unstack_heads(ref, r0, g, val):
    for j in range(GQA_GROUP):
        ref[pl.ds(r0, BLOCK), _head_cols(GQA_GROUP * g + j)] = val[j * BLOCK:(j + 1) * BLOCK, :]


def _head_lanes(h):
    return slice(h * BLOCK, (h + 1) * BLOCK)


def _group_lanes(g):
    return slice(g * GROUP_ROWS, (g + 1) * GROUP_ROWS)


def _head_softmax(st, bias_t, sink, no_prev):
    s = st * SCALE + bias_t
    row = lax.broadcasted_iota(jnp.int32, s.shape, 0)
    s = jnp.where(no_prev & (row < BLOCK), MASKED, s)
    m = jnp.maximum(jnp.max(s, axis=0, keepdims=True), sink)
    p = jnp.exp(s - m)
    ps = jnp.exp(sink - m)
    r = 1.0 / (jnp.sum(p, axis=0, keepdims=True) + ps)
    return p * r, ps * r


def _load_band(kf_sc, vf_sc, kp_ref, kc_ref, vp_ref, vc_ref, tq):
    kf_sc[0:BLOCK, :] = kp_ref[...]
    kf_sc[BLOCK:BLOCK + tq, :] = kc_ref[...]
    vf_sc[0:BLOCK, :] = vp_ref[...]
    vf_sc[BLOCK:BLOCK + tq, :] = vc_ref[...]


def _attn_fwd(q, k, v, bias_t, sinks, gain, name, nblk):
    T = q.shape[0]
    tq = nblk * BLOCK

    def body(sink_ref, q_ref, kc_ref, kp_ref, vc_ref, vp_ref, bias_ref, g_ref, raw_ref, nrm_ref,
             kf_sc, vf_sc, o_sc, st_sc, pt_sc):
        i = pl.program_id(0)
        _load_band(kf_sc, vf_sc, kp_ref, kc_ref, vp_ref, vc_ref, tq)

        def block(b, carry):
            r0 = pl.multiple_of(b * BLOCK, BLOCK)
            no_prev = (i == 0) & (b == 0)
            for g in range(N_KV_HEADS):
                kb = kf_sc[pl.ds(r0, 2 * BLOCK), _head_cols(g)]
                st_sc[:, _group_lanes(g)] = _dot_nt(kb, _stack_heads(q_ref, r0, g, BF16))
            for h in range(N_Q_HEADS):
                p, _ = _head_softmax(st_sc[:, _head_lanes(h)], bias_ref[h], sink_ref[h], no_prev)
                pt_sc[:, _head_lanes(h)] = p.astype(BF16)
            for g in range(N_KV_HEADS):
                vb = vf_sc[pl.ds(r0, 2 * BLOCK), _head_cols(g)]
                _unstack_heads(o_sc, r0, g, _dot_tn(pt_sc[:, _group_lanes(g)], vb))
            return carry

        lax.fori_loop(0, nblk, block, 0, unroll=True)
        o = o_sc[...]
        raw_ref[...] = o.astype(BF16)
        _, ohat = _rms_stats(o)
        nrm_ref[...] = (ohat * g_ref[...]).astype(BF16)

    cur = lambda i: (i, 0)
    prev = lambda i: (jnp.maximum(i * nblk - 1, 0), 0)
    lanes = N_Q_HEADS * BLOCK
    return pl.pallas_call(
        body, name=name, grid=(T // tq,),
        in_specs=[pl.BlockSpec(memory_space=pltpu.SMEM),
                  pl.BlockSpec((tq, ATTN_WIDTH), cur),
                  pl.BlockSpec((tq, KV_WIDTH), cur), pl.BlockSpec((BLOCK, KV_WIDTH), prev),
                  pl.BlockSpec((tq, KV_WIDTH), cur), pl.BlockSpec((BLOCK, KV_WIDTH), prev),
                  pl.BlockSpec(bias_t.shape, lambda i: (0, 0, 0)),
                  pl.BlockSpec((1, ATTN_WIDTH), lambda i: (0, 0))],
        out_specs=(pl.BlockSpec((tq, ATTN_WIDTH), cur), pl.BlockSpec((tq, ATTN_WIDTH), cur)),
        out_shape=(pltpu.HBM((T,ATTN_WIDTH), BF16), pltpu.HBM((T,ATTN_WIDTH), BF16)),
        scratch_shapes=[pltpu.VMEM((tq + BLOCK, KV_WIDTH), BF16), pltpu.VMEM((tq + BLOCK, KV_WIDTH), BF16),
                        pltpu.VMEM((tq, ATTN_WIDTH), F32),
                        pltpu.VMEM((2 * BLOCK, lanes), F32), pltpu.VMEM((2 * BLOCK, lanes), BF16)],
        compiler_params=_params("arbitrary"),
    )(sinks, *_hbm(q, k, k, v, v, bias_t, gain))


def _attn_bwd(dmixed, raw, q, k, v, bias_t, sinks, gain, name, nblk):
    T = q.shape[0]
    tq = nblk * BLOCK
    nt = T // tq
    lanes = N_Q_HEADS * BLOCK

    def body(sink_ref, dm_ref, raw_ref, q_ref, kc_ref, kp_ref, vc_ref, vp_ref, bias_ref, g_ref,
             dq_ref, dk_ref, dv_ref, dbias_ref, dsink_ref, dgain_ref,
             do_sc, dq_sc, kf_sc, vf_sc, dkf_sc, dvf_sc, st_sc, dpt_sc, pt_sc, dst_sc, drow_sc,
             qs_sc, dos_sc, dsink_sc):
        i = pl.program_id(0)
        tile = nt - 1 - i

        @pl.when(i == 0)
        def _():
            dkf_sc[...] = jnp.zeros_like(dkf_sc)
            dvf_sc[...] = jnp.zeros_like(dvf_sc)
            dsink_sc[...] = jnp.zeros_like(dsink_sc)
            dbias_ref[...] = jnp.zeros_like(dbias_ref)
            dgain_ref[...] = jnp.zeros_like(dgain_ref)

        carry_k = dkf_sc[0:BLOCK, :]
        carry_v = dvf_sc[0:BLOCK, :]
        dkf_sc[0:tq, :] = jnp.zeros((tq, KV_WIDTH), F32)
        dvf_sc[0:tq, :] = jnp.zeros((tq, KV_WIDTH), F32)
        dkf_sc[tq:tq + BLOCK, :] = carry_k
        dvf_sc[tq:tq + BLOCK, :] = carry_v
        _load_band(kf_sc, vf_sc, kp_ref, kc_ref, vp_ref, vc_ref, tq)

        do, dgain = _rms_bwd(dm_ref[...].astype(F32), raw_ref[...].astype(F32), g_ref[...])
        dgain_ref[...] += dgain
        do_sc[...] = do
        ones = jnp.ones((SUBLANES, HEAD_DIM), BF16)

        def block(b, carry):
            r0 = pl.multiple_of(b * BLOCK, BLOCK)
            no_prev = (tile == 0) & (b == 0)
            for g in range(N_KV_HEADS):
                kb = kf_sc[pl.ds(r0, 2 * BLOCK), _head_cols(g)]
                vb = vf_sc[pl.ds(r0, 2 * BLOCK), _head_cols(g)]
                qg = _stack_heads(q_ref, r0, g, BF16)
                dog = _stack_heads(do_sc, r0, g, F32)
                prod = dog * _stack_heads(raw_ref, r0, g, F32)
                hi = prod.astype(BF16)
                lo = (prod - hi.astype(F32)).astype(BF16)
                drow_sc[:, _group_lanes(g)] = _dot_nt(ones, hi) + _dot_nt(ones, lo)
                dogb = dog.astype(BF16)
                qs_sc[g] = qg
                dos_sc[g] = dogb
                st_sc[:, _group_lanes(g)] = _dot_nt(kb, qg)
                dpt_sc[:, _group_lanes(g)] = _dot_nt(vb, dogb)
            for h in range(N_Q_HEADS):
                hl = _head_lanes(h)
                p, ps = _head_softmax(st_sc[:, hl], bias_ref[h], sink_ref[h], no_prev)
                rowdot = drow_sc[0:1, hl]
                ds = p * (dpt_sc[:, hl] - rowdot)
                dsink_sc[h:h + 1, :] += -(ps * rowdot)
                dbias_ref[h] += ds
                dst_sc[:, hl] = ds.astype(BF16)
                pt_sc[:, hl] = p.astype(BF16)
            for g in range(N_KV_HEADS):
                kb = kf_sc[pl.ds(r0, 2 * BLOCK), _head_cols(g)]
                dsg = dst_sc[:, _group_lanes(g)]
                _unstack_heads(dq_sc, r0, g, _dot_tn(dsg, kb) * SCALE)
                dkf_sc[pl.ds(r0, 2 * BLOCK), _head_cols(g)] += _dot(dsg, qs_sc[g]) * SCALE
                dvf_sc[pl.ds(r0, 2 * BLOCK), _head_cols(g)] += _dot(pt_sc[:, _group_lanes(g)], dos_sc[g])
            return carry

        lax.fori_loop(0, nblk, block, 0, unroll=True)
        dq_ref[...] = dq_sc[...].astype(BF16)
        dk_ref[...] = dkf_sc[BLOCK:BLOCK + tq, :].astype(BF16)
        dv_ref[...] = dvf_sc[BLOCK:BLOCK + tq, :].astype(BF16)

        @pl.when(i == nt - 1)
        def _():
            tot = jnp.sum(dsink_sc[...], axis=1, keepdims=True)
            dsink_ref[...] = jnp.broadcast_to(tot, dsink_ref.shape)

    cur = lambda i: (nt - 1 - i, 0)
    prev = lambda i: (jnp.maximum((nt - 1 - i) * nblk - 1, 0), 0)
    const2 = lambda i: (0, 0)
    const3 = lambda i: (0, 0, 0)
    return pl.pallas_call(
        body, name=name, grid=(nt,),
        in_specs=[pl.BlockSpec(memory_space=pltpu.SMEM),
                  pl.BlockSpec((tq, ATTN_WIDTH), cur),
                  pl.BlockSpec((tq, ATTN_WIDTH), cur),
                  pl.BlockSpec((tq, ATTN_WIDTH), cur),
                  pl.BlockSpec((tq, KV_WIDTH), cur), pl.BlockSpec((BLOCK, KV_WIDTH), prev),
                  pl.BlockSpec((tq, KV_WIDTH), cur), pl.BlockSpec((BLOCK, KV_WIDTH), prev),
                  pl.BlockSpec(bias_t.shape, const3),
                  pl.BlockSpec((1, ATTN_WIDTH), const2)],
        out_specs=(pl.BlockSpec((tq, ATTN_WIDTH), cur),
                   pl.BlockSpec((tq, KV_WIDTH), cur), pl.BlockSpec((tq, KV_WIDTH), cur),
                   pl.BlockSpec(bias_t.shape, const3),
                   pl.BlockSpec((N_Q_HEADS, LANES), const2),
                   pl.BlockSpec((1, ATTN_WIDTH), const2)),
        out_shape=(pltpu.HBM((T,ATTN_WIDTH), BF16),
                   pltpu.HBM((T,KV_WIDTH), BF16), pltpu.HBM((T,KV_WIDTH), BF16),
                   jax.ShapeDtypeStruct(bias_t.shape, F32),
                   jax.ShapeDtypeStruct((N_Q_HEADS, LANES), F32),
                   jax.ShapeDtypeStruct((1, ATTN_WIDTH), F32)),
        scratch_shapes=[pltpu.VMEM((tq, ATTN_WIDTH), F32), pltpu.VMEM((tq, ATTN_WIDTH), F32),
                        pltpu.VMEM((tq + BLOCK, KV_WIDTH), BF16), pltpu.VMEM((tq + BLOCK, KV_WIDTH), BF16),
                        pltpu.VMEM((tq + BLOCK, KV_WIDTH), F32), pltpu.VMEM((tq + BLOCK, KV_WIDTH), F32),
                        pltpu.VMEM((2 * BLOCK, lanes), F32), pltpu.VMEM((2 * BLOCK, lanes), F32),
                        pltpu.VMEM((2 * BLOCK, lanes), BF16), pltpu.VMEM((2 * BLOCK, lanes), BF16),
                        pltpu.VMEM((SUBLANES, lanes), F32),
                        pltpu.VMEM((N_KV_HEADS, GROUP_ROWS, HEAD_DIM), BF16),
                        pltpu.VMEM((N_KV_HEADS, GROUP_ROWS, HEAD_DIM), BF16),
                        pltpu.VMEM((N_Q_HEADS, LANES), F32)],
        compiler_params=_params("arbitrary"),
    )(sinks, *_hbm(dmixed, raw, q, k, k, v, v, bias_t, gain))


def _shift_down(cu, tail):
    row = lax.broadcasted_iota(jnp.int32, cu.shape, 0)
    t6, t7 = tail[6:7, :], tail[7:8, :]
    s1 = jnp.where(row == 0, t7, pltpu.roll(cu, 1, 0))
    s2 = jnp.where(row == 0, t6, jnp.where(row == 1, t7, pltpu.roll(cu, 2, 0)))
    return s1, s2


def _shift_up(d, head):
    n = d.shape[0]
    row = lax.broadcasted_iota(jnp.int32, d.shape, 0)
    h0, h1 = head[0:1, :], head[1:2, :]
    s1 = jnp.where(row == n - 1, h0, pltpu.roll(d, n - 1, 0))
    s2 = jnp.where(row == n - 1, h1, jnp.where(row == n - 2, h0, pltpu.roll(d, n - 2, 0)))
    return s1, s2


def _mixout_fwd(x, attn_n, u, gb, gc, conv_w, gain, w_out, name, tm):
    T, D = x.shape

    def body(x_ref, an_ref, u_ref, b_ref, c_ref, cw_ref, g_ref, wo_ref, xo_ref, cn_ref, tail_sc):
        @pl.when(pl.program_id(0) == 0)
        def _():
            tail_sc[...] = jnp.zeros_like(tail_sc)

        cu = c_ref[...].astype(F32) * u_ref[...].astype(F32)
        s1, s2 = _shift_down(cu, tail_sc[...])
        tail_sc[...] = cu[tm - SUBLANES:tm, :]
        pre = cw_ref[0:1, :] * s2 + cw_ref[1:2, :] * s1 + cw_ref[2:3, :] * cu
        conv = b_ref[...].astype(F32) * pre
        _, chat = _rms_stats(conv)
        cn = (chat * g_ref[...]).astype(BF16)
        cn_ref[...] = cn
        xo_ref[...] = (x_ref[...] + _dot(an_ref[...], wo_ref[0:ATTN_WIDTH, :])
                       + _dot(cn, wo_ref[ATTN_WIDTH:ATTN_WIDTH + CONV_DIM, :]))

    row = lambda i: (i, 0)
    const = lambda i: (0, 0)
    return pl.pallas_call(
        body, name=name, grid=(T // tm,),
        in_specs=[pl.BlockSpec((tm, D), row), pl.BlockSpec((tm, ATTN_WIDTH), row),
                  pl.BlockSpec((tm, CONV_DIM), row), pl.BlockSpec((tm, CONV_DIM), row),
                  pl.BlockSpec((tm, CONV_DIM), row),
                  pl.BlockSpec(conv_w.shape, const), pl.BlockSpec((1, CONV_DIM), const),
                  pl.BlockSpec(w_out.shape, const)],
        out_specs=(pl.BlockSpec((tm, D), row), pl.BlockSpec((tm, CONV_DIM), row)),
        out_shape=(pltpu.HBM((T,D), F32), pltpu.HBM((T,CONV_DIM), BF16)),
        scratch_shapes=[pltpu.VMEM((SUBLANES, CONV_DIM), F32)],
        compiler_params=_params("arbitrary"),
    )(*_hbm(x, attn_n, u, gb, gc, conv_w, gain, w_out))


def _mixout_bwd(dy, attn_n, conv_n, w_out, after, name, tm):
    T, D = dy.shape
    W = ATTN_WIDTH + CONV_DIM
    nt = T // tm

    def body(dy_ref, an_ref, cn_ref, w_ref, after_ref, dm_ref, dw_ref, dw_sc):
        i = pl.program_id(0)

        @pl.when(i == 0)
        def _():
            dw_sc[...] = jnp.zeros_like(dw_sc)

        dyb = dy_ref[...].astype(BF16)
        dm_ref[...] = _dot_nt(dyb, w_ref[...]).astype(BF16)
        dw_sc[0:ATTN_WIDTH, :] += _dot_tn(an_ref[...], dyb)
        dw_sc[ATTN_WIDTH:W, :] += _dot_tn(cn_ref[...], dyb)

        @pl.when(i == nt - 1)
        def _():
            dw_ref[...] = dw_sc[...].astype(BF16)

    row = lambda i: (i, 0)
    const = lambda i: (0, 0)
    return pl.pallas_call(
        body, name=name, grid=(nt,),
        in_specs=[pl.BlockSpec((tm, D), row), pl.BlockSpec((tm, ATTN_WIDTH), row),
                  pl.BlockSpec((tm, CONV_DIM), row), pl.BlockSpec(w_out.shape, const),
                  pl.BlockSpec(memory_space=pl.ANY)],
        out_specs=(pl.BlockSpec((tm, W), row), pl.BlockSpec((W, D), const)),
        out_shape=(pltpu.HBM((T,W), BF16), pltpu.HBM((W, D), BF16)),
        scratch_shapes=[pltpu.VMEM((W, D), F32)],
        compiler_params=_params("arbitrary"),
    )(*_hbm(dy, attn_n, conv_n, w_out, after))


def _conv_bwd(dmixed, u, gb, gc, conv_w, gain, name, tc):
    T = u.shape[0]
    nt = T // tc
    per8 = tc // SUBLANES

    def body(dm_ref, u_ref, b_ref, c_ref, ut_ref, ct_ref, cw_ref, g_ref,
             du_ref, db_ref, dc_ref, dcw_ref, dgain_ref, head_sc):
        i = pl.program_id(0)

        @pl.when(i == 0)
        def _():
            head_sc[...] = jnp.zeros_like(head_sc)
            dcw_ref[...] = jnp.zeros_like(dcw_ref)
            dgain_ref[...] = jnp.zeros_like(dgain_ref)

        uu = u_ref[...].astype(F32)
        cc = c_ref[...].astype(F32)
        bb = b_ref[...].astype(F32)
        cu = cc * uu
        tail = jnp.where(i == nt - 1, 0.0, ct_ref[...].astype(F32) * ut_ref[...].astype(F32))
        s1, s2 = _shift_down(cu, tail)
        w0, w1, w2 = cw_ref[0:1, :], cw_ref[1:2, :], cw_ref[2:3, :]
        pre = w0 * s2 + w1 * s1 + w2 * cu
        dconv, dgain = _rms_bwd(dm_ref[...].astype(F32), bb * pre, g_ref[...])
        dgain_ref[...] += dgain
        db_ref[...] = (dconv * pre).astype(BF16)
        dpre = dconv * bb
        dcw_ref[0:1, :] += jnp.sum(dpre * s2, axis=0, keepdims=True)
        dcw_ref[1:2, :] += jnp.sum(dpre * s1, axis=0, keepdims=True)
        dcw_ref[2:3, :] += jnp.sum(dpre * cu, axis=0, keepdims=True)
        n1, n2 = _shift_up(dpre, head_sc[...])
        head_sc[...] = dpre[0:SUBLANES, :]
        dcu = w2 * dpre + w1 * n1 + w0 * n2
        du_ref[...] = (dcu * cc).astype(BF16)
        dc_ref[...] = (dcu * uu).astype(BF16)

    rev = lambda i: (nt - 1 - i, 0)
    rev_right = lambda i: (nt - 1 - i, 1)
    tail_map = lambda i: (jnp.maximum((nt - 1 - i) * per8 - 1, 0), 0)
    const = lambda i: (0, 0)
    return pl.pallas_call(
        body, name=name, grid=(nt,),
        in_specs=[pl.BlockSpec((tc, CONV_DIM), rev_right),
                  pl.BlockSpec((tc, CONV_DIM), rev), pl.BlockSpec((tc, CONV_DIM), rev),
                  pl.BlockSpec((tc, CONV_DIM), rev),
                  pl.BlockSpec((SUBLANES, CONV_DIM), tail_map), pl.BlockSpec((SUBLANES, CONV_DIM), tail_map),
                  pl.BlockSpec(conv_w.shape, const), pl.BlockSpec((1, CONV_DIM), const)],
        out_specs=(pl.BlockSpec((tc, CONV_DIM), rev), pl.BlockSpec((tc, CONV_DIM), rev),
                   pl.BlockSpec((tc, CONV_DIM), rev),
                   pl.BlockSpec((SUBLANES, CONV_DIM), const), pl.BlockSpec((1, CONV_DIM), const)),
        out_shape=(pltpu.HBM((T,CONV_DIM), BF16), pltpu.HBM((T,CONV_DIM), BF16),
                   pltpu.HBM((T,CONV_DIM), BF16),
                   jax.ShapeDtypeStruct((SUBLANES, CONV_DIM), F32), jax.ShapeDtypeStruct((1, CONV_DIM), F32)),
        scratch_shapes=[pltpu.VMEM((SUBLANES, CONV_DIM), F32)],
        compiler_params=_params("arbitrary"),
    )(*_hbm(dmixed, u, gb, gc, u, gc, conv_w, gain))


def _mixin_bwd(dy, x, gain, dz, w_in_t, name, tm):
    T, D = x.shape
    nz = len(dz)

    def body(dy_ref, x_ref, g_ref, *rest):
        dz_refs, wt_ref, dx_ref, dgain_ref, dz_sc = rest[:nz], rest[nz], rest[nz + 1], rest[nz + 2], rest[nz + 3]

        @pl.when(pl.program_id(0) == 0)
        def _():
            dgain_ref[...] = jnp.zeros_like(dgain_ref)

        for r, lo, hi in zip(dz_refs, _Z_SPLITS[:-1], _Z_SPLITS[1:]):
            dz_sc[:, lo:hi] = r[...]
        dx, dgain = _rms_bwd(_dot(dz_sc[...], wt_ref[...]), x_ref[...], g_ref[...])
        dgain_ref[...] += dgain
        dx_ref[...] = dy_ref[...] + dx

    row = lambda i: (i, 0)
    const = lambda i: (0, 0)
    return pl.pallas_call(
        body, name=name, grid=(T // tm,),
        in_specs=[pl.BlockSpec((tm, D), row), pl.BlockSpec((tm, D), row), pl.BlockSpec((1, D), const)]
                 + [pl.BlockSpec((tm, a.shape[1]), row) for a in dz]
                 + [pl.BlockSpec(w_in_t.shape, const)],
        out_specs=(pl.BlockSpec((tm, D), row), pl.BlockSpec((1, D), const)),
        out_shape=(pltpu.HBM((T,D), F32), jax.ShapeDtypeStruct((1, D), F32)),
        scratch_shapes=[pltpu.VMEM((tm, _Z_SPLITS[-1]), BF16)],
        compiler_params=_params("arbitrary"),
    )(*_hbm(dy, x, gain, *dz, w_in_t))


def _win_grad(dz, hn, name, tk):
    T, D = hn.shape
    nz = len(dz)
    nt = T // tk
    W = _Z_SPLITS[-1]

    def body(hn_ref, *rest):
        dz_refs, dw_ref, dw_sc, kv_sc = rest[:nz], rest[nz], rest[nz + 1], rest[nz + 2]
        i = pl.program_id(0)

        @pl.when(i == 0)
        def _():
            dw_sc[...] = jnp.zeros_like(dw_sc)

        hn = hn_ref[...]
        kv_sc[:, :KV_WIDTH] = dz_refs[1][...]
        kv_sc[:, KV_WIDTH:] = dz_refs[2][...]
        dw_sc[_Z_SPLITS[1]:_Z_SPLITS[3], :] += _dot_tn(kv_sc[...], hn)
        for n in (0, 3, 4, 5):
            dw_sc[_Z_SPLITS[n]:_Z_SPLITS[n + 1], :] += _dot_tn(dz_refs[n][...], hn)

        @pl.when(i == nt - 1)
        def _():
            dw_ref[...] = dw_sc[...].astype(BF16)

    row = lambda i: (i, 0)
    return pl.pallas_call(
        body, name=name, grid=(nt,),
        in_specs=[pl.BlockSpec((tk, D), row)] + [pl.BlockSpec((tk, a.shape[1]), row) for a in dz],
        out_specs=pl.BlockSpec((W, D), lambda i: (0, 0)),
        out_shape=pltpu.HBM((W, D), BF16),
        scratch_shapes=[pltpu.VMEM((W, D), F32), pltpu.VMEM((tk, 2 * KV_WIDTH), BF16)],
        compiler_params=_params("arbitrary"),
    )(*_hbm(hn, *dz))


def _adamw(parts, w, m, v, name, tr):
    P = parts.shape[0]
    R, C = w.shape

    def body(p_ref, w_ref, m_ref, v_ref, g_ref, d_ref, nm_ref, nv_ref):
        g = p_ref[0].astype(F32)
        for d in range(1, P):
            g = g + p_ref[d].astype(F32)
        nm = ADAM_B1 * m_ref[...] + (1.0 - ADAM_B1) * g
        nv = ADAM_B2 * v_ref[...] + (1.0 - ADAM_B2) * (g * g)
        m_hat = nm / (1.0 - ADAM_B1 ** ADAM_STEP)
        v_hat = nv / (1.0 - ADAM_B2 ** ADAM_STEP)
        g_ref[...] = g
        nm_ref[...] = nm
        nv_ref[...] = nv
        d_ref[...] = -ADAM_LR * (m_hat / (jnp.sqrt(v_hat) + ADAM_EPS) + ADAM_WD * w_ref[...])

    row = lambda i: (i, 0)
    spec = pl.BlockSpec((tr, C), row)
    shp = pltpu.HBM((R, C), F32)
    return pl.pallas_call(
        body, name=name, grid=(R // tr,),
        in_specs=[pl.BlockSpec((P, tr, C), lambda i: (0, i, 0)), spec, spec, spec],
        out_specs=(spec, spec, spec, spec),
        out_shape=(shp, shp, shp, shp),
        compiler_params=_params("arbitrary"),
    )(*_hbm(parts, w, m, v))


def _columns_of_blocks(g):
    n, R, w = g.shape
    return g.transpose(1, 0, 2).reshape(R, n * w)


def _pad_row(vec):
    vec = vec.reshape(1, -1)
    return jnp.pad(vec, ((0, 0), (0, PACK_COLS - vec.shape[1])))


def kernel(x, rel_bias_table, ffn1_norm, ffn1_w_gate, ffn1_w_up, ffn1_w_down, mix_norm, w_in, conv_w, attn_sinks, attn_out_norm, conv_out_norm, w_out, ffn2_norm, ffn2_w_gate, ffn2_w_up, ffn2_w_down, final_norm, loss_target, m_rel_bias_table, m_ffn1_norm, m_ffn1_w_gate, m_ffn1_w_up, m_ffn1_w_down, m_mix_norm, m_w_in, m_conv_w, m_attn_sinks, m_attn_out_norm, m_conv_out_norm, m_w_out, m_ffn2_norm, m_ffn2_w_gate, m_ffn2_w_up, m_ffn2_w_down, m_final_norm, v_rel_bias_table, v_ffn1_norm, v_ffn1_w_gate, v_ffn1_w_up, v_ffn1_w_down, v_mix_norm, v_w_in, v_conv_w, v_attn_sinks, v_attn_out_norm, v_conv_out_norm, v_w_out, v_ffn2_norm, v_ffn2_w_gate, v_ffn2_w_up, v_ffn2_w_down, v_final_norm):
    T, D = x.shape[1], x.shape[2]
    x0 = x[0]
    target = loss_target[0]
    tm = min(TM_FFN, T)
    tm_bwd = min(TM_FFN_BWD, T)
    tm_mix = min(TM_MIX, T)
    tk = min(TK_WGRAD, T)
    tf = TF_FFN
    nblk = min(ATTN_BLOCKS, T // BLOCK)
    me = 4 * lax.axis_index("x") + 2 * lax.axis_index("y") + lax.axis_index("c")

    big = {
        "ffn1_w_gate": (ffn1_w_gate[0], m_ffn1_w_gate[0], v_ffn1_w_gate[0], True),
        "ffn1_w_up": (ffn1_w_up[0], m_ffn1_w_up[0], v_ffn1_w_up[0], True),
        "ffn1_w_down": (ffn1_w_down[0], m_ffn1_w_down[0], v_ffn1_w_down[0], False),
        "w_in": (w_in[0], m_w_in[0], v_w_in[0], True),
        "w_out": (w_out[0], m_w_out[0], v_w_out[0], False),
        "ffn2_w_gate": (ffn2_w_gate[0], m_ffn2_w_gate[0], v_ffn2_w_gate[0], True),
        "ffn2_w_up": (ffn2_w_up[0], m_ffn2_w_up[0], v_ffn2_w_up[0], True),
        "ffn2_w_down": (ffn2_w_down[0], m_ffn2_w_down[0], v_ffn2_w_down[0], False),
    }

    def block_to_send(name):
        w, _, _, transposed = big[name]
        return (w.T if transposed else w).astype(BF16)

    names1 = ["ffn1_w_gate", "ffn1_w_up", "ffn1_w_down"]
    names_rest = ["w_in", "w_out", "ffn2_w_gate", "ffn2_w_up", "ffn2_w_down"]
    first = _gather_two_level([block_to_send(n) for n in names1], "gather_ffn1", ffn1_norm)
    wgt1, wut1, wd1 = [g.reshape(-1, D) for g in first]
    h_rest, token = _exchange_start([block_to_send(n) for n in names_rest[:2]] + [conv_w[0]]
                                    + [block_to_send(n) for n in names_rest[2:]], False,
                                    "gather_start_rest", wd1)

    wgu1 = _stack_gate_up(wgt1, wut1, tf)
    x1, xn1, gu1 = _ffn_fwd(x0, ffn1_norm, wgu1, wd1, token, "ffn1_fwd", tm, tf)
    mixw = _exchange_wait(h_rest[:3], False, "gather_wait_mix", x1)
    win_t = mixw[0].reshape(-1, D)
    wout = mixw[1].reshape(-1, D)
    cw = _columns_of_blocks(mixw[2])
    hn, q, k, v, u, gb, gc = _mixin_fwd(x1, mix_norm, win_t, "mixin_fwd", tm_mix)
    bucket = jnp.asarray(_bucket_table().T.copy())
    sinks = attn_sinks.reshape(-1)
    bias_t = _bias_build(rel_bias_table, bucket, "bias_build")
    attn_raw, attn_n = _attn_fwd(q, k, v, bias_t, sinks, attn_out_norm, "attn_fwd", nblk)
    x2, conv_n = _mixout_fwd(x1, attn_n, u, gb, gc, cw, conv_out_norm, wout, "mixout_fwd", tm_mix)
    wgt2, wut2, wd2 = [g.reshape(-1, D) for g in _exchange_wait(h_rest[3:], False, "gather_wait_ffn2", x2)]
    wgu2 = _stack_gate_up(wgt2, wut2, tf)
    dx3, xn2, gu2, d_final, loss_part = _ffn_fwd(x2, ffn2_norm, wgu2, wd2, x2, "ffn2_fwd", tm, tf,
                                                 head=(target, final_norm.reshape(1, D)))

    def blocks(g):
        return g.reshape(N_DEV, -1, D)

    dx2, d_ffn2_norm, dg2, du2, hh2, do2 = _ffn_bwd(
        dx3, x2, ffn2_norm, gu2, wgu2, wd2, dx3, "ffn2_bwd", tm_bwd, tf)
    d_wg2 = _tn_grad(dg2, xn2, dx2, "ffn2_wgrad_gate", BM_WGRAD, tk)
    d_wu2 = _tn_grad(du2, xn2, dx2, "ffn2_wgrad_up", BM_WGRAD, tk)
    d_wd2 = _tn_grad(hh2, do2, dx2, "ffn2_wgrad_down", BM_WGRAD, tk)
    handles2, token2 = _exchange_start([blocks(d_wg2), blocks(d_wu2), blocks(d_wd2)], True,
                                       "grads_start_ffn2", dx2)

    dmixed, d_wout = _mixout_bwd(dx2, attn_n, conv_n, wout, token2, "mixout_bwd", tm_mix)
    dq, dk, dv, dbias, dsink, d_attn_norm = _attn_bwd(
        dmixed, attn_raw, q, k, v, bias_t, sinks, attn_out_norm, "attn_bwd", nblk)
    du, dgb, dgc, d_cw, d_conv_norm = _conv_bwd(dmixed, u, gb, gc, cw, conv_out_norm, "conv_bwd", tm_mix)
    d_table = _bias_grad(dbias, bucket, "bias_grad")
    dz = [dq, dk, dv, du, dgb, dgc]
    dx1, d_mix_norm = _mixin_bwd(dx2, x1, mix_norm, dz, win_t, "mixin_bwd", tm_mix)
    d_win_t = _win_grad(dz, hn, "win_grad", min(TM_MIX, T))
    handles_mix, token_mix = _exchange_start([blocks(d_win_t), blocks(d_wout)], True, "grads_start_mix", d_table)

    dx0, d_ffn1_norm, dg1, du1, hh1, do1 = _ffn_bwd(
        dx1, x0, ffn1_norm, gu1, wgu1, wd1, token_mix, "ffn1_bwd", tm_bwd, tf)
    def pack(ffn1, mixn, ffn2, fin, attn_n_, conv_n_, sink_, extra, convw, table):
        rows = [_pad_row(ffn1), _pad_row(mixn), _pad_row(ffn2), _pad_row(fin),
                _pad_row(jnp.concatenate([attn_n_.reshape(-1), conv_n_.reshape(-1)])),
                _pad_row(sink_), _pad_row(extra),
                jnp.zeros((1, PACK_COLS), F32),
                jnp.pad(convw, ((0, 0), (0, PACK_COLS - convw.shape[1]))),
                _pad_row(table),
                jnp.zeros((PACK_ROWS - 12, PACK_COLS), F32)]
        return jnp.concatenate(rows, axis=0)

    def own_channels(a):
        full = jnp.zeros((a.shape[1], CONV_DIM), F32)
        return lax.dynamic_update_slice(full, a[0], (0, me * a.shape[2]))

    g_pack = pack(d_ffn1_norm, d_mix_norm, d_ffn2_norm, d_final, d_attn_norm, d_conv_norm,
                  dsink[:, 0], loss_part[0, :1], d_cw[:3], d_table[:, :N_Q_HEADS])
    zero1 = jnp.zeros((1,), F32)
    w_pack = pack(ffn1_norm, mix_norm, ffn2_norm, final_norm, attn_out_norm, conv_out_norm,
                  attn_sinks, zero1, own_channels(conv_w), rel_bias_table)
    m_pack = pack(m_ffn1_norm, m_mix_norm, m_ffn2_norm, m_final_norm, m_attn_out_norm, m_conv_out_norm,
                  m_attn_sinks, zero1, own_channels(m_conv_w), m_rel_bias_table)
    v_pack = pack(v_ffn1_norm, v_mix_norm, v_ffn2_norm, v_final_norm, v_attn_out_norm, v_conv_out_norm,
                  v_attn_sinks, zero1, own_channels(v_conv_w), v_rel_bias_table)
    h_small, token_small = _exchange_start([g_pack], False, "gather_small_start", dx0)

    d_wd1 = _tn_grad(hh1, do1, token_small, "ffn1_wgrad_down", BM_WGRAD, tk)
    handles1_d, token1 = _exchange_start([blocks(d_wd1)], True, "grads_start_ffn1_down", dx0)
    d_wg1 = _tn_grad(dg1, xn1, token1, "ffn1_wgrad_gate", BM_WGRAD, tk)
    handles1_g, token1 = _exchange_start([blocks(d_wg1)], True, "grads_start_ffn1_gate", token1)
    d_wu1 = _tn_grad(du1, xn1, token1, "ffn1_wgrad_up", BM_WGRAD, tk)
    handles1_u, token1 = _exchange_start([blocks(d_wu1)], True, "grads_start_ffn1_up", token1)

    res = {}

    def update(names, parts):
        last = None
        for name, p in zip(names, parts):
            w, m_, v_, transposed = big[name]
            if transposed:
                w, m_, v_ = w.T, m_.T, v_.T
            new = _adamw(p, w, m_, v_, "adamw_" + name, _row_tile(w.shape[0], ADAM_ROWS))
            res[name] = tuple((a.T if transposed else a)[None] for a in new)
            last = new[0]
        return last

    parts2 = _exchange_wait(handles2, True, "grads_wait_ffn2", token1)
    done2 = update(["ffn2_w_gate", "ffn2_w_up", "ffn2_w_down"], parts2)
    parts_mix = _exchange_wait(handles_mix, True, "grads_wait_mix", done2)
    done_mix = update(["w_in", "w_out"], parts_mix)
    (g_all,) = _exchange_wait(h_small, False, "gather_small_wait", done_mix)
    packs = _adamw(g_all, w_pack, m_pack, v_pack, "adamw_small", PACK_ROWS)

    parts1_d = _exchange_wait(handles1_d, True, "grads_wait_ffn1_down", packs[0])
    done1 = update(["ffn1_w_down"], parts1_d)
    parts1_g = _exchange_wait(handles1_g, True, "grads_wait_ffn1_gate", done1)
    done1 = update(["ffn1_w_gate"], parts1_g)
    parts1_u = _exchange_wait(handles1_u, True, "grads_wait_ffn1_up", done1)
    update(["ffn1_w_up"], parts1_u)

    def unpack(pk):
        cwb = lax.dynamic_slice(pk[8:11, :CONV_DIM], (0, me * conv_w.shape[2]), (3, conv_w.shape[2]))
        return {
            "ffn1_norm": pk[0:1, :D], "mix_norm": pk[1:2, :D], "ffn2_norm": pk[2:3, :D],
            "final_norm": pk[3, :D],
            "attn_out_norm": pk[4:5, :ATTN_WIDTH], "conv_out_norm": pk[4:5, ATTN_WIDTH:ATTN_WIDTH + CONV_DIM],
            "attn_sinks": pk[5:6, :N_Q_HEADS],
            "conv_w": cwb[None],
            "rel_bias_table": pk[11, :NUM_BUCKETS * N_Q_HEADS].reshape(NUM_BUCKETS, N_Q_HEADS),
        }

    small = [unpack(pk) for pk in packs]
    loss = packs[0][6, 0]

    order = ["rel_bias_table", "ffn1_norm", "ffn1_w_gate", "ffn1_w_up", "ffn1_w_down", "mix_norm", "w_in",
             "conv_w", "attn_sinks", "attn_out_norm", "conv_out_norm", "w_out", "ffn2_norm",
             "ffn2_w_gate", "ffn2_w_up", "ffn2_w_down", "final_norm"]
    outs = [loss, dx0[None]]
    for kind in range(4):
        for name in order:
            outs.append(res[name][kind] if name in res else small[kind][name])
    return tuple(outs)
```

```python
import math

import numpy as np
import jax
import jax.numpy as jnp
from jax import lax
from jax.experimental import pallas as pl
from jax.experimental.pallas import tpu as pltpu

F32 = jnp.float32
BF16 = jnp.bfloat16

N_DEV = 8
EPS = 1e-6
HEAD_DIM = 64
N_Q_HEADS = 8
N_KV_HEADS = 2
GQA_GROUP = 4
ATTN_WIDTH = 512
KV_WIDTH = 128
CONV_DIM = 512
BLOCK = 128
WINDOW = 128
NUM_BUCKETS = 32
MAX_DISTANCE = 128
SCALE = HEAD_DIM ** -0.5
MASKED = -1e30
GROUP_ROWS = GQA_GROUP * BLOCK

ADAM_LR = 0.001
ADAM_B1 = 0.9
ADAM_B2 = 0.999
ADAM_EPS = 1e-08
ADAM_WD = 0.01
ADAM_STEP = 10

VMEM_LIMIT_BYTES = 40 * 1024 * 1024
VMEM_LIMIT_FFN_BYTES = 50 * 1024 * 1024
SUBLANES = 8
LANES = 128
PACK_ROWS = 16
PACK_COLS = 1024

TM_FFN = 1024
TM_FFN_BWD = 1024
TM_MIX = 512
TM_MIX_WIDE = 1024
TK_WGRAD = 2048
TF_FFN = 256
BM_WGRAD = 1408
ROW_GROUPS = 4
ATTN_BLOCKS = 8
ADAM_ROWS = 256


def _row_tile(rows, limit):
    best = rows
    for t in range(16, min(rows, limit) + 1, 16):
        if rows % t == 0:
            best = t
    return best


def _params(*sem):
    return pltpu.CompilerParams(dimension_semantics=sem, vmem_limit_bytes=VMEM_LIMIT_BYTES)


def _hbm(*arrays):
    return [pltpu.with_memory_space_constraint(a, pltpu.HBM) for a in arrays]


def _dot(a, b):
    return jnp.dot(a, b, preferred_element_type=F32)


def _dot_nt(a, b):
    return lax.dot_general(a, b, (((1,), (1,)), ((), ())), preferred_element_type=F32)


def _dot_tn(a, b):
    return lax.dot_general(a, b, (((0,), (0,)), ((), ())), preferred_element_type=F32)


def _sigmoid(g):
    return 0.5 * jnp.tanh(0.5 * g) + 0.5


def _rms_stats(x):
    inv = lax.rsqrt(jnp.mean(x * x, axis=-1, keepdims=True) + EPS)
    return inv, x * inv


def _rms_bwd(dy, x, gain):
    inv, xhat = _rms_stats(x)
    dgain = jnp.sum(dy * xhat, axis=0, keepdims=True)
    dxh = dy * gain
    dx = inv * (dxh - xhat * jnp.mean(dxh * xhat, axis=-1, keepdims=True))
    return dx, dgain


def _peer_list():
    x, y, c = lax.axis_index("x"), lax.axis_index("y"), lax.axis_index("c")
    peers = []
    for k in range(1, N_DEV):
        px = 1 - x if (k >> 2) & 1 else x
        py = 1 - y if (k >> 1) & 1 else y
        pc = 1 - c if k & 1 else c
        peers.append((px, py, pc))
    return 4 * x + 2 * y + c, peers


def _gather_two_level(arrs, name, after):
    n = len(arrs)
    out_shape = tuple(jax.ShapeDtypeStruct((N_DEV,) + a.shape, a.dtype) for a in arrs)

    def body(*refs):
        ins, outs = refs[:n], refs[n + 1:2 * n + 1]
        send_sems, recv_sems, local_sems = refs[2 * n + 1:]
        x, y, c = lax.axis_index("x"), lax.axis_index("y"), lax.axis_index("c")
        me, sibling = (x, y, c), (x, y, 1 - c)
        chips = [(1 - x, y), (x, 1 - y), (1 - x, 1 - y)]

        def copy(a, k, block, to, src=None):
            slot = outs[a].at[4 * block[0] + 2 * block[1] + block[2]]
            return pltpu.make_async_remote_copy(
                src_ref=slot if src is None else src, dst_ref=slot,
                send_sem=send_sems.at[a, k], recv_sem=recv_sems.at[a, k],
                device_id=to, device_id_type=pl.DeviceIdType.MESH)

        started = []
        for a in range(n):
            loc = pltpu.make_async_copy(ins[a], outs[a].at[4 * x + 2 * y + c], local_sems.at[a])
            loc.start()
            started.append(loc)
        sends = []
        for a in range(n):
            sends.append(copy(a, 0, me, sibling, src=ins[a]))
            sends += [copy(a, 1 + j, me, (*chip, c), src=ins[a]) for j, chip in enumerate(chips)]
        for cp in sends:
            cp.start()
        for j, chip in enumerate(chips):
            for a in range(n):
                copy(a, 1 + j, (*chip, c), me).wait_recv()
                fwd = copy(a, 4 + j, (*chip, c), sibling)
                fwd.start()
                sends.append(fwd)
        for a in range(n):
            copy(a, 0, sibling, me).wait_recv()
            for j, chip in enumerate(chips):
                copy(a, 4 + j, (*chip, 1 - c), me).wait_recv()
        for cp in sends:
            cp.wait_send()
        for loc in started:
            loc.wait()

    hbm = pl.BlockSpec(memory_space=pl.ANY)
    return pl.pallas_call(
        body, name=name, out_shape=out_shape,
        in_specs=[hbm] * (n + 1), out_specs=tuple([hbm] * n),
        scratch_shapes=[pltpu.SemaphoreType.DMA((n, N_DEV - 1)),
                        pltpu.SemaphoreType.DMA((n, N_DEV - 1)),
                        pltpu.SemaphoreType.DMA((n,))],
    )(*arrs, after)


_HBM = pl.BlockSpec(memory_space=pltpu.HBM)
_SEM = pl.BlockSpec(memory_space=pltpu.SEMAPHORE)
_EFFECT = pltpu.SideEffectType.DATAFLOW_SIDE_EFFECTING


def _split_copies(srcs, lands, send_sems, recv_sems, scatter):
    me, peers = _peer_list()
    copies = []
    for a in range(len(srcs)):
        for k, (px, py, pc) in enumerate(peers):
            src = srcs[a].at[4 * px + 2 * py + pc] if scatter else srcs[a]
            copies.append(pltpu.make_async_remote_copy(
                src_ref=src, dst_ref=lands[a].at[me],
                send_sem=send_sems[a].at[k], recv_sem=recv_sems[a].at[k],
                device_id=(px, py, pc), device_id_type=pl.DeviceIdType.MESH))
    return copies


def _own_copies(srcs, lands, recv_sems, scatter):
    me, _ = _peer_list()
    return [pltpu.make_async_copy(srcs[a].at[me] if scatter else srcs[a], lands[a].at[me],
                                  recv_sems[a].at[N_DEV - 1]) for a in range(len(srcs))]


def _exchange_start(arrs, scatter, name, after):
    n = len(arrs)
    lands = [lax.empty(a.shape if scatter else (N_DEV,) + a.shape, a.dtype) for a in arrs]

    def body(*refs):
        srcs, lnds = refs[:n], refs[n:2 * n]
        outs = refs[2 * n + 1:]
        send_sems, recv_sems = outs[:n], outs[n:2 * n]
        token = outs[4 * n]
        for cp in _split_copies(srcs, lnds, send_sems, recv_sems, scatter):
            cp.start()
        for cp in _own_copies(srcs, lnds, recv_sems, scatter):
            cp.start()
        token[...] = jnp.zeros_like(token)

    sem = pltpu.SemaphoreType.DMA((N_DEV,))
    out_shape = ([sem] * (2 * n) + [pltpu.HBM(a.shape, a.dtype) for a in arrs]
                 + [pltpu.HBM(l.shape, l.dtype) for l in lands] + [jax.ShapeDtypeStruct((SUBLANES, LANES), F32)])
    res = pl.pallas_call(
        body, name=name, out_shape=tuple(out_shape),
        in_specs=[_HBM] * (2 * n) + [pl.BlockSpec(memory_space=pl.ANY)],
        out_specs=tuple([_SEM] * (2 * n) + [_HBM] * (2 * n) + [pl.BlockSpec(memory_space=pltpu.VMEM)]),
        input_output_aliases={i: 2 * n + i for i in range(2 * n)},
        compiler_params=pltpu.CompilerParams(has_side_effects=_EFFECT),
    )(*[pltpu.with_memory_space_constraint(a, pltpu.HBM) for a in arrs],
      *[pltpu.with_memory_space_constraint(l, pltpu.HBM) for l in lands], after)
    handles = [(res[2 * n + a], res[3 * n + a], res[a], res[n + a]) for a in range(n)]
    return handles, res[4 * n]


def _exchange_wait(handles, scatter, name, after):
    n = len(handles)

    def body(*refs):
        srcs, lnds = refs[:n], refs[n:2 * n]
        send_sems, recv_sems = refs[2 * n:3 * n], refs[3 * n:4 * n]
        for cp in _split_copies(srcs, lnds, send_sems, recv_sems, scatter):
            cp.wait_send()
            cp.wait_recv()
        for cp in _own_copies(srcs, lnds, recv_sems, scatter):
            cp.wait()

    srcs = [h[0] for h in handles]
    lands = [h[1] for h in handles]
    res = pl.pallas_call(
        body, name=name,
        out_shape=tuple([pltpu.HBM(a.shape, a.dtype) for a in srcs] + [pltpu.HBM(l.shape, l.dtype) for l in lands]),
        in_specs=[_HBM] * (2 * n) + [_SEM] * (2 * n) + [pl.BlockSpec(memory_space=pl.ANY)],
        out_specs=tuple([_HBM] * (2 * n)),
        input_output_aliases={i: i for i in range(2 * n)},
        compiler_params=pltpu.CompilerParams(has_side_effects=_EFFECT),
    )(*srcs, *lands, *[h[2] for h in handles], *[h[3] for h in handles], after)
    return list(res[n:])


def _row_groups(tm):
    return [slice(r * (tm // ROW_GROUPS), (r + 1) * (tm // ROW_GROUPS)) for r in range(ROW_GROUPS)]


def _stack_gate_up(wgt, wut, tf):
    F, D = wgt.shape
    return jnp.stack([wgt.reshape(F // tf, tf, D), wut.reshape(F // tf, tf, D)], axis=1).reshape(2 * F, D)


def _ffn_fwd(x, gain, wgu, wd, after, name, tm, tf, head=None):
    T, D = x.shape
    F = wd.shape[0]
    nj = F // tf
    n_in = 7 if head else 5

    def body(*refs):
        x_ref, g_ref, wgu_ref, wd_ref, after_ref = refs[:5]
        outs = refs[n_in:]
        xo_ref, xn_ref, gu_ref = outs[:3]
        xn_sc, acc_sc = outs[-2:]
        i, j = pl.program_id(0), pl.program_id(1)

        @pl.when(j == 0)
        def _():
            _, xhat = _rms_stats(x_ref[...])
            xn = (xhat * g_ref[...]).astype(BF16)
            xn_sc[...] = xn
            xn_ref[...] = xn
            acc_sc[...] = jnp.zeros_like(acc_sc)

        groups = _row_groups(tm)
        gus = [_dot_nt(xn_sc[rows, :], wgu_ref[...]) for rows in groups]
        hs = []
        for rows, gu in zip(groups, gus):
            gu_ref[rows, :] = gu.astype(BF16)
            g, u = gu[:, :tf], gu[:, tf:]
            hs.append((g * _sigmoid(g) * u).astype(BF16))
        for rows, h in zip(groups, hs):
            acc_sc[rows, :] += _dot(h, wd_ref[...])

        if head:
            t_ref, fg_ref = refs[5:7]
            dgain_ref, loss_ref = outs[3:5]

            @pl.when((i == 0) & (j == 0))
            def _():
                dgain_ref[...] = jnp.zeros_like(dgain_ref)
                loss_ref[...] = jnp.zeros_like(loss_ref)

        @pl.when(j == nj - 1)
        def _():
            for rows in groups:
                xo = x_ref[rows, :] + 0.5 * acc_sc[rows, :]
                if head:
                    fg = fg_ref[...]
                    _, xhat = _rms_stats(xo)
                    err = xhat * fg - t_ref[rows, :]
                    loss_ref[...] += 0.5 * jnp.sum(jnp.mean(err * err, axis=-1, keepdims=True),
                                                   axis=0, keepdims=True)
                    dx, dgain = _rms_bwd(err * (1.0 / D), xo, fg)
                    dgain_ref[...] += dgain
                    xo_ref[rows, :] = dx
                else:
                    xo_ref[rows, :] = xo

    tile = pl.BlockSpec((tm, D), lambda i, j: (i, 0))
    const = pl.BlockSpec((1, D), lambda i, j: (0, 0))
    in_specs = [tile, const, pl.BlockSpec((2 * tf, D), lambda i, j: (j, 0)),
                pl.BlockSpec((tf, D), lambda i, j: (j, 0)), pl.BlockSpec(memory_space=pl.ANY)]
    out_specs = [tile, pl.BlockSpec((tm, D), lambda i, j: (i, 0)), pl.BlockSpec((tm, 2 * tf), lambda i, j: (i, j))]
    out_shape = [pltpu.HBM((T,D), F32), pltpu.HBM((T,D), BF16),
                 pltpu.HBM((T,2 * F), BF16)]
    operands = [x, gain, wgu, wd, after]
    if head:
        in_specs += [tile, const]
        out_specs += [const, pl.BlockSpec((SUBLANES, LANES), lambda i, j: (0, 0))]
        out_shape += [jax.ShapeDtypeStruct((1, D), F32), jax.ShapeDtypeStruct((SUBLANES, LANES), F32)]
        operands += list(head)
    return pl.pallas_call(
        body, name=name, grid=(T // tm, nj),
        in_specs=in_specs, out_specs=tuple(out_specs), out_shape=tuple(out_shape),
        scratch_shapes=[pltpu.VMEM((tm, D), BF16), pltpu.VMEM((tm, D), F32)],
        compiler_params=pltpu.CompilerParams(dimension_semantics=("arbitrary", "arbitrary"),
                                             vmem_limit_bytes=VMEM_LIMIT_FFN_BYTES),
    )(*_hbm(*operands))


def _ffn_bwd(dy, x, gain, gu, wgu, wd, after, name, tm, tf):
    T, D = x.shape
    F = wd.shape[0]
    nj = F // tf

    def body(dy_ref, x_ref, g_ref, gu_ref, wgu_ref, wd_ref, after_ref,
             dx_ref, dgain_ref, dg_ref, du_ref, hh_ref, do_ref, do_sc, acc_sc, dgu_sc):
        i, j = pl.program_id(0), pl.program_id(1)

        @pl.when((i == 0) & (j == 0))
        def _():
            dgain_ref[...] = jnp.zeros_like(dgain_ref)

        @pl.when(j == 0)
        def _():
            do = (0.5 * dy_ref[...]).astype(BF16)
            do_sc[...] = do
            do_ref[...] = do
            acc_sc[...] = jnp.zeros_like(acc_sc)

        groups = _row_groups(tm)
        dhs = [_dot_nt(do_sc[rows, :], wd_ref[...]) for rows in groups]
        for rows, dh in zip(groups, dhs):
            g = gu_ref[rows, :tf].astype(F32)
            u = gu_ref[rows, tf:].astype(F32)
            sig = _sigmoid(g)
            s = g * sig
            dg = (dh * u * (sig + s * (1.0 - sig))).astype(BF16)
            du = (dh * s).astype(BF16)
            dg_ref[rows, :] = dg
            du_ref[rows, :] = du
            dgu_sc[rows, :tf] = dg
            dgu_sc[rows, tf:] = du
            hh_ref[rows, :] = (s * u).astype(BF16)
        for rows in groups:
            acc_sc[rows, :] += _dot(dgu_sc[rows, :], wgu_ref[...])

        @pl.when(j == nj - 1)
        def _():
            for rows in groups:
                dx, dgain = _rms_bwd(acc_sc[rows, :], x_ref[rows, :], g_ref[...])
                dgain_ref[...] += dgain
                dx_ref[rows, :] = dy_ref[rows, :] + dx

    tile_in = pl.BlockSpec((tm, D), lambda i, j: (i, 0))
    tile = pl.BlockSpec((tm, D), lambda i, j: (i, 0), pipeline_mode=pl.Buffered(1))
    return pl.pallas_call(
        body, name=name, grid=(T // tm, nj),
        in_specs=[tile_in, tile_in,
                  pl.BlockSpec((1, D), lambda i, j: (0, 0)),
                  pl.BlockSpec((tm, 2 * tf), lambda i, j: (i, j)),
                  pl.BlockSpec((2 * tf, D), lambda i, j: (j, 0)),
                  pl.BlockSpec((tf, D), lambda i, j: (j, 0)),
                  pl.BlockSpec(memory_space=pl.ANY)],
        out_specs=(tile,
                   pl.BlockSpec((1, D), lambda i, j: (0, 0)),
                   pl.BlockSpec((tm, tf), lambda i, j: (i, j)),
                   pl.BlockSpec((tm, tf), lambda i, j: (i, j)),
                   pl.BlockSpec((tm, tf), lambda i, j: (i, j)),
                   pl.BlockSpec((tm, D), lambda i, j: (i, 0))),
        out_shape=(pltpu.HBM((T,D), F32), jax.ShapeDtypeStruct((1, D), F32),
                   pltpu.HBM((T,F), BF16), pltpu.HBM((T,F), BF16),
                   pltpu.HBM((T,F), BF16), pltpu.HBM((T,D), BF16)),
        scratch_shapes=[pltpu.VMEM((tm, D), BF16), pltpu.VMEM((tm, D), F32), pltpu.VMEM((tm, 2 * tf), BF16)],
        compiler_params=pltpu.CompilerParams(dimension_semantics=("arbitrary", "arbitrary"),
                                             vmem_limit_bytes=VMEM_LIMIT_FFN_BYTES),
    )(*_hbm(dy, x, gain, gu, wgu, wd, after))


def _tn_grad(a, b, after, name, bm, tk):
    T, M = a.shape
    D = b.shape[1]
    nk = T // tk

    def body(a_ref, b_ref, after_ref, o_ref, acc_sc):
        k = pl.program_id(1)

        @pl.when(k == 0)
        def _():
            acc_sc[...] = jnp.zeros_like(acc_sc)

        acc_sc[...] += _dot_tn(a_ref[...], b_ref[...])

        @pl.when(k == nk - 1)
        def _():
            o_ref[...] = acc_sc[...].astype(BF16)

    return pl.pallas_call(
        body, name=name, grid=(M // bm, nk),
        in_specs=[pl.BlockSpec((tk, bm), lambda i, k: (k, i)), pl.BlockSpec((tk, D), lambda i, k: (k, 0)),
                  pl.BlockSpec(memory_space=pl.ANY)],
        out_specs=pl.BlockSpec((bm, D), lambda i, k: (i, 0)),
        out_shape=pltpu.HBM((M, D), BF16),
        scratch_shapes=[pltpu.VMEM((bm, D), F32)],
        compiler_params=_params("arbitrary", "arbitrary"),
    )(*_hbm(a, b, after))


_Z_SPLITS = (0, 512, 640, 768, 1280, 1792, 2304)


def _mixin_fwd(x, gain, w_in_t, name, tm):
    T, D = x.shape
    widths = [b - a for a, b in zip(_Z_SPLITS[:-1], _Z_SPLITS[1:])]

    def body(x_ref, g_ref, w_ref, hn_ref, *outs):
        _, xhat = _rms_stats(x_ref[...])
        hn = (xhat * g_ref[...]).astype(BF16)
        hn_ref[...] = hn
        for n in (0, 3, 4, 5):
            outs[n][...] = _dot_nt(hn, w_ref[_Z_SPLITS[n]:_Z_SPLITS[n + 1], :]).astype(BF16)
        kv = _dot_nt(hn, w_ref[_Z_SPLITS[1]:_Z_SPLITS[3], :]).astype(BF16)
        outs[1][...] = kv[:, :KV_WIDTH]
        outs[2][...] = kv[:, KV_WIDTH:]

    return pl.pallas_call(
        body, name=name, grid=(T // tm,),
        in_specs=[pl.BlockSpec((tm, D), lambda i: (i, 0)),
                  pl.BlockSpec((1, D), lambda i: (0, 0)),
                  pl.BlockSpec(w_in_t.shape, lambda i: (0, 0))],
        out_specs=tuple([pl.BlockSpec((tm, D), lambda i: (i, 0))]
                        + [pl.BlockSpec((tm, w), lambda i: (i, 0)) for w in widths]),
        out_shape=tuple([pltpu.HBM((T,D), BF16)]
                        + [pltpu.HBM((T,w), BF16) for w in widths]),
        compiler_params=_params("arbitrary"),
    )(*_hbm(x, gain, w_in_t))


def _bucket_table():
    qi = np.arange(BLOCK, dtype=np.int32)[:, None]
    kj = np.arange(2 * BLOCK, dtype=np.int32)[None, :]
    dist = qi + BLOCK - kj
    n = np.maximum(dist, 0)
    max_exact = NUM_BUCKETS // 2
    large = max_exact + (np.log(np.maximum(n, 1).astype(np.float32) / max_exact)
                         / math.log(MAX_DISTANCE / max_exact)
                         * (NUM_BUCKETS - max_exact)).astype(np.int32)
    large = np.minimum(large, NUM_BUCKETS - 1)
    bucket = np.where(n < max_exact, n, large).astype(np.int32)
    valid = (dist >= 0) & (dist < WINDOW)
    return np.where(valid, bucket, -1).astype(np.int32)


def _bias_build(table, bucket, name):
    def body(t_ref, b_ref, o_ref):
        bk = b_ref[...]
        for h in range(N_Q_HEADS):
            def step(b, acc):
                return jnp.where(bk == b, t_ref[b, h], acc)
            o_ref[h] = lax.fori_loop(0, NUM_BUCKETS, step, jnp.full(bk.shape, MASKED, F32))

    return pl.pallas_call(
        body, name=name,
        in_specs=[pl.BlockSpec(memory_space=pltpu.SMEM), pl.BlockSpec(memory_space=pltpu.VMEM)],
        out_specs=pl.BlockSpec(memory_space=pltpu.VMEM),
        out_shape=jax.ShapeDtypeStruct((N_Q_HEADS,) + bucket.shape, F32),
    )(table, bucket)


def _bias_grad(dbias, bucket, name):
    def body(d_ref, b_ref, o_ref):
        bk = b_ref[...]
        row = lax.broadcasted_iota(jnp.int32, o_ref.shape, 0)
        lane = lax.broadcasted_iota(jnp.int32, o_ref.shape, 1)
        res = jnp.zeros(o_ref.shape, F32)
        for h in range(N_Q_HEADS):
            d = d_ref[h]

            def step(b, acc):
                part = jnp.sum(jnp.where(bk == b, d, 0.0), axis=0, keepdims=True)
                return jnp.where(row == b, part, acc)
            per_lane = lax.fori_loop(0, NUM_BUCKETS, step, jnp.zeros(o_ref.shape, F32))
            res = jnp.where(lane == h, jnp.sum(per_lane, axis=1, keepdims=True), res)
        o_ref[...] = res

    return pl.pallas_call(
        body, name=name,
        in_specs=[pl.BlockSpec(memory_space=pltpu.VMEM), pl.BlockSpec(memory_space=pltpu.VMEM)],
        out_specs=pl.BlockSpec(memory_space=pltpu.VMEM),
        out_shape=jax.ShapeDtypeStruct((NUM_BUCKETS, LANES), F32),
    )(*_hbm(dbias, bucket))


def _head_cols(h):
    return slice(h * HEAD_DIM, (h + 1) * HEAD_DIM)


def _stack_heads(ref, r0, g, dtype):
    return jnp.concatenate(
        [ref[pl.ds(r0, BLOCK), _head_cols(GQA_GROUP * g + j)].astype(dtype) for j in range(GQA_GROUP)], axis=0)


def _unstack_heads(ref, r0, g, val):
    for j in range(GQA_GROUP):
        ref[pl.ds(r0, BLOCK), _head_cols(GQA_GROUP * g + j)] = val[j * BLOCK:(j + 1) * BLOCK, :]


def _head_lanes(h):
    return slice(h * BLOCK, (h + 1) * BLOCK)


def _group_lanes(g):
    return slice(g * GROUP_ROWS, (g + 1) * GROUP_ROWS)


def _head_softmax(st, bias_t, sink, no_prev):
    s = st * SCALE + bias_t
    row = lax.broadcasted_iota(jnp.int32, s.shape, 0)
    s = jnp.where(no_prev & (row < BLOCK), MASKED, s)
    m = jnp.maximum(jnp.max(s, axis=0, keepdims=True), sink)
    p = jnp.exp(s - m)
    ps = jnp.exp(sink - m)
    r = 1.0 / (jnp.sum(p, axis=0, keepdims=True) + ps)
    return p * r, ps * r


def _load_band(kf_sc, vf_sc, kp_ref, kc_ref, vp_ref, vc_ref, tq):
    kf_sc[0:BLOCK, :] = kp_ref[...]
    kf_sc[BLOCK:BLOCK + tq, :] = kc_ref[...]
    vf_sc[0:BLOCK, :] = vp_ref[...]
    vf_sc[BLOCK:BLOCK + tq, :] = vc_ref[...]


def _attn_fwd(q, k, v, bias_t, sinks, gain, name, nblk):
    T = q.shape[0]
    tq = nblk * BLOCK

    def body(sink_ref, q_ref, kc_ref, kp_ref, vc_ref, vp_ref, bias_ref, g_ref, raw_ref, nrm_ref,
             kf_sc, vf_sc, o_sc, st_sc, pt_sc):
        i = pl.program_id(0)
        _load_band(kf_sc, vf_sc, kp_ref, kc_ref, vp_ref, vc_ref, tq)

        def block(b, carry):
            r0 = pl.multiple_of(b * BLOCK, BLOCK)
            no_prev = (i == 0) & (b == 0)
            for g in range(N_KV_HEADS):
                kb = kf_sc[pl.ds(r0, 2 * BLOCK), _head_cols(g)]
                st_sc[:, _group_lanes(g)] = _dot_nt(kb, _stack_heads(q_ref, r0, g, BF16))
            for h in range(N_Q_HEADS):
                p, _ = _head_softmax(st_sc[:, _head_lanes(h)], bias_ref[h], sink_ref[h], no_prev)
                pt_sc[:, _head_lanes(h)] = p.astype(BF16)
            for g in range(N_KV_HEADS):
                vb = vf_sc[pl.ds(r0, 2 * BLOCK), _head_cols(g)]
                _unstack_heads(o_sc, r0, g, _dot_tn(pt_sc[:, _group_lanes(g)], vb))
            return carry

        lax.fori_loop(0, nblk, block, 0, unroll=True)
        o = o_sc[...]
        raw_ref[...] = o.astype(BF16)
        _, ohat = _rms_stats(o)
        nrm_ref[...] = (ohat * g_ref[...]).astype(BF16)

    cur = lambda i: (i, 0)
    prev = lambda i: (jnp.maximum(i * nblk - 1, 0), 0)
    lanes = N_Q_HEADS * BLOCK
    return pl.pallas_call(
        body, name=name, grid=(T // tq,),
        in_specs=[pl.BlockSpec(memory_space=pltpu.SMEM),
                  pl.BlockSpec((tq, ATTN_WIDTH), cur),
                  pl.BlockSpec((tq, KV_WIDTH), cur), pl.BlockSpec((BLOCK, KV_WIDTH), prev),
                  pl.BlockSpec((tq, KV_WIDTH), cur), pl.BlockSpec((BLOCK, KV_WIDTH), prev),
                  pl.BlockSpec(bias_t.shape, lambda i: (0, 0, 0)),
                  pl.BlockSpec((1, ATTN_WIDTH), lambda i: (0, 0))],
        out_specs=(pl.BlockSpec((tq, ATTN_WIDTH), cur), pl.BlockSpec((tq, ATTN_WIDTH), cur)),
        out_shape=(pltpu.HBM((T,ATTN_WIDTH), BF16), pltpu.HBM((T,ATTN_WIDTH), BF16)),
        scratch_shapes=[pltpu.VMEM((tq + BLOCK, KV_WIDTH), BF16), pltpu.VMEM((tq + BLOCK, KV_WIDTH), BF16),
                        pltpu.VMEM((tq, ATTN_WIDTH), F32),
                        pltpu.VMEM((2 * BLOCK, lanes), F32), pltpu.VMEM((2 * BLOCK, lanes), BF16)],
        compiler_params=_params("arbitrary"),
    )(sinks, *_hbm(q, k, k, v, v, bias_t, gain))


def _attn_bwd(dmixed, raw, q, k, v, bias_t, sinks, gain, name, nblk):
    T = q.shape[0]
    tq = nblk * BLOCK
    nt = T // tq
    lanes = N_Q_HEADS * BLOCK

    def body(sink_ref, dm_ref, raw_ref, q_ref, kc_ref, kp_ref, vc_ref, vp_ref, bias_ref, g_ref,
             dq_ref, dk_ref, dv_ref, dbias_ref, dsink_ref, dgain_ref,
             do_sc, dq_sc, kf_sc, vf_sc, dkf_sc, dvf_sc, st_sc, dpt_sc, pt_sc, dst_sc, drow_sc,
             qs_sc, dos_sc, dsink_sc):
        i = pl.program_id(0)
        tile = nt - 1 - i

        @pl.when(i == 0)
        def _():
            dkf_sc[...] = jnp.zeros_like(dkf_sc)
            dvf_sc[...] = jnp.zeros_like(dvf_sc)
            dsink_sc[...] = jnp.zeros_like(dsink_sc)
            dbias_ref[...] = jnp.zeros_like(dbias_ref)
            dgain_ref[...] = jnp.zeros_like(dgain_ref)

        carry_k = dkf_sc[0:BLOCK, :]
        carry_v = dvf_sc[0:BLOCK, :]
        dkf_sc[0:tq, :] = jnp.zeros((tq, KV_WIDTH), F32)
        dvf_sc[0:tq, :] = jnp.zeros((tq, KV_WIDTH), F32)
        dkf_sc[tq:tq + BLOCK, :] = carry_k
        dvf_sc[tq:tq + BLOCK, :] = carry_v
        _load_band(kf_sc, vf_sc, kp_ref, kc_ref, vp_ref, vc_ref, tq)

        do, dgain = _rms_bwd(dm_ref[...].astype(F32), raw_ref[...].astype(F32), g_ref[...])
        dgain_ref[...] += dgain
        do_sc[...] = do
        ones = jnp.ones((SUBLANES, HEAD_DIM), BF16)

        def block(b, carry):
            r0 = pl.multiple_of(b * BLOCK, BLOCK)
            no_prev = (tile == 0) & (b == 0)
            for g in range(N_KV_HEADS):
                kb = kf_sc[pl.ds(r0, 2 * BLOCK), _head_cols(g)]
                vb = vf_sc[pl.ds(r0, 2 * BLOCK), _head_cols(g)]
                qg = _stack_heads(q_ref, r0, g, BF16)
                dog = _stack_heads(do_sc, r0, g, F32)
                prod = dog * _stack_heads(raw_ref, r0, g, F32)
                hi = prod.astype(BF16)
                lo = (prod - hi.astype(F32)).astype(BF16)
                drow_sc[:, _group_lanes(g)] = _dot_nt(ones, hi) + _dot_nt(ones, lo)
                dogb = dog.astype(BF16)
                qs_sc[g] = qg
                dos_sc[g] = dogb
                st_sc[:, _group_lanes(g)] = _dot_nt(kb, qg)
                dpt_sc[:, _group_lanes(g)] = _dot_nt(vb, dogb)
            for h in range(N_Q_HEADS):
                hl = _head_lanes(h)
                p, ps = _head_softmax(st_sc[:, hl], bias_ref[h], sink_ref[h], no_prev)
                rowdot = drow_sc[0:1, hl]
                ds = p * (dpt_sc[:, hl] - rowdot)
                dsink_sc[h:h + 1, :] += -(ps * rowdot)
                dbias_ref[h] += ds
                dst_sc[:, hl] = ds.astype(BF16)
                pt_sc[:, hl] = p.astype(BF16)
            for g in range(N_KV_HEADS):
                kb = kf_sc[pl.ds(r0, 2 * BLOCK), _head_cols(g)]
                dsg = dst_sc[:, _group_lanes(g)]
                _unstack_heads(dq_sc, r0, g, _dot_tn(dsg, kb) * SCALE)
                dkf_sc[pl.ds(r0, 2 * BLOCK), _head_cols(g)] += _dot(dsg, qs_sc[g]) * SCALE
                dvf_sc[pl.ds(r0, 2 * BLOCK), _head_cols(g)] += _dot(pt_sc[:, _group_lanes(g)], dos_sc[g])
            return carry

        lax.fori_loop(0, nblk, block, 0, unroll=True)
        dq_ref[...] = dq_sc[...].astype(BF16)
        dk_ref[...] = dkf_sc[BLOCK:BLOCK + tq, :].astype(BF16)
        dv_ref[...] = dvf_sc[BLOCK:BLOCK + tq, :].astype(BF16)

        @pl.when(i == nt - 1)
        def _():
            tot = jnp.sum(dsink_sc[...], axis=1, keepdims=True)
            dsink_ref[...] = jnp.broadcast_to(tot, dsink_ref.shape)

    cur = lambda i: (nt - 1 - i, 0)
    prev = lambda i: (jnp.maximum((nt - 1 - i) * nblk - 1, 0), 0)
    const2 = lambda i: (0, 0)
    const3 = lambda i: (0, 0, 0)
    return pl.pallas_call(
        body, name=name, grid=(nt,),
        in_specs=[pl.BlockSpec(memory_space=pltpu.SMEM),
                  pl.BlockSpec((tq, ATTN_WIDTH), cur),
                  pl.BlockSpec((tq, ATTN_WIDTH), cur),
                  pl.BlockSpec((tq, ATTN_WIDTH), cur),
                  pl.BlockSpec((tq, KV_WIDTH), cur), pl.BlockSpec((BLOCK, KV_WIDTH), prev),
                  pl.BlockSpec((tq, KV_WIDTH), cur), pl.BlockSpec((BLOCK, KV_WIDTH), prev),
                  pl.BlockSpec(bias_t.shape, const3),
                  pl.BlockSpec((1, ATTN_WIDTH), const2)],
        out_specs=(pl.BlockSpec((tq, ATTN_WIDTH), cur),
                   pl.BlockSpec((tq, KV_WIDTH), cur), pl.BlockSpec((tq, KV_WIDTH), cur),
                   pl.BlockSpec(bias_t.shape, const3),
                   pl.BlockSpec((N_Q_HEADS, LANES), const2),
                   pl.BlockSpec((1, ATTN_WIDTH), const2)),
        out_shape=(pltpu.HBM((T,ATTN_WIDTH), BF16),
                   pltpu.HBM((T,KV_WIDTH), BF16), pltpu.HBM((T,KV_WIDTH), BF16),
                   jax.ShapeDtypeStruct(bias_t.shape, F32),
                   jax.ShapeDtypeStruct((N_Q_HEADS, LANES), F32),
                   jax.ShapeDtypeStruct((1, ATTN_WIDTH), F32)),
        scratch_shapes=[pltpu.VMEM((tq, ATTN_WIDTH), F32), pltpu.VMEM((tq, ATTN_WIDTH), F32),
                        pltpu.VMEM((tq + BLOCK, KV_WIDTH), BF16), pltpu.VMEM((tq + BLOCK, KV_WIDTH), BF16),
                        pltpu.VMEM((tq + BLOCK, KV_WIDTH), F32), pltpu.VMEM((tq + BLOCK, KV_WIDTH), F32),
                        pltpu.VMEM((2 * BLOCK, lanes), F32), pltpu.VMEM((2 * BLOCK, lanes), F32),
                        pltpu.VMEM((2 * BLOCK, lanes), BF16), pltpu.VMEM((2 * BLOCK, lanes), BF16),
                        pltpu.VMEM((SUBLANES, lanes), F32),
                        pltpu.VMEM((N_KV_HEADS, GROUP_ROWS, HEAD_DIM), BF16),
                        pltpu.VMEM((N_KV_HEADS, GROUP_ROWS, HEAD_DIM), BF16),
                        pltpu.VMEM((N_Q_HEADS, LANES), F32)],
        compiler_params=_params("arbitrary"),
    )(sinks, *_hbm(dmixed, raw, q, k, k, v, v, bias_t, gain))


def _shift_down(cu, tail):
    row = lax.broadcasted_iota(jnp.int32, cu.shape, 0)
    t6, t7 = tail[6:7, :], tail[7:8, :]
    s1 = jnp.where(row == 0, t7, pltpu.roll(cu, 1, 0))
    s2 = jnp.where(row == 0, t6, jnp.where(row == 1, t7, pltpu.roll(cu, 2, 0)))
    return s1, s2


def _shift_up(d, head):
    n = d.shape[0]
    row = lax.broadcasted_iota(jnp.int32, d.shape, 0)
    h0, h1 = head[0:1, :], head[1:2, :]
    s1 = jnp.where(row == n - 1, h0, pltpu.roll(d, n - 1, 0))
    s2 = jnp.where(row == n - 1, h1, jnp.where(row == n - 2, h0, pltpu.roll(d, n - 2, 0)))
    return s1, s2


def _mixout_fwd(x, attn_n, u, gb, gc, conv_w, gain, w_out, name, tm):
    T, D = x.shape

    def body(x_ref, an_ref, u_ref, b_ref, c_ref, cw_ref, g_ref, wo_ref, xo_ref, cn_ref, tail_sc):
        @pl.when(pl.program_id(0) == 0)
        def _():
            tail_sc[...] = jnp.zeros_like(tail_sc)

        cu = c_ref[...].astype(F32) * u_ref[...].astype(F32)
        s1, s2 = _shift_down(cu, tail_sc[...])
        tail_sc[...] = cu[tm - SUBLANES:tm, :]
        pre = cw_ref[0:1, :] * s2 + cw_ref[1:2, :] * s1 + cw_ref[2:3, :] * cu
        conv = b_ref[...].astype(F32) * pre
        _, chat = _rms_stats(conv)
        cn = (chat * g_ref[...]).astype(BF16)
        cn_ref[...] = cn
        xo_ref[...] = (x_ref[...] + _dot(an_ref[...], wo_ref[0:ATTN_WIDTH, :])
                       + _dot(cn, wo_ref[ATTN_WIDTH:ATTN_WIDTH + CONV_DIM, :]))

    row = lambda i: (i, 0)
    const = lambda i: (0, 0)
    return pl.pallas_call(
        body, name=name, grid=(T // tm,),
        in_specs=[pl.BlockSpec((tm, D), row), pl.BlockSpec((tm, ATTN_WIDTH), row),
                  pl.BlockSpec((tm, CONV_DIM), row), pl.BlockSpec((tm, CONV_DIM), row),
                  pl.BlockSpec((tm, CONV_DIM), row),
                  pl.BlockSpec(conv_w.shape, const), pl.BlockSpec((1, CONV_DIM), const),
                  pl.BlockSpec(w_out.shape, const)],
        out_specs=(pl.BlockSpec((tm, D), row), pl.BlockSpec((tm, CONV_DIM), row)),
        out_shape=(pltpu.HBM((T,D), F32), pltpu.HBM((T,CONV_DIM), BF16)),
        scratch_shapes=[pltpu.VMEM((SUBLANES, CONV_DIM), F32)],
        compiler_params=_params("arbitrary"),
    )(*_hbm(x, attn_n, u, gb, gc, conv_w, gain, w_out))


def _mixout_bwd(dy, attn_n, conv_n, w_out, after, name, tm):
    T, D = dy.shape
    W = ATTN_WIDTH + CONV_DIM
    nt = T // tm

    def body(dy_ref, an_ref, cn_ref, w_ref, after_ref, dm_ref, dw_ref, dw_sc):
        i = pl.program_id(0)

        @pl.when(i == 0)
        def _():
            dw_sc[...] = jnp.zeros_like(dw_sc)

        dyb = dy_ref[...].astype(BF16)
        dm_ref[...] = _dot_nt(dyb, w_ref[...]).astype(BF16)
        dw_sc[0:ATTN_WIDTH, :] += _dot_tn(an_ref[...], dyb)
        dw_sc[ATTN_WIDTH:W, :] += _dot_tn(cn_ref[...], dyb)

        @pl.when(i == nt - 1)
        def _():
            dw_ref[...] = dw_sc[...].astype(BF16)

    row = lambda i: (i, 0)
    const = lambda i: (0, 0)
    return pl.pallas_call(
        body, name=name, grid=(nt,),
        in_specs=[pl.BlockSpec((tm, D), row), pl.BlockSpec((tm, ATTN_WIDTH), row),
                  pl.BlockSpec((tm, CONV_DIM), row), pl.BlockSpec(w_out.shape, const),
                  pl.BlockSpec(memory_space=pl.ANY)],
        out_specs=(pl.BlockSpec((tm, W), row), pl.BlockSpec((W, D), const)),
        out_shape=(pltpu.HBM((T,W), BF16), pltpu.HBM((W, D), BF16)),
        scratch_shapes=[pltpu.VMEM((W, D), F32)],
        compiler_params=_params("arbitrary"),
    )(*_hbm(dy, attn_n, conv_n, w_out, after))


def _conv_bwd(dmixed, u, gb, gc, conv_w, gain, name, tc):
    T = u.shape[0]
    nt = T // tc
    per8 = tc // SUBLANES

    def body(dm_ref, u_ref, b_ref, c_ref, ut_ref, ct_ref, cw_ref, g_ref,
             du_ref, db_ref, dc_ref, dcw_ref, dgain_ref, head_sc):
        i = pl.program_id(0)

        @pl.when(i == 0)
        def _():
            head_sc[...] = jnp.zeros_like(head_sc)
            dcw_ref[...] = jnp.zeros_like(dcw_ref)
            dgain_ref[...] = jnp.zeros_like(dgain_ref)

        uu = u_ref[...].astype(F32)
        cc = c_ref[...].astype(F32)
        bb = b_ref[...].astype(F32)
        cu = cc * uu
        tail = jnp.where(i == nt - 1, 0.0, ct_ref[...].astype(F32) * ut_ref[...].astype(F32))
        s1, s2 = _shift_down(cu, tail)
        w0, w1, w2 = cw_ref[0:1, :], cw_ref[1:2, :], cw_ref[2:3, :]
        pre = w0 * s2 + w1 * s1 + w2 * cu
        dconv, dgain = _rms_bwd(dm_ref[...].astype(F32), bb * pre, g_ref[...])
        dgain_ref[...] += dgain
        db_ref[...] = (dconv * pre).astype(BF16)
        dpre = dconv * bb
        dcw_ref[0:1, :] += jnp.sum(dpre * s2, axis=0, keepdims=True)
        dcw_ref[1:2, :] += jnp.sum(dpre * s1, axis=0, keepdims=True)
        dcw_ref[2:3, :] += jnp.sum(dpre * cu, axis=0, keepdims=True)
        n1, n2 = _shift_up(dpre, head_sc[...])
        head_sc[...] = dpre[0:SUBLANES, :]
        dcu = w2 * dpre + w1 * n1 + w0 * n2
        du_ref[...] = (dcu * cc).astype(BF16)
        dc_ref[...] = (dcu * uu).astype(BF16)

    rev = lambda i: (nt - 1 - i, 0)
    rev_right = lambda i: (nt - 1 - i, 1)
    tail_map = lambda i: (jnp.maximum((nt - 1 - i) * per8 - 1, 0), 0)
    const = lambda i: (0, 0)
    return pl.pallas_call(
        body, name=name, grid=(nt,),
        in_specs=[pl.BlockSpec((tc, CONV_DIM), rev_right),
                  pl.BlockSpec((tc, CONV_DIM), rev), pl.BlockSpec((tc, CONV_DIM), rev),
                  pl.BlockSpec((tc, CONV_DIM), rev),
                  pl.BlockSpec((SUBLANES, CONV_DIM), tail_map), pl.BlockSpec((SUBLANES, CONV_DIM), tail_map),
                  pl.BlockSpec(conv_w.shape, const), pl.BlockSpec((1, CONV_DIM), const)],
        out_specs=(pl.BlockSpec((tc, CONV_DIM), rev), pl.BlockSpec((tc, CONV_DIM), rev),
                   pl.BlockSpec((tc, CONV_DIM), rev),
                   pl.BlockSpec((SUBLANES, CONV_DIM), const), pl.BlockSpec((1, CONV_DIM), const)),
        out_shape=(pltpu.HBM((T,CONV_DIM), BF16), pltpu.HBM((T,CONV_DIM), BF16),
                   pltpu.HBM((T,CONV_DIM), BF16),
                   jax.ShapeDtypeStruct((SUBLANES, CONV_DIM), F32), jax.ShapeDtypeStruct((1, CONV_DIM), F32)),
        scratch_shapes=[pltpu.VMEM((SUBLANES, CONV_DIM), F32)],
        compiler_params=_params("arbitrary"),
    )(*_hbm(dmixed, u, gb, gc, u, gc, conv_w, gain))


def _mixin_bwd(dy, x, gain, dz, w_in_t, name, tm):
    T, D = x.shape
    nz = len(dz)

    def body(dy_ref, x_ref, g_ref, *rest):
        dz_refs, wt_ref, dx_ref, dgain_ref, dz_sc = rest[:nz], rest[nz], rest[nz + 1], rest[nz + 2], rest[nz + 3]

        @pl.when(pl.program_id(0) == 0)
        def _():
            dgain_ref[...] = jnp.zeros_like(dgain_ref)

        for r, lo, hi in zip(dz_refs, _Z_SPLITS[:-1], _Z_SPLITS[1:]):
            dz_sc[:, lo:hi] = r[...]
        dx, dgain = _rms_bwd(_dot(dz_sc[...], wt_ref[...]), x_ref[...], g_ref[...])
        dgain_ref[...] += dgain
        dx_ref[...] = dy_ref[...] + dx

    row = lambda i: (i, 0)
    const = lambda i: (0, 0)
    return pl.pallas_call(
        body, name=name, grid=(T // tm,),
        in_specs=[pl.BlockSpec((tm, D), row), pl.BlockSpec((tm, D), row), pl.BlockSpec((1, D), const)]
                 + [pl.BlockSpec((tm, a.shape[1]), row) for a in dz]
                 + [pl.BlockSpec(w_in_t.shape, const)],
        out_specs=(pl.BlockSpec((tm, D), row), pl.BlockSpec((1, D), const)),
        out_shape=(pltpu.HBM((T,D), F32), jax.ShapeDtypeStruct((1, D), F32)),
        scratch_shapes=[pltpu.VMEM((tm, _Z_SPLITS[-1]), BF16)],
        compiler_params=_params("arbitrary"),
    )(*_hbm(dy, x, gain, *dz, w_in_t))


def _win_grad(dz, hn, name, tk):
    T, D = hn.shape
    nz = len(dz)
    nt = T // tk
    W = _Z_SPLITS[-1]

    def body(hn_ref, *rest):
        dz_refs, dw_ref, dw_sc, kv_sc = rest[:nz], rest[nz], rest[nz + 1], rest[nz + 2]
        i = pl.program_id(0)

        @pl.when(i == 0)
        def _():
            dw_sc[...] = jnp.zeros_like(dw_sc)

        hn = hn_ref[...]
        kv_sc[:, :KV_WIDTH] = dz_refs[1][...]
        kv_sc[:, KV_WIDTH:] = dz_refs[2][...]
        dw_sc[_Z_SPLITS[1]:_Z_SPLITS[3], :] += _dot_tn(kv_sc[...], hn)
        for n in (0, 3, 4, 5):
            dw_sc[_Z_SPLITS[n]:_Z_SPLITS[n + 1], :] += _dot_tn(dz_refs[n][...], hn)

        @pl.when(i == nt - 1)
        def _():
            dw_ref[...] = dw_sc[...].astype(BF16)

    row = lambda i: (i, 0)
    return pl.pallas_call(
        body, name=name, grid=(nt,),
        in_specs=[pl.BlockSpec((tk, D), row)] + [pl.BlockSpec((tk, a.shape[1]), row) for a in dz],
        out_specs=pl.BlockSpec((W, D), lambda i: (0, 0)),
        out_shape=pltpu.HBM((W, D), BF16),
        scratch_shapes=[pltpu.VMEM((W, D), F32), pltpu.VMEM((tk, 2 * KV_WIDTH), BF16)],
        compiler_params=_params("arbitrary"),
    )(*_hbm(hn, *dz))


def _adamw(parts, w, m, v, name, tr):
    P = parts.shape[0]
    R, C = w.shape

    def body(p_ref, w_ref, m_ref, v_ref, g_ref, d_ref, nm_ref, nv_ref):
        g = p_ref[0].astype(F32)
        for d in range(1, P):
            g = g + p_ref[d].astype(F32)
        nm = ADAM_B1 * m_ref[...] + (1.0 - ADAM_B1) * g
        nv = ADAM_B2 * v_ref[...] + (1.0 - ADAM_B2) * (g * g)
        m_hat = nm / (1.0 - ADAM_B1 ** ADAM_STEP)
        v_hat = nv / (1.0 - ADAM_B2 ** ADAM_STEP)
        g_ref[...] = g
        nm_ref[...] = nm
        nv_ref[...] = nv
        d_ref[...] = -ADAM_LR * (m_hat / (jnp.sqrt(v_hat) + ADAM_EPS) + ADAM_WD * w_ref[...])

    row = lambda i: (i, 0)
    spec = pl.BlockSpec((tr, C), row)
    shp = pltpu.HBM((R, C), F32)
    return pl.pallas_call(
        body, name=name, grid=(R // tr,),
        in_specs=[pl.BlockSpec((P, tr, C), lambda i: (0, i, 0)), spec, spec, spec],
        out_specs=(spec, spec, spec, spec),
        out_shape=(shp, shp, shp, shp),
        compiler_params=_params("arbitrary"),
    )(*_hbm(parts, w, m, v))


def _columns_of_blocks(g):
    n, R, w = g.shape
    return g.transpose(1, 0, 2).reshape(R, n * w)


def _pad_row(vec):
    vec = vec.reshape(1, -1)
    return jnp.pad(vec, ((0, 0), (0, PACK_COLS - vec.shape[1])))


def kernel(x, rel_bias_table, ffn1_norm, ffn1_w_gate, ffn1_w_up, ffn1_w_down, mix_norm, w_in, conv_w, attn_sinks, attn_out_norm, conv_out_norm, w_out, ffn2_norm, ffn2_w_gate, ffn2_w_up, ffn2_w_down, final_norm, loss_target, m_rel_bias_table, m_ffn1_norm, m_ffn1_w_gate, m_ffn1_w_up, m_ffn1_w_down, m_mix_norm, m_w_in, m_conv_w, m_attn_sinks, m_attn_out_norm, m_conv_out_norm, m_w_out, m_ffn2_norm, m_ffn2_w_gate, m_ffn2_w_up, m_ffn2_w_down, m_final_norm, v_rel_bias_table, v_ffn1_norm, v_ffn1_w_gate, v_ffn1_w_up, v_ffn1_w_down, v_mix_norm, v_w_in, v_conv_w, v_attn_sinks, v_attn_out_norm, v_conv_out_norm, v_w_out, v_ffn2_norm, v_ffn2_w_gate, v_ffn2_w_up, v_ffn2_w_down, v_final_norm):
    T, D = x.shape[1], x.shape[2]
    x0 = x[0]
    target = loss_target[0]
    tm = min(TM_FFN, T)
    tm_bwd = min(TM_FFN_BWD, T)
    tm_mix = min(TM_MIX, T)
    tm_wide = min(TM_MIX_WIDE, T)
    tk = min(TK_WGRAD, T)
    tf = TF_FFN
    nblk = min(ATTN_BLOCKS, T // BLOCK)
    me = 4 * lax.axis_index("x") + 2 * lax.axis_index("y") + lax.axis_index("c")

    big = {
        "ffn1_w_gate": (ffn1_w_gate[0], m_ffn1_w_gate[0], v_ffn1_w_gate[0], True),
        "ffn1_w_up": (ffn1_w_up[0], m_ffn1_w_up[0], v_ffn1_w_up[0], True),
        "ffn1_w_down": (ffn1_w_down[0], m_ffn1_w_down[0], v_ffn1_w_down[0], False),
        "w_in": (w_in[0], m_w_in[0], v_w_in[0], True),
        "w_out": (w_out[0], m_w_out[0], v_w_out[0], False),
        "ffn2_w_gate": (ffn2_w_gate[0], m_ffn2_w_gate[0], v_ffn2_w_gate[0], True),
        "ffn2_w_up": (ffn2_w_up[0], m_ffn2_w_up[0], v_ffn2_w_up[0], True),
        "ffn2_w_down": (ffn2_w_down[0], m_ffn2_w_down[0], v_ffn2_w_down[0], False),
    }

    def block_to_send(name):
        w, _, _, transposed = big[name]
        return (w.T if transposed else w).astype(BF16)

    names1 = ["ffn1_w_gate", "ffn1_w_up", "ffn1_w_down"]
    names_rest = ["w_in", "w_out", "ffn2_w_gate", "ffn2_w_up", "ffn2_w_down"]
    first = _gather_two_level([block_to_send(n) for n in names1], "gather_ffn1", ffn1_norm)
    wgt1, wut1, wd1 = [g.reshape(-1, D) for g in first]
    h_rest, token = _exchange_start([block_to_send(n) for n in names_rest[:2]] + [conv_w[0]]
                                    + [block_to_send(n) for n in names_rest[2:]], False,
                                    "gather_start_rest", wd1)

    wgu1 = _stack_gate_up(wgt1, wut1, tf)
    x1, xn1, gu1 = _ffn_fwd(x0, ffn1_norm, wgu1, wd1, token, "ffn1_fwd", tm, tf)
    mixw = _exchange_wait(h_rest[:3], False, "gather_wait_mix", x1)
    win_t = mixw[0].reshape(-1, D)
    wout = mixw[1].reshape(-1, D)
    cw = _columns_of_blocks(mixw[2])
    hn, q, k, v, u, gb, gc = _mixin_fwd(x1, mix_norm, win_t, "mixin_fwd", tm_wide)
    bucket = jnp.asarray(_bucket_table().T.copy())
    sinks = attn_sinks.reshape(-1)
    bias_t = _bias_build(rel_bias_table, bucket, "bias_build")
    attn_raw, attn_n = _attn_fwd(q, k, v, bias_t, sinks, attn_out_norm, "attn_fwd", nblk)
    x2, conv_n = _mixout_fwd(x1, attn_n, u, gb, gc, cw, conv_out_norm, wout, "mixout_fwd", tm_mix)
    wgt2, wut2, wd2 = [g.reshape(-1, D) for g in _exchange_wait(h_rest[3:], False, "gather_wait_ffn2", x2)]
    wgu2 = _stack_gate_up(wgt2, wut2, tf)
    dx3, xn2, gu2, d_final, loss_part = _ffn_fwd(x2, ffn2_norm, wgu2, wd2, x2, "ffn2_fwd", tm, tf,
                                                 head=(target, final_norm.reshape(1, D)))

    def blocks(g):
        return g.reshape(N_DEV, -1, D)

    dx2, d_ffn2_norm, dg2, du2, hh2, do2 = _ffn_bwd(
        dx3, x2, ffn2_norm, gu2, wgu2, wd2, dx3, "ffn2_bwd", tm_bwd, tf)
    d_wg2 = _tn_grad(dg2, xn2, dx2, "ffn2_wgrad_gate", BM_WGRAD, tk)
    d_wu2 = _tn_grad(du2, xn2, dx2, "ffn2_wgrad_up", BM_WGRAD, tk)
    d_wd2 = _tn_grad(hh2, do2, dx2, "ffn2_wgrad_down", BM_WGRAD, tk)
    handles2, token2 = _exchange_start([blocks(d_wg2), blocks(d_wu2), blocks(d_wd2)], True,
                                       "grads_start_ffn2", dx2)

    dmixed, d_wout = _mixout_bwd(dx2, attn_n, conv_n, wout, token2, "mixout_bwd", tm_wide)
    dq, dk, dv, dbias, dsink, d_attn_norm = _attn_bwd(
        dmixed, attn_raw, q, k, v, bias_t, sinks, attn_out_norm, "attn_bwd", nblk)
    du, dgb, dgc, d_cw, d_conv_norm = _conv_bwd(dmixed, u, gb, gc, cw, conv_out_norm, "conv_bwd", tm_wide)
    d_table = _bias_grad(dbias, bucket, "bias_grad")
    dz = [dq, dk, dv, du, dgb, dgc]
    dx1, d_mix_norm = _mixin_bwd(dx2, x1, mix_norm, dz, win_t, "mixin_bwd", tm_mix)
    d_win_t = _win_grad(dz, hn, "win_grad", tm_wide)
    handles_mix, token_mix = _exchange_start([blocks(d_win_t), blocks(d_wout)], True, "grads_start_mix", d_table)

    dx0, d_ffn1_norm, dg1, du1, hh1, do1 = _ffn_bwd(
        dx1, x0, ffn1_norm, gu1, wgu1, wd1, token_mix, "ffn1_bwd", tm_bwd, tf)
    def pack(ffn1, mixn, ffn2, fin, attn_n_, conv_n_, sink_, extra, convw, table):
        rows = [_pad_row(ffn1), _pad_row(mixn), _pad_row(ffn2), _pad_row(fin),
                _pad_row(jnp.concatenate([attn_n_.reshape(-1), conv_n_.reshape(-1)])),
                _pad_row(sink_), _pad_row(extra),
                jnp.zeros((1, PACK_COLS), F32),
                jnp.pad(convw, ((0, 0), (0, PACK_COLS - convw.shape[1]))),
                _pad_row(table),
                jnp.zeros((PACK_ROWS - 12, PACK_COLS), F32)]
        return jnp.concatenate(rows, axis=0)

    def own_channels(a):
        full = jnp.zeros((a.shape[1], CONV_DIM), F32)
        return lax.dynamic_update_slice(full, a[0], (0, me * a.shape[2]))

    g_pack = pack(d_ffn1_norm, d_mix_norm, d_ffn2_norm, d_final, d_attn_norm, d_conv_norm,
                  dsink[:, 0], loss_part[0, :1], d_cw[:3], d_table[:, :N_Q_HEADS])
    zero1 = jnp.zeros((1,), F32)
    w_pack = pack(ffn1_norm, mix_norm, ffn2_norm, final_norm, attn_out_norm, conv_out_norm,
                  attn_sinks, zero1, own_channels(conv_w), rel_bias_table)
    m_pack = pack(m_ffn1_norm, m_mix_norm, m_ffn2_norm, m_final_norm, m_attn_out_norm, m_conv_out_norm,
                  m_attn_sinks, zero1, own_channels(m_conv_w), m_rel_bias_table)
    v_pack = pack(v_ffn1_norm, v_mix_norm, v_ffn2_norm, v_final_norm, v_attn_out_norm, v_conv_out_norm,
                  v_attn_sinks, zero1, own_channels(v_conv_w), v_rel_bias_table)
    h_small, token_small = _exchange_start([g_pack], False, "gather_small_start", dx0)

    d_wd1 = _tn_grad(hh1, do1, token_small, "ffn1_wgrad_down", BM_WGRAD, tk)
    handles1_d, token1 = _exchange_start([blocks(d_wd1)], True, "grads_start_ffn1_down", dx0)
    d_wg1 = _tn_grad(dg1, xn1, token1, "ffn1_wgrad_gate", BM_WGRAD, tk)
    handles1_g, token1 = _exchange_start([blocks(d_wg1)], True, "grads_start_ffn1_gate", token1)
    d_wu1 = _tn_grad(du1, xn1, token1, "ffn1_wgrad_up", BM_WGRAD, tk)
    handles1_u, token1 = _exchange_start([blocks(d_wu1)], True, "grads_start_ffn1_up", token1)

    res = {}

    def update(names, parts):
        last = None
        for name, p in zip(names, parts):
            w, m_, v_, transposed = big[name]
            if transposed:
                w, m_, v_ = w.T, m_.T, v_.T
            new = _adamw(p, w, m_, v_, "adamw_" + name, _row_tile(w.shape[0], ADAM_ROWS))
            res[name] = tuple((a.T if transposed else a)[None] for a in new)
            last = new[0]
        return last

    parts2 = _exchange_wait(handles2, True, "grads_wait_ffn2", token1)
    done2 = update(["ffn2_w_gate", "ffn2_w_up", "ffn2_w_down"], parts2)
    parts_mix = _exchange_wait(handles_mix, True, "grads_wait_mix", done2)
    done_mix = update(["w_in", "w_out"], parts_mix)
    (g_all,) = _exchange_wait(h_small, False, "gather_small_wait", done_mix)
    packs = _adamw(g_all, w_pack, m_pack, v_pack, "adamw_small", PACK_ROWS)

    parts1_d = _exchange_wait(handles1_d, True, "grads_wait_ffn1_down", packs[0])
    done1 = update(["ffn1_w_down"], parts1_d)
    parts1_g = _exchange_wait(handles1_g, True, "grads_wait_ffn1_gate", done1)
    done1 = update(["ffn1_w_gate"], parts1_g)
    parts1_u = _exchange_wait(handles1_u, True, "grads_wait_ffn1_up", done1)
    update(["ffn1_w_up"], parts1_u)

    def unpack(pk):
        cwb = lax.dynamic_slice(pk[8:11, :CONV_DIM], (0, me * conv_w.shape[2]), (3, conv_w.shape[2]))
        return {
            "ffn1_norm": pk[0:1, :D], "mix_norm": pk[1:2, :D], "ffn2_norm": pk[2:3, :D],
            "final_norm": pk[3, :D],
            "attn_out_norm": pk[4:5, :ATTN_WIDTH], "conv_out_norm": pk[4:5, ATTN_WIDTH:ATTN_WIDTH + CONV_DIM],
            "attn_sinks": pk[5:6, :N_Q_HEADS],
            "conv_w": cwb[None],
            "rel_bias_table": pk[11, :NUM_BUCKETS * N_Q_HEADS].reshape(NUM_BUCKETS, N_Q_HEADS),
        }

    small = [unpack(pk) for pk in packs]
    loss = packs[0][6, 0]

    order = ["rel_bias_table", "ffn1_norm", "ffn1_w_gate", "ffn1_w_up", "ffn1_w_down", "mix_norm", "w_in",
             "conv_w", "attn_sinks", "attn_out_norm", "conv_out_norm", "w_out", "ffn2_norm",
             "ffn2_w_gate", "ffn2_w_up", "ffn2_w_down", "final_norm"]
    outs = [loss, dx0[None]]
    for kind in range(4):
        for name in order:
            outs.append(res[name][kind] if name in res else small[kind][name])
    return tuple(outs)
```

```python
import math

import numpy as np
import jax
import jax.numpy as jnp
from jax import lax
from jax.experimental import pallas as pl
from jax.experimental.pallas import tpu as pltpu

F32 = jnp.float32
BF16 = jnp.bfloat16

N_DEV = 8
EPS = 1e-6
HEAD_DIM = 64
N_Q_HEADS = 8
N_KV_HEADS = 2
GQA_GROUP = 4
ATTN_WIDTH = 512
KV_WIDTH = 128
CONV_DIM = 512
BLOCK = 128
WINDOW = 128
NUM_BUCKETS = 32
MAX_DISTANCE = 128
SCALE = HEAD_DIM ** -0.5
MASKED = -1e30
GROUP_ROWS = GQA_GROUP * BLOCK

ADAM_LR = 0.001
ADAM_B1 = 0.9
ADAM_B2 = 0.999
ADAM_EPS = 1e-08
ADAM_WD = 0.01
ADAM_STEP = 10

VMEM_LIMIT_BYTES = 40 * 1024 * 1024
VMEM_LIMIT_FFN_BYTES = 50 * 1024 * 1024
SUBLANES = 8
LANES = 128
PACK_ROWS = 16
PACK_COLS = 1024

TM_FFN = 1024
TM_FFN_BWD = 1024
TM_MIX = 512
TM_MIX_WIDE = 1024
TK_WGRAD = 2048
TF_FFN = 256
BM_WGRAD = 1408
ROW_GROUPS = 4
ATTN_BLOCKS = 8
ADAM_ROWS = 256


def _row_tile(rows, limit):
    best = rows
    for t in range(16, min(rows, limit) + 1, 16):
        if rows % t == 0:
            best = t
    return best


def _params(*sem):
    return pltpu.CompilerParams(dimension_semantics=sem, vmem_limit_bytes=VMEM_LIMIT_BYTES)


def _hbm(*arrays):
    return [pltpu.with_memory_space_constraint(a, pltpu.HBM) for a in arrays]


def _dot(a, b):
    return jnp.dot(a, b, preferred_element_type=F32)


def _dot_nt(a, b):
    return lax.dot_general(a, b, (((1,), (1,)), ((), ())), preferred_element_type=F32)


def _dot_tn(a, b):
    return lax.dot_general(a, b, (((0,), (0,)), ((), ())), preferred_element_type=F32)


def _sigmoid(g):
    return 0.5 * jnp.tanh(0.5 * g) + 0.5


def _rms_stats(x):
    inv = lax.rsqrt(jnp.mean(x * x, axis=-1, keepdims=True) + EPS)
    return inv, x * inv


def _rms_bwd(dy, x, gain):
    inv, xhat = _rms_stats(x)
    dgain = jnp.sum(dy * xhat, axis=0, keepdims=True)
    dxh = dy * gain
    dx = inv * (dxh - xhat * jnp.mean(dxh * xhat, axis=-1, keepdims=True))
    return dx, dgain


def _peer_list():
    x, y, c = lax.axis_index("x"), lax.axis_index("y"), lax.axis_index("c")
    peers = []
    for k in range(1, N_DEV):
        px = 1 - x if (k >> 2) & 1 else x
        py = 1 - y if (k >> 1) & 1 else y
        pc = 1 - c if k & 1 else c
        peers.append((px, py, pc))
    return 4 * x + 2 * y + c, peers


def _gather_two_level(arrs, name, after):
    n = len(arrs)
    out_shape = tuple(jax.ShapeDtypeStruct((N_DEV,) + a.shape, a.dtype) for a in arrs)

    def body(*refs):
        ins, outs = refs[:n], refs[n + 1:2 * n + 1]
        send_sems, recv_sems, local_sems = refs[2 * n + 1:]
        x, y, c = lax.axis_index("x"), lax.axis_index("y"), lax.axis_index("c")
        me, sibling = (x, y, c), (x, y, 1 - c)
        chips = [(1 - x, y), (x, 1 - y), (1 - x, 1 - y)]

        def copy(a, k, block, to, src=None):
            slot = outs[a].at[4 * block[0] + 2 * block[1] + block[2]]
            return pltpu.make_async_remote_copy(
                src_ref=slot if src is None else src, dst_ref=slot,
                send_sem=send_sems.at[a, k], recv_sem=recv_sems.at[a, k],
                device_id=to, device_id_type=pl.DeviceIdType.MESH)

        started = []
        for a in range(n):
            loc = pltpu.make_async_copy(ins[a], outs[a].at[4 * x + 2 * y + c], local_sems.at[a])
            loc.start()
            started.append(loc)
        sends = []
        for a in range(n):
            sends.append(copy(a, 0, me, sibling, src=ins[a]))
            sends += [copy(a, 1 + j, me, (*chip, c), src=ins[a]) for j, chip in enumerate(chips)]
        for cp in sends:
            cp.start()
        for j, chip in enumerate(chips):
            for a in range(n):
                copy(a, 1 + j, (*chip, c), me).wait_recv()
                fwd = copy(a, 4 + j, (*chip, c), sibling)
                fwd.start()
                sends.append(fwd)
        for a in range(n):
            copy(a, 0, sibling, me).wait_recv()
            for j, chip in enumerate(chips):
                copy(a, 4 + j, (*chip, 1 - c), me).wait_recv()
        for cp in sends:
            cp.wait_send()
        for loc in started:
            loc.wait()

    hbm = pl.BlockSpec(memory_space=pl.ANY)
    return pl.pallas_call(
        body, name=name, out_shape=out_shape,
        in_specs=[hbm] * (n + 1), out_specs=tuple([hbm] * n),
        scratch_shapes=[pltpu.SemaphoreType.DMA((n, N_DEV - 1)),
                        pltpu.SemaphoreType.DMA((n, N_DEV - 1)),
                        pltpu.SemaphoreType.DMA((n,))],
    )(*arrs, after)


_HBM = pl.BlockSpec(memory_space=pltpu.HBM)
_SEM = pl.BlockSpec(memory_space=pltpu.SEMAPHORE)
_EFFECT = pltpu.SideEffectType.DATAFLOW_SIDE_EFFECTING


def _split_copies(srcs, lands, send_sems, recv_sems, scatter):
    me, peers = _peer_list()
    copies = []
    for a in range(len(srcs)):
        for k, (px, py, pc) in enumerate(peers):
            src = srcs[a].at[4 * px + 2 * py + pc] if scatter else srcs[a]
            copies.append(pltpu.make_async_remote_copy(
                src_ref=src, dst_ref=lands[a].at[me],
                send_sem=send_sems[a].at[k], recv_sem=recv_sems[a].at[k],
                device_id=(px, py, pc), device_id_type=pl.DeviceIdType.MESH))
    return copies


def _own_copies(srcs, lands, recv_sems, scatter):
    me, _ = _peer_list()
    return [pltpu.make_async_copy(srcs[a].at[me] if scatter else srcs[a], lands[a].at[me],
                                  recv_sems[a].at[N_DEV - 1]) for a in range(len(srcs))]


def _exchange_start(arrs, scatter, name, after):
    n = len(arrs)
    lands = [lax.empty(a.shape if scatter else (N_DEV,) + a.shape, a.dtype) for a in arrs]

    def body(*refs):
        srcs, lnds = refs[:n], refs[n:2 * n]
        outs = refs[2 * n + 1:]
        send_sems, recv_sems = outs[:n], outs[n:2 * n]
        token = outs[4 * n]
        for cp in _split_copies(srcs, lnds, send_sems, recv_sems, scatter):
            cp.start()
        for cp in _own_copies(srcs, lnds, recv_sems, scatter):
            cp.start()
        token[...] = jnp.zeros_like(token)

    sem = pltpu.SemaphoreType.DMA((N_DEV,))
    out_shape = ([sem] * (2 * n) + [pltpu.HBM(a.shape, a.dtype) for a in arrs]
                 + [pltpu.HBM(l.shape, l.dtype) for l in lands] + [jax.ShapeDtypeStruct((SUBLANES, LANES), F32)])
    res = pl.pallas_call(
        body, name=name, out_shape=tuple(out_shape),
        in_specs=[_HBM] * (2 * n) + [pl.BlockSpec(memory_space=pl.ANY)],
        out_specs=tuple([_SEM] * (2 * n) + [_HBM] * (2 * n) + [pl.BlockSpec(memory_space=pltpu.VMEM)]),
        input_output_aliases={i: 2 * n + i for i in range(2 * n)},
        compiler_params=pltpu.CompilerParams(has_side_effects=_EFFECT),
    )(*[pltpu.with_memory_space_constraint(a, pltpu.HBM) for a in arrs],
      *[pltpu.with_memory_space_constraint(l, pltpu.HBM) for l in lands], after)
    handles = [(res[2 * n + a], res[3 * n + a], res[a], res[n + a]) for a in range(n)]
    return handles, res[4 * n]


def _exchange_wait(handles, scatter, name, after):
    n = len(handles)
    after = list(after) if isinstance(after, (list, tuple)) else [after]

    def body(*refs):
        srcs, lnds = refs[:n], refs[n:2 * n]
        send_sems, recv_sems = refs[2 * n:3 * n], refs[3 * n:4 * n]
        for cp in _split_copies(srcs, lnds, send_sems, recv_sems, scatter):
            cp.wait_send()
            cp.wait_recv()
        for cp in _own_copies(srcs, lnds, recv_sems, scatter):
            cp.wait()

    srcs = [h[0] for h in handles]
    lands = [h[1] for h in handles]
    res = pl.pallas_call(
        body, name=name,
        out_shape=tuple([pltpu.HBM(a.shape, a.dtype) for a in srcs] + [pltpu.HBM(l.shape, l.dtype) for l in lands]),
        in_specs=[_HBM] * (2 * n) + [_SEM] * (2 * n) + [pl.BlockSpec(memory_space=pl.ANY)] * len(after),
        out_specs=tuple([_HBM] * (2 * n)),
        input_output_aliases={i: i for i in range(2 * n)},
        compiler_params=pltpu.CompilerParams(has_side_effects=_EFFECT),
    )(*srcs, *lands, *[h[2] for h in handles], *[h[3] for h in handles], *after)
    return list(res[n:])


def _row_groups(tm):
    return [slice(r * (tm // ROW_GROUPS), (r + 1) * (tm // ROW_GROUPS)) for r in range(ROW_GROUPS)]


def _stack_gate_up(wgt, wut, tf):
    F, D = wgt.shape
    return jnp.stack([wgt.reshape(F // tf, tf, D), wut.reshape(F // tf, tf, D)], axis=1).reshape(2 * F, D)


def _ffn_fwd(x, gain, wgu, wd, after, name, tm, tf, head=None):
    T, D = x.shape
    F = wd.shape[0]
    nj = F // tf
    n_in = 7 if head else 5

    def body(*refs):
        x_ref, g_ref, wgu_ref, wd_ref, after_ref = refs[:5]
        outs = refs[n_in:]
        xo_ref, xn_ref, gu_ref = outs[:3]
        xn_sc, acc_sc = outs[-2:]
        i, j = pl.program_id(0), pl.program_id(1)

        @pl.when(j == 0)
        def _():
            _, xhat = _rms_stats(x_ref[...])
            xn = (xhat * g_ref[...]).astype(BF16)
            xn_sc[...] = xn
            xn_ref[...] = xn
            acc_sc[...] = jnp.zeros_like(acc_sc)

        groups = _row_groups(tm)
        gus = [_dot_nt(xn_sc[rows, :], wgu_ref[...]) for rows in groups]
        hs = []
        for rows, gu in zip(groups, gus):
            gu_ref[rows, :] = gu.astype(BF16)
            g, u = gu[:, :tf], gu[:, tf:]
            hs.append((g * _sigmoid(g) * u).astype(BF16))
        for rows, h in zip(groups, hs):
            acc_sc[rows, :] += _dot(h, wd_ref[...])

        if head:
            t_ref, fg_ref = refs[5:7]
            dgain_ref, loss_ref = outs[3:5]

            @pl.when((i == 0) & (j == 0))
            def _():
                dgain_ref[...] = jnp.zeros_like(dgain_ref)
                loss_ref[...] = jnp.zeros_like(loss_ref)

        @pl.when(j == nj - 1)
        def _():
            for rows in groups:
                xo = x_ref[rows, :] + 0.5 * acc_sc[rows, :]
                if head:
                    fg = fg_ref[...]
                    _, xhat = _rms_stats(xo)
                    err = xhat * fg - t_ref[rows, :]
                    loss_ref[...] += 0.5 * jnp.sum(jnp.mean(err * err, axis=-1, keepdims=True),
                                                   axis=0, keepdims=True)
                    dx, dgain = _rms_bwd(err * (1.0 / D), xo, fg)
                    dgain_ref[...] += dgain
                    xo_ref[rows, :] = dx
                else:
                    xo_ref[rows, :] = xo

    tile = pl.BlockSpec((tm, D), lambda i, j: (i, 0))
    const = pl.BlockSpec((1, D), lambda i, j: (0, 0))
    in_specs = [tile, const, pl.BlockSpec((2 * tf, D), lambda i, j: (j, 0)),
                pl.BlockSpec((tf, D), lambda i, j: (j, 0)), pl.BlockSpec(memory_space=pl.ANY)]
    out_specs = [tile, pl.BlockSpec((tm, D), lambda i, j: (i, 0)), pl.BlockSpec((tm, 2 * tf), lambda i, j: (i, j))]
    out_shape = [pltpu.HBM((T,D), F32), pltpu.HBM((T,D), BF16),
                 pltpu.HBM((T,2 * F), BF16)]
    operands = [x, gain, wgu, wd, after]
    if head:
        in_specs += [tile, const]
        out_specs += [const, pl.BlockSpec((SUBLANES, LANES), lambda i, j: (0, 0))]
        out_shape += [jax.ShapeDtypeStruct((1, D), F32), jax.ShapeDtypeStruct((SUBLANES, LANES), F32)]
        operands += list(head)
    return pl.pallas_call(
        body, name=name, grid=(T // tm, nj),
        in_specs=in_specs, out_specs=tuple(out_specs), out_shape=tuple(out_shape),
        scratch_shapes=[pltpu.VMEM((tm, D), BF16), pltpu.VMEM((tm, D), F32)],
        compiler_params=pltpu.CompilerParams(dimension_semantics=("arbitrary", "arbitrary"),
                                             vmem_limit_bytes=VMEM_LIMIT_FFN_BYTES),
    )(*_hbm(*operands))


def _ffn_bwd(dy, x, gain, gu, wgu, wd, after, name, tm, tf):
    T, D = x.shape
    F = wd.shape[0]
    nj = F // tf

    def body(dy_ref, x_ref, g_ref, gu_ref, wgu_ref, wd_ref, after_ref,
             dx_ref, dgain_ref, dg_ref, du_ref, hh_ref, do_ref, do_sc, acc_sc, dgu_sc):
        i, j = pl.program_id(0), pl.program_id(1)

        @pl.when((i == 0) & (j == 0))
        def _():
            dgain_ref[...] = jnp.zeros_like(dgain_ref)

        @pl.when(j == 0)
        def _():
            do = (0.5 * dy_ref[...]).astype(BF16)
            do_sc[...] = do
            do_ref[...] = do
            acc_sc[...] = jnp.zeros_like(acc_sc)

        groups = _row_groups(tm)
        dhs = [_dot_nt(do_sc[rows, :], wd_ref[...]) for rows in groups]
        for rows, dh in zip(groups, dhs):
            g = gu_ref[rows, :tf].astype(F32)
            u = gu_ref[rows, tf:].astype(F32)
            sig = _sigmoid(g)
            s = g * sig
            dg = (dh * u * (sig + s * (1.0 - sig))).astype(BF16)
            du = (dh * s).astype(BF16)
            dg_ref[rows, :] = dg
            du_ref[rows, :] = du
            dgu_sc[rows, :tf] = dg
            dgu_sc[rows, tf:] = du
            hh_ref[rows, :] = (s * u).astype(BF16)
        for rows in groups:
            acc_sc[rows, :] += _dot(dgu_sc[rows, :], wgu_ref[...])

        @pl.when(j == nj - 1)
        def _():
            for rows in groups:
                dx, dgain = _rms_bwd(acc_sc[rows, :], x_ref[rows, :], g_ref[...])
                dgain_ref[...] += dgain
                dx_ref[rows, :] = dy_ref[rows, :] + dx

    tile_in = pl.BlockSpec((tm, D), lambda i, j: (i, 0))
    tile = pl.BlockSpec((tm, D), lambda i, j: (i, 0), pipeline_mode=pl.Buffered(1))
    return pl.pallas_call(
        body, name=name, grid=(T // tm, nj),
        in_specs=[tile_in, tile_in,
                  pl.BlockSpec((1, D), lambda i, j: (0, 0)),
                  pl.BlockSpec((tm, 2 * tf), lambda i, j: (i, j)),
                  pl.BlockSpec((2 * tf, D), lambda i, j: (j, 0)),
                  pl.BlockSpec((tf, D), lambda i, j: (j, 0)),
                  pl.BlockSpec(memory_space=pl.ANY)],
        out_specs=(tile,
                   pl.BlockSpec((1, D), lambda i, j: (0, 0)),
                   pl.BlockSpec((tm, tf), lambda i, j: (i, j)),
                   pl.BlockSpec((tm, tf), lambda i, j: (i, j)),
                   pl.BlockSpec((tm, tf), lambda i, j: (i, j)),
                   pl.BlockSpec((tm, D), lambda i, j: (i, 0))),
        out_shape=(pltpu.HBM((T,D), F32), jax.ShapeDtypeStruct((1, D), F32),
                   pltpu.HBM((T,F), BF16), pltpu.HBM((T,F), BF16),
                   pltpu.HBM((T,F), BF16), pltpu.HBM((T,D), BF16)),
        scratch_shapes=[pltpu.VMEM((tm, D), BF16), pltpu.VMEM((tm, D), F32), pltpu.VMEM((tm, 2 * tf), BF16)],
        compiler_params=pltpu.CompilerParams(dimension_semantics=("arbitrary", "arbitrary"),
                                             vmem_limit_bytes=VMEM_LIMIT_FFN_BYTES),
    )(*_hbm(dy, x, gain, gu, wgu, wd, after))


def _tn_grad(a, b, after, name, bm, tk):
    T, M = a.shape
    D = b.shape[1]
    nk = T // tk

    def body(a_ref, b_ref, after_ref, o_ref, acc_sc):
        k = pl.program_id(1)

        @pl.when(k == 0)
        def _():
            acc_sc[...] = jnp.zeros_like(acc_sc)

        acc_sc[...] += _dot_tn(a_ref[...], b_ref[...])

        @pl.when(k == nk - 1)
        def _():
            o_ref[...] = acc_sc[...].astype(BF16)

    return pl.pallas_call(
        body, name=name, grid=(M // bm, nk),
        in_specs=[pl.BlockSpec((tk, bm), lambda i, k: (k, i)), pl.BlockSpec((tk, D), lambda i, k: (k, 0)),
                  pl.BlockSpec(memory_space=pl.ANY)],
        out_specs=pl.BlockSpec((bm, D), lambda i, k: (i, 0)),
        out_shape=pltpu.HBM((M, D), BF16),
        scratch_shapes=[pltpu.VMEM((bm, D), F32)],
        compiler_params=_params("arbitrary", "arbitrary"),
    )(*_hbm(a, b, after))


_Z_SPLITS = (0, 512, 640, 768, 1280, 1792, 2304)


def _mixin_fwd(x, gain, w_in_t, name, tm):
    T, D = x.shape
    widths = [b - a for a, b in zip(_Z_SPLITS[:-1], _Z_SPLITS[1:])]

    def body(x_ref, g_ref, w_ref, hn_ref, *outs):
        _, xhat = _rms_stats(x_ref[...])
        hn = (xhat * g_ref[...]).astype(BF16)
        hn_ref[...] = hn
        for n in (0, 3, 4, 5):
            outs[n][...] = _dot_nt(hn, w_ref[_Z_SPLITS[n]:_Z_SPLITS[n + 1], :]).astype(BF16)
        kv = _dot_nt(hn, w_ref[_Z_SPLITS[1]:_Z_SPLITS[3], :]).astype(BF16)
        outs[1][...] = kv[:, :KV_WIDTH]
        outs[2][...] = kv[:, KV_WIDTH:]

    return pl.pallas_call(
        body, name=name, grid=(T // tm,),
        in_specs=[pl.BlockSpec((tm, D), lambda i: (i, 0)),
                  pl.BlockSpec((1, D), lambda i: (0, 0)),
                  pl.BlockSpec(w_in_t.shape, lambda i: (0, 0))],
        out_specs=tuple([pl.BlockSpec((tm, D), lambda i: (i, 0))]
                        + [pl.BlockSpec((tm, w), lambda i: (i, 0)) for w in widths]),
        out_shape=tuple([pltpu.HBM((T,D), BF16)]
                        + [pltpu.HBM((T,w), BF16) for w in widths]),
        compiler_params=_params("arbitrary"),
    )(*_hbm(x, gain, w_in_t))


def _bucket_table():
    qi = np.arange(BLOCK, dtype=np.int32)[:, None]
    kj = np.arange(2 * BLOCK, dtype=np.int32)[None, :]
    dist = qi + BLOCK - kj
    n = np.maximum(dist, 0)
    max_exact = NUM_BUCKETS // 2
    large = max_exact + (np.log(np.maximum(n, 1).astype(np.float32) / max_exact)
                         / math.log(MAX_DISTANCE / max_exact)
                         * (NUM_BUCKETS - max_exact)).astype(np.int32)
    large = np.minimum(large, NUM_BUCKETS - 1)
    bucket = np.where(n < max_exact, n, large).astype(np.int32)
    valid = (dist >= 0) & (dist < WINDOW)
    return np.where(valid, bucket, -1).astype(np.int32)


def _bias_build(table, bucket, name):
    def body(t_ref, b_ref, o_ref):
        bk = b_ref[...]
        for h in range(N_Q_HEADS):
            def step(b, acc):
                return jnp.where(bk == b, t_ref[b, h], acc)
            o_ref[h] = lax.fori_loop(0, NUM_BUCKETS, step, jnp.full(bk.shape, MASKED, F32))

    return pl.pallas_call(
        body, name=name,
        in_specs=[pl.BlockSpec(memory_space=pltpu.SMEM), pl.BlockSpec(memory_space=pltpu.VMEM)],
        out_specs=pl.BlockSpec(memory_space=pltpu.VMEM),
        out_shape=jax.ShapeDtypeStruct((N_Q_HEADS,) + bucket.shape, F32),
    )(table, bucket)


def _bias_grad(dbias, bucket, name):
    def body(d_ref, b_ref, o_ref):
        bk = b_ref[...]
        row = lax.broadcasted_iota(jnp.int32, o_ref.shape, 0)
        lane = lax.broadcasted_iota(jnp.int32, o_ref.shape, 1)
        res = jnp.zeros(o_ref.shape, F32)
        for h in range(N_Q_HEADS):
            d = d_ref[h]

            def step(b, acc):
                part = jnp.sum(jnp.where(bk == b, d, 0.0), axis=0, keepdims=True)
                return jnp.where(row == b, part, acc)
            per_lane = lax.fori_loop(0, NUM_BUCKETS, step, jnp.zeros(o_ref.shape, F32))
            res = jnp.where(lane == h, jnp.sum(per_lane, axis=1, keepdims=True), res)
        o_ref[...] = res

    return pl.pallas_call(
        body, name=name,
        in_specs=[pl.BlockSpec(memory_space=pltpu.VMEM), pl.BlockSpec(memory_space=pltpu.VMEM)],
        out_specs=pl.BlockSpec(memory_space=pltpu.VMEM),
        out_shape=jax.ShapeDtypeStruct((NUM_BUCKETS, LANES), F32),
    )(*_hbm(dbias, bucket))


def _head_cols(h):
    return slice(h * HEAD_DIM, (h + 1) * HEAD_DIM)


def _stack_heads(ref, r0, g, dtype):
    return jnp.concatenate(
        [ref[pl.ds(r0, BLOCK), _head_cols(GQA_GROUP * g + j)].astype(dtype) for j in range(GQA_GROUP)], axis=0)


def _unstack_heads(ref, r0, g, val):
    for j in range(GQA_GROUP):
        ref[pl.ds(r0, BLOCK), _head_cols(GQA_GROUP * g + j)] = val[j * BLOCK:(j + 1) * BLOCK, :]


def _head_lanes(h):
    return slice(h * BLOCK, (h + 1) * BLOCK)


def _group_lanes(g):
    return slice(g * GROUP_ROWS, (g + 1) * GROUP_ROWS)


def _head_softmax(st, bias_t, sink, no_prev):
    s = st * SCALE + bias_t
    row = lax.broadcasted_iota(jnp.int32, s.shape, 0)
    s = jnp.where(no_prev & (row < BLOCK), MASKED, s)
    m = jnp.maximum(jnp.max(s, axis=0, keepdims=True), sink)
    p = jnp.exp(s - m)
    ps = jnp.exp(sink - m)
    r = 1.0 / (jnp.sum(p, axis=0, keepdims=True) + ps)
    return p * r, ps * r


def _load_band(kf_sc, vf_sc, kp_ref, kc_ref, vp_ref, vc_ref, tq):
    kf_sc[0:BLOCK, :] = kp_ref[...]
    kf_sc[BLOCK:BLOCK + tq, :] = kc_ref[...]
    vf_sc[0:BLOCK, :] = vp_ref[...]
    vf_sc[BLOCK:BLOCK + tq, :] = vc_ref[...]


def _attn_fwd(q, k, v, bias_t, sinks, gain, name, nblk):
    T = q.shape[0]
    tq = nblk * BLOCK

    def body(sink_ref, q_ref, kc_ref, kp_ref, vc_ref, vp_ref, bias_ref, g_ref, raw_ref, nrm_ref,
             kf_sc, vf_sc, o_sc, st_sc, pt_sc):
        i = pl.program_id(0)
        _load_band(kf_sc, vf_sc, kp_ref, kc_ref, vp_ref, vc_ref, tq)

        def block(b, carry):
            r0 = pl.multiple_of(b * BLOCK, BLOCK)
            no_prev = (i == 0) & (b == 0)
            for g in range(N_KV_HEADS):
                kb = kf_sc[pl.ds(r0, 2 * BLOCK), _head_cols(g)]
                st_sc[:, _group_lanes(g)] = _dot_nt(kb, _stack_heads(q_ref, r0, g, BF16))
            for h in range(N_Q_HEADS):
                p, _ = _head_softmax(st_sc[:, _head_lanes(h)], bias_ref[h], sink_ref[h], no_prev)
                pt_sc[:, _head_lanes(h)] = p.astype(BF16)
            for g in range(N_KV_HEADS):
                vb = vf_sc[pl.ds(r0, 2 * BLOCK), _head_cols(g)]
                _unstack_heads(o_sc, r0, g, _dot_tn(pt_sc[:, _group_lanes(g)], vb))
            return carry

        lax.fori_loop(0, nblk, block, 0, unroll=True)
        o = o_sc[...]
        raw_ref[...] = o.astype(BF16)
        _, ohat = _rms_stats(o)
        nrm_ref[...] = (ohat * g_ref[...]).astype(BF16)

    cur = lambda i: (i, 0)
    prev = lambda i: (jnp.maximum(i * nblk - 1, 0), 0)
    lanes = N_Q_HEADS * BLOCK
    return pl.pallas_call(
        body, name=name, grid=(T // tq,),
        in_specs=[pl.BlockSpec(memory_space=pltpu.SMEM),
                  pl.BlockSpec((tq, ATTN_WIDTH), cur),
                  pl.BlockSpec((tq, KV_WIDTH), cur), pl.BlockSpec((BLOCK, KV_WIDTH), prev),
                  pl.BlockSpec((tq, KV_WIDTH), cur), pl.BlockSpec((BLOCK, KV_WIDTH), prev),
                  pl.BlockSpec(bias_t.shape, lambda i: (0, 0, 0)),
                  pl.BlockSpec((1, ATTN_WIDTH), lambda i: (0, 0))],
        out_specs=(pl.BlockSpec((tq, ATTN_WIDTH), cur), pl.BlockSpec((tq, ATTN_WIDTH), cur)),
        out_shape=(pltpu.HBM((T,ATTN_WIDTH), BF16), pltpu.HBM((T,ATTN_WIDTH), BF16)),
        scratch_shapes=[pltpu.VMEM((tq + BLOCK, KV_WIDTH), BF16), pltpu.VMEM((tq + BLOCK, KV_WIDTH), BF16),
                        pltpu.VMEM((tq, ATTN_WIDTH), F32),
                        pltpu.VMEM((2 * BLOCK, lanes), F32), pltpu.VMEM((2 * BLOCK, lanes), BF16)],
        compiler_params=_params("arbitrary"),
    )(sinks, *_hbm(q, k, k, v, v, bias_t, gain))


def _attn_bwd(dmixed, raw, q, k, v, bias_t, sinks, gain, name, nblk):
    T = q.shape[0]
    tq = nblk * BLOCK
    nt = T // tq
    lanes = N_Q_HEADS * BLOCK

    def body(sink_ref, dm_ref, raw_ref, q_ref, kc_ref, kp_ref, vc_ref, vp_ref, bias_ref, g_ref,
             dq_ref, dk_ref, dv_ref, dbias_ref, dsink_ref, dgain_ref,
             do_sc, dq_sc, kf_sc, vf_sc, dkf_sc, dvf_sc, st_sc, dpt_sc, pt_sc, dst_sc, drow_sc,
             qs_sc, dos_sc, dsink_sc):
        i = pl.program_id(0)
        tile = nt - 1 - i

        @pl.when(i == 0)
        def _():
            dkf_sc[...] = jnp.zeros_like(dkf_sc)
            dvf_sc[...] = jnp.zeros_like(dvf_sc)
            dsink_sc[...] = jnp.zeros_like(dsink_sc)
            dbias_ref[...] = jnp.zeros_like(dbias_ref)
            dgain_ref[...] = jnp.zeros_like(dgain_ref)

        carry_k = dkf_sc[0:BLOCK, :]
        carry_v = dvf_sc[0:BLOCK, :]
        dkf_sc[0:tq, :] = jnp.zeros((tq, KV_WIDTH), F32)
        dvf_sc[0:tq, :] = jnp.zeros((tq, KV_WIDTH), F32)
        dkf_sc[tq:tq + BLOCK, :] = carry_k
        dvf_sc[tq:tq + BLOCK, :] = carry_v
        _load_band(kf_sc, vf_sc, kp_ref, kc_ref, vp_ref, vc_ref, tq)

        do, dgain = _rms_bwd(dm_ref[...].astype(F32), raw_ref[...].astype(F32), g_ref[...])
        dgain_ref[...] += dgain
        do_sc[...] = do
        ones = jnp.ones((SUBLANES, HEAD_DIM), BF16)

        def block(b, carry):
            r0 = pl.multiple_of(b * BLOCK, BLOCK)
            no_prev = (tile == 0) & (b == 0)
            for g in range(N_KV_HEADS):
                kb = kf_sc[pl.ds(r0, 2 * BLOCK), _head_cols(g)]
                vb = vf_sc[pl.ds(r0, 2 * BLOCK), _head_cols(g)]
                qg = _stack_heads(q_ref, r0, g, BF16)
                dog = _stack_heads(do_sc, r0, g, F32)
                prod = dog * _stack_heads(raw_ref, r0, g, F32)
                hi = prod.astype(BF16)
                lo = (prod - hi.astype(F32)).astype(BF16)
                drow_sc[:, _group_lanes(g)] = _dot_nt(ones, hi) + _dot_nt(ones, lo)
                dogb = dog.astype(BF16)
                qs_sc[g] = qg
                dos_sc[g] = dogb
                st_sc[:, _group_lanes(g)] = _dot_nt(kb, qg)
                dpt_sc[:, _group_lanes(g)] = _dot_nt(vb, dogb)
            for h in range(N_Q_HEADS):
                hl = _head_lanes(h)
                p, ps = _head_softmax(st_sc[:, hl], bias_ref[h], sink_ref[h], no_prev)
                rowdot = drow_sc[0:1, hl]
                ds = p * (dpt_sc[:, hl] - rowdot)
                dsink_sc[h:h + 1, :] += -(ps * rowdot)
                dbias_ref[h] += ds
                dst_sc[:, hl] = ds.astype(BF16)
                pt_sc[:, hl] = p.astype(BF16)
            for g in range(N_KV_HEADS):
                kb = kf_sc[pl.ds(r0, 2 * BLOCK), _head_cols(g)]
                dsg = dst_sc[:, _group_lanes(g)]
                _unstack_heads(dq_sc, r0, g, _dot_tn(dsg, kb) * SCALE)
                dkf_sc[pl.ds(r0, 2 * BLOCK), _head_cols(g)] += _dot(dsg, qs_sc[g]) * SCALE
                dvf_sc[pl.ds(r0, 2 * BLOCK), _head_cols(g)] += _dot(pt_sc[:, _group_lanes(g)], dos_sc[g])
            return carry

        lax.fori_loop(0, nblk, block, 0, unroll=True)
        dq_ref[...] = dq_sc[...].astype(BF16)
        dk_ref[...] = dkf_sc[BLOCK:BLOCK + tq, :].astype(BF16)
        dv_ref[...] = dvf_sc[BLOCK:BLOCK + tq, :].astype(BF16)

        @pl.when(i == nt - 1)
        def _():
            tot = jnp.sum(dsink_sc[...], axis=1, keepdims=True)
            dsink_ref[...] = jnp.broadcast_to(tot, dsink_ref.shape)

    cur = lambda i: (nt - 1 - i, 0)
    prev = lambda i: (jnp.maximum((nt - 1 - i) * nblk - 1, 0), 0)
    const2 = lambda i: (0, 0)
    const3 = lambda i: (0, 0, 0)
    return pl.pallas_call(
        body, name=name, grid=(nt,),
        in_specs=[pl.BlockSpec(memory_space=pltpu.SMEM),
                  pl.BlockSpec((tq, ATTN_WIDTH), cur),
                  pl.BlockSpec((tq, ATTN_WIDTH), cur),
                  pl.BlockSpec((tq, ATTN_WIDTH), cur),
                  pl.BlockSpec((tq, KV_WIDTH), cur), pl.BlockSpec((BLOCK, KV_WIDTH), prev),
                  pl.BlockSpec((tq, KV_WIDTH), cur), pl.BlockSpec((BLOCK, KV_WIDTH), prev),
                  pl.BlockSpec(bias_t.shape, const3),
                  pl.BlockSpec((1, ATTN_WIDTH), const2)],
        out_specs=(pl.BlockSpec((tq, ATTN_WIDTH), cur),
                   pl.BlockSpec((tq, KV_WIDTH), cur), pl.BlockSpec((tq, KV_WIDTH), cur),
                   pl.BlockSpec(bias_t.shape, const3),
                   pl.BlockSpec((N_Q_HEADS, LANES), const2),
                   pl.BlockSpec((1, ATTN_WIDTH), const2)),
        out_shape=(pltpu.HBM((T,ATTN_WIDTH), BF16),
                   pltpu.HBM((T,KV_WIDTH), BF16), pltpu.HBM((T,KV_WIDTH), BF16),
                   jax.ShapeDtypeStruct(bias_t.shape, F32),
                   jax.ShapeDtypeStruct((N_Q_HEADS, LANES), F32),
                   jax.ShapeDtypeStruct((1, ATTN_WIDTH), F32)),
        scratch_shapes=[pltpu.VMEM((tq, ATTN_WIDTH), F32), pltpu.VMEM((tq, ATTN_WIDTH), F32),
                        pltpu.VMEM((tq + BLOCK, KV_WIDTH), BF16), pltpu.VMEM((tq + BLOCK, KV_WIDTH), BF16),
                        pltpu.VMEM((tq + BLOCK, KV_WIDTH), F32), pltpu.VMEM((tq + BLOCK, KV_WIDTH), F32),
                        pltpu.VMEM((2 * BLOCK, lanes), F32), pltpu.VMEM((2 * BLOCK, lanes), F32),
                        pltpu.VMEM((2 * BLOCK, lanes), BF16), pltpu.VMEM((2 * BLOCK, lanes), BF16),
                        pltpu.VMEM((SUBLANES, lanes), F32),
                        pltpu.VMEM((N_KV_HEADS, GROUP_ROWS, HEAD_DIM), BF16),
                        pltpu.VMEM((N_KV_HEADS, GROUP_ROWS, HEAD_DIM), BF16),
                        pltpu.VMEM((N_Q_HEADS, LANES), F32)],
        compiler_params=_params("arbitrary"),
    )(sinks, *_hbm(dmixed, raw, q, k, k, v, v, bias_t, gain))


def _shift_down(cu, tail):
    row = lax.broadcasted_iota(jnp.int32, cu.shape, 0)
    t6, t7 = tail[6:7, :], tail[7:8, :]
    s1 = jnp.where(row == 0, t7, pltpu.roll(cu, 1, 0))
    s2 = jnp.where(row == 0, t6, jnp.where(row == 1, t7, pltpu.roll(cu, 2, 0)))
    return s1, s2


def _shift_up(d, head):
    n = d.shape[0]
    row = lax.broadcasted_iota(jnp.int32, d.shape, 0)
    h0, h1 = head[0:1, :], head[1:2, :]
    s1 = jnp.where(row == n - 1, h0, pltpu.roll(d, n - 1, 0))
    s2 = jnp.where(row == n - 1, h1, jnp.where(row == n - 2, h0, pltpu.roll(d, n - 2, 0)))
    return s1, s2


def _mixout_fwd(x, attn_n, u, gb, gc, conv_w, gain, w_out, name, tm):
    T, D = x.shape

    def body(x_ref, an_ref, u_ref, b_ref, c_ref, cw_ref, g_ref, wo_ref, xo_ref, cn_ref, tail_sc):
        @pl.when(pl.program_id(0) == 0)
        def _():
            tail_sc[...] = jnp.zeros_like(tail_sc)

        cu = c_ref[...].astype(F32) * u_ref[...].astype(F32)
        s1, s2 = _shift_down(cu, tail_sc[...])
        tail_sc[...] = cu[tm - SUBLANES:tm, :]
        pre = cw_ref[0:1, :] * s2 + cw_ref[1:2, :] * s1 + cw_ref[2:3, :] * cu
        conv = b_ref[...].astype(F32) * pre
        _, chat = _rms_stats(conv)
        cn = (chat * g_ref[...]).astype(BF16)
        cn_ref[...] = cn
        xo_ref[...] = (x_ref[...] + _dot(an_ref[...], wo_ref[0:ATTN_WIDTH, :])
                       + _dot(cn, wo_ref[ATTN_WIDTH:ATTN_WIDTH + CONV_DIM, :]))

    row = lambda i: (i, 0)
    const = lambda i: (0, 0)
    return pl.pallas_call(
        body, name=name, grid=(T // tm,),
        in_specs=[pl.BlockSpec((tm, D), row), pl.BlockSpec((tm, ATTN_WIDTH), row),
                  pl.BlockSpec((tm, CONV_DIM), row), pl.BlockSpec((tm, CONV_DIM), row),
                  pl.BlockSpec((tm, CONV_DIM), row),
                  pl.BlockSpec(conv_w.shape, const), pl.BlockSpec((1, CONV_DIM), const),
                  pl.BlockSpec(w_out.shape, const)],
        out_specs=(pl.BlockSpec((tm, D), row), pl.BlockSpec((tm, CONV_DIM), row)),
        out_shape=(pltpu.HBM((T,D), F32), pltpu.HBM((T,CONV_DIM), BF16)),
        scratch_shapes=[pltpu.VMEM((SUBLANES, CONV_DIM), F32)],
        compiler_params=_params("arbitrary"),
    )(*_hbm(x, attn_n, u, gb, gc, conv_w, gain, w_out))


def _mixout_bwd(dy, attn_n, conv_n, w_out, after, name, tm):
    T, D = dy.shape
    W = ATTN_WIDTH + CONV_DIM
    nt = T // tm

    def body(dy_ref, an_ref, cn_ref, w_ref, after_ref, dm_ref, dw_ref, dw_sc):
        i = pl.program_id(0)

        @pl.when(i == 0)
        def _():
            dw_sc[...] = jnp.zeros_like(dw_sc)

        dyb = dy_ref[...].astype(BF16)
        dm_ref[...] = _dot_nt(dyb, w_ref[...]).astype(BF16)
        dw_sc[0:ATTN_WIDTH, :] += _dot_tn(an_ref[...], dyb)
        dw_sc[ATTN_WIDTH:W, :] += _dot_tn(cn_ref[...], dyb)

        @pl.when(i == nt - 1)
        def _():
            dw_ref[...] = dw_sc[...].astype(BF16)

    row = lambda i: (i, 0)
    const = lambda i: (0, 0)
    return pl.pallas_call(
        body, name=name, grid=(nt,),
        in_specs=[pl.BlockSpec((tm, D), row), pl.BlockSpec((tm, ATTN_WIDTH), row),
                  pl.BlockSpec((tm, CONV_DIM), row), pl.BlockSpec(w_out.shape, const),
                  pl.BlockSpec(memory_space=pl.ANY)],
        out_specs=(pl.BlockSpec((tm, W), row), pl.BlockSpec((W, D), const)),
        out_shape=(pltpu.HBM((T,W), BF16), pltpu.HBM((W, D), BF16)),
        scratch_shapes=[pltpu.VMEM((W, D), F32)],
        compiler_params=_params("arbitrary"),
    )(*_hbm(dy, attn_n, conv_n, w_out, after))


def _conv_bwd(dmixed, u, gb, gc, conv_w, gain, name, tc):
    T = u.shape[0]
    nt = T // tc
    per8 = tc // SUBLANES

    def body(dm_ref, u_ref, b_ref, c_ref, ut_ref, ct_ref, cw_ref, g_ref,
             du_ref, db_ref, dc_ref, dcw_ref, dgain_ref, head_sc):
        i = pl.program_id(0)

        @pl.when(i == 0)
        def _():
            head_sc[...] = jnp.zeros_like(head_sc)
            dcw_ref[...] = jnp.zeros_like(dcw_ref)
            dgain_ref[...] = jnp.zeros_like(dgain_ref)

        uu = u_ref[...].astype(F32)
        cc = c_ref[...].astype(F32)
        bb = b_ref[...].astype(F32)
        cu = cc * uu
        tail = jnp.where(i == nt - 1, 0.0, ct_ref[...].astype(F32) * ut_ref[...].astype(F32))
        s1, s2 = _shift_down(cu, tail)
        w0, w1, w2 = cw_ref[0:1, :], cw_ref[1:2, :], cw_ref[2:3, :]
        pre = w0 * s2 + w1 * s1 + w2 * cu
        dconv, dgain = _rms_bwd(dm_ref[...].astype(F32), bb * pre, g_ref[...])
        dgain_ref[...] += dgain
        db_ref[...] = (dconv * pre).astype(BF16)
        dpre = dconv * bb
        dcw_ref[0:1, :] += jnp.sum(dpre * s2, axis=0, keepdims=True)
        dcw_ref[1:2, :] += jnp.sum(dpre * s1, axis=0, keepdims=True)
        dcw_ref[2:3, :] += jnp.sum(dpre * cu, axis=0, keepdims=True)
        n1, n2 = _shift_up(dpre, head_sc[...])
        head_sc[...] = dpre[0:SUBLANES, :]
        dcu = w2 * dpre + w1 * n1 + w0 * n2
        du_ref[...] = (dcu * cc).astype(BF16)
        dc_ref[...] = (dcu * uu).astype(BF16)

    rev = lambda i: (nt - 1 - i, 0)
    rev_right = lambda i: (nt - 1 - i, 1)
    tail_map = lambda i: (jnp.maximum((nt - 1 - i) * per8 - 1, 0), 0)
    const = lambda i: (0, 0)
    return pl.pallas_call(
        body, name=name, grid=(nt,),
        in_specs=[pl.BlockSpec((tc, CONV_DIM), rev_right),
                  pl.BlockSpec((tc, CONV_DIM), rev), pl.BlockSpec((tc, CONV_DIM), rev),
                  pl.BlockSpec((tc, CONV_DIM), rev),
                  pl.BlockSpec((SUBLANES, CONV_DIM), tail_map), pl.BlockSpec((SUBLANES, CONV_DIM), tail_map),
                  pl.BlockSpec(conv_w.shape, const), pl.BlockSpec((1, CONV_DIM), const)],
        out_specs=(pl.BlockSpec((tc, CONV_DIM), rev), pl.BlockSpec((tc, CONV_DIM), rev),
                   pl.BlockSpec((tc, CONV_DIM), rev),
                   pl.BlockSpec((SUBLANES, CONV_DIM), const), pl.BlockSpec((1, CONV_DIM), const)),
        out_shape=(pltpu.HBM((T,CONV_DIM), BF16), pltpu.HBM((T,CONV_DIM), BF16),
                   pltpu.HBM((T,CONV_DIM), BF16),
                   jax.ShapeDtypeStruct((SUBLANES, CONV_DIM), F32), jax.ShapeDtypeStruct((1, CONV_DIM), F32)),
        scratch_shapes=[pltpu.VMEM((SUBLANES, CONV_DIM), F32)],
        compiler_params=_params("arbitrary"),
    )(*_hbm(dmixed, u, gb, gc, u, gc, conv_w, gain))


def _mixin_bwd(dy, x, gain, dz, w_in_t, name, tm):
    T, D = x.shape
    nz = len(dz)

    def body(dy_ref, x_ref, g_ref, *rest):
        dz_refs, wt_ref, dx_ref, dgain_ref, dz_sc = rest[:nz], rest[nz], rest[nz + 1], rest[nz + 2], rest[nz + 3]

        @pl.when(pl.program_id(0) == 0)
        def _():
            dgain_ref[...] = jnp.zeros_like(dgain_ref)

        for r, lo, hi in zip(dz_refs, _Z_SPLITS[:-1], _Z_SPLITS[1:]):
            dz_sc[:, lo:hi] = r[...]
        dx, dgain = _rms_bwd(_dot(dz_sc[...], wt_ref[...]), x_ref[...], g_ref[...])
        dgain_ref[...] += dgain
        dx_ref[...] = dy_ref[...] + dx

    row = lambda i: (i, 0)
    const = lambda i: (0, 0)
    return pl.pallas_call(
        body, name=name, grid=(T // tm,),
        in_specs=[pl.BlockSpec((tm, D), row), pl.BlockSpec((tm, D), row), pl.BlockSpec((1, D), const)]
                 + [pl.BlockSpec((tm, a.shape[1]), row) for a in dz]
                 + [pl.BlockSpec(w_in_t.shape, const)],
        out_specs=(pl.BlockSpec((tm, D), row), pl.BlockSpec((1, D), const)),
        out_shape=(pltpu.HBM((T,D), F32), jax.ShapeDtypeStruct((1, D), F32)),
        scratch_shapes=[pltpu.VMEM((tm, _Z_SPLITS[-1]), BF16)],
        compiler_params=_params("arbitrary"),
    )(*_hbm(dy, x, gain, *dz, w_in_t))


def _win_grad(dz, hn, name, tk):
    T, D = hn.shape
    nz = len(dz)
    nt = T // tk
    W = _Z_SPLITS[-1]

    def body(hn_ref, *rest):
        dz_refs, dw_ref, dw_sc, kv_sc = rest[:nz], rest[nz], rest[nz + 1], rest[nz + 2]
        i = pl.program_id(0)

        @pl.when(i == 0)
        def _():
            dw_sc[...] = jnp.zeros_like(dw_sc)

        hn = hn_ref[...]
        kv_sc[:, :KV_WIDTH] = dz_refs[1][...]
        kv_sc[:, KV_WIDTH:] = dz_refs[2][...]
        dw_sc[_Z_SPLITS[1]:_Z_SPLITS[3], :] += _dot_tn(kv_sc[...], hn)
        for n in (0, 3, 4, 5):
            dw_sc[_Z_SPLITS[n]:_Z_SPLITS[n + 1], :] += _dot_tn(dz_refs[n][...], hn)

        @pl.when(i == nt - 1)
        def _():
            dw_ref[...] = dw_sc[...].astype(BF16)

    row = lambda i: (i, 0)
    return pl.pallas_call(
        body, name=name, grid=(nt,),
        in_specs=[pl.BlockSpec((tk, D), row)] + [pl.BlockSpec((tk, a.shape[1]), row) for a in dz],
        out_specs=pl.BlockSpec((W, D), lambda i: (0, 0)),
        out_shape=pltpu.HBM((W, D), BF16),
        scratch_shapes=[pltpu.VMEM((W, D), F32), pltpu.VMEM((tk, 2 * KV_WIDTH), BF16)],
        compiler_params=_params("arbitrary"),
    )(*_hbm(hn, *dz))


def _adamw(parts, w, m, v, name, tr):
    P = parts.shape[0]
    R, C = w.shape

    def body(p_ref, w_ref, m_ref, v_ref, g_ref, d_ref, nm_ref, nv_ref):
        g = p_ref[0].astype(F32)
        for d in range(1, P):
            g = g + p_ref[d].astype(F32)
        nm = ADAM_B1 * m_ref[...] + (1.0 - ADAM_B1) * g
        nv = ADAM_B2 * v_ref[...] + (1.0 - ADAM_B2) * (g * g)
        m_hat = nm / (1.0 - ADAM_B1 ** ADAM_STEP)
        v_hat = nv / (1.0 - ADAM_B2 ** ADAM_STEP)
        g_ref[...] = g
        nm_ref[...] = nm
        nv_ref[...] = nv
        d_ref[...] = -ADAM_LR * (m_hat / (jnp.sqrt(v_hat) + ADAM_EPS) + ADAM_WD * w_ref[...])

    row = lambda i: (i, 0)
    spec = pl.BlockSpec((tr, C), row)
    shp = pltpu.HBM((R, C), F32)
    return pl.pallas_call(
        body, name=name, grid=(R // tr,),
        in_specs=[pl.BlockSpec((P, tr, C), lambda i: (0, i, 0)), spec, spec, spec],
        out_specs=(spec, spec, spec, spec),
        out_shape=(shp, shp, shp, shp),
        compiler_params=_params("arbitrary"),
    )(*_hbm(parts, w, m, v))


def _columns_of_blocks(g):
    n, R, w = g.shape
    return g.transpose(1, 0, 2).reshape(R, n * w)


def _pad_row(vec):
    vec = vec.reshape(1, -1)
    return jnp.pad(vec, ((0, 0), (0, PACK_COLS - vec.shape[1])))


def kernel(x, rel_bias_table, ffn1_norm, ffn1_w_gate, ffn1_w_up, ffn1_w_down, mix_norm, w_in, conv_w, attn_sinks, attn_out_norm, conv_out_norm, w_out, ffn2_norm, ffn2_w_gate, ffn2_w_up, ffn2_w_down, final_norm, loss_target, m_rel_bias_table, m_ffn1_norm, m_ffn1_w_gate, m_ffn1_w_up, m_ffn1_w_down, m_mix_norm, m_w_in, m_conv_w, m_attn_sinks, m_attn_out_norm, m_conv_out_norm, m_w_out, m_ffn2_norm, m_ffn2_w_gate, m_ffn2_w_up, m_ffn2_w_down, m_final_norm, v_rel_bias_table, v_ffn1_norm, v_ffn1_w_gate, v_ffn1_w_up, v_ffn1_w_down, v_mix_norm, v_w_in, v_conv_w, v_attn_sinks, v_attn_out_norm, v_conv_out_norm, v_w_out, v_ffn2_norm, v_ffn2_w_gate, v_ffn2_w_up, v_ffn2_w_down, v_final_norm):
    T, D = x.shape[1], x.shape[2]
    x0 = x[0]
    target = loss_target[0]
    tm = min(TM_FFN, T)
    tm_bwd = min(TM_FFN_BWD, T)
    tm_mix = min(TM_MIX, T)
    tm_wide = min(TM_MIX_WIDE, T)
    tk = min(TK_WGRAD, T)
    tf = TF_FFN
    nblk = min(ATTN_BLOCKS, T // BLOCK)
    me = 4 * lax.axis_index("x") + 2 * lax.axis_index("y") + lax.axis_index("c")

    big = {
        "ffn1_w_gate": (ffn1_w_gate[0], m_ffn1_w_gate[0], v_ffn1_w_gate[0], True),
        "ffn1_w_up": (ffn1_w_up[0], m_ffn1_w_up[0], v_ffn1_w_up[0], True),
        "ffn1_w_down": (ffn1_w_down[0], m_ffn1_w_down[0], v_ffn1_w_down[0], False),
        "w_in": (w_in[0], m_w_in[0], v_w_in[0], True),
        "w_out": (w_out[0], m_w_out[0], v_w_out[0], False),
        "ffn2_w_gate": (ffn2_w_gate[0], m_ffn2_w_gate[0], v_ffn2_w_gate[0], True),
        "ffn2_w_up": (ffn2_w_up[0], m_ffn2_w_up[0], v_ffn2_w_up[0], True),
        "ffn2_w_down": (ffn2_w_down[0], m_ffn2_w_down[0], v_ffn2_w_down[0], False),
    }

    def block_to_send(name):
        w, _, _, transposed = big[name]
        return (w.T if transposed else w).astype(BF16)

    names1 = ["ffn1_w_gate", "ffn1_w_up", "ffn1_w_down"]
    names_rest = ["w_in", "w_out", "ffn2_w_gate", "ffn2_w_up", "ffn2_w_down"]
    first = _gather_two_level([block_to_send(n) for n in names1], "gather_ffn1", ffn1_norm)
    wgt1, wut1, wd1 = [g.reshape(-1, D) for g in first]
    h_rest, token = _exchange_start([block_to_send(n) for n in names_rest[:2]] + [conv_w[0]]
                                    + [block_to_send(n) for n in names_rest[2:]], False,
                                    "gather_start_rest", wd1)

    wgu1 = _stack_gate_up(wgt1, wut1, tf)
    x1, xn1, gu1 = _ffn_fwd(x0, ffn1_norm, wgu1, wd1, token, "ffn1_fwd", tm, tf)
    mixw = _exchange_wait(h_rest[:3], False, "gather_wait_mix", x1)
    win_t = mixw[0].reshape(-1, D)
    wout = mixw[1].reshape(-1, D)
    cw = _columns_of_blocks(mixw[2])
    hn, q, k, v, u, gb, gc = _mixin_fwd(x1, mix_norm, win_t, "mixin_fwd", tm_wide)
    bucket = jnp.asarray(_bucket_table().T.copy())
    sinks = attn_sinks.reshape(-1)
    bias_t = _bias_build(rel_bias_table, bucket, "bias_build")
    attn_raw, attn_n = _attn_fwd(q, k, v, bias_t, sinks, attn_out_norm, "attn_fwd", nblk)
    x2, conv_n = _mixout_fwd(x1, attn_n, u, gb, gc, cw, conv_out_norm, wout, "mixout_fwd", tm_mix)
    wgt2, wut2, wd2 = [g.reshape(-1, D) for g in _exchange_wait(h_rest[3:], False, "gather_wait_ffn2", x2)]
    wgu2 = _stack_gate_up(wgt2, wut2, tf)
    dx3, xn2, gu2, d_final, loss_part = _ffn_fwd(x2, ffn2_norm, wgu2, wd2, x2, "ffn2_fwd", tm, tf,
                                                 head=(target, final_norm.reshape(1, D)))

    def blocks(g):
        return g.reshape(N_DEV, -1, D)

    dx2, d_ffn2_norm, dg2, du2, hh2, do2 = _ffn_bwd(
        dx3, x2, ffn2_norm, gu2, wgu2, wd2, dx3, "ffn2_bwd", tm_bwd, tf)
    d_wg2 = _tn_grad(dg2, xn2, dx2, "ffn2_wgrad_gate", BM_WGRAD, tk)
    d_wu2 = _tn_grad(du2, xn2, dx2, "ffn2_wgrad_up", BM_WGRAD, tk)
    d_wd2 = _tn_grad(hh2, do2, dx2, "ffn2_wgrad_down", BM_WGRAD, tk)
    handles2, token2 = _exchange_start([blocks(d_wg2), blocks(d_wu2), blocks(d_wd2)], True,
                                       "grads_start_ffn2", dx2)

    dmixed, d_wout = _mixout_bwd(dx2, attn_n, conv_n, wout, token2, "mixout_bwd", tm_wide)
    dq, dk, dv, dbias, dsink, d_attn_norm = _attn_bwd(
        dmixed, attn_raw, q, k, v, bias_t, sinks, attn_out_norm, "attn_bwd", nblk)
    du, dgb, dgc, d_cw, d_conv_norm = _conv_bwd(dmixed, u, gb, gc, cw, conv_out_norm, "conv_bwd", tm_wide)
    d_table = _bias_grad(dbias, bucket, "bias_grad")
    dz = [dq, dk, dv, du, dgb, dgc]
    dx1, d_mix_norm = _mixin_bwd(dx2, x1, mix_norm, dz, win_t, "mixin_bwd", tm_mix)
    d_win_t = _win_grad(dz, hn, "win_grad", tm_wide)
    handles_mix, token_mix = _exchange_start([blocks(d_win_t), blocks(d_wout)], True, "grads_start_mix", d_table)

    dx0, d_ffn1_norm, dg1, du1, hh1, do1 = _ffn_bwd(
        dx1, x0, ffn1_norm, gu1, wgu1, wd1, token_mix, "ffn1_bwd", tm_bwd, tf)
    def pack(ffn1, mixn, ffn2, fin, attn_n_, conv_n_, sink_, extra, convw, table):
        rows = [_pad_row(ffn1), _pad_row(mixn), _pad_row(ffn2), _pad_row(fin),
                _pad_row(jnp.concatenate([attn_n_.reshape(-1), conv_n_.reshape(-1)])),
                _pad_row(sink_), _pad_row(extra),
                jnp.zeros((1, PACK_COLS), F32),
                jnp.pad(convw, ((0, 0), (0, PACK_COLS - convw.shape[1]))),
                _pad_row(table),
                jnp.zeros((PACK_ROWS - 12, PACK_COLS), F32)]
        return jnp.concatenate(rows, axis=0)

    def own_channels(a):
        full = jnp.zeros((a.shape[1], CONV_DIM), F32)
        return lax.dynamic_update_slice(full, a[0], (0, me * a.shape[2]))

    g_pack = pack(d_ffn1_norm, d_mix_norm, d_ffn2_norm, d_final, d_attn_norm, d_conv_norm,
                  dsink[:, 0], loss_part[0, :1], d_cw[:3], d_table[:, :N_Q_HEADS])
    zero1 = jnp.zeros((1,), F32)
    w_pack = pack(ffn1_norm, mix_norm, ffn2_norm, final_norm, attn_out_norm, conv_out_norm,
                  attn_sinks, zero1, own_channels(conv_w), rel_bias_table)
    m_pack = pack(m_ffn1_norm, m_mix_norm, m_ffn2_norm, m_final_norm, m_attn_out_norm, m_conv_out_norm,
                  m_attn_sinks, zero1, own_channels(m_conv_w), m_rel_bias_table)
    v_pack = pack(v_ffn1_norm, v_mix_norm, v_ffn2_norm, v_final_norm, v_attn_out_norm, v_conv_out_norm,
                  v_attn_sinks, zero1, own_channels(v_conv_w), v_rel_bias_table)
    h_small, token_small = _exchange_start([g_pack], False, "gather_small_start", dx0)

    d_wd1 = _tn_grad(hh1, do1, token_small, "ffn1_wgrad_down", BM_WGRAD, tk)
    handles1_d, token1 = _exchange_start([blocks(d_wd1)], True, "grads_start_ffn1_down", dx0)
    d_wg1 = _tn_grad(dg1, xn1, token1, "ffn1_wgrad_gate", BM_WGRAD, tk)
    handles1_g, token1 = _exchange_start([blocks(d_wg1)], True, "grads_start_ffn1_gate", token1)
    d_wu1 = _tn_grad(du1, xn1, token1, "ffn1_wgrad_up", BM_WGRAD, tk)
    handles1_u, token1 = _exchange_start([blocks(d_wu1)], True, "grads_start_ffn1_up", token1)

    res = {}

    def update(names, parts):
        last = None
        for name, p in zip(names, parts):
            w, m_, v_, transposed = big[name]
            if transposed:
                w, m_, v_ = w.T, m_.T, v_.T
            new = _adamw(p, w, m_, v_, "adamw_" + name, _row_tile(w.shape[0], ADAM_ROWS))
            res[name] = tuple((a.T if transposed else a)[None] for a in new)
            last = new[0]
        return last

    parts2 = _exchange_wait(handles2, True, "grads_wait_ffn2", token1)
    done2 = update(["ffn2_w_gate", "ffn2_w_up", "ffn2_w_down"], parts2)
    parts_mix = _exchange_wait(handles_mix, True, "grads_wait_mix", done2)
    done_mix = update(["w_in", "w_out"], parts_mix)
    (g_all,) = _exchange_wait(h_small, False, "gather_small_wait", done_mix)
    packs = _adamw(g_all, w_pack, m_pack, v_pack, "adamw_small", PACK_ROWS)

    def unpack(pk):
        cwb = lax.dynamic_slice(pk[8:11, :CONV_DIM], (0, me * conv_w.shape[2]), (3, conv_w.shape[2]))
        return {
            "ffn1_norm": pk[0:1, :D], "mix_norm": pk[1:2, :D], "ffn2_norm": pk[2:3, :D],
            "final_norm": pk[3, :D],
            "attn_out_norm": pk[4:5, :ATTN_WIDTH], "conv_out_norm": pk[4:5, ATTN_WIDTH:ATTN_WIDTH + CONV_DIM],
            "attn_sinks": pk[5:6, :N_Q_HEADS],
            "conv_w": cwb[None],
            "rel_bias_table": pk[11, :NUM_BUCKETS * N_Q_HEADS].reshape(NUM_BUCKETS, N_Q_HEADS),
        }

    small = [unpack(pk) for pk in packs]
    loss = packs[0][6, 0]

    ready = [r for name in res for r in res[name]] + [a for group in small for a in group.values()]
    parts1_d = _exchange_wait(handles1_d, True, "grads_wait_ffn1_down", ready)
    done1 = update(["ffn1_w_down"], parts1_d)
    parts1_g = _exchange_wait(handles1_g, True, "grads_wait_ffn1_gate", done1)
    done1 = update(["ffn1_w_gate"], parts1_g)
    parts1_u = _exchange_wait(handles1_u, True, "grads_wait_ffn1_up", done1)
    update(["ffn1_w_up"], parts1_u)

    order = ["rel_bias_table", "ffn1_norm", "ffn1_w_gate", "ffn1_w_up", "ffn1_w_down", "mix_norm", "w_in",
             "conv_w", "attn_sinks", "attn_out_norm", "conv_out_norm", "w_out", "ffn2_norm",
             "ffn2_w_gate", "ffn2_w_up", "ffn2_w_down", "final_norm"]
    outs = [loss, dx0[None]]
    for kind in range(4):
        for name in order:
            outs.append(res[name][kind] if name in res else small[kind][name])
    return tuple(outs)
```

```python
import math

import numpy as np
import jax
import jax.numpy as jnp
from jax import lax
from jax.experimental import pallas as pl
from jax.experimental.pallas import tpu as pltpu

F32 = jnp.float32
BF16 = jnp.bfloat16

N_DEV = 8
EPS = 1e-6
HEAD_DIM = 64
N_Q_HEADS = 8
N_KV_HEADS = 2
GQA_GROUP = 4
ATTN_WIDTH = 512
KV_WIDTH = 128
CONV_DIM = 512
BLOCK = 128
WINDOW = 128
NUM_BUCKETS = 32
MAX_DISTANCE = 128
SCALE = HEAD_DIM ** -0.5
MASKED = -1e30
GROUP_ROWS = GQA_GROUP * BLOCK

ADAM_LR = 0.001
ADAM_B1 = 0.9
ADAM_B2 = 0.999
ADAM_EPS = 1e-08
ADAM_WD = 0.01
ADAM_STEP = 10

VMEM_LIMIT_BYTES = 40 * 1024 * 1024
VMEM_LIMIT_FFN_BYTES = 50 * 1024 * 1024
SUBLANES = 8
LANES = 128
PACK_ROWS = 16
PACK_COLS = 1024

TM_FFN = 1024
TM_FFN_BWD = 1024
TM_MIX = 512
TM_MIX_WIDE = 1024
TK_WGRAD = 2048
TF_FFN = 256
BM_WGRAD = 1408
ROW_GROUPS = 4
ATTN_BLOCKS = 8
ADAM_ROWS = 256


def _row_tile(rows, limit):
    best = rows
    for t in range(16, min(rows, limit) + 1, 16):
        if rows % t == 0:
            best = t
    return best


def _params(*sem):
    return pltpu.CompilerParams(dimension_semantics=sem, vmem_limit_bytes=VMEM_LIMIT_BYTES)


def _hbm(*arrays):
    return [pltpu.with_memory_space_constraint(a, pltpu.HBM) for a in arrays]


def _dot(a, b):
    return jnp.dot(a, b, preferred_element_type=F32)


def _dot_nt(a, b):
    return lax.dot_general(a, b, (((1,), (1,)), ((), ())), preferred_element_type=F32)


def _dot_tn(a, b):
    return lax.dot_general(a, b, (((0,), (0,)), ((), ())), preferred_element_type=F32)


def _sigmoid(g):
    return 0.5 * jnp.tanh(0.5 * g) + 0.5


def _rms_stats(x):
    inv = lax.rsqrt(jnp.mean(x * x, axis=-1, keepdims=True) + EPS)
    return inv, x * inv


def _rms_bwd(dy, x, gain):
    inv, xhat = _rms_stats(x)
    dgain = jnp.sum(dy * xhat, axis=0, keepdims=True)
    dxh = dy * gain
    dx = inv * (dxh - xhat * jnp.mean(dxh * xhat, axis=-1, keepdims=True))
    return dx, dgain


def _peer_list():
    x, y, c = lax.axis_index("x"), lax.axis_index("y"), lax.axis_index("c")
    peers = []
    for k in range(1, N_DEV):
        px = 1 - x if (k >> 2) & 1 else x
        py = 1 - y if (k >> 1) & 1 else y
        pc = 1 - c if k & 1 else c
        peers.append((px, py, pc))
    return 4 * x + 2 * y + c, peers


def _gather_two_level(arrs, name, after):
    n = len(arrs)
    out_shape = tuple(jax.ShapeDtypeStruct((N_DEV,) + a.shape, a.dtype) for a in arrs)

    def body(*refs):
        ins, outs = refs[:n], refs[n + 1:2 * n + 1]
        send_sems, recv_sems, local_sems = refs[2 * n + 1:]
        x, y, c = lax.axis_index("x"), lax.axis_index("y"), lax.axis_index("c")
        me, sibling = (x, y, c), (x, y, 1 - c)
        chips = [(1 - x, y), (x, 1 - y), (1 - x, 1 - y)]

        def copy(a, k, block, to, src=None):
            slot = outs[a].at[4 * block[0] + 2 * block[1] + block[2]]
            return pltpu.make_async_remote_copy(
                src_ref=slot if src is None else src, dst_ref=slot,
                send_sem=send_sems.at[a, k], recv_sem=recv_sems.at[a, k],
                device_id=to, device_id_type=pl.DeviceIdType.MESH)

        started = []
        for a in range(n):
            loc = pltpu.make_async_copy(ins[a], outs[a].at[4 * x + 2 * y + c], local_sems.at[a])
            loc.start()
            started.append(loc)
        sends = []
        for a in range(n):
            sends.append(copy(a, 0, me, sibling, src=ins[a]))
            sends += [copy(a, 1 + j, me, (*chip, c), src=ins[a]) for j, chip in enumerate(chips)]
        for cp in sends:
            cp.start()
        for j, chip in enumerate(chips):
            for a in range(n):
                copy(a, 1 + j, (*chip, c), me).wait_recv()
                fwd = copy(a, 4 + j, (*chip, c), sibling)
                fwd.start()
                sends.append(fwd)
        for a in range(n):
            copy(a, 0, sibling, me).wait_recv()
            for j, chip in enumerate(chips):
                copy(a, 4 + j, (*chip, 1 - c), me).wait_recv()
        for cp in sends:
            cp.wait_send()
        for loc in started:
            loc.wait()

    hbm = pl.BlockSpec(memory_space=pl.ANY)
    return pl.pallas_call(
        body, name=name, out_shape=out_shape,
        in_specs=[hbm] * (n + 1), out_specs=tuple([hbm] * n),
        scratch_shapes=[pltpu.SemaphoreType.DMA((n, N_DEV - 1)),
                        pltpu.SemaphoreType.DMA((n, N_DEV - 1)),
                        pltpu.SemaphoreType.DMA((n,))],
    )(*arrs, after)


_HBM = pl.BlockSpec(memory_space=pltpu.HBM)
_SEM = pl.BlockSpec(memory_space=pltpu.SEMAPHORE)
_EFFECT = pltpu.SideEffectType.DATAFLOW_SIDE_EFFECTING


def _split_copies(srcs, lands, send_sems, recv_sems, scatter):
    me, peers = _peer_list()
    copies = []
    for a in range(len(srcs)):
        for k, (px, py, pc) in enumerate(peers):
            src = srcs[a].at[4 * px + 2 * py + pc] if scatter else srcs[a]
            copies.append(pltpu.make_async_remote_copy(
                src_ref=src, dst_ref=lands[a].at[me],
                send_sem=send_sems[a].at[k], recv_sem=recv_sems[a].at[k],
                device_id=(px, py, pc), device_id_type=pl.DeviceIdType.MESH))
    return copies


def _own_copies(srcs, lands, recv_sems, scatter):
    me, _ = _peer_list()
    return [pltpu.make_async_copy(srcs[a].at[me] if scatter else srcs[a], lands[a].at[me],
                                  recv_sems[a].at[N_DEV - 1]) for a in range(len(srcs))]


def _exchange_start(arrs, scatter, name, after):
    n = len(arrs)
    lands = [lax.empty(a.shape if scatter else (N_DEV,) + a.shape, a.dtype) for a in arrs]

    def body(*refs):
        srcs, lnds = refs[:n], refs[n:2 * n]
        outs = refs[2 * n + 1:]
        send_sems, recv_sems = outs[:n], outs[n:2 * n]
        token = outs[4 * n]
        for cp in _split_copies(srcs, lnds, send_sems, recv_sems, scatter):
            cp.start()
        for cp in _own_copies(srcs, lnds, recv_sems, scatter):
            cp.start()
        token[...] = jnp.zeros_like(token)

    sem = pltpu.SemaphoreType.DMA((N_DEV,))
    out_shape = ([sem] * (2 * n) + [pltpu.HBM(a.shape, a.dtype) for a in arrs]
                 + [pltpu.HBM(l.shape, l.dtype) for l in lands] + [jax.ShapeDtypeStruct((SUBLANES, LANES), F32)])
    res = pl.pallas_call(
        body, name=name, out_shape=tuple(out_shape),
        in_specs=[_HBM] * (2 * n) + [pl.BlockSpec(memory_space=pl.ANY)],
        out_specs=tuple([_SEM] * (2 * n) + [_HBM] * (2 * n) + [pl.BlockSpec(memory_space=pltpu.VMEM)]),
        input_output_aliases={i: 2 * n + i for i in range(2 * n)},
        compiler_params=pltpu.CompilerParams(has_side_effects=_EFFECT),
    )(*[pltpu.with_memory_space_constraint(a, pltpu.HBM) for a in arrs],
      *[pltpu.with_memory_space_constraint(l, pltpu.HBM) for l in lands], after)
    handles = [(res[2 * n + a], res[3 * n + a], res[a], res[n + a]) for a in range(n)]
    return handles, res[4 * n]


def _exchange_wait(handles, scatter, name, after):
    n = len(handles)
    after = list(after) if isinstance(after, (list, tuple)) else [after]

    def body(*refs):
        srcs, lnds = refs[:n], refs[n:2 * n]
        send_sems, recv_sems = refs[2 * n:3 * n], refs[3 * n:4 * n]
        for cp in _split_copies(srcs, lnds, send_sems, recv_sems, scatter):
            cp.wait_send()
            cp.wait_recv()
        for cp in _own_copies(srcs, lnds, recv_sems, scatter):
            cp.wait()

    srcs = [h[0] for h in handles]
    lands = [h[1] for h in handles]
    res = pl.pallas_call(
        body, name=name,
        out_shape=tuple([pltpu.HBM(a.shape, a.dtype) for a in srcs] + [pltpu.HBM(l.shape, l.dtype) for l in lands]),
        in_specs=[_HBM] * (2 * n) + [_SEM] * (2 * n) + [pl.BlockSpec(memory_space=pl.ANY)] * len(after),
        out_specs=tuple([_HBM] * (2 * n)),
        input_output_aliases={i: i for i in range(2 * n)},
        compiler_params=pltpu.CompilerParams(has_side_effects=_EFFECT),
    )(*srcs, *lands, *[h[2] for h in handles], *[h[3] for h in handles], *after)
    return list(res[n:])


def _row_groups(tm):
    return [slice(r * (tm // ROW_GROUPS), (r + 1) * (tm // ROW_GROUPS)) for r in range(ROW_GROUPS)]


def _stack_gate_up(wgt, wut, tf):
    F, D = wgt.shape
    return jnp.stack([wgt.reshape(F // tf, tf, D), wut.reshape(F // tf, tf, D)], axis=1).reshape(2 * F, D)


def _ffn_fwd(x, gain, wgu, wd, after, name, tm, tf, head=None):
    T, D = x.shape
    F = wd.shape[0]
    nj = F // tf
    n_in = 7 if head else 5

    def body(*refs):
        x_ref, g_ref, wgu_ref, wd_ref, after_ref = refs[:5]
        outs = refs[n_in:]
        xo_ref, xn_ref, gu_ref = outs[:3]
        xn_sc, acc_sc = outs[-2:]
        i, j = pl.program_id(0), pl.program_id(1)

        @pl.when(j == 0)
        def _():
            _, xhat = _rms_stats(x_ref[...])
            xn = (xhat * g_ref[...]).astype(BF16)
            xn_sc[...] = xn
            xn_ref[...] = xn
            acc_sc[...] = jnp.zeros_like(acc_sc)

        groups = _row_groups(tm)
        gus = [_dot_nt(xn_sc[rows, :], wgu_ref[...]) for rows in groups]
        hs = []
        for rows, gu in zip(groups, gus):
            gu_ref[rows, :] = gu.astype(BF16)
            g, u = gu[:, :tf], gu[:, tf:]
            hs.append((g * _sigmoid(g) * u).astype(BF16))
        for rows, h in zip(groups, hs):
            acc_sc[rows, :] += _dot(h, wd_ref[...])

        if head:
            t_ref, fg_ref = refs[5:7]
            dgain_ref, loss_ref = outs[3:5]

            @pl.when((i == 0) & (j == 0))
            def _():
                dgain_ref[...] = jnp.zeros_like(dgain_ref)
                loss_ref[...] = jnp.zeros_like(loss_ref)

        @pl.when(j == nj - 1)
        def _():
            for rows in groups:
                xo = x_ref[rows, :] + 0.5 * acc_sc[rows, :]
                if head:
                    fg = fg_ref[...]
                    _, xhat = _rms_stats(xo)
                    err = xhat * fg - t_ref[rows, :]
                    loss_ref[...] += 0.5 * jnp.sum(jnp.mean(err * err, axis=-1, keepdims=True),
                                                   axis=0, keepdims=True)
                    dx, dgain = _rms_bwd(err * (1.0 / D), xo, fg)
                    dgain_ref[...] += dgain
                    xo_ref[rows, :] = dx
                else:
                    xo_ref[rows, :] = xo

    tile = pl.BlockSpec((tm, D), lambda i, j: (i, 0))
    const = pl.BlockSpec((1, D), lambda i, j: (0, 0))
    in_specs = [tile, const, pl.BlockSpec((2 * tf, D), lambda i, j: (j, 0)),
                pl.BlockSpec((tf, D), lambda i, j: (j, 0)), pl.BlockSpec(memory_space=pl.ANY)]
    out_specs = [tile, pl.BlockSpec((tm, D), lambda i, j: (i, 0)), pl.BlockSpec((tm, 2 * tf), lambda i, j: (i, j))]
    out_shape = [pltpu.HBM((T,D), F32), pltpu.HBM((T,D), BF16),
                 pltpu.HBM((T,2 * F), BF16)]
    operands = [x, gain, wgu, wd, after]
    if head:
        in_specs += [tile, const]
        out_specs += [const, pl.BlockSpec((SUBLANES, LANES), lambda i, j: (0, 0))]
        out_shape += [jax.ShapeDtypeStruct((1, D), F32), jax.ShapeDtypeStruct((SUBLANES, LANES), F32)]
        operands += list(head)
    return pl.pallas_call(
        body, name=name, grid=(T // tm, nj),
        in_specs=in_specs, out_specs=tuple(out_specs), out_shape=tuple(out_shape),
        scratch_shapes=[pltpu.VMEM((tm, D), BF16), pltpu.VMEM((tm, D), F32)],
        compiler_params=pltpu.CompilerParams(dimension_semantics=("arbitrary", "arbitrary"),
                                             vmem_limit_bytes=VMEM_LIMIT_FFN_BYTES),
    )(*_hbm(*operands))


def _ffn_bwd(dy, x, gain, gu, wgu, wd, after, name, tm, tf):
    T, D = x.shape
    F = wd.shape[0]
    nj = F // tf

    def body(dy_ref, x_ref, g_ref, gu_ref, wgu_ref, wd_ref, after_ref,
             dx_ref, dgain_ref, dg_ref, du_ref, hh_ref, do_ref, do_sc, acc_sc, dgu_sc):
        i, j = pl.program_id(0), pl.program_id(1)

        @pl.when((i == 0) & (j == 0))
        def _():
            dgain_ref[...] = jnp.zeros_like(dgain_ref)

        @pl.when(j == 0)
        def _():
            do = (0.5 * dy_ref[...]).astype(BF16)
            do_sc[...] = do
            do_ref[...] = do
            acc_sc[...] = jnp.zeros_like(acc_sc)

        groups = _row_groups(tm)
        dhs = [_dot_nt(do_sc[rows, :], wd_ref[...]) for rows in groups]
        for rows, dh in zip(groups, dhs):
            g = gu_ref[rows, :tf].astype(F32)
            u = gu_ref[rows, tf:].astype(F32)
            sig = _sigmoid(g)
            s = g * sig
            dg = (dh * u * (sig + s * (1.0 - sig))).astype(BF16)
            du = (dh * s).astype(BF16)
            dg_ref[rows, :] = dg
            du_ref[rows, :] = du
            dgu_sc[rows, :tf] = dg
            dgu_sc[rows, tf:] = du
            hh_ref[rows, :] = (s * u).astype(BF16)
        for rows in groups:
            acc_sc[rows, :] += _dot(dgu_sc[rows, :], wgu_ref[...])

        @pl.when(j == nj - 1)
        def _():
            for rows in groups:
                dx, dgain = _rms_bwd(acc_sc[rows, :], x_ref[rows, :], g_ref[...])
                dgain_ref[...] += dgain
                dx_ref[rows, :] = dy_ref[rows, :] + dx

    tile_in = pl.BlockSpec((tm, D), lambda i, j: (i, 0))
    tile = pl.BlockSpec((tm, D), lambda i, j: (i, 0), pipeline_mode=pl.Buffered(1))
    return pl.pallas_call(
        body, name=name, grid=(T // tm, nj),
        in_specs=[tile_in, tile_in,
                  pl.BlockSpec((1, D), lambda i, j: (0, 0)),
                  pl.BlockSpec((tm, 2 * tf), lambda i, j: (i, j)),
                  pl.BlockSpec((2 * tf, D), lambda i, j: (j, 0)),
                  pl.BlockSpec((tf, D), lambda i, j: (j, 0)),
                  pl.BlockSpec(memory_space=pl.ANY)],
        out_specs=(tile,
                   pl.BlockSpec((1, D), lambda i, j: (0, 0)),
                   pl.BlockSpec((tm, tf), lambda i, j: (i, j)),
                   pl.BlockSpec((tm, tf), lambda i, j: (i, j)),
                   pl.BlockSpec((tm, tf), lambda i, j: (i, j)),
                   pl.BlockSpec((tm, D), lambda i, j: (i, 0))),
        out_shape=(pltpu.HBM((T,D), F32), jax.ShapeDtypeStruct((1, D), F32),
                   pltpu.HBM((T,F), BF16), pltpu.HBM((T,F), BF16),
                   pltpu.HBM((T,F), BF16), pltpu.HBM((T,D), BF16)),
        scratch_shapes=[pltpu.VMEM((tm, D), BF16), pltpu.VMEM((tm, D), F32), pltpu.VMEM((tm, 2 * tf), BF16)],
        compiler_params=pltpu.CompilerParams(dimension_semantics=("arbitrary", "arbitrary"),
                                             vmem_limit_bytes=VMEM_LIMIT_FFN_BYTES),
    )(*_hbm(dy, x, gain, gu, wgu, wd, after))


def _tn_grad(a, b, after, name, bm, tk):
    T, M = a.shape
    D = b.shape[1]
    nk = T // tk

    def body(a_ref, b_ref, after_ref, o_ref, acc_sc):
        k = pl.program_id(1)

        @pl.when(k == 0)
        def _():
            acc_sc[...] = jnp.zeros_like(acc_sc)

        acc_sc[...] += _dot_tn(a_ref[...], b_ref[...])

        @pl.when(k == nk - 1)
        def _():
            o_ref[...] = acc_sc[...].astype(BF16)

    return pl.pallas_call(
        body, name=name, grid=(M // bm, nk),
        in_specs=[pl.BlockSpec((tk, bm), lambda i, k: (k, i)), pl.BlockSpec((tk, D), lambda i, k: (k, 0)),
                  pl.BlockSpec(memory_space=pl.ANY)],
        out_specs=pl.BlockSpec((bm, D), lambda i, k: (i, 0)),
        out_shape=pltpu.HBM((M, D), BF16),
        scratch_shapes=[pltpu.VMEM((bm, D), F32)],
        compiler_params=_params("arbitrary", "arbitrary"),
    )(*_hbm(a, b, after))


_Z_SPLITS = (0, 512, 640, 768, 1280, 1792, 2304)


def _mixin_fwd(x, gain, w_in_t, name, tm):
    T, D = x.shape
    widths = [b - a for a, b in zip(_Z_SPLITS[:-1], _Z_SPLITS[1:])]

    def body(x_ref, g_ref, w_ref, hn_ref, *outs):
        _, xhat = _rms_stats(x_ref[...])
        hn = (xhat * g_ref[...]).astype(BF16)
        hn_ref[...] = hn
        for n in (0, 3, 4, 5):
            outs[n][...] = _dot_nt(hn, w_ref[_Z_SPLITS[n]:_Z_SPLITS[n + 1], :]).astype(BF16)
        kv = _dot_nt(hn, w_ref[_Z_SPLITS[1]:_Z_SPLITS[3], :]).astype(BF16)
        outs[1][...] = kv[:, :KV_WIDTH]
        outs[2][...] = kv[:, KV_WIDTH:]

    return pl.pallas_call(
        body, name=name, grid=(T // tm,),
        in_specs=[pl.BlockSpec((tm, D), lambda i: (i, 0)),
                  pl.BlockSpec((1, D), lambda i: (0, 0)),
                  pl.BlockSpec(w_in_t.shape, lambda i: (0, 0))],
        out_specs=tuple([pl.BlockSpec((tm, D), lambda i: (i, 0))]
                        + [pl.BlockSpec((tm, w), lambda i: (i, 0)) for w in widths]),
        out_shape=tuple([pltpu.HBM((T,D), BF16)]
                        + [pltpu.HBM((T,w), BF16) for w in widths]),
        compiler_params=_params("arbitrary"),
    )(*_hbm(x, gain, w_in_t))


def _bucket_table():
    qi = np.arange(BLOCK, dtype=np.int32)[:, None]
    kj = np.arange(2 * BLOCK, dtype=np.int32)[None, :]
    dist = qi + BLOCK - kj
    n = np.maximum(dist, 0)
    max_exact = NUM_BUCKETS // 2
    large = max_exact + (np.log(np.maximum(n, 1).astype(np.float32) / max_exact)
                         / math.log(MAX_DISTANCE / max_exact)
                         * (NUM_BUCKETS - max_exact)).astype(np.int32)
    large = np.minimum(large, NUM_BUCKETS - 1)
    bucket = np.where(n < max_exact, n, large).astype(np.int32)
    valid = (dist >= 0) & (dist < WINDOW)
    return np.where(valid, bucket, -1).astype(np.int32)


def _bias_build(table, bucket, name):
    def body(t_ref, b_ref, o_ref):
        bk = b_ref[...]
        for h in range(N_Q_HEADS):
            def step(b, acc):
                return jnp.where(bk == b, t_ref[b, h], acc)
            o_ref[h] = lax.fori_loop(0, NUM_BUCKETS, step, jnp.full(bk.shape, MASKED, F32))

    return pl.pallas_call(
        body, name=name,
        in_specs=[pl.BlockSpec(memory_space=pltpu.SMEM), pl.BlockSpec(memory_space=pltpu.VMEM)],
        out_specs=pl.BlockSpec(memory_space=pltpu.VMEM),
        out_shape=jax.ShapeDtypeStruct((N_Q_HEADS,) + bucket.shape, F32),
    )(table, bucket)


def _bias_grad(dbias, bucket, name):
    def body(d_ref, b_ref, o_ref):
        bk = b_ref[...]
        row = lax.broadcasted_iota(jnp.int32, o_ref.shape, 0)
        lane = lax.broadcasted_iota(jnp.int32, o_ref.shape, 1)
        res = jnp.zeros(o_ref.shape, F32)
        for h in range(N_Q_HEADS):
            d = d_ref[h]

            def step(b, acc):
                part = jnp.sum(jnp.where(bk == b, d, 0.0), axis=0, keepdims=True)
                return jnp.where(row == b, part, acc)
            per_lane = lax.fori_loop(0, NUM_BUCKETS, step, jnp.zeros(o_ref.shape, F32))
            res = jnp.where(lane == h, jnp.sum(per_lane, axis=1, keepdims=True), res)
        o_ref[...] = res

    return pl.pallas_call(
        body, name=name,
        in_specs=[pl.BlockSpec(memory_space=pltpu.VMEM), pl.BlockSpec(memory_space=pltpu.VMEM)],
        out_specs=pl.BlockSpec(memory_space=pltpu.VMEM),
        out_shape=jax.ShapeDtypeStruct((NUM_BUCKETS, LANES), F32),
    )(*_hbm(dbias, bucket))


def _head_cols(h):
    return slice(h * HEAD_DIM, (h + 1) * HEAD_DIM)


def _stack_heads(ref, r0, g, dtype):
    return jnp.concatenate(
        [ref[pl.ds(r0, BLOCK), _head_cols(GQA_GROUP * g + j)].astype(dtype) for j in range(GQA_GROUP)], axis=0)


def _unstack_heads(ref, r0, g, val):
    for j in range(GQA_GROUP):
        ref[pl.ds(r0, BLOCK), _head_cols(GQA_GROUP * g + j)] = val[j * BLOCK:(j + 1) * BLOCK, :]


def _head_lanes(h):
    return slice(h * BLOCK, (h + 1) * BLOCK)


def _group_lanes(g):
    return slice(g * GROUP_ROWS, (g + 1) * GROUP_ROWS)


def _head_softmax(st, bias_t, sink, no_prev):
    s = st * SCALE + bias_t
    row = lax.broadcasted_iota(jnp.int32, s.shape, 0)
    s = jnp.where(no_prev & (row < BLOCK), MASKED, s)
    m = jnp.maximum(jnp.max(s, axis=0, keepdims=True), sink)
    p = jnp.exp(s - m)
    ps = jnp.exp(sink - m)
    r = 1.0 / (jnp.sum(p, axis=0, keepdims=True) + ps)
    return p * r, ps * r


def _load_band(kf_sc, vf_sc, kp_ref, kc_ref, vp_ref, vc_ref, tq):
    kf_sc[0:BLOCK, :] = kp_ref[...]
    kf_sc[BLOCK:BLOCK + tq, :] = kc_ref[...]
    vf_sc[0:BLOCK, :] = vp_ref[...]
    vf_sc[BLOCK:BLOCK + tq, :] = vc_ref[...]


def _attn_fwd(q, k, v, bias_t, sinks, gain, name, nblk):
    T = q.shape[0]
    tq = nblk * BLOCK

    def body(sink_ref, q_ref, kc_ref, kp_ref, vc_ref, vp_ref, bias_ref, g_ref, raw_ref, nrm_ref,
             kf_sc, vf_sc, o_sc, st_sc, pt_sc):
        i = pl.program_id(0)
        _load_band(kf_sc, vf_sc, kp_ref, kc_ref, vp_ref, vc_ref, tq)

        def block(b, carry):
            r0 = pl.multiple_of(b * BLOCK, BLOCK)
            no_prev = (i == 0) & (b == 0)
            for g in range(N_KV_HEADS):
                kb = kf_sc[pl.ds(r0, 2 * BLOCK), _head_cols(g)]
                st_sc[:, _group_lanes(g)] = _dot_nt(kb, _stack_heads(q_ref, r0, g, BF16))
            for h in range(N_Q_HEADS):
                p, _ = _head_softmax(st_sc[:, _head_lanes(h)], bias_ref[h], sink_ref[h], no_prev)
                pt_sc[:, _head_lanes(h)] = p.astype(BF16)
            for g in range(N_KV_HEADS):
                vb = vf_sc[pl.ds(r0, 2 * BLOCK), _head_cols(g)]
                _unstack_heads(o_sc, r0, g, _dot_tn(pt_sc[:, _group_lanes(g)], vb))
            return carry

        lax.fori_loop(0, nblk, block, 0, unroll=True)
        o = o_sc[...]
        raw_ref[...] = o.astype(BF16)
        _, ohat = _rms_stats(o)
        nrm_ref[...] = (ohat * g_ref[...]).astype(BF16)

    cur = lambda i: (i, 0)
    prev = lambda i: (jnp.maximum(i * nblk - 1, 0), 0)
    lanes = N_Q_HEADS * BLOCK
    return pl.pallas_call(
        body, name=name, grid=(T // tq,),
        in_specs=[pl.BlockSpec(memory_space=pltpu.SMEM),
                  pl.BlockSpec((tq, ATTN_WIDTH), cur),
                  pl.BlockSpec((tq, KV_WIDTH), cur), pl.BlockSpec((BLOCK, KV_WIDTH), prev),
                  pl.BlockSpec((tq, KV_WIDTH), cur), pl.BlockSpec((BLOCK, KV_WIDTH), prev),
                  pl.BlockSpec(bias_t.shape, lambda i: (0, 0, 0)),
                  pl.BlockSpec((1, ATTN_WIDTH), lambda i: (0, 0))],
        out_specs=(pl.BlockSpec((tq, ATTN_WIDTH), cur), pl.BlockSpec((tq, ATTN_WIDTH), cur)),
        out_shape=(pltpu.HBM((T,ATTN_WIDTH), BF16), pltpu.HBM((T,ATTN_WIDTH), BF16)),
        scratch_shapes=[pltpu.VMEM((tq + BLOCK, KV_WIDTH), BF16), pltpu.VMEM((tq + BLOCK, KV_WIDTH), BF16),
                        pltpu.VMEM((tq, ATTN_WIDTH), F32),
                        pltpu.VMEM((2 * BLOCK, lanes), F32), pltpu.VMEM((2 * BLOCK, lanes), BF16)],
        compiler_params=_params("arbitrary"),
    )(sinks, *_hbm(q, k, k, v, v, bias_t, gain))


def _attn_bwd(dmixed, raw, q, k, v, bias_t, sinks, gain, name, nblk):
    T = q.shape[0]
    tq = nblk * BLOCK
    nt = T // tq
    lanes = N_Q_HEADS * BLOCK

    def body(sink_ref, dm_ref, raw_ref, q_ref, kc_ref, kp_ref, vc_ref, vp_ref, bias_ref, g_ref,
             dq_ref, dk_ref, dv_ref, dbias_ref, dsink_ref, dgain_ref,
             do_sc, dq_sc, kf_sc, vf_sc, dkf_sc, dvf_sc, st_sc, dpt_sc, pt_sc, dst_sc, drow_sc,
             qs_sc, dos_sc, dsink_sc):
        i = pl.program_id(0)
        tile = nt - 1 - i

        @pl.when(i == 0)
        def _():
            dkf_sc[...] = jnp.zeros_like(dkf_sc)
            dvf_sc[...] = jnp.zeros_like(dvf_sc)
            dsink_sc[...] = jnp.zeros_like(dsink_sc)
            dbias_ref[...] = jnp.zeros_like(dbias_ref)
            dgain_ref[...] = jnp.zeros_like(dgain_ref)

        carry_k = dkf_sc[0:BLOCK, :]
        carry_v = dvf_sc[0:BLOCK, :]
        dkf_sc[0:tq, :] = jnp.zeros((tq, KV_WIDTH), F32)
        dvf_sc[0:tq, :] = jnp.zeros((tq, KV_WIDTH), F32)
        dkf_sc[tq:tq + BLOCK, :] = carry_k
        dvf_sc[tq:tq + BLOCK, :] = carry_v
        _load_band(kf_sc, vf_sc, kp_ref, kc_ref, vp_ref, vc_ref, tq)

        do, dgain = _rms_bwd(dm_ref[...].astype(F32), raw_ref[...].astype(F32), g_ref[...])
        dgain_ref[...] += dgain
        do_sc[...] = do
        ones = jnp.ones((SUBLANES, HEAD_DIM), BF16)

        def block(b, carry):
            r0 = pl.multiple_of(b * BLOCK, BLOCK)
            no_prev = (tile == 0) & (b == 0)
            for g in range(N_KV_HEADS):
                kb = kf_sc[pl.ds(r0, 2 * BLOCK), _head_cols(g)]
                vb = vf_sc[pl.ds(r0, 2 * BLOCK), _head_cols(g)]
                qg = _stack_heads(q_ref, r0, g, BF16)
                dog = _stack_heads(do_sc, r0, g, F32)
                prod = dog * _stack_heads(raw_ref, r0, g, F32)
                hi = prod.astype(BF16)
                lo = (prod - hi.astype(F32)).astype(BF16)
                drow_sc[:, _group_lanes(g)] = _dot_nt(ones, hi) + _dot_nt(ones, lo)
                dogb = dog.astype(BF16)
                qs_sc[g] = qg
                dos_sc[g] = dogb
                st_sc[:, _group_lanes(g)] = _dot_nt(kb, qg)
                dpt_sc[:, _group_lanes(g)] = _dot_nt(vb, dogb)
            for h in range(N_Q_HEADS):
                hl = _head_lanes(h)
                p, ps = _head_softmax(st_sc[:, hl], bias_ref[h], sink_ref[h], no_prev)
                rowdot = drow_sc[0:1, hl]
                ds = p * (dpt_sc[:, hl] - rowdot)
                dsink_sc[h:h + 1, :] += -(ps * rowdot)
                dbias_ref[h] += ds
                dst_sc[:, hl] = ds.astype(BF16)
                pt_sc[:, hl] = p.astype(BF16)
            for g in range(N_KV_HEADS):
                kb = kf_sc[pl.ds(r0, 2 * BLOCK), _head_cols(g)]
                dsg = dst_sc[:, _group_lanes(g)]
                _unstack_heads(dq_sc, r0, g, _dot_tn(dsg, kb) * SCALE)
                dkf_sc[pl.ds(r0, 2 * BLOCK), _head_cols(g)] += _dot(dsg, qs_sc[g]) * SCALE
                dvf_sc[pl.ds(r0, 2 * BLOCK), _head_cols(g)] += _dot(pt_sc[:, _group_lanes(g)], dos_sc[g])
            return carry

        lax.fori_loop(0, nblk, block, 0, unroll=True)
        dq_ref[...] = dq_sc[...].astype(BF16)
        dk_ref[...] = dkf_sc[BLOCK:BLOCK + tq, :].astype(BF16)
        dv_ref[...] = dvf_sc[BLOCK:BLOCK + tq, :].astype(BF16)

        @pl.when(i == nt - 1)
        def _():
            tot = jnp.sum(dsink_sc[...], axis=1, keepdims=True)
            dsink_ref[...] = jnp.broadcast_to(tot, dsink_ref.shape)

    cur = lambda i: (nt - 1 - i, 0)
    prev = lambda i: (jnp.maximum((nt - 1 - i) * nblk - 1, 0), 0)
    const2 = lambda i: (0, 0)
    const3 = lambda i: (0, 0, 0)
    return pl.pallas_call(
        body, name=name, grid=(nt,),
        in_specs=[pl.BlockSpec(memory_space=pltpu.SMEM),
                  pl.BlockSpec((tq, ATTN_WIDTH), cur),
                  pl.BlockSpec((tq, ATTN_WIDTH), cur),
                  pl.BlockSpec((tq, ATTN_WIDTH), cur),
                  pl.BlockSpec((tq, KV_WIDTH), cur), pl.BlockSpec((BLOCK, KV_WIDTH), prev),
                  pl.BlockSpec((tq, KV_WIDTH), cur), pl.BlockSpec((BLOCK, KV_WIDTH), prev),
                  pl.BlockSpec(bias_t.shape, const3),
                  pl.BlockSpec((1, ATTN_WIDTH), const2)],
        out_specs=(pl.BlockSpec((tq, ATTN_WIDTH), cur),
                   pl.BlockSpec((tq, KV_WIDTH), cur), pl.BlockSpec((tq, KV_WIDTH), cur),
                   pl.BlockSpec(bias_t.shape, const3),
                   pl.BlockSpec((N_Q_HEADS, LANES), const2),
                   pl.BlockSpec((1, ATTN_WIDTH), const2)),
        out_shape=(pltpu.HBM((T,ATTN_WIDTH), BF16),
                   pltpu.HBM((T,KV_WIDTH), BF16), pltpu.HBM((T,KV_WIDTH), BF16),
                   jax.ShapeDtypeStruct(bias_t.shape, F32),
                   jax.ShapeDtypeStruct((N_Q_HEADS, LANES), F32),
                   jax.ShapeDtypeStruct((1, ATTN_WIDTH), F32)),
        scratch_shapes=[pltpu.VMEM((tq, ATTN_WIDTH), F32), pltpu.VMEM((tq, ATTN_WIDTH), F32),
                        pltpu.VMEM((tq + BLOCK, KV_WIDTH), BF16), pltpu.VMEM((tq + BLOCK, KV_WIDTH), BF16),
                        pltpu.VMEM((tq + BLOCK, KV_WIDTH), F32), pltpu.VMEM((tq + BLOCK, KV_WIDTH), F32),
                        pltpu.VMEM((2 * BLOCK, lanes), F32), pltpu.VMEM((2 * BLOCK, lanes), F32),
                        pltpu.VMEM((2 * BLOCK, lanes), BF16), pltpu.VMEM((2 * BLOCK, lanes), BF16),
                        pltpu.VMEM((SUBLANES, lanes), F32),
                        pltpu.VMEM((N_KV_HEADS, GROUP_ROWS, HEAD_DIM), BF16),
                        pltpu.VMEM((N_KV_HEADS, GROUP_ROWS, HEAD_DIM), BF16),
                        pltpu.VMEM((N_Q_HEADS, LANES), F32)],
        compiler_params=_params("arbitrary"),
    )(sinks, *_hbm(dmixed, raw, q, k, k, v, v, bias_t, gain))


def _shift_down(cu, tail):
    row = lax.broadcasted_iota(jnp.int32, cu.shape, 0)
    t6, t7 = tail[6:7, :], tail[7:8, :]
    s1 = jnp.where(row == 0, t7, pltpu.roll(cu, 1, 0))
    s2 = jnp.where(row == 0, t6, jnp.where(row == 1, t7, pltpu.roll(cu, 2, 0)))
    return s1, s2


def _shift_up(d, head):
    n = d.shape[0]
    row = lax.broadcasted_iota(jnp.int32, d.shape, 0)
    h0, h1 = head[0:1, :], head[1:2, :]
    s1 = jnp.where(row == n - 1, h0, pltpu.roll(d, n - 1, 0))
    s2 = jnp.where(row == n - 1, h1, jnp.where(row == n - 2, h0, pltpu.roll(d, n - 2, 0)))
    return s1, s2


def _mixout_fwd(x, attn_n, u, gb, gc, conv_w, gain, w_out, name, tm):
    T, D = x.shape

    def body(x_ref, an_ref, u_ref, b_ref, c_ref, cw_ref, g_ref, wo_ref, xo_ref, cn_ref, tail_sc):
        @pl.when(pl.program_id(0) == 0)
        def _():
            tail_sc[...] = jnp.zeros_like(tail_sc)

        cu = c_ref[...].astype(F32) * u_ref[...].astype(F32)
        s1, s2 = _shift_down(cu, tail_sc[...])
        tail_sc[...] = cu[tm - SUBLANES:tm, :]
        pre = cw_ref[0:1, :] * s2 + cw_ref[1:2, :] * s1 + cw_ref[2:3, :] * cu
        conv = b_ref[...].astype(F32) * pre
        _, chat = _rms_stats(conv)
        cn = (chat * g_ref[...]).astype(BF16)
        cn_ref[...] = cn
        xo_ref[...] = (x_ref[...] + _dot(an_ref[...], wo_ref[0:ATTN_WIDTH, :])
                       + _dot(cn, wo_ref[ATTN_WIDTH:ATTN_WIDTH + CONV_DIM, :]))

    row = lambda i: (i, 0)
    const = lambda i: (0, 0)
    return pl.pallas_call(
        body, name=name, grid=(T // tm,),
        in_specs=[pl.BlockSpec((tm, D), row), pl.BlockSpec((tm, ATTN_WIDTH), row),
                  pl.BlockSpec((tm, CONV_DIM), row), pl.BlockSpec((tm, CONV_DIM), row),
                  pl.BlockSpec((tm, CONV_DIM), row),
                  pl.BlockSpec(conv_w.shape, const), pl.BlockSpec((1, CONV_DIM), const),
                  pl.BlockSpec(w_out.shape, const)],
        out_specs=(pl.BlockSpec((tm, D), row), pl.BlockSpec((tm, CONV_DIM), row)),
        out_shape=(pltpu.HBM((T,D), F32), pltpu.HBM((T,CONV_DIM), BF16)),
        scratch_shapes=[pltpu.VMEM((SUBLANES, CONV_DIM), F32)],
        compiler_params=_params("arbitrary"),
    )(*_hbm(x, attn_n, u, gb, gc, conv_w, gain, w_out))


def _mixout_bwd(dy, attn_n, conv_n, w_out, after, name, tm):
    T, D = dy.shape
    W = ATTN_WIDTH + CONV_DIM
    nt = T // tm

    def body(dy_ref, an_ref, cn_ref, w_ref, after_ref, dm_ref, dw_ref, dw_sc):
        i = pl.program_id(0)

        @pl.when(i == 0)
        def _():
            dw_sc[...] = jnp.zeros_like(dw_sc)

        dyb = dy_ref[...].astype(BF16)
        dm_ref[...] = _dot_nt(dyb, w_ref[...]).astype(BF16)
        dw_sc[0:ATTN_WIDTH, :] += _dot_tn(an_ref[...], dyb)
        dw_sc[ATTN_WIDTH:W, :] += _dot_tn(cn_ref[...], dyb)

        @pl.when(i == nt - 1)
        def _():
            dw_ref[...] = dw_sc[...].astype(BF16)

    row = lambda i: (i, 0)
    const = lambda i: (0, 0)
    return pl.pallas_call(
        body, name=name, grid=(nt,),
        in_specs=[pl.BlockSpec((tm, D), row), pl.BlockSpec((tm, ATTN_WIDTH), row),
                  pl.BlockSpec((tm, CONV_DIM), row), pl.BlockSpec(w_out.shape, const),
                  pl.BlockSpec(memory_space=pl.ANY)],
        out_specs=(pl.BlockSpec((tm, W), row), pl.BlockSpec((W, D), const)),
        out_shape=(pltpu.HBM((T,W), BF16), pltpu.HBM((W, D), BF16)),
        scratch_shapes=[pltpu.VMEM((W, D), F32)],
        compiler_params=_params("arbitrary"),
    )(*_hbm(dy, attn_n, conv_n, w_out, after))


def _conv_bwd(dmixed, u, gb, gc, conv_w, gain, name, tc):
    T = u.shape[0]
    nt = T // tc
    per8 = tc // SUBLANES

    def body(dm_ref, u_ref, b_ref, c_ref, ut_ref, ct_ref, cw_ref, g_ref,
             du_ref, db_ref, dc_ref, dcw_ref, dgain_ref, head_sc):
        i = pl.program_id(0)

        @pl.when(i == 0)
        def _():
            head_sc[...] = jnp.zeros_like(head_sc)
            dcw_ref[...] = jnp.zeros_like(dcw_ref)
            dgain_ref[...] = jnp.zeros_like(dgain_ref)

        uu = u_ref[...].astype(F32)
        cc = c_ref[...].astype(F32)
        bb = b_ref[...].astype(F32)
        cu = cc * uu
        tail = jnp.where(i == nt - 1, 0.0, ct_ref[...].astype(F32) * ut_ref[...].astype(F32))
        s1, s2 = _shift_down(cu, tail)
        w0, w1, w2 = cw_ref[0:1, :], cw_ref[1:2, :], cw_ref[2:3, :]
        pre = w0 * s2 + w1 * s1 + w2 * cu
        dconv, dgain = _rms_bwd(dm_ref[...].astype(F32), bb * pre, g_ref[...])
        dgain_ref[...] += dgain
        db_ref[...] = (dconv * pre).astype(BF16)
        dpre = dconv * bb
        dcw_ref[0:1, :] += jnp.sum(dpre * s2, axis=0, keepdims=True)
        dcw_ref[1:2, :] += jnp.sum(dpre * s1, axis=0, keepdims=True)
        dcw_ref[2:3, :] += jnp.sum(dpre * cu, axis=0, keepdims=True)
        n1, n2 = _shift_up(dpre, head_sc[...])
        head_sc[...] = dpre[0:SUBLANES, :]
        dcu = w2 * dpre + w1 * n1 + w0 * n2
        du_ref[...] = (dcu * cc).astype(BF16)
        dc_ref[...] = (dcu * uu).astype(BF16)

    rev = lambda i: (nt - 1 - i, 0)
    rev_right = lambda i: (nt - 1 - i, 1)
    tail_map = lambda i: (jnp.maximum((nt - 1 - i) * per8 - 1, 0), 0)
    const = lambda i: (0, 0)
    return pl.pallas_call(
        body, name=name, grid=(nt,),
        in_specs=[pl.BlockSpec((tc, CONV_DIM), rev_right),
                  pl.BlockSpec((tc, CONV_DIM), rev), pl.BlockSpec((tc, CONV_DIM), rev),
                  pl.BlockSpec((tc, CONV_DIM), rev),
                  pl.BlockSpec((SUBLANES, CONV_DIM), tail_map), pl.BlockSpec((SUBLANES, CONV_DIM), tail_map),
                  pl.BlockSpec(conv_w.shape, const), pl.BlockSpec((1, CONV_DIM), const)],
        out_specs=(pl.BlockSpec((tc, CONV_DIM), rev), pl.BlockSpec((tc, CONV_DIM), rev),
                   pl.BlockSpec((tc, CONV_DIM), rev),
                   pl.BlockSpec((SUBLANES, CONV_DIM), const), pl.BlockSpec((1, CONV_DIM), const)),
        out_shape=(pltpu.HBM((T,CONV_DIM), BF16), pltpu.HBM((T,CONV_DIM), BF16),
                   pltpu.HBM((T,CONV_DIM), BF16),
                   jax.ShapeDtypeStruct((SUBLANES, CONV_DIM), F32), jax.ShapeDtypeStruct((1, CONV_DIM), F32)),
        scratch_shapes=[pltpu.VMEM((SUBLANES, CONV_DIM), F32)],
        compiler_params=_params("arbitrary"),
    )(*_hbm(dmixed, u, gb, gc, u, gc, conv_w, gain))


def _mixin_bwd(dy, x, gain, dz, w_in_t, name, tm):
    T, D = x.shape
    nz = len(dz)

    def body(dy_ref, x_ref, g_ref, *rest):
        dz_refs, wt_ref, dx_ref, dgain_ref, dz_sc = rest[:nz], rest[nz], rest[nz + 1], rest[nz + 2], rest[nz + 3]

        @pl.when(pl.program_id(0) == 0)
        def _():
            dgain_ref[...] = jnp.zeros_like(dgain_ref)

        for r, lo, hi in zip(dz_refs, _Z_SPLITS[:-1], _Z_SPLITS[1:]):
            dz_sc[:, lo:hi] = r[...]
        dx, dgain = _rms_bwd(_dot(dz_sc[...], wt_ref[...]), x_ref[...], g_ref[...])
        dgain_ref[...] += dgain
        dx_ref[...] = dy_ref[...] + dx

    row = lambda i: (i, 0)
    const = lambda i: (0, 0)
    return pl.pallas_call(
        body, name=name, grid=(T // tm,),
        in_specs=[pl.BlockSpec((tm, D), row), pl.BlockSpec((tm, D), row), pl.BlockSpec((1, D), const)]
                 + [pl.BlockSpec((tm, a.shape[1]), row) for a in dz]
                 + [pl.BlockSpec(w_in_t.shape, const)],
        out_specs=(pl.BlockSpec((tm, D), row), pl.BlockSpec((1, D), const)),
        out_shape=(pltpu.HBM((T,D), F32), jax.ShapeDtypeStruct((1, D), F32)),
        scratch_shapes=[pltpu.VMEM((tm, _Z_SPLITS[-1]), BF16)],
        compiler_params=_params("arbitrary"),
    )(*_hbm(dy, x, gain, *dz, w_in_t))


def _win_grad(dz, hn, name, tk):
    T, D = hn.shape
    nz = len(dz)
    nt = T // tk
    W = _Z_SPLITS[-1]

    def body(hn_ref, *rest):
        dz_refs, dw_ref, dw_sc, kv_sc = rest[:nz], rest[nz], rest[nz + 1], rest[nz + 2]
        i = pl.program_id(0)

        @pl.when(i == 0)
        def _():
            dw_sc[...] = jnp.zeros_like(dw_sc)

        hn = hn_ref[...]
        kv_sc[:, :KV_WIDTH] = dz_refs[1][...]
        kv_sc[:, KV_WIDTH:] = dz_refs[2][...]
        dw_sc[_Z_SPLITS[1]:_Z_SPLITS[3], :] += _dot_tn(kv_sc[...], hn)
        for n in (0, 3, 4, 5):
            dw_sc[_Z_SPLITS[n]:_Z_SPLITS[n + 1], :] += _dot_tn(dz_refs[n][...], hn)

        @pl.when(i == nt - 1)
        def _():
            dw_ref[...] = dw_sc[...].astype(BF16)

    row = lambda i: (i, 0)
    return pl.pallas_call(
        body, name=name, grid=(nt,),
        in_specs=[pl.BlockSpec((tk, D), row)] + [pl.BlockSpec((tk, a.shape[1]), row) for a in dz],
        out_specs=pl.BlockSpec((W, D), lambda i: (0, 0)),
        out_shape=pltpu.HBM((W, D), BF16),
        scratch_shapes=[pltpu.VMEM((W, D), F32), pltpu.VMEM((tk, 2 * KV_WIDTH), BF16)],
        compiler_params=_params("arbitrary"),
    )(*_hbm(hn, *dz))


def _adamw(parts, w, m, v, name, tr):
    P = parts.shape[0]
    R, C = w.shape

    def body(p_ref, w_ref, m_ref, v_ref, g_ref, d_ref, nm_ref, nv_ref):
        g = p_ref[0].astype(F32)
        for d in range(1, P):
            g = g + p_ref[d].astype(F32)
        nm = ADAM_B1 * m_ref[...] + (1.0 - ADAM_B1) * g
        nv = ADAM_B2 * v_ref[...] + (1.0 - ADAM_B2) * (g * g)
        m_hat = nm / (1.0 - ADAM_B1 ** ADAM_STEP)
        v_hat = nv / (1.0 - ADAM_B2 ** ADAM_STEP)
        g_ref[...] = g
        nm_ref[...] = nm
        nv_ref[...] = nv
        d_ref[...] = -ADAM_LR * (m_hat / (jnp.sqrt(v_hat) + ADAM_EPS) + ADAM_WD * w_ref[...])

    row = lambda i: (i, 0)
    spec = pl.BlockSpec((tr, C), row)
    shp = pltpu.HBM((R, C), F32)
    return pl.pallas_call(
        body, name=name, grid=(R // tr,),
        in_specs=[pl.BlockSpec((P, tr, C), lambda i: (0, i, 0)), spec, spec, spec],
        out_specs=(spec, spec, spec, spec),
        out_shape=(shp, shp, shp, shp),
        compiler_params=_params("arbitrary"),
    )(*_hbm(parts, w, m, v))


def _columns_of_blocks(g):
    n, R, w = g.shape
    return g.transpose(1, 0, 2).reshape(R, n * w)


def _pad_row(vec):
    vec = vec.reshape(1, -1)
    return jnp.pad(vec, ((0, 0), (0, PACK_COLS - vec.shape[1])))


def kernel(x, rel_bias_table, ffn1_norm, ffn1_w_gate, ffn1_w_up, ffn1_w_down, mix_norm, w_in, conv_w, attn_sinks, attn_out_norm, conv_out_norm, w_out, ffn2_norm, ffn2_w_gate, ffn2_w_up, ffn2_w_down, final_norm, loss_target, m_rel_bias_table, m_ffn1_norm, m_ffn1_w_gate, m_ffn1_w_up, m_ffn1_w_down, m_mix_norm, m_w_in, m_conv_w, m_attn_sinks, m_attn_out_norm, m_conv_out_norm, m_w_out, m_ffn2_norm, m_ffn2_w_gate, m_ffn2_w_up, m_ffn2_w_down, m_final_norm, v_rel_bias_table, v_ffn1_norm, v_ffn1_w_gate, v_ffn1_w_up, v_ffn1_w_down, v_mix_norm, v_w_in, v_conv_w, v_attn_sinks, v_attn_out_norm, v_conv_out_norm, v_w_out, v_ffn2_norm, v_ffn2_w_gate, v_ffn2_w_up, v_ffn2_w_down, v_final_norm):
    T, D = x.shape[1], x.shape[2]
    x0 = x[0]
    target = loss_target[0]
    tm = min(TM_FFN, T)
    tm_bwd = min(TM_FFN_BWD, T)
    tm_mix = min(TM_MIX, T)
    tm_wide = min(TM_MIX_WIDE, T)
    tk = min(TK_WGRAD, T)
    tf = TF_FFN
    nblk = min(ATTN_BLOCKS, T // BLOCK)
    me = 4 * lax.axis_index("x") + 2 * lax.axis_index("y") + lax.axis_index("c")

    big = {
        "ffn1_w_gate": (ffn1_w_gate[0], m_ffn1_w_gate[0], v_ffn1_w_gate[0], True),
        "ffn1_w_up": (ffn1_w_up[0], m_ffn1_w_up[0], v_ffn1_w_up[0], True),
        "ffn1_w_down": (ffn1_w_down[0], m_ffn1_w_down[0], v_ffn1_w_down[0], False),
        "w_in": (w_in[0], m_w_in[0], v_w_in[0], True),
        "w_out": (w_out[0], m_w_out[0], v_w_out[0], False),
        "ffn2_w_gate": (ffn2_w_gate[0], m_ffn2_w_gate[0], v_ffn2_w_gate[0], True),
        "ffn2_w_up": (ffn2_w_up[0], m_ffn2_w_up[0], v_ffn2_w_up[0], True),
        "ffn2_w_down": (ffn2_w_down[0], m_ffn2_w_down[0], v_ffn2_w_down[0], False),
    }

    def block_to_send(name):
        w, _, _, transposed = big[name]
        return (w.T if transposed else w).astype(BF16)

    names1 = ["ffn1_w_gate", "ffn1_w_up", "ffn1_w_down"]
    names_rest = ["w_in", "w_out", "ffn2_w_gate", "ffn2_w_up", "ffn2_w_down"]
    first = _gather_two_level([block_to_send(n) for n in names1], "gather_ffn1", ffn1_norm)
    wgt1, wut1, wd1 = [g.reshape(-1, D) for g in first]
    h_rest, token = _exchange_start([block_to_send(n) for n in names_rest[:2]] + [conv_w[0]]
                                    + [block_to_send(n) for n in names_rest[2:]], False,
                                    "gather_start_rest", wd1)

    wgu1 = _stack_gate_up(wgt1, wut1, tf)
    x1, xn1, gu1 = _ffn_fwd(x0, ffn1_norm, wgu1, wd1, token, "ffn1_fwd", tm, tf)
    mixw = _exchange_wait(h_rest[:3], False, "gather_wait_mix", x1)
    win_t = mixw[0].reshape(-1, D)
    wout = mixw[1].reshape(-1, D)
    cw = _columns_of_blocks(mixw[2])
    hn, q, k, v, u, gb, gc = _mixin_fwd(x1, mix_norm, win_t, "mixin_fwd", tm_wide)
    bucket = jnp.asarray(_bucket_table().T.copy())
    sinks = attn_sinks.reshape(-1)
    bias_t = _bias_build(rel_bias_table, bucket, "bias_build")
    attn_raw, attn_n = _attn_fwd(q, k, v, bias_t, sinks, attn_out_norm, "attn_fwd", nblk)
    x2, conv_n = _mixout_fwd(x1, attn_n, u, gb, gc, cw, conv_out_norm, wout, "mixout_fwd", tm_mix)
    wgt2, wut2, wd2 = [g.reshape(-1, D) for g in _exchange_wait(h_rest[3:], False, "gather_wait_ffn2", x2)]
    wgu2 = _stack_gate_up(wgt2, wut2, tf)
    dx3, xn2, gu2, d_final, loss_part = _ffn_fwd(x2, ffn2_norm, wgu2, wd2, x2, "ffn2_fwd", tm, tf,
                                                 head=(target, final_norm.reshape(1, D)))

    def blocks(g):
        return g.reshape(N_DEV, -1, D)

    dx2, d_ffn2_norm, dg2, du2, hh2, do2 = _ffn_bwd(
        dx3, x2, ffn2_norm, gu2, wgu2, wd2, dx3, "ffn2_bwd", tm_bwd, tf)
    d_wg2 = _tn_grad(dg2, xn2, dx2, "ffn2_wgrad_gate", BM_WGRAD, tk)
    d_wu2 = _tn_grad(du2, xn2, dx2, "ffn2_wgrad_up", BM_WGRAD, tk)
    d_wd2 = _tn_grad(hh2, do2, dx2, "ffn2_wgrad_down", BM_WGRAD, tk)
    handles2, token2 = _exchange_start([blocks(d_wg2), blocks(d_wu2), blocks(d_wd2)], True,
                                       "grads_start_ffn2", dx2)

    dmixed, d_wout = _mixout_bwd(dx2, attn_n, conv_n, wout, token2, "mixout_bwd", tm_wide)
    dq, dk, dv, dbias, dsink, d_attn_norm = _attn_bwd(
        dmixed, attn_raw, q, k, v, bias_t, sinks, attn_out_norm, "attn_bwd", nblk)
    du, dgb, dgc, d_cw, d_conv_norm = _conv_bwd(dmixed, u, gb, gc, cw, conv_out_norm, "conv_bwd", tm_wide)
    d_table = _bias_grad(dbias, bucket, "bias_grad")
    dz = [dq, dk, dv, du, dgb, dgc]
    dx1, d_mix_norm = _mixin_bwd(dx2, x1, mix_norm, dz, win_t, "mixin_bwd", tm_mix)
    d_win_t = _win_grad(dz, hn, "win_grad", tm_wide)
    handles_mix, token_mix = _exchange_start([blocks(d_win_t), blocks(d_wout)], True, "grads_start_mix", d_table)

    dx0, d_ffn1_norm, dg1, du1, hh1, do1 = _ffn_bwd(
        dx1, x0, ffn1_norm, gu1, wgu1, wd1, token_mix, "ffn1_bwd", tm_bwd, tf)
    def pack(ffn1, mixn, ffn2, fin, attn_n_, conv_n_, sink_, extra, convw, table):
        rows = [_pad_row(ffn1), _pad_row(mixn), _pad_row(ffn2), _pad_row(fin),
                _pad_row(jnp.concatenate([attn_n_.reshape(-1), conv_n_.reshape(-1)])),
                _pad_row(sink_), _pad_row(extra),
                jnp.zeros((1, PACK_COLS), F32),
                jnp.pad(convw, ((0, 0), (0, PACK_COLS - convw.shape[1]))),
                _pad_row(table),
                jnp.zeros((PACK_ROWS - 12, PACK_COLS), F32)]
        return jnp.concatenate(rows, axis=0)

    def own_channels(a):
        full = jnp.zeros((a.shape[1], CONV_DIM), F32)
        return lax.dynamic_update_slice(full, a[0], (0, me * a.shape[2]))

    g_pack = pack(d_ffn1_norm, d_mix_norm, d_ffn2_norm, d_final, d_attn_norm, d_conv_norm,
                  dsink[:, 0], loss_part[0, :1], d_cw[:3], d_table[:, :N_Q_HEADS])
    zero1 = jnp.zeros((1,), F32)
    w_pack = pack(ffn1_norm, mix_norm, ffn2_norm, final_norm, attn_out_norm, conv_out_norm,
                  attn_sinks, zero1, own_channels(conv_w), rel_bias_table)
    m_pack = pack(m_ffn1_norm, m_mix_norm, m_ffn2_norm, m_final_norm, m_attn_out_norm, m_conv_out_norm,
                  m_attn_sinks, zero1, own_channels(m_conv_w), m_rel_bias_table)
    v_pack = pack(v_ffn1_norm, v_mix_norm, v_ffn2_norm, v_final_norm, v_attn_out_norm, v_conv_out_norm,
                  v_attn_sinks, zero1, own_channels(v_conv_w), v_rel_bias_table)
    h_small, token_small = _exchange_start([g_pack], False, "gather_small_start", dx0)

    d_wd1 = _tn_grad(hh1, do1, token_small, "ffn1_wgrad_down", BM_WGRAD, tk)
    handles1_d, token1 = _exchange_start([blocks(d_wd1)], True, "grads_start_ffn1_down", dx0)
    d_wg1 = _tn_grad(dg1, xn1, token1, "ffn1_wgrad_gate", BM_WGRAD, tk)
    handles1_g, token1 = _exchange_start([blocks(d_wg1)], True, "grads_start_ffn1_gate", token1)
    d_wu1 = _tn_grad(du1, xn1, token1, "ffn1_wgrad_up", BM_WGRAD, tk)
    handles1_u, token1 = _exchange_start([blocks(d_wu1)], True, "grads_start_ffn1_up", token1)

    res = {}
    made = []

    def update(names, parts):
        last = None
        for name, p in zip(names, parts):
            w, m_, v_, transposed = big[name]
            if transposed:
                w, m_, v_ = w.T, m_.T, v_.T
            new = _adamw(p, w, m_, v_, "adamw_" + name, _row_tile(w.shape[0], ADAM_ROWS))
            res[name] = tuple((a.T if transposed else a)[None] for a in new)
            made.extend(new)
            last = new[0]
        return last

    parts2 = _exchange_wait(handles2, True, "grads_wait_ffn2", token1)
    done2 = update(["ffn2_w_gate", "ffn2_w_up", "ffn2_w_down"], parts2)
    parts_mix = _exchange_wait(handles_mix, True, "grads_wait_mix", done2)
    done_mix = update(["w_in", "w_out"], parts_mix)
    (g_all,) = _exchange_wait(h_small, False, "gather_small_wait", done_mix)
    packs = _adamw(g_all, w_pack, m_pack, v_pack, "adamw_small", PACK_ROWS)

    def unpack(pk):
        cwb = lax.dynamic_slice(pk[8:11, :CONV_DIM], (0, me * conv_w.shape[2]), (3, conv_w.shape[2]))
        return {
            "ffn1_norm": pk[0:1, :D], "mix_norm": pk[1:2, :D], "ffn2_norm": pk[2:3, :D],
            "final_norm": pk[3, :D],
            "attn_out_norm": pk[4:5, :ATTN_WIDTH], "conv_out_norm": pk[4:5, ATTN_WIDTH:ATTN_WIDTH + CONV_DIM],
            "attn_sinks": pk[5:6, :N_Q_HEADS],
            "conv_w": cwb[None],
            "rel_bias_table": pk[11, :NUM_BUCKETS * N_Q_HEADS].reshape(NUM_BUCKETS, N_Q_HEADS),
        }

    small = [unpack(pk) for pk in packs]
    loss = packs[0][6, 0]

    ready = list(made) + [a for group in small for a in group.values()]
    parts1_d = _exchange_wait(handles1_d, True, "grads_wait_ffn1_down", ready)
    done1 = update(["ffn1_w_down"], parts1_d)
    parts1_g = _exchange_wait(handles1_g, True, "grads_wait_ffn1_gate", done1)
    done1 = update(["ffn1_w_gate"], parts1_g)
    parts1_u = _exchange_wait(handles1_u, True, "grads_wait_ffn1_up", done1)
    update(["ffn1_w_up"], parts1_u)

    order = ["rel_bias_table", "ffn1_norm", "ffn1_w_gate", "ffn1_w_up", "ffn1_w_down", "mix_norm", "w_in",
             "conv_w", "attn_sinks", "attn_out_norm", "conv_out_norm", "w_out", "ffn2_norm",
             "ffn2_w_gate", "ffn2_w_up", "ffn2_w_down", "final_norm"]
    outs = [loss, dx0[None]]
    for kind in range(4):
        for name in order:
            outs.append(res[name][kind] if name in res else small[kind][name])
    return tuple(outs)
```

```python
import math

import numpy as np
import jax
import jax.numpy as jnp
from jax import lax
from jax.experimental import pallas as pl
from jax.experimental.pallas import tpu as pltpu

F32 = jnp.float32
BF16 = jnp.bfloat16

N_DEV = 8
EPS = 1e-6
HEAD_DIM = 64
N_Q_HEADS = 8
N_KV_HEADS = 2
GQA_GROUP = 4
ATTN_WIDTH = 512
KV_WIDTH = 128
CONV_DIM = 512
BLOCK = 128
WINDOW = 128
NUM_BUCKETS = 32
MAX_DISTANCE = 128
SCALE = HEAD_DIM ** -0.5
MASKED = -1e30
GROUP_ROWS = GQA_GROUP * BLOCK

ADAM_LR = 0.001
ADAM_B1 = 0.9
ADAM_B2 = 0.999
ADAM_EPS = 1e-08
ADAM_WD = 0.01
ADAM_STEP = 10

VMEM_LIMIT_BYTES = 40 * 1024 * 1024
VMEM_LIMIT_FFN_BYTES = 50 * 1024 * 1024
SUBLANES = 8
LANES = 128
PACK_ROWS = 16
PACK_COLS = 1024

TM_FFN = 1024
TM_FFN_BWD = 1024
TM_MIX = 512
TM_MIX_WIDE = 1024
TK_WGRAD = 2048
TF_FFN = 256
BM_WGRAD = 1408
ROW_GROUPS = 4
ATTN_BLOCKS = 8
ADAM_ROWS = 256


def _row_tile(rows, limit):
    best = rows
    for t in range(16, min(rows, limit) + 1, 16):
        if rows % t == 0:
            best = t
    return best


def _params(*sem):
    return pltpu.CompilerParams(dimension_semantics=sem, vmem_limit_bytes=VMEM_LIMIT_BYTES)


def _hbm(*arrays):
    return [pltpu.with_memory_space_constraint(a, pltpu.HBM) for a in arrays]


def _dot(a, b):
    return jnp.dot(a, b, preferred_element_type=F32)


def _dot_nt(a, b):
    return lax.dot_general(a, b, (((1,), (1,)), ((), ())), preferred_element_type=F32)


def _dot_tn(a, b):
    return lax.dot_general(a, b, (((0,), (0,)), ((), ())), preferred_element_type=F32)


def _sigmoid(g):
    return 0.5 * jnp.tanh(0.5 * g) + 0.5


def _rms_stats(x):
    inv = lax.rsqrt(jnp.mean(x * x, axis=-1, keepdims=True) + EPS)
    return inv, x * inv


def _rms_bwd(dy, x, gain):
    inv, xhat = _rms_stats(x)
    dgain = jnp.sum(dy * xhat, axis=0, keepdims=True)
    dxh = dy * gain
    dx = inv * (dxh - xhat * jnp.mean(dxh * xhat, axis=-1, keepdims=True))
    return dx, dgain


def _peer_list():
    x, y, c = lax.axis_index("x"), lax.axis_index("y"), lax.axis_index("c")
    peers = []
    for k in range(1, N_DEV):
        px = 1 - x if (k >> 2) & 1 else x
        py = 1 - y if (k >> 1) & 1 else y
        pc = 1 - c if k & 1 else c
        peers.append((px, py, pc))
    return 4 * x + 2 * y + c, peers


def _gather_two_level(arrs, name, after):
    n = len(arrs)
    out_shape = tuple(jax.ShapeDtypeStruct((N_DEV,) + a.shape, a.dtype) for a in arrs)

    def body(*refs):
        ins, outs = refs[:n], refs[n + 1:2 * n + 1]
        send_sems, recv_sems, local_sems = refs[2 * n + 1:]
        x, y, c = lax.axis_index("x"), lax.axis_index("y"), lax.axis_index("c")
        me, sibling = (x, y, c), (x, y, 1 - c)
        chips = [(1 - x, y), (x, 1 - y), (1 - x, 1 - y)]

        def copy(a, k, block, to, src=None):
            slot = outs[a].at[4 * block[0] + 2 * block[1] + block[2]]
            return pltpu.make_async_remote_copy(
                src_ref=slot if src is None else src, dst_ref=slot,
                send_sem=send_sems.at[a, k], recv_sem=recv_sems.at[a, k],
                device_id=to, device_id_type=pl.DeviceIdType.MESH)

        started = []
        for a in range(n):
            loc = pltpu.make_async_copy(ins[a], outs[a].at[4 * x + 2 * y + c], local_sems.at[a])
            loc.start()
            started.append(loc)
        south = c == 0
        relay_from = (jnp.where(south, 1 - x, x), jnp.where(south, y, 1 - y), c)
        relay_to = (jnp.where(south, x, 1 - x), jnp.where(south, 1 - y, y), c)
        sends = []
        for a in range(n):
            sends.append(copy(a, 0, me, sibling, src=ins[a]))
            sends += [copy(a, 1 + j, me, (*chips[j], c), src=ins[a]) for j in range(2)]
        for cp in sends:
            cp.start()
        for a in range(n):
            for j in range(2):
                copy(a, 1 + j, (*chips[j], c), me).wait_recv()
                fwd = copy(a, 4 + j, (*chips[j], c), sibling)
                fwd.start()
                sends.append(fwd)
            relay = copy(a, 3, relay_from, relay_to)
            relay.start()
            sends.append(relay)
        for a in range(n):
            copy(a, 3, (*chips[2], c), me).wait_recv()
            fwd = copy(a, 6, (*chips[2], c), sibling)
            fwd.start()
            sends.append(fwd)
        for a in range(n):
            copy(a, 0, sibling, me).wait_recv()
            for j, chip in enumerate(chips):
                copy(a, 4 + j, (*chip, 1 - c), me).wait_recv()
        for cp in sends:
            cp.wait_send()
        for loc in started:
            loc.wait()

    hbm = pl.BlockSpec(memory_space=pl.ANY)
    return pl.pallas_call(
        body, name=name, out_shape=out_shape,
        in_specs=[hbm] * (n + 1), out_specs=tuple([hbm] * n),
        scratch_shapes=[pltpu.SemaphoreType.DMA((n, N_DEV - 1)),
                        pltpu.SemaphoreType.DMA((n, N_DEV - 1)),
                        pltpu.SemaphoreType.DMA((n,))],
    )(*arrs, after)


_HBM = pl.BlockSpec(memory_space=pltpu.HBM)
_SEM = pl.BlockSpec(memory_space=pltpu.SEMAPHORE)
_EFFECT = pltpu.SideEffectType.DATAFLOW_SIDE_EFFECTING


def _split_copies(srcs, lands, send_sems, recv_sems, scatter):
    me, peers = _peer_list()
    copies = []
    for a in range(len(srcs)):
        for k, (px, py, pc) in enumerate(peers):
            src = srcs[a].at[4 * px + 2 * py + pc] if scatter else srcs[a]
            copies.append(pltpu.make_async_remote_copy(
                src_ref=src, dst_ref=lands[a].at[me],
                send_sem=send_sems[a].at[k], recv_sem=recv_sems[a].at[k],
                device_id=(px, py, pc), device_id_type=pl.DeviceIdType.MESH))
    return copies


def _own_copies(srcs, lands, recv_sems, scatter):
    me, _ = _peer_list()
    return [pltpu.make_async_copy(srcs[a].at[me] if scatter else srcs[a], lands[a].at[me],
                                  recv_sems[a].at[N_DEV - 1]) for a in range(len(srcs))]


def _exchange_start(arrs, scatter, name, after):
    n = len(arrs)
    lands = [lax.empty(a.shape if scatter else (N_DEV,) + a.shape, a.dtype) for a in arrs]

    def body(*refs):
        srcs, lnds = refs[:n], refs[n:2 * n]
        outs = refs[2 * n + 1:]
        send_sems, recv_sems = outs[:n], outs[n:2 * n]
        token = outs[4 * n]
        for cp in _split_copies(srcs, lnds, send_sems, recv_sems, scatter):
            cp.start()
        for cp in _own_copies(srcs, lnds, recv_sems, scatter):
            cp.start()
        token[...] = jnp.zeros_like(token)

    sem = pltpu.SemaphoreType.DMA((N_DEV,))
    out_shape = ([sem] * (2 * n) + [pltpu.HBM(a.shape, a.dtype) for a in arrs]
                 + [pltpu.HBM(l.shape, l.dtype) for l in lands] + [jax.ShapeDtypeStruct((SUBLANES, LANES), F32)])
    res = pl.pallas_call(
        body, name=name, out_shape=tuple(out_shape),
        in_specs=[_HBM] * (2 * n) + [pl.BlockSpec(memory_space=pl.ANY)],
        out_specs=tuple([_SEM] * (2 * n) + [_HBM] * (2 * n) + [pl.BlockSpec(memory_space=pltpu.VMEM)]),
        input_output_aliases={i: 2 * n + i for i in range(2 * n)},
        compiler_params=pltpu.CompilerParams(has_side_effects=_EFFECT),
    )(*[pltpu.with_memory_space_constraint(a, pltpu.HBM) for a in arrs],
      *[pltpu.with_memory_space_constraint(l, pltpu.HBM) for l in lands], after)
    handles = [(res[2 * n + a], res[3 * n + a], res[a], res[n + a]) for a in range(n)]
    return handles, res[4 * n]


def _exchange_wait(handles, scatter, name, after):
    n = len(handles)
    after = list(after) if isinstance(after, (list, tuple)) else [after]

    def body(*refs):
        srcs, lnds = refs[:n], refs[n:2 * n]
        send_sems, recv_sems = refs[2 * n:3 * n], refs[3 * n:4 * n]
        for cp in _split_copies(srcs, lnds, send_sems, recv_sems, scatter):
            cp.wait_send()
            cp.wait_recv()
        for cp in _own_copies(srcs, lnds, recv_sems, scatter):
            cp.wait()

    srcs = [h[0] for h in handles]
    lands = [h[1] for h in handles]
    res = pl.pallas_call(
        body, name=name,
        out_shape=tuple([pltpu.HBM(a.shape, a.dtype) for a in srcs] + [pltpu.HBM(l.shape, l.dtype) for l in lands]),
        in_specs=[_HBM] * (2 * n) + [_SEM] * (2 * n) + [pl.BlockSpec(memory_space=pl.ANY)] * len(after),
        out_specs=tuple([_HBM] * (2 * n)),
        input_output_aliases={i: i for i in range(2 * n)},
        compiler_params=pltpu.CompilerParams(has_side_effects=_EFFECT),
    )(*srcs, *lands, *[h[2] for h in handles], *[h[3] for h in handles], *after)
    return list(res[n:])


def _row_groups(tm):
    return [slice(r * (tm // ROW_GROUPS), (r + 1) * (tm // ROW_GROUPS)) for r in range(ROW_GROUPS)]


def _stack_gate_up(wgt, wut, tf):
    F, D = wgt.shape
    return jnp.stack([wgt.reshape(F // tf, tf, D), wut.reshape(F // tf, tf, D)], axis=1).reshape(2 * F, D)


def _ffn_fwd(x, gain, wgu, wd, after, name, tm, tf, head=None):
    T, D = x.shape
    F = wd.shape[0]
    nj = F // tf
    n_in = 7 if head else 5

    def body(*refs):
        x_ref, g_ref, wgu_ref, wd_ref, after_ref = refs[:5]
        outs = refs[n_in:]
        xo_ref, xn_ref, gu_ref = outs[:3]
        xn_sc, acc_sc = outs[-2:]
        i, j = pl.program_id(0), pl.program_id(1)

        @pl.when(j == 0)
        def _():
            _, xhat = _rms_stats(x_ref[...])
            xn = (xhat * g_ref[...]).astype(BF16)
            xn_sc[...] = xn
            xn_ref[...] = xn
            acc_sc[...] = jnp.zeros_like(acc_sc)

        groups = _row_groups(tm)
        gus = [_dot_nt(xn_sc[rows, :], wgu_ref[...]) for rows in groups]
        hs = []
        for rows, gu in zip(groups, gus):
            gu_ref[rows, :] = gu.astype(BF16)
            g, u = gu[:, :tf], gu[:, tf:]
            hs.append((g * _sigmoid(g) * u).astype(BF16))
        for rows, h in zip(groups, hs):
            acc_sc[rows, :] += _dot(h, wd_ref[...])

        if head:
            t_ref, fg_ref = refs[5:7]
            dgain_ref, loss_ref = outs[3:5]

            @pl.when((i == 0) & (j == 0))
            def _():
                dgain_ref[...] = jnp.zeros_like(dgain_ref)
                loss_ref[...] = jnp.zeros_like(loss_ref)

        @pl.when(j == nj - 1)
        def _():
            for rows in groups:
                xo = x_ref[rows, :] + 0.5 * acc_sc[rows, :]
                if head:
                    fg = fg_ref[...]
                    _, xhat = _rms_stats(xo)
                    err = xhat * fg - t_ref[rows, :]
                    loss_ref[...] += 0.5 * jnp.sum(jnp.mean(err * err, axis=-1, keepdims=True),
                                                   axis=0, keepdims=True)
                    dx, dgain = _rms_bwd(err * (1.0 / D), xo, fg)
                    dgain_ref[...] += dgain
                    xo_ref[rows, :] = dx
                else:
                    xo_ref[rows, :] = xo

    tile = pl.BlockSpec((tm, D), lambda i, j: (i, 0))
    const = pl.BlockSpec((1, D), lambda i, j: (0, 0))
    in_specs = [tile, const, pl.BlockSpec((2 * tf, D), lambda i, j: (j, 0)),
                pl.BlockSpec((tf, D), lambda i, j: (j, 0)), pl.BlockSpec(memory_space=pl.ANY)]
    out_specs = [tile, pl.BlockSpec((tm, D), lambda i, j: (i, 0)), pl.BlockSpec((tm, 2 * tf), lambda i, j: (i, j))]
    out_shape = [pltpu.HBM((T,D), F32), pltpu.HBM((T,D), BF16),
                 pltpu.HBM((T,2 * F), BF16)]
    operands = [x, gain, wgu, wd, after]
    if head:
        in_specs += [tile, const]
        out_specs += [const, pl.BlockSpec((SUBLANES, LANES), lambda i, j: (0, 0))]
        out_shape += [jax.ShapeDtypeStruct((1, D), F32), jax.ShapeDtypeStruct((SUBLANES, LANES), F32)]
        operands += list(head)
    return pl.pallas_call(
        body, name=name, grid=(T // tm, nj),
        in_specs=in_specs, out_specs=tuple(out_specs), out_shape=tuple(out_shape),
        scratch_shapes=[pltpu.VMEM((tm, D), BF16), pltpu.VMEM((tm, D), F32)],
        compiler_params=pltpu.CompilerParams(dimension_semantics=("arbitrary", "arbitrary"),
                                             vmem_limit_bytes=VMEM_LIMIT_FFN_BYTES),
    )(*_hbm(*operands))


def _ffn_bwd(dy, x, gain, gu, wgu, wd, after, name, tm, tf):
    T, D = x.shape
    F = wd.shape[0]
    nj = F // tf

    def body(dy_ref, x_ref, g_ref, gu_ref, wgu_ref, wd_ref, after_ref,
             dx_ref, dgain_ref, dg_ref, du_ref, hh_ref, do_ref, do_sc, acc_sc, dgu_sc):
        i, j = pl.program_id(0), pl.program_id(1)

        @pl.when((i == 0) & (j == 0))
        def _():
            dgain_ref[...] = jnp.zeros_like(dgain_ref)

        @pl.when(j == 0)
        def _():
            do = (0.5 * dy_ref[...]).astype(BF16)
            do_sc[...] = do
            do_ref[...] = do
            acc_sc[...] = jnp.zeros_like(acc_sc)

        groups = _row_groups(tm)
        dhs = [_dot_nt(do_sc[rows, :], wd_ref[...]) for rows in groups]
        for rows, dh in zip(groups, dhs):
            g = gu_ref[rows, :tf].astype(F32)
            u = gu_ref[rows, tf:].astype(F32)
            sig = _sigmoid(g)
            s = g * sig
            dg = (dh * u * (sig + s * (1.0 - sig))).astype(BF16)
            du = (dh * s).astype(BF16)
            dg_ref[rows, :] = dg
            du_ref[rows, :] = du
            dgu_sc[rows, :tf] = dg
            dgu_sc[rows, tf:] = du
            hh_ref[rows, :] = (s * u).astype(BF16)
        for rows in groups:
            acc_sc[rows, :] += _dot(dgu_sc[rows, :], wgu_ref[...])

        @pl.when(j == nj - 1)
        def _():
            for rows in groups:
                dx, dgain = _rms_bwd(acc_sc[rows, :], x_ref[rows, :], g_ref[...])
                dgain_ref[...] += dgain
                dx_ref[rows, :] = dy_ref[rows, :] + dx

    tile_in = pl.BlockSpec((tm, D), lambda i, j: (i, 0))
    tile = pl.BlockSpec((tm, D), lambda i, j: (i, 0), pipeline_mode=pl.Buffered(1))
    return pl.pallas_call(
        body, name=name, grid=(T // tm, nj),
        in_specs=[tile_in, tile_in,
                  pl.BlockSpec((1, D), lambda i, j: (0, 0)),
                  pl.BlockSpec((tm, 2 * tf), lambda i, j: (i, j)),
                  pl.BlockSpec((2 * tf, D), lambda i, j: (j, 0)),
                  pl.BlockSpec((tf, D), lambda i, j: (j, 0)),
                  pl.BlockSpec(memory_space=pl.ANY)],
        out_specs=(tile,
                   pl.BlockSpec((1, D), lambda i, j: (0, 0)),
                   pl.BlockSpec((tm, tf), lambda i, j: (i, j)),
                   pl.BlockSpec((tm, tf), lambda i, j: (i, j)),
                   pl.BlockSpec((tm, tf), lambda i, j: (i, j)),
                   pl.BlockSpec((tm, D), lambda i, j: (i, 0))),
        out_shape=(pltpu.HBM((T,D), F32), jax.ShapeDtypeStruct((1, D), F32),
                   pltpu.HBM((T,F), BF16), pltpu.HBM((T,F), BF16),
                   pltpu.HBM((T,F), BF16), pltpu.HBM((T,D), BF16)),
        scratch_shapes=[pltpu.VMEM((tm, D), BF16), pltpu.VMEM((tm, D), F32), pltpu.VMEM((tm, 2 * tf), BF16)],
        compiler_params=pltpu.CompilerParams(dimension_semantics=("arbitrary", "arbitrary"),
                                             vmem_limit_bytes=VMEM_LIMIT_FFN_BYTES),
    )(*_hbm(dy, x, gain, gu, wgu, wd, after))


def _tn_grad(a, b, after, name, bm, tk):
    T, M = a.shape
    D = b.shape[1]
    nk = T // tk

    def body(a_ref, b_ref, after_ref, o_ref, acc_sc):
        k = pl.program_id(1)

        @pl.when(k == 0)
        def _():
            acc_sc[...] = jnp.zeros_like(acc_sc)

        acc_sc[...] += _dot_tn(a_ref[...], b_ref[...])

        @pl.when(k == nk - 1)
        def _():
            o_ref[...] = acc_sc[...].astype(BF16)

    return pl.pallas_call(
        body, name=name, grid=(M // bm, nk),
        in_specs=[pl.BlockSpec((tk, bm), lambda i, k: (k, i)), pl.BlockSpec((tk, D), lambda i, k: (k, 0)),
                  pl.BlockSpec(memory_space=pl.ANY)],
        out_specs=pl.BlockSpec((bm, D), lambda i, k: (i, 0)),
        out_shape=pltpu.HBM((M, D), BF16),
        scratch_shapes=[pltpu.VMEM((bm, D), F32)],
        compiler_params=_params("arbitrary", "arbitrary"),
    )(*_hbm(a, b, after))


_Z_SPLITS = (0, 512, 640, 768, 1280, 1792, 2304)


def _mixin_fwd(x, gain, w_in_t, name, tm):
    T, D = x.shape
    widths = [b - a for a, b in zip(_Z_SPLITS[:-1], _Z_SPLITS[1:])]

    def body(x_ref, g_ref, w_ref, hn_ref, *outs):
        _, xhat = _rms_stats(x_ref[...])
        hn = (xhat * g_ref[...]).astype(BF16)
        hn_ref[...] = hn
        for n in (0, 3, 4, 5):
            outs[n][...] = _dot_nt(hn, w_ref[_Z_SPLITS[n]:_Z_SPLITS[n + 1], :]).astype(BF16)
        kv = _dot_nt(hn, w_ref[_Z_SPLITS[1]:_Z_SPLITS[3], :]).astype(BF16)
        outs[1][...] = kv[:, :KV_WIDTH]
        outs[2][...] = kv[:, KV_WIDTH:]

    return pl.pallas_call(
        body, name=name, grid=(T // tm,),
        in_specs=[pl.BlockSpec((tm, D), lambda i: (i, 0)),
                  pl.BlockSpec((1, D), lambda i: (0, 0)),
                  pl.BlockSpec(w_in_t.shape, lambda i: (0, 0))],
        out_specs=tuple([pl.BlockSpec((tm, D), lambda i: (i, 0))]
                        + [pl.BlockSpec((tm, w), lambda i: (i, 0)) for w in widths]),
        out_shape=tuple([pltpu.HBM((T,D), BF16)]
                        + [pltpu.HBM((T,w), BF16) for w in widths]),
        compiler_params=_params("arbitrary"),
    )(*_hbm(x, gain, w_in_t))


def _bucket_table():
    qi = np.arange(BLOCK, dtype=np.int32)[:, None]
    kj = np.arange(2 * BLOCK, dtype=np.int32)[None, :]
    dist = qi + BLOCK - kj
    n = np.maximum(dist, 0)
    max_exact = NUM_BUCKETS // 2
    large = max_exact + (np.log(np.maximum(n, 1).astype(np.float32) / max_exact)
                         / math.log(MAX_DISTANCE / max_exact)
                         * (NUM_BUCKETS - max_exact)).astype(np.int32)
    large = np.minimum(large, NUM_BUCKETS - 1)
    bucket = np.where(n < max_exact, n, large).astype(np.int32)
    valid = (dist >= 0) & (dist < WINDOW)
    return np.where(valid, bucket, -1).astype(np.int32)


def _bias_build(table, bucket, name):
    def body(t_ref, b_ref, o_ref):
        bk = b_ref[...]
        for h in range(N_Q_HEADS):
            def step(b, acc):
                return jnp.where(bk == b, t_ref[b, h], acc)
            o_ref[h] = lax.fori_loop(0, NUM_BUCKETS, step, jnp.full(bk.shape, MASKED, F32))

    return pl.pallas_call(
        body, name=name,
        in_specs=[pl.BlockSpec(memory_space=pltpu.SMEM), pl.BlockSpec(memory_space=pltpu.VMEM)],
        out_specs=pl.BlockSpec(memory_space=pltpu.VMEM),
        out_shape=jax.ShapeDtypeStruct((N_Q_HEADS,) + bucket.shape, F32),
    )(table, bucket)


def _bias_grad(dbias, bucket, name):
    def body(d_ref, b_ref, o_ref):
        bk = b_ref[...]
        row = lax.broadcasted_iota(jnp.int32, o_ref.shape, 0)
        lane = lax.broadcasted_iota(jnp.int32, o_ref.shape, 1)
        res = jnp.zeros(o_ref.shape, F32)
        for h in range(N_Q_HEADS):
            d = d_ref[h]

            def step(b, acc):
                part = jnp.sum(jnp.where(bk == b, d, 0.0), axis=0, keepdims=True)
                return jnp.where(row == b, part, acc)
            per_lane = lax.fori_loop(0, NUM_BUCKETS, step, jnp.zeros(o_ref.shape, F32))
            res = jnp.where(lane == h, jnp.sum(per_lane, axis=1, keepdims=True), res)
        o_ref[...] = res

    return pl.pallas_call(
        body, name=name,
        in_specs=[pl.BlockSpec(memory_space=pltpu.VMEM), pl.BlockSpec(memory_space=pltpu.VMEM)],
        out_specs=pl.BlockSpec(memory_space=pltpu.VMEM),
        out_shape=jax.ShapeDtypeStruct((NUM_BUCKETS, LANES), F32),
    )(*_hbm(dbias, bucket))


def _head_cols(h):
    return slice(h * HEAD_DIM, (h + 1) * HEAD_DIM)


def _stack_heads(ref, r0, g, dtype):
    return jnp.concatenate(
        [ref[pl.ds(r0, BLOCK), _head_cols(GQA_GROUP * g + j)].astype(dtype) for j in range(GQA_GROUP)], axis=0)


def _unstack_heads(ref, r0, g, val):
    for j in range(GQA_GROUP):
        ref[pl.ds(r0, BLOCK), _head_cols(GQA_GROUP * g + j)] = val[j * BLOCK:(j + 1) * BLOCK, :]


def _head_lanes(h):
    return slice(h * BLOCK, (h + 1) * BLOCK)


def _group_lanes(g):
    return slice(g * GROUP_ROWS, (g + 1) * GROUP_ROWS)


def _head_softmax(st, bias_t, sink, no_prev):
    s = st * SCALE + bias_t
    row = lax.broadcasted_iota(jnp.int32, s.shape, 0)
    s = jnp.where(no_prev & (row < BLOCK), MASKED, s)
    m = jnp.maximum(jnp.max(s, axis=0, keepdims=True), sink)
    p = jnp.exp(s - m)
    ps = jnp.exp(sink - m)
    r = 1.0 / (jnp.sum(p, axis=0, keepdims=True) + ps)
    return p * r, ps * r


def _load_band(kf_sc, vf_sc, kp_ref, kc_ref, vp_ref, vc_ref, tq):
    kf_sc[0:BLOCK, :] = kp_ref[...]
    kf_sc[BLOCK:BLOCK + tq, :] = kc_ref[...]
    vf_sc[0:BLOCK, :] = vp_ref[...]
    vf_sc[BLOCK:BLOCK + tq, :] = vc_ref[...]


def _attn_fwd(q, k, v, bias_t, sinks, gain, name, nblk):
    T = q.shape[0]
    tq = nblk * BLOCK

    def body(sink_ref, q_ref, kc_ref, kp_ref, vc_ref, vp_ref, bias_ref, g_ref, raw_ref, nrm_ref,
             kf_sc, vf_sc, o_sc, st_sc, pt_sc):
        i = pl.program_id(0)
        _load_band(kf_sc, vf_sc, kp_ref, kc_ref, vp_ref, vc_ref, tq)

        def block(b, carry):
            r0 = pl.multiple_of(b * BLOCK, BLOCK)
            no_prev = (i == 0) & (b == 0)
            for g in range(N_KV_HEADS):
                kb = kf_sc[pl.ds(r0, 2 * BLOCK), _head_cols(g)]
                st_sc[:, _group_lanes(g)] = _dot_nt(kb, _stack_heads(q_ref, r0, g, BF16))
            for h in range(N_Q_HEADS):
                p, _ = _head_softmax(st_sc[:, _head_lanes(h)], bias_ref[h], sink_ref[h], no_prev)
                pt_sc[:, _head_lanes(h)] = p.astype(BF16)
            for g in range(N_KV_HEADS):
                vb = vf_sc[pl.ds(r0, 2 * BLOCK), _head_cols(g)]
                _unstack_heads(o_sc, r0, g, _dot_tn(pt_sc[:, _group_lanes(g)], vb))
            return carry

        lax.fori_loop(0, nblk, block, 0, unroll=True)
        o = o_sc[...]
        raw_ref[...] = o.astype(BF16)
        _, ohat = _rms_stats(o)
        nrm_ref[...] = (ohat * g_ref[...]).astype(BF16)

    cur = lambda i: (i, 0)
    prev = lambda i: (jnp.maximum(i * nblk - 1, 0), 0)
    lanes = N_Q_HEADS * BLOCK
    return pl.pallas_call(
        body, name=name, grid=(T // tq,),
        in_specs=[pl.BlockSpec(memory_space=pltpu.SMEM),
                  pl.BlockSpec((tq, ATTN_WIDTH), cur),
                  pl.BlockSpec((tq, KV_WIDTH), cur), pl.BlockSpec((BLOCK, KV_WIDTH), prev),
                  pl.BlockSpec((tq, KV_WIDTH), cur), pl.BlockSpec((BLOCK, KV_WIDTH), prev),
                  pl.BlockSpec(bias_t.shape, lambda i: (0, 0, 0)),
                  pl.BlockSpec((1, ATTN_WIDTH), lambda i: (0, 0))],
        out_specs=(pl.BlockSpec((tq, ATTN_WIDTH), cur), pl.BlockSpec((tq, ATTN_WIDTH), cur)),
        out_shape=(pltpu.HBM((T,ATTN_WIDTH), BF16), pltpu.HBM((T,ATTN_WIDTH), BF16)),
        scratch_shapes=[pltpu.VMEM((tq + BLOCK, KV_WIDTH), BF16), pltpu.VMEM((tq + BLOCK, KV_WIDTH), BF16),
                        pltpu.VMEM((tq, ATTN_WIDTH), F32),
                        pltpu.VMEM((2 * BLOCK, lanes), F32), pltpu.VMEM((2 * BLOCK, lanes), BF16)],
        compiler_params=_params("arbitrary"),
    )(sinks, *_hbm(q, k, k, v, v, bias_t, gain))


def _attn_bwd(dmixed, raw, q, k, v, bias_t, sinks, gain, name, nblk):
    T = q.shape[0]
    tq = nblk * BLOCK
    nt = T // tq
    lanes = N_Q_HEADS * BLOCK

    def body(sink_ref, dm_ref, raw_ref, q_ref, kc_ref, kp_ref, vc_ref, vp_ref, bias_ref, g_ref,
             dq_ref, dk_ref, dv_ref, dbias_ref, dsink_ref, dgain_ref,
             do_sc, dq_sc, kf_sc, vf_sc, dkf_sc, dvf_sc, st_sc, dpt_sc, pt_sc, dst_sc, drow_sc,
             qs_sc, dos_sc, dsink_sc):
        i = pl.program_id(0)
        tile = nt - 1 - i

        @pl.when(i == 0)
        def _():
            dkf_sc[...] = jnp.zeros_like(dkf_sc)
            dvf_sc[...] = jnp.zeros_like(dvf_sc)
            dsink_sc[...] = jnp.zeros_like(dsink_sc)
            dbias_ref[...] = jnp.zeros_like(dbias_ref)
            dgain_ref[...] = jnp.zeros_like(dgain_ref)

        carry_k = dkf_sc[0:BLOCK, :]
        carry_v = dvf_sc[0:BLOCK, :]
        dkf_sc[0:tq, :] = jnp.zeros((tq, KV_WIDTH), F32)
        dvf_sc[0:tq, :] = jnp.zeros((tq, KV_WIDTH), F32)
        dkf_sc[tq:tq + BLOCK, :] = carry_k
        dvf_sc[tq:tq + BLOCK, :] = carry_v
        _load_band(kf_sc, vf_sc, kp_ref, kc_ref, vp_ref, vc_ref, tq)

        do, dgain = _rms_bwd(dm_ref[...].astype(F32), raw_ref[...].astype(F32), g_ref[...])
        dgain_ref[...] += dgain
        do_sc[...] = do
        ones = jnp.ones((SUBLANES, HEAD_DIM), BF16)

        def block(b, carry):
            r0 = pl.multiple_of(b * BLOCK, BLOCK)
            no_prev = (tile == 0) & (b == 0)
            for g in range(N_KV_HEADS):
                kb = kf_sc[pl.ds(r0, 2 * BLOCK), _head_cols(g)]
                vb = vf_sc[pl.ds(r0, 2 * BLOCK), _head_cols(g)]
                qg = _stack_heads(q_ref, r0, g, BF16)
                dog = _stack_heads(do_sc, r0, g, F32)
                prod = dog * _stack_heads(raw_ref, r0, g, F32)
                hi = prod.astype(BF16)
                lo = (prod - hi.astype(F32)).astype(BF16)
                drow_sc[:, _group_lanes(g)] = _dot_nt(ones, hi) + _dot_nt(ones, lo)
                dogb = dog.astype(BF16)
                qs_sc[g] = qg
                dos_sc[g] = dogb
                st_sc[:, _group_lanes(g)] = _dot_nt(kb, qg)
                dpt_sc[:, _group_lanes(g)] = _dot_nt(vb, dogb)
            for h in range(N_Q_HEADS):
                hl = _head_lanes(h)
                p, ps = _head_softmax(st_sc[:, hl], bias_ref[h], sink_ref[h], no_prev)
                rowdot = drow_sc[0:1, hl]
                ds = p * (dpt_sc[:, hl] - rowdot)
                dsink_sc[h:h + 1, :] += -(ps * rowdot)
                dbias_ref[h] += ds
                dst_sc[:, hl] = ds.astype(BF16)
                pt_sc[:, hl] = p.astype(BF16)
            for g in range(N_KV_HEADS):
                kb = kf_sc[pl.ds(r0, 2 * BLOCK), _head_cols(g)]
                dsg = dst_sc[:, _group_lanes(g)]
                _unstack_heads(dq_sc, r0, g, _dot_tn(dsg, kb) * SCALE)
                dkf_sc[pl.ds(r0, 2 * BLOCK), _head_cols(g)] += _dot(dsg, qs_sc[g]) * SCALE
                dvf_sc[pl.ds(r0, 2 * BLOCK), _head_cols(g)] += _dot(pt_sc[:, _group_lanes(g)], dos_sc[g])
            return carry

        lax.fori_loop(0, nblk, block, 0, unroll=True)
        dq_ref[...] = dq_sc[...].astype(BF16)
        dk_ref[...] = dkf_sc[BLOCK:BLOCK + tq, :].astype(BF16)
        dv_ref[...] = dvf_sc[BLOCK:BLOCK + tq, :].astype(BF16)

        @pl.when(i == nt - 1)
        def _():
            tot = jnp.sum(dsink_sc[...], axis=1, keepdims=True)
            dsink_ref[...] = jnp.broadcast_to(tot, dsink_ref.shape)

    cur = lambda i: (nt - 1 - i, 0)
    prev = lambda i: (jnp.maximum((nt - 1 - i) * nblk - 1, 0), 0)
    const2 = lambda i: (0, 0)
    const3 = lambda i: (0, 0, 0)
    return pl.pallas_call(
        body, name=name, grid=(nt,),
        in_specs=[pl.BlockSpec(memory_space=pltpu.SMEM),
                  pl.BlockSpec((tq, ATTN_WIDTH), cur),
                  pl.BlockSpec((tq, ATTN_WIDTH), cur),
                  pl.BlockSpec((tq, ATTN_WIDTH), cur),
                  pl.BlockSpec((tq, KV_WIDTH), cur), pl.BlockSpec((BLOCK, KV_WIDTH), prev),
                  pl.BlockSpec((tq, KV_WIDTH), cur), pl.BlockSpec((BLOCK, KV_WIDTH), prev),
                  pl.BlockSpec(bias_t.shape, const3),
                  pl.BlockSpec((1, ATTN_WIDTH), const2)],
        out_specs=(pl.BlockSpec((tq, ATTN_WIDTH), cur),
                   pl.BlockSpec((tq, KV_WIDTH), cur), pl.BlockSpec((tq, KV_WIDTH), cur),
                   pl.BlockSpec(bias_t.shape, const3),
                   pl.BlockSpec((N_Q_HEADS, LANES), const2),
                   pl.BlockSpec((1, ATTN_WIDTH), const2)),
        out_shape=(pltpu.HBM((T,ATTN_WIDTH), BF16),
                   pltpu.HBM((T,KV_WIDTH), BF16), pltpu.HBM((T,KV_WIDTH), BF16),
                   jax.ShapeDtypeStruct(bias_t.shape, F32),
                   jax.ShapeDtypeStruct((N_Q_HEADS, LANES), F32),
                   jax.ShapeDtypeStruct((1, ATTN_WIDTH), F32)),
        scratch_shapes=[pltpu.VMEM((tq, ATTN_WIDTH), F32), pltpu.VMEM((tq, ATTN_WIDTH), F32),
                        pltpu.VMEM((tq + BLOCK, KV_WIDTH), BF16), pltpu.VMEM((tq + BLOCK, KV_WIDTH), BF16),
                        pltpu.VMEM((tq + BLOCK, KV_WIDTH), F32), pltpu.VMEM((tq + BLOCK, KV_WIDTH), F32),
                        pltpu.VMEM((2 * BLOCK, lanes), F32), pltpu.VMEM((2 * BLOCK, lanes), F32),
                        pltpu.VMEM((2 * BLOCK, lanes), BF16), pltpu.VMEM((2 * BLOCK, lanes), BF16),
                        pltpu.VMEM((SUBLANES, lanes), F32),
                        pltpu.VMEM((N_KV_HEADS, GROUP_ROWS, HEAD_DIM), BF16),
                        pltpu.VMEM((N_KV_HEADS, GROUP_ROWS, HEAD_DIM), BF16),
                        pltpu.VMEM((N_Q_HEADS, LANES), F32)],
        compiler_params=_params("arbitrary"),
    )(sinks, *_hbm(dmixed, raw, q, k, k, v, v, bias_t, gain))


def _shift_down(cu, tail):
    row = lax.broadcasted_iota(jnp.int32, cu.shape, 0)
    t6, t7 = tail[6:7, :], tail[7:8, :]
    s1 = jnp.where(row == 0, t7, pltpu.roll(cu, 1, 0))
    s2 = jnp.where(row == 0, t6, jnp.where(row == 1, t7, pltpu.roll(cu, 2, 0)))
    return s1, s2


def _shift_up(d, head):
    n = d.shape[0]
    row = lax.broadcasted_iota(jnp.int32, d.shape, 0)
    h0, h1 = head[0:1, :], head[1:2, :]
    s1 = jnp.where(row == n - 1, h0, pltpu.roll(d, n - 1, 0))
    s2 = jnp.where(row == n - 1, h1, jnp.where(row == n - 2, h0, pltpu.roll(d, n - 2, 0)))
    return s1, s2


def _mixout_fwd(x, attn_n, u, gb, gc, conv_w, gain, w_out, name, tm):
    T, D = x.shape

    def body(x_ref, an_ref, u_ref, b_ref, c_ref, cw_ref, g_ref, wo_ref, xo_ref, cn_ref, tail_sc):
        @pl.when(pl.program_id(0) == 0)
        def _():
            tail_sc[...] = jnp.zeros_like(tail_sc)

        cu = c_ref[...].astype(F32) * u_ref[...].astype(F32)
        s1, s2 = _shift_down(cu, tail_sc[...])
        tail_sc[...] = cu[tm - SUBLANES:tm, :]
        pre = cw_ref[0:1, :] * s2 + cw_ref[1:2, :] * s1 + cw_ref[2:3, :] * cu
        conv = b_ref[...].astype(F32) * pre
        _, chat = _rms_stats(conv)
        cn = (chat * g_ref[...]).astype(BF16)
        cn_ref[...] = cn
        xo_ref[...] = (x_ref[...] + _dot(an_ref[...], wo_ref[0:ATTN_WIDTH, :])
                       + _dot(cn, wo_ref[ATTN_WIDTH:ATTN_WIDTH + CONV_DIM, :]))

    row = lambda i: (i, 0)
    const = lambda i: (0, 0)
    return pl.pallas_call(
        body, name=name, grid=(T // tm,),
        in_specs=[pl.BlockSpec((tm, D), row), pl.BlockSpec((tm, ATTN_WIDTH), row),
                  pl.BlockSpec((tm, CONV_DIM), row), pl.BlockSpec((tm, CONV_DIM), row),
                  pl.BlockSpec((tm, CONV_DIM), row),
                  pl.BlockSpec(conv_w.shape, const), pl.BlockSpec((1, CONV_DIM), const),
                  pl.BlockSpec(w_out.shape, const)],
        out_specs=(pl.BlockSpec((tm, D), row), pl.BlockSpec((tm, CONV_DIM), row)),
        out_shape=(pltpu.HBM((T,D), F32), pltpu.HBM((T,CONV_DIM), BF16)),
        scratch_shapes=[pltpu.VMEM((SUBLANES, CONV_DIM), F32)],
        compiler_params=_params("arbitrary"),
    )(*_hbm(x, attn_n, u, gb, gc, conv_w, gain, w_out))


def _mixout_bwd(dy, attn_n, conv_n, w_out, after, name, tm):
    T, D = dy.shape
    W = ATTN_WIDTH + CONV_DIM
    nt = T // tm

    def body(dy_ref, an_ref, cn_ref, w_ref, after_ref, dm_ref, dw_ref, dw_sc):
        i = pl.program_id(0)

        @pl.when(i == 0)
        def _():
            dw_sc[...] = jnp.zeros_like(dw_sc)

        dyb = dy_ref[...].astype(BF16)
        dm_ref[...] = _dot_nt(dyb, w_ref[...]).astype(BF16)
        dw_sc[0:ATTN_WIDTH, :] += _dot_tn(an_ref[...], dyb)
        dw_sc[ATTN_WIDTH:W, :] += _dot_tn(cn_ref[...], dyb)

        @pl.when(i == nt - 1)
        def _():
            dw_ref[...] = dw_sc[...].astype(BF16)

    row = lambda i: (i, 0)
    const = lambda i: (0, 0)
    return pl.pallas_call(
        body, name=name, grid=(nt,),
        in_specs=[pl.BlockSpec((tm, D), row), pl.BlockSpec((tm, ATTN_WIDTH), row),
                  pl.BlockSpec((tm, CONV_DIM), row), pl.BlockSpec(w_out.shape, const),
                  pl.BlockSpec(memory_space=pl.ANY)],
        out_specs=(pl.BlockSpec((tm, W), row), pl.BlockSpec((W, D), const)),
        out_shape=(pltpu.HBM((T,W), BF16), pltpu.HBM((W, D), BF16)),
        scratch_shapes=[pltpu.VMEM((W, D), F32)],
        compiler_params=_params("arbitrary"),
    )(*_hbm(dy, attn_n, conv_n, w_out, after))


def _conv_bwd(dmixed, u, gb, gc, conv_w, gain, name, tc):
    T = u.shape[0]
    nt = T // tc
    per8 = tc // SUBLANES

    def body(dm_ref, u_ref, b_ref, c_ref, ut_ref, ct_ref, cw_ref, g_ref,
             du_ref, db_ref, dc_ref, dcw_ref, dgain_ref, head_sc):
        i = pl.program_id(0)

        @pl.when(i == 0)
        def _():
            head_sc[...] = jnp.zeros_like(head_sc)
            dcw_ref[...] = jnp.zeros_like(dcw_ref)
            dgain_ref[...] = jnp.zeros_like(dgain_ref)

        uu = u_ref[...].astype(F32)
        cc = c_ref[...].astype(F32)
        bb = b_ref[...].astype(F32)
        cu = cc * uu
        tail = jnp.where(i == nt - 1, 0.0, ct_ref[...].astype(F32) * ut_ref[...].astype(F32))
        s1, s2 = _shift_down(cu, tail)
        w0, w1, w2 = cw_ref[0:1, :], cw_ref[1:2, :], cw_ref[2:3, :]
        pre = w0 * s2 + w1 * s1 + w2 * cu
        dconv, dgain = _rms_bwd(dm_ref[...].astype(F32), bb * pre, g_ref[...])
        dgain_ref[...] += dgain
        db_ref[...] = (dconv * pre).astype(BF16)
        dpre = dconv * bb
        dcw_ref[0:1, :] += jnp.sum(dpre * s2, axis=0, keepdims=True)
        dcw_ref[1:2, :] += jnp.sum(dpre * s1, axis=0, keepdims=True)
        dcw_ref[2:3, :] += jnp.sum(dpre * cu, axis=0, keepdims=True)
        n1, n2 = _shift_up(dpre, head_sc[...])
        head_sc[...] = dpre[0:SUBLANES, :]
        dcu = w2 * dpre + w1 * n1 + w0 * n2
        du_ref[...] = (dcu * cc).astype(BF16)
        dc_ref[...] = (dcu * uu).astype(BF16)

    rev = lambda i: (nt - 1 - i, 0)
    rev_right = lambda i: (nt - 1 - i, 1)
    tail_map = lambda i: (jnp.maximum((nt - 1 - i) * per8 - 1, 0), 0)
    const = lambda i: (0, 0)
    return pl.pallas_call(
        body, name=name, grid=(nt,),
        in_specs=[pl.BlockSpec((tc, CONV_DIM), rev_right),
                  pl.BlockSpec((tc, CONV_DIM), rev), pl.BlockSpec((tc, CONV_DIM), rev),
                  pl.BlockSpec((tc, CONV_DIM), rev),
                  pl.BlockSpec((SUBLANES, CONV_DIM), tail_map), pl.BlockSpec((SUBLANES, CONV_DIM), tail_map),
                  pl.BlockSpec(conv_w.shape, const), pl.BlockSpec((1, CONV_DIM), const)],
        out_specs=(pl.BlockSpec((tc, CONV_DIM), rev), pl.BlockSpec((tc, CONV_DIM), rev),
                   pl.BlockSpec((tc, CONV_DIM), rev),
                   pl.BlockSpec((SUBLANES, CONV_DIM), const), pl.BlockSpec((1, CONV_DIM), const)),
        out_shape=(pltpu.HBM((T,CONV_DIM), BF16), pltpu.HBM((T,CONV_DIM), BF16),
                   pltpu.HBM((T,CONV_DIM), BF16),
                   jax.ShapeDtypeStruct((SUBLANES, CONV_DIM), F32), jax.ShapeDtypeStruct((1, CONV_DIM), F32)),
        scratch_shapes=[pltpu.VMEM((SUBLANES, CONV_DIM), F32)],
        compiler_params=_params("arbitrary"),
    )(*_hbm(dmixed, u, gb, gc, u, gc, conv_w, gain))


def _mixin_bwd(dy, x, gain, dz, w_in_t, name, tm):
    T, D = x.shape
    nz = len(dz)

    def body(dy_ref, x_ref, g_ref, *rest):
        dz_refs, wt_ref, dx_ref, dgain_ref, dz_sc = rest[:nz], rest[nz], rest[nz + 1], rest[nz + 2], rest[nz + 3]

        @pl.when(pl.program_id(0) == 0)
        def _():
            dgain_ref[...] = jnp.zeros_like(dgain_ref)

        for r, lo, hi in zip(dz_refs, _Z_SPLITS[:-1], _Z_SPLITS[1:]):
            dz_sc[:, lo:hi] = r[...]
        dx, dgain = _rms_bwd(_dot(dz_sc[...], wt_ref[...]), x_ref[...], g_ref[...])
        dgain_ref[...] += dgain
        dx_ref[...] = dy_ref[...] + dx

    row = lambda i: (i, 0)
    const = lambda i: (0, 0)
    return pl.pallas_call(
        body, name=name, grid=(T // tm,),
        in_specs=[pl.BlockSpec((tm, D), row), pl.BlockSpec((tm, D), row), pl.BlockSpec((1, D), const)]
                 + [pl.BlockSpec((tm, a.shape[1]), row) for a in dz]
                 + [pl.BlockSpec(w_in_t.shape, const)],
        out_specs=(pl.BlockSpec((tm, D), row), pl.BlockSpec((1, D), const)),
        out_shape=(pltpu.HBM((T,D), F32), jax.ShapeDtypeStruct((1, D), F32)),
        scratch_shapes=[pltpu.VMEM((tm, _Z_SPLITS[-1]), BF16)],
        compiler_params=_params("arbitrary"),
    )(*_hbm(dy, x, gain, *dz, w_in_t))


def _win_grad(dz, hn, name, tk):
    T, D = hn.shape
    nz = len(dz)
    nt = T // tk
    W = _Z_SPLITS[-1]

    def body(hn_ref, *rest):
        dz_refs, dw_ref, dw_sc, kv_sc = rest[:nz], rest[nz], rest[nz + 1], rest[nz + 2]
        i = pl.program_id(0)

        @pl.when(i == 0)
        def _():
            dw_sc[...] = jnp.zeros_like(dw_sc)

        hn = hn_ref[...]
        kv_sc[:, :KV_WIDTH] = dz_refs[1][...]
        kv_sc[:, KV_WIDTH:] = dz_refs[2][...]
        dw_sc[_Z_SPLITS[1]:_Z_SPLITS[3], :] += _dot_tn(kv_sc[...], hn)
        for n in (0, 3, 4, 5):
            dw_sc[_Z_SPLITS[n]:_Z_SPLITS[n + 1], :] += _dot_tn(dz_refs[n][...], hn)

        @pl.when(i == nt - 1)
        def _():
            dw_ref[...] = dw_sc[...].astype(BF16)

    row = lambda i: (i, 0)
    return pl.pallas_call(
        body, name=name, grid=(nt,),
        in_specs=[pl.BlockSpec((tk, D), row)] + [pl.BlockSpec((tk, a.shape[1]), row) for a in dz],
        out_specs=pl.BlockSpec((W, D), lambda i: (0, 0)),
        out_shape=pltpu.HBM((W, D), BF16),
        scratch_shapes=[pltpu.VMEM((W, D), F32), pltpu.VMEM((tk, 2 * KV_WIDTH), BF16)],
        compiler_params=_params("arbitrary"),
    )(*_hbm(hn, *dz))


def _adamw(parts, w, m, v, name, tr):
    P = parts.shape[0]
    R, C = w.shape

    def body(p_ref, w_ref, m_ref, v_ref, g_ref, d_ref, nm_ref, nv_ref):
        g = p_ref[0].astype(F32)
        for d in range(1, P):
            g = g + p_ref[d].astype(F32)
        nm = ADAM_B1 * m_ref[...] + (1.0 - ADAM_B1) * g
        nv = ADAM_B2 * v_ref[...] + (1.0 - ADAM_B2) * (g * g)
        m_hat = nm / (1.0 - ADAM_B1 ** ADAM_STEP)
        v_hat = nv / (1.0 - ADAM_B2 ** ADAM_STEP)
        g_ref[...] = g
        nm_ref[...] = nm
        nv_ref[...] = nv
        d_ref[...] = -ADAM_LR * (m_hat / (jnp.sqrt(v_hat) + ADAM_EPS) + ADAM_WD * w_ref[...])

    row = lambda i: (i, 0)
    spec = pl.BlockSpec((tr, C), row)
    shp = pltpu.HBM((R, C), F32)
    return pl.pallas_call(
        body, name=name, grid=(R // tr,),
        in_specs=[pl.BlockSpec((P, tr, C), lambda i: (0, i, 0)), spec, spec, spec],
        out_specs=(spec, spec, spec, spec),
        out_shape=(shp, shp, shp, shp),
        compiler_params=_params("arbitrary"),
    )(*_hbm(parts, w, m, v))


def _columns_of_blocks(g):
    n, R, w = g.shape
    return g.transpose(1, 0, 2).reshape(R, n * w)


def _pad_row(vec):
    vec = vec.reshape(1, -1)
    return jnp.pad(vec, ((0, 0), (0, PACK_COLS - vec.shape[1])))


def kernel(x, rel_bias_table, ffn1_norm, ffn1_w_gate, ffn1_w_up, ffn1_w_down, mix_norm, w_in, conv_w, attn_sinks, attn_out_norm, conv_out_norm, w_out, ffn2_norm, ffn2_w_gate, ffn2_w_up, ffn2_w_down, final_norm, loss_target, m_rel_bias_table, m_ffn1_norm, m_ffn1_w_gate, m_ffn1_w_up, m_ffn1_w_down, m_mix_norm, m_w_in, m_conv_w, m_attn_sinks, m_attn_out_norm, m_conv_out_norm, m_w_out, m_ffn2_norm, m_ffn2_w_gate, m_ffn2_w_up, m_ffn2_w_down, m_final_norm, v_rel_bias_table, v_ffn1_norm, v_ffn1_w_gate, v_ffn1_w_up, v_ffn1_w_down, v_mix_norm, v_w_in, v_conv_w, v_attn_sinks, v_attn_out_norm, v_conv_out_norm, v_w_out, v_ffn2_norm, v_ffn2_w_gate, v_ffn2_w_up, v_ffn2_w_down, v_final_norm):
    T, D = x.shape[1], x.shape[2]
    x0 = x[0]
    target = loss_target[0]
    tm = min(TM_FFN, T)
    tm_bwd = min(TM_FFN_BWD, T)
    tm_mix = min(TM_MIX, T)
    tm_wide = min(TM_MIX_WIDE, T)
    tk = min(TK_WGRAD, T)
    tf = TF_FFN
    nblk = min(ATTN_BLOCKS, T // BLOCK)
    me = 4 * lax.axis_index("x") + 2 * lax.axis_index("y") + lax.axis_index("c")

    big = {
        "ffn1_w_gate": (ffn1_w_gate[0], m_ffn1_w_gate[0], v_ffn1_w_gate[0], True),
        "ffn1_w_up": (ffn1_w_up[0], m_ffn1_w_up[0], v_ffn1_w_up[0], True),
        "ffn1_w_down": (ffn1_w_down[0], m_ffn1_w_down[0], v_ffn1_w_down[0], False),
        "w_in": (w_in[0], m_w_in[0], v_w_in[0], True),
        "w_out": (w_out[0], m_w_out[0], v_w_out[0], False),
        "ffn2_w_gate": (ffn2_w_gate[0], m_ffn2_w_gate[0], v_ffn2_w_gate[0], True),
        "ffn2_w_up": (ffn2_w_up[0], m_ffn2_w_up[0], v_ffn2_w_up[0], True),
        "ffn2_w_down": (ffn2_w_down[0], m_ffn2_w_down[0], v_ffn2_w_down[0], False),
    }

    def block_to_send(name):
        w, _, _, transposed = big[name]
        return (w.T if transposed else w).astype(BF16)

    names1 = ["ffn1_w_gate", "ffn1_w_up", "ffn1_w_down"]
    names_rest = ["w_in", "w_out", "ffn2_w_gate", "ffn2_w_up", "ffn2_w_down"]
    first = _gather_two_level([block_to_send(n) for n in names1], "gather_ffn1", ffn1_norm)
    wgt1, wut1, wd1 = [g.reshape(-1, D) for g in first]
    h_rest, token = _exchange_start([block_to_send(n) for n in names_rest[:2]] + [conv_w[0]]
                                    + [block_to_send(n) for n in names_rest[2:]], False,
                                    "gather_start_rest", wd1)

    wgu1 = _stack_gate_up(wgt1, wut1, tf)
    x1, xn1, gu1 = _ffn_fwd(x0, ffn1_norm, wgu1, wd1, token, "ffn1_fwd", tm, tf)
    mixw = _exchange_wait(h_rest[:3], False, "gather_wait_mix", x1)
    win_t = mixw[0].reshape(-1, D)
    wout = mixw[1].reshape(-1, D)
    cw = _columns_of_blocks(mixw[2])
    hn, q, k, v, u, gb, gc = _mixin_fwd(x1, mix_norm, win_t, "mixin_fwd", tm_wide)
    bucket = jnp.asarray(_bucket_table().T.copy())
    sinks = attn_sinks.reshape(-1)
    bias_t = _bias_build(rel_bias_table, bucket, "bias_build")
    attn_raw, attn_n = _attn_fwd(q, k, v, bias_t, sinks, attn_out_norm, "attn_fwd", nblk)
    x2, conv_n = _mixout_fwd(x1, attn_n, u, gb, gc, cw, conv_out_norm, wout, "mixout_fwd", tm_mix)
    wgt2, wut2, wd2 = [g.reshape(-1, D) for g in _exchange_wait(h_rest[3:], False, "gather_wait_ffn2", x2)]
    wgu2 = _stack_gate_up(wgt2, wut2, tf)
    dx3, xn2, gu2, d_final, loss_part = _ffn_fwd(x2, ffn2_norm, wgu2, wd2, x2, "ffn2_fwd", tm, tf,
                                                 head=(target, final_norm.reshape(1, D)))

    def blocks(g):
        return g.reshape(N_DEV, -1, D)

    dx2, d_ffn2_norm, dg2, du2, hh2, do2 = _ffn_bwd(
        dx3, x2, ffn2_norm, gu2, wgu2, wd2, dx3, "ffn2_bwd", tm_bwd, tf)
    d_wg2 = _tn_grad(dg2, xn2, dx2, "ffn2_wgrad_gate", BM_WGRAD, tk)
    d_wu2 = _tn_grad(du2, xn2, dx2, "ffn2_wgrad_up", BM_WGRAD, tk)
    d_wd2 = _tn_grad(hh2, do2, dx2, "ffn2_wgrad_down", BM_WGRAD, tk)
    handles2, token2 = _exchange_start([blocks(d_wg2), blocks(d_wu2), blocks(d_wd2)], True,
                                       "grads_start_ffn2", dx2)

    dmixed, d_wout = _mixout_bwd(dx2, attn_n, conv_n, wout, token2, "mixout_bwd", tm_wide)
    dq, dk, dv, dbias, dsink, d_attn_norm = _attn_bwd(
        dmixed, attn_raw, q, k, v, bias_t, sinks, attn_out_norm, "attn_bwd", nblk)
    du, dgb, dgc, d_cw, d_conv_norm = _conv_bwd(dmixed, u, gb, gc, cw, conv_out_norm, "conv_bwd", tm_wide)
    d_table = _bias_grad(dbias, bucket, "bias_grad")
    dz = [dq, dk, dv, du, dgb, dgc]
    dx1, d_mix_norm = _mixin_bwd(dx2, x1, mix_norm, dz, win_t, "mixin_bwd", tm_mix)
    d_win_t = _win_grad(dz, hn, "win_grad", tm_wide)
    handles_mix, token_mix = _exchange_start([blocks(d_win_t), blocks(d_wout)], True, "grads_start_mix", d_table)

    dx0, d_ffn1_norm, dg1, du1, hh1, do1 = _ffn_bwd(
        dx1, x0, ffn1_norm, gu1, wgu1, wd1, token_mix, "ffn1_bwd", tm_bwd, tf)
    def pack(ffn1, mixn, ffn2, fin, attn_n_, conv_n_, sink_, extra, convw, table):
        rows = [_pad_row(ffn1), _pad_row(mixn), _pad_row(ffn2), _pad_row(fin),
                _pad_row(jnp.concatenate([attn_n_.reshape(-1), conv_n_.reshape(-1)])),
                _pad_row(sink_), _pad_row(extra),
                jnp.zeros((1, PACK_COLS), F32),
                jnp.pad(convw, ((0, 0), (0, PACK_COLS - convw.shape[1]))),
                _pad_row(table),
                jnp.zeros((PACK_ROWS - 12, PACK_COLS), F32)]
        return jnp.concatenate(rows, axis=0)

    def own_channels(a):
        full = jnp.zeros((a.shape[1], CONV_DIM), F32)
        return lax.dynamic_update_slice(full, a[0], (0, me * a.shape[2]))

    g_pack = pack(d_ffn1_norm, d_mix_norm, d_ffn2_norm, d_final, d_attn_norm, d_conv_norm,
                  dsink[:, 0], loss_part[0, :1], d_cw[:3], d_table[:, :N_Q_HEADS])
    zero1 = jnp.zeros((1,), F32)
    w_pack = pack(ffn1_norm, mix_norm, ffn2_norm, final_norm, attn_out_norm, conv_out_norm,
                  attn_sinks, zero1, own_channels(conv_w), rel_bias_table)
    m_pack = pack(m_ffn1_norm, m_mix_norm, m_ffn2_norm, m_final_norm, m_attn_out_norm, m_conv_out_norm,
                  m_attn_sinks, zero1, own_channels(m_conv_w), m_rel_bias_table)
    v_pack = pack(v_ffn1_norm, v_mix_norm, v_ffn2_norm, v_final_norm, v_attn_out_norm, v_conv_out_norm,
                  v_attn_sinks, zero1, own_channels(v_conv_w), v_rel_bias_table)
    h_small, token_small = _exchange_start([g_pack], False, "gather_small_start", dx0)

    d_wd1 = _tn_grad(hh1, do1, token_small, "ffn1_wgrad_down", BM_WGRAD, tk)
    handles1_d, token1 = _exchange_start([blocks(d_wd1)], True, "grads_start_ffn1_down", dx0)
    d_wg1 = _tn_grad(dg1, xn1, token1, "ffn1_wgrad_gate", BM_WGRAD, tk)
    handles1_g, token1 = _exchange_start([blocks(d_wg1)], True, "grads_start_ffn1_gate", token1)
    d_wu1 = _tn_grad(du1, xn1, token1, "ffn1_wgrad_up", BM_WGRAD, tk)
    handles1_u, token1 = _exchange_start([blocks(d_wu1)], True, "grads_start_ffn1_up", token1)

    res = {}
    made = []

    def update(names, parts):
        last = None
        for name, p in zip(names, parts):
            w, m_, v_, transposed = big[name]
            if transposed:
                w, m_, v_ = w.T, m_.T, v_.T
            new = _adamw(p, w, m_, v_, "adamw_" + name, _row_tile(w.shape[0], ADAM_ROWS))
            res[name] = tuple((a.T if transposed else a)[None] for a in new)
            made.extend(new)
            last = new[0]
        return last

    parts2 = _exchange_wait(handles2, True, "grads_wait_ffn2", token1)
    done2 = update(["ffn2_w_gate", "ffn2_w_up", "ffn2_w_down"], parts2)
    parts_mix = _exchange_wait(handles_mix, True, "grads_wait_mix", done2)
    done_mix = update(["w_in", "w_out"], parts_mix)
    (g_all,) = _exchange_wait(h_small, False, "gather_small_wait", done_mix)
    packs = _adamw(g_all, w_pack, m_pack, v_pack, "adamw_small", PACK_ROWS)

    def unpack(pk):
        cwb = lax.dynamic_slice(pk[8:11, :CONV_DIM], (0, me * conv_w.shape[2]), (3, conv_w.shape[2]))
        return {
            "ffn1_norm": pk[0:1, :D], "mix_norm": pk[1:2, :D], "ffn2_norm": pk[2:3, :D],
            "final_norm": pk[3, :D],
            "attn_out_norm": pk[4:5, :ATTN_WIDTH], "conv_out_norm": pk[4:5, ATTN_WIDTH:ATTN_WIDTH + CONV_DIM],
            "attn_sinks": pk[5:6, :N_Q_HEADS],
            "conv_w": cwb[None],
            "rel_bias_table": pk[11, :NUM_BUCKETS * N_Q_HEADS].reshape(NUM_BUCKETS, N_Q_HEADS),
        }

    small = [unpack(pk) for pk in packs]
    loss = packs[0][6, 0]

    ready = list(made) + [a for group in small for a in group.values()]
    parts1_d = _exchange_wait(handles1_d, True, "grads_wait_ffn1_down", ready)
    done1 = update(["ffn1_w_down"], parts1_d)
    parts1_g = _exchange_wait(handles1_g, True, "grads_wait_ffn1_gate", done1)
    done1 = update(["ffn1_w_gate"], parts1_g)
    parts1_u = _exchange_wait(handles1_u, True, "grads_wait_ffn1_up", done1)
    update(["ffn1_w_up"], parts1_u)

    order = ["rel_bias_table", "ffn1_norm", "ffn1_w_gate", "ffn1_w_up", "ffn1_w_down", "mix_norm", "w_in",
             "conv_w", "attn_sinks", "attn_out_norm", "conv_out_norm", "w_out", "ffn2_norm",
             "ffn2_w_gate", "ffn2_w_up", "ffn2_w_down", "final_norm"]
    outs = [loss, dx0[None]]
    for kind in range(4):
        for name in order:
            outs.append(res[name][kind] if name in res else small[kind][name])
    return tuple(outs)
```

```python
import math

import numpy as np
import jax
import jax.numpy as jnp
from jax import lax
from jax.experimental import pallas as pl
from jax.experimental.pallas import tpu as pltpu

F32 = jnp.float32
BF16 = jnp.bfloat16

N_DEV = 8
EPS = 1e-6
HEAD_DIM = 64
N_Q_HEADS = 8
N_KV_HEADS = 2
GQA_GROUP = 4
ATTN_WIDTH = 512
KV_WIDTH = 128
CONV_DIM = 512
BLOCK = 128
WINDOW = 128
NUM_BUCKETS = 32
MAX_DISTANCE = 128
SCALE = HEAD_DIM ** -0.5
MASKED = -1e30
GROUP_ROWS = GQA_GROUP * BLOCK

ADAM_LR = 0.001
ADAM_B1 = 0.9
ADAM_B2 = 0.999
ADAM_EPS = 1e-08
ADAM_WD = 0.01
ADAM_STEP = 10

VMEM_LIMIT_BYTES = 40 * 1024 * 1024
VMEM_LIMIT_FFN_BYTES = 50 * 1024 * 1024
SUBLANES = 8
LANES = 128
PACK_ROWS = 16
PACK_COLS = 1024

TM_FFN = 1024
TM_FFN_BWD = 1024
TM_MIX = 512
TM_MIX_WIDE = 1024
TK_WGRAD = 2048
TF_FFN = 256
BM_WGRAD = 1408
ROW_GROUPS = 4
ATTN_BLOCKS = 8
ADAM_ROWS = 256


def _row_tile(rows, limit):
    best = rows
    for t in range(16, min(rows, limit) + 1, 16):
        if rows % t == 0:
            best = t
    return best


def _params(*sem):
    return pltpu.CompilerParams(dimension_semantics=sem, vmem_limit_bytes=VMEM_LIMIT_BYTES)


def _hbm(*arrays):
    return [pltpu.with_memory_space_constraint(a, pltpu.HBM) for a in arrays]


def _dot(a, b):
    return jnp.dot(a, b, preferred_element_type=F32)


def _dot_nt(a, b):
    return lax.dot_general(a, b, (((1,), (1,)), ((), ())), preferred_element_type=F32)


def _dot_tn(a, b):
    return lax.dot_general(a, b, (((0,), (0,)), ((), ())), preferred_element_type=F32)


def _sigmoid(g):
    return 0.5 * jnp.tanh(0.5 * g) + 0.5


def _rms_stats(x):
    inv = lax.rsqrt(jnp.mean(x * x, axis=-1, keepdims=True) + EPS)
    return inv, x * inv


def _rms_bwd(dy, x, gain):
    inv, xhat = _rms_stats(x)
    dgain = jnp.sum(dy * xhat, axis=0, keepdims=True)
    dxh = dy * gain
    dx = inv * (dxh - xhat * jnp.mean(dxh * xhat, axis=-1, keepdims=True))
    return dx, dgain


def _peer_list():
    x, y, c = lax.axis_index("x"), lax.axis_index("y"), lax.axis_index("c")
    peers = []
    for k in range(1, N_DEV):
        px = 1 - x if (k >> 2) & 1 else x
        py = 1 - y if (k >> 1) & 1 else y
        pc = 1 - c if k & 1 else c
        peers.append((px, py, pc))
    return 4 * x + 2 * y + c, peers


def _gather_two_level(arrs, name, after):
    n = len(arrs)
    out_shape = tuple(jax.ShapeDtypeStruct((N_DEV,) + a.shape, a.dtype) for a in arrs)

    def body(*refs):
        ins, outs = refs[:n], refs[n + 1:2 * n + 1]
        send_sems, recv_sems, local_sems = refs[2 * n + 1:]
        x, y, c = lax.axis_index("x"), lax.axis_index("y"), lax.axis_index("c")
        me, sibling = (x, y, c), (x, y, 1 - c)
        chips = [(1 - x, y), (x, 1 - y), (1 - x, 1 - y)]

        def copy(a, k, block, to, src=None):
            slot = outs[a].at[4 * block[0] + 2 * block[1] + block[2]]
            return pltpu.make_async_remote_copy(
                src_ref=slot if src is None else src, dst_ref=slot,
                send_sem=send_sems.at[a, k], recv_sem=recv_sems.at[a, k],
                device_id=to, device_id_type=pl.DeviceIdType.MESH)

        started = []
        for a in range(n):
            loc = pltpu.make_async_copy(ins[a], outs[a].at[4 * x + 2 * y + c], local_sems.at[a])
            loc.start()
            started.append(loc)
        south = c == 0
        relay_from = (jnp.where(south, 1 - x, x), jnp.where(south, y, 1 - y), c)
        relay_to = (jnp.where(south, x, 1 - x), jnp.where(south, 1 - y, y), c)
        sends = []
        for a in range(n):
            sends += [copy(a, 1 + j, me, (*chips[j], c), src=ins[a]) for j in range(2)]
        for a in range(n):
            sends.append(copy(a, 0, me, sibling, src=ins[a]))
        for cp in sends:
            cp.start()
        for a in range(n):
            for j in range(2):
                copy(a, 1 + j, (*chips[j], c), me).wait_recv()
                if j == 1:
                    relay = copy(a, 3, relay_from, relay_to)
                    relay.start()
                    sends.append(relay)
                fwd = copy(a, 4 + j, (*chips[j], c), sibling)
                fwd.start()
                sends.append(fwd)
        for a in range(n):
            copy(a, 3, (*chips[2], c), me).wait_recv()
            fwd = copy(a, 6, (*chips[2], c), sibling)
            fwd.start()
            sends.append(fwd)
        for a in range(n):
            copy(a, 0, sibling, me).wait_recv()
            for j, chip in enumerate(chips):
                copy(a, 4 + j, (*chip, 1 - c), me).wait_recv()
        for cp in sends:
            cp.wait_send()
        for loc in started:
            loc.wait()

    hbm = pl.BlockSpec(memory_space=pl.ANY)
    return pl.pallas_call(
        body, name=name, out_shape=out_shape,
        in_specs=[hbm] * (n + 1), out_specs=tuple([hbm] * n),
        scratch_shapes=[pltpu.SemaphoreType.DMA((n, N_DEV - 1)),
                        pltpu.SemaphoreType.DMA((n, N_DEV - 1)),
                        pltpu.SemaphoreType.DMA((n,))],
    )(*arrs, after)


_HBM = pl.BlockSpec(memory_space=pltpu.HBM)
_SEM = pl.BlockSpec(memory_space=pltpu.SEMAPHORE)
_EFFECT = pltpu.SideEffectType.DATAFLOW_SIDE_EFFECTING


def _split_copies(srcs, lands, send_sems, recv_sems, scatter):
    me, peers = _peer_list()
    copies = []
    for a in range(len(srcs)):
        for k, (px, py, pc) in enumerate(peers):
            src = srcs[a].at[4 * px + 2 * py + pc] if scatter else srcs[a]
            copies.append(pltpu.make_async_remote_copy(
                src_ref=src, dst_ref=lands[a].at[me],
                send_sem=send_sems[a].at[k], recv_sem=recv_sems[a].at[k],
                device_id=(px, py, pc), device_id_type=pl.DeviceIdType.MESH))
    return copies


def _own_copies(srcs, lands, recv_sems, scatter):
    me, _ = _peer_list()
    return [pltpu.make_async_copy(srcs[a].at[me] if scatter else srcs[a], lands[a].at[me],
                                  recv_sems[a].at[N_DEV - 1]) for a in range(len(srcs))]


def _exchange_start(arrs, scatter, name, after):
    n = len(arrs)
    lands = [lax.empty(a.shape if scatter else (N_DEV,) + a.shape, a.dtype) for a in arrs]

    def body(*refs):
        srcs, lnds = refs[:n], refs[n:2 * n]
        outs = refs[2 * n + 1:]
        send_sems, recv_sems = outs[:n], outs[n:2 * n]
        token = outs[4 * n]
        for cp in _split_copies(srcs, lnds, send_sems, recv_sems, scatter):
            cp.start()
        for cp in _own_copies(srcs, lnds, recv_sems, scatter):
            cp.start()
        token[...] = jnp.zeros_like(token)

    sem = pltpu.SemaphoreType.DMA((N_DEV,))
    out_shape = ([sem] * (2 * n) + [pltpu.HBM(a.shape, a.dtype) for a in arrs]
                 + [pltpu.HBM(l.shape, l.dtype) for l in lands] + [jax.ShapeDtypeStruct((SUBLANES, LANES), F32)])
    res = pl.pallas_call(
        body, name=name, out_shape=tuple(out_shape),
        in_specs=[_HBM] * (2 * n) + [pl.BlockSpec(memory_space=pl.ANY)],
        out_specs=tuple([_SEM] * (2 * n) + [_HBM] * (2 * n) + [pl.BlockSpec(memory_space=pltpu.VMEM)]),
        input_output_aliases={i: 2 * n + i for i in range(2 * n)},
        compiler_params=pltpu.CompilerParams(has_side_effects=_EFFECT),
    )(*[pltpu.with_memory_space_constraint(a, pltpu.HBM) for a in arrs],
      *[pltpu.with_memory_space_constraint(l, pltpu.HBM) for l in lands], after)
    handles = [(res[2 * n + a], res[3 * n + a], res[a], res[n + a]) for a in range(n)]
    return handles, res[4 * n]


def _exchange_wait(handles, scatter, name, after):
    n = len(handles)
    after = list(after) if isinstance(after, (list, tuple)) else [after]

    def body(*refs):
        srcs, lnds = refs[:n], refs[n:2 * n]
        send_sems, recv_sems = refs[2 * n:3 * n], refs[3 * n:4 * n]
        for cp in _split_copies(srcs, lnds, send_sems, recv_sems, scatter):
            cp.wait_send()
            cp.wait_recv()
        for cp in _own_copies(srcs, lnds, recv_sems, scatter):
            cp.wait()

    srcs = [h[0] for h in handles]
    lands = [h[1] for h in handles]
    res = pl.pallas_call(
        body, name=name,
        out_shape=tuple([pltpu.HBM(a.shape, a.dtype) for a in srcs] + [pltpu.HBM(l.shape, l.dtype) for l in lands]),
        in_specs=[_HBM] * (2 * n) + [_SEM] * (2 * n) + [pl.BlockSpec(memory_space=pl.ANY)] * len(after),
        out_specs=tuple([_HBM] * (2 * n)),
        input_output_aliases={i: i for i in range(2 * n)},
        compiler_params=pltpu.CompilerParams(has_side_effects=_EFFECT),
    )(*srcs, *lands, *[h[2] for h in handles], *[h[3] for h in handles], *after)
    return list(res[n:])


def _row_groups(tm):
    return [slice(r * (tm // ROW_GROUPS), (r + 1) * (tm // ROW_GROUPS)) for r in range(ROW_GROUPS)]


def _stack_gate_up(wgt, wut, tf):
    F, D = wgt.shape
    return jnp.stack([wgt.reshape(F // tf, tf, D), wut.reshape(F // tf, tf, D)], axis=1).reshape(2 * F, D)


def _ffn_fwd(x, gain, wgu, wd, after, name, tm, tf, head=None):
    T, D = x.shape
    F = wd.shape[0]
    nj = F // tf
    n_in = 7 if head else 5

    def body(*refs):
        x_ref, g_ref, wgu_ref, wd_ref, after_ref = refs[:5]
        outs = refs[n_in:]
        xo_ref, xn_ref, gu_ref = outs[:3]
        xn_sc, acc_sc = outs[-2:]
        i, j = pl.program_id(0), pl.program_id(1)

        @pl.when(j == 0)
        def _():
            _, xhat = _rms_stats(x_ref[...])
            xn = (xhat * g_ref[...]).astype(BF16)
            xn_sc[...] = xn
            xn_ref[...] = xn
            acc_sc[...] = jnp.zeros_like(acc_sc)

        groups = _row_groups(tm)
        gus = [_dot_nt(xn_sc[rows, :], wgu_ref[...]) for rows in groups]
        hs = []
        for rows, gu in zip(groups, gus):
            gu_ref[rows, :] = gu.astype(BF16)
            g, u = gu[:, :tf], gu[:, tf:]
            hs.append((g * _sigmoid(g) * u).astype(BF16))
        for rows, h in zip(groups, hs):
            acc_sc[rows, :] += _dot(h, wd_ref[...])

        if head:
            t_ref, fg_ref = refs[5:7]
            dgain_ref, loss_ref = outs[3:5]

            @pl.when((i == 0) & (j == 0))
            def _():
                dgain_ref[...] = jnp.zeros_like(dgain_ref)
                loss_ref[...] = jnp.zeros_like(loss_ref)

        @pl.when(j == nj - 1)
        def _():
            for rows in groups:
                xo = x_ref[rows, :] + 0.5 * acc_sc[rows, :]
                if head:
                    fg = fg_ref[...]
                    _, xhat = _rms_stats(xo)
                    err = xhat * fg - t_ref[rows, :]
                    loss_ref[...] += 0.5 * jnp.sum(jnp.mean(err * err, axis=-1, keepdims=True),
                                                   axis=0, keepdims=True)
                    dx, dgain = _rms_bwd(err * (1.0 / D), xo, fg)
                    dgain_ref[...] += dgain
                    xo_ref[rows, :] = dx
                else:
                    xo_ref[rows, :] = xo

    tile = pl.BlockSpec((tm, D), lambda i, j: (i, 0))
    const = pl.BlockSpec((1, D), lambda i, j: (0, 0))
    in_specs = [tile, const, pl.BlockSpec((2 * tf, D), lambda i, j: (j, 0)),
                pl.BlockSpec((tf, D), lambda i, j: (j, 0)), pl.BlockSpec(memory_space=pl.ANY)]
    out_specs = [tile, pl.BlockSpec((tm, D), lambda i, j: (i, 0)), pl.BlockSpec((tm, 2 * tf), lambda i, j: (i, j))]
    out_shape = [pltpu.HBM((T,D), F32), pltpu.HBM((T,D), BF16),
                 pltpu.HBM((T,2 * F), BF16)]
    operands = [x, gain, wgu, wd, after]
    if head:
        in_specs += [tile, const]
        out_specs += [const, pl.BlockSpec((SUBLANES, LANES), lambda i, j: (0, 0))]
        out_shape += [jax.ShapeDtypeStruct((1, D), F32), jax.ShapeDtypeStruct((SUBLANES, LANES), F32)]
        operands += list(head)
    return pl.pallas_call(
        body, name=name, grid=(T // tm, nj),
        in_specs=in_specs, out_specs=tuple(out_specs), out_shape=tuple(out_shape),
        scratch_shapes=[pltpu.VMEM((tm, D), BF16), pltpu.VMEM((tm, D), F32)],
        compiler_params=pltpu.CompilerParams(dimension_semantics=("arbitrary", "arbitrary"),
                                             vmem_limit_bytes=VMEM_LIMIT_FFN_BYTES),
    )(*_hbm(*operands))


def _ffn_bwd(dy, x, gain, gu, wgu, wd, after, name, tm, tf):
    T, D = x.shape
    F = wd.shape[0]
    nj = F // tf

    def body(dy_ref, x_ref, g_ref, gu_ref, wgu_ref, wd_ref, after_ref,
             dx_ref, dgain_ref, dg_ref, du_ref, hh_ref, do_ref, do_sc, acc_sc, dgu_sc):
        i, j = pl.program_id(0), pl.program_id(1)

        @pl.when((i == 0) & (j == 0))
        def _():
            dgain_ref[...] = jnp.zeros_like(dgain_ref)

        @pl.when(j == 0)
        def _():
            do = (0.5 * dy_ref[...]).astype(BF16)
            do_sc[...] = do
            do_ref[...] = do
            acc_sc[...] = jnp.zeros_like(acc_sc)

        groups = _row_groups(tm)
        dhs = [_dot_nt(do_sc[rows, :], wd_ref[...]) for rows in groups]
        for rows, dh in zip(groups, dhs):
            g = gu_ref[rows, :tf].astype(F32)
            u = gu_ref[rows, tf:].astype(F32)
            sig = _sigmoid(g)
            s = g * sig
            dg = (dh * u * (sig + s * (1.0 - sig))).astype(BF16)
            du = (dh * s).astype(BF16)
            dg_ref[rows, :] = dg
            du_ref[rows, :] = du
            dgu_sc[rows, :tf] = dg
            dgu_sc[rows, tf:] = du
            hh_ref[rows, :] = (s * u).astype(BF16)
        for rows in groups:
            acc_sc[rows, :] += _dot(dgu_sc[rows, :], wgu_ref[...])

        @pl.when(j == nj - 1)
        def _():
            for rows in groups:
                dx, dgain = _rms_bwd(acc_sc[rows, :], x_ref[rows, :], g_ref[...])
                dgain_ref[...] += dgain
                dx_ref[rows, :] = dy_ref[rows, :] + dx

    tile_in = pl.BlockSpec((tm, D), lambda i, j: (i, 0))
    tile = pl.BlockSpec((tm, D), lambda i, j: (i, 0), pipeline_mode=pl.Buffered(1))
    return pl.pallas_call(
        body, name=name, grid=(T // tm, nj),
        in_specs=[tile_in, tile_in,
                  pl.BlockSpec((1, D), lambda i, j: (0, 0)),
                  pl.BlockSpec((tm, 2 * tf), lambda i, j: (i, j)),
                  pl.BlockSpec((2 * tf, D), lambda i, j: (j, 0)),
                  pl.BlockSpec((tf, D), lambda i, j: (j, 0)),
                  pl.BlockSpec(memory_space=pl.ANY)],
        out_specs=(tile,
                   pl.BlockSpec((1, D), lambda i, j: (0, 0)),
                   pl.BlockSpec((tm, tf), lambda i, j: (i, j)),
                   pl.BlockSpec((tm, tf), lambda i, j: (i, j)),
                   pl.BlockSpec((tm, tf), lambda i, j: (i, j)),
                   pl.BlockSpec((tm, D), lambda i, j: (i, 0))),
        out_shape=(pltpu.HBM((T,D), F32), jax.ShapeDtypeStruct((1, D), F32),
                   pltpu.HBM((T,F), BF16), pltpu.HBM((T,F), BF16),
                   pltpu.HBM((T,F), BF16), pltpu.HBM((T,D), BF16)),
        scratch_shapes=[pltpu.VMEM((tm, D), BF16), pltpu.VMEM((tm, D), F32), pltpu.VMEM((tm, 2 * tf), BF16)],
        compiler_params=pltpu.CompilerParams(dimension_semantics=("arbitrary", "arbitrary"),
                                             vmem_limit_bytes=VMEM_LIMIT_FFN_BYTES),
    )(*_hbm(dy, x, gain, gu, wgu, wd, after))


def _tn_grad(a, b, after, name, bm, tk):
    T, M = a.shape
    D = b.shape[1]
    nk = T // tk

    def body(a_ref, b_ref, after_ref, o_ref, acc_sc):
        k = pl.program_id(1)

        @pl.when(k == 0)
        def _():
            acc_sc[...] = jnp.zeros_like(acc_sc)

        acc_sc[...] += _dot_tn(a_ref[...], b_ref[...])

        @pl.when(k == nk - 1)
        def _():
            o_ref[...] = acc_sc[...].astype(BF16)

    return pl.pallas_call(
        body, name=name, grid=(M // bm, nk),
        in_specs=[pl.BlockSpec((tk, bm), lambda i, k: (k, i)), pl.BlockSpec((tk, D), lambda i, k: (k, 0)),
                  pl.BlockSpec(memory_space=pl.ANY)],
        out_specs=pl.BlockSpec((bm, D), lambda i, k: (i, 0)),
        out_shape=pltpu.HBM((M, D), BF16),
        scratch_shapes=[pltpu.VMEM((bm, D), F32)],
        compiler_params=_params("arbitrary", "arbitrary"),
    )(*_hbm(a, b, after))


_Z_SPLITS = (0, 512, 640, 768, 1280, 1792, 2304)


def _mixin_fwd(x, gain, w_in_t, name, tm):
    T, D = x.shape
    widths = [b - a for a, b in zip(_Z_SPLITS[:-1], _Z_SPLITS[1:])]

    def body(x_ref, g_ref, w_ref, hn_ref, *outs):
        _, xhat = _rms_stats(x_ref[...])
        hn = (xhat * g_ref[...]).astype(BF16)
        hn_ref[...] = hn
        for n in (0, 3, 4, 5):
            outs[n][...] = _dot_nt(hn, w_ref[_Z_SPLITS[n]:_Z_SPLITS[n + 1], :]).astype(BF16)
        kv = _dot_nt(hn, w_ref[_Z_SPLITS[1]:_Z_SPLITS[3], :]).astype(BF16)
        outs[1][...] = kv[:, :KV_WIDTH]
        outs[2][...] = kv[:, KV_WIDTH:]

    return pl.pallas_call(
        body, name=name, grid=(T // tm,),
        in_specs=[pl.BlockSpec((tm, D), lambda i: (i, 0)),
                  pl.BlockSpec((1, D), lambda i: (0, 0)),
                  pl.BlockSpec(w_in_t.shape, lambda i: (0, 0))],
        out_specs=tuple([pl.BlockSpec((tm, D), lambda i: (i, 0))]
                        + [pl.BlockSpec((tm, w), lambda i: (i, 0)) for w in widths]),
        out_shape=tuple([pltpu.HBM((T,D), BF16)]
                        + [pltpu.HBM((T,w), BF16) for w in widths]),
        compiler_params=_params("arbitrary"),
    )(*_hbm(x, gain, w_in_t))


def _bucket_table():
    qi = np.arange(BLOCK, dtype=np.int32)[:, None]
    kj = np.arange(2 * BLOCK, dtype=np.int32)[None, :]
    dist = qi + BLOCK - kj
    n = np.maximum(dist, 0)
    max_exact = NUM_BUCKETS // 2
    large = max_exact + (np.log(np.maximum(n, 1).astype(np.float32) / max_exact)
                         / math.log(MAX_DISTANCE / max_exact)
                         * (NUM_BUCKETS - max_exact)).astype(np.int32)
    large = np.minimum(large, NUM_BUCKETS - 1)
    bucket = np.where(n < max_exact, n, large).astype(np.int32)
    valid = (dist >= 0) & (dist < WINDOW)
    return np.where(valid, bucket, -1).astype(np.int32)


def _bias_build(table, bucket, name):
    def body(t_ref, b_ref, o_ref):
        bk = b_ref[...]
        for h in range(N_Q_HEADS):
            def step(b, acc):
                return jnp.where(bk == b, t_ref[b, h], acc)
            o_ref[h] = lax.fori_loop(0, NUM_BUCKETS, step, jnp.full(bk.shape, MASKED, F32))

    return pl.pallas_call(
        body, name=name,
        in_specs=[pl.BlockSpec(memory_space=pltpu.SMEM), pl.BlockSpec(memory_space=pltpu.VMEM)],
        out_specs=pl.BlockSpec(memory_space=pltpu.VMEM),
        out_shape=jax.ShapeDtypeStruct((N_Q_HEADS,) + bucket.shape, F32),
    )(table, bucket)


def _bias_grad(dbias, bucket, name):
    def body(d_ref, b_ref, o_ref):
        bk = b_ref[...]
        row = lax.broadcasted_iota(jnp.int32, o_ref.shape, 0)
        lane = lax.broadcasted_iota(jnp.int32, o_ref.shape, 1)
        res = jnp.zeros(o_ref.shape, F32)
        for h in range(N_Q_HEADS):
            d = d_ref[h]

            def step(b, acc):
                part = jnp.sum(jnp.where(bk == b, d, 0.0), axis=0, keepdims=True)
                return jnp.where(row == b, part, acc)
            per_lane = lax.fori_loop(0, NUM_BUCKETS, step, jnp.zeros(o_ref.shape, F32))
            res = jnp.where(lane == h, jnp.sum(per_lane, axis=1, keepdims=True), res)
        o_ref[...] = res

    return pl.pallas_call(
        body, name=name,
        in_specs=[pl.BlockSpec(memory_space=pltpu.VMEM), pl.BlockSpec(memory_space=pltpu.VMEM)],
        out_specs=pl.BlockSpec(memory_space=pltpu.VMEM),
        out_shape=jax.ShapeDtypeStruct((NUM_BUCKETS, LANES), F32),
    )(*_hbm(dbias, bucket))


def _head_cols(h):
    return slice(h * HEAD_DIM, (h + 1) * HEAD_DIM)


def _stack_heads(ref, r0, g, dtype):
    return jnp.concatenate(
        [ref[pl.ds(r0, BLOCK), _head_cols(GQA_GROUP * g + j)].astype(dtype) for j in range(GQA_GROUP)], axis=0)


def _unstack_heads(ref, r0, g, val):
    for j in range(GQA_GROUP):
        ref[pl.ds(r0, BLOCK), _head_cols(GQA_GROUP * g + j)] = val[j * BLOCK:(j + 1) * BLOCK, :]


def _head_lanes(h):
    return slice(h * BLOCK, (h + 1) * BLOCK)


def _group_lanes(g):
    return slice(g * GROUP_ROWS, (g + 1) * GROUP_ROWS)


def _head_softmax(st, bias_t, sink, no_prev):
    s = st * SCALE + bias_t
    row = lax.broadcasted_iota(jnp.int32, s.shape, 0)
    s = jnp.where(no_prev & (row < BLOCK), MASKED, s)
    m = jnp.maximum(jnp.max(s, axis=0, keepdims=True), sink)
    p = jnp.exp(s - m)
    ps = jnp.exp(sink - m)
    r = 1.0 / (jnp.sum(p, axis=0, keepdims=True) + ps)
    return p * r, ps * r


def _load_band(kf_sc, vf_sc, kp_ref, kc_ref, vp_ref, vc_ref, tq):
    kf_sc[0:BLOCK, :] = kp_ref[...]
    kf_sc[BLOCK:BLOCK + tq, :] = kc_ref[...]
    vf_sc[0:BLOCK, :] = vp_ref[...]
    vf_sc[BLOCK:BLOCK + tq, :] = vc_ref[...]


def _attn_fwd(q, k, v, bias_t, sinks, gain, name, nblk):
    T = q.shape[0]
    tq = nblk * BLOCK

    def body(sink_ref, q_ref, kc_ref, kp_ref, vc_ref, vp_ref, bias_ref, g_ref, raw_ref, nrm_ref,
             kf_sc, vf_sc, o_sc, st_sc, pt_sc):
        i = pl.program_id(0)
        _load_band(kf_sc, vf_sc, kp_ref, kc_ref, vp_ref, vc_ref, tq)

        def block(b, carry):
            r0 = pl.multiple_of(b * BLOCK, BLOCK)
            no_prev = (i == 0) & (b == 0)
            for g in range(N_KV_HEADS):
                kb = kf_sc[pl.ds(r0, 2 * BLOCK), _head_cols(g)]
                st_sc[:, _group_lanes(g)] = _dot_nt(kb, _stack_heads(q_ref, r0, g, BF16))
            for h in range(N_Q_HEADS):
                p, _ = _head_softmax(st_sc[:, _head_lanes(h)], bias_ref[h], sink_ref[h], no_prev)
                pt_sc[:, _head_lanes(h)] = p.astype(BF16)
            for g in range(N_KV_HEADS):
                vb = vf_sc[pl.ds(r0, 2 * BLOCK), _head_cols(g)]
                _unstack_heads(o_sc, r0, g, _dot_tn(pt_sc[:, _group_lanes(g)], vb))
            return carry

        lax.fori_loop(0, nblk, block, 0, unroll=True)
        o = o_sc[...]
        raw_ref[...] = o.astype(BF16)
        _, ohat = _rms_stats(o)
        nrm_ref[...] = (ohat * g_ref[...]).astype(BF16)

    cur = lambda i: (i, 0)
    prev = lambda i: (jnp.maximum(i * nblk - 1, 0), 0)
    lanes = N_Q_HEADS * BLOCK
    return pl.pallas_call(
        body, name=name, grid=(T // tq,),
        in_specs=[pl.BlockSpec(memory_space=pltpu.SMEM),
                  pl.BlockSpec((tq, ATTN_WIDTH), cur),
                  pl.BlockSpec((tq, KV_WIDTH), cur), pl.BlockSpec((BLOCK, KV_WIDTH), prev),
                  pl.BlockSpec((tq, KV_WIDTH), cur), pl.BlockSpec((BLOCK, KV_WIDTH), prev),
                  pl.BlockSpec(bias_t.shape, lambda i: (0, 0, 0)),
                  pl.BlockSpec((1, ATTN_WIDTH), lambda i: (0, 0))],
        out_specs=(pl.BlockSpec((tq, ATTN_WIDTH), cur), pl.BlockSpec((tq, ATTN_WIDTH), cur)),
        out_shape=(pltpu.HBM((T,ATTN_WIDTH), BF16), pltpu.HBM((T,ATTN_WIDTH), BF16)),
        scratch_shapes=[pltpu.VMEM((tq + BLOCK, KV_WIDTH), BF16), pltpu.VMEM((tq + BLOCK, KV_WIDTH), BF16),
                        pltpu.VMEM((tq, ATTN_WIDTH), F32),
                        pltpu.VMEM((2 * BLOCK, lanes), F32), pltpu.VMEM((2 * BLOCK, lanes), BF16)],
        compiler_params=_params("arbitrary"),
    )(sinks, *_hbm(q, k, k, v, v, bias_t, gain))


def _attn_bwd(dmixed, raw, q, k, v, bias_t, sinks, gain, name, nblk):
    T = q.shape[0]
    tq = nblk * BLOCK
    nt = T // tq
    lanes = N_Q_HEADS * BLOCK

    def body(sink_ref, dm_ref, raw_ref, q_ref, kc_ref, kp_ref, vc_ref, vp_ref, bias_ref, g_ref,
             dq_ref, dk_ref, dv_ref, dbias_ref, dsink_ref, dgain_ref,
             do_sc, dq_sc, kf_sc, vf_sc, dkf_sc, dvf_sc, st_sc, dpt_sc, pt_sc, dst_sc, drow_sc,
             qs_sc, dos_sc, dsink_sc):
        i = pl.program_id(0)
        tile = nt - 1 - i

        @pl.when(i == 0)
        def _():
            dkf_sc[...] = jnp.zeros_like(dkf_sc)
            dvf_sc[...] = jnp.zeros_like(dvf_sc)
            dsink_sc[...] = jnp.zeros_like(dsink_sc)
            dbias_ref[...] = jnp.zeros_like(dbias_ref)
            dgain_ref[...] = jnp.zeros_like(dgain_ref)

        carry_k = dkf_sc[0:BLOCK, :]
        carry_v = dvf_sc[0:BLOCK, :]
        dkf_sc[0:tq, :] = jnp.zeros((tq, KV_WIDTH), F32)
        dvf_sc[0:tq, :] = jnp.zeros((tq, KV_WIDTH), F32)
        dkf_sc[tq:tq + BLOCK, :] = carry_k
        dvf_sc[tq:tq + BLOCK, :] = carry_v
        _load_band(kf_sc, vf_sc, kp_ref, kc_ref, vp_ref, vc_ref, tq)

        do, dgain = _rms_bwd(dm_ref[...].astype(F32), raw_ref[...].astype(F32), g_ref[...])
        dgain_ref[...] += dgain
        do_sc[...] = do
        ones = jnp.ones((SUBLANES, HEAD_DIM), BF16)

        def block(b, carry):
            r0 = pl.multiple_of(b * BLOCK, BLOCK)
            no_prev = (tile == 0) & (b == 0)
            for g in range(N_KV_HEADS):
                kb = kf_sc[pl.ds(r0, 2 * BLOCK), _head_cols(g)]
                vb = vf_sc[pl.ds(r0, 2 * BLOCK), _head_cols(g)]
                qg = _stack_heads(q_ref, r0, g, BF16)
                dog = _stack_heads(do_sc, r0, g, F32)
                prod = dog * _stack_heads(raw_ref, r0, g, F32)
                hi = prod.astype(BF16)
                lo = (prod - hi.astype(F32)).astype(BF16)
                drow_sc[:, _group_lanes(g)] = _dot_nt(ones, hi) + _dot_nt(ones, lo)
                dogb = dog.astype(BF16)
                qs_sc[g] = qg
                dos_sc[g] = dogb
                st_sc[:, _group_lanes(g)] = _dot_nt(kb, qg)
                dpt_sc[:, _group_lanes(g)] = _dot_nt(vb, dogb)
            for h in range(N_Q_HEADS):
                hl = _head_lanes(h)
                p, ps = _head_softmax(st_sc[:, hl], bias_ref[h], sink_ref[h], no_prev)
                rowdot = drow_sc[0:1, hl]
                ds = p * (dpt_sc[:, hl] - rowdot)
                dsink_sc[h:h + 1, :] += -(ps * rowdot)
                dbias_ref[h] += ds
                dst_sc[:, hl] = ds.astype(BF16)
                pt_sc[:, hl] = p.astype(BF16)
            for g in range(N_KV_HEADS):
                kb = kf_sc[pl.ds(r0, 2 * BLOCK), _head_cols(g)]
                dsg = dst_sc[:, _group_lanes(g)]
                _unstack_heads(dq_sc, r0, g, _dot_tn(dsg, kb) * SCALE)
                dkf_sc[pl.ds(r0, 2 * BLOCK), _head_cols(g)] += _dot(dsg, qs_sc[g]) * SCALE
                dvf_sc[pl.ds(r0, 2 * BLOCK), _head_cols(g)] += _dot(pt_sc[:, _group_lanes(g)], dos_sc[g])
            return carry

        lax.fori_loop(0, nblk, block, 0, unroll=True)
        dq_ref[...] = dq_sc[...].astype(BF16)
        dk_ref[...] = dkf_sc[BLOCK:BLOCK + tq, :].astype(BF16)
        dv_ref[...] = dvf_sc[BLOCK:BLOCK + tq, :].astype(BF16)

        @pl.when(i == nt - 1)
        def _():
            tot = jnp.sum(dsink_sc[...], axis=1, keepdims=True)
            dsink_ref[...] = jnp.broadcast_to(tot, dsink_ref.shape)

    cur = lambda i: (nt - 1 - i, 0)
    prev = lambda i: (jnp.maximum((nt - 1 - i) * nblk - 1, 0), 0)
    const2 = lambda i: (0, 0)
    const3 = lambda i: (0, 0, 0)
    return pl.pallas_call(
        body, name=name, grid=(nt,),
        in_specs=[pl.BlockSpec(memory_space=pltpu.SMEM),
                  pl.BlockSpec((tq, ATTN_WIDTH), cur),
                  pl.BlockSpec((tq, ATTN_WIDTH), cur),
                  pl.BlockSpec((tq, ATTN_WIDTH), cur),
                  pl.BlockSpec((tq, KV_WIDTH), cur), pl.BlockSpec((BLOCK, KV_WIDTH), prev),
                  pl.BlockSpec((tq, KV_WIDTH), cur), pl.BlockSpec((BLOCK, KV_WIDTH), prev),
                  pl.BlockSpec(bias_t.shape, const3),
                  pl.BlockSpec((1, ATTN_WIDTH), const2)],
        out_specs=(pl.BlockSpec((tq, ATTN_WIDTH), cur),
                   pl.BlockSpec((tq, KV_WIDTH), cur), pl.BlockSpec((tq, KV_WIDTH), cur),
                   pl.BlockSpec(bias_t.shape, const3),
                   pl.BlockSpec((N_Q_HEADS, LANES), const2),
                   pl.BlockSpec((1, ATTN_WIDTH), const2)),
        out_shape=(pltpu.HBM((T,ATTN_WIDTH), BF16),
                   pltpu.HBM((T,KV_WIDTH), BF16), pltpu.HBM((T,KV_WIDTH), BF16),
                   jax.ShapeDtypeStruct(bias_t.shape, F32),
                   jax.ShapeDtypeStruct((N_Q_HEADS, LANES), F32),
                   jax.ShapeDtypeStruct((1, ATTN_WIDTH), F32)),
        scratch_shapes=[pltpu.VMEM((tq, ATTN_WIDTH), F32), pltpu.VMEM((tq, ATTN_WIDTH), F32),
                        pltpu.VMEM((tq + BLOCK, KV_WIDTH), BF16), pltpu.VMEM((tq + BLOCK, KV_WIDTH), BF16),
                        pltpu.VMEM((tq + BLOCK, KV_WIDTH), F32), pltpu.VMEM((tq + BLOCK, KV_WIDTH), F32),
                        pltpu.VMEM((2 * BLOCK, lanes), F32), pltpu.VMEM((2 * BLOCK, lanes), F32),
                        pltpu.VMEM((2 * BLOCK, lanes), BF16), pltpu.VMEM((2 * BLOCK, lanes), BF16),
                        pltpu.VMEM((SUBLANES, lanes), F32),
                        pltpu.VMEM((N_KV_HEADS, GROUP_ROWS, HEAD_DIM), BF16),
                        pltpu.VMEM((N_KV_HEADS, GROUP_ROWS, HEAD_DIM), BF16),
                        pltpu.VMEM((N_Q_HEADS, LANES), F32)],
        compiler_params=_params("arbitrary"),
    )(sinks, *_hbm(dmixed, raw, q, k, k, v, v, bias_t, gain))


def _shift_down(cu, tail):
    row = lax.broadcasted_iota(jnp.int32, cu.shape, 0)
    t6, t7 = tail[6:7, :], tail[7:8, :]
    s1 = jnp.where(row == 0, t7, pltpu.roll(cu, 1, 0))
    s2 = jnp.where(row == 0, t6, jnp.where(row == 1, t7, pltpu.roll(cu, 2, 0)))
    return s1, s2


def _shift_up(d, head):
    n = d.shape[0]
    row = lax.broadcasted_iota(jnp.int32, d.shape, 0)
    h0, h1 = head[0:1, :], head[1:2, :]
    s1 = jnp.where(row == n - 1, h0, pltpu.roll(d, n - 1, 0))
    s2 = jnp.where(row == n - 1, h1, jnp.where(row == n - 2, h0, pltpu.roll(d, n - 2, 0)))
    return s1, s2


def _mixout_fwd(x, attn_n, u, gb, gc, conv_w, gain, w_out, name, tm):
    T, D = x.shape

    def body(x_ref, an_ref, u_ref, b_ref, c_ref, cw_ref, g_ref, wo_ref, xo_ref, cn_ref, tail_sc):
        @pl.when(pl.program_id(0) == 0)
        def _():
            tail_sc[...] = jnp.zeros_like(tail_sc)

        cu = c_ref[...].astype(F32) * u_ref[...].astype(F32)
        s1, s2 = _shift_down(cu, tail_sc[...])
        tail_sc[...] = cu[tm - SUBLANES:tm, :]
        pre = cw_ref[0:1, :] * s2 + cw_ref[1:2, :] * s1 + cw_ref[2:3, :] * cu
        conv = b_ref[...].astype(F32) * pre
        _, chat = _rms_stats(conv)
        cn = (chat * g_ref[...]).astype(BF16)
        cn_ref[...] = cn
        xo_ref[...] = (x_ref[...] + _dot(an_ref[...], wo_ref[0:ATTN_WIDTH, :])
                       + _dot(cn, wo_ref[ATTN_WIDTH:ATTN_WIDTH + CONV_DIM, :]))

    row = lambda i: (i, 0)
    const = lambda i: (0, 0)
    return pl.pallas_call(
        body, name=name, grid=(T // tm,),
        in_specs=[pl.BlockSpec((tm, D), row), pl.BlockSpec((tm, ATTN_WIDTH), row),
                  pl.BlockSpec((tm, CONV_DIM), row), pl.BlockSpec((tm, CONV_DIM), row),
                  pl.BlockSpec((tm, CONV_DIM), row),
                  pl.BlockSpec(conv_w.shape, const), pl.BlockSpec((1, CONV_DIM), const),
                  pl.BlockSpec(w_out.shape, const)],
        out_specs=(pl.BlockSpec((tm, D), row), pl.BlockSpec((tm, CONV_DIM), row)),
        out_shape=(pltpu.HBM((T,D), F32), pltpu.HBM((T,CONV_DIM), BF16)),
        scratch_shapes=[pltpu.VMEM((SUBLANES, CONV_DIM), F32)],
        compiler_params=_params("arbitrary"),
    )(*_hbm(x, attn_n, u, gb, gc, conv_w, gain, w_out))


def _mixout_bwd(dy, attn_n, conv_n, w_out, after, name, tm):
    T, D = dy.shape
    W = ATTN_WIDTH + CONV_DIM
    nt = T // tm

    def body(dy_ref, an_ref, cn_ref, w_ref, after_ref, dm_ref, dw_ref, dw_sc):
        i = pl.program_id(0)

        @pl.when(i == 0)
        def _():
            dw_sc[...] = jnp.zeros_like(dw_sc)

        dyb = dy_ref[...].astype(BF16)
        dm_ref[...] = _dot_nt(dyb, w_ref[...]).astype(BF16)
        dw_sc[0:ATTN_WIDTH, :] += _dot_tn(an_ref[...], dyb)
        dw_sc[ATTN_WIDTH:W, :] += _dot_tn(cn_ref[...], dyb)

        @pl.when(i == nt - 1)
        def _():
            dw_ref[...] = dw_sc[...].astype(BF16)

    row = lambda i: (i, 0)
    const = lambda i: (0, 0)
    return pl.pallas_call(
        body, name=name, grid=(nt,),
        in_specs=[pl.BlockSpec((tm, D), row), pl.BlockSpec((tm, ATTN_WIDTH), row),
                  pl.BlockSpec((tm, CONV_DIM), row), pl.BlockSpec(w_out.shape, const),
                  pl.BlockSpec(memory_space=pl.ANY)],
        out_specs=(pl.BlockSpec((tm, W), row), pl.BlockSpec((W, D), const)),
        out_shape=(pltpu.HBM((T,W), BF16), pltpu.HBM((W, D), BF16)),
        scratch_shapes=[pltpu.VMEM((W, D), F32)],
        compiler_params=_params("arbitrary"),
    )(*_hbm(dy, attn_n, conv_n, w_out, after))


def _conv_bwd(dmixed, u, gb, gc, conv_w, gain, name, tc):
    T = u.shape[0]
    nt = T // tc
    per8 = tc // SUBLANES

    def body(dm_ref, u_ref, b_ref, c_ref, ut_ref, ct_ref, cw_ref, g_ref,
             du_ref, db_ref, dc_ref, dcw_ref, dgain_ref, head_sc):
        i = pl.program_id(0)

        @pl.when(i == 0)
        def _():
            head_sc[...] = jnp.zeros_like(head_sc)
            dcw_ref[...] = jnp.zeros_like(dcw_ref)
            dgain_ref[...] = jnp.zeros_like(dgain_ref)

        uu = u_ref[...].astype(F32)
        cc = c_ref[...].astype(F32)
        bb = b_ref[...].astype(F32)
        cu = cc * uu
        tail = jnp.where(i == nt - 1, 0.0, ct_ref[...].astype(F32) * ut_ref[...].astype(F32))
        s1, s2 = _shift_down(cu, tail)
        w0, w1, w2 = cw_ref[0:1, :], cw_ref[1:2, :], cw_ref[2:3, :]
        pre = w0 * s2 + w1 * s1 + w2 * cu
        dconv, dgain = _rms_bwd(dm_ref[...].astype(F32), bb * pre, g_ref[...])
        dgain_ref[...] += dgain
        db_ref[...] = (dconv * pre).astype(BF16)
        dpre = dconv * bb
        dcw_ref[0:1, :] += jnp.sum(dpre * s2, axis=0, keepdims=True)
        dcw_ref[1:2, :] += jnp.sum(dpre * s1, axis=0, keepdims=True)
        dcw_ref[2:3, :] += jnp.sum(dpre * cu, axis=0, keepdims=True)
        n1, n2 = _shift_up(dpre, head_sc[...])
        head_sc[...] = dpre[0:SUBLANES, :]
        dcu = w2 * dpre + w1 * n1 + w0 * n2
        du_ref[...] = (dcu * cc).astype(BF16)
        dc_ref[...] = (dcu * uu).astype(BF16)

    rev = lambda i: (nt - 1 - i, 0)
    rev_right = lambda i: (nt - 1 - i, 1)
    tail_map = lambda i: (jnp.maximum((nt - 1 - i) * per8 - 1, 0), 0)
    const = lambda i: (0, 0)
    return pl.pallas_call(
        body, name=name, grid=(nt,),
        in_specs=[pl.BlockSpec((tc, CONV_DIM), rev_right),
                  pl.BlockSpec((tc, CONV_DIM), rev), pl.BlockSpec((tc, CONV_DIM), rev),
                  pl.BlockSpec((tc, CONV_DIM), rev),
                  pl.BlockSpec((SUBLANES, CONV_DIM), tail_map), pl.BlockSpec((SUBLANES, CONV_DIM), tail_map),
                  pl.BlockSpec(conv_w.shape, const), pl.BlockSpec((1, CONV_DIM), const)],
        out_specs=(pl.BlockSpec((tc, CONV_DIM), rev), pl.BlockSpec((tc, CONV_DIM), rev),
                   pl.BlockSpec((tc, CONV_DIM), rev),
                   pl.BlockSpec((SUBLANES, CONV_DIM), const), pl.BlockSpec((1, CONV_DIM), const)),
        out_shape=(pltpu.HBM((T,CONV_DIM), BF16), pltpu.HBM((T,CONV_DIM), BF16),
                   pltpu.HBM((T,CONV_DIM), BF16),
                   jax.ShapeDtypeStruct((SUBLANES, CONV_DIM), F32), jax.ShapeDtypeStruct((1, CONV_DIM), F32)),
        scratch_shapes=[pltpu.VMEM((SUBLANES, CONV_DIM), F32)],
        compiler_params=_params("arbitrary"),
    )(*_hbm(dmixed, u, gb, gc, u, gc, conv_w, gain))


def _mixin_bwd(dy, x, gain, dz, w_in_t, name, tm):
    T, D = x.shape
    nz = len(dz)

    def body(dy_ref, x_ref, g_ref, *rest):
        dz_refs, wt_ref, dx_ref, dgain_ref, dz_sc = rest[:nz], rest[nz], rest[nz + 1], rest[nz + 2], rest[nz + 3]

        @pl.when(pl.program_id(0) == 0)
        def _():
            dgain_ref[...] = jnp.zeros_like(dgain_ref)

        for r, lo, hi in zip(dz_refs, _Z_SPLITS[:-1], _Z_SPLITS[1:]):
            dz_sc[:, lo:hi] = r[...]
        dx, dgain = _rms_bwd(_dot(dz_sc[...], wt_ref[...]), x_ref[...], g_ref[...])
        dgain_ref[...] += dgain
        dx_ref[...] = dy_ref[...] + dx

    row = lambda i: (i, 0)
    const = lambda i: (0, 0)
    return pl.pallas_call(
        body, name=name, grid=(T // tm,),
        in_specs=[pl.BlockSpec((tm, D), row), pl.BlockSpec((tm, D), row), pl.BlockSpec((1, D), const)]
                 + [pl.BlockSpec((tm, a.shape[1]), row) for a in dz]
                 + [pl.BlockSpec(w_in_t.shape, const)],
        out_specs=(pl.BlockSpec((tm, D), row), pl.BlockSpec((1, D), const)),
        out_shape=(pltpu.HBM((T,D), F32), jax.ShapeDtypeStruct((1, D), F32)),
        scratch_shapes=[pltpu.VMEM((tm, _Z_SPLITS[-1]), BF16)],
        compiler_params=_params("arbitrary"),
    )(*_hbm(dy, x, gain, *dz, w_in_t))


def _win_grad(dz, hn, name, tk):
    T, D = hn.shape
    nz = len(dz)
    nt = T // tk
    W = _Z_SPLITS[-1]

    def body(hn_ref, *rest):
        dz_refs, dw_ref, dw_sc, kv_sc = rest[:nz], rest[nz], rest[nz + 1], rest[nz + 2]
        i = pl.program_id(0)

        @pl.when(i == 0)
        def _():
            dw_sc[...] = jnp.zeros_like(dw_sc)

        hn = hn_ref[...]
        kv_sc[:, :KV_WIDTH] = dz_refs[1][...]
        kv_sc[:, KV_WIDTH:] = dz_refs[2][...]
        dw_sc[_Z_SPLITS[1]:_Z_SPLITS[3], :] += _dot_tn(kv_sc[...], hn)
        for n in (0, 3, 4, 5):
            dw_sc[_Z_SPLITS[n]:_Z_SPLITS[n + 1], :] += _dot_tn(dz_refs[n][...], hn)

        @pl.when(i == nt - 1)
        def _():
            dw_ref[...] = dw_sc[...].astype(BF16)

    row = lambda i: (i, 0)
    return pl.pallas_call(
        body, name=name, grid=(nt,),
        in_specs=[pl.BlockSpec((tk, D), row)] + [pl.BlockSpec((tk, a.shape[1]), row) for a in dz],
        out_specs=pl.BlockSpec((W, D), lambda i: (0, 0)),
        out_shape=pltpu.HBM((W, D), BF16),
        scratch_shapes=[pltpu.VMEM((W, D), F32), pltpu.VMEM((tk, 2 * KV_WIDTH), BF16)],
        compiler_params=_params("arbitrary"),
    )(*_hbm(hn, *dz))


def _adamw(parts, w, m, v, name, tr):
    P = parts.shape[0]
    R, C = w.shape

    def body(p_ref, w_ref, m_ref, v_ref, g_ref, d_ref, nm_ref, nv_ref):
        g = p_ref[0].astype(F32)
        for d in range(1, P):
            g = g + p_ref[d].astype(F32)
        nm = ADAM_B1 * m_ref[...] + (1.0 - ADAM_B1) * g
        nv = ADAM_B2 * v_ref[...] + (1.0 - ADAM_B2) * (g * g)
        m_hat = nm / (1.0 - ADAM_B1 ** ADAM_STEP)
        v_hat = nv / (1.0 - ADAM_B2 ** ADAM_STEP)
        g_ref[...] = g
        nm_ref[...] = nm
        nv_ref[...] = nv
        d_ref[...] = -ADAM_LR * (m_hat / (jnp.sqrt(v_hat) + ADAM_EPS) + ADAM_WD * w_ref[...])

    row = lambda i: (i, 0)
    spec = pl.BlockSpec((tr, C), row)
    shp = pltpu.HBM((R, C), F32)
    return pl.pallas_call(
        body, name=name, grid=(R // tr,),
        in_specs=[pl.BlockSpec((P, tr, C), lambda i: (0, i, 0)), spec, spec, spec],
        out_specs=(spec, spec, spec, spec),
        out_shape=(shp, shp, shp, shp),
        compiler_params=_params("arbitrary"),
    )(*_hbm(parts, w, m, v))


def _columns_of_blocks(g):
    n, R, w = g.shape
    return g.transpose(1, 0, 2).reshape(R, n * w)


def _pad_row(vec):
    vec = vec.reshape(1, -1)
    return jnp.pad(vec, ((0, 0), (0, PACK_COLS - vec.shape[1])))


def kernel(x, rel_bias_table, ffn1_norm, ffn1_w_gate, ffn1_w_up, ffn1_w_down, mix_norm, w_in, conv_w, attn_sinks, attn_out_norm, conv_out_norm, w_out, ffn2_norm, ffn2_w_gate, ffn2_w_up, ffn2_w_down, final_norm, loss_target, m_rel_bias_table, m_ffn1_norm, m_ffn1_w_gate, m_ffn1_w_up, m_ffn1_w_down, m_mix_norm, m_w_in, m_conv_w, m_attn_sinks, m_attn_out_norm, m_conv_out_norm, m_w_out, m_ffn2_norm, m_ffn2_w_gate, m_ffn2_w_up, m_ffn2_w_down, m_final_norm, v_rel_bias_table, v_ffn1_norm, v_ffn1_w_gate, v_ffn1_w_up, v_ffn1_w_down, v_mix_norm, v_w_in, v_conv_w, v_attn_sinks, v_attn_out_norm, v_conv_out_norm, v_w_out, v_ffn2_norm, v_ffn2_w_gate, v_ffn2_w_up, v_ffn2_w_down, v_final_norm):
    T, D = x.shape[1], x.shape[2]
    x0 = x[0]
    target = loss_target[0]
    tm = min(TM_FFN, T)
    tm_bwd = min(TM_FFN_BWD, T)
    tm_mix = min(TM_MIX, T)
    tm_wide = min(TM_MIX_WIDE, T)
    tk = min(TK_WGRAD, T)
    tf = TF_FFN
    nblk = min(ATTN_BLOCKS, T // BLOCK)
    me = 4 * lax.axis_index("x") + 2 * lax.axis_index("y") + lax.axis_index("c")

    big = {
        "ffn1_w_gate": (ffn1_w_gate[0], m_ffn1_w_gate[0], v_ffn1_w_gate[0], True),
        "ffn1_w_up": (ffn1_w_up[0], m_ffn1_w_up[0], v_ffn1_w_up[0], True),
        "ffn1_w_down": (ffn1_w_down[0], m_ffn1_w_down[0], v_ffn1_w_down[0], False),
        "w_in": (w_in[0], m_w_in[0], v_w_in[0], True),
        "w_out": (w_out[0], m_w_out[0], v_w_out[0], False),
        "ffn2_w_gate": (ffn2_w_gate[0], m_ffn2_w_gate[0], v_ffn2_w_gate[0], True),
        "ffn2_w_up": (ffn2_w_up[0], m_ffn2_w_up[0], v_ffn2_w_up[0], True),
        "ffn2_w_down": (ffn2_w_down[0], m_ffn2_w_down[0], v_ffn2_w_down[0], False),
    }

    def block_to_send(name):
        w, _, _, transposed = big[name]
        return (w.T if transposed else w).astype(BF16)

    names1 = ["ffn1_w_gate", "ffn1_w_up", "ffn1_w_down"]
    names_rest = ["w_in", "w_out", "ffn2_w_gate", "ffn2_w_up", "ffn2_w_down"]
    first = _gather_two_level([block_to_send(n) for n in names1], "gather_ffn1", ffn1_norm)
    wgt1, wut1, wd1 = [g.reshape(-1, D) for g in first]
    h_rest, token = _exchange_start([block_to_send(n) for n in names_rest[:2]] + [conv_w[0]]
                                    + [block_to_send(n) for n in names_rest[2:]], False,
                                    "gather_start_rest", wd1)

    wgu1 = _stack_gate_up(wgt1, wut1, tf)
    x1, xn1, gu1 = _ffn_fwd(x0, ffn1_norm, wgu1, wd1, token, "ffn1_fwd", tm, tf)
    mixw = _exchange_wait(h_rest[:3], False, "gather_wait_mix", x1)
    win_t = mixw[0].reshape(-1, D)
    wout = mixw[1].reshape(-1, D)
    cw = _columns_of_blocks(mixw[2])
    hn, q, k, v, u, gb, gc = _mixin_fwd(x1, mix_norm, win_t, "mixin_fwd", tm_wide)
    bucket = jnp.asarray(_bucket_table().T.copy())
    sinks = attn_sinks.reshape(-1)
    bias_t = _bias_build(rel_bias_table, bucket, "bias_build")
    attn_raw, attn_n = _attn_fwd(q, k, v, bias_t, sinks, attn_out_norm, "attn_fwd", nblk)
    x2, conv_n = _mixout_fwd(x1, attn_n, u, gb, gc, cw, conv_out_norm, wout, "mixout_fwd", tm_mix)
    wgt2, wut2, wd2 = [g.reshape(-1, D) for g in _exchange_wait(h_rest[3:], False, "gather_wait_ffn2", x2)]
    wgu2 = _stack_gate_up(wgt2, wut2, tf)
    dx3, xn2, gu2, d_final, loss_part = _ffn_fwd(x2, ffn2_norm, wgu2, wd2, x2, "ffn2_fwd", tm, tf,
                                                 head=(target, final_norm.reshape(1, D)))

    def blocks(g):
        return g.reshape(N_DEV, -1, D)

    dx2, d_ffn2_norm, dg2, du2, hh2, do2 = _ffn_bwd(
        dx3, x2, ffn2_norm, gu2, wgu2, wd2, dx3, "ffn2_bwd", tm_bwd, tf)
    d_wg2 = _tn_grad(dg2, xn2, dx2, "ffn2_wgrad_gate", BM_WGRAD, tk)
    d_wu2 = _tn_grad(du2, xn2, dx2, "ffn2_wgrad_up", BM_WGRAD, tk)
    d_wd2 = _tn_grad(hh2, do2, dx2, "ffn2_wgrad_down", BM_WGRAD, tk)
    handles2, token2 = _exchange_start([blocks(d_wg2), blocks(d_wu2), blocks(d_wd2)], True,
                                       "grads_start_ffn2", dx2)

    dmixed, d_wout = _mixout_bwd(dx2, attn_n, conv_n, wout, token2, "mixout_bwd", tm_wide)
    dq, dk, dv, dbias, dsink, d_attn_norm = _attn_bwd(
        dmixed, attn_raw, q, k, v, bias_t, sinks, attn_out_norm, "attn_bwd", nblk)
    du, dgb, dgc, d_cw, d_conv_norm = _conv_bwd(dmixed, u, gb, gc, cw, conv_out_norm, "conv_bwd", tm_wide)
    d_table = _bias_grad(dbias, bucket, "bias_grad")
    dz = [dq, dk, dv, du, dgb, dgc]
    dx1, d_mix_norm = _mixin_bwd(dx2, x1, mix_norm, dz, win_t, "mixin_bwd", tm_mix)
    d_win_t = _win_grad(dz, hn, "win_grad", tm_wide)
    handles_mix, token_mix = _exchange_start([blocks(d_win_t), blocks(d_wout)], True, "grads_start_mix", d_table)

    dx0, d_ffn1_norm, dg1, du1, hh1, do1 = _ffn_bwd(
        dx1, x0, ffn1_norm, gu1, wgu1, wd1, token_mix, "ffn1_bwd", tm_bwd, tf)
    def pack(ffn1, mixn, ffn2, fin, attn_n_, conv_n_, sink_, extra, convw, table):
        rows = [_pad_row(ffn1), _pad_row(mixn), _pad_row(ffn2), _pad_row(fin),
                _pad_row(jnp.concatenate([attn_n_.reshape(-1), conv_n_.reshape(-1)])),
                _pad_row(sink_), _pad_row(extra),
                jnp.zeros((1, PACK_COLS), F32),
                jnp.pad(convw, ((0, 0), (0, PACK_COLS - convw.shape[1]))),
                _pad_row(table),
                jnp.zeros((PACK_ROWS - 12, PACK_COLS), F32)]
        return jnp.concatenate(rows, axis=0)

    def own_channels(a):
        full = jnp.zeros((a.shape[1], CONV_DIM), F32)
        return lax.dynamic_update_slice(full, a[0], (0, me * a.shape[2]))

    g_pack = pack(d_ffn1_norm, d_mix_norm, d_ffn2_norm, d_final, d_attn_norm, d_conv_norm,
                  dsink[:, 0], loss_part[0, :1], d_cw[:3], d_table[:, :N_Q_HEADS])
    zero1 = jnp.zeros((1,), F32)
    w_pack = pack(ffn1_norm, mix_norm, ffn2_norm, final_norm, attn_out_norm, conv_out_norm,
                  attn_sinks, zero1, own_channels(conv_w), rel_bias_table)
    m_pack = pack(m_ffn1_norm, m_mix_norm, m_ffn2_norm, m_final_norm, m_attn_out_norm, m_conv_out_norm,
                  m_attn_sinks, zero1, own_channels(m_conv_w), m_rel_bias_table)
    v_pack = pack(v_ffn1_norm, v_mix_norm, v_ffn2_norm, v_final_norm, v_attn_out_norm, v_conv_out_norm,
                  v_attn_sinks, zero1, own_channels(v_conv_w), v_rel_bias_table)
    h_small, token_small = _exchange_start([g_pack], False, "gather_small_start", dx0)

    d_wd1 = _tn_grad(hh1, do1, token_small, "ffn1_wgrad_down", BM_WGRAD, tk)
    handles1_d, token1 = _exchange_start([blocks(d_wd1)], True, "grads_start_ffn1_down", dx0)
    d_wg1 = _tn_grad(dg1, xn1, token1, "ffn1_wgrad_gate", BM_WGRAD, tk)
    handles1_g, token1 = _exchange_start([blocks(d_wg1)], True, "grads_start_ffn1_gate", token1)
    d_wu1 = _tn_grad(du1, xn1, token1, "ffn1_wgrad_up", BM_WGRAD, tk)
    handles1_u, token1 = _exchange_start([blocks(d_wu1)], True, "grads_start_ffn1_up", token1)

    res = {}
    made = []

    def update(names, parts):
        last = None
        for name, p in zip(names, parts):
            w, m_, v_, transposed = big[name]
            if transposed:
                w, m_, v_ = w.T, m_.T, v_.T
            new = _adamw(p, w, m_, v_, "adamw_" + name, _row_tile(w.shape[0], ADAM_ROWS))
            res[name] = tuple((a.T if transposed else a)[None] for a in new)
            made.extend(new)
            last = new[0]
        return last

    parts2 = _exchange_wait(handles2, True, "grads_wait_ffn2", token1)
    done2 = update(["ffn2_w_gate", "ffn2_w_up", "ffn2_w_down"], parts2)
    parts_mix = _exchange_wait(handles_mix, True, "grads_wait_mix", done2)
    done_mix = update(["w_in", "w_out"], parts_mix)
    (g_all,) = _exchange_wait(h_small, False, "gather_small_wait", done_mix)
    packs = _adamw(g_all, w_pack, m_pack, v_pack, "adamw_small", PACK_ROWS)

    def unpack(pk):
        cwb = lax.dynamic_slice(pk[8:11, :CONV_DIM], (0, me * conv_w.shape[2]), (3, conv_w.shape[2]))
        return {
            "ffn1_norm": pk[0:1, :D], "mix_norm": pk[1:2, :D], "ffn2_norm": pk[2:3, :D],
            "final_norm": pk[3, :D],
            "attn_out_norm": pk[4:5, :ATTN_WIDTH], "conv_out_norm": pk[4:5, ATTN_WIDTH:ATTN_WIDTH + CONV_DIM],
            "attn_sinks": pk[5:6, :N_Q_HEADS],
            "conv_w": cwb[None],
            "rel_bias_table": pk[11, :NUM_BUCKETS * N_Q_HEADS].reshape(NUM_BUCKETS, N_Q_HEADS),
        }

    small = [unpack(pk) for pk in packs]
    loss = packs[0][6, 0]

    ready = list(made) + [a for group in small for a in group.values()]
    parts1_d = _exchange_wait(handles1_d, True, "grads_wait_ffn1_down", ready)
    done1 = update(["ffn1_w_down"], parts1_d)
    parts1_g = _exchange_wait(handles1_g, True, "grads_wait_ffn1_gate", done1)
    done1 = update(["ffn1_w_gate"], parts1_g)
    parts1_u = _exchange_wait(handles1_u, True, "grads_wait_ffn1_up", done1)
    update(["ffn1_w_up"], parts1_u)

    order = ["rel_bias_table", "ffn1_norm", "ffn1_w_gate", "ffn1_w_up", "ffn1_w_down", "mix_norm", "w_in",
             "conv_w", "attn_sinks", "attn_out_norm", "conv_out_norm", "w_out", "ffn2_norm",
             "ffn2_w_gate", "ffn2_w_up", "ffn2_w_down", "final_norm"]
    outs = [loss, dx0[None]]
    for kind in range(4):
        for name in order:
            outs.append(res[name][kind] if name in res else small[kind][name])
    return tuple(outs)
```
